```python
import jax, jax.numpy as jnp
from jax import lax
import numpy as np

D_MODEL = 1024
BATCH = 8
SEQ = 4096
DEPTH = 1

N_MLA_HEADS = 8
MLA_Q_RANK = 256
MLA_KV_RANK = 128
MLA_NOPE_DIM = 64
MLA_ROPE_DIM = 32
MLA_V_DIM = 64
ROPE_THETA = 10000.0
Q_BLOCK = 128

N_GDN_HEADS = 8
GDN_HEAD_DIM = 64
GDN_CONV = 4
GDN_CHUNK = 64

MLA_W = N_MLA_HEADS * MLA_V_DIM
GDN_W = N_GDN_HEADS * GDN_HEAD_DIM
D_MIX = MLA_W + GDN_W
D_IN = MLA_Q_RANK + MLA_KV_RANK + MLA_ROPE_DIM + 3 * GDN_W + 2 * N_GDN_HEADS + GDN_W

D_FF = 2816
EPS = 1e-6

kernel_name = "hybrid_mla_gdn_macaron_sandwich"


def rmsnorm(x, g):
    xf = x.astype(jnp.float32)
    y = xf * lax.rsqrt(jnp.mean(xf * xf, axis=-1, keepdims=True) + EPS)
    return (y * g.astype(jnp.float32)).astype(x.dtype)


def l2norm(x):
    xf = x.astype(jnp.float32)
    return xf * lax.rsqrt(jnp.sum(xf * xf, axis=-1, keepdims=True) + EPS)


def swiglu(x, w_gate, w_up, w_down):
    return (jax.nn.silu(x @ w_gate) * (x @ w_up)) @ w_down


def rope_tables(positions):
    half = MLA_ROPE_DIM // 2
    freqs = ROPE_THETA ** (-jnp.arange(half, dtype=jnp.float32) / half)
    ang = positions.astype(jnp.float32)[..., None] * freqs
    return jnp.cos(ang), jnp.sin(ang)


def apply_rope(x, cos, sin):
    x1, x2 = jnp.split(x.astype(jnp.float32), 2, axis=-1)
    return jnp.concatenate([x1 * cos - x2 * sin, x1 * sin + x2 * cos], axis=-1).astype(x.dtype)


def mla_group(c_q, c_kv, k_rope_raw, positions, q_norm_g, w_uq, kv_norm_g, w_ukv):
    B, T, _ = c_q.shape
    H = N_MLA_HEADS
    q = (rmsnorm(c_q, q_norm_g) @ w_uq).reshape(B, T, H, MLA_NOPE_DIM + MLA_ROPE_DIM)
    q_nope, q_pe = q[..., :MLA_NOPE_DIM], q[..., MLA_NOPE_DIM:]
    kv = (rmsnorm(c_kv, kv_norm_g) @ w_ukv).reshape(B, T, H, MLA_NOPE_DIM + MLA_V_DIM)
    k_nope, v = kv[..., :MLA_NOPE_DIM], kv[..., MLA_NOPE_DIM:]
    cos, sin = rope_tables(positions)
    q_pe = apply_rope(q_pe, cos[:, :, None], sin[:, :, None])
    k_pe = apply_rope(k_rope_raw, cos, sin)
    scale = (MLA_NOPE_DIM + MLA_ROPE_DIM) ** -0.5
    nb = T // Q_BLOCK
    qn_b = q_nope.reshape(B, nb, Q_BLOCK, H, MLA_NOPE_DIM).transpose(1, 0, 2, 3, 4)
    qp_b = q_pe.reshape(B, nb, Q_BLOCK, H, MLA_ROPE_DIM).transpose(1, 0, 2, 3, 4)
    key_pos = jnp.arange(T)

    def attend(args):
        qn, qp, blk = args
        s = (jnp.einsum('bqhd,bkhd->bhqk', qn, k_nope)
             + jnp.einsum('bqhr,bkr->bhqk', qp, k_pe)).astype(jnp.float32) * scale
        q_pos = blk * Q_BLOCK + jnp.arange(Q_BLOCK)
        causal = key_pos[None, :] <= q_pos[:, None]
        s = jnp.where(causal, s, -jnp.inf)
        p = jax.nn.softmax(s, axis=-1).astype(v.dtype)
        return jnp.einsum('bhqk,bkhd->bqhd', p, v)

    o = lax.map(attend, (qn_b, qp_b, jnp.arange(nb)))
    return o.transpose(1, 0, 2, 3, 4).reshape(B, T, H * MLA_V_DIM)


def causal_conv(x, w):
    K, C = w.shape
    return lax.conv_general_dilated(
        x, w[:, None, :], window_strides=(1,), padding=[(K - 1, 0)],
        dimension_numbers=('NWC', 'WIO', 'NWC'), feature_group_count=C)


def gated_delta_rule(q, k, v, g, beta):
    out_dtype = v.dtype
    B, T, H, dk = q.shape
    dv = v.shape[-1]
    C = GDN_CHUNK
    N = T // C
    f32 = jnp.float32
    q = q.astype(f32) * dk ** -0.5
    k, v, g, beta = k.astype(f32), v.astype(f32), g.astype(f32), beta.astype(f32)

    def to_chunks(t):
        return t.reshape((B, N, C, H) + t.shape[3:]).swapaxes(2, 3)

    qc, kc, vc, gc, bc = map(to_chunks, (q, k, v, g, beta))
    gc = jnp.cumsum(gc, axis=-1)
    tril = jnp.tril(jnp.ones((C, C), dtype=bool))
    strict = jnp.tril(jnp.ones((C, C), dtype=bool), -1)
    diff = gc[..., :, None] - gc[..., None, :]
    decay = jnp.exp(jnp.where(tril, diff, -jnp.inf))
    kb = kc * bc[..., None]
    L = jnp.where(strict, jnp.einsum('bnhid,bnhjd->bnhij', kb, kc) * decay, 0.0)
    A = jnp.eye(C, dtype=f32) + L
    w = lax.linalg.triangular_solve(A, kb * jnp.exp(gc)[..., None],
                                    left_side=True, lower=True, unit_diagonal=True)
    u = lax.linalg.triangular_solve(A, vc * bc[..., None],
                                    left_side=True, lower=True, unit_diagonal=True)
    attn = jnp.einsum('bnhid,bnhjd->bnhij', qc, kc) * decay
    q_dec = qc * jnp.exp(gc)[..., None]
    k_dec = kc * jnp.exp(gc[..., -1:] - gc)[..., None]
    g_last = jnp.exp(gc[..., -1])

    def step(S, xs):
        q_d, w_i, u_i, attn_i, k_d, gl = xs
        v_new = u_i - jnp.einsum('bhcd,bhde->bhce', w_i, S)
        o = jnp.einsum('bhcd,bhde->bhce', q_d, S) + jnp.einsum('bhij,bhje->bhie', attn_i, v_new)
        S = S * gl[..., None, None] + jnp.einsum('bhcd,bhce->bhde', k_d, v_new)
        return S, o

    xs = tuple(jnp.moveaxis(t, 1, 0) for t in (q_dec, w, u, attn, k_dec, g_last))
    S0 = jnp.zeros((B, H, dk, dv), f32)
    _, o = lax.scan(step, S0, xs)
    return o.transpose(1, 0, 3, 2, 4).reshape(B, T, H, dv).astype(out_dtype)


def _fwd_setup_inputs(seed: int = 0) -> dict:
    key = jax.random.key(seed)
    ks = jax.random.split(key, 32)
    f32 = jnp.float32

    def nrm(k, shape, fan_in):
        return jax.random.normal(k, shape, f32) * fan_in ** -0.5

    def gain(k, shape):
        return 1.0 + 0.02 * jax.random.normal(k, shape, f32)

    L = DEPTH
    x = jax.random.normal(ks[0], (BATCH, SEQ, D_MODEL), f32)
    offset = jax.random.randint(ks[1], (BATCH, 1), 0, 2048, dtype=jnp.int32)
    positions = offset + jnp.arange(SEQ, dtype=jnp.int32)[None, :]
    a_log = jnp.log(jax.random.uniform(ks[17], (L, N_GDN_HEADS), f32, 1.0, 16.0))
    dt = jnp.exp(jax.random.uniform(ks[18], (L, N_GDN_HEADS), f32, np.log(1e-3), np.log(1e-1)))
    dt_bias = dt + jnp.log(-jnp.expm1(-dt))
    return {
        "x": x,
        "positions": positions,
        "ffn1_pre_g": gain(ks[2], (L, D_MODEL)),
        "ffn1_w_gate": nrm(ks[3], (L, D_MODEL, D_FF), D_MODEL),
        "ffn1_w_up": nrm(ks[4], (L, D_MODEL, D_FF), D_MODEL),
        "ffn1_w_down": nrm(ks[5], (L, D_FF, D_MODEL), D_FF),
        "ffn1_post_g": gain(ks[6], (L, D_MODEL)),
        "mix_pre_g": gain(ks[7], (L, D_MODEL)),
        "w_in": nrm(ks[8], (L, D_MODEL, D_IN), D_MODEL),
        "mla_q_norm_g": gain(ks[9], (L, MLA_Q_RANK)),
        "mla_w_uq": nrm(ks[10], (L, MLA_Q_RANK, N_MLA_HEADS * (MLA_NOPE_DIM + MLA_ROPE_DIM)), MLA_Q_RANK),
        "mla_kv_norm_g": gain(ks[11], (L, MLA_KV_RANK)),
        "mla_w_ukv": nrm(ks[12], (L, MLA_KV_RANK, N_MLA_HEADS * (MLA_NOPE_DIM + MLA_V_DIM)), MLA_KV_RANK),
        "mla_out_g": gain(ks[13], (L, MLA_W)),
        "gdn_conv_w": nrm(ks[14], (L, GDN_CONV, 3 * GDN_W), GDN_CONV),
        "gdn_a_log": a_log,
        "gdn_dt_bias": dt_bias,
        "gdn_norm_g": gain(ks[15], (L, GDN_HEAD_DIM)),
        "w_out": nrm(ks[16], (L, D_MIX, D_MODEL), D_MIX),
        "mix_post_g": gain(ks[19], (L, D_MODEL)),
        "ffn2_pre_g": gain(ks[20], (L, D_MODEL)),
        "ffn2_w_gate": nrm(ks[21], (L, D_MODEL, D_FF), D_MODEL),
        "ffn2_w_up": nrm(ks[22], (L, D_MODEL, D_FF), D_MODEL),
        "ffn2_w_down": nrm(ks[23], (L, D_FF, D_MODEL), D_FF),
        "ffn2_post_g": gain(ks[24], (L, D_MODEL)),
    }


def _fwd_reference(x, positions, ffn1_pre_g, ffn1_w_gate, ffn1_w_up, ffn1_w_down, ffn1_post_g,
              mix_pre_g, w_in, mla_q_norm_g, mla_w_uq, mla_kv_norm_g, mla_w_ukv, mla_out_g,
              gdn_conv_w, gdn_a_log, gdn_dt_bias, gdn_norm_g, w_out, mix_post_g,
              ffn2_pre_g, ffn2_w_gate, ffn2_w_up, ffn2_w_down, ffn2_post_g):
    B, T, _ = x.shape
    H, dh = N_GDN_HEADS, GDN_HEAD_DIM
    sizes = (MLA_Q_RANK, MLA_KV_RANK, MLA_ROPE_DIM, 3 * GDN_W, N_GDN_HEADS, N_GDN_HEADS, GDN_W)
    cuts = []
    acc = 0
    for s in sizes[:-1]:
        acc += s
        cuts.append(acc)

    for l in range(DEPTH):
        h = swiglu(rmsnorm(x, ffn1_pre_g[l]), ffn1_w_gate[l], ffn1_w_up[l], ffn1_w_down[l])
        x = x + 0.5 * rmsnorm(h, ffn1_post_g[l])

        hn = rmsnorm(x, mix_pre_g[l])
        proj = hn @ w_in[l]
        c_q, c_kv, k_pe_raw, qkv, a, b, gate = jnp.split(proj, cuts, axis=-1)

        mla_o = mla_group(c_q, c_kv, k_pe_raw, positions,
                          mla_q_norm_g[l], mla_w_uq[l], mla_kv_norm_g[l], mla_w_ukv[l])
        mla_o = rmsnorm(mla_o, mla_out_g[l])

        qkv = jax.nn.silu(causal_conv(qkv, gdn_conv_w[l]))
        q, k, v = jnp.split(qkv, 3, axis=-1)
        q = l2norm(q.reshape(B, T, H, dh))
        k = l2norm(k.reshape(B, T, H, dh))
        v = v.reshape(B, T, H, dh)
        g = -jnp.exp(gdn_a_log[l].astype(jnp.float32)) * jax.nn.softplus(
            a.astype(jnp.float32) + gdn_dt_bias[l].astype(jnp.float32))
        beta = jax.nn.sigmoid(b.astype(jnp.float32))
        o = gated_delta_rule(q, k, v, g, beta)
        o = rmsnorm(o, gdn_norm_g[l]) * jax.nn.silu(gate.reshape(B, T, H, dh))
        gdn_o = o.reshape(B, T, GDN_W)

        mixed = jnp.concatenate([mla_o, gdn_o], axis=-1) @ w_out[l]
        x = x + rmsnorm(mixed, mix_post_g[l])

        h = swiglu(rmsnorm(x, ffn2_pre_g[l]), ffn2_w_gate[l], ffn2_w_up[l], ffn2_w_down[l])
        x = x + 0.5 * rmsnorm(h, ffn2_post_g[l])
    return x


import jax as _jax
import jax.numpy as _jnp

TWIN_FORMAT = 'train_step'
FWD_PARAMS = ['x', 'positions', 'ffn1_pre_g', 'ffn1_w_gate', 'ffn1_w_up', 'ffn1_w_down', 'ffn1_post_g', 'mix_pre_g', 'w_in', 'mla_q_norm_g', 'mla_w_uq', 'mla_kv_norm_g', 'mla_w_ukv', 'mla_out_g', 'gdn_conv_w', 'gdn_a_log', 'gdn_dt_bias', 'gdn_norm_g', 'w_out', 'mix_post_g', 'ffn2_pre_g', 'ffn2_w_gate', 'ffn2_w_up', 'ffn2_w_down', 'ffn2_post_g']
TWIN_WEIGHTS = ['ffn1_pre_g', 'ffn1_w_gate', 'ffn1_w_up', 'ffn1_w_down', 'ffn1_post_g', 'mix_pre_g', 'w_in', 'mla_q_norm_g', 'mla_w_uq', 'mla_kv_norm_g', 'mla_w_ukv', 'mla_out_g', 'gdn_conv_w', 'gdn_a_log', 'gdn_dt_bias', 'gdn_norm_g', 'w_out', 'mix_post_g', 'ffn2_pre_g', 'ffn2_w_gate', 'ffn2_w_up', 'ffn2_w_down', 'ffn2_post_g']
TWIN_DIFF_INPUT = 'x'
TWIN_INPUTS = ['x', 'positions', 'ffn1_pre_g', 'ffn1_w_gate', 'ffn1_w_up', 'ffn1_w_down', 'ffn1_post_g', 'mix_pre_g', 'w_in', 'mla_q_norm_g', 'mla_w_uq', 'mla_kv_norm_g', 'mla_w_ukv', 'mla_out_g', 'gdn_conv_w', 'gdn_a_log', 'gdn_dt_bias', 'gdn_norm_g', 'w_out', 'mix_post_g', 'ffn2_pre_g', 'ffn2_w_gate', 'ffn2_w_up', 'ffn2_w_down', 'ffn2_post_g', 'loss_target', 'm_ffn1_pre_g', 'm_ffn1_w_gate', 'm_ffn1_w_up', 'm_ffn1_w_down', 'm_ffn1_post_g', 'm_mix_pre_g', 'm_w_in', 'm_mla_q_norm_g', 'm_mla_w_uq', 'm_mla_kv_norm_g', 'm_mla_w_ukv', 'm_mla_out_g', 'm_gdn_conv_w', 'm_gdn_a_log', 'm_gdn_dt_bias', 'm_gdn_norm_g', 'm_w_out', 'm_mix_post_g', 'm_ffn2_pre_g', 'm_ffn2_w_gate', 'm_ffn2_w_up', 'm_ffn2_w_down', 'm_ffn2_post_g', 'v_ffn1_pre_g', 'v_ffn1_w_gate', 'v_ffn1_w_up', 'v_ffn1_w_down', 'v_ffn1_post_g', 'v_mix_pre_g', 'v_w_in', 'v_mla_q_norm_g', 'v_mla_w_uq', 'v_mla_kv_norm_g', 'v_mla_w_ukv', 'v_mla_out_g', 'v_gdn_conv_w', 'v_gdn_a_log', 'v_gdn_dt_bias', 'v_gdn_norm_g', 'v_w_out', 'v_mix_post_g', 'v_ffn2_pre_g', 'v_ffn2_w_gate', 'v_ffn2_w_up', 'v_ffn2_w_down', 'v_ffn2_post_g']
TWIN_OUTPUTS = ['loss', 'grad_x', 'grad_ffn1_pre_g', 'grad_ffn1_w_gate', 'grad_ffn1_w_up', 'grad_ffn1_w_down', 'grad_ffn1_post_g', 'grad_mix_pre_g', 'grad_w_in', 'grad_mla_q_norm_g', 'grad_mla_w_uq', 'grad_mla_kv_norm_g', 'grad_mla_w_ukv', 'grad_mla_out_g', 'grad_gdn_conv_w', 'grad_gdn_a_log', 'grad_gdn_dt_bias', 'grad_gdn_norm_g', 'grad_w_out', 'grad_mix_post_g', 'grad_ffn2_pre_g', 'grad_ffn2_w_gate', 'grad_ffn2_w_up', 'grad_ffn2_w_down', 'grad_ffn2_post_g', 'delta_ffn1_pre_g', 'delta_ffn1_w_gate', 'delta_ffn1_w_up', 'delta_ffn1_w_down', 'delta_ffn1_post_g', 'delta_mix_pre_g', 'delta_w_in', 'delta_mla_q_norm_g', 'delta_mla_w_uq', 'delta_mla_kv_norm_g', 'delta_mla_w_ukv', 'delta_mla_out_g', 'delta_gdn_conv_w', 'delta_gdn_a_log', 'delta_gdn_dt_bias', 'delta_gdn_norm_g', 'delta_w_out', 'delta_mix_post_g', 'delta_ffn2_pre_g', 'delta_ffn2_w_gate', 'delta_ffn2_w_up', 'delta_ffn2_w_down', 'delta_ffn2_post_g', 'new_m_ffn1_pre_g', 'new_m_ffn1_w_gate', 'new_m_ffn1_w_up', 'new_m_ffn1_w_down', 'new_m_ffn1_post_g', 'new_m_mix_pre_g', 'new_m_w_in', 'new_m_mla_q_norm_g', 'new_m_mla_w_uq', 'new_m_mla_kv_norm_g', 'new_m_mla_w_ukv', 'new_m_mla_out_g', 'new_m_gdn_conv_w', 'new_m_gdn_a_log', 'new_m_gdn_dt_bias', 'new_m_gdn_norm_g', 'new_m_w_out', 'new_m_mix_post_g', 'new_m_ffn2_pre_g', 'new_m_ffn2_w_gate', 'new_m_ffn2_w_up', 'new_m_ffn2_w_down', 'new_m_ffn2_post_g', 'new_v_ffn1_pre_g', 'new_v_ffn1_w_gate', 'new_v_ffn1_w_up', 'new_v_ffn1_w_down', 'new_v_ffn1_post_g', 'new_v_mix_pre_g', 'new_v_w_in', 'new_v_mla_q_norm_g', 'new_v_mla_w_uq', 'new_v_mla_kv_norm_g', 'new_v_mla_w_ukv', 'new_v_mla_out_g', 'new_v_gdn_conv_w', 'new_v_gdn_a_log', 'new_v_gdn_dt_bias', 'new_v_gdn_norm_g', 'new_v_w_out', 'new_v_mix_post_g', 'new_v_ffn2_pre_g', 'new_v_ffn2_w_gate', 'new_v_ffn2_w_up', 'new_v_ffn2_w_down', 'new_v_ffn2_post_g']
TWIN_LEAF_KINDS = {'loss': 'loss', 'grad_x': 'grad_x', 'grad_ffn1_pre_g': 'grad_w', 'grad_ffn1_w_gate': 'grad_w', 'grad_ffn1_w_up': 'grad_w', 'grad_ffn1_w_down': 'grad_w', 'grad_ffn1_post_g': 'grad_w', 'grad_mix_pre_g': 'grad_w', 'grad_w_in': 'grad_w', 'grad_mla_q_norm_g': 'grad_w', 'grad_mla_w_uq': 'grad_w', 'grad_mla_kv_norm_g': 'grad_w', 'grad_mla_w_ukv': 'grad_w', 'grad_mla_out_g': 'grad_w', 'grad_gdn_conv_w': 'grad_w', 'grad_gdn_a_log': 'grad_w', 'grad_gdn_dt_bias': 'grad_w', 'grad_gdn_norm_g': 'grad_w', 'grad_w_out': 'grad_w', 'grad_mix_post_g': 'grad_w', 'grad_ffn2_pre_g': 'grad_w', 'grad_ffn2_w_gate': 'grad_w', 'grad_ffn2_w_up': 'grad_w', 'grad_ffn2_w_down': 'grad_w', 'grad_ffn2_post_g': 'grad_w', 'delta_ffn1_pre_g': 'delta_w', 'delta_ffn1_w_gate': 'delta_w', 'delta_ffn1_w_up': 'delta_w', 'delta_ffn1_w_down': 'delta_w', 'delta_ffn1_post_g': 'delta_w', 'delta_mix_pre_g': 'delta_w', 'delta_w_in': 'delta_w', 'delta_mla_q_norm_g': 'delta_w', 'delta_mla_w_uq': 'delta_w', 'delta_mla_kv_norm_g': 'delta_w', 'delta_mla_w_ukv': 'delta_w', 'delta_mla_out_g': 'delta_w', 'delta_gdn_conv_w': 'delta_w', 'delta_gdn_a_log': 'delta_w', 'delta_gdn_dt_bias': 'delta_w', 'delta_gdn_norm_g': 'delta_w', 'delta_w_out': 'delta_w', 'delta_mix_post_g': 'delta_w', 'delta_ffn2_pre_g': 'delta_w', 'delta_ffn2_w_gate': 'delta_w', 'delta_ffn2_w_up': 'delta_w', 'delta_ffn2_w_down': 'delta_w', 'delta_ffn2_post_g': 'delta_w', 'new_m_ffn1_pre_g': 'new_m', 'new_m_ffn1_w_gate': 'new_m', 'new_m_ffn1_w_up': 'new_m', 'new_m_ffn1_w_down': 'new_m', 'new_m_ffn1_post_g': 'new_m', 'new_m_mix_pre_g': 'new_m', 'new_m_w_in': 'new_m', 'new_m_mla_q_norm_g': 'new_m', 'new_m_mla_w_uq': 'new_m', 'new_m_mla_kv_norm_g': 'new_m', 'new_m_mla_w_ukv': 'new_m', 'new_m_mla_out_g': 'new_m', 'new_m_gdn_conv_w': 'new_m', 'new_m_gdn_a_log': 'new_m', 'new_m_gdn_dt_bias': 'new_m', 'new_m_gdn_norm_g': 'new_m', 'new_m_w_out': 'new_m', 'new_m_mix_post_g': 'new_m', 'new_m_ffn2_pre_g': 'new_m', 'new_m_ffn2_w_gate': 'new_m', 'new_m_ffn2_w_up': 'new_m', 'new_m_ffn2_w_down': 'new_m', 'new_m_ffn2_post_g': 'new_m', 'new_v_ffn1_pre_g': 'new_v', 'new_v_ffn1_w_gate': 'new_v', 'new_v_ffn1_w_up': 'new_v', 'new_v_ffn1_w_down': 'new_v', 'new_v_ffn1_post_g': 'new_v', 'new_v_mix_pre_g': 'new_v', 'new_v_w_in': 'new_v', 'new_v_mla_q_norm_g': 'new_v', 'new_v_mla_w_uq': 'new_v', 'new_v_mla_kv_norm_g': 'new_v', 'new_v_mla_w_ukv': 'new_v', 'new_v_mla_out_g': 'new_v', 'new_v_gdn_conv_w': 'new_v', 'new_v_gdn_a_log': 'new_v', 'new_v_gdn_dt_bias': 'new_v', 'new_v_gdn_norm_g': 'new_v', 'new_v_w_out': 'new_v', 'new_v_mix_post_g': 'new_v', 'new_v_ffn2_pre_g': 'new_v', 'new_v_ffn2_w_gate': 'new_v', 'new_v_ffn2_w_up': 'new_v', 'new_v_ffn2_w_down': 'new_v', 'new_v_ffn2_post_g': 'new_v'}


def _forward(args):
    return _fwd_reference(*[args[k] for k in FWD_PARAMS])


def _output_shape():
    out = _jax.eval_shape(lambda: _forward(_fwd_setup_inputs(0)))
    return out.shape, out.dtype

N_MICROBATCH = 1
ADAM_LR = 0.001
ADAM_B1 = 0.9
ADAM_B2 = 0.999
ADAM_EPS = 1e-08
ADAM_WD = 0.01
ADAM_STEP = 10
PER_EXAMPLE_BATCH_AXIS = {'x': 0, 'positions': 0, 'loss_target': 0}
SHARED_INPUTS = []
_WEIGHT_DTYPES = {'ffn1_pre_g': _jnp.float32, 'ffn1_w_gate': _jnp.float32, 'ffn1_w_up': _jnp.float32, 'ffn1_w_down': _jnp.float32, 'ffn1_post_g': _jnp.float32, 'mix_pre_g': _jnp.float32, 'w_in': _jnp.float32, 'mla_q_norm_g': _jnp.float32, 'mla_w_uq': _jnp.float32, 'mla_kv_norm_g': _jnp.float32, 'mla_w_ukv': _jnp.float32, 'mla_out_g': _jnp.float32, 'gdn_conv_w': _jnp.float32, 'gdn_a_log': _jnp.float32, 'gdn_dt_bias': _jnp.float32, 'gdn_norm_g': _jnp.float32, 'w_out': _jnp.float32, 'mix_post_g': _jnp.float32, 'ffn2_pre_g': _jnp.float32, 'ffn2_w_gate': _jnp.float32, 'ffn2_w_up': _jnp.float32, 'ffn2_w_down': _jnp.float32, 'ffn2_post_g': _jnp.float32}
MOMENT_SCALE = {'ffn1_pre_g': 4.498600e-01, 'ffn1_w_gate': 1.889439e-01, 'ffn1_w_up': 1.970042e-01, 'ffn1_w_down': 3.336414e-01, 'ffn1_post_g': 7.863599e+00, 'mix_pre_g': 7.387690e-01, 'w_in': 4.497904e-01, 'mla_q_norm_g': 1.129047e+00, 'mla_w_uq': 6.494593e-01, 'mla_kv_norm_g': 2.959366e+00, 'mla_w_ukv': 8.465409e-01, 'mla_out_g': 9.603138e-01, 'gdn_conv_w': 2.611682e-01, 'gdn_a_log': 6.364789e-01, 'gdn_dt_bias': 6.112184e-01, 'gdn_norm_g': 8.614228e-01, 'w_out': 6.770869e-01, 'mix_post_g': 3.204612e+01, 'ffn2_pre_g': 4.333868e-01, 'ffn2_w_gate': 1.314709e-01, 'ffn2_w_up': 2.057549e-01, 'ffn2_w_down': 3.400755e-01, 'ffn2_post_g': 7.986411e+00}


def _to_microbatches(a, axis):
    t = _jnp.moveaxis(a, axis, 0)
    t = t.reshape((N_MICROBATCH, t.shape[0] // N_MICROBATCH) + t.shape[1:])
    return _jnp.moveaxis(t, 1, axis + 1)


def setup_inputs(seed: int = 0) -> dict:
    inp = _fwd_setup_inputs(seed)
    key = _jax.random.fold_in(_jax.random.key(seed), 7919)
    shape, _ = _output_shape()
    out = dict(inp)
    out["loss_target"] = _jax.random.normal(_jax.random.fold_in(key, 0), shape, _jnp.float32)
    for i, name in enumerate(TWIN_WEIGHTS):
        w = inp[name].astype(_jnp.float32)
        if MOMENT_SCALE is None:
            s = _jnp.sqrt(_jnp.mean(_jnp.square(w)) + 1e-30)
        else:
            s = MOMENT_SCALE[name]
        km, kv = _jax.random.split(_jax.random.fold_in(key, i + 1))
        out[name] = w
        out["m_" + name] = s * _jax.random.normal(km, w.shape, _jnp.float32)
        out["v_" + name] = (s * s) * _jax.random.uniform(kv, w.shape, _jnp.float32, 0.5, 1.5)
    if N_MICROBATCH > 1:
        for name, axis in PER_EXAMPLE_BATCH_AXIS.items():
            out[name] = _to_microbatches(out[name], axis)
    return {'x': out['x'], 'positions': out['positions'], 'ffn1_pre_g': out['ffn1_pre_g'], 'ffn1_w_gate': out['ffn1_w_gate'], 'ffn1_w_up': out['ffn1_w_up'], 'ffn1_w_down': out['ffn1_w_down'], 'ffn1_post_g': out['ffn1_post_g'], 'mix_pre_g': out['mix_pre_g'], 'w_in': out['w_in'], 'mla_q_norm_g': out['mla_q_norm_g'], 'mla_w_uq': out['mla_w_uq'], 'mla_kv_norm_g': out['mla_kv_norm_g'], 'mla_w_ukv': out['mla_w_ukv'], 'mla_out_g': out['mla_out_g'], 'gdn_conv_w': out['gdn_conv_w'], 'gdn_a_log': out['gdn_a_log'], 'gdn_dt_bias': out['gdn_dt_bias'], 'gdn_norm_g': out['gdn_norm_g'], 'w_out': out['w_out'], 'mix_post_g': out['mix_post_g'], 'ffn2_pre_g': out['ffn2_pre_g'], 'ffn2_w_gate': out['ffn2_w_gate'], 'ffn2_w_up': out['ffn2_w_up'], 'ffn2_w_down': out['ffn2_w_down'], 'ffn2_post_g': out['ffn2_post_g'], 'loss_target': out['loss_target'], 'm_ffn1_pre_g': out['m_ffn1_pre_g'], 'm_ffn1_w_gate': out['m_ffn1_w_gate'], 'm_ffn1_w_up': out['m_ffn1_w_up'], 'm_ffn1_w_down': out['m_ffn1_w_down'], 'm_ffn1_post_g': out['m_ffn1_post_g'], 'm_mix_pre_g': out['m_mix_pre_g'], 'm_w_in': out['m_w_in'], 'm_mla_q_norm_g': out['m_mla_q_norm_g'], 'm_mla_w_uq': out['m_mla_w_uq'], 'm_mla_kv_norm_g': out['m_mla_kv_norm_g'], 'm_mla_w_ukv': out['m_mla_w_ukv'], 'm_mla_out_g': out['m_mla_out_g'], 'm_gdn_conv_w': out['m_gdn_conv_w'], 'm_gdn_a_log': out['m_gdn_a_log'], 'm_gdn_dt_bias': out['m_gdn_dt_bias'], 'm_gdn_norm_g': out['m_gdn_norm_g'], 'm_w_out': out['m_w_out'], 'm_mix_post_g': out['m_mix_post_g'], 'm_ffn2_pre_g': out['m_ffn2_pre_g'], 'm_ffn2_w_gate': out['m_ffn2_w_gate'], 'm_ffn2_w_up': out['m_ffn2_w_up'], 'm_ffn2_w_down': out['m_ffn2_w_down'], 'm_ffn2_post_g': out['m_ffn2_post_g'], 'v_ffn1_pre_g': out['v_ffn1_pre_g'], 'v_ffn1_w_gate': out['v_ffn1_w_gate'], 'v_ffn1_w_up': out['v_ffn1_w_up'], 'v_ffn1_w_down': out['v_ffn1_w_down'], 'v_ffn1_post_g': out['v_ffn1_post_g'], 'v_mix_pre_g': out['v_mix_pre_g'], 'v_w_in': out['v_w_in'], 'v_mla_q_norm_g': out['v_mla_q_norm_g'], 'v_mla_w_uq': out['v_mla_w_uq'], 'v_mla_kv_norm_g': out['v_mla_kv_norm_g'], 'v_mla_w_ukv': out['v_mla_w_ukv'], 'v_mla_out_g': out['v_mla_out_g'], 'v_gdn_conv_w': out['v_gdn_conv_w'], 'v_gdn_a_log': out['v_gdn_a_log'], 'v_gdn_dt_bias': out['v_gdn_dt_bias'], 'v_gdn_norm_g': out['v_gdn_norm_g'], 'v_w_out': out['v_w_out'], 'v_mix_post_g': out['v_mix_post_g'], 'v_ffn2_pre_g': out['v_ffn2_pre_g'], 'v_ffn2_w_gate': out['v_ffn2_w_gate'], 'v_ffn2_w_up': out['v_ffn2_w_up'], 'v_ffn2_w_down': out['v_ffn2_w_down'], 'v_ffn2_post_g': out['v_ffn2_post_g']}


def _loss(weights, diff, rest, loss_target):
    with _jax.named_scope("forward"):
        args = {**rest, TWIN_DIFF_INPUT: diff, **{k: w.astype(_WEIGHT_DTYPES[k]) for k, w in weights.items()}}
        y = _forward(args)
    with _jax.named_scope("loss_head"):
        err = _jnp.square(y.astype(_jnp.float32) - loss_target)
        return 0.5 * _jnp.sum(_jnp.mean(err, axis=-1)) if err.ndim else 0.5 * err


def _adamw(w, g, m, v):
    m = ADAM_B1 * m + (1.0 - ADAM_B1) * g
    v = ADAM_B2 * v + (1.0 - ADAM_B2) * _jnp.square(g)
    m_hat = m / (1.0 - ADAM_B1 ** ADAM_STEP)
    v_hat = v / (1.0 - ADAM_B2 ** ADAM_STEP)
    delta = -ADAM_LR * (m_hat / (_jnp.sqrt(v_hat) + ADAM_EPS) + ADAM_WD * w)
    return delta, m, v


def reference(x, positions, ffn1_pre_g, ffn1_w_gate, ffn1_w_up, ffn1_w_down, ffn1_post_g, mix_pre_g, w_in, mla_q_norm_g, mla_w_uq, mla_kv_norm_g, mla_w_ukv, mla_out_g, gdn_conv_w, gdn_a_log, gdn_dt_bias, gdn_norm_g, w_out, mix_post_g, ffn2_pre_g, ffn2_w_gate, ffn2_w_up, ffn2_w_down, ffn2_post_g, loss_target, m_ffn1_pre_g, m_ffn1_w_gate, m_ffn1_w_up, m_ffn1_w_down, m_ffn1_post_g, m_mix_pre_g, m_w_in, m_mla_q_norm_g, m_mla_w_uq, m_mla_kv_norm_g, m_mla_w_ukv, m_mla_out_g, m_gdn_conv_w, m_gdn_a_log, m_gdn_dt_bias, m_gdn_norm_g, m_w_out, m_mix_post_g, m_ffn2_pre_g, m_ffn2_w_gate, m_ffn2_w_up, m_ffn2_w_down, m_ffn2_post_g, v_ffn1_pre_g, v_ffn1_w_gate, v_ffn1_w_up, v_ffn1_w_down, v_ffn1_post_g, v_mix_pre_g, v_w_in, v_mla_q_norm_g, v_mla_w_uq, v_mla_kv_norm_g, v_mla_w_ukv, v_mla_out_g, v_gdn_conv_w, v_gdn_a_log, v_gdn_dt_bias, v_gdn_norm_g, v_w_out, v_mix_post_g, v_ffn2_pre_g, v_ffn2_w_gate, v_ffn2_w_up, v_ffn2_w_down, v_ffn2_post_g):
    given = dict(x=x, positions=positions, ffn1_pre_g=ffn1_pre_g, ffn1_w_gate=ffn1_w_gate, ffn1_w_up=ffn1_w_up, ffn1_w_down=ffn1_w_down, ffn1_post_g=ffn1_post_g, mix_pre_g=mix_pre_g, w_in=w_in, mla_q_norm_g=mla_q_norm_g, mla_w_uq=mla_w_uq, mla_kv_norm_g=mla_kv_norm_g, mla_w_ukv=mla_w_ukv, mla_out_g=mla_out_g, gdn_conv_w=gdn_conv_w, gdn_a_log=gdn_a_log, gdn_dt_bias=gdn_dt_bias, gdn_norm_g=gdn_norm_g, w_out=w_out, mix_post_g=mix_post_g, ffn2_pre_g=ffn2_pre_g, ffn2_w_gate=ffn2_w_gate, ffn2_w_up=ffn2_w_up, ffn2_w_down=ffn2_w_down, ffn2_post_g=ffn2_post_g, loss_target=loss_target, m_ffn1_pre_g=m_ffn1_pre_g, m_ffn1_w_gate=m_ffn1_w_gate, m_ffn1_w_up=m_ffn1_w_up, m_ffn1_w_down=m_ffn1_w_down, m_ffn1_post_g=m_ffn1_post_g, m_mix_pre_g=m_mix_pre_g, m_w_in=m_w_in, m_mla_q_norm_g=m_mla_q_norm_g, m_mla_w_uq=m_mla_w_uq, m_mla_kv_norm_g=m_mla_kv_norm_g, m_mla_w_ukv=m_mla_w_ukv, m_mla_out_g=m_mla_out_g, m_gdn_conv_w=m_gdn_conv_w, m_gdn_a_log=m_gdn_a_log, m_gdn_dt_bias=m_gdn_dt_bias, m_gdn_norm_g=m_gdn_norm_g, m_w_out=m_w_out, m_mix_post_g=m_mix_post_g, m_ffn2_pre_g=m_ffn2_pre_g, m_ffn2_w_gate=m_ffn2_w_gate, m_ffn2_w_up=m_ffn2_w_up, m_ffn2_w_down=m_ffn2_w_down, m_ffn2_post_g=m_ffn2_post_g, v_ffn1_pre_g=v_ffn1_pre_g, v_ffn1_w_gate=v_ffn1_w_gate, v_ffn1_w_up=v_ffn1_w_up, v_ffn1_w_down=v_ffn1_w_down, v_ffn1_post_g=v_ffn1_post_g, v_mix_pre_g=v_mix_pre_g, v_w_in=v_w_in, v_mla_q_norm_g=v_mla_q_norm_g, v_mla_w_uq=v_mla_w_uq, v_mla_kv_norm_g=v_mla_kv_norm_g, v_mla_w_ukv=v_mla_w_ukv, v_mla_out_g=v_mla_out_g, v_gdn_conv_w=v_gdn_conv_w, v_gdn_a_log=v_gdn_a_log, v_gdn_dt_bias=v_gdn_dt_bias, v_gdn_norm_g=v_gdn_norm_g, v_w_out=v_w_out, v_mix_post_g=v_mix_post_g, v_ffn2_pre_g=v_ffn2_pre_g, v_ffn2_w_gate=v_ffn2_w_gate, v_ffn2_w_up=v_ffn2_w_up, v_ffn2_w_down=v_ffn2_w_down, v_ffn2_post_g=v_ffn2_post_g)
    weights = {n: given[n] for n in TWIN_WEIGHTS}
    shared = {n: given[n] for n in SHARED_INPUTS}
    per_example = {n: given[n] for n in ['x', 'positions']}
    grad_fn = _jax.value_and_grad(_loss, argnums=(0, 1))

    def one_microbatch(ex, loss_target):
        ex = dict(ex)
        diff = ex.pop(TWIN_DIFF_INPUT)
        return grad_fn(weights, diff, {**shared, **ex}, loss_target)

    if N_MICROBATCH == 1:
        loss, (grad_w, grad_x) = one_microbatch(per_example, given["loss_target"])
    else:
        def body(carry, xs):
            loss_sum, grad_sum = carry
            l_k, (gw_k, gx_k) = one_microbatch(xs[0], xs[1])
            with _jax.named_scope("update"):
                return (loss_sum + l_k, _jax.tree.map(_jnp.add, grad_sum, gw_k)), gx_k

        init = (_jnp.zeros((), _jnp.float32), _jax.tree.map(_jnp.zeros_like, weights))
        (loss, grad_w), grad_x = _jax.lax.scan(body, init, (per_example, given["loss_target"]))
    with _jax.named_scope("update"):
        delta_w, new_m, new_v = {}, {}, {}
        for n in TWIN_WEIGHTS:
            delta_w[n], new_m[n], new_v[n] = _adamw(weights[n], grad_w[n], given["m_" + n], given["v_" + n])
    return (loss, grad_x, *[grad_w[n] for n in TWIN_WEIGHTS], *[delta_w[n] for n in TWIN_WEIGHTS],
            *[new_m[n] for n in TWIN_WEIGHTS], *[new_v[n] for n in TWIN_WEIGHTS])
```

```python
import jax
import jax.numpy as jnp
from jax import lax
from jax.experimental import pallas as pl
from jax.experimental.pallas import tpu as pltpu

F32 = jnp.float32
BF16 = jnp.bfloat16
MM_DTYPE = BF16
HI = lax.Precision.HIGHEST
MESH = pl.DeviceIdType.MESH

D_MODEL = 1024
D_FF = 2816
N_HEADS = 8
MLA_Q_RANK = 256
MLA_KV_RANK = 128
MLA_NOPE = 64
MLA_ROPE = 32
MLA_V = 64
ROPE_THETA = 10000.0
GDN_DH = 64
GDN_W = N_HEADS * GDN_DH
GDN_CONV = 4
CHUNK = 64
HEAD_LANES = 128
MLA_PAD = N_HEADS * HEAD_LANES
EPS = 1e-6
N_SHARD = 4
LANES = 1024

PIN_CQ = 0
PIN_CKV = 256
PIN_KPE = 384
PIN_QKV = 512
PIN_AB = 2048
PIN_GATE = 2176
PIN_W = 2688

ADAM_LR = 0.001
ADAM_B1 = 0.9
ADAM_B2 = 0.999
ADAM_EPS = 1e-08
ADAM_WD = 0.01
ADAM_STEP = 10

VMEM_LIMIT_V7X = 56 * 1024 * 1024

BIG = ["ffn1_w_gate", "ffn1_w_up", "ffn1_w_down", "w_in", "mla_w_uq", "mla_w_ukv", "gdn_conv_w", "w_out",
       "ffn2_w_gate", "ffn2_w_up", "ffn2_w_down"]
SMALL = ["ffn1_pre_g", "ffn1_post_g", "mix_pre_g", "mla_q_norm_g", "mla_kv_norm_g", "mla_out_g", "gdn_a_log",
         "gdn_dt_bias", "gdn_norm_g", "mix_post_g", "ffn2_pre_g", "ffn2_post_g"]
WEIGHTS = ["ffn1_pre_g", "ffn1_w_gate", "ffn1_w_up", "ffn1_w_down", "ffn1_post_g", "mix_pre_g", "w_in",
           "mla_q_norm_g", "mla_w_uq", "mla_kv_norm_g", "mla_w_ukv", "mla_out_g", "gdn_conv_w", "gdn_a_log",
           "gdn_dt_bias", "gdn_norm_g", "w_out", "mix_post_g", "ffn2_pre_g", "ffn2_w_gate", "ffn2_w_up",
           "ffn2_w_down", "ffn2_post_g"]
SHARD_AXIS = {"ffn1_w_gate": 1, "ffn1_w_up": 1, "ffn1_w_down": 0, "w_in": 1, "mla_w_uq": 1, "mla_w_ukv": 1,
              "gdn_conv_w": 1, "w_out": 0, "ffn2_w_gate": 1, "ffn2_w_up": 1, "ffn2_w_down": 0}
SMALL_ROWS = 16


def _params(sem):
    return pltpu.CompilerParams(dimension_semantics=sem, vmem_limit_bytes=VMEM_LIMIT_V7X)


def _pick(dim, pref):
    if dim <= pref:
        return dim
    t = (pref // 128) * 128
    while t >= 128:
        if dim % t == 0:
            return t
        t -= 128
    return dim


def _rowwise(name, fn, row_ins, bc_ins, row_outs, acc_outs, tb):
    ents = []
    for e in row_ins:
        ents.append(e if isinstance(e, tuple) else (e, e.shape[1], 0, 0))
    over = [o[2] for o in row_outs if len(o) == 3]
    rows = over[0] if over else ents[0][0].shape[0]
    steps = rows // tb
    assert steps * tb == rows, (name, rows, tb)
    in_specs, args = [], []
    for a, w, j, r0 in ents:
        in_specs.append(pl.BlockSpec((tb, w), lambda i, j=j, r0=r0: (i + r0, j)))
        args.append(a)
    for b in bc_ins:
        in_specs.append(pl.BlockSpec(b.shape, lambda i: (0, 0)))
        args.append(b)
    out_shape = [jax.ShapeDtypeStruct((rows, o[0]), o[1]) for o in row_outs]
    out_shape += [jax.ShapeDtypeStruct((r, c), F32) for r, c in acc_outs]
    out_specs = [pl.BlockSpec((tb, o[0]), lambda i: (i, 0)) for o in row_outs]
    out_specs += [pl.BlockSpec((r, c), lambda i: (0, 0)) for r, c in acc_outs]
    n_in, n_ro, n_acc = len(args), len(row_outs), len(acc_outs)

    def body(*refs):
        vals = fn(*[r[...] for r in refs[:n_in]])
        if not isinstance(vals, (tuple, list)):
            vals = (vals,)
        for r, v in zip(refs[n_in:n_in + n_ro], vals[:n_ro]):
            r[...] = v.astype(r.dtype)
        if n_acc:
            acc_refs = refs[n_in + n_ro:]

            @pl.when(pl.program_id(0) == 0)
            def _():
                for r in acc_refs:
                    r[...] = jnp.zeros(r.shape, r.dtype)

            for r, v in zip(acc_refs, vals[n_ro:]):
                r[...] += v

    outs = pl.pallas_call(body, name=name, grid=(steps,), in_specs=in_specs, out_specs=out_specs,
                          out_shape=out_shape, compiler_params=_params(("arbitrary",)))(*args)
    return outs


def _mm(name, a, b, mode, out_dtype, tm=512, tn=512, tk=1024):
    if mode == "nn":
        (m, k), (k2, n) = a.shape, b.shape
    elif mode == "nt":
        (m, k), (n, k2) = a.shape, b.shape
    else:
        (k, m), (k2, n) = a.shape, b.shape
    assert k == k2, (name, a.shape, b.shape)
    tm, tn, tk = _pick(m, tm), _pick(n, tn), _pick(k, tk)
    nk = k // tk
    if mode == "nn":
        a_spec = pl.BlockSpec((tm, tk), lambda i, j, kk: (i, kk))
        b_spec = pl.BlockSpec((tk, tn), lambda i, j, kk: (kk, j))
        dims = (((1,), (0,)), ((), ()))
    elif mode == "nt":
        a_spec = pl.BlockSpec((tm, tk), lambda i, j, kk: (i, kk))
        b_spec = pl.BlockSpec((tn, tk), lambda i, j, kk: (j, kk))
        dims = (((1,), (1,)), ((), ()))
    else:
        a_spec = pl.BlockSpec((tk, tm), lambda i, j, kk: (kk, i))
        b_spec = pl.BlockSpec((tk, tn), lambda i, j, kk: (kk, j))
        dims = (((0,), (0,)), ((), ()))

    def body(a_ref, b_ref, o_ref, acc_ref):
        kk = pl.program_id(2)

        @pl.when(kk == 0)
        def _():
            acc_ref[...] = jnp.zeros(acc_ref.shape, F32)

        acc_ref[...] += lax.dot_general(a_ref[...].astype(MM_DTYPE), b_ref[...].astype(MM_DTYPE), dims,
                                        preferred_element_type=F32)

        @pl.when(kk == nk - 1)
        def _():
            o_ref[...] = acc_ref[...].astype(o_ref.dtype)

    return pl.pallas_call(
        body, name=name, grid=(m // tm, n // tn, nk), in_specs=[a_spec, b_spec],
        out_specs=pl.BlockSpec((tm, tn), lambda i, j, kk: (i, j)),
        out_shape=jax.ShapeDtypeStruct((m, n), out_dtype),
        scratch_shapes=[pltpu.VMEM((tm, tn), F32)],
        compiler_params=_params(("parallel", "parallel", "arbitrary")))(a, b)


def _rms_stats(x, n_real=None):
    n = x.shape[-1] if n_real is None else n_real
    return lax.rsqrt(jnp.sum(x * x, axis=-1, keepdims=True) / n + EPS)


def _rms_bwd(x, r, g, dz, n_real=None):
    n = x.shape[-1] if n_real is None else n_real
    xh = x * r
    dxh = dz * g
    dx = r * (dxh - xh * (jnp.sum(dxh * xh, axis=-1, keepdims=True) / n))
    return dx, jnp.sum(dz * xh, axis=0, keepdims=True)


def _sigmoid(x):
    return 1.0 / (1.0 + jnp.exp(-x))


def _roll(x, s, axis):
    return pltpu.roll(x, s, axis)


def _rope(x, c, s1, s2):
    return x * c + _roll(x, HEAD_LANES - MLA_ROPE // 2, 1) * s1 + _roll(x, MLA_ROPE // 2, 1) * s2


def _heads_apply(x, fn):
    return jnp.concatenate([fn(x[:, h * HEAD_LANES:(h + 1) * HEAD_LANES]) for h in range(N_HEADS)], axis=1)


def _ffn_fwd(tag, x, g_pre, wgu, wd, g_post, tb):
    d = x.shape[1]

    def pre(xb, g):
        return (xb * _rms_stats(xb) * g,)

    (n,) = _rowwise(tag + "_pre", pre, [x], [g_pre], [(d, BF16)], [], tb)
    au = _mm(tag + "_gu", n, wgu, "nn", BF16)

    def act(a, u):
        a = a.astype(F32)
        return ((a * _sigmoid(a)) * u.astype(F32),)

    (s,) = _rowwise(tag + "_act", act, [(au, D_FF, 0, 0), (au, D_FF, 1, 0)], [], [(D_FF, BF16)], [], tb // 2)
    h = _mm(tag + "_down", s, wd, "nn", F32)

    def post(xb, hb, g):
        return (xb + 0.5 * (hb * _rms_stats(hb) * g),)

    (y,) = _rowwise(tag + "_post", post, [x, h], [g_post], [(d, F32)], [], tb)
    return y, (x, n, au, s, h)


def _ffn_bwd(tag, dy, saved, g_pre, wgu, wd, g_post, tb):
    x, n, au, s, h = saved
    d = x.shape[1]

    def post_b(hb, dyb, g):
        dh, dg = _rms_bwd(hb, _rms_stats(hb), g, 0.5 * dyb)
        return dh, dg

    dh, dg_post = _rowwise(tag + "_post_b", post_b, [h, dy], [g_post], [(d, BF16)], [(1, d)], tb)
    ds = _mm(tag + "_down_bx", dh, wd, "nt", BF16)
    dwd = _mm(tag + "_down_bw", s, dh, "tn", F32)

    def act_b(a, u, dsb):
        a, u, dsb = a.astype(F32), u.astype(F32), dsb.astype(F32)
        sg = _sigmoid(a)
        da = dsb * u * (sg * (1.0 + a * (1.0 - sg)))
        du = dsb * (a * sg)
        return (jnp.concatenate([da, du], axis=1),)

    (dau,) = _rowwise(tag + "_act_b", act_b, [(au, D_FF, 0, 0), (au, D_FF, 1, 0), ds], [], [(2 * D_FF, BF16)],
                      [], tb // 2)
    dn = _mm(tag + "_gu_bx", dau, wgu, "nt", F32)
    dwgu = _mm(tag + "_gu_bw", n, dau, "tn", F32)

    def pre_b(xb, dnb, dyb, g):
        dx, dg = _rms_bwd(xb, _rms_stats(xb), g, dnb)
        return dyb + dx, dg

    dx, dg_pre = _rowwise(tag + "_pre_b", pre_b, [x, dn, dy], [g_pre], [(d, F32)], [(1, d)], tb)
    return dx, dg_pre, dwgu[:, :D_FF], dwgu[:, D_FF:], dwd, dg_post


NEG = -1e30


def _attn_scale():
    return (MLA_NOPE + MLA_ROPE) ** -0.5


def _attn_fwd(q, k, v, tq):
    t = q.shape[0]
    nq = t // tq
    scale = _attn_scale()

    def body(q_ref, k_ref, v_ref, o_ref, lse_ref, m_s, l_s, acc_s):
        qi, ki = pl.program_id(1), pl.program_id(2)

        @pl.when(ki == 0)
        def _():
            m_s[...] = jnp.full(m_s.shape, NEG, F32)
            l_s[...] = jnp.zeros(l_s.shape, F32)
            acc_s[...] = jnp.zeros(acc_s.shape, F32)

        @pl.when(ki <= qi)
        def _():
            s = lax.dot_general(q_ref[...], k_ref[...], (((1,), (1,)), ((), ())), preferred_element_type=F32) * scale
            row = lax.broadcasted_iota(jnp.int32, s.shape, 0) + qi * tq
            col = lax.broadcasted_iota(jnp.int32, s.shape, 1) + ki * tq
            s = jnp.where(col <= row, s, NEG)
            m_old = m_s[...]
            m_new = jnp.maximum(m_old, jnp.max(s, axis=1, keepdims=True))
            alpha = jnp.exp(m_old - m_new)
            p = jnp.exp(s - m_new[:, :1])
            l_s[...] = l_s[...] * alpha + jnp.sum(p, axis=1, keepdims=True)
            acc_s[...] = acc_s[...] * alpha + jnp.dot(p.astype(MM_DTYPE), v_ref[...], preferred_element_type=F32)
            m_s[...] = m_new

        @pl.when(ki == qi)
        def _():
            o_ref[...] = acc_s[...] / l_s[...]
            lse_ref[...] = m_s[...] + jnp.log(l_s[...])

    blk = lambda h, qi, ki: (qi, h)
    kblk = lambda h, qi, ki: (jnp.minimum(ki, qi), h)
    return pl.pallas_call(
        body, name="mla_attn_fwd", grid=(N_HEADS, nq, nq),
        in_specs=[pl.BlockSpec((tq, HEAD_LANES), blk), pl.BlockSpec((tq, HEAD_LANES), kblk),
                  pl.BlockSpec((tq, HEAD_LANES), kblk)],
        out_specs=[pl.BlockSpec((tq, HEAD_LANES), blk), pl.BlockSpec((tq, HEAD_LANES), blk)],
        out_shape=[jax.ShapeDtypeStruct((t, MLA_PAD), F32), jax.ShapeDtypeStruct((t, MLA_PAD), F32)],
        scratch_shapes=[pltpu.VMEM((tq, HEAD_LANES), F32)] * 3,
        compiler_params=_params(("parallel", "parallel", "arbitrary")))(q, k, v)


def _attn_probs(q, k, lse, qi, ki, tq, scale):
    s = lax.dot_general(q, k, (((1,), (1,)), ((), ())), preferred_element_type=F32) * scale
    row = lax.broadcasted_iota(jnp.int32, s.shape, 0) + qi * tq
    col = lax.broadcasted_iota(jnp.int32, s.shape, 1) + ki * tq
    return jnp.where(col <= row, jnp.exp(s - lse[:, :1]), 0.0)


def _attn_bwd_dq(q, k, v, do, lse, delta, tq):
    t = q.shape[0]
    nq = t // tq
    scale = _attn_scale()

    def body(q_ref, k_ref, v_ref, do_ref, lse_ref, dl_ref, dq_ref, acc_s):
        qi, ki = pl.program_id(1), pl.program_id(2)

        @pl.when(ki == 0)
        def _():
            acc_s[...] = jnp.zeros(acc_s.shape, F32)

        @pl.when(ki <= qi)
        def _():
            p = _attn_probs(q_ref[...], k_ref[...], lse_ref[...], qi, ki, tq, scale)
            dp = lax.dot_general(do_ref[...], v_ref[...], (((1,), (1,)), ((), ())), preferred_element_type=F32)
            ds = p * (dp - dl_ref[...][:, :1]) * scale
            acc_s[...] += jnp.dot(ds.astype(MM_DTYPE), k_ref[...], preferred_element_type=F32)

        @pl.when(ki == qi)
        def _():
            dq_ref[...] = acc_s[...]

    blk = lambda h, qi, ki: (qi, h)
    kblk = lambda h, qi, ki: (jnp.minimum(ki, qi), h)
    qs = pl.BlockSpec((tq, HEAD_LANES), blk)
    ks = pl.BlockSpec((tq, HEAD_LANES), kblk)
    return pl.pallas_call(
        body, name="mla_attn_bwd_dq", grid=(N_HEADS, nq, nq), in_specs=[qs, ks, ks, qs, qs, qs], out_specs=qs,
        out_shape=jax.ShapeDtypeStruct((t, MLA_PAD), F32), scratch_shapes=[pltpu.VMEM((tq, HEAD_LANES), F32)],
        compiler_params=_params(("parallel", "parallel", "arbitrary")))(q, k, v, do, lse, delta)


def _attn_bwd_dkv(q, k, v, do, lse, delta, tq):
    t = q.shape[0]
    nq = t // tq
    scale = _attn_scale()

    def body(q_ref, k_ref, v_ref, do_ref, lse_ref, dl_ref, dk_ref, dv_ref, dk_s, dv_s):
        ki, qi = pl.program_id(1), pl.program_id(2)

        @pl.when(qi == 0)
        def _():
            dk_s[...] = jnp.zeros(dk_s.shape, F32)
            dv_s[...] = jnp.zeros(dv_s.shape, F32)

        @pl.when(qi >= ki)
        def _():
            p = _attn_probs(q_ref[...], k_ref[...], lse_ref[...], qi, ki, tq, scale)
            dv_s[...] += lax.dot_general(p.astype(MM_DTYPE), do_ref[...], (((0,), (0,)), ((), ())),
                                         preferred_element_type=F32)
            dp = lax.dot_general(do_ref[...], v_ref[...], (((1,), (1,)), ((), ())), preferred_element_type=F32)
            ds = p * (dp - dl_ref[...][:, :1]) * scale
            dk_s[...] += lax.dot_general(ds.astype(MM_DTYPE), q_ref[...], (((0,), (0,)), ((), ())),
                                         preferred_element_type=F32)

        @pl.when(qi == nq - 1)
        def _():
            dk_ref[...] = dk_s[...]
            dv_ref[...] = dv_s[...]

    kblk = lambda h, ki, qi: (ki, h)
    qblk = lambda h, ki, qi: (jnp.maximum(qi, ki), h)
    qs = pl.BlockSpec((tq, HEAD_LANES), qblk)
    ks = pl.BlockSpec((tq, HEAD_LANES), kblk)
    return pl.pallas_call(
        body, name="mla_attn_bwd_dkv", grid=(N_HEADS, nq, nq), in_specs=[qs, ks, ks, qs, qs, qs],
        out_specs=[ks, ks],
        out_shape=[jax.ShapeDtypeStruct((t, MLA_PAD), F32), jax.ShapeDtypeStruct((t, MLA_PAD), F32)],
        scratch_shapes=[pltpu.VMEM((tq, HEAD_LANES), F32)] * 2,
        compiler_params=_params(("parallel", "parallel", "arbitrary")))(q, k, v, do, lse, delta)


def _dotf(a, b, dims=(((1,), (0,)), ((), ()))):
    return lax.dot_general(a, b, dims, preferred_element_type=F32, precision=HI)


NT = (((1,), (1,)), ((), ()))
TN = (((0,), (0,)), ((), ()))


def _tri_masks():
    row = lax.broadcasted_iota(jnp.int32, (CHUNK, CHUNK), 0)
    col = lax.broadcasted_iota(jnp.int32, (CHUNK, CHUNK), 1)
    return row, col


def _gdn_chunk_common(k, gb, bb, row, col):
    tril = row >= col
    ltri = tril.astype(F32)
    umat = (row <= col).astype(F32)
    gcc = _dotf(ltri, gb)
    gcr = _dotf(gb, umat, TN)
    dm = jnp.exp(jnp.where(tril, gcc - gcr, NEG))
    kb = k * bb
    lm = jnp.where(row > col, _dotf(kb, k, NT) * dm, 0.0)
    return gcc, dm, kb, lm, umat


def _unit_lower_inverse(lt3):
    shape = lt3.shape
    row = lax.broadcasted_iota(jnp.int32, shape, 1)
    col = lax.broadcasted_iota(jnp.int32, shape, 2)
    eye = (row == col).astype(F32)

    def step(i, t):
        c = jnp.sum(jnp.where(col == i, lt3, 0.0), axis=2, keepdims=True)
        new = jnp.where(col[:, :1] == i, 1.0, 0.0) - jnp.sum(c * t, axis=1, keepdims=True)
        return jnp.where(row == i, new, t)

    return lax.fori_loop(1, CHUNK, step, eye)


def _gdn_fwd(q, k, v, gb, bb):
    nh, t, dh = q.shape
    nchunk = t // CHUNK

    def body(q_ref, k_ref, v_ref, g_ref, b_ref, o_ref, sall_ref, tall_ref, s_s):
        @pl.when(pl.program_id(0) == 0)
        def _():
            s_s[...] = jnp.zeros(s_s.shape, F32)

        row, col = _tri_masks()
        pre = []
        for h in range(nh):
            pre.append(_gdn_chunk_common(k_ref[h], g_ref[h], b_ref[h], row, col))
        tinv = _unit_lower_inverse(jnp.stack([p[3].T for p in pre]))
        for h in range(nh):
            gcc, dm, kb, lm, _ = pre[h]
            qh, kh, vh, bbh = q_ref[h], k_ref[h], v_ref[h], b_ref[h]
            eg = jnp.exp(gcc)
            glr = gcc[CHUNK - 1:CHUNK, :]
            th = tinv[h]
            w = _dotf(th, kb * eg)
            u = _dotf(th, vh * bbh)
            at = jnp.where(row >= col, _dotf(qh, kh, NT) * dm, 0.0)
            sh = s_s[h]
            vn = u - _dotf(w, sh)
            o_ref[h] = _dotf(qh * eg, sh) + _dotf(at, vn)
            kd = kh * jnp.exp(glr - gcc)
            sall_ref[h, 0] = sh
            tall_ref[h] = th
            s_s[h] = sh * jnp.exp(glr) + _dotf(kd, vn, TN)

    blk = pl.BlockSpec((nh, CHUNK, dh), lambda n: (0, n, 0))
    return pl.pallas_call(
        body, name="gdn_fwd", grid=(nchunk,), in_specs=[blk] * 5,
        out_specs=[blk, pl.BlockSpec((nh, 1, dh, dh), lambda n: (0, n, 0, 0)), blk],
        out_shape=[jax.ShapeDtypeStruct((nh, t, dh), F32), jax.ShapeDtypeStruct((nh, nchunk, dh, dh), F32),
                   jax.ShapeDtypeStruct((nh, t, CHUNK), F32)],
        scratch_shapes=[pltpu.VMEM((nh, dh, dh), F32)],
        compiler_params=_params(("arbitrary",)))(q, k, v, gb, bb)


def _gdn_bwd(q, k, v, gb, bb, sall, tall, do):
    nh, t, dh = q.shape
    nchunk = t // CHUNK

    def body(q_ref, k_ref, v_ref, g_ref, b_ref, sall_ref, tall_ref, do_ref,
             dq_ref, dk_ref, dv_ref, dg_ref, db_ref, ds_s):
        @pl.when(pl.program_id(0) == 0)
        def _():
            ds_s[...] = jnp.zeros(ds_s.shape, F32)

        row, col = _tri_masks()
        tril, stril = row >= col, row > col
        ones = jnp.ones((CHUNK, CHUNK), F32)
        for h in range(nh):
            qh, kh, vh, gbh, bbh = q_ref[h], k_ref[h], v_ref[h], g_ref[h], b_ref[h]
            sh, th, doh, dsp = sall_ref[h, 0], tall_ref[h], do_ref[h], ds_s[h]
            gcc, dm, kb, lm, umat = _gdn_chunk_common(kh, gbh, bbh, row, col)
            eg = jnp.exp(gcc)
            glr = gcc[CHUNK - 1:CHUNK, :]
            glv = jnp.exp(glr)
            egl = jnp.exp(glr - gcc)
            rw, ru = kb * eg, vh * bbh
            w, u = _dotf(th, rw), _dotf(th, ru)
            at = jnp.where(tril, _dotf(qh, kh, NT) * dm, 0.0)
            qd, kd = qh * eg, kh * egl
            vn = u - _dotf(w, sh)
            ds_new = dsp * glv
            dgl = jnp.sum(dsp * sh)
            dkd = _dotf(vn, dsp, NT)
            dvn = _dotf(kd, dsp)
            dqd = _dotf(doh, sh, NT)
            ds_new = ds_new + _dotf(qd, doh, TN)
            dat = jnp.where(tril, _dotf(doh, vn, NT), 0.0)
            dvn = dvn + _dotf(at, doh, TN)
            dw = -_dotf(dvn, sh, NT)
            ds_new = ds_new - _dotf(w, dvn, TN)
            ds_s[h] = ds_new
            dpa = dat * dm
            dq = _dotf(dpa, kh) + dqd * eg
            dk = _dotf(dpa, qh, TN)
            dgam = jnp.sum(dqd * qd, axis=1, keepdims=True)
            t6 = jnp.sum(dkd * kd, axis=1, keepdims=True)
            dk = dk + dkd * egl
            dgam = dgam - t6
            dgam_last = jnp.sum(t6) + dgl * glv
            drw = _dotf(th, dw, TN)
            dru = _dotf(th, dvn, TN)
            dl = -jnp.where(stril, _dotf(drw, w, NT) + _dotf(dru, u, NT), 0.0)
            dkb = drw * eg
            dgam = dgam + jnp.sum(drw * rw, axis=1, keepdims=True)
            dv_ref[h] = dru * bbh
            dbeta = jnp.sum(dru * vh, axis=1, keepdims=True)
            dp2 = dl * dm
            dkb = dkb + _dotf(dp2, kh)
            dk = dk + _dotf(dp2, kb, TN)
            dk = dk + dkb * bbh
            dbeta = dbeta + jnp.sum(dkb * kh, axis=1, keepdims=True)
            e = dat * at + dl * lm
            dgam_b = dgam + jnp.sum(e, axis=1, keepdims=True) - _dotf(e, ones, TN)
            dgam_b = dgam_b + jnp.where(row == CHUNK - 1, dgam_last, 0.0)
            dq_ref[h] = dq
            dk_ref[h] = dk
            dg_ref[h] = _dotf(umat, dgam_b)
            db_ref[h] = dbeta + jnp.zeros((CHUNK, dh), F32)

    rev = lambda n: (0, nchunk - 1 - n, 0)
    blk = pl.BlockSpec((nh, CHUNK, dh), rev)
    sblk = pl.BlockSpec((nh, 1, dh, dh), lambda n: (0, nchunk - 1 - n, 0, 0))
    out = jax.ShapeDtypeStruct((nh, t, dh), F32)
    return pl.pallas_call(
        body, name="gdn_bwd", grid=(nchunk,), in_specs=[blk] * 5 + [sblk, blk, blk], out_specs=[blk] * 5,
        out_shape=[out] * 5, scratch_shapes=[pltpu.VMEM((nh, dh, dh), F32)],
        compiler_params=_params(("arbitrary",)))(q, k, v, gb, bb, sall, tall, do)


def _group_ones():
    r = lax.broadcasted_iota(jnp.int32, (GDN_W, GDN_W), 0) // GDN_DH
    c = lax.broadcasted_iota(jnp.int32, (GDN_W, GDN_W), 1) // GDN_DH
    return (r == c).astype(F32)


def _conv_taps(x, xprev, w, has_prev):
    row = lax.broadcasted_iota(jnp.int32, x.shape, 0)
    out = x * w[GDN_CONV - 1:GDN_CONV, :]
    for s in range(1, GDN_CONV):
        sh = jnp.where(row >= s, _roll(x, s, 0), _roll(xprev, s, 0) * has_prev)
        out = out + sh * w[GDN_CONV - 1 - s:GDN_CONV - s, :]
    return out


def _to_heads(x):
    t = x.shape[0]
    return x.reshape(t, N_HEADS, GDN_DH).transpose(1, 0, 2)


def _from_heads(x):
    nh, t, dh = x.shape
    return x.transpose(1, 0, 2).reshape(t, nh * dh)


def _lane_scalar_heads(x8):
    t = x8.shape[0]
    return jnp.broadcast_to(x8.T[:, :, None], (N_HEADS, t, GDN_DH))


def _mixer_fwd(x, positions, w, tb):
    t, d = x.shape
    tables = _rope_tables(positions)

    def pre(xb, g):
        return (xb * _rms_stats(xb) * g,)

    (hn,) = _rowwise("mix_pre", pre, [x], [w["mix_pre_g"]], [(d, BF16)], [], tb)
    proj = _mm("mix_in", hn, w["w_in_pad"], "nn", F32)

    def mla_pre(p0, gq, gkv):
        cq, ckv = p0[:, :MLA_Q_RANK], p0[:, MLA_Q_RANK:MLA_Q_RANK + MLA_KV_RANK]
        return cq * _rms_stats(cq) * gq, ckv * _rms_stats(ckv) * gkv

    nq, nkv = _rowwise("mla_pre", mla_pre, [(proj, 512, 0, 0)], [w["mla_q_norm_g"], w["mla_kv_norm_g"]],
                       [(MLA_Q_RANK, BF16), (MLA_KV_RANK, BF16)], [], tb)
    qraw = _mm("mla_uq", nq, w["w_uq_pad"], "nn", F32)
    kv = _mm("mla_ukv", nkv, w["w_kv_pad"], "nn", F32)

    def rope_f(qr, kn, vv, kpe, c, s1, s2):
        qo = _heads_apply(qr, lambda xh: _rope(xh, c, s1, s2))
        kp = _rope(kpe, c, s1, s2)
        return qo, kn + jnp.tile(kp, (1, N_HEADS)), vv

    q, k, v = _rowwise("mla_rope", rope_f,
                       [qraw, (kv, MLA_PAD, 0, 0), (kv, MLA_PAD, 1, 0), (proj, HEAD_LANES, PIN_KPE // HEAD_LANES, 0),
                        tables[0], tables[1], tables[2]], [],
                       [(MLA_PAD, BF16)] * 3, [], tb // 2)
    tq = min(512, t)
    o, lse = _attn_fwd(q, k, v, tq)

    def mla_post(ob, g):
        return (ob * _rms_stats(ob, N_HEADS * MLA_V) * g,)

    (mo,) = _rowwise("mla_post", mla_post, [o], [w["mla_out_g_pad"]], [(MLA_PAD, BF16)], [], tb)

    gones = _group_ones()
    qkv_cols = [(proj, GDN_W, PIN_QKV // GDN_W + j, 0) for j in range(3)]
    steps = t // tb

    def gdn_pre(xq, xk, xv, pq, pk, pv, cw, go, has_prev):
        outs = []
        for j, (xc, xp) in enumerate(((xq, pq), (xk, pk), (xv, pv))):
            c = _conv_taps(xc, xp, cw[:, j * GDN_W:(j + 1) * GDN_W], has_prev)
            a = c * _sigmoid(c)
            if j < 2:
                rn = lax.rsqrt(_dotf(a * a, go) + EPS)
                a = a * rn
                if j == 0:
                    a = a * (GDN_DH ** -0.5)
            outs.append(a)
        return tuple(outs)

    qn, kn, vv = _gdn_pre_call("gdn_pre", gdn_pre, proj, w["conv_w"], gones, tb, steps)
    ab = _rowwise("gdn_gate_f", lambda abb, al, dtb: _gb_fwd(abb, al, dtb),
                  [(proj, HEAD_LANES, PIN_AB // HEAD_LANES, 0)], [w["a_log_pad"], w["dt_bias_pad"]],
                  [(HEAD_LANES, F32), (HEAD_LANES, F32)], [], tb)
    g8, b8 = ab[0][:, :N_HEADS], ab[1][:, N_HEADS:2 * N_HEADS]
    qh, kh, vh = _to_heads(qn), _to_heads(kn), _to_heads(vv)
    gbh, bbh = _lane_scalar_heads(g8), _lane_scalar_heads(b8)
    oh, sall, tall = _gdn_fwd(qh, kh, vh, gbh, bbh)
    gate_h = _to_heads(proj[:, PIN_GATE:PIN_GATE + GDN_W]).reshape(N_HEADS * t, GDN_DH)
    oh2 = oh.reshape(N_HEADS * t, GDN_DH)

    def gdn_post(ob, gt, g):
        return (ob * _rms_stats(ob) * g * (gt * _sigmoid(gt)),)

    (go_h,) = _rowwise("gdn_post", gdn_post, [oh2, gate_h], [w["gdn_norm_g"]], [(GDN_DH, BF16)], [], 8 * tb)
    gdn_o = _from_heads(go_h.reshape(N_HEADS, t, GDN_DH))

    cat = jnp.concatenate([mo, gdn_o], axis=1)
    mixed = _mm("mix_out", cat, w["w_out_pad"], "nn", F32)

    def post(xb, hb, g):
        return (xb + hb * _rms_stats(hb) * g,)

    (y,) = _rowwise("mix_post", post, [x, mixed], [w["mix_post_g"]], [(d, F32)], [], tb)
    saved = dict(x=x, hn=hn, proj=proj, nq=nq, nkv=nkv, q=q, k=k, v=v, o=o, lse=lse, qh=qh, kh=kh, vh=vh,
                 gbh=gbh, bbh=bbh, oh2=oh2, sall=sall, tall=tall, gate_h=gate_h, cat=cat, mixed=mixed,
                 tables=tables, g128=ab[0], b128=ab[1])
    return y, saved


def _gdn_pre_call(name, fn, proj, conv_w, gones, tb, steps):
    t = proj.shape[0]
    base = PIN_QKV // GDN_W
    cur = [pl.BlockSpec((tb, GDN_W), lambda i, j=j: (i, base + j)) for j in range(3)]
    prev = [pl.BlockSpec((tb, GDN_W), lambda i, j=j: (jnp.maximum(i - 1, 0), base + j)) for j in range(3)]

    def body(xq, xk, xv, pq, pk, pv, cw, go, oq, ok, ov):
        has_prev = jnp.where(pl.program_id(0) == 0, 0.0, 1.0)
        outs = fn(xq[...], xk[...], xv[...], pq[...], pk[...], pv[...], cw[...], go[...], has_prev)
        for r, val in zip((oq, ok, ov), outs):
            r[...] = val

    return pl.pallas_call(
        body, name=name, grid=(steps,),
        in_specs=cur + prev + [pl.BlockSpec(conv_w.shape, lambda i: (0, 0)), pl.BlockSpec(gones.shape, lambda i: (0, 0))],
        out_specs=[pl.BlockSpec((tb, GDN_W), lambda i: (i, 0))] * 3,
        out_shape=[jax.ShapeDtypeStruct((t, GDN_W), F32)] * 3,
        compiler_params=_params(("arbitrary",)))(proj, proj, proj, proj, proj, proj, conv_w, gones)


def _softplus(x):
    return jnp.maximum(x, 0.0) + jnp.log1p(jnp.exp(-jnp.abs(x)))


def _gb_fwd(ab, a_log, dt_bias):
    g = -jnp.exp(a_log) * _softplus(ab + dt_bias)
    return g, _sigmoid(ab)


def _rope_tables(positions):
    half = MLA_ROPE // 2
    freqs = ROPE_THETA ** (-jnp.arange(half, dtype=F32) / half)
    ang = positions.reshape(-1).astype(F32)[:, None] * freqs
    cos, sin = jnp.cos(ang), jnp.sin(ang)
    t = ang.shape[0]
    one = jnp.ones((t, MLA_NOPE), F32)
    z16, z32, z64 = jnp.zeros((t, half), F32), jnp.zeros((t, MLA_ROPE), F32), jnp.zeros((t, MLA_NOPE), F32)
    c = jnp.concatenate([one, cos, cos, jnp.ones((t, MLA_ROPE), F32)], axis=1)
    s1 = jnp.concatenate([z64, -sin, z16, z32], axis=1)
    s2 = jnp.concatenate([z64, z16, sin, z32], axis=1)
    return c, s1, s2


def _mixer_bwd(dy, sv, w, tb):
    x, proj = sv["x"], sv["proj"]
    t, d = x.shape
    c, s1, s2 = sv["tables"]
    grads = {}

    def post_b(hb, dyb, g):
        return _rms_bwd(hb, _rms_stats(hb), g, dyb)

    dmixed, grads["mix_post_g"] = _rowwise("mix_post_b", post_b, [sv["mixed"], dy], [w["mix_post_g"]],
                                           [(d, BF16)], [(1, d)], tb)
    dcat = _mm("mix_out_bx", dmixed, w["w_out_pad"], "nt", F32)
    grads["w_out_pad"] = _mm("mix_out_bw", sv["cat"], dmixed, "tn", F32)

    def mla_post_b(ob, dmo, g):
        do, dg = _rms_bwd(ob, _rms_stats(ob, N_HEADS * MLA_V), g, dmo, N_HEADS * MLA_V)
        prod = do * ob
        delta = _heads_apply(prod, lambda ph: jnp.sum(ph, axis=1, keepdims=True) + jnp.zeros_like(ph))
        return do, delta, dg

    do, delta, grads["mla_out_g_pad"] = _rowwise(
        "mla_post_b", mla_post_b, [sv["o"], (dcat, MLA_PAD, 0, 0)], [w["mla_out_g_pad"]],
        [(MLA_PAD, BF16), (MLA_PAD, F32)], [(1, MLA_PAD)], tb // 2)
    tq = min(512, t)
    dq = _attn_bwd_dq(sv["q"], sv["k"], sv["v"], do, sv["lse"], delta, tq)
    dk, dv = _attn_bwd_dkv(sv["q"], sv["k"], sv["v"], do, sv["lse"], delta, tq)

    def rope_b(dqb, dkb, dvb, cc, a1, a2):
        dqr = _heads_apply(dqb, lambda xh: _rope(xh, cc, -a1, -a2))
        ksum = dkb[:, :HEAD_LANES]
        for h in range(1, N_HEADS):
            ksum = ksum + dkb[:, h * HEAD_LANES:(h + 1) * HEAD_LANES]
        lane = lax.broadcasted_iota(jnp.int32, ksum.shape, 1)
        keep = (lane >= MLA_NOPE) & (lane < MLA_NOPE + MLA_ROPE)
        dkpe = jnp.where(keep, _rope(ksum, cc, -a1, -a2), 0.0)
        return dqr, jnp.concatenate([dkb, dvb], axis=1), dkpe

    dqraw, dkv, dkpe = _rowwise("mla_rope_b", rope_b, [dq, dk, dv, c, s1, s2], [],
                                [(MLA_PAD, BF16), (2 * MLA_PAD, BF16), (HEAD_LANES, F32)], [], tb // 2)
    dnq = _mm("mla_uq_bx", dqraw, w["w_uq_pad"], "nt", F32)
    grads["w_uq_pad"] = _mm("mla_uq_bw", sv["nq"], dqraw, "tn", F32)
    dnkv = _mm("mla_ukv_bx", dkv, w["w_kv_pad"], "nt", F32)
    grads["w_kv_pad"] = _mm("mla_ukv_bw", sv["nkv"], dkv, "tn", F32)

    def mla_pre_b(p0, dnqb, dnkvb, dkpeb, gq, gkv):
        cq, ckv = p0[:, :MLA_Q_RANK], p0[:, MLA_Q_RANK:MLA_Q_RANK + MLA_KV_RANK]
        dcq, dgq = _rms_bwd(cq, _rms_stats(cq), gq, dnqb)
        dckv, dgkv = _rms_bwd(ckv, _rms_stats(ckv), gkv, dnkvb)
        return jnp.concatenate([dcq, dckv, dkpeb], axis=1), dgq, dgkv

    dp0, grads["mla_q_norm_g"], grads["mla_kv_norm_g"] = _rowwise(
        "mla_pre_b", mla_pre_b, [(proj, 512, 0, 0), dnq, dnkv, dkpe], [w["mla_q_norm_g"], w["mla_kv_norm_g"]],
        [(512, BF16)], [(1, MLA_Q_RANK), (1, MLA_KV_RANK)], tb)

    dgo_h = _to_heads(dcat[:, MLA_PAD:]).reshape(N_HEADS * t, GDN_DH)

    def gdn_post_b(ob, gt, dob, g):
        sg = _sigmoid(gt)
        sil = gt * sg
        r = _rms_stats(ob)
        dz = dob * sil
        do_, dg = _rms_bwd(ob, r, g, dz)
        dgate = dob * (ob * r * g) * (sg * (1.0 + gt * (1.0 - sg)))
        return do_, dgate, dg

    doh, dgate_h, grads["gdn_norm_g"] = _rowwise(
        "gdn_post_b", gdn_post_b, [sv["oh2"], sv["gate_h"], dgo_h], [w["gdn_norm_g"]],
        [(GDN_DH, F32), (GDN_DH, BF16)], [(1, GDN_DH)], 8 * tb)
    dqh, dkh, dvh, dgh, dbh = _gdn_bwd(sv["qh"], sv["kh"], sv["vh"], sv["gbh"], sv["bbh"], sv["sall"], sv["tall"],
                                       doh.reshape(N_HEADS, t, GDN_DH))
    dqn, dkn, dvv = _from_heads(dqh), _from_heads(dkh), _from_heads(dvh)
    dg8, db8 = dgh[:, :, 0].T, dbh[:, :, 0].T
    dgb128 = jnp.concatenate([dg8, db8, jnp.zeros((t, HEAD_LANES - 2 * N_HEADS), F32)], axis=1)
    gones = _group_ones()
    steps = t // tb

    def gdn_pre_b(xq, xk, xv, pq, pk, pv, dq_, dk_, dv_, cw, go, has_prev):
        outs = []
        for j, (xc, xp, dd) in enumerate(((xq, pq, dq_), (xk, pk, dk_), (xv, pv, dv_))):
            cc = _conv_taps(xc, xp, cw[:, j * GDN_W:(j + 1) * GDN_W], has_prev)
            sg = _sigmoid(cc)
            a = cc * sg
            if j < 2:
                rn = lax.rsqrt(_dotf(a * a, go) + EPS)
                if j == 0:
                    dd = dd * (GDN_DH ** -0.5)
                da = rn * dd - a * (rn * rn * rn) * _dotf(dd * a, go)
            else:
                da = dd
            outs.append(da * (sg * (1.0 + cc * (1.0 - sg))))
        return tuple(outs)

    dcq, dck, dcv = _gdn_pre_b_call("gdn_pre_b", gdn_pre_b, proj, (dqn, dkn, dvv), w["conv_w"], gones, tb, steps)
    dxq, dxk, dxv, dcw = _conv_bwd_call("gdn_conv_b", proj, (dcq, dck, dcv), w["conv_w"], tb, steps)
    grads["conv_w"] = dcw

    def gb_b(abb, g128, b128, dgb, al, dtb):
        xs = abb + dtb
        sg = _sigmoid(xs)
        ea = jnp.exp(al)
        lane = lax.broadcasted_iota(jnp.int32, abb.shape, 1)
        is_a = lane < N_HEADS
        is_b = (lane >= N_HEADS) & (lane < 2 * N_HEADS)
        dg_ = jnp.where(is_a, dgb, 0.0)
        dab = dg_ * (-ea * sg) + jnp.where(is_b, dgb * b128 * (1.0 - b128), 0.0)
        return dab, jnp.sum(dg_ * g128, axis=0, keepdims=True), jnp.sum(dg_ * (-ea * sg), axis=0, keepdims=True)

    dab, grads["a_log_pad"], grads["dt_bias_pad"] = _rowwise(
        "gdn_gate_b", gb_b, [(proj, HEAD_LANES, PIN_AB // HEAD_LANES, 0), sv["g128"], sv["b128"], dgb128],
        [w["a_log_pad"], w["dt_bias_pad"]], [(HEAD_LANES, BF16)], [(1, HEAD_LANES), (1, HEAD_LANES)], tb)
    dgate = _from_heads(dgate_h.reshape(N_HEADS, t, GDN_DH))

    dproj = jnp.concatenate([dp0, dxq, dxk, dxv, dab, dgate], axis=1)
    dhn = _mm("mix_in_bx", dproj, w["w_in_pad"], "nt", F32)
    grads["w_in_pad"] = _mm("mix_in_bw", sv["hn"], dproj, "tn", F32)

    def pre_b(xb, dnb, dyb, g):
        dx, dg = _rms_bwd(xb, _rms_stats(xb), g, dnb)
        return dyb + dx, dg

    dx, grads["mix_pre_g"] = _rowwise("mix_pre_b", pre_b, [x, dhn, dy], [w["mix_pre_g"]], [(d, F32)], [(1, d)], tb)
    return dx, grads


def _gdn_pre_b_call(name, fn, proj, dd, conv_w, gones, tb, steps):
    t = proj.shape[0]
    base = PIN_QKV // GDN_W
    cur = [pl.BlockSpec((tb, GDN_W), lambda i, j=j: (i, base + j)) for j in range(3)]
    prev = [pl.BlockSpec((tb, GDN_W), lambda i, j=j: (jnp.maximum(i - 1, 0), base + j)) for j in range(3)]
    dsp = [pl.BlockSpec((tb, GDN_W), lambda i: (i, 0))] * 3

    def body(xq, xk, xv, pq, pk, pv, d0, d1, d2, cw, go, oq, ok, ov):
        has_prev = jnp.where(pl.program_id(0) == 0, 0.0, 1.0)
        outs = fn(xq[...], xk[...], xv[...], pq[...], pk[...], pv[...], d0[...], d1[...], d2[...], cw[...], go[...],
                  has_prev)
        for r, val in zip((oq, ok, ov), outs):
            r[...] = val

    return pl.pallas_call(
        body, name=name, grid=(steps,),
        in_specs=cur + prev + dsp + [pl.BlockSpec(conv_w.shape, lambda i: (0, 0)),
                                     pl.BlockSpec(gones.shape, lambda i: (0, 0))],
        out_specs=[pl.BlockSpec((tb, GDN_W), lambda i: (i, 0))] * 3,
        out_shape=[jax.ShapeDtypeStruct((t, GDN_W), F32)] * 3,
        compiler_params=_params(("arbitrary",)))(proj, proj, proj, proj, proj, proj, *dd, conv_w, gones)


def _conv_bwd_call(name, proj, dc, conv_w, tb, steps):
    t = proj.shape[0]
    base = PIN_QKV // GDN_W
    cur = [pl.BlockSpec((tb, GDN_W), lambda i, j=j: (i, base + j)) for j in range(3)]
    prev = [pl.BlockSpec((tb, GDN_W), lambda i, j=j: (jnp.maximum(i - 1, 0), base + j)) for j in range(3)]
    dcur = [pl.BlockSpec((tb, GDN_W), lambda i: (i, 0))] * 3
    dnext = [pl.BlockSpec((tb, GDN_W), lambda i: (jnp.minimum(i + 1, steps - 1), 0))] * 3

    def body(xq, xk, xv, pq, pk, pv, d0, d1, d2, n0, n1, n2, cw, oq, ok, ov, dw_ref):
        i = pl.program_id(0)
        has_prev = jnp.where(i == 0, 0.0, 1.0)
        has_next = jnp.where(i == steps - 1, 0.0, 1.0)

        @pl.when(i == 0)
        def _():
            dw_ref[...] = jnp.zeros(dw_ref.shape, F32)

        wv = cw[...]
        dws = []
        for j, (xr, pr, dr, nr, orf) in enumerate(((xq, pq, d0, n0, oq), (xk, pk, d1, n1, ok), (xv, pv, d2, n2, ov))):
            x, xp, dcv, dnx = xr[...], pr[...], dr[...], nr[...]
            wj = wv[:, j * GDN_W:(j + 1) * GDN_W]
            row = lax.broadcasted_iota(jnp.int32, x.shape, 0)
            dx = dcv * wj[GDN_CONV - 1:GDN_CONV, :]
            rows_w = [jnp.sum(dcv * x, axis=0, keepdims=True)]
            for s in range(1, GDN_CONV):
                up = jnp.where(row < tb - s, _roll(dcv, tb - s, 0), _roll(dnx, tb - s, 0) * has_next)
                dx = dx + up * wj[GDN_CONV - 1 - s:GDN_CONV - s, :]
                sh = jnp.where(row >= s, _roll(x, s, 0), _roll(xp, s, 0) * has_prev)
                rows_w.append(jnp.sum(dcv * sh, axis=0, keepdims=True))
            orf[...] = dx.astype(orf.dtype)
            dws.append(jnp.concatenate(rows_w[::-1], axis=0))
        dw_ref[...] += jnp.concatenate(dws, axis=1)

    outs = pl.pallas_call(
        body, name=name, grid=(steps,),
        in_specs=cur + prev + dcur + dnext + [pl.BlockSpec(conv_w.shape, lambda i: (0, 0))],
        out_specs=[pl.BlockSpec((tb, GDN_W), lambda i: (i, 0))] * 3 + [pl.BlockSpec(conv_w.shape, lambda i: (0, 0))],
        out_shape=[jax.ShapeDtypeStruct((t, GDN_W), BF16)] * 3 + [jax.ShapeDtypeStruct(conv_w.shape, F32)],
        compiler_params=_params(("arbitrary",)))(proj, proj, proj, proj, proj, proj, *dc, *dc, conv_w)
    return outs


def _pad_heads_cols(wm, per_head):
    r = wm.shape[0]
    return jnp.pad(wm.reshape(r, N_HEADS, per_head), ((0, 0), (0, 0), (0, HEAD_LANES - per_head))).reshape(r, MLA_PAD)


def _unpad_heads_cols(wm, per_head):
    r = wm.shape[0]
    return wm.reshape(r, N_HEADS, HEAD_LANES)[:, :, :per_head].reshape(r, N_HEADS * per_head)


def _win_to_pad(wi):
    r = wi.shape[0]
    z = lambda n: jnp.zeros((r, n), wi.dtype)
    o = MLA_Q_RANK + MLA_KV_RANK
    kpe = wi[:, o:o + MLA_ROPE]
    o2 = o + MLA_ROPE
    qkv = wi[:, o2:o2 + 3 * GDN_W]
    o3 = o2 + 3 * GDN_W
    ab = wi[:, o3:o3 + 2 * N_HEADS]
    gate = wi[:, o3 + 2 * N_HEADS:]
    return jnp.concatenate([wi[:, :o], z(MLA_NOPE), kpe, z(HEAD_LANES - MLA_NOPE - MLA_ROPE), qkv, ab,
                            z(HEAD_LANES - 2 * N_HEADS), gate], axis=1)


def _win_from_pad(wp):
    return jnp.concatenate([wp[:, :PIN_KPE], wp[:, PIN_KPE + MLA_NOPE:PIN_KPE + MLA_NOPE + MLA_ROPE],
                            wp[:, PIN_QKV:PIN_AB], wp[:, PIN_AB:PIN_AB + 2 * N_HEADS], wp[:, PIN_GATE:]], axis=1)


def _wkv_to_pad(wkv):
    r = wkv.shape[0]
    w3 = wkv.reshape(r, N_HEADS, MLA_NOPE + MLA_V)
    kpart = jnp.pad(w3[:, :, :MLA_NOPE], ((0, 0), (0, 0), (0, HEAD_LANES - MLA_NOPE))).reshape(r, MLA_PAD)
    vpart = jnp.pad(w3[:, :, MLA_NOPE:], ((0, 0), (0, 0), (0, HEAD_LANES - MLA_V))).reshape(r, MLA_PAD)
    return jnp.concatenate([kpart, vpart], axis=1)


def _wkv_from_pad(wp):
    r = wp.shape[0]
    kpart = wp[:, :MLA_PAD].reshape(r, N_HEADS, HEAD_LANES)[:, :, :MLA_NOPE]
    vpart = wp[:, MLA_PAD:].reshape(r, N_HEADS, HEAD_LANES)[:, :, :MLA_V]
    return jnp.concatenate([kpart, vpart], axis=2).reshape(r, N_HEADS * (MLA_NOPE + MLA_V))


def _wout_to_pad(wo):
    n = wo.shape[1]
    mla = jnp.pad(wo[:N_HEADS * MLA_V].reshape(N_HEADS, MLA_V, n), ((0, 0), (0, HEAD_LANES - MLA_V), (0, 0)))
    return jnp.concatenate([mla.reshape(MLA_PAD, n), wo[N_HEADS * MLA_V:]], axis=0)


def _wout_from_pad(wp):
    n = wp.shape[1]
    mla = wp[:MLA_PAD].reshape(N_HEADS, HEAD_LANES, n)[:, :MLA_V].reshape(N_HEADS * MLA_V, n)
    return jnp.concatenate([mla, wp[MLA_PAD:]], axis=0)


def _pad_lanes(v, n):
    return jnp.pad(v, ((0, 0), (0, n - v.shape[1])))


def _compute_weights(full):
    w = {}
    for tag in ("ffn1", "ffn2"):
        w[tag + "_wgu"] = jnp.concatenate([full[tag + "_w_gate"], full[tag + "_w_up"]], axis=1).astype(MM_DTYPE)
        w[tag + "_wd"] = full[tag + "_w_down"].astype(MM_DTYPE)
    w["w_in_pad"] = _win_to_pad(full["w_in"]).astype(MM_DTYPE)
    w["w_uq_pad"] = _pad_heads_cols(full["mla_w_uq"], MLA_NOPE + MLA_ROPE).astype(MM_DTYPE)
    w["w_kv_pad"] = _wkv_to_pad(full["mla_w_ukv"]).astype(MM_DTYPE)
    w["w_out_pad"] = _wout_to_pad(full["w_out"]).astype(MM_DTYPE)
    w["conv_w"] = full["gdn_conv_w"].astype(F32)
    for n in ("ffn1_pre_g", "ffn1_post_g", "mix_pre_g", "mla_q_norm_g", "mla_kv_norm_g", "gdn_norm_g", "mix_post_g",
              "ffn2_pre_g", "ffn2_post_g"):
        w[n] = full[n]
    w["mla_out_g_pad"] = _pad_heads_cols(full["mla_out_g"], MLA_V)
    w["a_log_pad"] = _pad_lanes(full["gdn_a_log"], HEAD_LANES)
    w["dt_bias_pad"] = _pad_lanes(full["gdn_dt_bias"], HEAD_LANES)
    return w


def _local_step(x, positions, loss_target, full):
    t, d = x.shape
    tb = min(512, t)
    w = _compute_weights(full)
    x1, sv1 = _ffn_fwd("ffn1", x, w["ffn1_pre_g"], w["ffn1_wgu"], w["ffn1_wd"], w["ffn1_post_g"], tb)
    x2, svm = _mixer_fwd(x1, positions, w, tb)
    x3, sv2 = _ffn_fwd("ffn2", x2, w["ffn2_pre_g"], w["ffn2_wgu"], w["ffn2_wd"], w["ffn2_post_g"], tb)

    def loss_f(yb, tg):
        e = yb - tg
        return e * (1.0 / d), jnp.sum(e * e, axis=0, keepdims=True)

    dy, lsum = _rowwise("loss", loss_f, [x3, loss_target], [], [(d, F32)], [(1, d)], tb)
    g = {}
    dx2, g["ffn2_pre_g"], g["ffn2_w_gate"], g["ffn2_w_up"], g["ffn2_w_down"], g["ffn2_post_g"] = _ffn_bwd(
        "ffn2", dy, sv2, w["ffn2_pre_g"], w["ffn2_wgu"], w["ffn2_wd"], w["ffn2_post_g"], tb)
    dx1, gm = _mixer_bwd(dx2, svm, w, tb)
    dx0, g["ffn1_pre_g"], g["ffn1_w_gate"], g["ffn1_w_up"], g["ffn1_w_down"], g["ffn1_post_g"] = _ffn_bwd(
        "ffn1", dx1, sv1, w["ffn1_pre_g"], w["ffn1_wgu"], w["ffn1_wd"], w["ffn1_post_g"], tb)
    g["mix_pre_g"], g["mix_post_g"] = gm["mix_pre_g"], gm["mix_post_g"]
    g["mla_q_norm_g"], g["mla_kv_norm_g"] = gm["mla_q_norm_g"], gm["mla_kv_norm_g"]
    g["gdn_norm_g"] = gm["gdn_norm_g"]
    g["w_in"] = _win_from_pad(gm["w_in_pad"])
    g["mla_w_uq"] = _unpad_heads_cols(gm["w_uq_pad"], MLA_NOPE + MLA_ROPE)
    g["mla_w_ukv"] = _wkv_from_pad(gm["w_kv_pad"])
    g["mla_out_g"] = _unpad_heads_cols(gm["mla_out_g_pad"], MLA_V)
    g["gdn_conv_w"] = gm["conv_w"]
    g["gdn_a_log"] = gm["a_log_pad"][:, :N_HEADS]
    g["gdn_dt_bias"] = gm["dt_bias_pad"][:, :N_HEADS]
    g["w_out"] = _wout_from_pad(gm["w_out_pad"])
    return lsum, dx0, g


HBM_SPEC = pl.BlockSpec(memory_space=pltpu.HBM)


def _place():
    return lax.axis_index("x"), lax.axis_index("y"), lax.axis_index("c")


def _gather_shards(wp):
    r, l = wp.shape

    def body(w_ref, out_ref, send_sems, recv_sems, local_sem):
        x, y, c = _place()
        chips = [(1 - x, y), (x, 1 - y), (1 - x, 1 - y)]
        mine = pltpu.make_async_copy(w_ref, out_ref.at[2 * x + y], local_sem)
        mine.start()
        sends = [pltpu.make_async_remote_copy(src_ref=w_ref, dst_ref=out_ref.at[2 * x + y], send_sem=send_sems.at[j],
                                              recv_sem=recv_sems.at[j], device_id=(px, py, c), device_id_type=MESH)
                 for j, (px, py) in enumerate(chips)]
        for cp in sends:
            cp.start()
        for j, (px, py) in enumerate(chips):
            pltpu.make_async_remote_copy(src_ref=w_ref, dst_ref=out_ref.at[2 * px + py], send_sem=send_sems.at[j],
                                         recv_sem=recv_sems.at[j], device_id=(px, py, c),
                                         device_id_type=MESH).wait_recv()
        for cp in sends:
            cp.wait_send()
        mine.wait()

    return pl.pallas_call(
        body, name="gather_weight_shards", in_specs=[HBM_SPEC], out_specs=HBM_SPEC,
        out_shape=jax.ShapeDtypeStruct((N_SHARD, r, l), wp.dtype),
        scratch_shapes=[pltpu.SemaphoreType.DMA((3,)), pltpu.SemaphoreType.DMA((3,)), pltpu.SemaphoreType.DMA])(wp)


def _swap_with_sibling(v):
    def body(v_ref, out_ref, send_sem, recv_sem):
        x, y, c = _place()
        cp = pltpu.make_async_remote_copy(src_ref=v_ref, dst_ref=out_ref, send_sem=send_sem, recv_sem=recv_sem,
                                          device_id=(x, y, 1 - c), device_id_type=MESH)
        cp.start()
        cp.wait()

    return pl.pallas_call(
        body, name="swap_grad_halves", in_specs=[HBM_SPEC], out_specs=HBM_SPEC,
        out_shape=jax.ShapeDtypeStruct(v.shape, v.dtype),
        scratch_shapes=[pltpu.SemaphoreType.DMA, pltpu.SemaphoreType.DMA])(v)


def _scatter_to_chips(p):
    def body(p_ref, out_ref, send_sems, recv_sems, local_sem):
        x, y, c = _place()
        me = 2 * x + y
        chips = [(1 - x, y), (x, 1 - y), (1 - x, 1 - y)]
        mine = pltpu.make_async_copy(p_ref.at[me], out_ref.at[me], local_sem)
        mine.start()
        sends = [pltpu.make_async_remote_copy(src_ref=p_ref.at[2 * px + py], dst_ref=out_ref.at[me],
                                              send_sem=send_sems.at[j], recv_sem=recv_sems.at[j],
                                              device_id=(px, py, c), device_id_type=MESH)
                 for j, (px, py) in enumerate(chips)]
        for cp in sends:
            cp.start()
        for j, (px, py) in enumerate(chips):
            pltpu.make_async_remote_copy(src_ref=p_ref.at[me], dst_ref=out_ref.at[2 * px + py],
                                         send_sem=send_sems.at[j], recv_sem=recv_sems.at[j], device_id=(px, py, c),
                                         device_id_type=MESH).wait_recv()
        for cp in sends:
            cp.wait_send()
        mine.wait()

    return pl.pallas_call(
        body, name="scatter_grad_quarters", in_specs=[HBM_SPEC], out_specs=HBM_SPEC,
        out_shape=jax.ShapeDtypeStruct(p.shape, p.dtype),
        scratch_shapes=[pltpu.SemaphoreType.DMA((3,)), pltpu.SemaphoreType.DMA((3,)), pltpu.SemaphoreType.DMA])(p)


def _share_halves(hs):
    def body(h_ref, out_ref, send_sem, recv_sem, local_sem):
        x, y, c = _place()
        mine = pltpu.make_async_copy(h_ref, out_ref.at[c], local_sem)
        mine.start()
        cp = pltpu.make_async_remote_copy(src_ref=h_ref, dst_ref=out_ref.at[c], send_sem=send_sem, recv_sem=recv_sem,
                                          device_id=(x, y, 1 - c), device_id_type=MESH)
        cp.start()
        pltpu.make_async_remote_copy(src_ref=h_ref, dst_ref=out_ref.at[1 - c], send_sem=send_sem, recv_sem=recv_sem,
                                     device_id=(x, y, 1 - c), device_id_type=MESH).wait_recv()
        cp.wait_send()
        mine.wait()

    return pl.pallas_call(
        body, name="share_grad_halves", in_specs=[HBM_SPEC], out_specs=HBM_SPEC,
        out_shape=jax.ShapeDtypeStruct((2,) + hs.shape, hs.dtype),
        scratch_shapes=[pltpu.SemaphoreType.DMA, pltpu.SemaphoreType.DMA, pltpu.SemaphoreType.DMA])(hs)


def _gather_small(sp):
    def body(s_ref, out_ref, send_sems, recv_sems, local_sem):
        x, y, c = _place()
        me = 4 * x + 2 * y + c
        peers = [(x ^ (m >> 2), y ^ ((m >> 1) & 1), c ^ (m & 1)) for m in range(1, 8)]
        mine = pltpu.make_async_copy(s_ref, out_ref.at[me], local_sem)
        mine.start()
        sends = [pltpu.make_async_remote_copy(src_ref=s_ref, dst_ref=out_ref.at[me], send_sem=send_sems.at[j],
                                              recv_sem=recv_sems.at[j], device_id=p, device_id_type=MESH)
                 for j, p in enumerate(peers)]
        for cp in sends:
            cp.start()
        for j, (px, py, pc) in enumerate(peers):
            pltpu.make_async_remote_copy(src_ref=s_ref, dst_ref=out_ref.at[4 * px + 2 * py + pc],
                                         send_sem=send_sems.at[j], recv_sem=recv_sems.at[j], device_id=(px, py, pc),
                                         device_id_type=MESH).wait_recv()
        for cp in sends:
            cp.wait_send()
        mine.wait()

    return pl.pallas_call(
        body, name="gather_small_grads", in_specs=[HBM_SPEC], out_specs=HBM_SPEC,
        out_shape=jax.ShapeDtypeStruct((8,) + sp.shape, sp.dtype),
        scratch_shapes=[pltpu.SemaphoreType.DMA((7,)), pltpu.SemaphoreType.DMA((7,)), pltpu.SemaphoreType.DMA])(sp)


def _shard_shapes(shards):
    return [(n, shards[n].shape) for n in BIG]


def _pack_rows(total):
    rows = -(-total // LANES)
    return -(-rows // 32) * 32


def _pack(arrs, dtype):
    flat = jnp.concatenate([a.reshape(-1).astype(dtype) for a in arrs])
    rows = _pack_rows(flat.shape[0])
    return jnp.pad(flat, (0, rows * LANES - flat.shape[0])).reshape(rows, LANES)


def _unpack(buf, shapes):
    flat = buf.reshape(-1)
    out, off = {}, 0
    for n, shp in shapes:
        size = shp[0] * shp[1]
        out[n] = flat[off:off + size].reshape(shp)
        off += size
    return out


def _adamw(name, wv, g, m, v, tb):
    c1 = 1.0 - ADAM_B1 ** ADAM_STEP
    c2 = 1.0 - ADAM_B2 ** ADAM_STEP

    def fn(wb, gb, mb, vb):
        m2 = ADAM_B1 * mb + (1.0 - ADAM_B1) * gb
        v2 = ADAM_B2 * vb + (1.0 - ADAM_B2) * (gb * gb)
        delta = -ADAM_LR * ((m2 / c1) / (jnp.sqrt(v2 / c2) + ADAM_EPS) + ADAM_WD * wb)
        return delta, m2, v2

    cols = wv.shape[1]
    return _rowwise(name, fn, [wv, g, m, v], [], [(cols, F32)] * 3, [], tb)


def _row_tile(rows, pref):
    if rows <= pref:
        return rows
    t = pref
    while t >= 8:
        if rows % t == 0 and t % 8 == 0:
            return t
        t -= 8
    return rows


def kernel(x, positions, ffn1_pre_g, ffn1_w_gate, ffn1_w_up, ffn1_w_down, ffn1_post_g, mix_pre_g, w_in, mla_q_norm_g, mla_w_uq, mla_kv_norm_g, mla_w_ukv, mla_out_g, gdn_conv_w, gdn_a_log, gdn_dt_bias, gdn_norm_g, w_out, mix_post_g, ffn2_pre_g, ffn2_w_gate, ffn2_w_up, ffn2_w_down, ffn2_post_g, loss_target, m_ffn1_pre_g, m_ffn1_w_gate, m_ffn1_w_up, m_ffn1_w_down, m_ffn1_post_g, m_mix_pre_g, m_w_in, m_mla_q_norm_g, m_mla_w_uq, m_mla_kv_norm_g, m_mla_w_ukv, m_mla_out_g, m_gdn_conv_w, m_gdn_a_log, m_gdn_dt_bias, m_gdn_norm_g, m_w_out, m_mix_post_g, m_ffn2_pre_g, m_ffn2_w_gate, m_ffn2_w_up, m_ffn2_w_down, m_ffn2_post_g, v_ffn1_pre_g, v_ffn1_w_gate, v_ffn1_w_up, v_ffn1_w_down, v_ffn1_post_g, v_mix_pre_g, v_w_in, v_mla_q_norm_g, v_mla_w_uq, v_mla_kv_norm_g, v_mla_w_ukv, v_mla_out_g, v_gdn_conv_w, v_gdn_a_log, v_gdn_dt_bias, v_gdn_norm_g, v_w_out, v_mix_post_g, v_ffn2_pre_g, v_ffn2_w_gate, v_ffn2_w_up, v_ffn2_w_down, v_ffn2_post_g):
    args = dict(locals())
    wsh = {n: args[n][0] for n in WEIGHTS}
    msh = {n: args["m_" + n][0] if args["m_" + n].ndim == 3 else args["m_" + n] for n in WEIGHTS}
    vsh = {n: args["v_" + n][0] if args["v_" + n].ndim == 3 else args["v_" + n] for n in WEIGHTS}
    for n in SMALL:
        wsh[n] = args[n]
    shapes = [(n, wsh[n].shape) for n in BIG]

    gathered = _gather_shards(_pack([wsh[n] for n in BIG], MM_DTYPE))
    full = {n: wsh[n] for n in SMALL}
    parts = [_unpack(gathered[q], shapes) for q in range(N_SHARD)]
    for n in BIG:
        full[n] = jnp.concatenate([parts[q][n] for q in range(N_SHARD)], axis=SHARD_AXIS[n])

    lsum, grad_x, g = _local_step(x[0], positions, loss_target[0], full)
    loss = lax.psum(0.5 * jnp.sum(lsum) / x.shape[-1], ("x", "y", "c"))

    c = lax.axis_index("c")
    quarters = []
    for q in range(N_SHARD):
        quarters.append(_pack([jnp.split(g[n], N_SHARD, axis=SHARD_AXIS[n])[q] for n in BIG], MM_DTYPE))
    rows = quarters[0].shape[0]
    half = rows // 2
    gp = jnp.stack(quarters).reshape(N_SHARD, 2, half, LANES)
    keep = lax.dynamic_index_in_dim(gp, c, axis=1, keepdims=False)
    give = lax.dynamic_index_in_dim(gp, 1 - c, axis=1, keepdims=False)
    got = _swap_with_sibling(give)
    tr = _row_tile(N_SHARD * half, 1024)
    (pair,) = _rowwise("add_pair", lambda a, b: (a.astype(F32) + b.astype(F32),),
                       [keep.reshape(N_SHARD * half, LANES), got.reshape(N_SHARD * half, LANES)], [],
                       [(LANES, MM_DTYPE)], [], tr)
    slabs = _scatter_to_chips(pair.reshape(N_SHARD, half, LANES)).reshape(N_SHARD * half, LANES)
    th = _row_tile(half, 1024)
    nb = half // th
    (hsum,) = _rowwise("add_chips",
                       lambda a, b, cc, dd: (((a.astype(F32) + b.astype(F32)) + cc.astype(F32)) + dd.astype(F32),),
                       [(slabs, LANES, 0, q * nb) for q in range(N_SHARD)], [], [(LANES, F32, half)], [], th)
    gfull = _share_halves(hsum).reshape(rows, LANES)
    gsh = _unpack(gfull, shapes)

    small_shapes = [(n, wsh[n].shape) for n in SMALL]
    pack_small = lambda d: jnp.concatenate(
        [_pad_lanes(d[n].astype(F32), LANES) for n in SMALL] + [jnp.zeros((SMALL_ROWS - len(SMALL), LANES), F32)], axis=0)
    slots = _gather_small(pack_small(g))

    c1 = 1.0 - ADAM_B1 ** ADAM_STEP
    c2 = 1.0 - ADAM_B2 ** ADAM_STEP

    def small_update(wb, mb, vb, s8):
        gs = s8[0:SMALL_ROWS]
        for d in range(1, 8):
            gs = gs + s8[d * SMALL_ROWS:(d + 1) * SMALL_ROWS]
        m2 = ADAM_B1 * mb + (1.0 - ADAM_B1) * gs
        v2 = ADAM_B2 * vb + (1.0 - ADAM_B2) * (gs * gs)
        delta = -ADAM_LR * ((m2 / c1) / (jnp.sqrt(v2 / c2) + ADAM_EPS) + ADAM_WD * wb)
        return gs, delta, m2, v2

    sg, sd, sm, sv_ = _rowwise("adamw_small", small_update,
                               [pack_small(wsh), pack_small(msh), pack_small(vsh)],
                               [slots.reshape(8 * SMALL_ROWS, LANES)], [(LANES, F32)] * 4, [], SMALL_ROWS)
    grads, deltas, new_m, new_v = {}, {}, {}, {}
    for i, (n, shp) in enumerate(small_shapes):
        grads[n], deltas[n] = sg[i:i + 1, :shp[1]], sd[i:i + 1, :shp[1]]
        new_m[n], new_v[n] = sm[i:i + 1, :shp[1]], sv_[i:i + 1, :shp[1]]
    for n in BIG:
        grads[n] = gsh[n]
        r = wsh[n].shape[0]
        deltas[n], new_m[n], new_v[n] = _adamw("adamw_" + n, wsh[n], gsh[n], msh[n], vsh[n], _row_tile(r, 256))

    def shaped(d, n):
        return d[n][None] if n in BIG else d[n]

    return (loss, grad_x[None], *[shaped(grads, n) for n in WEIGHTS], *[shaped(deltas, n) for n in WEIGHTS],
            *[shaped(new_m, n) for n in WEIGHTS], *[shaped(new_v, n) for n in WEIGHTS])
```

```python
import jax
import jax.numpy as jnp
from jax import lax
from jax.experimental import pallas as pl
from jax.experimental.pallas import tpu as pltpu

F32 = jnp.float32
BF16 = jnp.bfloat16
MM_DTYPE = BF16
HI = lax.Precision.HIGHEST
MESH = pl.DeviceIdType.MESH

D_MODEL = 1024
D_FF = 2816
N_HEADS = 8
MLA_Q_RANK = 256
MLA_KV_RANK = 128
MLA_NOPE = 64
MLA_ROPE = 32
MLA_V = 64
ROPE_THETA = 10000.0
GDN_DH = 64
GDN_W = N_HEADS * GDN_DH
GDN_CONV = 4
CHUNK = 64
HEAD_LANES = 128
MLA_PAD = N_HEADS * HEAD_LANES
EPS = 1e-6
N_SHARD = 4
LANES = 1024

PIN_CQ = 0
PIN_CKV = 256
PIN_KPE = 384
PIN_QKV = 512
PIN_AB = 2048
PIN_GATE = 2176
PIN_W = 2688

ADAM_LR = 0.001
ADAM_B1 = 0.9
ADAM_B2 = 0.999
ADAM_EPS = 1e-08
ADAM_WD = 0.01
ADAM_STEP = 10

VMEM_LIMIT_V7X = 56 * 1024 * 1024

BIG = ["ffn1_w_gate", "ffn1_w_up", "ffn1_w_down", "w_in", "mla_w_uq", "mla_w_ukv", "gdn_conv_w", "w_out",
       "ffn2_w_gate", "ffn2_w_up", "ffn2_w_down"]
SMALL = ["ffn1_pre_g", "ffn1_post_g", "mix_pre_g", "mla_q_norm_g", "mla_kv_norm_g", "mla_out_g", "gdn_a_log",
         "gdn_dt_bias", "gdn_norm_g", "mix_post_g", "ffn2_pre_g", "ffn2_post_g"]
WEIGHTS = ["ffn1_pre_g", "ffn1_w_gate", "ffn1_w_up", "ffn1_w_down", "ffn1_post_g", "mix_pre_g", "w_in",
           "mla_q_norm_g", "mla_w_uq", "mla_kv_norm_g", "mla_w_ukv", "mla_out_g", "gdn_conv_w", "gdn_a_log",
           "gdn_dt_bias", "gdn_norm_g", "w_out", "mix_post_g", "ffn2_pre_g", "ffn2_w_gate", "ffn2_w_up",
           "ffn2_w_down", "ffn2_post_g"]
SHARD_AXIS = {"ffn1_w_gate": 1, "ffn1_w_up": 1, "ffn1_w_down": 0, "w_in": 1, "mla_w_uq": 1, "mla_w_ukv": 1,
              "gdn_conv_w": 1, "w_out": 0, "ffn2_w_gate": 1, "ffn2_w_up": 1, "ffn2_w_down": 0}
SMALL_ROWS = 16


def _params(sem):
    return pltpu.CompilerParams(dimension_semantics=sem, vmem_limit_bytes=VMEM_LIMIT_V7X)


def _pick(dim, pref):
    if dim <= pref:
        return dim
    t = (pref // 128) * 128
    while t >= 128:
        if dim % t == 0:
            return t
        t -= 128
    return dim


def _rowwise(name, fn, row_ins, bc_ins, row_outs, acc_outs, tb):
    ents = []
    for e in row_ins:
        ents.append(e if isinstance(e, tuple) else (e, e.shape[1], 0, 0))
    over = [o[2] for o in row_outs if len(o) == 3]
    rows = over[0] if over else ents[0][0].shape[0]
    steps = rows // tb
    assert steps * tb == rows, (name, rows, tb)
    in_specs, args = [], []
    for a, w, j, r0 in ents:
        in_specs.append(pl.BlockSpec((tb, w), lambda i, j=j, r0=r0: (i + r0, j)))
        args.append(a)
    for b in bc_ins:
        in_specs.append(pl.BlockSpec(b.shape, lambda i: (0, 0)))
        args.append(b)
    out_shape = [jax.ShapeDtypeStruct((rows, o[0]), o[1]) for o in row_outs]
    out_shape += [jax.ShapeDtypeStruct((r, c), F32) for r, c in acc_outs]
    out_specs = [pl.BlockSpec((tb, o[0]), lambda i: (i, 0)) for o in row_outs]
    out_specs += [pl.BlockSpec((r, c), lambda i: (0, 0)) for r, c in acc_outs]
    n_in, n_ro, n_acc = len(args), len(row_outs), len(acc_outs)

    def body(*refs):
        vals = fn(*[r[...] for r in refs[:n_in]])
        if not isinstance(vals, (tuple, list)):
            vals = (vals,)
        for r, v in zip(refs[n_in:n_in + n_ro], vals[:n_ro]):
            r[...] = v.astype(r.dtype)
        if n_acc:
            acc_refs = refs[n_in + n_ro:]

            @pl.when(pl.program_id(0) == 0)
            def _():
                for r in acc_refs:
                    r[...] = jnp.zeros(r.shape, r.dtype)

            for r, v in zip(acc_refs, vals[n_ro:]):
                r[...] += v

    outs = pl.pallas_call(body, name=name, grid=(steps,), in_specs=in_specs, out_specs=out_specs,
                          out_shape=out_shape, compiler_params=_params(("arbitrary",)))(*args)
    return outs


def _mm(name, a, b, mode, out_dtype, tm=512, tn=512, tk=1024):
    if mode == "nn":
        (m, k), (k2, n) = a.shape, b.shape
    elif mode == "nt":
        (m, k), (n, k2) = a.shape, b.shape
    else:
        (k, m), (k2, n) = a.shape, b.shape
    assert k == k2, (name, a.shape, b.shape)
    tm, tn, tk = _pick(m, tm), _pick(n, tn), _pick(k, tk)
    nk = k // tk
    if mode == "nn":
        a_spec = pl.BlockSpec((tm, tk), lambda i, j, kk: (i, kk))
        b_spec = pl.BlockSpec((tk, tn), lambda i, j, kk: (kk, j))
        dims = (((1,), (0,)), ((), ()))
    elif mode == "nt":
        a_spec = pl.BlockSpec((tm, tk), lambda i, j, kk: (i, kk))
        b_spec = pl.BlockSpec((tn, tk), lambda i, j, kk: (j, kk))
        dims = (((1,), (1,)), ((), ()))
    else:
        a_spec = pl.BlockSpec((tk, tm), lambda i, j, kk: (kk, i))
        b_spec = pl.BlockSpec((tk, tn), lambda i, j, kk: (kk, j))
        dims = (((0,), (0,)), ((), ()))

    def body(a_ref, b_ref, o_ref, acc_ref):
        kk = pl.program_id(2)

        @pl.when(kk == 0)
        def _():
            acc_ref[...] = jnp.zeros(acc_ref.shape, F32)

        acc_ref[...] += lax.dot_general(a_ref[...].astype(MM_DTYPE), b_ref[...].astype(MM_DTYPE), dims,
                                        preferred_element_type=F32)

        @pl.when(kk == nk - 1)
        def _():
            o_ref[...] = acc_ref[...].astype(o_ref.dtype)

    return pl.pallas_call(
        body, name=name, grid=(m // tm, n // tn, nk), in_specs=[a_spec, b_spec],
        out_specs=pl.BlockSpec((tm, tn), lambda i, j, kk: (i, j)),
        out_shape=jax.ShapeDtypeStruct((m, n), out_dtype),
        scratch_shapes=[pltpu.VMEM((tm, tn), F32)],
        compiler_params=_params(("parallel", "parallel", "arbitrary")))(a, b)


def _rms_stats(x, n_real=None):
    n = x.shape[-1] if n_real is None else n_real
    return lax.rsqrt(jnp.sum(x * x, axis=-1, keepdims=True) / n + EPS)


def _rms_bwd(x, r, g, dz, n_real=None):
    n = x.shape[-1] if n_real is None else n_real
    xh = x * r
    dxh = dz * g
    dx = r * (dxh - xh * (jnp.sum(dxh * xh, axis=-1, keepdims=True) / n))
    return dx, jnp.sum(dz * xh, axis=0, keepdims=True)


def _sigmoid(x):
    return 1.0 / (1.0 + jnp.exp(-x))


def _roll(x, s, axis):
    return pltpu.roll(x, s, axis)


def _rope(x, c, s1, s2):
    return x * c + _roll(x, HEAD_LANES - MLA_ROPE // 2, 1) * s1 + _roll(x, MLA_ROPE // 2, 1) * s2


def _heads_apply(x, fn):
    return jnp.concatenate([fn(x[:, h * HEAD_LANES:(h + 1) * HEAD_LANES]) for h in range(N_HEADS)], axis=1)


def _ffn_fwd(tag, x, g_pre, wgu, wd, g_post, tb):
    d = x.shape[1]

    def pre(xb, g):
        return (xb * _rms_stats(xb) * g,)

    (n,) = _rowwise(tag + "_pre", pre, [x], [g_pre], [(d, BF16)], [], tb)
    au = _mm(tag + "_gu", n, wgu, "nn", BF16)

    def act(a, u):
        a = a.astype(F32)
        return ((a * _sigmoid(a)) * u.astype(F32),)

    (s,) = _rowwise(tag + "_act", act, [(au, D_FF, 0, 0), (au, D_FF, 1, 0)], [], [(D_FF, BF16)], [], tb // 2)
    h = _mm(tag + "_down", s, wd, "nn", F32)

    def post(xb, hb, g):
        return (xb + 0.5 * (hb * _rms_stats(hb) * g),)

    (y,) = _rowwise(tag + "_post", post, [x, h], [g_post], [(d, F32)], [], tb)
    return y, (x, n, au, s, h)


def _ffn_bwd(tag, dy, saved, g_pre, wgu, wd, g_post, tb):
    x, n, au, s, h = saved
    d = x.shape[1]

    def post_b(hb, dyb, g):
        dh, dg = _rms_bwd(hb, _rms_stats(hb), g, 0.5 * dyb)
        return dh, dg

    dh, dg_post = _rowwise(tag + "_post_b", post_b, [h, dy], [g_post], [(d, BF16)], [(1, d)], tb)
    ds = _mm(tag + "_down_bx", dh, wd, "nt", BF16)
    dwd = _mm(tag + "_down_bw", s, dh, "tn", F32)

    def act_b(a, u, dsb):
        a, u, dsb = a.astype(F32), u.astype(F32), dsb.astype(F32)
        sg = _sigmoid(a)
        da = dsb * u * (sg * (1.0 + a * (1.0 - sg)))
        du = dsb * (a * sg)
        return (jnp.concatenate([da, du], axis=1),)

    (dau,) = _rowwise(tag + "_act_b", act_b, [(au, D_FF, 0, 0), (au, D_FF, 1, 0), ds], [], [(2 * D_FF, BF16)],
                      [], tb // 2)
    dn = _mm(tag + "_gu_bx", dau, wgu, "nt", F32)
    dwgu = _mm(tag + "_gu_bw", n, dau, "tn", F32)

    def pre_b(xb, dnb, dyb, g):
        dx, dg = _rms_bwd(xb, _rms_stats(xb), g, dnb)
        return dyb + dx, dg

    dx, dg_pre = _rowwise(tag + "_pre_b", pre_b, [x, dn, dy], [g_pre], [(d, F32)], [(1, d)], tb)
    return dx, dg_pre, dwgu[:, :D_FF], dwgu[:, D_FF:], dwd, dg_post


NEG = -1e30


def _attn_scale():
    return (MLA_NOPE + MLA_ROPE) ** -0.5


def _causal_pairs(nq, by_key):
    if by_key:
        pairs = [(qi, ki) for ki in range(nq) for qi in range(ki, nq)]
    else:
        pairs = [(qi, ki) for qi in range(nq) for ki in range(qi + 1)]
    return jnp.asarray([p[0] for p in pairs], jnp.int32), jnp.asarray([p[1] for p in pairs], jnp.int32)


def _below_diagonal(shape):
    return lax.broadcasted_iota(jnp.int32, shape, 1) <= lax.broadcasted_iota(jnp.int32, shape, 0)


def _attn_call(name, body, tables, args, in_kinds, out_kinds, scratch, t, tq):
    qmap = lambda h, p, qt, kt: (qt[p], h)
    kmap = lambda h, p, qt, kt: (kt[p], h)
    spec = lambda kind: pl.BlockSpec((tq, HEAD_LANES), qmap if kind == "q" else kmap)
    grid_spec = pltpu.PrefetchScalarGridSpec(
        num_scalar_prefetch=2, grid=(N_HEADS, tables[0].shape[0]), in_specs=[spec(kd) for kd in in_kinds],
        out_specs=[spec(kd) for kd in out_kinds], scratch_shapes=scratch)
    return pl.pallas_call(body, name=name, grid_spec=grid_spec,
                          out_shape=[jax.ShapeDtypeStruct((t, MLA_PAD), F32) for _ in out_kinds],
                          compiler_params=_params(("parallel", "arbitrary")))(*tables, *args)


def _attn_fwd(q, k, v, tq):
    t = q.shape[0]
    nq = t // tq

    def body(qt, kt, q_ref, k_ref, v_ref, o_ref, lse_ref, m_s, l_s, acc_s):
        p_id = pl.program_id(1)
        qi, ki = qt[p_id], kt[p_id]

        @pl.when(ki == 0)
        def _():
            m_s[...] = jnp.full(m_s.shape, NEG, F32)
            l_s[...] = jnp.zeros(l_s.shape, F32)
            acc_s[...] = jnp.zeros(acc_s.shape, F32)

        def scores():
            return lax.dot_general(q_ref[...], k_ref[...], (((1,), (1,)), ((), ())), preferred_element_type=F32)

        def update(s):
            m_old = m_s[...]
            m_new = jnp.maximum(m_old, jnp.max(s, axis=1, keepdims=True))
            alpha = jnp.exp(m_old - m_new)
            p = jnp.exp(s - m_new[:, :1])
            l_s[...] = l_s[...] * alpha + jnp.sum(p, axis=1, keepdims=True)
            acc_s[...] = acc_s[...] * alpha + jnp.dot(p.astype(MM_DTYPE), v_ref[...], preferred_element_type=F32)
            m_s[...] = m_new

        @pl.when(ki < qi)
        def _():
            update(scores())

        @pl.when(ki == qi)
        def _():
            s = scores()
            update(jnp.where(_below_diagonal(s.shape), s, NEG))
            o_ref[...] = acc_s[...] / l_s[...]
            lse_ref[...] = m_s[...] + jnp.log(l_s[...])

    return _attn_call("mla_attn_fwd", body, _causal_pairs(nq, False), (q, k, v), "qkk", "qq",
                      [pltpu.VMEM((tq, HEAD_LANES), F32)] * 3, t, tq)


def _attn_probs(q, k, lse, diagonal):
    s = lax.dot_general(q, k, (((1,), (1,)), ((), ())), preferred_element_type=F32)
    p = jnp.exp(s - lse[:, :1])
    return jnp.where(_below_diagonal(s.shape), p, 0.0) if diagonal else p


def _attn_bwd_dq(q, k, v, do, lse, delta, tq):
    t = q.shape[0]
    nq = t // tq

    def body(qt, kt, q_ref, k_ref, v_ref, do_ref, lse_ref, dl_ref, dq_ref, acc_s):
        p_id = pl.program_id(1)
        qi, ki = qt[p_id], kt[p_id]

        @pl.when(ki == 0)
        def _():
            acc_s[...] = jnp.zeros(acc_s.shape, F32)

        def step(diagonal):
            p = _attn_probs(q_ref[...], k_ref[...], lse_ref[...], diagonal)
            dp = lax.dot_general(do_ref[...], v_ref[...], (((1,), (1,)), ((), ())), preferred_element_type=F32)
            ds = p * (dp - dl_ref[...][:, :1])
            acc_s[...] += jnp.dot(ds.astype(MM_DTYPE), k_ref[...], preferred_element_type=F32)

        @pl.when(ki < qi)
        def _():
            step(False)

        @pl.when(ki == qi)
        def _():
            step(True)
            dq_ref[...] = acc_s[...]

    return _attn_call("mla_attn_bwd_dq", body, _causal_pairs(nq, False), (q, k, v, do, lse, delta), "qkkqqq", "q",
                      [pltpu.VMEM((tq, HEAD_LANES), F32)], t, tq)[0]


def _attn_bwd_dkv(q, k, v, do, lse, delta, tq):
    t = q.shape[0]
    nq = t // tq

    def body(qt, kt, q_ref, k_ref, v_ref, do_ref, lse_ref, dl_ref, dk_ref, dv_ref, dk_s, dv_s):
        p_id = pl.program_id(1)
        qi, ki = qt[p_id], kt[p_id]

        def step(diagonal):
            p = _attn_probs(q_ref[...], k_ref[...], lse_ref[...], diagonal)
            dv_s[...] += lax.dot_general(p.astype(MM_DTYPE), do_ref[...], (((0,), (0,)), ((), ())),
                                         preferred_element_type=F32)
            dp = lax.dot_general(do_ref[...], v_ref[...], (((1,), (1,)), ((), ())), preferred_element_type=F32)
            ds = p * (dp - dl_ref[...][:, :1])
            dk_s[...] += lax.dot_general(ds.astype(MM_DTYPE), q_ref[...], (((0,), (0,)), ((), ())),
                                         preferred_element_type=F32)

        @pl.when(qi == ki)
        def _():
            dk_s[...] = jnp.zeros(dk_s.shape, F32)
            dv_s[...] = jnp.zeros(dv_s.shape, F32)
            step(True)

        @pl.when(qi > ki)
        def _():
            step(False)

        @pl.when(qi == nq - 1)
        def _():
            dk_ref[...] = dk_s[...]
            dv_ref[...] = dv_s[...]

    return _attn_call("mla_attn_bwd_dkv", body, _causal_pairs(nq, True), (q, k, v, do, lse, delta), "qkkqqq", "kk",
                      [pltpu.VMEM((tq, HEAD_LANES), F32)] * 2, t, tq)


def _dotf(a, b, dims=(((1,), (0,)), ((), ()))):
    return lax.dot_general(a, b, dims, preferred_element_type=F32, precision=HI)


def _dot1(a, b, dims=(((1,), (0,)), ((), ()))):
    return lax.dot_general(a.astype(MM_DTYPE), b.astype(MM_DTYPE), dims, preferred_element_type=F32)


NN3 = (((2,), (1,)), ((0,), (0,)))
NT3 = (((2,), (2,)), ((0,), (0,)))
TN3 = (((1,), (1,)), ((0,), (0,)))


def _tri_masks(nh):
    shape = (nh, CHUNK, CHUNK)
    return lax.broadcasted_iota(jnp.int32, shape, 1), lax.broadcasted_iota(jnp.int32, shape, 2)


def _gdn_chunk_common(k, gb, bb, row, col):
    tril = row >= col
    ltri = tril.astype(F32)
    umat = (row <= col).astype(F32)
    gcc = _dotf(ltri, gb, NN3)
    gcr = _dotf(gb, umat, TN3)
    dm = jnp.exp(jnp.where(tril, gcc - gcr, NEG))
    kb = k * bb
    lm = jnp.where(row > col, _dot1(kb, k, NT3) * dm, 0.0)
    return gcc, dm, kb, lm, umat


def _unit_lower_inverse(lm, eye):
    t = eye - lm
    p = lm
    for _ in range(CHUNK.bit_length() - 2):
        p = _dotf(p, p, NN3)
        t = t + _dotf(t, p, NN3)
    return t


def _gdn_fwd(q, k, v, gb, bb):
    nh, t, dh = q.shape
    nchunk = t // CHUNK

    def body(q_ref, k_ref, v_ref, g_ref, b_ref, o_ref, sall_ref, tall_ref, s_s):
        @pl.when(pl.program_id(0) == 0)
        def _():
            s_s[...] = jnp.zeros(s_s.shape, F32)

        row, col = _tri_masks(nh)
        qh, kh, vh, bbh = q_ref[...], k_ref[...], v_ref[...], b_ref[...]
        gcc, dm, kb, lm, _ = _gdn_chunk_common(kh, g_ref[...], bbh, row, col)
        eg = jnp.exp(gcc)
        glr = gcc[:, CHUNK - 1:CHUNK, :]
        th = _unit_lower_inverse(lm, (row == col).astype(F32))
        w = _dot1(th, kb * eg, NN3)
        u = _dot1(th, vh * bbh, NN3)
        at = jnp.where(row >= col, _dot1(qh, kh, NT3) * dm, 0.0)
        sh = s_s[...]
        vn = u - _dot1(w, sh, NN3)
        o_ref[...] = _dot1(qh * eg, sh, NN3) + _dot1(at, vn, NN3)
        kd = kh * jnp.exp(glr - gcc)
        sall_ref[:, 0] = sh
        tall_ref[...] = th
        s_s[...] = sh * jnp.exp(glr) + _dot1(kd, vn, TN3)

    blk = pl.BlockSpec((nh, CHUNK, dh), lambda n: (0, n, 0))
    return pl.pallas_call(
        body, name="gdn_fwd", grid=(nchunk,), in_specs=[blk] * 5,
        out_specs=[blk, pl.BlockSpec((nh, 1, dh, dh), lambda n: (0, n, 0, 0)), blk],
        out_shape=[jax.ShapeDtypeStruct((nh, t, dh), F32), jax.ShapeDtypeStruct((nh, nchunk, dh, dh), F32),
                   jax.ShapeDtypeStruct((nh, t, CHUNK), F32)],
        scratch_shapes=[pltpu.VMEM((nh, dh, dh), F32)],
        compiler_params=_params(("arbitrary",)))(q, k, v, gb, bb)


def _gdn_bwd(q, k, v, gb, bb, sall, tall, do):
    nh, t, dh = q.shape
    nchunk = t // CHUNK

    def body(q_ref, k_ref, v_ref, g_ref, b_ref, sall_ref, tall_ref, do_ref,
             dq_ref, dk_ref, dv_ref, dg_ref, db_ref, ds_s):
        @pl.when(pl.program_id(0) == 0)
        def _():
            ds_s[...] = jnp.zeros(ds_s.shape, F32)

        row, col = _tri_masks(nh)
        tril, stril = row >= col, row > col
        rsum = lambda x: jnp.sum(x, axis=2, keepdims=True)
        qh, kh, vh, gbh, bbh = q_ref[...], k_ref[...], v_ref[...], g_ref[...], b_ref[...]
        sh, th, doh, dsp = sall_ref[:, 0], tall_ref[...], do_ref[...], ds_s[...]
        gcc, dm, kb, lm, umat = _gdn_chunk_common(kh, gbh, bbh, row, col)
        eg = jnp.exp(gcc)
        glr = gcc[:, CHUNK - 1:CHUNK, :]
        glv = jnp.exp(glr)
        egl = jnp.exp(glr - gcc)
        rw, ru = kb * eg, vh * bbh
        w, u = _dot1(th, rw, NN3), _dot1(th, ru, NN3)
        at = jnp.where(tril, _dot1(qh, kh, NT3) * dm, 0.0)
        qd, kd = qh * eg, kh * egl
        vn = u - _dot1(w, sh, NN3)
        dgl = jnp.sum(rsum(dsp * sh), axis=1, keepdims=True)
        dkd = _dot1(vn, dsp, NT3)
        dvn = _dot1(kd, dsp, NN3)
        dqd = _dot1(doh, sh, NT3)
        dat = jnp.where(tril, _dot1(doh, vn, NT3), 0.0)
        dvn = dvn + _dot1(at, doh, TN3)
        dw = -_dot1(dvn, sh, NT3)
        ds_s[...] = dsp * glv + _dot1(qd, doh, TN3) - _dot1(w, dvn, TN3)
        dpa = dat * dm
        dq_ref[...] = _dot1(dpa, kh, NN3) + dqd * eg
        dk = _dot1(dpa, qh, TN3) + dkd * egl
        t6 = rsum(dkd * kd)
        dgam = rsum(dqd * qd) - t6
        dgam_last = jnp.sum(t6, axis=1, keepdims=True) + dgl * glv
        drw = _dot1(th, dw, TN3)
        dru = _dot1(th, dvn, TN3)
        dl = -jnp.where(stril, _dot1(drw, w, NT3) + _dot1(dru, u, NT3), 0.0)
        dgam = dgam + rsum(drw * rw)
        dv_ref[...] = dru * bbh
        dp2 = dl * dm
        dkb = drw * eg + _dot1(dp2, kh, NN3)
        dk_ref[...] = dk + _dot1(dp2, kb, TN3) + dkb * bbh
        db_ref[...] = rsum(dru * vh) + rsum(dkb * kh) + jnp.zeros((nh, CHUNK, dh), F32)
        e = dat * at + dl * lm
        dgam_b = dgam + rsum(e) - _dotf(e, jnp.ones((nh, CHUNK, CHUNK), F32), TN3)
        dgam_b = dgam_b + jnp.where(row == CHUNK - 1, dgam_last, 0.0)
        dg_ref[...] = _dotf(umat, dgam_b, NN3)

    rev = lambda n: (0, nchunk - 1 - n, 0)
    blk = pl.BlockSpec((nh, CHUNK, dh), rev)
    sblk = pl.BlockSpec((nh, 1, dh, dh), lambda n: (0, nchunk - 1 - n, 0, 0))
    out = jax.ShapeDtypeStruct((nh, t, dh), F32)
    return pl.pallas_call(
        body, name="gdn_bwd", grid=(nchunk,), in_specs=[blk] * 5 + [sblk, blk, blk], out_specs=[blk] * 5,
        out_shape=[out] * 5, scratch_shapes=[pltpu.VMEM((nh, dh, dh), F32)],
        compiler_params=_params(("arbitrary",)))(q, k, v, gb, bb, sall, tall, do)


def _group_ones():
    r = lax.broadcasted_iota(jnp.int32, (GDN_W, GDN_W), 0) // GDN_DH
    c = lax.broadcasted_iota(jnp.int32, (GDN_W, GDN_W), 1) // GDN_DH
    return (r == c).astype(F32)


def _conv_taps(x, xprev, w, has_prev):
    row = lax.broadcasted_iota(jnp.int32, x.shape, 0)
    out = x * w[GDN_CONV - 1:GDN_CONV, :]
    for s in range(1, GDN_CONV):
        sh = jnp.where(row >= s, _roll(x, s, 0), _roll(xprev, s, 0) * has_prev)
        out = out + sh * w[GDN_CONV - 1 - s:GDN_CONV - s, :]
    return out


def _to_heads(x):
    t = x.shape[0]
    return x.reshape(t, N_HEADS, GDN_DH).transpose(1, 0, 2)


def _from_heads(x):
    nh, t, dh = x.shape
    return x.transpose(1, 0, 2).reshape(t, nh * dh)


def _lane_scalar_heads(x8):
    t = x8.shape[0]
    return jnp.broadcast_to(x8.T[:, :, None], (N_HEADS, t, GDN_DH))


def _mixer_fwd(x, positions, w, tb):
    t, d = x.shape
    tables = _rope_tables(positions)

    def pre(xb, g):
        return (xb * _rms_stats(xb) * g,)

    (hn,) = _rowwise("mix_pre", pre, [x], [w["mix_pre_g"]], [(d, BF16)], [], tb)
    proj = _mm("mix_in", hn, w["w_in_pad"], "nn", F32)

    def mla_pre(p0, gq, gkv):
        cq, ckv = p0[:, :MLA_Q_RANK], p0[:, MLA_Q_RANK:MLA_Q_RANK + MLA_KV_RANK]
        return cq * _rms_stats(cq) * gq, ckv * _rms_stats(ckv) * gkv

    nq, nkv = _rowwise("mla_pre", mla_pre, [(proj, 512, 0, 0)], [w["mla_q_norm_g"], w["mla_kv_norm_g"]],
                       [(MLA_Q_RANK, BF16), (MLA_KV_RANK, BF16)], [], tb)
    qraw = _mm("mla_uq", nq, w["w_uq_pad"], "nn", F32)
    kv = _mm("mla_ukv", nkv, w["w_kv_pad"], "nn", F32)

    def rope_f(qr, kn, vv, kpe, c, s1, s2):
        qo = _heads_apply(qr, lambda xh: _rope(xh, c, s1, s2)) * _attn_scale()
        kp = _rope(kpe, c, s1, s2)
        return qo, kn + jnp.tile(kp, (1, N_HEADS)), vv

    q, k, v = _rowwise("mla_rope", rope_f,
                       [qraw, (kv, MLA_PAD, 0, 0), (kv, MLA_PAD, 1, 0), (proj, HEAD_LANES, PIN_KPE // HEAD_LANES, 0),
                        tables[0], tables[1], tables[2]], [],
                       [(MLA_PAD, BF16)] * 3, [], tb // 2)
    tq = min(512, t)
    o, lse = _attn_fwd(q, k, v, tq)

    def mla_post(ob, g):
        return (ob * _rms_stats(ob, N_HEADS * MLA_V) * g,)

    (mo,) = _rowwise("mla_post", mla_post, [o], [w["mla_out_g_pad"]], [(MLA_PAD, BF16)], [], tb)

    gones = _group_ones()
    qkv_cols = [(proj, GDN_W, PIN_QKV // GDN_W + j, 0) for j in range(3)]
    steps = t // tb

    def gdn_pre(xq, xk, xv, pq, pk, pv, cw, go, has_prev):
        outs = []
        for j, (xc, xp) in enumerate(((xq, pq), (xk, pk), (xv, pv))):
            c = _conv_taps(xc, xp, cw[:, j * GDN_W:(j + 1) * GDN_W], has_prev)
            a = c * _sigmoid(c)
            if j < 2:
                rn = lax.rsqrt(_dotf(a * a, go) + EPS)
                a = a * rn
                if j == 0:
                    a = a * (GDN_DH ** -0.5)
            outs.append(a)
        return tuple(outs)

    qn, kn, vv = _gdn_pre_call("gdn_pre", gdn_pre, proj, w["conv_w"], gones, tb, steps)
    ab = _rowwise("gdn_gate_f", lambda abb, al, dtb: _gb_fwd(abb, al, dtb),
                  [(proj, HEAD_LANES, PIN_AB // HEAD_LANES, 0)], [w["a_log_pad"], w["dt_bias_pad"]],
                  [(HEAD_LANES, F32), (HEAD_LANES, F32)], [], tb)
    g8, b8 = ab[0][:, :N_HEADS], ab[1][:, N_HEADS:2 * N_HEADS]
    qh, kh, vh = _to_heads(qn), _to_heads(kn), _to_heads(vv)
    gbh, bbh = _lane_scalar_heads(g8), _lane_scalar_heads(b8)
    oh, sall, tall = _gdn_fwd(qh, kh, vh, gbh, bbh)
    gate_h = _to_heads(proj[:, PIN_GATE:PIN_GATE + GDN_W]).reshape(N_HEADS * t, GDN_DH)
    oh2 = oh.reshape(N_HEADS * t, GDN_DH)

    def gdn_post(ob, gt, g):
        return (ob * _rms_stats(ob) * g * (gt * _sigmoid(gt)),)

    (go_h,) = _rowwise("gdn_post", gdn_post, [oh2, gate_h], [w["gdn_norm_g"]], [(GDN_DH, BF16)], [], 8 * tb)
    gdn_o = _from_heads(go_h.reshape(N_HEADS, t, GDN_DH))

    cat = jnp.concatenate([mo, gdn_o], axis=1)
    mixed = _mm("mix_out", cat, w["w_out_pad"], "nn", F32)

    def post(xb, hb, g):
        return (xb + hb * _rms_stats(hb) * g,)

    (y,) = _rowwise("mix_post", post, [x, mixed], [w["mix_post_g"]], [(d, F32)], [], tb)
    saved = dict(x=x, hn=hn, proj=proj, nq=nq, nkv=nkv, q=q, k=k, v=v, o=o, lse=lse, qh=qh, kh=kh, vh=vh,
                 gbh=gbh, bbh=bbh, oh2=oh2, sall=sall, tall=tall, gate_h=gate_h, cat=cat, mixed=mixed,
                 tables=tables, g128=ab[0], b128=ab[1])
    return y, saved


def _gdn_pre_call(name, fn, proj, conv_w, gones, tb, steps):
    t = proj.shape[0]
    base = PIN_QKV // GDN_W
    cur = [pl.BlockSpec((tb, GDN_W), lambda i, j=j: (i, base + j)) for j in range(3)]
    prev = [pl.BlockSpec((tb, GDN_W), lambda i, j=j: (jnp.maximum(i - 1, 0), base + j)) for j in range(3)]

    def body(xq, xk, xv, pq, pk, pv, cw, go, oq, ok, ov):
        has_prev = jnp.where(pl.program_id(0) == 0, 0.0, 1.0)
        outs = fn(xq[...], xk[...], xv[...], pq[...], pk[...], pv[...], cw[...], go[...], has_prev)
        for r, val in zip((oq, ok, ov), outs):
            r[...] = val

    return pl.pallas_call(
        body, name=name, grid=(steps,),
        in_specs=cur + prev + [pl.BlockSpec(conv_w.shape, lambda i: (0, 0)), pl.BlockSpec(gones.shape, lambda i: (0, 0))],
        out_specs=[pl.BlockSpec((tb, GDN_W), lambda i: (i, 0))] * 3,
        out_shape=[jax.ShapeDtypeStruct((t, GDN_W), F32)] * 3,
        compiler_params=_params(("arbitrary",)))(proj, proj, proj, proj, proj, proj, conv_w, gones)


def _softplus(x):
    return jnp.maximum(x, 0.0) + jnp.log1p(jnp.exp(-jnp.abs(x)))


def _gb_fwd(ab, a_log, dt_bias):
    g = -jnp.exp(a_log) * _softplus(ab + dt_bias)
    return g, _sigmoid(ab)


def _rope_tables(positions):
    half = MLA_ROPE // 2
    freqs = ROPE_THETA ** (-jnp.arange(half, dtype=F32) / half)
    ang = positions.reshape(-1).astype(F32)[:, None] * freqs
    cos, sin = jnp.cos(ang), jnp.sin(ang)
    t = ang.shape[0]
    one = jnp.ones((t, MLA_NOPE), F32)
    z16, z32, z64 = jnp.zeros((t, half), F32), jnp.zeros((t, MLA_ROPE), F32), jnp.zeros((t, MLA_NOPE), F32)
    c = jnp.concatenate([one, cos, cos, jnp.ones((t, MLA_ROPE), F32)], axis=1)
    s1 = jnp.concatenate([z64, -sin, z16, z32], axis=1)
    s2 = jnp.concatenate([z64, z16, sin, z32], axis=1)
    return c, s1, s2


def _mixer_bwd(dy, sv, w, tb):
    x, proj = sv["x"], sv["proj"]
    t, d = x.shape
    c, s1, s2 = sv["tables"]
    grads = {}

    def post_b(hb, dyb, g):
        return _rms_bwd(hb, _rms_stats(hb), g, dyb)

    dmixed, grads["mix_post_g"] = _rowwise("mix_post_b", post_b, [sv["mixed"], dy], [w["mix_post_g"]],
                                           [(d, BF16)], [(1, d)], tb)
    dcat = _mm("mix_out_bx", dmixed, w["w_out_pad"], "nt", F32)
    grads["w_out_pad"] = _mm("mix_out_bw", sv["cat"], dmixed, "tn", F32)

    def mla_post_b(ob, dmo, g):
        do, dg = _rms_bwd(ob, _rms_stats(ob, N_HEADS * MLA_V), g, dmo, N_HEADS * MLA_V)
        prod = do * ob
        delta = _heads_apply(prod, lambda ph: jnp.sum(ph, axis=1, keepdims=True) + jnp.zeros_like(ph))
        return do, delta, dg

    do, delta, grads["mla_out_g_pad"] = _rowwise(
        "mla_post_b", mla_post_b, [sv["o"], (dcat, MLA_PAD, 0, 0)], [w["mla_out_g_pad"]],
        [(MLA_PAD, BF16), (MLA_PAD, F32)], [(1, MLA_PAD)], tb // 2)
    tq = min(512, t)
    dq = _attn_bwd_dq(sv["q"], sv["k"], sv["v"], do, sv["lse"], delta, tq)
    dk, dv = _attn_bwd_dkv(sv["q"], sv["k"], sv["v"], do, sv["lse"], delta, tq)

    def rope_b(dqb, dkb, dvb, cc, a1, a2):
        dqr = _heads_apply(dqb * _attn_scale(), lambda xh: _rope(xh, cc, -a1, -a2))
        ksum = dkb[:, :HEAD_LANES]
        for h in range(1, N_HEADS):
            ksum = ksum + dkb[:, h * HEAD_LANES:(h + 1) * HEAD_LANES]
        lane = lax.broadcasted_iota(jnp.int32, ksum.shape, 1)
        keep = (lane >= MLA_NOPE) & (lane < MLA_NOPE + MLA_ROPE)
        dkpe = jnp.where(keep, _rope(ksum, cc, -a1, -a2), 0.0)
        return dqr, jnp.concatenate([dkb, dvb], axis=1), dkpe

    dqraw, dkv, dkpe = _rowwise("mla_rope_b", rope_b, [dq, dk, dv, c, s1, s2], [],
                                [(MLA_PAD, BF16), (2 * MLA_PAD, BF16), (HEAD_LANES, F32)], [], tb // 2)
    dnq = _mm("mla_uq_bx", dqraw, w["w_uq_pad"], "nt", F32)
    grads["w_uq_pad"] = _mm("mla_uq_bw", sv["nq"], dqraw, "tn", F32)
    dnkv = _mm("mla_ukv_bx", dkv, w["w_kv_pad"], "nt", F32)
    grads["w_kv_pad"] = _mm("mla_ukv_bw", sv["nkv"], dkv, "tn", F32)

    def mla_pre_b(p0, dnqb, dnkvb, dkpeb, gq, gkv):
        cq, ckv = p0[:, :MLA_Q_RANK], p0[:, MLA_Q_RANK:MLA_Q_RANK + MLA_KV_RANK]
        dcq, dgq = _rms_bwd(cq, _rms_stats(cq), gq, dnqb)
        dckv, dgkv = _rms_bwd(ckv, _rms_stats(ckv), gkv, dnkvb)
        return jnp.concatenate([dcq, dckv, dkpeb], axis=1), dgq, dgkv

    dp0, grads["mla_q_norm_g"], grads["mla_kv_norm_g"] = _rowwise(
        "mla_pre_b", mla_pre_b, [(proj, 512, 0, 0), dnq, dnkv, dkpe], [w["mla_q_norm_g"], w["mla_kv_norm_g"]],
        [(512, BF16)], [(1, MLA_Q_RANK), (1, MLA_KV_RANK)], tb)

    dgo_h = _to_heads(dcat[:, MLA_PAD:]).reshape(N_HEADS * t, GDN_DH)

    def gdn_post_b(ob, gt, dob, g):
        sg = _sigmoid(gt)
        sil = gt * sg
        r = _rms_stats(ob)
        dz = dob * sil
        do_, dg = _rms_bwd(ob, r, g, dz)
        dgate = dob * (ob * r * g) * (sg * (1.0 + gt * (1.0 - sg)))
        return do_, dgate, dg

    doh, dgate_h, grads["gdn_norm_g"] = _rowwise(
        "gdn_post_b", gdn_post_b, [sv["oh2"], sv["gate_h"], dgo_h], [w["gdn_norm_g"]],
        [(GDN_DH, F32), (GDN_DH, BF16)], [(1, GDN_DH)], 8 * tb)
    dqh, dkh, dvh, dgh, dbh = _gdn_bwd(sv["qh"], sv["kh"], sv["vh"], sv["gbh"], sv["bbh"], sv["sall"], sv["tall"],
                                       doh.reshape(N_HEADS, t, GDN_DH))
    dqn, dkn, dvv = _from_heads(dqh), _from_heads(dkh), _from_heads(dvh)
    dg8, db8 = dgh[:, :, 0].T, dbh[:, :, 0].T
    dgb128 = jnp.concatenate([dg8, db8, jnp.zeros((t, HEAD_LANES - 2 * N_HEADS), F32)], axis=1)
    gones = _group_ones()
    steps = t // tb

    def gdn_pre_b(xq, xk, xv, pq, pk, pv, dq_, dk_, dv_, cw, go, has_prev):
        outs = []
        for j, (xc, xp, dd) in enumerate(((xq, pq, dq_), (xk, pk, dk_), (xv, pv, dv_))):
            cc = _conv_taps(xc, xp, cw[:, j * GDN_W:(j + 1) * GDN_W], has_prev)
            sg = _sigmoid(cc)
            a = cc * sg
            if j < 2:
                rn = lax.rsqrt(_dotf(a * a, go) + EPS)
                if j == 0:
                    dd = dd * (GDN_DH ** -0.5)
                da = rn * dd - a * (rn * rn * rn) * _dotf(dd * a, go)
            else:
                da = dd
            outs.append(da * (sg * (1.0 + cc * (1.0 - sg))))
        return tuple(outs)

    dcq, dck, dcv = _gdn_pre_b_call("gdn_pre_b", gdn_pre_b, proj, (dqn, dkn, dvv), w["conv_w"], gones, tb, steps)
    dxq, dxk, dxv, dcw = _conv_bwd_call("gdn_conv_b", proj, (dcq, dck, dcv), w["conv_w"], tb, steps)
    grads["conv_w"] = dcw

    def gb_b(abb, g128, b128, dgb, al, dtb):
        xs = abb + dtb
        sg = _sigmoid(xs)
        ea = jnp.exp(al)
        lane = lax.broadcasted_iota(jnp.int32, abb.shape, 1)
        is_a = lane < N_HEADS
        is_b = (lane >= N_HEADS) & (lane < 2 * N_HEADS)
        dg_ = jnp.where(is_a, dgb, 0.0)
        dab = dg_ * (-ea * sg) + jnp.where(is_b, dgb * b128 * (1.0 - b128), 0.0)
        return dab, jnp.sum(dg_ * g128, axis=0, keepdims=True), jnp.sum(dg_ * (-ea * sg), axis=0, keepdims=True)

    dab, grads["a_log_pad"], grads["dt_bias_pad"] = _rowwise(
        "gdn_gate_b", gb_b, [(proj, HEAD_LANES, PIN_AB // HEAD_LANES, 0), sv["g128"], sv["b128"], dgb128],
        [w["a_log_pad"], w["dt_bias_pad"]], [(HEAD_LANES, BF16)], [(1, HEAD_LANES), (1, HEAD_LANES)], tb)
    dgate = _from_heads(dgate_h.reshape(N_HEADS, t, GDN_DH))

    dproj = jnp.concatenate([dp0, dxq, dxk, dxv, dab, dgate], axis=1)
    dhn = _mm("mix_in_bx", dproj, w["w_in_pad"], "nt", F32)
    grads["w_in_pad"] = _mm("mix_in_bw", sv["hn"], dproj, "tn", F32)

    def pre_b(xb, dnb, dyb, g):
        dx, dg = _rms_bwd(xb, _rms_stats(xb), g, dnb)
        return dyb + dx, dg

    dx, grads["mix_pre_g"] = _rowwise("mix_pre_b", pre_b, [x, dhn, dy], [w["mix_pre_g"]], [(d, F32)], [(1, d)], tb)
    return dx, grads


def _gdn_pre_b_call(name, fn, proj, dd, conv_w, gones, tb, steps):
    t = proj.shape[0]
    base = PIN_QKV // GDN_W
    cur = [pl.BlockSpec((tb, GDN_W), lambda i, j=j: (i, base + j)) for j in range(3)]
    prev = [pl.BlockSpec((tb, GDN_W), lambda i, j=j: (jnp.maximum(i - 1, 0), base + j)) for j in range(3)]
    dsp = [pl.BlockSpec((tb, GDN_W), lambda i: (i, 0))] * 3

    def body(xq, xk, xv, pq, pk, pv, d0, d1, d2, cw, go, oq, ok, ov):
        has_prev = jnp.where(pl.program_id(0) == 0, 0.0, 1.0)
        outs = fn(xq[...], xk[...], xv[...], pq[...], pk[...], pv[...], d0[...], d1[...], d2[...], cw[...], go[...],
                  has_prev)
        for r, val in zip((oq, ok, ov), outs):
            r[...] = val

    return pl.pallas_call(
        body, name=name, grid=(steps,),
        in_specs=cur + prev + dsp + [pl.BlockSpec(conv_w.shape, lambda i: (0, 0)),
                                     pl.BlockSpec(gones.shape, lambda i: (0, 0))],
        out_specs=[pl.BlockSpec((tb, GDN_W), lambda i: (i, 0))] * 3,
        out_shape=[jax.ShapeDtypeStruct((t, GDN_W), F32)] * 3,
        compiler_params=_params(("arbitrary",)))(proj, proj, proj, proj, proj, proj, *dd, conv_w, gones)


def _conv_bwd_call(name, proj, dc, conv_w, tb, steps):
    t = proj.shape[0]
    base = PIN_QKV // GDN_W
    cur = [pl.BlockSpec((tb, GDN_W), lambda i, j=j: (i, base + j)) for j in range(3)]
    prev = [pl.BlockSpec((tb, GDN_W), lambda i, j=j: (jnp.maximum(i - 1, 0), base + j)) for j in range(3)]
    dcur = [pl.BlockSpec((tb, GDN_W), lambda i: (i, 0))] * 3
    dnext = [pl.BlockSpec((tb, GDN_W), lambda i: (jnp.minimum(i + 1, steps - 1), 0))] * 3

    def body(xq, xk, xv, pq, pk, pv, d0, d1, d2, n0, n1, n2, cw, oq, ok, ov, dw_ref):
        i = pl.program_id(0)
        has_prev = jnp.where(i == 0, 0.0, 1.0)
        has_next = jnp.where(i == steps - 1, 0.0, 1.0)

        @pl.when(i == 0)
        def _():
            dw_ref[...] = jnp.zeros(dw_ref.shape, F32)

        wv = cw[...]
        dws = []
        for j, (xr, pr, dr, nr, orf) in enumerate(((xq, pq, d0, n0, oq), (xk, pk, d1, n1, ok), (xv, pv, d2, n2, ov))):
            x, xp, dcv, dnx = xr[...], pr[...], dr[...], nr[...]
            wj = wv[:, j * GDN_W:(j + 1) * GDN_W]
            row = lax.broadcasted_iota(jnp.int32, x.shape, 0)
            dx = dcv * wj[GDN_CONV - 1:GDN_CONV, :]
            rows_w = [jnp.sum(dcv * x, axis=0, keepdims=True)]
            for s in range(1, GDN_CONV):
                up = jnp.where(row < tb - s, _roll(dcv, tb - s, 0), _roll(dnx, tb - s, 0) * has_next)
                dx = dx + up * wj[GDN_CONV - 1 - s:GDN_CONV - s, :]
                sh = jnp.where(row >= s, _roll(x, s, 0), _roll(xp, s, 0) * has_prev)
                rows_w.append(jnp.sum(dcv * sh, axis=0, keepdims=True))
            orf[...] = dx.astype(orf.dtype)
            dws.append(jnp.concatenate(rows_w[::-1], axis=0))
        dw_ref[...] += jnp.concatenate(dws, axis=1)

    outs = pl.pallas_call(
        body, name=name, grid=(steps,),
        in_specs=cur + prev + dcur + dnext + [pl.BlockSpec(conv_w.shape, lambda i: (0, 0))],
        out_specs=[pl.BlockSpec((tb, GDN_W), lambda i: (i, 0))] * 3 + [pl.BlockSpec(conv_w.shape, lambda i: (0, 0))],
        out_shape=[jax.ShapeDtypeStruct((t, GDN_W), BF16)] * 3 + [jax.ShapeDtypeStruct(conv_w.shape, F32)],
        compiler_params=_params(("arbitrary",)))(proj, proj, proj, proj, proj, proj, *dc, *dc, conv_w)
    return outs


def _pad_heads_cols(wm, per_head):
    r = wm.shape[0]
    return jnp.pad(wm.reshape(r, N_HEADS, per_head), ((0, 0), (0, 0), (0, HEAD_LANES - per_head))).reshape(r, MLA_PAD)


def _unpad_heads_cols(wm, per_head):
    r = wm.shape[0]
    return wm.reshape(r, N_HEADS, HEAD_LANES)[:, :, :per_head].reshape(r, N_HEADS * per_head)


def _win_to_pad(wi):
    r = wi.shape[0]
    z = lambda n: jnp.zeros((r, n), wi.dtype)
    o = MLA_Q_RANK + MLA_KV_RANK
    kpe = wi[:, o:o + MLA_ROPE]
    o2 = o + MLA_ROPE
    qkv = wi[:, o2:o2 + 3 * GDN_W]
    o3 = o2 + 3 * GDN_W
    ab = wi[:, o3:o3 + 2 * N_HEADS]
    gate = wi[:, o3 + 2 * N_HEADS:]
    return jnp.concatenate([wi[:, :o], z(MLA_NOPE), kpe, z(HEAD_LANES - MLA_NOPE - MLA_ROPE), qkv, ab,
                            z(HEAD_LANES - 2 * N_HEADS), gate], axis=1)


def _win_from_pad(wp):
    return jnp.concatenate([wp[:, :PIN_KPE], wp[:, PIN_KPE + MLA_NOPE:PIN_KPE + MLA_NOPE + MLA_ROPE],
                            wp[:, PIN_QKV:PIN_AB], wp[:, PIN_AB:PIN_AB + 2 * N_HEADS], wp[:, PIN_GATE:]], axis=1)


def _wkv_to_pad(wkv):
    r = wkv.shape[0]
    w3 = wkv.reshape(r, N_HEADS, MLA_NOPE + MLA_V)
    kpart = jnp.pad(w3[:, :, :MLA_NOPE], ((0, 0), (0, 0), (0, HEAD_LANES - MLA_NOPE))).reshape(r, MLA_PAD)
    vpart = jnp.pad(w3[:, :, MLA_NOPE:], ((0, 0), (0, 0), (0, HEAD_LANES - MLA_V))).reshape(r, MLA_PAD)
    return jnp.concatenate([kpart, vpart], axis=1)


def _wkv_from_pad(wp):
    r = wp.shape[0]
    kpart = wp[:, :MLA_PAD].reshape(r, N_HEADS, HEAD_LANES)[:, :, :MLA_NOPE]
    vpart = wp[:, MLA_PAD:].reshape(r, N_HEADS, HEAD_LANES)[:, :, :MLA_V]
    return jnp.concatenate([kpart, vpart], axis=2).reshape(r, N_HEADS * (MLA_NOPE + MLA_V))


def _wout_to_pad(wo):
    n = wo.shape[1]
    mla = jnp.pad(wo[:N_HEADS * MLA_V].reshape(N_HEADS, MLA_V, n), ((0, 0), (0, HEAD_LANES - MLA_V), (0, 0)))
    return jnp.concatenate([mla.reshape(MLA_PAD, n), wo[N_HEADS * MLA_V:]], axis=0)


def _wout_from_pad(wp):
    n = wp.shape[1]
    mla = wp[:MLA_PAD].reshape(N_HEADS, HEAD_LANES, n)[:, :MLA_V].reshape(N_HEADS * MLA_V, n)
    return jnp.concatenate([mla, wp[MLA_PAD:]], axis=0)


def _pad_lanes(v, n):
    return jnp.pad(v, ((0, 0), (0, n - v.shape[1])))


def _compute_weights(full):
    w = {}
    for tag in ("ffn1", "ffn2"):
        w[tag + "_wgu"] = jnp.concatenate([full[tag + "_w_gate"], full[tag + "_w_up"]], axis=1).astype(MM_DTYPE)
        w[tag + "_wd"] = full[tag + "_w_down"].astype(MM_DTYPE)
    w["w_in_pad"] = _win_to_pad(full["w_in"]).astype(MM_DTYPE)
    w["w_uq_pad"] = _pad_heads_cols(full["mla_w_uq"], MLA_NOPE + MLA_ROPE).astype(MM_DTYPE)
    w["w_kv_pad"] = _wkv_to_pad(full["mla_w_ukv"]).astype(MM_DTYPE)
    w["w_out_pad"] = _wout_to_pad(full["w_out"]).astype(MM_DTYPE)
    w["conv_w"] = full["gdn_conv_w"].astype(F32)
    for n in ("ffn1_pre_g", "ffn1_post_g", "mix_pre_g", "mla_q_norm_g", "mla_kv_norm_g", "gdn_norm_g", "mix_post_g",
              "ffn2_pre_g", "ffn2_post_g"):
        w[n] = full[n]
    w["mla_out_g_pad"] = _pad_heads_cols(full["mla_out_g"], MLA_V)
    w["a_log_pad"] = _pad_lanes(full["gdn_a_log"], HEAD_LANES)
    w["dt_bias_pad"] = _pad_lanes(full["gdn_dt_bias"], HEAD_LANES)
    return w


def _local_step(x, positions, loss_target, full):
    t, d = x.shape
    tb = min(512, t)
    w = _compute_weights(full)
    x1, sv1 = _ffn_fwd("ffn1", x, w["ffn1_pre_g"], w["ffn1_wgu"], w["ffn1_wd"], w["ffn1_post_g"], tb)
    x2, svm = _mixer_fwd(x1, positions, w, tb)
    x3, sv2 = _ffn_fwd("ffn2", x2, w["ffn2_pre_g"], w["ffn2_wgu"], w["ffn2_wd"], w["ffn2_post_g"], tb)

    def loss_f(yb, tg):
        e = yb - tg
        return e * (1.0 / d), jnp.sum(e * e, axis=0, keepdims=True)

    dy, lsum = _rowwise("loss", loss_f, [x3, loss_target], [], [(d, F32)], [(1, d)], tb)
    g = {}
    dx2, g["ffn2_pre_g"], g["ffn2_w_gate"], g["ffn2_w_up"], g["ffn2_w_down"], g["ffn2_post_g"] = _ffn_bwd(
        "ffn2", dy, sv2, w["ffn2_pre_g"], w["ffn2_wgu"], w["ffn2_wd"], w["ffn2_post_g"], tb)
    dx1, gm = _mixer_bwd(dx2, svm, w, tb)
    dx0, g["ffn1_pre_g"], g["ffn1_w_gate"], g["ffn1_w_up"], g["ffn1_w_down"], g["ffn1_post_g"] = _ffn_bwd(
        "ffn1", dx1, sv1, w["ffn1_pre_g"], w["ffn1_wgu"], w["ffn1_wd"], w["ffn1_post_g"], tb)
    g["mix_pre_g"], g["mix_post_g"] = gm["mix_pre_g"], gm["mix_post_g"]
    g["mla_q_norm_g"], g["mla_kv_norm_g"] = gm["mla_q_norm_g"], gm["mla_kv_norm_g"]
    g["gdn_norm_g"] = gm["gdn_norm_g"]
    g["w_in"] = _win_from_pad(gm["w_in_pad"])
    g["mla_w_uq"] = _unpad_heads_cols(gm["w_uq_pad"], MLA_NOPE + MLA_ROPE)
    g["mla_w_ukv"] = _wkv_from_pad(gm["w_kv_pad"])
    g["mla_out_g"] = _unpad_heads_cols(gm["mla_out_g_pad"], MLA_V)
    g["gdn_conv_w"] = gm["conv_w"]
    g["gdn_a_log"] = gm["a_log_pad"][:, :N_HEADS]
    g["gdn_dt_bias"] = gm["dt_bias_pad"][:, :N_HEADS]
    g["w_out"] = _wout_from_pad(gm["w_out_pad"])
    return lsum, dx0, g


HBM_SPEC = pl.BlockSpec(memory_space=pltpu.HBM)


def _place():
    return lax.axis_index("x"), lax.axis_index("y"), lax.axis_index("c")


def _gather_shards(wp):
    r, l = wp.shape

    def body(w_ref, out_ref, send_sems, recv_sems, local_sem):
        x, y, c = _place()
        chips = [(1 - x, y), (x, 1 - y), (1 - x, 1 - y)]
        mine = pltpu.make_async_copy(w_ref, out_ref.at[2 * x + y], local_sem)
        mine.start()
        sends = [pltpu.make_async_remote_copy(src_ref=w_ref, dst_ref=out_ref.at[2 * x + y], send_sem=send_sems.at[j],
                                              recv_sem=recv_sems.at[j], device_id=(px, py, c), device_id_type=MESH)
                 for j, (px, py) in enumerate(chips)]
        for cp in sends:
            cp.start()
        for j, (px, py) in enumerate(chips):
            pltpu.make_async_remote_copy(src_ref=w_ref, dst_ref=out_ref.at[2 * px + py], send_sem=send_sems.at[j],
                                         recv_sem=recv_sems.at[j], device_id=(px, py, c),
                                         device_id_type=MESH).wait_recv()
        for cp in sends:
            cp.wait_send()
        mine.wait()

    return pl.pallas_call(
        body, name="gather_weight_shards", in_specs=[HBM_SPEC], out_specs=HBM_SPEC,
        out_shape=jax.ShapeDtypeStruct((N_SHARD, r, l), wp.dtype),
        scratch_shapes=[pltpu.SemaphoreType.DMA((3,)), pltpu.SemaphoreType.DMA((3,)), pltpu.SemaphoreType.DMA])(wp)


def _swap_with_sibling(v):
    def body(v_ref, out_ref, send_sem, recv_sem):
        x, y, c = _place()
        cp = pltpu.make_async_remote_copy(src_ref=v_ref, dst_ref=out_ref, send_sem=send_sem, recv_sem=recv_sem,
                                          device_id=(x, y, 1 - c), device_id_type=MESH)
        cp.start()
        cp.wait()

    return pl.pallas_call(
        body, name="swap_grad_halves", in_specs=[HBM_SPEC], out_specs=HBM_SPEC,
        out_shape=jax.ShapeDtypeStruct(v.shape, v.dtype),
        scratch_shapes=[pltpu.SemaphoreType.DMA, pltpu.SemaphoreType.DMA])(v)


def _scatter_to_chips(p):
    def body(p_ref, out_ref, send_sems, recv_sems, local_sem):
        x, y, c = _place()
        me = 2 * x + y
        chips = [(1 - x, y), (x, 1 - y), (1 - x, 1 - y)]
        mine = pltpu.make_async_copy(p_ref.at[me], out_ref.at[me], local_sem)
        mine.start()
        sends = [pltpu.make_async_remote_copy(src_ref=p_ref.at[2 * px + py], dst_ref=out_ref.at[me],
                                              send_sem=send_sems.at[j], recv_sem=recv_sems.at[j],
                                              device_id=(px, py, c), device_id_type=MESH)
                 for j, (px, py) in enumerate(chips)]
        for cp in sends:
            cp.start()
        for j, (px, py) in enumerate(chips):
            pltpu.make_async_remote_copy(src_ref=p_ref.at[me], dst_ref=out_ref.at[2 * px + py],
                                         send_sem=send_sems.at[j], recv_sem=recv_sems.at[j], device_id=(px, py, c),
                                         device_id_type=MESH).wait_recv()
        for cp in sends:
            cp.wait_send()
        mine.wait()

    return pl.pallas_call(
        body, name="scatter_grad_quarters", in_specs=[HBM_SPEC], out_specs=HBM_SPEC,
        out_shape=jax.ShapeDtypeStruct(p.shape, p.dtype),
        scratch_shapes=[pltpu.SemaphoreType.DMA((3,)), pltpu.SemaphoreType.DMA((3,)), pltpu.SemaphoreType.DMA])(p)


def _share_halves(hs):
    def body(h_ref, out_ref, send_sem, recv_sem, local_sem):
        x, y, c = _place()
        mine = pltpu.make_async_copy(h_ref, out_ref.at[c], local_sem)
        mine.start()
        cp = pltpu.make_async_remote_copy(src_ref=h_ref, dst_ref=out_ref.at[c], send_sem=send_sem, recv_sem=recv_sem,
                                          device_id=(x, y, 1 - c), device_id_type=MESH)
        cp.start()
        pltpu.make_async_remote_copy(src_ref=h_ref, dst_ref=out_ref.at[1 - c], send_sem=send_sem, recv_sem=recv_sem,
                                     device_id=(x, y, 1 - c), device_id_type=MESH).wait_recv()
        cp.wait_send()
        mine.wait()

    return pl.pallas_call(
        body, name="share_grad_halves", in_specs=[HBM_SPEC], out_specs=HBM_SPEC,
        out_shape=jax.ShapeDtypeStruct((2,) + hs.shape, hs.dtype),
        scratch_shapes=[pltpu.SemaphoreType.DMA, pltpu.SemaphoreType.DMA, pltpu.SemaphoreType.DMA])(hs)


def _gather_small(sp):
    def body(s_ref, out_ref, send_sems, recv_sems, local_sem):
        x, y, c = _place()
        me = 4 * x + 2 * y + c
        peers = [(x ^ (m >> 2), y ^ ((m >> 1) & 1), c ^ (m & 1)) for m in range(1, 8)]
        mine = pltpu.make_async_copy(s_ref, out_ref.at[me], local_sem)
        mine.start()
        sends = [pltpu.make_async_remote_copy(src_ref=s_ref, dst_ref=out_ref.at[me], send_sem=send_sems.at[j],
                                              recv_sem=recv_sems.at[j], device_id=p, device_id_type=MESH)
                 for j, p in enumerate(peers)]
        for cp in sends:
            cp.start()
        for j, (px, py, pc) in enumerate(peers):
            pltpu.make_async_remote_copy(src_ref=s_ref, dst_ref=out_ref.at[4 * px + 2 * py + pc],
                                         send_sem=send_sems.at[j], recv_sem=recv_sems.at[j], device_id=(px, py, pc),
                                         device_id_type=MESH).wait_recv()
        for cp in sends:
            cp.wait_send()
        mine.wait()

    return pl.pallas_call(
        body, name="gather_small_grads", in_specs=[HBM_SPEC], out_specs=HBM_SPEC,
        out_shape=jax.ShapeDtypeStruct((8,) + sp.shape, sp.dtype),
        scratch_shapes=[pltpu.SemaphoreType.DMA((7,)), pltpu.SemaphoreType.DMA((7,)), pltpu.SemaphoreType.DMA])(sp)


def _shard_shapes(shards):
    return [(n, shards[n].shape) for n in BIG]


def _pack_rows(total):
    rows = -(-total // LANES)
    return -(-rows // 32) * 32


def _pack(arrs, dtype):
    flat = jnp.concatenate([a.reshape(-1).astype(dtype) for a in arrs])
    rows = _pack_rows(flat.shape[0])
    return jnp.pad(flat, (0, rows * LANES - flat.shape[0])).reshape(rows, LANES)


def _unpack(buf, shapes):
    flat = buf.reshape(-1)
    out, off = {}, 0
    for n, shp in shapes:
        size = shp[0] * shp[1]
        out[n] = flat[off:off + size].reshape(shp)
        off += size
    return out


def _adamw(name, wv, g, m, v, tb):
    c1 = 1.0 - ADAM_B1 ** ADAM_STEP
    c2 = 1.0 - ADAM_B2 ** ADAM_STEP

    def fn(wb, gb, mb, vb):
        m2 = ADAM_B1 * mb + (1.0 - ADAM_B1) * gb
        v2 = ADAM_B2 * vb + (1.0 - ADAM_B2) * (gb * gb)
        delta = -ADAM_LR * ((m2 / c1) / (jnp.sqrt(v2 / c2) + ADAM_EPS) + ADAM_WD * wb)
        return delta, m2, v2

    cols = wv.shape[1]
    return _rowwise(name, fn, [wv, g, m, v], [], [(cols, F32)] * 3, [], tb)


def _row_tile(rows, pref):
    if rows <= pref:
        return rows
    t = pref
    while t >= 8:
        if rows % t == 0 and t % 8 == 0:
            return t
        t -= 8
    return rows


def kernel(x, positions, ffn1_pre_g, ffn1_w_gate, ffn1_w_up, ffn1_w_down, ffn1_post_g, mix_pre_g, w_in, mla_q_norm_g, mla_w_uq, mla_kv_norm_g, mla_w_ukv, mla_out_g, gdn_conv_w, gdn_a_log, gdn_dt_bias, gdn_norm_g, w_out, mix_post_g, ffn2_pre_g, ffn2_w_gate, ffn2_w_up, ffn2_w_down, ffn2_post_g, loss_target, m_ffn1_pre_g, m_ffn1_w_gate, m_ffn1_w_up, m_ffn1_w_down, m_ffn1_post_g, m_mix_pre_g, m_w_in, m_mla_q_norm_g, m_mla_w_uq, m_mla_kv_norm_g, m_mla_w_ukv, m_mla_out_g, m_gdn_conv_w, m_gdn_a_log, m_gdn_dt_bias, m_gdn_norm_g, m_w_out, m_mix_post_g, m_ffn2_pre_g, m_ffn2_w_gate, m_ffn2_w_up, m_ffn2_w_down, m_ffn2_post_g, v_ffn1_pre_g, v_ffn1_w_gate, v_ffn1_w_up, v_ffn1_w_down, v_ffn1_post_g, v_mix_pre_g, v_w_in, v_mla_q_norm_g, v_mla_w_uq, v_mla_kv_norm_g, v_mla_w_ukv, v_mla_out_g, v_gdn_conv_w, v_gdn_a_log, v_gdn_dt_bias, v_gdn_norm_g, v_w_out, v_mix_post_g, v_ffn2_pre_g, v_ffn2_w_gate, v_ffn2_w_up, v_ffn2_w_down, v_ffn2_post_g):
    args = dict(locals())
    wsh = {n: args[n][0] for n in WEIGHTS}
    msh = {n: args["m_" + n][0] if args["m_" + n].ndim == 3 else args["m_" + n] for n in WEIGHTS}
    vsh = {n: args["v_" + n][0] if args["v_" + n].ndim == 3 else args["v_" + n] for n in WEIGHTS}
    for n in SMALL:
        wsh[n] = args[n]
    shapes = [(n, wsh[n].shape) for n in BIG]

    gathered = _gather_shards(_pack([wsh[n] for n in BIG], MM_DTYPE))
    full = {n: wsh[n] for n in SMALL}
    parts = [_unpack(gathered[q], shapes) for q in range(N_SHARD)]
    for n in BIG:
        full[n] = jnp.concatenate([parts[q][n] for q in range(N_SHARD)], axis=SHARD_AXIS[n])

    lsum, grad_x, g = _local_step(x[0], positions, loss_target[0], full)
    loss = lax.psum(0.5 * jnp.sum(lsum) / x.shape[-1], ("x", "y", "c"))

    c = lax.axis_index("c")
    quarters = []
    for q in range(N_SHARD):
        quarters.append(_pack([jnp.split(g[n], N_SHARD, axis=SHARD_AXIS[n])[q] for n in BIG], MM_DTYPE))
    rows = quarters[0].shape[0]
    half = rows // 2
    gp = jnp.stack(quarters).reshape(N_SHARD, 2, half, LANES)
    keep = lax.dynamic_index_in_dim(gp, c, axis=1, keepdims=False)
    give = lax.dynamic_index_in_dim(gp, 1 - c, axis=1, keepdims=False)
    got = _swap_with_sibling(give)
    tr = _row_tile(N_SHARD * half, 1024)
    (pair,) = _rowwise("add_pair", lambda a, b: (a.astype(F32) + b.astype(F32),),
                       [keep.reshape(N_SHARD * half, LANES), got.reshape(N_SHARD * half, LANES)], [],
                       [(LANES, MM_DTYPE)], [], tr)
    slabs = _scatter_to_chips(pair.reshape(N_SHARD, half, LANES)).reshape(N_SHARD * half, LANES)
    th = _row_tile(half, 1024)
    nb = half // th
    (hsum,) = _rowwise("add_chips",
                       lambda a, b, cc, dd: (((a.astype(F32) + b.astype(F32)) + cc.astype(F32)) + dd.astype(F32),),
                       [(slabs, LANES, 0, q * nb) for q in range(N_SHARD)], [], [(LANES, F32, half)], [], th)
    gfull = _share_halves(hsum).reshape(rows, LANES)
    gsh = _unpack(gfull, shapes)

    small_shapes = [(n, wsh[n].shape) for n in SMALL]
    pack_small = lambda d: jnp.concatenate(
        [_pad_lanes(d[n].astype(F32), LANES) for n in SMALL] + [jnp.zeros((SMALL_ROWS - len(SMALL), LANES), F32)], axis=0)
    slots = _gather_small(pack_small(g))

    c1 = 1.0 - ADAM_B1 ** ADAM_STEP
    c2 = 1.0 - ADAM_B2 ** ADAM_STEP

    def small_update(wb, mb, vb, s8):
        gs = s8[0:SMALL_ROWS]
        for d in range(1, 8):
            gs = gs + s8[d * SMALL_ROWS:(d + 1) * SMALL_ROWS]
        m2 = ADAM_B1 * mb + (1.0 - ADAM_B1) * gs
        v2 = ADAM_B2 * vb + (1.0 - ADAM_B2) * (gs * gs)
        delta = -ADAM_LR * ((m2 / c1) / (jnp.sqrt(v2 / c2) + ADAM_EPS) + ADAM_WD * wb)
        return gs, delta, m2, v2

    sg, sd, sm, sv_ = _rowwise("adamw_small", small_update,
                               [pack_small(wsh), pack_small(msh), pack_small(vsh)],
                               [slots.reshape(8 * SMALL_ROWS, LANES)], [(LANES, F32)] * 4, [], SMALL_ROWS)
    grads, deltas, new_m, new_v = {}, {}, {}, {}
    for i, (n, shp) in enumerate(small_shapes):
        grads[n], deltas[n] = sg[i:i + 1, :shp[1]], sd[i:i + 1, :shp[1]]
        new_m[n], new_v[n] = sm[i:i + 1, :shp[1]], sv_[i:i + 1, :shp[1]]
    for n in BIG:
        grads[n] = gsh[n]
        r = wsh[n].shape[0]
        deltas[n], new_m[n], new_v[n] = _adamw("adamw_" + n, wsh[n], gsh[n], msh[n], vsh[n], _row_tile(r, 256))

    def shaped(d, n):
        return d[n][None] if n in BIG else d[n]

    return (loss, grad_x[None], *[shaped(grads, n) for n in WEIGHTS], *[shaped(deltas, n) for n in WEIGHTS],
            *[shaped(new_m, n) for n in WEIGHTS], *[shaped(new_v, n) for n in WEIGHTS])
```

```python
import jax
import jax.numpy as jnp
from jax import lax
from jax.experimental import pallas as pl
from jax.experimental.pallas import tpu as pltpu

F32 = jnp.float32
BF16 = jnp.bfloat16
MM_DTYPE = BF16
HI = lax.Precision.HIGHEST
MESH = pl.DeviceIdType.MESH

D_MODEL = 1024
D_FF = 2816
N_HEADS = 8
MLA_Q_RANK = 256
MLA_KV_RANK = 128
MLA_NOPE = 64
MLA_ROPE = 32
MLA_V = 64
ROPE_THETA = 10000.0
GDN_DH = 64
GDN_W = N_HEADS * GDN_DH
GDN_CONV = 4
CHUNK = 64
HEAD_LANES = 128
MLA_PAD = N_HEADS * HEAD_LANES
EPS = 1e-6
N_SHARD = 4
LANES = 1024

PIN_QKV = 0
PIN_MLA = 1536
PIN_KPE = 1920
PIN_GATE = 2048
PIN_AB = 2560
PIN_W = 2688
CAT_W = MLA_PAD + GDN_W

ADAM_LR = 0.001
ADAM_B1 = 0.9
ADAM_B2 = 0.999
ADAM_EPS = 1e-08
ADAM_WD = 0.01
ADAM_STEP = 10

VMEM_LIMIT_V7X = 56 * 1024 * 1024

BIG = ["ffn1_w_gate", "ffn1_w_up", "ffn1_w_down", "w_in", "mla_w_uq", "mla_w_ukv", "gdn_conv_w", "w_out",
       "ffn2_w_gate", "ffn2_w_up", "ffn2_w_down"]
FFN_BIG = ["ffn1_w_gate", "ffn1_w_up", "ffn1_w_down", "ffn2_w_gate", "ffn2_w_up", "ffn2_w_down"]
MIX_BIG = ["w_in", "mla_w_uq", "mla_w_ukv", "gdn_conv_w", "w_out"]
SMALL = ["ffn1_pre_g", "ffn1_post_g", "mix_pre_g", "mla_q_norm_g", "mla_kv_norm_g", "mla_out_g", "gdn_a_log",
         "gdn_dt_bias", "gdn_norm_g", "mix_post_g", "ffn2_pre_g", "ffn2_post_g"]
WEIGHTS = ["ffn1_pre_g", "ffn1_w_gate", "ffn1_w_up", "ffn1_w_down", "ffn1_post_g", "mix_pre_g", "w_in",
           "mla_q_norm_g", "mla_w_uq", "mla_kv_norm_g", "mla_w_ukv", "mla_out_g", "gdn_conv_w", "gdn_a_log",
           "gdn_dt_bias", "gdn_norm_g", "w_out", "mix_post_g", "ffn2_pre_g", "ffn2_w_gate", "ffn2_w_up",
           "ffn2_w_down", "ffn2_post_g"]
SHARD_AXIS = {"ffn1_w_gate": 1, "ffn1_w_up": 1, "ffn1_w_down": 0, "w_in": 1, "mla_w_uq": 1, "mla_w_ukv": 1,
              "gdn_conv_w": 1, "w_out": 0, "ffn2_w_gate": 1, "ffn2_w_up": 1, "ffn2_w_down": 0}
SMALL_ROWS = 16


def _params(sem):
    return pltpu.CompilerParams(dimension_semantics=sem, vmem_limit_bytes=VMEM_LIMIT_V7X)


def _pick(dim, pref):
    if dim <= pref:
        return dim
    t = (pref // 128) * 128
    while t >= 128:
        if dim % t == 0:
            return t
        t -= 128
    return dim


ANY_SPEC = pl.BlockSpec(memory_space=pl.ANY)


def _rowwise(name, fn, row_ins, bc_ins, row_outs, acc_outs, tb, wide=None, carry=None):
    ents = []
    for e in row_ins:
        ents.append(e if isinstance(e, tuple) else (e, e.shape[1], 0, 0))
    over = [o[2] for o in row_outs if len(o) == 3]
    rows = over[0] if over else ents[0][0].shape[0]
    steps = rows // tb
    assert steps * tb == rows, (name, rows, tb)
    in_specs, args = [], []
    for a, w, j, r0 in ents:
        in_specs.append(pl.BlockSpec((tb, w), lambda i, j=j, r0=r0: (i + r0, j)))
        args.append(a)
    for b in bc_ins:
        in_specs.append(pl.BlockSpec(b.shape, lambda i: (0, 0)))
        args.append(b)
    n_in = len(args)
    aliases = {}
    if carry is not None:
        in_specs.append(ANY_SPEC)
        args.append(carry)
        aliases = {n_in: 0}
    out_shape = [jax.ShapeDtypeStruct((rows, o[0]), o[1]) for o in row_outs]
    out_specs = [pl.BlockSpec((tb, o[0]), lambda i: (i, 0)) for o in row_outs]
    if wide is not None:
        out_shape[0] = jax.ShapeDtypeStruct((rows, wide[0]), row_outs[0][1])
        out_specs[0] = pl.BlockSpec((tb, row_outs[0][0]), lambda i: (i, wide[1]))
    out_shape += [jax.ShapeDtypeStruct((r, c), F32) for r, c in acc_outs]
    out_specs += [pl.BlockSpec((r, c), lambda i: (0, 0)) for r, c in acc_outs]
    n_ro, n_acc, n_args = len(row_outs), len(acc_outs), len(args)

    def body(*refs):
        vals = fn(*[r[...] for r in refs[:n_in]])
        if not isinstance(vals, (tuple, list)):
            vals = (vals,)
        for r, v in zip(refs[n_args:n_args + n_ro], vals[:n_ro]):
            r[...] = v.astype(r.dtype)
        if n_acc:
            acc_refs = refs[n_args + n_ro:]

            @pl.when(pl.program_id(0) == 0)
            def _():
                for r in acc_refs:
                    r[...] = jnp.zeros(r.shape, r.dtype)

            for r, v in zip(acc_refs, vals[n_ro:]):
                r[...] += v

    outs = pl.pallas_call(body, name=name, grid=(steps,), in_specs=in_specs, out_specs=out_specs,
                          out_shape=out_shape, input_output_aliases=aliases,
                          compiler_params=_params(("arbitrary",)))(*args)
    return outs


def _mm(name, a, b, mode, out_dtype, tm=512, tn=512, tk=1024):
    if mode == "nn":
        (m, k), (k2, n) = a.shape, b.shape
    elif mode == "nt":
        (m, k), (n, k2) = a.shape, b.shape
    else:
        (k, m), (k2, n) = a.shape, b.shape
    assert k == k2, (name, a.shape, b.shape)
    tm, tn, tk = _pick(m, tm), _pick(n, tn), _pick(k, tk)
    nk = k // tk
    if mode == "nn":
        a_spec = pl.BlockSpec((tm, tk), lambda i, j, kk: (i, kk))
        b_spec = pl.BlockSpec((tk, tn), lambda i, j, kk: (kk, j))
        dims = (((1,), (0,)), ((), ()))
    elif mode == "nt":
        a_spec = pl.BlockSpec((tm, tk), lambda i, j, kk: (i, kk))
        b_spec = pl.BlockSpec((tn, tk), lambda i, j, kk: (j, kk))
        dims = (((1,), (1,)), ((), ()))
    else:
        a_spec = pl.BlockSpec((tk, tm), lambda i, j, kk: (kk, i))
        b_spec = pl.BlockSpec((tk, tn), lambda i, j, kk: (kk, j))
        dims = (((0,), (0,)), ((), ()))

    def body(a_ref, b_ref, o_ref, acc_ref):
        kk = pl.program_id(2)

        @pl.when(kk == 0)
        def _():
            acc_ref[...] = jnp.zeros(acc_ref.shape, F32)

        acc_ref[...] += lax.dot_general(a_ref[...].astype(MM_DTYPE), b_ref[...].astype(MM_DTYPE), dims,
                                        preferred_element_type=F32)

        @pl.when(kk == nk - 1)
        def _():
            o_ref[...] = acc_ref[...].astype(o_ref.dtype)

    return pl.pallas_call(
        body, name=name, grid=(m // tm, n // tn, nk), in_specs=[a_spec, b_spec],
        out_specs=pl.BlockSpec((tm, tn), lambda i, j, kk: (i, j)),
        out_shape=jax.ShapeDtypeStruct((m, n), out_dtype),
        scratch_shapes=[pltpu.VMEM((tm, tn), F32)],
        compiler_params=_params(("parallel", "parallel", "arbitrary")))(a, b)


def _rms_stats(x, n_real=None):
    n = x.shape[-1] if n_real is None else n_real
    return lax.rsqrt(jnp.sum(x * x, axis=-1, keepdims=True) / n + EPS)


def _rms_bwd(x, r, g, dz, n_real=None):
    n = x.shape[-1] if n_real is None else n_real
    xh = x * r
    dxh = dz * g
    dx = r * (dxh - xh * (jnp.sum(dxh * xh, axis=-1, keepdims=True) / n))
    return dx, jnp.sum(dz * xh, axis=0, keepdims=True)


def _sigmoid(x):
    return 1.0 / (1.0 + jnp.exp(-x))


def _roll(x, s, axis):
    return pltpu.roll(x, s, axis)


def _rope(x, c, s1, s2):
    return x * c + _roll(x, HEAD_LANES - MLA_ROPE // 2, 1) * s1 + _roll(x, MLA_ROPE // 2, 1) * s2


def _heads_apply(x, fn):
    return jnp.concatenate([fn(x[:, h * HEAD_LANES:(h + 1) * HEAD_LANES]) for h in range(N_HEADS)], axis=1)


def _ffn_fwd(tag, x, g_pre, wg, wu, wd, g_post, tm):
    t, d = x.shape
    ns, _, fs = wg.shape
    nt = t // tm
    row = pl.BlockSpec((tm, d), lambda i, q: (i, 0))
    vec = pl.BlockSpec((1, d), lambda i, q: (0, 0))
    act3 = pl.BlockSpec((1, tm, fs), lambda i, q: (q, i, 0))
    wcol = pl.BlockSpec((1, d, fs), lambda i, q: (q, 0, 0))
    wrow = pl.BlockSpec((1, fs, d), lambda i, q: (q, 0, 0))

    def gate_up(x_ref, g_ref, wg_ref, wu_ref, n_ref, a_ref, u_ref, s_ref, n_s):
        @pl.when(pl.program_id(1) == 0)
        def _():
            xb = x_ref[...]
            n_s[...] = (xb * _rms_stats(xb) * g_ref[...]).astype(MM_DTYPE)
            n_ref[...] = n_s[...]

        n = n_s[...]
        a = jnp.dot(n, wg_ref[0], preferred_element_type=F32)
        u = jnp.dot(n, wu_ref[0], preferred_element_type=F32)
        a_ref[0] = a.astype(a_ref.dtype)
        u_ref[0] = u.astype(u_ref.dtype)
        s_ref[0] = ((a * _sigmoid(a)) * u).astype(s_ref.dtype)

    n, a, u, s = pl.pallas_call(
        gate_up, name=tag + "_gate_up", grid=(nt, ns), in_specs=[row, vec, wcol, wcol],
        out_specs=[row, act3, act3, act3],
        out_shape=[jax.ShapeDtypeStruct((t, d), MM_DTYPE)] + [jax.ShapeDtypeStruct((ns, t, fs), MM_DTYPE)] * 3,
        scratch_shapes=[pltpu.VMEM((tm, d), MM_DTYPE)],
        compiler_params=_params(("parallel", "arbitrary")))(x, g_pre, wg, wu)

    def down(s_ref, wd_ref, x_ref, g_ref, h_ref, y_ref, acc):
        q = pl.program_id(1)

        @pl.when(q == 0)
        def _():
            acc[...] = jnp.zeros(acc.shape, F32)

        acc[...] += jnp.dot(s_ref[0], wd_ref[0], preferred_element_type=F32)

        @pl.when(q == ns - 1)
        def _():
            hb = acc[...]
            h_ref[...] = hb
            y_ref[...] = x_ref[...] + 0.5 * (hb * _rms_stats(hb) * g_ref[...])

    h, y = pl.pallas_call(
        down, name=tag + "_down", grid=(nt, ns), in_specs=[act3, wrow, row, vec], out_specs=[row, row],
        out_shape=[jax.ShapeDtypeStruct((t, d), F32)] * 2, scratch_shapes=[pltpu.VMEM((tm, d), F32)],
        compiler_params=_params(("parallel", "arbitrary")))(s, wd, x, g_post)
    return y, (x, n, a, u, s, h)


def _ffn_bwd(tag, dy, saved, g_pre, wg, wu, wd, g_post, tm, tk):
    x, n, a, u, s, h = saved
    t, d = x.shape
    ns, _, fs = wg.shape
    nt, nk = t // tm, t // tk
    row = pl.BlockSpec((tm, d), lambda i, q: (i, 0))
    vec = pl.BlockSpec((1, d), lambda i, q: (0, 0))
    act3 = pl.BlockSpec((1, tm, fs), lambda i, q: (q, i, 0))
    wcol = pl.BlockSpec((1, d, fs), lambda i, q: (q, 0, 0))
    wrow = pl.BlockSpec((1, fs, d), lambda i, q: (q, 0, 0))
    nt_dims = (((1,), (1,)), ((), ()))
    tn_dims = (((0,), (0,)), ((), ()))

    def down_b(h_ref, dy_ref, g_ref, wd_ref, a_ref, u_ref, dh_ref, da_ref, du_ref, dg_ref, dh_s):
        i, q = pl.program_id(0), pl.program_id(1)

        @pl.when((i == 0) & (q == 0))
        def _():
            dg_ref[...] = jnp.zeros(dg_ref.shape, F32)

        @pl.when(q == 0)
        def _():
            hb = h_ref[...]
            dh, dg = _rms_bwd(hb, _rms_stats(hb), g_ref[...], 0.5 * dy_ref[...])
            dh_s[...] = dh.astype(MM_DTYPE)
            dh_ref[...] = dh_s[...]
            dg_ref[...] += dg

        ds = lax.dot_general(dh_s[...], wd_ref[0], nt_dims, preferred_element_type=F32)
        ab, ub = a_ref[0].astype(F32), u_ref[0].astype(F32)
        sg = _sigmoid(ab)
        da_ref[0] = (ds * ub * (sg * (1.0 + ab * (1.0 - sg)))).astype(da_ref.dtype)
        du_ref[0] = (ds * (ab * sg)).astype(du_ref.dtype)

    dh, da, du, dg_post = pl.pallas_call(
        down_b, name=tag + "_down_b", grid=(nt, ns), in_specs=[row, row, vec, wrow, act3, act3],
        out_specs=[row, act3, act3, vec],
        out_shape=[jax.ShapeDtypeStruct((t, d), MM_DTYPE)] + [jax.ShapeDtypeStruct((ns, t, fs), MM_DTYPE)] * 2
        + [jax.ShapeDtypeStruct((1, d), F32)],
        scratch_shapes=[pltpu.VMEM((tm, d), MM_DTYPE)],
        compiler_params=_params(("arbitrary", "arbitrary")))(h, dy, g_post, wd, a, u)

    def down_w(s_ref, dh_ref, dw_ref, acc):
        kk = pl.program_id(1)

        @pl.when(kk == 0)
        def _():
            acc[...] = jnp.zeros(acc.shape, F32)

        acc[...] += lax.dot_general(s_ref[0], dh_ref[...], tn_dims, preferred_element_type=F32)

        @pl.when(kk == nk - 1)
        def _():
            dw_ref[0] = acc[...].astype(dw_ref.dtype)

    dwd = pl.pallas_call(
        down_w, name=tag + "_down_w", grid=(ns, nk),
        in_specs=[pl.BlockSpec((1, tk, fs), lambda q, kk: (q, kk, 0)), pl.BlockSpec((tk, d), lambda q, kk: (kk, 0))],
        out_specs=pl.BlockSpec((1, fs, d), lambda q, kk: (q, 0, 0)),
        out_shape=jax.ShapeDtypeStruct((ns, fs, d), MM_DTYPE), scratch_shapes=[pltpu.VMEM((fs, d), F32)],
        compiler_params=_params(("parallel", "arbitrary")))(s, dh)

    def gate_up_b(da_ref, du_ref, wg_ref, wu_ref, x_ref, dy_ref, g_ref, dx_ref, dg_ref, acc):
        i, q = pl.program_id(0), pl.program_id(1)

        @pl.when((i == 0) & (q == 0))
        def _():
            dg_ref[...] = jnp.zeros(dg_ref.shape, F32)

        @pl.when(q == 0)
        def _():
            acc[...] = jnp.zeros(acc.shape, F32)

        acc[...] += (lax.dot_general(da_ref[0], wg_ref[0], nt_dims, preferred_element_type=F32)
                     + lax.dot_general(du_ref[0], wu_ref[0], nt_dims, preferred_element_type=F32))

        @pl.when(q == ns - 1)
        def _():
            xb = x_ref[...]
            dx, dg = _rms_bwd(xb, _rms_stats(xb), g_ref[...], acc[...])
            dx_ref[...] = dy_ref[...] + dx
            dg_ref[...] += dg

    dx, dg_pre = pl.pallas_call(
        gate_up_b, name=tag + "_gate_up_b", grid=(nt, ns), in_specs=[act3, act3, wcol, wcol, row, row, vec],
        out_specs=[row, vec], out_shape=[jax.ShapeDtypeStruct((t, d), F32), jax.ShapeDtypeStruct((1, d), F32)],
        scratch_shapes=[pltpu.VMEM((tm, d), F32)],
        compiler_params=_params(("arbitrary", "arbitrary")))(da, du, wg, wu, x, dy, g_pre)

    def gate_up_w(n_ref, da_ref, du_ref, dwg_ref, dwu_ref, acc_g, acc_u):
        kk = pl.program_id(1)

        @pl.when(kk == 0)
        def _():
            acc_g[...] = jnp.zeros(acc_g.shape, F32)
            acc_u[...] = jnp.zeros(acc_u.shape, F32)

        nb = n_ref[...]
        acc_g[...] += lax.dot_general(nb, da_ref[0], tn_dims, preferred_element_type=F32)
        acc_u[...] += lax.dot_general(nb, du_ref[0], tn_dims, preferred_element_type=F32)

        @pl.when(kk == nk - 1)
        def _():
            dwg_ref[0] = acc_g[...].astype(dwg_ref.dtype)
            dwu_ref[0] = acc_u[...].astype(dwu_ref.dtype)

    k3 = pl.BlockSpec((1, tk, fs), lambda q, kk: (q, kk, 0))
    wout = pl.BlockSpec((1, d, fs), lambda q, kk: (q, 0, 0))
    dwg, dwu = pl.pallas_call(
        gate_up_w, name=tag + "_gate_up_w", grid=(ns, nk),
        in_specs=[pl.BlockSpec((tk, d), lambda q, kk: (kk, 0)), k3, k3], out_specs=[wout, wout],
        out_shape=[jax.ShapeDtypeStruct((ns, d, fs), MM_DTYPE)] * 2,
        scratch_shapes=[pltpu.VMEM((d, fs), F32)] * 2,
        compiler_params=_params(("parallel", "arbitrary")))(n, da, du)
    return dx, dg_pre, dwg, dwu, dwd, dg_post


NEG = -1e30


def _attn_scale():
    return (MLA_NOPE + MLA_ROPE) ** -0.5


def _causal_pairs(nq, by_key):
    if by_key:
        pairs = [(qi, ki) for ki in range(nq) for qi in range(ki, nq)]
    else:
        pairs = [(qi, ki) for qi in range(nq) for ki in range(qi + 1)]
    return jnp.asarray([p[0] for p in pairs], jnp.int32), jnp.asarray([p[1] for p in pairs], jnp.int32)


def _below_diagonal(shape):
    return lax.broadcasted_iota(jnp.int32, shape, 1) <= lax.broadcasted_iota(jnp.int32, shape, 0)


def _attn_call(name, body, tables, args, in_kinds, out_kinds, scratch, t, tq):
    qmap = lambda h, p, qt, kt: (qt[p], h)
    kmap = lambda h, p, qt, kt: (kt[p], h)
    spec = lambda kind: pl.BlockSpec((tq, HEAD_LANES), qmap if kind == "q" else kmap)
    grid_spec = pltpu.PrefetchScalarGridSpec(
        num_scalar_prefetch=2, grid=(N_HEADS, tables[0].shape[0]), in_specs=[spec(kd) for kd in in_kinds],
        out_specs=[spec(kd) for kd in out_kinds], scratch_shapes=scratch)
    return pl.pallas_call(body, name=name, grid_spec=grid_spec,
                          out_shape=[jax.ShapeDtypeStruct((t, MLA_PAD), F32) for _ in out_kinds],
                          compiler_params=_params(("parallel", "arbitrary")))(*tables, *args)


def _attn_fwd(q, k, v, tq):
    t = q.shape[0]
    nq = t // tq

    def body(qt, kt, q_ref, k_ref, v_ref, o_ref, lse_ref, m_s, l_s, acc_s):
        p_id = pl.program_id(1)
        qi, ki = qt[p_id], kt[p_id]

        @pl.when(ki == 0)
        def _():
            m_s[...] = jnp.full(m_s.shape, NEG, F32)
            l_s[...] = jnp.zeros(l_s.shape, F32)
            acc_s[...] = jnp.zeros(acc_s.shape, F32)

        def scores():
            return lax.dot_general(q_ref[...], k_ref[...], (((1,), (1,)), ((), ())), preferred_element_type=F32)

        def update(s):
            m_old = m_s[...]
            m_new = jnp.maximum(m_old, jnp.max(s, axis=1, keepdims=True))
            alpha = jnp.exp(m_old - m_new)
            p = jnp.exp(s - m_new[:, :1])
            l_s[...] = l_s[...] * alpha + jnp.sum(p, axis=1, keepdims=True)
            acc_s[...] = acc_s[...] * alpha + jnp.dot(p.astype(MM_DTYPE), v_ref[...], preferred_element_type=F32)
            m_s[...] = m_new

        @pl.when(ki < qi)
        def _():
            update(scores())

        @pl.when(ki == qi)
        def _():
            s = scores()
            update(jnp.where(_below_diagonal(s.shape), s, NEG))
            o_ref[...] = acc_s[...] / l_s[...]
            lse_ref[...] = m_s[...] + jnp.log(l_s[...])

    return _attn_call("mla_attn_fwd", body, _causal_pairs(nq, False), (q, k, v), "qkk", "qq",
                      [pltpu.VMEM((tq, HEAD_LANES), F32)] * 3, t, tq)


def _attn_probs(q, k, lse, diagonal):
    s = lax.dot_general(q, k, (((1,), (1,)), ((), ())), preferred_element_type=F32)
    p = jnp.exp(s - lse[:, :1])
    return jnp.where(_below_diagonal(s.shape), p, 0.0) if diagonal else p


def _attn_bwd_dq(q, k, v, do, lse, delta, tq):
    t = q.shape[0]
    nq = t // tq

    def body(qt, kt, q_ref, k_ref, v_ref, do_ref, lse_ref, dl_ref, dq_ref, acc_s):
        p_id = pl.program_id(1)
        qi, ki = qt[p_id], kt[p_id]

        @pl.when(ki == 0)
        def _():
            acc_s[...] = jnp.zeros(acc_s.shape, F32)

        def step(diagonal):
            p = _attn_probs(q_ref[...], k_ref[...], lse_ref[...], diagonal)
            dp = lax.dot_general(do_ref[...], v_ref[...], (((1,), (1,)), ((), ())), preferred_element_type=F32)
            ds = p * (dp - dl_ref[...][:, :1])
            acc_s[...] += jnp.dot(ds.astype(MM_DTYPE), k_ref[...], preferred_element_type=F32)

        @pl.when(ki < qi)
        def _():
            step(False)

        @pl.when(ki == qi)
        def _():
            step(True)
            dq_ref[...] = acc_s[...]

    return _attn_call("mla_attn_bwd_dq", body, _causal_pairs(nq, False), (q, k, v, do, lse, delta), "qkkqqq", "q",
                      [pltpu.VMEM((tq, HEAD_LANES), F32)], t, tq)[0]


def _attn_bwd_dkv(q, k, v, do, lse, delta, tq):
    t = q.shape[0]
    nq = t // tq

    def body(qt, kt, q_ref, k_ref, v_ref, do_ref, lse_ref, dl_ref, dk_ref, dv_ref, dk_s, dv_s):
        p_id = pl.program_id(1)
        qi, ki = qt[p_id], kt[p_id]

        def step(diagonal):
            p = _attn_probs(q_ref[...], k_ref[...], lse_ref[...], diagonal)
            dv_s[...] += lax.dot_general(p.astype(MM_DTYPE), do_ref[...], (((0,), (0,)), ((), ())),
                                         preferred_element_type=F32)
            dp = lax.dot_general(do_ref[...], v_ref[...], (((1,), (1,)), ((), ())), preferred_element_type=F32)
            ds = p * (dp - dl_ref[...][:, :1])
            dk_s[...] += lax.dot_general(ds.astype(MM_DTYPE), q_ref[...], (((0,), (0,)), ((), ())),
                                         preferred_element_type=F32)

        @pl.when(qi == ki)
        def _():
            dk_s[...] = jnp.zeros(dk_s.shape, F32)
            dv_s[...] = jnp.zeros(dv_s.shape, F32)
            step(True)

        @pl.when(qi > ki)
        def _():
            step(False)

        @pl.when(qi == nq - 1)
        def _():
            dk_ref[...] = dk_s[...]
            dv_ref[...] = dv_s[...]

    return _attn_call("mla_attn_bwd_dkv", body, _causal_pairs(nq, True), (q, k, v, do, lse, delta), "qkkqqq", "kk",
                      [pltpu.VMEM((tq, HEAD_LANES), F32)] * 2, t, tq)


def _dotf(a, b, dims=(((1,), (0,)), ((), ()))):
    return lax.dot_general(a, b, dims, preferred_element_type=F32, precision=HI)


def _dot1(a, b, dims=(((1,), (0,)), ((), ()))):
    return lax.dot_general(a.astype(MM_DTYPE), b.astype(MM_DTYPE), dims, preferred_element_type=F32)


def _dot3(a, b, dims=(((1,), (0,)), ((), ()))):
    return lax.dot_general(a, b, dims, preferred_element_type=F32, precision=lax.Precision.HIGH)


NN3 = (((2,), (1,)), ((0,), (0,)))
NT3 = (((2,), (2,)), ((0,), (0,)))
TN3 = (((1,), (1,)), ((0,), (0,)))


def _tri_masks(nh):
    shape = (nh, CHUNK, CHUNK)
    return lax.broadcasted_iota(jnp.int32, shape, 1), lax.broadcasted_iota(jnp.int32, shape, 2)


def _gdn_chunk_common(k, gb, bb, row, col, dot=_dot1):
    tril = row >= col
    ltri = tril.astype(F32)
    umat = (row <= col).astype(F32)
    gcc = _dotf(ltri, gb, NN3)
    gcr = _dotf(gb, umat, TN3)
    dm = jnp.exp(jnp.where(tril, gcc - gcr, NEG))
    kb = k * bb
    lm = jnp.where(row > col, dot(kb, k, NT3) * dm, 0.0)
    return gcc, dm, kb, lm, umat


def _unit_lower_inverse(lm, eye):
    t = eye - lm
    p = lm
    for _ in range(CHUNK.bit_length() - 2):
        p = _dotf(p, p, NN3)
        t = t + _dotf(t, p, NN3)
    return t


def _gdn_fwd(q, k, v, gb, bb):
    nh, t, dh = q.shape
    nchunk = t // CHUNK

    def body(q_ref, k_ref, v_ref, g_ref, b_ref, o_ref, sall_ref, tall_ref, s_s):
        @pl.when(pl.program_id(0) == 0)
        def _():
            s_s[...] = jnp.zeros(s_s.shape, F32)

        row, col = _tri_masks(nh)
        qh, kh, vh, bbh = q_ref[...], k_ref[...], v_ref[...], b_ref[...]
        gcc, dm, kb, lm, _ = _gdn_chunk_common(kh, g_ref[...], bbh, row, col)
        eg = jnp.exp(gcc)
        glr = gcc[:, CHUNK - 1:CHUNK, :]
        th = _unit_lower_inverse(lm, (row == col).astype(F32))
        w = _dot1(th, kb * eg, NN3)
        u = _dot1(th, vh * bbh, NN3)
        at = jnp.where(row >= col, _dot1(qh, kh, NT3) * dm, 0.0)
        sh = s_s[...]
        vn = u - _dot1(w, sh, NN3)
        o_ref[...] = _dot1(qh * eg, sh, NN3) + _dot1(at, vn, NN3)
        kd = kh * jnp.exp(glr - gcc)
        sall_ref[:, 0] = sh
        tall_ref[...] = th
        s_s[...] = sh * jnp.exp(glr) + _dot1(kd, vn, TN3)

    blk = pl.BlockSpec((nh, CHUNK, dh), lambda n: (0, n, 0))
    return pl.pallas_call(
        body, name="gdn_fwd", grid=(nchunk,), in_specs=[blk] * 5,
        out_specs=[blk, pl.BlockSpec((nh, 1, dh, dh), lambda n: (0, n, 0, 0)), blk],
        out_shape=[jax.ShapeDtypeStruct((nh, t, dh), F32), jax.ShapeDtypeStruct((nh, nchunk, dh, dh), F32),
                   jax.ShapeDtypeStruct((nh, t, CHUNK), F32)],
        scratch_shapes=[pltpu.VMEM((nh, dh, dh), F32)],
        compiler_params=_params(("arbitrary",)))(q, k, v, gb, bb)


def _gdn_bwd(q, k, v, gb, bb, sall, tall, do):
    nh, t, dh = q.shape
    nchunk = t // CHUNK

    def body(q_ref, k_ref, v_ref, g_ref, b_ref, sall_ref, tall_ref, do_ref,
             dq_ref, dk_ref, dv_ref, dg_ref, db_ref, ds_s):
        @pl.when(pl.program_id(0) == 0)
        def _():
            ds_s[...] = jnp.zeros(ds_s.shape, F32)

        row, col = _tri_masks(nh)
        tril, stril = row >= col, row > col
        rsum = lambda x: jnp.sum(x, axis=2, keepdims=True)
        qh, kh, vh, gbh, bbh = q_ref[...], k_ref[...], v_ref[...], g_ref[...], b_ref[...]
        sh, th, doh, dsp = sall_ref[:, 0], tall_ref[...], do_ref[...], ds_s[...]
        gcc, dm, kb, lm, umat = _gdn_chunk_common(kh, gbh, bbh, row, col, _dot3)
        eg = jnp.exp(gcc)
        glr = gcc[:, CHUNK - 1:CHUNK, :]
        glv = jnp.exp(glr)
        egl = jnp.exp(glr - gcc)
        rw, ru = kb * eg, vh * bbh
        w, u = _dot3(th, rw, NN3), _dot3(th, ru, NN3)
        at = jnp.where(tril, _dot3(qh, kh, NT3) * dm, 0.0)
        qd, kd = qh * eg, kh * egl
        vn = u - _dot3(w, sh, NN3)
        dgl = jnp.sum(rsum(dsp * sh), axis=1, keepdims=True)
        dkd = _dot3(vn, dsp, NT3)
        dvn = _dot3(kd, dsp, NN3)
        dqd = _dot3(doh, sh, NT3)
        dat = jnp.where(tril, _dot3(doh, vn, NT3), 0.0)
        dvn = dvn + _dot3(at, doh, TN3)
        dw = -_dot3(dvn, sh, NT3)
        ds_s[...] = dsp * glv + _dot3(qd, doh, TN3) - _dot3(w, dvn, TN3)
        dpa = dat * dm
        dq_ref[...] = _dot3(dpa, kh, NN3) + dqd * eg
        dk = _dot3(dpa, qh, TN3) + dkd * egl
        t6 = rsum(dkd * kd)
        dgam = rsum(dqd * qd) - t6
        dgam_last = jnp.sum(t6, axis=1, keepdims=True) + dgl * glv
        drw = _dot3(th, dw, TN3)
        dru = _dot3(th, dvn, TN3)
        dl = -jnp.where(stril, _dot3(drw, w, NT3) + _dot3(dru, u, NT3), 0.0)
        dgam = dgam + rsum(drw * rw)
        dv_ref[...] = dru * bbh
        dp2 = dl * dm
        dkb = drw * eg + _dot3(dp2, kh, NN3)
        dk_ref[...] = dk + _dot3(dp2, kb, TN3) + dkb * bbh
        db_ref[...] = rsum(dru * vh) + rsum(dkb * kh) + jnp.zeros((nh, CHUNK, dh), F32)
        e = dat * at + dl * lm
        dgam_b = dgam + rsum(e) - _dotf(e, jnp.ones((nh, CHUNK, CHUNK), F32), TN3)
        dgam_b = dgam_b + jnp.where(row == CHUNK - 1, dgam_last, 0.0)
        dg_ref[...] = _dotf(umat, dgam_b, NN3)

    rev = lambda n: (0, nchunk - 1 - n, 0)
    blk = pl.BlockSpec((nh, CHUNK, dh), rev)
    sblk = pl.BlockSpec((nh, 1, dh, dh), lambda n: (0, nchunk - 1 - n, 0, 0))
    out = jax.ShapeDtypeStruct((nh, t, dh), F32)
    return pl.pallas_call(
        body, name="gdn_bwd", grid=(nchunk,), in_specs=[blk] * 5 + [sblk, blk, blk], out_specs=[blk] * 5,
        out_shape=[out] * 5, scratch_shapes=[pltpu.VMEM((nh, dh, dh), F32)],
        compiler_params=_params(("arbitrary",)))(q, k, v, gb, bb, sall, tall, do)


def _group_ones():
    r = lax.broadcasted_iota(jnp.int32, (GDN_W, GDN_W), 0) // GDN_DH
    c = lax.broadcasted_iota(jnp.int32, (GDN_W, GDN_W), 1) // GDN_DH
    return (r == c).astype(F32)


def _conv_taps(x, xprev, w, has_prev):
    row = lax.broadcasted_iota(jnp.int32, x.shape, 0)
    out = x * w[GDN_CONV - 1:GDN_CONV, :]
    for s in range(1, GDN_CONV):
        sh = jnp.where(row >= s, _roll(x, s, 0), _roll(xprev, s, 0) * has_prev)
        out = out + sh * w[GDN_CONV - 1 - s:GDN_CONV - s, :]
    return out


def _head_cols(x, h):
    return x[:, h * GDN_DH:(h + 1) * GDN_DH]


def _heads_spec(tb):
    return pl.BlockSpec((N_HEADS, tb, GDN_DH), lambda i: (0, i, 0))


def _mixer_fwd(x, positions, w, tb):
    t, d = x.shape
    tables = _rope_tables(positions)

    def pre(xb, g):
        return (xb * _rms_stats(xb) * g,)

    (hn,) = _rowwise("mix_pre", pre, [x], [w["mix_pre_g"]], [(d, BF16)], [], tb)
    proj = _mm("mix_in", hn, w["w_in_pad"], "nn", F32)

    def mla_pre(p0, gq, gkv):
        cq, ckv = p0[:, :MLA_Q_RANK], p0[:, MLA_Q_RANK:MLA_Q_RANK + MLA_KV_RANK]
        return cq * _rms_stats(cq) * gq, ckv * _rms_stats(ckv) * gkv

    nq, nkv = _rowwise("mla_pre", mla_pre, [(proj, 512, PIN_MLA // 512, 0)],
                       [w["mla_q_norm_g"], w["mla_kv_norm_g"]], [(MLA_Q_RANK, BF16), (MLA_KV_RANK, BF16)], [], tb)
    qraw = _mm("mla_uq", nq, w["w_uq_pad"], "nn", F32)
    kv = _mm("mla_ukv", nkv, w["w_kv_pad"], "nn", F32)

    def rope_f(qr, kn, vv, kpe, c, s1, s2):
        qo = _heads_apply(qr, lambda xh: _rope(xh, c, s1, s2)) * _attn_scale()
        kp = _rope(kpe, c, s1, s2)
        return qo, kn + jnp.tile(kp, (1, N_HEADS)), vv

    q, k, v = _rowwise("mla_rope", rope_f,
                       [qraw, (kv, MLA_PAD, 0, 0), (kv, MLA_PAD, 1, 0), (proj, HEAD_LANES, PIN_KPE // HEAD_LANES, 0),
                        tables[0], tables[1], tables[2]], [],
                       [(MLA_PAD, BF16)] * 3, [], tb // 2)
    tq = min(512, t)
    o, lse = _attn_fwd(q, k, v, tq)

    def mla_post(ob, g):
        return (ob * _rms_stats(ob, N_HEADS * MLA_V) * g,)

    (cat,) = _rowwise("mla_post", mla_post, [o], [w["mla_out_g_pad"]], [(MLA_PAD, BF16)], [], tb, wide=(CAT_W, 0))

    gones = _group_ones()
    steps = t // tb

    def gdn_pre(xq, xk, xv, pq, pk, pv, cw, go, has_prev):
        outs = []
        for j, (xc, xp) in enumerate(((xq, pq), (xk, pk), (xv, pv))):
            c = _conv_taps(xc, xp, cw[:, j * GDN_W:(j + 1) * GDN_W], has_prev)
            a = c * _sigmoid(c)
            if j < 2:
                rn = lax.rsqrt(_dotf(a * a, go) + EPS)
                a = a * rn
                if j == 0:
                    a = a * (GDN_DH ** -0.5)
            outs.append(a)
        return tuple(outs)

    qh, kh, vh = _gdn_pre_call("gdn_pre", gdn_pre, proj, w["conv_w"], gones, tb, steps)
    heads_shape = jax.ShapeDtypeStruct((N_HEADS, t, GDN_DH), F32)
    lanes_shape = jax.ShapeDtypeStruct((t, HEAD_LANES), F32)
    lanes_spec = pl.BlockSpec((tb, HEAD_LANES), lambda i: (i, 0))
    vec_spec = lambda n: pl.BlockSpec((1, n), lambda i: (0, 0))

    def gate_f(ab_ref, al_ref, dt_ref, g_ref, b_ref, gh_ref, bh_ref):
        g, b = _gb_fwd(ab_ref[...], al_ref[...], dt_ref[...])
        g_ref[...] = g
        b_ref[...] = b
        for h in range(N_HEADS):
            gh_ref[h] = jnp.broadcast_to(g[:, h:h + 1], (tb, GDN_DH))
            bh_ref[h] = jnp.broadcast_to(b[:, N_HEADS + h:N_HEADS + h + 1], (tb, GDN_DH))

    g128, b128, gbh, bbh = pl.pallas_call(
        gate_f, name="gdn_gate_f", grid=(steps,),
        in_specs=[pl.BlockSpec((tb, HEAD_LANES), lambda i: (i, PIN_AB // HEAD_LANES)), vec_spec(HEAD_LANES),
                  vec_spec(HEAD_LANES)],
        out_specs=[lanes_spec, lanes_spec, _heads_spec(tb), _heads_spec(tb)],
        out_shape=[lanes_shape, lanes_shape, heads_shape, heads_shape],
        compiler_params=_params(("arbitrary",)))(proj, w["a_log_pad"], w["dt_bias_pad"])
    oh, sall, tall = _gdn_fwd(qh, kh, vh, gbh, bbh)

    def gdn_post(o_ref, gt_ref, g_ref, cat_in, cat_ref):
        gt, g = gt_ref[...], g_ref[...]
        outs = []
        for h in range(N_HEADS):
            ob, gth = o_ref[h], _head_cols(gt, h)
            outs.append(ob * _rms_stats(ob) * g * (gth * _sigmoid(gth)))
        cat_ref[...] = jnp.concatenate(outs, axis=1).astype(cat_ref.dtype)

    gate_spec = pl.BlockSpec((tb, GDN_W), lambda i: (i, PIN_GATE // GDN_W))
    cat = pl.pallas_call(
        gdn_post, name="gdn_post", grid=(steps,),
        in_specs=[_heads_spec(tb), gate_spec, vec_spec(GDN_DH), ANY_SPEC],
        out_specs=pl.BlockSpec((tb, GDN_W), lambda i: (i, MLA_PAD // GDN_W)),
        out_shape=jax.ShapeDtypeStruct((t, CAT_W), BF16), input_output_aliases={3: 0},
        compiler_params=_params(("arbitrary",)))(oh, proj, w["gdn_norm_g"], cat)
    mixed = _mm("mix_out", cat, w["w_out_pad"], "nn", F32)

    def post(xb, hb, g):
        return (xb + hb * _rms_stats(hb) * g,)

    (y,) = _rowwise("mix_post", post, [x, mixed], [w["mix_post_g"]], [(d, F32)], [], tb)
    saved = dict(x=x, hn=hn, proj=proj, nq=nq, nkv=nkv, q=q, k=k, v=v, o=o, lse=lse, qh=qh, kh=kh, vh=vh,
                 gbh=gbh, bbh=bbh, oh=oh, sall=sall, tall=tall, cat=cat, mixed=mixed,
                 tables=tables, g128=g128, b128=b128)
    return y, saved


def _qkv_specs(tb):
    base = PIN_QKV // GDN_W
    cur = [pl.BlockSpec((tb, GDN_W), lambda i, j=j: (i, base + j)) for j in range(3)]
    prev = [pl.BlockSpec((tb, GDN_W), lambda i, j=j: (jnp.maximum(i - 1, 0), base + j)) for j in range(3)]
    return cur + prev


def _gdn_pre_call(name, fn, proj, conv_w, gones, tb, steps):
    t = proj.shape[0]

    def body(xq, xk, xv, pq, pk, pv, cw, go, oq, ok, ov):
        has_prev = jnp.where(pl.program_id(0) == 0, 0.0, 1.0)
        outs = fn(xq[...], xk[...], xv[...], pq[...], pk[...], pv[...], cw[...], go[...], has_prev)
        for r, val in zip((oq, ok, ov), outs):
            for h in range(N_HEADS):
                r[h] = _head_cols(val, h)

    return pl.pallas_call(
        body, name=name, grid=(steps,),
        in_specs=_qkv_specs(tb) + [pl.BlockSpec(conv_w.shape, lambda i: (0, 0)),
                                   pl.BlockSpec(gones.shape, lambda i: (0, 0))],
        out_specs=[_heads_spec(tb)] * 3,
        out_shape=[jax.ShapeDtypeStruct((N_HEADS, t, GDN_DH), F32)] * 3,
        compiler_params=_params(("arbitrary",)))(proj, proj, proj, proj, proj, proj, conv_w, gones)


def _softplus(x):
    return jnp.maximum(x, 0.0) + jnp.log1p(jnp.exp(-jnp.abs(x)))


def _gb_fwd(ab, a_log, dt_bias):
    g = -jnp.exp(a_log) * _softplus(ab + dt_bias)
    return g, _sigmoid(ab)


def _rope_tables(positions):
    half = MLA_ROPE // 2
    freqs = ROPE_THETA ** (-jnp.arange(half, dtype=F32) / half)
    ang = positions.reshape(-1).astype(F32)[:, None] * freqs
    cos, sin = jnp.cos(ang), jnp.sin(ang)
    t = ang.shape[0]
    one = jnp.ones((t, MLA_NOPE), F32)
    z16, z32, z64 = jnp.zeros((t, half), F32), jnp.zeros((t, MLA_ROPE), F32), jnp.zeros((t, MLA_NOPE), F32)
    c = jnp.concatenate([one, cos, cos, jnp.ones((t, MLA_ROPE), F32)], axis=1)
    s1 = jnp.concatenate([z64, -sin, z16, z32], axis=1)
    s2 = jnp.concatenate([z64, z16, sin, z32], axis=1)
    return c, s1, s2


def _mixer_bwd(dy, sv, w, tb):
    x, proj = sv["x"], sv["proj"]
    t, d = x.shape
    c, s1, s2 = sv["tables"]
    grads = {}

    def post_b(hb, dyb, g):
        return _rms_bwd(hb, _rms_stats(hb), g, dyb)

    dmixed, grads["mix_post_g"] = _rowwise("mix_post_b", post_b, [sv["mixed"], dy], [w["mix_post_g"]],
                                           [(d, BF16)], [(1, d)], tb)
    dcat = _mm("mix_out_bx", dmixed, w["w_out_pad"], "nt", F32)
    grads["w_out_pad"] = _mm("mix_out_bw", sv["cat"], dmixed, "tn", F32)
    steps = t // tb
    vec_spec = lambda n: pl.BlockSpec((1, n), lambda i: (0, 0))

    def gdn_post_b(o_ref, gt_ref, do_ref, g_ref, dproj_ref, doh_ref, dg_ref):
        @pl.when(pl.program_id(0) == 0)
        def _():
            dg_ref[...] = jnp.zeros(dg_ref.shape, F32)

        gt, dob, g = gt_ref[...], do_ref[...], g_ref[...]
        dgates = []
        for h in range(N_HEADS):
            ob, gth, dobh = o_ref[h], _head_cols(gt, h), _head_cols(dob, h)
            sg = _sigmoid(gth)
            r = _rms_stats(ob)
            dxo, dg = _rms_bwd(ob, r, g, dobh * (gth * sg))
            doh_ref[h] = dxo
            dg_ref[...] += dg
            dgates.append(dobh * (ob * r * g) * (sg * (1.0 + gth * (1.0 - sg))))
        dproj_ref[...] = jnp.concatenate(dgates, axis=1).astype(dproj_ref.dtype)

    dproj, doh, grads["gdn_norm_g"] = pl.pallas_call(
        gdn_post_b, name="gdn_post_b", grid=(steps,),
        in_specs=[_heads_spec(tb), pl.BlockSpec((tb, GDN_W), lambda i: (i, PIN_GATE // GDN_W)),
                  pl.BlockSpec((tb, GDN_W), lambda i: (i, MLA_PAD // GDN_W)), vec_spec(GDN_DH)],
        out_specs=[pl.BlockSpec((tb, GDN_W), lambda i: (i, PIN_GATE // GDN_W)), _heads_spec(tb), vec_spec(GDN_DH)],
        out_shape=[jax.ShapeDtypeStruct((t, PIN_W), BF16), jax.ShapeDtypeStruct((N_HEADS, t, GDN_DH), F32),
                   jax.ShapeDtypeStruct((1, GDN_DH), F32)],
        compiler_params=_params(("arbitrary",)))(sv["oh"], proj, dcat, w["gdn_norm_g"])

    def mla_post_b(ob, dmo, g):
        do, dg = _rms_bwd(ob, _rms_stats(ob, N_HEADS * MLA_V), g, dmo, N_HEADS * MLA_V)
        prod = do * ob
        delta = _heads_apply(prod, lambda ph: jnp.sum(ph, axis=1, keepdims=True) + jnp.zeros_like(ph))
        return do, delta, dg

    do, delta, grads["mla_out_g_pad"] = _rowwise(
        "mla_post_b", mla_post_b, [sv["o"], (dcat, MLA_PAD, 0, 0)], [w["mla_out_g_pad"]],
        [(MLA_PAD, BF16), (MLA_PAD, F32)], [(1, MLA_PAD)], tb // 2)
    tq = min(512, t)
    dq = _attn_bwd_dq(sv["q"], sv["k"], sv["v"], do, sv["lse"], delta, tq)
    dk, dv = _attn_bwd_dkv(sv["q"], sv["k"], sv["v"], do, sv["lse"], delta, tq)

    def rope_b(dqb, dkb, dvb, cc, a1, a2):
        dqr = _heads_apply(dqb * _attn_scale(), lambda xh: _rope(xh, cc, -a1, -a2))
        ksum = dkb[:, :HEAD_LANES]
        for h in range(1, N_HEADS):
            ksum = ksum + dkb[:, h * HEAD_LANES:(h + 1) * HEAD_LANES]
        lane = lax.broadcasted_iota(jnp.int32, ksum.shape, 1)
        keep = (lane >= MLA_NOPE) & (lane < MLA_NOPE + MLA_ROPE)
        dkpe = jnp.where(keep, _rope(ksum, cc, -a1, -a2), 0.0)
        return dqr, jnp.concatenate([dkb, dvb], axis=1), dkpe

    dqraw, dkv, dkpe = _rowwise("mla_rope_b", rope_b, [dq, dk, dv, c, s1, s2], [],
                                [(MLA_PAD, BF16), (2 * MLA_PAD, BF16), (HEAD_LANES, F32)], [], tb // 2)
    dnq = _mm("mla_uq_bx", dqraw, w["w_uq_pad"], "nt", F32)
    grads["w_uq_pad"] = _mm("mla_uq_bw", sv["nq"], dqraw, "tn", F32)
    dnkv = _mm("mla_ukv_bx", dkv, w["w_kv_pad"], "nt", F32)
    grads["w_kv_pad"] = _mm("mla_ukv_bw", sv["nkv"], dkv, "tn", F32)

    def mla_pre_b(p0, dnqb, dnkvb, dkpeb, gq, gkv):
        cq, ckv = p0[:, :MLA_Q_RANK], p0[:, MLA_Q_RANK:MLA_Q_RANK + MLA_KV_RANK]
        dcq, dgq = _rms_bwd(cq, _rms_stats(cq), gq, dnqb)
        dckv, dgkv = _rms_bwd(ckv, _rms_stats(ckv), gkv, dnkvb)
        return jnp.concatenate([dcq, dckv, dkpeb], axis=1), dgq, dgkv

    dproj, grads["mla_q_norm_g"], grads["mla_kv_norm_g"] = _rowwise(
        "mla_pre_b", mla_pre_b, [(proj, 512, PIN_MLA // 512, 0), dnq, dnkv, dkpe],
        [w["mla_q_norm_g"], w["mla_kv_norm_g"]], [(512, BF16)], [(1, MLA_Q_RANK), (1, MLA_KV_RANK)], tb,
        wide=(PIN_W, PIN_MLA // 512), carry=dproj)

    dqh, dkh, dvh, dgh, dbh = _gdn_bwd(sv["qh"], sv["kh"], sv["vh"], sv["gbh"], sv["bbh"], sv["sall"], sv["tall"], doh)
    gones = _group_ones()

    def gdn_pre_b(xq, xk, xv, pq, pk, pv, dq_, dk_, dv_, cw, go, has_prev):
        outs = []
        for j, (xc, xp, dd) in enumerate(((xq, pq, dq_), (xk, pk, dk_), (xv, pv, dv_))):
            cc = _conv_taps(xc, xp, cw[:, j * GDN_W:(j + 1) * GDN_W], has_prev)
            sg = _sigmoid(cc)
            a = cc * sg
            if j < 2:
                rn = lax.rsqrt(_dotf(a * a, go) + EPS)
                if j == 0:
                    dd = dd * (GDN_DH ** -0.5)
                da = rn * dd - a * (rn * rn * rn) * _dotf(dd * a, go)
            else:
                da = dd
            outs.append(da * (sg * (1.0 + cc * (1.0 - sg))))
        return tuple(outs)

    dcq, dck, dcv = _gdn_pre_b_call("gdn_pre_b", gdn_pre_b, proj, (dqh, dkh, dvh), w["conv_w"], gones, tb, steps)
    dproj, grads["conv_w"] = _conv_bwd_call("gdn_conv_b", proj, (dcq, dck, dcv), w["conv_w"], dproj, tb, steps)

    def gate_b(ab_ref, g_ref, b_ref, dgh_ref, dbh_ref, al_ref, dt_ref, carry_ref, dab_ref, dal_ref, ddt_ref):
        @pl.when(pl.program_id(0) == 0)
        def _():
            dal_ref[...] = jnp.zeros(dal_ref.shape, F32)
            ddt_ref[...] = jnp.zeros(ddt_ref.shape, F32)

        ab, g128, b128 = ab_ref[...], g_ref[...], b_ref[...]
        lane = lax.broadcasted_iota(jnp.int32, ab.shape, 1)
        dg_ = jnp.zeros(ab.shape, F32)
        db_ = jnp.zeros(ab.shape, F32)
        for h in range(N_HEADS):
            dg_ = dg_ + jnp.where(lane == h, jnp.broadcast_to(dgh_ref[h][:, 0:1], ab.shape), 0.0)
            db_ = db_ + jnp.where(lane == N_HEADS + h, jnp.broadcast_to(dbh_ref[h][:, 0:1], ab.shape), 0.0)
        slope = -jnp.exp(al_ref[...]) * _sigmoid(ab + dt_ref[...])
        dab_ref[...] = (dg_ * slope + db_ * b128 * (1.0 - b128)).astype(dab_ref.dtype)
        dal_ref[...] += jnp.sum(dg_ * g128, axis=0, keepdims=True)
        ddt_ref[...] += jnp.sum(dg_ * slope, axis=0, keepdims=True)

    lanes_spec = pl.BlockSpec((tb, HEAD_LANES), lambda i: (i, 0))
    ab_spec = pl.BlockSpec((tb, HEAD_LANES), lambda i: (i, PIN_AB // HEAD_LANES))
    dproj, grads["a_log_pad"], grads["dt_bias_pad"] = pl.pallas_call(
        gate_b, name="gdn_gate_b", grid=(steps,),
        in_specs=[ab_spec, lanes_spec, lanes_spec, _heads_spec(tb), _heads_spec(tb), vec_spec(HEAD_LANES),
                  vec_spec(HEAD_LANES), ANY_SPEC],
        out_specs=[ab_spec, vec_spec(HEAD_LANES), vec_spec(HEAD_LANES)],
        out_shape=[jax.ShapeDtypeStruct((t, PIN_W), BF16), jax.ShapeDtypeStruct((1, HEAD_LANES), F32),
                   jax.ShapeDtypeStruct((1, HEAD_LANES), F32)],
        input_output_aliases={7: 0},
        compiler_params=_params(("arbitrary",)))(proj, sv["g128"], sv["b128"], dgh, dbh, w["a_log_pad"],
                                                 w["dt_bias_pad"], dproj)
    dhn = _mm("mix_in_bx", dproj, w["w_in_pad"], "nt", F32)
    grads["w_in_pad"] = _mm("mix_in_bw", sv["hn"], dproj, "tn", F32)

    def pre_b(xb, dnb, dyb, g):
        dx, dg = _rms_bwd(xb, _rms_stats(xb), g, dnb)
        return dyb + dx, dg

    dx, grads["mix_pre_g"] = _rowwise("mix_pre_b", pre_b, [x, dhn, dy], [w["mix_pre_g"]], [(d, F32)], [(1, d)], tb)
    return dx, grads


def _gdn_pre_b_call(name, fn, proj, dd, conv_w, gones, tb, steps):
    t = proj.shape[0]

    def body(xq, xk, xv, pq, pk, pv, d0, d1, d2, cw, go, oq, ok, ov):
        has_prev = jnp.where(pl.program_id(0) == 0, 0.0, 1.0)
        dd_rows = [jnp.concatenate([dr[h] for h in range(N_HEADS)], axis=1) for dr in (d0, d1, d2)]
        outs = fn(xq[...], xk[...], xv[...], pq[...], pk[...], pv[...], *dd_rows, cw[...], go[...], has_prev)
        for r, val in zip((oq, ok, ov), outs):
            r[...] = val

    return pl.pallas_call(
        body, name=name, grid=(steps,),
        in_specs=_qkv_specs(tb) + [_heads_spec(tb)] * 3 + [pl.BlockSpec(conv_w.shape, lambda i: (0, 0)),
                                                          pl.BlockSpec(gones.shape, lambda i: (0, 0))],
        out_specs=[pl.BlockSpec((tb, GDN_W), lambda i: (i, 0))] * 3,
        out_shape=[jax.ShapeDtypeStruct((t, GDN_W), F32)] * 3,
        compiler_params=_params(("arbitrary",)))(proj, proj, proj, proj, proj, proj, *dd, conv_w, gones)


def _conv_bwd_call(name, proj, dc, conv_w, dproj, tb, steps):
    t = proj.shape[0]
    dcur = [pl.BlockSpec((tb, GDN_W), lambda i: (i, 0))] * 3
    dnext = [pl.BlockSpec((tb, GDN_W), lambda i: (jnp.minimum(i + 1, steps - 1), 0))] * 3

    def body(xq, xk, xv, pq, pk, pv, d0, d1, d2, n0, n1, n2, cw, carry_ref, dx_ref, dw_ref):
        i = pl.program_id(0)
        has_prev = jnp.where(i == 0, 0.0, 1.0)
        has_next = jnp.where(i == steps - 1, 0.0, 1.0)

        @pl.when(i == 0)
        def _():
            dw_ref[...] = jnp.zeros(dw_ref.shape, F32)

        wv = cw[...]
        dws, dxs = [], []
        for j, (xr, pr, dr, nr) in enumerate(((xq, pq, d0, n0), (xk, pk, d1, n1), (xv, pv, d2, n2))):
            x, xp, dcv, dnx = xr[...], pr[...], dr[...], nr[...]
            wj = wv[:, j * GDN_W:(j + 1) * GDN_W]
            row = lax.broadcasted_iota(jnp.int32, x.shape, 0)
            dx = dcv * wj[GDN_CONV - 1:GDN_CONV, :]
            rows_w = [jnp.sum(dcv * x, axis=0, keepdims=True)]
            for s in range(1, GDN_CONV):
                up = jnp.where(row < tb - s, _roll(dcv, tb - s, 0), _roll(dnx, tb - s, 0) * has_next)
                dx = dx + up * wj[GDN_CONV - 1 - s:GDN_CONV - s, :]
                sh = jnp.where(row >= s, _roll(x, s, 0), _roll(xp, s, 0) * has_prev)
                rows_w.append(jnp.sum(dcv * sh, axis=0, keepdims=True))
            dxs.append(dx)
            dws.append(jnp.concatenate(rows_w[::-1], axis=0))
        dx_ref[...] = jnp.concatenate(dxs, axis=1).astype(dx_ref.dtype)
        dw_ref[...] += jnp.concatenate(dws, axis=1)

    return pl.pallas_call(
        body, name=name, grid=(steps,),
        in_specs=_qkv_specs(tb) + dcur + dnext + [pl.BlockSpec(conv_w.shape, lambda i: (0, 0)), ANY_SPEC],
        out_specs=[pl.BlockSpec((tb, 3 * GDN_W), lambda i: (i, PIN_QKV // (3 * GDN_W))),
                   pl.BlockSpec(conv_w.shape, lambda i: (0, 0))],
        out_shape=[jax.ShapeDtypeStruct((t, PIN_W), BF16), jax.ShapeDtypeStruct(conv_w.shape, F32)],
        input_output_aliases={13: 0},
        compiler_params=_params(("arbitrary",)))(proj, proj, proj, proj, proj, proj, *dc, *dc, conv_w, dproj)


def _pad_heads_cols(wm, per_head):
    r = wm.shape[0]
    return jnp.pad(wm.reshape(r, N_HEADS, per_head), ((0, 0), (0, 0), (0, HEAD_LANES - per_head))).reshape(r, MLA_PAD)


def _unpad_heads_cols(wm, per_head):
    r = wm.shape[0]
    return wm.reshape(r, N_HEADS, HEAD_LANES)[:, :, :per_head].reshape(r, N_HEADS * per_head)


def _win_to_pad(wi):
    r = wi.shape[0]
    z = lambda n: jnp.zeros((r, n), wi.dtype)
    o = MLA_Q_RANK + MLA_KV_RANK
    kpe = wi[:, o:o + MLA_ROPE]
    o2 = o + MLA_ROPE
    qkv = wi[:, o2:o2 + 3 * GDN_W]
    o3 = o2 + 3 * GDN_W
    ab = wi[:, o3:o3 + 2 * N_HEADS]
    gate = wi[:, o3 + 2 * N_HEADS:]
    return jnp.concatenate([qkv, wi[:, :o], z(MLA_NOPE), kpe, z(HEAD_LANES - MLA_NOPE - MLA_ROPE), gate, ab,
                            z(HEAD_LANES - 2 * N_HEADS)], axis=1)


def _win_from_pad(wp):
    return jnp.concatenate([wp[:, PIN_MLA:PIN_KPE], wp[:, PIN_KPE + MLA_NOPE:PIN_KPE + MLA_NOPE + MLA_ROPE],
                            wp[:, PIN_QKV:PIN_QKV + 3 * GDN_W], wp[:, PIN_AB:PIN_AB + 2 * N_HEADS],
                            wp[:, PIN_GATE:PIN_GATE + GDN_W]], axis=1)


def _wkv_to_pad(wkv):
    r = wkv.shape[0]
    w3 = wkv.reshape(r, N_HEADS, MLA_NOPE + MLA_V)
    kpart = jnp.pad(w3[:, :, :MLA_NOPE], ((0, 0), (0, 0), (0, HEAD_LANES - MLA_NOPE))).reshape(r, MLA_PAD)
    vpart = jnp.pad(w3[:, :, MLA_NOPE:], ((0, 0), (0, 0), (0, HEAD_LANES - MLA_V))).reshape(r, MLA_PAD)
    return jnp.concatenate([kpart, vpart], axis=1)


def _wkv_from_pad(wp):
    r = wp.shape[0]
    kpart = wp[:, :MLA_PAD].reshape(r, N_HEADS, HEAD_LANES)[:, :, :MLA_NOPE]
    vpart = wp[:, MLA_PAD:].reshape(r, N_HEADS, HEAD_LANES)[:, :, :MLA_V]
    return jnp.concatenate([kpart, vpart], axis=2).reshape(r, N_HEADS * (MLA_NOPE + MLA_V))


def _wout_to_pad(wo):
    n = wo.shape[1]
    mla = jnp.pad(wo[:N_HEADS * MLA_V].reshape(N_HEADS, MLA_V, n), ((0, 0), (0, HEAD_LANES - MLA_V), (0, 0)))
    return jnp.concatenate([mla.reshape(MLA_PAD, n), wo[N_HEADS * MLA_V:]], axis=0)


def _wout_from_pad(wp):
    n = wp.shape[1]
    mla = wp[:MLA_PAD].reshape(N_HEADS, HEAD_LANES, n)[:, :MLA_V].reshape(N_HEADS * MLA_V, n)
    return jnp.concatenate([mla, wp[MLA_PAD:]], axis=0)


def _pad_lanes(v, n):
    return jnp.pad(v, ((0, 0), (0, n - v.shape[1])))


def _compute_weights(full):
    w = {}
    for n in FFN_BIG:
        w[n] = full[n].astype(MM_DTYPE)
    w["w_in_pad"] = _win_to_pad(full["w_in"]).astype(MM_DTYPE)
    w["w_uq_pad"] = _pad_heads_cols(full["mla_w_uq"], MLA_NOPE + MLA_ROPE).astype(MM_DTYPE)
    w["w_kv_pad"] = _wkv_to_pad(full["mla_w_ukv"]).astype(MM_DTYPE)
    w["w_out_pad"] = _wout_to_pad(full["w_out"]).astype(MM_DTYPE)
    w["conv_w"] = full["gdn_conv_w"].astype(F32)
    for n in ("ffn1_pre_g", "ffn1_post_g", "mix_pre_g", "mla_q_norm_g", "mla_kv_norm_g", "gdn_norm_g", "mix_post_g",
              "ffn2_pre_g", "ffn2_post_g"):
        w[n] = full[n]
    w["mla_out_g_pad"] = _pad_heads_cols(full["mla_out_g"], MLA_V)
    w["a_log_pad"] = _pad_lanes(full["gdn_a_log"], HEAD_LANES)
    w["dt_bias_pad"] = _pad_lanes(full["gdn_dt_bias"], HEAD_LANES)
    return w


def _local_step(x, positions, loss_target, full):
    t, d = x.shape
    tb = min(512, t)
    tm = min(1024, t)
    w = _compute_weights(full)
    ffn = lambda tag: (w[tag + "_pre_g"], w[tag + "_w_gate"], w[tag + "_w_up"], w[tag + "_w_down"], w[tag + "_post_g"])
    x1, sv1 = _ffn_fwd("ffn1", x, *ffn("ffn1"), tm)
    x2, svm = _mixer_fwd(x1, positions, w, tb)
    x3, sv2 = _ffn_fwd("ffn2", x2, *ffn("ffn2"), tm)

    def loss_f(yb, tg):
        e = yb - tg
        return e * (1.0 / d), jnp.sum(e * e, axis=0, keepdims=True)

    dy, lsum = _rowwise("loss", loss_f, [x3, loss_target], [], [(d, F32)], [(1, d)], tb)
    g = {}
    dx2, g["ffn2_pre_g"], g["ffn2_w_gate"], g["ffn2_w_up"], g["ffn2_w_down"], g["ffn2_post_g"] = _ffn_bwd(
        "ffn2", dy, sv2, *ffn("ffn2"), tm, tm)
    dx1, gm = _mixer_bwd(dx2, svm, w, tb)
    dx0, g["ffn1_pre_g"], g["ffn1_w_gate"], g["ffn1_w_up"], g["ffn1_w_down"], g["ffn1_post_g"] = _ffn_bwd(
        "ffn1", dx1, sv1, *ffn("ffn1"), tm, tm)
    g["mix_pre_g"], g["mix_post_g"] = gm["mix_pre_g"], gm["mix_post_g"]
    g["mla_q_norm_g"], g["mla_kv_norm_g"] = gm["mla_q_norm_g"], gm["mla_kv_norm_g"]
    g["gdn_norm_g"] = gm["gdn_norm_g"]
    g["w_in"] = _win_from_pad(gm["w_in_pad"])
    g["mla_w_uq"] = _unpad_heads_cols(gm["w_uq_pad"], MLA_NOPE + MLA_ROPE)
    g["mla_w_ukv"] = _wkv_from_pad(gm["w_kv_pad"])
    g["mla_out_g"] = _unpad_heads_cols(gm["mla_out_g_pad"], MLA_V)
    g["gdn_conv_w"] = gm["conv_w"]
    g["gdn_a_log"] = gm["a_log_pad"][:, :N_HEADS]
    g["gdn_dt_bias"] = gm["dt_bias_pad"][:, :N_HEADS]
    g["w_out"] = _wout_from_pad(gm["w_out_pad"])
    return lsum, dx0, g


HBM_SPEC = pl.BlockSpec(memory_space=pltpu.HBM)


def _place():
    return lax.axis_index("x"), lax.axis_index("y"), lax.axis_index("c")


def _exchange_call(name, body, ins, out_shapes, n_remote, n_local):
    return pl.pallas_call(
        body, name=name, in_specs=[HBM_SPEC] * len(ins), out_specs=[HBM_SPEC] * len(out_shapes), out_shape=out_shapes,
        scratch_shapes=[pltpu.SemaphoreType.DMA((n_remote,)), pltpu.SemaphoreType.DMA((n_remote,)),
                        pltpu.SemaphoreType.DMA((n_local,))])(*ins)


def _other_chips(x, y):
    return [(1 - x, y), (x, 1 - y), (1 - x, 1 - y)]


def _gather_shards(ws):
    nw = len(ws)

    def body(*refs):
        w_refs, out_refs = refs[:nw], refs[nw:2 * nw]
        send_sems, recv_sems, local_sems = refs[2 * nw:]
        x, y, c = _place()
        chips = _other_chips(x, y)
        local, sends = [], []
        for i, (w_ref, out_ref) in enumerate(zip(w_refs, out_refs)):
            local.append(pltpu.make_async_copy(w_ref, out_ref.at[2 * x + y], local_sems.at[i]))
            local[-1].start()
            for j, (px, py) in enumerate(chips):
                sends.append(pltpu.make_async_remote_copy(
                    src_ref=w_ref, dst_ref=out_ref.at[2 * x + y], send_sem=send_sems.at[3 * i + j],
                    recv_sem=recv_sems.at[3 * i + j], device_id=(px, py, c), device_id_type=MESH))
                sends[-1].start()
        for i, (w_ref, out_ref) in enumerate(zip(w_refs, out_refs)):
            for j, (px, py) in enumerate(chips):
                pltpu.make_async_remote_copy(
                    src_ref=w_ref, dst_ref=out_ref.at[2 * px + py], send_sem=send_sems.at[3 * i + j],
                    recv_sem=recv_sems.at[3 * i + j], device_id=(px, py, c), device_id_type=MESH).wait_recv()
        for cp in sends:
            cp.wait_send()
        for cp in local:
            cp.wait()

    outs = [jax.ShapeDtypeStruct((N_SHARD,) + w.shape, w.dtype) for w in ws]
    return _exchange_call("gather_weight_shards", body, ws, outs, 3 * nw, nw)


def _swap_halves(gs):
    ng = len(gs)

    def body(*refs):
        g_refs, keep_refs, got_refs = refs[:ng], refs[ng:2 * ng], refs[2 * ng:3 * ng]
        send_sems, recv_sems, local_sems = refs[3 * ng:]
        x, y, c = _place()
        local, sends = [], []
        for i, (g_ref, keep_ref, got_ref) in enumerate(zip(g_refs, keep_refs, got_refs)):
            hr = keep_ref.shape[1]
            local.append(pltpu.make_async_copy(g_ref.at[:, pl.ds(c * hr, hr)], keep_ref, local_sems.at[i]))
            local[-1].start()
            sends.append(pltpu.make_async_remote_copy(
                src_ref=g_ref.at[:, pl.ds((1 - c) * hr, hr)], dst_ref=got_ref, send_sem=send_sems.at[i],
                recv_sem=recv_sems.at[i], device_id=(x, y, 1 - c), device_id_type=MESH))
            sends[-1].start()
        for cp in sends:
            cp.wait()
        for cp in local:
            cp.wait()

    halves = [jax.ShapeDtypeStruct((g.shape[0], g.shape[1] // 2, g.shape[2]), g.dtype) for g in gs]
    outs = _exchange_call("swap_grad_halves", body, gs, halves + halves, ng, ng)
    return outs[:ng], outs[ng:]


def _scatter_to_chips(ps):
    n = len(ps)

    def body(*refs):
        p_refs, out_refs = refs[:n], refs[n:2 * n]
        send_sems, recv_sems, local_sems = refs[2 * n:]
        x, y, c = _place()
        me = 2 * x + y
        chips = _other_chips(x, y)
        local, sends = [], []
        for i, (p_ref, out_ref) in enumerate(zip(p_refs, out_refs)):
            local.append(pltpu.make_async_copy(p_ref.at[me], out_ref.at[me], local_sems.at[i]))
            local[-1].start()
            for j, (px, py) in enumerate(chips):
                sends.append(pltpu.make_async_remote_copy(
                    src_ref=p_ref.at[2 * px + py], dst_ref=out_ref.at[me], send_sem=send_sems.at[3 * i + j],
                    recv_sem=recv_sems.at[3 * i + j], device_id=(px, py, c), device_id_type=MESH))
                sends[-1].start()
        for i, (p_ref, out_ref) in enumerate(zip(p_refs, out_refs)):
            for j, (px, py) in enumerate(chips):
                pltpu.make_async_remote_copy(
                    src_ref=p_ref.at[me], dst_ref=out_ref.at[2 * px + py], send_sem=send_sems.at[3 * i + j],
                    recv_sem=recv_sems.at[3 * i + j], device_id=(px, py, c), device_id_type=MESH).wait_recv()
        for cp in sends:
            cp.wait_send()
        for cp in local:
            cp.wait()

    outs = [jax.ShapeDtypeStruct(p.shape, p.dtype) for p in ps]
    return _exchange_call("scatter_grad_quarters", body, ps, outs, 3 * n, n)


def _share_halves(hs):
    n = len(hs)

    def body(*refs):
        h_refs, out_refs = refs[:n], refs[n:2 * n]
        send_sems, recv_sems, local_sems = refs[2 * n:]
        x, y, c = _place()
        local, sends = [], []
        for i, (h_ref, out_ref) in enumerate(zip(h_refs, out_refs)):
            local.append(pltpu.make_async_copy(h_ref, out_ref.at[c], local_sems.at[i]))
            local[-1].start()
            sends.append(pltpu.make_async_remote_copy(
                src_ref=h_ref, dst_ref=out_ref.at[c], send_sem=send_sems.at[i], recv_sem=recv_sems.at[i],
                device_id=(x, y, 1 - c), device_id_type=MESH))
            sends[-1].start()
        for i, (h_ref, out_ref) in enumerate(zip(h_refs, out_refs)):
            pltpu.make_async_remote_copy(
                src_ref=h_ref, dst_ref=out_ref.at[1 - c], send_sem=send_sems.at[i], recv_sem=recv_sems.at[i],
                device_id=(x, y, 1 - c), device_id_type=MESH).wait_recv()
        for cp in sends:
            cp.wait_send()
        for cp in local:
            cp.wait()

    outs = [jax.ShapeDtypeStruct((2,) + h.shape, h.dtype) for h in hs]
    return _exchange_call("share_grad_halves", body, hs, outs, n, n)


def _gather_small(sp):
    def body(s_ref, out_ref, send_sems, recv_sems, local_sem):
        x, y, c = _place()
        me = 4 * x + 2 * y + c
        peers = [(x ^ (m >> 2), y ^ ((m >> 1) & 1), c ^ (m & 1)) for m in range(1, 8)]
        mine = pltpu.make_async_copy(s_ref, out_ref.at[me], local_sem)
        mine.start()
        sends = [pltpu.make_async_remote_copy(src_ref=s_ref, dst_ref=out_ref.at[me], send_sem=send_sems.at[j],
                                              recv_sem=recv_sems.at[j], device_id=p, device_id_type=MESH)
                 for j, p in enumerate(peers)]
        for cp in sends:
            cp.start()
        for j, (px, py, pc) in enumerate(peers):
            pltpu.make_async_remote_copy(src_ref=s_ref, dst_ref=out_ref.at[4 * px + 2 * py + pc],
                                         send_sem=send_sems.at[j], recv_sem=recv_sems.at[j], device_id=(px, py, pc),
                                         device_id_type=MESH).wait_recv()
        for cp in sends:
            cp.wait_send()
        mine.wait()

    return pl.pallas_call(
        body, name="gather_small_grads", in_specs=[HBM_SPEC], out_specs=HBM_SPEC,
        out_shape=jax.ShapeDtypeStruct((8,) + sp.shape, sp.dtype),
        scratch_shapes=[pltpu.SemaphoreType.DMA((7,)), pltpu.SemaphoreType.DMA((7,)), pltpu.SemaphoreType.DMA])(sp)


def _pack_rows(total):
    rows = -(-total // LANES)
    return -(-rows // 32) * 32


def _pack(arrs, dtype):
    flat = jnp.concatenate([a.reshape(-1).astype(dtype) for a in arrs])
    rows = _pack_rows(flat.shape[0])
    return jnp.pad(flat, (0, rows * LANES - flat.shape[0])).reshape(rows, LANES)


def _unpack(buf, shapes):
    flat = buf.reshape(-1)
    out, off = {}, 0
    for n, shp in shapes:
        size = shp[0] * shp[1]
        out[n] = flat[off:off + size].reshape(shp)
        off += size
    return out


def _adamw(name, wv, g, m, v, tb):
    c1 = 1.0 - ADAM_B1 ** ADAM_STEP
    c2 = 1.0 - ADAM_B2 ** ADAM_STEP

    def fn(wb, gb, mb, vb):
        m2 = ADAM_B1 * mb + (1.0 - ADAM_B1) * gb
        v2 = ADAM_B2 * vb + (1.0 - ADAM_B2) * (gb * gb)
        delta = -ADAM_LR * ((m2 / c1) / (jnp.sqrt(v2 / c2) + ADAM_EPS) + ADAM_WD * wb)
        return delta, m2, v2

    cols = wv.shape[1]
    return _rowwise(name, fn, [wv, g, m, v], [], [(cols, F32)] * 3, [], tb)


def _row_tile(rows, pref):
    if rows <= pref:
        return rows
    t = pref
    while t >= 8:
        if rows % t == 0 and t % 8 == 0:
            return t
        t -= 8
    return rows


def kernel(x, positions, ffn1_pre_g, ffn1_w_gate, ffn1_w_up, ffn1_w_down, ffn1_post_g, mix_pre_g, w_in, mla_q_norm_g, mla_w_uq, mla_kv_norm_g, mla_w_ukv, mla_out_g, gdn_conv_w, gdn_a_log, gdn_dt_bias, gdn_norm_g, w_out, mix_post_g, ffn2_pre_g, ffn2_w_gate, ffn2_w_up, ffn2_w_down, ffn2_post_g, loss_target, m_ffn1_pre_g, m_ffn1_w_gate, m_ffn1_w_up, m_ffn1_w_down, m_ffn1_post_g, m_mix_pre_g, m_w_in, m_mla_q_norm_g, m_mla_w_uq, m_mla_kv_norm_g, m_mla_w_ukv, m_mla_out_g, m_gdn_conv_w, m_gdn_a_log, m_gdn_dt_bias, m_gdn_norm_g, m_w_out, m_mix_post_g, m_ffn2_pre_g, m_ffn2_w_gate, m_ffn2_w_up, m_ffn2_w_down, m_ffn2_post_g, v_ffn1_pre_g, v_ffn1_w_gate, v_ffn1_w_up, v_ffn1_w_down, v_ffn1_post_g, v_mix_pre_g, v_w_in, v_mla_q_norm_g, v_mla_w_uq, v_mla_kv_norm_g, v_mla_w_ukv, v_mla_out_g, v_gdn_conv_w, v_gdn_a_log, v_gdn_dt_bias, v_gdn_norm_g, v_w_out, v_mix_post_g, v_ffn2_pre_g, v_ffn2_w_gate, v_ffn2_w_up, v_ffn2_w_down, v_ffn2_post_g):
    args = dict(locals())
    wsh = {n: args[n][0] for n in WEIGHTS}
    msh = {n: args["m_" + n][0] if args["m_" + n].ndim == 3 else args["m_" + n] for n in WEIGHTS}
    vsh = {n: args["v_" + n][0] if args["v_" + n].ndim == 3 else args["v_" + n] for n in WEIGHTS}
    for n in SMALL:
        wsh[n] = args[n]
    mix_shapes = [(n, wsh[n].shape) for n in MIX_BIG]

    gathered = _gather_shards([wsh[n].astype(MM_DTYPE) for n in FFN_BIG] + [_pack([wsh[n] for n in MIX_BIG], MM_DTYPE)])
    full = {n: wsh[n] for n in SMALL}
    for n, gw in zip(FFN_BIG, gathered):
        full[n] = gw
    parts = [_unpack(gathered[-1][q], mix_shapes) for q in range(N_SHARD)]
    for n in MIX_BIG:
        full[n] = jnp.concatenate([parts[q][n] for q in range(N_SHARD)], axis=SHARD_AXIS[n])

    lsum, grad_x, g = _local_step(x[0], positions, loss_target[0], full)
    loss = lax.psum(0.5 * jnp.sum(lsum) / x.shape[-1], ("x", "y", "c"))

    mix_quarters = [_pack([jnp.split(g[n], N_SHARD, axis=SHARD_AXIS[n])[q] for n in MIX_BIG], MM_DTYPE)
                    for q in range(N_SHARD)]
    keep, got = _swap_halves([g[n] for n in FFN_BIG] + [jnp.stack(mix_quarters)])
    pairs = []
    for i, (kp, gt) in enumerate(zip(keep, got)):
        ns_, hr, cols = kp.shape
        (pr,) = _rowwise("add_pair_%d" % i, lambda a, b: (a.astype(F32) + b.astype(F32),),
                         [kp.reshape(ns_ * hr, cols), gt.reshape(ns_ * hr, cols)], [], [(cols, MM_DTYPE)], [],
                         _row_tile(ns_ * hr, 512))
        pairs.append(pr.reshape(ns_, hr, cols))
    slabs = _scatter_to_chips(pairs)
    halves = []
    for i, sl in enumerate(slabs):
        ns_, hr, cols = sl.shape
        th = _row_tile(hr, 512)
        (hs,) = _rowwise("add_chips_%d" % i,
                         lambda a, b, cc, dd: (((a.astype(F32) + b.astype(F32)) + cc.astype(F32)) + dd.astype(F32),),
                         [(sl.reshape(ns_ * hr, cols), cols, 0, q * (hr // th)) for q in range(N_SHARD)], [],
                         [(cols, F32, hr)], [], th)
        halves.append(hs)
    shared = [s2.reshape(2 * s2.shape[1], s2.shape[2]) for s2 in _share_halves(halves)]
    gsh = _unpack(shared[-1], mix_shapes)
    for n, sg_ in zip(FFN_BIG, shared):
        gsh[n] = sg_

    small_shapes = [(n, wsh[n].shape) for n in SMALL]
    pack_small = lambda d: jnp.concatenate(
        [_pad_lanes(d[n].astype(F32), LANES) for n in SMALL] + [jnp.zeros((SMALL_ROWS - len(SMALL), LANES), F32)], axis=0)
    slots = _gather_small(pack_small(g))

    c1 = 1.0 - ADAM_B1 ** ADAM_STEP
    c2 = 1.0 - ADAM_B2 ** ADAM_STEP

    def small_update(wb, mb, vb, s8):
        gs = s8[0:SMALL_ROWS]
        for d in range(1, 8):
            gs = gs + s8[d * SMALL_ROWS:(d + 1) * SMALL_ROWS]
        m2 = ADAM_B1 * mb + (1.0 - ADAM_B1) * gs
        v2 = ADAM_B2 * vb + (1.0 - ADAM_B2) * (gs * gs)
        delta = -ADAM_LR * ((m2 / c1) / (jnp.sqrt(v2 / c2) + ADAM_EPS) + ADAM_WD * wb)
        return gs, delta, m2, v2

    sg, sd, sm, sv_ = _rowwise("adamw_small", small_update,
                               [pack_small(wsh), pack_small(msh), pack_small(vsh)],
                               [slots.reshape(8 * SMALL_ROWS, LANES)], [(LANES, F32)] * 4, [], SMALL_ROWS)
    grads, deltas, new_m, new_v = {}, {}, {}, {}
    for i, (n, shp) in enumerate(small_shapes):
        grads[n], deltas[n] = sg[i:i + 1, :shp[1]], sd[i:i + 1, :shp[1]]
        new_m[n], new_v[n] = sm[i:i + 1, :shp[1]], sv_[i:i + 1, :shp[1]]
    for n in BIG:
        grads[n] = gsh[n]
        r = wsh[n].shape[0]
        deltas[n], new_m[n], new_v[n] = _adamw("adamw_" + n, wsh[n], gsh[n], msh[n], vsh[n], _row_tile(r, 256))

    def shaped(d, n):
        return d[n][None] if n in BIG else d[n]

    return (loss, grad_x[None], *[shaped(grads, n) for n in WEIGHTS], *[shaped(deltas, n) for n in WEIGHTS],
            *[shaped(new_m, n) for n in WEIGHTS], *[shaped(new_v, n) for n in WEIGHTS])
```

```python
import functools

import jax
import jax.numpy as jnp
from jax import lax
from jax.experimental import pallas as pl
from jax.experimental.pallas import tpu as pltpu

F32 = jnp.float32
BF16 = jnp.bfloat16
MM_DTYPE = BF16
HI = lax.Precision.HIGHEST
MESH = pl.DeviceIdType.MESH

D_MODEL = 1024
D_FF = 2816
N_HEADS = 8
MLA_Q_RANK = 256
MLA_KV_RANK = 128
MLA_NOPE = 64
MLA_ROPE = 32
MLA_V = 64
ROPE_THETA = 10000.0
GDN_DH = 64
GDN_W = N_HEADS * GDN_DH
GDN_CONV = 4
CHUNK = 64
HEAD_LANES = 128
MLA_PAD = N_HEADS * HEAD_LANES
EPS = 1e-6
N_SHARD = 4
LANES = 1024

PIN_QKV = 0
PIN_MLA = 1536
PIN_KPE = 1920
PIN_GATE = 2048
PIN_AB = 2560
PIN_W = 2688
CAT_W = MLA_PAD + GDN_W

ADAM_LR = 0.001
ADAM_B1 = 0.9
ADAM_B2 = 0.999
ADAM_EPS = 1e-08
ADAM_WD = 0.01
ADAM_STEP = 10

VMEM_LIMIT_V7X = 56 * 1024 * 1024

BIG = ["ffn1_w_gate", "ffn1_w_up", "ffn1_w_down", "w_in", "mla_w_uq", "mla_w_ukv", "gdn_conv_w", "w_out",
       "ffn2_w_gate", "ffn2_w_up", "ffn2_w_down"]
FFN_BIG = ["ffn1_w_gate", "ffn1_w_up", "ffn1_w_down", "ffn2_w_gate", "ffn2_w_up", "ffn2_w_down"]
MIX_BIG = ["w_in", "mla_w_uq", "mla_w_ukv", "gdn_conv_w", "w_out"]
SMALL = ["ffn1_pre_g", "ffn1_post_g", "mix_pre_g", "mla_q_norm_g", "mla_kv_norm_g", "mla_out_g", "gdn_a_log",
         "gdn_dt_bias", "gdn_norm_g", "mix_post_g", "ffn2_pre_g", "ffn2_post_g"]
WEIGHTS = ["ffn1_pre_g", "ffn1_w_gate", "ffn1_w_up", "ffn1_w_down", "ffn1_post_g", "mix_pre_g", "w_in",
           "mla_q_norm_g", "mla_w_uq", "mla_kv_norm_g", "mla_w_ukv", "mla_out_g", "gdn_conv_w", "gdn_a_log",
           "gdn_dt_bias", "gdn_norm_g", "w_out", "mix_post_g", "ffn2_pre_g", "ffn2_w_gate", "ffn2_w_up",
           "ffn2_w_down", "ffn2_post_g"]
SHARD_AXIS = {"ffn1_w_gate": 1, "ffn1_w_up": 1, "ffn1_w_down": 0, "w_in": 1, "mla_w_uq": 1, "mla_w_ukv": 1,
              "gdn_conv_w": 1, "w_out": 0, "ffn2_w_gate": 1, "ffn2_w_up": 1, "ffn2_w_down": 0}
SMALL_ROWS = 16


def _params(sem):
    return pltpu.CompilerParams(dimension_semantics=sem, vmem_limit_bytes=VMEM_LIMIT_V7X)


def _pick(dim, pref):
    if dim <= pref:
        return dim
    t = (pref // 128) * 128
    while t >= 128:
        if dim % t == 0:
            return t
        t -= 128
    return dim


ANY_SPEC = pl.BlockSpec(memory_space=pl.ANY)


def _rowwise(name, fn, row_ins, bc_ins, row_outs, acc_outs, tb, wide=None, carry=None):
    ents = []
    for e in row_ins:
        ents.append(e if isinstance(e, tuple) else (e, e.shape[1], 0, 0))
    over = [o[2] for o in row_outs if len(o) == 3]
    rows = over[0] if over else ents[0][0].shape[0]
    steps = rows // tb
    assert steps * tb == rows, (name, rows, tb)
    in_specs, args = [], []
    for a, w, j, r0 in ents:
        in_specs.append(pl.BlockSpec((tb, w), lambda i, j=j, r0=r0: (i + r0, j)))
        args.append(a)
    for b in bc_ins:
        in_specs.append(pl.BlockSpec(b.shape, lambda i: (0, 0)))
        args.append(b)
    n_in = len(args)
    aliases = {}
    if carry is not None:
        in_specs.append(ANY_SPEC)
        args.append(carry)
        aliases = {n_in: 0}
    out_shape = [jax.ShapeDtypeStruct((rows, o[0]), o[1]) for o in row_outs]
    out_specs = [pl.BlockSpec((tb, o[0]), lambda i: (i, 0)) for o in row_outs]
    if wide is not None:
        out_shape[0] = jax.ShapeDtypeStruct((rows, wide[0]), row_outs[0][1])
        out_specs[0] = pl.BlockSpec((tb, row_outs[0][0]), lambda i: (i, wide[1]))
    out_shape += [jax.ShapeDtypeStruct((r, c), F32) for r, c in acc_outs]
    out_specs += [pl.BlockSpec((r, c), lambda i: (0, 0)) for r, c in acc_outs]
    n_ro, n_acc, n_args = len(row_outs), len(acc_outs), len(args)

    def body(*refs):
        vals = fn(*[r[...] for r in refs[:n_in]])
        if not isinstance(vals, (tuple, list)):
            vals = (vals,)
        for r, v in zip(refs[n_args:n_args + n_ro], vals[:n_ro]):
            r[...] = v.astype(r.dtype)
        if n_acc:
            acc_refs = refs[n_args + n_ro:]

            @pl.when(pl.program_id(0) == 0)
            def _():
                for r in acc_refs:
                    r[...] = jnp.zeros(r.shape, r.dtype)

            for r, v in zip(acc_refs, vals[n_ro:]):
                r[...] += v

    outs = pl.pallas_call(body, name=name, grid=(steps,), in_specs=in_specs, out_specs=out_specs,
                          out_shape=out_shape, input_output_aliases=aliases,
                          compiler_params=_params(("arbitrary",)))(*args)
    return outs


def _mm(name, a, b, mode, out_dtype, tm=512, tn=512, tk=1024):
    if mode == "nn":
        (m, k), (k2, n) = a.shape, b.shape
    elif mode == "nt":
        (m, k), (n, k2) = a.shape, b.shape
    else:
        (k, m), (k2, n) = a.shape, b.shape
    assert k == k2, (name, a.shape, b.shape)
    tm, tn, tk = _pick(m, tm), _pick(n, tn), _pick(k, tk)
    nk = k // tk
    if mode == "nn":
        a_spec = pl.BlockSpec((tm, tk), lambda i, j, kk: (i, kk))
        b_spec = pl.BlockSpec((tk, tn), lambda i, j, kk: (kk, j))
        dims = (((1,), (0,)), ((), ()))
    elif mode == "nt":
        a_spec = pl.BlockSpec((tm, tk), lambda i, j, kk: (i, kk))
        b_spec = pl.BlockSpec((tn, tk), lambda i, j, kk: (j, kk))
        dims = (((1,), (1,)), ((), ()))
    else:
        a_spec = pl.BlockSpec((tk, tm), lambda i, j, kk: (kk, i))
        b_spec = pl.BlockSpec((tk, tn), lambda i, j, kk: (kk, j))
        dims = (((0,), (0,)), ((), ()))

    def body(a_ref, b_ref, o_ref, acc_ref):
        kk = pl.program_id(2)

        @pl.when(kk == 0)
        def _():
            acc_ref[...] = jnp.zeros(acc_ref.shape, F32)

        acc_ref[...] += lax.dot_general(a_ref[...].astype(MM_DTYPE), b_ref[...].astype(MM_DTYPE), dims,
                                        preferred_element_type=F32)

        @pl.when(kk == nk - 1)
        def _():
            o_ref[...] = acc_ref[...].astype(o_ref.dtype)

    return pl.pallas_call(
        body, name=name, grid=(m // tm, n // tn, nk), in_specs=[a_spec, b_spec],
        out_specs=pl.BlockSpec((tm, tn), lambda i, j, kk: (i, j)),
        out_shape=jax.ShapeDtypeStruct((m, n), out_dtype),
        scratch_shapes=[pltpu.VMEM((tm, tn), F32)],
        compiler_params=_params(("parallel", "parallel", "arbitrary")))(a, b)


def _rms_stats(x, n_real=None):
    n = x.shape[-1] if n_real is None else n_real
    return lax.rsqrt(jnp.sum(x * x, axis=-1, keepdims=True) / n + EPS)


def _rms_bwd(x, r, g, dz, n_real=None):
    n = x.shape[-1] if n_real is None else n_real
    xh = x * r
    dxh = dz * g
    dx = r * (dxh - xh * (jnp.sum(dxh * xh, axis=-1, keepdims=True) / n))
    return dx, jnp.sum(dz * xh, axis=0, keepdims=True)


def _sigmoid(x):
    return 1.0 / (1.0 + jnp.exp(-x))


def _roll(x, s, axis):
    return pltpu.roll(x, s, axis)


def _rope(x, c, s1, s2):
    return x * c + _roll(x, HEAD_LANES - MLA_ROPE // 2, 1) * s1 + _roll(x, MLA_ROPE // 2, 1) * s2


def _heads_apply(x, fn):
    return jnp.concatenate([fn(x[:, h * HEAD_LANES:(h + 1) * HEAD_LANES]) for h in range(N_HEADS)], axis=1)


def _ffn_fwd(tag, x, g_pre, wg, wu, wd, g_post, tm):
    t, d = x.shape
    ns, _, fs = wg.shape
    nt = t // tm
    row = pl.BlockSpec((tm, d), lambda i, q: (i, 0))
    vec = pl.BlockSpec((1, d), lambda i, q: (0, 0))
    act3 = pl.BlockSpec((1, tm, fs), lambda i, q: (q, i, 0))
    wcol = pl.BlockSpec((1, d, fs), lambda i, q: (q, 0, 0))
    wrow = pl.BlockSpec((1, fs, d), lambda i, q: (q, 0, 0))

    def gate_up(x_ref, g_ref, wg_ref, wu_ref, n_ref, a_ref, u_ref, s_ref, n_s):
        @pl.when(pl.program_id(1) == 0)
        def _():
            xb = x_ref[...]
            n_s[...] = (xb * _rms_stats(xb) * g_ref[...]).astype(MM_DTYPE)
            n_ref[...] = n_s[...]

        n = n_s[...]
        a = jnp.dot(n, wg_ref[0], preferred_element_type=F32)
        u = jnp.dot(n, wu_ref[0], preferred_element_type=F32)
        a_ref[0] = a.astype(a_ref.dtype)
        u_ref[0] = u.astype(u_ref.dtype)
        s_ref[0] = ((a * _sigmoid(a)) * u).astype(s_ref.dtype)

    n, a, u, s = pl.pallas_call(
        gate_up, name=tag + "_gate_up", grid=(nt, ns), in_specs=[row, vec, wcol, wcol],
        out_specs=[row, act3, act3, act3],
        out_shape=[jax.ShapeDtypeStruct((t, d), MM_DTYPE)] + [jax.ShapeDtypeStruct((ns, t, fs), MM_DTYPE)] * 3,
        scratch_shapes=[pltpu.VMEM((tm, d), MM_DTYPE)],
        compiler_params=_params(("parallel", "arbitrary")))(x, g_pre, wg, wu)

    def down(s_ref, wd_ref, x_ref, g_ref, h_ref, y_ref, acc):
        q = pl.program_id(1)

        @pl.when(q == 0)
        def _():
            acc[...] = jnp.zeros(acc.shape, F32)

        acc[...] += jnp.dot(s_ref[0], wd_ref[0], preferred_element_type=F32)

        @pl.when(q == ns - 1)
        def _():
            hb = acc[...]
            h_ref[...] = hb
            y_ref[...] = x_ref[...] + 0.5 * (hb * _rms_stats(hb) * g_ref[...])

    h, y = pl.pallas_call(
        down, name=tag + "_down", grid=(nt, ns), in_specs=[act3, wrow, row, vec], out_specs=[row, row],
        out_shape=[jax.ShapeDtypeStruct((t, d), F32)] * 2, scratch_shapes=[pltpu.VMEM((tm, d), F32)],
        compiler_params=_params(("parallel", "arbitrary")))(s, wd, x, g_post)
    return y, (x, n, a, u, s, h)


def _ffn_bwd(tag, dy, saved, g_pre, wg, wu, wd, g_post, tm, tk):
    x, n, a, u, s, h = saved
    t, d = x.shape
    ns, _, fs = wg.shape
    nt, nk = t // tm, t // tk
    row = pl.BlockSpec((tm, d), lambda i, q: (i, 0))
    vec = pl.BlockSpec((1, d), lambda i, q: (0, 0))
    act3 = pl.BlockSpec((1, tm, fs), lambda i, q: (q, i, 0))
    wcol = pl.BlockSpec((1, d, fs), lambda i, q: (q, 0, 0))
    wrow = pl.BlockSpec((1, fs, d), lambda i, q: (q, 0, 0))
    nt_dims = (((1,), (1,)), ((), ()))
    tn_dims = (((0,), (0,)), ((), ()))

    def down_b(h_ref, dy_ref, g_ref, wd_ref, a_ref, u_ref, dh_ref, da_ref, du_ref, dg_ref, dh_s):
        i, q = pl.program_id(0), pl.program_id(1)

        @pl.when((i == 0) & (q == 0))
        def _():
            dg_ref[...] = jnp.zeros(dg_ref.shape, F32)

        @pl.when(q == 0)
        def _():
            hb = h_ref[...]
            dh, dg = _rms_bwd(hb, _rms_stats(hb), g_ref[...], 0.5 * dy_ref[...])
            dh_s[...] = dh.astype(MM_DTYPE)
            dh_ref[...] = dh_s[...]
            dg_ref[...] += dg

        ds = lax.dot_general(dh_s[...], wd_ref[0], nt_dims, preferred_element_type=F32)
        ab, ub = a_ref[0].astype(F32), u_ref[0].astype(F32)
        sg = _sigmoid(ab)
        da_ref[0] = (ds * ub * (sg * (1.0 + ab * (1.0 - sg)))).astype(da_ref.dtype)
        du_ref[0] = (ds * (ab * sg)).astype(du_ref.dtype)

    dh, da, du, dg_post = pl.pallas_call(
        down_b, name=tag + "_down_b", grid=(nt, ns), in_specs=[row, row, vec, wrow, act3, act3],
        out_specs=[row, act3, act3, vec],
        out_shape=[jax.ShapeDtypeStruct((t, d), MM_DTYPE)] + [jax.ShapeDtypeStruct((ns, t, fs), MM_DTYPE)] * 2
        + [jax.ShapeDtypeStruct((1, d), F32)],
        scratch_shapes=[pltpu.VMEM((tm, d), MM_DTYPE)],
        compiler_params=_params(("arbitrary", "arbitrary")))(h, dy, g_post, wd, a, u)

    def down_w(s_ref, dh_ref, dw_ref, acc):
        kk = pl.program_id(1)

        @pl.when(kk == 0)
        def _():
            acc[...] = jnp.zeros(acc.shape, F32)

        acc[...] += lax.dot_general(s_ref[0], dh_ref[...], tn_dims, preferred_element_type=F32)

        @pl.when(kk == nk - 1)
        def _():
            dw_ref[0] = acc[...].astype(dw_ref.dtype)

    dwd = pl.pallas_call(
        down_w, name=tag + "_down_w", grid=(ns, nk),
        in_specs=[pl.BlockSpec((1, tk, fs), lambda q, kk: (q, kk, 0)), pl.BlockSpec((tk, d), lambda q, kk: (kk, 0))],
        out_specs=pl.BlockSpec((1, fs, d), lambda q, kk: (q, 0, 0)),
        out_shape=jax.ShapeDtypeStruct((ns, fs, d), MM_DTYPE), scratch_shapes=[pltpu.VMEM((fs, d), F32)],
        compiler_params=_params(("parallel", "arbitrary")))(s, dh)

    def gate_up_b(da_ref, du_ref, wg_ref, wu_ref, x_ref, dy_ref, g_ref, dx_ref, dg_ref, acc):
        i, q = pl.program_id(0), pl.program_id(1)

        @pl.when((i == 0) & (q == 0))
        def _():
            dg_ref[...] = jnp.zeros(dg_ref.shape, F32)

        @pl.when(q == 0)
        def _():
            acc[...] = jnp.zeros(acc.shape, F32)

        acc[...] += (lax.dot_general(da_ref[0], wg_ref[0], nt_dims, preferred_element_type=F32)
                     + lax.dot_general(du_ref[0], wu_ref[0], nt_dims, preferred_element_type=F32))

        @pl.when(q == ns - 1)
        def _():
            xb = x_ref[...]
            dx, dg = _rms_bwd(xb, _rms_stats(xb), g_ref[...], acc[...])
            dx_ref[...] = dy_ref[...] + dx
            dg_ref[...] += dg

    dx, dg_pre = pl.pallas_call(
        gate_up_b, name=tag + "_gate_up_b", grid=(nt, ns), in_specs=[act3, act3, wcol, wcol, row, row, vec],
        out_specs=[row, vec], out_shape=[jax.ShapeDtypeStruct((t, d), F32), jax.ShapeDtypeStruct((1, d), F32)],
        scratch_shapes=[pltpu.VMEM((tm, d), F32)],
        compiler_params=_params(("arbitrary", "arbitrary")))(da, du, wg, wu, x, dy, g_pre)

    def gate_up_w(n_ref, da_ref, du_ref, dwg_ref, dwu_ref, acc_g, acc_u):
        kk = pl.program_id(1)

        @pl.when(kk == 0)
        def _():
            acc_g[...] = jnp.zeros(acc_g.shape, F32)
            acc_u[...] = jnp.zeros(acc_u.shape, F32)

        nb = n_ref[...]
        acc_g[...] += lax.dot_general(nb, da_ref[0], tn_dims, preferred_element_type=F32)
        acc_u[...] += lax.dot_general(nb, du_ref[0], tn_dims, preferred_element_type=F32)

        @pl.when(kk == nk - 1)
        def _():
            dwg_ref[0] = acc_g[...].astype(dwg_ref.dtype)
            dwu_ref[0] = acc_u[...].astype(dwu_ref.dtype)

    k3 = pl.BlockSpec((1, tk, fs), lambda q, kk: (q, kk, 0))
    wout = pl.BlockSpec((1, d, fs), lambda q, kk: (q, 0, 0))
    dwg, dwu = pl.pallas_call(
        gate_up_w, name=tag + "_gate_up_w", grid=(ns, nk),
        in_specs=[pl.BlockSpec((tk, d), lambda q, kk: (kk, 0)), k3, k3], out_specs=[wout, wout],
        out_shape=[jax.ShapeDtypeStruct((ns, d, fs), MM_DTYPE)] * 2,
        scratch_shapes=[pltpu.VMEM((d, fs), F32)] * 2,
        compiler_params=_params(("parallel", "arbitrary")))(n, da, du)
    return dx, dg_pre, dwg, dwu, dwd, dg_post


NEG = -1e30


def _attn_scale():
    return (MLA_NOPE + MLA_ROPE) ** -0.5


def _causal_pairs(nq, by_key):
    if by_key:
        pairs = [(qi, ki) for ki in range(nq) for qi in range(ki, nq)]
    else:
        pairs = [(qi, ki) for qi in range(nq) for ki in range(qi + 1)]
    return jnp.asarray([p[0] for p in pairs], jnp.int32), jnp.asarray([p[1] for p in pairs], jnp.int32)


def _below_diagonal(shape):
    return lax.broadcasted_iota(jnp.int32, shape, 1) <= lax.broadcasted_iota(jnp.int32, shape, 0)


def _attn_call(name, body, tables, args, in_kinds, out_kinds, scratch, t, tq):
    qmap = lambda h, p, qt, kt: (qt[p], h)
    kmap = lambda h, p, qt, kt: (kt[p], h)
    spec = lambda kind: pl.BlockSpec((tq, HEAD_LANES), qmap if kind == "q" else kmap)
    grid_spec = pltpu.PrefetchScalarGridSpec(
        num_scalar_prefetch=2, grid=(N_HEADS, tables[0].shape[0]), in_specs=[spec(kd) for kd in in_kinds],
        out_specs=[spec(kd) for kd in out_kinds], scratch_shapes=scratch)
    return pl.pallas_call(body, name=name, grid_spec=grid_spec,
                          out_shape=[jax.ShapeDtypeStruct((t, MLA_PAD), F32) for _ in out_kinds],
                          compiler_params=_params(("parallel", "arbitrary")))(*tables, *args)


def _attn_fwd(q, k, v, tq):
    t = q.shape[0]
    nq = t // tq

    def body(qt, kt, q_ref, k_ref, v_ref, o_ref, lse_ref, m_s, l_s, acc_s):
        p_id = pl.program_id(1)
        qi, ki = qt[p_id], kt[p_id]

        @pl.when(ki == 0)
        def _():
            m_s[...] = jnp.full(m_s.shape, NEG, F32)
            l_s[...] = jnp.zeros(l_s.shape, F32)
            acc_s[...] = jnp.zeros(acc_s.shape, F32)

        def scores():
            return lax.dot_general(q_ref[...], k_ref[...], (((1,), (1,)), ((), ())), preferred_element_type=F32)

        def update(s):
            m_old = m_s[...]
            m_new = jnp.maximum(m_old, jnp.max(s, axis=1, keepdims=True))
            alpha = jnp.exp(m_old - m_new)
            p = jnp.exp(s - m_new[:, :1])
            l_s[...] = l_s[...] * alpha + jnp.sum(p, axis=1, keepdims=True)
            acc_s[...] = acc_s[...] * alpha + jnp.dot(p.astype(MM_DTYPE), v_ref[...], preferred_element_type=F32)
            m_s[...] = m_new

        @pl.when(ki < qi)
        def _():
            update(scores())

        @pl.when(ki == qi)
        def _():
            s = scores()
            update(jnp.where(_below_diagonal(s.shape), s, NEG))
            o_ref[...] = acc_s[...] / l_s[...]
            lse_ref[...] = m_s[...] + jnp.log(l_s[...])

    return _attn_call("mla_attn_fwd", body, _causal_pairs(nq, False), (q, k, v), "qkk", "qq",
                      [pltpu.VMEM((tq, HEAD_LANES), F32)] * 3, t, tq)


def _attn_probs(q, k, lse, diagonal):
    s = lax.dot_general(q, k, (((1,), (1,)), ((), ())), preferred_element_type=F32)
    p = jnp.exp(s - lse[:, :1])
    return jnp.where(_below_diagonal(s.shape), p, 0.0) if diagonal else p


def _attn_bwd_dq(q, k, v, do, lse, delta, tq):
    t = q.shape[0]
    nq = t // tq

    def body(qt, kt, q_ref, k_ref, v_ref, do_ref, lse_ref, dl_ref, dq_ref, acc_s):
        p_id = pl.program_id(1)
        qi, ki = qt[p_id], kt[p_id]

        @pl.when(ki == 0)
        def _():
            acc_s[...] = jnp.zeros(acc_s.shape, F32)

        def step(diagonal):
            p = _attn_probs(q_ref[...], k_ref[...], lse_ref[...], diagonal)
            dp = lax.dot_general(do_ref[...], v_ref[...], (((1,), (1,)), ((), ())), preferred_element_type=F32)
            ds = p * (dp - dl_ref[...][:, :1])
            acc_s[...] += jnp.dot(ds.astype(MM_DTYPE), k_ref[...], preferred_element_type=F32)

        @pl.when(ki < qi)
        def _():
            step(False)

        @pl.when(ki == qi)
        def _():
            step(True)
            dq_ref[...] = acc_s[...]

    return _attn_call("mla_attn_bwd_dq", body, _causal_pairs(nq, False), (q, k, v, do, lse, delta), "qkkqqq", "q",
                      [pltpu.VMEM((tq, HEAD_LANES), F32)], t, tq)[0]


def _attn_bwd_dkv(q, k, v, do, lse, delta, tq):
    t = q.shape[0]
    nq = t // tq

    def body(qt, kt, q_ref, k_ref, v_ref, do_ref, lse_ref, dl_ref, dk_ref, dv_ref, dk_s, dv_s):
        p_id = pl.program_id(1)
        qi, ki = qt[p_id], kt[p_id]

        def step(diagonal):
            p = _attn_probs(q_ref[...], k_ref[...], lse_ref[...], diagonal)
            dv_s[...] += lax.dot_general(p.astype(MM_DTYPE), do_ref[...], (((0,), (0,)), ((), ())),
                                         preferred_element_type=F32)
            dp = lax.dot_general(do_ref[...], v_ref[...], (((1,), (1,)), ((), ())), preferred_element_type=F32)
            ds = p * (dp - dl_ref[...][:, :1])
            dk_s[...] += lax.dot_general(ds.astype(MM_DTYPE), q_ref[...], (((0,), (0,)), ((), ())),
                                         preferred_element_type=F32)

        @pl.when(qi == ki)
        def _():
            dk_s[...] = jnp.zeros(dk_s.shape, F32)
            dv_s[...] = jnp.zeros(dv_s.shape, F32)
            step(True)

        @pl.when(qi > ki)
        def _():
            step(False)

        @pl.when(qi == nq - 1)
        def _():
            dk_ref[...] = dk_s[...]
            dv_ref[...] = dv_s[...]

    return _attn_call("mla_attn_bwd_dkv", body, _causal_pairs(nq, True), (q, k, v, do, lse, delta), "qkkqqq", "kk",
                      [pltpu.VMEM((tq, HEAD_LANES), F32)] * 2, t, tq)


def _dotf(a, b, dims=(((1,), (0,)), ((), ()))):
    return lax.dot_general(a, b, dims, preferred_element_type=F32, precision=HI)


def _dot1(a, b, dims=(((1,), (0,)), ((), ()))):
    return lax.dot_general(a.astype(MM_DTYPE), b.astype(MM_DTYPE), dims, preferred_element_type=F32)


def _dot3(a, b, dims=(((1,), (0,)), ((), ()))):
    return lax.dot_general(a, b, dims, preferred_element_type=F32, precision=lax.Precision.HIGH)


NN3 = (((2,), (1,)), ((0,), (0,)))
NT3 = (((2,), (2,)), ((0,), (0,)))
TN3 = (((1,), (1,)), ((0,), (0,)))


def _tri_masks(nh):
    shape = (nh, CHUNK, CHUNK)
    return lax.broadcasted_iota(jnp.int32, shape, 1), lax.broadcasted_iota(jnp.int32, shape, 2)


def _gdn_chunk_common(k, gb, bb, row, col, dot=_dot1):
    tril = row >= col
    ltri = tril.astype(F32)
    umat = (row <= col).astype(F32)
    gcc = _dotf(ltri, gb, NN3)
    gcr = _dotf(gb, umat, TN3)
    dm = jnp.exp(jnp.where(tril, gcc - gcr, NEG))
    kb = k * bb
    lm = jnp.where(row > col, dot(kb, k, NT3) * dm, 0.0)
    return gcc, dm, kb, lm, umat


def _unit_lower_inverse(lm, eye):
    t = eye - lm
    p = lm
    for _ in range(CHUNK.bit_length() - 2):
        p = _dotf(p, p, NN3)
        t = t + _dotf(t, p, NN3)
    return t


def _gdn_fwd(q, k, v, gb, bb):
    nh, t, dh = q.shape
    nchunk = t // CHUNK

    def body(q_ref, k_ref, v_ref, g_ref, b_ref, o_ref, sall_ref, tall_ref, s_s):
        @pl.when(pl.program_id(0) == 0)
        def _():
            s_s[...] = jnp.zeros(s_s.shape, F32)

        row, col = _tri_masks(nh)
        qh, kh, vh, bbh = q_ref[...], k_ref[...], v_ref[...], b_ref[...]
        gcc, dm, kb, lm, _ = _gdn_chunk_common(kh, g_ref[...], bbh, row, col)
        eg = jnp.exp(gcc)
        glr = gcc[:, CHUNK - 1:CHUNK, :]
        th = _unit_lower_inverse(lm, (row == col).astype(F32))
        w = _dot1(th, kb * eg, NN3)
        u = _dot1(th, vh * bbh, NN3)
        at = jnp.where(row >= col, _dot1(qh, kh, NT3) * dm, 0.0)
        sh = s_s[...]
        vn = u - _dot1(w, sh, NN3)
        o_ref[...] = _dot1(qh * eg, sh, NN3) + _dot1(at, vn, NN3)
        kd = kh * jnp.exp(glr - gcc)
        sall_ref[:, 0] = sh
        tall_ref[...] = th
        s_s[...] = sh * jnp.exp(glr) + _dot1(kd, vn, TN3)

    blk = pl.BlockSpec((nh, CHUNK, dh), lambda n: (0, n, 0))
    return pl.pallas_call(
        body, name="gdn_fwd", grid=(nchunk,), in_specs=[blk] * 5,
        out_specs=[blk, pl.BlockSpec((nh, 1, dh, dh), lambda n: (0, n, 0, 0)), blk],
        out_shape=[jax.ShapeDtypeStruct((nh, t, dh), F32), jax.ShapeDtypeStruct((nh, nchunk, dh, dh), F32),
                   jax.ShapeDtypeStruct((nh, t, CHUNK), F32)],
        scratch_shapes=[pltpu.VMEM((nh, dh, dh), F32)],
        compiler_params=_params(("arbitrary",)))(q, k, v, gb, bb)


def _gdn_bwd(q, k, v, gb, bb, sall, tall, do):
    nh, t, dh = q.shape
    nchunk = t // CHUNK

    def body(q_ref, k_ref, v_ref, g_ref, b_ref, sall_ref, tall_ref, do_ref,
             dq_ref, dk_ref, dv_ref, dg_ref, db_ref, ds_s):
        @pl.when(pl.program_id(0) == 0)
        def _():
            ds_s[...] = jnp.zeros(ds_s.shape, F32)

        row, col = _tri_masks(nh)
        tril, stril = row >= col, row > col
        rsum = lambda x: jnp.sum(x, axis=2, keepdims=True)
        qh, kh, vh, gbh, bbh = q_ref[...], k_ref[...], v_ref[...], g_ref[...], b_ref[...]
        sh, th, doh, dsp = sall_ref[:, 0], tall_ref[...], do_ref[...], ds_s[...]
        gcc, dm, kb, lm, umat = _gdn_chunk_common(kh, gbh, bbh, row, col, _dot3)
        eg = jnp.exp(gcc)
        glr = gcc[:, CHUNK - 1:CHUNK, :]
        glv = jnp.exp(glr)
        egl = jnp.exp(glr - gcc)
        rw, ru = kb * eg, vh * bbh
        w, u = _dot3(th, rw, NN3), _dot3(th, ru, NN3)
        at = jnp.where(tril, _dot3(qh, kh, NT3) * dm, 0.0)
        qd, kd = qh * eg, kh * egl
        vn = u - _dot3(w, sh, NN3)
        dgl = jnp.sum(rsum(dsp * sh), axis=1, keepdims=True)
        dkd = _dot3(vn, dsp, NT3)
        dvn = _dot3(kd, dsp, NN3)
        dqd = _dot3(doh, sh, NT3)
        dat = jnp.where(tril, _dot3(doh, vn, NT3), 0.0)
        dvn = dvn + _dot3(at, doh, TN3)
        dw = -_dot3(dvn, sh, NT3)
        ds_s[...] = dsp * glv + _dot3(qd, doh, TN3) - _dot3(w, dvn, TN3)
        dpa = dat * dm
        dq_ref[...] = _dot3(dpa, kh, NN3) + dqd * eg
        dk = _dot3(dpa, qh, TN3) + dkd * egl
        t6 = rsum(dkd * kd)
        dgam = rsum(dqd * qd) - t6
        dgam_last = jnp.sum(t6, axis=1, keepdims=True) + dgl * glv
        drw = _dot3(th, dw, TN3)
        dru = _dot3(th, dvn, TN3)
        dl = -jnp.where(stril, _dot3(drw, w, NT3) + _dot3(dru, u, NT3), 0.0)
        dgam = dgam + rsum(drw * rw)
        dv_ref[...] = dru * bbh
        dp2 = dl * dm
        dkb = drw * eg + _dot3(dp2, kh, NN3)
        dk_ref[...] = dk + _dot3(dp2, kb, TN3) + dkb * bbh
        db_ref[...] = rsum(dru * vh) + rsum(dkb * kh) + jnp.zeros((nh, CHUNK, dh), F32)
        e = dat * at + dl * lm
        dgam_b = dgam + rsum(e) - _dotf(e, jnp.ones((nh, CHUNK, CHUNK), F32), TN3)
        dgam_b = dgam_b + jnp.where(row == CHUNK - 1, dgam_last, 0.0)
        dg_ref[...] = _dotf(umat, dgam_b, NN3)

    rev = lambda n: (0, nchunk - 1 - n, 0)
    blk = pl.BlockSpec((nh, CHUNK, dh), rev)
    sblk = pl.BlockSpec((nh, 1, dh, dh), lambda n: (0, nchunk - 1 - n, 0, 0))
    out = jax.ShapeDtypeStruct((nh, t, dh), F32)
    return pl.pallas_call(
        body, name="gdn_bwd", grid=(nchunk,), in_specs=[blk] * 5 + [sblk, blk, blk], out_specs=[blk] * 5,
        out_shape=[out] * 5, scratch_shapes=[pltpu.VMEM((nh, dh, dh), F32)],
        compiler_params=_params(("arbitrary",)))(q, k, v, gb, bb, sall, tall, do)


def _group_ones():
    r = lax.broadcasted_iota(jnp.int32, (GDN_W, GDN_W), 0) // GDN_DH
    c = lax.broadcasted_iota(jnp.int32, (GDN_W, GDN_W), 1) // GDN_DH
    return (r == c).astype(F32)


def _conv_taps(x, xprev, w, has_prev):
    row = lax.broadcasted_iota(jnp.int32, x.shape, 0)
    out = x * w[GDN_CONV - 1:GDN_CONV, :]
    for s in range(1, GDN_CONV):
        sh = jnp.where(row >= s, _roll(x, s, 0), _roll(xprev, s, 0) * has_prev)
        out = out + sh * w[GDN_CONV - 1 - s:GDN_CONV - s, :]
    return out


def _head_cols(x, h):
    return x[:, h * GDN_DH:(h + 1) * GDN_DH]


def _heads_spec(tb):
    return pl.BlockSpec((N_HEADS, tb, GDN_DH), lambda i: (0, i, 0))


def _mixer_fwd(x, positions, w, tb):
    t, d = x.shape
    tables = _rope_tables(positions)

    def pre(xb, g):
        return (xb * _rms_stats(xb) * g,)

    (hn,) = _rowwise("mix_pre", pre, [x], [w["mix_pre_g"]], [(d, BF16)], [], tb)
    proj = _mm("mix_in", hn, w["w_in_pad"], "nn", F32)

    def mla_pre(p0, gq, gkv):
        cq, ckv = p0[:, :MLA_Q_RANK], p0[:, MLA_Q_RANK:MLA_Q_RANK + MLA_KV_RANK]
        return cq * _rms_stats(cq) * gq, ckv * _rms_stats(ckv) * gkv

    nq, nkv = _rowwise("mla_pre", mla_pre, [(proj, 512, PIN_MLA // 512, 0)],
                       [w["mla_q_norm_g"], w["mla_kv_norm_g"]], [(MLA_Q_RANK, BF16), (MLA_KV_RANK, BF16)], [], tb)
    qraw = _mm("mla_uq", nq, w["w_uq_pad"], "nn", F32)
    kv = _mm("mla_ukv", nkv, w["w_kv_pad"], "nn", F32)

    def rope_f(qr, kn, vv, kpe, c, s1, s2):
        qo = _heads_apply(qr, lambda xh: _rope(xh, c, s1, s2)) * _attn_scale()
        kp = _rope(kpe, c, s1, s2)
        return qo, kn + jnp.tile(kp, (1, N_HEADS)), vv

    q, k, v = _rowwise("mla_rope", rope_f,
                       [qraw, (kv, MLA_PAD, 0, 0), (kv, MLA_PAD, 1, 0), (proj, HEAD_LANES, PIN_KPE // HEAD_LANES, 0),
                        tables[0], tables[1], tables[2]], [],
                       [(MLA_PAD, BF16)] * 3, [], tb // 2)
    tq = min(512, t)
    o, lse = _attn_fwd(q, k, v, tq)

    def mla_post(ob, g):
        return (ob * _rms_stats(ob, N_HEADS * MLA_V) * g,)

    (cat,) = _rowwise("mla_post", mla_post, [o], [w["mla_out_g_pad"]], [(MLA_PAD, BF16)], [], tb, wide=(CAT_W, 0))

    gones = _group_ones()
    steps = t // tb

    def gdn_pre(xq, xk, xv, pq, pk, pv, cw, go, has_prev):
        outs = []
        for j, (xc, xp) in enumerate(((xq, pq), (xk, pk), (xv, pv))):
            c = _conv_taps(xc, xp, cw[:, j * GDN_W:(j + 1) * GDN_W], has_prev)
            a = c * _sigmoid(c)
            if j < 2:
                rn = lax.rsqrt(_dotf(a * a, go) + EPS)
                a = a * rn
                if j == 0:
                    a = a * (GDN_DH ** -0.5)
            outs.append(a)
        return tuple(outs)

    qh, kh, vh = _gdn_pre_call("gdn_pre", gdn_pre, proj, w["conv_w"], gones, tb, steps)
    heads_shape = jax.ShapeDtypeStruct((N_HEADS, t, GDN_DH), F32)
    lanes_shape = jax.ShapeDtypeStruct((t, HEAD_LANES), F32)
    lanes_spec = pl.BlockSpec((tb, HEAD_LANES), lambda i: (i, 0))
    vec_spec = lambda n: pl.BlockSpec((1, n), lambda i: (0, 0))

    def gate_f(ab_ref, al_ref, dt_ref, g_ref, b_ref, gh_ref, bh_ref):
        g, b = _gb_fwd(ab_ref[...], al_ref[...], dt_ref[...])
        g_ref[...] = g
        b_ref[...] = b
        for h in range(N_HEADS):
            gh_ref[h] = jnp.broadcast_to(g[:, h:h + 1], (tb, GDN_DH))
            bh_ref[h] = jnp.broadcast_to(b[:, N_HEADS + h:N_HEADS + h + 1], (tb, GDN_DH))

    g128, b128, gbh, bbh = pl.pallas_call(
        gate_f, name="gdn_gate_f", grid=(steps,),
        in_specs=[pl.BlockSpec((tb, HEAD_LANES), lambda i: (i, PIN_AB // HEAD_LANES)), vec_spec(HEAD_LANES),
                  vec_spec(HEAD_LANES)],
        out_specs=[lanes_spec, lanes_spec, _heads_spec(tb), _heads_spec(tb)],
        out_shape=[lanes_shape, lanes_shape, heads_shape, heads_shape],
        compiler_params=_params(("arbitrary",)))(proj, w["a_log_pad"], w["dt_bias_pad"])
    oh, sall, tall = _gdn_fwd(qh, kh, vh, gbh, bbh)

    def gdn_post(o_ref, gt_ref, g_ref, cat_in, cat_ref):
        gt, g = gt_ref[...], g_ref[...]
        outs = []
        for h in range(N_HEADS):
            ob, gth = o_ref[h], _head_cols(gt, h)
            outs.append(ob * _rms_stats(ob) * g * (gth * _sigmoid(gth)))
        cat_ref[...] = jnp.concatenate(outs, axis=1).astype(cat_ref.dtype)

    gate_spec = pl.BlockSpec((tb, GDN_W), lambda i: (i, PIN_GATE // GDN_W))
    cat = pl.pallas_call(
        gdn_post, name="gdn_post", grid=(steps,),
        in_specs=[_heads_spec(tb), gate_spec, vec_spec(GDN_DH), ANY_SPEC],
        out_specs=pl.BlockSpec((tb, GDN_W), lambda i: (i, MLA_PAD // GDN_W)),
        out_shape=jax.ShapeDtypeStruct((t, CAT_W), BF16), input_output_aliases={3: 0},
        compiler_params=_params(("arbitrary",)))(oh, proj, w["gdn_norm_g"], cat)
    mixed = _mm("mix_out", cat, w["w_out_pad"], "nn", F32)

    def post(xb, hb, g):
        return (xb + hb * _rms_stats(hb) * g,)

    (y,) = _rowwise("mix_post", post, [x, mixed], [w["mix_post_g"]], [(d, F32)], [], tb)
    saved = dict(x=x, hn=hn, proj=proj, nq=nq, nkv=nkv, q=q, k=k, v=v, o=o, lse=lse, qh=qh, kh=kh, vh=vh,
                 gbh=gbh, bbh=bbh, oh=oh, sall=sall, tall=tall, cat=cat, mixed=mixed,
                 tables=tables, g128=g128, b128=b128)
    return y, saved


def _qkv_specs(tb):
    base = PIN_QKV // GDN_W
    cur = [pl.BlockSpec((tb, GDN_W), lambda i, j=j: (i, base + j)) for j in range(3)]
    prev = [pl.BlockSpec((tb, GDN_W), lambda i, j=j: (jnp.maximum(i - 1, 0), base + j)) for j in range(3)]
    return cur + prev


def _gdn_pre_call(name, fn, proj, conv_w, gones, tb, steps):
    t = proj.shape[0]

    def body(xq, xk, xv, pq, pk, pv, cw, go, oq, ok, ov):
        has_prev = jnp.where(pl.program_id(0) == 0, 0.0, 1.0)
        outs = fn(xq[...], xk[...], xv[...], pq[...], pk[...], pv[...], cw[...], go[...], has_prev)
        for r, val in zip((oq, ok, ov), outs):
            for h in range(N_HEADS):
                r[h] = _head_cols(val, h)

    return pl.pallas_call(
        body, name=name, grid=(steps,),
        in_specs=_qkv_specs(tb) + [pl.BlockSpec(conv_w.shape, lambda i: (0, 0)),
                                   pl.BlockSpec(gones.shape, lambda i: (0, 0))],
        out_specs=[_heads_spec(tb)] * 3,
        out_shape=[jax.ShapeDtypeStruct((N_HEADS, t, GDN_DH), F32)] * 3,
        compiler_params=_params(("arbitrary",)))(proj, proj, proj, proj, proj, proj, conv_w, gones)


def _softplus(x):
    return jnp.maximum(x, 0.0) + jnp.log1p(jnp.exp(-jnp.abs(x)))


def _gb_fwd(ab, a_log, dt_bias):
    g = -jnp.exp(a_log) * _softplus(ab + dt_bias)
    return g, _sigmoid(ab)


def _rope_tables(positions):
    half = MLA_ROPE // 2
    freqs = ROPE_THETA ** (-jnp.arange(half, dtype=F32) / half)
    ang = positions.reshape(-1).astype(F32)[:, None] * freqs
    cos, sin = jnp.cos(ang), jnp.sin(ang)
    t = ang.shape[0]
    one = jnp.ones((t, MLA_NOPE), F32)
    z16, z32, z64 = jnp.zeros((t, half), F32), jnp.zeros((t, MLA_ROPE), F32), jnp.zeros((t, MLA_NOPE), F32)
    c = jnp.concatenate([one, cos, cos, jnp.ones((t, MLA_ROPE), F32)], axis=1)
    s1 = jnp.concatenate([z64, -sin, z16, z32], axis=1)
    s2 = jnp.concatenate([z64, z16, sin, z32], axis=1)
    return c, s1, s2


def _mixer_bwd(dy, sv, w, tb):
    x, proj = sv["x"], sv["proj"]
    t, d = x.shape
    c, s1, s2 = sv["tables"]
    grads = {}

    def post_b(hb, dyb, g):
        return _rms_bwd(hb, _rms_stats(hb), g, dyb)

    dmixed, grads["mix_post_g"] = _rowwise("mix_post_b", post_b, [sv["mixed"], dy], [w["mix_post_g"]],
                                           [(d, BF16)], [(1, d)], tb)
    dcat = _mm("mix_out_bx", dmixed, w["w_out_pad"], "nt", F32)
    grads["w_out_pad"] = _mm("mix_out_bw", sv["cat"], dmixed, "tn", F32)
    steps = t // tb
    vec_spec = lambda n: pl.BlockSpec((1, n), lambda i: (0, 0))

    def gdn_post_b(o_ref, gt_ref, do_ref, g_ref, dproj_ref, doh_ref, dg_ref):
        @pl.when(pl.program_id(0) == 0)
        def _():
            dg_ref[...] = jnp.zeros(dg_ref.shape, F32)

        gt, dob, g = gt_ref[...], do_ref[...], g_ref[...]
        dgates = []
        for h in range(N_HEADS):
            ob, gth, dobh = o_ref[h], _head_cols(gt, h), _head_cols(dob, h)
            sg = _sigmoid(gth)
            r = _rms_stats(ob)
            dxo, dg = _rms_bwd(ob, r, g, dobh * (gth * sg))
            doh_ref[h] = dxo
            dg_ref[...] += dg
            dgates.append(dobh * (ob * r * g) * (sg * (1.0 + gth * (1.0 - sg))))
        dproj_ref[...] = jnp.concatenate(dgates, axis=1).astype(dproj_ref.dtype)

    dproj, doh, grads["gdn_norm_g"] = pl.pallas_call(
        gdn_post_b, name="gdn_post_b", grid=(steps,),
        in_specs=[_heads_spec(tb), pl.BlockSpec((tb, GDN_W), lambda i: (i, PIN_GATE // GDN_W)),
                  pl.BlockSpec((tb, GDN_W), lambda i: (i, MLA_PAD // GDN_W)), vec_spec(GDN_DH)],
        out_specs=[pl.BlockSpec((tb, GDN_W), lambda i: (i, PIN_GATE // GDN_W)), _heads_spec(tb), vec_spec(GDN_DH)],
        out_shape=[jax.ShapeDtypeStruct((t, PIN_W), BF16), jax.ShapeDtypeStruct((N_HEADS, t, GDN_DH), F32),
                   jax.ShapeDtypeStruct((1, GDN_DH), F32)],
        compiler_params=_params(("arbitrary",)))(sv["oh"], proj, dcat, w["gdn_norm_g"])

    def mla_post_b(ob, dmo, g):
        do, dg = _rms_bwd(ob, _rms_stats(ob, N_HEADS * MLA_V), g, dmo, N_HEADS * MLA_V)
        prod = do * ob
        delta = _heads_apply(prod, lambda ph: jnp.sum(ph, axis=1, keepdims=True) + jnp.zeros_like(ph))
        return do, delta, dg

    do, delta, grads["mla_out_g_pad"] = _rowwise(
        "mla_post_b", mla_post_b, [sv["o"], (dcat, MLA_PAD, 0, 0)], [w["mla_out_g_pad"]],
        [(MLA_PAD, BF16), (MLA_PAD, F32)], [(1, MLA_PAD)], tb // 2)
    tq = min(512, t)
    dq = _attn_bwd_dq(sv["q"], sv["k"], sv["v"], do, sv["lse"], delta, tq)
    dk, dv = _attn_bwd_dkv(sv["q"], sv["k"], sv["v"], do, sv["lse"], delta, tq)

    def rope_b(dqb, dkb, dvb, cc, a1, a2):
        dqr = _heads_apply(dqb * _attn_scale(), lambda xh: _rope(xh, cc, -a1, -a2))
        ksum = dkb[:, :HEAD_LANES]
        for h in range(1, N_HEADS):
            ksum = ksum + dkb[:, h * HEAD_LANES:(h + 1) * HEAD_LANES]
        lane = lax.broadcasted_iota(jnp.int32, ksum.shape, 1)
        keep = (lane >= MLA_NOPE) & (lane < MLA_NOPE + MLA_ROPE)
        dkpe = jnp.where(keep, _rope(ksum, cc, -a1, -a2), 0.0)
        return dqr, jnp.concatenate([dkb, dvb], axis=1), dkpe

    dqraw, dkv, dkpe = _rowwise("mla_rope_b", rope_b, [dq, dk, dv, c, s1, s2], [],
                                [(MLA_PAD, BF16), (2 * MLA_PAD, BF16), (HEAD_LANES, F32)], [], tb // 2)
    dnq = _mm("mla_uq_bx", dqraw, w["w_uq_pad"], "nt", F32)
    grads["w_uq_pad"] = _mm("mla_uq_bw", sv["nq"], dqraw, "tn", F32)
    dnkv = _mm("mla_ukv_bx", dkv, w["w_kv_pad"], "nt", F32)
    grads["w_kv_pad"] = _mm("mla_ukv_bw", sv["nkv"], dkv, "tn", F32)

    def mla_pre_b(p0, dnqb, dnkvb, dkpeb, gq, gkv):
        cq, ckv = p0[:, :MLA_Q_RANK], p0[:, MLA_Q_RANK:MLA_Q_RANK + MLA_KV_RANK]
        dcq, dgq = _rms_bwd(cq, _rms_stats(cq), gq, dnqb)
        dckv, dgkv = _rms_bwd(ckv, _rms_stats(ckv), gkv, dnkvb)
        return jnp.concatenate([dcq, dckv, dkpeb], axis=1), dgq, dgkv

    dproj, grads["mla_q_norm_g"], grads["mla_kv_norm_g"] = _rowwise(
        "mla_pre_b", mla_pre_b, [(proj, 512, PIN_MLA // 512, 0), dnq, dnkv, dkpe],
        [w["mla_q_norm_g"], w["mla_kv_norm_g"]], [(512, BF16)], [(1, MLA_Q_RANK), (1, MLA_KV_RANK)], tb,
        wide=(PIN_W, PIN_MLA // 512), carry=dproj)

    dqh, dkh, dvh, dgh, dbh = _gdn_bwd(sv["qh"], sv["kh"], sv["vh"], sv["gbh"], sv["bbh"], sv["sall"], sv["tall"], doh)
    gones = _group_ones()

    def gdn_pre_b(xq, xk, xv, pq, pk, pv, dq_, dk_, dv_, cw, go, has_prev):
        outs = []
        for j, (xc, xp, dd) in enumerate(((xq, pq, dq_), (xk, pk, dk_), (xv, pv, dv_))):
            cc = _conv_taps(xc, xp, cw[:, j * GDN_W:(j + 1) * GDN_W], has_prev)
            sg = _sigmoid(cc)
            a = cc * sg
            if j < 2:
                rn = lax.rsqrt(_dotf(a * a, go) + EPS)
                if j == 0:
                    dd = dd * (GDN_DH ** -0.5)
                da = rn * dd - a * (rn * rn * rn) * _dotf(dd * a, go)
            else:
                da = dd
            outs.append(da * (sg * (1.0 + cc * (1.0 - sg))))
        return tuple(outs)

    dcq, dck, dcv = _gdn_pre_b_call("gdn_pre_b", gdn_pre_b, proj, (dqh, dkh, dvh), w["conv_w"], gones, tb, steps)
    dproj, grads["conv_w"] = _conv_bwd_call("gdn_conv_b", proj, (dcq, dck, dcv), w["conv_w"], dproj, tb, steps)

    def gate_b(ab_ref, g_ref, b_ref, dgh_ref, dbh_ref, al_ref, dt_ref, carry_ref, dab_ref, dal_ref, ddt_ref):
        @pl.when(pl.program_id(0) == 0)
        def _():
            dal_ref[...] = jnp.zeros(dal_ref.shape, F32)
            ddt_ref[...] = jnp.zeros(ddt_ref.shape, F32)

        ab, g128, b128 = ab_ref[...], g_ref[...], b_ref[...]
        lane = lax.broadcasted_iota(jnp.int32, ab.shape, 1)
        dg_ = jnp.zeros(ab.shape, F32)
        db_ = jnp.zeros(ab.shape, F32)
        for h in range(N_HEADS):
            dg_ = dg_ + jnp.where(lane == h, jnp.broadcast_to(dgh_ref[h][:, 0:1], ab.shape), 0.0)
            db_ = db_ + jnp.where(lane == N_HEADS + h, jnp.broadcast_to(dbh_ref[h][:, 0:1], ab.shape), 0.0)
        slope = -jnp.exp(al_ref[...]) * _sigmoid(ab + dt_ref[...])
        dab_ref[...] = (dg_ * slope + db_ * b128 * (1.0 - b128)).astype(dab_ref.dtype)
        dal_ref[...] += jnp.sum(dg_ * g128, axis=0, keepdims=True)
        ddt_ref[...] += jnp.sum(dg_ * slope, axis=0, keepdims=True)

    lanes_spec = pl.BlockSpec((tb, HEAD_LANES), lambda i: (i, 0))
    ab_spec = pl.BlockSpec((tb, HEAD_LANES), lambda i: (i, PIN_AB // HEAD_LANES))
    dproj, grads["a_log_pad"], grads["dt_bias_pad"] = pl.pallas_call(
        gate_b, name="gdn_gate_b", grid=(steps,),
        in_specs=[ab_spec, lanes_spec, lanes_spec, _heads_spec(tb), _heads_spec(tb), vec_spec(HEAD_LANES),
                  vec_spec(HEAD_LANES), ANY_SPEC],
        out_specs=[ab_spec, vec_spec(HEAD_LANES), vec_spec(HEAD_LANES)],
        out_shape=[jax.ShapeDtypeStruct((t, PIN_W), BF16), jax.ShapeDtypeStruct((1, HEAD_LANES), F32),
                   jax.ShapeDtypeStruct((1, HEAD_LANES), F32)],
        input_output_aliases={7: 0},
        compiler_params=_params(("arbitrary",)))(proj, sv["g128"], sv["b128"], dgh, dbh, w["a_log_pad"],
                                                 w["dt_bias_pad"], dproj)
    dhn = _mm("mix_in_bx", dproj, w["w_in_pad"], "nt", F32)
    grads["w_in_pad"] = _mm("mix_in_bw", sv["hn"], dproj, "tn", F32)

    def pre_b(xb, dnb, dyb, g):
        dx, dg = _rms_bwd(xb, _rms_stats(xb), g, dnb)
        return dyb + dx, dg

    dx, grads["mix_pre_g"] = _rowwise("mix_pre_b", pre_b, [x, dhn, dy], [w["mix_pre_g"]], [(d, F32)], [(1, d)], tb)
    return dx, grads


def _gdn_pre_b_call(name, fn, proj, dd, conv_w, gones, tb, steps):
    t = proj.shape[0]

    def body(xq, xk, xv, pq, pk, pv, d0, d1, d2, cw, go, oq, ok, ov):
        has_prev = jnp.where(pl.program_id(0) == 0, 0.0, 1.0)
        dd_rows = [jnp.concatenate([dr[h] for h in range(N_HEADS)], axis=1) for dr in (d0, d1, d2)]
        outs = fn(xq[...], xk[...], xv[...], pq[...], pk[...], pv[...], *dd_rows, cw[...], go[...], has_prev)
        for r, val in zip((oq, ok, ov), outs):
            r[...] = val

    return pl.pallas_call(
        body, name=name, grid=(steps,),
        in_specs=_qkv_specs(tb) + [_heads_spec(tb)] * 3 + [pl.BlockSpec(conv_w.shape, lambda i: (0, 0)),
                                                          pl.BlockSpec(gones.shape, lambda i: (0, 0))],
        out_specs=[pl.BlockSpec((tb, GDN_W), lambda i: (i, 0))] * 3,
        out_shape=[jax.ShapeDtypeStruct((t, GDN_W), F32)] * 3,
        compiler_params=_params(("arbitrary",)))(proj, proj, proj, proj, proj, proj, *dd, conv_w, gones)


def _conv_bwd_call(name, proj, dc, conv_w, dproj, tb, steps):
    t = proj.shape[0]
    dcur = [pl.BlockSpec((tb, GDN_W), lambda i: (i, 0))] * 3
    dnext = [pl.BlockSpec((tb, GDN_W), lambda i: (jnp.minimum(i + 1, steps - 1), 0))] * 3

    def body(xq, xk, xv, pq, pk, pv, d0, d1, d2, n0, n1, n2, cw, carry_ref, dx_ref, dw_ref):
        i = pl.program_id(0)
        has_prev = jnp.where(i == 0, 0.0, 1.0)
        has_next = jnp.where(i == steps - 1, 0.0, 1.0)

        @pl.when(i == 0)
        def _():
            dw_ref[...] = jnp.zeros(dw_ref.shape, F32)

        wv = cw[...]
        dws, dxs = [], []
        for j, (xr, pr, dr, nr) in enumerate(((xq, pq, d0, n0), (xk, pk, d1, n1), (xv, pv, d2, n2))):
            x, xp, dcv, dnx = xr[...], pr[...], dr[...], nr[...]
            wj = wv[:, j * GDN_W:(j + 1) * GDN_W]
            row = lax.broadcasted_iota(jnp.int32, x.shape, 0)
            dx = dcv * wj[GDN_CONV - 1:GDN_CONV, :]
            rows_w = [jnp.sum(dcv * x, axis=0, keepdims=True)]
            for s in range(1, GDN_CONV):
                up = jnp.where(row < tb - s, _roll(dcv, tb - s, 0), _roll(dnx, tb - s, 0) * has_next)
                dx = dx + up * wj[GDN_CONV - 1 - s:GDN_CONV - s, :]
                sh = jnp.where(row >= s, _roll(x, s, 0), _roll(xp, s, 0) * has_prev)
                rows_w.append(jnp.sum(dcv * sh, axis=0, keepdims=True))
            dxs.append(dx)
            dws.append(jnp.concatenate(rows_w[::-1], axis=0))
        dx_ref[...] = jnp.concatenate(dxs, axis=1).astype(dx_ref.dtype)
        dw_ref[...] += jnp.concatenate(dws, axis=1)

    return pl.pallas_call(
        body, name=name, grid=(steps,),
        in_specs=_qkv_specs(tb) + dcur + dnext + [pl.BlockSpec(conv_w.shape, lambda i: (0, 0)), ANY_SPEC],
        out_specs=[pl.BlockSpec((tb, 3 * GDN_W), lambda i: (i, PIN_QKV // (3 * GDN_W))),
                   pl.BlockSpec(conv_w.shape, lambda i: (0, 0))],
        out_shape=[jax.ShapeDtypeStruct((t, PIN_W), BF16), jax.ShapeDtypeStruct(conv_w.shape, F32)],
        input_output_aliases={13: 0},
        compiler_params=_params(("arbitrary",)))(proj, proj, proj, proj, proj, proj, *dc, *dc, conv_w, dproj)


def _pad_heads_cols(wm, per_head):
    r = wm.shape[0]
    return jnp.pad(wm.reshape(r, N_HEADS, per_head), ((0, 0), (0, 0), (0, HEAD_LANES - per_head))).reshape(r, MLA_PAD)


def _unpad_heads_cols(wm, per_head):
    r = wm.shape[0]
    return wm.reshape(r, N_HEADS, HEAD_LANES)[:, :, :per_head].reshape(r, N_HEADS * per_head)


def _win_to_pad(wi):
    r = wi.shape[0]
    z = lambda n: jnp.zeros((r, n), wi.dtype)
    o = MLA_Q_RANK + MLA_KV_RANK
    kpe = wi[:, o:o + MLA_ROPE]
    o2 = o + MLA_ROPE
    qkv = wi[:, o2:o2 + 3 * GDN_W]
    o3 = o2 + 3 * GDN_W
    ab = wi[:, o3:o3 + 2 * N_HEADS]
    gate = wi[:, o3 + 2 * N_HEADS:]
    return jnp.concatenate([qkv, wi[:, :o], z(MLA_NOPE), kpe, z(HEAD_LANES - MLA_NOPE - MLA_ROPE), gate, ab,
                            z(HEAD_LANES - 2 * N_HEADS)], axis=1)


def _win_from_pad(wp):
    return jnp.concatenate([wp[:, PIN_MLA:PIN_KPE], wp[:, PIN_KPE + MLA_NOPE:PIN_KPE + MLA_NOPE + MLA_ROPE],
                            wp[:, PIN_QKV:PIN_QKV + 3 * GDN_W], wp[:, PIN_AB:PIN_AB + 2 * N_HEADS],
                            wp[:, PIN_GATE:PIN_GATE + GDN_W]], axis=1)


def _wkv_to_pad(wkv):
    r = wkv.shape[0]
    w3 = wkv.reshape(r, N_HEADS, MLA_NOPE + MLA_V)
    kpart = jnp.pad(w3[:, :, :MLA_NOPE], ((0, 0), (0, 0), (0, HEAD_LANES - MLA_NOPE))).reshape(r, MLA_PAD)
    vpart = jnp.pad(w3[:, :, MLA_NOPE:], ((0, 0), (0, 0), (0, HEAD_LANES - MLA_V))).reshape(r, MLA_PAD)
    return jnp.concatenate([kpart, vpart], axis=1)


def _wkv_from_pad(wp):
    r = wp.shape[0]
    kpart = wp[:, :MLA_PAD].reshape(r, N_HEADS, HEAD_LANES)[:, :, :MLA_NOPE]
    vpart = wp[:, MLA_PAD:].reshape(r, N_HEADS, HEAD_LANES)[:, :, :MLA_V]
    return jnp.concatenate([kpart, vpart], axis=2).reshape(r, N_HEADS * (MLA_NOPE + MLA_V))


def _wout_to_pad(wo):
    n = wo.shape[1]
    mla = jnp.pad(wo[:N_HEADS * MLA_V].reshape(N_HEADS, MLA_V, n), ((0, 0), (0, HEAD_LANES - MLA_V), (0, 0)))
    return jnp.concatenate([mla.reshape(MLA_PAD, n), wo[N_HEADS * MLA_V:]], axis=0)


def _wout_from_pad(wp):
    n = wp.shape[1]
    mla = wp[:MLA_PAD].reshape(N_HEADS, HEAD_LANES, n)[:, :MLA_V].reshape(N_HEADS * MLA_V, n)
    return jnp.concatenate([mla, wp[MLA_PAD:]], axis=0)


def _pad_lanes(v, n):
    return jnp.pad(v, ((0, 0), (0, n - v.shape[1])))


def _compute_weights(full):
    w = {}
    for n in FFN_BIG:
        w[n] = full[n].astype(MM_DTYPE)
    w["w_in_pad"] = _win_to_pad(full["w_in"]).astype(MM_DTYPE)
    w["w_uq_pad"] = _pad_heads_cols(full["mla_w_uq"], MLA_NOPE + MLA_ROPE).astype(MM_DTYPE)
    w["w_kv_pad"] = _wkv_to_pad(full["mla_w_ukv"]).astype(MM_DTYPE)
    w["w_out_pad"] = _wout_to_pad(full["w_out"]).astype(MM_DTYPE)
    w["conv_w"] = full["gdn_conv_w"].astype(F32)
    for n in ("ffn1_pre_g", "ffn1_post_g", "mix_pre_g", "mla_q_norm_g", "mla_kv_norm_g", "gdn_norm_g", "mix_post_g",
              "ffn2_pre_g", "ffn2_post_g"):
        w[n] = full[n]
    w["mla_out_g_pad"] = _pad_heads_cols(full["mla_out_g"], MLA_V)
    w["a_log_pad"] = _pad_lanes(full["gdn_a_log"], HEAD_LANES)
    w["dt_bias_pad"] = _pad_lanes(full["gdn_dt_bias"], HEAD_LANES)
    return w


def _local_step(x, positions, loss_target, full):
    t, d = x.shape
    tb = min(512, t)
    tm = min(1024, t)
    w = _compute_weights(full)
    ffn = lambda tag: (w[tag + "_pre_g"], w[tag + "_w_gate"], w[tag + "_w_up"], w[tag + "_w_down"], w[tag + "_post_g"])
    x1, sv1 = _ffn_fwd("ffn1", x, *ffn("ffn1"), tm)
    x2, svm = _mixer_fwd(x1, positions, w, tb)
    x3, sv2 = _ffn_fwd("ffn2", x2, *ffn("ffn2"), tm)

    def loss_f(yb, tg):
        e = yb - tg
        return e * (1.0 / d), jnp.sum(e * e, axis=0, keepdims=True)

    dy, lsum = _rowwise("loss", loss_f, [x3, loss_target], [], [(d, F32)], [(1, d)], tb)
    g = {}
    dx2, g["ffn2_pre_g"], g["ffn2_w_gate"], g["ffn2_w_up"], g["ffn2_w_down"], g["ffn2_post_g"] = _ffn_bwd(
        "ffn2", dy, sv2, *ffn("ffn2"), tm, tm)
    dx1, gm = _mixer_bwd(dx2, svm, w, tb)
    dx0, g["ffn1_pre_g"], g["ffn1_w_gate"], g["ffn1_w_up"], g["ffn1_w_down"], g["ffn1_post_g"] = _ffn_bwd(
        "ffn1", dx1, sv1, *ffn("ffn1"), tm, tm)
    g["mix_pre_g"], g["mix_post_g"] = gm["mix_pre_g"], gm["mix_post_g"]
    g["mla_q_norm_g"], g["mla_kv_norm_g"] = gm["mla_q_norm_g"], gm["mla_kv_norm_g"]
    g["gdn_norm_g"] = gm["gdn_norm_g"]
    g["w_in"] = _win_from_pad(gm["w_in_pad"])
    g["mla_w_uq"] = _unpad_heads_cols(gm["w_uq_pad"], MLA_NOPE + MLA_ROPE)
    g["mla_w_ukv"] = _wkv_from_pad(gm["w_kv_pad"])
    g["mla_out_g"] = _unpad_heads_cols(gm["mla_out_g_pad"], MLA_V)
    g["gdn_conv_w"] = gm["conv_w"]
    g["gdn_a_log"] = gm["a_log_pad"][:, :N_HEADS]
    g["gdn_dt_bias"] = gm["dt_bias_pad"][:, :N_HEADS]
    g["w_out"] = _wout_from_pad(gm["w_out_pad"])
    return lsum, dx0, g


HBM_SPEC = pl.BlockSpec(memory_space=pltpu.HBM)


def _place():
    return lax.axis_index("x"), lax.axis_index("y"), lax.axis_index("c")


def _exchange_call(name, body, ins, out_shapes, n_remote, n_local):
    return pl.pallas_call(
        body, name=name, in_specs=[HBM_SPEC] * len(ins), out_specs=[HBM_SPEC] * len(out_shapes), out_shape=out_shapes,
        scratch_shapes=[pltpu.SemaphoreType.DMA((n_remote,)), pltpu.SemaphoreType.DMA((n_remote,)),
                        pltpu.SemaphoreType.DMA((n_local,))])(*ins)


def _other_chips(x, y):
    return [(1 - x, y), (x, 1 - y), (1 - x, 1 - y)]


def _at_each_chip(fn):
    x, y, _ = _place()
    for cx in range(2):
        for cy in range(2):
            pl.when((x == cx) & (y == cy))(functools.partial(fn, cx, cy))


def _at_each_device(fn):
    x, y, c = _place()
    for cx in range(2):
        for cy in range(2):
            for cc in range(2):
                pl.when((x == cx) & (y == cy) & (c == cc))(functools.partial(fn, cx, cy, cc))


def _at_each_core(fn):
    c = lax.axis_index("c")
    for cc in range(2):
        pl.when(c == cc)(functools.partial(fn, cc))


def _gather_shards(ws):
    nw = len(ws)

    def body(*refs):
        w_refs, out_refs = refs[:nw], refs[nw:2 * nw]
        send_sems, recv_sems, local_sems = refs[2 * nw:]

        def run(x, y, c):
            chips = _other_chips(x, y)
            me, sibling = 2 * x + y, (x, y, 1 - c)

            def half(ref, which):
                hr = ref.shape[0] // 2
                return ref.at[pl.ds(which * hr, hr)]

            def over_ici(i, j, src, slab, to):
                return pltpu.make_async_remote_copy(
                    src_ref=half(src, c), dst_ref=half(out_refs[i].at[slab], c), send_sem=send_sems.at[6 * i + j],
                    recv_sem=recv_sems.at[6 * i + j], device_id=to, device_id_type=MESH)

            def over_d2d(i, j, slab, which):
                return pltpu.make_async_remote_copy(
                    src_ref=half(out_refs[i].at[slab], which), dst_ref=half(out_refs[i].at[slab], which),
                    send_sem=send_sems.at[6 * i + 3 + j], recv_sem=recv_sems.at[6 * i + 3 + j], device_id=sibling,
                    device_id_type=MESH)

            local, sends, passed = [], [], []
            for i, w_ref in enumerate(w_refs):
                local.append(pltpu.make_async_copy(w_ref, out_refs[i].at[me], local_sems.at[i]))
                local[-1].start()
                for j, (px, py) in enumerate(chips):
                    sends.append(over_ici(i, j, w_ref, me, (px, py, c)))
                    sends[-1].start()
            for i, w_ref in enumerate(w_refs):
                for j, (px, py) in enumerate(chips):
                    over_ici(i, j, w_ref, 2 * px + py, (px, py, c)).wait_recv()
                    passed.append(over_d2d(i, j, 2 * px + py, c))
                    passed[-1].start()
            for i in range(nw):
                for j, (px, py) in enumerate(chips):
                    over_d2d(i, j, 2 * px + py, 1 - c).wait_recv()
            for cp in sends + passed:
                cp.wait_send()
            for cp in local:
                cp.wait()

        _at_each_device(run)

    outs = [jax.ShapeDtypeStruct((N_SHARD,) + w.shape, w.dtype) for w in ws]
    return _exchange_call("gather_weight_shards", body, ws, outs, 6 * nw, nw)


def _swap_halves(gs):
    ng = len(gs)

    def body(*refs):
        g_refs, keep_refs, got_refs = refs[:ng], refs[ng:2 * ng], refs[2 * ng:3 * ng]
        send_sems, recv_sems, local_sems = refs[3 * ng:]
        x, y, _ = _place()

        def run(c):
            local, sends = [], []
            for i, (g_ref, keep_ref, got_ref) in enumerate(zip(g_refs, keep_refs, got_refs)):
                hr = keep_ref.shape[1]
                local.append(pltpu.make_async_copy(g_ref.at[:, pl.ds(c * hr, hr)], keep_ref, local_sems.at[i]))
                local[-1].start()
                sends.append(pltpu.make_async_remote_copy(
                    src_ref=g_ref.at[:, pl.ds((1 - c) * hr, hr)], dst_ref=got_ref, send_sem=send_sems.at[i],
                    recv_sem=recv_sems.at[i], device_id=(x, y, 1 - c), device_id_type=MESH))
                sends[-1].start()
            for cp in sends:
                cp.wait()
            for cp in local:
                cp.wait()

        _at_each_core(run)

    halves = [jax.ShapeDtypeStruct((g.shape[0], g.shape[1] // 2, g.shape[2]), g.dtype) for g in gs]
    outs = _exchange_call("swap_grad_halves", body, gs, halves + halves, ng, ng)
    return outs[:ng], outs[ng:]


def _scatter_to_chips(ps):
    n = len(ps)

    def body(*refs):
        p_refs, out_refs = refs[:n], refs[n:2 * n]
        send_sems, recv_sems, local_sems = refs[2 * n:]
        c = lax.axis_index("c")

        def run(x, y):
            me = 2 * x + y
            chips = _other_chips(x, y)
            local, sends = [], []
            for i, (p_ref, out_ref) in enumerate(zip(p_refs, out_refs)):
                local.append(pltpu.make_async_copy(p_ref.at[me], out_ref.at[me], local_sems.at[i]))
                local[-1].start()
                for j, (px, py) in enumerate(chips):
                    sends.append(pltpu.make_async_remote_copy(
                        src_ref=p_ref.at[2 * px + py], dst_ref=out_ref.at[me], send_sem=send_sems.at[3 * i + j],
                        recv_sem=recv_sems.at[3 * i + j], device_id=(px, py, c), device_id_type=MESH))
                    sends[-1].start()
            for i, (p_ref, out_ref) in enumerate(zip(p_refs, out_refs)):
                for j, (px, py) in enumerate(chips):
                    pltpu.make_async_remote_copy(
                        src_ref=p_ref.at[me], dst_ref=out_ref.at[2 * px + py], send_sem=send_sems.at[3 * i + j],
                        recv_sem=recv_sems.at[3 * i + j], device_id=(px, py, c), device_id_type=MESH).wait_recv()
            for cp in sends:
                cp.wait_send()
            for cp in local:
                cp.wait()

        _at_each_chip(run)

    outs = [jax.ShapeDtypeStruct(p.shape, p.dtype) for p in ps]
    return _exchange_call("scatter_grad_quarters", body, ps, outs, 3 * n, n)


def _share_halves(hs):
    n = len(hs)

    def body(*refs):
        h_refs, out_refs = refs[:n], refs[n:2 * n]
        send_sems, recv_sems, local_sems = refs[2 * n:]
        x, y, _ = _place()

        def run(c):
            local, sends = [], []
            for i, (h_ref, out_ref) in enumerate(zip(h_refs, out_refs)):
                local.append(pltpu.make_async_copy(h_ref, out_ref.at[c], local_sems.at[i]))
                local[-1].start()
                sends.append(pltpu.make_async_remote_copy(
                    src_ref=h_ref, dst_ref=out_ref.at[c], send_sem=send_sems.at[i], recv_sem=recv_sems.at[i],
                    device_id=(x, y, 1 - c), device_id_type=MESH))
                sends[-1].start()
            for i, (h_ref, out_ref) in enumerate(zip(h_refs, out_refs)):
                pltpu.make_async_remote_copy(
                    src_ref=h_ref, dst_ref=out_ref.at[1 - c], send_sem=send_sems.at[i], recv_sem=recv_sems.at[i],
                    device_id=(x, y, 1 - c), device_id_type=MESH).wait_recv()
            for cp in sends:
                cp.wait_send()
            for cp in local:
                cp.wait()

        _at_each_core(run)

    outs = [jax.ShapeDtypeStruct((2,) + h.shape, h.dtype) for h in hs]
    return _exchange_call("share_grad_halves", body, hs, outs, n, n)


def _gather_small(sp):
    def body(s_ref, out_ref, send_sems, recv_sems, local_sem):
        x, y, c = _place()
        me = 4 * x + 2 * y + c
        peers = [(x ^ (m >> 2), y ^ ((m >> 1) & 1), c ^ (m & 1)) for m in range(1, 8)]
        mine = pltpu.make_async_copy(s_ref, out_ref.at[me], local_sem)
        mine.start()
        sends = [pltpu.make_async_remote_copy(src_ref=s_ref, dst_ref=out_ref.at[me], send_sem=send_sems.at[j],
                                              recv_sem=recv_sems.at[j], device_id=p, device_id_type=MESH)
                 for j, p in enumerate(peers)]
        for cp in sends:
            cp.start()
        for j, (px, py, pc) in enumerate(peers):
            pltpu.make_async_remote_copy(src_ref=s_ref, dst_ref=out_ref.at[4 * px + 2 * py + pc],
                                         send_sem=send_sems.at[j], recv_sem=recv_sems.at[j], device_id=(px, py, pc),
                                         device_id_type=MESH).wait_recv()
        for cp in sends:
            cp.wait_send()
        mine.wait()

    return pl.pallas_call(
        body, name="gather_small_grads", in_specs=[HBM_SPEC], out_specs=HBM_SPEC,
        out_shape=jax.ShapeDtypeStruct((8,) + sp.shape, sp.dtype),
        scratch_shapes=[pltpu.SemaphoreType.DMA((7,)), pltpu.SemaphoreType.DMA((7,)), pltpu.SemaphoreType.DMA])(sp)


def _pack_rows(total):
    rows = -(-total // LANES)
    return -(-rows // 32) * 32


def _pack(arrs, dtype):
    flat = jnp.concatenate([a.reshape(-1).astype(dtype) for a in arrs])
    rows = _pack_rows(flat.shape[0])
    return jnp.pad(flat, (0, rows * LANES - flat.shape[0])).reshape(rows, LANES)


def _unpack(buf, shapes):
    flat = buf.reshape(-1)
    out, off = {}, 0
    for n, shp in shapes:
        size = shp[0] * shp[1]
        out[n] = flat[off:off + size].reshape(shp)
        off += size
    return out


def _adamw(name, wv, g, m, v, tb):
    c1 = 1.0 - ADAM_B1 ** ADAM_STEP
    c2 = 1.0 - ADAM_B2 ** ADAM_STEP

    def fn(wb, gb, mb, vb):
        m2 = ADAM_B1 * mb + (1.0 - ADAM_B1) * gb
        v2 = ADAM_B2 * vb + (1.0 - ADAM_B2) * (gb * gb)
        delta = -ADAM_LR * ((m2 / c1) / (jnp.sqrt(v2 / c2) + ADAM_EPS) + ADAM_WD * wb)
        return delta, m2, v2

    cols = wv.shape[1]
    return _rowwise(name, fn, [wv, g, m, v], [], [(cols, F32)] * 3, [], tb)


def _row_tile(rows, pref):
    if rows <= pref:
        return rows
    t = pref
    while t >= 8:
        if rows % t == 0 and t % 8 == 0:
            return t
        t -= 8
    return rows


def kernel(x, positions, ffn1_pre_g, ffn1_w_gate, ffn1_w_up, ffn1_w_down, ffn1_post_g, mix_pre_g, w_in, mla_q_norm_g, mla_w_uq, mla_kv_norm_g, mla_w_ukv, mla_out_g, gdn_conv_w, gdn_a_log, gdn_dt_bias, gdn_norm_g, w_out, mix_post_g, ffn2_pre_g, ffn2_w_gate, ffn2_w_up, ffn2_w_down, ffn2_post_g, loss_target, m_ffn1_pre_g, m_ffn1_w_gate, m_ffn1_w_up, m_ffn1_w_down, m_ffn1_post_g, m_mix_pre_g, m_w_in, m_mla_q_norm_g, m_mla_w_uq, m_mla_kv_norm_g, m_mla_w_ukv, m_mla_out_g, m_gdn_conv_w, m_gdn_a_log, m_gdn_dt_bias, m_gdn_norm_g, m_w_out, m_mix_post_g, m_ffn2_pre_g, m_ffn2_w_gate, m_ffn2_w_up, m_ffn2_w_down, m_ffn2_post_g, v_ffn1_pre_g, v_ffn1_w_gate, v_ffn1_w_up, v_ffn1_w_down, v_ffn1_post_g, v_mix_pre_g, v_w_in, v_mla_q_norm_g, v_mla_w_uq, v_mla_kv_norm_g, v_mla_w_ukv, v_mla_out_g, v_gdn_conv_w, v_gdn_a_log, v_gdn_dt_bias, v_gdn_norm_g, v_w_out, v_mix_post_g, v_ffn2_pre_g, v_ffn2_w_gate, v_ffn2_w_up, v_ffn2_w_down, v_ffn2_post_g):
    args = dict(locals())
    wsh = {n: args[n][0] for n in WEIGHTS}
    msh = {n: args["m_" + n][0] if args["m_" + n].ndim == 3 else args["m_" + n] for n in WEIGHTS}
    vsh = {n: args["v_" + n][0] if args["v_" + n].ndim == 3 else args["v_" + n] for n in WEIGHTS}
    for n in SMALL:
        wsh[n] = args[n]
    mix_shapes = [(n, wsh[n].shape) for n in MIX_BIG]

    gathered = _gather_shards([wsh[n].astype(MM_DTYPE) for n in FFN_BIG] + [_pack([wsh[n] for n in MIX_BIG], MM_DTYPE)])
    full = {n: wsh[n] for n in SMALL}
    for n, gw in zip(FFN_BIG, gathered):
        full[n] = gw
    parts = [_unpack(gathered[-1][q], mix_shapes) for q in range(N_SHARD)]
    for n in MIX_BIG:
        full[n] = jnp.concatenate([parts[q][n] for q in range(N_SHARD)], axis=SHARD_AXIS[n])

    lsum, grad_x, g = _local_step(x[0], positions, loss_target[0], full)
    loss = lax.psum(0.5 * jnp.sum(lsum) / x.shape[-1], ("x", "y", "c"))

    mix_quarters = [_pack([jnp.split(g[n], N_SHARD, axis=SHARD_AXIS[n])[q] for n in MIX_BIG], MM_DTYPE)
                    for q in range(N_SHARD)]
    keep, got = _swap_halves([g[n] for n in FFN_BIG] + [jnp.stack(mix_quarters)])
    pairs = []
    for i, (kp, gt) in enumerate(zip(keep, got)):
        ns_, hr, cols = kp.shape
        (pr,) = _rowwise("add_pair_%d" % i, lambda a, b: (a.astype(F32) + b.astype(F32),),
                         [kp.reshape(ns_ * hr, cols), gt.reshape(ns_ * hr, cols)], [], [(cols, MM_DTYPE)], [],
                         _row_tile(ns_ * hr, 512))
        pairs.append(pr.reshape(ns_, hr, cols))
    slabs = _scatter_to_chips(pairs)
    halves = []
    for i, sl in enumerate(slabs):
        ns_, hr, cols = sl.shape
        th = _row_tile(hr, 512)
        (hs,) = _rowwise("add_chips_%d" % i,
                         lambda a, b, cc, dd: (((a.astype(F32) + b.astype(F32)) + cc.astype(F32)) + dd.astype(F32),),
                         [(sl.reshape(ns_ * hr, cols), cols, 0, q * (hr // th)) for q in range(N_SHARD)], [],
                         [(cols, F32, hr)], [], th)
        halves.append(hs)
    shared = [s2.reshape(2 * s2.shape[1], s2.shape[2]) for s2 in _share_halves(halves)]
    gsh = _unpack(shared[-1], mix_shapes)
    for n, sg_ in zip(FFN_BIG, shared):
        gsh[n] = sg_

    small_shapes = [(n, wsh[n].shape) for n in SMALL]
    pack_small = lambda d: jnp.concatenate(
        [_pad_lanes(d[n].astype(F32), LANES) for n in SMALL] + [jnp.zeros((SMALL_ROWS - len(SMALL), LANES), F32)], axis=0)
    slots = _gather_small(pack_small(g))

    c1 = 1.0 - ADAM_B1 ** ADAM_STEP
    c2 = 1.0 - ADAM_B2 ** ADAM_STEP

    def small_update(wb, mb, vb, s8):
        gs = s8[0:SMALL_ROWS]
        for d in range(1, 8):
            gs = gs + s8[d * SMALL_ROWS:(d + 1) * SMALL_ROWS]
        m2 = ADAM_B1 * mb + (1.0 - ADAM_B1) * gs
        v2 = ADAM_B2 * vb + (1.0 - ADAM_B2) * (gs * gs)
        delta = -ADAM_LR * ((m2 / c1) / (jnp.sqrt(v2 / c2) + ADAM_EPS) + ADAM_WD * wb)
        return gs, delta, m2, v2

    sg, sd, sm, sv_ = _rowwise("adamw_small", small_update,
                               [pack_small(wsh), pack_small(msh), pack_small(vsh)],
                               [slots.reshape(8 * SMALL_ROWS, LANES)], [(LANES, F32)] * 4, [], SMALL_ROWS)
    grads, deltas, new_m, new_v = {}, {}, {}, {}
    for i, (n, shp) in enumerate(small_shapes):
        grads[n], deltas[n] = sg[i:i + 1, :shp[1]], sd[i:i + 1, :shp[1]]
        new_m[n], new_v[n] = sm[i:i + 1, :shp[1]], sv_[i:i + 1, :shp[1]]
    for n in BIG:
        grads[n] = gsh[n]
        r = wsh[n].shape[0]
        deltas[n], new_m[n], new_v[n] = _adamw("adamw_" + n, wsh[n], gsh[n], msh[n], vsh[n], _row_tile(r, 256))

    def shaped(d, n):
        return d[n][None] if n in BIG else d[n]

    return (loss, grad_x[None], *[shaped(grads, n) for n in WEIGHTS], *[shaped(deltas, n) for n in WEIGHTS],
            *[shaped(new_m, n) for n in WEIGHTS], *[shaped(new_v, n) for n in WEIGHTS])
```

```python
import functools

import jax
import jax.numpy as jnp
from jax import lax
from jax.experimental import pallas as pl
from jax.experimental.pallas import tpu as pltpu

F32 = jnp.float32
BF16 = jnp.bfloat16
MM_DTYPE = BF16
HI = lax.Precision.HIGHEST
MESH = pl.DeviceIdType.MESH

D_MODEL = 1024
D_FF = 2816
N_HEADS = 8
MLA_Q_RANK = 256
MLA_KV_RANK = 128
MLA_NOPE = 64
MLA_ROPE = 32
MLA_V = 64
ROPE_THETA = 10000.0
GDN_DH = 64
GDN_W = N_HEADS * GDN_DH
GDN_CONV = 4
CHUNK = 64
HEAD_LANES = 128
MLA_PAD = N_HEADS * HEAD_LANES
EPS = 1e-6
N_SHARD = 4
LANES = 1024

PIN_QKV = 0
PIN_MLA = 1536
PIN_KPE = 1920
PIN_GATE = 2048
PIN_AB = 2560
PIN_W = 2688
CAT_W = MLA_PAD + GDN_W

ADAM_LR = 0.001
ADAM_B1 = 0.9
ADAM_B2 = 0.999
ADAM_EPS = 1e-08
ADAM_WD = 0.01
ADAM_STEP = 10

VMEM_LIMIT_V7X = 56 * 1024 * 1024

BIG = ["ffn1_w_gate", "ffn1_w_up", "ffn1_w_down", "w_in", "mla_w_uq", "mla_w_ukv", "gdn_conv_w", "w_out",
       "ffn2_w_gate", "ffn2_w_up", "ffn2_w_down"]
FFN_BIG = ["ffn1_w_gate", "ffn1_w_up", "ffn1_w_down", "ffn2_w_gate", "ffn2_w_up", "ffn2_w_down"]
MIX_BIG = ["w_in", "mla_w_uq", "mla_w_ukv", "gdn_conv_w", "w_out"]
SMALL = ["ffn1_pre_g", "ffn1_post_g", "mix_pre_g", "mla_q_norm_g", "mla_kv_norm_g", "mla_out_g", "gdn_a_log",
         "gdn_dt_bias", "gdn_norm_g", "mix_post_g", "ffn2_pre_g", "ffn2_post_g"]
WEIGHTS = ["ffn1_pre_g", "ffn1_w_gate", "ffn1_w_up", "ffn1_w_down", "ffn1_post_g", "mix_pre_g", "w_in",
           "mla_q_norm_g", "mla_w_uq", "mla_kv_norm_g", "mla_w_ukv", "mla_out_g", "gdn_conv_w", "gdn_a_log",
           "gdn_dt_bias", "gdn_norm_g", "w_out", "mix_post_g", "ffn2_pre_g", "ffn2_w_gate", "ffn2_w_up",
           "ffn2_w_down", "ffn2_post_g"]
SHARD_AXIS = {"ffn1_w_gate": 1, "ffn1_w_up": 1, "ffn1_w_down": 0, "w_in": 1, "mla_w_uq": 1, "mla_w_ukv": 1,
              "gdn_conv_w": 1, "w_out": 0, "ffn2_w_gate": 1, "ffn2_w_up": 1, "ffn2_w_down": 0}
SMALL_ROWS = 16


def _params(sem):
    return pltpu.CompilerParams(dimension_semantics=sem, vmem_limit_bytes=VMEM_LIMIT_V7X)


def _pick(dim, pref):
    if dim <= pref:
        return dim
    t = (pref // 128) * 128
    while t >= 128:
        if dim % t == 0:
            return t
        t -= 128
    return dim


ANY_SPEC = pl.BlockSpec(memory_space=pl.ANY)


def _rowwise(name, fn, row_ins, bc_ins, row_outs, acc_outs, tb, wide=None, carry=None):
    ents = []
    for e in row_ins:
        ents.append(e if isinstance(e, tuple) else (e, e.shape[1], 0, 0))
    over = [o[2] for o in row_outs if len(o) == 3]
    rows = over[0] if over else ents[0][0].shape[0]
    steps = rows // tb
    assert steps * tb == rows, (name, rows, tb)
    in_specs, args = [], []
    for a, w, j, r0 in ents:
        in_specs.append(pl.BlockSpec((tb, w), lambda i, j=j, r0=r0: (i + r0, j)))
        args.append(a)
    for b in bc_ins:
        in_specs.append(pl.BlockSpec(b.shape, lambda i: (0, 0)))
        args.append(b)
    n_in = len(args)
    aliases = {}
    if carry is not None:
        in_specs.append(ANY_SPEC)
        args.append(carry)
        aliases = {n_in: 0}
    out_shape = [jax.ShapeDtypeStruct((rows, o[0]), o[1]) for o in row_outs]
    out_specs = [pl.BlockSpec((tb, o[0]), lambda i: (i, 0)) for o in row_outs]
    if wide is not None:
        out_shape[0] = jax.ShapeDtypeStruct((rows, wide[0]), row_outs[0][1])
        out_specs[0] = pl.BlockSpec((tb, row_outs[0][0]), lambda i: (i, wide[1]))
    out_shape += [jax.ShapeDtypeStruct((r, c), F32) for r, c in acc_outs]
    out_specs += [pl.BlockSpec((r, c), lambda i: (0, 0)) for r, c in acc_outs]
    n_ro, n_acc, n_args = len(row_outs), len(acc_outs), len(args)

    def body(*refs):
        vals = fn(*[r[...] for r in refs[:n_in]])
        if not isinstance(vals, (tuple, list)):
            vals = (vals,)
        for r, v in zip(refs[n_args:n_args + n_ro], vals[:n_ro]):
            r[...] = v.astype(r.dtype)
        if n_acc:
            acc_refs = refs[n_args + n_ro:]

            @pl.when(pl.program_id(0) == 0)
            def _():
                for r in acc_refs:
                    r[...] = jnp.zeros(r.shape, r.dtype)

            for r, v in zip(acc_refs, vals[n_ro:]):
                r[...] += v

    outs = pl.pallas_call(body, name=name, grid=(steps,), in_specs=in_specs, out_specs=out_specs,
                          out_shape=out_shape, input_output_aliases=aliases,
                          compiler_params=_params(("arbitrary",)))(*args)
    return outs


def _mm(name, a, b, mode, out_dtype, tm=512, tn=512, tk=1024):
    if mode == "nn":
        (m, k), (k2, n) = a.shape, b.shape
    elif mode == "nt":
        (m, k), (n, k2) = a.shape, b.shape
    else:
        (k, m), (k2, n) = a.shape, b.shape
    assert k == k2, (name, a.shape, b.shape)
    tm, tn, tk = _pick(m, tm), _pick(n, tn), _pick(k, tk)
    nk = k // tk
    if mode == "nn":
        a_spec = pl.BlockSpec((tm, tk), lambda i, j, kk: (i, kk))
        b_spec = pl.BlockSpec((tk, tn), lambda i, j, kk: (kk, j))
        dims = (((1,), (0,)), ((), ()))
    elif mode == "nt":
        a_spec = pl.BlockSpec((tm, tk), lambda i, j, kk: (i, kk))
        b_spec = pl.BlockSpec((tn, tk), lambda i, j, kk: (j, kk))
        dims = (((1,), (1,)), ((), ()))
    else:
        a_spec = pl.BlockSpec((tk, tm), lambda i, j, kk: (kk, i))
        b_spec = pl.BlockSpec((tk, tn), lambda i, j, kk: (kk, j))
        dims = (((0,), (0,)), ((), ()))

    def body(a_ref, b_ref, o_ref, acc_ref):
        kk = pl.program_id(2)

        @pl.when(kk == 0)
        def _():
            acc_ref[...] = jnp.zeros(acc_ref.shape, F32)

        acc_ref[...] += lax.dot_general(a_ref[...].astype(MM_DTYPE), b_ref[...].astype(MM_DTYPE), dims,
                                        preferred_element_type=F32)

        @pl.when(kk == nk - 1)
        def _():
            o_ref[...] = acc_ref[...].astype(o_ref.dtype)

    return pl.pallas_call(
        body, name=name, grid=(m // tm, n // tn, nk), in_specs=[a_spec, b_spec],
        out_specs=pl.BlockSpec((tm, tn), lambda i, j, kk: (i, j)),
        out_shape=jax.ShapeDtypeStruct((m, n), out_dtype),
        scratch_shapes=[pltpu.VMEM((tm, tn), F32)],
        compiler_params=_params(("parallel", "parallel", "arbitrary")))(a, b)


def _rms_stats(x, n_real=None):
    n = x.shape[-1] if n_real is None else n_real
    return lax.rsqrt(jnp.sum(x * x, axis=-1, keepdims=True) / n + EPS)


def _rms_bwd(x, r, g, dz, n_real=None):
    n = x.shape[-1] if n_real is None else n_real
    xh = x * r
    dxh = dz * g
    dx = r * (dxh - xh * (jnp.sum(dxh * xh, axis=-1, keepdims=True) / n))
    return dx, jnp.sum(dz * xh, axis=0, keepdims=True)


def _sigmoid(x):
    return 1.0 / (1.0 + jnp.exp(-x))


def _roll(x, s, axis):
    return pltpu.roll(x, s, axis)


def _rope(x, c, s1, s2):
    return x * c + _roll(x, HEAD_LANES - MLA_ROPE // 2, 1) * s1 + _roll(x, MLA_ROPE // 2, 1) * s2


def _heads_apply(x, fn):
    return jnp.concatenate([fn(x[:, h * HEAD_LANES:(h + 1) * HEAD_LANES]) for h in range(N_HEADS)], axis=1)


def _ffn_fwd(tag, x, g_pre, wg, wu, wd, g_post, tm):
    t, d = x.shape
    ns, _, fs = wg.shape
    nt = t // tm
    row = pl.BlockSpec((tm, d), lambda i, q: (i, 0))
    vec = pl.BlockSpec((1, d), lambda i, q: (0, 0))
    act3 = pl.BlockSpec((1, tm, fs), lambda i, q: (q, i, 0))
    wcol = pl.BlockSpec((1, d, fs), lambda i, q: (q, 0, 0))
    wrow = pl.BlockSpec((1, fs, d), lambda i, q: (q, 0, 0))

    def gate_up(x_ref, g_ref, wg_ref, wu_ref, n_ref, a_ref, u_ref, s_ref, n_s):
        @pl.when(pl.program_id(1) == 0)
        def _():
            xb = x_ref[...]
            n_s[...] = (xb * _rms_stats(xb) * g_ref[...]).astype(MM_DTYPE)
            n_ref[...] = n_s[...]

        n = n_s[...]
        a = jnp.dot(n, wg_ref[0], preferred_element_type=F32)
        u = jnp.dot(n, wu_ref[0], preferred_element_type=F32)
        a_ref[0] = a.astype(a_ref.dtype)
        u_ref[0] = u.astype(u_ref.dtype)
        s_ref[0] = ((a * _sigmoid(a)) * u).astype(s_ref.dtype)

    n, a, u, s = pl.pallas_call(
        gate_up, name=tag + "_gate_up", grid=(nt, ns), in_specs=[row, vec, wcol, wcol],
        out_specs=[row, act3, act3, act3],
        out_shape=[jax.ShapeDtypeStruct((t, d), MM_DTYPE)] + [jax.ShapeDtypeStruct((ns, t, fs), MM_DTYPE)] * 3,
        scratch_shapes=[pltpu.VMEM((tm, d), MM_DTYPE)],
        compiler_params=_params(("parallel", "arbitrary")))(x, g_pre, wg, wu)

    def down(s_ref, wd_ref, x_ref, g_ref, h_ref, y_ref, acc):
        q = pl.program_id(1)

        @pl.when(q == 0)
        def _():
            acc[...] = jnp.zeros(acc.shape, F32)

        acc[...] += jnp.dot(s_ref[0], wd_ref[0], preferred_element_type=F32)

        @pl.when(q == ns - 1)
        def _():
            hb = acc[...]
            h_ref[...] = hb
            y_ref[...] = x_ref[...] + 0.5 * (hb * _rms_stats(hb) * g_ref[...])

    h, y = pl.pallas_call(
        down, name=tag + "_down", grid=(nt, ns), in_specs=[act3, wrow, row, vec], out_specs=[row, row],
        out_shape=[jax.ShapeDtypeStruct((t, d), F32)] * 2, scratch_shapes=[pltpu.VMEM((tm, d), F32)],
        compiler_params=_params(("parallel", "arbitrary")))(s, wd, x, g_post)
    return y, (x, n, a, u, s, h)


def _ffn_bwd(tag, dy, saved, g_pre, wg, wu, wd, g_post, tm, tk):
    x, n, a, u, s, h = saved
    t, d = x.shape
    ns, _, fs = wg.shape
    nt, nk = t // tm, t // tk
    row = pl.BlockSpec((tm, d), lambda i, q: (i, 0))
    vec = pl.BlockSpec((1, d), lambda i, q: (0, 0))
    act3 = pl.BlockSpec((1, tm, fs), lambda i, q: (q, i, 0))
    wcol = pl.BlockSpec((1, d, fs), lambda i, q: (q, 0, 0))
    wrow = pl.BlockSpec((1, fs, d), lambda i, q: (q, 0, 0))
    nt_dims = (((1,), (1,)), ((), ()))
    tn_dims = (((0,), (0,)), ((), ()))

    def down_b(h_ref, dy_ref, g_ref, wd_ref, a_ref, u_ref, dh_ref, da_ref, du_ref, dg_ref, dh_s):
        i, q = pl.program_id(0), pl.program_id(1)

        @pl.when((i == 0) & (q == 0))
        def _():
            dg_ref[...] = jnp.zeros(dg_ref.shape, F32)

        @pl.when(q == 0)
        def _():
            hb = h_ref[...]
            dh, dg = _rms_bwd(hb, _rms_stats(hb), g_ref[...], 0.5 * dy_ref[...])
            dh_s[...] = dh.astype(MM_DTYPE)
            dh_ref[...] = dh_s[...]
            dg_ref[...] += dg

        ds = lax.dot_general(dh_s[...], wd_ref[0], nt_dims, preferred_element_type=F32)
        ab, ub = a_ref[0].astype(F32), u_ref[0].astype(F32)
        sg = _sigmoid(ab)
        da_ref[0] = (ds * ub * (sg * (1.0 + ab * (1.0 - sg)))).astype(da_ref.dtype)
        du_ref[0] = (ds * (ab * sg)).astype(du_ref.dtype)

    dh, da, du, dg_post = pl.pallas_call(
        down_b, name=tag + "_down_b", grid=(nt, ns), in_specs=[row, row, vec, wrow, act3, act3],
        out_specs=[row, act3, act3, vec],
        out_shape=[jax.ShapeDtypeStruct((t, d), MM_DTYPE)] + [jax.ShapeDtypeStruct((ns, t, fs), MM_DTYPE)] * 2
        + [jax.ShapeDtypeStruct((1, d), F32)],
        scratch_shapes=[pltpu.VMEM((tm, d), MM_DTYPE)],
        compiler_params=_params(("arbitrary", "arbitrary")))(h, dy, g_post, wd, a, u)

    def down_w(s_ref, dh_ref, dw_ref, acc):
        kk = pl.program_id(1)

        @pl.when(kk == 0)
        def _():
            acc[...] = jnp.zeros(acc.shape, F32)

        acc[...] += lax.dot_general(s_ref[0], dh_ref[...], tn_dims, preferred_element_type=F32)

        @pl.when(kk == nk - 1)
        def _():
            dw_ref[0] = acc[...].astype(dw_ref.dtype)

    dwd = pl.pallas_call(
        down_w, name=tag + "_down_w", grid=(ns, nk),
        in_specs=[pl.BlockSpec((1, tk, fs), lambda q, kk: (q, kk, 0)), pl.BlockSpec((tk, d), lambda q, kk: (kk, 0))],
        out_specs=pl.BlockSpec((1, fs, d), lambda q, kk: (q, 0, 0)),
        out_shape=jax.ShapeDtypeStruct((ns, fs, d), MM_DTYPE), scratch_shapes=[pltpu.VMEM((fs, d), F32)],
        compiler_params=_params(("parallel", "arbitrary")))(s, dh)

    def gate_up_b(da_ref, du_ref, wg_ref, wu_ref, x_ref, dy_ref, g_ref, dx_ref, dg_ref, acc):
        i, q = pl.program_id(0), pl.program_id(1)

        @pl.when((i == 0) & (q == 0))
        def _():
            dg_ref[...] = jnp.zeros(dg_ref.shape, F32)

        @pl.when(q == 0)
        def _():
            acc[...] = jnp.zeros(acc.shape, F32)

        acc[...] += (lax.dot_general(da_ref[0], wg_ref[0], nt_dims, preferred_element_type=F32)
                     + lax.dot_general(du_ref[0], wu_ref[0], nt_dims, preferred_element_type=F32))

        @pl.when(q == ns - 1)
        def _():
            xb = x_ref[...]
            dx, dg = _rms_bwd(xb, _rms_stats(xb), g_ref[...], acc[...])
            dx_ref[...] = dy_ref[...] + dx
            dg_ref[...] += dg

    dx, dg_pre = pl.pallas_call(
        gate_up_b, name=tag + "_gate_up_b", grid=(nt, ns), in_specs=[act3, act3, wcol, wcol, row, row, vec],
        out_specs=[row, vec], out_shape=[jax.ShapeDtypeStruct((t, d), F32), jax.ShapeDtypeStruct((1, d), F32)],
        scratch_shapes=[pltpu.VMEM((tm, d), F32)],
        compiler_params=_params(("arbitrary", "arbitrary")))(da, du, wg, wu, x, dy, g_pre)

    def gate_up_w(n_ref, da_ref, du_ref, dwg_ref, dwu_ref, acc_g, acc_u):
        kk = pl.program_id(1)

        @pl.when(kk == 0)
        def _():
            acc_g[...] = jnp.zeros(acc_g.shape, F32)
            acc_u[...] = jnp.zeros(acc_u.shape, F32)

        nb = n_ref[...]
        acc_g[...] += lax.dot_general(nb, da_ref[0], tn_dims, preferred_element_type=F32)
        acc_u[...] += lax.dot_general(nb, du_ref[0], tn_dims, preferred_element_type=F32)

        @pl.when(kk == nk - 1)
        def _():
            dwg_ref[0] = acc_g[...].astype(dwg_ref.dtype)
            dwu_ref[0] = acc_u[...].astype(dwu_ref.dtype)

    k3 = pl.BlockSpec((1, tk, fs), lambda q, kk: (q, kk, 0))
    wout = pl.BlockSpec((1, d, fs), lambda q, kk: (q, 0, 0))
    dwg, dwu = pl.pallas_call(
        gate_up_w, name=tag + "_gate_up_w", grid=(ns, nk),
        in_specs=[pl.BlockSpec((tk, d), lambda q, kk: (kk, 0)), k3, k3], out_specs=[wout, wout],
        out_shape=[jax.ShapeDtypeStruct((ns, d, fs), MM_DTYPE)] * 2,
        scratch_shapes=[pltpu.VMEM((d, fs), F32)] * 2,
        compiler_params=_params(("parallel", "arbitrary")))(n, da, du)
    return dx, dg_pre, dwg, dwu, dwd, dg_post


NEG = -1e30


def _attn_scale():
    return (MLA_NOPE + MLA_ROPE) ** -0.5


def _causal_pairs(nq, by_key):
    if by_key:
        pairs = [(qi, ki) for ki in range(nq) for qi in range(ki, nq)]
    else:
        pairs = [(qi, ki) for qi in range(nq) for ki in range(qi + 1)]
    return jnp.asarray([p[0] for p in pairs], jnp.int32), jnp.asarray([p[1] for p in pairs], jnp.int32)


def _below_diagonal(shape):
    return lax.broadcasted_iota(jnp.int32, shape, 1) <= lax.broadcasted_iota(jnp.int32, shape, 0)


def _attn_call(name, body, tables, args, in_kinds, out_kinds, scratch, t, tq):
    qmap = lambda h, p, qt, kt: (qt[p], h)
    kmap = lambda h, p, qt, kt: (kt[p], h)
    spec = lambda kind: pl.BlockSpec((tq, HEAD_LANES), qmap if kind == "q" else kmap)
    grid_spec = pltpu.PrefetchScalarGridSpec(
        num_scalar_prefetch=2, grid=(N_HEADS, tables[0].shape[0]), in_specs=[spec(kd) for kd in in_kinds],
        out_specs=[spec(kd) for kd in out_kinds], scratch_shapes=scratch)
    return pl.pallas_call(body, name=name, grid_spec=grid_spec,
                          out_shape=[jax.ShapeDtypeStruct((t, MLA_PAD), F32) for _ in out_kinds],
                          compiler_params=_params(("parallel", "arbitrary")))(*tables, *args)


def _attn_fwd(q, k, v, tq):
    t = q.shape[0]
    nq = t // tq

    def body(qt, kt, q_ref, k_ref, v_ref, o_ref, lse_ref, m_s, l_s, acc_s):
        p_id = pl.program_id(1)
        qi, ki = qt[p_id], kt[p_id]

        @pl.when(ki == 0)
        def _():
            m_s[...] = jnp.full(m_s.shape, NEG, F32)
            l_s[...] = jnp.zeros(l_s.shape, F32)
            acc_s[...] = jnp.zeros(acc_s.shape, F32)

        def scores():
            return lax.dot_general(q_ref[...], k_ref[...], (((1,), (1,)), ((), ())), preferred_element_type=F32)

        def update(s):
            m_old = m_s[...]
            m_new = jnp.maximum(m_old, jnp.max(s, axis=1, keepdims=True))
            alpha = jnp.exp(m_old - m_new)
            p = jnp.exp(s - m_new[:, :1])
            l_s[...] = l_s[...] * alpha + jnp.sum(p, axis=1, keepdims=True)
            acc_s[...] = acc_s[...] * alpha + jnp.dot(p.astype(MM_DTYPE), v_ref[...], preferred_element_type=F32)
            m_s[...] = m_new

        @pl.when(ki < qi)
        def _():
            update(scores())

        @pl.when(ki == qi)
        def _():
            s = scores()
            update(jnp.where(_below_diagonal(s.shape), s, NEG))
            o_ref[...] = acc_s[...] / l_s[...]
            lse_ref[...] = m_s[...] + jnp.log(l_s[...])

    return _attn_call("mla_attn_fwd", body, _causal_pairs(nq, False), (q, k, v), "qkk", "qq",
                      [pltpu.VMEM((tq, HEAD_LANES), F32)] * 3, t, tq)


def _attn_probs(q, k, lse, diagonal):
    s = lax.dot_general(q, k, (((1,), (1,)), ((), ())), preferred_element_type=F32)
    p = jnp.exp(s - lse[:, :1])
    return jnp.where(_below_diagonal(s.shape), p, 0.0) if diagonal else p


def _attn_bwd_dq(q, k, v, do, lse, delta, tq):
    t = q.shape[0]
    nq = t // tq

    def body(qt, kt, q_ref, k_ref, v_ref, do_ref, lse_ref, dl_ref, dq_ref, acc_s):
        p_id = pl.program_id(1)
        qi, ki = qt[p_id], kt[p_id]

        @pl.when(ki == 0)
        def _():
            acc_s[...] = jnp.zeros(acc_s.shape, F32)

        def step(diagonal):
            p = _attn_probs(q_ref[...], k_ref[...], lse_ref[...], diagonal)
            dp = lax.dot_general(do_ref[...], v_ref[...], (((1,), (1,)), ((), ())), preferred_element_type=F32)
            ds = p * (dp - dl_ref[...][:, :1])
            acc_s[...] += jnp.dot(ds.astype(MM_DTYPE), k_ref[...], preferred_element_type=F32)

        @pl.when(ki < qi)
        def _():
            step(False)

        @pl.when(ki == qi)
        def _():
            step(True)
            dq_ref[...] = acc_s[...]

    return _attn_call("mla_attn_bwd_dq", body, _causal_pairs(nq, False), (q, k, v, do, lse, delta), "qkkqqq", "q",
                      [pltpu.VMEM((tq, HEAD_LANES), F32)], t, tq)[0]


def _attn_bwd_dkv(q, k, v, do, lse, delta, tq):
    t = q.shape[0]
    nq = t // tq

    def body(qt, kt, q_ref, k_ref, v_ref, do_ref, lse_ref, dl_ref, dk_ref, dv_ref, dk_s, dv_s):
        p_id = pl.program_id(1)
        qi, ki = qt[p_id], kt[p_id]

        def step(diagonal):
            p = _attn_probs(q_ref[...], k_ref[...], lse_ref[...], diagonal)
            dv_s[...] += lax.dot_general(p.astype(MM_DTYPE), do_ref[...], (((0,), (0,)), ((), ())),
                                         preferred_element_type=F32)
            dp = lax.dot_general(do_ref[...], v_ref[...], (((1,), (1,)), ((), ())), preferred_element_type=F32)
            ds = p * (dp - dl_ref[...][:, :1])
            dk_s[...] += lax.dot_general(ds.astype(MM_DTYPE), q_ref[...], (((0,), (0,)), ((), ())),
                                         preferred_element_type=F32)

        @pl.when(qi == ki)
        def _():
            dk_s[...] = jnp.zeros(dk_s.shape, F32)
            dv_s[...] = jnp.zeros(dv_s.shape, F32)
            step(True)

        @pl.when(qi > ki)
        def _():
            step(False)

        @pl.when(qi == nq - 1)
        def _():
            dk_ref[...] = dk_s[...]
            dv_ref[...] = dv_s[...]

    return _attn_call("mla_attn_bwd_dkv", body, _causal_pairs(nq, True), (q, k, v, do, lse, delta), "qkkqqq", "kk",
                      [pltpu.VMEM((tq, HEAD_LANES), F32)] * 2, t, tq)


def _dotf(a, b, dims=(((1,), (0,)), ((), ()))):
    return lax.dot_general(a, b, dims, preferred_element_type=F32, precision=HI)


def _dot1(a, b, dims=(((1,), (0,)), ((), ()))):
    return lax.dot_general(a.astype(MM_DTYPE), b.astype(MM_DTYPE), dims, preferred_element_type=F32)


def _dot3(a, b, dims=(((1,), (0,)), ((), ()))):
    return lax.dot_general(a, b, dims, preferred_element_type=F32, precision=lax.Precision.HIGH)


NN3 = (((2,), (1,)), ((0,), (0,)))
NT3 = (((2,), (2,)), ((0,), (0,)))
TN3 = (((1,), (1,)), ((0,), (0,)))


def _tri_masks(nh):
    shape = (nh, CHUNK, CHUNK)
    return lax.broadcasted_iota(jnp.int32, shape, 1), lax.broadcasted_iota(jnp.int32, shape, 2)


def _gdn_chunk_common(k, gb, bb, row, col, dot=_dot1):
    tril = row >= col
    ltri = tril.astype(F32)
    umat = (row <= col).astype(F32)
    gcc = _dotf(ltri, gb, NN3)
    gcr = _dotf(gb, umat, TN3)
    dm = jnp.exp(jnp.where(tril, gcc - gcr, NEG))
    kb = k * bb
    lm = jnp.where(row > col, dot(kb, k, NT3) * dm, 0.0)
    return gcc, dm, kb, lm, umat


def _unit_lower_inverse(lm, eye):
    t = eye - lm
    p = lm
    for _ in range(CHUNK.bit_length() - 2):
        p = _dotf(p, p, NN3)
        t = t + _dotf(t, p, NN3)
    return t


def _gdn_fwd(q, k, v, gb, bb):
    nh, t, dh = q.shape
    nchunk = t // CHUNK

    def body(q_ref, k_ref, v_ref, g_ref, b_ref, o_ref, sall_ref, tall_ref, s_s):
        @pl.when(pl.program_id(0) == 0)
        def _():
            s_s[...] = jnp.zeros(s_s.shape, F32)

        row, col = _tri_masks(nh)
        qh, kh, vh, bbh = q_ref[...], k_ref[...], v_ref[...], b_ref[...]
        gcc, dm, kb, lm, _ = _gdn_chunk_common(kh, g_ref[...], bbh, row, col)
        eg = jnp.exp(gcc)
        glr = gcc[:, CHUNK - 1:CHUNK, :]
        th = _unit_lower_inverse(lm, (row == col).astype(F32))
        w = _dot1(th, kb * eg, NN3)
        u = _dot1(th, vh * bbh, NN3)
        at = jnp.where(row >= col, _dot1(qh, kh, NT3) * dm, 0.0)
        sh = s_s[...]
        vn = u - _dot1(w, sh, NN3)
        o_ref[...] = _dot1(qh * eg, sh, NN3) + _dot1(at, vn, NN3)
        kd = kh * jnp.exp(glr - gcc)
        sall_ref[:, 0] = sh
        tall_ref[...] = th
        s_s[...] = sh * jnp.exp(glr) + _dot1(kd, vn, TN3)

    blk = pl.BlockSpec((nh, CHUNK, dh), lambda n: (0, n, 0))
    return pl.pallas_call(
        body, name="gdn_fwd", grid=(nchunk,), in_specs=[blk] * 5,
        out_specs=[blk, pl.BlockSpec((nh, 1, dh, dh), lambda n: (0, n, 0, 0)), blk],
        out_shape=[jax.ShapeDtypeStruct((nh, t, dh), F32), jax.ShapeDtypeStruct((nh, nchunk, dh, dh), F32),
                   jax.ShapeDtypeStruct((nh, t, CHUNK), F32)],
        scratch_shapes=[pltpu.VMEM((nh, dh, dh), F32)],
        compiler_params=_params(("arbitrary",)))(q, k, v, gb, bb)


def _gdn_bwd(q, k, v, gb, bb, sall, tall, do):
    nh, t, dh = q.shape
    nchunk = t // CHUNK

    def body(q_ref, k_ref, v_ref, g_ref, b_ref, sall_ref, tall_ref, do_ref,
             dq_ref, dk_ref, dv_ref, dg_ref, db_ref, ds_s):
        @pl.when(pl.program_id(0) == 0)
        def _():
            ds_s[...] = jnp.zeros(ds_s.shape, F32)

        row, col = _tri_masks(nh)
        tril, stril = row >= col, row > col
        rsum = lambda x: jnp.sum(x, axis=2, keepdims=True)
        qh, kh, vh, gbh, bbh = q_ref[...], k_ref[...], v_ref[...], g_ref[...], b_ref[...]
        sh, th, doh, dsp = sall_ref[:, 0], tall_ref[...], do_ref[...], ds_s[...]
        gcc, dm, kb, lm, umat = _gdn_chunk_common(kh, gbh, bbh, row, col, _dot3)
        eg = jnp.exp(gcc)
        glr = gcc[:, CHUNK - 1:CHUNK, :]
        glv = jnp.exp(glr)
        egl = jnp.exp(glr - gcc)
        rw, ru = kb * eg, vh * bbh
        w, u = _dot3(th, rw, NN3), _dot3(th, ru, NN3)
        at = jnp.where(tril, _dot3(qh, kh, NT3) * dm, 0.0)
        qd, kd = qh * eg, kh * egl
        vn = u - _dot3(w, sh, NN3)
        dgl = jnp.sum(rsum(dsp * sh), axis=1, keepdims=True)
        dkd = _dot3(vn, dsp, NT3)
        dvn = _dot3(kd, dsp, NN3)
        dqd = _dot3(doh, sh, NT3)
        dat = jnp.where(tril, _dot3(doh, vn, NT3), 0.0)
        dvn = dvn + _dot3(at, doh, TN3)
        dw = -_dot3(dvn, sh, NT3)
        ds_s[...] = dsp * glv + _dot3(qd, doh, TN3) - _dot3(w, dvn, TN3)
        dpa = dat * dm
        dq_ref[...] = _dot3(dpa, kh, NN3) + dqd * eg
        dk = _dot3(dpa, qh, TN3) + dkd * egl
        t6 = rsum(dkd * kd)
        dgam = rsum(dqd * qd) - t6
        dgam_last = jnp.sum(t6, axis=1, keepdims=True) + dgl * glv
        drw = _dot3(th, dw, TN3)
        dru = _dot3(th, dvn, TN3)
        dl = -jnp.where(stril, _dot3(drw, w, NT3) + _dot3(dru, u, NT3), 0.0)
        dgam = dgam + rsum(drw * rw)
        dv_ref[...] = dru * bbh
        dp2 = dl * dm
        dkb = drw * eg + _dot3(dp2, kh, NN3)
        dk_ref[...] = dk + _dot3(dp2, kb, TN3) + dkb * bbh
        db_ref[...] = rsum(dru * vh) + rsum(dkb * kh) + jnp.zeros((nh, CHUNK, dh), F32)
        e = dat * at + dl * lm
        dgam_b = dgam + rsum(e) - _dotf(e, jnp.ones((nh, CHUNK, CHUNK), F32), TN3)
        dgam_b = dgam_b + jnp.where(row == CHUNK - 1, dgam_last, 0.0)
        dg_ref[...] = _dotf(umat, dgam_b, NN3)

    rev = lambda n: (0, nchunk - 1 - n, 0)
    blk = pl.BlockSpec((nh, CHUNK, dh), rev)
    sblk = pl.BlockSpec((nh, 1, dh, dh), lambda n: (0, nchunk - 1 - n, 0, 0))
    out = jax.ShapeDtypeStruct((nh, t, dh), F32)
    return pl.pallas_call(
        body, name="gdn_bwd", grid=(nchunk,), in_specs=[blk] * 5 + [sblk, blk, blk], out_specs=[blk] * 5,
        out_shape=[out] * 5, scratch_shapes=[pltpu.VMEM((nh, dh, dh), F32)],
        compiler_params=_params(("arbitrary",)))(q, k, v, gb, bb, sall, tall, do)


def _group_ones():
    r = lax.broadcasted_iota(jnp.int32, (GDN_W, GDN_W), 0) // GDN_DH
    c = lax.broadcasted_iota(jnp.int32, (GDN_W, GDN_W), 1) // GDN_DH
    return (r == c).astype(F32)


def _conv_taps(x, xprev, w, has_prev):
    row = lax.broadcasted_iota(jnp.int32, x.shape, 0)
    out = x * w[GDN_CONV - 1:GDN_CONV, :]
    for s in range(1, GDN_CONV):
        sh = jnp.where(row >= s, _roll(x, s, 0), _roll(xprev, s, 0) * has_prev)
        out = out + sh * w[GDN_CONV - 1 - s:GDN_CONV - s, :]
    return out


def _head_cols(x, h):
    return x[:, h * GDN_DH:(h + 1) * GDN_DH]


def _heads_spec(tb):
    return pl.BlockSpec((N_HEADS, tb, GDN_DH), lambda i: (0, i, 0))


def _mixer_fwd(x, positions, w, tb):
    t, d = x.shape
    tables = _rope_tables(positions)

    def pre(xb, g):
        return (xb * _rms_stats(xb) * g,)

    (hn,) = _rowwise("mix_pre", pre, [x], [w["mix_pre_g"]], [(d, BF16)], [], tb)
    proj = _mm("mix_in", hn, w["w_in_pad"], "nn", F32)

    def mla_pre(p0, gq, gkv):
        cq, ckv = p0[:, :MLA_Q_RANK], p0[:, MLA_Q_RANK:MLA_Q_RANK + MLA_KV_RANK]
        return cq * _rms_stats(cq) * gq, ckv * _rms_stats(ckv) * gkv

    nq, nkv = _rowwise("mla_pre", mla_pre, [(proj, 512, PIN_MLA // 512, 0)],
                       [w["mla_q_norm_g"], w["mla_kv_norm_g"]], [(MLA_Q_RANK, BF16), (MLA_KV_RANK, BF16)], [], tb)
    qraw = _mm("mla_uq", nq, w["w_uq_pad"], "nn", F32)
    kv = _mm("mla_ukv", nkv, w["w_kv_pad"], "nn", F32)

    def rope_f(qr, kn, vv, kpe, c, s1, s2):
        qo = _heads_apply(qr, lambda xh: _rope(xh, c, s1, s2)) * _attn_scale()
        kp = _rope(kpe, c, s1, s2)
        return qo, kn + jnp.tile(kp, (1, N_HEADS)), vv

    q, k, v = _rowwise("mla_rope", rope_f,
                       [qraw, (kv, MLA_PAD, 0, 0), (kv, MLA_PAD, 1, 0), (proj, HEAD_LANES, PIN_KPE // HEAD_LANES, 0),
                        tables[0], tables[1], tables[2]], [],
                       [(MLA_PAD, BF16)] * 3, [], tb // 2)
    tq = min(512, t)
    o, lse = _attn_fwd(q, k, v, tq)

    def mla_post(ob, g):
        return (ob * _rms_stats(ob, N_HEADS * MLA_V) * g,)

    (cat,) = _rowwise("mla_post", mla_post, [o], [w["mla_out_g_pad"]], [(MLA_PAD, BF16)], [], tb, wide=(CAT_W, 0))

    gones = _group_ones()
    steps = t // tb

    def gdn_pre(xq, xk, xv, pq, pk, pv, cw, go, has_prev):
        outs = []
        for j, (xc, xp) in enumerate(((xq, pq), (xk, pk), (xv, pv))):
            c = _conv_taps(xc, xp, cw[:, j * GDN_W:(j + 1) * GDN_W], has_prev)
            a = c * _sigmoid(c)
            if j < 2:
                rn = lax.rsqrt(_dotf(a * a, go) + EPS)
                a = a * rn
                if j == 0:
                    a = a * (GDN_DH ** -0.5)
            outs.append(a)
        return tuple(outs)

    qh, kh, vh = _gdn_pre_call("gdn_pre", gdn_pre, proj, w["conv_w"], gones, tb, steps)
    heads_shape = jax.ShapeDtypeStruct((N_HEADS, t, GDN_DH), F32)
    lanes_shape = jax.ShapeDtypeStruct((t, HEAD_LANES), F32)
    lanes_spec = pl.BlockSpec((tb, HEAD_LANES), lambda i: (i, 0))
    vec_spec = lambda n: pl.BlockSpec((1, n), lambda i: (0, 0))

    def gate_f(ab_ref, al_ref, dt_ref, g_ref, b_ref, gh_ref, bh_ref):
        g, b = _gb_fwd(ab_ref[...], al_ref[...], dt_ref[...])
        g_ref[...] = g
        b_ref[...] = b
        for h in range(N_HEADS):
            gh_ref[h] = jnp.broadcast_to(g[:, h:h + 1], (tb, GDN_DH))
            bh_ref[h] = jnp.broadcast_to(b[:, N_HEADS + h:N_HEADS + h + 1], (tb, GDN_DH))

    g128, b128, gbh, bbh = pl.pallas_call(
        gate_f, name="gdn_gate_f", grid=(steps,),
        in_specs=[pl.BlockSpec((tb, HEAD_LANES), lambda i: (i, PIN_AB // HEAD_LANES)), vec_spec(HEAD_LANES),
                  vec_spec(HEAD_LANES)],
        out_specs=[lanes_spec, lanes_spec, _heads_spec(tb), _heads_spec(tb)],
        out_shape=[lanes_shape, lanes_shape, heads_shape, heads_shape],
        compiler_params=_params(("arbitrary",)))(proj, w["a_log_pad"], w["dt_bias_pad"])
    oh, sall, tall = _gdn_fwd(qh, kh, vh, gbh, bbh)

    def gdn_post(o_ref, gt_ref, g_ref, cat_in, cat_ref):
        gt, g = gt_ref[...], g_ref[...]
        outs = []
        for h in range(N_HEADS):
            ob, gth = o_ref[h], _head_cols(gt, h)
            outs.append(ob * _rms_stats(ob) * g * (gth * _sigmoid(gth)))
        cat_ref[...] = jnp.concatenate(outs, axis=1).astype(cat_ref.dtype)

    gate_spec = pl.BlockSpec((tb, GDN_W), lambda i: (i, PIN_GATE // GDN_W))
    cat = pl.pallas_call(
        gdn_post, name="gdn_post", grid=(steps,),
        in_specs=[_heads_spec(tb), gate_spec, vec_spec(GDN_DH), ANY_SPEC],
        out_specs=pl.BlockSpec((tb, GDN_W), lambda i: (i, MLA_PAD // GDN_W)),
        out_shape=jax.ShapeDtypeStruct((t, CAT_W), BF16), input_output_aliases={3: 0},
        compiler_params=_params(("arbitrary",)))(oh, proj, w["gdn_norm_g"], cat)
    mixed = _mm("mix_out", cat, w["w_out_pad"], "nn", F32)

    def post(xb, hb, g):
        return (xb + hb * _rms_stats(hb) * g,)

    (y,) = _rowwise("mix_post", post, [x, mixed], [w["mix_post_g"]], [(d, F32)], [], tb)
    saved = dict(x=x, hn=hn, proj=proj, nq=nq, nkv=nkv, q=q, k=k, v=v, o=o, lse=lse, qh=qh, kh=kh, vh=vh,
                 gbh=gbh, bbh=bbh, oh=oh, sall=sall, tall=tall, cat=cat, mixed=mixed,
                 tables=tables, g128=g128, b128=b128)
    return y, saved


def _qkv_specs(tb):
    base = PIN_QKV // GDN_W
    cur = [pl.BlockSpec((tb, GDN_W), lambda i, j=j: (i, base + j)) for j in range(3)]
    prev = [pl.BlockSpec((tb, GDN_W), lambda i, j=j: (jnp.maximum(i - 1, 0), base + j)) for j in range(3)]
    return cur + prev


def _gdn_pre_call(name, fn, proj, conv_w, gones, tb, steps):
    t = proj.shape[0]

    def body(xq, xk, xv, pq, pk, pv, cw, go, oq, ok, ov):
        has_prev = jnp.where(pl.program_id(0) == 0, 0.0, 1.0)
        outs = fn(xq[...], xk[...], xv[...], pq[...], pk[...], pv[...], cw[...], go[...], has_prev)
        for r, val in zip((oq, ok, ov), outs):
            for h in range(N_HEADS):
                r[h] = _head_cols(val, h)

    return pl.pallas_call(
        body, name=name, grid=(steps,),
        in_specs=_qkv_specs(tb) + [pl.BlockSpec(conv_w.shape, lambda i: (0, 0)),
                                   pl.BlockSpec(gones.shape, lambda i: (0, 0))],
        out_specs=[_heads_spec(tb)] * 3,
        out_shape=[jax.ShapeDtypeStruct((N_HEADS, t, GDN_DH), F32)] * 3,
        compiler_params=_params(("arbitrary",)))(proj, proj, proj, proj, proj, proj, conv_w, gones)


def _softplus(x):
    return jnp.maximum(x, 0.0) + jnp.log1p(jnp.exp(-jnp.abs(x)))


def _gb_fwd(ab, a_log, dt_bias):
    g = -jnp.exp(a_log) * _softplus(ab + dt_bias)
    return g, _sigmoid(ab)


def _rope_tables(positions):
    half = MLA_ROPE // 2
    freqs = ROPE_THETA ** (-jnp.arange(half, dtype=F32) / half)
    ang = positions.reshape(-1).astype(F32)[:, None] * freqs
    cos, sin = jnp.cos(ang), jnp.sin(ang)
    t = ang.shape[0]
    one = jnp.ones((t, MLA_NOPE), F32)
    z16, z32, z64 = jnp.zeros((t, half), F32), jnp.zeros((t, MLA_ROPE), F32), jnp.zeros((t, MLA_NOPE), F32)
    c = jnp.concatenate([one, cos, cos, jnp.ones((t, MLA_ROPE), F32)], axis=1)
    s1 = jnp.concatenate([z64, -sin, z16, z32], axis=1)
    s2 = jnp.concatenate([z64, z16, sin, z32], axis=1)
    return c, s1, s2


def _mixer_bwd(dy, sv, w, tb):
    x, proj = sv["x"], sv["proj"]
    t, d = x.shape
    c, s1, s2 = sv["tables"]
    grads = {}

    def post_b(hb, dyb, g):
        return _rms_bwd(hb, _rms_stats(hb), g, dyb)

    dmixed, grads["mix_post_g"] = _rowwise("mix_post_b", post_b, [sv["mixed"], dy], [w["mix_post_g"]],
                                           [(d, BF16)], [(1, d)], tb)
    dcat = _mm("mix_out_bx", dmixed, w["w_out_pad"], "nt", F32)
    grads["w_out_pad"] = _mm("mix_out_bw", sv["cat"], dmixed, "tn", F32)
    steps = t // tb
    vec_spec = lambda n: pl.BlockSpec((1, n), lambda i: (0, 0))

    def gdn_post_b(o_ref, gt_ref, do_ref, g_ref, dproj_ref, doh_ref, dg_ref):
        @pl.when(pl.program_id(0) == 0)
        def _():
            dg_ref[...] = jnp.zeros(dg_ref.shape, F32)

        gt, dob, g = gt_ref[...], do_ref[...], g_ref[...]
        dgates = []
        for h in range(N_HEADS):
            ob, gth, dobh = o_ref[h], _head_cols(gt, h), _head_cols(dob, h)
            sg = _sigmoid(gth)
            r = _rms_stats(ob)
            dxo, dg = _rms_bwd(ob, r, g, dobh * (gth * sg))
            doh_ref[h] = dxo
            dg_ref[...] += dg
            dgates.append(dobh * (ob * r * g) * (sg * (1.0 + gth * (1.0 - sg))))
        dproj_ref[...] = jnp.concatenate(dgates, axis=1).astype(dproj_ref.dtype)

    dproj, doh, grads["gdn_norm_g"] = pl.pallas_call(
        gdn_post_b, name="gdn_post_b", grid=(steps,),
        in_specs=[_heads_spec(tb), pl.BlockSpec((tb, GDN_W), lambda i: (i, PIN_GATE // GDN_W)),
                  pl.BlockSpec((tb, GDN_W), lambda i: (i, MLA_PAD // GDN_W)), vec_spec(GDN_DH)],
        out_specs=[pl.BlockSpec((tb, GDN_W), lambda i: (i, PIN_GATE // GDN_W)), _heads_spec(tb), vec_spec(GDN_DH)],
        out_shape=[jax.ShapeDtypeStruct((t, PIN_W), BF16), jax.ShapeDtypeStruct((N_HEADS, t, GDN_DH), F32),
                   jax.ShapeDtypeStruct((1, GDN_DH), F32)],
        compiler_params=_params(("arbitrary",)))(sv["oh"], proj, dcat, w["gdn_norm_g"])

    def mla_post_b(ob, dmo, g):
        do, dg = _rms_bwd(ob, _rms_stats(ob, N_HEADS * MLA_V), g, dmo, N_HEADS * MLA_V)
        prod = do * ob
        delta = _heads_apply(prod, lambda ph: jnp.sum(ph, axis=1, keepdims=True) + jnp.zeros_like(ph))
        return do, delta, dg

    do, delta, grads["mla_out_g_pad"] = _rowwise(
        "mla_post_b", mla_post_b, [sv["o"], (dcat, MLA_PAD, 0, 0)], [w["mla_out_g_pad"]],
        [(MLA_PAD, BF16), (MLA_PAD, F32)], [(1, MLA_PAD)], tb // 2)
    tq = min(512, t)
    dq = _attn_bwd_dq(sv["q"], sv["k"], sv["v"], do, sv["lse"], delta, tq)
    dk, dv = _attn_bwd_dkv(sv["q"], sv["k"], sv["v"], do, sv["lse"], delta, tq)

    def rope_b(dqb, dkb, dvb, cc, a1, a2):
        dqr = _heads_apply(dqb * _attn_scale(), lambda xh: _rope(xh, cc, -a1, -a2))
        ksum = dkb[:, :HEAD_LANES]
        for h in range(1, N_HEADS):
            ksum = ksum + dkb[:, h * HEAD_LANES:(h + 1) * HEAD_LANES]
        lane = lax.broadcasted_iota(jnp.int32, ksum.shape, 1)
        keep = (lane >= MLA_NOPE) & (lane < MLA_NOPE + MLA_ROPE)
        dkpe = jnp.where(keep, _rope(ksum, cc, -a1, -a2), 0.0)
        return dqr, jnp.concatenate([dkb, dvb], axis=1), dkpe

    dqraw, dkv, dkpe = _rowwise("mla_rope_b", rope_b, [dq, dk, dv, c, s1, s2], [],
                                [(MLA_PAD, BF16), (2 * MLA_PAD, BF16), (HEAD_LANES, F32)], [], tb // 2)
    dnq = _mm("mla_uq_bx", dqraw, w["w_uq_pad"], "nt", F32)
    grads["w_uq_pad"] = _mm("mla_uq_bw", sv["nq"], dqraw, "tn", F32)
    dnkv = _mm("mla_ukv_bx", dkv, w["w_kv_pad"], "nt", F32)
    grads["w_kv_pad"] = _mm("mla_ukv_bw", sv["nkv"], dkv, "tn", F32)

    def mla_pre_b(p0, dnqb, dnkvb, dkpeb, gq, gkv):
        cq, ckv = p0[:, :MLA_Q_RANK], p0[:, MLA_Q_RANK:MLA_Q_RANK + MLA_KV_RANK]
        dcq, dgq = _rms_bwd(cq, _rms_stats(cq), gq, dnqb)
        dckv, dgkv = _rms_bwd(ckv, _rms_stats(ckv), gkv, dnkvb)
        return jnp.concatenate([dcq, dckv, dkpeb], axis=1), dgq, dgkv

    dproj, grads["mla_q_norm_g"], grads["mla_kv_norm_g"] = _rowwise(
        "mla_pre_b", mla_pre_b, [(proj, 512, PIN_MLA // 512, 0), dnq, dnkv, dkpe],
        [w["mla_q_norm_g"], w["mla_kv_norm_g"]], [(512, BF16)], [(1, MLA_Q_RANK), (1, MLA_KV_RANK)], tb,
        wide=(PIN_W, PIN_MLA // 512), carry=dproj)

    dqh, dkh, dvh, dgh, dbh = _gdn_bwd(sv["qh"], sv["kh"], sv["vh"], sv["gbh"], sv["bbh"], sv["sall"], sv["tall"], doh)
    gones = _group_ones()

    def gdn_pre_b(xq, xk, xv, pq, pk, pv, dq_, dk_, dv_, cw, go, has_prev):
        outs = []
        for j, (xc, xp, dd) in enumerate(((xq, pq, dq_), (xk, pk, dk_), (xv, pv, dv_))):
            cc = _conv_taps(xc, xp, cw[:, j * GDN_W:(j + 1) * GDN_W], has_prev)
            sg = _sigmoid(cc)
            a = cc * sg
            if j < 2:
                rn = lax.rsqrt(_dotf(a * a, go) + EPS)
                if j == 0:
                    dd = dd * (GDN_DH ** -0.5)
                da = rn * dd - a * (rn * rn * rn) * _dotf(dd * a, go)
            else:
                da = dd
            outs.append(da * (sg * (1.0 + cc * (1.0 - sg))))
        return tuple(outs)

    dcq, dck, dcv = _gdn_pre_b_call("gdn_pre_b", gdn_pre_b, proj, (dqh, dkh, dvh), w["conv_w"], gones, tb, steps)
    dproj, grads["conv_w"] = _conv_bwd_call("gdn_conv_b", proj, (dcq, dck, dcv), w["conv_w"], dproj, tb, steps)

    def gate_b(ab_ref, g_ref, b_ref, dgh_ref, dbh_ref, al_ref, dt_ref, carry_ref, dab_ref, dal_ref, ddt_ref):
        @pl.when(pl.program_id(0) == 0)
        def _():
            dal_ref[...] = jnp.zeros(dal_ref.shape, F32)
            ddt_ref[...] = jnp.zeros(ddt_ref.shape, F32)

        ab, g128, b128 = ab_ref[...], g_ref[...], b_ref[...]
        lane = lax.broadcasted_iota(jnp.int32, ab.shape, 1)
        dg_ = jnp.zeros(ab.shape, F32)
        db_ = jnp.zeros(ab.shape, F32)
        for h in range(N_HEADS):
            dg_ = dg_ + jnp.where(lane == h, jnp.broadcast_to(dgh_ref[h][:, 0:1], ab.shape), 0.0)
            db_ = db_ + jnp.where(lane == N_HEADS + h, jnp.broadcast_to(dbh_ref[h][:, 0:1], ab.shape), 0.0)
        slope = -jnp.exp(al_ref[...]) * _sigmoid(ab + dt_ref[...])
        dab_ref[...] = (dg_ * slope + db_ * b128 * (1.0 - b128)).astype(dab_ref.dtype)
        dal_ref[...] += jnp.sum(dg_ * g128, axis=0, keepdims=True)
        ddt_ref[...] += jnp.sum(dg_ * slope, axis=0, keepdims=True)

    lanes_spec = pl.BlockSpec((tb, HEAD_LANES), lambda i: (i, 0))
    ab_spec = pl.BlockSpec((tb, HEAD_LANES), lambda i: (i, PIN_AB // HEAD_LANES))
    dproj, grads["a_log_pad"], grads["dt_bias_pad"] = pl.pallas_call(
        gate_b, name="gdn_gate_b", grid=(steps,),
        in_specs=[ab_spec, lanes_spec, lanes_spec, _heads_spec(tb), _heads_spec(tb), vec_spec(HEAD_LANES),
                  vec_spec(HEAD_LANES), ANY_SPEC],
        out_specs=[ab_spec, vec_spec(HEAD_LANES), vec_spec(HEAD_LANES)],
        out_shape=[jax.ShapeDtypeStruct((t, PIN_W), BF16), jax.ShapeDtypeStruct((1, HEAD_LANES), F32),
                   jax.ShapeDtypeStruct((1, HEAD_LANES), F32)],
        input_output_aliases={7: 0},
        compiler_params=_params(("arbitrary",)))(proj, sv["g128"], sv["b128"], dgh, dbh, w["a_log_pad"],
                                                 w["dt_bias_pad"], dproj)
    dhn = _mm("mix_in_bx", dproj, w["w_in_pad"], "nt", F32)
    grads["w_in_pad"] = _mm("mix_in_bw", sv["hn"], dproj, "tn", F32)

    def pre_b(xb, dnb, dyb, g):
        dx, dg = _rms_bwd(xb, _rms_stats(xb), g, dnb)
        return dyb + dx, dg

    dx, grads["mix_pre_g"] = _rowwise("mix_pre_b", pre_b, [x, dhn, dy], [w["mix_pre_g"]], [(d, F32)], [(1, d)], tb)
    return dx, grads


def _gdn_pre_b_call(name, fn, proj, dd, conv_w, gones, tb, steps):
    t = proj.shape[0]

    def body(xq, xk, xv, pq, pk, pv, d0, d1, d2, cw, go, oq, ok, ov):
        has_prev = jnp.where(pl.program_id(0) == 0, 0.0, 1.0)
        dd_rows = [jnp.concatenate([dr[h] for h in range(N_HEADS)], axis=1) for dr in (d0, d1, d2)]
        outs = fn(xq[...], xk[...], xv[...], pq[...], pk[...], pv[...], *dd_rows, cw[...], go[...], has_prev)
        for r, val in zip((oq, ok, ov), outs):
            r[...] = val

    return pl.pallas_call(
        body, name=name, grid=(steps,),
        in_specs=_qkv_specs(tb) + [_heads_spec(tb)] * 3 + [pl.BlockSpec(conv_w.shape, lambda i: (0, 0)),
                                                          pl.BlockSpec(gones.shape, lambda i: (0, 0))],
        out_specs=[pl.BlockSpec((tb, GDN_W), lambda i: (i, 0))] * 3,
        out_shape=[jax.ShapeDtypeStruct((t, GDN_W), F32)] * 3,
        compiler_params=_params(("arbitrary",)))(proj, proj, proj, proj, proj, proj, *dd, conv_w, gones)


def _conv_bwd_call(name, proj, dc, conv_w, dproj, tb, steps):
    t = proj.shape[0]
    dcur = [pl.BlockSpec((tb, GDN_W), lambda i: (i, 0))] * 3
    dnext = [pl.BlockSpec((tb, GDN_W), lambda i: (jnp.minimum(i + 1, steps - 1), 0))] * 3

    def body(xq, xk, xv, pq, pk, pv, d0, d1, d2, n0, n1, n2, cw, carry_ref, dx_ref, dw_ref):
        i = pl.program_id(0)
        has_prev = jnp.where(i == 0, 0.0, 1.0)
        has_next = jnp.where(i == steps - 1, 0.0, 1.0)

        @pl.when(i == 0)
        def _():
            dw_ref[...] = jnp.zeros(dw_ref.shape, F32)

        wv = cw[...]
        dws, dxs = [], []
        for j, (xr, pr, dr, nr) in enumerate(((xq, pq, d0, n0), (xk, pk, d1, n1), (xv, pv, d2, n2))):
            x, xp, dcv, dnx = xr[...], pr[...], dr[...], nr[...]
            wj = wv[:, j * GDN_W:(j + 1) * GDN_W]
            row = lax.broadcasted_iota(jnp.int32, x.shape, 0)
            dx = dcv * wj[GDN_CONV - 1:GDN_CONV, :]
            rows_w = [jnp.sum(dcv * x, axis=0, keepdims=True)]
            for s in range(1, GDN_CONV):
                up = jnp.where(row < tb - s, _roll(dcv, tb - s, 0), _roll(dnx, tb - s, 0) * has_next)
                dx = dx + up * wj[GDN_CONV - 1 - s:GDN_CONV - s, :]
                sh = jnp.where(row >= s, _roll(x, s, 0), _roll(xp, s, 0) * has_prev)
                rows_w.append(jnp.sum(dcv * sh, axis=0, keepdims=True))
            dxs.append(dx)
            dws.append(jnp.concatenate(rows_w[::-1], axis=0))
        dx_ref[...] = jnp.concatenate(dxs, axis=1).astype(dx_ref.dtype)
        dw_ref[...] += jnp.concatenate(dws, axis=1)

    return pl.pallas_call(
        body, name=name, grid=(steps,),
        in_specs=_qkv_specs(tb) + dcur + dnext + [pl.BlockSpec(conv_w.shape, lambda i: (0, 0)), ANY_SPEC],
        out_specs=[pl.BlockSpec((tb, 3 * GDN_W), lambda i: (i, PIN_QKV // (3 * GDN_W))),
                   pl.BlockSpec(conv_w.shape, lambda i: (0, 0))],
        out_shape=[jax.ShapeDtypeStruct((t, PIN_W), BF16), jax.ShapeDtypeStruct(conv_w.shape, F32)],
        input_output_aliases={13: 0},
        compiler_params=_params(("arbitrary",)))(proj, proj, proj, proj, proj, proj, *dc, *dc, conv_w, dproj)


def _pad_heads_cols(wm, per_head):
    r = wm.shape[0]
    return jnp.pad(wm.reshape(r, N_HEADS, per_head), ((0, 0), (0, 0), (0, HEAD_LANES - per_head))).reshape(r, MLA_PAD)


def _unpad_heads_cols(wm, per_head):
    r = wm.shape[0]
    return wm.reshape(r, N_HEADS, HEAD_LANES)[:, :, :per_head].reshape(r, N_HEADS * per_head)


def _win_to_pad(wi):
    r = wi.shape[0]
    z = lambda n: jnp.zeros((r, n), wi.dtype)
    o = MLA_Q_RANK + MLA_KV_RANK
    kpe = wi[:, o:o + MLA_ROPE]
    o2 = o + MLA_ROPE
    qkv = wi[:, o2:o2 + 3 * GDN_W]
    o3 = o2 + 3 * GDN_W
    ab = wi[:, o3:o3 + 2 * N_HEADS]
    gate = wi[:, o3 + 2 * N_HEADS:]
    return jnp.concatenate([qkv, wi[:, :o], z(MLA_NOPE), kpe, z(HEAD_LANES - MLA_NOPE - MLA_ROPE), gate, ab,
                            z(HEAD_LANES - 2 * N_HEADS)], axis=1)


def _win_from_pad(wp):
    return jnp.concatenate([wp[:, PIN_MLA:PIN_KPE], wp[:, PIN_KPE + MLA_NOPE:PIN_KPE + MLA_NOPE + MLA_ROPE],
                            wp[:, PIN_QKV:PIN_QKV + 3 * GDN_W], wp[:, PIN_AB:PIN_AB + 2 * N_HEADS],
                            wp[:, PIN_GATE:PIN_GATE + GDN_W]], axis=1)


def _wkv_to_pad(wkv):
    r = wkv.shape[0]
    w3 = wkv.reshape(r, N_HEADS, MLA_NOPE + MLA_V)
    kpart = jnp.pad(w3[:, :, :MLA_NOPE], ((0, 0), (0, 0), (0, HEAD_LANES - MLA_NOPE))).reshape(r, MLA_PAD)
    vpart = jnp.pad(w3[:, :, MLA_NOPE:], ((0, 0), (0, 0), (0, HEAD_LANES - MLA_V))).reshape(r, MLA_PAD)
    return jnp.concatenate([kpart, vpart], axis=1)


def _wkv_from_pad(wp):
    r = wp.shape[0]
    kpart = wp[:, :MLA_PAD].reshape(r, N_HEADS, HEAD_LANES)[:, :, :MLA_NOPE]
    vpart = wp[:, MLA_PAD:].reshape(r, N_HEADS, HEAD_LANES)[:, :, :MLA_V]
    return jnp.concatenate([kpart, vpart], axis=2).reshape(r, N_HEADS * (MLA_NOPE + MLA_V))


def _wout_to_pad(wo):
    n = wo.shape[1]
    mla = jnp.pad(wo[:N_HEADS * MLA_V].reshape(N_HEADS, MLA_V, n), ((0, 0), (0, HEAD_LANES - MLA_V), (0, 0)))
    return jnp.concatenate([mla.reshape(MLA_PAD, n), wo[N_HEADS * MLA_V:]], axis=0)


def _wout_from_pad(wp):
    n = wp.shape[1]
    mla = wp[:MLA_PAD].reshape(N_HEADS, HEAD_LANES, n)[:, :MLA_V].reshape(N_HEADS * MLA_V, n)
    return jnp.concatenate([mla, wp[MLA_PAD:]], axis=0)


def _pad_lanes(v, n):
    return jnp.pad(v, ((0, 0), (0, n - v.shape[1])))


def _compute_weights(full):
    w = {}
    for n in FFN_BIG:
        w[n] = full[n].astype(MM_DTYPE)
    w["w_in_pad"] = _win_to_pad(full["w_in"]).astype(MM_DTYPE)
    w["w_uq_pad"] = _pad_heads_cols(full["mla_w_uq"], MLA_NOPE + MLA_ROPE).astype(MM_DTYPE)
    w["w_kv_pad"] = _wkv_to_pad(full["mla_w_ukv"]).astype(MM_DTYPE)
    w["w_out_pad"] = _wout_to_pad(full["w_out"]).astype(MM_DTYPE)
    w["conv_w"] = full["gdn_conv_w"].astype(F32)
    for n in ("ffn1_pre_g", "ffn1_post_g", "mix_pre_g", "mla_q_norm_g", "mla_kv_norm_g", "gdn_norm_g", "mix_post_g",
              "ffn2_pre_g", "ffn2_post_g"):
        w[n] = full[n]
    w["mla_out_g_pad"] = _pad_heads_cols(full["mla_out_g"], MLA_V)
    w["a_log_pad"] = _pad_lanes(full["gdn_a_log"], HEAD_LANES)
    w["dt_bias_pad"] = _pad_lanes(full["gdn_dt_bias"], HEAD_LANES)
    return w


def _local_step(x, positions, loss_target, full):
    t, d = x.shape
    tb = min(512, t)
    tm = min(1024, t)
    w = _compute_weights(full)
    ffn = lambda tag: (w[tag + "_pre_g"], w[tag + "_w_gate"], w[tag + "_w_up"], w[tag + "_w_down"], w[tag + "_post_g"])
    x1, sv1 = _ffn_fwd("ffn1", x, *ffn("ffn1"), tm)
    x2, svm = _mixer_fwd(x1, positions, w, tb)
    x3, sv2 = _ffn_fwd("ffn2", x2, *ffn("ffn2"), tm)

    def loss_f(yb, tg):
        e = yb - tg
        return e * (1.0 / d), jnp.sum(e * e, axis=0, keepdims=True)

    dy, lsum = _rowwise("loss", loss_f, [x3, loss_target], [], [(d, F32)], [(1, d)], tb)
    g = {}
    dx2, g["ffn2_pre_g"], g["ffn2_w_gate"], g["ffn2_w_up"], g["ffn2_w_down"], g["ffn2_post_g"] = _ffn_bwd(
        "ffn2", dy, sv2, *ffn("ffn2"), tm, tm)
    dx1, gm = _mixer_bwd(dx2, svm, w, tb)
    dx0, g["ffn1_pre_g"], g["ffn1_w_gate"], g["ffn1_w_up"], g["ffn1_w_down"], g["ffn1_post_g"] = _ffn_bwd(
        "ffn1", dx1, sv1, *ffn("ffn1"), tm, tm)
    g["mix_pre_g"], g["mix_post_g"] = gm["mix_pre_g"], gm["mix_post_g"]
    g["mla_q_norm_g"], g["mla_kv_norm_g"] = gm["mla_q_norm_g"], gm["mla_kv_norm_g"]
    g["gdn_norm_g"] = gm["gdn_norm_g"]
    g["w_in"] = _win_from_pad(gm["w_in_pad"])
    g["mla_w_uq"] = _unpad_heads_cols(gm["w_uq_pad"], MLA_NOPE + MLA_ROPE)
    g["mla_w_ukv"] = _wkv_from_pad(gm["w_kv_pad"])
    g["mla_out_g"] = _unpad_heads_cols(gm["mla_out_g_pad"], MLA_V)
    g["gdn_conv_w"] = gm["conv_w"]
    g["gdn_a_log"] = gm["a_log_pad"][:, :N_HEADS]
    g["gdn_dt_bias"] = gm["dt_bias_pad"][:, :N_HEADS]
    g["w_out"] = _wout_from_pad(gm["w_out_pad"])
    return lsum, dx0, g


HBM_SPEC = pl.BlockSpec(memory_space=pltpu.HBM)


def _place():
    return lax.axis_index("x"), lax.axis_index("y"), lax.axis_index("c")


def _exchange_call(name, body, ins, out_shapes, n_remote, n_local):
    return pl.pallas_call(
        body, name=name, in_specs=[HBM_SPEC] * len(ins), out_specs=[HBM_SPEC] * len(out_shapes), out_shape=out_shapes,
        scratch_shapes=[pltpu.SemaphoreType.DMA((n_remote,)), pltpu.SemaphoreType.DMA((n_remote,)),
                        pltpu.SemaphoreType.DMA((n_local,))])(*ins)


def _other_chips(x, y):
    return [(1 - x, y), (x, 1 - y), (1 - x, 1 - y)]


def _at_each_chip(fn):
    x, y, _ = _place()
    for cx in range(2):
        for cy in range(2):
            pl.when((x == cx) & (y == cy))(functools.partial(fn, cx, cy))


def _at_each_device(fn):
    x, y, c = _place()
    for cx in range(2):
        for cy in range(2):
            for cc in range(2):
                pl.when((x == cx) & (y == cy) & (c == cc))(functools.partial(fn, cx, cy, cc))


def _at_each_core(fn):
    c = lax.axis_index("c")
    for cc in range(2):
        pl.when(c == cc)(functools.partial(fn, cc))


def _gather_shards(ws):
    nw = len(ws)

    def body(*refs):
        w_refs, out_refs = refs[:nw], refs[nw:2 * nw]
        send_sems, recv_sems, local_sems = refs[2 * nw:]

        def run(x, y, c):
            chips = _other_chips(x, y)
            me, sibling = 2 * x + y, (x, y, 1 - c)

            def half(ref, which):
                hr = ref.shape[0] // 2
                return ref.at[pl.ds(which * hr, hr)]

            def over_ici(i, j, src, slab, to):
                return pltpu.make_async_remote_copy(
                    src_ref=half(src, c), dst_ref=half(out_refs[i].at[slab], c), send_sem=send_sems.at[7 * i + j],
                    recv_sem=recv_sems.at[7 * i + j], device_id=to, device_id_type=MESH)

            def over_d2d(i, j, slab, which):
                return pltpu.make_async_remote_copy(
                    src_ref=half(out_refs[i].at[slab], which), dst_ref=half(out_refs[i].at[slab], which),
                    send_sem=send_sems.at[7 * i + 3 + j], recv_sem=recv_sems.at[7 * i + 3 + j], device_id=sibling,
                    device_id_type=MESH)

            def own(i, w_ref):
                return pltpu.make_async_remote_copy(
                    src_ref=w_ref, dst_ref=out_refs[i].at[me], send_sem=send_sems.at[7 * i + 6],
                    recv_sem=recv_sems.at[7 * i + 6], device_id=sibling, device_id_type=MESH)

            sends, passed = [], []
            for i, w_ref in enumerate(w_refs):
                for j, (px, py) in enumerate(chips):
                    sends.append(over_ici(i, j, w_ref, me, (px, py, c)))
                    sends[-1].start()
            for i, w_ref in enumerate(w_refs):
                sends.append(own(i, w_ref))
                sends[-1].start()
            for i, w_ref in enumerate(w_refs):
                for j, (px, py) in enumerate(chips):
                    over_ici(i, j, w_ref, 2 * px + py, (px, py, c)).wait_recv()
                    passed.append(over_d2d(i, j, 2 * px + py, c))
                    passed[-1].start()
            for i, w_ref in enumerate(w_refs):
                own(i, w_ref).wait_recv()
                for j, (px, py) in enumerate(chips):
                    over_d2d(i, j, 2 * px + py, 1 - c).wait_recv()
            for cp in sends + passed:
                cp.wait_send()

        _at_each_device(run)

    outs = [jax.ShapeDtypeStruct((N_SHARD,) + w.shape, w.dtype) for w in ws]
    return _exchange_call("gather_weight_shards", body, ws, outs, 7 * nw, 1)


def _swap_halves(gs):
    ng = len(gs)

    def body(*refs):
        g_refs, got_refs = refs[:ng], refs[ng:2 * ng]
        send_sems, recv_sems, _ = refs[2 * ng:]
        x, y, _ = _place()

        def run(c):
            sends = []
            for i, (g_ref, got_ref) in enumerate(zip(g_refs, got_refs)):
                hr = got_ref.shape[1]
                sends.append(pltpu.make_async_remote_copy(
                    src_ref=g_ref.at[:, pl.ds((1 - c) * hr, hr)], dst_ref=got_ref, send_sem=send_sems.at[i],
                    recv_sem=recv_sems.at[i], device_id=(x, y, 1 - c), device_id_type=MESH))
                sends[-1].start()
            for cp in sends:
                cp.wait()

        _at_each_core(run)

    halves = [jax.ShapeDtypeStruct((g.shape[0], g.shape[1] // 2, g.shape[2]), g.dtype) for g in gs]
    return _exchange_call("swap_grad_halves", body, gs, halves, ng, 1)


def _scatter_to_chips(ps):
    n = len(ps)

    def body(*refs):
        p_refs, out_refs = refs[:n], refs[n:2 * n]
        send_sems, recv_sems, _ = refs[2 * n:]
        c = lax.axis_index("c")

        def run(x, y):
            me = 2 * x + y
            chips = _other_chips(x, y)
            sends = []
            for i, (p_ref, out_ref) in enumerate(zip(p_refs, out_refs)):
                for j, (px, py) in enumerate(chips):
                    sends.append(pltpu.make_async_remote_copy(
                        src_ref=p_ref.at[2 * px + py], dst_ref=out_ref.at[j], send_sem=send_sems.at[3 * i + j],
                        recv_sem=recv_sems.at[3 * i + j], device_id=(px, py, c), device_id_type=MESH))
                    sends[-1].start()
            for i, (p_ref, out_ref) in enumerate(zip(p_refs, out_refs)):
                for j, (px, py) in enumerate(chips):
                    pltpu.make_async_remote_copy(
                        src_ref=p_ref.at[me], dst_ref=out_ref.at[j], send_sem=send_sems.at[3 * i + j],
                        recv_sem=recv_sems.at[3 * i + j], device_id=(px, py, c), device_id_type=MESH).wait_recv()
            for cp in sends:
                cp.wait_send()

        _at_each_chip(run)

    outs = [jax.ShapeDtypeStruct((3,) + p.shape[1:], p.dtype) for p in ps]
    return _exchange_call("scatter_grad_quarters", body, ps, outs, 3 * n, 1)


def _share_halves(hs):
    n = len(hs)

    def body(*refs):
        h_refs, out_refs = refs[:n], refs[n:2 * n]
        send_sems, recv_sems, _ = refs[2 * n:]
        x, y, c = _place()
        sends = []
        for i, (h_ref, out_ref) in enumerate(zip(h_refs, out_refs)):
            sends.append(pltpu.make_async_remote_copy(
                src_ref=h_ref, dst_ref=out_ref, send_sem=send_sems.at[i], recv_sem=recv_sems.at[i],
                device_id=(x, y, 1 - c), device_id_type=MESH))
            sends[-1].start()
        for cp in sends:
            cp.wait()

    outs = [jax.ShapeDtypeStruct(h.shape, h.dtype) for h in hs]
    return _exchange_call("share_grad_halves", body, hs, outs, n, 1)


def _scalar_grid_call(name, body, scalars, grid, in_specs, out_specs, out_shape, args):
    grid_spec = pltpu.PrefetchScalarGridSpec(num_scalar_prefetch=len(scalars), grid=grid, in_specs=in_specs,
                                             out_specs=out_specs)
    return pl.pallas_call(body, name=name, grid_spec=grid_spec, out_shape=out_shape,
                          compiler_params=_params(("arbitrary",) * len(grid)))(*scalars, *args)


def _add_pair(name, g, got, core):
    ns_, hr, cols = got.shape
    th = _row_tile(hr, 512)
    nb = hr // th

    def body(core_ref, g_ref, got_ref, out_ref):
        out_ref[...] = (g_ref[...].astype(F32) + got_ref[...].astype(F32)).astype(out_ref.dtype)

    blk = pl.BlockSpec((1, th, cols), lambda q, j, core_ref: (q, j, 0))
    own = pl.BlockSpec((1, th, cols), lambda q, j, core_ref: (q, core_ref[0] * nb + j, 0))
    return _scalar_grid_call(name, body, [core], (ns_, nb), [own, blk], blk,
                             jax.ShapeDtypeStruct(got.shape, got.dtype), [g, got])


def _add_chips(name, pairs, slabs, chip):
    _, hr, cols = slabs.shape
    th = _row_tile(hr, 512)

    def body(chip_ref, own_ref, s0_ref, s1_ref, s2_ref, out_ref):
        total = own_ref[0].astype(F32) + s0_ref[0].astype(F32)
        out_ref[...] = (total + s1_ref[0].astype(F32)) + s2_ref[0].astype(F32)

    own = pl.BlockSpec((1, th, cols), lambda j, chip_ref: (chip_ref[0], j, 0))
    others = [pl.BlockSpec((1, th, cols), lambda j, chip_ref, k=k: (k, j, 0)) for k in range(3)]
    return _scalar_grid_call(name, body, [chip], (hr // th,), [own] + others,
                             pl.BlockSpec((th, cols), lambda j, chip_ref: (j, 0)),
                             jax.ShapeDtypeStruct((hr, cols), F32), [pairs, slabs, slabs, slabs])


def _join_halves(name, mine, other, core):
    hr, cols = mine.shape
    th = _row_tile(hr, 512)
    nb = hr // th

    def body(core_ref, mine_ref, other_ref, out_ref):
        is_mine = pl.program_id(0) == core_ref[0]

        @pl.when(is_mine)
        def _():
            out_ref[...] = mine_ref[...]

        @pl.when(jnp.logical_not(is_mine))
        def _():
            out_ref[...] = other_ref[...]

    blk = pl.BlockSpec((th, cols), lambda h, j, core_ref: (j, 0))
    return _scalar_grid_call(name, body, [core], (2, nb), [blk, blk],
                             pl.BlockSpec((th, cols), lambda h, j, core_ref: (h * nb + j, 0)),
                             jax.ShapeDtypeStruct((2 * hr, cols), mine.dtype), [mine, other])


def _gather_small(sp):
    def body(s_ref, out_ref, send_sems, recv_sems, local_sem):
        x, y, c = _place()
        me = 4 * x + 2 * y + c
        peers = [(x ^ (m >> 2), y ^ ((m >> 1) & 1), c ^ (m & 1)) for m in range(1, 8)]
        mine = pltpu.make_async_copy(s_ref, out_ref.at[me], local_sem)
        mine.start()
        sends = [pltpu.make_async_remote_copy(src_ref=s_ref, dst_ref=out_ref.at[me], send_sem=send_sems.at[j],
                                              recv_sem=recv_sems.at[j], device_id=p, device_id_type=MESH)
                 for j, p in enumerate(peers)]
        for cp in sends:
            cp.start()
        for j, (px, py, pc) in enumerate(peers):
            pltpu.make_async_remote_copy(src_ref=s_ref, dst_ref=out_ref.at[4 * px + 2 * py + pc],
                                         send_sem=send_sems.at[j], recv_sem=recv_sems.at[j], device_id=(px, py, pc),
                                         device_id_type=MESH).wait_recv()
        for cp in sends:
            cp.wait_send()
        mine.wait()

    return pl.pallas_call(
        body, name="gather_small_grads", in_specs=[HBM_SPEC], out_specs=HBM_SPEC,
        out_shape=jax.ShapeDtypeStruct((8,) + sp.shape, sp.dtype),
        scratch_shapes=[pltpu.SemaphoreType.DMA((7,)), pltpu.SemaphoreType.DMA((7,)), pltpu.SemaphoreType.DMA])(sp)


def _pack_rows(total):
    rows = -(-total // LANES)
    return -(-rows // 32) * 32


def _pack(arrs, dtype):
    flat = jnp.concatenate([a.reshape(-1).astype(dtype) for a in arrs])
    rows = _pack_rows(flat.shape[0])
    return jnp.pad(flat, (0, rows * LANES - flat.shape[0])).reshape(rows, LANES)


def _unpack(buf, shapes):
    flat = buf.reshape(-1)
    out, off = {}, 0
    for n, shp in shapes:
        size = shp[0] * shp[1]
        out[n] = flat[off:off + size].reshape(shp)
        off += size
    return out


def _adamw(name, wv, g, m, v, tb):
    c1 = 1.0 - ADAM_B1 ** ADAM_STEP
    c2 = 1.0 - ADAM_B2 ** ADAM_STEP

    def fn(wb, gb, mb, vb):
        m2 = ADAM_B1 * mb + (1.0 - ADAM_B1) * gb
        v2 = ADAM_B2 * vb + (1.0 - ADAM_B2) * (gb * gb)
        delta = -ADAM_LR * ((m2 / c1) / (jnp.sqrt(v2 / c2) + ADAM_EPS) + ADAM_WD * wb)
        return delta, m2, v2

    cols = wv.shape[1]
    return _rowwise(name, fn, [wv, g, m, v], [], [(cols, F32)] * 3, [], tb)


def _row_tile(rows, pref):
    if rows <= pref:
        return rows
    t = pref
    while t >= 8:
        if rows % t == 0 and t % 8 == 0:
            return t
        t -= 8
    return rows


def kernel(x, positions, ffn1_pre_g, ffn1_w_gate, ffn1_w_up, ffn1_w_down, ffn1_post_g, mix_pre_g, w_in, mla_q_norm_g, mla_w_uq, mla_kv_norm_g, mla_w_ukv, mla_out_g, gdn_conv_w, gdn_a_log, gdn_dt_bias, gdn_norm_g, w_out, mix_post_g, ffn2_pre_g, ffn2_w_gate, ffn2_w_up, ffn2_w_down, ffn2_post_g, loss_target, m_ffn1_pre_g, m_ffn1_w_gate, m_ffn1_w_up, m_ffn1_w_down, m_ffn1_post_g, m_mix_pre_g, m_w_in, m_mla_q_norm_g, m_mla_w_uq, m_mla_kv_norm_g, m_mla_w_ukv, m_mla_out_g, m_gdn_conv_w, m_gdn_a_log, m_gdn_dt_bias, m_gdn_norm_g, m_w_out, m_mix_post_g, m_ffn2_pre_g, m_ffn2_w_gate, m_ffn2_w_up, m_ffn2_w_down, m_ffn2_post_g, v_ffn1_pre_g, v_ffn1_w_gate, v_ffn1_w_up, v_ffn1_w_down, v_ffn1_post_g, v_mix_pre_g, v_w_in, v_mla_q_norm_g, v_mla_w_uq, v_mla_kv_norm_g, v_mla_w_ukv, v_mla_out_g, v_gdn_conv_w, v_gdn_a_log, v_gdn_dt_bias, v_gdn_norm_g, v_w_out, v_mix_post_g, v_ffn2_pre_g, v_ffn2_w_gate, v_ffn2_w_up, v_ffn2_w_down, v_ffn2_post_g):
    args = dict(locals())
    wsh = {n: args[n][0] for n in WEIGHTS}
    msh = {n: args["m_" + n][0] if args["m_" + n].ndim == 3 else args["m_" + n] for n in WEIGHTS}
    vsh = {n: args["v_" + n][0] if args["v_" + n].ndim == 3 else args["v_" + n] for n in WEIGHTS}
    for n in SMALL:
        wsh[n] = args[n]
    mix_shapes = [(n, wsh[n].shape) for n in MIX_BIG]

    gathered = _gather_shards([wsh[n].astype(MM_DTYPE) for n in FFN_BIG] + [_pack([wsh[n] for n in MIX_BIG], MM_DTYPE)])
    full = {n: wsh[n] for n in SMALL}
    for n, gw in zip(FFN_BIG, gathered):
        full[n] = gw
    parts = [_unpack(gathered[-1][q], mix_shapes) for q in range(N_SHARD)]
    for n in MIX_BIG:
        full[n] = jnp.concatenate([parts[q][n] for q in range(N_SHARD)], axis=SHARD_AXIS[n])

    lsum, grad_x, g = _local_step(x[0], positions, loss_target[0], full)
    loss = lax.psum(0.5 * jnp.sum(lsum) / x.shape[-1], ("x", "y", "c"))

    mix_quarters = [_pack([jnp.split(g[n], N_SHARD, axis=SHARD_AXIS[n])[q] for n in MIX_BIG], MM_DTYPE)
                    for q in range(N_SHARD)]
    core = lax.axis_index("c").astype(jnp.int32).reshape(1)
    chip = (2 * lax.axis_index("x") + lax.axis_index("y")).astype(jnp.int32).reshape(1)
    mine = [g[n] for n in FFN_BIG] + [jnp.stack(mix_quarters)]
    got = _swap_halves(mine)
    pairs = [_add_pair("add_pair_%d" % i, gi, gt, core) for i, (gi, gt) in enumerate(zip(mine, got))]
    slabs = _scatter_to_chips(pairs)
    halves = [_add_chips("add_chips_%d" % i, pr, sl, chip) for i, (pr, sl) in enumerate(zip(pairs, slabs))]
    others = _share_halves(halves)
    shared = [_join_halves("join_halves_%d" % i, hm, ho, core) for i, (hm, ho) in enumerate(zip(halves, others))]
    gsh = _unpack(shared[-1], mix_shapes)
    for n, sg_ in zip(FFN_BIG, shared):
        gsh[n] = sg_

    small_shapes = [(n, wsh[n].shape) for n in SMALL]
    pack_small = lambda d: jnp.concatenate(
        [_pad_lanes(d[n].astype(F32), LANES) for n in SMALL] + [jnp.zeros((SMALL_ROWS - len(SMALL), LANES), F32)], axis=0)
    slots = _gather_small(pack_small(g))

    c1 = 1.0 - ADAM_B1 ** ADAM_STEP
    c2 = 1.0 - ADAM_B2 ** ADAM_STEP

    def small_update(wb, mb, vb, s8):
        gs = s8[0:SMALL_ROWS]
        for d in range(1, 8):
            gs = gs + s8[d * SMALL_ROWS:(d + 1) * SMALL_ROWS]
        m2 = ADAM_B1 * mb + (1.0 - ADAM_B1) * gs
        v2 = ADAM_B2 * vb + (1.0 - ADAM_B2) * (gs * gs)
        delta = -ADAM_LR * ((m2 / c1) / (jnp.sqrt(v2 / c2) + ADAM_EPS) + ADAM_WD * wb)
        return gs, delta, m2, v2

    sg, sd, sm, sv_ = _rowwise("adamw_small", small_update,
                               [pack_small(wsh), pack_small(msh), pack_small(vsh)],
                               [slots.reshape(8 * SMALL_ROWS, LANES)], [(LANES, F32)] * 4, [], SMALL_ROWS)
    grads, deltas, new_m, new_v = {}, {}, {}, {}
    for i, (n, shp) in enumerate(small_shapes):
        grads[n], deltas[n] = sg[i:i + 1, :shp[1]], sd[i:i + 1, :shp[1]]
        new_m[n], new_v[n] = sm[i:i + 1, :shp[1]], sv_[i:i + 1, :shp[1]]
    for n in BIG:
        grads[n] = gsh[n]
        r = wsh[n].shape[0]
        deltas[n], new_m[n], new_v[n] = _adamw("adamw_" + n, wsh[n], gsh[n], msh[n], vsh[n], _row_tile(r, 256))

    def shaped(d, n):
        return d[n][None] if n in BIG else d[n]

    return (loss, grad_x[None], *[shaped(grads, n) for n in WEIGHTS], *[shaped(deltas, n) for n in WEIGHTS],
            *[shaped(new_m, n) for n in WEIGHTS], *[shaped(new_v, n) for n in WEIGHTS])
```

```python
import functools

import jax
import jax.numpy as jnp
from jax import lax
from jax.experimental import pallas as pl
from jax.experimental.pallas import tpu as pltpu

F32 = jnp.float32
BF16 = jnp.bfloat16
MM_DTYPE = BF16
HI = lax.Precision.HIGHEST
MESH = pl.DeviceIdType.MESH

D_MODEL = 1024
D_FF = 2816
N_HEADS = 8
MLA_Q_RANK = 256
MLA_KV_RANK = 128
MLA_NOPE = 64
MLA_ROPE = 32
MLA_V = 64
ROPE_THETA = 10000.0
GDN_DH = 64
GDN_W = N_HEADS * GDN_DH
GDN_CONV = 4
CHUNK = 64
HEAD_LANES = 128
MLA_PAD = N_HEADS * HEAD_LANES
EPS = 1e-6
N_SHARD = 4
LANES = 1024

PIN_QKV = 0
PIN_MLA = 1536
PIN_KPE = 1920
PIN_GATE = 2048
PIN_AB = 2560
PIN_W = 2688
CAT_W = MLA_PAD + GDN_W

ADAM_LR = 0.001
ADAM_B1 = 0.9
ADAM_B2 = 0.999
ADAM_EPS = 1e-08
ADAM_WD = 0.01
ADAM_STEP = 10

VMEM_LIMIT_V7X = 56 * 1024 * 1024

BIG = ["ffn1_w_gate", "ffn1_w_up", "ffn1_w_down", "w_in", "mla_w_uq", "mla_w_ukv", "gdn_conv_w", "w_out",
       "ffn2_w_gate", "ffn2_w_up", "ffn2_w_down"]
FFN_BIG = ["ffn1_w_gate", "ffn1_w_up", "ffn1_w_down", "ffn2_w_gate", "ffn2_w_up", "ffn2_w_down"]
MIX_BIG = ["w_in", "mla_w_uq", "mla_w_ukv", "gdn_conv_w", "w_out"]
SMALL = ["ffn1_pre_g", "ffn1_post_g", "mix_pre_g", "mla_q_norm_g", "mla_kv_norm_g", "mla_out_g", "gdn_a_log",
         "gdn_dt_bias", "gdn_norm_g", "mix_post_g", "ffn2_pre_g", "ffn2_post_g"]
WEIGHTS = ["ffn1_pre_g", "ffn1_w_gate", "ffn1_w_up", "ffn1_w_down", "ffn1_post_g", "mix_pre_g", "w_in",
           "mla_q_norm_g", "mla_w_uq", "mla_kv_norm_g", "mla_w_ukv", "mla_out_g", "gdn_conv_w", "gdn_a_log",
           "gdn_dt_bias", "gdn_norm_g", "w_out", "mix_post_g", "ffn2_pre_g", "ffn2_w_gate", "ffn2_w_up",
           "ffn2_w_down", "ffn2_post_g"]
SHARD_AXIS = {"ffn1_w_gate": 1, "ffn1_w_up": 1, "ffn1_w_down": 0, "w_in": 1, "mla_w_uq": 1, "mla_w_ukv": 1,
              "gdn_conv_w": 1, "w_out": 0, "ffn2_w_gate": 1, "ffn2_w_up": 1, "ffn2_w_down": 0}
SMALL_ROWS = 16


def _params(sem):
    return pltpu.CompilerParams(dimension_semantics=sem, vmem_limit_bytes=VMEM_LIMIT_V7X)


def _pick(dim, pref):
    if dim <= pref:
        return dim
    t = (pref // 128) * 128
    while t >= 128:
        if dim % t == 0:
            return t
        t -= 128
    return dim


ANY_SPEC = pl.BlockSpec(memory_space=pl.ANY)


def _rowwise(name, fn, row_ins, bc_ins, row_outs, acc_outs, tb, wide=None, carry=None):
    ents = []
    for e in row_ins:
        ents.append(e if isinstance(e, tuple) else (e, e.shape[1], 0, 0))
    over = [o[2] for o in row_outs if len(o) == 3]
    rows = over[0] if over else ents[0][0].shape[0]
    steps = rows // tb
    assert steps * tb == rows, (name, rows, tb)
    in_specs, args = [], []
    for a, w, j, r0 in ents:
        in_specs.append(pl.BlockSpec((tb, w), lambda i, j=j, r0=r0: (i + r0, j)))
        args.append(a)
    for b in bc_ins:
        in_specs.append(pl.BlockSpec(b.shape, lambda i: (0, 0)))
        args.append(b)
    n_in = len(args)
    aliases = {}
    if carry is not None:
        in_specs.append(ANY_SPEC)
        args.append(carry)
        aliases = {n_in: 0}
    out_shape = [jax.ShapeDtypeStruct((rows, o[0]), o[1]) for o in row_outs]
    out_specs = [pl.BlockSpec((tb, o[0]), lambda i: (i, 0)) for o in row_outs]
    if wide is not None:
        out_shape[0] = jax.ShapeDtypeStruct((rows, wide[0]), row_outs[0][1])
        out_specs[0] = pl.BlockSpec((tb, row_outs[0][0]), lambda i: (i, wide[1]))
    out_shape += [jax.ShapeDtypeStruct((r, c), F32) for r, c in acc_outs]
    out_specs += [pl.BlockSpec((r, c), lambda i: (0, 0)) for r, c in acc_outs]
    n_ro, n_acc, n_args = len(row_outs), len(acc_outs), len(args)

    def body(*refs):
        vals = fn(*[r[...] for r in refs[:n_in]])
        if not isinstance(vals, (tuple, list)):
            vals = (vals,)
        for r, v in zip(refs[n_args:n_args + n_ro], vals[:n_ro]):
            r[...] = v.astype(r.dtype)
        if n_acc:
            acc_refs = refs[n_args + n_ro:]

            @pl.when(pl.program_id(0) == 0)
            def _():
                for r in acc_refs:
                    r[...] = jnp.zeros(r.shape, r.dtype)

            for r, v in zip(acc_refs, vals[n_ro:]):
                r[...] += v

    outs = pl.pallas_call(body, name=name, grid=(steps,), in_specs=in_specs, out_specs=out_specs,
                          out_shape=out_shape, input_output_aliases=aliases,
                          compiler_params=_params(("arbitrary",)))(*args)
    return outs


def _mm(name, a, b, mode, out_dtype, tm=512, tn=512, tk=1024):
    if mode == "nn":
        (m, k), (k2, n) = a.shape, b.shape
    elif mode == "nt":
        (m, k), (n, k2) = a.shape, b.shape
    else:
        (k, m), (k2, n) = a.shape, b.shape
    assert k == k2, (name, a.shape, b.shape)
    tm, tn, tk = _pick(m, tm), _pick(n, tn), _pick(k, tk)
    nk = k // tk
    if mode == "nn":
        a_spec = pl.BlockSpec((tm, tk), lambda i, j, kk: (i, kk))
        b_spec = pl.BlockSpec((tk, tn), lambda i, j, kk: (kk, j))
        dims = (((1,), (0,)), ((), ()))
    elif mode == "nt":
        a_spec = pl.BlockSpec((tm, tk), lambda i, j, kk: (i, kk))
        b_spec = pl.BlockSpec((tn, tk), lambda i, j, kk: (j, kk))
        dims = (((1,), (1,)), ((), ()))
    else:
        a_spec = pl.BlockSpec((tk, tm), lambda i, j, kk: (kk, i))
        b_spec = pl.BlockSpec((tk, tn), lambda i, j, kk: (kk, j))
        dims = (((0,), (0,)), ((), ()))

    def body(a_ref, b_ref, o_ref, acc_ref):
        kk = pl.program_id(2)

        @pl.when(kk == 0)
        def _():
            acc_ref[...] = jnp.zeros(acc_ref.shape, F32)

        acc_ref[...] += lax.dot_general(a_ref[...].astype(MM_DTYPE), b_ref[...].astype(MM_DTYPE), dims,
                                        preferred_element_type=F32)

        @pl.when(kk == nk - 1)
        def _():
            o_ref[...] = acc_ref[...].astype(o_ref.dtype)

    return pl.pallas_call(
        body, name=name, grid=(m // tm, n // tn, nk), in_specs=[a_spec, b_spec],
        out_specs=pl.BlockSpec((tm, tn), lambda i, j, kk: (i, j)),
        out_shape=jax.ShapeDtypeStruct((m, n), out_dtype),
        scratch_shapes=[pltpu.VMEM((tm, tn), F32)],
        compiler_params=_params(("parallel", "parallel", "arbitrary")))(a, b)


def _rms_stats(x, n_real=None):
    n = x.shape[-1] if n_real is None else n_real
    return lax.rsqrt(jnp.sum(x * x, axis=-1, keepdims=True) / n + EPS)


def _rms_bwd(x, r, g, dz, n_real=None):
    n = x.shape[-1] if n_real is None else n_real
    xh = x * r
    dxh = dz * g
    dx = r * (dxh - xh * (jnp.sum(dxh * xh, axis=-1, keepdims=True) / n))
    return dx, jnp.sum(dz * xh, axis=0, keepdims=True)


def _sigmoid(x):
    return 1.0 / (1.0 + jnp.exp(-x))


def _roll(x, s, axis):
    return pltpu.roll(x, s, axis)


def _rope(x, c, s1, s2):
    return x * c + _roll(x, HEAD_LANES - MLA_ROPE // 2, 1) * s1 + _roll(x, MLA_ROPE // 2, 1) * s2


def _heads_apply(x, fn):
    return jnp.concatenate([fn(x[:, h * HEAD_LANES:(h + 1) * HEAD_LANES]) for h in range(N_HEADS)], axis=1)


def _ffn_fwd(tag, x, g_pre, wg, wu, wd, g_post, tm):
    t, d = x.shape
    ns, _, fs = wg.shape
    nt = t // tm
    row = pl.BlockSpec((tm, d), lambda i, q: (i, 0))
    vec = pl.BlockSpec((1, d), lambda i, q: (0, 0))
    act3 = pl.BlockSpec((1, tm, fs), lambda i, q: (q, i, 0))
    wcol = pl.BlockSpec((1, d, fs), lambda i, q: (q, 0, 0))
    wrow = pl.BlockSpec((1, fs, d), lambda i, q: (q, 0, 0))

    def gate_up(x_ref, g_ref, wg_ref, wu_ref, n_ref, a_ref, u_ref, s_ref, n_s):
        @pl.when(pl.program_id(1) == 0)
        def _():
            xb = x_ref[...]
            n_s[...] = (xb * _rms_stats(xb) * g_ref[...]).astype(MM_DTYPE)
            n_ref[...] = n_s[...]

        n = n_s[...]
        a = jnp.dot(n, wg_ref[0], preferred_element_type=F32)
        u = jnp.dot(n, wu_ref[0], preferred_element_type=F32)
        a_ref[0] = a.astype(a_ref.dtype)
        u_ref[0] = u.astype(u_ref.dtype)
        s_ref[0] = ((a * _sigmoid(a)) * u).astype(s_ref.dtype)

    n, a, u, s = pl.pallas_call(
        gate_up, name=tag + "_gate_up", grid=(nt, ns), in_specs=[row, vec, wcol, wcol],
        out_specs=[row, act3, act3, act3],
        out_shape=[jax.ShapeDtypeStruct((t, d), MM_DTYPE)] + [jax.ShapeDtypeStruct((ns, t, fs), MM_DTYPE)] * 3,
        scratch_shapes=[pltpu.VMEM((tm, d), MM_DTYPE)],
        compiler_params=_params(("parallel", "arbitrary")))(x, g_pre, wg, wu)

    def down(s_ref, wd_ref, x_ref, g_ref, h_ref, y_ref, acc):
        q = pl.program_id(1)

        @pl.when(q == 0)
        def _():
            acc[...] = jnp.zeros(acc.shape, F32)

        acc[...] += jnp.dot(s_ref[0], wd_ref[0], preferred_element_type=F32)

        @pl.when(q == ns - 1)
        def _():
            hb = acc[...]
            h_ref[...] = hb
            y_ref[...] = x_ref[...] + 0.5 * (hb * _rms_stats(hb) * g_ref[...])

    h, y = pl.pallas_call(
        down, name=tag + "_down", grid=(nt, ns), in_specs=[act3, wrow, row, vec], out_specs=[row, row],
        out_shape=[jax.ShapeDtypeStruct((t, d), F32)] * 2, scratch_shapes=[pltpu.VMEM((tm, d), F32)],
        compiler_params=_params(("parallel", "arbitrary")))(s, wd, x, g_post)
    return y, (x, n, a, u, s, h)


def _ffn_bwd(tag, dy, saved, g_pre, wg, wu, wd, g_post, tm, tk):
    x, n, a, u, s, h = saved
    t, d = x.shape
    ns, _, fs = wg.shape
    nt, nk = t // tm, t // tk
    row = pl.BlockSpec((tm, d), lambda i, q: (i, 0))
    vec = pl.BlockSpec((1, d), lambda i, q: (0, 0))
    act3 = pl.BlockSpec((1, tm, fs), lambda i, q: (q, i, 0))
    wcol = pl.BlockSpec((1, d, fs), lambda i, q: (q, 0, 0))
    wrow = pl.BlockSpec((1, fs, d), lambda i, q: (q, 0, 0))
    nt_dims = (((1,), (1,)), ((), ()))
    tn_dims = (((0,), (0,)), ((), ()))

    def down_b(h_ref, dy_ref, g_ref, wd_ref, a_ref, u_ref, dh_ref, da_ref, du_ref, dg_ref, dh_s):
        i, q = pl.program_id(0), pl.program_id(1)

        @pl.when((i == 0) & (q == 0))
        def _():
            dg_ref[...] = jnp.zeros(dg_ref.shape, F32)

        @pl.when(q == 0)
        def _():
            hb = h_ref[...]
            dh, dg = _rms_bwd(hb, _rms_stats(hb), g_ref[...], 0.5 * dy_ref[...])
            dh_s[...] = dh.astype(MM_DTYPE)
            dh_ref[...] = dh_s[...]
            dg_ref[...] += dg

        ds = lax.dot_general(dh_s[...], wd_ref[0], nt_dims, preferred_element_type=F32)
        ab, ub = a_ref[0].astype(F32), u_ref[0].astype(F32)
        sg = _sigmoid(ab)
        da_ref[0] = (ds * ub * (sg * (1.0 + ab * (1.0 - sg)))).astype(da_ref.dtype)
        du_ref[0] = (ds * (ab * sg)).astype(du_ref.dtype)

    dh, da, du, dg_post = pl.pallas_call(
        down_b, name=tag + "_down_b", grid=(nt, ns), in_specs=[row, row, vec, wrow, act3, act3],
        out_specs=[row, act3, act3, vec],
        out_shape=[jax.ShapeDtypeStruct((t, d), MM_DTYPE)] + [jax.ShapeDtypeStruct((ns, t, fs), MM_DTYPE)] * 2
        + [jax.ShapeDtypeStruct((1, d), F32)],
        scratch_shapes=[pltpu.VMEM((tm, d), MM_DTYPE)],
        compiler_params=_params(("arbitrary", "arbitrary")))(h, dy, g_post, wd, a, u)

    def down_w(s_ref, dh_ref, dw_ref, acc):
        kk = pl.program_id(1)

        @pl.when(kk == 0)
        def _():
            acc[...] = jnp.zeros(acc.shape, F32)

        acc[...] += lax.dot_general(s_ref[0], dh_ref[...], tn_dims, preferred_element_type=F32)

        @pl.when(kk == nk - 1)
        def _():
            dw_ref[0] = acc[...].astype(dw_ref.dtype)

    dwd = pl.pallas_call(
        down_w, name=tag + "_down_w", grid=(ns, nk),
        in_specs=[pl.BlockSpec((1, tk, fs), lambda q, kk: (q, kk, 0)), pl.BlockSpec((tk, d), lambda q, kk: (kk, 0))],
        out_specs=pl.BlockSpec((1, fs, d), lambda q, kk: (q, 0, 0)),
        out_shape=jax.ShapeDtypeStruct((ns, fs, d), MM_DTYPE), scratch_shapes=[pltpu.VMEM((fs, d), F32)],
        compiler_params=_params(("parallel", "arbitrary")))(s, dh)

    def gate_up_b(da_ref, du_ref, wg_ref, wu_ref, x_ref, dy_ref, g_ref, dx_ref, dg_ref, acc):
        i, q = pl.program_id(0), pl.program_id(1)

        @pl.when((i == 0) & (q == 0))
        def _():
            dg_ref[...] = jnp.zeros(dg_ref.shape, F32)

        @pl.when(q == 0)
        def _():
            acc[...] = jnp.zeros(acc.shape, F32)

        acc[...] += (lax.dot_general(da_ref[0], wg_ref[0], nt_dims, preferred_element_type=F32)
                     + lax.dot_general(du_ref[0], wu_ref[0], nt_dims, preferred_element_type=F32))

        @pl.when(q == ns - 1)
        def _():
            xb = x_ref[...]
            dx, dg = _rms_bwd(xb, _rms_stats(xb), g_ref[...], acc[...])
            dx_ref[...] = dy_ref[...] + dx
            dg_ref[...] += dg

    dx, dg_pre = pl.pallas_call(
        gate_up_b, name=tag + "_gate_up_b", grid=(nt, ns), in_specs=[act3, act3, wcol, wcol, row, row, vec],
        out_specs=[row, vec], out_shape=[jax.ShapeDtypeStruct((t, d), F32), jax.ShapeDtypeStruct((1, d), F32)],
        scratch_shapes=[pltpu.VMEM((tm, d), F32)],
        compiler_params=_params(("arbitrary", "arbitrary")))(da, du, wg, wu, x, dy, g_pre)

    def gate_up_w(n_ref, da_ref, du_ref, dwg_ref, dwu_ref, acc_g, acc_u):
        kk = pl.program_id(1)

        @pl.when(kk == 0)
        def _():
            acc_g[...] = jnp.zeros(acc_g.shape, F32)
            acc_u[...] = jnp.zeros(acc_u.shape, F32)

        nb = n_ref[...]
        acc_g[...] += lax.dot_general(nb, da_ref[0], tn_dims, preferred_element_type=F32)
        acc_u[...] += lax.dot_general(nb, du_ref[0], tn_dims, preferred_element_type=F32)

        @pl.when(kk == nk - 1)
        def _():
            dwg_ref[0] = acc_g[...].astype(dwg_ref.dtype)
            dwu_ref[0] = acc_u[...].astype(dwu_ref.dtype)

    k3 = pl.BlockSpec((1, tk, fs), lambda q, kk: (q, kk, 0))
    wout = pl.BlockSpec((1, d, fs), lambda q, kk: (q, 0, 0))
    dwg, dwu = pl.pallas_call(
        gate_up_w, name=tag + "_gate_up_w", grid=(ns, nk),
        in_specs=[pl.BlockSpec((tk, d), lambda q, kk: (kk, 0)), k3, k3], out_specs=[wout, wout],
        out_shape=[jax.ShapeDtypeStruct((ns, d, fs), MM_DTYPE)] * 2,
        scratch_shapes=[pltpu.VMEM((d, fs), F32)] * 2,
        compiler_params=_params(("parallel", "arbitrary")))(n, da, du)
    return dx, dg_pre, dwg, dwu, dwd, dg_post


NEG = -1e30


def _attn_scale():
    return (MLA_NOPE + MLA_ROPE) ** -0.5


def _causal_pairs(nq, by_key):
    if by_key:
        pairs = [(qi, ki) for ki in range(nq) for qi in range(ki, nq)]
    else:
        pairs = [(qi, ki) for qi in range(nq) for ki in range(qi + 1)]
    return jnp.asarray([p[0] for p in pairs], jnp.int32), jnp.asarray([p[1] for p in pairs], jnp.int32)


def _below_diagonal(shape):
    return lax.broadcasted_iota(jnp.int32, shape, 1) <= lax.broadcasted_iota(jnp.int32, shape, 0)


def _attn_call(name, body, tables, args, in_kinds, out_kinds, scratch, t, tq):
    qmap = lambda h, p, qt, kt: (qt[p], h)
    kmap = lambda h, p, qt, kt: (kt[p], h)
    spec = lambda kind: pl.BlockSpec((tq, HEAD_LANES), qmap if kind == "q" else kmap)
    grid_spec = pltpu.PrefetchScalarGridSpec(
        num_scalar_prefetch=2, grid=(N_HEADS, tables[0].shape[0]), in_specs=[spec(kd) for kd in in_kinds],
        out_specs=[spec(kd) for kd in out_kinds], scratch_shapes=scratch)
    return pl.pallas_call(body, name=name, grid_spec=grid_spec,
                          out_shape=[jax.ShapeDtypeStruct((t, MLA_PAD), F32) for _ in out_kinds],
                          compiler_params=_params(("parallel", "arbitrary")))(*tables, *args)


def _attn_fwd(q, k, v, tq):
    t = q.shape[0]
    nq = t // tq

    def body(qt, kt, q_ref, k_ref, v_ref, o_ref, lse_ref, m_s, l_s, acc_s):
        p_id = pl.program_id(1)
        qi, ki = qt[p_id], kt[p_id]

        @pl.when(ki == 0)
        def _():
            m_s[...] = jnp.full(m_s.shape, NEG, F32)
            l_s[...] = jnp.zeros(l_s.shape, F32)
            acc_s[...] = jnp.zeros(acc_s.shape, F32)

        def scores():
            return lax.dot_general(q_ref[...], k_ref[...], (((1,), (1,)), ((), ())), preferred_element_type=F32)

        def update(s):
            m_old = m_s[...]
            m_new = jnp.maximum(m_old, jnp.max(s, axis=1, keepdims=True))
            alpha = jnp.exp(m_old - m_new)
            p = jnp.exp(s - m_new[:, :1])
            l_s[...] = l_s[...] * alpha + jnp.sum(p, axis=1, keepdims=True)
            acc_s[...] = acc_s[...] * alpha + jnp.dot(p.astype(MM_DTYPE), v_ref[...], preferred_element_type=F32)
            m_s[...] = m_new

        @pl.when(ki < qi)
        def _():
            update(scores())

        @pl.when(ki == qi)
        def _():
            s = scores()
            update(jnp.where(_below_diagonal(s.shape), s, NEG))
            o_ref[...] = acc_s[...] / l_s[...]
            lse_ref[...] = m_s[...] + jnp.log(l_s[...])

    return _attn_call("mla_attn_fwd", body, _causal_pairs(nq, False), (q, k, v), "qkk", "qq",
                      [pltpu.VMEM((tq, HEAD_LANES), F32)] * 3, t, tq)


def _attn_probs(q, k, lse, diagonal):
    s = lax.dot_general(q, k, (((1,), (1,)), ((), ())), preferred_element_type=F32)
    p = jnp.exp(s - lse[:, :1])
    return jnp.where(_below_diagonal(s.shape), p, 0.0) if diagonal else p


def _attn_bwd_dq(q, k, v, do, lse, delta, tq):
    t = q.shape[0]
    nq = t // tq

    def body(qt, kt, q_ref, k_ref, v_ref, do_ref, lse_ref, dl_ref, dq_ref, acc_s):
        p_id = pl.program_id(1)
        qi, ki = qt[p_id], kt[p_id]

        @pl.when(ki == 0)
        def _():
            acc_s[...] = jnp.zeros(acc_s.shape, F32)

        def step(diagonal):
            p = _attn_probs(q_ref[...], k_ref[...], lse_ref[...], diagonal)
            dp = lax.dot_general(do_ref[...], v_ref[...], (((1,), (1,)), ((), ())), preferred_element_type=F32)
            ds = p * (dp - dl_ref[...][:, :1])
            acc_s[...] += jnp.dot(ds.astype(MM_DTYPE), k_ref[...], preferred_element_type=F32)

        @pl.when(ki < qi)
        def _():
            step(False)

        @pl.when(ki == qi)
        def _():
            step(True)
            dq_ref[...] = acc_s[...]

    return _attn_call("mla_attn_bwd_dq", body, _causal_pairs(nq, False), (q, k, v, do, lse, delta), "qkkqqq", "q",
                      [pltpu.VMEM((tq, HEAD_LANES), F32)], t, tq)[0]


def _attn_bwd_dkv(q, k, v, do, lse, delta, tq):
    t = q.shape[0]
    nq = t // tq

    def body(qt, kt, q_ref, k_ref, v_ref, do_ref, lse_ref, dl_ref, dk_ref, dv_ref, dk_s, dv_s):
        p_id = pl.program_id(1)
        qi, ki = qt[p_id], kt[p_id]

        def step(diagonal):
            p = _attn_probs(q_ref[...], k_ref[...], lse_ref[...], diagonal)
            dv_s[...] += lax.dot_general(p.astype(MM_DTYPE), do_ref[...], (((0,), (0,)), ((), ())),
                                         preferred_element_type=F32)
            dp = lax.dot_general(do_ref[...], v_ref[...], (((1,), (1,)), ((), ())), preferred_element_type=F32)
            ds = p * (dp - dl_ref[...][:, :1])
            dk_s[...] += lax.dot_general(ds.astype(MM_DTYPE), q_ref[...], (((0,), (0,)), ((), ())),
                                         preferred_element_type=F32)

        @pl.when(qi == ki)
        def _():
            dk_s[...] = jnp.zeros(dk_s.shape, F32)
            dv_s[...] = jnp.zeros(dv_s.shape, F32)
            step(True)

        @pl.when(qi > ki)
        def _():
            step(False)

        @pl.when(qi == nq - 1)
        def _():
            dk_ref[...] = dk_s[...]
            dv_ref[...] = dv_s[...]

    return _attn_call("mla_attn_bwd_dkv", body, _causal_pairs(nq, True), (q, k, v, do, lse, delta), "qkkqqq", "kk",
                      [pltpu.VMEM((tq, HEAD_LANES), F32)] * 2, t, tq)


def _dotf(a, b, dims=(((1,), (0,)), ((), ()))):
    return lax.dot_general(a, b, dims, preferred_element_type=F32, precision=HI)


def _dot1(a, b, dims=(((1,), (0,)), ((), ()))):
    return lax.dot_general(a.astype(MM_DTYPE), b.astype(MM_DTYPE), dims, preferred_element_type=F32)


def _dot3(a, b, dims=(((1,), (0,)), ((), ()))):
    return lax.dot_general(a, b, dims, preferred_element_type=F32, precision=lax.Precision.HIGH)


NN3 = (((2,), (1,)), ((0,), (0,)))
NT3 = (((2,), (2,)), ((0,), (0,)))
TN3 = (((1,), (1,)), ((0,), (0,)))


def _tri_masks(nh):
    shape = (nh, CHUNK, CHUNK)
    return lax.broadcasted_iota(jnp.int32, shape, 1), lax.broadcasted_iota(jnp.int32, shape, 2)


def _gdn_chunk_common(k, gcc, bb, row, col, dot=_dot1):
    tril = row >= col
    gcr = jnp.swapaxes(gcc, 1, 2)
    dm = jnp.exp(jnp.where(tril, gcc - gcr, NEG))
    kb = k * bb
    lm = jnp.where(row > col, dot(kb, k, NT3) * dm, 0.0)
    return dm, kb, lm


def _unit_lower_inverse(lm, eye):
    t = eye - lm
    p = lm
    for _ in range(CHUNK.bit_length() - 2):
        p = _dot3(p, p, NN3)
        t = t + _dot3(t, p, NN3)
    return t


def _chunk_sum_matrix(tb, upper):
    r = lax.broadcasted_iota(jnp.int32, (tb, tb), 0)
    c = lax.broadcasted_iota(jnp.int32, (tb, tb), 1)
    same = (r // CHUNK) == (c // CHUNK)
    return (same & ((c >= r) if upper else (c <= r))).astype(F32)


def _gdn_fwd(q, k, v, gb, bb):
    nh, t, dh = q.shape
    nchunk = t // CHUNK

    def body(q_ref, k_ref, v_ref, g_ref, b_ref, o_ref, sall_ref, tall_ref, s_s):
        @pl.when(pl.program_id(0) == 0)
        def _():
            s_s[...] = jnp.zeros(s_s.shape, F32)

        row, col = _tri_masks(nh)
        qh, kh, vh, bbh, gcc = q_ref[...], k_ref[...], v_ref[...], b_ref[...], g_ref[...]
        dm, kb, lm = _gdn_chunk_common(kh, gcc, bbh, row, col)
        eg = jnp.exp(gcc)
        glr = gcc[:, CHUNK - 1:CHUNK, :]
        th = _unit_lower_inverse(lm, (row == col).astype(F32))
        w = _dot1(th, kb * eg, NN3)
        u = _dot1(th, vh * bbh, NN3)
        at = jnp.where(row >= col, _dot1(qh, kh, NT3) * dm, 0.0)
        sh = s_s[...]
        vn = u - _dot1(w, sh, NN3)
        o_ref[...] = _dot1(qh * eg, sh, NN3) + _dot1(at, vn, NN3)
        kd = kh * jnp.exp(glr - gcc)
        sall_ref[:, 0] = sh
        tall_ref[...] = th
        s_s[...] = sh * jnp.exp(glr) + _dot1(kd, vn, TN3)

    blk = pl.BlockSpec((nh, CHUNK, dh), lambda n: (0, n, 0))
    return pl.pallas_call(
        body, name="gdn_fwd", grid=(nchunk,), in_specs=[blk] * 5,
        out_specs=[blk, pl.BlockSpec((nh, 1, dh, dh), lambda n: (0, n, 0, 0)), blk],
        out_shape=[jax.ShapeDtypeStruct((nh, t, dh), F32), jax.ShapeDtypeStruct((nh, nchunk, dh, dh), F32),
                   jax.ShapeDtypeStruct((nh, t, CHUNK), F32)],
        scratch_shapes=[pltpu.VMEM((nh, dh, dh), F32)],
        compiler_params=_params(("arbitrary",)))(q, k, v, gb, bb)


def _gdn_bwd(q, k, v, gb, bb, sall, tall, do):
    nh, t, dh = q.shape
    nchunk = t // CHUNK

    def body(q_ref, k_ref, v_ref, g_ref, b_ref, sall_ref, tall_ref, do_ref,
             dq_ref, dk_ref, dv_ref, dg_ref, db_ref, ds_s):
        @pl.when(pl.program_id(0) == 0)
        def _():
            ds_s[...] = jnp.zeros(ds_s.shape, F32)

        row, col = _tri_masks(nh)
        tril, stril = row >= col, row > col
        rsum = lambda x: jnp.sum(x, axis=2, keepdims=True)
        qh, kh, vh, gcc, bbh = q_ref[...], k_ref[...], v_ref[...], g_ref[...], b_ref[...]
        sh, th, doh, dsp = sall_ref[:, 0], tall_ref[...], do_ref[...], ds_s[...]
        dm, kb, lm = _gdn_chunk_common(kh, gcc, bbh, row, col, _dot3)
        eg = jnp.exp(gcc)
        glr = gcc[:, CHUNK - 1:CHUNK, :]
        glv = jnp.exp(glr)
        egl = jnp.exp(glr - gcc)
        rw, ru = kb * eg, vh * bbh
        w, u = _dot3(th, rw, NN3), _dot3(th, ru, NN3)
        at = jnp.where(tril, _dot3(qh, kh, NT3) * dm, 0.0)
        qd, kd = qh * eg, kh * egl
        vn = u - _dot3(w, sh, NN3)
        dgl = jnp.sum(rsum(dsp * sh), axis=1, keepdims=True)
        dkd = _dot3(vn, dsp, NT3)
        dvn = _dot3(kd, dsp, NN3)
        dqd = _dot3(doh, sh, NT3)
        dat = jnp.where(tril, _dot3(doh, vn, NT3), 0.0)
        dvn = dvn + _dot3(at, doh, TN3)
        dw = -_dot3(dvn, sh, NT3)
        ds_s[...] = dsp * glv + _dot3(qd, doh, TN3) - _dot3(w, dvn, TN3)
        dpa = dat * dm
        dq_ref[...] = _dot3(dpa, kh, NN3) + dqd * eg
        dk = _dot3(dpa, qh, TN3) + dkd * egl
        t6 = rsum(dkd * kd)
        dgam = rsum(dqd * qd) - t6
        dgam_last = jnp.sum(t6, axis=1, keepdims=True) + dgl * glv
        drw = _dot3(th, dw, TN3)
        dru = _dot3(th, dvn, TN3)
        dl = -jnp.where(stril, _dot3(drw, w, NT3) + _dot3(dru, u, NT3), 0.0)
        dgam = dgam + rsum(drw * rw)
        dv_ref[...] = dru * bbh
        dp2 = dl * dm
        dkb = drw * eg + _dot3(dp2, kh, NN3)
        dk_ref[...] = dk + _dot3(dp2, kb, TN3) + dkb * bbh
        db_ref[...] = rsum(dru * vh) + rsum(dkb * kh) + jnp.zeros((nh, CHUNK, dh), F32)
        e = dat * at + dl * lm
        dgam_b = dgam + rsum(e) - _dotf(e, jnp.ones((nh, CHUNK, CHUNK), F32), TN3)
        dgam_b = dgam_b + jnp.where(row == CHUNK - 1, dgam_last, 0.0)
        dg_ref[...] = dgam_b

    rev = lambda n: (0, nchunk - 1 - n, 0)
    blk = pl.BlockSpec((nh, CHUNK, dh), rev)
    sblk = pl.BlockSpec((nh, 1, dh, dh), lambda n: (0, nchunk - 1 - n, 0, 0))
    out = jax.ShapeDtypeStruct((nh, t, dh), F32)
    return pl.pallas_call(
        body, name="gdn_bwd", grid=(nchunk,), in_specs=[blk] * 5 + [sblk, blk, blk], out_specs=[blk] * 5,
        out_shape=[out] * 5, scratch_shapes=[pltpu.VMEM((nh, dh, dh), F32)],
        compiler_params=_params(("arbitrary",)))(q, k, v, gb, bb, sall, tall, do)


def _group_ones():
    r = lax.broadcasted_iota(jnp.int32, (GDN_W, GDN_W), 0) // GDN_DH
    c = lax.broadcasted_iota(jnp.int32, (GDN_W, GDN_W), 1) // GDN_DH
    return (r == c).astype(F32)


def _conv_taps(x, xprev, w, has_prev):
    row = lax.broadcasted_iota(jnp.int32, x.shape, 0)
    out = x * w[GDN_CONV - 1:GDN_CONV, :]
    for s in range(1, GDN_CONV):
        sh = jnp.where(row >= s, _roll(x, s, 0), _roll(xprev, s, 0) * has_prev)
        out = out + sh * w[GDN_CONV - 1 - s:GDN_CONV - s, :]
    return out


def _head_cols(x, h):
    return x[:, h * GDN_DH:(h + 1) * GDN_DH]


def _heads_spec(tb):
    return pl.BlockSpec((N_HEADS, tb, GDN_DH), lambda i: (0, i, 0))


def _mixer_fwd(x, positions, w, tb):
    t, d = x.shape
    tables = _rope_tables(positions)

    def pre(xb, g):
        return (xb * _rms_stats(xb) * g,)

    (hn,) = _rowwise("mix_pre", pre, [x], [w["mix_pre_g"]], [(d, BF16)], [], tb)
    proj = _mm("mix_in", hn, w["w_in_pad"], "nn", F32)

    def mla_pre(p0, gq, gkv):
        cq, ckv = p0[:, :MLA_Q_RANK], p0[:, MLA_Q_RANK:MLA_Q_RANK + MLA_KV_RANK]
        return cq * _rms_stats(cq) * gq, ckv * _rms_stats(ckv) * gkv

    nq, nkv = _rowwise("mla_pre", mla_pre, [(proj, 512, PIN_MLA // 512, 0)],
                       [w["mla_q_norm_g"], w["mla_kv_norm_g"]], [(MLA_Q_RANK, BF16), (MLA_KV_RANK, BF16)], [], tb)
    qraw = _mm("mla_uq", nq, w["w_uq_pad"], "nn", F32)
    kv = _mm("mla_ukv", nkv, w["w_kv_pad"], "nn", F32)

    def rope_f(qr, kn, vv, kpe, c, s1, s2):
        qo = _heads_apply(qr, lambda xh: _rope(xh, c, s1, s2)) * _attn_scale()
        kp = _rope(kpe, c, s1, s2)
        return qo, kn + jnp.tile(kp, (1, N_HEADS)), vv

    q, k, v = _rowwise("mla_rope", rope_f,
                       [qraw, (kv, MLA_PAD, 0, 0), (kv, MLA_PAD, 1, 0), (proj, HEAD_LANES, PIN_KPE // HEAD_LANES, 0),
                        tables[0], tables[1], tables[2]], [],
                       [(MLA_PAD, BF16)] * 3, [], tb // 2)
    tq = min(512, t)
    o, lse = _attn_fwd(q, k, v, tq)

    def mla_post(ob, g):
        return (ob * _rms_stats(ob, N_HEADS * MLA_V) * g,)

    (cat,) = _rowwise("mla_post", mla_post, [o], [w["mla_out_g_pad"]], [(MLA_PAD, BF16)], [], tb, wide=(CAT_W, 0))

    gones = _group_ones()
    steps = t // tb

    def gdn_pre(xq, xk, xv, pq, pk, pv, cw, go, has_prev):
        outs = []
        for j, (xc, xp) in enumerate(((xq, pq), (xk, pk), (xv, pv))):
            c = _conv_taps(xc, xp, cw[:, j * GDN_W:(j + 1) * GDN_W], has_prev)
            a = c * _sigmoid(c)
            if j < 2:
                rn = lax.rsqrt(_dotf(a * a, go) + EPS)
                a = a * rn
                if j == 0:
                    a = a * (GDN_DH ** -0.5)
            outs.append(a)
        return tuple(outs)

    qh, kh, vh = _gdn_pre_call("gdn_pre", gdn_pre, proj, w["conv_w"], gones, tb, steps)
    heads_shape = jax.ShapeDtypeStruct((N_HEADS, t, GDN_DH), F32)
    lanes_shape = jax.ShapeDtypeStruct((t, HEAD_LANES), F32)
    lanes_spec = pl.BlockSpec((tb, HEAD_LANES), lambda i: (i, 0))
    vec_spec = lambda n: pl.BlockSpec((1, n), lambda i: (0, 0))

    def gate_f(ab_ref, al_ref, dt_ref, g_ref, b_ref, gh_ref, bh_ref):
        g, b = _gb_fwd(ab_ref[...], al_ref[...], dt_ref[...])
        g_ref[...] = g
        b_ref[...] = b
        gc = _dotf(_chunk_sum_matrix(tb, False), g)
        for h in range(N_HEADS):
            gh_ref[h] = jnp.broadcast_to(gc[:, h:h + 1], (tb, GDN_DH))
            bh_ref[h] = jnp.broadcast_to(b[:, N_HEADS + h:N_HEADS + h + 1], (tb, GDN_DH))

    g128, b128, gbh, bbh = pl.pallas_call(
        gate_f, name="gdn_gate_f", grid=(steps,),
        in_specs=[pl.BlockSpec((tb, HEAD_LANES), lambda i: (i, PIN_AB // HEAD_LANES)), vec_spec(HEAD_LANES),
                  vec_spec(HEAD_LANES)],
        out_specs=[lanes_spec, lanes_spec, _heads_spec(tb), _heads_spec(tb)],
        out_shape=[lanes_shape, lanes_shape, heads_shape, heads_shape],
        compiler_params=_params(("arbitrary",)))(proj, w["a_log_pad"], w["dt_bias_pad"])
    oh, sall, tall = _gdn_fwd(qh, kh, vh, gbh, bbh)

    def gdn_post(o_ref, gt_ref, g_ref, cat_in, cat_ref):
        gt, g = gt_ref[...], g_ref[...]
        outs = []
        for h in range(N_HEADS):
            ob, gth = o_ref[h], _head_cols(gt, h)
            outs.append(ob * _rms_stats(ob) * g * (gth * _sigmoid(gth)))
        cat_ref[...] = jnp.concatenate(outs, axis=1).astype(cat_ref.dtype)

    gate_spec = pl.BlockSpec((tb, GDN_W), lambda i: (i, PIN_GATE // GDN_W))
    cat = pl.pallas_call(
        gdn_post, name="gdn_post", grid=(steps,),
        in_specs=[_heads_spec(tb), gate_spec, vec_spec(GDN_DH), ANY_SPEC],
        out_specs=pl.BlockSpec((tb, GDN_W), lambda i: (i, MLA_PAD // GDN_W)),
        out_shape=jax.ShapeDtypeStruct((t, CAT_W), BF16), input_output_aliases={3: 0},
        compiler_params=_params(("arbitrary",)))(oh, proj, w["gdn_norm_g"], cat)
    mixed = _mm("mix_out", cat, w["w_out_pad"], "nn", F32)

    def post(xb, hb, g):
        return (xb + hb * _rms_stats(hb) * g,)

    (y,) = _rowwise("mix_post", post, [x, mixed], [w["mix_post_g"]], [(d, F32)], [], tb)
    saved = dict(x=x, hn=hn, proj=proj, nq=nq, nkv=nkv, q=q, k=k, v=v, o=o, lse=lse, qh=qh, kh=kh, vh=vh,
                 gbh=gbh, bbh=bbh, oh=oh, sall=sall, tall=tall, cat=cat, mixed=mixed,
                 tables=tables, g128=g128, b128=b128)
    return y, saved


def _qkv_specs(tb):
    base = PIN_QKV // GDN_W
    cur = [pl.BlockSpec((tb, GDN_W), lambda i, j=j: (i, base + j)) for j in range(3)]
    prev = [pl.BlockSpec((tb, GDN_W), lambda i, j=j: (jnp.maximum(i - 1, 0), base + j)) for j in range(3)]
    return cur + prev


def _gdn_pre_call(name, fn, proj, conv_w, gones, tb, steps):
    t = proj.shape[0]

    def body(xq, xk, xv, pq, pk, pv, cw, go, oq, ok, ov):
        has_prev = jnp.where(pl.program_id(0) == 0, 0.0, 1.0)
        outs = fn(xq[...], xk[...], xv[...], pq[...], pk[...], pv[...], cw[...], go[...], has_prev)
        for r, val in zip((oq, ok, ov), outs):
            for h in range(N_HEADS):
                r[h] = _head_cols(val, h)

    return pl.pallas_call(
        body, name=name, grid=(steps,),
        in_specs=_qkv_specs(tb) + [pl.BlockSpec(conv_w.shape, lambda i: (0, 0)),
                                   pl.BlockSpec(gones.shape, lambda i: (0, 0))],
        out_specs=[_heads_spec(tb)] * 3,
        out_shape=[jax.ShapeDtypeStruct((N_HEADS, t, GDN_DH), F32)] * 3,
        compiler_params=_params(("arbitrary",)))(proj, proj, proj, proj, proj, proj, conv_w, gones)


def _softplus(x):
    return jnp.maximum(x, 0.0) + jnp.log1p(jnp.exp(-jnp.abs(x)))


def _gb_fwd(ab, a_log, dt_bias):
    g = -jnp.exp(a_log) * _softplus(ab + dt_bias)
    return g, _sigmoid(ab)


def _rope_tables(positions):
    half = MLA_ROPE // 2
    freqs = ROPE_THETA ** (-jnp.arange(half, dtype=F32) / half)
    ang = positions.reshape(-1).astype(F32)[:, None] * freqs
    cos, sin = jnp.cos(ang), jnp.sin(ang)
    t = ang.shape[0]
    one = jnp.ones((t, MLA_NOPE), F32)
    z16, z32, z64 = jnp.zeros((t, half), F32), jnp.zeros((t, MLA_ROPE), F32), jnp.zeros((t, MLA_NOPE), F32)
    c = jnp.concatenate([one, cos, cos, jnp.ones((t, MLA_ROPE), F32)], axis=1)
    s1 = jnp.concatenate([z64, -sin, z16, z32], axis=1)
    s2 = jnp.concatenate([z64, z16, sin, z32], axis=1)
    return c, s1, s2


def _mixer_bwd(dy, sv, w, tb):
    x, proj = sv["x"], sv["proj"]
    t, d = x.shape
    c, s1, s2 = sv["tables"]
    grads = {}

    def post_b(hb, dyb, g):
        return _rms_bwd(hb, _rms_stats(hb), g, dyb)

    dmixed, grads["mix_post_g"] = _rowwise("mix_post_b", post_b, [sv["mixed"], dy], [w["mix_post_g"]],
                                           [(d, BF16)], [(1, d)], tb)
    dcat = _mm("mix_out_bx", dmixed, w["w_out_pad"], "nt", F32)
    grads["w_out_pad"] = _mm("mix_out_bw", sv["cat"], dmixed, "tn", F32)
    steps = t // tb
    vec_spec = lambda n: pl.BlockSpec((1, n), lambda i: (0, 0))

    def gdn_post_b(o_ref, gt_ref, do_ref, g_ref, dproj_ref, doh_ref, dg_ref):
        @pl.when(pl.program_id(0) == 0)
        def _():
            dg_ref[...] = jnp.zeros(dg_ref.shape, F32)

        gt, dob, g = gt_ref[...], do_ref[...], g_ref[...]
        dgates = []
        for h in range(N_HEADS):
            ob, gth, dobh = o_ref[h], _head_cols(gt, h), _head_cols(dob, h)
            sg = _sigmoid(gth)
            r = _rms_stats(ob)
            dxo, dg = _rms_bwd(ob, r, g, dobh * (gth * sg))
            doh_ref[h] = dxo
            dg_ref[...] += dg
            dgates.append(dobh * (ob * r * g) * (sg * (1.0 + gth * (1.0 - sg))))
        dproj_ref[...] = jnp.concatenate(dgates, axis=1).astype(dproj_ref.dtype)

    dproj, doh, grads["gdn_norm_g"] = pl.pallas_call(
        gdn_post_b, name="gdn_post_b", grid=(steps,),
        in_specs=[_heads_spec(tb), pl.BlockSpec((tb, GDN_W), lambda i: (i, PIN_GATE // GDN_W)),
                  pl.BlockSpec((tb, GDN_W), lambda i: (i, MLA_PAD // GDN_W)), vec_spec(GDN_DH)],
        out_specs=[pl.BlockSpec((tb, GDN_W), lambda i: (i, PIN_GATE // GDN_W)), _heads_spec(tb), vec_spec(GDN_DH)],
        out_shape=[jax.ShapeDtypeStruct((t, PIN_W), BF16), jax.ShapeDtypeStruct((N_HEADS, t, GDN_DH), F32),
                   jax.ShapeDtypeStruct((1, GDN_DH), F32)],
        compiler_params=_params(("arbitrary",)))(sv["oh"], proj, dcat, w["gdn_norm_g"])

    def mla_post_b(ob, dmo, g):
        do, dg = _rms_bwd(ob, _rms_stats(ob, N_HEADS * MLA_V), g, dmo, N_HEADS * MLA_V)
        prod = do * ob
        delta = _heads_apply(prod, lambda ph: jnp.sum(ph, axis=1, keepdims=True) + jnp.zeros_like(ph))
        return do, delta, dg

    do, delta, grads["mla_out_g_pad"] = _rowwise(
        "mla_post_b", mla_post_b, [sv["o"], (dcat, MLA_PAD, 0, 0)], [w["mla_out_g_pad"]],
        [(MLA_PAD, BF16), (MLA_PAD, F32)], [(1, MLA_PAD)], tb // 2)
    tq = min(512, t)
    dq = _attn_bwd_dq(sv["q"], sv["k"], sv["v"], do, sv["lse"], delta, tq)
    dk, dv = _attn_bwd_dkv(sv["q"], sv["k"], sv["v"], do, sv["lse"], delta, tq)

    def rope_b(dqb, dkb, dvb, cc, a1, a2):
        dqr = _heads_apply(dqb * _attn_scale(), lambda xh: _rope(xh, cc, -a1, -a2))
        ksum = dkb[:, :HEAD_LANES]
        for h in range(1, N_HEADS):
            ksum = ksum + dkb[:, h * HEAD_LANES:(h + 1) * HEAD_LANES]
        lane = lax.broadcasted_iota(jnp.int32, ksum.shape, 1)
        keep = (lane >= MLA_NOPE) & (lane < MLA_NOPE + MLA_ROPE)
        dkpe = jnp.where(keep, _rope(ksum, cc, -a1, -a2), 0.0)
        return dqr, jnp.concatenate([dkb, dvb], axis=1), dkpe

    dqraw, dkv, dkpe = _rowwise("mla_rope_b", rope_b, [dq, dk, dv, c, s1, s2], [],
                                [(MLA_PAD, BF16), (2 * MLA_PAD, BF16), (HEAD_LANES, F32)], [], tb // 2)
    dnq = _mm("mla_uq_bx", dqraw, w["w_uq_pad"], "nt", F32)
    grads["w_uq_pad"] = _mm("mla_uq_bw", sv["nq"], dqraw, "tn", F32)
    dnkv = _mm("mla_ukv_bx", dkv, w["w_kv_pad"], "nt", F32)
    grads["w_kv_pad"] = _mm("mla_ukv_bw", sv["nkv"], dkv, "tn", F32)

    def mla_pre_b(p0, dnqb, dnkvb, dkpeb, gq, gkv):
        cq, ckv = p0[:, :MLA_Q_RANK], p0[:, MLA_Q_RANK:MLA_Q_RANK + MLA_KV_RANK]
        dcq, dgq = _rms_bwd(cq, _rms_stats(cq), gq, dnqb)
        dckv, dgkv = _rms_bwd(ckv, _rms_stats(ckv), gkv, dnkvb)
        return jnp.concatenate([dcq, dckv, dkpeb], axis=1), dgq, dgkv

    dproj, grads["mla_q_norm_g"], grads["mla_kv_norm_g"] = _rowwise(
        "mla_pre_b", mla_pre_b, [(proj, 512, PIN_MLA // 512, 0), dnq, dnkv, dkpe],
        [w["mla_q_norm_g"], w["mla_kv_norm_g"]], [(512, BF16)], [(1, MLA_Q_RANK), (1, MLA_KV_RANK)], tb,
        wide=(PIN_W, PIN_MLA // 512), carry=dproj)

    dqh, dkh, dvh, dgh, dbh = _gdn_bwd(sv["qh"], sv["kh"], sv["vh"], sv["gbh"], sv["bbh"], sv["sall"], sv["tall"], doh)
    gones = _group_ones()

    def gdn_pre_b(xq, xk, xv, pq, pk, pv, dq_, dk_, dv_, cw, go, has_prev):
        outs = []
        for j, (xc, xp, dd) in enumerate(((xq, pq, dq_), (xk, pk, dk_), (xv, pv, dv_))):
            cc = _conv_taps(xc, xp, cw[:, j * GDN_W:(j + 1) * GDN_W], has_prev)
            sg = _sigmoid(cc)
            a = cc * sg
            if j < 2:
                rn = lax.rsqrt(_dotf(a * a, go) + EPS)
                if j == 0:
                    dd = dd * (GDN_DH ** -0.5)
                da = rn * dd - a * (rn * rn * rn) * _dotf(dd * a, go)
            else:
                da = dd
            outs.append(da * (sg * (1.0 + cc * (1.0 - sg))))
        return tuple(outs)

    dcq, dck, dcv = _gdn_pre_b_call("gdn_pre_b", gdn_pre_b, proj, (dqh, dkh, dvh), w["conv_w"], gones, tb, steps)
    dproj, grads["conv_w"] = _conv_bwd_call("gdn_conv_b", proj, (dcq, dck, dcv), w["conv_w"], dproj, tb, steps)

    def gate_b(ab_ref, g_ref, b_ref, dgh_ref, dbh_ref, al_ref, dt_ref, carry_ref, dab_ref, dal_ref, ddt_ref):
        @pl.when(pl.program_id(0) == 0)
        def _():
            dal_ref[...] = jnp.zeros(dal_ref.shape, F32)
            ddt_ref[...] = jnp.zeros(ddt_ref.shape, F32)

        ab, g128, b128 = ab_ref[...], g_ref[...], b_ref[...]
        lane = lax.broadcasted_iota(jnp.int32, ab.shape, 1)
        dg_ = jnp.zeros(ab.shape, F32)
        db_ = jnp.zeros(ab.shape, F32)
        for h in range(N_HEADS):
            dg_ = dg_ + jnp.where(lane == h, jnp.broadcast_to(dgh_ref[h][:, 0:1], ab.shape), 0.0)
            db_ = db_ + jnp.where(lane == N_HEADS + h, jnp.broadcast_to(dbh_ref[h][:, 0:1], ab.shape), 0.0)
        dg_ = _dotf(_chunk_sum_matrix(tb, True), dg_)
        slope = -jnp.exp(al_ref[...]) * _sigmoid(ab + dt_ref[...])
        dab_ref[...] = (dg_ * slope + db_ * b128 * (1.0 - b128)).astype(dab_ref.dtype)
        dal_ref[...] += jnp.sum(dg_ * g128, axis=0, keepdims=True)
        ddt_ref[...] += jnp.sum(dg_ * slope, axis=0, keepdims=True)

    lanes_spec = pl.BlockSpec((tb, HEAD_LANES), lambda i: (i, 0))
    ab_spec = pl.BlockSpec((tb, HEAD_LANES), lambda i: (i, PIN_AB // HEAD_LANES))
    dproj, grads["a_log_pad"], grads["dt_bias_pad"] = pl.pallas_call(
        gate_b, name="gdn_gate_b", grid=(steps,),
        in_specs=[ab_spec, lanes_spec, lanes_spec, _heads_spec(tb), _heads_spec(tb), vec_spec(HEAD_LANES),
                  vec_spec(HEAD_LANES), ANY_SPEC],
        out_specs=[ab_spec, vec_spec(HEAD_LANES), vec_spec(HEAD_LANES)],
        out_shape=[jax.ShapeDtypeStruct((t, PIN_W), BF16), jax.ShapeDtypeStruct((1, HEAD_LANES), F32),
                   jax.ShapeDtypeStruct((1, HEAD_LANES), F32)],
        input_output_aliases={7: 0},
        compiler_params=_params(("arbitrary",)))(proj, sv["g128"], sv["b128"], dgh, dbh, w["a_log_pad"],
                                                 w["dt_bias_pad"], dproj)
    dhn = _mm("mix_in_bx", dproj, w["w_in_pad"], "nt", F32)
    grads["w_in_pad"] = _mm("mix_in_bw", sv["hn"], dproj, "tn", F32)

    def pre_b(xb, dnb, dyb, g):
        dx, dg = _rms_bwd(xb, _rms_stats(xb), g, dnb)
        return dyb + dx, dg

    dx, grads["mix_pre_g"] = _rowwise("mix_pre_b", pre_b, [x, dhn, dy], [w["mix_pre_g"]], [(d, F32)], [(1, d)], tb)
    return dx, grads


def _gdn_pre_b_call(name, fn, proj, dd, conv_w, gones, tb, steps):
    t = proj.shape[0]

    def body(xq, xk, xv, pq, pk, pv, d0, d1, d2, cw, go, oq, ok, ov):
        has_prev = jnp.where(pl.program_id(0) == 0, 0.0, 1.0)
        dd_rows = [jnp.concatenate([dr[h] for h in range(N_HEADS)], axis=1) for dr in (d0, d1, d2)]
        outs = fn(xq[...], xk[...], xv[...], pq[...], pk[...], pv[...], *dd_rows, cw[...], go[...], has_prev)
        for r, val in zip((oq, ok, ov), outs):
            r[...] = val

    return pl.pallas_call(
        body, name=name, grid=(steps,),
        in_specs=_qkv_specs(tb) + [_heads_spec(tb)] * 3 + [pl.BlockSpec(conv_w.shape, lambda i: (0, 0)),
                                                          pl.BlockSpec(gones.shape, lambda i: (0, 0))],
        out_specs=[pl.BlockSpec((tb, GDN_W), lambda i: (i, 0))] * 3,
        out_shape=[jax.ShapeDtypeStruct((t, GDN_W), F32)] * 3,
        compiler_params=_params(("arbitrary",)))(proj, proj, proj, proj, proj, proj, *dd, conv_w, gones)


def _conv_bwd_call(name, proj, dc, conv_w, dproj, tb, steps):
    t = proj.shape[0]
    dcur = [pl.BlockSpec((tb, GDN_W), lambda i: (i, 0))] * 3
    dnext = [pl.BlockSpec((tb, GDN_W), lambda i: (jnp.minimum(i + 1, steps - 1), 0))] * 3

    def body(xq, xk, xv, pq, pk, pv, d0, d1, d2, n0, n1, n2, cw, carry_ref, dx_ref, dw_ref):
        i = pl.program_id(0)
        has_prev = jnp.where(i == 0, 0.0, 1.0)
        has_next = jnp.where(i == steps - 1, 0.0, 1.0)

        @pl.when(i == 0)
        def _():
            dw_ref[...] = jnp.zeros(dw_ref.shape, F32)

        wv = cw[...]
        dws, dxs = [], []
        for j, (xr, pr, dr, nr) in enumerate(((xq, pq, d0, n0), (xk, pk, d1, n1), (xv, pv, d2, n2))):
            x, xp, dcv, dnx = xr[...], pr[...], dr[...], nr[...]
            wj = wv[:, j * GDN_W:(j + 1) * GDN_W]
            row = lax.broadcasted_iota(jnp.int32, x.shape, 0)
            dx = dcv * wj[GDN_CONV - 1:GDN_CONV, :]
            rows_w = [jnp.sum(dcv * x, axis=0, keepdims=True)]
            for s in range(1, GDN_CONV):
                up = jnp.where(row < tb - s, _roll(dcv, tb - s, 0), _roll(dnx, tb - s, 0) * has_next)
                dx = dx + up * wj[GDN_CONV - 1 - s:GDN_CONV - s, :]
                sh = jnp.where(row >= s, _roll(x, s, 0), _roll(xp, s, 0) * has_prev)
                rows_w.append(jnp.sum(dcv * sh, axis=0, keepdims=True))
            dxs.append(dx)
            dws.append(jnp.concatenate(rows_w[::-1], axis=0))
        dx_ref[...] = jnp.concatenate(dxs, axis=1).astype(dx_ref.dtype)
        dw_ref[...] += jnp.concatenate(dws, axis=1)

    return pl.pallas_call(
        body, name=name, grid=(steps,),
        in_specs=_qkv_specs(tb) + dcur + dnext + [pl.BlockSpec(conv_w.shape, lambda i: (0, 0)), ANY_SPEC],
        out_specs=[pl.BlockSpec((tb, 3 * GDN_W), lambda i: (i, PIN_QKV // (3 * GDN_W))),
                   pl.BlockSpec(conv_w.shape, lambda i: (0, 0))],
        out_shape=[jax.ShapeDtypeStruct((t, PIN_W), BF16), jax.ShapeDtypeStruct(conv_w.shape, F32)],
        input_output_aliases={13: 0},
        compiler_params=_params(("arbitrary",)))(proj, proj, proj, proj, proj, proj, *dc, *dc, conv_w, dproj)


def _pad_heads_cols(wm, per_head):
    r = wm.shape[0]
    return jnp.pad(wm.reshape(r, N_HEADS, per_head), ((0, 0), (0, 0), (0, HEAD_LANES - per_head))).reshape(r, MLA_PAD)


def _unpad_heads_cols(wm, per_head):
    r = wm.shape[0]
    return wm.reshape(r, N_HEADS, HEAD_LANES)[:, :, :per_head].reshape(r, N_HEADS * per_head)


def _win_to_pad(wi):
    r = wi.shape[0]
    z = lambda n: jnp.zeros((r, n), wi.dtype)
    o = MLA_Q_RANK + MLA_KV_RANK
    kpe = wi[:, o:o + MLA_ROPE]
    o2 = o + MLA_ROPE
    qkv = wi[:, o2:o2 + 3 * GDN_W]
    o3 = o2 + 3 * GDN_W
    ab = wi[:, o3:o3 + 2 * N_HEADS]
    gate = wi[:, o3 + 2 * N_HEADS:]
    return jnp.concatenate([qkv, wi[:, :o], z(MLA_NOPE), kpe, z(HEAD_LANES - MLA_NOPE - MLA_ROPE), gate, ab,
                            z(HEAD_LANES - 2 * N_HEADS)], axis=1)


def _win_from_pad(wp):
    return jnp.concatenate([wp[:, PIN_MLA:PIN_KPE], wp[:, PIN_KPE + MLA_NOPE:PIN_KPE + MLA_NOPE + MLA_ROPE],
                            wp[:, PIN_QKV:PIN_QKV + 3 * GDN_W], wp[:, PIN_AB:PIN_AB + 2 * N_HEADS],
                            wp[:, PIN_GATE:PIN_GATE + GDN_W]], axis=1)


def _wkv_to_pad(wkv):
    r = wkv.shape[0]
    w3 = wkv.reshape(r, N_HEADS, MLA_NOPE + MLA_V)
    kpart = jnp.pad(w3[:, :, :MLA_NOPE], ((0, 0), (0, 0), (0, HEAD_LANES - MLA_NOPE))).reshape(r, MLA_PAD)
    vpart = jnp.pad(w3[:, :, MLA_NOPE:], ((0, 0), (0, 0), (0, HEAD_LANES - MLA_V))).reshape(r, MLA_PAD)
    return jnp.concatenate([kpart, vpart], axis=1)


def _wkv_from_pad(wp):
    r = wp.shape[0]
    kpart = wp[:, :MLA_PAD].reshape(r, N_HEADS, HEAD_LANES)[:, :, :MLA_NOPE]
    vpart = wp[:, MLA_PAD:].reshape(r, N_HEADS, HEAD_LANES)[:, :, :MLA_V]
    return jnp.concatenate([kpart, vpart], axis=2).reshape(r, N_HEADS * (MLA_NOPE + MLA_V))


def _wout_to_pad(wo):
    n = wo.shape[1]
    mla = jnp.pad(wo[:N_HEADS * MLA_V].reshape(N_HEADS, MLA_V, n), ((0, 0), (0, HEAD_LANES - MLA_V), (0, 0)))
    return jnp.concatenate([mla.reshape(MLA_PAD, n), wo[N_HEADS * MLA_V:]], axis=0)


def _wout_from_pad(wp):
    n = wp.shape[1]
    mla = wp[:MLA_PAD].reshape(N_HEADS, HEAD_LANES, n)[:, :MLA_V].reshape(N_HEADS * MLA_V, n)
    return jnp.concatenate([mla, wp[MLA_PAD:]], axis=0)


def _pad_lanes(v, n):
    return jnp.pad(v, ((0, 0), (0, n - v.shape[1])))


def _compute_weights(full):
    w = {}
    for n in FFN_BIG:
        w[n] = full[n].astype(MM_DTYPE)
    w["w_in_pad"] = _win_to_pad(full["w_in"]).astype(MM_DTYPE)
    w["w_uq_pad"] = _pad_heads_cols(full["mla_w_uq"], MLA_NOPE + MLA_ROPE).astype(MM_DTYPE)
    w["w_kv_pad"] = _wkv_to_pad(full["mla_w_ukv"]).astype(MM_DTYPE)
    w["w_out_pad"] = _wout_to_pad(full["w_out"]).astype(MM_DTYPE)
    w["conv_w"] = full["gdn_conv_w"].astype(F32)
    for n in ("ffn1_pre_g", "ffn1_post_g", "mix_pre_g", "mla_q_norm_g", "mla_kv_norm_g", "gdn_norm_g", "mix_post_g",
              "ffn2_pre_g", "ffn2_post_g"):
        w[n] = full[n]
    w["mla_out_g_pad"] = _pad_heads_cols(full["mla_out_g"], MLA_V)
    w["a_log_pad"] = _pad_lanes(full["gdn_a_log"], HEAD_LANES)
    w["dt_bias_pad"] = _pad_lanes(full["gdn_dt_bias"], HEAD_LANES)
    return w


def _local_step(x, positions, loss_target, full):
    t, d = x.shape
    tb = min(512, t)
    tm = min(1024, t)
    w = _compute_weights(full)
    ffn = lambda tag: (w[tag + "_pre_g"], w[tag + "_w_gate"], w[tag + "_w_up"], w[tag + "_w_down"], w[tag + "_post_g"])
    x1, sv1 = _ffn_fwd("ffn1", x, *ffn("ffn1"), tm)
    x2, svm = _mixer_fwd(x1, positions, w, tb)
    x3, sv2 = _ffn_fwd("ffn2", x2, *ffn("ffn2"), tm)

    def loss_f(yb, tg):
        e = yb - tg
        return e * (1.0 / d), jnp.sum(e * e, axis=0, keepdims=True)

    dy, lsum = _rowwise("loss", loss_f, [x3, loss_target], [], [(d, F32)], [(1, d)], tb)
    g = {}
    dx2, g["ffn2_pre_g"], g["ffn2_w_gate"], g["ffn2_w_up"], g["ffn2_w_down"], g["ffn2_post_g"] = _ffn_bwd(
        "ffn2", dy, sv2, *ffn("ffn2"), tm, tm)
    dx1, gm = _mixer_bwd(dx2, svm, w, tb)
    dx0, g["ffn1_pre_g"], g["ffn1_w_gate"], g["ffn1_w_up"], g["ffn1_w_down"], g["ffn1_post_g"] = _ffn_bwd(
        "ffn1", dx1, sv1, *ffn("ffn1"), tm, tm)
    g["mix_pre_g"], g["mix_post_g"] = gm["mix_pre_g"], gm["mix_post_g"]
    g["mla_q_norm_g"], g["mla_kv_norm_g"] = gm["mla_q_norm_g"], gm["mla_kv_norm_g"]
    g["gdn_norm_g"] = gm["gdn_norm_g"]
    g["w_in"] = _win_from_pad(gm["w_in_pad"])
    g["mla_w_uq"] = _unpad_heads_cols(gm["w_uq_pad"], MLA_NOPE + MLA_ROPE)
    g["mla_w_ukv"] = _wkv_from_pad(gm["w_kv_pad"])
    g["mla_out_g"] = _unpad_heads_cols(gm["mla_out_g_pad"], MLA_V)
    g["gdn_conv_w"] = gm["conv_w"]
    g["gdn_a_log"] = gm["a_log_pad"][:, :N_HEADS]
    g["gdn_dt_bias"] = gm["dt_bias_pad"][:, :N_HEADS]
    g["w_out"] = _wout_from_pad(gm["w_out_pad"])
    return lsum, dx0, g


HBM_SPEC = pl.BlockSpec(memory_space=pltpu.HBM)


def _place():
    return lax.axis_index("x"), lax.axis_index("y"), lax.axis_index("c")


def _exchange_call(name, body, ins, out_shapes, n_remote, n_local):
    return pl.pallas_call(
        body, name=name, in_specs=[HBM_SPEC] * len(ins), out_specs=[HBM_SPEC] * len(out_shapes), out_shape=out_shapes,
        scratch_shapes=[pltpu.SemaphoreType.DMA((n_remote,)), pltpu.SemaphoreType.DMA((n_remote,)),
                        pltpu.SemaphoreType.DMA((n_local,))])(*ins)


def _other_chips(x, y):
    return [(1 - x, y), (x, 1 - y), (1 - x, 1 - y)]


def _at_each_chip(fn):
    x, y, _ = _place()
    for cx in range(2):
        for cy in range(2):
            pl.when((x == cx) & (y == cy))(functools.partial(fn, cx, cy))


def _at_each_device(fn):
    x, y, c = _place()
    for cx in range(2):
        for cy in range(2):
            for cc in range(2):
                pl.when((x == cx) & (y == cy) & (c == cc))(functools.partial(fn, cx, cy, cc))


def _at_each_core(fn):
    c = lax.axis_index("c")
    for cc in range(2):
        pl.when(c == cc)(functools.partial(fn, cc))


def _gather_shards(ws):
    nw = len(ws)

    def body(*refs):
        w_refs, out_refs = refs[:nw], refs[nw:2 * nw]
        send_sems, recv_sems, local_sems = refs[2 * nw:]

        def run(x, y, c):
            chips = _other_chips(x, y)
            me, sibling = 2 * x + y, (x, y, 1 - c)

            def half(ref, which):
                hr = ref.shape[0] // 2
                return ref.at[pl.ds(which * hr, hr)]

            def over_ici(i, j, src, slab, to):
                return pltpu.make_async_remote_copy(
                    src_ref=half(src, c), dst_ref=half(out_refs[i].at[slab], c), send_sem=send_sems.at[7 * i + j],
                    recv_sem=recv_sems.at[7 * i + j], device_id=to, device_id_type=MESH)

            def over_d2d(i, j, slab, which):
                return pltpu.make_async_remote_copy(
                    src_ref=half(out_refs[i].at[slab], which), dst_ref=half(out_refs[i].at[slab], which),
                    send_sem=send_sems.at[7 * i + 3 + j], recv_sem=recv_sems.at[7 * i + 3 + j], device_id=sibling,
                    device_id_type=MESH)

            def own(i, w_ref):
                return pltpu.make_async_remote_copy(
                    src_ref=w_ref, dst_ref=out_refs[i].at[me], send_sem=send_sems.at[7 * i + 6],
                    recv_sem=recv_sems.at[7 * i + 6], device_id=sibling, device_id_type=MESH)

            sends, passed = [], []
            for i, w_ref in enumerate(w_refs):
                for j, (px, py) in enumerate(chips):
                    sends.append(over_ici(i, j, w_ref, me, (px, py, c)))
                    sends[-1].start()
            for i, w_ref in enumerate(w_refs):
                sends.append(own(i, w_ref))
                sends[-1].start()
            for i, w_ref in enumerate(w_refs):
                for j, (px, py) in enumerate(chips):
                    over_ici(i, j, w_ref, 2 * px + py, (px, py, c)).wait_recv()
                    passed.append(over_d2d(i, j, 2 * px + py, c))
                    passed[-1].start()
            for i, w_ref in enumerate(w_refs):
                own(i, w_ref).wait_recv()
                for j, (px, py) in enumerate(chips):
                    over_d2d(i, j, 2 * px + py, 1 - c).wait_recv()
            for cp in sends + passed:
                cp.wait_send()

        _at_each_device(run)

    outs = [jax.ShapeDtypeStruct((N_SHARD,) + w.shape, w.dtype) for w in ws]
    return _exchange_call("gather_weight_shards", body, ws, outs, 7 * nw, 1)


def _swap_halves(gs):
    ng = len(gs)

    def body(*refs):
        g_refs, got_refs = refs[:ng], refs[ng:2 * ng]
        send_sems, recv_sems, _ = refs[2 * ng:]
        x, y, _ = _place()

        def run(c):
            sends = []
            for i, (g_ref, got_ref) in enumerate(zip(g_refs, got_refs)):
                hr = got_ref.shape[1]
                sends.append(pltpu.make_async_remote_copy(
                    src_ref=g_ref.at[:, pl.ds((1 - c) * hr, hr)], dst_ref=got_ref, send_sem=send_sems.at[i],
                    recv_sem=recv_sems.at[i], device_id=(x, y, 1 - c), device_id_type=MESH))
                sends[-1].start()
            for cp in sends:
                cp.wait()

        _at_each_core(run)

    halves = [jax.ShapeDtypeStruct((g.shape[0], g.shape[1] // 2, g.shape[2]), g.dtype) for g in gs]
    return _exchange_call("swap_grad_halves", body, gs, halves, ng, 1)


def _scatter_to_chips(ps):
    n = len(ps)

    def body(*refs):
        p_refs, out_refs = refs[:n], refs[n:2 * n]
        send_sems, recv_sems, _ = refs[2 * n:]
        c = lax.axis_index("c")

        def run(x, y):
            me = 2 * x + y
            chips = _other_chips(x, y)
            sends = []
            for i, (p_ref, out_ref) in enumerate(zip(p_refs, out_refs)):
                for j, (px, py) in enumerate(chips):
                    sends.append(pltpu.make_async_remote_copy(
                        src_ref=p_ref.at[2 * px + py], dst_ref=out_ref.at[j], send_sem=send_sems.at[3 * i + j],
                        recv_sem=recv_sems.at[3 * i + j], device_id=(px, py, c), device_id_type=MESH))
                    sends[-1].start()
            for i, (p_ref, out_ref) in enumerate(zip(p_refs, out_refs)):
                for j, (px, py) in enumerate(chips):
                    pltpu.make_async_remote_copy(
                        src_ref=p_ref.at[me], dst_ref=out_ref.at[j], send_sem=send_sems.at[3 * i + j],
                        recv_sem=recv_sems.at[3 * i + j], device_id=(px, py, c), device_id_type=MESH).wait_recv()
            for cp in sends:
                cp.wait_send()

        _at_each_chip(run)

    outs = [jax.ShapeDtypeStruct((3,) + p.shape[1:], p.dtype) for p in ps]
    return _exchange_call("scatter_grad_quarters", body, ps, outs, 3 * n, 1)


def _share_halves(hs):
    n = len(hs)

    def body(*refs):
        h_refs, out_refs = refs[:n], refs[n:2 * n]
        send_sems, recv_sems, _ = refs[2 * n:]
        x, y, c = _place()
        sends = []
        for i, (h_ref, out_ref) in enumerate(zip(h_refs, out_refs)):
            sends.append(pltpu.make_async_remote_copy(
                src_ref=h_ref, dst_ref=out_ref, send_sem=send_sems.at[i], recv_sem=recv_sems.at[i],
                device_id=(x, y, 1 - c), device_id_type=MESH))
            sends[-1].start()
        for cp in sends:
            cp.wait()

    outs = [jax.ShapeDtypeStruct(h.shape, h.dtype) for h in hs]
    return _exchange_call("share_grad_halves", body, hs, outs, n, 1)


def _scalar_grid_call(name, body, scalars, grid, in_specs, out_specs, out_shape, args):
    grid_spec = pltpu.PrefetchScalarGridSpec(num_scalar_prefetch=len(scalars), grid=grid, in_specs=in_specs,
                                             out_specs=out_specs)
    return pl.pallas_call(body, name=name, grid_spec=grid_spec, out_shape=out_shape,
                          compiler_params=_params(("arbitrary",) * len(grid)))(*scalars, *args)


def _add_pair(name, g, got, core):
    ns_, hr, cols = got.shape
    th = _row_tile(hr, 512)
    nb = hr // th

    def body(core_ref, g_ref, got_ref, out_ref):
        out_ref[...] = (g_ref[...].astype(F32) + got_ref[...].astype(F32)).astype(out_ref.dtype)

    blk = pl.BlockSpec((1, th, cols), lambda q, j, core_ref: (q, j, 0))
    own = pl.BlockSpec((1, th, cols), lambda q, j, core_ref: (q, core_ref[0] * nb + j, 0))
    return _scalar_grid_call(name, body, [core], (ns_, nb), [own, blk], blk,
                             jax.ShapeDtypeStruct(got.shape, got.dtype), [g, got])


def _add_chips(name, pairs, slabs, chip):
    _, hr, cols = slabs.shape
    th = _row_tile(hr, 512)

    def body(chip_ref, own_ref, s0_ref, s1_ref, s2_ref, out_ref):
        total = own_ref[0].astype(F32) + s0_ref[0].astype(F32)
        out_ref[...] = (total + s1_ref[0].astype(F32)) + s2_ref[0].astype(F32)

    own = pl.BlockSpec((1, th, cols), lambda j, chip_ref: (chip_ref[0], j, 0))
    others = [pl.BlockSpec((1, th, cols), lambda j, chip_ref, k=k: (k, j, 0)) for k in range(3)]
    return _scalar_grid_call(name, body, [chip], (hr // th,), [own] + others,
                             pl.BlockSpec((th, cols), lambda j, chip_ref: (j, 0)),
                             jax.ShapeDtypeStruct((hr, cols), F32), [pairs, slabs, slabs, slabs])


def _join_halves(name, mine, other, core):
    hr, cols = mine.shape
    th = _row_tile(hr, 512)
    nb = hr // th

    def body(core_ref, mine_ref, other_ref, out_ref):
        is_mine = pl.program_id(0) == core_ref[0]

        @pl.when(is_mine)
        def _():
            out_ref[...] = mine_ref[...]

        @pl.when(jnp.logical_not(is_mine))
        def _():
            out_ref[...] = other_ref[...]

    blk = pl.BlockSpec((th, cols), lambda h, j, core_ref: (j, 0))
    return _scalar_grid_call(name, body, [core], (2, nb), [blk, blk],
                             pl.BlockSpec((th, cols), lambda h, j, core_ref: (h * nb + j, 0)),
                             jax.ShapeDtypeStruct((2 * hr, cols), mine.dtype), [mine, other])


def _gather_small(sp):
    def body(s_ref, out_ref, send_sems, recv_sems, local_sem):
        x, y, c = _place()
        me = 4 * x + 2 * y + c
        peers = [(x ^ (m >> 2), y ^ ((m >> 1) & 1), c ^ (m & 1)) for m in range(1, 8)]
        mine = pltpu.make_async_copy(s_ref, out_ref.at[me], local_sem)
        mine.start()
        sends = [pltpu.make_async_remote_copy(src_ref=s_ref, dst_ref=out_ref.at[me], send_sem=send_sems.at[j],
                                              recv_sem=recv_sems.at[j], device_id=p, device_id_type=MESH)
                 for j, p in enumerate(peers)]
        for cp in sends:
            cp.start()
        for j, (px, py, pc) in enumerate(peers):
            pltpu.make_async_remote_copy(src_ref=s_ref, dst_ref=out_ref.at[4 * px + 2 * py + pc],
                                         send_sem=send_sems.at[j], recv_sem=recv_sems.at[j], device_id=(px, py, pc),
                                         device_id_type=MESH).wait_recv()
        for cp in sends:
            cp.wait_send()
        mine.wait()

    return pl.pallas_call(
        body, name="gather_small_grads", in_specs=[HBM_SPEC], out_specs=HBM_SPEC,
        out_shape=jax.ShapeDtypeStruct((8,) + sp.shape, sp.dtype),
        scratch_shapes=[pltpu.SemaphoreType.DMA((7,)), pltpu.SemaphoreType.DMA((7,)), pltpu.SemaphoreType.DMA])(sp)


def _pack_rows(total):
    rows = -(-total // LANES)
    return -(-rows // 32) * 32


def _pack(arrs, dtype):
    flat = jnp.concatenate([a.reshape(-1).astype(dtype) for a in arrs])
    rows = _pack_rows(flat.shape[0])
    return jnp.pad(flat, (0, rows * LANES - flat.shape[0])).reshape(rows, LANES)


def _unpack(buf, shapes):
    flat = buf.reshape(-1)
    out, off = {}, 0
    for n, shp in shapes:
        size = shp[0] * shp[1]
        out[n] = flat[off:off + size].reshape(shp)
        off += size
    return out


def _adamw(name, wv, g, m, v, tb):
    c1 = 1.0 - ADAM_B1 ** ADAM_STEP
    c2 = 1.0 - ADAM_B2 ** ADAM_STEP

    def fn(wb, gb, mb, vb):
        m2 = ADAM_B1 * mb + (1.0 - ADAM_B1) * gb
        v2 = ADAM_B2 * vb + (1.0 - ADAM_B2) * (gb * gb)
        delta = -ADAM_LR * ((m2 / c1) / (jnp.sqrt(v2 / c2) + ADAM_EPS) + ADAM_WD * wb)
        return delta, m2, v2

    cols = wv.shape[1]
    return _rowwise(name, fn, [wv, g, m, v], [], [(cols, F32)] * 3, [], tb)


def _row_tile(rows, pref):
    if rows <= pref:
        return rows
    t = pref
    while t >= 8:
        if rows % t == 0 and t % 8 == 0:
            return t
        t -= 8
    return rows


def kernel(x, positions, ffn1_pre_g, ffn1_w_gate, ffn1_w_up, ffn1_w_down, ffn1_post_g, mix_pre_g, w_in, mla_q_norm_g, mla_w_uq, mla_kv_norm_g, mla_w_ukv, mla_out_g, gdn_conv_w, gdn_a_log, gdn_dt_bias, gdn_norm_g, w_out, mix_post_g, ffn2_pre_g, ffn2_w_gate, ffn2_w_up, ffn2_w_down, ffn2_post_g, loss_target, m_ffn1_pre_g, m_ffn1_w_gate, m_ffn1_w_up, m_ffn1_w_down, m_ffn1_post_g, m_mix_pre_g, m_w_in, m_mla_q_norm_g, m_mla_w_uq, m_mla_kv_norm_g, m_mla_w_ukv, m_mla_out_g, m_gdn_conv_w, m_gdn_a_log, m_gdn_dt_bias, m_gdn_norm_g, m_w_out, m_mix_post_g, m_ffn2_pre_g, m_ffn2_w_gate, m_ffn2_w_up, m_ffn2_w_down, m_ffn2_post_g, v_ffn1_pre_g, v_ffn1_w_gate, v_ffn1_w_up, v_ffn1_w_down, v_ffn1_post_g, v_mix_pre_g, v_w_in, v_mla_q_norm_g, v_mla_w_uq, v_mla_kv_norm_g, v_mla_w_ukv, v_mla_out_g, v_gdn_conv_w, v_gdn_a_log, v_gdn_dt_bias, v_gdn_norm_g, v_w_out, v_mix_post_g, v_ffn2_pre_g, v_ffn2_w_gate, v_ffn2_w_up, v_ffn2_w_down, v_ffn2_post_g):
    args = dict(locals())
    wsh = {n: args[n][0] for n in WEIGHTS}
    msh = {n: args["m_" + n][0] if args["m_" + n].ndim == 3 else args["m_" + n] for n in WEIGHTS}
    vsh = {n: args["v_" + n][0] if args["v_" + n].ndim == 3 else args["v_" + n] for n in WEIGHTS}
    for n in SMALL:
        wsh[n] = args[n]
    mix_shapes = [(n, wsh[n].shape) for n in MIX_BIG]

    gathered = _gather_shards([wsh[n].astype(MM_DTYPE) for n in FFN_BIG] + [_pack([wsh[n] for n in MIX_BIG], MM_DTYPE)])
    full = {n: wsh[n] for n in SMALL}
    for n, gw in zip(FFN_BIG, gathered):
        full[n] = gw
    parts = [_unpack(gathered[-1][q], mix_shapes) for q in range(N_SHARD)]
    for n in MIX_BIG:
        full[n] = jnp.concatenate([parts[q][n] for q in range(N_SHARD)], axis=SHARD_AXIS[n])

    lsum, grad_x, g = _local_step(x[0], positions, loss_target[0], full)
    loss = lax.psum(0.5 * jnp.sum(lsum) / x.shape[-1], ("x", "y", "c"))

    mix_quarters = [_pack([jnp.split(g[n], N_SHARD, axis=SHARD_AXIS[n])[q] for n in MIX_BIG], MM_DTYPE)
                    for q in range(N_SHARD)]
    core = lax.axis_index("c").astype(jnp.int32).reshape(1)
    chip = (2 * lax.axis_index("x") + lax.axis_index("y")).astype(jnp.int32).reshape(1)
    mine = [g[n] for n in FFN_BIG] + [jnp.stack(mix_quarters)]
    got = _swap_halves(mine)
    pairs = [_add_pair("add_pair_%d" % i, gi, gt, core) for i, (gi, gt) in enumerate(zip(mine, got))]
    slabs = _scatter_to_chips(pairs)
    halves = [_add_chips("add_chips_%d" % i, pr, sl, chip) for i, (pr, sl) in enumerate(zip(pairs, slabs))]
    others = _share_halves(halves)
    shared = [_join_halves("join_halves_%d" % i, hm, ho, core) for i, (hm, ho) in enumerate(zip(halves, others))]
    gsh = _unpack(shared[-1], mix_shapes)
    for n, sg_ in zip(FFN_BIG, shared):
        gsh[n] = sg_

    small_shapes = [(n, wsh[n].shape) for n in SMALL]
    pack_small = lambda d: jnp.concatenate(
        [_pad_lanes(d[n].astype(F32), LANES) for n in SMALL] + [jnp.zeros((SMALL_ROWS - len(SMALL), LANES), F32)], axis=0)
    slots = _gather_small(pack_small(g))

    c1 = 1.0 - ADAM_B1 ** ADAM_STEP
    c2 = 1.0 - ADAM_B2 ** ADAM_STEP

    def small_update(wb, mb, vb, s8):
        gs = s8[0:SMALL_ROWS]
        for d in range(1, 8):
            gs = gs + s8[d * SMALL_ROWS:(d + 1) * SMALL_ROWS]
        m2 = ADAM_B1 * mb + (1.0 - ADAM_B1) * gs
        v2 = ADAM_B2 * vb + (1.0 - ADAM_B2) * (gs * gs)
        delta = -ADAM_LR * ((m2 / c1) / (jnp.sqrt(v2 / c2) + ADAM_EPS) + ADAM_WD * wb)
        return gs, delta, m2, v2

    sg, sd, sm, sv_ = _rowwise("adamw_small", small_update,
                               [pack_small(wsh), pack_small(msh), pack_small(vsh)],
                               [slots.reshape(8 * SMALL_ROWS, LANES)], [(LANES, F32)] * 4, [], SMALL_ROWS)
    grads, deltas, new_m, new_v = {}, {}, {}, {}
    for i, (n, shp) in enumerate(small_shapes):
        grads[n], deltas[n] = sg[i:i + 1, :shp[1]], sd[i:i + 1, :shp[1]]
        new_m[n], new_v[n] = sm[i:i + 1, :shp[1]], sv_[i:i + 1, :shp[1]]
    for n in BIG:
        grads[n] = gsh[n]
        r = wsh[n].shape[0]
        deltas[n], new_m[n], new_v[n] = _adamw("adamw_" + n, wsh[n], gsh[n], msh[n], vsh[n], _row_tile(r, 256))

    def shaped(d, n):
        return d[n][None] if n in BIG else d[n]

    return (loss, grad_x[None], *[shaped(grads, n) for n in WEIGHTS], *[shaped(deltas, n) for n in WEIGHTS],
            *[shaped(new_m, n) for n in WEIGHTS], *[shaped(new_v, n) for n in WEIGHTS])
```

```python
import functools

import jax
import jax.numpy as jnp
from jax import lax
from jax.experimental import pallas as pl
from jax.experimental.pallas import tpu as pltpu

F32 = jnp.float32
BF16 = jnp.bfloat16
MM_DTYPE = BF16
HI = lax.Precision.HIGHEST
MESH = pl.DeviceIdType.MESH

D_MODEL = 1024
D_FF = 2816
N_HEADS = 8
MLA_Q_RANK = 256
MLA_KV_RANK = 128
MLA_NOPE = 64
MLA_ROPE = 32
MLA_V = 64
ROPE_THETA = 10000.0
GDN_DH = 64
GDN_W = N_HEADS * GDN_DH
GDN_CONV = 4
CHUNK = 64
HEAD_LANES = 128
MLA_PAD = N_HEADS * HEAD_LANES
EPS = 1e-6
N_SHARD = 4
LANES = 1024

PIN_QKV = 0
PIN_MLA = 1536
PIN_KPE = 1920
PIN_GATE = 2048
PIN_AB = 2560
PIN_W = 2688
CAT_W = MLA_PAD + GDN_W

ADAM_LR = 0.001
ADAM_B1 = 0.9
ADAM_B2 = 0.999
ADAM_EPS = 1e-08
ADAM_WD = 0.01
ADAM_STEP = 10

VMEM_LIMIT_V7X = 56 * 1024 * 1024

BIG = ["ffn1_w_gate", "ffn1_w_up", "ffn1_w_down", "w_in", "mla_w_uq", "mla_w_ukv", "gdn_conv_w", "w_out",
       "ffn2_w_gate", "ffn2_w_up", "ffn2_w_down"]
FFN_BIG = ["ffn1_w_gate", "ffn1_w_up", "ffn1_w_down", "ffn2_w_gate", "ffn2_w_up", "ffn2_w_down"]
MIX_BIG = ["w_in", "mla_w_uq", "mla_w_ukv", "gdn_conv_w", "w_out"]
SMALL = ["ffn1_pre_g", "ffn1_post_g", "mix_pre_g", "mla_q_norm_g", "mla_kv_norm_g", "mla_out_g", "gdn_a_log",
         "gdn_dt_bias", "gdn_norm_g", "mix_post_g", "ffn2_pre_g", "ffn2_post_g"]
WEIGHTS = ["ffn1_pre_g", "ffn1_w_gate", "ffn1_w_up", "ffn1_w_down", "ffn1_post_g", "mix_pre_g", "w_in",
           "mla_q_norm_g", "mla_w_uq", "mla_kv_norm_g", "mla_w_ukv", "mla_out_g", "gdn_conv_w", "gdn_a_log",
           "gdn_dt_bias", "gdn_norm_g", "w_out", "mix_post_g", "ffn2_pre_g", "ffn2_w_gate", "ffn2_w_up",
           "ffn2_w_down", "ffn2_post_g"]
SHARD_AXIS = {"ffn1_w_gate": 1, "ffn1_w_up": 1, "ffn1_w_down": 0, "w_in": 1, "mla_w_uq": 1, "mla_w_ukv": 1,
              "gdn_conv_w": 1, "w_out": 0, "ffn2_w_gate": 1, "ffn2_w_up": 1, "ffn2_w_down": 0}
SMALL_ROWS = 16


def _params(sem):
    return pltpu.CompilerParams(dimension_semantics=sem, vmem_limit_bytes=VMEM_LIMIT_V7X)


def _pick(dim, pref):
    if dim <= pref:
        return dim
    t = (pref // 128) * 128
    while t >= 128:
        if dim % t == 0:
            return t
        t -= 128
    return dim


ANY_SPEC = pl.BlockSpec(memory_space=pl.ANY)


def _rowwise(name, fn, row_ins, bc_ins, row_outs, acc_outs, tb, wide=None, carry=None):
    ents = []
    for e in row_ins:
        ents.append(e if isinstance(e, tuple) else (e, e.shape[1], 0, 0))
    over = [o[2] for o in row_outs if len(o) == 3]
    rows = over[0] if over else ents[0][0].shape[0]
    steps = rows // tb
    assert steps * tb == rows, (name, rows, tb)
    in_specs, args = [], []
    for a, w, j, r0 in ents:
        in_specs.append(pl.BlockSpec((tb, w), lambda i, j=j, r0=r0: (i + r0, j)))
        args.append(a)
    for b in bc_ins:
        in_specs.append(pl.BlockSpec(b.shape, lambda i: (0, 0)))
        args.append(b)
    n_in = len(args)
    aliases = {}
    if carry is not None:
        in_specs.append(ANY_SPEC)
        args.append(carry)
        aliases = {n_in: 0}
    out_shape = [jax.ShapeDtypeStruct((rows, o[0]), o[1]) for o in row_outs]
    out_specs = [pl.BlockSpec((tb, o[0]), lambda i: (i, 0)) for o in row_outs]
    if wide is not None:
        out_shape[0] = jax.ShapeDtypeStruct((rows, wide[0]), row_outs[0][1])
        out_specs[0] = pl.BlockSpec((tb, row_outs[0][0]), lambda i: (i, wide[1]))
    out_shape += [jax.ShapeDtypeStruct((r, c), F32) for r, c in acc_outs]
    out_specs += [pl.BlockSpec((r, c), lambda i: (0, 0)) for r, c in acc_outs]
    n_ro, n_acc, n_args = len(row_outs), len(acc_outs), len(args)

    def body(*refs):
        vals = fn(*[r[...] for r in refs[:n_in]])
        if not isinstance(vals, (tuple, list)):
            vals = (vals,)
        for r, v in zip(refs[n_args:n_args + n_ro], vals[:n_ro]):
            r[...] = v.astype(r.dtype)
        if n_acc:
            acc_refs = refs[n_args + n_ro:]

            @pl.when(pl.program_id(0) == 0)
            def _():
                for r in acc_refs:
                    r[...] = jnp.zeros(r.shape, r.dtype)

            for r, v in zip(acc_refs, vals[n_ro:]):
                r[...] += v

    outs = pl.pallas_call(body, name=name, grid=(steps,), in_specs=in_specs, out_specs=out_specs,
                          out_shape=out_shape, input_output_aliases=aliases,
                          compiler_params=_params(("arbitrary",)))(*args)
    return outs


def _mm(name, a, b, mode, out_dtype, tm=512, tn=512, tk=1024):
    if mode == "nn":
        (m, k), (k2, n) = a.shape, b.shape
    elif mode == "nt":
        (m, k), (n, k2) = a.shape, b.shape
    else:
        (k, m), (k2, n) = a.shape, b.shape
    assert k == k2, (name, a.shape, b.shape)
    tm, tn, tk = _pick(m, tm), _pick(n, tn), _pick(k, tk)
    nk = k // tk
    if mode == "nn":
        a_spec = pl.BlockSpec((tm, tk), lambda i, j, kk: (i, kk))
        b_spec = pl.BlockSpec((tk, tn), lambda i, j, kk: (kk, j))
        dims = (((1,), (0,)), ((), ()))
    elif mode == "nt":
        a_spec = pl.BlockSpec((tm, tk), lambda i, j, kk: (i, kk))
        b_spec = pl.BlockSpec((tn, tk), lambda i, j, kk: (j, kk))
        dims = (((1,), (1,)), ((), ()))
    else:
        a_spec = pl.BlockSpec((tk, tm), lambda i, j, kk: (kk, i))
        b_spec = pl.BlockSpec((tk, tn), lambda i, j, kk: (kk, j))
        dims = (((0,), (0,)), ((), ()))

    def body(a_ref, b_ref, o_ref, acc_ref):
        kk = pl.program_id(2)

        @pl.when(kk == 0)
        def _():
            acc_ref[...] = jnp.zeros(acc_ref.shape, F32)

        acc_ref[...] += lax.dot_general(a_ref[...].astype(MM_DTYPE), b_ref[...].astype(MM_DTYPE), dims,
                                        preferred_element_type=F32)

        @pl.when(kk == nk - 1)
        def _():
            o_ref[...] = acc_ref[...].astype(o_ref.dtype)

    return pl.pallas_call(
        body, name=name, grid=(m // tm, n // tn, nk), in_specs=[a_spec, b_spec],
        out_specs=pl.BlockSpec((tm, tn), lambda i, j, kk: (i, j)),
        out_shape=jax.ShapeDtypeStruct((m, n), out_dtype),
        scratch_shapes=[pltpu.VMEM((tm, tn), F32)],
        compiler_params=_params(("parallel", "parallel", "arbitrary")))(a, b)


def _rms_stats(x, n_real=None):
    n = x.shape[-1] if n_real is None else n_real
    return lax.rsqrt(jnp.sum(x * x, axis=-1, keepdims=True) / n + EPS)


def _rms_bwd(x, r, g, dz, n_real=None):
    n = x.shape[-1] if n_real is None else n_real
    xh = x * r
    dxh = dz * g
    dx = r * (dxh - xh * (jnp.sum(dxh * xh, axis=-1, keepdims=True) / n))
    return dx, jnp.sum(dz * xh, axis=0, keepdims=True)


def _sigmoid(x):
    return 1.0 / (1.0 + jnp.exp(-x))


def _roll(x, s, axis):
    return pltpu.roll(x, s, axis)


def _rope(x, c, s1, s2):
    return x * c + _roll(x, HEAD_LANES - MLA_ROPE // 2, 1) * s1 + _roll(x, MLA_ROPE // 2, 1) * s2


def _heads_apply(x, fn):
    return jnp.concatenate([fn(x[:, h * HEAD_LANES:(h + 1) * HEAD_LANES]) for h in range(N_HEADS)], axis=1)


def _ffn_fwd(tag, x, g_pre, wg, wu, wd, g_post, tm):
    t, d = x.shape
    ns, _, fs = wg.shape
    nt = t // tm
    row = pl.BlockSpec((tm, d), lambda i, q: (i, 0))
    vec = pl.BlockSpec((1, d), lambda i, q: (0, 0))
    act3 = pl.BlockSpec((1, tm, fs), lambda i, q: (q, i, 0))
    wcol = pl.BlockSpec((1, d, fs), lambda i, q: (q, 0, 0))
    wrow = pl.BlockSpec((1, fs, d), lambda i, q: (q, 0, 0))

    def gate_up(x_ref, g_ref, wg_ref, wu_ref, n_ref, a_ref, u_ref, s_ref, n_s):
        @pl.when(pl.program_id(1) == 0)
        def _():
            xb = x_ref[...]
            n_s[...] = (xb * _rms_stats(xb) * g_ref[...]).astype(MM_DTYPE)
            n_ref[...] = n_s[...]

        n = n_s[...]
        a = jnp.dot(n, wg_ref[0], preferred_element_type=F32)
        u = jnp.dot(n, wu_ref[0], preferred_element_type=F32)
        a_ref[0] = a.astype(a_ref.dtype)
        u_ref[0] = u.astype(u_ref.dtype)
        s_ref[0] = ((a * _sigmoid(a)) * u).astype(s_ref.dtype)

    n, a, u, s = pl.pallas_call(
        gate_up, name=tag + "_gate_up", grid=(nt, ns), in_specs=[row, vec, wcol, wcol],
        out_specs=[row, act3, act3, act3],
        out_shape=[jax.ShapeDtypeStruct((t, d), MM_DTYPE)] + [jax.ShapeDtypeStruct((ns, t, fs), MM_DTYPE)] * 3,
        scratch_shapes=[pltpu.VMEM((tm, d), MM_DTYPE)],
        compiler_params=_params(("parallel", "arbitrary")))(x, g_pre, wg, wu)

    def down(s_ref, wd_ref, x_ref, g_ref, h_ref, y_ref, acc):
        q = pl.program_id(1)

        @pl.when(q == 0)
        def _():
            acc[...] = jnp.zeros(acc.shape, F32)

        acc[...] += jnp.dot(s_ref[0], wd_ref[0], preferred_element_type=F32)

        @pl.when(q == ns - 1)
        def _():
            hb = acc[...]
            h_ref[...] = hb
            y_ref[...] = x_ref[...] + 0.5 * (hb * _rms_stats(hb) * g_ref[...])

    h, y = pl.pallas_call(
        down, name=tag + "_down", grid=(nt, ns), in_specs=[act3, wrow, row, vec], out_specs=[row, row],
        out_shape=[jax.ShapeDtypeStruct((t, d), F32)] * 2, scratch_shapes=[pltpu.VMEM((tm, d), F32)],
        compiler_params=_params(("parallel", "arbitrary")))(s, wd, x, g_post)
    return y, (x, n, a, u, s, h)


def _ffn_bwd(tag, dy, saved, g_pre, wg, wu, wd, g_post, tm, tk):
    x, n, a, u, s, h = saved
    t, d = x.shape
    ns, _, fs = wg.shape
    nt, nk = t // tm, t // tk
    row = pl.BlockSpec((tm, d), lambda i, q: (i, 0))
    vec = pl.BlockSpec((1, d), lambda i, q: (0, 0))
    act3 = pl.BlockSpec((1, tm, fs), lambda i, q: (q, i, 0))
    wcol = pl.BlockSpec((1, d, fs), lambda i, q: (q, 0, 0))
    wrow = pl.BlockSpec((1, fs, d), lambda i, q: (q, 0, 0))
    nt_dims = (((1,), (1,)), ((), ()))
    tn_dims = (((0,), (0,)), ((), ()))

    def down_b(h_ref, dy_ref, g_ref, wd_ref, a_ref, u_ref, dh_ref, da_ref, du_ref, dg_ref, dh_s):
        i, q = pl.program_id(0), pl.program_id(1)

        @pl.when((i == 0) & (q == 0))
        def _():
            dg_ref[...] = jnp.zeros(dg_ref.shape, F32)

        @pl.when(q == 0)
        def _():
            hb = h_ref[...]
            dh, dg = _rms_bwd(hb, _rms_stats(hb), g_ref[...], 0.5 * dy_ref[...])
            dh_s[...] = dh.astype(MM_DTYPE)
            dh_ref[...] = dh_s[...]
            dg_ref[...] += dg

        ds = lax.dot_general(dh_s[...], wd_ref[0], nt_dims, preferred_element_type=F32)
        ab, ub = a_ref[0].astype(F32), u_ref[0].astype(F32)
        sg = _sigmoid(ab)
        da_ref[0] = (ds * ub * (sg * (1.0 + ab * (1.0 - sg)))).astype(da_ref.dtype)
        du_ref[0] = (ds * (ab * sg)).astype(du_ref.dtype)

    dh, da, du, dg_post = pl.pallas_call(
        down_b, name=tag + "_down_b", grid=(nt, ns), in_specs=[row, row, vec, wrow, act3, act3],
        out_specs=[row, act3, act3, vec],
        out_shape=[jax.ShapeDtypeStruct((t, d), MM_DTYPE)] + [jax.ShapeDtypeStruct((ns, t, fs), MM_DTYPE)] * 2
        + [jax.ShapeDtypeStruct((1, d), F32)],
        scratch_shapes=[pltpu.VMEM((tm, d), MM_DTYPE)],
        compiler_params=_params(("arbitrary", "arbitrary")))(h, dy, g_post, wd, a, u)

    def down_w(s_ref, dh_ref, dw_ref, acc):
        kk = pl.program_id(1)

        @pl.when(kk == 0)
        def _():
            acc[...] = jnp.zeros(acc.shape, F32)

        acc[...] += lax.dot_general(s_ref[0], dh_ref[...], tn_dims, preferred_element_type=F32)

        @pl.when(kk == nk - 1)
        def _():
            dw_ref[0] = acc[...].astype(dw_ref.dtype)

    dwd = pl.pallas_call(
        down_w, name=tag + "_down_w", grid=(ns, nk),
        in_specs=[pl.BlockSpec((1, tk, fs), lambda q, kk: (q, kk, 0)), pl.BlockSpec((tk, d), lambda q, kk: (kk, 0))],
        out_specs=pl.BlockSpec((1, fs, d), lambda q, kk: (q, 0, 0)),
        out_shape=jax.ShapeDtypeStruct((ns, fs, d), MM_DTYPE), scratch_shapes=[pltpu.VMEM((fs, d), F32)],
        compiler_params=_params(("parallel", "arbitrary")))(s, dh)

    def gate_up_b(da_ref, du_ref, wg_ref, wu_ref, x_ref, dy_ref, g_ref, dx_ref, dg_ref, acc):
        i, q = pl.program_id(0), pl.program_id(1)

        @pl.when((i == 0) & (q == 0))
        def _():
            dg_ref[...] = jnp.zeros(dg_ref.shape, F32)

        @pl.when(q == 0)
        def _():
            acc[...] = jnp.zeros(acc.shape, F32)

        acc[...] += (lax.dot_general(da_ref[0], wg_ref[0], nt_dims, preferred_element_type=F32)
                     + lax.dot_general(du_ref[0], wu_ref[0], nt_dims, preferred_element_type=F32))

        @pl.when(q == ns - 1)
        def _():
            xb = x_ref[...]
            dx, dg = _rms_bwd(xb, _rms_stats(xb), g_ref[...], acc[...])
            dx_ref[...] = dy_ref[...] + dx
            dg_ref[...] += dg

    dx, dg_pre = pl.pallas_call(
        gate_up_b, name=tag + "_gate_up_b", grid=(nt, ns), in_specs=[act3, act3, wcol, wcol, row, row, vec],
        out_specs=[row, vec], out_shape=[jax.ShapeDtypeStruct((t, d), F32), jax.ShapeDtypeStruct((1, d), F32)],
        scratch_shapes=[pltpu.VMEM((tm, d), F32)],
        compiler_params=_params(("arbitrary", "arbitrary")))(da, du, wg, wu, x, dy, g_pre)

    def gate_up_w(n_ref, da_ref, du_ref, dwg_ref, dwu_ref, acc_g, acc_u):
        kk = pl.program_id(1)

        @pl.when(kk == 0)
        def _():
            acc_g[...] = jnp.zeros(acc_g.shape, F32)
            acc_u[...] = jnp.zeros(acc_u.shape, F32)

        nb = n_ref[...]
        acc_g[...] += lax.dot_general(nb, da_ref[0], tn_dims, preferred_element_type=F32)
        acc_u[...] += lax.dot_general(nb, du_ref[0], tn_dims, preferred_element_type=F32)

        @pl.when(kk == nk - 1)
        def _():
            dwg_ref[0] = acc_g[...].astype(dwg_ref.dtype)
            dwu_ref[0] = acc_u[...].astype(dwu_ref.dtype)

    k3 = pl.BlockSpec((1, tk, fs), lambda q, kk: (q, kk, 0))
    wout = pl.BlockSpec((1, d, fs), lambda q, kk: (q, 0, 0))
    dwg, dwu = pl.pallas_call(
        gate_up_w, name=tag + "_gate_up_w", grid=(ns, nk),
        in_specs=[pl.BlockSpec((tk, d), lambda q, kk: (kk, 0)), k3, k3], out_specs=[wout, wout],
        out_shape=[jax.ShapeDtypeStruct((ns, d, fs), MM_DTYPE)] * 2,
        scratch_shapes=[pltpu.VMEM((d, fs), F32)] * 2,
        compiler_params=_params(("parallel", "arbitrary")))(n, da, du)
    return dx, dg_pre, dwg, dwu, dwd, dg_post


NEG = -1e30


def _attn_scale():
    return (MLA_NOPE + MLA_ROPE) ** -0.5


def _causal_pairs(nq, by_key):
    if by_key:
        pairs = [(qi, ki) for ki in range(nq) for qi in range(ki, nq)]
    else:
        pairs = [(qi, ki) for qi in range(nq) for ki in range(qi + 1)]
    return jnp.asarray([p[0] for p in pairs], jnp.int32), jnp.asarray([p[1] for p in pairs], jnp.int32)


def _below_diagonal(shape):
    return lax.broadcasted_iota(jnp.int32, shape, 1) <= lax.broadcasted_iota(jnp.int32, shape, 0)


def _attn_call(name, body, tables, args, in_kinds, out_kinds, scratch, t, tq, carried=None):
    qmap = lambda h, p, qt, kt: (qt[p], h)
    kmap = lambda h, p, qt, kt: (kt[p], h)
    spec = lambda kind: pl.BlockSpec((tq, HEAD_LANES), qmap if kind == "q" else kmap)
    n_pairs = tables[0].shape[0]
    n_in, n_out, n_scr = len(in_kinds), len(out_kinds), len(scratch)
    x_ins = list(carried.ins) if carried else []
    x_outs = list(carried.outs) if carried else []
    x_scr = [pltpu.SemaphoreType.DMA((carried.n_sem,)), pltpu.SemaphoreType.DMA((carried.n_sem,))] if carried else []

    def full_body(qt, kt, *refs):
        ins, refs = refs[:n_in], refs[n_in:]
        xi, refs = refs[:len(x_ins)], refs[len(x_ins):]
        outs, refs = refs[:n_out], refs[n_out:]
        xo, refs = refs[:len(x_outs)], refs[len(x_outs):]
        scr, sems = refs[:n_scr], refs[n_scr:]
        if carried:
            @pl.when((pl.program_id(0) == 0) & (pl.program_id(1) == 0))
            def _():
                carried.start(xi, xo, *sems)

        body(qt, kt, *ins, *outs, *scr)
        if carried:
            @pl.when((pl.program_id(0) == N_HEADS - 1) & (pl.program_id(1) == n_pairs - 1))
            def _():
                carried.finish(xi, xo, *sems)

    grid_spec = pltpu.PrefetchScalarGridSpec(
        num_scalar_prefetch=2, grid=(N_HEADS, n_pairs),
        in_specs=[spec(kd) for kd in in_kinds] + [HBM_SPEC] * len(x_ins),
        out_specs=[spec(kd) for kd in out_kinds] + [HBM_SPEC] * len(x_outs), scratch_shapes=list(scratch) + x_scr)
    return pl.pallas_call(full_body, name=name, grid_spec=grid_spec,
                          out_shape=[jax.ShapeDtypeStruct((t, MLA_PAD), F32) for _ in out_kinds] + x_outs,
                          compiler_params=_params(("arbitrary", "arbitrary")))(*tables, *args, *x_ins)


class _Carried:
    def __init__(self, ins, outs, n_sem, start, finish):
        self.ins, self.outs, self.n_sem, self.start, self.finish = ins, outs, n_sem, start, finish


def _attn_fwd(q, k, v, tq, carried=None):
    t = q.shape[0]
    nq = t // tq

    def body(qt, kt, q_ref, k_ref, v_ref, o_ref, lse_ref, m_s, l_s, acc_s):
        p_id = pl.program_id(1)
        qi, ki = qt[p_id], kt[p_id]

        @pl.when(ki == 0)
        def _():
            m_s[...] = jnp.full(m_s.shape, NEG, F32)
            l_s[...] = jnp.zeros(l_s.shape, F32)
            acc_s[...] = jnp.zeros(acc_s.shape, F32)

        def scores():
            return lax.dot_general(q_ref[...], k_ref[...], (((1,), (1,)), ((), ())), preferred_element_type=F32)

        def update(s):
            m_old = m_s[...]
            m_new = jnp.maximum(m_old, jnp.max(s, axis=1, keepdims=True))
            alpha = jnp.exp(m_old - m_new)
            p = jnp.exp(s - m_new[:, :1])
            l_s[...] = l_s[...] * alpha + jnp.sum(p, axis=1, keepdims=True)
            acc_s[...] = acc_s[...] * alpha + jnp.dot(p.astype(MM_DTYPE), v_ref[...], preferred_element_type=F32)
            m_s[...] = m_new

        @pl.when(ki < qi)
        def _():
            update(scores())

        @pl.when(ki == qi)
        def _():
            s = scores()
            update(jnp.where(_below_diagonal(s.shape), s, NEG))
            o_ref[...] = acc_s[...] / l_s[...]
            lse_ref[...] = m_s[...] + jnp.log(l_s[...])

    return _attn_call("mla_attn_fwd", body, _causal_pairs(nq, False), (q, k, v), "qkk", "qq",
                      [pltpu.VMEM((tq, HEAD_LANES), F32)] * 3, t, tq, carried)


def _attn_probs(q, k, lse, diagonal):
    s = lax.dot_general(q, k, (((1,), (1,)), ((), ())), preferred_element_type=F32)
    p = jnp.exp(s - lse[:, :1])
    return jnp.where(_below_diagonal(s.shape), p, 0.0) if diagonal else p


def _attn_bwd_dq(q, k, v, do, lse, delta, tq):
    t = q.shape[0]
    nq = t // tq

    def body(qt, kt, q_ref, k_ref, v_ref, do_ref, lse_ref, dl_ref, dq_ref, acc_s):
        p_id = pl.program_id(1)
        qi, ki = qt[p_id], kt[p_id]

        @pl.when(ki == 0)
        def _():
            acc_s[...] = jnp.zeros(acc_s.shape, F32)

        def step(diagonal):
            p = _attn_probs(q_ref[...], k_ref[...], lse_ref[...], diagonal)
            dp = lax.dot_general(do_ref[...], v_ref[...], (((1,), (1,)), ((), ())), preferred_element_type=F32)
            ds = p * (dp - dl_ref[...][:, :1])
            acc_s[...] += jnp.dot(ds.astype(MM_DTYPE), k_ref[...], preferred_element_type=F32)

        @pl.when(ki < qi)
        def _():
            step(False)

        @pl.when(ki == qi)
        def _():
            step(True)
            dq_ref[...] = acc_s[...]

    return _attn_call("mla_attn_bwd_dq", body, _causal_pairs(nq, False), (q, k, v, do, lse, delta), "qkkqqq", "q",
                      [pltpu.VMEM((tq, HEAD_LANES), F32)], t, tq)[0]


def _attn_bwd_dkv(q, k, v, do, lse, delta, tq, carried=None):
    t = q.shape[0]
    nq = t // tq

    def body(qt, kt, q_ref, k_ref, v_ref, do_ref, lse_ref, dl_ref, dk_ref, dv_ref, dk_s, dv_s):
        p_id = pl.program_id(1)
        qi, ki = qt[p_id], kt[p_id]

        def step(diagonal):
            p = _attn_probs(q_ref[...], k_ref[...], lse_ref[...], diagonal)
            dv_s[...] += lax.dot_general(p.astype(MM_DTYPE), do_ref[...], (((0,), (0,)), ((), ())),
                                         preferred_element_type=F32)
            dp = lax.dot_general(do_ref[...], v_ref[...], (((1,), (1,)), ((), ())), preferred_element_type=F32)
            ds = p * (dp - dl_ref[...][:, :1])
            dk_s[...] += lax.dot_general(ds.astype(MM_DTYPE), q_ref[...], (((0,), (0,)), ((), ())),
                                         preferred_element_type=F32)

        @pl.when(qi == ki)
        def _():
            dk_s[...] = jnp.zeros(dk_s.shape, F32)
            dv_s[...] = jnp.zeros(dv_s.shape, F32)
            step(True)

        @pl.when(qi > ki)
        def _():
            step(False)

        @pl.when(qi == nq - 1)
        def _():
            dk_ref[...] = dk_s[...]
            dv_ref[...] = dv_s[...]

    return _attn_call("mla_attn_bwd_dkv", body, _causal_pairs(nq, True), (q, k, v, do, lse, delta), "qkkqqq", "kk",
                      [pltpu.VMEM((tq, HEAD_LANES), F32)] * 2, t, tq, carried)


def _dotf(a, b, dims=(((1,), (0,)), ((), ()))):
    return lax.dot_general(a, b, dims, preferred_element_type=F32, precision=HI)


def _dot1(a, b, dims=(((1,), (0,)), ((), ()))):
    return lax.dot_general(a.astype(MM_DTYPE), b.astype(MM_DTYPE), dims, preferred_element_type=F32)


def _dot3(a, b, dims=(((1,), (0,)), ((), ()))):
    return lax.dot_general(a, b, dims, preferred_element_type=F32, precision=lax.Precision.HIGH)


NN3 = (((2,), (1,)), ((0,), (0,)))
NT3 = (((2,), (2,)), ((0,), (0,)))
TN3 = (((1,), (1,)), ((0,), (0,)))


def _tri_masks(nh):
    shape = (nh, CHUNK, CHUNK)
    return lax.broadcasted_iota(jnp.int32, shape, 1), lax.broadcasted_iota(jnp.int32, shape, 2)


def _gdn_chunk_common(k, gcc, bb, row, col, dot=_dot1):
    tril = row >= col
    gcr = jnp.swapaxes(gcc, 1, 2)
    dm = jnp.exp(jnp.where(tril, gcc - gcr, NEG))
    kb = k * bb
    lm = jnp.where(row > col, dot(kb, k, NT3) * dm, 0.0)
    return dm, kb, lm


def _unit_lower_inverse(lm, eye):
    t = eye - lm
    p = lm
    for _ in range(CHUNK.bit_length() - 2):
        p = _dot3(p, p, NN3)
        t = t + _dot3(t, p, NN3)
    return t


def _chunk_sum_matrix(tb, upper):
    r = lax.broadcasted_iota(jnp.int32, (tb, tb), 0)
    c = lax.broadcasted_iota(jnp.int32, (tb, tb), 1)
    same = (r // CHUNK) == (c // CHUNK)
    return (same & ((c >= r) if upper else (c <= r))).astype(F32)


def _gdn_fwd(q, k, v, gb, bb):
    nh, t, dh = q.shape
    nchunk = t // CHUNK

    def body(q_ref, k_ref, v_ref, g_ref, b_ref, o_ref, sall_ref, tall_ref, s_s):
        @pl.when(pl.program_id(0) == 0)
        def _():
            s_s[...] = jnp.zeros(s_s.shape, F32)

        row, col = _tri_masks(nh)
        qh, kh, vh, bbh, gcc = q_ref[...], k_ref[...], v_ref[...], b_ref[...], g_ref[...]
        dm, kb, lm = _gdn_chunk_common(kh, gcc, bbh, row, col)
        eg = jnp.exp(gcc)
        glr = gcc[:, CHUNK - 1:CHUNK, :]
        th = _unit_lower_inverse(lm, (row == col).astype(F32))
        w = _dot1(th, kb * eg, NN3)
        u = _dot1(th, vh * bbh, NN3)
        at = jnp.where(row >= col, _dot1(qh, kh, NT3) * dm, 0.0)
        sh = s_s[...]
        vn = u - _dot1(w, sh, NN3)
        o_ref[...] = _dot1(qh * eg, sh, NN3) + _dot1(at, vn, NN3)
        kd = kh * jnp.exp(glr - gcc)
        sall_ref[:, 0] = sh
        tall_ref[...] = th
        s_s[...] = sh * jnp.exp(glr) + _dot1(kd, vn, TN3)

    blk = pl.BlockSpec((nh, CHUNK, dh), lambda n: (0, n, 0))
    return pl.pallas_call(
        body, name="gdn_fwd", grid=(nchunk,), in_specs=[blk] * 5,
        out_specs=[blk, pl.BlockSpec((nh, 1, dh, dh), lambda n: (0, n, 0, 0)), blk],
        out_shape=[jax.ShapeDtypeStruct((nh, t, dh), F32), jax.ShapeDtypeStruct((nh, nchunk, dh, dh), F32),
                   jax.ShapeDtypeStruct((nh, t, CHUNK), F32)],
        scratch_shapes=[pltpu.VMEM((nh, dh, dh), F32)],
        compiler_params=_params(("arbitrary",)))(q, k, v, gb, bb)


def _gdn_bwd(q, k, v, gb, bb, sall, tall, do):
    nh, t, dh = q.shape
    nchunk = t // CHUNK

    def body(q_ref, k_ref, v_ref, g_ref, b_ref, sall_ref, tall_ref, do_ref,
             dq_ref, dk_ref, dv_ref, dg_ref, db_ref, ds_s):
        @pl.when(pl.program_id(0) == 0)
        def _():
            ds_s[...] = jnp.zeros(ds_s.shape, F32)

        row, col = _tri_masks(nh)
        tril, stril = row >= col, row > col
        rsum = lambda x: jnp.sum(x, axis=2, keepdims=True)
        qh, kh, vh, gcc, bbh = q_ref[...], k_ref[...], v_ref[...], g_ref[...], b_ref[...]
        sh, th, doh, dsp = sall_ref[:, 0], tall_ref[...], do_ref[...], ds_s[...]
        dm, kb, lm = _gdn_chunk_common(kh, gcc, bbh, row, col, _dot3)
        eg = jnp.exp(gcc)
        glr = gcc[:, CHUNK - 1:CHUNK, :]
        glv = jnp.exp(glr)
        egl = jnp.exp(glr - gcc)
        rw, ru = kb * eg, vh * bbh
        w, u = _dot3(th, rw, NN3), _dot3(th, ru, NN3)
        at = jnp.where(tril, _dot3(qh, kh, NT3) * dm, 0.0)
        qd, kd = qh * eg, kh * egl
        vn = u - _dot3(w, sh, NN3)
        dgl = jnp.sum(rsum(dsp * sh), axis=1, keepdims=True)
        dkd = _dot3(vn, dsp, NT3)
        dvn = _dot3(kd, dsp, NN3)
        dqd = _dot3(doh, sh, NT3)
        dat = jnp.where(tril, _dot3(doh, vn, NT3), 0.0)
        dvn = dvn + _dot3(at, doh, TN3)
        dw = -_dot3(dvn, sh, NT3)
        ds_s[...] = dsp * glv + _dot3(qd, doh, TN3) - _dot3(w, dvn, TN3)
        dpa = dat * dm
        dq_ref[...] = _dot3(dpa, kh, NN3) + dqd * eg
        dk = _dot3(dpa, qh, TN3) + dkd * egl
        t6 = rsum(dkd * kd)
        dgam = rsum(dqd * qd) - t6
        dgam_last = jnp.sum(t6, axis=1, keepdims=True) + dgl * glv
        drw = _dot3(th, dw, TN3)
        dru = _dot3(th, dvn, TN3)
        dl = -jnp.where(stril, _dot3(drw, w, NT3) + _dot3(dru, u, NT3), 0.0)
        dgam = dgam + rsum(drw * rw)
        dv_ref[...] = dru * bbh
        dp2 = dl * dm
        dkb = drw * eg + _dot3(dp2, kh, NN3)
        dk_ref[...] = dk + _dot3(dp2, kb, TN3) + dkb * bbh
        db_ref[...] = rsum(dru * vh) + rsum(dkb * kh) + jnp.zeros((nh, CHUNK, dh), F32)
        e = dat * at + dl * lm
        dgam_b = dgam + rsum(e) - _dotf(e, jnp.ones((nh, CHUNK, CHUNK), F32), TN3)
        dgam_b = dgam_b + jnp.where(row == CHUNK - 1, dgam_last, 0.0)
        dg_ref[...] = dgam_b

    rev = lambda n: (0, nchunk - 1 - n, 0)
    blk = pl.BlockSpec((nh, CHUNK, dh), rev)
    sblk = pl.BlockSpec((nh, 1, dh, dh), lambda n: (0, nchunk - 1 - n, 0, 0))
    out = jax.ShapeDtypeStruct((nh, t, dh), F32)
    return pl.pallas_call(
        body, name="gdn_bwd", grid=(nchunk,), in_specs=[blk] * 5 + [sblk, blk, blk], out_specs=[blk] * 5,
        out_shape=[out] * 5, scratch_shapes=[pltpu.VMEM((nh, dh, dh), F32)],
        compiler_params=_params(("arbitrary",)))(q, k, v, gb, bb, sall, tall, do)


def _group_ones():
    r = lax.broadcasted_iota(jnp.int32, (GDN_W, GDN_W), 0) // GDN_DH
    c = lax.broadcasted_iota(jnp.int32, (GDN_W, GDN_W), 1) // GDN_DH
    return (r == c).astype(F32)


def _conv_taps(x, xprev, w, has_prev):
    row = lax.broadcasted_iota(jnp.int32, x.shape, 0)
    out = x * w[GDN_CONV - 1:GDN_CONV, :]
    for s in range(1, GDN_CONV):
        sh = jnp.where(row >= s, _roll(x, s, 0), _roll(xprev, s, 0) * has_prev)
        out = out + sh * w[GDN_CONV - 1 - s:GDN_CONV - s, :]
    return out


def _head_cols(x, h):
    return x[:, h * GDN_DH:(h + 1) * GDN_DH]


def _heads_spec(tb):
    return pl.BlockSpec((N_HEADS, tb, GDN_DH), lambda i: (0, i, 0))


def _mixer_fwd(x, positions, w, tb, carried=None):
    t, d = x.shape
    tables = _rope_tables(positions)

    def pre(xb, g):
        return (xb * _rms_stats(xb) * g,)

    (hn,) = _rowwise("mix_pre", pre, [x], [w["mix_pre_g"]], [(d, BF16)], [], tb)
    proj = _mm("mix_in", hn, w["w_in_pad"], "nn", F32)

    def mla_pre(p0, gq, gkv):
        cq, ckv = p0[:, :MLA_Q_RANK], p0[:, MLA_Q_RANK:MLA_Q_RANK + MLA_KV_RANK]
        return cq * _rms_stats(cq) * gq, ckv * _rms_stats(ckv) * gkv

    nq, nkv = _rowwise("mla_pre", mla_pre, [(proj, 512, PIN_MLA // 512, 0)],
                       [w["mla_q_norm_g"], w["mla_kv_norm_g"]], [(MLA_Q_RANK, BF16), (MLA_KV_RANK, BF16)], [], tb)
    qraw = _mm("mla_uq", nq, w["w_uq_pad"], "nn", F32)
    kv = _mm("mla_ukv", nkv, w["w_kv_pad"], "nn", F32)

    def rope_f(qr, kn, vv, kpe, c, s1, s2):
        qo = _heads_apply(qr, lambda xh: _rope(xh, c, s1, s2)) * _attn_scale()
        kp = _rope(kpe, c, s1, s2)
        return qo, kn + jnp.tile(kp, (1, N_HEADS)), vv

    q, k, v = _rowwise("mla_rope", rope_f,
                       [qraw, (kv, MLA_PAD, 0, 0), (kv, MLA_PAD, 1, 0), (proj, HEAD_LANES, PIN_KPE // HEAD_LANES, 0),
                        tables[0], tables[1], tables[2]], [],
                       [(MLA_PAD, BF16)] * 3, [], tb // 2)
    tq = min(512, t)
    o, lse, *carried_out = _attn_fwd(q, k, v, tq, carried)

    def mla_post(ob, g):
        return (ob * _rms_stats(ob, N_HEADS * MLA_V) * g,)

    (cat,) = _rowwise("mla_post", mla_post, [o], [w["mla_out_g_pad"]], [(MLA_PAD, BF16)], [], tb, wide=(CAT_W, 0))

    gones = _group_ones()
    steps = t // tb

    def gdn_pre(xq, xk, xv, pq, pk, pv, cw, go, has_prev):
        outs = []
        for j, (xc, xp) in enumerate(((xq, pq), (xk, pk), (xv, pv))):
            c = _conv_taps(xc, xp, cw[:, j * GDN_W:(j + 1) * GDN_W], has_prev)
            a = c * _sigmoid(c)
            if j < 2:
                rn = lax.rsqrt(_dotf(a * a, go) + EPS)
                a = a * rn
                if j == 0:
                    a = a * (GDN_DH ** -0.5)
            outs.append(a)
        return tuple(outs)

    qh, kh, vh = _gdn_pre_call("gdn_pre", gdn_pre, proj, w["conv_w"], gones, tb, steps)
    heads_shape = jax.ShapeDtypeStruct((N_HEADS, t, GDN_DH), F32)
    lanes_shape = jax.ShapeDtypeStruct((t, HEAD_LANES), F32)
    lanes_spec = pl.BlockSpec((tb, HEAD_LANES), lambda i: (i, 0))
    vec_spec = lambda n: pl.BlockSpec((1, n), lambda i: (0, 0))

    def gate_f(ab_ref, al_ref, dt_ref, g_ref, b_ref, gh_ref, bh_ref):
        g, b = _gb_fwd(ab_ref[...], al_ref[...], dt_ref[...])
        g_ref[...] = g
        b_ref[...] = b
        gc = _dotf(_chunk_sum_matrix(tb, False), g)
        for h in range(N_HEADS):
            gh_ref[h] = jnp.broadcast_to(gc[:, h:h + 1], (tb, GDN_DH))
            bh_ref[h] = jnp.broadcast_to(b[:, N_HEADS + h:N_HEADS + h + 1], (tb, GDN_DH))

    g128, b128, gbh, bbh = pl.pallas_call(
        gate_f, name="gdn_gate_f", grid=(steps,),
        in_specs=[pl.BlockSpec((tb, HEAD_LANES), lambda i: (i, PIN_AB // HEAD_LANES)), vec_spec(HEAD_LANES),
                  vec_spec(HEAD_LANES)],
        out_specs=[lanes_spec, lanes_spec, _heads_spec(tb), _heads_spec(tb)],
        out_shape=[lanes_shape, lanes_shape, heads_shape, heads_shape],
        compiler_params=_params(("arbitrary",)))(proj, w["a_log_pad"], w["dt_bias_pad"])
    oh, sall, tall = _gdn_fwd(qh, kh, vh, gbh, bbh)

    def gdn_post(o_ref, gt_ref, g_ref, cat_in, cat_ref):
        gt, g = gt_ref[...], g_ref[...]
        outs = []
        for h in range(N_HEADS):
            ob, gth = o_ref[h], _head_cols(gt, h)
            outs.append(ob * _rms_stats(ob) * g * (gth * _sigmoid(gth)))
        cat_ref[...] = jnp.concatenate(outs, axis=1).astype(cat_ref.dtype)

    gate_spec = pl.BlockSpec((tb, GDN_W), lambda i: (i, PIN_GATE // GDN_W))
    cat = pl.pallas_call(
        gdn_post, name="gdn_post", grid=(steps,),
        in_specs=[_heads_spec(tb), gate_spec, vec_spec(GDN_DH), ANY_SPEC],
        out_specs=pl.BlockSpec((tb, GDN_W), lambda i: (i, MLA_PAD // GDN_W)),
        out_shape=jax.ShapeDtypeStruct((t, CAT_W), BF16), input_output_aliases={3: 0},
        compiler_params=_params(("arbitrary",)))(oh, proj, w["gdn_norm_g"], cat)
    mixed = _mm("mix_out", cat, w["w_out_pad"], "nn", F32)

    def post(xb, hb, g):
        return (xb + hb * _rms_stats(hb) * g,)

    (y,) = _rowwise("mix_post", post, [x, mixed], [w["mix_post_g"]], [(d, F32)], [], tb)
    saved = dict(x=x, hn=hn, proj=proj, nq=nq, nkv=nkv, q=q, k=k, v=v, o=o, lse=lse, qh=qh, kh=kh, vh=vh,
                 gbh=gbh, bbh=bbh, oh=oh, sall=sall, tall=tall, cat=cat, mixed=mixed,
                 tables=tables, g128=g128, b128=b128)
    return y, saved, carried_out


def _qkv_specs(tb):
    base = PIN_QKV // GDN_W
    cur = [pl.BlockSpec((tb, GDN_W), lambda i, j=j: (i, base + j)) for j in range(3)]
    prev = [pl.BlockSpec((tb, GDN_W), lambda i, j=j: (jnp.maximum(i - 1, 0), base + j)) for j in range(3)]
    return cur + prev


def _gdn_pre_call(name, fn, proj, conv_w, gones, tb, steps):
    t = proj.shape[0]

    def body(xq, xk, xv, pq, pk, pv, cw, go, oq, ok, ov):
        has_prev = jnp.where(pl.program_id(0) == 0, 0.0, 1.0)
        outs = fn(xq[...], xk[...], xv[...], pq[...], pk[...], pv[...], cw[...], go[...], has_prev)
        for r, val in zip((oq, ok, ov), outs):
            for h in range(N_HEADS):
                r[h] = _head_cols(val, h)

    return pl.pallas_call(
        body, name=name, grid=(steps,),
        in_specs=_qkv_specs(tb) + [pl.BlockSpec(conv_w.shape, lambda i: (0, 0)),
                                   pl.BlockSpec(gones.shape, lambda i: (0, 0))],
        out_specs=[_heads_spec(tb)] * 3,
        out_shape=[jax.ShapeDtypeStruct((N_HEADS, t, GDN_DH), F32)] * 3,
        compiler_params=_params(("arbitrary",)))(proj, proj, proj, proj, proj, proj, conv_w, gones)


def _softplus(x):
    return jnp.maximum(x, 0.0) + jnp.log1p(jnp.exp(-jnp.abs(x)))


def _gb_fwd(ab, a_log, dt_bias):
    g = -jnp.exp(a_log) * _softplus(ab + dt_bias)
    return g, _sigmoid(ab)


def _rope_tables(positions):
    half = MLA_ROPE // 2
    freqs = ROPE_THETA ** (-jnp.arange(half, dtype=F32) / half)
    ang = positions.reshape(-1).astype(F32)[:, None] * freqs
    cos, sin = jnp.cos(ang), jnp.sin(ang)
    t = ang.shape[0]
    one = jnp.ones((t, MLA_NOPE), F32)
    z16, z32, z64 = jnp.zeros((t, half), F32), jnp.zeros((t, MLA_ROPE), F32), jnp.zeros((t, MLA_NOPE), F32)
    c = jnp.concatenate([one, cos, cos, jnp.ones((t, MLA_ROPE), F32)], axis=1)
    s1 = jnp.concatenate([z64, -sin, z16, z32], axis=1)
    s2 = jnp.concatenate([z64, z16, sin, z32], axis=1)
    return c, s1, s2


def _mixer_bwd(dy, sv, w, tb, carried=None):
    x, proj = sv["x"], sv["proj"]
    t, d = x.shape
    c, s1, s2 = sv["tables"]
    grads = {}

    def post_b(hb, dyb, g):
        return _rms_bwd(hb, _rms_stats(hb), g, dyb)

    dmixed, grads["mix_post_g"] = _rowwise("mix_post_b", post_b, [sv["mixed"], dy], [w["mix_post_g"]],
                                           [(d, BF16)], [(1, d)], tb)
    dcat = _mm("mix_out_bx", dmixed, w["w_out_pad"], "nt", F32)
    grads["w_out_pad"] = _mm("mix_out_bw", sv["cat"], dmixed, "tn", F32)
    steps = t // tb
    vec_spec = lambda n: pl.BlockSpec((1, n), lambda i: (0, 0))

    def gdn_post_b(o_ref, gt_ref, do_ref, g_ref, dproj_ref, doh_ref, dg_ref):
        @pl.when(pl.program_id(0) == 0)
        def _():
            dg_ref[...] = jnp.zeros(dg_ref.shape, F32)

        gt, dob, g = gt_ref[...], do_ref[...], g_ref[...]
        dgates = []
        for h in range(N_HEADS):
            ob, gth, dobh = o_ref[h], _head_cols(gt, h), _head_cols(dob, h)
            sg = _sigmoid(gth)
            r = _rms_stats(ob)
            dxo, dg = _rms_bwd(ob, r, g, dobh * (gth * sg))
            doh_ref[h] = dxo
            dg_ref[...] += dg
            dgates.append(dobh * (ob * r * g) * (sg * (1.0 + gth * (1.0 - sg))))
        dproj_ref[...] = jnp.concatenate(dgates, axis=1).astype(dproj_ref.dtype)

    dproj, doh, grads["gdn_norm_g"] = pl.pallas_call(
        gdn_post_b, name="gdn_post_b", grid=(steps,),
        in_specs=[_heads_spec(tb), pl.BlockSpec((tb, GDN_W), lambda i: (i, PIN_GATE // GDN_W)),
                  pl.BlockSpec((tb, GDN_W), lambda i: (i, MLA_PAD // GDN_W)), vec_spec(GDN_DH)],
        out_specs=[pl.BlockSpec((tb, GDN_W), lambda i: (i, PIN_GATE // GDN_W)), _heads_spec(tb), vec_spec(GDN_DH)],
        out_shape=[jax.ShapeDtypeStruct((t, PIN_W), BF16), jax.ShapeDtypeStruct((N_HEADS, t, GDN_DH), F32),
                   jax.ShapeDtypeStruct((1, GDN_DH), F32)],
        compiler_params=_params(("arbitrary",)))(sv["oh"], proj, dcat, w["gdn_norm_g"])

    def mla_post_b(ob, dmo, g):
        do, dg = _rms_bwd(ob, _rms_stats(ob, N_HEADS * MLA_V), g, dmo, N_HEADS * MLA_V)
        prod = do * ob
        delta = _heads_apply(prod, lambda ph: jnp.sum(ph, axis=1, keepdims=True) + jnp.zeros_like(ph))
        return do, delta, dg

    do, delta, grads["mla_out_g_pad"] = _rowwise(
        "mla_post_b", mla_post_b, [sv["o"], (dcat, MLA_PAD, 0, 0)], [w["mla_out_g_pad"]],
        [(MLA_PAD, BF16), (MLA_PAD, F32)], [(1, MLA_PAD)], tb // 2)
    tq = min(512, t)
    dq = _attn_bwd_dq(sv["q"], sv["k"], sv["v"], do, sv["lse"], delta, tq)
    dk, dv, *carried_out = _attn_bwd_dkv(sv["q"], sv["k"], sv["v"], do, sv["lse"], delta, tq, carried)

    def rope_b(dqb, dkb, dvb, cc, a1, a2):
        dqr = _heads_apply(dqb * _attn_scale(), lambda xh: _rope(xh, cc, -a1, -a2))
        ksum = dkb[:, :HEAD_LANES]
        for h in range(1, N_HEADS):
            ksum = ksum + dkb[:, h * HEAD_LANES:(h + 1) * HEAD_LANES]
        lane = lax.broadcasted_iota(jnp.int32, ksum.shape, 1)
        keep = (lane >= MLA_NOPE) & (lane < MLA_NOPE + MLA_ROPE)
        dkpe = jnp.where(keep, _rope(ksum, cc, -a1, -a2), 0.0)
        return dqr, jnp.concatenate([dkb, dvb], axis=1), dkpe

    dqraw, dkv, dkpe = _rowwise("mla_rope_b", rope_b, [dq, dk, dv, c, s1, s2], [],
                                [(MLA_PAD, BF16), (2 * MLA_PAD, BF16), (HEAD_LANES, F32)], [], tb // 2)
    dnq = _mm("mla_uq_bx", dqraw, w["w_uq_pad"], "nt", F32)
    grads["w_uq_pad"] = _mm("mla_uq_bw", sv["nq"], dqraw, "tn", F32)
    dnkv = _mm("mla_ukv_bx", dkv, w["w_kv_pad"], "nt", F32)
    grads["w_kv_pad"] = _mm("mla_ukv_bw", sv["nkv"], dkv, "tn", F32)

    def mla_pre_b(p0, dnqb, dnkvb, dkpeb, gq, gkv):
        cq, ckv = p0[:, :MLA_Q_RANK], p0[:, MLA_Q_RANK:MLA_Q_RANK + MLA_KV_RANK]
        dcq, dgq = _rms_bwd(cq, _rms_stats(cq), gq, dnqb)
        dckv, dgkv = _rms_bwd(ckv, _rms_stats(ckv), gkv, dnkvb)
        return jnp.concatenate([dcq, dckv, dkpeb], axis=1), dgq, dgkv

    dproj, grads["mla_q_norm_g"], grads["mla_kv_norm_g"] = _rowwise(
        "mla_pre_b", mla_pre_b, [(proj, 512, PIN_MLA // 512, 0), dnq, dnkv, dkpe],
        [w["mla_q_norm_g"], w["mla_kv_norm_g"]], [(512, BF16)], [(1, MLA_Q_RANK), (1, MLA_KV_RANK)], tb,
        wide=(PIN_W, PIN_MLA // 512), carry=dproj)

    dqh, dkh, dvh, dgh, dbh = _gdn_bwd(sv["qh"], sv["kh"], sv["vh"], sv["gbh"], sv["bbh"], sv["sall"], sv["tall"], doh)
    gones = _group_ones()

    def gdn_pre_b(xq, xk, xv, pq, pk, pv, dq_, dk_, dv_, cw, go, has_prev):
        outs = []
        for j, (xc, xp, dd) in enumerate(((xq, pq, dq_), (xk, pk, dk_), (xv, pv, dv_))):
            cc = _conv_taps(xc, xp, cw[:, j * GDN_W:(j + 1) * GDN_W], has_prev)
            sg = _sigmoid(cc)
            a = cc * sg
            if j < 2:
                rn = lax.rsqrt(_dotf(a * a, go) + EPS)
                if j == 0:
                    dd = dd * (GDN_DH ** -0.5)
                da = rn * dd - a * (rn * rn * rn) * _dotf(dd * a, go)
            else:
                da = dd
            outs.append(da * (sg * (1.0 + cc * (1.0 - sg))))
        return tuple(outs)

    dcq, dck, dcv = _gdn_pre_b_call("gdn_pre_b", gdn_pre_b, proj, (dqh, dkh, dvh), w["conv_w"], gones, tb, steps)
    dproj, grads["conv_w"] = _conv_bwd_call("gdn_conv_b", proj, (dcq, dck, dcv), w["conv_w"], dproj, tb, steps)

    def gate_b(ab_ref, g_ref, b_ref, dgh_ref, dbh_ref, al_ref, dt_ref, carry_ref, dab_ref, dal_ref, ddt_ref):
        @pl.when(pl.program_id(0) == 0)
        def _():
            dal_ref[...] = jnp.zeros(dal_ref.shape, F32)
            ddt_ref[...] = jnp.zeros(ddt_ref.shape, F32)

        ab, g128, b128 = ab_ref[...], g_ref[...], b_ref[...]
        lane = lax.broadcasted_iota(jnp.int32, ab.shape, 1)
        dg_ = jnp.zeros(ab.shape, F32)
        db_ = jnp.zeros(ab.shape, F32)
        for h in range(N_HEADS):
            dg_ = dg_ + jnp.where(lane == h, jnp.broadcast_to(dgh_ref[h][:, 0:1], ab.shape), 0.0)
            db_ = db_ + jnp.where(lane == N_HEADS + h, jnp.broadcast_to(dbh_ref[h][:, 0:1], ab.shape), 0.0)
        dg_ = _dotf(_chunk_sum_matrix(tb, True), dg_)
        slope = -jnp.exp(al_ref[...]) * _sigmoid(ab + dt_ref[...])
        dab_ref[...] = (dg_ * slope + db_ * b128 * (1.0 - b128)).astype(dab_ref.dtype)
        dal_ref[...] += jnp.sum(dg_ * g128, axis=0, keepdims=True)
        ddt_ref[...] += jnp.sum(dg_ * slope, axis=0, keepdims=True)

    lanes_spec = pl.BlockSpec((tb, HEAD_LANES), lambda i: (i, 0))
    ab_spec = pl.BlockSpec((tb, HEAD_LANES), lambda i: (i, PIN_AB // HEAD_LANES))
    dproj, grads["a_log_pad"], grads["dt_bias_pad"] = pl.pallas_call(
        gate_b, name="gdn_gate_b", grid=(steps,),
        in_specs=[ab_spec, lanes_spec, lanes_spec, _heads_spec(tb), _heads_spec(tb), vec_spec(HEAD_LANES),
                  vec_spec(HEAD_LANES), ANY_SPEC],
        out_specs=[ab_spec, vec_spec(HEAD_LANES), vec_spec(HEAD_LANES)],
        out_shape=[jax.ShapeDtypeStruct((t, PIN_W), BF16), jax.ShapeDtypeStruct((1, HEAD_LANES), F32),
                   jax.ShapeDtypeStruct((1, HEAD_LANES), F32)],
        input_output_aliases={7: 0},
        compiler_params=_params(("arbitrary",)))(proj, sv["g128"], sv["b128"], dgh, dbh, w["a_log_pad"],
                                                 w["dt_bias_pad"], dproj)
    dhn = _mm("mix_in_bx", dproj, w["w_in_pad"], "nt", F32)
    grads["w_in_pad"] = _mm("mix_in_bw", sv["hn"], dproj, "tn", F32)

    def pre_b(xb, dnb, dyb, g):
        dx, dg = _rms_bwd(xb, _rms_stats(xb), g, dnb)
        return dyb + dx, dg

    dx, grads["mix_pre_g"] = _rowwise("mix_pre_b", pre_b, [x, dhn, dy], [w["mix_pre_g"]], [(d, F32)], [(1, d)], tb)
    return dx, grads, carried_out


def _gdn_pre_b_call(name, fn, proj, dd, conv_w, gones, tb, steps):
    t = proj.shape[0]

    def body(xq, xk, xv, pq, pk, pv, d0, d1, d2, cw, go, oq, ok, ov):
        has_prev = jnp.where(pl.program_id(0) == 0, 0.0, 1.0)
        dd_rows = [jnp.concatenate([dr[h] for h in range(N_HEADS)], axis=1) for dr in (d0, d1, d2)]
        outs = fn(xq[...], xk[...], xv[...], pq[...], pk[...], pv[...], *dd_rows, cw[...], go[...], has_prev)
        for r, val in zip((oq, ok, ov), outs):
            r[...] = val

    return pl.pallas_call(
        body, name=name, grid=(steps,),
        in_specs=_qkv_specs(tb) + [_heads_spec(tb)] * 3 + [pl.BlockSpec(conv_w.shape, lambda i: (0, 0)),
                                                          pl.BlockSpec(gones.shape, lambda i: (0, 0))],
        out_specs=[pl.BlockSpec((tb, GDN_W), lambda i: (i, 0))] * 3,
        out_shape=[jax.ShapeDtypeStruct((t, GDN_W), F32)] * 3,
        compiler_params=_params(("arbitrary",)))(proj, proj, proj, proj, proj, proj, *dd, conv_w, gones)


def _conv_bwd_call(name, proj, dc, conv_w, dproj, tb, steps):
    t = proj.shape[0]
    dcur = [pl.BlockSpec((tb, GDN_W), lambda i: (i, 0))] * 3
    dnext = [pl.BlockSpec((tb, GDN_W), lambda i: (jnp.minimum(i + 1, steps - 1), 0))] * 3

    def body(xq, xk, xv, pq, pk, pv, d0, d1, d2, n0, n1, n2, cw, carry_ref, dx_ref, dw_ref):
        i = pl.program_id(0)
        has_prev = jnp.where(i == 0, 0.0, 1.0)
        has_next = jnp.where(i == steps - 1, 0.0, 1.0)

        @pl.when(i == 0)
        def _():
            dw_ref[...] = jnp.zeros(dw_ref.shape, F32)

        wv = cw[...]
        dws, dxs = [], []
        for j, (xr, pr, dr, nr) in enumerate(((xq, pq, d0, n0), (xk, pk, d1, n1), (xv, pv, d2, n2))):
            x, xp, dcv, dnx = xr[...], pr[...], dr[...], nr[...]
            wj = wv[:, j * GDN_W:(j + 1) * GDN_W]
            row = lax.broadcasted_iota(jnp.int32, x.shape, 0)
            dx = dcv * wj[GDN_CONV - 1:GDN_CONV, :]
            rows_w = [jnp.sum(dcv * x, axis=0, keepdims=True)]
            for s in range(1, GDN_CONV):
                up = jnp.where(row < tb - s, _roll(dcv, tb - s, 0), _roll(dnx, tb - s, 0) * has_next)
                dx = dx + up * wj[GDN_CONV - 1 - s:GDN_CONV - s, :]
                sh = jnp.where(row >= s, _roll(x, s, 0), _roll(xp, s, 0) * has_prev)
                rows_w.append(jnp.sum(dcv * sh, axis=0, keepdims=True))
            dxs.append(dx)
            dws.append(jnp.concatenate(rows_w[::-1], axis=0))
        dx_ref[...] = jnp.concatenate(dxs, axis=1).astype(dx_ref.dtype)
        dw_ref[...] += jnp.concatenate(dws, axis=1)

    return pl.pallas_call(
        body, name=name, grid=(steps,),
        in_specs=_qkv_specs(tb) + dcur + dnext + [pl.BlockSpec(conv_w.shape, lambda i: (0, 0)), ANY_SPEC],
        out_specs=[pl.BlockSpec((tb, 3 * GDN_W), lambda i: (i, PIN_QKV // (3 * GDN_W))),
                   pl.BlockSpec(conv_w.shape, lambda i: (0, 0))],
        out_shape=[jax.ShapeDtypeStruct((t, PIN_W), BF16), jax.ShapeDtypeStruct(conv_w.shape, F32)],
        input_output_aliases={13: 0},
        compiler_params=_params(("arbitrary",)))(proj, proj, proj, proj, proj, proj, *dc, *dc, conv_w, dproj)


def _pad_heads_cols(wm, per_head):
    r = wm.shape[0]
    return jnp.pad(wm.reshape(r, N_HEADS, per_head), ((0, 0), (0, 0), (0, HEAD_LANES - per_head))).reshape(r, MLA_PAD)


def _unpad_heads_cols(wm, per_head):
    r = wm.shape[0]
    return wm.reshape(r, N_HEADS, HEAD_LANES)[:, :, :per_head].reshape(r, N_HEADS * per_head)


def _win_to_pad(wi):
    r = wi.shape[0]
    z = lambda n: jnp.zeros((r, n), wi.dtype)
    o = MLA_Q_RANK + MLA_KV_RANK
    kpe = wi[:, o:o + MLA_ROPE]
    o2 = o + MLA_ROPE
    qkv = wi[:, o2:o2 + 3 * GDN_W]
    o3 = o2 + 3 * GDN_W
    ab = wi[:, o3:o3 + 2 * N_HEADS]
    gate = wi[:, o3 + 2 * N_HEADS:]
    return jnp.concatenate([qkv, wi[:, :o], z(MLA_NOPE), kpe, z(HEAD_LANES - MLA_NOPE - MLA_ROPE), gate, ab,
                            z(HEAD_LANES - 2 * N_HEADS)], axis=1)


def _win_from_pad(wp):
    return jnp.concatenate([wp[:, PIN_MLA:PIN_KPE], wp[:, PIN_KPE + MLA_NOPE:PIN_KPE + MLA_NOPE + MLA_ROPE],
                            wp[:, PIN_QKV:PIN_QKV + 3 * GDN_W], wp[:, PIN_AB:PIN_AB + 2 * N_HEADS],
                            wp[:, PIN_GATE:PIN_GATE + GDN_W]], axis=1)


def _wkv_to_pad(wkv):
    r = wkv.shape[0]
    w3 = wkv.reshape(r, N_HEADS, MLA_NOPE + MLA_V)
    kpart = jnp.pad(w3[:, :, :MLA_NOPE], ((0, 0), (0, 0), (0, HEAD_LANES - MLA_NOPE))).reshape(r, MLA_PAD)
    vpart = jnp.pad(w3[:, :, MLA_NOPE:], ((0, 0), (0, 0), (0, HEAD_LANES - MLA_V))).reshape(r, MLA_PAD)
    return jnp.concatenate([kpart, vpart], axis=1)


def _wkv_from_pad(wp):
    r = wp.shape[0]
    kpart = wp[:, :MLA_PAD].reshape(r, N_HEADS, HEAD_LANES)[:, :, :MLA_NOPE]
    vpart = wp[:, MLA_PAD:].reshape(r, N_HEADS, HEAD_LANES)[:, :, :MLA_V]
    return jnp.concatenate([kpart, vpart], axis=2).reshape(r, N_HEADS * (MLA_NOPE + MLA_V))


def _wout_to_pad(wo):
    n = wo.shape[1]
    mla = jnp.pad(wo[:N_HEADS * MLA_V].reshape(N_HEADS, MLA_V, n), ((0, 0), (0, HEAD_LANES - MLA_V), (0, 0)))
    return jnp.concatenate([mla.reshape(MLA_PAD, n), wo[N_HEADS * MLA_V:]], axis=0)


def _wout_from_pad(wp):
    n = wp.shape[1]
    mla = wp[:MLA_PAD].reshape(N_HEADS, HEAD_LANES, n)[:, :MLA_V].reshape(N_HEADS * MLA_V, n)
    return jnp.concatenate([mla, wp[MLA_PAD:]], axis=0)


def _pad_lanes(v, n):
    return jnp.pad(v, ((0, 0), (0, n - v.shape[1])))


def _compute_weights(full):
    w = {}
    for n in FFN_BIG:
        if n in full:
            w[n] = full[n].astype(MM_DTYPE)
    w["w_in_pad"] = _win_to_pad(full["w_in"]).astype(MM_DTYPE)
    w["w_uq_pad"] = _pad_heads_cols(full["mla_w_uq"], MLA_NOPE + MLA_ROPE).astype(MM_DTYPE)
    w["w_kv_pad"] = _wkv_to_pad(full["mla_w_ukv"]).astype(MM_DTYPE)
    w["w_out_pad"] = _wout_to_pad(full["w_out"]).astype(MM_DTYPE)
    w["conv_w"] = full["gdn_conv_w"].astype(F32)
    for n in ("ffn1_pre_g", "ffn1_post_g", "mix_pre_g", "mla_q_norm_g", "mla_kv_norm_g", "gdn_norm_g", "mix_post_g",
              "ffn2_pre_g", "ffn2_post_g"):
        w[n] = full[n]
    w["mla_out_g_pad"] = _pad_heads_cols(full["mla_out_g"], MLA_V)
    w["a_log_pad"] = _pad_lanes(full["gdn_a_log"], HEAD_LANES)
    w["dt_bias_pad"] = _pad_lanes(full["gdn_dt_bias"], HEAD_LANES)
    return w


FFN2_BIG = FFN_BIG[3:]


def _local_step(x, positions, loss_target, full, late=None):
    t, d = x.shape
    tb = min(512, t)
    tm = min(1024, t)
    w = _compute_weights(full)
    ffn = lambda tag: (w[tag + "_pre_g"], w[tag + "_w_gate"], w[tag + "_w_up"], w[tag + "_w_down"], w[tag + "_post_g"])
    x1, sv1 = _ffn_fwd("ffn1", x, *ffn("ffn1"), tm)
    x2, svm, gathered = _mixer_fwd(x1, positions, w, tb, _carried_gather(late[0]) if late else None)
    for n, gw in zip(FFN2_BIG, gathered):
        w[n] = gw
    x3, sv2 = _ffn_fwd("ffn2", x2, *ffn("ffn2"), tm)

    def loss_f(yb, tg):
        e = yb - tg
        return e * (1.0 / d), jnp.sum(e * e, axis=0, keepdims=True)

    dy, lsum = _rowwise("loss", loss_f, [x3, loss_target], [], [(d, F32)], [(1, d)], tb)
    g = {}
    dx2, g["ffn2_pre_g"], g["ffn2_w_gate"], g["ffn2_w_up"], g["ffn2_w_down"], g["ffn2_post_g"] = _ffn_bwd(
        "ffn2", dy, sv2, *ffn("ffn2"), tm, tm)
    if late:
        _, core, chip = late
        mine = [g[n] for n in FFN2_BIG]
        got = _swap_halves(mine, "late")
        pairs = [_add_pair("add_pair_late_%d" % i, gi, gt, core) for i, (gi, gt) in enumerate(zip(mine, got))]
        dx1, gm, slabs = _mixer_bwd(dx2, svm, w, tb, _carried_scatter(pairs))
        for i, n in enumerate(FFN2_BIG):
            g[n] = _add_chips("add_chips_late_%d" % i, pairs[i], slabs[i], chip)
    else:
        dx1, gm, _ = _mixer_bwd(dx2, svm, w, tb)
    dx0, g["ffn1_pre_g"], g["ffn1_w_gate"], g["ffn1_w_up"], g["ffn1_w_down"], g["ffn1_post_g"] = _ffn_bwd(
        "ffn1", dx1, sv1, *ffn("ffn1"), tm, tm)
    g["mix_pre_g"], g["mix_post_g"] = gm["mix_pre_g"], gm["mix_post_g"]
    g["mla_q_norm_g"], g["mla_kv_norm_g"] = gm["mla_q_norm_g"], gm["mla_kv_norm_g"]
    g["gdn_norm_g"] = gm["gdn_norm_g"]
    g["w_in"] = _win_from_pad(gm["w_in_pad"])
    g["mla_w_uq"] = _unpad_heads_cols(gm["w_uq_pad"], MLA_NOPE + MLA_ROPE)
    g["mla_w_ukv"] = _wkv_from_pad(gm["w_kv_pad"])
    g["mla_out_g"] = _unpad_heads_cols(gm["mla_out_g_pad"], MLA_V)
    g["gdn_conv_w"] = gm["conv_w"]
    g["gdn_a_log"] = gm["a_log_pad"][:, :N_HEADS]
    g["gdn_dt_bias"] = gm["dt_bias_pad"][:, :N_HEADS]
    g["w_out"] = _wout_from_pad(gm["w_out_pad"])
    return lsum, dx0, g


HBM_SPEC = pl.BlockSpec(memory_space=pltpu.HBM)


def _place():
    return lax.axis_index("x"), lax.axis_index("y"), lax.axis_index("c")


def _exchange_call(name, body, ins, out_shapes, n_remote, n_local):
    return pl.pallas_call(
        body, name=name, in_specs=[HBM_SPEC] * len(ins), out_specs=[HBM_SPEC] * len(out_shapes), out_shape=out_shapes,
        scratch_shapes=[pltpu.SemaphoreType.DMA((n_remote,)), pltpu.SemaphoreType.DMA((n_remote,)),
                        pltpu.SemaphoreType.DMA((n_local,))])(*ins)


def _other_chips(x, y):
    return [(1 - x, y), (x, 1 - y), (1 - x, 1 - y)]


def _at_each_chip(fn):
    x, y, _ = _place()
    for cx in range(2):
        for cy in range(2):
            pl.when((x == cx) & (y == cy))(functools.partial(fn, cx, cy))


def _at_each_device(fn):
    x, y, c = _place()
    for cx in range(2):
        for cy in range(2):
            for cc in range(2):
                pl.when((x == cx) & (y == cy) & (c == cc))(functools.partial(fn, cx, cy, cc))


def _at_each_core(fn):
    c = lax.axis_index("c")
    for cc in range(2):
        pl.when(c == cc)(functools.partial(fn, cc))


def _gather_shards(ws):
    nw = len(ws)

    def body(*refs):
        w_refs, out_refs = refs[:nw], refs[nw:2 * nw]
        send_sems, recv_sems, local_sems = refs[2 * nw:]

        def run(x, y, c):
            chips = _other_chips(x, y)
            me, sibling = 2 * x + y, (x, y, 1 - c)

            def half(ref, which):
                hr = ref.shape[0] // 2
                return ref.at[pl.ds(which * hr, hr)]

            def over_ici(i, j, src, slab, to):
                return pltpu.make_async_remote_copy(
                    src_ref=half(src, c), dst_ref=half(out_refs[i].at[slab], c), send_sem=send_sems.at[7 * i + j],
                    recv_sem=recv_sems.at[7 * i + j], device_id=to, device_id_type=MESH)

            def over_d2d(i, j, slab, which):
                return pltpu.make_async_remote_copy(
                    src_ref=half(out_refs[i].at[slab], which), dst_ref=half(out_refs[i].at[slab], which),
                    send_sem=send_sems.at[7 * i + 3 + j], recv_sem=recv_sems.at[7 * i + 3 + j], device_id=sibling,
                    device_id_type=MESH)

            def own(i, w_ref):
                return pltpu.make_async_remote_copy(
                    src_ref=w_ref, dst_ref=out_refs[i].at[me], send_sem=send_sems.at[7 * i + 6],
                    recv_sem=recv_sems.at[7 * i + 6], device_id=sibling, device_id_type=MESH)

            sends, passed = [], []
            for i, w_ref in enumerate(w_refs):
                for j, (px, py) in enumerate(chips):
                    sends.append(over_ici(i, j, w_ref, me, (px, py, c)))
                    sends[-1].start()
            for i, w_ref in enumerate(w_refs):
                sends.append(own(i, w_ref))
                sends[-1].start()
            for i, w_ref in enumerate(w_refs):
                for j, (px, py) in enumerate(chips):
                    over_ici(i, j, w_ref, 2 * px + py, (px, py, c)).wait_recv()
                    passed.append(over_d2d(i, j, 2 * px + py, c))
                    passed[-1].start()
            for i, w_ref in enumerate(w_refs):
                own(i, w_ref).wait_recv()
                for j, (px, py) in enumerate(chips):
                    over_d2d(i, j, 2 * px + py, 1 - c).wait_recv()
            for cp in sends + passed:
                cp.wait_send()

        _at_each_device(run)

    outs = [jax.ShapeDtypeStruct((N_SHARD,) + w.shape, w.dtype) for w in ws]
    return _exchange_call("gather_weight_shards", body, ws, outs, 7 * nw, 1)


def _swap_halves(gs, tag=""):
    ng = len(gs)

    def body(*refs):
        g_refs, got_refs = refs[:ng], refs[ng:2 * ng]
        send_sems, recv_sems, _ = refs[2 * ng:]
        x, y, _ = _place()

        def run(c):
            sends = []
            for i, (g_ref, got_ref) in enumerate(zip(g_refs, got_refs)):
                hr = got_ref.shape[1]
                sends.append(pltpu.make_async_remote_copy(
                    src_ref=g_ref.at[:, pl.ds((1 - c) * hr, hr)], dst_ref=got_ref, send_sem=send_sems.at[i],
                    recv_sem=recv_sems.at[i], device_id=(x, y, 1 - c), device_id_type=MESH))
                sends[-1].start()
            for cp in sends:
                cp.wait()

        _at_each_core(run)

    halves = [jax.ShapeDtypeStruct((g.shape[0], g.shape[1] // 2, g.shape[2]), g.dtype) for g in gs]
    return _exchange_call("swap_grad_halves" + tag, body, gs, halves, ng, 1)


def _scatter_copies(p_refs, out_refs, send_sems, recv_sems, x, y):
    c = lax.axis_index("c")
    copies = []
    for i, (p_ref, out_ref) in enumerate(zip(p_refs, out_refs)):
        for j, (px, py) in enumerate(_other_chips(x, y)):
            copies.append(pltpu.make_async_remote_copy(
                src_ref=p_ref.at[2 * px + py], dst_ref=out_ref.at[j], send_sem=send_sems.at[3 * i + j],
                recv_sem=recv_sems.at[3 * i + j], device_id=(px, py, c), device_id_type=MESH))
    return copies


def _start_all(make, *refs):
    def run(x, y):
        for cp in make(*refs, x, y):
            cp.start()

    _at_each_chip(run)


def _wait_all(make, *refs):
    def run(x, y):
        copies = make(*refs, x, y)
        for cp in copies:
            cp.wait_recv()
        for cp in copies:
            cp.wait_send()

    _at_each_chip(run)


def _scatter_shapes(ps):
    return [jax.ShapeDtypeStruct((3,) + p.shape[1:], p.dtype) for p in ps]


def _scatter_to_chips(ps):
    n = len(ps)

    def body(*refs):
        p_refs, out_refs = refs[:n], refs[n:2 * n]
        send_sems, recv_sems, _ = refs[2 * n:]
        _start_all(_scatter_copies, p_refs, out_refs, send_sems, recv_sems)
        _wait_all(_scatter_copies, p_refs, out_refs, send_sems, recv_sems)

    return _exchange_call("scatter_grad_quarters", body, ps, _scatter_shapes(ps), 3 * n, 1)


def _carried_scatter(ps):
    return _Carried(ps, _scatter_shapes(ps), 3 * len(ps), functools.partial(_start_all, _scatter_copies),
                    functools.partial(_wait_all, _scatter_copies))


def _direct_gather_copies(w_refs, out_refs, send_sems, recv_sems, x, y, arriving):
    c = lax.axis_index("c")
    me = 2 * x + y
    peers = [((px, py, c), 2 * px + py) for px, py in _other_chips(x, y)] + [((x, y, 1 - c), me)]
    copies = []
    for i, (w_ref, out_ref) in enumerate(zip(w_refs, out_refs)):
        for j, (peer, slab) in enumerate(peers):
            copies.append(pltpu.make_async_remote_copy(
                src_ref=w_ref, dst_ref=out_ref.at[slab if arriving else me], send_sem=send_sems.at[4 * i + j],
                recv_sem=recv_sems.at[4 * i + j], device_id=peer, device_id_type=MESH))
    return copies


def _carried_gather(ws):
    def start(w_refs, out_refs, send_sems, recv_sems):
        def run(x, y):
            for cp in _direct_gather_copies(w_refs, out_refs, send_sems, recv_sems, x, y, False):
                cp.start()

        _at_each_chip(run)

    def finish(w_refs, out_refs, send_sems, recv_sems):
        def run(x, y):
            for cp in _direct_gather_copies(w_refs, out_refs, send_sems, recv_sems, x, y, True):
                cp.wait_recv()
            for cp in _direct_gather_copies(w_refs, out_refs, send_sems, recv_sems, x, y, False):
                cp.wait_send()

        _at_each_chip(run)

    outs = [jax.ShapeDtypeStruct((N_SHARD,) + w.shape, w.dtype) for w in ws]
    return _Carried(ws, outs, 4 * len(ws), start, finish)


def _share_halves(hs):
    n = len(hs)

    def body(*refs):
        h_refs, out_refs = refs[:n], refs[n:2 * n]
        send_sems, recv_sems, _ = refs[2 * n:]
        x, y, c = _place()
        sends = []
        for i, (h_ref, out_ref) in enumerate(zip(h_refs, out_refs)):
            sends.append(pltpu.make_async_remote_copy(
                src_ref=h_ref, dst_ref=out_ref, send_sem=send_sems.at[i], recv_sem=recv_sems.at[i],
                device_id=(x, y, 1 - c), device_id_type=MESH))
            sends[-1].start()
        for cp in sends:
            cp.wait()

    outs = [jax.ShapeDtypeStruct(h.shape, h.dtype) for h in hs]
    return _exchange_call("share_grad_halves", body, hs, outs, n, 1)


def _scalar_grid_call(name, body, scalars, grid, in_specs, out_specs, out_shape, args):
    grid_spec = pltpu.PrefetchScalarGridSpec(num_scalar_prefetch=len(scalars), grid=grid, in_specs=in_specs,
                                             out_specs=out_specs)
    return pl.pallas_call(body, name=name, grid_spec=grid_spec, out_shape=out_shape,
                          compiler_params=_params(("arbitrary",) * len(grid)))(*scalars, *args)


def _add_pair(name, g, got, core):
    ns_, hr, cols = got.shape
    th = _row_tile(hr, 512)
    nb = hr // th

    def body(core_ref, g_ref, got_ref, out_ref):
        out_ref[...] = (g_ref[...].astype(F32) + got_ref[...].astype(F32)).astype(out_ref.dtype)

    blk = pl.BlockSpec((1, th, cols), lambda q, j, core_ref: (q, j, 0))
    own = pl.BlockSpec((1, th, cols), lambda q, j, core_ref: (q, core_ref[0] * nb + j, 0))
    return _scalar_grid_call(name, body, [core], (ns_, nb), [own, blk], blk,
                             jax.ShapeDtypeStruct(got.shape, got.dtype), [g, got])


def _add_chips(name, pairs, slabs, chip):
    _, hr, cols = slabs.shape
    th = _row_tile(hr, 512)

    def body(chip_ref, own_ref, s0_ref, s1_ref, s2_ref, out_ref):
        total = own_ref[0].astype(F32) + s0_ref[0].astype(F32)
        out_ref[...] = (total + s1_ref[0].astype(F32)) + s2_ref[0].astype(F32)

    own = pl.BlockSpec((1, th, cols), lambda j, chip_ref: (chip_ref[0], j, 0))
    others = [pl.BlockSpec((1, th, cols), lambda j, chip_ref, k=k: (k, j, 0)) for k in range(3)]
    return _scalar_grid_call(name, body, [chip], (hr // th,), [own] + others,
                             pl.BlockSpec((th, cols), lambda j, chip_ref: (j, 0)),
                             jax.ShapeDtypeStruct((hr, cols), F32), [pairs, slabs, slabs, slabs])


def _join_halves(name, mine, other, core):
    hr, cols = mine.shape
    th = _row_tile(hr, 512)
    nb = hr // th

    def body(core_ref, mine_ref, other_ref, out_ref):
        is_mine = pl.program_id(0) == core_ref[0]

        @pl.when(is_mine)
        def _():
            out_ref[...] = mine_ref[...]

        @pl.when(jnp.logical_not(is_mine))
        def _():
            out_ref[...] = other_ref[...]

    blk = pl.BlockSpec((th, cols), lambda h, j, core_ref: (j, 0))
    return _scalar_grid_call(name, body, [core], (2, nb), [blk, blk],
                             pl.BlockSpec((th, cols), lambda h, j, core_ref: (h * nb + j, 0)),
                             jax.ShapeDtypeStruct((2 * hr, cols), mine.dtype), [mine, other])


def _gather_small(sp):
    def body(s_ref, out_ref, send_sems, recv_sems, local_sem):
        x, y, c = _place()
        me = 4 * x + 2 * y + c
        peers = [(x ^ (m >> 2), y ^ ((m >> 1) & 1), c ^ (m & 1)) for m in range(1, 8)]
        mine = pltpu.make_async_copy(s_ref, out_ref.at[me], local_sem)
        mine.start()
        sends = [pltpu.make_async_remote_copy(src_ref=s_ref, dst_ref=out_ref.at[me], send_sem=send_sems.at[j],
                                              recv_sem=recv_sems.at[j], device_id=p, device_id_type=MESH)
                 for j, p in enumerate(peers)]
        for cp in sends:
            cp.start()
        for j, (px, py, pc) in enumerate(peers):
            pltpu.make_async_remote_copy(src_ref=s_ref, dst_ref=out_ref.at[4 * px + 2 * py + pc],
                                         send_sem=send_sems.at[j], recv_sem=recv_sems.at[j], device_id=(px, py, pc),
                                         device_id_type=MESH).wait_recv()
        for cp in sends:
            cp.wait_send()
        mine.wait()

    return pl.pallas_call(
        body, name="gather_small_grads", in_specs=[HBM_SPEC], out_specs=HBM_SPEC,
        out_shape=jax.ShapeDtypeStruct((8,) + sp.shape, sp.dtype),
        scratch_shapes=[pltpu.SemaphoreType.DMA((7,)), pltpu.SemaphoreType.DMA((7,)), pltpu.SemaphoreType.DMA])(sp)


def _pack_rows(total):
    rows = -(-total // LANES)
    return -(-rows // 32) * 32


def _pack(arrs, dtype):
    flat = jnp.concatenate([a.reshape(-1).astype(dtype) for a in arrs])
    rows = _pack_rows(flat.shape[0])
    return jnp.pad(flat, (0, rows * LANES - flat.shape[0])).reshape(rows, LANES)


def _unpack(buf, shapes):
    flat = buf.reshape(-1)
    out, off = {}, 0
    for n, shp in shapes:
        size = shp[0] * shp[1]
        out[n] = flat[off:off + size].reshape(shp)
        off += size
    return out


def _adamw(name, wv, g, m, v, tb):
    c1 = 1.0 - ADAM_B1 ** ADAM_STEP
    c2 = 1.0 - ADAM_B2 ** ADAM_STEP

    def fn(wb, gb, mb, vb):
        m2 = ADAM_B1 * mb + (1.0 - ADAM_B1) * gb
        v2 = ADAM_B2 * vb + (1.0 - ADAM_B2) * (gb * gb)
        delta = -ADAM_LR * ((m2 / c1) / (jnp.sqrt(v2 / c2) + ADAM_EPS) + ADAM_WD * wb)
        return delta, m2, v2

    cols = wv.shape[1]
    return _rowwise(name, fn, [wv, g, m, v], [], [(cols, F32)] * 3, [], tb)


def _row_tile(rows, pref):
    if rows <= pref:
        return rows
    t = pref
    while t >= 8:
        if rows % t == 0 and t % 8 == 0:
            return t
        t -= 8
    return rows


def kernel(x, positions, ffn1_pre_g, ffn1_w_gate, ffn1_w_up, ffn1_w_down, ffn1_post_g, mix_pre_g, w_in, mla_q_norm_g, mla_w_uq, mla_kv_norm_g, mla_w_ukv, mla_out_g, gdn_conv_w, gdn_a_log, gdn_dt_bias, gdn_norm_g, w_out, mix_post_g, ffn2_pre_g, ffn2_w_gate, ffn2_w_up, ffn2_w_down, ffn2_post_g, loss_target, m_ffn1_pre_g, m_ffn1_w_gate, m_ffn1_w_up, m_ffn1_w_down, m_ffn1_post_g, m_mix_pre_g, m_w_in, m_mla_q_norm_g, m_mla_w_uq, m_mla_kv_norm_g, m_mla_w_ukv, m_mla_out_g, m_gdn_conv_w, m_gdn_a_log, m_gdn_dt_bias, m_gdn_norm_g, m_w_out, m_mix_post_g, m_ffn2_pre_g, m_ffn2_w_gate, m_ffn2_w_up, m_ffn2_w_down, m_ffn2_post_g, v_ffn1_pre_g, v_ffn1_w_gate, v_ffn1_w_up, v_ffn1_w_down, v_ffn1_post_g, v_mix_pre_g, v_w_in, v_mla_q_norm_g, v_mla_w_uq, v_mla_kv_norm_g, v_mla_w_ukv, v_mla_out_g, v_gdn_conv_w, v_gdn_a_log, v_gdn_dt_bias, v_gdn_norm_g, v_w_out, v_mix_post_g, v_ffn2_pre_g, v_ffn2_w_gate, v_ffn2_w_up, v_ffn2_w_down, v_ffn2_post_g):
    args = dict(locals())
    wsh = {n: args[n][0] for n in WEIGHTS}
    msh = {n: args["m_" + n][0] if args["m_" + n].ndim == 3 else args["m_" + n] for n in WEIGHTS}
    vsh = {n: args["v_" + n][0] if args["v_" + n].ndim == 3 else args["v_" + n] for n in WEIGHTS}
    for n in SMALL:
        wsh[n] = args[n]
    mix_shapes = [(n, wsh[n].shape) for n in MIX_BIG]

    early = FFN_BIG[:3]
    gathered = _gather_shards([wsh[n].astype(MM_DTYPE) for n in early] + [_pack([wsh[n] for n in MIX_BIG], MM_DTYPE)])
    full = {n: wsh[n] for n in SMALL}
    for n, gw in zip(early, gathered):
        full[n] = gw
    parts = [_unpack(gathered[-1][q], mix_shapes) for q in range(N_SHARD)]
    for n in MIX_BIG:
        full[n] = jnp.concatenate([parts[q][n] for q in range(N_SHARD)], axis=SHARD_AXIS[n])

    core = lax.axis_index("c").astype(jnp.int32).reshape(1)
    chip = (2 * lax.axis_index("x") + lax.axis_index("y")).astype(jnp.int32).reshape(1)
    late = ([wsh[n].astype(MM_DTYPE) for n in FFN2_BIG], core, chip)
    lsum, grad_x, g = _local_step(x[0], positions, loss_target[0], full, late)
    loss = lax.psum(0.5 * jnp.sum(lsum) / x.shape[-1], ("x", "y", "c"))

    mix_quarters = [_pack([jnp.split(g[n], N_SHARD, axis=SHARD_AXIS[n])[q] for n in MIX_BIG], MM_DTYPE)
                    for q in range(N_SHARD)]
    mine = [g[n] for n in early] + [jnp.stack(mix_quarters)]
    got = _swap_halves(mine)
    pairs = [_add_pair("add_pair_%d" % i, gi, gt, core) for i, (gi, gt) in enumerate(zip(mine, got))]
    slabs = _scatter_to_chips(pairs)
    halves = [_add_chips("add_chips_%d" % i, pr, sl, chip) for i, (pr, sl) in enumerate(zip(pairs, slabs))]
    halves = halves[:3] + [g[n] for n in FFN2_BIG] + halves[3:]
    others = _share_halves(halves)
    shared = [_join_halves("join_halves_%d" % i, hm, ho, core) for i, (hm, ho) in enumerate(zip(halves, others))]
    gsh = _unpack(shared[-1], mix_shapes)
    for n, sg_ in zip(FFN_BIG, shared):
        gsh[n] = sg_

    small_shapes = [(n, wsh[n].shape) for n in SMALL]
    pack_small = lambda d: jnp.concatenate(
        [_pad_lanes(d[n].astype(F32), LANES) for n in SMALL] + [jnp.zeros((SMALL_ROWS - len(SMALL), LANES), F32)], axis=0)
    slots = _gather_small(pack_small(g))

    c1 = 1.0 - ADAM_B1 ** ADAM_STEP
    c2 = 1.0 - ADAM_B2 ** ADAM_STEP

    def small_update(wb, mb, vb, s8):
        gs = s8[0:SMALL_ROWS]
        for d in range(1, 8):
            gs = gs + s8[d * SMALL_ROWS:(d + 1) * SMALL_ROWS]
        m2 = ADAM_B1 * mb + (1.0 - ADAM_B1) * gs
        v2 = ADAM_B2 * vb + (1.0 - ADAM_B2) * (gs * gs)
        delta = -ADAM_LR * ((m2 / c1) / (jnp.sqrt(v2 / c2) + ADAM_EPS) + ADAM_WD * wb)
        return gs, delta, m2, v2

    sg, sd, sm, sv_ = _rowwise("adamw_small", small_update,
                               [pack_small(wsh), pack_small(msh), pack_small(vsh)],
                               [slots.reshape(8 * SMALL_ROWS, LANES)], [(LANES, F32)] * 4, [], SMALL_ROWS)
    grads, deltas, new_m, new_v = {}, {}, {}, {}
    for i, (n, shp) in enumerate(small_shapes):
        grads[n], deltas[n] = sg[i:i + 1, :shp[1]], sd[i:i + 1, :shp[1]]
        new_m[n], new_v[n] = sm[i:i + 1, :shp[1]], sv_[i:i + 1, :shp[1]]
    for n in BIG:
        grads[n] = gsh[n]
        r = wsh[n].shape[0]
        deltas[n], new_m[n], new_v[n] = _adamw("adamw_" + n, wsh[n], gsh[n], msh[n], vsh[n], _row_tile(r, 256))

    def shaped(d, n):
        return d[n][None] if n in BIG else d[n]

    return (loss, grad_x[None], *[shaped(grads, n) for n in WEIGHTS], *[shaped(deltas, n) for n in WEIGHTS],
            *[shaped(new_m, n) for n in WEIGHTS], *[shaped(new_v, n) for n in WEIGHTS])
```

```python
import functools

import jax
import jax.numpy as jnp
from jax import lax
from jax.experimental import pallas as pl
from jax.experimental.pallas import tpu as pltpu

F32 = jnp.float32
BF16 = jnp.bfloat16
MM_DTYPE = BF16
HI = lax.Precision.HIGHEST
MESH = pl.DeviceIdType.MESH

D_MODEL = 1024
D_FF = 2816
N_HEADS = 8
MLA_Q_RANK = 256
MLA_KV_RANK = 128
MLA_NOPE = 64
MLA_ROPE = 32
MLA_V = 64
ROPE_THETA = 10000.0
GDN_DH = 64
GDN_W = N_HEADS * GDN_DH
GDN_CONV = 4
CHUNK = 64
HEAD_LANES = 128
HEADS_PER_STEP = 2
MLA_PAD = N_HEADS * HEAD_LANES
EPS = 1e-6
N_SHARD = 4
LANES = 1024

PIN_QKV = 0
PIN_MLA = 1536
PIN_KPE = 1920
PIN_GATE = 2048
PIN_AB = 2560
PIN_W = 2688
CAT_W = MLA_PAD + GDN_W

ADAM_LR = 0.001
ADAM_B1 = 0.9
ADAM_B2 = 0.999
ADAM_EPS = 1e-08
ADAM_WD = 0.01
ADAM_STEP = 10

VMEM_LIMIT_V7X = 56 * 1024 * 1024

BIG = ["ffn1_w_gate", "ffn1_w_up", "ffn1_w_down", "w_in", "mla_w_uq", "mla_w_ukv", "gdn_conv_w", "w_out",
       "ffn2_w_gate", "ffn2_w_up", "ffn2_w_down"]
FFN_BIG = ["ffn1_w_gate", "ffn1_w_up", "ffn1_w_down", "ffn2_w_gate", "ffn2_w_up", "ffn2_w_down"]
MIX_BIG = ["w_in", "mla_w_uq", "mla_w_ukv", "gdn_conv_w", "w_out"]
SMALL = ["ffn1_pre_g", "ffn1_post_g", "mix_pre_g", "mla_q_norm_g", "mla_kv_norm_g", "mla_out_g", "gdn_a_log",
         "gdn_dt_bias", "gdn_norm_g", "mix_post_g", "ffn2_pre_g", "ffn2_post_g"]
WEIGHTS = ["ffn1_pre_g", "ffn1_w_gate", "ffn1_w_up", "ffn1_w_down", "ffn1_post_g", "mix_pre_g", "w_in",
           "mla_q_norm_g", "mla_w_uq", "mla_kv_norm_g", "mla_w_ukv", "mla_out_g", "gdn_conv_w", "gdn_a_log",
           "gdn_dt_bias", "gdn_norm_g", "w_out", "mix_post_g", "ffn2_pre_g", "ffn2_w_gate", "ffn2_w_up",
           "ffn2_w_down", "ffn2_post_g"]
SHARD_AXIS = {"ffn1_w_gate": 1, "ffn1_w_up": 1, "ffn1_w_down": 0, "w_in": 1, "mla_w_uq": 1, "mla_w_ukv": 1,
              "gdn_conv_w": 1, "w_out": 0, "ffn2_w_gate": 1, "ffn2_w_up": 1, "ffn2_w_down": 0}
SMALL_ROWS = 16


def _params(sem):
    return pltpu.CompilerParams(dimension_semantics=sem, vmem_limit_bytes=VMEM_LIMIT_V7X)


def _pick(dim, pref):
    if dim <= pref:
        return dim
    t = (pref // 128) * 128
    while t >= 128:
        if dim % t == 0:
            return t
        t -= 128
    return dim


ANY_SPEC = pl.BlockSpec(memory_space=pl.ANY)


def _rowwise(name, fn, row_ins, bc_ins, row_outs, acc_outs, tb, wide=None, carry=None):
    ents = []
    for e in row_ins:
        ents.append(e if isinstance(e, tuple) else (e, e.shape[1], 0, 0))
    over = [o[2] for o in row_outs if len(o) == 3]
    rows = over[0] if over else ents[0][0].shape[0]
    steps = rows // tb
    assert steps * tb == rows, (name, rows, tb)
    in_specs, args = [], []
    for a, w, j, r0 in ents:
        in_specs.append(pl.BlockSpec((tb, w), lambda i, j=j, r0=r0: (i + r0, j)))
        args.append(a)
    for b in bc_ins:
        in_specs.append(pl.BlockSpec(b.shape, lambda i: (0, 0)))
        args.append(b)
    n_in = len(args)
    aliases = {}
    if carry is not None:
        in_specs.append(ANY_SPEC)
        args.append(carry)
        aliases = {n_in: 0}
    out_shape = [jax.ShapeDtypeStruct((rows, o[0]), o[1]) for o in row_outs]
    out_specs = [pl.BlockSpec((tb, o[0]), lambda i: (i, 0)) for o in row_outs]
    if wide is not None:
        out_shape[0] = jax.ShapeDtypeStruct((rows, wide[0]), row_outs[0][1])
        out_specs[0] = pl.BlockSpec((tb, row_outs[0][0]), lambda i: (i, wide[1]))
    out_shape += [jax.ShapeDtypeStruct((r, c), F32) for r, c in acc_outs]
    out_specs += [pl.BlockSpec((r, c), lambda i: (0, 0)) for r, c in acc_outs]
    n_ro, n_acc, n_args = len(row_outs), len(acc_outs), len(args)

    def body(*refs):
        vals = fn(*[r[...] for r in refs[:n_in]])
        if not isinstance(vals, (tuple, list)):
            vals = (vals,)
        for r, v in zip(refs[n_args:n_args + n_ro], vals[:n_ro]):
            r[...] = v.astype(r.dtype)
        if n_acc:
            acc_refs = refs[n_args + n_ro:]

            @pl.when(pl.program_id(0) == 0)
            def _():
                for r in acc_refs:
                    r[...] = jnp.zeros(r.shape, r.dtype)

            for r, v in zip(acc_refs, vals[n_ro:]):
                r[...] += v

    outs = pl.pallas_call(body, name=name, grid=(steps,), in_specs=in_specs, out_specs=out_specs,
                          out_shape=out_shape, input_output_aliases=aliases,
                          compiler_params=_params(("arbitrary",)))(*args)
    return outs


def _mm(name, a, b, mode, out_dtype, tm=512, tn=512, tk=1024):
    if mode == "nn":
        (m, k), (k2, n) = a.shape, b.shape
    elif mode == "nt":
        (m, k), (n, k2) = a.shape, b.shape
    else:
        (k, m), (k2, n) = a.shape, b.shape
    assert k == k2, (name, a.shape, b.shape)
    tm, tn, tk = _pick(m, tm), _pick(n, tn), _pick(k, tk)
    nk = k // tk
    if mode == "nn":
        a_spec = pl.BlockSpec((tm, tk), lambda i, j, kk: (i, kk))
        b_spec = pl.BlockSpec((tk, tn), lambda i, j, kk: (kk, j))
        dims = (((1,), (0,)), ((), ()))
    elif mode == "nt":
        a_spec = pl.BlockSpec((tm, tk), lambda i, j, kk: (i, kk))
        b_spec = pl.BlockSpec((tn, tk), lambda i, j, kk: (j, kk))
        dims = (((1,), (1,)), ((), ()))
    else:
        a_spec = pl.BlockSpec((tk, tm), lambda i, j, kk: (kk, i))
        b_spec = pl.BlockSpec((tk, tn), lambda i, j, kk: (kk, j))
        dims = (((0,), (0,)), ((), ()))

    def body(a_ref, b_ref, o_ref, acc_ref):
        kk = pl.program_id(2)

        @pl.when(kk == 0)
        def _():
            acc_ref[...] = jnp.zeros(acc_ref.shape, F32)

        acc_ref[...] += lax.dot_general(a_ref[...].astype(MM_DTYPE), b_ref[...].astype(MM_DTYPE), dims,
                                        preferred_element_type=F32)

        @pl.when(kk == nk - 1)
        def _():
            o_ref[...] = acc_ref[...].astype(o_ref.dtype)

    return pl.pallas_call(
        body, name=name, grid=(m // tm, n // tn, nk), in_specs=[a_spec, b_spec],
        out_specs=pl.BlockSpec((tm, tn), lambda i, j, kk: (i, j)),
        out_shape=jax.ShapeDtypeStruct((m, n), out_dtype),
        scratch_shapes=[pltpu.VMEM((tm, tn), F32)],
        compiler_params=_params(("parallel", "parallel", "arbitrary")))(a, b)


def _rms_stats(x, n_real=None):
    n = x.shape[-1] if n_real is None else n_real
    return lax.rsqrt(jnp.sum(x * x, axis=-1, keepdims=True) / n + EPS)


def _rms_bwd(x, r, g, dz, n_real=None):
    n = x.shape[-1] if n_real is None else n_real
    xh = x * r
    dxh = dz * g
    dx = r * (dxh - xh * (jnp.sum(dxh * xh, axis=-1, keepdims=True) / n))
    return dx, jnp.sum(dz * xh, axis=0, keepdims=True)


def _sigmoid(x):
    return 0.5 * jnp.tanh(0.5 * x) + 0.5


def _roll(x, s, axis):
    return pltpu.roll(x, s, axis)


def _rope(x, c, s1, s2):
    return x * c + _roll(x, HEAD_LANES - MLA_ROPE // 2, 1) * s1 + _roll(x, MLA_ROPE // 2, 1) * s2


def _heads_apply(x, fn):
    return jnp.concatenate([fn(x[:, h * HEAD_LANES:(h + 1) * HEAD_LANES]) for h in range(N_HEADS)], axis=1)


ROW_CHUNK = 256


def _row_chunks(rows):
    step = min(ROW_CHUNK, rows)
    return [pl.ds(r, step) for r in range(0, rows, step)]


def _ffn_fwd(tag, x, g_pre, wg, wu, wd, g_post, tm):
    t, d = x.shape
    ns, _, fs = wg.shape
    nt = t // tm
    row = pl.BlockSpec((tm, d), lambda i, q: (i, 0))
    vec = pl.BlockSpec((1, d), lambda i, q: (0, 0))
    act3 = pl.BlockSpec((1, tm, fs), lambda i, q: (q, i, 0))
    wcol = pl.BlockSpec((1, d, fs), lambda i, q: (q, 0, 0))
    wrow = pl.BlockSpec((1, fs, d), lambda i, q: (q, 0, 0))

    def gate_up(x_ref, g_ref, wg_ref, wu_ref, n_ref, a_ref, u_ref, s_ref, n_s):
        @pl.when(pl.program_id(1) == 0)
        def _():
            for r in _row_chunks(tm):
                xb = x_ref[r, :]
                n_s[r, :] = (xb * _rms_stats(xb) * g_ref[...]).astype(MM_DTYPE)
            n_ref[...] = n_s[...]

        for r in _row_chunks(tm):
            n = n_s[r, :]
            a = jnp.dot(n, wg_ref[0], preferred_element_type=F32)
            u = jnp.dot(n, wu_ref[0], preferred_element_type=F32)
            a_ref[0, r, :] = a.astype(a_ref.dtype)
            u_ref[0, r, :] = u.astype(u_ref.dtype)
            s_ref[0, r, :] = ((a * _sigmoid(a)) * u).astype(s_ref.dtype)

    n, a, u, s = pl.pallas_call(
        gate_up, name=tag + "_gate_up", grid=(nt, ns), in_specs=[row, vec, wcol, wcol],
        out_specs=[row, act3, act3, act3],
        out_shape=[jax.ShapeDtypeStruct((t, d), MM_DTYPE)] + [jax.ShapeDtypeStruct((ns, t, fs), MM_DTYPE)] * 3,
        scratch_shapes=[pltpu.VMEM((tm, d), MM_DTYPE)],
        compiler_params=_params(("parallel", "arbitrary")))(x, g_pre, wg, wu)

    def down(s_ref, wd_ref, x_ref, g_ref, h_ref, y_ref, acc):
        q = pl.program_id(1)

        @pl.when(q == 0)
        def _():
            acc[...] = jnp.zeros(acc.shape, F32)

        for r in _row_chunks(tm):
            acc[r, :] += jnp.dot(s_ref[0, r, :], wd_ref[0], preferred_element_type=F32)

        @pl.when(q == ns - 1)
        def _():
            for r in _row_chunks(tm):
                hb = acc[r, :]
                h_ref[r, :] = hb
                y_ref[r, :] = x_ref[r, :] + 0.5 * (hb * _rms_stats(hb) * g_ref[...])

    h, y = pl.pallas_call(
        down, name=tag + "_down", grid=(nt, ns), in_specs=[act3, wrow, row, vec], out_specs=[row, row],
        out_shape=[jax.ShapeDtypeStruct((t, d), F32)] * 2, scratch_shapes=[pltpu.VMEM((tm, d), F32)],
        compiler_params=_params(("parallel", "arbitrary")))(s, wd, x, g_post)
    return y, (x, n, a, u, s, h)


def _ffn_bwd(tag, dy, saved, g_pre, wg, wu, wd, g_post, tm, tk):
    x, n, a, u, s, h = saved
    t, d = x.shape
    ns, _, fs = wg.shape
    nt, nk = t // tm, t // tk
    row = pl.BlockSpec((tm, d), lambda i, q: (i, 0))
    vec = pl.BlockSpec((1, d), lambda i, q: (0, 0))
    act3 = pl.BlockSpec((1, tm, fs), lambda i, q: (q, i, 0))
    wcol = pl.BlockSpec((1, d, fs), lambda i, q: (q, 0, 0))
    wrow = pl.BlockSpec((1, fs, d), lambda i, q: (q, 0, 0))
    nt_dims = (((1,), (1,)), ((), ()))
    tn_dims = (((0,), (0,)), ((), ()))

    def down_b(h_ref, dy_ref, g_ref, wd_ref, a_ref, u_ref, dh_ref, da_ref, du_ref, dg_ref, dh_s):
        i, q = pl.program_id(0), pl.program_id(1)

        @pl.when((i == 0) & (q == 0))
        def _():
            dg_ref[...] = jnp.zeros(dg_ref.shape, F32)

        @pl.when(q == 0)
        def _():
            for r in _row_chunks(tm):
                hb = h_ref[r, :]
                dh, dg = _rms_bwd(hb, _rms_stats(hb), g_ref[...], 0.5 * dy_ref[r, :])
                dh_s[r, :] = dh.astype(MM_DTYPE)
                dg_ref[...] += dg
            dh_ref[...] = dh_s[...]

        for r in _row_chunks(tm):
            ds = lax.dot_general(dh_s[r, :], wd_ref[0], nt_dims, preferred_element_type=F32)
            ab, ub = a_ref[0, r, :].astype(F32), u_ref[0, r, :].astype(F32)
            sg = _sigmoid(ab)
            da_ref[0, r, :] = (ds * ub * (sg * (1.0 + ab * (1.0 - sg)))).astype(da_ref.dtype)
            du_ref[0, r, :] = (ds * (ab * sg)).astype(du_ref.dtype)

    dh, da, du, dg_post = pl.pallas_call(
        down_b, name=tag + "_down_b", grid=(nt, ns), in_specs=[row, row, vec, wrow, act3, act3],
        out_specs=[row, act3, act3, vec],
        out_shape=[jax.ShapeDtypeStruct((t, d), MM_DTYPE)] + [jax.ShapeDtypeStruct((ns, t, fs), MM_DTYPE)] * 2
        + [jax.ShapeDtypeStruct((1, d), F32)],
        scratch_shapes=[pltpu.VMEM((tm, d), MM_DTYPE)],
        compiler_params=_params(("arbitrary", "arbitrary")))(h, dy, g_post, wd, a, u)

    def down_w(s_ref, dh_ref, dw_ref, acc):
        kk = pl.program_id(1)

        @pl.when(kk == 0)
        def _():
            acc[...] = jnp.zeros(acc.shape, F32)

        acc[...] += lax.dot_general(s_ref[0], dh_ref[...], tn_dims, preferred_element_type=F32)

        @pl.when(kk == nk - 1)
        def _():
            dw_ref[0] = acc[...].astype(dw_ref.dtype)

    dwd = pl.pallas_call(
        down_w, name=tag + "_down_w", grid=(ns, nk),
        in_specs=[pl.BlockSpec((1, tk, fs), lambda q, kk: (q, kk, 0)), pl.BlockSpec((tk, d), lambda q, kk: (kk, 0))],
        out_specs=pl.BlockSpec((1, fs, d), lambda q, kk: (q, 0, 0)),
        out_shape=jax.ShapeDtypeStruct((ns, fs, d), MM_DTYPE), scratch_shapes=[pltpu.VMEM((fs, d), F32)],
        compiler_params=_params(("parallel", "arbitrary")))(s, dh)

    def gate_up_b(da_ref, du_ref, wg_ref, wu_ref, x_ref, dy_ref, g_ref, dx_ref, dg_ref, acc):
        i, q = pl.program_id(0), pl.program_id(1)

        @pl.when((i == 0) & (q == 0))
        def _():
            dg_ref[...] = jnp.zeros(dg_ref.shape, F32)

        @pl.when(q == 0)
        def _():
            acc[...] = jnp.zeros(acc.shape, F32)

        for r in _row_chunks(tm):
            acc[r, :] += (lax.dot_general(da_ref[0, r, :], wg_ref[0], nt_dims, preferred_element_type=F32)
                          + lax.dot_general(du_ref[0, r, :], wu_ref[0], nt_dims, preferred_element_type=F32))

        @pl.when(q == ns - 1)
        def _():
            for r in _row_chunks(tm):
                xb = x_ref[r, :]
                dx, dg = _rms_bwd(xb, _rms_stats(xb), g_ref[...], acc[r, :])
                dx_ref[r, :] = dy_ref[r, :] + dx
                dg_ref[...] += dg

    dx, dg_pre = pl.pallas_call(
        gate_up_b, name=tag + "_gate_up_b", grid=(nt, ns), in_specs=[act3, act3, wcol, wcol, row, row, vec],
        out_specs=[row, vec], out_shape=[jax.ShapeDtypeStruct((t, d), F32), jax.ShapeDtypeStruct((1, d), F32)],
        scratch_shapes=[pltpu.VMEM((tm, d), F32)],
        compiler_params=_params(("arbitrary", "arbitrary")))(da, du, wg, wu, x, dy, g_pre)

    def gate_up_w(n_ref, da_ref, du_ref, dwg_ref, dwu_ref, acc_g, acc_u):
        kk = pl.program_id(1)

        @pl.when(kk == 0)
        def _():
            acc_g[...] = jnp.zeros(acc_g.shape, F32)
            acc_u[...] = jnp.zeros(acc_u.shape, F32)

        nb = n_ref[...]
        acc_g[...] += lax.dot_general(nb, da_ref[0], tn_dims, preferred_element_type=F32)
        acc_u[...] += lax.dot_general(nb, du_ref[0], tn_dims, preferred_element_type=F32)

        @pl.when(kk == nk - 1)
        def _():
            dwg_ref[0] = acc_g[...].astype(dwg_ref.dtype)
            dwu_ref[0] = acc_u[...].astype(dwu_ref.dtype)

    k3 = pl.BlockSpec((1, tk, fs), lambda q, kk: (q, kk, 0))
    wout = pl.BlockSpec((1, d, fs), lambda q, kk: (q, 0, 0))
    dwg, dwu = pl.pallas_call(
        gate_up_w, name=tag + "_gate_up_w", grid=(ns, nk),
        in_specs=[pl.BlockSpec((tk, d), lambda q, kk: (kk, 0)), k3, k3], out_specs=[wout, wout],
        out_shape=[jax.ShapeDtypeStruct((ns, d, fs), MM_DTYPE)] * 2,
        scratch_shapes=[pltpu.VMEM((d, fs), F32)] * 2,
        compiler_params=_params(("parallel", "arbitrary")))(n, da, du)
    return dx, dg_pre, dwg, dwu, dwd, dg_post


NEG = -1e30


def _attn_scale():
    return (MLA_NOPE + MLA_ROPE) ** -0.5


def _causal_pairs(nq, by_key):
    if by_key:
        pairs = [(qi, ki) for ki in range(nq) for qi in range(ki, nq)]
    else:
        pairs = [(qi, ki) for qi in range(nq) for ki in range(qi + 1)]
    return jnp.asarray([p[0] for p in pairs], jnp.int32), jnp.asarray([p[1] for p in pairs], jnp.int32)


def _below_diagonal(shape):
    return lax.broadcasted_iota(jnp.int32, shape, 1) <= lax.broadcasted_iota(jnp.int32, shape, 0)


def _attn_call(name, body, tables, args, in_kinds, out_kinds, scratch, t, tq, carried=None):
    qmap = lambda h, p, qt, kt: (qt[p], h)
    kmap = lambda h, p, qt, kt: (kt[p], h)
    width = HEADS_PER_STEP * HEAD_LANES
    spec = lambda kind: pl.BlockSpec((tq, width), qmap if kind == "q" else kmap)
    n_pairs = tables[0].shape[0]
    n_groups = N_HEADS // HEADS_PER_STEP
    n_in, n_out, n_scr = len(in_kinds), len(out_kinds), scratch
    x_ins = list(carried.ins) if carried else []
    x_outs = list(carried.outs) if carried else []
    x_scr = [pltpu.SemaphoreType.DMA((carried.n_sem,)), pltpu.SemaphoreType.DMA((carried.n_sem,))] if carried else []

    def full_body(qt, kt, *refs):
        ins, refs = refs[:n_in], refs[n_in:]
        xi, refs = refs[:len(x_ins)], refs[len(x_ins):]
        outs, refs = refs[:n_out], refs[n_out:]
        xo, refs = refs[:len(x_outs)], refs[len(x_outs):]
        scr, sems = refs[:n_scr], refs[n_scr:]
        if carried:
            @pl.when((pl.program_id(0) == 0) & (pl.program_id(1) == 0))
            def _():
                carried.start(xi, xo, *sems)

        heads = [tuple(r.at[:, pl.ds(hh * HEAD_LANES, HEAD_LANES)] for r in (*ins, *outs, *scr))
                 for hh in range(HEADS_PER_STEP)]
        body(qt, kt, heads)
        if carried:
            @pl.when((pl.program_id(0) == n_groups - 1) & (pl.program_id(1) == n_pairs - 1))
            def _():
                carried.finish(xi, xo, *sems)

    grid_spec = pltpu.PrefetchScalarGridSpec(
        num_scalar_prefetch=2, grid=(n_groups, n_pairs),
        in_specs=[spec(kd) for kd in in_kinds] + [HBM_SPEC] * len(x_ins),
        out_specs=[spec(kd) for kd in out_kinds] + [HBM_SPEC] * len(x_outs),
        scratch_shapes=[pltpu.VMEM((tq, width), F32)] * n_scr + x_scr)
    return pl.pallas_call(full_body, name=name, grid_spec=grid_spec,
                          out_shape=[jax.ShapeDtypeStruct((t, MLA_PAD), F32) for _ in out_kinds] + x_outs,
                          compiler_params=_params(("arbitrary", "arbitrary")))(*tables, *args, *x_ins)


class _Carried:
    def __init__(self, ins, outs, n_sem, start, finish):
        self.ins, self.outs, self.n_sem, self.start, self.finish = ins, outs, n_sem, start, finish


def _attn_fwd(q, k, v, tq, carried=None):
    t = q.shape[0]
    nq = t // tq

    def body(qt, kt, heads):
        p_id = pl.program_id(1)
        qi, ki = qt[p_id], kt[p_id]

        @pl.when(ki == 0)
        def _():
            for _, _, _, _, _, m_s, l_s, acc_s in heads:
                m_s[...] = jnp.full(m_s.shape, NEG, F32)
                l_s[...] = jnp.zeros(l_s.shape, F32)
                acc_s[...] = jnp.zeros(acc_s.shape, F32)

        def update(diagonal):
            for q_ref, k_ref, v_ref, _, _, m_s, l_s, acc_s in heads:
                s = lax.dot_general(q_ref[...], k_ref[...], (((1,), (1,)), ((), ())), preferred_element_type=F32)
                if diagonal:
                    s = jnp.where(_below_diagonal(s.shape), s, NEG)
                m_old = m_s[...]
                m_new = jnp.maximum(m_old, jnp.max(s, axis=1, keepdims=True))
                alpha = jnp.exp(m_old - m_new)
                p = jnp.exp(s - m_new[:, :1])
                l_s[...] = l_s[...] * alpha + jnp.sum(p, axis=1, keepdims=True)
                acc_s[...] = acc_s[...] * alpha + jnp.dot(p.astype(MM_DTYPE), v_ref[...], preferred_element_type=F32)
                m_s[...] = m_new

        @pl.when(ki < qi)
        def _():
            update(False)

        @pl.when(ki == qi)
        def _():
            update(True)
            for _, _, _, o_ref, lse_ref, m_s, l_s, acc_s in heads:
                o_ref[...] = acc_s[...] / l_s[...]
                lse_ref[...] = m_s[...] + jnp.log(l_s[...])

    return _attn_call("mla_attn_fwd", body, _causal_pairs(nq, False), (q, k, v), "qkk", "qq", 3, t, tq, carried)


def _attn_probs(q, k, lse, diagonal):
    s = lax.dot_general(q, k, (((1,), (1,)), ((), ())), preferred_element_type=F32)
    p = jnp.exp(s - lse[:, :1])
    return jnp.where(_below_diagonal(s.shape), p, 0.0) if diagonal else p


def _attn_bwd_dq(q, k, v, do, lse, delta, tq):
    t = q.shape[0]
    nq = t // tq

    def body(qt, kt, heads):
        p_id = pl.program_id(1)
        qi, ki = qt[p_id], kt[p_id]

        @pl.when(ki == 0)
        def _():
            for refs in heads:
                refs[-1][...] = jnp.zeros(refs[-1].shape, F32)

        def step(diagonal):
            for q_ref, k_ref, v_ref, do_ref, lse_ref, dl_ref, _, acc_s in heads:
                p = _attn_probs(q_ref[...], k_ref[...], lse_ref[...], diagonal)
                dp = lax.dot_general(do_ref[...], v_ref[...], (((1,), (1,)), ((), ())), preferred_element_type=F32)
                ds = p * (dp - dl_ref[...][:, :1])
                acc_s[...] += jnp.dot(ds.astype(MM_DTYPE), k_ref[...], preferred_element_type=F32)

        @pl.when(ki < qi)
        def _():
            step(False)

        @pl.when(ki == qi)
        def _():
            step(True)
            for refs in heads:
                refs[-2][...] = refs[-1][...]

    return _attn_call("mla_attn_bwd_dq", body, _causal_pairs(nq, False), (q, k, v, do, lse, delta), "qkkqqq", "q",
                      1, t, tq)[0]


def _attn_bwd_dkv(q, k, v, do, lse, delta, tq, carried=None):
    t = q.shape[0]
    nq = t // tq

    def body(qt, kt, heads):
        p_id = pl.program_id(1)
        qi, ki = qt[p_id], kt[p_id]

        def step(diagonal):
            for q_ref, k_ref, v_ref, do_ref, lse_ref, dl_ref, _, _, dk_s, dv_s in heads:
                p = _attn_probs(q_ref[...], k_ref[...], lse_ref[...], diagonal)
                dv_s[...] += lax.dot_general(p.astype(MM_DTYPE), do_ref[...], (((0,), (0,)), ((), ())),
                                             preferred_element_type=F32)
                dp = lax.dot_general(do_ref[...], v_ref[...], (((1,), (1,)), ((), ())), preferred_element_type=F32)
                ds = p * (dp - dl_ref[...][:, :1])
                dk_s[...] += lax.dot_general(ds.astype(MM_DTYPE), q_ref[...], (((0,), (0,)), ((), ())),
                                             preferred_element_type=F32)

        @pl.when(qi == ki)
        def _():
            for refs in heads:
                refs[-2][...] = jnp.zeros(refs[-2].shape, F32)
                refs[-1][...] = jnp.zeros(refs[-1].shape, F32)
            step(True)

        @pl.when(qi > ki)
        def _():
            step(False)

        @pl.when(qi == nq - 1)
        def _():
            for refs in heads:
                refs[-4][...] = refs[-2][...]
                refs[-3][...] = refs[-1][...]

    return _attn_call("mla_attn_bwd_dkv", body, _causal_pairs(nq, True), (q, k, v, do, lse, delta), "qkkqqq", "kk",
                      2, t, tq, carried)


def _dotf(a, b, dims=(((1,), (0,)), ((), ()))):
    return lax.dot_general(a, b, dims, preferred_element_type=F32, precision=HI)


def _dot1(a, b, dims=(((1,), (0,)), ((), ()))):
    return lax.dot_general(a.astype(MM_DTYPE), b.astype(MM_DTYPE), dims, preferred_element_type=F32)


def _dot3(a, b, dims=(((1,), (0,)), ((), ()))):
    return lax.dot_general(a, b, dims, preferred_element_type=F32, precision=lax.Precision.HIGH)


NN3 = (((2,), (1,)), ((0,), (0,)))
NT3 = (((2,), (2,)), ((0,), (0,)))
TN3 = (((1,), (1,)), ((0,), (0,)))


def _tri_masks(nh):
    shape = (nh, CHUNK, CHUNK)
    return lax.broadcasted_iota(jnp.int32, shape, 1), lax.broadcasted_iota(jnp.int32, shape, 2)


def _gdn_chunk_common(k, gcc, bb, row, col, dot=_dot1):
    tril = row >= col
    gcr = jnp.swapaxes(gcc, 1, 2)
    dm = jnp.exp(jnp.where(tril, gcc - gcr, NEG))
    kb = k * bb
    lm = jnp.where(row > col, dot(kb, k, NT3) * dm, 0.0)
    return dm, kb, lm


def _unit_lower_inverse(lm, eye):
    t = eye - lm
    p = lm
    for _ in range(CHUNK.bit_length() - 2):
        p = _dot3(p, p, NN3)
        t = t + _dot3(t, p, NN3)
    return t


def _chunk_sum_matrix(tb, upper):
    r = lax.broadcasted_iota(jnp.int32, (tb, tb), 0)
    c = lax.broadcasted_iota(jnp.int32, (tb, tb), 1)
    same = (r // CHUNK) == (c // CHUNK)
    return (same & ((c >= r) if upper else (c <= r))).astype(F32)


def _gdn_fwd(q, k, v, gb, bb):
    nh, t, dh = q.shape
    nchunk = t // CHUNK

    def body(q_ref, k_ref, v_ref, g_ref, b_ref, o_ref, sall_ref, tall_ref, s_s):
        @pl.when(pl.program_id(0) == 0)
        def _():
            s_s[...] = jnp.zeros(s_s.shape, F32)

        row, col = _tri_masks(nh)
        qh, kh, vh, bbh, gcc = q_ref[...], k_ref[...], v_ref[...], b_ref[...], g_ref[...]
        dm, kb, lm = _gdn_chunk_common(kh, gcc, bbh, row, col)
        eg = jnp.exp(gcc)
        glr = gcc[:, CHUNK - 1:CHUNK, :]
        th = _unit_lower_inverse(lm, (row == col).astype(F32))
        w = _dot1(th, kb * eg, NN3)
        u = _dot1(th, vh * bbh, NN3)
        at = jnp.where(row >= col, _dot1(qh, kh, NT3) * dm, 0.0)
        sh = s_s[...]
        vn = u - _dot1(w, sh, NN3)
        o_ref[...] = _dot1(qh * eg, sh, NN3) + _dot1(at, vn, NN3)
        kd = kh * jnp.exp(glr - gcc)
        sall_ref[:, 0] = sh
        tall_ref[...] = th
        s_s[...] = sh * jnp.exp(glr) + _dot1(kd, vn, TN3)

    blk = pl.BlockSpec((nh, CHUNK, dh), lambda n: (0, n, 0))
    return pl.pallas_call(
        body, name="gdn_fwd", grid=(nchunk,), in_specs=[blk] * 5,
        out_specs=[blk, pl.BlockSpec((nh, 1, dh, dh), lambda n: (0, n, 0, 0)), blk],
        out_shape=[jax.ShapeDtypeStruct((nh, t, dh), F32), jax.ShapeDtypeStruct((nh, nchunk, dh, dh), F32),
                   jax.ShapeDtypeStruct((nh, t, CHUNK), F32)],
        scratch_shapes=[pltpu.VMEM((nh, dh, dh), F32)],
        compiler_params=_params(("arbitrary",)))(q, k, v, gb, bb)


def _gdn_bwd(q, k, v, gb, bb, sall, tall, do):
    nh, t, dh = q.shape
    nchunk = t // CHUNK

    def body(q_ref, k_ref, v_ref, g_ref, b_ref, sall_ref, tall_ref, do_ref,
             dq_ref, dk_ref, dv_ref, dg_ref, db_ref, ds_s):
        @pl.when(pl.program_id(0) == 0)
        def _():
            ds_s[...] = jnp.zeros(ds_s.shape, F32)

        row, col = _tri_masks(nh)
        tril, stril = row >= col, row > col
        rsum = lambda x: jnp.sum(x, axis=2, keepdims=True)
        qh, kh, vh, gcc, bbh = q_ref[...], k_ref[...], v_ref[...], g_ref[...], b_ref[...]
        sh, th, doh, dsp = sall_ref[:, 0], tall_ref[...], do_ref[...], ds_s[...]
        dm, kb, lm = _gdn_chunk_common(kh, gcc, bbh, row, col, _dot3)
        eg = jnp.exp(gcc)
        glr = gcc[:, CHUNK - 1:CHUNK, :]
        glv = jnp.exp(glr)
        egl = jnp.exp(glr - gcc)
        rw, ru = kb * eg, vh * bbh
        w, u = _dot3(th, rw, NN3), _dot3(th, ru, NN3)
        at = jnp.where(tril, _dot3(qh, kh, NT3) * dm, 0.0)
        qd, kd = qh * eg, kh * egl
        vn = u - _dot3(w, sh, NN3)
        dgl = jnp.sum(rsum(dsp * sh), axis=1, keepdims=True)
        dkd = _dot3(vn, dsp, NT3)
        dvn = _dot3(kd, dsp, NN3)
        dqd = _dot3(doh, sh, NT3)
        dat = jnp.where(tril, _dot3(doh, vn, NT3), 0.0)
        dvn = dvn + _dot3(at, doh, TN3)
        dw = -_dot3(dvn, sh, NT3)
        ds_s[...] = dsp * glv + _dot3(qd, doh, TN3) - _dot3(w, dvn, TN3)
        dpa = dat * dm
        dq_ref[...] = _dot3(dpa, kh, NN3) + dqd * eg
        dk = _dot3(dpa, qh, TN3) + dkd * egl
        t6 = rsum(dkd * kd)
        dgam = rsum(dqd * qd) - t6
        dgam_last = jnp.sum(t6, axis=1, keepdims=True) + dgl * glv
        drw = _dot3(th, dw, TN3)
        dru = _dot3(th, dvn, TN3)
        dl = -jnp.where(stril, _dot3(drw, w, NT3) + _dot3(dru, u, NT3), 0.0)
        dgam = dgam + rsum(drw * rw)
        dv_ref[...] = dru * bbh
        dp2 = dl * dm
        dkb = drw * eg + _dot3(dp2, kh, NN3)
        dk_ref[...] = dk + _dot3(dp2, kb, TN3) + dkb * bbh
        db_ref[...] = rsum(dru * vh) + rsum(dkb * kh) + jnp.zeros((nh, CHUNK, dh), F32)
        e = dat * at + dl * lm
        dgam_b = dgam + rsum(e) - _dotf(e, jnp.ones((nh, CHUNK, CHUNK), F32), TN3)
        dgam_b = dgam_b + jnp.where(row == CHUNK - 1, dgam_last, 0.0)
        dg_ref[...] = dgam_b

    rev = lambda n: (0, nchunk - 1 - n, 0)
    blk = pl.BlockSpec((nh, CHUNK, dh), rev)
    sblk = pl.BlockSpec((nh, 1, dh, dh), lambda n: (0, nchunk - 1 - n, 0, 0))
    out = jax.ShapeDtypeStruct((nh, t, dh), F32)
    return pl.pallas_call(
        body, name="gdn_bwd", grid=(nchunk,), in_specs=[blk] * 5 + [sblk, blk, blk], out_specs=[blk] * 5,
        out_shape=[out] * 5, scratch_shapes=[pltpu.VMEM((nh, dh, dh), F32)],
        compiler_params=_params(("arbitrary",)))(q, k, v, gb, bb, sall, tall, do)


def _group_ones():
    r = lax.broadcasted_iota(jnp.int32, (GDN_W, GDN_W), 0) // GDN_DH
    c = lax.broadcasted_iota(jnp.int32, (GDN_W, GDN_W), 1) // GDN_DH
    return (r == c).astype(F32)


def _conv_taps(x, xprev, w, has_prev):
    row = lax.broadcasted_iota(jnp.int32, x.shape, 0)
    out = x * w[GDN_CONV - 1:GDN_CONV, :]
    for s in range(1, GDN_CONV):
        sh = jnp.where(row >= s, _roll(x, s, 0), _roll(xprev, s, 0) * has_prev)
        out = out + sh * w[GDN_CONV - 1 - s:GDN_CONV - s, :]
    return out


def _head_cols(x, h):
    return x[:, h * GDN_DH:(h + 1) * GDN_DH]


def _heads_spec(tb):
    return pl.BlockSpec((N_HEADS, tb, GDN_DH), lambda i: (0, i, 0))


def _mixer_fwd(x, positions, w, tb, carried=None):
    t, d = x.shape
    tables = _rope_tables(positions)

    def pre(xb, g):
        return (xb * _rms_stats(xb) * g,)

    (hn,) = _rowwise("mix_pre", pre, [x], [w["mix_pre_g"]], [(d, BF16)], [], tb)
    proj = _mm("mix_in", hn, w["w_in_pad"], "nn", F32)

    def mla_pre(p0, gq, gkv):
        cq, ckv = p0[:, :MLA_Q_RANK], p0[:, MLA_Q_RANK:MLA_Q_RANK + MLA_KV_RANK]
        return cq * _rms_stats(cq) * gq, ckv * _rms_stats(ckv) * gkv

    nq, nkv = _rowwise("mla_pre", mla_pre, [(proj, 512, PIN_MLA // 512, 0)],
                       [w["mla_q_norm_g"], w["mla_kv_norm_g"]], [(MLA_Q_RANK, BF16), (MLA_KV_RANK, BF16)], [], tb)
    qraw = _mm("mla_uq", nq, w["w_uq_pad"], "nn", F32)
    kv = _mm("mla_ukv", nkv, w["w_kv_pad"], "nn", F32)

    def rope_f(qr, kn, vv, kpe, c, s1, s2):
        qo = _heads_apply(qr, lambda xh: _rope(xh, c, s1, s2)) * _attn_scale()
        kp = _rope(kpe, c, s1, s2)
        return qo, kn + jnp.tile(kp, (1, N_HEADS)), vv

    q, k, v = _rowwise("mla_rope", rope_f,
                       [qraw, (kv, MLA_PAD, 0, 0), (kv, MLA_PAD, 1, 0), (proj, HEAD_LANES, PIN_KPE // HEAD_LANES, 0),
                        tables[0], tables[1], tables[2]], [],
                       [(MLA_PAD, BF16)] * 3, [], tb // 2)
    tq = min(512, t)
    o, lse, *carried_out = _attn_fwd(q, k, v, tq, carried)

    def mla_post(ob, g):
        return (ob * _rms_stats(ob, N_HEADS * MLA_V) * g,)

    (cat,) = _rowwise("mla_post", mla_post, [o], [w["mla_out_g_pad"]], [(MLA_PAD, BF16)], [], tb, wide=(CAT_W, 0))

    gones = _group_ones()
    steps = t // tb

    def gdn_pre(xq, xk, xv, pq, pk, pv, cw, go, has_prev):
        outs = []
        for j, (xc, xp) in enumerate(((xq, pq), (xk, pk), (xv, pv))):
            c = _conv_taps(xc, xp, cw[:, j * GDN_W:(j + 1) * GDN_W], has_prev)
            a = c * _sigmoid(c)
            if j < 2:
                rn = lax.rsqrt(_dotf(a * a, go) + EPS)
                a = a * rn
                if j == 0:
                    a = a * (GDN_DH ** -0.5)
            outs.append(a)
        return tuple(outs)

    qh, kh, vh = _gdn_pre_call("gdn_pre", gdn_pre, proj, w["conv_w"], gones, tb, steps)
    heads_shape = jax.ShapeDtypeStruct((N_HEADS, t, GDN_DH), F32)
    lanes_shape = jax.ShapeDtypeStruct((t, HEAD_LANES), F32)
    lanes_spec = pl.BlockSpec((tb, HEAD_LANES), lambda i: (i, 0))
    vec_spec = lambda n: pl.BlockSpec((1, n), lambda i: (0, 0))

    def gate_f(ab_ref, al_ref, dt_ref, g_ref, b_ref, gh_ref, bh_ref):
        g, b = _gb_fwd(ab_ref[...], al_ref[...], dt_ref[...])
        g_ref[...] = g
        b_ref[...] = b
        gc = _dotf(_chunk_sum_matrix(tb, False), g)
        for h in range(N_HEADS):
            gh_ref[h] = jnp.broadcast_to(gc[:, h:h + 1], (tb, GDN_DH))
            bh_ref[h] = jnp.broadcast_to(b[:, N_HEADS + h:N_HEADS + h + 1], (tb, GDN_DH))

    g128, b128, gbh, bbh = pl.pallas_call(
        gate_f, name="gdn_gate_f", grid=(steps,),
        in_specs=[pl.BlockSpec((tb, HEAD_LANES), lambda i: (i, PIN_AB // HEAD_LANES)), vec_spec(HEAD_LANES),
                  vec_spec(HEAD_LANES)],
        out_specs=[lanes_spec, lanes_spec, _heads_spec(tb), _heads_spec(tb)],
        out_shape=[lanes_shape, lanes_shape, heads_shape, heads_shape],
        compiler_params=_params(("arbitrary",)))(proj, w["a_log_pad"], w["dt_bias_pad"])
    oh, sall, tall = _gdn_fwd(qh, kh, vh, gbh, bbh)

    def gdn_post(o_ref, gt_ref, g_ref, cat_in, cat_ref):
        gt, g = gt_ref[...], g_ref[...]
        outs = []
        for h in range(N_HEADS):
            ob, gth = o_ref[h], _head_cols(gt, h)
            outs.append(ob * _rms_stats(ob) * g * (gth * _sigmoid(gth)))
        cat_ref[...] = jnp.concatenate(outs, axis=1).astype(cat_ref.dtype)

    gate_spec = pl.BlockSpec((tb, GDN_W), lambda i: (i, PIN_GATE // GDN_W))
    cat = pl.pallas_call(
        gdn_post, name="gdn_post", grid=(steps,),
        in_specs=[_heads_spec(tb), gate_spec, vec_spec(GDN_DH), ANY_SPEC],
        out_specs=pl.BlockSpec((tb, GDN_W), lambda i: (i, MLA_PAD // GDN_W)),
        out_shape=jax.ShapeDtypeStruct((t, CAT_W), BF16), input_output_aliases={3: 0},
        compiler_params=_params(("arbitrary",)))(oh, proj, w["gdn_norm_g"], cat)
    mixed = _mm("mix_out", cat, w["w_out_pad"], "nn", F32)

    def post(xb, hb, g):
        return (xb + hb * _rms_stats(hb) * g,)

    (y,) = _rowwise("mix_post", post, [x, mixed], [w["mix_post_g"]], [(d, F32)], [], tb)
    saved = dict(x=x, hn=hn, proj=proj, nq=nq, nkv=nkv, q=q, k=k, v=v, o=o, lse=lse, qh=qh, kh=kh, vh=vh,
                 gbh=gbh, bbh=bbh, oh=oh, sall=sall, tall=tall, cat=cat, mixed=mixed,
                 tables=tables, g128=g128, b128=b128)
    return y, saved, carried_out


def _qkv_specs(tb):
    base = PIN_QKV // GDN_W
    cur = [pl.BlockSpec((tb, GDN_W), lambda i, j=j: (i, base + j)) for j in range(3)]
    prev = [pl.BlockSpec((tb, GDN_W), lambda i, j=j: (jnp.maximum(i - 1, 0), base + j)) for j in range(3)]
    return cur + prev


def _gdn_pre_call(name, fn, proj, conv_w, gones, tb, steps):
    t = proj.shape[0]

    def body(xq, xk, xv, pq, pk, pv, cw, go, oq, ok, ov):
        has_prev = jnp.where(pl.program_id(0) == 0, 0.0, 1.0)
        outs = fn(xq[...], xk[...], xv[...], pq[...], pk[...], pv[...], cw[...], go[...], has_prev)
        for r, val in zip((oq, ok, ov), outs):
            for h in range(N_HEADS):
                r[h] = _head_cols(val, h)

    return pl.pallas_call(
        body, name=name, grid=(steps,),
        in_specs=_qkv_specs(tb) + [pl.BlockSpec(conv_w.shape, lambda i: (0, 0)),
                                   pl.BlockSpec(gones.shape, lambda i: (0, 0))],
        out_specs=[_heads_spec(tb)] * 3,
        out_shape=[jax.ShapeDtypeStruct((N_HEADS, t, GDN_DH), F32)] * 3,
        compiler_params=_params(("arbitrary",)))(proj, proj, proj, proj, proj, proj, conv_w, gones)


def _softplus(x):
    return jnp.maximum(x, 0.0) + jnp.log1p(jnp.exp(-jnp.abs(x)))


def _gb_fwd(ab, a_log, dt_bias):
    g = -jnp.exp(a_log) * _softplus(ab + dt_bias)
    return g, _sigmoid(ab)


def _rope_tables(positions):
    half = MLA_ROPE // 2
    freqs = ROPE_THETA ** (-jnp.arange(half, dtype=F32) / half)
    ang = positions.reshape(-1).astype(F32)[:, None] * freqs
    cos, sin = jnp.cos(ang), jnp.sin(ang)
    t = ang.shape[0]
    one = jnp.ones((t, MLA_NOPE), F32)
    z16, z32, z64 = jnp.zeros((t, half), F32), jnp.zeros((t, MLA_ROPE), F32), jnp.zeros((t, MLA_NOPE), F32)
    c = jnp.concatenate([one, cos, cos, jnp.ones((t, MLA_ROPE), F32)], axis=1)
    s1 = jnp.concatenate([z64, -sin, z16, z32], axis=1)
    s2 = jnp.concatenate([z64, z16, sin, z32], axis=1)
    return c, s1, s2


def _mixer_bwd(dy, sv, w, tb, carried=None):
    x, proj = sv["x"], sv["proj"]
    t, d = x.shape
    c, s1, s2 = sv["tables"]
    grads = {}

    def post_b(hb, dyb, g):
        return _rms_bwd(hb, _rms_stats(hb), g, dyb)

    dmixed, grads["mix_post_g"] = _rowwise("mix_post_b", post_b, [sv["mixed"], dy], [w["mix_post_g"]],
                                           [(d, BF16)], [(1, d)], tb)
    dcat = _mm("mix_out_bx", dmixed, w["w_out_pad"], "nt", F32)
    grads["w_out_pad"] = _mm("mix_out_bw", sv["cat"], dmixed, "tn", F32)
    steps = t // tb
    vec_spec = lambda n: pl.BlockSpec((1, n), lambda i: (0, 0))

    def gdn_post_b(o_ref, gt_ref, do_ref, g_ref, dproj_ref, doh_ref, dg_ref):
        @pl.when(pl.program_id(0) == 0)
        def _():
            dg_ref[...] = jnp.zeros(dg_ref.shape, F32)

        gt, dob, g = gt_ref[...], do_ref[...], g_ref[...]
        dgates = []
        for h in range(N_HEADS):
            ob, gth, dobh = o_ref[h], _head_cols(gt, h), _head_cols(dob, h)
            sg = _sigmoid(gth)
            r = _rms_stats(ob)
            dxo, dg = _rms_bwd(ob, r, g, dobh * (gth * sg))
            doh_ref[h] = dxo
            dg_ref[...] += dg
            dgates.append(dobh * (ob * r * g) * (sg * (1.0 + gth * (1.0 - sg))))
        dproj_ref[...] = jnp.concatenate(dgates, axis=1).astype(dproj_ref.dtype)

    dproj, doh, grads["gdn_norm_g"] = pl.pallas_call(
        gdn_post_b, name="gdn_post_b", grid=(steps,),
        in_specs=[_heads_spec(tb), pl.BlockSpec((tb, GDN_W), lambda i: (i, PIN_GATE // GDN_W)),
                  pl.BlockSpec((tb, GDN_W), lambda i: (i, MLA_PAD // GDN_W)), vec_spec(GDN_DH)],
        out_specs=[pl.BlockSpec((tb, GDN_W), lambda i: (i, PIN_GATE // GDN_W)), _heads_spec(tb), vec_spec(GDN_DH)],
        out_shape=[jax.ShapeDtypeStruct((t, PIN_W), BF16), jax.ShapeDtypeStruct((N_HEADS, t, GDN_DH), F32),
                   jax.ShapeDtypeStruct((1, GDN_DH), F32)],
        compiler_params=_params(("arbitrary",)))(sv["oh"], proj, dcat, w["gdn_norm_g"])

    def mla_post_b(ob, dmo, g):
        do, dg = _rms_bwd(ob, _rms_stats(ob, N_HEADS * MLA_V), g, dmo, N_HEADS * MLA_V)
        prod = do * ob
        delta = _heads_apply(prod, lambda ph: jnp.sum(ph, axis=1, keepdims=True) + jnp.zeros_like(ph))
        return do, delta, dg

    do, delta, grads["mla_out_g_pad"] = _rowwise(
        "mla_post_b", mla_post_b, [sv["o"], (dcat, MLA_PAD, 0, 0)], [w["mla_out_g_pad"]],
        [(MLA_PAD, BF16), (MLA_PAD, F32)], [(1, MLA_PAD)], tb // 2)
    tq = min(512, t)
    dq = _attn_bwd_dq(sv["q"], sv["k"], sv["v"], do, sv["lse"], delta, tq)
    dk, dv, *carried_out = _attn_bwd_dkv(sv["q"], sv["k"], sv["v"], do, sv["lse"], delta, tq, carried)

    def rope_b(dqb, dkb, dvb, cc, a1, a2):
        dqr = _heads_apply(dqb * _attn_scale(), lambda xh: _rope(xh, cc, -a1, -a2))
        ksum = dkb[:, :HEAD_LANES]
        for h in range(1, N_HEADS):
            ksum = ksum + dkb[:, h * HEAD_LANES:(h + 1) * HEAD_LANES]
        lane = lax.broadcasted_iota(jnp.int32, ksum.shape, 1)
        keep = (lane >= MLA_NOPE) & (lane < MLA_NOPE + MLA_ROPE)
        dkpe = jnp.where(keep, _rope(ksum, cc, -a1, -a2), 0.0)
        return dqr, jnp.concatenate([dkb, dvb], axis=1), dkpe

    dqraw, dkv, dkpe = _rowwise("mla_rope_b", rope_b, [dq, dk, dv, c, s1, s2], [],
                                [(MLA_PAD, BF16), (2 * MLA_PAD, BF16), (HEAD_LANES, F32)], [], tb // 2)
    dnq = _mm("mla_uq_bx", dqraw, w["w_uq_pad"], "nt", F32)
    grads["w_uq_pad"] = _mm("mla_uq_bw", sv["nq"], dqraw, "tn", F32)
    dnkv = _mm("mla_ukv_bx", dkv, w["w_kv_pad"], "nt", F32)
    grads["w_kv_pad"] = _mm("mla_ukv_bw", sv["nkv"], dkv, "tn", F32)

    def mla_pre_b(p0, dnqb, dnkvb, dkpeb, gq, gkv):
        cq, ckv = p0[:, :MLA_Q_RANK], p0[:, MLA_Q_RANK:MLA_Q_RANK + MLA_KV_RANK]
        dcq, dgq = _rms_bwd(cq, _rms_stats(cq), gq, dnqb)
        dckv, dgkv = _rms_bwd(ckv, _rms_stats(ckv), gkv, dnkvb)
        return jnp.concatenate([dcq, dckv, dkpeb], axis=1), dgq, dgkv

    dproj, grads["mla_q_norm_g"], grads["mla_kv_norm_g"] = _rowwise(
        "mla_pre_b", mla_pre_b, [(proj, 512, PIN_MLA // 512, 0), dnq, dnkv, dkpe],
        [w["mla_q_norm_g"], w["mla_kv_norm_g"]], [(512, BF16)], [(1, MLA_Q_RANK), (1, MLA_KV_RANK)], tb,
        wide=(PIN_W, PIN_MLA // 512), carry=dproj)

    dqh, dkh, dvh, dgh, dbh = _gdn_bwd(sv["qh"], sv["kh"], sv["vh"], sv["gbh"], sv["bbh"], sv["sall"], sv["tall"], doh)
    gones = _group_ones()

    def gdn_pre_b(xq, xk, xv, pq, pk, pv, dq_, dk_, dv_, cw, go, has_prev):
        outs = []
        for j, (xc, xp, dd) in enumerate(((xq, pq, dq_), (xk, pk, dk_), (xv, pv, dv_))):
            cc = _conv_taps(xc, xp, cw[:, j * GDN_W:(j + 1) * GDN_W], has_prev)
            sg = _sigmoid(cc)
            a = cc * sg
            if j < 2:
                rn = lax.rsqrt(_dotf(a * a, go) + EPS)
                if j == 0:
                    dd = dd * (GDN_DH ** -0.5)
                da = rn * dd - a * (rn * rn * rn) * _dotf(dd * a, go)
            else:
                da = dd
            outs.append(da * (sg * (1.0 + cc * (1.0 - sg))))
        return tuple(outs)

    dcq, dck, dcv = _gdn_pre_b_call("gdn_pre_b", gdn_pre_b, proj, (dqh, dkh, dvh), w["conv_w"], gones, tb, steps)
    dproj, grads["conv_w"] = _conv_bwd_call("gdn_conv_b", proj, (dcq, dck, dcv), w["conv_w"], dproj, tb, steps)

    def gate_b(ab_ref, g_ref, b_ref, dgh_ref, dbh_ref, al_ref, dt_ref, carry_ref, dab_ref, dal_ref, ddt_ref):
        @pl.when(pl.program_id(0) == 0)
        def _():
            dal_ref[...] = jnp.zeros(dal_ref.shape, F32)
            ddt_ref[...] = jnp.zeros(ddt_ref.shape, F32)

        ab, g128, b128 = ab_ref[...], g_ref[...], b_ref[...]
        lane = lax.broadcasted_iota(jnp.int32, ab.shape, 1)
        dg_ = jnp.zeros(ab.shape, F32)
        db_ = jnp.zeros(ab.shape, F32)
        for h in range(N_HEADS):
            dg_ = dg_ + jnp.where(lane == h, jnp.broadcast_to(dgh_ref[h][:, 0:1], ab.shape), 0.0)
            db_ = db_ + jnp.where(lane == N_HEADS + h, jnp.broadcast_to(dbh_ref[h][:, 0:1], ab.shape), 0.0)
        dg_ = _dotf(_chunk_sum_matrix(tb, True), dg_)
        slope = -jnp.exp(al_ref[...]) * _sigmoid(ab + dt_ref[...])
        dab_ref[...] = (dg_ * slope + db_ * b128 * (1.0 - b128)).astype(dab_ref.dtype)
        dal_ref[...] += jnp.sum(dg_ * g128, axis=0, keepdims=True)
        ddt_ref[...] += jnp.sum(dg_ * slope, axis=0, keepdims=True)

    lanes_spec = pl.BlockSpec((tb, HEAD_LANES), lambda i: (i, 0))
    ab_spec = pl.BlockSpec((tb, HEAD_LANES), lambda i: (i, PIN_AB // HEAD_LANES))
    dproj, grads["a_log_pad"], grads["dt_bias_pad"] = pl.pallas_call(
        gate_b, name="gdn_gate_b", grid=(steps,),
        in_specs=[ab_spec, lanes_spec, lanes_spec, _heads_spec(tb), _heads_spec(tb), vec_spec(HEAD_LANES),
                  vec_spec(HEAD_LANES), ANY_SPEC],
        out_specs=[ab_spec, vec_spec(HEAD_LANES), vec_spec(HEAD_LANES)],
        out_shape=[jax.ShapeDtypeStruct((t, PIN_W), BF16), jax.ShapeDtypeStruct((1, HEAD_LANES), F32),
                   jax.ShapeDtypeStruct((1, HEAD_LANES), F32)],
        input_output_aliases={7: 0},
        compiler_params=_params(("arbitrary",)))(proj, sv["g128"], sv["b128"], dgh, dbh, w["a_log_pad"],
                                                 w["dt_bias_pad"], dproj)
    dhn = _mm("mix_in_bx", dproj, w["w_in_pad"], "nt", F32)
    grads["w_in_pad"] = _mm("mix_in_bw", sv["hn"], dproj, "tn", F32)

    def pre_b(xb, dnb, dyb, g):
        dx, dg = _rms_bwd(xb, _rms_stats(xb), g, dnb)
        return dyb + dx, dg

    dx, grads["mix_pre_g"] = _rowwise("mix_pre_b", pre_b, [x, dhn, dy], [w["mix_pre_g"]], [(d, F32)], [(1, d)], tb)
    return dx, grads, carried_out


def _gdn_pre_b_call(name, fn, proj, dd, conv_w, gones, tb, steps):
    t = proj.shape[0]

    def body(xq, xk, xv, pq, pk, pv, d0, d1, d2, cw, go, oq, ok, ov):
        has_prev = jnp.where(pl.program_id(0) == 0, 0.0, 1.0)
        dd_rows = [jnp.concatenate([dr[h] for h in range(N_HEADS)], axis=1) for dr in (d0, d1, d2)]
        outs = fn(xq[...], xk[...], xv[...], pq[...], pk[...], pv[...], *dd_rows, cw[...], go[...], has_prev)
        for r, val in zip((oq, ok, ov), outs):
            r[...] = val

    return pl.pallas_call(
        body, name=name, grid=(steps,),
        in_specs=_qkv_specs(tb) + [_heads_spec(tb)] * 3 + [pl.BlockSpec(conv_w.shape, lambda i: (0, 0)),
                                                          pl.BlockSpec(gones.shape, lambda i: (0, 0))],
        out_specs=[pl.BlockSpec((tb, GDN_W), lambda i: (i, 0))] * 3,
        out_shape=[jax.ShapeDtypeStruct((t, GDN_W), F32)] * 3,
        compiler_params=_params(("arbitrary",)))(proj, proj, proj, proj, proj, proj, *dd, conv_w, gones)


def _conv_bwd_call(name, proj, dc, conv_w, dproj, tb, steps):
    t = proj.shape[0]
    dcur = [pl.BlockSpec((tb, GDN_W), lambda i: (i, 0))] * 3
    dnext = [pl.BlockSpec((tb, GDN_W), lambda i: (jnp.minimum(i + 1, steps - 1), 0))] * 3

    def body(xq, xk, xv, pq, pk, pv, d0, d1, d2, n0, n1, n2, cw, carry_ref, dx_ref, dw_ref):
        i = pl.program_id(0)
        has_prev = jnp.where(i == 0, 0.0, 1.0)
        has_next = jnp.where(i == steps - 1, 0.0, 1.0)

        @pl.when(i == 0)
        def _():
            dw_ref[...] = jnp.zeros(dw_ref.shape, F32)

        wv = cw[...]
        dws, dxs = [], []
        for j, (xr, pr, dr, nr) in enumerate(((xq, pq, d0, n0), (xk, pk, d1, n1), (xv, pv, d2, n2))):
            x, xp, dcv, dnx = xr[...], pr[...], dr[...], nr[...]
            wj = wv[:, j * GDN_W:(j + 1) * GDN_W]
            row = lax.broadcasted_iota(jnp.int32, x.shape, 0)
            dx = dcv * wj[GDN_CONV - 1:GDN_CONV, :]
            rows_w = [jnp.sum(dcv * x, axis=0, keepdims=True)]
            for s in range(1, GDN_CONV):
                up = jnp.where(row < tb - s, _roll(dcv, tb - s, 0), _roll(dnx, tb - s, 0) * has_next)
                dx = dx + up * wj[GDN_CONV - 1 - s:GDN_CONV - s, :]
                sh = jnp.where(row >= s, _roll(x, s, 0), _roll(xp, s, 0) * has_prev)
                rows_w.append(jnp.sum(dcv * sh, axis=0, keepdims=True))
            dxs.append(dx)
            dws.append(jnp.concatenate(rows_w[::-1], axis=0))
        dx_ref[...] = jnp.concatenate(dxs, axis=1).astype(dx_ref.dtype)
        dw_ref[...] += jnp.concatenate(dws, axis=1)

    return pl.pallas_call(
        body, name=name, grid=(steps,),
        in_specs=_qkv_specs(tb) + dcur + dnext + [pl.BlockSpec(conv_w.shape, lambda i: (0, 0)), ANY_SPEC],
        out_specs=[pl.BlockSpec((tb, 3 * GDN_W), lambda i: (i, PIN_QKV // (3 * GDN_W))),
                   pl.BlockSpec(conv_w.shape, lambda i: (0, 0))],
        out_shape=[jax.ShapeDtypeStruct((t, PIN_W), BF16), jax.ShapeDtypeStruct(conv_w.shape, F32)],
        input_output_aliases={13: 0},
        compiler_params=_params(("arbitrary",)))(proj, proj, proj, proj, proj, proj, *dc, *dc, conv_w, dproj)


def _pad_heads_cols(wm, per_head):
    r = wm.shape[0]
    return jnp.pad(wm.reshape(r, N_HEADS, per_head), ((0, 0), (0, 0), (0, HEAD_LANES - per_head))).reshape(r, MLA_PAD)


def _unpad_heads_cols(wm, per_head):
    r = wm.shape[0]
    return wm.reshape(r, N_HEADS, HEAD_LANES)[:, :, :per_head].reshape(r, N_HEADS * per_head)


def _win_to_pad(wi):
    r = wi.shape[0]
    z = lambda n: jnp.zeros((r, n), wi.dtype)
    o = MLA_Q_RANK + MLA_KV_RANK
    kpe = wi[:, o:o + MLA_ROPE]
    o2 = o + MLA_ROPE
    qkv = wi[:, o2:o2 + 3 * GDN_W]
    o3 = o2 + 3 * GDN_W
    ab = wi[:, o3:o3 + 2 * N_HEADS]
    gate = wi[:, o3 + 2 * N_HEADS:]
    return jnp.concatenate([qkv, wi[:, :o], z(MLA_NOPE), kpe, z(HEAD_LANES - MLA_NOPE - MLA_ROPE), gate, ab,
                            z(HEAD_LANES - 2 * N_HEADS)], axis=1)


def _win_from_pad(wp):
    return jnp.concatenate([wp[:, PIN_MLA:PIN_KPE], wp[:, PIN_KPE + MLA_NOPE:PIN_KPE + MLA_NOPE + MLA_ROPE],
                            wp[:, PIN_QKV:PIN_QKV + 3 * GDN_W], wp[:, PIN_AB:PIN_AB + 2 * N_HEADS],
                            wp[:, PIN_GATE:PIN_GATE + GDN_W]], axis=1)


def _wkv_to_pad(wkv):
    r = wkv.shape[0]
    w3 = wkv.reshape(r, N_HEADS, MLA_NOPE + MLA_V)
    kpart = jnp.pad(w3[:, :, :MLA_NOPE], ((0, 0), (0, 0), (0, HEAD_LANES - MLA_NOPE))).reshape(r, MLA_PAD)
    vpart = jnp.pad(w3[:, :, MLA_NOPE:], ((0, 0), (0, 0), (0, HEAD_LANES - MLA_V))).reshape(r, MLA_PAD)
    return jnp.concatenate([kpart, vpart], axis=1)


def _wkv_from_pad(wp):
    r = wp.shape[0]
    kpart = wp[:, :MLA_PAD].reshape(r, N_HEADS, HEAD_LANES)[:, :, :MLA_NOPE]
    vpart = wp[:, MLA_PAD:].reshape(r, N_HEADS, HEAD_LANES)[:, :, :MLA_V]
    return jnp.concatenate([kpart, vpart], axis=2).reshape(r, N_HEADS * (MLA_NOPE + MLA_V))


def _wout_to_pad(wo):
    n = wo.shape[1]
    mla = jnp.pad(wo[:N_HEADS * MLA_V].reshape(N_HEADS, MLA_V, n), ((0, 0), (0, HEAD_LANES - MLA_V), (0, 0)))
    return jnp.concatenate([mla.reshape(MLA_PAD, n), wo[N_HEADS * MLA_V:]], axis=0)


def _wout_from_pad(wp):
    n = wp.shape[1]
    mla = wp[:MLA_PAD].reshape(N_HEADS, HEAD_LANES, n)[:, :MLA_V].reshape(N_HEADS * MLA_V, n)
    return jnp.concatenate([mla, wp[MLA_PAD:]], axis=0)


def _pad_lanes(v, n):
    return jnp.pad(v, ((0, 0), (0, n - v.shape[1])))


def _compute_weights(full):
    w = {}
    for n in FFN_BIG:
        if n in full:
            w[n] = full[n].astype(MM_DTYPE)
    w["w_in_pad"] = _win_to_pad(full["w_in"]).astype(MM_DTYPE)
    w["w_uq_pad"] = _pad_heads_cols(full["mla_w_uq"], MLA_NOPE + MLA_ROPE).astype(MM_DTYPE)
    w["w_kv_pad"] = _wkv_to_pad(full["mla_w_ukv"]).astype(MM_DTYPE)
    w["w_out_pad"] = _wout_to_pad(full["w_out"]).astype(MM_DTYPE)
    w["conv_w"] = full["gdn_conv_w"].astype(F32)
    for n in ("ffn1_pre_g", "ffn1_post_g", "mix_pre_g", "mla_q_norm_g", "mla_kv_norm_g", "gdn_norm_g", "mix_post_g",
              "ffn2_pre_g", "ffn2_post_g"):
        w[n] = full[n]
    w["mla_out_g_pad"] = _pad_heads_cols(full["mla_out_g"], MLA_V)
    w["a_log_pad"] = _pad_lanes(full["gdn_a_log"], HEAD_LANES)
    w["dt_bias_pad"] = _pad_lanes(full["gdn_dt_bias"], HEAD_LANES)
    return w


FFN2_BIG = FFN_BIG[3:]


def _local_step(x, positions, loss_target, full, late=None):
    t, d = x.shape
    tb = min(512, t)
    tm = min(1024, t)
    w = _compute_weights(full)
    ffn = lambda tag: (w[tag + "_pre_g"], w[tag + "_w_gate"], w[tag + "_w_up"], w[tag + "_w_down"], w[tag + "_post_g"])
    x1, sv1 = _ffn_fwd("ffn1", x, *ffn("ffn1"), tm)
    x2, svm, gathered = _mixer_fwd(x1, positions, w, tb, _carried_gather(late[0]) if late else None)
    for n, gw in zip(FFN2_BIG, gathered):
        w[n] = gw
    x3, sv2 = _ffn_fwd("ffn2", x2, *ffn("ffn2"), tm)

    def loss_f(yb, tg):
        e = yb - tg
        return e * (1.0 / d), jnp.sum(e * e, axis=0, keepdims=True)

    dy, lsum = _rowwise("loss", loss_f, [x3, loss_target], [], [(d, F32)], [(1, d)], tb)
    g = {}
    dx2, g["ffn2_pre_g"], g["ffn2_w_gate"], g["ffn2_w_up"], g["ffn2_w_down"], g["ffn2_post_g"] = _ffn_bwd(
        "ffn2", dy, sv2, *ffn("ffn2"), tm, tm)
    if late:
        _, core, chip = late
        mine = [g[n] for n in FFN2_BIG]
        got = _swap_halves(mine, "late")
        pairs = [_add_pair("add_pair_late_%d" % i, gi, gt, core) for i, (gi, gt) in enumerate(zip(mine, got))]
        dx1, gm, slabs = _mixer_bwd(dx2, svm, w, tb, _carried_scatter(pairs))
        for i, n in enumerate(FFN2_BIG):
            g[n] = _add_chips("add_chips_late_%d" % i, pairs[i], slabs[i], chip)
    else:
        dx1, gm, _ = _mixer_bwd(dx2, svm, w, tb)
    dx0, g["ffn1_pre_g"], g["ffn1_w_gate"], g["ffn1_w_up"], g["ffn1_w_down"], g["ffn1_post_g"] = _ffn_bwd(
        "ffn1", dx1, sv1, *ffn("ffn1"), tm, tm)
    g["mix_pre_g"], g["mix_post_g"] = gm["mix_pre_g"], gm["mix_post_g"]
    g["mla_q_norm_g"], g["mla_kv_norm_g"] = gm["mla_q_norm_g"], gm["mla_kv_norm_g"]
    g["gdn_norm_g"] = gm["gdn_norm_g"]
    g["w_in"] = _win_from_pad(gm["w_in_pad"])
    g["mla_w_uq"] = _unpad_heads_cols(gm["w_uq_pad"], MLA_NOPE + MLA_ROPE)
    g["mla_w_ukv"] = _wkv_from_pad(gm["w_kv_pad"])
    g["mla_out_g"] = _unpad_heads_cols(gm["mla_out_g_pad"], MLA_V)
    g["gdn_conv_w"] = gm["conv_w"]
    g["gdn_a_log"] = gm["a_log_pad"][:, :N_HEADS]
    g["gdn_dt_bias"] = gm["dt_bias_pad"][:, :N_HEADS]
    g["w_out"] = _wout_from_pad(gm["w_out_pad"])
    return lsum, dx0, g


HBM_SPEC = pl.BlockSpec(memory_space=pltpu.HBM)


def _place():
    return lax.axis_index("x"), lax.axis_index("y"), lax.axis_index("c")


def _exchange_call(name, body, ins, out_shapes, n_remote, n_local):
    return pl.pallas_call(
        body, name=name, in_specs=[HBM_SPEC] * len(ins), out_specs=[HBM_SPEC] * len(out_shapes), out_shape=out_shapes,
        scratch_shapes=[pltpu.SemaphoreType.DMA((n_remote,)), pltpu.SemaphoreType.DMA((n_remote,)),
                        pltpu.SemaphoreType.DMA((n_local,))])(*ins)


def _other_chips(x, y):
    return [(1 - x, y), (x, 1 - y), (1 - x, 1 - y)]


def _at_each_chip(fn):
    x, y, _ = _place()
    for cx in range(2):
        for cy in range(2):
            pl.when((x == cx) & (y == cy))(functools.partial(fn, cx, cy))


def _at_each_device(fn):
    x, y, c = _place()
    for cx in range(2):
        for cy in range(2):
            for cc in range(2):
                pl.when((x == cx) & (y == cy) & (c == cc))(functools.partial(fn, cx, cy, cc))


def _at_each_core(fn):
    c = lax.axis_index("c")
    for cc in range(2):
        pl.when(c == cc)(functools.partial(fn, cc))


def _gather_shards(ws):
    nw = len(ws)

    def body(*refs):
        w_refs, out_refs = refs[:nw], refs[nw:2 * nw]
        send_sems, recv_sems, local_sems = refs[2 * nw:]

        def run(x, y, c):
            chips = _other_chips(x, y)
            me, sibling = 2 * x + y, (x, y, 1 - c)

            def half(ref, which):
                hr = ref.shape[0] // 2
                return ref.at[pl.ds(which * hr, hr)]

            def over_ici(i, j, src, slab, to):
                return pltpu.make_async_remote_copy(
                    src_ref=half(src, c), dst_ref=half(out_refs[i].at[slab], c), send_sem=send_sems.at[7 * i + j],
                    recv_sem=recv_sems.at[7 * i + j], device_id=to, device_id_type=MESH)

            def over_d2d(i, j, slab, which):
                return pltpu.make_async_remote_copy(
                    src_ref=half(out_refs[i].at[slab], which), dst_ref=half(out_refs[i].at[slab], which),
                    send_sem=send_sems.at[7 * i + 3 + j], recv_sem=recv_sems.at[7 * i + 3 + j], device_id=sibling,
                    device_id_type=MESH)

            def own(i, w_ref):
                return pltpu.make_async_remote_copy(
                    src_ref=w_ref, dst_ref=out_refs[i].at[me], send_sem=send_sems.at[7 * i + 6],
                    recv_sem=recv_sems.at[7 * i + 6], device_id=sibling, device_id_type=MESH)

            sends, passed = [], []
            for i, w_ref in enumerate(w_refs):
                for j, (px, py) in enumerate(chips):
                    sends.append(over_ici(i, j, w_ref, me, (px, py, c)))
                    sends[-1].start()
            for i, w_ref in enumerate(w_refs):
                sends.append(own(i, w_ref))
                sends[-1].start()
            for i, w_ref in enumerate(w_refs):
                for j, (px, py) in enumerate(chips):
                    over_ici(i, j, w_ref, 2 * px + py, (px, py, c)).wait_recv()
                    passed.append(over_d2d(i, j, 2 * px + py, c))
                    passed[-1].start()
            for i, w_ref in enumerate(w_refs):
                own(i, w_ref).wait_recv()
                for j, (px, py) in enumerate(chips):
                    over_d2d(i, j, 2 * px + py, 1 - c).wait_recv()
            for cp in sends + passed:
                cp.wait_send()

        _at_each_device(run)

    outs = [jax.ShapeDtypeStruct((N_SHARD,) + w.shape, w.dtype) for w in ws]
    return _exchange_call("gather_weight_shards", body, ws, outs, 7 * nw, 1)


def _swap_halves(gs, tag=""):
    ng = len(gs)

    def body(*refs):
        g_refs, got_refs = refs[:ng], refs[ng:2 * ng]
        send_sems, recv_sems, _ = refs[2 * ng:]
        x, y, _ = _place()

        def run(c):
            sends = []
            for i, (g_ref, got_ref) in enumerate(zip(g_refs, got_refs)):
                hr = got_ref.shape[1]
                sends.append(pltpu.make_async_remote_copy(
                    src_ref=g_ref.at[:, pl.ds((1 - c) * hr, hr)], dst_ref=got_ref, send_sem=send_sems.at[i],
                    recv_sem=recv_sems.at[i], device_id=(x, y, 1 - c), device_id_type=MESH))
                sends[-1].start()
            for cp in sends:
                cp.wait()

        _at_each_core(run)

    halves = [jax.ShapeDtypeStruct((g.shape[0], g.shape[1] // 2, g.shape[2]), g.dtype) for g in gs]
    return _exchange_call("swap_grad_halves" + tag, body, gs, halves, ng, 1)


def _scatter_copies(p_refs, out_refs, send_sems, recv_sems, x, y):
    c = lax.axis_index("c")
    copies = []
    for i, (p_ref, out_ref) in enumerate(zip(p_refs, out_refs)):
        for j, (px, py) in enumerate(_other_chips(x, y)):
            copies.append(pltpu.make_async_remote_copy(
                src_ref=p_ref.at[2 * px + py], dst_ref=out_ref.at[j], send_sem=send_sems.at[3 * i + j],
                recv_sem=recv_sems.at[3 * i + j], device_id=(px, py, c), device_id_type=MESH))
    return copies


def _start_all(make, *refs):
    def run(x, y):
        for cp in make(*refs, x, y):
            cp.start()

    _at_each_chip(run)


def _wait_all(make, *refs):
    def run(x, y):
        copies = make(*refs, x, y)
        for cp in copies:
            cp.wait_recv()
        for cp in copies:
            cp.wait_send()

    _at_each_chip(run)


def _scatter_shapes(ps):
    return [jax.ShapeDtypeStruct((3,) + p.shape[1:], p.dtype) for p in ps]


def _scatter_to_chips(ps):
    n = len(ps)

    def body(*refs):
        p_refs, out_refs = refs[:n], refs[n:2 * n]
        send_sems, recv_sems, _ = refs[2 * n:]
        _start_all(_scatter_copies, p_refs, out_refs, send_sems, recv_sems)
        _wait_all(_scatter_copies, p_refs, out_refs, send_sems, recv_sems)

    return _exchange_call("scatter_grad_quarters", body, ps, _scatter_shapes(ps), 3 * n, 1)


def _carried_scatter(ps):
    return _Carried(ps, _scatter_shapes(ps), 3 * len(ps), functools.partial(_start_all, _scatter_copies),
                    functools.partial(_wait_all, _scatter_copies))


def _direct_gather_copies(w_refs, out_refs, send_sems, recv_sems, x, y, arriving):
    c = lax.axis_index("c")
    me = 2 * x + y
    peers = [((px, py, c), 2 * px + py) for px, py in _other_chips(x, y)] + [((x, y, 1 - c), me)]
    copies = []
    for i, (w_ref, out_ref) in enumerate(zip(w_refs, out_refs)):
        for j, (peer, slab) in enumerate(peers):
            copies.append(pltpu.make_async_remote_copy(
                src_ref=w_ref, dst_ref=out_ref.at[slab if arriving else me], send_sem=send_sems.at[4 * i + j],
                recv_sem=recv_sems.at[4 * i + j], device_id=peer, device_id_type=MESH))
    return copies


def _carried_gather(ws):
    def start(w_refs, out_refs, send_sems, recv_sems):
        def run(x, y):
            for cp in _direct_gather_copies(w_refs, out_refs, send_sems, recv_sems, x, y, False):
                cp.start()

        _at_each_chip(run)

    def finish(w_refs, out_refs, send_sems, recv_sems):
        def run(x, y):
            for cp in _direct_gather_copies(w_refs, out_refs, send_sems, recv_sems, x, y, True):
                cp.wait_recv()
            for cp in _direct_gather_copies(w_refs, out_refs, send_sems, recv_sems, x, y, False):
                cp.wait_send()

        _at_each_chip(run)

    outs = [jax.ShapeDtypeStruct((N_SHARD,) + w.shape, w.dtype) for w in ws]
    return _Carried(ws, outs, 4 * len(ws), start, finish)


def _share_halves(hs):
    n = len(hs)

    def body(*refs):
        h_refs, out_refs = refs[:n], refs[n:2 * n]
        send_sems, recv_sems, _ = refs[2 * n:]
        x, y, c = _place()
        sends = []
        for i, (h_ref, out_ref) in enumerate(zip(h_refs, out_refs)):
            sends.append(pltpu.make_async_remote_copy(
                src_ref=h_ref, dst_ref=out_ref, send_sem=send_sems.at[i], recv_sem=recv_sems.at[i],
                device_id=(x, y, 1 - c), device_id_type=MESH))
            sends[-1].start()
        for cp in sends:
            cp.wait()

    outs = [jax.ShapeDtypeStruct(h.shape, h.dtype) for h in hs]
    return _exchange_call("share_grad_halves", body, hs, outs, n, 1)


def _scalar_grid_call(name, body, scalars, grid, in_specs, out_specs, out_shape, args):
    grid_spec = pltpu.PrefetchScalarGridSpec(num_scalar_prefetch=len(scalars), grid=grid, in_specs=in_specs,
                                             out_specs=out_specs)
    return pl.pallas_call(body, name=name, grid_spec=grid_spec, out_shape=out_shape,
                          compiler_params=_params(("arbitrary",) * len(grid)))(*scalars, *args)


def _add_pair(name, g, got, core):
    ns_, hr, cols = got.shape
    th = _row_tile(hr, 512)
    nb = hr // th

    def body(core_ref, g_ref, got_ref, out_ref):
        out_ref[...] = (g_ref[...].astype(F32) + got_ref[...].astype(F32)).astype(out_ref.dtype)

    blk = pl.BlockSpec((1, th, cols), lambda q, j, core_ref: (q, j, 0))
    own = pl.BlockSpec((1, th, cols), lambda q, j, core_ref: (q, core_ref[0] * nb + j, 0))
    return _scalar_grid_call(name, body, [core], (ns_, nb), [own, blk], blk,
                             jax.ShapeDtypeStruct(got.shape, got.dtype), [g, got])


def _add_chips(name, pairs, slabs, chip):
    _, hr, cols = slabs.shape
    th = _row_tile(hr, 512)

    def body(chip_ref, own_ref, s0_ref, s1_ref, s2_ref, out_ref):
        total = own_ref[0].astype(F32) + s0_ref[0].astype(F32)
        out_ref[...] = (total + s1_ref[0].astype(F32)) + s2_ref[0].astype(F32)

    own = pl.BlockSpec((1, th, cols), lambda j, chip_ref: (chip_ref[0], j, 0))
    others = [pl.BlockSpec((1, th, cols), lambda j, chip_ref, k=k: (k, j, 0)) for k in range(3)]
    return _scalar_grid_call(name, body, [chip], (hr // th,), [own] + others,
                             pl.BlockSpec((th, cols), lambda j, chip_ref: (j, 0)),
                             jax.ShapeDtypeStruct((hr, cols), F32), [pairs, slabs, slabs, slabs])


def _join_halves(name, mine, other, core):
    hr, cols = mine.shape
    th = _row_tile(hr, 512)
    nb = hr // th

    def body(core_ref, mine_ref, other_ref, out_ref):
        is_mine = pl.program_id(0) == core_ref[0]

        @pl.when(is_mine)
        def _():
            out_ref[0] = mine_ref[...]

        @pl.when(jnp.logical_not(is_mine))
        def _():
            out_ref[0] = other_ref[...]

    blk = pl.BlockSpec((th, cols), lambda h, j, core_ref: (j, 0))
    return _scalar_grid_call(name, body, [core], (2, nb), [blk, blk],
                             pl.BlockSpec((1, th, cols), lambda h, j, core_ref: (0, h * nb + j, 0)),
                             jax.ShapeDtypeStruct((1, 2 * hr, cols), mine.dtype), [mine, other])


def _gather_small(sp):
    def body(s_ref, out_ref, send_sems, recv_sems, local_sem):
        x, y, c = _place()
        me = 4 * x + 2 * y + c
        peers = [(x ^ (m >> 2), y ^ ((m >> 1) & 1), c ^ (m & 1)) for m in range(1, 8)]
        mine = pltpu.make_async_copy(s_ref, out_ref.at[me], local_sem)
        mine.start()
        sends = [pltpu.make_async_remote_copy(src_ref=s_ref, dst_ref=out_ref.at[me], send_sem=send_sems.at[j],
                                              recv_sem=recv_sems.at[j], device_id=p, device_id_type=MESH)
                 for j, p in enumerate(peers)]
        for cp in sends:
            cp.start()
        for j, (px, py, pc) in enumerate(peers):
            pltpu.make_async_remote_copy(src_ref=s_ref, dst_ref=out_ref.at[4 * px + 2 * py + pc],
                                         send_sem=send_sems.at[j], recv_sem=recv_sems.at[j], device_id=(px, py, pc),
                                         device_id_type=MESH).wait_recv()
        for cp in sends:
            cp.wait_send()
        mine.wait()

    return pl.pallas_call(
        body, name="gather_small_grads", in_specs=[HBM_SPEC], out_specs=HBM_SPEC,
        out_shape=jax.ShapeDtypeStruct((8,) + sp.shape, sp.dtype),
        scratch_shapes=[pltpu.SemaphoreType.DMA((7,)), pltpu.SemaphoreType.DMA((7,)), pltpu.SemaphoreType.DMA])(sp)


def _pack_rows(total):
    rows = -(-total // LANES)
    return -(-rows // 32) * 32


def _pack(arrs, dtype):
    flat = jnp.concatenate([a.reshape(-1).astype(dtype) for a in arrs])
    rows = _pack_rows(flat.shape[0])
    return jnp.pad(flat, (0, rows * LANES - flat.shape[0])).reshape(rows, LANES)


def _unpack(buf, shapes):
    flat = buf.reshape(-1)
    out, off = {}, 0
    for n, shp in shapes:
        size = shp[0] * shp[1]
        out[n] = flat[off:off + size].reshape(shp)
        off += size
    return out


def _adamw(name, w3, g, m3, v3, tb):
    c1 = 1.0 - ADAM_B1 ** ADAM_STEP
    c2 = 1.0 - ADAM_B2 ** ADAM_STEP
    _, r, cols = w3.shape
    emit = g.ndim == 2
    blk3 = pl.BlockSpec((1, tb, cols), lambda i: (0, i, 0))
    g_spec = pl.BlockSpec((tb, cols), lambda i: (i, 0)) if emit else blk3

    def body(w_ref, g_ref, m_ref, v_ref, *out_refs):
        gb = g_ref[...] if emit else g_ref[0]
        m2 = ADAM_B1 * m_ref[0] + (1.0 - ADAM_B1) * gb
        v2 = ADAM_B2 * v_ref[0] + (1.0 - ADAM_B2) * (gb * gb)
        out_refs[-3][0] = -ADAM_LR * ((m2 / c1) / (jnp.sqrt(v2 / c2) + ADAM_EPS) + ADAM_WD * w_ref[0])
        out_refs[-2][0] = m2
        out_refs[-1][0] = v2
        if emit:
            out_refs[0][0] = gb

    n_out = 4 if emit else 3
    outs = pl.pallas_call(
        body, name=name, grid=(r // tb,), in_specs=[blk3, g_spec, blk3, blk3], out_specs=[blk3] * n_out,
        out_shape=[jax.ShapeDtypeStruct((1, r, cols), F32)] * n_out,
        compiler_params=_params(("arbitrary",)))(w3, g, m3, v3)
    return outs if emit else [g] + list(outs)


def _row_tile(rows, pref):
    if rows <= pref:
        return rows
    t = pref
    while t >= 8:
        if rows % t == 0 and t % 8 == 0:
            return t
        t -= 8
    return rows


def kernel(x, positions, ffn1_pre_g, ffn1_w_gate, ffn1_w_up, ffn1_w_down, ffn1_post_g, mix_pre_g, w_in, mla_q_norm_g, mla_w_uq, mla_kv_norm_g, mla_w_ukv, mla_out_g, gdn_conv_w, gdn_a_log, gdn_dt_bias, gdn_norm_g, w_out, mix_post_g, ffn2_pre_g, ffn2_w_gate, ffn2_w_up, ffn2_w_down, ffn2_post_g, loss_target, m_ffn1_pre_g, m_ffn1_w_gate, m_ffn1_w_up, m_ffn1_w_down, m_ffn1_post_g, m_mix_pre_g, m_w_in, m_mla_q_norm_g, m_mla_w_uq, m_mla_kv_norm_g, m_mla_w_ukv, m_mla_out_g, m_gdn_conv_w, m_gdn_a_log, m_gdn_dt_bias, m_gdn_norm_g, m_w_out, m_mix_post_g, m_ffn2_pre_g, m_ffn2_w_gate, m_ffn2_w_up, m_ffn2_w_down, m_ffn2_post_g, v_ffn1_pre_g, v_ffn1_w_gate, v_ffn1_w_up, v_ffn1_w_down, v_ffn1_post_g, v_mix_pre_g, v_w_in, v_mla_q_norm_g, v_mla_w_uq, v_mla_kv_norm_g, v_mla_w_ukv, v_mla_out_g, v_gdn_conv_w, v_gdn_a_log, v_gdn_dt_bias, v_gdn_norm_g, v_w_out, v_mix_post_g, v_ffn2_pre_g, v_ffn2_w_gate, v_ffn2_w_up, v_ffn2_w_down, v_ffn2_post_g):
    args = dict(locals())
    wsh = {n: args[n][0] for n in WEIGHTS}
    msh = {n: args["m_" + n][0] if args["m_" + n].ndim == 3 else args["m_" + n] for n in WEIGHTS}
    vsh = {n: args["v_" + n][0] if args["v_" + n].ndim == 3 else args["v_" + n] for n in WEIGHTS}
    for n in SMALL:
        wsh[n] = args[n]
    mix_shapes = [(n, wsh[n].shape) for n in MIX_BIG]

    early = FFN_BIG[:3]
    gathered = _gather_shards([wsh[n].astype(MM_DTYPE) for n in early] + [_pack([wsh[n] for n in MIX_BIG], MM_DTYPE)])
    full = {n: wsh[n] for n in SMALL}
    for n, gw in zip(early, gathered):
        full[n] = gw
    parts = [_unpack(gathered[-1][q], mix_shapes) for q in range(N_SHARD)]
    for n in MIX_BIG:
        full[n] = jnp.concatenate([parts[q][n] for q in range(N_SHARD)], axis=SHARD_AXIS[n])

    core = lax.axis_index("c").astype(jnp.int32).reshape(1)
    chip = (2 * lax.axis_index("x") + lax.axis_index("y")).astype(jnp.int32).reshape(1)
    late = ([wsh[n].astype(MM_DTYPE) for n in FFN2_BIG], core, chip)
    lsum, grad_x, g = _local_step(x[0], positions, loss_target[0], full, late)
    loss = lax.psum(0.5 * jnp.sum(lsum) / x.shape[-1], ("x", "y", "c"))

    mix_quarters = [_pack([jnp.split(g[n], N_SHARD, axis=SHARD_AXIS[n])[q] for n in MIX_BIG], MM_DTYPE)
                    for q in range(N_SHARD)]
    mine = [g[n] for n in early] + [jnp.stack(mix_quarters)]
    got = _swap_halves(mine)
    pairs = [_add_pair("add_pair_%d" % i, gi, gt, core) for i, (gi, gt) in enumerate(zip(mine, got))]
    slabs = _scatter_to_chips(pairs)
    halves = [_add_chips("add_chips_%d" % i, pr, sl, chip) for i, (pr, sl) in enumerate(zip(pairs, slabs))]
    halves = halves[:3] + [g[n] for n in FFN2_BIG] + halves[3:]
    others = _share_halves(halves)
    shared = [_join_halves("join_halves_%d" % i, hm, ho, core) for i, (hm, ho) in enumerate(zip(halves, others))]
    gsh = _unpack(shared[-1], mix_shapes)
    for n, sg_ in zip(FFN_BIG, shared):
        gsh[n] = sg_

    small_shapes = [(n, wsh[n].shape) for n in SMALL]
    pack_small = lambda d: jnp.concatenate(
        [_pad_lanes(d[n].astype(F32), LANES) for n in SMALL] + [jnp.zeros((SMALL_ROWS - len(SMALL), LANES), F32)], axis=0)
    slots = _gather_small(pack_small(g))

    c1 = 1.0 - ADAM_B1 ** ADAM_STEP
    c2 = 1.0 - ADAM_B2 ** ADAM_STEP

    def small_update(wb, mb, vb, s8):
        gs = s8[0:SMALL_ROWS]
        for d in range(1, 8):
            gs = gs + s8[d * SMALL_ROWS:(d + 1) * SMALL_ROWS]
        m2 = ADAM_B1 * mb + (1.0 - ADAM_B1) * gs
        v2 = ADAM_B2 * vb + (1.0 - ADAM_B2) * (gs * gs)
        delta = -ADAM_LR * ((m2 / c1) / (jnp.sqrt(v2 / c2) + ADAM_EPS) + ADAM_WD * wb)
        return gs, delta, m2, v2

    sg, sd, sm, sv_ = _rowwise("adamw_small", small_update,
                               [pack_small(wsh), pack_small(msh), pack_small(vsh)],
                               [slots.reshape(8 * SMALL_ROWS, LANES)], [(LANES, F32)] * 4, [], SMALL_ROWS)
    grads, deltas, new_m, new_v = {}, {}, {}, {}
    for i, (n, shp) in enumerate(small_shapes):
        grads[n], deltas[n] = sg[i:i + 1, :shp[1]], sd[i:i + 1, :shp[1]]
        new_m[n], new_v[n] = sm[i:i + 1, :shp[1]], sv_[i:i + 1, :shp[1]]
    for n in BIG:
        grads[n], deltas[n], new_m[n], new_v[n] = _adamw("adamw_" + n, args[n], gsh[n], args["m_" + n], args["v_" + n],
                                                         _row_tile(args[n].shape[1], 256))

    return (loss, grad_x[None], *[grads[n] for n in WEIGHTS], *[deltas[n] for n in WEIGHTS],
            *[new_m[n] for n in WEIGHTS], *[new_v[n] for n in WEIGHTS])
```

```python
import functools

import jax
import jax.numpy as jnp
from jax import lax
from jax.experimental import pallas as pl
from jax.experimental.pallas import tpu as pltpu

F32 = jnp.float32
BF16 = jnp.bfloat16
MM_DTYPE = BF16
HI = lax.Precision.HIGHEST
MESH = pl.DeviceIdType.MESH

D_MODEL = 1024
D_FF = 2816
N_HEADS = 8
MLA_Q_RANK = 256
MLA_KV_RANK = 128
MLA_NOPE = 64
MLA_ROPE = 32
MLA_V = 64
ROPE_THETA = 10000.0
GDN_DH = 64
GDN_W = N_HEADS * GDN_DH
GDN_CONV = 4
CHUNK = 64
HEAD_LANES = 128
HEADS_PER_STEP = 8
MLA_PAD = N_HEADS * HEAD_LANES
EPS = 1e-6
N_SHARD = 4
LANES = 1024

PIN_QKV = 0
PIN_MLA = 1536
PIN_KPE = 1920
PIN_GATE = 2048
PIN_AB = 2560
PIN_W = 2688
CAT_W = MLA_PAD + GDN_W

ADAM_LR = 0.001
ADAM_B1 = 0.9
ADAM_B2 = 0.999
ADAM_EPS = 1e-08
ADAM_WD = 0.01
ADAM_STEP = 10

VMEM_LIMIT_V7X = 56 * 1024 * 1024

BIG = ["ffn1_w_gate", "ffn1_w_up", "ffn1_w_down", "w_in", "mla_w_uq", "mla_w_ukv", "gdn_conv_w", "w_out",
       "ffn2_w_gate", "ffn2_w_up", "ffn2_w_down"]
FFN_BIG = ["ffn1_w_gate", "ffn1_w_up", "ffn1_w_down", "ffn2_w_gate", "ffn2_w_up", "ffn2_w_down"]
MIX_BIG = ["w_in", "mla_w_uq", "mla_w_ukv", "gdn_conv_w", "w_out"]
SMALL = ["ffn1_pre_g", "ffn1_post_g", "mix_pre_g", "mla_q_norm_g", "mla_kv_norm_g", "mla_out_g", "gdn_a_log",
         "gdn_dt_bias", "gdn_norm_g", "mix_post_g", "ffn2_pre_g", "ffn2_post_g"]
WEIGHTS = ["ffn1_pre_g", "ffn1_w_gate", "ffn1_w_up", "ffn1_w_down", "ffn1_post_g", "mix_pre_g", "w_in",
           "mla_q_norm_g", "mla_w_uq", "mla_kv_norm_g", "mla_w_ukv", "mla_out_g", "gdn_conv_w", "gdn_a_log",
           "gdn_dt_bias", "gdn_norm_g", "w_out", "mix_post_g", "ffn2_pre_g", "ffn2_w_gate", "ffn2_w_up",
           "ffn2_w_down", "ffn2_post_g"]
SHARD_AXIS = {"ffn1_w_gate": 1, "ffn1_w_up": 1, "ffn1_w_down": 0, "w_in": 1, "mla_w_uq": 1, "mla_w_ukv": 1,
              "gdn_conv_w": 1, "w_out": 0, "ffn2_w_gate": 1, "ffn2_w_up": 1, "ffn2_w_down": 0}
SMALL_ROWS = 16


def _params(sem):
    return pltpu.CompilerParams(dimension_semantics=sem, vmem_limit_bytes=VMEM_LIMIT_V7X)


def _pick(dim, pref):
    if dim <= pref:
        return dim
    t = (pref // 128) * 128
    while t >= 128:
        if dim % t == 0:
            return t
        t -= 128
    return dim


ANY_SPEC = pl.BlockSpec(memory_space=pl.ANY)


def _rowwise(name, fn, row_ins, bc_ins, row_outs, acc_outs, tb, wide=None, carry=None):
    ents = []
    for e in row_ins:
        ents.append(e if isinstance(e, tuple) else (e, e.shape[1], 0, 0))
    over = [o[2] for o in row_outs if len(o) == 3]
    rows = over[0] if over else ents[0][0].shape[0]
    steps = rows // tb
    assert steps * tb == rows, (name, rows, tb)
    in_specs, args = [], []
    for a, w, j, r0 in ents:
        in_specs.append(pl.BlockSpec((tb, w), lambda i, j=j, r0=r0: (i + r0, j)))
        args.append(a)
    for b in bc_ins:
        in_specs.append(pl.BlockSpec(b.shape, lambda i: (0, 0)))
        args.append(b)
    n_in = len(args)
    aliases = {}
    if carry is not None:
        in_specs.append(ANY_SPEC)
        args.append(carry)
        aliases = {n_in: 0}
    out_shape = [jax.ShapeDtypeStruct((rows, o[0]), o[1]) for o in row_outs]
    out_specs = [pl.BlockSpec((tb, o[0]), lambda i: (i, 0)) for o in row_outs]
    if wide is not None:
        out_shape[0] = jax.ShapeDtypeStruct((rows, wide[0]), row_outs[0][1])
        out_specs[0] = pl.BlockSpec((tb, row_outs[0][0]), lambda i: (i, wide[1]))
    out_shape += [jax.ShapeDtypeStruct((r, c), F32) for r, c in acc_outs]
    out_specs += [pl.BlockSpec((r, c), lambda i: (0, 0)) for r, c in acc_outs]
    n_ro, n_acc, n_args = len(row_outs), len(acc_outs), len(args)

    def body(*refs):
        vals = fn(*[r[...] for r in refs[:n_in]])
        if not isinstance(vals, (tuple, list)):
            vals = (vals,)
        for r, v in zip(refs[n_args:n_args + n_ro], vals[:n_ro]):
            r[...] = v.astype(r.dtype)
        if n_acc:
            acc_refs = refs[n_args + n_ro:]

            @pl.when(pl.program_id(0) == 0)
            def _():
                for r in acc_refs:
                    r[...] = jnp.zeros(r.shape, r.dtype)

            for r, v in zip(acc_refs, vals[n_ro:]):
                r[...] += v

    outs = pl.pallas_call(body, name=name, grid=(steps,), in_specs=in_specs, out_specs=out_specs,
                          out_shape=out_shape, input_output_aliases=aliases,
                          compiler_params=_params(("arbitrary",)))(*args)
    return outs


def _mm(name, a, b, mode, out_dtype, tm=512, tn=512, tk=1024):
    if mode == "nn":
        (m, k), (k2, n) = a.shape, b.shape
    elif mode == "nt":
        (m, k), (n, k2) = a.shape, b.shape
    else:
        (k, m), (k2, n) = a.shape, b.shape
    assert k == k2, (name, a.shape, b.shape)
    tm, tn, tk = _pick(m, tm), _pick(n, tn), _pick(k, tk)
    nk = k // tk
    if mode == "nn":
        a_spec = pl.BlockSpec((tm, tk), lambda i, j, kk: (i, kk))
        b_spec = pl.BlockSpec((tk, tn), lambda i, j, kk: (kk, j))
        dims = (((1,), (0,)), ((), ()))
    elif mode == "nt":
        a_spec = pl.BlockSpec((tm, tk), lambda i, j, kk: (i, kk))
        b_spec = pl.BlockSpec((tn, tk), lambda i, j, kk: (j, kk))
        dims = (((1,), (1,)), ((), ()))
    else:
        a_spec = pl.BlockSpec((tk, tm), lambda i, j, kk: (kk, i))
        b_spec = pl.BlockSpec((tk, tn), lambda i, j, kk: (kk, j))
        dims = (((0,), (0,)), ((), ()))

    def body(a_ref, b_ref, o_ref, acc_ref):
        kk = pl.program_id(2)

        @pl.when(kk == 0)
        def _():
            acc_ref[...] = jnp.zeros(acc_ref.shape, F32)

        acc_ref[...] += lax.dot_general(a_ref[...].astype(MM_DTYPE), b_ref[...].astype(MM_DTYPE), dims,
                                        preferred_element_type=F32)

        @pl.when(kk == nk - 1)
        def _():
            o_ref[...] = acc_ref[...].astype(o_ref.dtype)

    return pl.pallas_call(
        body, name=name, grid=(m // tm, n // tn, nk), in_specs=[a_spec, b_spec],
        out_specs=pl.BlockSpec((tm, tn), lambda i, j, kk: (i, j)),
        out_shape=jax.ShapeDtypeStruct((m, n), out_dtype),
        scratch_shapes=[pltpu.VMEM((tm, tn), F32)],
        compiler_params=_params(("parallel", "parallel", "arbitrary")))(a, b)


def _rms_stats(x, n_real=None):
    n = x.shape[-1] if n_real is None else n_real
    return lax.rsqrt(jnp.sum(x * x, axis=-1, keepdims=True) / n + EPS)


def _rms_bwd(x, r, g, dz, n_real=None):
    n = x.shape[-1] if n_real is None else n_real
    xh = x * r
    dxh = dz * g
    dx = r * (dxh - xh * (jnp.sum(dxh * xh, axis=-1, keepdims=True) / n))
    return dx, jnp.sum(dz * xh, axis=0, keepdims=True)


def _sigmoid(x):
    return 0.5 * jnp.tanh(0.5 * x) + 0.5


def _roll(x, s, axis):
    return pltpu.roll(x, s, axis)


def _rope(x, c, s1, s2):
    return x * c + _roll(x, HEAD_LANES - MLA_ROPE // 2, 1) * s1 + _roll(x, MLA_ROPE // 2, 1) * s2


def _heads_apply(x, fn):
    return jnp.concatenate([fn(x[:, h * HEAD_LANES:(h + 1) * HEAD_LANES]) for h in range(N_HEADS)], axis=1)


ROW_CHUNK = 256


def _row_chunks(rows):
    step = min(ROW_CHUNK, rows)
    return [pl.ds(r, step) for r in range(0, rows, step)]


def _ffn_fwd(tag, x, g_pre, wg, wu, wd, g_post, tm):
    t, d = x.shape
    ns, _, fs = wg.shape
    nt = t // tm
    row = pl.BlockSpec((tm, d), lambda i, q: (i, 0))
    vec = pl.BlockSpec((1, d), lambda i, q: (0, 0))
    act3 = pl.BlockSpec((1, tm, fs), lambda i, q: (q, i, 0))
    wcol = pl.BlockSpec((1, d, fs), lambda i, q: (q, 0, 0))
    wrow = pl.BlockSpec((1, fs, d), lambda i, q: (q, 0, 0))

    def gate_up(x_ref, g_ref, wg_ref, wu_ref, n_ref, a_ref, u_ref, s_ref, n_s):
        @pl.when(pl.program_id(1) == 0)
        def _():
            for r in _row_chunks(tm):
                xb = x_ref[r, :]
                n_s[r, :] = (xb * _rms_stats(xb) * g_ref[...]).astype(MM_DTYPE)
            n_ref[...] = n_s[...]

        for r in _row_chunks(tm):
            n = n_s[r, :]
            a = jnp.dot(n, wg_ref[0], preferred_element_type=F32)
            u = jnp.dot(n, wu_ref[0], preferred_element_type=F32)
            a_ref[0, r, :] = a.astype(a_ref.dtype)
            u_ref[0, r, :] = u.astype(u_ref.dtype)
            s_ref[0, r, :] = ((a * _sigmoid(a)) * u).astype(s_ref.dtype)

    n, a, u, s = pl.pallas_call(
        gate_up, name=tag + "_gate_up", grid=(nt, ns), in_specs=[row, vec, wcol, wcol],
        out_specs=[row, act3, act3, act3],
        out_shape=[jax.ShapeDtypeStruct((t, d), MM_DTYPE)] + [jax.ShapeDtypeStruct((ns, t, fs), MM_DTYPE)] * 3,
        scratch_shapes=[pltpu.VMEM((tm, d), MM_DTYPE)],
        compiler_params=_params(("parallel", "arbitrary")))(x, g_pre, wg, wu)

    def down(s_ref, wd_ref, x_ref, g_ref, h_ref, y_ref, acc):
        q = pl.program_id(1)

        @pl.when(q == 0)
        def _():
            acc[...] = jnp.zeros(acc.shape, F32)

        for r in _row_chunks(tm):
            acc[r, :] += jnp.dot(s_ref[0, r, :], wd_ref[0], preferred_element_type=F32)

        @pl.when(q == ns - 1)
        def _():
            for r in _row_chunks(tm):
                hb = acc[r, :]
                h_ref[r, :] = hb
                y_ref[r, :] = x_ref[r, :] + 0.5 * (hb * _rms_stats(hb) * g_ref[...])

    h, y = pl.pallas_call(
        down, name=tag + "_down", grid=(nt, ns), in_specs=[act3, wrow, row, vec], out_specs=[row, row],
        out_shape=[jax.ShapeDtypeStruct((t, d), F32)] * 2, scratch_shapes=[pltpu.VMEM((tm, d), F32)],
        compiler_params=_params(("parallel", "arbitrary")))(s, wd, x, g_post)
    return y, (x, n, a, u, s, h)


def _carry(body, n_in, n_out, grid, carried):
    if carried is None:
        return body, [], [], [], [], []
    nx_in, nx_out = len(carried.ins), len(carried.outs)

    def wrapped(*refs):
        ins, rest = refs[:n_in], refs[n_in:]
        xi, rest = rest[:nx_in], rest[nx_in:]
        outs, rest = rest[:n_out], rest[n_out:]
        xo, rest = rest[:nx_out], rest[nx_out:]
        scr, sems = rest[:len(rest) - 2], rest[len(rest) - 2:]
        first, last = True, True
        for dim, size in enumerate(grid):
            first = first & (pl.program_id(dim) == 0)
            last = last & (pl.program_id(dim) == size - 1)

        @pl.when(first)
        def _():
            carried.start(xi, xo, *sems)

        body(*ins, *outs, *scr)

        @pl.when(last)
        def _():
            carried.finish(xi, xo, *sems)

    sems = [pltpu.SemaphoreType.DMA((carried.n_sem,)), pltpu.SemaphoreType.DMA((carried.n_sem,))]
    return (wrapped, [HBM_SPEC] * nx_in, [HBM_SPEC] * nx_out, list(carried.outs), sems, list(carried.ins))


def _ffn_bwd(tag, dy, saved, g_pre, wg, wu, wd, g_post, tm, tk, carried_down=None, make_carried_up=None):
    x, n, a, u, s, h = saved
    t, d = x.shape
    ns, _, fs = wg.shape
    nt, nk = t // tm, t // tk
    row = pl.BlockSpec((tm, d), lambda i, q: (i, 0))
    vec = pl.BlockSpec((1, d), lambda i, q: (0, 0))
    act3 = pl.BlockSpec((1, tm, fs), lambda i, q: (q, i, 0))
    wcol = pl.BlockSpec((1, d, fs), lambda i, q: (q, 0, 0))
    wrow = pl.BlockSpec((1, fs, d), lambda i, q: (q, 0, 0))
    nt_dims = (((1,), (1,)), ((), ()))
    tn_dims = (((0,), (0,)), ((), ()))

    def down_b(h_ref, dy_ref, g_ref, wd_ref, a_ref, u_ref, dh_ref, da_ref, du_ref, dg_ref, dh_s):
        i, q = pl.program_id(0), pl.program_id(1)

        @pl.when((i == 0) & (q == 0))
        def _():
            dg_ref[...] = jnp.zeros(dg_ref.shape, F32)

        @pl.when(q == 0)
        def _():
            for r in _row_chunks(tm):
                hb = h_ref[r, :]
                dh, dg = _rms_bwd(hb, _rms_stats(hb), g_ref[...], 0.5 * dy_ref[r, :])
                dh_s[r, :] = dh.astype(MM_DTYPE)
                dg_ref[...] += dg
            dh_ref[...] = dh_s[...]

        for r in _row_chunks(tm):
            ds = lax.dot_general(dh_s[r, :], wd_ref[0], nt_dims, preferred_element_type=F32)
            ab, ub = a_ref[0, r, :].astype(F32), u_ref[0, r, :].astype(F32)
            sg = _sigmoid(ab)
            da_ref[0, r, :] = (ds * ub * (sg * (1.0 + ab * (1.0 - sg)))).astype(da_ref.dtype)
            du_ref[0, r, :] = (ds * (ab * sg)).astype(du_ref.dtype)

    down_b, x_in, x_out, x_shape, x_scr, x_args = _carry(down_b, 6, 4, (nt, ns), carried_down)
    dh, da, du, dg_post, *from_down = pl.pallas_call(
        down_b, name=tag + "_down_b", grid=(nt, ns), in_specs=[row, row, vec, wrow, act3, act3] + x_in,
        out_specs=[row, act3, act3, vec] + x_out,
        out_shape=[jax.ShapeDtypeStruct((t, d), MM_DTYPE)] + [jax.ShapeDtypeStruct((ns, t, fs), MM_DTYPE)] * 2
        + [jax.ShapeDtypeStruct((1, d), F32)] + x_shape,
        scratch_shapes=[pltpu.VMEM((tm, d), MM_DTYPE)] + x_scr,
        compiler_params=_params(("arbitrary", "arbitrary")))(h, dy, g_post, wd, a, u, *x_args)

    def down_w(s_ref, dh_ref, dw_ref, acc):
        kk = pl.program_id(1)

        @pl.when(kk == 0)
        def _():
            acc[...] = jnp.zeros(acc.shape, F32)

        acc[...] += lax.dot_general(s_ref[0], dh_ref[...], tn_dims, preferred_element_type=F32)

        @pl.when(kk == nk - 1)
        def _():
            dw_ref[0] = acc[...].astype(dw_ref.dtype)

    dwd = pl.pallas_call(
        down_w, name=tag + "_down_w", grid=(ns, nk),
        in_specs=[pl.BlockSpec((1, tk, fs), lambda q, kk: (q, kk, 0)), pl.BlockSpec((tk, d), lambda q, kk: (kk, 0))],
        out_specs=pl.BlockSpec((1, fs, d), lambda q, kk: (q, 0, 0)),
        out_shape=jax.ShapeDtypeStruct((ns, fs, d), MM_DTYPE), scratch_shapes=[pltpu.VMEM((fs, d), F32)],
        compiler_params=_params(("parallel", "arbitrary")))(s, dh)

    def gate_up_b(da_ref, du_ref, wg_ref, wu_ref, x_ref, dy_ref, g_ref, dx_ref, dg_ref, acc):
        i, q = pl.program_id(0), pl.program_id(1)

        @pl.when((i == 0) & (q == 0))
        def _():
            dg_ref[...] = jnp.zeros(dg_ref.shape, F32)

        @pl.when(q == 0)
        def _():
            acc[...] = jnp.zeros(acc.shape, F32)

        for r in _row_chunks(tm):
            acc[r, :] += (lax.dot_general(da_ref[0, r, :], wg_ref[0], nt_dims, preferred_element_type=F32)
                          + lax.dot_general(du_ref[0, r, :], wu_ref[0], nt_dims, preferred_element_type=F32))

        @pl.when(q == ns - 1)
        def _():
            for r in _row_chunks(tm):
                xb = x_ref[r, :]
                dx, dg = _rms_bwd(xb, _rms_stats(xb), g_ref[...], acc[r, :])
                dx_ref[r, :] = dy_ref[r, :] + dx
                dg_ref[...] += dg

    def gate_up_w(n_ref, da_ref, du_ref, dwg_ref, dwu_ref, acc_g, acc_u):
        kk = pl.program_id(1)

        @pl.when(kk == 0)
        def _():
            acc_g[...] = jnp.zeros(acc_g.shape, F32)
            acc_u[...] = jnp.zeros(acc_u.shape, F32)

        nb = n_ref[...]
        acc_g[...] += lax.dot_general(nb, da_ref[0], tn_dims, preferred_element_type=F32)
        acc_u[...] += lax.dot_general(nb, du_ref[0], tn_dims, preferred_element_type=F32)

        @pl.when(kk == nk - 1)
        def _():
            dwg_ref[0] = acc_g[...].astype(dwg_ref.dtype)
            dwu_ref[0] = acc_u[...].astype(dwu_ref.dtype)

    k3 = pl.BlockSpec((1, tk, fs), lambda q, kk: (q, kk, 0))
    wout = pl.BlockSpec((1, d, fs), lambda q, kk: (q, 0, 0))
    dwg, dwu = pl.pallas_call(
        gate_up_w, name=tag + "_gate_up_w", grid=(ns, nk),
        in_specs=[pl.BlockSpec((tk, d), lambda q, kk: (kk, 0)), k3, k3], out_specs=[wout, wout],
        out_shape=[jax.ShapeDtypeStruct((ns, d, fs), MM_DTYPE)] * 2,
        scratch_shapes=[pltpu.VMEM((d, fs), F32)] * 2,
        compiler_params=_params(("parallel", "arbitrary")))(n, da, du)

    carried_up = make_carried_up(dwg, dwu, dwd) if make_carried_up else None
    gate_up_b, x_in, x_out, x_shape, x_scr, x_args = _carry(gate_up_b, 7, 2, (nt, ns), carried_up)
    dx, dg_pre, *from_up = pl.pallas_call(
        gate_up_b, name=tag + "_gate_up_b", grid=(nt, ns), in_specs=[act3, act3, wcol, wcol, row, row, vec] + x_in,
        out_specs=[row, vec] + x_out,
        out_shape=[jax.ShapeDtypeStruct((t, d), F32), jax.ShapeDtypeStruct((1, d), F32)] + x_shape,
        scratch_shapes=[pltpu.VMEM((tm, d), F32)] + x_scr,
        compiler_params=_params(("arbitrary", "arbitrary")))(da, du, wg, wu, x, dy, g_pre, *x_args)
    return dx, dg_pre, dwg, dwu, dwd, dg_post, from_down, from_up


NEG = -1e30


def _attn_scale():
    return (MLA_NOPE + MLA_ROPE) ** -0.5


def _causal_pairs(nq, by_key):
    if by_key:
        pairs = [(qi, ki) for ki in range(nq) for qi in range(ki, nq)]
    else:
        pairs = [(qi, ki) for qi in range(nq) for ki in range(qi + 1)]
    return jnp.asarray([p[0] for p in pairs], jnp.int32), jnp.asarray([p[1] for p in pairs], jnp.int32)


def _below_diagonal(shape):
    return lax.broadcasted_iota(jnp.int32, shape, 1) <= lax.broadcasted_iota(jnp.int32, shape, 0)


def _attn_call(name, body, tables, args, in_kinds, out_kinds, scratch, t, tq, carried=None):
    qmap = lambda h, p, qt, kt: (qt[p], h)
    kmap = lambda h, p, qt, kt: (kt[p], h)
    width = HEADS_PER_STEP * HEAD_LANES
    spec = lambda kind: pl.BlockSpec((tq, width), qmap if kind == "q" else kmap)
    n_pairs = tables[0].shape[0]
    n_groups = N_HEADS // HEADS_PER_STEP
    n_in, n_out, n_scr = len(in_kinds), len(out_kinds), scratch
    x_ins = list(carried.ins) if carried else []
    x_outs = list(carried.outs) if carried else []
    x_scr = [pltpu.SemaphoreType.DMA((carried.n_sem,)), pltpu.SemaphoreType.DMA((carried.n_sem,))] if carried else []

    def full_body(qt, kt, *refs):
        ins, refs = refs[:n_in], refs[n_in:]
        xi, refs = refs[:len(x_ins)], refs[len(x_ins):]
        outs, refs = refs[:n_out], refs[n_out:]
        xo, refs = refs[:len(x_outs)], refs[len(x_outs):]
        scr, sems = refs[:n_scr], refs[n_scr:]
        if carried:
            @pl.when((pl.program_id(0) == 0) & (pl.program_id(1) == 0))
            def _():
                carried.start(xi, xo, *sems)

        heads = [tuple(r.at[:, pl.ds(hh * HEAD_LANES, HEAD_LANES)] for r in (*ins, *outs, *scr))
                 for hh in range(HEADS_PER_STEP)]
        body(qt, kt, heads)
        if carried:
            @pl.when((pl.program_id(0) == n_groups - 1) & (pl.program_id(1) == n_pairs - 1))
            def _():
                carried.finish(xi, xo, *sems)

    grid_spec = pltpu.PrefetchScalarGridSpec(
        num_scalar_prefetch=2, grid=(n_groups, n_pairs),
        in_specs=[spec(kd) for kd in in_kinds] + [HBM_SPEC] * len(x_ins),
        out_specs=[spec(kd) for kd in out_kinds] + [HBM_SPEC] * len(x_outs),
        scratch_shapes=[pltpu.VMEM((tq, width), F32)] * n_scr + x_scr)
    return pl.pallas_call(full_body, name=name, grid_spec=grid_spec,
                          out_shape=[jax.ShapeDtypeStruct((t, MLA_PAD), F32) for _ in out_kinds] + x_outs,
                          compiler_params=_params(("arbitrary", "arbitrary")))(*tables, *args, *x_ins)


class _Carried:
    def __init__(self, ins, outs, n_sem, start, finish):
        self.ins, self.outs, self.n_sem, self.start, self.finish = ins, outs, n_sem, start, finish


def _attn_fwd(q, k, v, tq, carried=None):
    t = q.shape[0]
    nq = t // tq

    def body(qt, kt, heads):
        p_id = pl.program_id(1)
        qi, ki = qt[p_id], kt[p_id]

        @pl.when(ki == 0)
        def _():
            for _, _, _, _, _, m_s, l_s, acc_s in heads:
                m_s[...] = jnp.full(m_s.shape, NEG, F32)
                l_s[...] = jnp.zeros(l_s.shape, F32)
                acc_s[...] = jnp.zeros(acc_s.shape, F32)

        def update(diagonal):
            for q_ref, k_ref, v_ref, _, _, m_s, l_s, acc_s in heads:
                s = lax.dot_general(q_ref[...], k_ref[...], (((1,), (1,)), ((), ())), preferred_element_type=F32)
                if diagonal:
                    s = jnp.where(_below_diagonal(s.shape), s, NEG)
                m_old = m_s[...]
                m_new = jnp.maximum(m_old, jnp.max(s, axis=1, keepdims=True))
                alpha = jnp.exp(m_old - m_new)
                p = jnp.exp(s - m_new[:, :1])
                l_s[...] = l_s[...] * alpha + jnp.sum(p, axis=1, keepdims=True)
                acc_s[...] = acc_s[...] * alpha + jnp.dot(p.astype(MM_DTYPE), v_ref[...], preferred_element_type=F32)
                m_s[...] = m_new

        @pl.when(ki < qi)
        def _():
            update(False)

        @pl.when(ki == qi)
        def _():
            update(True)
            for _, _, _, o_ref, lse_ref, m_s, l_s, acc_s in heads:
                o_ref[...] = acc_s[...] / l_s[...]
                lse_ref[...] = m_s[...] + jnp.log(l_s[...])

    return _attn_call("mla_attn_fwd", body, _causal_pairs(nq, False), (q, k, v), "qkk", "qq", 3, t, tq, carried)


def _attn_probs(q, k, lse, diagonal):
    s = lax.dot_general(q, k, (((1,), (1,)), ((), ())), preferred_element_type=F32)
    p = jnp.exp(s - lse[:, :1])
    return jnp.where(_below_diagonal(s.shape), p, 0.0) if diagonal else p


def _attn_bwd_dq(q, k, v, do, lse, delta, tq):
    t = q.shape[0]
    nq = t // tq

    def body(qt, kt, heads):
        p_id = pl.program_id(1)
        qi, ki = qt[p_id], kt[p_id]

        @pl.when(ki == 0)
        def _():
            for refs in heads:
                refs[-1][...] = jnp.zeros(refs[-1].shape, F32)

        def step(diagonal):
            for q_ref, k_ref, v_ref, do_ref, lse_ref, dl_ref, _, acc_s in heads:
                p = _attn_probs(q_ref[...], k_ref[...], lse_ref[...], diagonal)
                dp = lax.dot_general(do_ref[...], v_ref[...], (((1,), (1,)), ((), ())), preferred_element_type=F32)
                ds = p * (dp - dl_ref[...][:, :1])
                acc_s[...] += jnp.dot(ds.astype(MM_DTYPE), k_ref[...], preferred_element_type=F32)

        @pl.when(ki < qi)
        def _():
            step(False)

        @pl.when(ki == qi)
        def _():
            step(True)
            for refs in heads:
                refs[-2][...] = refs[-1][...]

    return _attn_call("mla_attn_bwd_dq", body, _causal_pairs(nq, False), (q, k, v, do, lse, delta), "qkkqqq", "q",
                      1, t, tq)[0]


def _attn_bwd_dkv(q, k, v, do, lse, delta, tq, carried=None):
    t = q.shape[0]
    nq = t // tq

    def body(qt, kt, heads):
        p_id = pl.program_id(1)
        qi, ki = qt[p_id], kt[p_id]

        def step(diagonal):
            for q_ref, k_ref, v_ref, do_ref, lse_ref, dl_ref, _, _, dk_s, dv_s in heads:
                p = _attn_probs(q_ref[...], k_ref[...], lse_ref[...], diagonal)
                dv_s[...] += lax.dot_general(p.astype(MM_DTYPE), do_ref[...], (((0,), (0,)), ((), ())),
                                             preferred_element_type=F32)
                dp = lax.dot_general(do_ref[...], v_ref[...], (((1,), (1,)), ((), ())), preferred_element_type=F32)
                ds = p * (dp - dl_ref[...][:, :1])
                dk_s[...] += lax.dot_general(ds.astype(MM_DTYPE), q_ref[...], (((0,), (0,)), ((), ())),
                                             preferred_element_type=F32)

        @pl.when(qi == ki)
        def _():
            for refs in heads:
                refs[-2][...] = jnp.zeros(refs[-2].shape, F32)
                refs[-1][...] = jnp.zeros(refs[-1].shape, F32)
            step(True)

        @pl.when(qi > ki)
        def _():
            step(False)

        @pl.when(qi == nq - 1)
        def _():
            for refs in heads:
                refs[-4][...] = refs[-2][...]
                refs[-3][...] = refs[-1][...]

    return _attn_call("mla_attn_bwd_dkv", body, _causal_pairs(nq, True), (q, k, v, do, lse, delta), "qkkqqq", "kk",
                      2, t, tq, carried)


def _dotf(a, b, dims=(((1,), (0,)), ((), ()))):
    return lax.dot_general(a, b, dims, preferred_element_type=F32, precision=HI)


def _dot1(a, b, dims=(((1,), (0,)), ((), ()))):
    return lax.dot_general(a.astype(MM_DTYPE), b.astype(MM_DTYPE), dims, preferred_element_type=F32)


def _dot3(a, b, dims=(((1,), (0,)), ((), ()))):
    return lax.dot_general(a, b, dims, preferred_element_type=F32, precision=lax.Precision.HIGH)


NN3 = (((2,), (1,)), ((0,), (0,)))
NT3 = (((2,), (2,)), ((0,), (0,)))
TN3 = (((1,), (1,)), ((0,), (0,)))


def _tri_masks(nh):
    shape = (nh, CHUNK, CHUNK)
    return lax.broadcasted_iota(jnp.int32, shape, 1), lax.broadcasted_iota(jnp.int32, shape, 2)


def _gdn_chunk_common(k, gcc, bb, row, col, dot=_dot1):
    tril = row >= col
    gcr = jnp.swapaxes(gcc, 1, 2)
    dm = jnp.exp(jnp.where(tril, gcc - gcr, NEG))
    kb = k * bb
    lm = jnp.where(row > col, dot(kb, k, NT3) * dm, 0.0)
    return dm, kb, lm


def _unit_lower_inverse(lm, eye):
    t = eye - lm
    p = lm
    for _ in range(CHUNK.bit_length() - 2):
        p = _dot3(p, p, NN3)
        t = t + _dot3(t, p, NN3)
    return t


def _chunk_sum_matrix(tb, upper):
    r = lax.broadcasted_iota(jnp.int32, (tb, tb), 0)
    c = lax.broadcasted_iota(jnp.int32, (tb, tb), 1)
    same = (r // CHUNK) == (c // CHUNK)
    return (same & ((c >= r) if upper else (c <= r))).astype(F32)


def _gdn_fwd(q, k, v, gb, bb):
    nh, t, dh = q.shape
    nchunk = t // CHUNK

    def body(q_ref, k_ref, v_ref, g_ref, b_ref, o_ref, sall_ref, tall_ref, s_s):
        @pl.when(pl.program_id(0) == 0)
        def _():
            s_s[...] = jnp.zeros(s_s.shape, F32)

        row, col = _tri_masks(nh)
        qh, kh, vh, bbh, gcc = q_ref[...], k_ref[...], v_ref[...], b_ref[...], g_ref[...]
        dm, kb, lm = _gdn_chunk_common(kh, gcc, bbh, row, col)
        eg = jnp.exp(gcc)
        glr = gcc[:, CHUNK - 1:CHUNK, :]
        th = _unit_lower_inverse(lm, (row == col).astype(F32))
        w = _dot1(th, kb * eg, NN3)
        u = _dot1(th, vh * bbh, NN3)
        at = jnp.where(row >= col, _dot1(qh, kh, NT3) * dm, 0.0)
        sh = s_s[...]
        vn = u - _dot1(w, sh, NN3)
        o_ref[...] = _dot1(qh * eg, sh, NN3) + _dot1(at, vn, NN3)
        kd = kh * jnp.exp(glr - gcc)
        sall_ref[:, 0] = sh
        tall_ref[...] = th
        s_s[...] = sh * jnp.exp(glr) + _dot1(kd, vn, TN3)

    blk = pl.BlockSpec((nh, CHUNK, dh), lambda n: (0, n, 0))
    return pl.pallas_call(
        body, name="gdn_fwd", grid=(nchunk,), in_specs=[blk] * 5,
        out_specs=[blk, pl.BlockSpec((nh, 1, dh, dh), lambda n: (0, n, 0, 0)), blk],
        out_shape=[jax.ShapeDtypeStruct((nh, t, dh), F32), jax.ShapeDtypeStruct((nh, nchunk, dh, dh), F32),
                   jax.ShapeDtypeStruct((nh, t, CHUNK), F32)],
        scratch_shapes=[pltpu.VMEM((nh, dh, dh), F32)],
        compiler_params=_params(("arbitrary",)))(q, k, v, gb, bb)


def _gdn_bwd(q, k, v, gb, bb, sall, tall, do):
    nh, t, dh = q.shape
    nchunk = t // CHUNK

    def body(q_ref, k_ref, v_ref, g_ref, b_ref, sall_ref, tall_ref, do_ref,
             dq_ref, dk_ref, dv_ref, dg_ref, db_ref, ds_s):
        @pl.when(pl.program_id(0) == 0)
        def _():
            ds_s[...] = jnp.zeros(ds_s.shape, F32)

        row, col = _tri_masks(nh)
        tril, stril = row >= col, row > col
        rsum = lambda x: jnp.sum(x, axis=2, keepdims=True)
        qh, kh, vh, gcc, bbh = q_ref[...], k_ref[...], v_ref[...], g_ref[...], b_ref[...]
        sh, th, doh, dsp = sall_ref[:, 0], tall_ref[...], do_ref[...], ds_s[...]
        dm, kb, lm = _gdn_chunk_common(kh, gcc, bbh, row, col, _dot3)
        eg = jnp.exp(gcc)
        glr = gcc[:, CHUNK - 1:CHUNK, :]
        glv = jnp.exp(glr)
        egl = jnp.exp(glr - gcc)
        rw, ru = kb * eg, vh * bbh
        w, u = _dot3(th, rw, NN3), _dot3(th, ru, NN3)
        at = jnp.where(tril, _dot3(qh, kh, NT3) * dm, 0.0)
        qd, kd = qh * eg, kh * egl
        vn = u - _dot3(w, sh, NN3)
        dgl = jnp.sum(rsum(dsp * sh), axis=1, keepdims=True)
        dkd = _dot3(vn, dsp, NT3)
        dvn = _dot3(kd, dsp, NN3)
        dqd = _dot3(doh, sh, NT3)
        dat = jnp.where(tril, _dot3(doh, vn, NT3), 0.0)
        dvn = dvn + _dot3(at, doh, TN3)
        dw = -_dot3(dvn, sh, NT3)
        ds_s[...] = dsp * glv + _dot3(qd, doh, TN3) - _dot3(w, dvn, TN3)
        dpa = dat * dm
        dq_ref[...] = _dot3(dpa, kh, NN3) + dqd * eg
        dk = _dot3(dpa, qh, TN3) + dkd * egl
        t6 = rsum(dkd * kd)
        dgam = rsum(dqd * qd) - t6
        dgam_last = jnp.sum(t6, axis=1, keepdims=True) + dgl * glv
        drw = _dot3(th, dw, TN3)
        dru = _dot3(th, dvn, TN3)
        dl = -jnp.where(stril, _dot3(drw, w, NT3) + _dot3(dru, u, NT3), 0.0)
        dgam = dgam + rsum(drw * rw)
        dv_ref[...] = dru * bbh
        dp2 = dl * dm
        dkb = drw * eg + _dot3(dp2, kh, NN3)
        dk_ref[...] = dk + _dot3(dp2, kb, TN3) + dkb * bbh
        db_ref[...] = rsum(dru * vh) + rsum(dkb * kh) + jnp.zeros((nh, CHUNK, dh), F32)
        e = dat * at + dl * lm
        dgam_b = dgam + rsum(e) - _dotf(e, jnp.ones((nh, CHUNK, CHUNK), F32), TN3)
        dgam_b = dgam_b + jnp.where(row == CHUNK - 1, dgam_last, 0.0)
        dg_ref[...] = dgam_b

    rev = lambda n: (0, nchunk - 1 - n, 0)
    blk = pl.BlockSpec((nh, CHUNK, dh), rev)
    sblk = pl.BlockSpec((nh, 1, dh, dh), lambda n: (0, nchunk - 1 - n, 0, 0))
    out = jax.ShapeDtypeStruct((nh, t, dh), F32)
    return pl.pallas_call(
        body, name="gdn_bwd", grid=(nchunk,), in_specs=[blk] * 5 + [sblk, blk, blk], out_specs=[blk] * 5,
        out_shape=[out] * 5, scratch_shapes=[pltpu.VMEM((nh, dh, dh), F32)],
        compiler_params=_params(("arbitrary",)))(q, k, v, gb, bb, sall, tall, do)


def _group_ones():
    r = lax.broadcasted_iota(jnp.int32, (GDN_W, GDN_W), 0) // GDN_DH
    c = lax.broadcasted_iota(jnp.int32, (GDN_W, GDN_W), 1) // GDN_DH
    return (r == c).astype(F32)


def _conv_taps(x, xprev, w, has_prev):
    row = lax.broadcasted_iota(jnp.int32, x.shape, 0)
    out = x * w[GDN_CONV - 1:GDN_CONV, :]
    for s in range(1, GDN_CONV):
        sh = jnp.where(row >= s, _roll(x, s, 0), _roll(xprev, s, 0) * has_prev)
        out = out + sh * w[GDN_CONV - 1 - s:GDN_CONV - s, :]
    return out


def _head_cols(x, h):
    return x[:, h * GDN_DH:(h + 1) * GDN_DH]


def _heads_spec(tb):
    return pl.BlockSpec((N_HEADS, tb, GDN_DH), lambda i: (0, i, 0))


def _mixer_fwd(x, positions, w, tb, carried=None):
    t, d = x.shape
    tables = _rope_tables(positions)

    def pre(xb, g):
        return (xb * _rms_stats(xb) * g,)

    (hn,) = _rowwise("mix_pre", pre, [x], [w["mix_pre_g"]], [(d, BF16)], [], tb)
    proj = _mm("mix_in", hn, w["w_in_pad"], "nn", F32)

    def mla_pre(p0, gq, gkv):
        cq, ckv = p0[:, :MLA_Q_RANK], p0[:, MLA_Q_RANK:MLA_Q_RANK + MLA_KV_RANK]
        return cq * _rms_stats(cq) * gq, ckv * _rms_stats(ckv) * gkv

    nq, nkv = _rowwise("mla_pre", mla_pre, [(proj, 512, PIN_MLA // 512, 0)],
                       [w["mla_q_norm_g"], w["mla_kv_norm_g"]], [(MLA_Q_RANK, BF16), (MLA_KV_RANK, BF16)], [], tb)
    qraw = _mm("mla_uq", nq, w["w_uq_pad"], "nn", F32)
    kv = _mm("mla_ukv", nkv, w["w_kv_pad"], "nn", F32)

    def rope_f(qr, kn, vv, kpe, c, s1, s2):
        qo = _heads_apply(qr, lambda xh: _rope(xh, c, s1, s2)) * _attn_scale()
        kp = _rope(kpe, c, s1, s2)
        return qo, kn + jnp.tile(kp, (1, N_HEADS)), vv

    q, k, v = _rowwise("mla_rope", rope_f,
                       [qraw, (kv, MLA_PAD, 0, 0), (kv, MLA_PAD, 1, 0), (proj, HEAD_LANES, PIN_KPE // HEAD_LANES, 0),
                        tables[0], tables[1], tables[2]], [],
                       [(MLA_PAD, BF16)] * 3, [], tb // 2)
    tq = min(512, t)
    o, lse, *carried_out = _attn_fwd(q, k, v, tq, carried)

    def mla_post(ob, g):
        return (ob * _rms_stats(ob, N_HEADS * MLA_V) * g,)

    (cat,) = _rowwise("mla_post", mla_post, [o], [w["mla_out_g_pad"]], [(MLA_PAD, BF16)], [], tb, wide=(CAT_W, 0))

    gones = _group_ones()
    steps = t // tb

    def gdn_pre(xq, xk, xv, pq, pk, pv, cw, go, has_prev):
        outs = []
        for j, (xc, xp) in enumerate(((xq, pq), (xk, pk), (xv, pv))):
            c = _conv_taps(xc, xp, cw[:, j * GDN_W:(j + 1) * GDN_W], has_prev)
            a = c * _sigmoid(c)
            if j < 2:
                rn = lax.rsqrt(_dotf(a * a, go) + EPS)
                a = a * rn
                if j == 0:
                    a = a * (GDN_DH ** -0.5)
            outs.append(a)
        return tuple(outs)

    qh, kh, vh = _gdn_pre_call("gdn_pre", gdn_pre, proj, w["conv_w"], gones, tb, steps)
    heads_shape = jax.ShapeDtypeStruct((N_HEADS, t, GDN_DH), F32)
    lanes_shape = jax.ShapeDtypeStruct((t, HEAD_LANES), F32)
    lanes_spec = pl.BlockSpec((tb, HEAD_LANES), lambda i: (i, 0))
    vec_spec = lambda n: pl.BlockSpec((1, n), lambda i: (0, 0))

    def gate_f(ab_ref, al_ref, dt_ref, g_ref, b_ref, gh_ref, bh_ref):
        g, b = _gb_fwd(ab_ref[...], al_ref[...], dt_ref[...])
        g_ref[...] = g
        b_ref[...] = b
        gc = _dotf(_chunk_sum_matrix(tb, False), g)
        for h in range(N_HEADS):
            gh_ref[h] = jnp.broadcast_to(gc[:, h:h + 1], (tb, GDN_DH))
            bh_ref[h] = jnp.broadcast_to(b[:, N_HEADS + h:N_HEADS + h + 1], (tb, GDN_DH))

    g128, b128, gbh, bbh = pl.pallas_call(
        gate_f, name="gdn_gate_f", grid=(steps,),
        in_specs=[pl.BlockSpec((tb, HEAD_LANES), lambda i: (i, PIN_AB // HEAD_LANES)), vec_spec(HEAD_LANES),
                  vec_spec(HEAD_LANES)],
        out_specs=[lanes_spec, lanes_spec, _heads_spec(tb), _heads_spec(tb)],
        out_shape=[lanes_shape, lanes_shape, heads_shape, heads_shape],
        compiler_params=_params(("arbitrary",)))(proj, w["a_log_pad"], w["dt_bias_pad"])
    oh, sall, tall = _gdn_fwd(qh, kh, vh, gbh, bbh)

    def gdn_post(o_ref, gt_ref, g_ref, cat_in, cat_ref):
        gt, g = gt_ref[...], g_ref[...]
        outs = []
        for h in range(N_HEADS):
            ob, gth = o_ref[h], _head_cols(gt, h)
            outs.append(ob * _rms_stats(ob) * g * (gth * _sigmoid(gth)))
        cat_ref[...] = jnp.concatenate(outs, axis=1).astype(cat_ref.dtype)

    gate_spec = pl.BlockSpec((tb, GDN_W), lambda i: (i, PIN_GATE // GDN_W))
    cat = pl.pallas_call(
        gdn_post, name="gdn_post", grid=(steps,),
        in_specs=[_heads_spec(tb), gate_spec, vec_spec(GDN_DH), ANY_SPEC],
        out_specs=pl.BlockSpec((tb, GDN_W), lambda i: (i, MLA_PAD // GDN_W)),
        out_shape=jax.ShapeDtypeStruct((t, CAT_W), BF16), input_output_aliases={3: 0},
        compiler_params=_params(("arbitrary",)))(oh, proj, w["gdn_norm_g"], cat)
    mixed = _mm("mix_out", cat, w["w_out_pad"], "nn", F32)

    def post(xb, hb, g):
        return (xb + hb * _rms_stats(hb) * g,)

    (y,) = _rowwise("mix_post", post, [x, mixed], [w["mix_post_g"]], [(d, F32)], [], tb)
    saved = dict(x=x, hn=hn, proj=proj, nq=nq, nkv=nkv, q=q, k=k, v=v, o=o, lse=lse, qh=qh, kh=kh, vh=vh,
                 gbh=gbh, bbh=bbh, oh=oh, sall=sall, tall=tall, cat=cat, mixed=mixed,
                 tables=tables, g128=g128, b128=b128)
    return y, saved, carried_out


def _qkv_specs(tb):
    base = PIN_QKV // GDN_W
    cur = [pl.BlockSpec((tb, GDN_W), lambda i, j=j: (i, base + j)) for j in range(3)]
    prev = [pl.BlockSpec((tb, GDN_W), lambda i, j=j: (jnp.maximum(i - 1, 0), base + j)) for j in range(3)]
    return cur + prev


def _gdn_pre_call(name, fn, proj, conv_w, gones, tb, steps):
    t = proj.shape[0]

    def body(xq, xk, xv, pq, pk, pv, cw, go, oq, ok, ov):
        has_prev = jnp.where(pl.program_id(0) == 0, 0.0, 1.0)
        outs = fn(xq[...], xk[...], xv[...], pq[...], pk[...], pv[...], cw[...], go[...], has_prev)
        for r, val in zip((oq, ok, ov), outs):
            for h in range(N_HEADS):
                r[h] = _head_cols(val, h)

    return pl.pallas_call(
        body, name=name, grid=(steps,),
        in_specs=_qkv_specs(tb) + [pl.BlockSpec(conv_w.shape, lambda i: (0, 0)),
                                   pl.BlockSpec(gones.shape, lambda i: (0, 0))],
        out_specs=[_heads_spec(tb)] * 3,
        out_shape=[jax.ShapeDtypeStruct((N_HEADS, t, GDN_DH), F32)] * 3,
        compiler_params=_params(("arbitrary",)))(proj, proj, proj, proj, proj, proj, conv_w, gones)


def _softplus(x):
    return jnp.maximum(x, 0.0) + jnp.log1p(jnp.exp(-jnp.abs(x)))


def _gb_fwd(ab, a_log, dt_bias):
    g = -jnp.exp(a_log) * _softplus(ab + dt_bias)
    return g, _sigmoid(ab)


def _rope_tables(positions):
    half = MLA_ROPE // 2
    freqs = ROPE_THETA ** (-jnp.arange(half, dtype=F32) / half)
    ang = positions.reshape(-1).astype(F32)[:, None] * freqs
    cos, sin = jnp.cos(ang), jnp.sin(ang)
    t = ang.shape[0]
    one = jnp.ones((t, MLA_NOPE), F32)
    z16, z32, z64 = jnp.zeros((t, half), F32), jnp.zeros((t, MLA_ROPE), F32), jnp.zeros((t, MLA_NOPE), F32)
    c = jnp.concatenate([one, cos, cos, jnp.ones((t, MLA_ROPE), F32)], axis=1)
    s1 = jnp.concatenate([z64, -sin, z16, z32], axis=1)
    s2 = jnp.concatenate([z64, z16, sin, z32], axis=1)
    return c, s1, s2


def _mixer_bwd(dy, sv, w, tb, carried=None):
    x, proj = sv["x"], sv["proj"]
    t, d = x.shape
    c, s1, s2 = sv["tables"]
    grads = {}

    def post_b(hb, dyb, g):
        return _rms_bwd(hb, _rms_stats(hb), g, dyb)

    dmixed, grads["mix_post_g"] = _rowwise("mix_post_b", post_b, [sv["mixed"], dy], [w["mix_post_g"]],
                                           [(d, BF16)], [(1, d)], tb)
    dcat = _mm("mix_out_bx", dmixed, w["w_out_pad"], "nt", F32)
    grads["w_out_pad"] = _mm("mix_out_bw", sv["cat"], dmixed, "tn", F32)
    steps = t // tb
    vec_spec = lambda n: pl.BlockSpec((1, n), lambda i: (0, 0))

    def gdn_post_b(o_ref, gt_ref, do_ref, g_ref, dproj_ref, doh_ref, dg_ref):
        @pl.when(pl.program_id(0) == 0)
        def _():
            dg_ref[...] = jnp.zeros(dg_ref.shape, F32)

        gt, dob, g = gt_ref[...], do_ref[...], g_ref[...]
        dgates = []
        for h in range(N_HEADS):
            ob, gth, dobh = o_ref[h], _head_cols(gt, h), _head_cols(dob, h)
            sg = _sigmoid(gth)
            r = _rms_stats(ob)
            dxo, dg = _rms_bwd(ob, r, g, dobh * (gth * sg))
            doh_ref[h] = dxo
            dg_ref[...] += dg
            dgates.append(dobh * (ob * r * g) * (sg * (1.0 + gth * (1.0 - sg))))
        dproj_ref[...] = jnp.concatenate(dgates, axis=1).astype(dproj_ref.dtype)

    dproj, doh, grads["gdn_norm_g"] = pl.pallas_call(
        gdn_post_b, name="gdn_post_b", grid=(steps,),
        in_specs=[_heads_spec(tb), pl.BlockSpec((tb, GDN_W), lambda i: (i, PIN_GATE // GDN_W)),
                  pl.BlockSpec((tb, GDN_W), lambda i: (i, MLA_PAD // GDN_W)), vec_spec(GDN_DH)],
        out_specs=[pl.BlockSpec((tb, GDN_W), lambda i: (i, PIN_GATE // GDN_W)), _heads_spec(tb), vec_spec(GDN_DH)],
        out_shape=[jax.ShapeDtypeStruct((t, PIN_W), BF16), jax.ShapeDtypeStruct((N_HEADS, t, GDN_DH), F32),
                   jax.ShapeDtypeStruct((1, GDN_DH), F32)],
        compiler_params=_params(("arbitrary",)))(sv["oh"], proj, dcat, w["gdn_norm_g"])

    def mla_post_b(ob, dmo, g):
        do, dg = _rms_bwd(ob, _rms_stats(ob, N_HEADS * MLA_V), g, dmo, N_HEADS * MLA_V)
        prod = do * ob
        delta = _heads_apply(prod, lambda ph: jnp.sum(ph, axis=1, keepdims=True) + jnp.zeros_like(ph))
        return do, delta, dg

    do, delta, grads["mla_out_g_pad"] = _rowwise(
        "mla_post_b", mla_post_b, [sv["o"], (dcat, MLA_PAD, 0, 0)], [w["mla_out_g_pad"]],
        [(MLA_PAD, BF16), (MLA_PAD, F32)], [(1, MLA_PAD)], tb // 2)
    tq = min(512, t)
    dq = _attn_bwd_dq(sv["q"], sv["k"], sv["v"], do, sv["lse"], delta, tq)
    dk, dv, *carried_out = _attn_bwd_dkv(sv["q"], sv["k"], sv["v"], do, sv["lse"], delta, tq, carried)

    def rope_b(dqb, dkb, dvb, cc, a1, a2):
        dqr = _heads_apply(dqb * _attn_scale(), lambda xh: _rope(xh, cc, -a1, -a2))
        ksum = dkb[:, :HEAD_LANES]
        for h in range(1, N_HEADS):
            ksum = ksum + dkb[:, h * HEAD_LANES:(h + 1) * HEAD_LANES]
        lane = lax.broadcasted_iota(jnp.int32, ksum.shape, 1)
        keep = (lane >= MLA_NOPE) & (lane < MLA_NOPE + MLA_ROPE)
        dkpe = jnp.where(keep, _rope(ksum, cc, -a1, -a2), 0.0)
        return dqr, jnp.concatenate([dkb, dvb], axis=1), dkpe

    dqraw, dkv, dkpe = _rowwise("mla_rope_b", rope_b, [dq, dk, dv, c, s1, s2], [],
                                [(MLA_PAD, BF16), (2 * MLA_PAD, BF16), (HEAD_LANES, F32)], [], tb // 2)
    dnq = _mm("mla_uq_bx", dqraw, w["w_uq_pad"], "nt", F32)
    grads["w_uq_pad"] = _mm("mla_uq_bw", sv["nq"], dqraw, "tn", F32)
    dnkv = _mm("mla_ukv_bx", dkv, w["w_kv_pad"], "nt", F32)
    grads["w_kv_pad"] = _mm("mla_ukv_bw", sv["nkv"], dkv, "tn", F32)

    def mla_pre_b(p0, dnqb, dnkvb, dkpeb, gq, gkv):
        cq, ckv = p0[:, :MLA_Q_RANK], p0[:, MLA_Q_RANK:MLA_Q_RANK + MLA_KV_RANK]
        dcq, dgq = _rms_bwd(cq, _rms_stats(cq), gq, dnqb)
        dckv, dgkv = _rms_bwd(ckv, _rms_stats(ckv), gkv, dnkvb)
        return jnp.concatenate([dcq, dckv, dkpeb], axis=1), dgq, dgkv

    dproj, grads["mla_q_norm_g"], grads["mla_kv_norm_g"] = _rowwise(
        "mla_pre_b", mla_pre_b, [(proj, 512, PIN_MLA // 512, 0), dnq, dnkv, dkpe],
        [w["mla_q_norm_g"], w["mla_kv_norm_g"]], [(512, BF16)], [(1, MLA_Q_RANK), (1, MLA_KV_RANK)], tb,
        wide=(PIN_W, PIN_MLA // 512), carry=dproj)

    dqh, dkh, dvh, dgh, dbh = _gdn_bwd(sv["qh"], sv["kh"], sv["vh"], sv["gbh"], sv["bbh"], sv["sall"], sv["tall"], doh)
    gones = _group_ones()

    def gdn_pre_b(xq, xk, xv, pq, pk, pv, dq_, dk_, dv_, cw, go, has_prev):
        outs = []
        for j, (xc, xp, dd) in enumerate(((xq, pq, dq_), (xk, pk, dk_), (xv, pv, dv_))):
            cc = _conv_taps(xc, xp, cw[:, j * GDN_W:(j + 1) * GDN_W], has_prev)
            sg = _sigmoid(cc)
            a = cc * sg
            if j < 2:
                rn = lax.rsqrt(_dotf(a * a, go) + EPS)
                if j == 0:
                    dd = dd * (GDN_DH ** -0.5)
                da = rn * dd - a * (rn * rn * rn) * _dotf(dd * a, go)
            else:
                da = dd
            outs.append(da * (sg * (1.0 + cc * (1.0 - sg))))
        return tuple(outs)

    dcq, dck, dcv = _gdn_pre_b_call("gdn_pre_b", gdn_pre_b, proj, (dqh, dkh, dvh), w["conv_w"], gones, tb, steps)
    dproj, grads["conv_w"] = _conv_bwd_call("gdn_conv_b", proj, (dcq, dck, dcv), w["conv_w"], dproj, tb, steps)

    def gate_b(ab_ref, g_ref, b_ref, dgh_ref, dbh_ref, al_ref, dt_ref, carry_ref, dab_ref, dal_ref, ddt_ref):
        @pl.when(pl.program_id(0) == 0)
        def _():
            dal_ref[...] = jnp.zeros(dal_ref.shape, F32)
            ddt_ref[...] = jnp.zeros(ddt_ref.shape, F32)

        ab, g128, b128 = ab_ref[...], g_ref[...], b_ref[...]
        lane = lax.broadcasted_iota(jnp.int32, ab.shape, 1)
        dg_ = jnp.zeros(ab.shape, F32)
        db_ = jnp.zeros(ab.shape, F32)
        for h in range(N_HEADS):
            dg_ = dg_ + jnp.where(lane == h, jnp.broadcast_to(dgh_ref[h][:, 0:1], ab.shape), 0.0)
            db_ = db_ + jnp.where(lane == N_HEADS + h, jnp.broadcast_to(dbh_ref[h][:, 0:1], ab.shape), 0.0)
        dg_ = _dotf(_chunk_sum_matrix(tb, True), dg_)
        slope = -jnp.exp(al_ref[...]) * _sigmoid(ab + dt_ref[...])
        dab_ref[...] = (dg_ * slope + db_ * b128 * (1.0 - b128)).astype(dab_ref.dtype)
        dal_ref[...] += jnp.sum(dg_ * g128, axis=0, keepdims=True)
        ddt_ref[...] += jnp.sum(dg_ * slope, axis=0, keepdims=True)

    lanes_spec = pl.BlockSpec((tb, HEAD_LANES), lambda i: (i, 0))
    ab_spec = pl.BlockSpec((tb, HEAD_LANES), lambda i: (i, PIN_AB // HEAD_LANES))
    dproj, grads["a_log_pad"], grads["dt_bias_pad"] = pl.pallas_call(
        gate_b, name="gdn_gate_b", grid=(steps,),
        in_specs=[ab_spec, lanes_spec, lanes_spec, _heads_spec(tb), _heads_spec(tb), vec_spec(HEAD_LANES),
                  vec_spec(HEAD_LANES), ANY_SPEC],
        out_specs=[ab_spec, vec_spec(HEAD_LANES), vec_spec(HEAD_LANES)],
        out_shape=[jax.ShapeDtypeStruct((t, PIN_W), BF16), jax.ShapeDtypeStruct((1, HEAD_LANES), F32),
                   jax.ShapeDtypeStruct((1, HEAD_LANES), F32)],
        input_output_aliases={7: 0},
        compiler_params=_params(("arbitrary",)))(proj, sv["g128"], sv["b128"], dgh, dbh, w["a_log_pad"],
                                                 w["dt_bias_pad"], dproj)
    dhn = _mm("mix_in_bx", dproj, w["w_in_pad"], "nt", F32)
    grads["w_in_pad"] = _mm("mix_in_bw", sv["hn"], dproj, "tn", F32)

    def pre_b(xb, dnb, dyb, g):
        dx, dg = _rms_bwd(xb, _rms_stats(xb), g, dnb)
        return dyb + dx, dg

    dx, grads["mix_pre_g"] = _rowwise("mix_pre_b", pre_b, [x, dhn, dy], [w["mix_pre_g"]], [(d, F32)], [(1, d)], tb)
    return dx, grads, carried_out


def _gdn_pre_b_call(name, fn, proj, dd, conv_w, gones, tb, steps):
    t = proj.shape[0]

    def body(xq, xk, xv, pq, pk, pv, d0, d1, d2, cw, go, oq, ok, ov):
        has_prev = jnp.where(pl.program_id(0) == 0, 0.0, 1.0)
        dd_rows = [jnp.concatenate([dr[h] for h in range(N_HEADS)], axis=1) for dr in (d0, d1, d2)]
        outs = fn(xq[...], xk[...], xv[...], pq[...], pk[...], pv[...], *dd_rows, cw[...], go[...], has_prev)
        for r, val in zip((oq, ok, ov), outs):
            r[...] = val

    return pl.pallas_call(
        body, name=name, grid=(steps,),
        in_specs=_qkv_specs(tb) + [_heads_spec(tb)] * 3 + [pl.BlockSpec(conv_w.shape, lambda i: (0, 0)),
                                                          pl.BlockSpec(gones.shape, lambda i: (0, 0))],
        out_specs=[pl.BlockSpec((tb, GDN_W), lambda i: (i, 0))] * 3,
        out_shape=[jax.ShapeDtypeStruct((t, GDN_W), F32)] * 3,
        compiler_params=_params(("arbitrary",)))(proj, proj, proj, proj, proj, proj, *dd, conv_w, gones)


def _conv_bwd_call(name, proj, dc, conv_w, dproj, tb, steps):
    t = proj.shape[0]
    dcur = [pl.BlockSpec((tb, GDN_W), lambda i: (i, 0))] * 3
    dnext = [pl.BlockSpec((tb, GDN_W), lambda i: (jnp.minimum(i + 1, steps - 1), 0))] * 3

    def body(xq, xk, xv, pq, pk, pv, d0, d1, d2, n0, n1, n2, cw, carry_ref, dx_ref, dw_ref):
        i = pl.program_id(0)
        has_prev = jnp.where(i == 0, 0.0, 1.0)
        has_next = jnp.where(i == steps - 1, 0.0, 1.0)

        @pl.when(i == 0)
        def _():
            dw_ref[...] = jnp.zeros(dw_ref.shape, F32)

        wv = cw[...]
        dws, dxs = [], []
        for j, (xr, pr, dr, nr) in enumerate(((xq, pq, d0, n0), (xk, pk, d1, n1), (xv, pv, d2, n2))):
            x, xp, dcv, dnx = xr[...], pr[...], dr[...], nr[...]
            wj = wv[:, j * GDN_W:(j + 1) * GDN_W]
            row = lax.broadcasted_iota(jnp.int32, x.shape, 0)
            dx = dcv * wj[GDN_CONV - 1:GDN_CONV, :]
            rows_w = [jnp.sum(dcv * x, axis=0, keepdims=True)]
            for s in range(1, GDN_CONV):
                up = jnp.where(row < tb - s, _roll(dcv, tb - s, 0), _roll(dnx, tb - s, 0) * has_next)
                dx = dx + up * wj[GDN_CONV - 1 - s:GDN_CONV - s, :]
                sh = jnp.where(row >= s, _roll(x, s, 0), _roll(xp, s, 0) * has_prev)
                rows_w.append(jnp.sum(dcv * sh, axis=0, keepdims=True))
            dxs.append(dx)
            dws.append(jnp.concatenate(rows_w[::-1], axis=0))
        dx_ref[...] = jnp.concatenate(dxs, axis=1).astype(dx_ref.dtype)
        dw_ref[...] += jnp.concatenate(dws, axis=1)

    return pl.pallas_call(
        body, name=name, grid=(steps,),
        in_specs=_qkv_specs(tb) + dcur + dnext + [pl.BlockSpec(conv_w.shape, lambda i: (0, 0)), ANY_SPEC],
        out_specs=[pl.BlockSpec((tb, 3 * GDN_W), lambda i: (i, PIN_QKV // (3 * GDN_W))),
                   pl.BlockSpec(conv_w.shape, lambda i: (0, 0))],
        out_shape=[jax.ShapeDtypeStruct((t, PIN_W), BF16), jax.ShapeDtypeStruct(conv_w.shape, F32)],
        input_output_aliases={13: 0},
        compiler_params=_params(("arbitrary",)))(proj, proj, proj, proj, proj, proj, *dc, *dc, conv_w, dproj)


def _pad_heads_cols(wm, per_head):
    r = wm.shape[0]
    return jnp.pad(wm.reshape(r, N_HEADS, per_head), ((0, 0), (0, 0), (0, HEAD_LANES - per_head))).reshape(r, MLA_PAD)


def _unpad_heads_cols(wm, per_head):
    r = wm.shape[0]
    return wm.reshape(r, N_HEADS, HEAD_LANES)[:, :, :per_head].reshape(r, N_HEADS * per_head)


def _win_to_pad(wi):
    r = wi.shape[0]
    z = lambda n: jnp.zeros((r, n), wi.dtype)
    o = MLA_Q_RANK + MLA_KV_RANK
    kpe = wi[:, o:o + MLA_ROPE]
    o2 = o + MLA_ROPE
    qkv = wi[:, o2:o2 + 3 * GDN_W]
    o3 = o2 + 3 * GDN_W
    ab = wi[:, o3:o3 + 2 * N_HEADS]
    gate = wi[:, o3 + 2 * N_HEADS:]
    return jnp.concatenate([qkv, wi[:, :o], z(MLA_NOPE), kpe, z(HEAD_LANES - MLA_NOPE - MLA_ROPE), gate, ab,
                            z(HEAD_LANES - 2 * N_HEADS)], axis=1)


def _win_from_pad(wp):
    return jnp.concatenate([wp[:, PIN_MLA:PIN_KPE], wp[:, PIN_KPE + MLA_NOPE:PIN_KPE + MLA_NOPE + MLA_ROPE],
                            wp[:, PIN_QKV:PIN_QKV + 3 * GDN_W], wp[:, PIN_AB:PIN_AB + 2 * N_HEADS],
                            wp[:, PIN_GATE:PIN_GATE + GDN_W]], axis=1)


def _wkv_to_pad(wkv):
    r = wkv.shape[0]
    w3 = wkv.reshape(r, N_HEADS, MLA_NOPE + MLA_V)
    kpart = jnp.pad(w3[:, :, :MLA_NOPE], ((0, 0), (0, 0), (0, HEAD_LANES - MLA_NOPE))).reshape(r, MLA_PAD)
    vpart = jnp.pad(w3[:, :, MLA_NOPE:], ((0, 0), (0, 0), (0, HEAD_LANES - MLA_V))).reshape(r, MLA_PAD)
    return jnp.concatenate([kpart, vpart], axis=1)


def _wkv_from_pad(wp):
    r = wp.shape[0]
    kpart = wp[:, :MLA_PAD].reshape(r, N_HEADS, HEAD_LANES)[:, :, :MLA_NOPE]
    vpart = wp[:, MLA_PAD:].reshape(r, N_HEADS, HEAD_LANES)[:, :, :MLA_V]
    return jnp.concatenate([kpart, vpart], axis=2).reshape(r, N_HEADS * (MLA_NOPE + MLA_V))


def _wout_to_pad(wo):
    n = wo.shape[1]
    mla = jnp.pad(wo[:N_HEADS * MLA_V].reshape(N_HEADS, MLA_V, n), ((0, 0), (0, HEAD_LANES - MLA_V), (0, 0)))
    return jnp.concatenate([mla.reshape(MLA_PAD, n), wo[N_HEADS * MLA_V:]], axis=0)


def _wout_from_pad(wp):
    n = wp.shape[1]
    mla = wp[:MLA_PAD].reshape(N_HEADS, HEAD_LANES, n)[:, :MLA_V].reshape(N_HEADS * MLA_V, n)
    return jnp.concatenate([mla, wp[MLA_PAD:]], axis=0)


def _pad_lanes(v, n):
    return jnp.pad(v, ((0, 0), (0, n - v.shape[1])))


def _compute_weights(full):
    w = {}
    for n in FFN_BIG:
        if n in full:
            w[n] = full[n].astype(MM_DTYPE)
    w["w_in_pad"] = _win_to_pad(full["w_in"]).astype(MM_DTYPE)
    w["w_uq_pad"] = _pad_heads_cols(full["mla_w_uq"], MLA_NOPE + MLA_ROPE).astype(MM_DTYPE)
    w["w_kv_pad"] = _wkv_to_pad(full["mla_w_ukv"]).astype(MM_DTYPE)
    w["w_out_pad"] = _wout_to_pad(full["w_out"]).astype(MM_DTYPE)
    w["conv_w"] = full["gdn_conv_w"].astype(F32)
    for n in ("ffn1_pre_g", "ffn1_post_g", "mix_pre_g", "mla_q_norm_g", "mla_kv_norm_g", "gdn_norm_g", "mix_post_g",
              "ffn2_pre_g", "ffn2_post_g"):
        w[n] = full[n]
    w["mla_out_g_pad"] = _pad_heads_cols(full["mla_out_g"], MLA_V)
    w["a_log_pad"] = _pad_lanes(full["gdn_a_log"], HEAD_LANES)
    w["dt_bias_pad"] = _pad_lanes(full["gdn_dt_bias"], HEAD_LANES)
    return w


FFN2_BIG = FFN_BIG[3:]


def _local_step(x, positions, loss_target, full, late=None):
    t, d = x.shape
    tb = min(512, t)
    tm = min(1024, t)
    w = _compute_weights(full)
    ffn = lambda tag: (w[tag + "_pre_g"], w[tag + "_w_gate"], w[tag + "_w_up"], w[tag + "_w_down"], w[tag + "_post_g"])
    x1, sv1 = _ffn_fwd("ffn1", x, *ffn("ffn1"), tm)
    x2, svm, gathered = _mixer_fwd(x1, positions, w, tb, _carried_gather(late[0]) if late else None)
    for n, gw in zip(FFN2_BIG, gathered):
        w[n] = gw
    x3, sv2 = _ffn_fwd("ffn2", x2, *ffn("ffn2"), tm)

    def loss_f(yb, tg):
        e = yb - tg
        return e * (1.0 / d), jnp.sum(e * e, axis=0, keepdims=True)

    dy, lsum = _rowwise("loss", loss_f, [x3, loss_target], [], [(d, F32)], [(1, d)], tb)
    g = {}
    dx2, g["ffn2_pre_g"], g["ffn2_w_gate"], g["ffn2_w_up"], g["ffn2_w_down"], g["ffn2_post_g"], _, _ = _ffn_bwd(
        "ffn2", dy, sv2, *ffn("ffn2"), tm, tm)

    def pair_sums(arrs, tag):
        got = _swap_halves(arrs, tag)
        return [_add_pair("add_pair%s_%d" % (tag, i), gi, gt, late[1]) for i, (gi, gt) in enumerate(zip(arrs, got))]

    def chip_sums(pairs, slabs, tag):
        return [_add_chips("add_chips%s_%d" % (tag, i), pr, sl, late[2]) for i, (pr, sl) in enumerate(zip(pairs, slabs))]

    if late:
        pairs2 = pair_sums([g[n] for n in FFN2_BIG], "_ffn2")
        dx1, gm, slabs2 = _mixer_bwd(dx2, svm, w, tb, _carried_scatter(pairs2))
        for n, hs in zip(FFN2_BIG, chip_sums(pairs2, slabs2, "_ffn2")):
            g[n] = hs
    else:
        dx1, gm, _ = _mixer_bwd(dx2, svm, w, tb)
    g["w_in"] = _win_from_pad(gm["w_in_pad"])
    g["mla_w_uq"] = _unpad_heads_cols(gm["w_uq_pad"], MLA_NOPE + MLA_ROPE)
    g["mla_w_ukv"] = _wkv_from_pad(gm["w_kv_pad"])
    g["gdn_conv_w"] = gm["conv_w"]
    g["w_out"] = _wout_from_pad(gm["w_out_pad"])
    ffn1_names = FFN_BIG[:3]
    if late:
        quarters = [_pack([jnp.split(g[n], N_SHARD, axis=SHARD_AXIS[n])[q] for n in MIX_BIG], MM_DTYPE)
                    for q in range(N_SHARD)]
        pairs_m = pair_sums([jnp.stack(quarters)], "_mix")
        pairs1 = []

        def make_up(*dws):
            pairs1.extend(pair_sums(list(dws), "_ffn1"))
            return _carried_scatter(pairs1)

        dx0, g["ffn1_pre_g"], _, _, _, g["ffn1_post_g"], slabs_m, slabs1 = _ffn_bwd(
            "ffn1", dx1, sv1, *ffn("ffn1"), tm, tm, _carried_scatter(pairs_m), make_up)
        for n, hs in zip(ffn1_names, chip_sums(pairs1, slabs1, "_ffn1")):
            g[n] = hs
        g["mix_pack"] = chip_sums(pairs_m, slabs_m, "_mix")[0]
    else:
        dx0, g["ffn1_pre_g"], g["ffn1_w_gate"], g["ffn1_w_up"], g["ffn1_w_down"], g["ffn1_post_g"], _, _ = _ffn_bwd(
            "ffn1", dx1, sv1, *ffn("ffn1"), tm, tm)
    g["mix_pre_g"], g["mix_post_g"] = gm["mix_pre_g"], gm["mix_post_g"]
    g["mla_q_norm_g"], g["mla_kv_norm_g"] = gm["mla_q_norm_g"], gm["mla_kv_norm_g"]
    g["gdn_norm_g"] = gm["gdn_norm_g"]
    g["mla_out_g"] = _unpad_heads_cols(gm["mla_out_g_pad"], MLA_V)
    g["gdn_a_log"] = gm["a_log_pad"][:, :N_HEADS]
    g["gdn_dt_bias"] = gm["dt_bias_pad"][:, :N_HEADS]
    return lsum, dx0, g


HBM_SPEC = pl.BlockSpec(memory_space=pltpu.HBM)


def _place():
    return lax.axis_index("x"), lax.axis_index("y"), lax.axis_index("c")


def _exchange_call(name, body, ins, out_shapes, n_remote, n_local):
    return pl.pallas_call(
        body, name=name, in_specs=[HBM_SPEC] * len(ins), out_specs=[HBM_SPEC] * len(out_shapes), out_shape=out_shapes,
        scratch_shapes=[pltpu.SemaphoreType.DMA((n_remote,)), pltpu.SemaphoreType.DMA((n_remote,)),
                        pltpu.SemaphoreType.DMA((n_local,))])(*ins)


def _other_chips(x, y):
    return [(1 - x, y), (x, 1 - y), (1 - x, 1 - y)]


def _at_each_chip(fn):
    x, y, _ = _place()
    for cx in range(2):
        for cy in range(2):
            pl.when((x == cx) & (y == cy))(functools.partial(fn, cx, cy))


def _at_each_device(fn):
    x, y, c = _place()
    for cx in range(2):
        for cy in range(2):
            for cc in range(2):
                pl.when((x == cx) & (y == cy) & (c == cc))(functools.partial(fn, cx, cy, cc))


def _at_each_core(fn):
    c = lax.axis_index("c")
    for cc in range(2):
        pl.when(c == cc)(functools.partial(fn, cc))


def _gather_shards(ws):
    nw = len(ws)

    def body(*refs):
        w_refs, out_refs = refs[:nw], refs[nw:2 * nw]
        send_sems, recv_sems, local_sems = refs[2 * nw:]

        def run(x, y, c):
            chips = _other_chips(x, y)
            me, sibling = 2 * x + y, (x, y, 1 - c)

            def half(ref, which):
                hr = ref.shape[0] // 2
                return ref.at[pl.ds(which * hr, hr)]

            def over_ici(i, j, src, slab, to):
                return pltpu.make_async_remote_copy(
                    src_ref=half(src, c), dst_ref=half(out_refs[i].at[slab], c), send_sem=send_sems.at[7 * i + j],
                    recv_sem=recv_sems.at[7 * i + j], device_id=to, device_id_type=MESH)

            def over_d2d(i, j, slab, which):
                return pltpu.make_async_remote_copy(
                    src_ref=half(out_refs[i].at[slab], which), dst_ref=half(out_refs[i].at[slab], which),
                    send_sem=send_sems.at[7 * i + 3 + j], recv_sem=recv_sems.at[7 * i + 3 + j], device_id=sibling,
                    device_id_type=MESH)

            def own(i, w_ref):
                return pltpu.make_async_remote_copy(
                    src_ref=w_ref, dst_ref=out_refs[i].at[me], send_sem=send_sems.at[7 * i + 6],
                    recv_sem=recv_sems.at[7 * i + 6], device_id=sibling, device_id_type=MESH)

            sends, passed = [], []
            for i, w_ref in enumerate(w_refs):
                for j, (px, py) in enumerate(chips):
                    sends.append(over_ici(i, j, w_ref, me, (px, py, c)))
                    sends[-1].start()
            for i, w_ref in enumerate(w_refs):
                sends.append(own(i, w_ref))
                sends[-1].start()
            for i, w_ref in enumerate(w_refs):
                for j, (px, py) in enumerate(chips):
                    over_ici(i, j, w_ref, 2 * px + py, (px, py, c)).wait_recv()
                    passed.append(over_d2d(i, j, 2 * px + py, c))
                    passed[-1].start()
            for i, w_ref in enumerate(w_refs):
                own(i, w_ref).wait_recv()
                for j, (px, py) in enumerate(chips):
                    over_d2d(i, j, 2 * px + py, 1 - c).wait_recv()
            for cp in sends + passed:
                cp.wait_send()

        _at_each_device(run)

    outs = [jax.ShapeDtypeStruct((N_SHARD,) + w.shape, w.dtype) for w in ws]
    return _exchange_call("gather_weight_shards", body, ws, outs, 7 * nw, 1)


def _swap_halves(gs, tag=""):
    ng = len(gs)

    def body(*refs):
        g_refs, got_refs = refs[:ng], refs[ng:2 * ng]
        send_sems, recv_sems, _ = refs[2 * ng:]
        x, y, _ = _place()

        def run(c):
            sends = []
            for i, (g_ref, got_ref) in enumerate(zip(g_refs, got_refs)):
                hr = got_ref.shape[1]
                sends.append(pltpu.make_async_remote_copy(
                    src_ref=g_ref.at[:, pl.ds((1 - c) * hr, hr)], dst_ref=got_ref, send_sem=send_sems.at[i],
                    recv_sem=recv_sems.at[i], device_id=(x, y, 1 - c), device_id_type=MESH))
                sends[-1].start()
            for cp in sends:
                cp.wait()

        _at_each_core(run)

    halves = [jax.ShapeDtypeStruct((g.shape[0], g.shape[1] // 2, g.shape[2]), g.dtype) for g in gs]
    return _exchange_call("swap_grad_halves" + tag, body, gs, halves, ng, 1)


def _scatter_copies(p_refs, out_refs, send_sems, recv_sems, x, y):
    c = lax.axis_index("c")
    copies = []
    for i, (p_ref, out_ref) in enumerate(zip(p_refs, out_refs)):
        for j, (px, py) in enumerate(_other_chips(x, y)):
            copies.append(pltpu.make_async_remote_copy(
                src_ref=p_ref.at[2 * px + py], dst_ref=out_ref.at[j], send_sem=send_sems.at[3 * i + j],
                recv_sem=recv_sems.at[3 * i + j], device_id=(px, py, c), device_id_type=MESH))
    return copies


def _start_all(make, *refs):
    def run(x, y):
        for cp in make(*refs, x, y):
            cp.start()

    _at_each_chip(run)


def _wait_all(make, *refs):
    def run(x, y):
        copies = make(*refs, x, y)
        for cp in copies:
            cp.wait_recv()
        for cp in copies:
            cp.wait_send()

    _at_each_chip(run)


def _scatter_shapes(ps):
    return [jax.ShapeDtypeStruct((3,) + p.shape[1:], p.dtype) for p in ps]


def _carried_scatter(ps):
    return _Carried(ps, _scatter_shapes(ps), 3 * len(ps), functools.partial(_start_all, _scatter_copies),
                    functools.partial(_wait_all, _scatter_copies))


def _direct_gather_copies(w_refs, out_refs, send_sems, recv_sems, x, y, arriving):
    c = lax.axis_index("c")
    me = 2 * x + y
    peers = [((px, py, c), 2 * px + py) for px, py in _other_chips(x, y)] + [((x, y, 1 - c), me)]
    copies = []
    for i, (w_ref, out_ref) in enumerate(zip(w_refs, out_refs)):
        for j, (peer, slab) in enumerate(peers):
            copies.append(pltpu.make_async_remote_copy(
                src_ref=w_ref, dst_ref=out_ref.at[slab if arriving else me], send_sem=send_sems.at[4 * i + j],
                recv_sem=recv_sems.at[4 * i + j], device_id=peer, device_id_type=MESH))
    return copies


def _carried_gather(ws):
    def start(w_refs, out_refs, send_sems, recv_sems):
        def run(x, y):
            for cp in _direct_gather_copies(w_refs, out_refs, send_sems, recv_sems, x, y, False):
                cp.start()

        _at_each_chip(run)

    def finish(w_refs, out_refs, send_sems, recv_sems):
        def run(x, y):
            for cp in _direct_gather_copies(w_refs, out_refs, send_sems, recv_sems, x, y, True):
                cp.wait_recv()
            for cp in _direct_gather_copies(w_refs, out_refs, send_sems, recv_sems, x, y, False):
                cp.wait_send()

        _at_each_chip(run)

    outs = [jax.ShapeDtypeStruct((N_SHARD,) + w.shape, w.dtype) for w in ws]
    return _Carried(ws, outs, 4 * len(ws), start, finish)


def _share_halves(hs):
    n = len(hs)

    def body(*refs):
        h_refs, out_refs = refs[:n], refs[n:2 * n]
        send_sems, recv_sems, _ = refs[2 * n:]
        x, y, c = _place()
        sends = []
        for i, (h_ref, out_ref) in enumerate(zip(h_refs, out_refs)):
            sends.append(pltpu.make_async_remote_copy(
                src_ref=h_ref, dst_ref=out_ref, send_sem=send_sems.at[i], recv_sem=recv_sems.at[i],
                device_id=(x, y, 1 - c), device_id_type=MESH))
            sends[-1].start()
        for cp in sends:
            cp.wait()

    outs = [jax.ShapeDtypeStruct(h.shape, h.dtype) for h in hs]
    return _exchange_call("share_grad_halves", body, hs, outs, n, 1)


def _scalar_grid_call(name, body, scalars, grid, in_specs, out_specs, out_shape, args):
    grid_spec = pltpu.PrefetchScalarGridSpec(num_scalar_prefetch=len(scalars), grid=grid, in_specs=in_specs,
                                             out_specs=out_specs)
    return pl.pallas_call(body, name=name, grid_spec=grid_spec, out_shape=out_shape,
                          compiler_params=_params(("arbitrary",) * len(grid)))(*scalars, *args)


def _add_pair(name, g, got, core):
    ns_, hr, cols = got.shape
    th = _row_tile(hr, 512)
    nb = hr // th

    def body(core_ref, g_ref, got_ref, out_ref):
        out_ref[...] = (g_ref[...].astype(F32) + got_ref[...].astype(F32)).astype(out_ref.dtype)

    blk = pl.BlockSpec((1, th, cols), lambda q, j, core_ref: (q, j, 0))
    own = pl.BlockSpec((1, th, cols), lambda q, j, core_ref: (q, core_ref[0] * nb + j, 0))
    return _scalar_grid_call(name, body, [core], (ns_, nb), [own, blk], blk,
                             jax.ShapeDtypeStruct(got.shape, got.dtype), [g, got])


def _add_chips(name, pairs, slabs, chip):
    _, hr, cols = slabs.shape
    th = _row_tile(hr, 512)

    def body(chip_ref, own_ref, s0_ref, s1_ref, s2_ref, out_ref):
        total = own_ref[0].astype(F32) + s0_ref[0].astype(F32)
        out_ref[...] = (total + s1_ref[0].astype(F32)) + s2_ref[0].astype(F32)

    own = pl.BlockSpec((1, th, cols), lambda j, chip_ref: (chip_ref[0], j, 0))
    others = [pl.BlockSpec((1, th, cols), lambda j, chip_ref, k=k: (k, j, 0)) for k in range(3)]
    return _scalar_grid_call(name, body, [chip], (hr // th,), [own] + others,
                             pl.BlockSpec((th, cols), lambda j, chip_ref: (j, 0)),
                             jax.ShapeDtypeStruct((hr, cols), F32), [pairs, slabs, slabs, slabs])


def _join_halves(name, mine, other, core):
    hr, cols = mine.shape
    th = _row_tile(hr, 512)
    nb = hr // th

    def body(core_ref, mine_ref, other_ref, out_ref):
        is_mine = pl.program_id(0) == core_ref[0]

        @pl.when(is_mine)
        def _():
            out_ref[0] = mine_ref[...]

        @pl.when(jnp.logical_not(is_mine))
        def _():
            out_ref[0] = other_ref[...]

    blk = pl.BlockSpec((th, cols), lambda h, j, core_ref: (j, 0))
    return _scalar_grid_call(name, body, [core], (2, nb), [blk, blk],
                             pl.BlockSpec((1, th, cols), lambda h, j, core_ref: (0, h * nb + j, 0)),
                             jax.ShapeDtypeStruct((1, 2 * hr, cols), mine.dtype), [mine, other])


def _gather_small(sp):
    def body(s_ref, out_ref, send_sems, recv_sems, local_sem):
        x, y, c = _place()
        me = 4 * x + 2 * y + c
        peers = [(x ^ (m >> 2), y ^ ((m >> 1) & 1), c ^ (m & 1)) for m in range(1, 8)]
        mine = pltpu.make_async_copy(s_ref, out_ref.at[me], local_sem)
        mine.start()
        sends = [pltpu.make_async_remote_copy(src_ref=s_ref, dst_ref=out_ref.at[me], send_sem=send_sems.at[j],
                                              recv_sem=recv_sems.at[j], device_id=p, device_id_type=MESH)
                 for j, p in enumerate(peers)]
        for cp in sends:
            cp.start()
        for j, (px, py, pc) in enumerate(peers):
            pltpu.make_async_remote_copy(src_ref=s_ref, dst_ref=out_ref.at[4 * px + 2 * py + pc],
                                         send_sem=send_sems.at[j], recv_sem=recv_sems.at[j], device_id=(px, py, pc),
                                         device_id_type=MESH).wait_recv()
        for cp in sends:
            cp.wait_send()
        mine.wait()

    return pl.pallas_call(
        body, name="gather_small_grads", in_specs=[HBM_SPEC], out_specs=HBM_SPEC,
        out_shape=jax.ShapeDtypeStruct((8,) + sp.shape, sp.dtype),
        scratch_shapes=[pltpu.SemaphoreType.DMA((7,)), pltpu.SemaphoreType.DMA((7,)), pltpu.SemaphoreType.DMA])(sp)


def _pack_rows(total):
    rows = -(-total // LANES)
    return -(-rows // 32) * 32


def _pack(arrs, dtype):
    flat = jnp.concatenate([a.reshape(-1).astype(dtype) for a in arrs])
    rows = _pack_rows(flat.shape[0])
    return jnp.pad(flat, (0, rows * LANES - flat.shape[0])).reshape(rows, LANES)


def _unpack(buf, shapes):
    flat = buf.reshape(-1)
    out, off = {}, 0
    for n, shp in shapes:
        size = shp[0] * shp[1]
        out[n] = flat[off:off + size].reshape(shp)
        off += size
    return out


def _adamw(name, w3, g, m3, v3, tb):
    c1 = 1.0 - ADAM_B1 ** ADAM_STEP
    c2 = 1.0 - ADAM_B2 ** ADAM_STEP
    _, r, cols = w3.shape
    emit = g.ndim == 2
    blk3 = pl.BlockSpec((1, tb, cols), lambda i: (0, i, 0))
    g_spec = pl.BlockSpec((tb, cols), lambda i: (i, 0)) if emit else blk3

    def body(w_ref, g_ref, m_ref, v_ref, *out_refs):
        gb = g_ref[...] if emit else g_ref[0]
        m2 = ADAM_B1 * m_ref[0] + (1.0 - ADAM_B1) * gb
        v2 = ADAM_B2 * v_ref[0] + (1.0 - ADAM_B2) * (gb * gb)
        out_refs[-3][0] = -ADAM_LR * ((m2 / c1) / (jnp.sqrt(v2 / c2) + ADAM_EPS) + ADAM_WD * w_ref[0])
        out_refs[-2][0] = m2
        out_refs[-1][0] = v2
        if emit:
            out_refs[0][0] = gb

    n_out = 4 if emit else 3
    outs = pl.pallas_call(
        body, name=name, grid=(r // tb,), in_specs=[blk3, g_spec, blk3, blk3], out_specs=[blk3] * n_out,
        out_shape=[jax.ShapeDtypeStruct((1, r, cols), F32)] * n_out,
        compiler_params=_params(("arbitrary",)))(w3, g, m3, v3)
    return outs if emit else [g] + list(outs)


def _row_tile(rows, pref):
    if rows <= pref:
        return rows
    t = pref
    while t >= 8:
        if rows % t == 0 and t % 8 == 0:
            return t
        t -= 8
    return rows


def kernel(x, positions, ffn1_pre_g, ffn1_w_gate, ffn1_w_up, ffn1_w_down, ffn1_post_g, mix_pre_g, w_in, mla_q_norm_g, mla_w_uq, mla_kv_norm_g, mla_w_ukv, mla_out_g, gdn_conv_w, gdn_a_log, gdn_dt_bias, gdn_norm_g, w_out, mix_post_g, ffn2_pre_g, ffn2_w_gate, ffn2_w_up, ffn2_w_down, ffn2_post_g, loss_target, m_ffn1_pre_g, m_ffn1_w_gate, m_ffn1_w_up, m_ffn1_w_down, m_ffn1_post_g, m_mix_pre_g, m_w_in, m_mla_q_norm_g, m_mla_w_uq, m_mla_kv_norm_g, m_mla_w_ukv, m_mla_out_g, m_gdn_conv_w, m_gdn_a_log, m_gdn_dt_bias, m_gdn_norm_g, m_w_out, m_mix_post_g, m_ffn2_pre_g, m_ffn2_w_gate, m_ffn2_w_up, m_ffn2_w_down, m_ffn2_post_g, v_ffn1_pre_g, v_ffn1_w_gate, v_ffn1_w_up, v_ffn1_w_down, v_ffn1_post_g, v_mix_pre_g, v_w_in, v_mla_q_norm_g, v_mla_w_uq, v_mla_kv_norm_g, v_mla_w_ukv, v_mla_out_g, v_gdn_conv_w, v_gdn_a_log, v_gdn_dt_bias, v_gdn_norm_g, v_w_out, v_mix_post_g, v_ffn2_pre_g, v_ffn2_w_gate, v_ffn2_w_up, v_ffn2_w_down, v_ffn2_post_g):
    args = dict(locals())
    wsh = {n: args[n][0] for n in WEIGHTS}
    msh = {n: args["m_" + n][0] if args["m_" + n].ndim == 3 else args["m_" + n] for n in WEIGHTS}
    vsh = {n: args["v_" + n][0] if args["v_" + n].ndim == 3 else args["v_" + n] for n in WEIGHTS}
    for n in SMALL:
        wsh[n] = args[n]
    mix_shapes = [(n, wsh[n].shape) for n in MIX_BIG]

    early = FFN_BIG[:3]
    gathered = _gather_shards([wsh[n].astype(MM_DTYPE) for n in early] + [_pack([wsh[n] for n in MIX_BIG], MM_DTYPE)])
    full = {n: wsh[n] for n in SMALL}
    for n, gw in zip(early, gathered):
        full[n] = gw
    parts = [_unpack(gathered[-1][q], mix_shapes) for q in range(N_SHARD)]
    for n in MIX_BIG:
        full[n] = jnp.concatenate([parts[q][n] for q in range(N_SHARD)], axis=SHARD_AXIS[n])

    core = lax.axis_index("c").astype(jnp.int32).reshape(1)
    chip = (2 * lax.axis_index("x") + lax.axis_index("y")).astype(jnp.int32).reshape(1)
    late = ([wsh[n].astype(MM_DTYPE) for n in FFN2_BIG], core, chip)
    lsum, grad_x, g = _local_step(x[0], positions, loss_target[0], full, late)
    loss = lax.psum(0.5 * jnp.sum(lsum) / x.shape[-1], ("x", "y", "c"))

    halves = [g[n] for n in FFN_BIG] + [g["mix_pack"]]
    others = _share_halves(halves)
    shared = [_join_halves("join_halves_%d" % i, hm, ho, core) for i, (hm, ho) in enumerate(zip(halves, others))]
    gsh = _unpack(shared[-1], mix_shapes)
    for n, sg_ in zip(FFN_BIG, shared):
        gsh[n] = sg_

    small_shapes = [(n, wsh[n].shape) for n in SMALL]
    pack_small = lambda d: jnp.concatenate(
        [_pad_lanes(d[n].astype(F32), LANES) for n in SMALL] + [jnp.zeros((SMALL_ROWS - len(SMALL), LANES), F32)], axis=0)
    slots = _gather_small(pack_small(g))

    c1 = 1.0 - ADAM_B1 ** ADAM_STEP
    c2 = 1.0 - ADAM_B2 ** ADAM_STEP

    def small_update(wb, mb, vb, s8):
        gs = s8[0:SMALL_ROWS]
        for d in range(1, 8):
            gs = gs + s8[d * SMALL_ROWS:(d + 1) * SMALL_ROWS]
        m2 = ADAM_B1 * mb + (1.0 - ADAM_B1) * gs
        v2 = ADAM_B2 * vb + (1.0 - ADAM_B2) * (gs * gs)
        delta = -ADAM_LR * ((m2 / c1) / (jnp.sqrt(v2 / c2) + ADAM_EPS) + ADAM_WD * wb)
        return gs, delta, m2, v2

    sg, sd, sm, sv_ = _rowwise("adamw_small", small_update,
                               [pack_small(wsh), pack_small(msh), pack_small(vsh)],
                               [slots.reshape(8 * SMALL_ROWS, LANES)], [(LANES, F32)] * 4, [], SMALL_ROWS)
    grads, deltas, new_m, new_v = {}, {}, {}, {}
    for i, (n, shp) in enumerate(small_shapes):
        grads[n], deltas[n] = sg[i:i + 1, :shp[1]], sd[i:i + 1, :shp[1]]
        new_m[n], new_v[n] = sm[i:i + 1, :shp[1]], sv_[i:i + 1, :shp[1]]
    for n in BIG:
        grads[n], deltas[n], new_m[n], new_v[n] = _adamw("adamw_" + n, args[n], gsh[n], args["m_" + n], args["v_" + n],
                                                         _row_tile(args[n].shape[1], 256))

    return (loss, grad_x[None], *[grads[n] for n in WEIGHTS], *[deltas[n] for n in WEIGHTS],
            *[new_m[n] for n in WEIGHTS], *[new_v[n] for n in WEIGHTS])
```

```python
import functools

import jax
import jax.numpy as jnp
from jax import lax
from jax.experimental import pallas as pl
from jax.experimental.pallas import tpu as pltpu

F32 = jnp.float32
BF16 = jnp.bfloat16
MM_DTYPE = BF16
HI = lax.Precision.HIGHEST
MESH = pl.DeviceIdType.MESH

D_MODEL = 1024
D_FF = 2816
N_HEADS = 8
MLA_Q_RANK = 256
MLA_KV_RANK = 128
MLA_NOPE = 64
MLA_ROPE = 32
MLA_V = 64
ROPE_THETA = 10000.0
GDN_DH = 64
GDN_W = N_HEADS * GDN_DH
GDN_CONV = 4
CHUNK = 64
HEAD_LANES = 128
HEADS_PER_STEP = 8
MLA_PAD = N_HEADS * HEAD_LANES
EPS = 1e-6
N_SHARD = 4
LANES = 1024

PIN_QKV = 0
PIN_MLA = 1536
PIN_KPE = 1920
PIN_GATE = 2048
PIN_AB = 2560
PIN_W = 2688
CAT_W = MLA_PAD + GDN_W

ADAM_LR = 0.001
ADAM_B1 = 0.9
ADAM_B2 = 0.999
ADAM_EPS = 1e-08
ADAM_WD = 0.01
ADAM_STEP = 10

VMEM_LIMIT_V7X = 56 * 1024 * 1024

BIG = ["ffn1_w_gate", "ffn1_w_up", "ffn1_w_down", "w_in", "mla_w_uq", "mla_w_ukv", "gdn_conv_w", "w_out",
       "ffn2_w_gate", "ffn2_w_up", "ffn2_w_down"]
FFN_BIG = ["ffn1_w_gate", "ffn1_w_up", "ffn1_w_down", "ffn2_w_gate", "ffn2_w_up", "ffn2_w_down"]
MIX_BIG = ["w_in", "mla_w_uq", "mla_w_ukv", "gdn_conv_w", "w_out"]
SMALL = ["ffn1_pre_g", "ffn1_post_g", "mix_pre_g", "mla_q_norm_g", "mla_kv_norm_g", "mla_out_g", "gdn_a_log",
         "gdn_dt_bias", "gdn_norm_g", "mix_post_g", "ffn2_pre_g", "ffn2_post_g"]
WEIGHTS = ["ffn1_pre_g", "ffn1_w_gate", "ffn1_w_up", "ffn1_w_down", "ffn1_post_g", "mix_pre_g", "w_in",
           "mla_q_norm_g", "mla_w_uq", "mla_kv_norm_g", "mla_w_ukv", "mla_out_g", "gdn_conv_w", "gdn_a_log",
           "gdn_dt_bias", "gdn_norm_g", "w_out", "mix_post_g", "ffn2_pre_g", "ffn2_w_gate", "ffn2_w_up",
           "ffn2_w_down", "ffn2_post_g"]
SHARD_AXIS = {"ffn1_w_gate": 1, "ffn1_w_up": 1, "ffn1_w_down": 0, "w_in": 1, "mla_w_uq": 1, "mla_w_ukv": 1,
              "gdn_conv_w": 1, "w_out": 0, "ffn2_w_gate": 1, "ffn2_w_up": 1, "ffn2_w_down": 0}
SMALL_ROWS = 16


def _params(sem):
    return pltpu.CompilerParams(dimension_semantics=sem, vmem_limit_bytes=VMEM_LIMIT_V7X)


def _pick(dim, pref):
    if dim <= pref:
        return dim
    t = (pref // 128) * 128
    while t >= 128:
        if dim % t == 0:
            return t
        t -= 128
    return dim


ANY_SPEC = pl.BlockSpec(memory_space=pl.ANY)


def _rowwise(name, fn, row_ins, bc_ins, row_outs, acc_outs, tb, wide=None, carry=None):
    ents = []
    for e in row_ins:
        ents.append(e if isinstance(e, tuple) else (e, e.shape[1], 0, 0))
    over = [o[2] for o in row_outs if len(o) == 3]
    rows = over[0] if over else ents[0][0].shape[0]
    steps = rows // tb
    assert steps * tb == rows, (name, rows, tb)
    in_specs, args = [], []
    for a, w, j, r0 in ents:
        in_specs.append(pl.BlockSpec((tb, w), lambda i, j=j, r0=r0: (i + r0, j)))
        args.append(a)
    for b in bc_ins:
        in_specs.append(pl.BlockSpec(b.shape, lambda i: (0, 0)))
        args.append(b)
    n_in = len(args)
    aliases = {}
    if carry is not None:
        in_specs.append(ANY_SPEC)
        args.append(carry)
        aliases = {n_in: 0}
    out_shape = [jax.ShapeDtypeStruct((rows, o[0]), o[1]) for o in row_outs]
    out_specs = [pl.BlockSpec((tb, o[0]), lambda i: (i, 0)) for o in row_outs]
    if wide is not None:
        out_shape[0] = jax.ShapeDtypeStruct((rows, wide[0]), row_outs[0][1])
        out_specs[0] = pl.BlockSpec((tb, row_outs[0][0]), lambda i: (i, wide[1]))
    out_shape += [jax.ShapeDtypeStruct((r, c), F32) for r, c in acc_outs]
    out_specs += [pl.BlockSpec((r, c), lambda i: (0, 0)) for r, c in acc_outs]
    n_ro, n_acc, n_args = len(row_outs), len(acc_outs), len(args)

    def body(*refs):
        vals = fn(*[r[...] for r in refs[:n_in]])
        if not isinstance(vals, (tuple, list)):
            vals = (vals,)
        for r, v in zip(refs[n_args:n_args + n_ro], vals[:n_ro]):
            r[...] = v.astype(r.dtype)
        if n_acc:
            acc_refs = refs[n_args + n_ro:]

            @pl.when(pl.program_id(0) == 0)
            def _():
                for r in acc_refs:
                    r[...] = jnp.zeros(r.shape, r.dtype)

            for r, v in zip(acc_refs, vals[n_ro:]):
                r[...] += v

    outs = pl.pallas_call(body, name=name, grid=(steps,), in_specs=in_specs, out_specs=out_specs,
                          out_shape=out_shape, input_output_aliases=aliases,
                          compiler_params=_params(("arbitrary",)))(*args)
    return outs


def _mm(name, a, b, mode, out_dtype, tm=1024, tn=1024, tk=1024):
    if mode == "nn":
        (m, k), (k2, n) = a.shape, b.shape
    elif mode == "nt":
        (m, k), (n, k2) = a.shape, b.shape
    else:
        (k, m), (k2, n) = a.shape, b.shape
    assert k == k2, (name, a.shape, b.shape)
    tm, tn, tk = _pick(m, tm), _pick(n, tn), _pick(k, tk)
    nk = k // tk
    if mode == "nn":
        a_spec = pl.BlockSpec((tm, tk), lambda i, j, kk: (i, kk))
        b_spec = pl.BlockSpec((tk, tn), lambda i, j, kk: (kk, j))
        dims = (((1,), (0,)), ((), ()))
    elif mode == "nt":
        a_spec = pl.BlockSpec((tm, tk), lambda i, j, kk: (i, kk))
        b_spec = pl.BlockSpec((tn, tk), lambda i, j, kk: (j, kk))
        dims = (((1,), (1,)), ((), ()))
    else:
        a_spec = pl.BlockSpec((tk, tm), lambda i, j, kk: (kk, i))
        b_spec = pl.BlockSpec((tk, tn), lambda i, j, kk: (kk, j))
        dims = (((0,), (0,)), ((), ()))

    def body(a_ref, b_ref, o_ref, acc_ref):
        kk = pl.program_id(2)

        @pl.when(kk == 0)
        def _():
            acc_ref[...] = jnp.zeros(acc_ref.shape, F32)

        acc_ref[...] += lax.dot_general(a_ref[...].astype(MM_DTYPE), b_ref[...].astype(MM_DTYPE), dims,
                                        preferred_element_type=F32)

        @pl.when(kk == nk - 1)
        def _():
            o_ref[...] = acc_ref[...].astype(o_ref.dtype)

    return pl.pallas_call(
        body, name=name, grid=(m // tm, n // tn, nk), in_specs=[a_spec, b_spec],
        out_specs=pl.BlockSpec((tm, tn), lambda i, j, kk: (i, j)),
        out_shape=jax.ShapeDtypeStruct((m, n), out_dtype),
        scratch_shapes=[pltpu.VMEM((tm, tn), F32)],
        compiler_params=_params(("parallel", "parallel", "arbitrary")))(a, b)


def _rms_stats(x, n_real=None):
    n = x.shape[-1] if n_real is None else n_real
    return lax.rsqrt(jnp.sum(x * x, axis=-1, keepdims=True) / n + EPS)


def _rms_bwd(x, r, g, dz, n_real=None):
    n = x.shape[-1] if n_real is None else n_real
    xh = x * r
    dxh = dz * g
    dx = r * (dxh - xh * (jnp.sum(dxh * xh, axis=-1, keepdims=True) / n))
    return dx, jnp.sum(dz * xh, axis=0, keepdims=True)


def _sigmoid(x):
    return 0.5 * jnp.tanh(0.5 * x) + 0.5


def _roll(x, s, axis):
    return pltpu.roll(x, s, axis)


def _rope(x, c, s1, s2):
    return x * c + _roll(x, HEAD_LANES - MLA_ROPE // 2, 1) * s1 + _roll(x, MLA_ROPE // 2, 1) * s2


def _heads_apply(x, fn):
    return jnp.concatenate([fn(x[:, h * HEAD_LANES:(h + 1) * HEAD_LANES]) for h in range(N_HEADS)], axis=1)


ROW_CHUNK = 256


def _row_chunks(rows):
    step = min(ROW_CHUNK, rows)
    return [pl.ds(r, step) for r in range(0, rows, step)]


def _ffn_fwd(tag, x, g_pre, wg, wu, wd, g_post, tm):
    t, d = x.shape
    ns, _, fs = wg.shape
    nt = t // tm
    row = pl.BlockSpec((tm, d), lambda i, q: (i, 0))
    vec = pl.BlockSpec((1, d), lambda i, q: (0, 0))
    act3 = pl.BlockSpec((1, tm, fs), lambda i, q: (q, i, 0))
    wcol = pl.BlockSpec((1, d, fs), lambda i, q: (q, 0, 0))
    wrow = pl.BlockSpec((1, fs, d), lambda i, q: (q, 0, 0))

    def gate_up(x_ref, g_ref, wg_ref, wu_ref, n_ref, a_ref, u_ref, s_ref, n_s):
        @pl.when(pl.program_id(1) == 0)
        def _():
            for r in _row_chunks(tm):
                xb = x_ref[r, :]
                n_s[r, :] = (xb * _rms_stats(xb) * g_ref[...]).astype(MM_DTYPE)
            n_ref[...] = n_s[...]

        for r in _row_chunks(tm):
            n = n_s[r, :]
            a = jnp.dot(n, wg_ref[0], preferred_element_type=F32)
            u = jnp.dot(n, wu_ref[0], preferred_element_type=F32)
            a_ref[0, r, :] = a.astype(a_ref.dtype)
            u_ref[0, r, :] = u.astype(u_ref.dtype)
            s_ref[0, r, :] = ((a * _sigmoid(a)) * u).astype(s_ref.dtype)

    n, a, u, s = pl.pallas_call(
        gate_up, name=tag + "_gate_up", grid=(nt, ns), in_specs=[row, vec, wcol, wcol],
        out_specs=[row, act3, act3, act3],
        out_shape=[jax.ShapeDtypeStruct((t, d), MM_DTYPE)] + [jax.ShapeDtypeStruct((ns, t, fs), MM_DTYPE)] * 3,
        scratch_shapes=[pltpu.VMEM((tm, d), MM_DTYPE)],
        compiler_params=_params(("parallel", "arbitrary")))(x, g_pre, wg, wu)

    def down(s_ref, wd_ref, x_ref, g_ref, h_ref, y_ref, acc):
        q = pl.program_id(1)

        @pl.when(q == 0)
        def _():
            acc[...] = jnp.zeros(acc.shape, F32)

        for r in _row_chunks(tm):
            acc[r, :] += jnp.dot(s_ref[0, r, :], wd_ref[0], preferred_element_type=F32)

        @pl.when(q == ns - 1)
        def _():
            for r in _row_chunks(tm):
                hb = acc[r, :]
                h_ref[r, :] = hb
                y_ref[r, :] = x_ref[r, :] + 0.5 * (hb * _rms_stats(hb) * g_ref[...])

    h, y = pl.pallas_call(
        down, name=tag + "_down", grid=(nt, ns), in_specs=[act3, wrow, row, vec], out_specs=[row, row],
        out_shape=[jax.ShapeDtypeStruct((t, d), F32)] * 2, scratch_shapes=[pltpu.VMEM((tm, d), F32)],
        compiler_params=_params(("parallel", "arbitrary")))(s, wd, x, g_post)
    return y, (x, n, a, u, s, h)


def _carry(body, n_in, n_out, grid, carried):
    if carried is None:
        return body, [], [], [], [], []
    nx_in, nx_out = len(carried.ins), len(carried.outs)

    def wrapped(*refs):
        ins, rest = refs[:n_in], refs[n_in:]
        xi, rest = rest[:nx_in], rest[nx_in:]
        outs, rest = rest[:n_out], rest[n_out:]
        xo, rest = rest[:nx_out], rest[nx_out:]
        scr, sems = rest[:len(rest) - 2], rest[len(rest) - 2:]
        first, last = True, True
        for dim, size in enumerate(grid):
            first = first & (pl.program_id(dim) == 0)
            last = last & (pl.program_id(dim) == size - 1)

        @pl.when(first)
        def _():
            carried.start(xi, xo, *sems)

        body(*ins, *outs, *scr)

        @pl.when(last)
        def _():
            carried.finish(xi, xo, *sems)

    sems = [pltpu.SemaphoreType.DMA((carried.n_sem,)), pltpu.SemaphoreType.DMA((carried.n_sem,))]
    return (wrapped, [HBM_SPEC] * nx_in, [HBM_SPEC] * nx_out, list(carried.outs), sems, list(carried.ins))


def _ffn_bwd(tag, dy, saved, g_pre, wg, wu, wd, g_post, tm, tk, carried_down=None, make_carried_up=None):
    x, n, a, u, s, h = saved
    t, d = x.shape
    ns, _, fs = wg.shape
    nt, nk = t // tm, t // tk
    row = pl.BlockSpec((tm, d), lambda i, q: (i, 0))
    vec = pl.BlockSpec((1, d), lambda i, q: (0, 0))
    act3 = pl.BlockSpec((1, tm, fs), lambda i, q: (q, i, 0))
    wcol = pl.BlockSpec((1, d, fs), lambda i, q: (q, 0, 0))
    wrow = pl.BlockSpec((1, fs, d), lambda i, q: (q, 0, 0))
    nt_dims = (((1,), (1,)), ((), ()))
    tn_dims = (((0,), (0,)), ((), ()))

    def down_b(h_ref, dy_ref, g_ref, wd_ref, a_ref, u_ref, dh_ref, da_ref, du_ref, dg_ref, dh_s):
        i, q = pl.program_id(0), pl.program_id(1)

        @pl.when((i == 0) & (q == 0))
        def _():
            dg_ref[...] = jnp.zeros(dg_ref.shape, F32)

        @pl.when(q == 0)
        def _():
            for r in _row_chunks(tm):
                hb = h_ref[r, :]
                dh, dg = _rms_bwd(hb, _rms_stats(hb), g_ref[...], 0.5 * dy_ref[r, :])
                dh_s[r, :] = dh.astype(MM_DTYPE)
                dg_ref[...] += dg
            dh_ref[...] = dh_s[...]

        for r in _row_chunks(tm):
            ds = lax.dot_general(dh_s[r, :], wd_ref[0], nt_dims, preferred_element_type=F32)
            ab, ub = a_ref[0, r, :].astype(F32), u_ref[0, r, :].astype(F32)
            sg = _sigmoid(ab)
            da_ref[0, r, :] = (ds * ub * (sg * (1.0 + ab * (1.0 - sg)))).astype(da_ref.dtype)
            du_ref[0, r, :] = (ds * (ab * sg)).astype(du_ref.dtype)

    down_b, x_in, x_out, x_shape, x_scr, x_args = _carry(down_b, 6, 4, (nt, ns), carried_down)
    dh, da, du, dg_post, *from_down = pl.pallas_call(
        down_b, name=tag + "_down_b", grid=(nt, ns), in_specs=[row, row, vec, wrow, act3, act3] + x_in,
        out_specs=[row, act3, act3, vec] + x_out,
        out_shape=[jax.ShapeDtypeStruct((t, d), MM_DTYPE)] + [jax.ShapeDtypeStruct((ns, t, fs), MM_DTYPE)] * 2
        + [jax.ShapeDtypeStruct((1, d), F32)] + x_shape,
        scratch_shapes=[pltpu.VMEM((tm, d), MM_DTYPE)] + x_scr,
        compiler_params=_params(("arbitrary", "arbitrary")))(h, dy, g_post, wd, a, u, *x_args)

    def down_w(s_ref, dh_ref, dw_ref, acc):
        kk = pl.program_id(1)

        @pl.when(kk == 0)
        def _():
            acc[...] = jnp.zeros(acc.shape, F32)

        acc[...] += lax.dot_general(s_ref[0], dh_ref[...], tn_dims, preferred_element_type=F32)

        @pl.when(kk == nk - 1)
        def _():
            dw_ref[0] = acc[...].astype(dw_ref.dtype)

    dwd = pl.pallas_call(
        down_w, name=tag + "_down_w", grid=(ns, nk),
        in_specs=[pl.BlockSpec((1, tk, fs), lambda q, kk: (q, kk, 0)), pl.BlockSpec((tk, d), lambda q, kk: (kk, 0))],
        out_specs=pl.BlockSpec((1, fs, d), lambda q, kk: (q, 0, 0)),
        out_shape=jax.ShapeDtypeStruct((ns, fs, d), MM_DTYPE), scratch_shapes=[pltpu.VMEM((fs, d), F32)],
        compiler_params=_params(("parallel", "arbitrary")))(s, dh)

    def gate_up_b(da_ref, du_ref, wg_ref, wu_ref, x_ref, dy_ref, g_ref, dx_ref, dg_ref, acc):
        i, q = pl.program_id(0), pl.program_id(1)

        @pl.when((i == 0) & (q == 0))
        def _():
            dg_ref[...] = jnp.zeros(dg_ref.shape, F32)

        @pl.when(q == 0)
        def _():
            acc[...] = jnp.zeros(acc.shape, F32)

        for r in _row_chunks(tm):
            acc[r, :] += (lax.dot_general(da_ref[0, r, :], wg_ref[0], nt_dims, preferred_element_type=F32)
                          + lax.dot_general(du_ref[0, r, :], wu_ref[0], nt_dims, preferred_element_type=F32))

        @pl.when(q == ns - 1)
        def _():
            for r in _row_chunks(tm):
                xb = x_ref[r, :]
                dx, dg = _rms_bwd(xb, _rms_stats(xb), g_ref[...], acc[r, :])
                dx_ref[r, :] = dy_ref[r, :] + dx
                dg_ref[...] += dg

    def gate_up_w(n_ref, da_ref, du_ref, dwg_ref, dwu_ref, acc_g, acc_u):
        kk = pl.program_id(1)

        @pl.when(kk == 0)
        def _():
            acc_g[...] = jnp.zeros(acc_g.shape, F32)
            acc_u[...] = jnp.zeros(acc_u.shape, F32)

        nb = n_ref[...]
        acc_g[...] += lax.dot_general(nb, da_ref[0], tn_dims, preferred_element_type=F32)
        acc_u[...] += lax.dot_general(nb, du_ref[0], tn_dims, preferred_element_type=F32)

        @pl.when(kk == nk - 1)
        def _():
            dwg_ref[0] = acc_g[...].astype(dwg_ref.dtype)
            dwu_ref[0] = acc_u[...].astype(dwu_ref.dtype)

    k3 = pl.BlockSpec((1, tk, fs), lambda q, kk: (q, kk, 0))
    wout = pl.BlockSpec((1, d, fs), lambda q, kk: (q, 0, 0))
    dwg, dwu = pl.pallas_call(
        gate_up_w, name=tag + "_gate_up_w", grid=(ns, nk),
        in_specs=[pl.BlockSpec((tk, d), lambda q, kk: (kk, 0)), k3, k3], out_specs=[wout, wout],
        out_shape=[jax.ShapeDtypeStruct((ns, d, fs), MM_DTYPE)] * 2,
        scratch_shapes=[pltpu.VMEM((d, fs), F32)] * 2,
        compiler_params=_params(("parallel", "arbitrary")))(n, da, du)

    carried_up = make_carried_up(dwg, dwu, dwd) if make_carried_up else None
    gate_up_b, x_in, x_out, x_shape, x_scr, x_args = _carry(gate_up_b, 7, 2, (nt, ns), carried_up)
    dx, dg_pre, *from_up = pl.pallas_call(
        gate_up_b, name=tag + "_gate_up_b", grid=(nt, ns), in_specs=[act3, act3, wcol, wcol, row, row, vec] + x_in,
        out_specs=[row, vec] + x_out,
        out_shape=[jax.ShapeDtypeStruct((t, d), F32), jax.ShapeDtypeStruct((1, d), F32)] + x_shape,
        scratch_shapes=[pltpu.VMEM((tm, d), F32)] + x_scr,
        compiler_params=_params(("arbitrary", "arbitrary")))(da, du, wg, wu, x, dy, g_pre, *x_args)
    return dx, dg_pre, dwg, dwu, dwd, dg_post, from_down, from_up


NEG = -1e30


def _attn_scale():
    return (MLA_NOPE + MLA_ROPE) ** -0.5


def _causal_pairs(nq, by_key):
    if by_key:
        pairs = [(qi, ki) for ki in range(nq) for qi in range(ki, nq)]
    else:
        pairs = [(qi, ki) for qi in range(nq) for ki in range(qi + 1)]
    return jnp.asarray([p[0] for p in pairs], jnp.int32), jnp.asarray([p[1] for p in pairs], jnp.int32)


def _below_diagonal(shape):
    return lax.broadcasted_iota(jnp.int32, shape, 1) <= lax.broadcasted_iota(jnp.int32, shape, 0)


def _attn_call(name, body, tables, args, in_kinds, out_kinds, scratch, t, tq, carried=None):
    qmap = lambda h, p, qt, kt: (qt[p], h)
    kmap = lambda h, p, qt, kt: (kt[p], h)
    width = HEADS_PER_STEP * HEAD_LANES
    spec = lambda kind: pl.BlockSpec((tq, width), qmap if kind == "q" else kmap)
    n_pairs = tables[0].shape[0]
    n_groups = N_HEADS // HEADS_PER_STEP
    n_in, n_out, n_scr = len(in_kinds), len(out_kinds), scratch
    x_ins = list(carried.ins) if carried else []
    x_outs = list(carried.outs) if carried else []
    x_scr = [pltpu.SemaphoreType.DMA((carried.n_sem,)), pltpu.SemaphoreType.DMA((carried.n_sem,))] if carried else []

    def full_body(qt, kt, *refs):
        ins, refs = refs[:n_in], refs[n_in:]
        xi, refs = refs[:len(x_ins)], refs[len(x_ins):]
        outs, refs = refs[:n_out], refs[n_out:]
        xo, refs = refs[:len(x_outs)], refs[len(x_outs):]
        scr, sems = refs[:n_scr], refs[n_scr:]
        if carried:
            @pl.when((pl.program_id(0) == 0) & (pl.program_id(1) == 0))
            def _():
                carried.start(xi, xo, *sems)

        heads = [tuple(r.at[:, pl.ds(hh * HEAD_LANES, HEAD_LANES)] for r in (*ins, *outs, *scr))
                 for hh in range(HEADS_PER_STEP)]
        body(qt, kt, heads)
        if carried:
            @pl.when((pl.program_id(0) == n_groups - 1) & (pl.program_id(1) == n_pairs - 1))
            def _():
                carried.finish(xi, xo, *sems)

    grid_spec = pltpu.PrefetchScalarGridSpec(
        num_scalar_prefetch=2, grid=(n_groups, n_pairs),
        in_specs=[spec(kd) for kd in in_kinds] + [HBM_SPEC] * len(x_ins),
        out_specs=[spec(kd) for kd in out_kinds] + [HBM_SPEC] * len(x_outs),
        scratch_shapes=[pltpu.VMEM((tq, width), F32)] * n_scr + x_scr)
    return pl.pallas_call(full_body, name=name, grid_spec=grid_spec,
                          out_shape=[jax.ShapeDtypeStruct((t, MLA_PAD), F32) for _ in out_kinds] + x_outs,
                          compiler_params=_params(("arbitrary", "arbitrary")))(*tables, *args, *x_ins)


class _Carried:
    def __init__(self, ins, outs, n_sem, start, finish):
        self.ins, self.outs, self.n_sem, self.start, self.finish = ins, outs, n_sem, start, finish


def _attn_fwd(q, k, v, tq, carried=None):
    t = q.shape[0]
    nq = t // tq

    def body(qt, kt, heads):
        p_id = pl.program_id(1)
        qi, ki = qt[p_id], kt[p_id]

        @pl.when(ki == 0)
        def _():
            for _, _, _, _, _, m_s, l_s, acc_s in heads:
                m_s[...] = jnp.full(m_s.shape, NEG, F32)
                l_s[...] = jnp.zeros(l_s.shape, F32)
                acc_s[...] = jnp.zeros(acc_s.shape, F32)

        def update(diagonal):
            for q_ref, k_ref, v_ref, _, _, m_s, l_s, acc_s in heads:
                s = lax.dot_general(q_ref[...], k_ref[...], (((1,), (1,)), ((), ())), preferred_element_type=F32)
                if diagonal:
                    s = jnp.where(_below_diagonal(s.shape), s, NEG)
                m_old = m_s[...]
                m_new = jnp.maximum(m_old, jnp.max(s, axis=1, keepdims=True))
                alpha = jnp.exp(m_old - m_new)
                p = jnp.exp(s - m_new[:, :1])
                l_s[...] = l_s[...] * alpha + jnp.sum(p, axis=1, keepdims=True)
                acc_s[...] = acc_s[...] * alpha + jnp.dot(p.astype(MM_DTYPE), v_ref[...], preferred_element_type=F32)
                m_s[...] = m_new

        @pl.when(ki < qi)
        def _():
            update(False)

        @pl.when(ki == qi)
        def _():
            update(True)
            for _, _, _, o_ref, lse_ref, m_s, l_s, acc_s in heads:
                o_ref[...] = acc_s[...] / l_s[...]
                lse_ref[...] = m_s[...] + jnp.log(l_s[...])

    return _attn_call("mla_attn_fwd", body, _causal_pairs(nq, False), (q, k, v), "qkk", "qq", 3, t, tq, carried)


def _attn_probs(q, k, lse, diagonal):
    s = lax.dot_general(q, k, (((1,), (1,)), ((), ())), preferred_element_type=F32)
    p = jnp.exp(s - lse[:, :1])
    return jnp.where(_below_diagonal(s.shape), p, 0.0) if diagonal else p


def _attn_bwd_dq(q, k, v, do, lse, delta, tq):
    t = q.shape[0]
    nq = t // tq

    def body(qt, kt, heads):
        p_id = pl.program_id(1)
        qi, ki = qt[p_id], kt[p_id]

        @pl.when(ki == 0)
        def _():
            for refs in heads:
                refs[-1][...] = jnp.zeros(refs[-1].shape, F32)

        def step(diagonal):
            for q_ref, k_ref, v_ref, do_ref, lse_ref, dl_ref, _, acc_s in heads:
                p = _attn_probs(q_ref[...], k_ref[...], lse_ref[...], diagonal)
                dp = lax.dot_general(do_ref[...], v_ref[...], (((1,), (1,)), ((), ())), preferred_element_type=F32)
                ds = p * (dp - dl_ref[...][:, :1])
                acc_s[...] += jnp.dot(ds.astype(MM_DTYPE), k_ref[...], preferred_element_type=F32)

        @pl.when(ki < qi)
        def _():
            step(False)

        @pl.when(ki == qi)
        def _():
            step(True)
            for refs in heads:
                refs[-2][...] = refs[-1][...]

    return _attn_call("mla_attn_bwd_dq", body, _causal_pairs(nq, False), (q, k, v, do, lse, delta), "qkkqqq", "q",
                      1, t, tq)[0]


def _attn_bwd_dkv(q, k, v, do, lse, delta, tq, carried=None):
    t = q.shape[0]
    nq = t // tq

    def body(qt, kt, heads):
        p_id = pl.program_id(1)
        qi, ki = qt[p_id], kt[p_id]

        def step(diagonal):
            for q_ref, k_ref, v_ref, do_ref, lse_ref, dl_ref, _, _, dk_s, dv_s in heads:
                p = _attn_probs(q_ref[...], k_ref[...], lse_ref[...], diagonal)
                dv_s[...] += lax.dot_general(p.astype(MM_DTYPE), do_ref[...], (((0,), (0,)), ((), ())),
                                             preferred_element_type=F32)
                dp = lax.dot_general(do_ref[...], v_ref[...], (((1,), (1,)), ((), ())), preferred_element_type=F32)
                ds = p * (dp - dl_ref[...][:, :1])
                dk_s[...] += lax.dot_general(ds.astype(MM_DTYPE), q_ref[...], (((0,), (0,)), ((), ())),
                                             preferred_element_type=F32)

        @pl.when(qi == ki)
        def _():
            for refs in heads:
                refs[-2][...] = jnp.zeros(refs[-2].shape, F32)
                refs[-1][...] = jnp.zeros(refs[-1].shape, F32)
            step(True)

        @pl.when(qi > ki)
        def _():
            step(False)

        @pl.when(qi == nq - 1)
        def _():
            for refs in heads:
                refs[-4][...] = refs[-2][...]
                refs[-3][...] = refs[-1][...]

    return _attn_call("mla_attn_bwd_dkv", body, _causal_pairs(nq, True), (q, k, v, do, lse, delta), "qkkqqq", "kk",
                      2, t, tq, carried)


def _dotf(a, b, dims=(((1,), (0,)), ((), ()))):
    return lax.dot_general(a, b, dims, preferred_element_type=F32, precision=HI)


def _dot1(a, b, dims=(((1,), (0,)), ((), ()))):
    return lax.dot_general(a.astype(MM_DTYPE), b.astype(MM_DTYPE), dims, preferred_element_type=F32)


def _dot3(a, b, dims=(((1,), (0,)), ((), ()))):
    return lax.dot_general(a, b, dims, preferred_element_type=F32, precision=lax.Precision.HIGH)


NN3 = (((2,), (1,)), ((0,), (0,)))
NT3 = (((2,), (2,)), ((0,), (0,)))
TN3 = (((1,), (1,)), ((0,), (0,)))


def _tri_masks(nh):
    shape = (nh, CHUNK, CHUNK)
    return lax.broadcasted_iota(jnp.int32, shape, 1), lax.broadcasted_iota(jnp.int32, shape, 2)


def _gdn_chunk_common(k, gcc, bb, row, col, dot=_dot1):
    tril = row >= col
    gcr = jnp.swapaxes(gcc, 1, 2)
    dm = jnp.exp(jnp.where(tril, gcc - gcr, NEG))
    kb = k * bb
    lm = jnp.where(row > col, dot(kb, k, NT3) * dm, 0.0)
    return dm, kb, lm


def _unit_lower_inverse(lm, eye):
    t = eye - lm
    p = lm
    for _ in range(CHUNK.bit_length() - 2):
        p = _dot3(p, p, NN3)
        t = t + _dot3(t, p, NN3)
    return t


def _chunk_sum_matrix(tb, upper):
    r = lax.broadcasted_iota(jnp.int32, (tb, tb), 0)
    c = lax.broadcasted_iota(jnp.int32, (tb, tb), 1)
    same = (r // CHUNK) == (c // CHUNK)
    return (same & ((c >= r) if upper else (c <= r))).astype(F32)


def _gdn_fwd(q, k, v, gb, bb):
    nh, t, dh = q.shape
    nchunk = t // CHUNK

    def body(q_ref, k_ref, v_ref, g_ref, b_ref, o_ref, sall_ref, tall_ref, s_s):
        @pl.when(pl.program_id(0) == 0)
        def _():
            s_s[...] = jnp.zeros(s_s.shape, F32)

        row, col = _tri_masks(nh)
        qh, kh, vh, bbh, gcc = q_ref[...], k_ref[...], v_ref[...], b_ref[...], g_ref[...]
        dm, kb, lm = _gdn_chunk_common(kh, gcc, bbh, row, col)
        eg = jnp.exp(gcc)
        glr = gcc[:, CHUNK - 1:CHUNK, :]
        th = _unit_lower_inverse(lm, (row == col).astype(F32))
        w = _dot1(th, kb * eg, NN3)
        u = _dot1(th, vh * bbh, NN3)
        at = jnp.where(row >= col, _dot1(qh, kh, NT3) * dm, 0.0)
        sh = s_s[...]
        vn = u - _dot1(w, sh, NN3)
        o_ref[...] = _dot1(qh * eg, sh, NN3) + _dot1(at, vn, NN3)
        kd = kh * jnp.exp(glr - gcc)
        sall_ref[:, 0] = sh
        tall_ref[...] = th
        s_s[...] = sh * jnp.exp(glr) + _dot1(kd, vn, TN3)

    blk = pl.BlockSpec((nh, CHUNK, dh), lambda n: (0, n, 0))
    return pl.pallas_call(
        body, name="gdn_fwd", grid=(nchunk,), in_specs=[blk] * 5,
        out_specs=[blk, pl.BlockSpec((nh, 1, dh, dh), lambda n: (0, n, 0, 0)), blk],
        out_shape=[jax.ShapeDtypeStruct((nh, t, dh), F32), jax.ShapeDtypeStruct((nh, nchunk, dh, dh), F32),
                   jax.ShapeDtypeStruct((nh, t, CHUNK), F32)],
        scratch_shapes=[pltpu.VMEM((nh, dh, dh), F32)],
        compiler_params=_params(("arbitrary",)))(q, k, v, gb, bb)


def _gdn_bwd(q, k, v, gb, bb, sall, tall, do):
    nh, t, dh = q.shape
    nchunk = t // CHUNK

    def body(q_ref, k_ref, v_ref, g_ref, b_ref, sall_ref, tall_ref, do_ref,
             dq_ref, dk_ref, dv_ref, dg_ref, db_ref, ds_s):
        @pl.when(pl.program_id(0) == 0)
        def _():
            ds_s[...] = jnp.zeros(ds_s.shape, F32)

        row, col = _tri_masks(nh)
        tril, stril = row >= col, row > col
        rsum = lambda x: jnp.sum(x, axis=2, keepdims=True)
        qh, kh, vh, gcc, bbh = q_ref[...], k_ref[...], v_ref[...], g_ref[...], b_ref[...]
        sh, th, doh, dsp = sall_ref[:, 0], tall_ref[...], do_ref[...], ds_s[...]
        dm, kb, lm = _gdn_chunk_common(kh, gcc, bbh, row, col, _dot3)
        eg = jnp.exp(gcc)
        glr = gcc[:, CHUNK - 1:CHUNK, :]
        glv = jnp.exp(glr)
        egl = jnp.exp(glr - gcc)
        rw, ru = kb * eg, vh * bbh
        w, u = _dot3(th, rw, NN3), _dot3(th, ru, NN3)
        at = jnp.where(tril, _dot3(qh, kh, NT3) * dm, 0.0)
        qd, kd = qh * eg, kh * egl
        vn = u - _dot3(w, sh, NN3)
        dgl = jnp.sum(rsum(dsp * sh), axis=1, keepdims=True)
        dkd = _dot3(vn, dsp, NT3)
        dvn = _dot3(kd, dsp, NN3)
        dqd = _dot3(doh, sh, NT3)
        dat = jnp.where(tril, _dot3(doh, vn, NT3), 0.0)
        dvn = dvn + _dot3(at, doh, TN3)
        dw = -_dot3(dvn, sh, NT3)
        ds_s[...] = dsp * glv + _dot3(qd, doh, TN3) - _dot3(w, dvn, TN3)
        dpa = dat * dm
        dq_ref[...] = _dot3(dpa, kh, NN3) + dqd * eg
        dk = _dot3(dpa, qh, TN3) + dkd * egl
        t6 = rsum(dkd * kd)
        dgam = rsum(dqd * qd) - t6
        dgam_last = jnp.sum(t6, axis=1, keepdims=True) + dgl * glv
        drw = _dot3(th, dw, TN3)
        dru = _dot3(th, dvn, TN3)
        dl = -jnp.where(stril, _dot3(drw, w, NT3) + _dot3(dru, u, NT3), 0.0)
        dgam = dgam + rsum(drw * rw)
        dv_ref[...] = dru * bbh
        dp2 = dl * dm
        dkb = drw * eg + _dot3(dp2, kh, NN3)
        dk_ref[...] = dk + _dot3(dp2, kb, TN3) + dkb * bbh
        db_ref[...] = rsum(dru * vh) + rsum(dkb * kh) + jnp.zeros((nh, CHUNK, dh), F32)
        e = dat * at + dl * lm
        dgam_b = dgam + rsum(e) - _dotf(e, jnp.ones((nh, CHUNK, CHUNK), F32), TN3)
        dgam_b = dgam_b + jnp.where(row == CHUNK - 1, dgam_last, 0.0)
        dg_ref[...] = dgam_b

    rev = lambda n: (0, nchunk - 1 - n, 0)
    blk = pl.BlockSpec((nh, CHUNK, dh), rev)
    sblk = pl.BlockSpec((nh, 1, dh, dh), lambda n: (0, nchunk - 1 - n, 0, 0))
    out = jax.ShapeDtypeStruct((nh, t, dh), F32)
    return pl.pallas_call(
        body, name="gdn_bwd", grid=(nchunk,), in_specs=[blk] * 5 + [sblk, blk, blk], out_specs=[blk] * 5,
        out_shape=[out] * 5, scratch_shapes=[pltpu.VMEM((nh, dh, dh), F32)],
        compiler_params=_params(("arbitrary",)))(q, k, v, gb, bb, sall, tall, do)


def _group_ones():
    r = lax.broadcasted_iota(jnp.int32, (GDN_W, GDN_W), 0) // GDN_DH
    c = lax.broadcasted_iota(jnp.int32, (GDN_W, GDN_W), 1) // GDN_DH
    return (r == c).astype(F32)


def _conv_taps(x, xprev, w, has_prev):
    row = lax.broadcasted_iota(jnp.int32, x.shape, 0)
    out = x * w[GDN_CONV - 1:GDN_CONV, :]
    for s in range(1, GDN_CONV):
        sh = jnp.where(row >= s, _roll(x, s, 0), _roll(xprev, s, 0) * has_prev)
        out = out + sh * w[GDN_CONV - 1 - s:GDN_CONV - s, :]
    return out


def _head_cols(x, h):
    return x[:, h * GDN_DH:(h + 1) * GDN_DH]


def _heads_spec(tb):
    return pl.BlockSpec((N_HEADS, tb, GDN_DH), lambda i: (0, i, 0))


def _mixer_fwd(x, positions, w, tb, carried=None):
    t, d = x.shape
    tables = _rope_tables(positions)

    def pre(xb, g):
        return (xb * _rms_stats(xb) * g,)

    (hn,) = _rowwise("mix_pre", pre, [x], [w["mix_pre_g"]], [(d, BF16)], [], tb)
    proj = _mm("mix_in", hn, w["w_in_pad"], "nn", F32)

    def mla_pre(p0, gq, gkv):
        cq, ckv = p0[:, :MLA_Q_RANK], p0[:, MLA_Q_RANK:MLA_Q_RANK + MLA_KV_RANK]
        return cq * _rms_stats(cq) * gq, ckv * _rms_stats(ckv) * gkv

    nq, nkv = _rowwise("mla_pre", mla_pre, [(proj, 512, PIN_MLA // 512, 0)],
                       [w["mla_q_norm_g"], w["mla_kv_norm_g"]], [(MLA_Q_RANK, BF16), (MLA_KV_RANK, BF16)], [], tb)
    qraw = _mm("mla_uq", nq, w["w_uq_pad"], "nn", F32)
    kv = _mm("mla_ukv", nkv, w["w_kv_pad"], "nn", F32)

    def rope_f(qr, kn, vv, kpe, c, s1, s2):
        qo = _heads_apply(qr, lambda xh: _rope(xh, c, s1, s2)) * _attn_scale()
        kp = _rope(kpe, c, s1, s2)
        return qo, kn + jnp.tile(kp, (1, N_HEADS)), vv

    q, k, v = _rowwise("mla_rope", rope_f,
                       [qraw, (kv, MLA_PAD, 0, 0), (kv, MLA_PAD, 1, 0), (proj, HEAD_LANES, PIN_KPE // HEAD_LANES, 0),
                        tables[0], tables[1], tables[2]], [],
                       [(MLA_PAD, BF16)] * 3, [], tb // 2)
    tq = min(512, t)
    o, lse, *carried_out = _attn_fwd(q, k, v, tq, carried)

    def mla_post(ob, g):
        return (ob * _rms_stats(ob, N_HEADS * MLA_V) * g,)

    (cat,) = _rowwise("mla_post", mla_post, [o], [w["mla_out_g_pad"]], [(MLA_PAD, BF16)], [], tb, wide=(CAT_W, 0))

    gones = _group_ones()
    steps = t // tb

    def gdn_pre(xq, xk, xv, pq, pk, pv, cw, go, has_prev):
        outs = []
        for j, (xc, xp) in enumerate(((xq, pq), (xk, pk), (xv, pv))):
            c = _conv_taps(xc, xp, cw[:, j * GDN_W:(j + 1) * GDN_W], has_prev)
            a = c * _sigmoid(c)
            if j < 2:
                rn = lax.rsqrt(_dotf(a * a, go) + EPS)
                a = a * rn
                if j == 0:
                    a = a * (GDN_DH ** -0.5)
            outs.append(a)
        return tuple(outs)

    qh, kh, vh = _gdn_pre_call("gdn_pre", gdn_pre, proj, w["conv_w"], gones, tb, steps)
    heads_shape = jax.ShapeDtypeStruct((N_HEADS, t, GDN_DH), F32)
    lanes_shape = jax.ShapeDtypeStruct((t, HEAD_LANES), F32)
    lanes_spec = pl.BlockSpec((tb, HEAD_LANES), lambda i: (i, 0))
    vec_spec = lambda n: pl.BlockSpec((1, n), lambda i: (0, 0))

    def gate_f(ab_ref, al_ref, dt_ref, g_ref, b_ref, gh_ref, bh_ref):
        g, b = _gb_fwd(ab_ref[...], al_ref[...], dt_ref[...])
        g_ref[...] = g
        b_ref[...] = b
        gc = _dotf(_chunk_sum_matrix(tb, False), g)
        for h in range(N_HEADS):
            gh_ref[h] = jnp.broadcast_to(gc[:, h:h + 1], (tb, GDN_DH))
            bh_ref[h] = jnp.broadcast_to(b[:, N_HEADS + h:N_HEADS + h + 1], (tb, GDN_DH))

    g128, b128, gbh, bbh = pl.pallas_call(
        gate_f, name="gdn_gate_f", grid=(steps,),
        in_specs=[pl.BlockSpec((tb, HEAD_LANES), lambda i: (i, PIN_AB // HEAD_LANES)), vec_spec(HEAD_LANES),
                  vec_spec(HEAD_LANES)],
        out_specs=[lanes_spec, lanes_spec, _heads_spec(tb), _heads_spec(tb)],
        out_shape=[lanes_shape, lanes_shape, heads_shape, heads_shape],
        compiler_params=_params(("arbitrary",)))(proj, w["a_log_pad"], w["dt_bias_pad"])
    oh, sall, tall = _gdn_fwd(qh, kh, vh, gbh, bbh)

    def gdn_post(o_ref, gt_ref, g_ref, cat_in, cat_ref):
        gt, g = gt_ref[...], g_ref[...]
        outs = []
        for h in range(N_HEADS):
            ob, gth = o_ref[h], _head_cols(gt, h)
            outs.append(ob * _rms_stats(ob) * g * (gth * _sigmoid(gth)))
        cat_ref[...] = jnp.concatenate(outs, axis=1).astype(cat_ref.dtype)

    gate_spec = pl.BlockSpec((tb, GDN_W), lambda i: (i, PIN_GATE // GDN_W))
    cat = pl.pallas_call(
        gdn_post, name="gdn_post", grid=(steps,),
        in_specs=[_heads_spec(tb), gate_spec, vec_spec(GDN_DH), ANY_SPEC],
        out_specs=pl.BlockSpec((tb, GDN_W), lambda i: (i, MLA_PAD // GDN_W)),
        out_shape=jax.ShapeDtypeStruct((t, CAT_W), BF16), input_output_aliases={3: 0},
        compiler_params=_params(("arbitrary",)))(oh, proj, w["gdn_norm_g"], cat)
    mixed = _mm("mix_out", cat, w["w_out_pad"], "nn", F32)

    def post(xb, hb, g):
        return (xb + hb * _rms_stats(hb) * g,)

    (y,) = _rowwise("mix_post", post, [x, mixed], [w["mix_post_g"]], [(d, F32)], [], tb)
    saved = dict(x=x, hn=hn, proj=proj, nq=nq, nkv=nkv, q=q, k=k, v=v, o=o, lse=lse, qh=qh, kh=kh, vh=vh,
                 gbh=gbh, bbh=bbh, oh=oh, sall=sall, tall=tall, cat=cat, mixed=mixed,
                 tables=tables, g128=g128, b128=b128)
    return y, saved, carried_out


def _qkv_specs(tb):
    base = PIN_QKV // GDN_W
    cur = [pl.BlockSpec((tb, GDN_W), lambda i, j=j: (i, base + j)) for j in range(3)]
    prev = [pl.BlockSpec((tb, GDN_W), lambda i, j=j: (jnp.maximum(i - 1, 0), base + j)) for j in range(3)]
    return cur + prev


def _gdn_pre_call(name, fn, proj, conv_w, gones, tb, steps):
    t = proj.shape[0]

    def body(xq, xk, xv, pq, pk, pv, cw, go, oq, ok, ov):
        has_prev = jnp.where(pl.program_id(0) == 0, 0.0, 1.0)
        outs = fn(xq[...], xk[...], xv[...], pq[...], pk[...], pv[...], cw[...], go[...], has_prev)
        for r, val in zip((oq, ok, ov), outs):
            for h in range(N_HEADS):
                r[h] = _head_cols(val, h)

    return pl.pallas_call(
        body, name=name, grid=(steps,),
        in_specs=_qkv_specs(tb) + [pl.BlockSpec(conv_w.shape, lambda i: (0, 0)),
                                   pl.BlockSpec(gones.shape, lambda i: (0, 0))],
        out_specs=[_heads_spec(tb)] * 3,
        out_shape=[jax.ShapeDtypeStruct((N_HEADS, t, GDN_DH), F32)] * 3,
        compiler_params=_params(("arbitrary",)))(proj, proj, proj, proj, proj, proj, conv_w, gones)


def _softplus(x):
    return jnp.maximum(x, 0.0) + jnp.log1p(jnp.exp(-jnp.abs(x)))


def _gb_fwd(ab, a_log, dt_bias):
    g = -jnp.exp(a_log) * _softplus(ab + dt_bias)
    return g, _sigmoid(ab)


def _rope_tables(positions):
    half = MLA_ROPE // 2
    freqs = ROPE_THETA ** (-jnp.arange(half, dtype=F32) / half)
    ang = positions.reshape(-1).astype(F32)[:, None] * freqs
    cos, sin = jnp.cos(ang), jnp.sin(ang)
    t = ang.shape[0]
    one = jnp.ones((t, MLA_NOPE), F32)
    z16, z32, z64 = jnp.zeros((t, half), F32), jnp.zeros((t, MLA_ROPE), F32), jnp.zeros((t, MLA_NOPE), F32)
    c = jnp.concatenate([one, cos, cos, jnp.ones((t, MLA_ROPE), F32)], axis=1)
    s1 = jnp.concatenate([z64, -sin, z16, z32], axis=1)
    s2 = jnp.concatenate([z64, z16, sin, z32], axis=1)
    return c, s1, s2


def _mixer_bwd(dy, sv, w, tb, carried=None):
    x, proj = sv["x"], sv["proj"]
    t, d = x.shape
    c, s1, s2 = sv["tables"]
    grads = {}

    def post_b(hb, dyb, g):
        return _rms_bwd(hb, _rms_stats(hb), g, dyb)

    dmixed, grads["mix_post_g"] = _rowwise("mix_post_b", post_b, [sv["mixed"], dy], [w["mix_post_g"]],
                                           [(d, BF16)], [(1, d)], tb)
    dcat = _mm("mix_out_bx", dmixed, w["w_out_pad"], "nt", F32)
    grads["w_out_pad"] = _mm("mix_out_bw", sv["cat"], dmixed, "tn", F32)
    steps = t // tb
    vec_spec = lambda n: pl.BlockSpec((1, n), lambda i: (0, 0))

    def gdn_post_b(o_ref, gt_ref, do_ref, g_ref, dproj_ref, doh_ref, dg_ref):
        @pl.when(pl.program_id(0) == 0)
        def _():
            dg_ref[...] = jnp.zeros(dg_ref.shape, F32)

        gt, dob, g = gt_ref[...], do_ref[...], g_ref[...]
        dgates = []
        for h in range(N_HEADS):
            ob, gth, dobh = o_ref[h], _head_cols(gt, h), _head_cols(dob, h)
            sg = _sigmoid(gth)
            r = _rms_stats(ob)
            dxo, dg = _rms_bwd(ob, r, g, dobh * (gth * sg))
            doh_ref[h] = dxo
            dg_ref[...] += dg
            dgates.append(dobh * (ob * r * g) * (sg * (1.0 + gth * (1.0 - sg))))
        dproj_ref[...] = jnp.concatenate(dgates, axis=1).astype(dproj_ref.dtype)

    dproj, doh, grads["gdn_norm_g"] = pl.pallas_call(
        gdn_post_b, name="gdn_post_b", grid=(steps,),
        in_specs=[_heads_spec(tb), pl.BlockSpec((tb, GDN_W), lambda i: (i, PIN_GATE // GDN_W)),
                  pl.BlockSpec((tb, GDN_W), lambda i: (i, MLA_PAD // GDN_W)), vec_spec(GDN_DH)],
        out_specs=[pl.BlockSpec((tb, GDN_W), lambda i: (i, PIN_GATE // GDN_W)), _heads_spec(tb), vec_spec(GDN_DH)],
        out_shape=[jax.ShapeDtypeStruct((t, PIN_W), BF16), jax.ShapeDtypeStruct((N_HEADS, t, GDN_DH), F32),
                   jax.ShapeDtypeStruct((1, GDN_DH), F32)],
        compiler_params=_params(("arbitrary",)))(sv["oh"], proj, dcat, w["gdn_norm_g"])

    def mla_post_b(ob, dmo, g):
        do, dg = _rms_bwd(ob, _rms_stats(ob, N_HEADS * MLA_V), g, dmo, N_HEADS * MLA_V)
        prod = do * ob
        delta = _heads_apply(prod, lambda ph: jnp.sum(ph, axis=1, keepdims=True) + jnp.zeros_like(ph))
        return do, delta, dg

    do, delta, grads["mla_out_g_pad"] = _rowwise(
        "mla_post_b", mla_post_b, [sv["o"], (dcat, MLA_PAD, 0, 0)], [w["mla_out_g_pad"]],
        [(MLA_PAD, BF16), (MLA_PAD, F32)], [(1, MLA_PAD)], tb // 2)
    tq = min(512, t)
    dq = _attn_bwd_dq(sv["q"], sv["k"], sv["v"], do, sv["lse"], delta, tq)
    dk, dv, *carried_out = _attn_bwd_dkv(sv["q"], sv["k"], sv["v"], do, sv["lse"], delta, tq, carried)

    def rope_b(dqb, dkb, dvb, cc, a1, a2):
        dqr = _heads_apply(dqb * _attn_scale(), lambda xh: _rope(xh, cc, -a1, -a2))
        ksum = dkb[:, :HEAD_LANES]
        for h in range(1, N_HEADS):
            ksum = ksum + dkb[:, h * HEAD_LANES:(h + 1) * HEAD_LANES]
        lane = lax.broadcasted_iota(jnp.int32, ksum.shape, 1)
        keep = (lane >= MLA_NOPE) & (lane < MLA_NOPE + MLA_ROPE)
        dkpe = jnp.where(keep, _rope(ksum, cc, -a1, -a2), 0.0)
        return dqr, jnp.concatenate([dkb, dvb], axis=1), dkpe

    dqraw, dkv, dkpe = _rowwise("mla_rope_b", rope_b, [dq, dk, dv, c, s1, s2], [],
                                [(MLA_PAD, BF16), (2 * MLA_PAD, BF16), (HEAD_LANES, F32)], [], tb // 2)
    dnq = _mm("mla_uq_bx", dqraw, w["w_uq_pad"], "nt", F32)
    grads["w_uq_pad"] = _mm("mla_uq_bw", sv["nq"], dqraw, "tn", F32)
    dnkv = _mm("mla_ukv_bx", dkv, w["w_kv_pad"], "nt", F32)
    grads["w_kv_pad"] = _mm("mla_ukv_bw", sv["nkv"], dkv, "tn", F32)

    def mla_pre_b(p0, dnqb, dnkvb, dkpeb, gq, gkv):
        cq, ckv = p0[:, :MLA_Q_RANK], p0[:, MLA_Q_RANK:MLA_Q_RANK + MLA_KV_RANK]
        dcq, dgq = _rms_bwd(cq, _rms_stats(cq), gq, dnqb)
        dckv, dgkv = _rms_bwd(ckv, _rms_stats(ckv), gkv, dnkvb)
        return jnp.concatenate([dcq, dckv, dkpeb], axis=1), dgq, dgkv

    dproj, grads["mla_q_norm_g"], grads["mla_kv_norm_g"] = _rowwise(
        "mla_pre_b", mla_pre_b, [(proj, 512, PIN_MLA // 512, 0), dnq, dnkv, dkpe],
        [w["mla_q_norm_g"], w["mla_kv_norm_g"]], [(512, BF16)], [(1, MLA_Q_RANK), (1, MLA_KV_RANK)], tb,
        wide=(PIN_W, PIN_MLA // 512), carry=dproj)

    dqh, dkh, dvh, dgh, dbh = _gdn_bwd(sv["qh"], sv["kh"], sv["vh"], sv["gbh"], sv["bbh"], sv["sall"], sv["tall"], doh)
    gones = _group_ones()

    def gdn_pre_b(xq, xk, xv, pq, pk, pv, dq_, dk_, dv_, cw, go, has_prev):
        outs = []
        for j, (xc, xp, dd) in enumerate(((xq, pq, dq_), (xk, pk, dk_), (xv, pv, dv_))):
            cc = _conv_taps(xc, xp, cw[:, j * GDN_W:(j + 1) * GDN_W], has_prev)
            sg = _sigmoid(cc)
            a = cc * sg
            if j < 2:
                rn = lax.rsqrt(_dotf(a * a, go) + EPS)
                if j == 0:
                    dd = dd * (GDN_DH ** -0.5)
                da = rn * dd - a * (rn * rn * rn) * _dotf(dd * a, go)
            else:
                da = dd
            outs.append(da * (sg * (1.0 + cc * (1.0 - sg))))
        return tuple(outs)

    dcq, dck, dcv = _gdn_pre_b_call("gdn_pre_b", gdn_pre_b, proj, (dqh, dkh, dvh), w["conv_w"], gones, tb, steps)
    dproj, grads["conv_w"] = _conv_bwd_call("gdn_conv_b", proj, (dcq, dck, dcv), w["conv_w"], dproj, tb, steps)

    def gate_b(ab_ref, g_ref, b_ref, dgh_ref, dbh_ref, al_ref, dt_ref, carry_ref, dab_ref, dal_ref, ddt_ref):
        @pl.when(pl.program_id(0) == 0)
        def _():
            dal_ref[...] = jnp.zeros(dal_ref.shape, F32)
            ddt_ref[...] = jnp.zeros(ddt_ref.shape, F32)

        ab, g128, b128 = ab_ref[...], g_ref[...], b_ref[...]
        lane = lax.broadcasted_iota(jnp.int32, ab.shape, 1)
        dg_ = jnp.zeros(ab.shape, F32)
        db_ = jnp.zeros(ab.shape, F32)
        for h in range(N_HEADS):
            dg_ = dg_ + jnp.where(lane == h, jnp.broadcast_to(dgh_ref[h][:, 0:1], ab.shape), 0.0)
            db_ = db_ + jnp.where(lane == N_HEADS + h, jnp.broadcast_to(dbh_ref[h][:, 0:1], ab.shape), 0.0)
        dg_ = _dotf(_chunk_sum_matrix(tb, True), dg_)
        slope = -jnp.exp(al_ref[...]) * _sigmoid(ab + dt_ref[...])
        dab_ref[...] = (dg_ * slope + db_ * b128 * (1.0 - b128)).astype(dab_ref.dtype)
        dal_ref[...] += jnp.sum(dg_ * g128, axis=0, keepdims=True)
        ddt_ref[...] += jnp.sum(dg_ * slope, axis=0, keepdims=True)

    lanes_spec = pl.BlockSpec((tb, HEAD_LANES), lambda i: (i, 0))
    ab_spec = pl.BlockSpec((tb, HEAD_LANES), lambda i: (i, PIN_AB // HEAD_LANES))
    dproj, grads["a_log_pad"], grads["dt_bias_pad"] = pl.pallas_call(
        gate_b, name="gdn_gate_b", grid=(steps,),
        in_specs=[ab_spec, lanes_spec, lanes_spec, _heads_spec(tb), _heads_spec(tb), vec_spec(HEAD_LANES),
                  vec_spec(HEAD_LANES), ANY_SPEC],
        out_specs=[ab_spec, vec_spec(HEAD_LANES), vec_spec(HEAD_LANES)],
        out_shape=[jax.ShapeDtypeStruct((t, PIN_W), BF16), jax.ShapeDtypeStruct((1, HEAD_LANES), F32),
                   jax.ShapeDtypeStruct((1, HEAD_LANES), F32)],
        input_output_aliases={7: 0},
        compiler_params=_params(("arbitrary",)))(proj, sv["g128"], sv["b128"], dgh, dbh, w["a_log_pad"],
                                                 w["dt_bias_pad"], dproj)
    dhn = _mm("mix_in_bx", dproj, w["w_in_pad"], "nt", F32)
    grads["w_in_pad"] = _mm("mix_in_bw", sv["hn"], dproj, "tn", F32)

    def pre_b(xb, dnb, dyb, g):
        dx, dg = _rms_bwd(xb, _rms_stats(xb), g, dnb)
        return dyb + dx, dg

    dx, grads["mix_pre_g"] = _rowwise("mix_pre_b", pre_b, [x, dhn, dy], [w["mix_pre_g"]], [(d, F32)], [(1, d)], tb)
    return dx, grads, carried_out


def _gdn_pre_b_call(name, fn, proj, dd, conv_w, gones, tb, steps):
    t = proj.shape[0]

    def body(xq, xk, xv, pq, pk, pv, d0, d1, d2, cw, go, oq, ok, ov):
        has_prev = jnp.where(pl.program_id(0) == 0, 0.0, 1.0)
        dd_rows = [jnp.concatenate([dr[h] for h in range(N_HEADS)], axis=1) for dr in (d0, d1, d2)]
        outs = fn(xq[...], xk[...], xv[...], pq[...], pk[...], pv[...], *dd_rows, cw[...], go[...], has_prev)
        for r, val in zip((oq, ok, ov), outs):
            r[...] = val

    return pl.pallas_call(
        body, name=name, grid=(steps,),
        in_specs=_qkv_specs(tb) + [_heads_spec(tb)] * 3 + [pl.BlockSpec(conv_w.shape, lambda i: (0, 0)),
                                                          pl.BlockSpec(gones.shape, lambda i: (0, 0))],
        out_specs=[pl.BlockSpec((tb, GDN_W), lambda i: (i, 0))] * 3,
        out_shape=[jax.ShapeDtypeStruct((t, GDN_W), F32)] * 3,
        compiler_params=_params(("arbitrary",)))(proj, proj, proj, proj, proj, proj, *dd, conv_w, gones)


def _conv_bwd_call(name, proj, dc, conv_w, dproj, tb, steps):
    t = proj.shape[0]
    dcur = [pl.BlockSpec((tb, GDN_W), lambda i: (i, 0))] * 3
    dnext = [pl.BlockSpec((tb, GDN_W), lambda i: (jnp.minimum(i + 1, steps - 1), 0))] * 3

    def body(xq, xk, xv, pq, pk, pv, d0, d1, d2, n0, n1, n2, cw, carry_ref, dx_ref, dw_ref):
        i = pl.program_id(0)
        has_prev = jnp.where(i == 0, 0.0, 1.0)
        has_next = jnp.where(i == steps - 1, 0.0, 1.0)

        @pl.when(i == 0)
        def _():
            dw_ref[...] = jnp.zeros(dw_ref.shape, F32)

        wv = cw[...]
        dws, dxs = [], []
        for j, (xr, pr, dr, nr) in enumerate(((xq, pq, d0, n0), (xk, pk, d1, n1), (xv, pv, d2, n2))):
            x, xp, dcv, dnx = xr[...], pr[...], dr[...], nr[...]
            wj = wv[:, j * GDN_W:(j + 1) * GDN_W]
            row = lax.broadcasted_iota(jnp.int32, x.shape, 0)
            dx = dcv * wj[GDN_CONV - 1:GDN_CONV, :]
            rows_w = [jnp.sum(dcv * x, axis=0, keepdims=True)]
            for s in range(1, GDN_CONV):
                up = jnp.where(row < tb - s, _roll(dcv, tb - s, 0), _roll(dnx, tb - s, 0) * has_next)
                dx = dx + up * wj[GDN_CONV - 1 - s:GDN_CONV - s, :]
                sh = jnp.where(row >= s, _roll(x, s, 0), _roll(xp, s, 0) * has_prev)
                rows_w.append(jnp.sum(dcv * sh, axis=0, keepdims=True))
            dxs.append(dx)
            dws.append(jnp.concatenate(rows_w[::-1], axis=0))
        dx_ref[...] = jnp.concatenate(dxs, axis=1).astype(dx_ref.dtype)
        dw_ref[...] += jnp.concatenate(dws, axis=1)

    return pl.pallas_call(
        body, name=name, grid=(steps,),
        in_specs=_qkv_specs(tb) + dcur + dnext + [pl.BlockSpec(conv_w.shape, lambda i: (0, 0)), ANY_SPEC],
        out_specs=[pl.BlockSpec((tb, 3 * GDN_W), lambda i: (i, PIN_QKV // (3 * GDN_W))),
                   pl.BlockSpec(conv_w.shape, lambda i: (0, 0))],
        out_shape=[jax.ShapeDtypeStruct((t, PIN_W), BF16), jax.ShapeDtypeStruct(conv_w.shape, F32)],
        input_output_aliases={13: 0},
        compiler_params=_params(("arbitrary",)))(proj, proj, proj, proj, proj, proj, *dc, *dc, conv_w, dproj)


def _pad_heads_cols(wm, per_head):
    r = wm.shape[0]
    return jnp.pad(wm.reshape(r, N_HEADS, per_head), ((0, 0), (0, 0), (0, HEAD_LANES - per_head))).reshape(r, MLA_PAD)


def _unpad_heads_cols(wm, per_head):
    r = wm.shape[0]
    return wm.reshape(r, N_HEADS, HEAD_LANES)[:, :, :per_head].reshape(r, N_HEADS * per_head)


def _win_to_pad(wi):
    r = wi.shape[0]
    z = lambda n: jnp.zeros((r, n), wi.dtype)
    o = MLA_Q_RANK + MLA_KV_RANK
    kpe = wi[:, o:o + MLA_ROPE]
    o2 = o + MLA_ROPE
    qkv = wi[:, o2:o2 + 3 * GDN_W]
    o3 = o2 + 3 * GDN_W
    ab = wi[:, o3:o3 + 2 * N_HEADS]
    gate = wi[:, o3 + 2 * N_HEADS:]
    return jnp.concatenate([qkv, wi[:, :o], z(MLA_NOPE), kpe, z(HEAD_LANES - MLA_NOPE - MLA_ROPE), gate, ab,
                            z(HEAD_LANES - 2 * N_HEADS)], axis=1)


def _win_from_pad(wp):
    return jnp.concatenate([wp[:, PIN_MLA:PIN_KPE], wp[:, PIN_KPE + MLA_NOPE:PIN_KPE + MLA_NOPE + MLA_ROPE],
                            wp[:, PIN_QKV:PIN_QKV + 3 * GDN_W], wp[:, PIN_AB:PIN_AB + 2 * N_HEADS],
                            wp[:, PIN_GATE:PIN_GATE + GDN_W]], axis=1)


def _wkv_to_pad(wkv):
    r = wkv.shape[0]
    w3 = wkv.reshape(r, N_HEADS, MLA_NOPE + MLA_V)
    kpart = jnp.pad(w3[:, :, :MLA_NOPE], ((0, 0), (0, 0), (0, HEAD_LANES - MLA_NOPE))).reshape(r, MLA_PAD)
    vpart = jnp.pad(w3[:, :, MLA_NOPE:], ((0, 0), (0, 0), (0, HEAD_LANES - MLA_V))).reshape(r, MLA_PAD)
    return jnp.concatenate([kpart, vpart], axis=1)


def _wkv_from_pad(wp):
    r = wp.shape[0]
    kpart = wp[:, :MLA_PAD].reshape(r, N_HEADS, HEAD_LANES)[:, :, :MLA_NOPE]
    vpart = wp[:, MLA_PAD:].reshape(r, N_HEADS, HEAD_LANES)[:, :, :MLA_V]
    return jnp.concatenate([kpart, vpart], axis=2).reshape(r, N_HEADS * (MLA_NOPE + MLA_V))


def _wout_to_pad(wo):
    n = wo.shape[1]
    mla = jnp.pad(wo[:N_HEADS * MLA_V].reshape(N_HEADS, MLA_V, n), ((0, 0), (0, HEAD_LANES - MLA_V), (0, 0)))
    return jnp.concatenate([mla.reshape(MLA_PAD, n), wo[N_HEADS * MLA_V:]], axis=0)


def _wout_from_pad(wp):
    n = wp.shape[1]
    mla = wp[:MLA_PAD].reshape(N_HEADS, HEAD_LANES, n)[:, :MLA_V].reshape(N_HEADS * MLA_V, n)
    return jnp.concatenate([mla, wp[MLA_PAD:]], axis=0)


def _pad_lanes(v, n):
    return jnp.pad(v, ((0, 0), (0, n - v.shape[1])))


def _compute_weights(full):
    w = {}
    for n in FFN_BIG:
        if n in full:
            w[n] = full[n].astype(MM_DTYPE)
    w["w_in_pad"] = _win_to_pad(full["w_in"]).astype(MM_DTYPE)
    w["w_uq_pad"] = _pad_heads_cols(full["mla_w_uq"], MLA_NOPE + MLA_ROPE).astype(MM_DTYPE)
    w["w_kv_pad"] = _wkv_to_pad(full["mla_w_ukv"]).astype(MM_DTYPE)
    w["w_out_pad"] = _wout_to_pad(full["w_out"]).astype(MM_DTYPE)
    w["conv_w"] = full["gdn_conv_w"].astype(F32)
    for n in ("ffn1_pre_g", "ffn1_post_g", "mix_pre_g", "mla_q_norm_g", "mla_kv_norm_g", "gdn_norm_g", "mix_post_g",
              "ffn2_pre_g", "ffn2_post_g"):
        w[n] = full[n]
    w["mla_out_g_pad"] = _pad_heads_cols(full["mla_out_g"], MLA_V)
    w["a_log_pad"] = _pad_lanes(full["gdn_a_log"], HEAD_LANES)
    w["dt_bias_pad"] = _pad_lanes(full["gdn_dt_bias"], HEAD_LANES)
    return w


FFN2_BIG = FFN_BIG[3:]


def _local_step(x, positions, loss_target, full, late=None):
    t, d = x.shape
    tb = min(512, t)
    tm = min(1024, t)
    w = _compute_weights(full)
    ffn = lambda tag: (w[tag + "_pre_g"], w[tag + "_w_gate"], w[tag + "_w_up"], w[tag + "_w_down"], w[tag + "_post_g"])
    x1, sv1 = _ffn_fwd("ffn1", x, *ffn("ffn1"), tm)
    x2, svm, gathered = _mixer_fwd(x1, positions, w, tb, _carried_gather(late[0]) if late else None)
    for n, gw in zip(FFN2_BIG, gathered):
        w[n] = gw
    x3, sv2 = _ffn_fwd("ffn2", x2, *ffn("ffn2"), tm)

    def loss_f(yb, tg):
        e = yb - tg
        return e * (1.0 / d), jnp.sum(e * e, axis=0, keepdims=True)

    dy, lsum = _rowwise("loss", loss_f, [x3, loss_target], [], [(d, F32)], [(1, d)], tb)
    g = {}
    dx2, g["ffn2_pre_g"], g["ffn2_w_gate"], g["ffn2_w_up"], g["ffn2_w_down"], g["ffn2_post_g"], _, _ = _ffn_bwd(
        "ffn2", dy, sv2, *ffn("ffn2"), tm, tm)

    def pair_sums(arrs, tag):
        got = _swap_halves(arrs, tag)
        return [_add_pair("add_pair%s_%d" % (tag, i), gi, gt, late[1]) for i, (gi, gt) in enumerate(zip(arrs, got))]

    def chip_sums(pairs, slabs, tag):
        return [_add_chips("add_chips%s_%d" % (tag, i), pr, sl, late[2]) for i, (pr, sl) in enumerate(zip(pairs, slabs))]

    if late:
        pairs2 = pair_sums([g[n] for n in FFN2_BIG], "_ffn2")
        dx1, gm, slabs2 = _mixer_bwd(dx2, svm, w, tb, _carried_scatter(pairs2))
        for n, hs in zip(FFN2_BIG, chip_sums(pairs2, slabs2, "_ffn2")):
            g[n] = hs
    else:
        dx1, gm, _ = _mixer_bwd(dx2, svm, w, tb)
    g["w_in"] = _win_from_pad(gm["w_in_pad"])
    g["mla_w_uq"] = _unpad_heads_cols(gm["w_uq_pad"], MLA_NOPE + MLA_ROPE)
    g["mla_w_ukv"] = _wkv_from_pad(gm["w_kv_pad"])
    g["gdn_conv_w"] = gm["conv_w"]
    g["w_out"] = _wout_from_pad(gm["w_out_pad"])
    ffn1_names = FFN_BIG[:3]
    if late:
        quarters = [_pack([jnp.split(g[n], N_SHARD, axis=SHARD_AXIS[n])[q] for n in MIX_BIG], MM_DTYPE)
                    for q in range(N_SHARD)]
        pairs_m = pair_sums([jnp.stack(quarters)], "_mix")
        pairs1 = []

        def make_up(*dws):
            pairs1.extend(pair_sums(list(dws), "_ffn1"))
            return _carried_scatter(pairs1)

        dx0, g["ffn1_pre_g"], _, _, _, g["ffn1_post_g"], slabs_m, slabs1 = _ffn_bwd(
            "ffn1", dx1, sv1, *ffn("ffn1"), tm, tm, _carried_scatter(pairs_m), make_up)
        for n, hs in zip(ffn1_names, chip_sums(pairs1, slabs1, "_ffn1")):
            g[n] = hs
        g["mix_pack"] = chip_sums(pairs_m, slabs_m, "_mix")[0]
    else:
        dx0, g["ffn1_pre_g"], g["ffn1_w_gate"], g["ffn1_w_up"], g["ffn1_w_down"], g["ffn1_post_g"], _, _ = _ffn_bwd(
            "ffn1", dx1, sv1, *ffn("ffn1"), tm, tm)
    g["mix_pre_g"], g["mix_post_g"] = gm["mix_pre_g"], gm["mix_post_g"]
    g["mla_q_norm_g"], g["mla_kv_norm_g"] = gm["mla_q_norm_g"], gm["mla_kv_norm_g"]
    g["gdn_norm_g"] = gm["gdn_norm_g"]
    g["mla_out_g"] = _unpad_heads_cols(gm["mla_out_g_pad"], MLA_V)
    g["gdn_a_log"] = gm["a_log_pad"][:, :N_HEADS]
    g["gdn_dt_bias"] = gm["dt_bias_pad"][:, :N_HEADS]
    return lsum, dx0, g


HBM_SPEC = pl.BlockSpec(memory_space=pltpu.HBM)


def _place():
    return lax.axis_index("x"), lax.axis_index("y"), lax.axis_index("c")


def _exchange_call(name, body, ins, out_shapes, n_remote, n_local):
    return pl.pallas_call(
        body, name=name, in_specs=[HBM_SPEC] * len(ins), out_specs=[HBM_SPEC] * len(out_shapes), out_shape=out_shapes,
        scratch_shapes=[pltpu.SemaphoreType.DMA((n_remote,)), pltpu.SemaphoreType.DMA((n_remote,)),
                        pltpu.SemaphoreType.DMA((n_local,))])(*ins)


def _other_chips(x, y):
    return [(1 - x, y), (x, 1 - y), (1 - x, 1 - y)]


def _at_each_chip(fn):
    x, y, _ = _place()
    for cx in range(2):
        for cy in range(2):
            pl.when((x == cx) & (y == cy))(functools.partial(fn, cx, cy))


def _at_each_device(fn):
    x, y, c = _place()
    for cx in range(2):
        for cy in range(2):
            for cc in range(2):
                pl.when((x == cx) & (y == cy) & (c == cc))(functools.partial(fn, cx, cy, cc))


def _at_each_core(fn):
    c = lax.axis_index("c")
    for cc in range(2):
        pl.when(c == cc)(functools.partial(fn, cc))


def _gather_shards(ws):
    nw = len(ws)

    def body(*refs):
        w_refs, out_refs = refs[:nw], refs[nw:2 * nw]
        send_sems, recv_sems, local_sems = refs[2 * nw:]

        def run(x, y, c):
            chips = _other_chips(x, y)
            me, sibling = 2 * x + y, (x, y, 1 - c)

            def half(ref, which):
                hr = ref.shape[0] // 2
                return ref.at[pl.ds(which * hr, hr)]

            def over_ici(i, j, src, slab, to):
                return pltpu.make_async_remote_copy(
                    src_ref=half(src, c), dst_ref=half(out_refs[i].at[slab], c), send_sem=send_sems.at[7 * i + j],
                    recv_sem=recv_sems.at[7 * i + j], device_id=to, device_id_type=MESH)

            def over_d2d(i, j, slab, which):
                return pltpu.make_async_remote_copy(
                    src_ref=half(out_refs[i].at[slab], which), dst_ref=half(out_refs[i].at[slab], which),
                    send_sem=send_sems.at[7 * i + 3 + j], recv_sem=recv_sems.at[7 * i + 3 + j], device_id=sibling,
                    device_id_type=MESH)

            def own(i, w_ref):
                return pltpu.make_async_remote_copy(
                    src_ref=w_ref, dst_ref=out_refs[i].at[me], send_sem=send_sems.at[7 * i + 6],
                    recv_sem=recv_sems.at[7 * i + 6], device_id=sibling, device_id_type=MESH)

            sends, passed = [], []
            for i, w_ref in enumerate(w_refs):
                for j, (px, py) in enumerate(chips):
                    sends.append(over_ici(i, j, w_ref, me, (px, py, c)))
                    sends[-1].start()
            for i, w_ref in enumerate(w_refs):
                sends.append(own(i, w_ref))
                sends[-1].start()
            for i, w_ref in enumerate(w_refs):
                for j, (px, py) in enumerate(chips):
                    over_ici(i, j, w_ref, 2 * px + py, (px, py, c)).wait_recv()
                    passed.append(over_d2d(i, j, 2 * px + py, c))
                    passed[-1].start()
            for i, w_ref in enumerate(w_refs):
                own(i, w_ref).wait_recv()
                for j, (px, py) in enumerate(chips):
                    over_d2d(i, j, 2 * px + py, 1 - c).wait_recv()
            for cp in sends + passed:
                cp.wait_send()

        _at_each_device(run)

    outs = [jax.ShapeDtypeStruct((N_SHARD,) + w.shape, w.dtype) for w in ws]
    return _exchange_call("gather_weight_shards", body, ws, outs, 7 * nw, 1)


def _swap_halves(gs, tag=""):
    ng = len(gs)

    def body(*refs):
        g_refs, got_refs = refs[:ng], refs[ng:2 * ng]
        send_sems, recv_sems, _ = refs[2 * ng:]
        x, y, _ = _place()

        def run(c):
            sends = []
            for i, (g_ref, got_ref) in enumerate(zip(g_refs, got_refs)):
                hr = got_ref.shape[1]
                sends.append(pltpu.make_async_remote_copy(
                    src_ref=g_ref.at[:, pl.ds((1 - c) * hr, hr)], dst_ref=got_ref, send_sem=send_sems.at[i],
                    recv_sem=recv_sems.at[i], device_id=(x, y, 1 - c), device_id_type=MESH))
                sends[-1].start()
            for cp in sends:
                cp.wait()

        _at_each_core(run)

    halves = [jax.ShapeDtypeStruct((g.shape[0], g.shape[1] // 2, g.shape[2]), g.dtype) for g in gs]
    return _exchange_call("swap_grad_halves" + tag, body, gs, halves, ng, 1)


def _scatter_copies(p_refs, out_refs, send_sems, recv_sems, x, y):
    c = lax.axis_index("c")
    copies = []
    for i, (p_ref, out_ref) in enumerate(zip(p_refs, out_refs)):
        for j, (px, py) in enumerate(_other_chips(x, y)):
            copies.append(pltpu.make_async_remote_copy(
                src_ref=p_ref.at[2 * px + py], dst_ref=out_ref.at[j], send_sem=send_sems.at[3 * i + j],
                recv_sem=recv_sems.at[3 * i + j], device_id=(px, py, c), device_id_type=MESH))
    return copies


def _start_all(make, *refs):
    def run(x, y):
        for cp in make(*refs, x, y):
            cp.start()

    _at_each_chip(run)


def _wait_all(make, *refs):
    def run(x, y):
        copies = make(*refs, x, y)
        for cp in copies:
            cp.wait_recv()
        for cp in copies:
            cp.wait_send()

    _at_each_chip(run)


def _scatter_shapes(ps):
    return [jax.ShapeDtypeStruct((3,) + p.shape[1:], p.dtype) for p in ps]


def _carried_scatter(ps):
    return _Carried(ps, _scatter_shapes(ps), 3 * len(ps), functools.partial(_start_all, _scatter_copies),
                    functools.partial(_wait_all, _scatter_copies))


def _direct_gather_copies(w_refs, out_refs, send_sems, recv_sems, x, y, arriving):
    c = lax.axis_index("c")
    me = 2 * x + y
    peers = [((px, py, c), 2 * px + py) for px, py in _other_chips(x, y)] + [((x, y, 1 - c), me)]
    copies = []
    for i, (w_ref, out_ref) in enumerate(zip(w_refs, out_refs)):
        for j, (peer, slab) in enumerate(peers):
            copies.append(pltpu.make_async_remote_copy(
                src_ref=w_ref, dst_ref=out_ref.at[slab if arriving else me], send_sem=send_sems.at[4 * i + j],
                recv_sem=recv_sems.at[4 * i + j], device_id=peer, device_id_type=MESH))
    return copies


def _carried_gather(ws):
    def start(w_refs, out_refs, send_sems, recv_sems):
        def run(x, y):
            for cp in _direct_gather_copies(w_refs, out_refs, send_sems, recv_sems, x, y, False):
                cp.start()

        _at_each_chip(run)

    def finish(w_refs, out_refs, send_sems, recv_sems):
        def run(x, y):
            for cp in _direct_gather_copies(w_refs, out_refs, send_sems, recv_sems, x, y, True):
                cp.wait_recv()
            for cp in _direct_gather_copies(w_refs, out_refs, send_sems, recv_sems, x, y, False):
                cp.wait_send()

        _at_each_chip(run)

    outs = [jax.ShapeDtypeStruct((N_SHARD,) + w.shape, w.dtype) for w in ws]
    return _Carried(ws, outs, 4 * len(ws), start, finish)


def _share_halves(hs):
    n = len(hs)

    def body(*refs):
        h_refs, out_refs = refs[:n], refs[n:2 * n]
        send_sems, recv_sems, _ = refs[2 * n:]
        x, y, c = _place()
        sends = []
        for i, (h_ref, out_ref) in enumerate(zip(h_refs, out_refs)):
            sends.append(pltpu.make_async_remote_copy(
                src_ref=h_ref, dst_ref=out_ref, send_sem=send_sems.at[i], recv_sem=recv_sems.at[i],
                device_id=(x, y, 1 - c), device_id_type=MESH))
            sends[-1].start()
        for cp in sends:
            cp.wait()

    outs = [jax.ShapeDtypeStruct(h.shape, h.dtype) for h in hs]
    return _exchange_call("share_grad_halves", body, hs, outs, n, 1)


def _scalar_grid_call(name, body, scalars, grid, in_specs, out_specs, out_shape, args):
    grid_spec = pltpu.PrefetchScalarGridSpec(num_scalar_prefetch=len(scalars), grid=grid, in_specs=in_specs,
                                             out_specs=out_specs)
    return pl.pallas_call(body, name=name, grid_spec=grid_spec, out_shape=out_shape,
                          compiler_params=_params(("arbitrary",) * len(grid)))(*scalars, *args)


def _add_pair(name, g, got, core):
    ns_, hr, cols = got.shape
    th = _row_tile(hr, 512)
    nb = hr // th

    def body(core_ref, g_ref, got_ref, out_ref):
        out_ref[...] = (g_ref[...].astype(F32) + got_ref[...].astype(F32)).astype(out_ref.dtype)

    blk = pl.BlockSpec((1, th, cols), lambda q, j, core_ref: (q, j, 0))
    own = pl.BlockSpec((1, th, cols), lambda q, j, core_ref: (q, core_ref[0] * nb + j, 0))
    return _scalar_grid_call(name, body, [core], (ns_, nb), [own, blk], blk,
                             jax.ShapeDtypeStruct(got.shape, got.dtype), [g, got])


def _add_chips(name, pairs, slabs, chip):
    _, hr, cols = slabs.shape
    th = _row_tile(hr, 512)

    def body(chip_ref, own_ref, s0_ref, s1_ref, s2_ref, out_ref):
        total = own_ref[0].astype(F32) + s0_ref[0].astype(F32)
        out_ref[...] = (total + s1_ref[0].astype(F32)) + s2_ref[0].astype(F32)

    own = pl.BlockSpec((1, th, cols), lambda j, chip_ref: (chip_ref[0], j, 0))
    others = [pl.BlockSpec((1, th, cols), lambda j, chip_ref, k=k: (k, j, 0)) for k in range(3)]
    return _scalar_grid_call(name, body, [chip], (hr // th,), [own] + others,
                             pl.BlockSpec((th, cols), lambda j, chip_ref: (j, 0)),
                             jax.ShapeDtypeStruct((hr, cols), F32), [pairs, slabs, slabs, slabs])


def _join_halves(name, mine, other, core):
    hr, cols = mine.shape
    th = _row_tile(hr, 512)
    nb = hr // th

    def body(core_ref, mine_ref, other_ref, out_ref):
        is_mine = pl.program_id(0) == core_ref[0]

        @pl.when(is_mine)
        def _():
            out_ref[0] = mine_ref[...]

        @pl.when(jnp.logical_not(is_mine))
        def _():
            out_ref[0] = other_ref[...]

    blk = pl.BlockSpec((th, cols), lambda h, j, core_ref: (j, 0))
    return _scalar_grid_call(name, body, [core], (2, nb), [blk, blk],
                             pl.BlockSpec((1, th, cols), lambda h, j, core_ref: (0, h * nb + j, 0)),
                             jax.ShapeDtypeStruct((1, 2 * hr, cols), mine.dtype), [mine, other])


def _gather_small(sp):
    def body(s_ref, out_ref, send_sems, recv_sems, local_sem):
        x, y, c = _place()
        me = 4 * x + 2 * y + c
        peers = [(x ^ (m >> 2), y ^ ((m >> 1) & 1), c ^ (m & 1)) for m in range(1, 8)]
        mine = pltpu.make_async_copy(s_ref, out_ref.at[me], local_sem)
        mine.start()
        sends = [pltpu.make_async_remote_copy(src_ref=s_ref, dst_ref=out_ref.at[me], send_sem=send_sems.at[j],
                                              recv_sem=recv_sems.at[j], device_id=p, device_id_type=MESH)
                 for j, p in enumerate(peers)]
        for cp in sends:
            cp.start()
        for j, (px, py, pc) in enumerate(peers):
            pltpu.make_async_remote_copy(src_ref=s_ref, dst_ref=out_ref.at[4 * px + 2 * py + pc],
                                         send_sem=send_sems.at[j], recv_sem=recv_sems.at[j], device_id=(px, py, pc),
                                         device_id_type=MESH).wait_recv()
        for cp in sends:
            cp.wait_send()
        mine.wait()

    return pl.pallas_call(
        body, name="gather_small_grads", in_specs=[HBM_SPEC], out_specs=HBM_SPEC,
        out_shape=jax.ShapeDtypeStruct((8,) + sp.shape, sp.dtype),
        scratch_shapes=[pltpu.SemaphoreType.DMA((7,)), pltpu.SemaphoreType.DMA((7,)), pltpu.SemaphoreType.DMA])(sp)


def _pack_rows(total):
    rows = -(-total // LANES)
    return -(-rows // 32) * 32


def _pack(arrs, dtype):
    flat = jnp.concatenate([a.reshape(-1).astype(dtype) for a in arrs])
    rows = _pack_rows(flat.shape[0])
    return jnp.pad(flat, (0, rows * LANES - flat.shape[0])).reshape(rows, LANES)


def _unpack(buf, shapes):
    flat = buf.reshape(-1)
    out, off = {}, 0
    for n, shp in shapes:
        size = shp[0] * shp[1]
        out[n] = flat[off:off + size].reshape(shp)
        off += size
    return out


def _to_wire(name, w3):
    _, r, cols = w3.shape
    tb = _row_tile(r, 512)

    def body(w_ref, o_ref):
        o_ref[...] = w_ref[0].astype(o_ref.dtype)

    return pl.pallas_call(
        body, name=name, grid=(r // tb,), in_specs=[pl.BlockSpec((1, tb, cols), lambda i: (0, i, 0))],
        out_specs=pl.BlockSpec((tb, cols), lambda i: (i, 0)), out_shape=jax.ShapeDtypeStruct((r, cols), MM_DTYPE),
        compiler_params=_params(("arbitrary",)))(w3)


def _adamw(name, w3, g, m3, v3, tb):
    c1 = 1.0 - ADAM_B1 ** ADAM_STEP
    c2 = 1.0 - ADAM_B2 ** ADAM_STEP
    _, r, cols = w3.shape
    emit = g.ndim == 2
    blk3 = pl.BlockSpec((1, tb, cols), lambda i: (0, i, 0))
    g_spec = pl.BlockSpec((tb, cols), lambda i: (i, 0)) if emit else blk3

    def body(w_ref, g_ref, m_ref, v_ref, *out_refs):
        gb = g_ref[...] if emit else g_ref[0]
        m2 = ADAM_B1 * m_ref[0] + (1.0 - ADAM_B1) * gb
        v2 = ADAM_B2 * v_ref[0] + (1.0 - ADAM_B2) * (gb * gb)
        out_refs[-3][0] = -ADAM_LR * ((m2 / c1) / (jnp.sqrt(v2 / c2) + ADAM_EPS) + ADAM_WD * w_ref[0])
        out_refs[-2][0] = m2
        out_refs[-1][0] = v2
        if emit:
            out_refs[0][0] = gb

    n_out = 4 if emit else 3
    outs = pl.pallas_call(
        body, name=name, grid=(r // tb,), in_specs=[blk3, g_spec, blk3, blk3], out_specs=[blk3] * n_out,
        out_shape=[jax.ShapeDtypeStruct((1, r, cols), F32)] * n_out,
        compiler_params=_params(("arbitrary",)))(w3, g, m3, v3)
    return outs if emit else [g] + list(outs)


def _row_tile(rows, pref):
    if rows <= pref:
        return rows
    t = pref
    while t >= 8:
        if rows % t == 0 and t % 8 == 0:
            return t
        t -= 8
    return rows


def kernel(x, positions, ffn1_pre_g, ffn1_w_gate, ffn1_w_up, ffn1_w_down, ffn1_post_g, mix_pre_g, w_in, mla_q_norm_g, mla_w_uq, mla_kv_norm_g, mla_w_ukv, mla_out_g, gdn_conv_w, gdn_a_log, gdn_dt_bias, gdn_norm_g, w_out, mix_post_g, ffn2_pre_g, ffn2_w_gate, ffn2_w_up, ffn2_w_down, ffn2_post_g, loss_target, m_ffn1_pre_g, m_ffn1_w_gate, m_ffn1_w_up, m_ffn1_w_down, m_ffn1_post_g, m_mix_pre_g, m_w_in, m_mla_q_norm_g, m_mla_w_uq, m_mla_kv_norm_g, m_mla_w_ukv, m_mla_out_g, m_gdn_conv_w, m_gdn_a_log, m_gdn_dt_bias, m_gdn_norm_g, m_w_out, m_mix_post_g, m_ffn2_pre_g, m_ffn2_w_gate, m_ffn2_w_up, m_ffn2_w_down, m_ffn2_post_g, v_ffn1_pre_g, v_ffn1_w_gate, v_ffn1_w_up, v_ffn1_w_down, v_ffn1_post_g, v_mix_pre_g, v_w_in, v_mla_q_norm_g, v_mla_w_uq, v_mla_kv_norm_g, v_mla_w_ukv, v_mla_out_g, v_gdn_conv_w, v_gdn_a_log, v_gdn_dt_bias, v_gdn_norm_g, v_w_out, v_mix_post_g, v_ffn2_pre_g, v_ffn2_w_gate, v_ffn2_w_up, v_ffn2_w_down, v_ffn2_post_g):
    args = dict(locals())
    wsh = {n: args[n][0] for n in WEIGHTS}
    msh = {n: args["m_" + n][0] if args["m_" + n].ndim == 3 else args["m_" + n] for n in WEIGHTS}
    vsh = {n: args["v_" + n][0] if args["v_" + n].ndim == 3 else args["v_" + n] for n in WEIGHTS}
    for n in SMALL:
        wsh[n] = args[n]
    mix_shapes = [(n, wsh[n].shape) for n in MIX_BIG]

    early = FFN_BIG[:3]
    gathered = _gather_shards([_to_wire("wire_" + n, args[n]) for n in early]
                              + [_pack([wsh[n] for n in MIX_BIG], MM_DTYPE)])
    full = {n: wsh[n] for n in SMALL}
    for n, gw in zip(early, gathered):
        full[n] = gw
    parts = [_unpack(gathered[-1][q], mix_shapes) for q in range(N_SHARD)]
    for n in MIX_BIG:
        full[n] = jnp.concatenate([parts[q][n] for q in range(N_SHARD)], axis=SHARD_AXIS[n])

    core = lax.axis_index("c").astype(jnp.int32).reshape(1)
    chip = (2 * lax.axis_index("x") + lax.axis_index("y")).astype(jnp.int32).reshape(1)
    late = ([_to_wire("wire_" + n, args[n]) for n in FFN2_BIG], core, chip)
    lsum, grad_x, g = _local_step(x[0], positions, loss_target[0], full, late)
    loss = lax.psum(0.5 * jnp.sum(lsum) / x.shape[-1], ("x", "y", "c"))

    halves = [g[n] for n in FFN_BIG] + [g["mix_pack"]]
    others = _share_halves(halves)
    shared = [_join_halves("join_halves_%d" % i, hm, ho, core) for i, (hm, ho) in enumerate(zip(halves, others))]
    gsh = _unpack(shared[-1], mix_shapes)
    for n, sg_ in zip(FFN_BIG, shared):
        gsh[n] = sg_

    small_shapes = [(n, wsh[n].shape) for n in SMALL]
    pack_small = lambda d: jnp.concatenate(
        [_pad_lanes(d[n].astype(F32), LANES) for n in SMALL] + [jnp.zeros((SMALL_ROWS - len(SMALL), LANES), F32)], axis=0)
    slots = _gather_small(pack_small(g))

    c1 = 1.0 - ADAM_B1 ** ADAM_STEP
    c2 = 1.0 - ADAM_B2 ** ADAM_STEP

    def small_update(wb, mb, vb, s8):
        gs = s8[0:SMALL_ROWS]
        for d in range(1, 8):
            gs = gs + s8[d * SMALL_ROWS:(d + 1) * SMALL_ROWS]
        m2 = ADAM_B1 * mb + (1.0 - ADAM_B1) * gs
        v2 = ADAM_B2 * vb + (1.0 - ADAM_B2) * (gs * gs)
        delta = -ADAM_LR * ((m2 / c1) / (jnp.sqrt(v2 / c2) + ADAM_EPS) + ADAM_WD * wb)
        return gs, delta, m2, v2

    sg, sd, sm, sv_ = _rowwise("adamw_small", small_update,
                               [pack_small(wsh), pack_small(msh), pack_small(vsh)],
                               [slots.reshape(8 * SMALL_ROWS, LANES)], [(LANES, F32)] * 4, [], SMALL_ROWS)
    grads, deltas, new_m, new_v = {}, {}, {}, {}
    for i, (n, shp) in enumerate(small_shapes):
        grads[n], deltas[n] = sg[i:i + 1, :shp[1]], sd[i:i + 1, :shp[1]]
        new_m[n], new_v[n] = sm[i:i + 1, :shp[1]], sv_[i:i + 1, :shp[1]]
    for n in BIG:
        grads[n], deltas[n], new_m[n], new_v[n] = _adamw("adamw_" + n, args[n], gsh[n], args["m_" + n], args["v_" + n],
                                                         _row_tile(args[n].shape[1], 256))

    return (loss, grad_x[None], *[grads[n] for n in WEIGHTS], *[deltas[n] for n in WEIGHTS],
            *[new_m[n] for n in WEIGHTS], *[new_v[n] for n in WEIGHTS])
```

```python
import functools

import jax
import jax.numpy as jnp
from jax import lax
from jax.experimental import pallas as pl
from jax.experimental.pallas import tpu as pltpu

F32 = jnp.float32
BF16 = jnp.bfloat16
MM_DTYPE = BF16
HI = lax.Precision.HIGHEST
MESH = pl.DeviceIdType.MESH

D_MODEL = 1024
D_FF = 2816
N_HEADS = 8
MLA_Q_RANK = 256
MLA_KV_RANK = 128
MLA_NOPE = 64
MLA_ROPE = 32
MLA_V = 64
ROPE_THETA = 10000.0
GDN_DH = 64
GDN_W = N_HEADS * GDN_DH
GDN_CONV = 4
CHUNK = 64
HEAD_LANES = 128
HEADS_PER_STEP = 8
MLA_PAD = N_HEADS * HEAD_LANES
EPS = 1e-6
N_SHARD = 4
LANES = 1024

PIN_QKV = 0
PIN_MLA = 1536
PIN_KPE = 1920
PIN_GATE = 2048
PIN_AB = 2560
PIN_W = 2688
CAT_W = MLA_PAD + GDN_W

ADAM_LR = 0.001
ADAM_B1 = 0.9
ADAM_B2 = 0.999
ADAM_EPS = 1e-08
ADAM_WD = 0.01
ADAM_STEP = 10

VMEM_LIMIT_V7X = 56 * 1024 * 1024

BIG = ["ffn1_w_gate", "ffn1_w_up", "ffn1_w_down", "w_in", "mla_w_uq", "mla_w_ukv", "gdn_conv_w", "w_out",
       "ffn2_w_gate", "ffn2_w_up", "ffn2_w_down"]
FFN_BIG = ["ffn1_w_gate", "ffn1_w_up", "ffn1_w_down", "ffn2_w_gate", "ffn2_w_up", "ffn2_w_down"]
TRANSPOSED = ["ffn1_w_gate", "ffn1_w_up", "ffn2_w_gate", "ffn2_w_up"]
MIX_BIG = ["w_in", "mla_w_uq", "mla_w_ukv", "gdn_conv_w", "w_out"]
SMALL = ["ffn1_pre_g", "ffn1_post_g", "mix_pre_g", "mla_q_norm_g", "mla_kv_norm_g", "mla_out_g", "gdn_a_log",
         "gdn_dt_bias", "gdn_norm_g", "mix_post_g", "ffn2_pre_g", "ffn2_post_g"]
WEIGHTS = ["ffn1_pre_g", "ffn1_w_gate", "ffn1_w_up", "ffn1_w_down", "ffn1_post_g", "mix_pre_g", "w_in",
           "mla_q_norm_g", "mla_w_uq", "mla_kv_norm_g", "mla_w_ukv", "mla_out_g", "gdn_conv_w", "gdn_a_log",
           "gdn_dt_bias", "gdn_norm_g", "w_out", "mix_post_g", "ffn2_pre_g", "ffn2_w_gate", "ffn2_w_up",
           "ffn2_w_down", "ffn2_post_g"]
SHARD_AXIS = {"ffn1_w_gate": 1, "ffn1_w_up": 1, "ffn1_w_down": 0, "w_in": 1, "mla_w_uq": 1, "mla_w_ukv": 1,
              "gdn_conv_w": 1, "w_out": 0, "ffn2_w_gate": 1, "ffn2_w_up": 1, "ffn2_w_down": 0}
SMALL_ROWS = 16


def _params(sem):
    return pltpu.CompilerParams(dimension_semantics=sem, vmem_limit_bytes=VMEM_LIMIT_V7X)


def _pick(dim, pref):
    if dim <= pref:
        return dim
    t = (pref // 128) * 128
    while t >= 128:
        if dim % t == 0:
            return t
        t -= 128
    return dim


ANY_SPEC = pl.BlockSpec(memory_space=pl.ANY)


def _rowwise(name, fn, row_ins, bc_ins, row_outs, acc_outs, tb, wide=None, carry=None):
    ents = []
    for e in row_ins:
        ents.append(e if isinstance(e, tuple) else (e, e.shape[1], 0, 0))
    over = [o[2] for o in row_outs if len(o) == 3]
    rows = over[0] if over else ents[0][0].shape[0]
    steps = rows // tb
    assert steps * tb == rows, (name, rows, tb)
    in_specs, args = [], []
    for a, w, j, r0 in ents:
        in_specs.append(pl.BlockSpec((tb, w), lambda i, j=j, r0=r0: (i + r0, j)))
        args.append(a)
    for b in bc_ins:
        in_specs.append(pl.BlockSpec(b.shape, lambda i: (0, 0)))
        args.append(b)
    n_in = len(args)
    aliases = {}
    if carry is not None:
        in_specs.append(ANY_SPEC)
        args.append(carry)
        aliases = {n_in: 0}
    out_shape = [jax.ShapeDtypeStruct((rows, o[0]), o[1]) for o in row_outs]
    out_specs = [pl.BlockSpec((tb, o[0]), lambda i: (i, 0)) for o in row_outs]
    if wide is not None:
        out_shape[0] = jax.ShapeDtypeStruct((rows, wide[0]), row_outs[0][1])
        out_specs[0] = pl.BlockSpec((tb, row_outs[0][0]), lambda i: (i, wide[1]))
    out_shape += [jax.ShapeDtypeStruct((r, c), F32) for r, c in acc_outs]
    out_specs += [pl.BlockSpec((r, c), lambda i: (0, 0)) for r, c in acc_outs]
    n_ro, n_acc, n_args = len(row_outs), len(acc_outs), len(args)

    def body(*refs):
        vals = fn(*[r[...] for r in refs[:n_in]])
        if not isinstance(vals, (tuple, list)):
            vals = (vals,)
        for r, v in zip(refs[n_args:n_args + n_ro], vals[:n_ro]):
            r[...] = v.astype(r.dtype)
        if n_acc:
            acc_refs = refs[n_args + n_ro:]

            @pl.when(pl.program_id(0) == 0)
            def _():
                for r in acc_refs:
                    r[...] = jnp.zeros(r.shape, r.dtype)

            for r, v in zip(acc_refs, vals[n_ro:]):
                r[...] += v

    outs = pl.pallas_call(body, name=name, grid=(steps,), in_specs=in_specs, out_specs=out_specs,
                          out_shape=out_shape, input_output_aliases=aliases,
                          compiler_params=_params(("arbitrary",)))(*args)
    return outs


def _mm(name, a, b, mode, out_dtype, tm=1024, tn=1024, tk=1024):
    if mode == "nn":
        (m, k), (k2, n) = a.shape, b.shape
    elif mode == "nt":
        (m, k), (n, k2) = a.shape, b.shape
    else:
        (k, m), (k2, n) = a.shape, b.shape
    assert k == k2, (name, a.shape, b.shape)
    tm, tn, tk = _pick(m, tm), _pick(n, tn), _pick(k, tk)
    nk = k // tk
    if mode == "nn":
        a_spec = pl.BlockSpec((tm, tk), lambda i, j, kk: (i, kk))
        b_spec = pl.BlockSpec((tk, tn), lambda i, j, kk: (kk, j))
        dims = (((1,), (0,)), ((), ()))
    elif mode == "nt":
        a_spec = pl.BlockSpec((tm, tk), lambda i, j, kk: (i, kk))
        b_spec = pl.BlockSpec((tn, tk), lambda i, j, kk: (j, kk))
        dims = (((1,), (1,)), ((), ()))
    else:
        a_spec = pl.BlockSpec((tk, tm), lambda i, j, kk: (kk, i))
        b_spec = pl.BlockSpec((tk, tn), lambda i, j, kk: (kk, j))
        dims = (((0,), (0,)), ((), ()))

    def body(a_ref, b_ref, o_ref, acc_ref):
        kk = pl.program_id(2)

        @pl.when(kk == 0)
        def _():
            acc_ref[...] = jnp.zeros(acc_ref.shape, F32)

        acc_ref[...] += lax.dot_general(a_ref[...].astype(MM_DTYPE), b_ref[...].astype(MM_DTYPE), dims,
                                        preferred_element_type=F32)

        @pl.when(kk == nk - 1)
        def _():
            o_ref[...] = acc_ref[...].astype(o_ref.dtype)

    return pl.pallas_call(
        body, name=name, grid=(m // tm, n // tn, nk), in_specs=[a_spec, b_spec],
        out_specs=pl.BlockSpec((tm, tn), lambda i, j, kk: (i, j)),
        out_shape=jax.ShapeDtypeStruct((m, n), out_dtype),
        scratch_shapes=[pltpu.VMEM((tm, tn), F32)],
        compiler_params=_params(("parallel", "parallel", "arbitrary")))(a, b)


def _rms_stats(x, n_real=None):
    n = x.shape[-1] if n_real is None else n_real
    return lax.rsqrt(jnp.sum(x * x, axis=-1, keepdims=True) / n + EPS)


def _rms_bwd(x, r, g, dz, n_real=None):
    n = x.shape[-1] if n_real is None else n_real
    xh = x * r
    dxh = dz * g
    dx = r * (dxh - xh * (jnp.sum(dxh * xh, axis=-1, keepdims=True) / n))
    return dx, jnp.sum(dz * xh, axis=0, keepdims=True)


def _sigmoid(x):
    return 0.5 * jnp.tanh(0.5 * x) + 0.5


def _roll(x, s, axis):
    return pltpu.roll(x, s, axis)


def _rope(x, c, s1, s2):
    return x * c + _roll(x, HEAD_LANES - MLA_ROPE // 2, 1) * s1 + _roll(x, MLA_ROPE // 2, 1) * s2


def _heads_apply(x, fn):
    return jnp.concatenate([fn(x[:, h * HEAD_LANES:(h + 1) * HEAD_LANES]) for h in range(N_HEADS)], axis=1)


ROW_CHUNK = 256


def _row_chunks(rows):
    step = min(ROW_CHUNK, rows)
    return [pl.ds(r, step) for r in range(0, rows, step)]


def _ffn_fwd(tag, x, g_pre, wg, wu, wd, g_post, tm):
    t, d = x.shape
    ns, fs, _ = wg.shape
    nt = t // tm
    row = pl.BlockSpec((tm, d), lambda i, q: (i, 0))
    vec = pl.BlockSpec((1, d), lambda i, q: (0, 0))
    act3 = pl.BlockSpec((1, tm, fs), lambda i, q: (q, i, 0))
    wrow = pl.BlockSpec((1, fs, d), lambda i, q: (q, 0, 0))
    nt_dims = (((1,), (1,)), ((), ()))

    def gate_up(x_ref, g_ref, wg_ref, wu_ref, n_ref, a_ref, u_ref, s_ref, n_s):
        @pl.when(pl.program_id(1) == 0)
        def _():
            for r in _row_chunks(tm):
                xb = x_ref[r, :]
                n_s[r, :] = (xb * _rms_stats(xb) * g_ref[...]).astype(MM_DTYPE)
            n_ref[...] = n_s[...]

        for r in _row_chunks(tm):
            n = n_s[r, :]
            a = lax.dot_general(n, wg_ref[0], nt_dims, preferred_element_type=F32)
            u = lax.dot_general(n, wu_ref[0], nt_dims, preferred_element_type=F32)
            a_ref[0, r, :] = a.astype(a_ref.dtype)
            u_ref[0, r, :] = u.astype(u_ref.dtype)
            s_ref[0, r, :] = ((a * _sigmoid(a)) * u).astype(s_ref.dtype)

    n, a, u, s = pl.pallas_call(
        gate_up, name=tag + "_gate_up", grid=(nt, ns), in_specs=[row, vec, wrow, wrow],
        out_specs=[row, act3, act3, act3],
        out_shape=[jax.ShapeDtypeStruct((t, d), MM_DTYPE)] + [jax.ShapeDtypeStruct((ns, t, fs), MM_DTYPE)] * 3,
        scratch_shapes=[pltpu.VMEM((tm, d), MM_DTYPE)],
        compiler_params=_params(("parallel", "arbitrary")))(x, g_pre, wg, wu)

    def down(s_ref, wd_ref, x_ref, g_ref, h_ref, y_ref, acc):
        q = pl.program_id(1)

        @pl.when(q == 0)
        def _():
            acc[...] = jnp.zeros(acc.shape, F32)

        for r in _row_chunks(tm):
            acc[r, :] += jnp.dot(s_ref[0, r, :], wd_ref[0], preferred_element_type=F32)

        @pl.when(q == ns - 1)
        def _():
            for r in _row_chunks(tm):
                hb = acc[r, :]
                h_ref[r, :] = hb
                y_ref[r, :] = x_ref[r, :] + 0.5 * (hb * _rms_stats(hb) * g_ref[...])

    h, y = pl.pallas_call(
        down, name=tag + "_down", grid=(nt, ns), in_specs=[act3, wrow, row, vec], out_specs=[row, row],
        out_shape=[jax.ShapeDtypeStruct((t, d), F32)] * 2, scratch_shapes=[pltpu.VMEM((tm, d), F32)],
        compiler_params=_params(("parallel", "arbitrary")))(s, wd, x, g_post)
    return y, (x, n, a, u, s, h)


def _carry(body, n_in, n_out, grid, carried):
    if carried is None:
        return body, [], [], [], [], []
    nx_in, nx_out = len(carried.ins), len(carried.outs)

    def wrapped(*refs):
        ins, rest = refs[:n_in], refs[n_in:]
        xi, rest = rest[:nx_in], rest[nx_in:]
        outs, rest = rest[:n_out], rest[n_out:]
        xo, rest = rest[:nx_out], rest[nx_out:]
        scr, sems = rest[:len(rest) - 2], rest[len(rest) - 2:]
        first, last = True, True
        for dim, size in enumerate(grid):
            first = first & (pl.program_id(dim) == 0)
            last = last & (pl.program_id(dim) == size - 1)

        @pl.when(first)
        def _():
            carried.start(xi, xo, *sems)

        body(*ins, *outs, *scr)

        @pl.when(last)
        def _():
            carried.finish(xi, xo, *sems)

    sems = [pltpu.SemaphoreType.DMA((carried.n_sem,)), pltpu.SemaphoreType.DMA((carried.n_sem,))]
    return (wrapped, [HBM_SPEC] * nx_in, [HBM_SPEC] * nx_out, list(carried.outs), sems, list(carried.ins))


def _ffn_bwd(tag, dy, saved, g_pre, wg, wu, wd, g_post, tm, tk, carried_down=None, make_carried_up=None):
    x, n, a, u, s, h = saved
    t, d = x.shape
    ns, fs, _ = wg.shape
    nt, nk = t // tm, t // tk
    row = pl.BlockSpec((tm, d), lambda i, q: (i, 0))
    vec = pl.BlockSpec((1, d), lambda i, q: (0, 0))
    act3 = pl.BlockSpec((1, tm, fs), lambda i, q: (q, i, 0))
    wrow = pl.BlockSpec((1, fs, d), lambda i, q: (q, 0, 0))
    nt_dims = (((1,), (1,)), ((), ()))
    tn_dims = (((0,), (0,)), ((), ()))

    def down_b(h_ref, dy_ref, g_ref, wd_ref, a_ref, u_ref, dh_ref, da_ref, du_ref, dg_ref, dh_s):
        i, q = pl.program_id(0), pl.program_id(1)

        @pl.when((i == 0) & (q == 0))
        def _():
            dg_ref[...] = jnp.zeros(dg_ref.shape, F32)

        @pl.when(q == 0)
        def _():
            for r in _row_chunks(tm):
                hb = h_ref[r, :]
                dh, dg = _rms_bwd(hb, _rms_stats(hb), g_ref[...], 0.5 * dy_ref[r, :])
                dh_s[r, :] = dh.astype(MM_DTYPE)
                dg_ref[...] += dg
            dh_ref[...] = dh_s[...]

        for r in _row_chunks(tm):
            ds = lax.dot_general(dh_s[r, :], wd_ref[0], nt_dims, preferred_element_type=F32)
            ab, ub = a_ref[0, r, :].astype(F32), u_ref[0, r, :].astype(F32)
            sg = _sigmoid(ab)
            da_ref[0, r, :] = (ds * ub * (sg * (1.0 + ab * (1.0 - sg)))).astype(da_ref.dtype)
            du_ref[0, r, :] = (ds * (ab * sg)).astype(du_ref.dtype)

    down_b, x_in, x_out, x_shape, x_scr, x_args = _carry(down_b, 6, 4, (nt, ns), carried_down)
    dh, da, du, dg_post, *from_down = pl.pallas_call(
        down_b, name=tag + "_down_b", grid=(nt, ns), in_specs=[row, row, vec, wrow, act3, act3] + x_in,
        out_specs=[row, act3, act3, vec] + x_out,
        out_shape=[jax.ShapeDtypeStruct((t, d), MM_DTYPE)] + [jax.ShapeDtypeStruct((ns, t, fs), MM_DTYPE)] * 2
        + [jax.ShapeDtypeStruct((1, d), F32)] + x_shape,
        scratch_shapes=[pltpu.VMEM((tm, d), MM_DTYPE)] + x_scr,
        compiler_params=_params(("arbitrary", "arbitrary")))(h, dy, g_post, wd, a, u, *x_args)

    def down_w(s_ref, dh_ref, dw_ref, acc):
        kk = pl.program_id(1)

        @pl.when(kk == 0)
        def _():
            acc[...] = jnp.zeros(acc.shape, F32)

        acc[...] += lax.dot_general(s_ref[0], dh_ref[...], tn_dims, preferred_element_type=F32)

        @pl.when(kk == nk - 1)
        def _():
            dw_ref[0] = acc[...].astype(dw_ref.dtype)

    dwd = pl.pallas_call(
        down_w, name=tag + "_down_w", grid=(ns, nk),
        in_specs=[pl.BlockSpec((1, tk, fs), lambda q, kk: (q, kk, 0)), pl.BlockSpec((tk, d), lambda q, kk: (kk, 0))],
        out_specs=pl.BlockSpec((1, fs, d), lambda q, kk: (q, 0, 0)),
        out_shape=jax.ShapeDtypeStruct((ns, fs, d), MM_DTYPE), scratch_shapes=[pltpu.VMEM((fs, d), F32)],
        compiler_params=_params(("parallel", "arbitrary")))(s, dh)

    def gate_up_b(da_ref, du_ref, wg_ref, wu_ref, x_ref, dy_ref, g_ref, dx_ref, dg_ref, acc):
        i, q = pl.program_id(0), pl.program_id(1)

        @pl.when((i == 0) & (q == 0))
        def _():
            dg_ref[...] = jnp.zeros(dg_ref.shape, F32)

        @pl.when(q == 0)
        def _():
            acc[...] = jnp.zeros(acc.shape, F32)

        for r in _row_chunks(tm):
            acc[r, :] += (jnp.dot(da_ref[0, r, :], wg_ref[0], preferred_element_type=F32)
                          + jnp.dot(du_ref[0, r, :], wu_ref[0], preferred_element_type=F32))

        @pl.when(q == ns - 1)
        def _():
            for r in _row_chunks(tm):
                xb = x_ref[r, :]
                dx, dg = _rms_bwd(xb, _rms_stats(xb), g_ref[...], acc[r, :])
                dx_ref[r, :] = dy_ref[r, :] + dx
                dg_ref[...] += dg

    def gate_up_w(n_ref, da_ref, du_ref, dwg_ref, dwu_ref, acc_g, acc_u):
        kk = pl.program_id(1)

        @pl.when(kk == 0)
        def _():
            acc_g[...] = jnp.zeros(acc_g.shape, F32)
            acc_u[...] = jnp.zeros(acc_u.shape, F32)

        nb = n_ref[...]
        acc_g[...] += lax.dot_general(da_ref[0], nb, tn_dims, preferred_element_type=F32)
        acc_u[...] += lax.dot_general(du_ref[0], nb, tn_dims, preferred_element_type=F32)

        @pl.when(kk == nk - 1)
        def _():
            dwg_ref[0] = acc_g[...].astype(dwg_ref.dtype)
            dwu_ref[0] = acc_u[...].astype(dwu_ref.dtype)

    k3 = pl.BlockSpec((1, tk, fs), lambda q, kk: (q, kk, 0))
    wout = pl.BlockSpec((1, fs, d), lambda q, kk: (q, 0, 0))
    dwg, dwu = pl.pallas_call(
        gate_up_w, name=tag + "_gate_up_w", grid=(ns, nk),
        in_specs=[pl.BlockSpec((tk, d), lambda q, kk: (kk, 0)), k3, k3], out_specs=[wout, wout],
        out_shape=[jax.ShapeDtypeStruct((ns, fs, d), MM_DTYPE)] * 2,
        scratch_shapes=[pltpu.VMEM((fs, d), F32)] * 2,
        compiler_params=_params(("parallel", "arbitrary")))(n, da, du)

    carried_up = make_carried_up(dwg, dwu, dwd) if make_carried_up else None
    gate_up_b, x_in, x_out, x_shape, x_scr, x_args = _carry(gate_up_b, 7, 2, (nt, ns), carried_up)
    dx, dg_pre, *from_up = pl.pallas_call(
        gate_up_b, name=tag + "_gate_up_b", grid=(nt, ns), in_specs=[act3, act3, wrow, wrow, row, row, vec] + x_in,
        out_specs=[row, vec] + x_out,
        out_shape=[jax.ShapeDtypeStruct((t, d), F32), jax.ShapeDtypeStruct((1, d), F32)] + x_shape,
        scratch_shapes=[pltpu.VMEM((tm, d), F32)] + x_scr,
        compiler_params=_params(("arbitrary", "arbitrary")))(da, du, wg, wu, x, dy, g_pre, *x_args)
    return dx, dg_pre, dwg, dwu, dwd, dg_post, from_down, from_up


NEG = -1e30


def _attn_scale():
    return (MLA_NOPE + MLA_ROPE) ** -0.5


def _causal_pairs(nq, by_key):
    if by_key:
        pairs = [(qi, ki) for ki in range(nq) for qi in range(ki, nq)]
    else:
        pairs = [(qi, ki) for qi in range(nq) for ki in range(qi + 1)]
    return jnp.asarray([p[0] for p in pairs], jnp.int32), jnp.asarray([p[1] for p in pairs], jnp.int32)


def _below_diagonal(shape):
    return lax.broadcasted_iota(jnp.int32, shape, 1) <= lax.broadcasted_iota(jnp.int32, shape, 0)


def _attn_call(name, body, tables, args, in_kinds, out_kinds, scratch, t, tq, carried=None):
    qmap = lambda h, p, qt, kt: (qt[p], h)
    kmap = lambda h, p, qt, kt: (kt[p], h)
    width = HEADS_PER_STEP * HEAD_LANES
    spec = lambda kind: pl.BlockSpec((tq, width), qmap if kind == "q" else kmap)
    n_pairs = tables[0].shape[0]
    n_groups = N_HEADS // HEADS_PER_STEP
    n_in, n_out, n_scr = len(in_kinds), len(out_kinds), scratch
    x_ins = list(carried.ins) if carried else []
    x_outs = list(carried.outs) if carried else []
    x_scr = [pltpu.SemaphoreType.DMA((carried.n_sem,)), pltpu.SemaphoreType.DMA((carried.n_sem,))] if carried else []

    def full_body(qt, kt, *refs):
        ins, refs = refs[:n_in], refs[n_in:]
        xi, refs = refs[:len(x_ins)], refs[len(x_ins):]
        outs, refs = refs[:n_out], refs[n_out:]
        xo, refs = refs[:len(x_outs)], refs[len(x_outs):]
        scr, sems = refs[:n_scr], refs[n_scr:]
        if carried:
            @pl.when((pl.program_id(0) == 0) & (pl.program_id(1) == 0))
            def _():
                carried.start(xi, xo, *sems)

        heads = [tuple(r.at[:, pl.ds(hh * HEAD_LANES, HEAD_LANES)] for r in (*ins, *outs, *scr))
                 for hh in range(HEADS_PER_STEP)]
        body(qt, kt, heads)
        if carried:
            @pl.when((pl.program_id(0) == n_groups - 1) & (pl.program_id(1) == n_pairs - 1))
            def _():
                carried.finish(xi, xo, *sems)

    grid_spec = pltpu.PrefetchScalarGridSpec(
        num_scalar_prefetch=2, grid=(n_groups, n_pairs),
        in_specs=[spec(kd) for kd in in_kinds] + [HBM_SPEC] * len(x_ins),
        out_specs=[spec(kd) for kd in out_kinds] + [HBM_SPEC] * len(x_outs),
        scratch_shapes=[pltpu.VMEM((tq, width), F32)] * n_scr + x_scr)
    return pl.pallas_call(full_body, name=name, grid_spec=grid_spec,
                          out_shape=[jax.ShapeDtypeStruct((t, MLA_PAD), F32) for _ in out_kinds] + x_outs,
                          compiler_params=_params(("arbitrary", "arbitrary")))(*tables, *args, *x_ins)


class _Carried:
    def __init__(self, ins, outs, n_sem, start, finish):
        self.ins, self.outs, self.n_sem, self.start, self.finish = ins, outs, n_sem, start, finish


def _attn_fwd(q, k, v, tq, carried=None):
    t = q.shape[0]
    nq = t // tq

    def body(qt, kt, heads):
        p_id = pl.program_id(1)
        qi, ki = qt[p_id], kt[p_id]

        @pl.when(ki == 0)
        def _():
            for _, _, _, _, _, m_s, l_s, acc_s in heads:
                m_s[...] = jnp.full(m_s.shape, NEG, F32)
                l_s[...] = jnp.zeros(l_s.shape, F32)
                acc_s[...] = jnp.zeros(acc_s.shape, F32)

        def update(diagonal):
            for q_ref, k_ref, v_ref, _, _, m_s, l_s, acc_s in heads:
                s = lax.dot_general(q_ref[...], k_ref[...], (((1,), (1,)), ((), ())), preferred_element_type=F32)
                if diagonal:
                    s = jnp.where(_below_diagonal(s.shape), s, NEG)
                m_old = m_s[...]
                m_new = jnp.maximum(m_old, jnp.max(s, axis=1, keepdims=True))
                alpha = jnp.exp(m_old - m_new)
                p = jnp.exp(s - m_new[:, :1])
                l_s[...] = l_s[...] * alpha + jnp.sum(p, axis=1, keepdims=True)
                acc_s[...] = acc_s[...] * alpha + jnp.dot(p.astype(MM_DTYPE), v_ref[...], preferred_element_type=F32)
                m_s[...] = m_new

        @pl.when(ki < qi)
        def _():
            update(False)

        @pl.when(ki == qi)
        def _():
            update(True)
            for _, _, _, o_ref, lse_ref, m_s, l_s, acc_s in heads:
                o_ref[...] = acc_s[...] / l_s[...]
                lse_ref[...] = m_s[...] + jnp.log(l_s[...])

    return _attn_call("mla_attn_fwd", body, _causal_pairs(nq, False), (q, k, v), "qkk", "qq", 3, t, tq, carried)


def _attn_probs(q, k, lse, diagonal):
    s = lax.dot_general(q, k, (((1,), (1,)), ((), ())), preferred_element_type=F32)
    p = jnp.exp(s - lse[:, :1])
    return jnp.where(_below_diagonal(s.shape), p, 0.0) if diagonal else p


def _attn_bwd_dq(q, k, v, do, lse, delta, tq):
    t = q.shape[0]
    nq = t // tq

    def body(qt, kt, heads):
        p_id = pl.program_id(1)
        qi, ki = qt[p_id], kt[p_id]

        @pl.when(ki == 0)
        def _():
            for refs in heads:
                refs[-1][...] = jnp.zeros(refs[-1].shape, F32)

        def step(diagonal):
            for q_ref, k_ref, v_ref, do_ref, lse_ref, dl_ref, _, acc_s in heads:
                p = _attn_probs(q_ref[...], k_ref[...], lse_ref[...], diagonal)
                dp = lax.dot_general(do_ref[...], v_ref[...], (((1,), (1,)), ((), ())), preferred_element_type=F32)
                ds = p * (dp - dl_ref[...][:, :1])
                acc_s[...] += jnp.dot(ds.astype(MM_DTYPE), k_ref[...], preferred_element_type=F32)

        @pl.when(ki < qi)
        def _():
            step(False)

        @pl.when(ki == qi)
        def _():
            step(True)
            for refs in heads:
                refs[-2][...] = refs[-1][...]

    return _attn_call("mla_attn_bwd_dq", body, _causal_pairs(nq, False), (q, k, v, do, lse, delta), "qkkqqq", "q",
                      1, t, tq)[0]


def _attn_bwd_dkv(q, k, v, do, lse, delta, tq, carried=None):
    t = q.shape[0]
    nq = t // tq

    def body(qt, kt, heads):
        p_id = pl.program_id(1)
        qi, ki = qt[p_id], kt[p_id]

        def step(diagonal):
            for q_ref, k_ref, v_ref, do_ref, lse_ref, dl_ref, _, _, dk_s, dv_s in heads:
                p = _attn_probs(q_ref[...], k_ref[...], lse_ref[...], diagonal)
                dv_s[...] += lax.dot_general(p.astype(MM_DTYPE), do_ref[...], (((0,), (0,)), ((), ())),
                                             preferred_element_type=F32)
                dp = lax.dot_general(do_ref[...], v_ref[...], (((1,), (1,)), ((), ())), preferred_element_type=F32)
                ds = p * (dp - dl_ref[...][:, :1])
                dk_s[...] += lax.dot_general(ds.astype(MM_DTYPE), q_ref[...], (((0,), (0,)), ((), ())),
                                             preferred_element_type=F32)

        @pl.when(qi == ki)
        def _():
            for refs in heads:
                refs[-2][...] = jnp.zeros(refs[-2].shape, F32)
                refs[-1][...] = jnp.zeros(refs[-1].shape, F32)
            step(True)

        @pl.when(qi > ki)
        def _():
            step(False)

        @pl.when(qi == nq - 1)
        def _():
            for refs in heads:
                refs[-4][...] = refs[-2][...]
                refs[-3][...] = refs[-1][...]

    return _attn_call("mla_attn_bwd_dkv", body, _causal_pairs(nq, True), (q, k, v, do, lse, delta), "qkkqqq", "kk",
                      2, t, tq, carried)


def _dotf(a, b, dims=(((1,), (0,)), ((), ()))):
    return lax.dot_general(a, b, dims, preferred_element_type=F32, precision=HI)


def _dot1(a, b, dims=(((1,), (0,)), ((), ()))):
    return lax.dot_general(a.astype(MM_DTYPE), b.astype(MM_DTYPE), dims, preferred_element_type=F32)


def _dot3(a, b, dims=(((1,), (0,)), ((), ()))):
    return lax.dot_general(a, b, dims, preferred_element_type=F32, precision=lax.Precision.HIGH)


NN3 = (((2,), (1,)), ((0,), (0,)))
NT3 = (((2,), (2,)), ((0,), (0,)))
TN3 = (((1,), (1,)), ((0,), (0,)))


def _tri_masks(nh):
    shape = (nh, CHUNK, CHUNK)
    return lax.broadcasted_iota(jnp.int32, shape, 1), lax.broadcasted_iota(jnp.int32, shape, 2)


def _gdn_chunk_common(k, gcc, bb, row, col, dot=_dot1):
    tril = row >= col
    gcr = jnp.swapaxes(gcc, 1, 2)
    dm = jnp.exp(jnp.where(tril, gcc - gcr, NEG))
    kb = k * bb
    lm = jnp.where(row > col, dot(kb, k, NT3) * dm, 0.0)
    return dm, kb, lm


def _unit_lower_inverse(lm, eye):
    t = eye - lm
    p = lm
    for _ in range(CHUNK.bit_length() - 2):
        p = _dot3(p, p, NN3)
        t = t + _dot3(t, p, NN3)
    return t


def _chunk_sum_matrix(tb, upper):
    r = lax.broadcasted_iota(jnp.int32, (tb, tb), 0)
    c = lax.broadcasted_iota(jnp.int32, (tb, tb), 1)
    same = (r // CHUNK) == (c // CHUNK)
    return (same & ((c >= r) if upper else (c <= r))).astype(F32)


def _gdn_fwd(q, k, v, gb, bb):
    nh, t, dh = q.shape
    nchunk = t // CHUNK

    def body(q_ref, k_ref, v_ref, g_ref, b_ref, o_ref, sall_ref, tall_ref, s_s):
        @pl.when(pl.program_id(0) == 0)
        def _():
            s_s[...] = jnp.zeros(s_s.shape, F32)

        row, col = _tri_masks(nh)
        qh, kh, vh, bbh, gcc = q_ref[...], k_ref[...], v_ref[...], b_ref[...], g_ref[...]
        dm, kb, lm = _gdn_chunk_common(kh, gcc, bbh, row, col)
        eg = jnp.exp(gcc)
        glr = gcc[:, CHUNK - 1:CHUNK, :]
        th = _unit_lower_inverse(lm, (row == col).astype(F32))
        w = _dot1(th, kb * eg, NN3)
        u = _dot1(th, vh * bbh, NN3)
        at = jnp.where(row >= col, _dot1(qh, kh, NT3) * dm, 0.0)
        sh = s_s[...]
        vn = u - _dot1(w, sh, NN3)
        o_ref[...] = _dot1(qh * eg, sh, NN3) + _dot1(at, vn, NN3)
        kd = kh * jnp.exp(glr - gcc)
        sall_ref[:, 0] = sh
        tall_ref[...] = th
        s_s[...] = sh * jnp.exp(glr) + _dot1(kd, vn, TN3)

    blk = pl.BlockSpec((nh, CHUNK, dh), lambda n: (0, n, 0))
    return pl.pallas_call(
        body, name="gdn_fwd", grid=(nchunk,), in_specs=[blk] * 5,
        out_specs=[blk, pl.BlockSpec((nh, 1, dh, dh), lambda n: (0, n, 0, 0)), blk],
        out_shape=[jax.ShapeDtypeStruct((nh, t, dh), F32), jax.ShapeDtypeStruct((nh, nchunk, dh, dh), F32),
                   jax.ShapeDtypeStruct((nh, t, CHUNK), F32)],
        scratch_shapes=[pltpu.VMEM((nh, dh, dh), F32)],
        compiler_params=_params(("arbitrary",)))(q, k, v, gb, bb)


def _gdn_bwd(q, k, v, gb, bb, sall, tall, do):
    nh, t, dh = q.shape
    nchunk = t // CHUNK

    def body(q_ref, k_ref, v_ref, g_ref, b_ref, sall_ref, tall_ref, do_ref,
             dq_ref, dk_ref, dv_ref, dg_ref, db_ref, ds_s):
        @pl.when(pl.program_id(0) == 0)
        def _():
            ds_s[...] = jnp.zeros(ds_s.shape, F32)

        row, col = _tri_masks(nh)
        tril, stril = row >= col, row > col
        rsum = lambda x: jnp.sum(x, axis=2, keepdims=True)
        qh, kh, vh, gcc, bbh = q_ref[...], k_ref[...], v_ref[...], g_ref[...], b_ref[...]
        sh, th, doh, dsp = sall_ref[:, 0], tall_ref[...], do_ref[...], ds_s[...]
        dm, kb, lm = _gdn_chunk_common(kh, gcc, bbh, row, col, _dot3)
        eg = jnp.exp(gcc)
        glr = gcc[:, CHUNK - 1:CHUNK, :]
        glv = jnp.exp(glr)
        egl = jnp.exp(glr - gcc)
        rw, ru = kb * eg, vh * bbh
        w, u = _dot3(th, rw, NN3), _dot3(th, ru, NN3)
        at = jnp.where(tril, _dot3(qh, kh, NT3) * dm, 0.0)
        qd, kd = qh * eg, kh * egl
        vn = u - _dot3(w, sh, NN3)
        dgl = jnp.sum(rsum(dsp * sh), axis=1, keepdims=True)
        dkd = _dot3(vn, dsp, NT3)
        dvn = _dot3(kd, dsp, NN3)
        dqd = _dot3(doh, sh, NT3)
        dat = jnp.where(tril, _dot3(doh, vn, NT3), 0.0)
        dvn = dvn + _dot3(at, doh, TN3)
        dw = -_dot3(dvn, sh, NT3)
        ds_s[...] = dsp * glv + _dot3(qd, doh, TN3) - _dot3(w, dvn, TN3)
        dpa = dat * dm
        dq_ref[...] = _dot3(dpa, kh, NN3) + dqd * eg
        dk = _dot3(dpa, qh, TN3) + dkd * egl
        t6 = rsum(dkd * kd)
        dgam = rsum(dqd * qd) - t6
        dgam_last = jnp.sum(t6, axis=1, keepdims=True) + dgl * glv
        drw = _dot3(th, dw, TN3)
        dru = _dot3(th, dvn, TN3)
        dl = -jnp.where(stril, _dot3(drw, w, NT3) + _dot3(dru, u, NT3), 0.0)
        dgam = dgam + rsum(drw * rw)
        dv_ref[...] = dru * bbh
        dp2 = dl * dm
        dkb = drw * eg + _dot3(dp2, kh, NN3)
        dk_ref[...] = dk + _dot3(dp2, kb, TN3) + dkb * bbh
        db_ref[...] = rsum(dru * vh) + rsum(dkb * kh) + jnp.zeros((nh, CHUNK, dh), F32)
        e = dat * at + dl * lm
        dgam_b = dgam + rsum(e) - _dotf(e, jnp.ones((nh, CHUNK, CHUNK), F32), TN3)
        dgam_b = dgam_b + jnp.where(row == CHUNK - 1, dgam_last, 0.0)
        dg_ref[...] = dgam_b

    rev = lambda n: (0, nchunk - 1 - n, 0)
    blk = pl.BlockSpec((nh, CHUNK, dh), rev)
    sblk = pl.BlockSpec((nh, 1, dh, dh), lambda n: (0, nchunk - 1 - n, 0, 0))
    out = jax.ShapeDtypeStruct((nh, t, dh), F32)
    return pl.pallas_call(
        body, name="gdn_bwd", grid=(nchunk,), in_specs=[blk] * 5 + [sblk, blk, blk], out_specs=[blk] * 5,
        out_shape=[out] * 5, scratch_shapes=[pltpu.VMEM((nh, dh, dh), F32)],
        compiler_params=_params(("arbitrary",)))(q, k, v, gb, bb, sall, tall, do)


def _group_ones():
    r = lax.broadcasted_iota(jnp.int32, (GDN_W, GDN_W), 0) // GDN_DH
    c = lax.broadcasted_iota(jnp.int32, (GDN_W, GDN_W), 1) // GDN_DH
    return (r == c).astype(F32)


def _conv_taps(x, xprev, w, has_prev):
    row = lax.broadcasted_iota(jnp.int32, x.shape, 0)
    out = x * w[GDN_CONV - 1:GDN_CONV, :]
    for s in range(1, GDN_CONV):
        sh = jnp.where(row >= s, _roll(x, s, 0), _roll(xprev, s, 0) * has_prev)
        out = out + sh * w[GDN_CONV - 1 - s:GDN_CONV - s, :]
    return out


def _head_cols(x, h):
    return x[:, h * GDN_DH:(h + 1) * GDN_DH]


def _heads_spec(tb):
    return pl.BlockSpec((N_HEADS, tb, GDN_DH), lambda i: (0, i, 0))


def _mixer_fwd(x, positions, w, tb, carried=None):
    t, d = x.shape
    tables = _rope_tables(positions)

    def pre(xb, g):
        return (xb * _rms_stats(xb) * g,)

    (hn,) = _rowwise("mix_pre", pre, [x], [w["mix_pre_g"]], [(d, BF16)], [], tb)
    proj = _mm("mix_in", hn, w["w_in_pad"], "nn", F32)

    def mla_pre(p0, gq, gkv):
        cq, ckv = p0[:, :MLA_Q_RANK], p0[:, MLA_Q_RANK:MLA_Q_RANK + MLA_KV_RANK]
        return cq * _rms_stats(cq) * gq, ckv * _rms_stats(ckv) * gkv

    nq, nkv = _rowwise("mla_pre", mla_pre, [(proj, 512, PIN_MLA // 512, 0)],
                       [w["mla_q_norm_g"], w["mla_kv_norm_g"]], [(MLA_Q_RANK, BF16), (MLA_KV_RANK, BF16)], [], tb)
    qraw = _mm("mla_uq", nq, w["w_uq_pad"], "nn", F32)
    kv = _mm("mla_ukv", nkv, w["w_kv_pad"], "nn", F32)

    def rope_f(qr, kn, vv, kpe, c, s1, s2):
        qo = _heads_apply(qr, lambda xh: _rope(xh, c, s1, s2)) * _attn_scale()
        kp = _rope(kpe, c, s1, s2)
        return qo, kn + jnp.tile(kp, (1, N_HEADS)), vv

    q, k, v = _rowwise("mla_rope", rope_f,
                       [qraw, (kv, MLA_PAD, 0, 0), (kv, MLA_PAD, 1, 0), (proj, HEAD_LANES, PIN_KPE // HEAD_LANES, 0),
                        tables[0], tables[1], tables[2]], [],
                       [(MLA_PAD, BF16)] * 3, [], tb // 2)
    tq = min(512, t)
    o, lse, *carried_out = _attn_fwd(q, k, v, tq, carried)

    def mla_post(ob, g):
        return (ob * _rms_stats(ob, N_HEADS * MLA_V) * g,)

    (cat,) = _rowwise("mla_post", mla_post, [o], [w["mla_out_g_pad"]], [(MLA_PAD, BF16)], [], tb, wide=(CAT_W, 0))

    gones = _group_ones()
    steps = t // tb

    def gdn_pre(xq, xk, xv, pq, pk, pv, cw, go, has_prev):
        outs = []
        for j, (xc, xp) in enumerate(((xq, pq), (xk, pk), (xv, pv))):
            c = _conv_taps(xc, xp, cw[:, j * GDN_W:(j + 1) * GDN_W], has_prev)
            a = c * _sigmoid(c)
            if j < 2:
                rn = lax.rsqrt(_dotf(a * a, go) + EPS)
                a = a * rn
                if j == 0:
                    a = a * (GDN_DH ** -0.5)
            outs.append(a)
        return tuple(outs)

    qh, kh, vh = _gdn_pre_call("gdn_pre", gdn_pre, proj, w["conv_w"], gones, tb, steps)
    heads_shape = jax.ShapeDtypeStruct((N_HEADS, t, GDN_DH), F32)
    lanes_shape = jax.ShapeDtypeStruct((t, HEAD_LANES), F32)
    lanes_spec = pl.BlockSpec((tb, HEAD_LANES), lambda i: (i, 0))
    vec_spec = lambda n: pl.BlockSpec((1, n), lambda i: (0, 0))

    def gate_f(ab_ref, al_ref, dt_ref, g_ref, b_ref, gh_ref, bh_ref):
        g, b = _gb_fwd(ab_ref[...], al_ref[...], dt_ref[...])
        g_ref[...] = g
        b_ref[...] = b
        gc = _dotf(_chunk_sum_matrix(tb, False), g)
        for h in range(N_HEADS):
            gh_ref[h] = jnp.broadcast_to(gc[:, h:h + 1], (tb, GDN_DH))
            bh_ref[h] = jnp.broadcast_to(b[:, N_HEADS + h:N_HEADS + h + 1], (tb, GDN_DH))

    g128, b128, gbh, bbh = pl.pallas_call(
        gate_f, name="gdn_gate_f", grid=(steps,),
        in_specs=[pl.BlockSpec((tb, HEAD_LANES), lambda i: (i, PIN_AB // HEAD_LANES)), vec_spec(HEAD_LANES),
                  vec_spec(HEAD_LANES)],
        out_specs=[lanes_spec, lanes_spec, _heads_spec(tb), _heads_spec(tb)],
        out_shape=[lanes_shape, lanes_shape, heads_shape, heads_shape],
        compiler_params=_params(("arbitrary",)))(proj, w["a_log_pad"], w["dt_bias_pad"])
    oh, sall, tall = _gdn_fwd(qh, kh, vh, gbh, bbh)

    def gdn_post(o_ref, gt_ref, g_ref, cat_in, cat_ref):
        gt, g = gt_ref[...], g_ref[...]
        outs = []
        for h in range(N_HEADS):
            ob, gth = o_ref[h], _head_cols(gt, h)
            outs.append(ob * _rms_stats(ob) * g * (gth * _sigmoid(gth)))
        cat_ref[...] = jnp.concatenate(outs, axis=1).astype(cat_ref.dtype)

    gate_spec = pl.BlockSpec((tb, GDN_W), lambda i: (i, PIN_GATE // GDN_W))
    cat = pl.pallas_call(
        gdn_post, name="gdn_post", grid=(steps,),
        in_specs=[_heads_spec(tb), gate_spec, vec_spec(GDN_DH), ANY_SPEC],
        out_specs=pl.BlockSpec((tb, GDN_W), lambda i: (i, MLA_PAD // GDN_W)),
        out_shape=jax.ShapeDtypeStruct((t, CAT_W), BF16), input_output_aliases={3: 0},
        compiler_params=_params(("arbitrary",)))(oh, proj, w["gdn_norm_g"], cat)
    mixed = _mm("mix_out", cat, w["w_out_pad"], "nn", F32)

    def post(xb, hb, g):
        return (xb + hb * _rms_stats(hb) * g,)

    (y,) = _rowwise("mix_post", post, [x, mixed], [w["mix_post_g"]], [(d, F32)], [], tb)
    saved = dict(x=x, hn=hn, proj=proj, nq=nq, nkv=nkv, q=q, k=k, v=v, o=o, lse=lse, qh=qh, kh=kh, vh=vh,
                 gbh=gbh, bbh=bbh, oh=oh, sall=sall, tall=tall, cat=cat, mixed=mixed,
                 tables=tables, g128=g128, b128=b128)
    return y, saved, carried_out


def _qkv_specs(tb):
    base = PIN_QKV // GDN_W
    cur = [pl.BlockSpec((tb, GDN_W), lambda i, j=j: (i, base + j)) for j in range(3)]
    prev = [pl.BlockSpec((tb, GDN_W), lambda i, j=j: (jnp.maximum(i - 1, 0), base + j)) for j in range(3)]
    return cur + prev


def _gdn_pre_call(name, fn, proj, conv_w, gones, tb, steps):
    t = proj.shape[0]

    def body(xq, xk, xv, pq, pk, pv, cw, go, oq, ok, ov):
        has_prev = jnp.where(pl.program_id(0) == 0, 0.0, 1.0)
        outs = fn(xq[...], xk[...], xv[...], pq[...], pk[...], pv[...], cw[...], go[...], has_prev)
        for r, val in zip((oq, ok, ov), outs):
            for h in range(N_HEADS):
                r[h] = _head_cols(val, h)

    return pl.pallas_call(
        body, name=name, grid=(steps,),
        in_specs=_qkv_specs(tb) + [pl.BlockSpec(conv_w.shape, lambda i: (0, 0)),
                                   pl.BlockSpec(gones.shape, lambda i: (0, 0))],
        out_specs=[_heads_spec(tb)] * 3,
        out_shape=[jax.ShapeDtypeStruct((N_HEADS, t, GDN_DH), F32)] * 3,
        compiler_params=_params(("arbitrary",)))(proj, proj, proj, proj, proj, proj, conv_w, gones)


def _softplus(x):
    return jnp.maximum(x, 0.0) + jnp.log1p(jnp.exp(-jnp.abs(x)))


def _gb_fwd(ab, a_log, dt_bias):
    g = -jnp.exp(a_log) * _softplus(ab + dt_bias)
    return g, _sigmoid(ab)


def _rope_tables(positions):
    half = MLA_ROPE // 2
    freqs = ROPE_THETA ** (-jnp.arange(half, dtype=F32) / half)
    ang = positions.reshape(-1).astype(F32)[:, None] * freqs
    cos, sin = jnp.cos(ang), jnp.sin(ang)
    t = ang.shape[0]
    one = jnp.ones((t, MLA_NOPE), F32)
    z16, z32, z64 = jnp.zeros((t, half), F32), jnp.zeros((t, MLA_ROPE), F32), jnp.zeros((t, MLA_NOPE), F32)
    c = jnp.concatenate([one, cos, cos, jnp.ones((t, MLA_ROPE), F32)], axis=1)
    s1 = jnp.concatenate([z64, -sin, z16, z32], axis=1)
    s2 = jnp.concatenate([z64, z16, sin, z32], axis=1)
    return c, s1, s2


def _mixer_bwd(dy, sv, w, tb, carried=None):
    x, proj = sv["x"], sv["proj"]
    t, d = x.shape
    c, s1, s2 = sv["tables"]
    grads = {}

    def post_b(hb, dyb, g):
        return _rms_bwd(hb, _rms_stats(hb), g, dyb)

    dmixed, grads["mix_post_g"] = _rowwise("mix_post_b", post_b, [sv["mixed"], dy], [w["mix_post_g"]],
                                           [(d, BF16)], [(1, d)], tb)
    dcat = _mm("mix_out_bx", dmixed, w["w_out_pad"], "nt", F32)
    grads["w_out_pad"] = _mm("mix_out_bw", sv["cat"], dmixed, "tn", F32)
    steps = t // tb
    vec_spec = lambda n: pl.BlockSpec((1, n), lambda i: (0, 0))

    def gdn_post_b(o_ref, gt_ref, do_ref, g_ref, dproj_ref, doh_ref, dg_ref):
        @pl.when(pl.program_id(0) == 0)
        def _():
            dg_ref[...] = jnp.zeros(dg_ref.shape, F32)

        gt, dob, g = gt_ref[...], do_ref[...], g_ref[...]
        dgates = []
        for h in range(N_HEADS):
            ob, gth, dobh = o_ref[h], _head_cols(gt, h), _head_cols(dob, h)
            sg = _sigmoid(gth)
            r = _rms_stats(ob)
            dxo, dg = _rms_bwd(ob, r, g, dobh * (gth * sg))
            doh_ref[h] = dxo
            dg_ref[...] += dg
            dgates.append(dobh * (ob * r * g) * (sg * (1.0 + gth * (1.0 - sg))))
        dproj_ref[...] = jnp.concatenate(dgates, axis=1).astype(dproj_ref.dtype)

    dproj, doh, grads["gdn_norm_g"] = pl.pallas_call(
        gdn_post_b, name="gdn_post_b", grid=(steps,),
        in_specs=[_heads_spec(tb), pl.BlockSpec((tb, GDN_W), lambda i: (i, PIN_GATE // GDN_W)),
                  pl.BlockSpec((tb, GDN_W), lambda i: (i, MLA_PAD // GDN_W)), vec_spec(GDN_DH)],
        out_specs=[pl.BlockSpec((tb, GDN_W), lambda i: (i, PIN_GATE // GDN_W)), _heads_spec(tb), vec_spec(GDN_DH)],
        out_shape=[jax.ShapeDtypeStruct((t, PIN_W), BF16), jax.ShapeDtypeStruct((N_HEADS, t, GDN_DH), F32),
                   jax.ShapeDtypeStruct((1, GDN_DH), F32)],
        compiler_params=_params(("arbitrary",)))(sv["oh"], proj, dcat, w["gdn_norm_g"])

    def mla_post_b(ob, dmo, g):
        do, dg = _rms_bwd(ob, _rms_stats(ob, N_HEADS * MLA_V), g, dmo, N_HEADS * MLA_V)
        prod = do * ob
        delta = _heads_apply(prod, lambda ph: jnp.sum(ph, axis=1, keepdims=True) + jnp.zeros_like(ph))
        return do, delta, dg

    do, delta, grads["mla_out_g_pad"] = _rowwise(
        "mla_post_b", mla_post_b, [sv["o"], (dcat, MLA_PAD, 0, 0)], [w["mla_out_g_pad"]],
        [(MLA_PAD, BF16), (MLA_PAD, F32)], [(1, MLA_PAD)], tb // 2)
    tq = min(512, t)
    dq = _attn_bwd_dq(sv["q"], sv["k"], sv["v"], do, sv["lse"], delta, tq)
    dk, dv, *carried_out = _attn_bwd_dkv(sv["q"], sv["k"], sv["v"], do, sv["lse"], delta, tq, carried)

    def rope_b(dqb, dkb, dvb, cc, a1, a2):
        dqr = _heads_apply(dqb * _attn_scale(), lambda xh: _rope(xh, cc, -a1, -a2))
        ksum = dkb[:, :HEAD_LANES]
        for h in range(1, N_HEADS):
            ksum = ksum + dkb[:, h * HEAD_LANES:(h + 1) * HEAD_LANES]
        lane = lax.broadcasted_iota(jnp.int32, ksum.shape, 1)
        keep = (lane >= MLA_NOPE) & (lane < MLA_NOPE + MLA_ROPE)
        dkpe = jnp.where(keep, _rope(ksum, cc, -a1, -a2), 0.0)
        return dqr, jnp.concatenate([dkb, dvb], axis=1), dkpe

    dqraw, dkv, dkpe = _rowwise("mla_rope_b", rope_b, [dq, dk, dv, c, s1, s2], [],
                                [(MLA_PAD, BF16), (2 * MLA_PAD, BF16), (HEAD_LANES, F32)], [], tb // 2)
    dnq = _mm("mla_uq_bx", dqraw, w["w_uq_pad"], "nt", F32)
    grads["w_uq_pad"] = _mm("mla_uq_bw", sv["nq"], dqraw, "tn", F32)
    dnkv = _mm("mla_ukv_bx", dkv, w["w_kv_pad"], "nt", F32)
    grads["w_kv_pad"] = _mm("mla_ukv_bw", sv["nkv"], dkv, "tn", F32)

    def mla_pre_b(p0, dnqb, dnkvb, dkpeb, gq, gkv):
        cq, ckv = p0[:, :MLA_Q_RANK], p0[:, MLA_Q_RANK:MLA_Q_RANK + MLA_KV_RANK]
        dcq, dgq = _rms_bwd(cq, _rms_stats(cq), gq, dnqb)
        dckv, dgkv = _rms_bwd(ckv, _rms_stats(ckv), gkv, dnkvb)
        return jnp.concatenate([dcq, dckv, dkpeb], axis=1), dgq, dgkv

    dproj, grads["mla_q_norm_g"], grads["mla_kv_norm_g"] = _rowwise(
        "mla_pre_b", mla_pre_b, [(proj, 512, PIN_MLA // 512, 0), dnq, dnkv, dkpe],
        [w["mla_q_norm_g"], w["mla_kv_norm_g"]], [(512, BF16)], [(1, MLA_Q_RANK), (1, MLA_KV_RANK)], tb,
        wide=(PIN_W, PIN_MLA // 512), carry=dproj)

    dqh, dkh, dvh, dgh, dbh = _gdn_bwd(sv["qh"], sv["kh"], sv["vh"], sv["gbh"], sv["bbh"], sv["sall"], sv["tall"], doh)
    gones = _group_ones()

    def gdn_pre_b(xq, xk, xv, pq, pk, pv, dq_, dk_, dv_, cw, go, has_prev):
        outs = []
        for j, (xc, xp, dd) in enumerate(((xq, pq, dq_), (xk, pk, dk_), (xv, pv, dv_))):
            cc = _conv_taps(xc, xp, cw[:, j * GDN_W:(j + 1) * GDN_W], has_prev)
            sg = _sigmoid(cc)
            a = cc * sg
            if j < 2:
                rn = lax.rsqrt(_dotf(a * a, go) + EPS)
                if j == 0:
                    dd = dd * (GDN_DH ** -0.5)
                da = rn * dd - a * (rn * rn * rn) * _dotf(dd * a, go)
            else:
                da = dd
            outs.append(da * (sg * (1.0 + cc * (1.0 - sg))))
        return tuple(outs)

    dcq, dck, dcv = _gdn_pre_b_call("gdn_pre_b", gdn_pre_b, proj, (dqh, dkh, dvh), w["conv_w"], gones, tb, steps)
    dproj, grads["conv_w"] = _conv_bwd_call("gdn_conv_b", proj, (dcq, dck, dcv), w["conv_w"], dproj, tb, steps)

    def gate_b(ab_ref, g_ref, b_ref, dgh_ref, dbh_ref, al_ref, dt_ref, carry_ref, dab_ref, dal_ref, ddt_ref):
        @pl.when(pl.program_id(0) == 0)
        def _():
            dal_ref[...] = jnp.zeros(dal_ref.shape, F32)
            ddt_ref[...] = jnp.zeros(ddt_ref.shape, F32)

        ab, g128, b128 = ab_ref[...], g_ref[...], b_ref[...]
        lane = lax.broadcasted_iota(jnp.int32, ab.shape, 1)
        dg_ = jnp.zeros(ab.shape, F32)
        db_ = jnp.zeros(ab.shape, F32)
        for h in range(N_HEADS):
            dg_ = dg_ + jnp.where(lane == h, jnp.broadcast_to(dgh_ref[h][:, 0:1], ab.shape), 0.0)
            db_ = db_ + jnp.where(lane == N_HEADS + h, jnp.broadcast_to(dbh_ref[h][:, 0:1], ab.shape), 0.0)
        dg_ = _dotf(_chunk_sum_matrix(tb, True), dg_)
        slope = -jnp.exp(al_ref[...]) * _sigmoid(ab + dt_ref[...])
        dab_ref[...] = (dg_ * slope + db_ * b128 * (1.0 - b128)).astype(dab_ref.dtype)
        dal_ref[...] += jnp.sum(dg_ * g128, axis=0, keepdims=True)
        ddt_ref[...] += jnp.sum(dg_ * slope, axis=0, keepdims=True)

    lanes_spec = pl.BlockSpec((tb, HEAD_LANES), lambda i: (i, 0))
    ab_spec = pl.BlockSpec((tb, HEAD_LANES), lambda i: (i, PIN_AB // HEAD_LANES))
    dproj, grads["a_log_pad"], grads["dt_bias_pad"] = pl.pallas_call(
        gate_b, name="gdn_gate_b", grid=(steps,),
        in_specs=[ab_spec, lanes_spec, lanes_spec, _heads_spec(tb), _heads_spec(tb), vec_spec(HEAD_LANES),
                  vec_spec(HEAD_LANES), ANY_SPEC],
        out_specs=[ab_spec, vec_spec(HEAD_LANES), vec_spec(HEAD_LANES)],
        out_shape=[jax.ShapeDtypeStruct((t, PIN_W), BF16), jax.ShapeDtypeStruct((1, HEAD_LANES), F32),
                   jax.ShapeDtypeStruct((1, HEAD_LANES), F32)],
        input_output_aliases={7: 0},
        compiler_params=_params(("arbitrary",)))(proj, sv["g128"], sv["b128"], dgh, dbh, w["a_log_pad"],
                                                 w["dt_bias_pad"], dproj)
    dhn = _mm("mix_in_bx", dproj, w["w_in_pad"], "nt", F32)
    grads["w_in_pad"] = _mm("mix_in_bw", sv["hn"], dproj, "tn", F32)

    def pre_b(xb, dnb, dyb, g):
        dx, dg = _rms_bwd(xb, _rms_stats(xb), g, dnb)
        return dyb + dx, dg

    dx, grads["mix_pre_g"] = _rowwise("mix_pre_b", pre_b, [x, dhn, dy], [w["mix_pre_g"]], [(d, F32)], [(1, d)], tb)
    return dx, grads, carried_out


def _gdn_pre_b_call(name, fn, proj, dd, conv_w, gones, tb, steps):
    t = proj.shape[0]

    def body(xq, xk, xv, pq, pk, pv, d0, d1, d2, cw, go, oq, ok, ov):
        has_prev = jnp.where(pl.program_id(0) == 0, 0.0, 1.0)
        dd_rows = [jnp.concatenate([dr[h] for h in range(N_HEADS)], axis=1) for dr in (d0, d1, d2)]
        outs = fn(xq[...], xk[...], xv[...], pq[...], pk[...], pv[...], *dd_rows, cw[...], go[...], has_prev)
        for r, val in zip((oq, ok, ov), outs):
            r[...] = val

    return pl.pallas_call(
        body, name=name, grid=(steps,),
        in_specs=_qkv_specs(tb) + [_heads_spec(tb)] * 3 + [pl.BlockSpec(conv_w.shape, lambda i: (0, 0)),
                                                          pl.BlockSpec(gones.shape, lambda i: (0, 0))],
        out_specs=[pl.BlockSpec((tb, GDN_W), lambda i: (i, 0))] * 3,
        out_shape=[jax.ShapeDtypeStruct((t, GDN_W), F32)] * 3,
        compiler_params=_params(("arbitrary",)))(proj, proj, proj, proj, proj, proj, *dd, conv_w, gones)


def _conv_bwd_call(name, proj, dc, conv_w, dproj, tb, steps):
    t = proj.shape[0]
    dcur = [pl.BlockSpec((tb, GDN_W), lambda i: (i, 0))] * 3
    dnext = [pl.BlockSpec((tb, GDN_W), lambda i: (jnp.minimum(i + 1, steps - 1), 0))] * 3

    def body(xq, xk, xv, pq, pk, pv, d0, d1, d2, n0, n1, n2, cw, carry_ref, dx_ref, dw_ref):
        i = pl.program_id(0)
        has_prev = jnp.where(i == 0, 0.0, 1.0)
        has_next = jnp.where(i == steps - 1, 0.0, 1.0)

        @pl.when(i == 0)
        def _():
            dw_ref[...] = jnp.zeros(dw_ref.shape, F32)

        wv = cw[...]
        dws, dxs = [], []
        for j, (xr, pr, dr, nr) in enumerate(((xq, pq, d0, n0), (xk, pk, d1, n1), (xv, pv, d2, n2))):
            x, xp, dcv, dnx = xr[...], pr[...], dr[...], nr[...]
            wj = wv[:, j * GDN_W:(j + 1) * GDN_W]
            row = lax.broadcasted_iota(jnp.int32, x.shape, 0)
            dx = dcv * wj[GDN_CONV - 1:GDN_CONV, :]
            rows_w = [jnp.sum(dcv * x, axis=0, keepdims=True)]
            for s in range(1, GDN_CONV):
                up = jnp.where(row < tb - s, _roll(dcv, tb - s, 0), _roll(dnx, tb - s, 0) * has_next)
                dx = dx + up * wj[GDN_CONV - 1 - s:GDN_CONV - s, :]
                sh = jnp.where(row >= s, _roll(x, s, 0), _roll(xp, s, 0) * has_prev)
                rows_w.append(jnp.sum(dcv * sh, axis=0, keepdims=True))
            dxs.append(dx)
            dws.append(jnp.concatenate(rows_w[::-1], axis=0))
        dx_ref[...] = jnp.concatenate(dxs, axis=1).astype(dx_ref.dtype)
        dw_ref[...] += jnp.concatenate(dws, axis=1)

    return pl.pallas_call(
        body, name=name, grid=(steps,),
        in_specs=_qkv_specs(tb) + dcur + dnext + [pl.BlockSpec(conv_w.shape, lambda i: (0, 0)), ANY_SPEC],
        out_specs=[pl.BlockSpec((tb, 3 * GDN_W), lambda i: (i, PIN_QKV // (3 * GDN_W))),
                   pl.BlockSpec(conv_w.shape, lambda i: (0, 0))],
        out_shape=[jax.ShapeDtypeStruct((t, PIN_W), BF16), jax.ShapeDtypeStruct(conv_w.shape, F32)],
        input_output_aliases={13: 0},
        compiler_params=_params(("arbitrary",)))(proj, proj, proj, proj, proj, proj, *dc, *dc, conv_w, dproj)


def _pad_heads_cols(wm, per_head):
    r = wm.shape[0]
    return jnp.pad(wm.reshape(r, N_HEADS, per_head), ((0, 0), (0, 0), (0, HEAD_LANES - per_head))).reshape(r, MLA_PAD)


def _unpad_heads_cols(wm, per_head):
    r = wm.shape[0]
    return wm.reshape(r, N_HEADS, HEAD_LANES)[:, :, :per_head].reshape(r, N_HEADS * per_head)


def _win_to_pad(wi):
    r = wi.shape[0]
    z = lambda n: jnp.zeros((r, n), wi.dtype)
    o = MLA_Q_RANK + MLA_KV_RANK
    kpe = wi[:, o:o + MLA_ROPE]
    o2 = o + MLA_ROPE
    qkv = wi[:, o2:o2 + 3 * GDN_W]
    o3 = o2 + 3 * GDN_W
    ab = wi[:, o3:o3 + 2 * N_HEADS]
    gate = wi[:, o3 + 2 * N_HEADS:]
    return jnp.concatenate([qkv, wi[:, :o], z(MLA_NOPE), kpe, z(HEAD_LANES - MLA_NOPE - MLA_ROPE), gate, ab,
                            z(HEAD_LANES - 2 * N_HEADS)], axis=1)


def _win_from_pad(wp):
    return jnp.concatenate([wp[:, PIN_MLA:PIN_KPE], wp[:, PIN_KPE + MLA_NOPE:PIN_KPE + MLA_NOPE + MLA_ROPE],
                            wp[:, PIN_QKV:PIN_QKV + 3 * GDN_W], wp[:, PIN_AB:PIN_AB + 2 * N_HEADS],
                            wp[:, PIN_GATE:PIN_GATE + GDN_W]], axis=1)


def _wkv_to_pad(wkv):
    r = wkv.shape[0]
    w3 = wkv.reshape(r, N_HEADS, MLA_NOPE + MLA_V)
    kpart = jnp.pad(w3[:, :, :MLA_NOPE], ((0, 0), (0, 0), (0, HEAD_LANES - MLA_NOPE))).reshape(r, MLA_PAD)
    vpart = jnp.pad(w3[:, :, MLA_NOPE:], ((0, 0), (0, 0), (0, HEAD_LANES - MLA_V))).reshape(r, MLA_PAD)
    return jnp.concatenate([kpart, vpart], axis=1)


def _wkv_from_pad(wp):
    r = wp.shape[0]
    kpart = wp[:, :MLA_PAD].reshape(r, N_HEADS, HEAD_LANES)[:, :, :MLA_NOPE]
    vpart = wp[:, MLA_PAD:].reshape(r, N_HEADS, HEAD_LANES)[:, :, :MLA_V]
    return jnp.concatenate([kpart, vpart], axis=2).reshape(r, N_HEADS * (MLA_NOPE + MLA_V))


def _wout_to_pad(wo):
    n = wo.shape[1]
    mla = jnp.pad(wo[:N_HEADS * MLA_V].reshape(N_HEADS, MLA_V, n), ((0, 0), (0, HEAD_LANES - MLA_V), (0, 0)))
    return jnp.concatenate([mla.reshape(MLA_PAD, n), wo[N_HEADS * MLA_V:]], axis=0)


def _wout_from_pad(wp):
    n = wp.shape[1]
    mla = wp[:MLA_PAD].reshape(N_HEADS, HEAD_LANES, n)[:, :MLA_V].reshape(N_HEADS * MLA_V, n)
    return jnp.concatenate([mla, wp[MLA_PAD:]], axis=0)


def _pad_lanes(v, n):
    return jnp.pad(v, ((0, 0), (0, n - v.shape[1])))


def _compute_weights(full):
    w = {}
    for n in FFN_BIG:
        if n in full:
            w[n] = full[n].astype(MM_DTYPE)
    w["w_in_pad"] = _win_to_pad(full["w_in"]).astype(MM_DTYPE)
    w["w_uq_pad"] = _pad_heads_cols(full["mla_w_uq"], MLA_NOPE + MLA_ROPE).astype(MM_DTYPE)
    w["w_kv_pad"] = _wkv_to_pad(full["mla_w_ukv"]).astype(MM_DTYPE)
    w["w_out_pad"] = _wout_to_pad(full["w_out"]).astype(MM_DTYPE)
    w["conv_w"] = full["gdn_conv_w"].astype(F32)
    for n in ("ffn1_pre_g", "ffn1_post_g", "mix_pre_g", "mla_q_norm_g", "mla_kv_norm_g", "gdn_norm_g", "mix_post_g",
              "ffn2_pre_g", "ffn2_post_g"):
        w[n] = full[n]
    w["mla_out_g_pad"] = _pad_heads_cols(full["mla_out_g"], MLA_V)
    w["a_log_pad"] = _pad_lanes(full["gdn_a_log"], HEAD_LANES)
    w["dt_bias_pad"] = _pad_lanes(full["gdn_dt_bias"], HEAD_LANES)
    return w


FFN2_BIG = FFN_BIG[3:]


def _local_step(x, positions, loss_target, full, late=None):
    t, d = x.shape
    tb = min(512, t)
    tm = min(1024, t)
    w = _compute_weights(full)
    ffn = lambda tag: (w[tag + "_pre_g"], w[tag + "_w_gate"], w[tag + "_w_up"], w[tag + "_w_down"], w[tag + "_post_g"])
    x1, sv1 = _ffn_fwd("ffn1", x, *ffn("ffn1"), tm)
    x2, svm, gathered = _mixer_fwd(x1, positions, w, tb, _carried_gather(late[0]) if late else None)
    for n, gw in zip(FFN2_BIG, gathered):
        w[n] = gw
    x3, sv2 = _ffn_fwd("ffn2", x2, *ffn("ffn2"), tm)

    def loss_f(yb, tg):
        e = yb - tg
        return e * (1.0 / d), jnp.sum(e * e, axis=0, keepdims=True)

    dy, lsum = _rowwise("loss", loss_f, [x3, loss_target], [], [(d, F32)], [(1, d)], tb)
    g = {}
    dx2, g["ffn2_pre_g"], g["ffn2_w_gate"], g["ffn2_w_up"], g["ffn2_w_down"], g["ffn2_post_g"], _, _ = _ffn_bwd(
        "ffn2", dy, sv2, *ffn("ffn2"), tm, tm)

    def pair_sums(arrs, tag):
        got = _swap_halves(arrs, tag)
        return [_add_pair("add_pair%s_%d" % (tag, i), gi, gt, late[1]) for i, (gi, gt) in enumerate(zip(arrs, got))]

    def chip_sums(pairs, slabs, tag):
        return [_add_chips("add_chips%s_%d" % (tag, i), pr, sl, late[2]) for i, (pr, sl) in enumerate(zip(pairs, slabs))]

    if late:
        pairs2 = pair_sums([g[n] for n in FFN2_BIG], "_ffn2")
        dx1, gm, slabs2 = _mixer_bwd(dx2, svm, w, tb, _carried_scatter(pairs2))
        for n, hs in zip(FFN2_BIG, chip_sums(pairs2, slabs2, "_ffn2")):
            g[n] = hs
    else:
        dx1, gm, _ = _mixer_bwd(dx2, svm, w, tb)
    g["w_in"] = _win_from_pad(gm["w_in_pad"])
    g["mla_w_uq"] = _unpad_heads_cols(gm["w_uq_pad"], MLA_NOPE + MLA_ROPE)
    g["mla_w_ukv"] = _wkv_from_pad(gm["w_kv_pad"])
    g["gdn_conv_w"] = gm["conv_w"]
    g["w_out"] = _wout_from_pad(gm["w_out_pad"])
    ffn1_names = FFN_BIG[:3]
    if late:
        quarters = [_pack([jnp.split(g[n], N_SHARD, axis=SHARD_AXIS[n])[q] for n in MIX_BIG], MM_DTYPE)
                    for q in range(N_SHARD)]
        pairs_m = pair_sums([jnp.stack(quarters)], "_mix")
        pairs1 = []

        def make_up(*dws):
            pairs1.extend(pair_sums(list(dws), "_ffn1"))
            return _carried_scatter(pairs1)

        dx0, g["ffn1_pre_g"], _, _, _, g["ffn1_post_g"], slabs_m, slabs1 = _ffn_bwd(
            "ffn1", dx1, sv1, *ffn("ffn1"), tm, tm, _carried_scatter(pairs_m), make_up)
        for n, hs in zip(ffn1_names, chip_sums(pairs1, slabs1, "_ffn1")):
            g[n] = hs
        g["mix_pack"] = chip_sums(pairs_m, slabs_m, "_mix")[0]
    else:
        dx0, g["ffn1_pre_g"], g["ffn1_w_gate"], g["ffn1_w_up"], g["ffn1_w_down"], g["ffn1_post_g"], _, _ = _ffn_bwd(
            "ffn1", dx1, sv1, *ffn("ffn1"), tm, tm)
    g["mix_pre_g"], g["mix_post_g"] = gm["mix_pre_g"], gm["mix_post_g"]
    g["mla_q_norm_g"], g["mla_kv_norm_g"] = gm["mla_q_norm_g"], gm["mla_kv_norm_g"]
    g["gdn_norm_g"] = gm["gdn_norm_g"]
    g["mla_out_g"] = _unpad_heads_cols(gm["mla_out_g_pad"], MLA_V)
    g["gdn_a_log"] = gm["a_log_pad"][:, :N_HEADS]
    g["gdn_dt_bias"] = gm["dt_bias_pad"][:, :N_HEADS]
    return lsum, dx0, g


HBM_SPEC = pl.BlockSpec(memory_space=pltpu.HBM)


def _place():
    return lax.axis_index("x"), lax.axis_index("y"), lax.axis_index("c")


def _exchange_call(name, body, ins, out_shapes, n_remote, n_local):
    return pl.pallas_call(
        body, name=name, in_specs=[HBM_SPEC] * len(ins), out_specs=[HBM_SPEC] * len(out_shapes), out_shape=out_shapes,
        scratch_shapes=[pltpu.SemaphoreType.DMA((n_remote,)), pltpu.SemaphoreType.DMA((n_remote,)),
                        pltpu.SemaphoreType.DMA((n_local,))])(*ins)


def _other_chips(x, y):
    return [(1 - x, y), (x, 1 - y), (1 - x, 1 - y)]


def _at_each_chip(fn):
    x, y, _ = _place()
    for cx in range(2):
        for cy in range(2):
            pl.when((x == cx) & (y == cy))(functools.partial(fn, cx, cy))


def _at_each_device(fn):
    x, y, c = _place()
    for cx in range(2):
        for cy in range(2):
            for cc in range(2):
                pl.when((x == cx) & (y == cy) & (c == cc))(functools.partial(fn, cx, cy, cc))


def _at_each_core(fn):
    c = lax.axis_index("c")
    for cc in range(2):
        pl.when(c == cc)(functools.partial(fn, cc))


def _gather_shards(ws):
    nw = len(ws)

    def body(*refs):
        w_refs, out_refs = refs[:nw], refs[nw:2 * nw]
        send_sems, recv_sems, local_sems = refs[2 * nw:]

        def run(x, y, c):
            chips = _other_chips(x, y)
            me, sibling = 2 * x + y, (x, y, 1 - c)

            def half(ref, which):
                hr = ref.shape[0] // 2
                return ref.at[pl.ds(which * hr, hr)]

            def over_ici(i, j, src, slab, to):
                return pltpu.make_async_remote_copy(
                    src_ref=half(src, c), dst_ref=half(out_refs[i].at[slab], c), send_sem=send_sems.at[7 * i + j],
                    recv_sem=recv_sems.at[7 * i + j], device_id=to, device_id_type=MESH)

            def over_d2d(i, j, slab, which):
                return pltpu.make_async_remote_copy(
                    src_ref=half(out_refs[i].at[slab], which), dst_ref=half(out_refs[i].at[slab], which),
                    send_sem=send_sems.at[7 * i + 3 + j], recv_sem=recv_sems.at[7 * i + 3 + j], device_id=sibling,
                    device_id_type=MESH)

            def own(i, w_ref):
                return pltpu.make_async_remote_copy(
                    src_ref=w_ref, dst_ref=out_refs[i].at[me], send_sem=send_sems.at[7 * i + 6],
                    recv_sem=recv_sems.at[7 * i + 6], device_id=sibling, device_id_type=MESH)

            sends, passed = [], []
            for i, w_ref in enumerate(w_refs):
                for j, (px, py) in enumerate(chips):
                    sends.append(over_ici(i, j, w_ref, me, (px, py, c)))
                    sends[-1].start()
            for i, w_ref in enumerate(w_refs):
                sends.append(own(i, w_ref))
                sends[-1].start()
            for i, w_ref in enumerate(w_refs):
                for j, (px, py) in enumerate(chips):
                    over_ici(i, j, w_ref, 2 * px + py, (px, py, c)).wait_recv()
                    passed.append(over_d2d(i, j, 2 * px + py, c))
                    passed[-1].start()
            for i, w_ref in enumerate(w_refs):
                own(i, w_ref).wait_recv()
                for j, (px, py) in enumerate(chips):
                    over_d2d(i, j, 2 * px + py, 1 - c).wait_recv()
            for cp in sends + passed:
                cp.wait_send()

        _at_each_device(run)

    outs = [jax.ShapeDtypeStruct((N_SHARD,) + w.shape, w.dtype) for w in ws]
    return _exchange_call("gather_weight_shards", body, ws, outs, 7 * nw, 1)


def _swap_halves(gs, tag=""):
    ng = len(gs)

    def body(*refs):
        g_refs, got_refs = refs[:ng], refs[ng:2 * ng]
        send_sems, recv_sems, _ = refs[2 * ng:]
        x, y, _ = _place()

        def run(c):
            sends = []
            for i, (g_ref, got_ref) in enumerate(zip(g_refs, got_refs)):
                hr = got_ref.shape[1]
                sends.append(pltpu.make_async_remote_copy(
                    src_ref=g_ref.at[:, pl.ds((1 - c) * hr, hr)], dst_ref=got_ref, send_sem=send_sems.at[i],
                    recv_sem=recv_sems.at[i], device_id=(x, y, 1 - c), device_id_type=MESH))
                sends[-1].start()
            for cp in sends:
                cp.wait()

        _at_each_core(run)

    halves = [jax.ShapeDtypeStruct((g.shape[0], g.shape[1] // 2, g.shape[2]), g.dtype) for g in gs]
    return _exchange_call("swap_grad_halves" + tag, body, gs, halves, ng, 1)


def _scatter_copies(p_refs, out_refs, send_sems, recv_sems, x, y):
    c = lax.axis_index("c")
    copies = []
    for i, (p_ref, out_ref) in enumerate(zip(p_refs, out_refs)):
        for j, (px, py) in enumerate(_other_chips(x, y)):
            copies.append(pltpu.make_async_remote_copy(
                src_ref=p_ref.at[2 * px + py], dst_ref=out_ref.at[j], send_sem=send_sems.at[3 * i + j],
                recv_sem=recv_sems.at[3 * i + j], device_id=(px, py, c), device_id_type=MESH))
    return copies


def _start_all(make, *refs):
    def run(x, y):
        for cp in make(*refs, x, y):
            cp.start()

    _at_each_chip(run)


def _wait_all(make, *refs):
    def run(x, y):
        copies = make(*refs, x, y)
        for cp in copies:
            cp.wait_recv()
        for cp in copies:
            cp.wait_send()

    _at_each_chip(run)


def _scatter_shapes(ps):
    return [jax.ShapeDtypeStruct((3,) + p.shape[1:], p.dtype) for p in ps]


def _carried_scatter(ps):
    return _Carried(ps, _scatter_shapes(ps), 3 * len(ps), functools.partial(_start_all, _scatter_copies),
                    functools.partial(_wait_all, _scatter_copies))


def _direct_gather_copies(w_refs, out_refs, send_sems, recv_sems, x, y, arriving):
    c = lax.axis_index("c")
    me = 2 * x + y
    peers = [((px, py, c), 2 * px + py) for px, py in _other_chips(x, y)] + [((x, y, 1 - c), me)]
    copies = []
    for i, (w_ref, out_ref) in enumerate(zip(w_refs, out_refs)):
        for j, (peer, slab) in enumerate(peers):
            copies.append(pltpu.make_async_remote_copy(
                src_ref=w_ref, dst_ref=out_ref.at[slab if arriving else me], send_sem=send_sems.at[4 * i + j],
                recv_sem=recv_sems.at[4 * i + j], device_id=peer, device_id_type=MESH))
    return copies


def _carried_gather(ws):
    def start(w_refs, out_refs, send_sems, recv_sems):
        def run(x, y):
            for cp in _direct_gather_copies(w_refs, out_refs, send_sems, recv_sems, x, y, False):
                cp.start()

        _at_each_chip(run)

    def finish(w_refs, out_refs, send_sems, recv_sems):
        def run(x, y):
            for cp in _direct_gather_copies(w_refs, out_refs, send_sems, recv_sems, x, y, True):
                cp.wait_recv()
            for cp in _direct_gather_copies(w_refs, out_refs, send_sems, recv_sems, x, y, False):
                cp.wait_send()

        _at_each_chip(run)

    outs = [jax.ShapeDtypeStruct((N_SHARD,) + w.shape, w.dtype) for w in ws]
    return _Carried(ws, outs, 4 * len(ws), start, finish)


def _share_halves(hs):
    n = len(hs)

    def body(*refs):
        h_refs, out_refs = refs[:n], refs[n:2 * n]
        send_sems, recv_sems, _ = refs[2 * n:]
        x, y, c = _place()
        sends = []
        for i, (h_ref, out_ref) in enumerate(zip(h_refs, out_refs)):
            sends.append(pltpu.make_async_remote_copy(
                src_ref=h_ref, dst_ref=out_ref, send_sem=send_sems.at[i], recv_sem=recv_sems.at[i],
                device_id=(x, y, 1 - c), device_id_type=MESH))
            sends[-1].start()
        for cp in sends:
            cp.wait()

    outs = [jax.ShapeDtypeStruct(h.shape, h.dtype) for h in hs]
    return _exchange_call("share_grad_halves", body, hs, outs, n, 1)


def _scalar_grid_call(name, body, scalars, grid, in_specs, out_specs, out_shape, args):
    grid_spec = pltpu.PrefetchScalarGridSpec(num_scalar_prefetch=len(scalars), grid=grid, in_specs=in_specs,
                                             out_specs=out_specs)
    return pl.pallas_call(body, name=name, grid_spec=grid_spec, out_shape=out_shape,
                          compiler_params=_params(("arbitrary",) * len(grid)))(*scalars, *args)


def _add_pair(name, g, got, core):
    ns_, hr, cols = got.shape
    th = _row_tile(hr, 512)
    nb = hr // th

    def body(core_ref, g_ref, got_ref, out_ref):
        out_ref[...] = (g_ref[...].astype(F32) + got_ref[...].astype(F32)).astype(out_ref.dtype)

    blk = pl.BlockSpec((1, th, cols), lambda q, j, core_ref: (q, j, 0))
    own = pl.BlockSpec((1, th, cols), lambda q, j, core_ref: (q, core_ref[0] * nb + j, 0))
    return _scalar_grid_call(name, body, [core], (ns_, nb), [own, blk], blk,
                             jax.ShapeDtypeStruct(got.shape, got.dtype), [g, got])


def _add_chips(name, pairs, slabs, chip):
    _, hr, cols = slabs.shape
    th = _row_tile(hr, 512)

    def body(chip_ref, own_ref, s0_ref, s1_ref, s2_ref, out_ref):
        total = own_ref[0].astype(F32) + s0_ref[0].astype(F32)
        out_ref[...] = (total + s1_ref[0].astype(F32)) + s2_ref[0].astype(F32)

    own = pl.BlockSpec((1, th, cols), lambda j, chip_ref: (chip_ref[0], j, 0))
    others = [pl.BlockSpec((1, th, cols), lambda j, chip_ref, k=k: (k, j, 0)) for k in range(3)]
    return _scalar_grid_call(name, body, [chip], (hr // th,), [own] + others,
                             pl.BlockSpec((th, cols), lambda j, chip_ref: (j, 0)),
                             jax.ShapeDtypeStruct((hr, cols), F32), [pairs, slabs, slabs, slabs])


def _join_halves(name, mine, other, core):
    hr, cols = mine.shape
    th = _row_tile(hr, 512)
    nb = hr // th

    def body(core_ref, mine_ref, other_ref, out_ref):
        is_mine = pl.program_id(0) == core_ref[0]

        @pl.when(is_mine)
        def _():
            out_ref[0] = mine_ref[...]

        @pl.when(jnp.logical_not(is_mine))
        def _():
            out_ref[0] = other_ref[...]

    blk = pl.BlockSpec((th, cols), lambda h, j, core_ref: (j, 0))
    return _scalar_grid_call(name, body, [core], (2, nb), [blk, blk],
                             pl.BlockSpec((1, th, cols), lambda h, j, core_ref: (0, h * nb + j, 0)),
                             jax.ShapeDtypeStruct((1, 2 * hr, cols), mine.dtype), [mine, other])


def _gather_small(sp):
    def body(s_ref, out_ref, send_sems, recv_sems, local_sem):
        x, y, c = _place()
        me = 4 * x + 2 * y + c
        peers = [(x ^ (m >> 2), y ^ ((m >> 1) & 1), c ^ (m & 1)) for m in range(1, 8)]
        mine = pltpu.make_async_copy(s_ref, out_ref.at[me], local_sem)
        mine.start()
        sends = [pltpu.make_async_remote_copy(src_ref=s_ref, dst_ref=out_ref.at[me], send_sem=send_sems.at[j],
                                              recv_sem=recv_sems.at[j], device_id=p, device_id_type=MESH)
                 for j, p in enumerate(peers)]
        for cp in sends:
            cp.start()
        for j, (px, py, pc) in enumerate(peers):
            pltpu.make_async_remote_copy(src_ref=s_ref, dst_ref=out_ref.at[4 * px + 2 * py + pc],
                                         send_sem=send_sems.at[j], recv_sem=recv_sems.at[j], device_id=(px, py, pc),
                                         device_id_type=MESH).wait_recv()
        for cp in sends:
            cp.wait_send()
        mine.wait()

    return pl.pallas_call(
        body, name="gather_small_grads", in_specs=[HBM_SPEC], out_specs=HBM_SPEC,
        out_shape=jax.ShapeDtypeStruct((8,) + sp.shape, sp.dtype),
        scratch_shapes=[pltpu.SemaphoreType.DMA((7,)), pltpu.SemaphoreType.DMA((7,)), pltpu.SemaphoreType.DMA])(sp)


def _pack_rows(total):
    rows = -(-total // LANES)
    return -(-rows // 32) * 32


def _pack(arrs, dtype):
    flat = jnp.concatenate([a.reshape(-1).astype(dtype) for a in arrs])
    rows = _pack_rows(flat.shape[0])
    return jnp.pad(flat, (0, rows * LANES - flat.shape[0])).reshape(rows, LANES)


def _unpack(buf, shapes):
    flat = buf.reshape(-1)
    out, off = {}, 0
    for n, shp in shapes:
        size = shp[0] * shp[1]
        out[n] = flat[off:off + size].reshape(shp)
        off += size
    return out


def _to_wire(name, w3):
    _, r, cols = w3.shape
    tb = _row_tile(r, 512)

    def body(w_ref, o_ref):
        o_ref[...] = w_ref[0].astype(o_ref.dtype)

    return pl.pallas_call(
        body, name=name, grid=(r // tb,), in_specs=[pl.BlockSpec((1, tb, cols), lambda i: (0, i, 0))],
        out_specs=pl.BlockSpec((tb, cols), lambda i: (i, 0)), out_shape=jax.ShapeDtypeStruct((r, cols), MM_DTYPE),
        compiler_params=_params(("arbitrary",)))(w3)


def _adamw(name, w3, g, m3, v3, tb):
    c1 = 1.0 - ADAM_B1 ** ADAM_STEP
    c2 = 1.0 - ADAM_B2 ** ADAM_STEP
    _, r, cols = w3.shape
    emit = g.ndim == 2
    blk3 = pl.BlockSpec((1, tb, cols), lambda i: (0, i, 0))
    g_spec = pl.BlockSpec((tb, cols), lambda i: (i, 0)) if emit else blk3

    def body(w_ref, g_ref, m_ref, v_ref, *out_refs):
        gb = g_ref[...] if emit else g_ref[0]
        m2 = ADAM_B1 * m_ref[0] + (1.0 - ADAM_B1) * gb
        v2 = ADAM_B2 * v_ref[0] + (1.0 - ADAM_B2) * (gb * gb)
        out_refs[-3][0] = -ADAM_LR * ((m2 / c1) / (jnp.sqrt(v2 / c2) + ADAM_EPS) + ADAM_WD * w_ref[0])
        out_refs[-2][0] = m2
        out_refs[-1][0] = v2
        if emit:
            out_refs[0][0] = gb

    n_out = 4 if emit else 3
    outs = pl.pallas_call(
        body, name=name, grid=(r // tb,), in_specs=[blk3, g_spec, blk3, blk3], out_specs=[blk3] * n_out,
        out_shape=[jax.ShapeDtypeStruct((1, r, cols), F32)] * n_out,
        compiler_params=_params(("arbitrary",)))(w3, g, m3, v3)
    return outs if emit else [g] + list(outs)


def _row_tile(rows, pref):
    if rows <= pref:
        return rows
    t = pref
    while t >= 8:
        if rows % t == 0 and t % 8 == 0:
            return t
        t -= 8
    return rows


def kernel(x, positions, ffn1_pre_g, ffn1_w_gate, ffn1_w_up, ffn1_w_down, ffn1_post_g, mix_pre_g, w_in, mla_q_norm_g, mla_w_uq, mla_kv_norm_g, mla_w_ukv, mla_out_g, gdn_conv_w, gdn_a_log, gdn_dt_bias, gdn_norm_g, w_out, mix_post_g, ffn2_pre_g, ffn2_w_gate, ffn2_w_up, ffn2_w_down, ffn2_post_g, loss_target, m_ffn1_pre_g, m_ffn1_w_gate, m_ffn1_w_up, m_ffn1_w_down, m_ffn1_post_g, m_mix_pre_g, m_w_in, m_mla_q_norm_g, m_mla_w_uq, m_mla_kv_norm_g, m_mla_w_ukv, m_mla_out_g, m_gdn_conv_w, m_gdn_a_log, m_gdn_dt_bias, m_gdn_norm_g, m_w_out, m_mix_post_g, m_ffn2_pre_g, m_ffn2_w_gate, m_ffn2_w_up, m_ffn2_w_down, m_ffn2_post_g, v_ffn1_pre_g, v_ffn1_w_gate, v_ffn1_w_up, v_ffn1_w_down, v_ffn1_post_g, v_mix_pre_g, v_w_in, v_mla_q_norm_g, v_mla_w_uq, v_mla_kv_norm_g, v_mla_w_ukv, v_mla_out_g, v_gdn_conv_w, v_gdn_a_log, v_gdn_dt_bias, v_gdn_norm_g, v_w_out, v_mix_post_g, v_ffn2_pre_g, v_ffn2_w_gate, v_ffn2_w_up, v_ffn2_w_down, v_ffn2_post_g):
    args = dict(locals())
    wsh = {n: args[n][0] for n in WEIGHTS}
    msh = {n: args["m_" + n][0] if args["m_" + n].ndim == 3 else args["m_" + n] for n in WEIGHTS}
    vsh = {n: args["v_" + n][0] if args["v_" + n].ndim == 3 else args["v_" + n] for n in WEIGHTS}
    for n in SMALL:
        wsh[n] = args[n]
    mix_shapes = [(n, wsh[n].shape) for n in MIX_BIG]

    early = FFN_BIG[:3]
    held = lambda a, n: jnp.swapaxes(a, 1, 2) if n in TRANSPOSED else a
    gathered = _gather_shards([_to_wire("wire_" + n, held(args[n], n)) for n in early]
                              + [_pack([wsh[n] for n in MIX_BIG], MM_DTYPE)])
    full = {n: wsh[n] for n in SMALL}
    for n, gw in zip(early, gathered):
        full[n] = gw
    parts = [_unpack(gathered[-1][q], mix_shapes) for q in range(N_SHARD)]
    for n in MIX_BIG:
        full[n] = jnp.concatenate([parts[q][n] for q in range(N_SHARD)], axis=SHARD_AXIS[n])

    core = lax.axis_index("c").astype(jnp.int32).reshape(1)
    chip = (2 * lax.axis_index("x") + lax.axis_index("y")).astype(jnp.int32).reshape(1)
    late = ([_to_wire("wire_" + n, held(args[n], n)) for n in FFN2_BIG], core, chip)
    lsum, grad_x, g = _local_step(x[0], positions, loss_target[0], full, late)
    loss = lax.psum(0.5 * jnp.sum(lsum) / x.shape[-1], ("x", "y", "c"))

    halves = [g[n] for n in FFN_BIG] + [g["mix_pack"]]
    others = _share_halves(halves)
    shared = [_join_halves("join_halves_%d" % i, hm, ho, core) for i, (hm, ho) in enumerate(zip(halves, others))]
    gsh = _unpack(shared[-1], mix_shapes)
    for n, sg_ in zip(FFN_BIG, shared):
        gsh[n] = sg_

    small_shapes = [(n, wsh[n].shape) for n in SMALL]
    pack_small = lambda d: jnp.concatenate(
        [_pad_lanes(d[n].astype(F32), LANES) for n in SMALL] + [jnp.zeros((SMALL_ROWS - len(SMALL), LANES), F32)], axis=0)
    slots = _gather_small(pack_small(g))

    c1 = 1.0 - ADAM_B1 ** ADAM_STEP
    c2 = 1.0 - ADAM_B2 ** ADAM_STEP

    def small_update(wb, mb, vb, s8):
        gs = s8[0:SMALL_ROWS]
        for d in range(1, 8):
            gs = gs + s8[d * SMALL_ROWS:(d + 1) * SMALL_ROWS]
        m2 = ADAM_B1 * mb + (1.0 - ADAM_B1) * gs
        v2 = ADAM_B2 * vb + (1.0 - ADAM_B2) * (gs * gs)
        delta = -ADAM_LR * ((m2 / c1) / (jnp.sqrt(v2 / c2) + ADAM_EPS) + ADAM_WD * wb)
        return gs, delta, m2, v2

    sg, sd, sm, sv_ = _rowwise("adamw_small", small_update,
                               [pack_small(wsh), pack_small(msh), pack_small(vsh)],
                               [slots.reshape(8 * SMALL_ROWS, LANES)], [(LANES, F32)] * 4, [], SMALL_ROWS)
    grads, deltas, new_m, new_v = {}, {}, {}, {}
    for i, (n, shp) in enumerate(small_shapes):
        grads[n], deltas[n] = sg[i:i + 1, :shp[1]], sd[i:i + 1, :shp[1]]
        new_m[n], new_v[n] = sm[i:i + 1, :shp[1]], sv_[i:i + 1, :shp[1]]
    for n in BIG:
        w3 = held(args[n], n)
        outs = _adamw("adamw_" + n, w3, gsh[n], held(args["m_" + n], n), held(args["v_" + n], n),
                      _row_tile(w3.shape[1], 256))
        grads[n], deltas[n], new_m[n], new_v[n] = [held(o, n) for o in outs]

    return (loss, grad_x[None], *[grads[n] for n in WEIGHTS], *[deltas[n] for n in WEIGHTS],
            *[new_m[n] for n in WEIGHTS], *[new_v[n] for n in WEIGHTS])
```

```python
import functools

import jax
import jax.numpy as jnp
from jax import lax
from jax.experimental import pallas as pl
from jax.experimental.pallas import tpu as pltpu

F32 = jnp.float32
BF16 = jnp.bfloat16
MM_DTYPE = BF16
MESH = pl.DeviceIdType.MESH

D_MODEL = 1024
D_FF = 2816
N_HEADS = 8
MLA_Q_RANK = 256
MLA_KV_RANK = 128
MLA_NOPE = 64
MLA_ROPE = 32
MLA_V = 64
ROPE_THETA = 10000.0
GDN_DH = 64
GDN_W = N_HEADS * GDN_DH
GDN_CONV = 4
CHUNK = 64
HEAD_LANES = 128
HEADS_PER_STEP = 8
MLA_PAD = N_HEADS * HEAD_LANES
EPS = 1e-6
N_SHARD = 4
LANES = 1024

PIN_QKV = 0
PIN_MLA = 1536
PIN_KPE = 1920
PIN_GATE = 2048
PIN_AB = 2560
PIN_W = 2688
CAT_W = MLA_PAD + GDN_W

ADAM_LR = 0.001
ADAM_B1 = 0.9
ADAM_B2 = 0.999
ADAM_EPS = 1e-08
ADAM_WD = 0.01
ADAM_STEP = 10

VMEM_LIMIT_V7X = 56 * 1024 * 1024

BIG = ["ffn1_w_gate", "ffn1_w_up", "ffn1_w_down", "w_in", "mla_w_uq", "mla_w_ukv", "gdn_conv_w", "w_out",
       "ffn2_w_gate", "ffn2_w_up", "ffn2_w_down"]
FFN_BIG = ["ffn1_w_gate", "ffn1_w_up", "ffn1_w_down", "ffn2_w_gate", "ffn2_w_up", "ffn2_w_down"]
TRANSPOSED = ["ffn1_w_gate", "ffn1_w_up", "ffn2_w_gate", "ffn2_w_up"]
MIX_BIG = ["w_in", "mla_w_uq", "mla_w_ukv", "gdn_conv_w", "w_out"]
SMALL = ["ffn1_pre_g", "ffn1_post_g", "mix_pre_g", "mla_q_norm_g", "mla_kv_norm_g", "mla_out_g", "gdn_a_log",
         "gdn_dt_bias", "gdn_norm_g", "mix_post_g", "ffn2_pre_g", "ffn2_post_g"]
WEIGHTS = ["ffn1_pre_g", "ffn1_w_gate", "ffn1_w_up", "ffn1_w_down", "ffn1_post_g", "mix_pre_g", "w_in",
           "mla_q_norm_g", "mla_w_uq", "mla_kv_norm_g", "mla_w_ukv", "mla_out_g", "gdn_conv_w", "gdn_a_log",
           "gdn_dt_bias", "gdn_norm_g", "w_out", "mix_post_g", "ffn2_pre_g", "ffn2_w_gate", "ffn2_w_up",
           "ffn2_w_down", "ffn2_post_g"]
SHARD_AXIS = {"ffn1_w_gate": 1, "ffn1_w_up": 1, "ffn1_w_down": 0, "w_in": 1, "mla_w_uq": 1, "mla_w_ukv": 1,
              "gdn_conv_w": 1, "w_out": 0, "ffn2_w_gate": 1, "ffn2_w_up": 1, "ffn2_w_down": 0}
SMALL_ROWS = 16


def _params(sem):
    return pltpu.CompilerParams(dimension_semantics=sem, vmem_limit_bytes=VMEM_LIMIT_V7X)


def _pick(dim, pref):
    if dim <= pref:
        return dim
    t = (pref // 128) * 128
    while t >= 128:
        if dim % t == 0:
            return t
        t -= 128
    return dim


ANY_SPEC = pl.BlockSpec(memory_space=pl.ANY)


def _rowwise(name, fn, row_ins, bc_ins, row_outs, acc_outs, tb, wide=None, carry=None):
    ents = []
    for e in row_ins:
        ents.append(e if isinstance(e, tuple) else (e, e.shape[1], 0, 0))
    over = [o[2] for o in row_outs if len(o) == 3]
    rows = over[0] if over else ents[0][0].shape[0]
    steps = rows // tb
    assert steps * tb == rows, (name, rows, tb)
    in_specs, args = [], []
    for a, w, j, r0 in ents:
        in_specs.append(pl.BlockSpec((tb, w), lambda i, j=j, r0=r0: (i + r0, j)))
        args.append(a)
    for b in bc_ins:
        in_specs.append(pl.BlockSpec(b.shape, lambda i: (0, 0)))
        args.append(b)
    n_in = len(args)
    aliases = {}
    if carry is not None:
        in_specs.append(ANY_SPEC)
        args.append(carry)
        aliases = {n_in: 0}
    out_shape = [jax.ShapeDtypeStruct((rows, o[0]), o[1]) for o in row_outs]
    out_specs = [pl.BlockSpec((tb, o[0]), lambda i: (i, 0)) for o in row_outs]
    if wide is not None:
        out_shape[0] = jax.ShapeDtypeStruct((rows, wide[0]), row_outs[0][1])
        out_specs[0] = pl.BlockSpec((tb, row_outs[0][0]), lambda i: (i, wide[1]))
    out_shape += [jax.ShapeDtypeStruct((r, c), F32) for r, c in acc_outs]
    out_specs += [pl.BlockSpec((r, c), lambda i: (0, 0)) for r, c in acc_outs]
    n_ro, n_acc, n_args = len(row_outs), len(acc_outs), len(args)

    def body(*refs):
        vals = fn(*[r[...] for r in refs[:n_in]])
        if not isinstance(vals, (tuple, list)):
            vals = (vals,)
        for r, v in zip(refs[n_args:n_args + n_ro], vals[:n_ro]):
            r[...] = v.astype(r.dtype)
        if n_acc:
            acc_refs = refs[n_args + n_ro:]

            @pl.when(pl.program_id(0) == 0)
            def _():
                for r in acc_refs:
                    r[...] = jnp.zeros(r.shape, r.dtype)

            for r, v in zip(acc_refs, vals[n_ro:]):
                r[...] += v

    outs = pl.pallas_call(body, name=name, grid=(steps,), in_specs=in_specs, out_specs=out_specs,
                          out_shape=out_shape, input_output_aliases=aliases,
                          compiler_params=_params(("arbitrary",)))(*args)
    return outs


def _mm(name, a, b, mode, out_dtype, tm=1024, tn=1024, tk=1024):
    if mode == "nn":
        (m, k), (k2, n) = a.shape, b.shape
    elif mode == "nt":
        (m, k), (n, k2) = a.shape, b.shape
    else:
        (k, m), (k2, n) = a.shape, b.shape
    assert k == k2, (name, a.shape, b.shape)
    tm, tn, tk = _pick(m, tm), _pick(n, tn), _pick(k, tk)
    nk = k // tk
    if mode == "nn":
        a_spec = pl.BlockSpec((tm, tk), lambda i, j, kk: (i, kk))
        b_spec = pl.BlockSpec((tk, tn), lambda i, j, kk: (kk, j))
        dims = (((1,), (0,)), ((), ()))
    elif mode == "nt":
        a_spec = pl.BlockSpec((tm, tk), lambda i, j, kk: (i, kk))
        b_spec = pl.BlockSpec((tn, tk), lambda i, j, kk: (j, kk))
        dims = (((1,), (1,)), ((), ()))
    else:
        a_spec = pl.BlockSpec((tk, tm), lambda i, j, kk: (kk, i))
        b_spec = pl.BlockSpec((tk, tn), lambda i, j, kk: (kk, j))
        dims = (((0,), (0,)), ((), ()))

    def body(a_ref, b_ref, o_ref, acc_ref):
        kk = pl.program_id(2)

        @pl.when(kk == 0)
        def _():
            acc_ref[...] = jnp.zeros(acc_ref.shape, F32)

        acc_ref[...] += lax.dot_general(a_ref[...].astype(MM_DTYPE), b_ref[...].astype(MM_DTYPE), dims,
                                        preferred_element_type=F32)

        @pl.when(kk == nk - 1)
        def _():
            o_ref[...] = acc_ref[...].astype(o_ref.dtype)

    return pl.pallas_call(
        body, name=name, grid=(m // tm, n // tn, nk), in_specs=[a_spec, b_spec],
        out_specs=pl.BlockSpec((tm, tn), lambda i, j, kk: (i, j)),
        out_shape=jax.ShapeDtypeStruct((m, n), out_dtype),
        scratch_shapes=[pltpu.VMEM((tm, tn), F32)],
        compiler_params=_params(("parallel", "parallel", "arbitrary")))(a, b)


def _rms_stats(x, n_real=None):
    n = x.shape[-1] if n_real is None else n_real
    return lax.rsqrt(jnp.sum(x * x, axis=-1, keepdims=True) / n + EPS)


def _rms_bwd(x, r, g, dz, n_real=None):
    n = x.shape[-1] if n_real is None else n_real
    xh = x * r
    dxh = dz * g
    dx = r * (dxh - xh * (jnp.sum(dxh * xh, axis=-1, keepdims=True) / n))
    return dx, jnp.sum(dz * xh, axis=0, keepdims=True)


def _sigmoid(x):
    return 0.5 * jnp.tanh(0.5 * x) + 0.5


def _roll(x, s, axis):
    return pltpu.roll(x, s, axis)


def _rope(x, c, s1, s2):
    return x * c + _roll(x, HEAD_LANES - MLA_ROPE // 2, 1) * s1 + _roll(x, MLA_ROPE // 2, 1) * s2


def _heads_apply(x, fn):
    return jnp.concatenate([fn(x[:, h * HEAD_LANES:(h + 1) * HEAD_LANES]) for h in range(N_HEADS)], axis=1)


ROW_CHUNK = 256


def _row_chunks(rows):
    step = min(ROW_CHUNK, rows)
    return [pl.ds(r, step) for r in range(0, rows, step)]


def _ffn_fwd(tag, x, g_pre, wg, wu, wd, g_post, tm):
    t, d = x.shape
    ns, fs, _ = wg.shape
    nt = t // tm
    row = pl.BlockSpec((tm, d), lambda i, q: (i, 0))
    vec = pl.BlockSpec((1, d), lambda i, q: (0, 0))
    act3 = pl.BlockSpec((1, tm, fs), lambda i, q: (q, i, 0))
    wrow = pl.BlockSpec((1, fs, d), lambda i, q: (q, 0, 0))
    nt_dims = (((1,), (1,)), ((), ()))

    def gate_up(x_ref, g_ref, wg_ref, wu_ref, n_ref, a_ref, u_ref, s_ref, n_s):
        @pl.when(pl.program_id(1) == 0)
        def _():
            for r in _row_chunks(tm):
                xb = x_ref[r, :]
                n_s[r, :] = (xb * _rms_stats(xb) * g_ref[...]).astype(MM_DTYPE)
            n_ref[...] = n_s[...]

        for r in _row_chunks(tm):
            n = n_s[r, :]
            a = lax.dot_general(n, wg_ref[0], nt_dims, preferred_element_type=F32)
            u = lax.dot_general(n, wu_ref[0], nt_dims, preferred_element_type=F32)
            a_ref[0, r, :] = a.astype(a_ref.dtype)
            u_ref[0, r, :] = u.astype(u_ref.dtype)
            s_ref[0, r, :] = ((a * _sigmoid(a)) * u).astype(s_ref.dtype)

    n, a, u, s = pl.pallas_call(
        gate_up, name=tag + "_gate_up", grid=(nt, ns), in_specs=[row, vec, wrow, wrow],
        out_specs=[row, act3, act3, act3],
        out_shape=[jax.ShapeDtypeStruct((t, d), MM_DTYPE)] + [jax.ShapeDtypeStruct((ns, t, fs), MM_DTYPE)] * 3,
        scratch_shapes=[pltpu.VMEM((tm, d), MM_DTYPE)],
        compiler_params=_params(("parallel", "arbitrary")))(x, g_pre, wg, wu)

    def down(s_ref, wd_ref, x_ref, g_ref, h_ref, y_ref, acc):
        q = pl.program_id(1)

        @pl.when(q == 0)
        def _():
            acc[...] = jnp.zeros(acc.shape, F32)

        for r in _row_chunks(tm):
            acc[r, :] += jnp.dot(s_ref[0, r, :], wd_ref[0], preferred_element_type=F32)

        @pl.when(q == ns - 1)
        def _():
            for r in _row_chunks(tm):
                hb = acc[r, :]
                h_ref[r, :] = hb
                y_ref[r, :] = x_ref[r, :] + 0.5 * (hb * _rms_stats(hb) * g_ref[...])

    h, y = pl.pallas_call(
        down, name=tag + "_down", grid=(nt, ns), in_specs=[act3, wrow, row, vec], out_specs=[row, row],
        out_shape=[jax.ShapeDtypeStruct((t, d), F32)] * 2, scratch_shapes=[pltpu.VMEM((tm, d), F32)],
        compiler_params=_params(("parallel", "arbitrary")))(s, wd, x, g_post)
    return y, (x, n, a, u, s, h)


def _carry(body, n_in, n_out, grid, carried):
    if carried is None:
        return body, [], [], [], [], []
    nx_in, nx_out = len(carried.ins), len(carried.outs)

    def wrapped(*refs):
        ins, rest = refs[:n_in], refs[n_in:]
        xi, rest = rest[:nx_in], rest[nx_in:]
        outs, rest = rest[:n_out], rest[n_out:]
        xo, rest = rest[:nx_out], rest[nx_out:]
        scr, sems = rest[:len(rest) - 2], rest[len(rest) - 2:]
        first, last = True, True
        for dim, size in enumerate(grid):
            first = first & (pl.program_id(dim) == 0)
            last = last & (pl.program_id(dim) == size - 1)

        @pl.when(first)
        def _():
            carried.start(xi, xo, *sems)

        body(*ins, *outs, *scr)

        @pl.when(last)
        def _():
            carried.finish(xi, xo, *sems)

    sems = [pltpu.SemaphoreType.DMA((carried.n_sem,)), pltpu.SemaphoreType.DMA((carried.n_sem,))]
    return (wrapped, [HBM_SPEC] * nx_in, [HBM_SPEC] * nx_out, list(carried.outs), sems, list(carried.ins))


def _ffn_bwd(tag, dy, saved, g_pre, wg, wu, wd, g_post, tm, tk, carried_down=None, make_carried_up=None):
    x, n, a, u, s, h = saved
    t, d = x.shape
    ns, fs, _ = wg.shape
    nt, nk = t // tm, t // tk
    row = pl.BlockSpec((tm, d), lambda i, q: (i, 0))
    vec = pl.BlockSpec((1, d), lambda i, q: (0, 0))
    act3 = pl.BlockSpec((1, tm, fs), lambda i, q: (q, i, 0))
    wrow = pl.BlockSpec((1, fs, d), lambda i, q: (q, 0, 0))
    nt_dims = (((1,), (1,)), ((), ()))
    tn_dims = (((0,), (0,)), ((), ()))

    def down_b(h_ref, dy_ref, g_ref, wd_ref, a_ref, u_ref, dh_ref, da_ref, du_ref, dg_ref, dh_s):
        i, q = pl.program_id(0), pl.program_id(1)

        @pl.when((i == 0) & (q == 0))
        def _():
            dg_ref[...] = jnp.zeros(dg_ref.shape, F32)

        @pl.when(q == 0)
        def _():
            for r in _row_chunks(tm):
                hb = h_ref[r, :]
                dh, dg = _rms_bwd(hb, _rms_stats(hb), g_ref[...], 0.5 * dy_ref[r, :])
                dh_s[r, :] = dh.astype(MM_DTYPE)
                dg_ref[...] += dg
            dh_ref[...] = dh_s[...]

        for r in _row_chunks(tm):
            ds = lax.dot_general(dh_s[r, :], wd_ref[0], nt_dims, preferred_element_type=F32)
            ab, ub = a_ref[0, r, :].astype(F32), u_ref[0, r, :].astype(F32)
            sg = _sigmoid(ab)
            da_ref[0, r, :] = (ds * ub * (sg * (1.0 + ab * (1.0 - sg)))).astype(da_ref.dtype)
            du_ref[0, r, :] = (ds * (ab * sg)).astype(du_ref.dtype)

    down_b, x_in, x_out, x_shape, x_scr, x_args = _carry(down_b, 6, 4, (nt, ns), carried_down)
    dh, da, du, dg_post, *from_down = pl.pallas_call(
        down_b, name=tag + "_down_b", grid=(nt, ns), in_specs=[row, row, vec, wrow, act3, act3] + x_in,
        out_specs=[row, act3, act3, vec] + x_out,
        out_shape=[jax.ShapeDtypeStruct((t, d), MM_DTYPE)] + [jax.ShapeDtypeStruct((ns, t, fs), MM_DTYPE)] * 2
        + [jax.ShapeDtypeStruct((1, d), F32)] + x_shape,
        scratch_shapes=[pltpu.VMEM((tm, d), MM_DTYPE)] + x_scr,
        compiler_params=_params(("arbitrary", "arbitrary")))(h, dy, g_post, wd, a, u, *x_args)

    def down_w(s_ref, dh_ref, dw_ref, acc):
        kk = pl.program_id(1)

        @pl.when(kk == 0)
        def _():
            acc[...] = jnp.zeros(acc.shape, F32)

        acc[...] += lax.dot_general(s_ref[0], dh_ref[...], tn_dims, preferred_element_type=F32)

        @pl.when(kk == nk - 1)
        def _():
            dw_ref[0] = acc[...].astype(dw_ref.dtype)

    dwd = pl.pallas_call(
        down_w, name=tag + "_down_w", grid=(ns, nk),
        in_specs=[pl.BlockSpec((1, tk, fs), lambda q, kk: (q, kk, 0)), pl.BlockSpec((tk, d), lambda q, kk: (kk, 0))],
        out_specs=pl.BlockSpec((1, fs, d), lambda q, kk: (q, 0, 0)),
        out_shape=jax.ShapeDtypeStruct((ns, fs, d), MM_DTYPE), scratch_shapes=[pltpu.VMEM((fs, d), F32)],
        compiler_params=_params(("parallel", "arbitrary")))(s, dh)

    def gate_up_b(da_ref, du_ref, wg_ref, wu_ref, x_ref, dy_ref, g_ref, dx_ref, dg_ref, acc):
        i, q = pl.program_id(0), pl.program_id(1)

        @pl.when((i == 0) & (q == 0))
        def _():
            dg_ref[...] = jnp.zeros(dg_ref.shape, F32)

        @pl.when(q == 0)
        def _():
            acc[...] = jnp.zeros(acc.shape, F32)

        for r in _row_chunks(tm):
            acc[r, :] += (jnp.dot(da_ref[0, r, :], wg_ref[0], preferred_element_type=F32)
                          + jnp.dot(du_ref[0, r, :], wu_ref[0], preferred_element_type=F32))

        @pl.when(q == ns - 1)
        def _():
            for r in _row_chunks(tm):
                xb = x_ref[r, :]
                dx, dg = _rms_bwd(xb, _rms_stats(xb), g_ref[...], acc[r, :])
                dx_ref[r, :] = dy_ref[r, :] + dx
                dg_ref[...] += dg

    def gate_up_w(n_ref, da_ref, du_ref, dwg_ref, dwu_ref, acc_g, acc_u):
        kk = pl.program_id(1)

        @pl.when(kk == 0)
        def _():
            acc_g[...] = jnp.zeros(acc_g.shape, F32)
            acc_u[...] = jnp.zeros(acc_u.shape, F32)

        nb = n_ref[...]
        acc_g[...] += lax.dot_general(da_ref[0], nb, tn_dims, preferred_element_type=F32)
        acc_u[...] += lax.dot_general(du_ref[0], nb, tn_dims, preferred_element_type=F32)

        @pl.when(kk == nk - 1)
        def _():
            dwg_ref[0] = acc_g[...].astype(dwg_ref.dtype)
            dwu_ref[0] = acc_u[...].astype(dwu_ref.dtype)

    k3 = pl.BlockSpec((1, tk, fs), lambda q, kk: (q, kk, 0))
    wout = pl.BlockSpec((1, fs, d), lambda q, kk: (q, 0, 0))
    dwg, dwu = pl.pallas_call(
        gate_up_w, name=tag + "_gate_up_w", grid=(ns, nk),
        in_specs=[pl.BlockSpec((tk, d), lambda q, kk: (kk, 0)), k3, k3], out_specs=[wout, wout],
        out_shape=[jax.ShapeDtypeStruct((ns, fs, d), MM_DTYPE)] * 2,
        scratch_shapes=[pltpu.VMEM((fs, d), F32)] * 2,
        compiler_params=_params(("parallel", "arbitrary")))(n, da, du)

    carried_up = make_carried_up(dwg, dwu, dwd) if make_carried_up else None
    gate_up_b, x_in, x_out, x_shape, x_scr, x_args = _carry(gate_up_b, 7, 2, (nt, ns), carried_up)
    dx, dg_pre, *from_up = pl.pallas_call(
        gate_up_b, name=tag + "_gate_up_b", grid=(nt, ns), in_specs=[act3, act3, wrow, wrow, row, row, vec] + x_in,
        out_specs=[row, vec] + x_out,
        out_shape=[jax.ShapeDtypeStruct((t, d), F32), jax.ShapeDtypeStruct((1, d), F32)] + x_shape,
        scratch_shapes=[pltpu.VMEM((tm, d), F32)] + x_scr,
        compiler_params=_params(("arbitrary", "arbitrary")))(da, du, wg, wu, x, dy, g_pre, *x_args)
    return dx, dg_pre, dwg, dwu, dwd, dg_post, from_down, from_up


NEG = -1e30


def _attn_scale():
    return (MLA_NOPE + MLA_ROPE) ** -0.5


def _causal_pairs(nq, by_key):
    if by_key:
        pairs = [(qi, ki) for ki in range(nq) for qi in range(ki, nq)]
    else:
        pairs = [(qi, ki) for qi in range(nq) for ki in range(qi + 1)]
    return jnp.asarray([p[0] for p in pairs], jnp.int32), jnp.asarray([p[1] for p in pairs], jnp.int32)


def _below_diagonal(shape):
    return lax.broadcasted_iota(jnp.int32, shape, 1) <= lax.broadcasted_iota(jnp.int32, shape, 0)


def _attn_call(name, body, tables, args, in_kinds, out_kinds, scratch, t, tq, carried=None):
    qmap = lambda h, p, qt, kt: (qt[p], h)
    kmap = lambda h, p, qt, kt: (kt[p], h)
    width = HEADS_PER_STEP * HEAD_LANES
    spec = lambda kind: pl.BlockSpec((tq, width), qmap if kind == "q" else kmap)
    n_pairs = tables[0].shape[0]
    n_groups = N_HEADS // HEADS_PER_STEP
    n_in, n_out, n_scr = len(in_kinds), len(out_kinds), scratch
    x_ins = list(carried.ins) if carried else []
    x_outs = list(carried.outs) if carried else []
    x_scr = [pltpu.SemaphoreType.DMA((carried.n_sem,)), pltpu.SemaphoreType.DMA((carried.n_sem,))] if carried else []

    def full_body(qt, kt, *refs):
        ins, refs = refs[:n_in], refs[n_in:]
        xi, refs = refs[:len(x_ins)], refs[len(x_ins):]
        outs, refs = refs[:n_out], refs[n_out:]
        xo, refs = refs[:len(x_outs)], refs[len(x_outs):]
        scr, sems = refs[:n_scr], refs[n_scr:]
        if carried:
            @pl.when((pl.program_id(0) == 0) & (pl.program_id(1) == 0))
            def _():
                carried.start(xi, xo, *sems)

        heads = [tuple(r.at[:, pl.ds(hh * HEAD_LANES, HEAD_LANES)] for r in (*ins, *outs, *scr))
                 for hh in range(HEADS_PER_STEP)]
        body(qt, kt, heads)
        if carried:
            @pl.when((pl.program_id(0) == n_groups - 1) & (pl.program_id(1) == n_pairs - 1))
            def _():
                carried.finish(xi, xo, *sems)

    grid_spec = pltpu.PrefetchScalarGridSpec(
        num_scalar_prefetch=2, grid=(n_groups, n_pairs),
        in_specs=[spec(kd) for kd in in_kinds] + [HBM_SPEC] * len(x_ins),
        out_specs=[spec(kd) for kd in out_kinds] + [HBM_SPEC] * len(x_outs),
        scratch_shapes=[pltpu.VMEM((tq, width), F32)] * n_scr + x_scr)
    return pl.pallas_call(full_body, name=name, grid_spec=grid_spec,
                          out_shape=[jax.ShapeDtypeStruct((t, MLA_PAD), F32) for _ in out_kinds] + x_outs,
                          compiler_params=_params(("arbitrary", "arbitrary")))(*tables, *args, *x_ins)


class _Carried:
    def __init__(self, ins, outs, n_sem, start, finish):
        self.ins, self.outs, self.n_sem, self.start, self.finish = ins, outs, n_sem, start, finish


def _attn_fwd(q, k, v, tq, carried=None):
    t = q.shape[0]
    nq = t // tq

    def body(qt, kt, heads):
        p_id = pl.program_id(1)
        qi, ki = qt[p_id], kt[p_id]

        @pl.when(ki == 0)
        def _():
            for _, _, _, _, _, m_s, l_s, acc_s in heads:
                m_s[...] = jnp.full(m_s.shape, NEG, F32)
                l_s[...] = jnp.zeros(l_s.shape, F32)
                acc_s[...] = jnp.zeros(acc_s.shape, F32)

        def update(diagonal):
            for q_ref, k_ref, v_ref, _, _, m_s, l_s, acc_s in heads:
                s = lax.dot_general(q_ref[...], k_ref[...], (((1,), (1,)), ((), ())), preferred_element_type=F32)
                if diagonal:
                    s = jnp.where(_below_diagonal(s.shape), s, NEG)
                m_old = m_s[...]
                m_new = jnp.maximum(m_old, jnp.max(s, axis=1, keepdims=True))
                alpha = jnp.exp(m_old - m_new)
                p = jnp.exp(s - m_new[:, :1])
                l_s[...] = l_s[...] * alpha + jnp.sum(p, axis=1, keepdims=True)
                acc_s[...] = acc_s[...] * alpha + jnp.dot(p.astype(MM_DTYPE), v_ref[...], preferred_element_type=F32)
                m_s[...] = m_new

        @pl.when(ki < qi)
        def _():
            update(False)

        @pl.when(ki == qi)
        def _():
            update(True)
            for _, _, _, o_ref, lse_ref, m_s, l_s, acc_s in heads:
                o_ref[...] = acc_s[...] / l_s[...]
                lse_ref[...] = m_s[...] + jnp.log(l_s[...])

    return _attn_call("mla_attn_fwd", body, _causal_pairs(nq, False), (q, k, v), "qkk", "qq", 3, t, tq, carried)


def _attn_probs(q, k, lse, diagonal):
    s = lax.dot_general(q, k, (((1,), (1,)), ((), ())), preferred_element_type=F32)
    p = jnp.exp(s - lse[:, :1])
    return jnp.where(_below_diagonal(s.shape), p, 0.0) if diagonal else p


def _attn_bwd_dq(q, k, v, do, lse, delta, tq):
    t = q.shape[0]
    nq = t // tq

    def body(qt, kt, heads):
        p_id = pl.program_id(1)
        qi, ki = qt[p_id], kt[p_id]

        @pl.when(ki == 0)
        def _():
            for refs in heads:
                refs[-1][...] = jnp.zeros(refs[-1].shape, F32)

        def step(diagonal):
            for q_ref, k_ref, v_ref, do_ref, lse_ref, dl_ref, _, acc_s in heads:
                p = _attn_probs(q_ref[...], k_ref[...], lse_ref[...], diagonal)
                dp = lax.dot_general(do_ref[...], v_ref[...], (((1,), (1,)), ((), ())), preferred_element_type=F32)
                ds = p * (dp - dl_ref[...][:, :1])
                acc_s[...] += jnp.dot(ds.astype(MM_DTYPE), k_ref[...], preferred_element_type=F32)

        @pl.when(ki < qi)
        def _():
            step(False)

        @pl.when(ki == qi)
        def _():
            step(True)
            for refs in heads:
                refs[-2][...] = refs[-1][...]

    return _attn_call("mla_attn_bwd_dq", body, _causal_pairs(nq, False), (q, k, v, do, lse, delta), "qkkqqq", "q",
                      1, t, tq)[0]


def _attn_bwd_dkv(q, k, v, do, lse, delta, tq, carried=None):
    t = q.shape[0]
    nq = t // tq

    def body(qt, kt, heads):
        p_id = pl.program_id(1)
        qi, ki = qt[p_id], kt[p_id]

        def step(diagonal):
            for q_ref, k_ref, v_ref, do_ref, lse_ref, dl_ref, _, _, dk_s, dv_s in heads:
                p = _attn_probs(q_ref[...], k_ref[...], lse_ref[...], diagonal)
                dv_s[...] += lax.dot_general(p.astype(MM_DTYPE), do_ref[...], (((0,), (0,)), ((), ())),
                                             preferred_element_type=F32)
                dp = lax.dot_general(do_ref[...], v_ref[...], (((1,), (1,)), ((), ())), preferred_element_type=F32)
                ds = p * (dp - dl_ref[...][:, :1])
                dk_s[...] += lax.dot_general(ds.astype(MM_DTYPE), q_ref[...], (((0,), (0,)), ((), ())),
                                             preferred_element_type=F32)

        @pl.when(qi == ki)
        def _():
            for refs in heads:
                refs[-2][...] = jnp.zeros(refs[-2].shape, F32)
                refs[-1][...] = jnp.zeros(refs[-1].shape, F32)
            step(True)

        @pl.when(qi > ki)
        def _():
            step(False)

        @pl.when(qi == nq - 1)
        def _():
            for refs in heads:
                refs[-4][...] = refs[-2][...]
                refs[-3][...] = refs[-1][...]

    return _attn_call("mla_attn_bwd_dkv", body, _causal_pairs(nq, True), (q, k, v, do, lse, delta), "qkkqqq", "kk",
                      2, t, tq, carried)


def _dot01(a, b, dims=(((1,), (0,)), ((), ())), ones="rhs"):
    val, sel = (a, b) if ones == "rhs" else (b, a)
    head = val.astype(BF16)
    tail = (val - head.astype(F32)).astype(BF16)
    sel = sel.astype(BF16)
    dot = lambda part: (lax.dot_general(part, sel, dims, preferred_element_type=F32) if ones == "rhs"
                        else lax.dot_general(sel, part, dims, preferred_element_type=F32))
    return dot(head) + dot(tail)


def _dot1(a, b, dims=(((1,), (0,)), ((), ()))):
    return lax.dot_general(a.astype(MM_DTYPE), b.astype(MM_DTYPE), dims, preferred_element_type=F32)


def _dot3(a, b, dims=(((1,), (0,)), ((), ()))):
    return lax.dot_general(a, b, dims, preferred_element_type=F32, precision=lax.Precision.HIGH)


NN3 = (((2,), (1,)), ((0,), (0,)))
NT3 = (((2,), (2,)), ((0,), (0,)))
TN3 = (((1,), (1,)), ((0,), (0,)))


def _tri_masks(nh):
    shape = (nh, CHUNK, CHUNK)
    return lax.broadcasted_iota(jnp.int32, shape, 1), lax.broadcasted_iota(jnp.int32, shape, 2)


def _gdn_chunk_common(k, gcc, bb, row, col, dot=_dot1):
    tril = row >= col
    gcr = jnp.swapaxes(gcc, 1, 2)
    dm = jnp.exp(jnp.where(tril, gcc - gcr, NEG))
    kb = k * bb
    lm = jnp.where(row > col, dot(kb, k, NT3) * dm, 0.0)
    return dm, kb, lm


def _unit_lower_inverse(lm, eye):
    t = eye - lm
    p = lm
    for _ in range(CHUNK.bit_length() - 2):
        p = _dot3(p, p, NN3)
        t = t + _dot3(t, p, NN3)
    return t


def _chunk_sum_matrix(tb, upper):
    r = lax.broadcasted_iota(jnp.int32, (tb, tb), 0)
    c = lax.broadcasted_iota(jnp.int32, (tb, tb), 1)
    same = (r // CHUNK) == (c // CHUNK)
    return (same & ((c >= r) if upper else (c <= r))).astype(F32)


def _gdn_fwd(q, k, v, gb, bb):
    nh, t, dh = q.shape
    nchunk = t // CHUNK

    def body(q_ref, k_ref, v_ref, g_ref, b_ref, o_ref, sall_ref, tall_ref, s_s):
        @pl.when(pl.program_id(0) == 0)
        def _():
            s_s[...] = jnp.zeros(s_s.shape, F32)

        row, col = _tri_masks(nh)
        qh, kh, vh, bbh, gcc = q_ref[...], k_ref[...], v_ref[...], b_ref[...], g_ref[...]
        dm, kb, lm = _gdn_chunk_common(kh, gcc, bbh, row, col)
        eg = jnp.exp(gcc)
        glr = gcc[:, CHUNK - 1:CHUNK, :]
        th = _unit_lower_inverse(lm, (row == col).astype(F32))
        w = _dot1(th, kb * eg, NN3)
        u = _dot1(th, vh * bbh, NN3)
        at = jnp.where(row >= col, _dot1(qh, kh, NT3) * dm, 0.0)
        sh = s_s[...]
        vn = u - _dot1(w, sh, NN3)
        o_ref[...] = _dot1(qh * eg, sh, NN3) + _dot1(at, vn, NN3)
        kd = kh * jnp.exp(glr - gcc)
        sall_ref[:, 0] = sh
        tall_ref[...] = th
        s_s[...] = sh * jnp.exp(glr) + _dot1(kd, vn, TN3)

    blk = pl.BlockSpec((nh, CHUNK, dh), lambda n: (0, n, 0))
    return pl.pallas_call(
        body, name="gdn_fwd", grid=(nchunk,), in_specs=[blk] * 5,
        out_specs=[blk, pl.BlockSpec((nh, 1, dh, dh), lambda n: (0, n, 0, 0)), blk],
        out_shape=[jax.ShapeDtypeStruct((nh, t, dh), F32), jax.ShapeDtypeStruct((nh, nchunk, dh, dh), F32),
                   jax.ShapeDtypeStruct((nh, t, CHUNK), F32)],
        scratch_shapes=[pltpu.VMEM((nh, dh, dh), F32)],
        compiler_params=_params(("arbitrary",)))(q, k, v, gb, bb)


def _gdn_bwd(q, k, v, gb, bb, sall, tall, do):
    nh, t, dh = q.shape
    nchunk = t // CHUNK

    def body(q_ref, k_ref, v_ref, g_ref, b_ref, sall_ref, tall_ref, do_ref,
             dq_ref, dk_ref, dv_ref, dg_ref, db_ref, ds_s):
        @pl.when(pl.program_id(0) == 0)
        def _():
            ds_s[...] = jnp.zeros(ds_s.shape, F32)

        row, col = _tri_masks(nh)
        tril, stril = row >= col, row > col
        rsum = lambda x: jnp.sum(x, axis=2, keepdims=True)
        qh, kh, vh, gcc, bbh = q_ref[...], k_ref[...], v_ref[...], g_ref[...], b_ref[...]
        sh, th, doh, dsp = sall_ref[:, 0], tall_ref[...], do_ref[...], ds_s[...]
        dm, kb, lm = _gdn_chunk_common(kh, gcc, bbh, row, col, _dot3)
        eg = jnp.exp(gcc)
        glr = gcc[:, CHUNK - 1:CHUNK, :]
        glv = jnp.exp(glr)
        egl = jnp.exp(glr - gcc)
        rw, ru = kb * eg, vh * bbh
        w, u = _dot3(th, rw, NN3), _dot3(th, ru, NN3)
        at = jnp.where(tril, _dot3(qh, kh, NT3) * dm, 0.0)
        qd, kd = qh * eg, kh * egl
        vn = u - _dot3(w, sh, NN3)
        dgl = jnp.sum(rsum(dsp * sh), axis=1, keepdims=True)
        dkd = _dot3(vn, dsp, NT3)
        dvn = _dot3(kd, dsp, NN3)
        dqd = _dot3(doh, sh, NT3)
        dat = jnp.where(tril, _dot3(doh, vn, NT3), 0.0)
        dvn = dvn + _dot3(at, doh, TN3)
        dw = -_dot3(dvn, sh, NT3)
        ds_s[...] = dsp * glv + _dot3(qd, doh, TN3) - _dot3(w, dvn, TN3)
        dpa = dat * dm
        dq_ref[...] = _dot1(dpa, kh, NN3) + dqd * eg
        dk = _dot1(dpa, qh, TN3) + dkd * egl
        t6 = rsum(dkd * kd)
        dgam = rsum(dqd * qd) - t6
        dgam_last = jnp.sum(t6, axis=1, keepdims=True) + dgl * glv
        drw = _dot3(th, dw, TN3)
        dru = _dot3(th, dvn, TN3)
        dl = -jnp.where(stril, _dot3(drw, w, NT3) + _dot3(dru, u, NT3), 0.0)
        dgam = dgam + rsum(drw * rw)
        dv_ref[...] = dru * bbh
        dp2 = dl * dm
        dkb = drw * eg + _dot1(dp2, kh, NN3)
        dk_ref[...] = dk + _dot1(dp2, kb, TN3) + dkb * bbh
        db_ref[...] = rsum(dru * vh) + rsum(dkb * kh) + jnp.zeros((nh, CHUNK, dh), F32)
        e = dat * at + dl * lm
        dgam_b = dgam + rsum(e) - _dot01(e, jnp.ones((nh, CHUNK, CHUNK), F32), TN3)
        dgam_b = dgam_b + jnp.where(row == CHUNK - 1, dgam_last, 0.0)
        dg_ref[...] = dgam_b

    rev = lambda n: (0, nchunk - 1 - n, 0)
    blk = pl.BlockSpec((nh, CHUNK, dh), rev)
    sblk = pl.BlockSpec((nh, 1, dh, dh), lambda n: (0, nchunk - 1 - n, 0, 0))
    out = jax.ShapeDtypeStruct((nh, t, dh), F32)
    return pl.pallas_call(
        body, name="gdn_bwd", grid=(nchunk,), in_specs=[blk] * 5 + [sblk, blk, blk], out_specs=[blk] * 5,
        out_shape=[out] * 5, scratch_shapes=[pltpu.VMEM((nh, dh, dh), F32)],
        compiler_params=_params(("arbitrary",)))(q, k, v, gb, bb, sall, tall, do)


def _group_ones():
    r = lax.broadcasted_iota(jnp.int32, (GDN_W, GDN_W), 0) // GDN_DH
    c = lax.broadcasted_iota(jnp.int32, (GDN_W, GDN_W), 1) // GDN_DH
    return (r == c).astype(F32)


def _conv_taps(x, xprev, w, has_prev):
    row = lax.broadcasted_iota(jnp.int32, x.shape, 0)
    out = x * w[GDN_CONV - 1:GDN_CONV, :]
    for s in range(1, GDN_CONV):
        sh = jnp.where(row >= s, _roll(x, s, 0), _roll(xprev, s, 0) * has_prev)
        out = out + sh * w[GDN_CONV - 1 - s:GDN_CONV - s, :]
    return out


def _head_cols(x, h):
    return x[:, h * GDN_DH:(h + 1) * GDN_DH]


def _heads_spec(tb):
    return pl.BlockSpec((N_HEADS, tb, GDN_DH), lambda i: (0, i, 0))


def _mixer_fwd(x, positions, w, tb, carried=None):
    t, d = x.shape
    tables = _rope_tables(positions)

    def pre(xb, g):
        return (xb * _rms_stats(xb) * g,)

    (hn,) = _rowwise("mix_pre", pre, [x], [w["mix_pre_g"]], [(d, BF16)], [], tb)
    proj = _mm("mix_in", hn, w["w_in_pad"], "nn", F32)

    def mla_pre(p0, gq, gkv):
        cq, ckv = p0[:, :MLA_Q_RANK], p0[:, MLA_Q_RANK:MLA_Q_RANK + MLA_KV_RANK]
        return cq * _rms_stats(cq) * gq, ckv * _rms_stats(ckv) * gkv

    nq, nkv = _rowwise("mla_pre", mla_pre, [(proj, 512, PIN_MLA // 512, 0)],
                       [w["mla_q_norm_g"], w["mla_kv_norm_g"]], [(MLA_Q_RANK, BF16), (MLA_KV_RANK, BF16)], [], tb)
    qraw = _mm("mla_uq", nq, w["w_uq_pad"], "nn", F32)
    kv = _mm("mla_ukv", nkv, w["w_kv_pad"], "nn", F32)

    def rope_f(qr, kn, vv, kpe, c, s1, s2):
        qo = _heads_apply(qr, lambda xh: _rope(xh, c, s1, s2)) * _attn_scale()
        kp = _rope(kpe, c, s1, s2)
        return qo, kn + jnp.tile(kp, (1, N_HEADS)), vv

    q, k, v = _rowwise("mla_rope", rope_f,
                       [qraw, (kv, MLA_PAD, 0, 0), (kv, MLA_PAD, 1, 0), (proj, HEAD_LANES, PIN_KPE // HEAD_LANES, 0),
                        tables[0], tables[1], tables[2]], [],
                       [(MLA_PAD, BF16)] * 3, [], tb // 2)
    tq = min(512, t)
    o, lse, *carried_out = _attn_fwd(q, k, v, tq, carried)

    def mla_post(ob, g):
        return (ob * _rms_stats(ob, N_HEADS * MLA_V) * g,)

    (cat,) = _rowwise("mla_post", mla_post, [o], [w["mla_out_g_pad"]], [(MLA_PAD, BF16)], [], tb, wide=(CAT_W, 0))

    gones = _group_ones()
    steps = t // tb

    def gdn_pre(xq, xk, xv, pq, pk, pv, cw, go, has_prev):
        outs = []
        for j, (xc, xp) in enumerate(((xq, pq), (xk, pk), (xv, pv))):
            c = _conv_taps(xc, xp, cw[:, j * GDN_W:(j + 1) * GDN_W], has_prev)
            a = c * _sigmoid(c)
            if j < 2:
                rn = lax.rsqrt(_dot01(a * a, go) + EPS)
                a = a * rn
                if j == 0:
                    a = a * (GDN_DH ** -0.5)
            outs.append(a)
        return tuple(outs)

    qh, kh, vh = _gdn_pre_call("gdn_pre", gdn_pre, proj, w["conv_w"], gones, tb, steps)
    heads_shape = jax.ShapeDtypeStruct((N_HEADS, t, GDN_DH), F32)
    lanes_shape = jax.ShapeDtypeStruct((t, HEAD_LANES), F32)
    lanes_spec = pl.BlockSpec((tb, HEAD_LANES), lambda i: (i, 0))
    vec_spec = lambda n: pl.BlockSpec((1, n), lambda i: (0, 0))

    def gate_f(ab_ref, al_ref, dt_ref, g_ref, b_ref, gh_ref, bh_ref):
        g, b = _gb_fwd(ab_ref[...], al_ref[...], dt_ref[...])
        g_ref[...] = g
        b_ref[...] = b
        gc = _dot01(_chunk_sum_matrix(tb, False), g, ones="lhs")
        for h in range(N_HEADS):
            gh_ref[h] = jnp.broadcast_to(gc[:, h:h + 1], (tb, GDN_DH))
            bh_ref[h] = jnp.broadcast_to(b[:, N_HEADS + h:N_HEADS + h + 1], (tb, GDN_DH))

    g128, b128, gbh, bbh = pl.pallas_call(
        gate_f, name="gdn_gate_f", grid=(steps,),
        in_specs=[pl.BlockSpec((tb, HEAD_LANES), lambda i: (i, PIN_AB // HEAD_LANES)), vec_spec(HEAD_LANES),
                  vec_spec(HEAD_LANES)],
        out_specs=[lanes_spec, lanes_spec, _heads_spec(tb), _heads_spec(tb)],
        out_shape=[lanes_shape, lanes_shape, heads_shape, heads_shape],
        compiler_params=_params(("arbitrary",)))(proj, w["a_log_pad"], w["dt_bias_pad"])
    oh, sall, tall = _gdn_fwd(qh, kh, vh, gbh, bbh)

    def gdn_post(o_ref, gt_ref, g_ref, cat_in, cat_ref):
        gt, g = gt_ref[...], g_ref[...]
        outs = []
        for h in range(N_HEADS):
            ob, gth = o_ref[h], _head_cols(gt, h)
            outs.append(ob * _rms_stats(ob) * g * (gth * _sigmoid(gth)))
        cat_ref[...] = jnp.concatenate(outs, axis=1).astype(cat_ref.dtype)

    gate_spec = pl.BlockSpec((tb, GDN_W), lambda i: (i, PIN_GATE // GDN_W))
    cat = pl.pallas_call(
        gdn_post, name="gdn_post", grid=(steps,),
        in_specs=[_heads_spec(tb), gate_spec, vec_spec(GDN_DH), ANY_SPEC],
        out_specs=pl.BlockSpec((tb, GDN_W), lambda i: (i, MLA_PAD // GDN_W)),
        out_shape=jax.ShapeDtypeStruct((t, CAT_W), BF16), input_output_aliases={3: 0},
        compiler_params=_params(("arbitrary",)))(oh, proj, w["gdn_norm_g"], cat)
    mixed = _mm("mix_out", cat, w["w_out_pad"], "nn", F32)

    def post(xb, hb, g):
        return (xb + hb * _rms_stats(hb) * g,)

    (y,) = _rowwise("mix_post", post, [x, mixed], [w["mix_post_g"]], [(d, F32)], [], tb)
    saved = dict(x=x, hn=hn, proj=proj, nq=nq, nkv=nkv, q=q, k=k, v=v, o=o, lse=lse, qh=qh, kh=kh, vh=vh,
                 gbh=gbh, bbh=bbh, oh=oh, sall=sall, tall=tall, cat=cat, mixed=mixed,
                 tables=tables, g128=g128, b128=b128)
    return y, saved, carried_out


def _qkv_specs(tb):
    base = PIN_QKV // GDN_W
    cur = [pl.BlockSpec((tb, GDN_W), lambda i, j=j: (i, base + j)) for j in range(3)]
    prev = [pl.BlockSpec((tb, GDN_W), lambda i, j=j: (jnp.maximum(i - 1, 0), base + j)) for j in range(3)]
    return cur + prev


def _gdn_pre_call(name, fn, proj, conv_w, gones, tb, steps):
    t = proj.shape[0]

    def body(xq, xk, xv, pq, pk, pv, cw, go, oq, ok, ov):
        has_prev = jnp.where(pl.program_id(0) == 0, 0.0, 1.0)
        outs = fn(xq[...], xk[...], xv[...], pq[...], pk[...], pv[...], cw[...], go[...], has_prev)
        for r, val in zip((oq, ok, ov), outs):
            for h in range(N_HEADS):
                r[h] = _head_cols(val, h)

    return pl.pallas_call(
        body, name=name, grid=(steps,),
        in_specs=_qkv_specs(tb) + [pl.BlockSpec(conv_w.shape, lambda i: (0, 0)),
                                   pl.BlockSpec(gones.shape, lambda i: (0, 0))],
        out_specs=[_heads_spec(tb)] * 3,
        out_shape=[jax.ShapeDtypeStruct((N_HEADS, t, GDN_DH), F32)] * 3,
        compiler_params=_params(("arbitrary",)))(proj, proj, proj, proj, proj, proj, conv_w, gones)


def _softplus(x):
    return jnp.maximum(x, 0.0) + jnp.log1p(jnp.exp(-jnp.abs(x)))


def _gb_fwd(ab, a_log, dt_bias):
    g = -jnp.exp(a_log) * _softplus(ab + dt_bias)
    return g, _sigmoid(ab)


def _rope_tables(positions):
    half = MLA_ROPE // 2
    freqs = ROPE_THETA ** (-jnp.arange(half, dtype=F32) / half)
    ang = positions.reshape(-1).astype(F32)[:, None] * freqs
    cos, sin = jnp.cos(ang), jnp.sin(ang)
    t = ang.shape[0]
    one = jnp.ones((t, MLA_NOPE), F32)
    z16, z32, z64 = jnp.zeros((t, half), F32), jnp.zeros((t, MLA_ROPE), F32), jnp.zeros((t, MLA_NOPE), F32)
    c = jnp.concatenate([one, cos, cos, jnp.ones((t, MLA_ROPE), F32)], axis=1)
    s1 = jnp.concatenate([z64, -sin, z16, z32], axis=1)
    s2 = jnp.concatenate([z64, z16, sin, z32], axis=1)
    return c, s1, s2


def _mixer_bwd(dy, sv, w, tb, carried=None):
    x, proj = sv["x"], sv["proj"]
    t, d = x.shape
    c, s1, s2 = sv["tables"]
    grads = {}

    def post_b(hb, dyb, g):
        return _rms_bwd(hb, _rms_stats(hb), g, dyb)

    dmixed, grads["mix_post_g"] = _rowwise("mix_post_b", post_b, [sv["mixed"], dy], [w["mix_post_g"]],
                                           [(d, BF16)], [(1, d)], tb)
    dcat = _mm("mix_out_bx", dmixed, w["w_out_pad"], "nt", F32)
    grads["w_out_pad"] = _mm("mix_out_bw", sv["cat"], dmixed, "tn", F32)
    steps = t // tb
    vec_spec = lambda n: pl.BlockSpec((1, n), lambda i: (0, 0))

    def gdn_post_b(o_ref, gt_ref, do_ref, g_ref, dproj_ref, doh_ref, dg_ref):
        @pl.when(pl.program_id(0) == 0)
        def _():
            dg_ref[...] = jnp.zeros(dg_ref.shape, F32)

        gt, dob, g = gt_ref[...], do_ref[...], g_ref[...]
        dgates = []
        for h in range(N_HEADS):
            ob, gth, dobh = o_ref[h], _head_cols(gt, h), _head_cols(dob, h)
            sg = _sigmoid(gth)
            r = _rms_stats(ob)
            dxo, dg = _rms_bwd(ob, r, g, dobh * (gth * sg))
            doh_ref[h] = dxo
            dg_ref[...] += dg
            dgates.append(dobh * (ob * r * g) * (sg * (1.0 + gth * (1.0 - sg))))
        dproj_ref[...] = jnp.concatenate(dgates, axis=1).astype(dproj_ref.dtype)

    dproj, doh, grads["gdn_norm_g"] = pl.pallas_call(
        gdn_post_b, name="gdn_post_b", grid=(steps,),
        in_specs=[_heads_spec(tb), pl.BlockSpec((tb, GDN_W), lambda i: (i, PIN_GATE // GDN_W)),
                  pl.BlockSpec((tb, GDN_W), lambda i: (i, MLA_PAD // GDN_W)), vec_spec(GDN_DH)],
        out_specs=[pl.BlockSpec((tb, GDN_W), lambda i: (i, PIN_GATE // GDN_W)), _heads_spec(tb), vec_spec(GDN_DH)],
        out_shape=[jax.ShapeDtypeStruct((t, PIN_W), BF16), jax.ShapeDtypeStruct((N_HEADS, t, GDN_DH), F32),
                   jax.ShapeDtypeStruct((1, GDN_DH), F32)],
        compiler_params=_params(("arbitrary",)))(sv["oh"], proj, dcat, w["gdn_norm_g"])

    def mla_post_b(ob, dmo, g):
        do, dg = _rms_bwd(ob, _rms_stats(ob, N_HEADS * MLA_V), g, dmo, N_HEADS * MLA_V)
        prod = do * ob
        delta = _heads_apply(prod, lambda ph: jnp.sum(ph, axis=1, keepdims=True) + jnp.zeros_like(ph))
        return do, delta, dg

    do, delta, grads["mla_out_g_pad"] = _rowwise(
        "mla_post_b", mla_post_b, [sv["o"], (dcat, MLA_PAD, 0, 0)], [w["mla_out_g_pad"]],
        [(MLA_PAD, BF16), (MLA_PAD, F32)], [(1, MLA_PAD)], tb // 2)
    tq = min(512, t)
    dq = _attn_bwd_dq(sv["q"], sv["k"], sv["v"], do, sv["lse"], delta, tq)
    dk, dv, *carried_out = _attn_bwd_dkv(sv["q"], sv["k"], sv["v"], do, sv["lse"], delta, tq, carried)

    def rope_b(dqb, dkb, dvb, cc, a1, a2):
        dqr = _heads_apply(dqb * _attn_scale(), lambda xh: _rope(xh, cc, -a1, -a2))
        ksum = dkb[:, :HEAD_LANES]
        for h in range(1, N_HEADS):
            ksum = ksum + dkb[:, h * HEAD_LANES:(h + 1) * HEAD_LANES]
        lane = lax.broadcasted_iota(jnp.int32, ksum.shape, 1)
        keep = (lane >= MLA_NOPE) & (lane < MLA_NOPE + MLA_ROPE)
        dkpe = jnp.where(keep, _rope(ksum, cc, -a1, -a2), 0.0)
        return dqr, jnp.concatenate([dkb, dvb], axis=1), dkpe

    dqraw, dkv, dkpe = _rowwise("mla_rope_b", rope_b, [dq, dk, dv, c, s1, s2], [],
                                [(MLA_PAD, BF16), (2 * MLA_PAD, BF16), (HEAD_LANES, F32)], [], tb // 2)
    dnq = _mm("mla_uq_bx", dqraw, w["w_uq_pad"], "nt", F32)
    grads["w_uq_pad"] = _mm("mla_uq_bw", sv["nq"], dqraw, "tn", F32)
    dnkv = _mm("mla_ukv_bx", dkv, w["w_kv_pad"], "nt", F32)
    grads["w_kv_pad"] = _mm("mla_ukv_bw", sv["nkv"], dkv, "tn", F32)

    def mla_pre_b(p0, dnqb, dnkvb, dkpeb, gq, gkv):
        cq, ckv = p0[:, :MLA_Q_RANK], p0[:, MLA_Q_RANK:MLA_Q_RANK + MLA_KV_RANK]
        dcq, dgq = _rms_bwd(cq, _rms_stats(cq), gq, dnqb)
        dckv, dgkv = _rms_bwd(ckv, _rms_stats(ckv), gkv, dnkvb)
        return jnp.concatenate([dcq, dckv, dkpeb], axis=1), dgq, dgkv

    dproj, grads["mla_q_norm_g"], grads["mla_kv_norm_g"] = _rowwise(
        "mla_pre_b", mla_pre_b, [(proj, 512, PIN_MLA // 512, 0), dnq, dnkv, dkpe],
        [w["mla_q_norm_g"], w["mla_kv_norm_g"]], [(512, BF16)], [(1, MLA_Q_RANK), (1, MLA_KV_RANK)], tb,
        wide=(PIN_W, PIN_MLA // 512), carry=dproj)

    dqh, dkh, dvh, dgh, dbh = _gdn_bwd(sv["qh"], sv["kh"], sv["vh"], sv["gbh"], sv["bbh"], sv["sall"], sv["tall"], doh)
    gones = _group_ones()

    def gdn_pre_b(xq, xk, xv, pq, pk, pv, dq_, dk_, dv_, cw, go, has_prev):
        outs = []
        for j, (xc, xp, dd) in enumerate(((xq, pq, dq_), (xk, pk, dk_), (xv, pv, dv_))):
            cc = _conv_taps(xc, xp, cw[:, j * GDN_W:(j + 1) * GDN_W], has_prev)
            sg = _sigmoid(cc)
            a = cc * sg
            if j < 2:
                rn = lax.rsqrt(_dot01(a * a, go) + EPS)
                if j == 0:
                    dd = dd * (GDN_DH ** -0.5)
                da = rn * dd - a * (rn * rn * rn) * _dot01(dd * a, go)
            else:
                da = dd
            outs.append(da * (sg * (1.0 + cc * (1.0 - sg))))
        return tuple(outs)

    dcq, dck, dcv = _gdn_pre_b_call("gdn_pre_b", gdn_pre_b, proj, (dqh, dkh, dvh), w["conv_w"], gones, tb, steps)
    dproj, grads["conv_w"] = _conv_bwd_call("gdn_conv_b", proj, (dcq, dck, dcv), w["conv_w"], dproj, tb, steps)

    def gate_b(ab_ref, g_ref, b_ref, dgh_ref, dbh_ref, al_ref, dt_ref, carry_ref, dab_ref, dal_ref, ddt_ref):
        @pl.when(pl.program_id(0) == 0)
        def _():
            dal_ref[...] = jnp.zeros(dal_ref.shape, F32)
            ddt_ref[...] = jnp.zeros(ddt_ref.shape, F32)

        ab, g128, b128 = ab_ref[...], g_ref[...], b_ref[...]
        lane = lax.broadcasted_iota(jnp.int32, ab.shape, 1)
        dg_ = jnp.zeros(ab.shape, F32)
        db_ = jnp.zeros(ab.shape, F32)
        for h in range(N_HEADS):
            dg_ = dg_ + jnp.where(lane == h, jnp.broadcast_to(dgh_ref[h][:, 0:1], ab.shape), 0.0)
            db_ = db_ + jnp.where(lane == N_HEADS + h, jnp.broadcast_to(dbh_ref[h][:, 0:1], ab.shape), 0.0)
        dg_ = _dot01(_chunk_sum_matrix(tb, True), dg_, ones="lhs")
        slope = -jnp.exp(al_ref[...]) * _sigmoid(ab + dt_ref[...])
        dab_ref[...] = (dg_ * slope + db_ * b128 * (1.0 - b128)).astype(dab_ref.dtype)
        dal_ref[...] += jnp.sum(dg_ * g128, axis=0, keepdims=True)
        ddt_ref[...] += jnp.sum(dg_ * slope, axis=0, keepdims=True)

    lanes_spec = pl.BlockSpec((tb, HEAD_LANES), lambda i: (i, 0))
    ab_spec = pl.BlockSpec((tb, HEAD_LANES), lambda i: (i, PIN_AB // HEAD_LANES))
    dproj, grads["a_log_pad"], grads["dt_bias_pad"] = pl.pallas_call(
        gate_b, name="gdn_gate_b", grid=(steps,),
        in_specs=[ab_spec, lanes_spec, lanes_spec, _heads_spec(tb), _heads_spec(tb), vec_spec(HEAD_LANES),
                  vec_spec(HEAD_LANES), ANY_SPEC],
        out_specs=[ab_spec, vec_spec(HEAD_LANES), vec_spec(HEAD_LANES)],
        out_shape=[jax.ShapeDtypeStruct((t, PIN_W), BF16), jax.ShapeDtypeStruct((1, HEAD_LANES), F32),
                   jax.ShapeDtypeStruct((1, HEAD_LANES), F32)],
        input_output_aliases={7: 0},
        compiler_params=_params(("arbitrary",)))(proj, sv["g128"], sv["b128"], dgh, dbh, w["a_log_pad"],
                                                 w["dt_bias_pad"], dproj)
    dhn = _mm("mix_in_bx", dproj, w["w_in_pad"], "nt", F32)
    grads["w_in_pad"] = _mm("mix_in_bw", sv["hn"], dproj, "tn", F32)

    def pre_b(xb, dnb, dyb, g):
        dx, dg = _rms_bwd(xb, _rms_stats(xb), g, dnb)
        return dyb + dx, dg

    dx, grads["mix_pre_g"] = _rowwise("mix_pre_b", pre_b, [x, dhn, dy], [w["mix_pre_g"]], [(d, F32)], [(1, d)], tb)
    return dx, grads, carried_out


def _gdn_pre_b_call(name, fn, proj, dd, conv_w, gones, tb, steps):
    t = proj.shape[0]

    def body(xq, xk, xv, pq, pk, pv, d0, d1, d2, cw, go, oq, ok, ov):
        has_prev = jnp.where(pl.program_id(0) == 0, 0.0, 1.0)
        dd_rows = [jnp.concatenate([dr[h] for h in range(N_HEADS)], axis=1) for dr in (d0, d1, d2)]
        outs = fn(xq[...], xk[...], xv[...], pq[...], pk[...], pv[...], *dd_rows, cw[...], go[...], has_prev)
        for r, val in zip((oq, ok, ov), outs):
            r[...] = val

    return pl.pallas_call(
        body, name=name, grid=(steps,),
        in_specs=_qkv_specs(tb) + [_heads_spec(tb)] * 3 + [pl.BlockSpec(conv_w.shape, lambda i: (0, 0)),
                                                          pl.BlockSpec(gones.shape, lambda i: (0, 0))],
        out_specs=[pl.BlockSpec((tb, GDN_W), lambda i: (i, 0))] * 3,
        out_shape=[jax.ShapeDtypeStruct((t, GDN_W), F32)] * 3,
        compiler_params=_params(("arbitrary",)))(proj, proj, proj, proj, proj, proj, *dd, conv_w, gones)


def _conv_bwd_call(name, proj, dc, conv_w, dproj, tb, steps):
    t = proj.shape[0]
    dcur = [pl.BlockSpec((tb, GDN_W), lambda i: (i, 0))] * 3
    dnext = [pl.BlockSpec((tb, GDN_W), lambda i: (jnp.minimum(i + 1, steps - 1), 0))] * 3

    def body(xq, xk, xv, pq, pk, pv, d0, d1, d2, n0, n1, n2, cw, carry_ref, dx_ref, dw_ref):
        i = pl.program_id(0)
        has_prev = jnp.where(i == 0, 0.0, 1.0)
        has_next = jnp.where(i == steps - 1, 0.0, 1.0)

        @pl.when(i == 0)
        def _():
            dw_ref[...] = jnp.zeros(dw_ref.shape, F32)

        wv = cw[...]
        dws, dxs = [], []
        for j, (xr, pr, dr, nr) in enumerate(((xq, pq, d0, n0), (xk, pk, d1, n1), (xv, pv, d2, n2))):
            x, xp, dcv, dnx = xr[...], pr[...], dr[...], nr[...]
            wj = wv[:, j * GDN_W:(j + 1) * GDN_W]
            row = lax.broadcasted_iota(jnp.int32, x.shape, 0)
            dx = dcv * wj[GDN_CONV - 1:GDN_CONV, :]
            rows_w = [jnp.sum(dcv * x, axis=0, keepdims=True)]
            for s in range(1, GDN_CONV):
                up = jnp.where(row < tb - s, _roll(dcv, tb - s, 0), _roll(dnx, tb - s, 0) * has_next)
                dx = dx + up * wj[GDN_CONV - 1 - s:GDN_CONV - s, :]
                sh = jnp.where(row >= s, _roll(x, s, 0), _roll(xp, s, 0) * has_prev)
                rows_w.append(jnp.sum(dcv * sh, axis=0, keepdims=True))
            dxs.append(dx)
            dws.append(jnp.concatenate(rows_w[::-1], axis=0))
        dx_ref[...] = jnp.concatenate(dxs, axis=1).astype(dx_ref.dtype)
        dw_ref[...] += jnp.concatenate(dws, axis=1)

    return pl.pallas_call(
        body, name=name, grid=(steps,),
        in_specs=_qkv_specs(tb) + dcur + dnext + [pl.BlockSpec(conv_w.shape, lambda i: (0, 0)), ANY_SPEC],
        out_specs=[pl.BlockSpec((tb, 3 * GDN_W), lambda i: (i, PIN_QKV // (3 * GDN_W))),
                   pl.BlockSpec(conv_w.shape, lambda i: (0, 0))],
        out_shape=[jax.ShapeDtypeStruct((t, PIN_W), BF16), jax.ShapeDtypeStruct(conv_w.shape, F32)],
        input_output_aliases={13: 0},
        compiler_params=_params(("arbitrary",)))(proj, proj, proj, proj, proj, proj, *dc, *dc, conv_w, dproj)


def _pad_heads_cols(wm, per_head):
    r = wm.shape[0]
    return jnp.pad(wm.reshape(r, N_HEADS, per_head), ((0, 0), (0, 0), (0, HEAD_LANES - per_head))).reshape(r, MLA_PAD)


def _unpad_heads_cols(wm, per_head):
    r = wm.shape[0]
    return wm.reshape(r, N_HEADS, HEAD_LANES)[:, :, :per_head].reshape(r, N_HEADS * per_head)


def _win_to_pad(wi):
    r = wi.shape[0]
    z = lambda n: jnp.zeros((r, n), wi.dtype)
    o = MLA_Q_RANK + MLA_KV_RANK
    kpe = wi[:, o:o + MLA_ROPE]
    o2 = o + MLA_ROPE
    qkv = wi[:, o2:o2 + 3 * GDN_W]
    o3 = o2 + 3 * GDN_W
    ab = wi[:, o3:o3 + 2 * N_HEADS]
    gate = wi[:, o3 + 2 * N_HEADS:]
    return jnp.concatenate([qkv, wi[:, :o], z(MLA_NOPE), kpe, z(HEAD_LANES - MLA_NOPE - MLA_ROPE), gate, ab,
                            z(HEAD_LANES - 2 * N_HEADS)], axis=1)


def _win_from_pad(wp):
    return jnp.concatenate([wp[:, PIN_MLA:PIN_KPE], wp[:, PIN_KPE + MLA_NOPE:PIN_KPE + MLA_NOPE + MLA_ROPE],
                            wp[:, PIN_QKV:PIN_QKV + 3 * GDN_W], wp[:, PIN_AB:PIN_AB + 2 * N_HEADS],
                            wp[:, PIN_GATE:PIN_GATE + GDN_W]], axis=1)


def _wkv_to_pad(wkv):
    r = wkv.shape[0]
    w3 = wkv.reshape(r, N_HEADS, MLA_NOPE + MLA_V)
    kpart = jnp.pad(w3[:, :, :MLA_NOPE], ((0, 0), (0, 0), (0, HEAD_LANES - MLA_NOPE))).reshape(r, MLA_PAD)
    vpart = jnp.pad(w3[:, :, MLA_NOPE:], ((0, 0), (0, 0), (0, HEAD_LANES - MLA_V))).reshape(r, MLA_PAD)
    return jnp.concatenate([kpart, vpart], axis=1)


def _wkv_from_pad(wp):
    r = wp.shape[0]
    kpart = wp[:, :MLA_PAD].reshape(r, N_HEADS, HEAD_LANES)[:, :, :MLA_NOPE]
    vpart = wp[:, MLA_PAD:].reshape(r, N_HEADS, HEAD_LANES)[:, :, :MLA_V]
    return jnp.concatenate([kpart, vpart], axis=2).reshape(r, N_HEADS * (MLA_NOPE + MLA_V))


def _wout_to_pad(wo):
    n = wo.shape[1]
    mla = jnp.pad(wo[:N_HEADS * MLA_V].reshape(N_HEADS, MLA_V, n), ((0, 0), (0, HEAD_LANES - MLA_V), (0, 0)))
    return jnp.concatenate([mla.reshape(MLA_PAD, n), wo[N_HEADS * MLA_V:]], axis=0)


def _wout_from_pad(wp):
    n = wp.shape[1]
    mla = wp[:MLA_PAD].reshape(N_HEADS, HEAD_LANES, n)[:, :MLA_V].reshape(N_HEADS * MLA_V, n)
    return jnp.concatenate([mla, wp[MLA_PAD:]], axis=0)


def _pad_lanes(v, n):
    return jnp.pad(v, ((0, 0), (0, n - v.shape[1])))


def _compute_weights(full):
    w = {}
    for n in FFN_BIG:
        if n in full:
            w[n] = full[n].astype(MM_DTYPE)
    w["w_in_pad"] = _win_to_pad(full["w_in"]).astype(MM_DTYPE)
    w["w_uq_pad"] = _pad_heads_cols(full["mla_w_uq"], MLA_NOPE + MLA_ROPE).astype(MM_DTYPE)
    w["w_kv_pad"] = _wkv_to_pad(full["mla_w_ukv"]).astype(MM_DTYPE)
    w["w_out_pad"] = _wout_to_pad(full["w_out"]).astype(MM_DTYPE)
    w["conv_w"] = full["gdn_conv_w"].astype(F32)
    for n in ("ffn1_pre_g", "ffn1_post_g", "mix_pre_g", "mla_q_norm_g", "mla_kv_norm_g", "gdn_norm_g", "mix_post_g",
              "ffn2_pre_g", "ffn2_post_g"):
        w[n] = full[n]
    w["mla_out_g_pad"] = _pad_heads_cols(full["mla_out_g"], MLA_V)
    w["a_log_pad"] = _pad_lanes(full["gdn_a_log"], HEAD_LANES)
    w["dt_bias_pad"] = _pad_lanes(full["gdn_dt_bias"], HEAD_LANES)
    return w


FFN2_BIG = FFN_BIG[3:]


def _local_step(x, positions, loss_target, full, late=None):
    t, d = x.shape
    tb = min(512, t)
    tm = min(1024, t)
    w = _compute_weights(full)
    ffn = lambda tag: (w[tag + "_pre_g"], w[tag + "_w_gate"], w[tag + "_w_up"], w[tag + "_w_down"], w[tag + "_post_g"])
    x1, sv1 = _ffn_fwd("ffn1", x, *ffn("ffn1"), tm)
    x2, svm, gathered = _mixer_fwd(x1, positions, w, tb, _carried_gather(late[0]) if late else None)
    for n, gw in zip(FFN2_BIG, gathered):
        w[n] = gw
    x3, sv2 = _ffn_fwd("ffn2", x2, *ffn("ffn2"), tm)

    def loss_f(yb, tg):
        e = yb - tg
        return e * (1.0 / d), jnp.sum(e * e, axis=0, keepdims=True)

    dy, lsum = _rowwise("loss", loss_f, [x3, loss_target], [], [(d, F32)], [(1, d)], tb)
    g = {}
    dx2, g["ffn2_pre_g"], g["ffn2_w_gate"], g["ffn2_w_up"], g["ffn2_w_down"], g["ffn2_post_g"], _, _ = _ffn_bwd(
        "ffn2", dy, sv2, *ffn("ffn2"), tm, tm)

    def pair_sums(arrs, tag):
        got = _swap_halves(arrs, tag)
        return [_add_pair("add_pair%s_%d" % (tag, i), gi, gt, late[1]) for i, (gi, gt) in enumerate(zip(arrs, got))]

    def chip_sums(pairs, slabs, tag):
        return [_add_chips("add_chips%s_%d" % (tag, i), pr, sl, late[2]) for i, (pr, sl) in enumerate(zip(pairs, slabs))]

    if late:
        pairs2 = pair_sums([g[n] for n in FFN2_BIG], "_ffn2")
        dx1, gm, slabs2 = _mixer_bwd(dx2, svm, w, tb, _carried_scatter(pairs2))
        for n, hs in zip(FFN2_BIG, chip_sums(pairs2, slabs2, "_ffn2")):
            g[n] = hs
    else:
        dx1, gm, _ = _mixer_bwd(dx2, svm, w, tb)
    g["w_in"] = _win_from_pad(gm["w_in_pad"])
    g["mla_w_uq"] = _unpad_heads_cols(gm["w_uq_pad"], MLA_NOPE + MLA_ROPE)
    g["mla_w_ukv"] = _wkv_from_pad(gm["w_kv_pad"])
    g["gdn_conv_w"] = gm["conv_w"]
    g["w_out"] = _wout_from_pad(gm["w_out_pad"])
    ffn1_names = FFN_BIG[:3]
    if late:
        quarters = [_pack([jnp.split(g[n], N_SHARD, axis=SHARD_AXIS[n])[q] for n in MIX_BIG], MM_DTYPE)
                    for q in range(N_SHARD)]
        pairs_m = pair_sums([jnp.stack(quarters)], "_mix")
        pairs1 = []

        def make_up(*dws):
            pairs1.extend(pair_sums(list(dws), "_ffn1"))
            return _carried_scatter(pairs1)

        dx0, g["ffn1_pre_g"], _, _, _, g["ffn1_post_g"], slabs_m, slabs1 = _ffn_bwd(
            "ffn1", dx1, sv1, *ffn("ffn1"), tm, tm, _carried_scatter(pairs_m), make_up)
        for n, hs in zip(ffn1_names, chip_sums(pairs1, slabs1, "_ffn1")):
            g[n] = hs
        g["mix_pack"] = chip_sums(pairs_m, slabs_m, "_mix")[0]
    else:
        dx0, g["ffn1_pre_g"], g["ffn1_w_gate"], g["ffn1_w_up"], g["ffn1_w_down"], g["ffn1_post_g"], _, _ = _ffn_bwd(
            "ffn1", dx1, sv1, *ffn("ffn1"), tm, tm)
    g["mix_pre_g"], g["mix_post_g"] = gm["mix_pre_g"], gm["mix_post_g"]
    g["mla_q_norm_g"], g["mla_kv_norm_g"] = gm["mla_q_norm_g"], gm["mla_kv_norm_g"]
    g["gdn_norm_g"] = gm["gdn_norm_g"]
    g["mla_out_g"] = _unpad_heads_cols(gm["mla_out_g_pad"], MLA_V)
    g["gdn_a_log"] = gm["a_log_pad"][:, :N_HEADS]
    g["gdn_dt_bias"] = gm["dt_bias_pad"][:, :N_HEADS]
    return lsum, dx0, g


HBM_SPEC = pl.BlockSpec(memory_space=pltpu.HBM)


def _place():
    return lax.axis_index("x"), lax.axis_index("y"), lax.axis_index("c")


def _exchange_call(name, body, ins, out_shapes, n_remote, n_local):
    return pl.pallas_call(
        body, name=name, in_specs=[HBM_SPEC] * len(ins), out_specs=[HBM_SPEC] * len(out_shapes), out_shape=out_shapes,
        scratch_shapes=[pltpu.SemaphoreType.DMA((n_remote,)), pltpu.SemaphoreType.DMA((n_remote,)),
                        pltpu.SemaphoreType.DMA((n_local,))])(*ins)


def _other_chips(x, y):
    return [(1 - x, y), (x, 1 - y), (1 - x, 1 - y)]


def _at_each_chip(fn):
    x, y, _ = _place()
    for cx in range(2):
        for cy in range(2):
            pl.when((x == cx) & (y == cy))(functools.partial(fn, cx, cy))


def _at_each_device(fn):
    x, y, c = _place()
    for cx in range(2):
        for cy in range(2):
            for cc in range(2):
                pl.when((x == cx) & (y == cy) & (c == cc))(functools.partial(fn, cx, cy, cc))


def _at_each_core(fn):
    c = lax.axis_index("c")
    for cc in range(2):
        pl.when(c == cc)(functools.partial(fn, cc))


def _gather_shards(ws):
    nw = len(ws)

    def body(*refs):
        w_refs, out_refs = refs[:nw], refs[nw:2 * nw]
        send_sems, recv_sems, local_sems = refs[2 * nw:]

        def run(x, y, c):
            chips = _other_chips(x, y)
            me, sibling = 2 * x + y, (x, y, 1 - c)

            def half(ref, which):
                hr = ref.shape[0] // 2
                return ref.at[pl.ds(which * hr, hr)]

            def over_ici(i, j, src, slab, to):
                return pltpu.make_async_remote_copy(
                    src_ref=half(src, c), dst_ref=half(out_refs[i].at[slab], c), send_sem=send_sems.at[7 * i + j],
                    recv_sem=recv_sems.at[7 * i + j], device_id=to, device_id_type=MESH)

            def over_d2d(i, j, slab, which):
                return pltpu.make_async_remote_copy(
                    src_ref=half(out_refs[i].at[slab], which), dst_ref=half(out_refs[i].at[slab], which),
                    send_sem=send_sems.at[7 * i + 3 + j], recv_sem=recv_sems.at[7 * i + 3 + j], device_id=sibling,
                    device_id_type=MESH)

            def own(i, w_ref):
                return pltpu.make_async_remote_copy(
                    src_ref=w_ref, dst_ref=out_refs[i].at[me], send_sem=send_sems.at[7 * i + 6],
                    recv_sem=recv_sems.at[7 * i + 6], device_id=sibling, device_id_type=MESH)

            sends, passed = [], []
            for i, w_ref in enumerate(w_refs):
                for j, (px, py) in enumerate(chips):
                    sends.append(over_ici(i, j, w_ref, me, (px, py, c)))
                    sends[-1].start()
            for i, w_ref in enumerate(w_refs):
                sends.append(own(i, w_ref))
                sends[-1].start()
            for i, w_ref in enumerate(w_refs):
                for j, (px, py) in enumerate(chips):
                    over_ici(i, j, w_ref, 2 * px + py, (px, py, c)).wait_recv()
                    passed.append(over_d2d(i, j, 2 * px + py, c))
                    passed[-1].start()
            for i, w_ref in enumerate(w_refs):
                own(i, w_ref).wait_recv()
                for j, (px, py) in enumerate(chips):
                    over_d2d(i, j, 2 * px + py, 1 - c).wait_recv()
            for cp in sends + passed:
                cp.wait_send()

        _at_each_device(run)

    outs = [jax.ShapeDtypeStruct((N_SHARD,) + w.shape, w.dtype) for w in ws]
    return _exchange_call("gather_weight_shards", body, ws, outs, 7 * nw, 1)


def _swap_halves(gs, tag=""):
    ng = len(gs)

    def body(*refs):
        g_refs, got_refs = refs[:ng], refs[ng:2 * ng]
        send_sems, recv_sems, _ = refs[2 * ng:]
        x, y, _ = _place()

        def run(c):
            sends = []
            for i, (g_ref, got_ref) in enumerate(zip(g_refs, got_refs)):
                hr = got_ref.shape[1]
                sends.append(pltpu.make_async_remote_copy(
                    src_ref=g_ref.at[:, pl.ds((1 - c) * hr, hr)], dst_ref=got_ref, send_sem=send_sems.at[i],
                    recv_sem=recv_sems.at[i], device_id=(x, y, 1 - c), device_id_type=MESH))
                sends[-1].start()
            for cp in sends:
                cp.wait()

        _at_each_core(run)

    halves = [jax.ShapeDtypeStruct((g.shape[0], g.shape[1] // 2, g.shape[2]), g.dtype) for g in gs]
    return _exchange_call("swap_grad_halves" + tag, body, gs, halves, ng, 1)


def _scatter_copies(p_refs, out_refs, send_sems, recv_sems, x, y):
    c = lax.axis_index("c")
    copies = []
    for i, (p_ref, out_ref) in enumerate(zip(p_refs, out_refs)):
        for j, (px, py) in enumerate(_other_chips(x, y)):
            copies.append(pltpu.make_async_remote_copy(
                src_ref=p_ref.at[2 * px + py], dst_ref=out_ref.at[j], send_sem=send_sems.at[3 * i + j],
                recv_sem=recv_sems.at[3 * i + j], device_id=(px, py, c), device_id_type=MESH))
    return copies


def _start_all(make, *refs):
    def run(x, y):
        for cp in make(*refs, x, y):
            cp.start()

    _at_each_chip(run)


def _wait_all(make, *refs):
    def run(x, y):
        copies = make(*refs, x, y)
        for cp in copies:
            cp.wait_recv()
        for cp in copies:
            cp.wait_send()

    _at_each_chip(run)


def _scatter_shapes(ps):
    return [jax.ShapeDtypeStruct((3,) + p.shape[1:], p.dtype) for p in ps]


def _carried_scatter(ps):
    return _Carried(ps, _scatter_shapes(ps), 3 * len(ps), functools.partial(_start_all, _scatter_copies),
                    functools.partial(_wait_all, _scatter_copies))


def _direct_gather_copies(w_refs, out_refs, send_sems, recv_sems, x, y, arriving):
    c = lax.axis_index("c")
    me = 2 * x + y
    peers = [((px, py, c), 2 * px + py) for px, py in _other_chips(x, y)] + [((x, y, 1 - c), me)]
    copies = []
    for i, (w_ref, out_ref) in enumerate(zip(w_refs, out_refs)):
        for j, (peer, slab) in enumerate(peers):
            copies.append(pltpu.make_async_remote_copy(
                src_ref=w_ref, dst_ref=out_ref.at[slab if arriving else me], send_sem=send_sems.at[4 * i + j],
                recv_sem=recv_sems.at[4 * i + j], device_id=peer, device_id_type=MESH))
    return copies


def _carried_gather(ws):
    def start(w_refs, out_refs, send_sems, recv_sems):
        def run(x, y):
            for cp in _direct_gather_copies(w_refs, out_refs, send_sems, recv_sems, x, y, False):
                cp.start()

        _at_each_chip(run)

    def finish(w_refs, out_refs, send_sems, recv_sems):
        def run(x, y):
            for cp in _direct_gather_copies(w_refs, out_refs, send_sems, recv_sems, x, y, True):
                cp.wait_recv()
            for cp in _direct_gather_copies(w_refs, out_refs, send_sems, recv_sems, x, y, False):
                cp.wait_send()

        _at_each_chip(run)

    outs = [jax.ShapeDtypeStruct((N_SHARD,) + w.shape, w.dtype) for w in ws]
    return _Carried(ws, outs, 4 * len(ws), start, finish)


def _share_halves(hs):
    n = len(hs)

    def body(*refs):
        h_refs, out_refs = refs[:n], refs[n:2 * n]
        send_sems, recv_sems, _ = refs[2 * n:]
        x, y, c = _place()
        sends = []
        for i, (h_ref, out_ref) in enumerate(zip(h_refs, out_refs)):
            sends.append(pltpu.make_async_remote_copy(
                src_ref=h_ref, dst_ref=out_ref, send_sem=send_sems.at[i], recv_sem=recv_sems.at[i],
                device_id=(x, y, 1 - c), device_id_type=MESH))
            sends[-1].start()
        for cp in sends:
            cp.wait()

    outs = [jax.ShapeDtypeStruct(h.shape, h.dtype) for h in hs]
    return _exchange_call("share_grad_halves", body, hs, outs, n, 1)


def _scalar_grid_call(name, body, scalars, grid, in_specs, out_specs, out_shape, args):
    grid_spec = pltpu.PrefetchScalarGridSpec(num_scalar_prefetch=len(scalars), grid=grid, in_specs=in_specs,
                                             out_specs=out_specs)
    return pl.pallas_call(body, name=name, grid_spec=grid_spec, out_shape=out_shape,
                          compiler_params=_params(("arbitrary",) * len(grid)))(*scalars, *args)


def _add_pair(name, g, got, core):
    ns_, hr, cols = got.shape
    th = _row_tile(hr, 512)
    nb = hr // th

    def body(core_ref, g_ref, got_ref, out_ref):
        out_ref[...] = (g_ref[...].astype(F32) + got_ref[...].astype(F32)).astype(out_ref.dtype)

    blk = pl.BlockSpec((1, th, cols), lambda q, j, core_ref: (q, j, 0))
    own = pl.BlockSpec((1, th, cols), lambda q, j, core_ref: (q, core_ref[0] * nb + j, 0))
    return _scalar_grid_call(name, body, [core], (ns_, nb), [own, blk], blk,
                             jax.ShapeDtypeStruct(got.shape, got.dtype), [g, got])


def _add_chips(name, pairs, slabs, chip):
    _, hr, cols = slabs.shape
    th = _row_tile(hr, 512)

    def body(chip_ref, own_ref, s0_ref, s1_ref, s2_ref, out_ref):
        total = own_ref[0].astype(F32) + s0_ref[0].astype(F32)
        out_ref[...] = (total + s1_ref[0].astype(F32)) + s2_ref[0].astype(F32)

    own = pl.BlockSpec((1, th, cols), lambda j, chip_ref: (chip_ref[0], j, 0))
    others = [pl.BlockSpec((1, th, cols), lambda j, chip_ref, k=k: (k, j, 0)) for k in range(3)]
    return _scalar_grid_call(name, body, [chip], (hr // th,), [own] + others,
                             pl.BlockSpec((th, cols), lambda j, chip_ref: (j, 0)),
                             jax.ShapeDtypeStruct((hr, cols), F32), [pairs, slabs, slabs, slabs])


def _join_halves(name, mine, other, core):
    hr, cols = mine.shape
    th = _row_tile(hr, 512)
    nb = hr // th

    def body(core_ref, mine_ref, other_ref, out_ref):
        is_mine = pl.program_id(0) == core_ref[0]

        @pl.when(is_mine)
        def _():
            out_ref[0] = mine_ref[...]

        @pl.when(jnp.logical_not(is_mine))
        def _():
            out_ref[0] = other_ref[...]

    blk = pl.BlockSpec((th, cols), lambda h, j, core_ref: (j, 0))
    return _scalar_grid_call(name, body, [core], (2, nb), [blk, blk],
                             pl.BlockSpec((1, th, cols), lambda h, j, core_ref: (0, h * nb + j, 0)),
                             jax.ShapeDtypeStruct((1, 2 * hr, cols), mine.dtype), [mine, other])


def _gather_small(sp):
    def body(s_ref, out_ref, send_sems, recv_sems, local_sem):
        x, y, c = _place()
        me = 4 * x + 2 * y + c
        peers = [(x ^ (m >> 2), y ^ ((m >> 1) & 1), c ^ (m & 1)) for m in range(1, 8)]
        mine = pltpu.make_async_copy(s_ref, out_ref.at[me], local_sem)
        mine.start()
        sends = [pltpu.make_async_remote_copy(src_ref=s_ref, dst_ref=out_ref.at[me], send_sem=send_sems.at[j],
                                              recv_sem=recv_sems.at[j], device_id=p, device_id_type=MESH)
                 for j, p in enumerate(peers)]
        for cp in sends:
            cp.start()
        for j, (px, py, pc) in enumerate(peers):
            pltpu.make_async_remote_copy(src_ref=s_ref, dst_ref=out_ref.at[4 * px + 2 * py + pc],
                                         send_sem=send_sems.at[j], recv_sem=recv_sems.at[j], device_id=(px, py, pc),
                                         device_id_type=MESH).wait_recv()
        for cp in sends:
            cp.wait_send()
        mine.wait()

    return pl.pallas_call(
        body, name="gather_small_grads", in_specs=[HBM_SPEC], out_specs=HBM_SPEC,
        out_shape=jax.ShapeDtypeStruct((8,) + sp.shape, sp.dtype),
        scratch_shapes=[pltpu.SemaphoreType.DMA((7,)), pltpu.SemaphoreType.DMA((7,)), pltpu.SemaphoreType.DMA])(sp)


def _pack_rows(total):
    rows = -(-total // LANES)
    return -(-rows // 32) * 32


def _pack(arrs, dtype):
    flat = jnp.concatenate([a.reshape(-1).astype(dtype) for a in arrs])
    rows = _pack_rows(flat.shape[0])
    return jnp.pad(flat, (0, rows * LANES - flat.shape[0])).reshape(rows, LANES)


def _unpack(buf, shapes):
    flat = buf.reshape(-1)
    out, off = {}, 0
    for n, shp in shapes:
        size = shp[0] * shp[1]
        out[n] = flat[off:off + size].reshape(shp)
        off += size
    return out


def _to_wire(name, w3):
    _, r, cols = w3.shape
    tb = _row_tile(r, 512)

    def body(w_ref, o_ref):
        o_ref[...] = w_ref[0].astype(o_ref.dtype)

    return pl.pallas_call(
        body, name=name, grid=(r // tb,), in_specs=[pl.BlockSpec((1, tb, cols), lambda i: (0, i, 0))],
        out_specs=pl.BlockSpec((tb, cols), lambda i: (i, 0)), out_shape=jax.ShapeDtypeStruct((r, cols), MM_DTYPE),
        compiler_params=_params(("arbitrary",)))(w3)


def _adamw(name, w3, g, m3, v3, tb):
    c1 = 1.0 - ADAM_B1 ** ADAM_STEP
    c2 = 1.0 - ADAM_B2 ** ADAM_STEP
    _, r, cols = w3.shape
    emit = g.ndim == 2
    blk3 = pl.BlockSpec((1, tb, cols), lambda i: (0, i, 0))
    g_spec = pl.BlockSpec((tb, cols), lambda i: (i, 0)) if emit else blk3

    def body(w_ref, g_ref, m_ref, v_ref, *out_refs):
        gb = g_ref[...] if emit else g_ref[0]
        m2 = ADAM_B1 * m_ref[0] + (1.0 - ADAM_B1) * gb
        v2 = ADAM_B2 * v_ref[0] + (1.0 - ADAM_B2) * (gb * gb)
        out_refs[-3][0] = -ADAM_LR * ((m2 / c1) / (jnp.sqrt(v2 / c2) + ADAM_EPS) + ADAM_WD * w_ref[0])
        out_refs[-2][0] = m2
        out_refs[-1][0] = v2
        if emit:
            out_refs[0][0] = gb

    n_out = 4 if emit else 3
    outs = pl.pallas_call(
        body, name=name, grid=(r // tb,), in_specs=[blk3, g_spec, blk3, blk3], out_specs=[blk3] * n_out,
        out_shape=[jax.ShapeDtypeStruct((1, r, cols), F32)] * n_out,
        compiler_params=_params(("arbitrary",)))(w3, g, m3, v3)
    return outs if emit else [g] + list(outs)


def _row_tile(rows, pref):
    if rows <= pref:
        return rows
    t = pref
    while t >= 8:
        if rows % t == 0 and t % 8 == 0:
            return t
        t -= 8
    return rows


def kernel(x, positions, ffn1_pre_g, ffn1_w_gate, ffn1_w_up, ffn1_w_down, ffn1_post_g, mix_pre_g, w_in, mla_q_norm_g, mla_w_uq, mla_kv_norm_g, mla_w_ukv, mla_out_g, gdn_conv_w, gdn_a_log, gdn_dt_bias, gdn_norm_g, w_out, mix_post_g, ffn2_pre_g, ffn2_w_gate, ffn2_w_up, ffn2_w_down, ffn2_post_g, loss_target, m_ffn1_pre_g, m_ffn1_w_gate, m_ffn1_w_up, m_ffn1_w_down, m_ffn1_post_g, m_mix_pre_g, m_w_in, m_mla_q_norm_g, m_mla_w_uq, m_mla_kv_norm_g, m_mla_w_ukv, m_mla_out_g, m_gdn_conv_w, m_gdn_a_log, m_gdn_dt_bias, m_gdn_norm_g, m_w_out, m_mix_post_g, m_ffn2_pre_g, m_ffn2_w_gate, m_ffn2_w_up, m_ffn2_w_down, m_ffn2_post_g, v_ffn1_pre_g, v_ffn1_w_gate, v_ffn1_w_up, v_ffn1_w_down, v_ffn1_post_g, v_mix_pre_g, v_w_in, v_mla_q_norm_g, v_mla_w_uq, v_mla_kv_norm_g, v_mla_w_ukv, v_mla_out_g, v_gdn_conv_w, v_gdn_a_log, v_gdn_dt_bias, v_gdn_norm_g, v_w_out, v_mix_post_g, v_ffn2_pre_g, v_ffn2_w_gate, v_ffn2_w_up, v_ffn2_w_down, v_ffn2_post_g):
    args = dict(locals())
    wsh = {n: args[n][0] for n in WEIGHTS}
    msh = {n: args["m_" + n][0] if args["m_" + n].ndim == 3 else args["m_" + n] for n in WEIGHTS}
    vsh = {n: args["v_" + n][0] if args["v_" + n].ndim == 3 else args["v_" + n] for n in WEIGHTS}
    for n in SMALL:
        wsh[n] = args[n]
    mix_shapes = [(n, wsh[n].shape) for n in MIX_BIG]

    early = FFN_BIG[:3]
    held = lambda a, n: jnp.swapaxes(a, 1, 2) if n in TRANSPOSED else a
    gathered = _gather_shards([_to_wire("wire_" + n, held(args[n], n)) for n in early]
                              + [_pack([wsh[n] for n in MIX_BIG], MM_DTYPE)])
    full = {n: wsh[n] for n in SMALL}
    for n, gw in zip(early, gathered):
        full[n] = gw
    parts = [_unpack(gathered[-1][q], mix_shapes) for q in range(N_SHARD)]
    for n in MIX_BIG:
        full[n] = jnp.concatenate([parts[q][n] for q in range(N_SHARD)], axis=SHARD_AXIS[n])

    core = lax.axis_index("c").astype(jnp.int32).reshape(1)
    chip = (2 * lax.axis_index("x") + lax.axis_index("y")).astype(jnp.int32).reshape(1)
    late = ([_to_wire("wire_" + n, held(args[n], n)) for n in FFN2_BIG], core, chip)
    lsum, grad_x, g = _local_step(x[0], positions, loss_target[0], full, late)
    loss = lax.psum(0.5 * jnp.sum(lsum) / x.shape[-1], ("x", "y", "c"))

    halves = [g[n] for n in FFN_BIG] + [g["mix_pack"]]
    others = _share_halves(halves)
    shared = [_join_halves("join_halves_%d" % i, hm, ho, core) for i, (hm, ho) in enumerate(zip(halves, others))]
    gsh = _unpack(shared[-1], mix_shapes)
    for n, sg_ in zip(FFN_BIG, shared):
        gsh[n] = sg_

    small_shapes = [(n, wsh[n].shape) for n in SMALL]
    pack_small = lambda d: jnp.concatenate(
        [_pad_lanes(d[n].astype(F32), LANES) for n in SMALL] + [jnp.zeros((SMALL_ROWS - len(SMALL), LANES), F32)], axis=0)
    slots = _gather_small(pack_small(g))

    c1 = 1.0 - ADAM_B1 ** ADAM_STEP
    c2 = 1.0 - ADAM_B2 ** ADAM_STEP

    def small_update(wb, mb, vb, s8):
        gs = s8[0:SMALL_ROWS]
        for d in range(1, 8):
            gs = gs + s8[d * SMALL_ROWS:(d + 1) * SMALL_ROWS]
        m2 = ADAM_B1 * mb + (1.0 - ADAM_B1) * gs
        v2 = ADAM_B2 * vb + (1.0 - ADAM_B2) * (gs * gs)
        delta = -ADAM_LR * ((m2 / c1) / (jnp.sqrt(v2 / c2) + ADAM_EPS) + ADAM_WD * wb)
        return gs, delta, m2, v2

    sg, sd, sm, sv_ = _rowwise("adamw_small", small_update,
                               [pack_small(wsh), pack_small(msh), pack_small(vsh)],
                               [slots.reshape(8 * SMALL_ROWS, LANES)], [(LANES, F32)] * 4, [], SMALL_ROWS)
    grads, deltas, new_m, new_v = {}, {}, {}, {}
    for i, (n, shp) in enumerate(small_shapes):
        grads[n], deltas[n] = sg[i:i + 1, :shp[1]], sd[i:i + 1, :shp[1]]
        new_m[n], new_v[n] = sm[i:i + 1, :shp[1]], sv_[i:i + 1, :shp[1]]
    for n in BIG:
        w3 = held(args[n], n)
        outs = _adamw("adamw_" + n, w3, gsh[n], held(args["m_" + n], n), held(args["v_" + n], n),
                      _row_tile(w3.shape[1], 256))
        grads[n], deltas[n], new_m[n], new_v[n] = [held(o, n) for o in outs]

    return (loss, grad_x[None], *[grads[n] for n in WEIGHTS], *[deltas[n] for n in WEIGHTS],
            *[new_m[n] for n in WEIGHTS], *[new_v[n] for n in WEIGHTS])
```

```python
import functools

import jax
import jax.numpy as jnp
from jax import lax
from jax.experimental import pallas as pl
from jax.experimental.pallas import tpu as pltpu

F32 = jnp.float32
BF16 = jnp.bfloat16
MM_DTYPE = BF16
MESH = pl.DeviceIdType.MESH

D_MODEL = 1024
D_FF = 2816
N_HEADS = 8
MLA_Q_RANK = 256
MLA_KV_RANK = 128
MLA_NOPE = 64
MLA_ROPE = 32
MLA_V = 64
ROPE_THETA = 10000.0
GDN_DH = 64
GDN_W = N_HEADS * GDN_DH
GDN_CONV = 4
CHUNK = 64
HEAD_LANES = 128
HEADS_PER_STEP = 8
MLA_PAD = N_HEADS * HEAD_LANES
EPS = 1e-6
N_SHARD = 4
LANES = 1024

PIN_QKV = 0
PIN_MLA = 1536
PIN_KPE = 1920
PIN_GATE = 2048
PIN_AB = 2560
PIN_W = 2688
CAT_W = MLA_PAD + GDN_W

ADAM_LR = 0.001
ADAM_B1 = 0.9
ADAM_B2 = 0.999
ADAM_EPS = 1e-08
ADAM_WD = 0.01
ADAM_STEP = 10

VMEM_LIMIT_V7X = 56 * 1024 * 1024

BIG = ["ffn1_w_gate", "ffn1_w_up", "ffn1_w_down", "w_in", "mla_w_uq", "mla_w_ukv", "gdn_conv_w", "w_out",
       "ffn2_w_gate", "ffn2_w_up", "ffn2_w_down"]
FFN_BIG = ["ffn1_w_gate", "ffn1_w_up", "ffn1_w_down", "ffn2_w_gate", "ffn2_w_up", "ffn2_w_down"]
TRANSPOSED = ["ffn1_w_gate", "ffn1_w_up", "ffn2_w_gate", "ffn2_w_up", "w_in"]
MIX_BIG = ["mla_w_uq", "mla_w_ukv", "gdn_conv_w", "w_out"]
SMALL = ["ffn1_pre_g", "ffn1_post_g", "mix_pre_g", "mla_q_norm_g", "mla_kv_norm_g", "mla_out_g", "gdn_a_log",
         "gdn_dt_bias", "gdn_norm_g", "mix_post_g", "ffn2_pre_g", "ffn2_post_g"]
WEIGHTS = ["ffn1_pre_g", "ffn1_w_gate", "ffn1_w_up", "ffn1_w_down", "ffn1_post_g", "mix_pre_g", "w_in",
           "mla_q_norm_g", "mla_w_uq", "mla_kv_norm_g", "mla_w_ukv", "mla_out_g", "gdn_conv_w", "gdn_a_log",
           "gdn_dt_bias", "gdn_norm_g", "w_out", "mix_post_g", "ffn2_pre_g", "ffn2_w_gate", "ffn2_w_up",
           "ffn2_w_down", "ffn2_post_g"]
SHARD_AXIS = {"ffn1_w_gate": 1, "ffn1_w_up": 1, "ffn1_w_down": 0, "w_in": 1, "mla_w_uq": 1, "mla_w_ukv": 1,
              "gdn_conv_w": 1, "w_out": 0, "ffn2_w_gate": 1, "ffn2_w_up": 1, "ffn2_w_down": 0}
SMALL_ROWS = 16


def _params(sem):
    return pltpu.CompilerParams(dimension_semantics=sem, vmem_limit_bytes=VMEM_LIMIT_V7X)


def _pick(dim, pref):
    if dim <= pref:
        return dim
    t = (pref // 128) * 128
    while t >= 128:
        if dim % t == 0:
            return t
        t -= 128
    return dim


ANY_SPEC = pl.BlockSpec(memory_space=pl.ANY)


def _rowwise(name, fn, row_ins, bc_ins, row_outs, acc_outs, tb, wide=None, carry=None):
    ents = []
    for e in row_ins:
        ents.append(e if isinstance(e, tuple) else (e, e.shape[1], 0, 0))
    over = [o[2] for o in row_outs if len(o) == 3]
    rows = over[0] if over else ents[0][0].shape[0]
    steps = rows // tb
    assert steps * tb == rows, (name, rows, tb)
    in_specs, args = [], []
    for a, w, j, r0 in ents:
        in_specs.append(pl.BlockSpec((tb, w), lambda i, j=j, r0=r0: (i + r0, j)))
        args.append(a)
    for b in bc_ins:
        in_specs.append(pl.BlockSpec(b.shape, lambda i: (0, 0)))
        args.append(b)
    n_in = len(args)
    aliases = {}
    if carry is not None:
        in_specs.append(ANY_SPEC)
        args.append(carry)
        aliases = {n_in: 0}
    out_shape = [jax.ShapeDtypeStruct((rows, o[0]), o[1]) for o in row_outs]
    out_specs = [pl.BlockSpec((tb, o[0]), lambda i: (i, 0)) for o in row_outs]
    if wide is not None:
        out_shape[0] = jax.ShapeDtypeStruct((rows, wide[0]), row_outs[0][1])
        out_specs[0] = pl.BlockSpec((tb, row_outs[0][0]), lambda i: (i, wide[1]))
    out_shape += [jax.ShapeDtypeStruct((r, c), F32) for r, c in acc_outs]
    out_specs += [pl.BlockSpec((r, c), lambda i: (0, 0)) for r, c in acc_outs]
    n_ro, n_acc, n_args = len(row_outs), len(acc_outs), len(args)

    def body(*refs):
        vals = fn(*[r[...] for r in refs[:n_in]])
        if not isinstance(vals, (tuple, list)):
            vals = (vals,)
        for r, v in zip(refs[n_args:n_args + n_ro], vals[:n_ro]):
            r[...] = v.astype(r.dtype)
        if n_acc:
            acc_refs = refs[n_args + n_ro:]

            @pl.when(pl.program_id(0) == 0)
            def _():
                for r in acc_refs:
                    r[...] = jnp.zeros(r.shape, r.dtype)

            for r, v in zip(acc_refs, vals[n_ro:]):
                r[...] += v

    outs = pl.pallas_call(body, name=name, grid=(steps,), in_specs=in_specs, out_specs=out_specs,
                          out_shape=out_shape, input_output_aliases=aliases,
                          compiler_params=_params(("arbitrary",)))(*args)
    return outs


def _mm(name, a, b, mode, out_dtype, tm=1024, tn=1024, tk=1024):
    if mode == "nn":
        (m, k), (k2, n) = a.shape, b.shape
    elif mode == "nt":
        (m, k), (n, k2) = a.shape, b.shape
    else:
        (k, m), (k2, n) = a.shape, b.shape
    assert k == k2, (name, a.shape, b.shape)
    tm, tn, tk = _pick(m, tm), _pick(n, tn), _pick(k, tk)
    nk = k // tk
    if mode == "nn":
        a_spec = pl.BlockSpec((tm, tk), lambda i, j, kk: (i, kk))
        b_spec = pl.BlockSpec((tk, tn), lambda i, j, kk: (kk, j))
        dims = (((1,), (0,)), ((), ()))
    elif mode == "nt":
        a_spec = pl.BlockSpec((tm, tk), lambda i, j, kk: (i, kk))
        b_spec = pl.BlockSpec((tn, tk), lambda i, j, kk: (j, kk))
        dims = (((1,), (1,)), ((), ()))
    else:
        a_spec = pl.BlockSpec((tk, tm), lambda i, j, kk: (kk, i))
        b_spec = pl.BlockSpec((tk, tn), lambda i, j, kk: (kk, j))
        dims = (((0,), (0,)), ((), ()))

    def body(a_ref, b_ref, o_ref, acc_ref):
        kk = pl.program_id(2)

        @pl.when(kk == 0)
        def _():
            acc_ref[...] = jnp.zeros(acc_ref.shape, F32)

        acc_ref[...] += lax.dot_general(a_ref[...].astype(MM_DTYPE), b_ref[...].astype(MM_DTYPE), dims,
                                        preferred_element_type=F32)

        @pl.when(kk == nk - 1)
        def _():
            o_ref[...] = acc_ref[...].astype(o_ref.dtype)

    return pl.pallas_call(
        body, name=name, grid=(m // tm, n // tn, nk), in_specs=[a_spec, b_spec],
        out_specs=pl.BlockSpec((tm, tn), lambda i, j, kk: (i, j)),
        out_shape=jax.ShapeDtypeStruct((m, n), out_dtype),
        scratch_shapes=[pltpu.VMEM((tm, tn), F32)],
        compiler_params=_params(("parallel", "parallel", "arbitrary")))(a, b)


def _rms_stats(x, n_real=None):
    n = x.shape[-1] if n_real is None else n_real
    return lax.rsqrt(jnp.sum(x * x, axis=-1, keepdims=True) / n + EPS)


def _rms_bwd(x, r, g, dz, n_real=None):
    n = x.shape[-1] if n_real is None else n_real
    xh = x * r
    dxh = dz * g
    dx = r * (dxh - xh * (jnp.sum(dxh * xh, axis=-1, keepdims=True) / n))
    return dx, jnp.sum(dz * xh, axis=0, keepdims=True)


def _sigmoid(x):
    return 0.5 * jnp.tanh(0.5 * x) + 0.5


def _roll(x, s, axis):
    return pltpu.roll(x, s, axis)


def _rope(x, c, s1, s2):
    return x * c + _roll(x, HEAD_LANES - MLA_ROPE // 2, 1) * s1 + _roll(x, MLA_ROPE // 2, 1) * s2


def _heads_apply(x, fn):
    return jnp.concatenate([fn(x[:, h * HEAD_LANES:(h + 1) * HEAD_LANES]) for h in range(N_HEADS)], axis=1)


ROW_CHUNK = 256


def _row_chunks(rows):
    step = min(ROW_CHUNK, rows)
    return [pl.ds(r, step) for r in range(0, rows, step)]


def _ffn_fwd(tag, x, g_pre, wg, wu, wd, g_post, tm):
    t, d = x.shape
    ns, fs, _ = wg.shape
    nt = t // tm
    row = pl.BlockSpec((tm, d), lambda i, q: (i, 0))
    vec = pl.BlockSpec((1, d), lambda i, q: (0, 0))
    act3 = pl.BlockSpec((1, tm, fs), lambda i, q: (q, i, 0))
    wrow = pl.BlockSpec((1, fs, d), lambda i, q: (q, 0, 0))
    nt_dims = (((1,), (1,)), ((), ()))

    def gate_up(x_ref, g_ref, wg_ref, wu_ref, n_ref, a_ref, u_ref, s_ref, n_s):
        @pl.when(pl.program_id(1) == 0)
        def _():
            for r in _row_chunks(tm):
                xb = x_ref[r, :]
                n_s[r, :] = (xb * _rms_stats(xb) * g_ref[...]).astype(MM_DTYPE)
            n_ref[...] = n_s[...]

        for r in _row_chunks(tm):
            n = n_s[r, :]
            a = lax.dot_general(n, wg_ref[0], nt_dims, preferred_element_type=F32)
            u = lax.dot_general(n, wu_ref[0], nt_dims, preferred_element_type=F32)
            a_ref[0, r, :] = a.astype(a_ref.dtype)
            u_ref[0, r, :] = u.astype(u_ref.dtype)
            s_ref[0, r, :] = ((a * _sigmoid(a)) * u).astype(s_ref.dtype)

    n, a, u, s = pl.pallas_call(
        gate_up, name=tag + "_gate_up", grid=(nt, ns), in_specs=[row, vec, wrow, wrow],
        out_specs=[row, act3, act3, act3],
        out_shape=[jax.ShapeDtypeStruct((t, d), MM_DTYPE)] + [jax.ShapeDtypeStruct((ns, t, fs), MM_DTYPE)] * 3,
        scratch_shapes=[pltpu.VMEM((tm, d), MM_DTYPE)],
        compiler_params=_params(("parallel", "arbitrary")))(x, g_pre, wg, wu)

    def down(s_ref, wd_ref, x_ref, g_ref, h_ref, y_ref, acc):
        q = pl.program_id(1)

        @pl.when(q == 0)
        def _():
            acc[...] = jnp.zeros(acc.shape, F32)

        for r in _row_chunks(tm):
            acc[r, :] += jnp.dot(s_ref[0, r, :], wd_ref[0], preferred_element_type=F32)

        @pl.when(q == ns - 1)
        def _():
            for r in _row_chunks(tm):
                hb = acc[r, :]
                h_ref[r, :] = hb
                y_ref[r, :] = x_ref[r, :] + 0.5 * (hb * _rms_stats(hb) * g_ref[...])

    h, y = pl.pallas_call(
        down, name=tag + "_down", grid=(nt, ns), in_specs=[act3, wrow, row, vec], out_specs=[row, row],
        out_shape=[jax.ShapeDtypeStruct((t, d), F32)] * 2, scratch_shapes=[pltpu.VMEM((tm, d), F32)],
        compiler_params=_params(("parallel", "arbitrary")))(s, wd, x, g_post)
    return y, (x, n, a, u, s, h)


def _carry(body, n_in, n_out, grid, carried):
    if carried is None:
        return body, [], [], [], [], []
    nx_in, nx_out = len(carried.ins), len(carried.outs)

    def wrapped(*refs):
        ins, rest = refs[:n_in], refs[n_in:]
        xi, rest = rest[:nx_in], rest[nx_in:]
        outs, rest = rest[:n_out], rest[n_out:]
        xo, rest = rest[:nx_out], rest[nx_out:]
        scr, sems = rest[:len(rest) - 2], rest[len(rest) - 2:]
        first, last = True, True
        for dim, size in enumerate(grid):
            first = first & (pl.program_id(dim) == 0)
            last = last & (pl.program_id(dim) == size - 1)

        @pl.when(first)
        def _():
            carried.start(xi, xo, *sems)

        body(*ins, *outs, *scr)

        @pl.when(last)
        def _():
            carried.finish(xi, xo, *sems)

    sems = [pltpu.SemaphoreType.DMA((carried.n_sem,)), pltpu.SemaphoreType.DMA((carried.n_sem,))]
    return (wrapped, [HBM_SPEC] * nx_in, [HBM_SPEC] * nx_out, list(carried.outs), sems, list(carried.ins))


def _ffn_bwd(tag, dy, saved, g_pre, wg, wu, wd, g_post, tm, tk, carried_down=None, make_carried_up=None):
    x, n, a, u, s, h = saved
    t, d = x.shape
    ns, fs, _ = wg.shape
    nt, nk = t // tm, t // tk
    row = pl.BlockSpec((tm, d), lambda i, q: (i, 0))
    vec = pl.BlockSpec((1, d), lambda i, q: (0, 0))
    act3 = pl.BlockSpec((1, tm, fs), lambda i, q: (q, i, 0))
    wrow = pl.BlockSpec((1, fs, d), lambda i, q: (q, 0, 0))
    nt_dims = (((1,), (1,)), ((), ()))
    tn_dims = (((0,), (0,)), ((), ()))

    def down_b(h_ref, dy_ref, g_ref, wd_ref, a_ref, u_ref, dh_ref, da_ref, du_ref, dg_ref, dh_s):
        i, q = pl.program_id(0), pl.program_id(1)

        @pl.when((i == 0) & (q == 0))
        def _():
            dg_ref[...] = jnp.zeros(dg_ref.shape, F32)

        @pl.when(q == 0)
        def _():
            for r in _row_chunks(tm):
                hb = h_ref[r, :]
                dh, dg = _rms_bwd(hb, _rms_stats(hb), g_ref[...], 0.5 * dy_ref[r, :])
                dh_s[r, :] = dh.astype(MM_DTYPE)
                dg_ref[...] += dg
            dh_ref[...] = dh_s[...]

        for r in _row_chunks(tm):
            ds = lax.dot_general(dh_s[r, :], wd_ref[0], nt_dims, preferred_element_type=F32)
            ab, ub = a_ref[0, r, :].astype(F32), u_ref[0, r, :].astype(F32)
            sg = _sigmoid(ab)
            da_ref[0, r, :] = (ds * ub * (sg * (1.0 + ab * (1.0 - sg)))).astype(da_ref.dtype)
            du_ref[0, r, :] = (ds * (ab * sg)).astype(du_ref.dtype)

    down_b, x_in, x_out, x_shape, x_scr, x_args = _carry(down_b, 6, 4, (nt, ns), carried_down)
    dh, da, du, dg_post, *from_down = pl.pallas_call(
        down_b, name=tag + "_down_b", grid=(nt, ns), in_specs=[row, row, vec, wrow, act3, act3] + x_in,
        out_specs=[row, act3, act3, vec] + x_out,
        out_shape=[jax.ShapeDtypeStruct((t, d), MM_DTYPE)] + [jax.ShapeDtypeStruct((ns, t, fs), MM_DTYPE)] * 2
        + [jax.ShapeDtypeStruct((1, d), F32)] + x_shape,
        scratch_shapes=[pltpu.VMEM((tm, d), MM_DTYPE)] + x_scr,
        compiler_params=_params(("arbitrary", "arbitrary")))(h, dy, g_post, wd, a, u, *x_args)

    def down_w(s_ref, dh_ref, dw_ref, acc):
        kk = pl.program_id(1)

        @pl.when(kk == 0)
        def _():
            acc[...] = jnp.zeros(acc.shape, F32)

        acc[...] += lax.dot_general(s_ref[0], dh_ref[...], tn_dims, preferred_element_type=F32)

        @pl.when(kk == nk - 1)
        def _():
            dw_ref[0] = acc[...].astype(dw_ref.dtype)

    dwd = pl.pallas_call(
        down_w, name=tag + "_down_w", grid=(ns, nk),
        in_specs=[pl.BlockSpec((1, tk, fs), lambda q, kk: (q, kk, 0)), pl.BlockSpec((tk, d), lambda q, kk: (kk, 0))],
        out_specs=pl.BlockSpec((1, fs, d), lambda q, kk: (q, 0, 0)),
        out_shape=jax.ShapeDtypeStruct((ns, fs, d), MM_DTYPE), scratch_shapes=[pltpu.VMEM((fs, d), F32)],
        compiler_params=_params(("parallel", "arbitrary")))(s, dh)

    def gate_up_b(da_ref, du_ref, wg_ref, wu_ref, x_ref, dy_ref, g_ref, dx_ref, dg_ref, acc):
        i, q = pl.program_id(0), pl.program_id(1)

        @pl.when((i == 0) & (q == 0))
        def _():
            dg_ref[...] = jnp.zeros(dg_ref.shape, F32)

        @pl.when(q == 0)
        def _():
            acc[...] = jnp.zeros(acc.shape, F32)

        for r in _row_chunks(tm):
            acc[r, :] += (jnp.dot(da_ref[0, r, :], wg_ref[0], preferred_element_type=F32)
                          + jnp.dot(du_ref[0, r, :], wu_ref[0], preferred_element_type=F32))

        @pl.when(q == ns - 1)
        def _():
            for r in _row_chunks(tm):
                xb = x_ref[r, :]
                dx, dg = _rms_bwd(xb, _rms_stats(xb), g_ref[...], acc[r, :])
                dx_ref[r, :] = dy_ref[r, :] + dx
                dg_ref[...] += dg

    def gate_up_w(n_ref, da_ref, du_ref, dwg_ref, dwu_ref, acc_g, acc_u):
        kk = pl.program_id(1)

        @pl.when(kk == 0)
        def _():
            acc_g[...] = jnp.zeros(acc_g.shape, F32)
            acc_u[...] = jnp.zeros(acc_u.shape, F32)

        nb = n_ref[...]
        acc_g[...] += lax.dot_general(da_ref[0], nb, tn_dims, preferred_element_type=F32)
        acc_u[...] += lax.dot_general(du_ref[0], nb, tn_dims, preferred_element_type=F32)

        @pl.when(kk == nk - 1)
        def _():
            dwg_ref[0] = acc_g[...].astype(dwg_ref.dtype)
            dwu_ref[0] = acc_u[...].astype(dwu_ref.dtype)

    k3 = pl.BlockSpec((1, tk, fs), lambda q, kk: (q, kk, 0))
    wout = pl.BlockSpec((1, fs, d), lambda q, kk: (q, 0, 0))
    dwg, dwu = pl.pallas_call(
        gate_up_w, name=tag + "_gate_up_w", grid=(ns, nk),
        in_specs=[pl.BlockSpec((tk, d), lambda q, kk: (kk, 0)), k3, k3], out_specs=[wout, wout],
        out_shape=[jax.ShapeDtypeStruct((ns, fs, d), MM_DTYPE)] * 2,
        scratch_shapes=[pltpu.VMEM((fs, d), F32)] * 2,
        compiler_params=_params(("parallel", "arbitrary")))(n, da, du)

    carried_up = make_carried_up(dwg, dwu, dwd) if make_carried_up else None
    gate_up_b, x_in, x_out, x_shape, x_scr, x_args = _carry(gate_up_b, 7, 2, (nt, ns), carried_up)
    dx, dg_pre, *from_up = pl.pallas_call(
        gate_up_b, name=tag + "_gate_up_b", grid=(nt, ns), in_specs=[act3, act3, wrow, wrow, row, row, vec] + x_in,
        out_specs=[row, vec] + x_out,
        out_shape=[jax.ShapeDtypeStruct((t, d), F32), jax.ShapeDtypeStruct((1, d), F32)] + x_shape,
        scratch_shapes=[pltpu.VMEM((tm, d), F32)] + x_scr,
        compiler_params=_params(("arbitrary", "arbitrary")))(da, du, wg, wu, x, dy, g_pre, *x_args)
    return dx, dg_pre, dwg, dwu, dwd, dg_post, from_down, from_up


NEG = -1e30


def _attn_scale():
    return (MLA_NOPE + MLA_ROPE) ** -0.5


def _causal_pairs(nq, by_key):
    if by_key:
        pairs = [(qi, ki) for ki in range(nq) for qi in range(ki, nq)]
    else:
        pairs = [(qi, ki) for qi in range(nq) for ki in range(qi + 1)]
    return jnp.asarray([p[0] for p in pairs], jnp.int32), jnp.asarray([p[1] for p in pairs], jnp.int32)


def _below_diagonal(shape):
    return lax.broadcasted_iota(jnp.int32, shape, 1) <= lax.broadcasted_iota(jnp.int32, shape, 0)


def _attn_call(name, body, tables, args, in_kinds, out_kinds, scratch, t, tq, carried=None):
    qmap = lambda h, p, qt, kt: (qt[p], h)
    kmap = lambda h, p, qt, kt: (kt[p], h)
    width = HEADS_PER_STEP * HEAD_LANES
    spec = lambda kind: pl.BlockSpec((tq, width), qmap if kind == "q" else kmap)
    n_pairs = tables[0].shape[0]
    n_groups = N_HEADS // HEADS_PER_STEP
    n_in, n_out, n_scr = len(in_kinds), len(out_kinds), scratch
    x_ins = list(carried.ins) if carried else []
    x_outs = list(carried.outs) if carried else []
    x_scr = [pltpu.SemaphoreType.DMA((carried.n_sem,)), pltpu.SemaphoreType.DMA((carried.n_sem,))] if carried else []

    def full_body(qt, kt, *refs):
        ins, refs = refs[:n_in], refs[n_in:]
        xi, refs = refs[:len(x_ins)], refs[len(x_ins):]
        outs, refs = refs[:n_out], refs[n_out:]
        xo, refs = refs[:len(x_outs)], refs[len(x_outs):]
        scr, sems = refs[:n_scr], refs[n_scr:]
        if carried:
            @pl.when((pl.program_id(0) == 0) & (pl.program_id(1) == 0))
            def _():
                carried.start(xi, xo, *sems)

        heads = [tuple(r.at[:, pl.ds(hh * HEAD_LANES, HEAD_LANES)] for r in (*ins, *outs, *scr))
                 for hh in range(HEADS_PER_STEP)]
        body(qt, kt, heads)
        if carried:
            @pl.when((pl.program_id(0) == n_groups - 1) & (pl.program_id(1) == n_pairs - 1))
            def _():
                carried.finish(xi, xo, *sems)

    grid_spec = pltpu.PrefetchScalarGridSpec(
        num_scalar_prefetch=2, grid=(n_groups, n_pairs),
        in_specs=[spec(kd) for kd in in_kinds] + [HBM_SPEC] * len(x_ins),
        out_specs=[spec(kd) for kd in out_kinds] + [HBM_SPEC] * len(x_outs),
        scratch_shapes=[pltpu.VMEM((tq, width), F32)] * n_scr + x_scr)
    return pl.pallas_call(full_body, name=name, grid_spec=grid_spec,
                          out_shape=[jax.ShapeDtypeStruct((t, MLA_PAD), F32) for _ in out_kinds] + x_outs,
                          compiler_params=_params(("arbitrary", "arbitrary")))(*tables, *args, *x_ins)


class _Carried:
    def __init__(self, ins, outs, n_sem, start, finish):
        self.ins, self.outs, self.n_sem, self.start, self.finish = ins, outs, n_sem, start, finish


def _attn_fwd(q, k, v, tq, carried=None):
    t = q.shape[0]
    nq = t // tq

    def body(qt, kt, heads):
        p_id = pl.program_id(1)
        qi, ki = qt[p_id], kt[p_id]

        @pl.when(ki == 0)
        def _():
            for _, _, _, _, _, m_s, l_s, acc_s in heads:
                m_s[...] = jnp.full(m_s.shape, NEG, F32)
                l_s[...] = jnp.zeros(l_s.shape, F32)
                acc_s[...] = jnp.zeros(acc_s.shape, F32)

        def update(diagonal):
            for q_ref, k_ref, v_ref, _, _, m_s, l_s, acc_s in heads:
                s = lax.dot_general(q_ref[...], k_ref[...], (((1,), (1,)), ((), ())), preferred_element_type=F32)
                if diagonal:
                    s = jnp.where(_below_diagonal(s.shape), s, NEG)
                m_old = m_s[...]
                m_new = jnp.maximum(m_old, jnp.max(s, axis=1, keepdims=True))
                alpha = jnp.exp(m_old - m_new)
                p = jnp.exp(s - m_new[:, :1])
                l_s[...] = l_s[...] * alpha + jnp.sum(p, axis=1, keepdims=True)
                acc_s[...] = acc_s[...] * alpha + jnp.dot(p.astype(MM_DTYPE), v_ref[...], preferred_element_type=F32)
                m_s[...] = m_new

        @pl.when(ki < qi)
        def _():
            update(False)

        @pl.when(ki == qi)
        def _():
            update(True)
            for _, _, _, o_ref, lse_ref, m_s, l_s, acc_s in heads:
                o_ref[...] = acc_s[...] / l_s[...]
                lse_ref[...] = m_s[...] + jnp.log(l_s[...])

    return _attn_call("mla_attn_fwd", body, _causal_pairs(nq, False), (q, k, v), "qkk", "qq", 3, t, tq, carried)


def _attn_probs(q, k, lse, diagonal):
    s = lax.dot_general(q, k, (((1,), (1,)), ((), ())), preferred_element_type=F32)
    p = jnp.exp(s - lse[:, :1])
    return jnp.where(_below_diagonal(s.shape), p, 0.0) if diagonal else p


def _attn_bwd_dq(q, k, v, do, lse, delta, tq):
    t = q.shape[0]
    nq = t // tq

    def body(qt, kt, heads):
        p_id = pl.program_id(1)
        qi, ki = qt[p_id], kt[p_id]

        @pl.when(ki == 0)
        def _():
            for refs in heads:
                refs[-1][...] = jnp.zeros(refs[-1].shape, F32)

        def step(diagonal):
            for q_ref, k_ref, v_ref, do_ref, lse_ref, dl_ref, _, acc_s in heads:
                p = _attn_probs(q_ref[...], k_ref[...], lse_ref[...], diagonal)
                dp = lax.dot_general(do_ref[...], v_ref[...], (((1,), (1,)), ((), ())), preferred_element_type=F32)
                ds = p * (dp - dl_ref[...][:, :1])
                acc_s[...] += jnp.dot(ds.astype(MM_DTYPE), k_ref[...], preferred_element_type=F32)

        @pl.when(ki < qi)
        def _():
            step(False)

        @pl.when(ki == qi)
        def _():
            step(True)
            for refs in heads:
                refs[-2][...] = refs[-1][...]

    return _attn_call("mla_attn_bwd_dq", body, _causal_pairs(nq, False), (q, k, v, do, lse, delta), "qkkqqq", "q",
                      1, t, tq)[0]


def _attn_bwd_dkv(q, k, v, do, lse, delta, tq, carried=None):
    t = q.shape[0]
    nq = t // tq

    def body(qt, kt, heads):
        p_id = pl.program_id(1)
        qi, ki = qt[p_id], kt[p_id]

        def step(diagonal):
            for q_ref, k_ref, v_ref, do_ref, lse_ref, dl_ref, _, _, dk_s, dv_s in heads:
                p = _attn_probs(q_ref[...], k_ref[...], lse_ref[...], diagonal)
                dv_s[...] += lax.dot_general(p.astype(MM_DTYPE), do_ref[...], (((0,), (0,)), ((), ())),
                                             preferred_element_type=F32)
                dp = lax.dot_general(do_ref[...], v_ref[...], (((1,), (1,)), ((), ())), preferred_element_type=F32)
                ds = p * (dp - dl_ref[...][:, :1])
                dk_s[...] += lax.dot_general(ds.astype(MM_DTYPE), q_ref[...], (((0,), (0,)), ((), ())),
                                             preferred_element_type=F32)

        @pl.when(qi == ki)
        def _():
            for refs in heads:
                refs[-2][...] = jnp.zeros(refs[-2].shape, F32)
                refs[-1][...] = jnp.zeros(refs[-1].shape, F32)
            step(True)

        @pl.when(qi > ki)
        def _():
            step(False)

        @pl.when(qi == nq - 1)
        def _():
            for refs in heads:
                refs[-4][...] = refs[-2][...]
                refs[-3][...] = refs[-1][...]

    return _attn_call("mla_attn_bwd_dkv", body, _causal_pairs(nq, True), (q, k, v, do, lse, delta), "qkkqqq", "kk",
                      2, t, tq, carried)


def _dot01(a, b, dims=(((1,), (0,)), ((), ())), ones="rhs"):
    val, sel = (a, b) if ones == "rhs" else (b, a)
    head = val.astype(BF16)
    tail = (val - head.astype(F32)).astype(BF16)
    sel = sel.astype(BF16)
    dot = lambda part: (lax.dot_general(part, sel, dims, preferred_element_type=F32) if ones == "rhs"
                        else lax.dot_general(sel, part, dims, preferred_element_type=F32))
    return dot(head) + dot(tail)


def _dot1(a, b, dims=(((1,), (0,)), ((), ()))):
    return lax.dot_general(a.astype(MM_DTYPE), b.astype(MM_DTYPE), dims, preferred_element_type=F32)


def _dot3(a, b, dims=(((1,), (0,)), ((), ()))):
    return lax.dot_general(a, b, dims, preferred_element_type=F32, precision=lax.Precision.HIGH)


NN3 = (((2,), (1,)), ((0,), (0,)))
NT3 = (((2,), (2,)), ((0,), (0,)))
TN3 = (((1,), (1,)), ((0,), (0,)))


def _tri_masks(nh):
    shape = (nh, CHUNK, CHUNK)
    return lax.broadcasted_iota(jnp.int32, shape, 1), lax.broadcasted_iota(jnp.int32, shape, 2)


def _gdn_chunk_common(k, gcc, bb, row, col, dot=_dot1):
    tril = row >= col
    gcr = jnp.swapaxes(gcc, 1, 2)
    dm = jnp.exp(jnp.where(tril, gcc - gcr, NEG))
    kb = k * bb
    lm = jnp.where(row > col, dot(kb, k, NT3) * dm, 0.0)
    return dm, kb, lm


def _unit_lower_inverse(lm, eye):
    t = eye - lm
    p = lm
    for _ in range(CHUNK.bit_length() - 2):
        p = _dot3(p, p, NN3)
        t = t + _dot3(t, p, NN3)
    return t


def _chunk_sum_matrix(tb, upper):
    r = lax.broadcasted_iota(jnp.int32, (tb, tb), 0)
    c = lax.broadcasted_iota(jnp.int32, (tb, tb), 1)
    same = (r // CHUNK) == (c // CHUNK)
    return (same & ((c >= r) if upper else (c <= r))).astype(F32)


def _gdn_fwd(q, k, v, gb, bb):
    nh, t, dh = q.shape
    nchunk = t // CHUNK

    def body(q_ref, k_ref, v_ref, g_ref, b_ref, o_ref, sall_ref, tall_ref, s_s):
        @pl.when(pl.program_id(0) == 0)
        def _():
            s_s[...] = jnp.zeros(s_s.shape, F32)

        row, col = _tri_masks(nh)
        qh, kh, vh, bbh, gcc = q_ref[...], k_ref[...], v_ref[...], b_ref[...], g_ref[...]
        dm, kb, lm = _gdn_chunk_common(kh, gcc, bbh, row, col)
        eg = jnp.exp(gcc)
        glr = gcc[:, CHUNK - 1:CHUNK, :]
        th = _unit_lower_inverse(lm, (row == col).astype(F32))
        w = _dot1(th, kb * eg, NN3)
        u = _dot1(th, vh * bbh, NN3)
        at = jnp.where(row >= col, _dot1(qh, kh, NT3) * dm, 0.0)
        sh = s_s[...]
        vn = u - _dot1(w, sh, NN3)
        o_ref[...] = _dot1(qh * eg, sh, NN3) + _dot1(at, vn, NN3)
        kd = kh * jnp.exp(glr - gcc)
        sall_ref[:, 0] = sh
        tall_ref[...] = th
        s_s[...] = sh * jnp.exp(glr) + _dot1(kd, vn, TN3)

    blk = pl.BlockSpec((nh, CHUNK, dh), lambda n: (0, n, 0))
    return pl.pallas_call(
        body, name="gdn_fwd", grid=(nchunk,), in_specs=[blk] * 5,
        out_specs=[blk, pl.BlockSpec((nh, 1, dh, dh), lambda n: (0, n, 0, 0)), blk],
        out_shape=[jax.ShapeDtypeStruct((nh, t, dh), F32), jax.ShapeDtypeStruct((nh, nchunk, dh, dh), F32),
                   jax.ShapeDtypeStruct((nh, t, CHUNK), F32)],
        scratch_shapes=[pltpu.VMEM((nh, dh, dh), F32)],
        compiler_params=_params(("arbitrary",)))(q, k, v, gb, bb)


def _gdn_bwd(q, k, v, gb, bb, sall, tall, do):
    nh, t, dh = q.shape
    nchunk = t // CHUNK

    def body(q_ref, k_ref, v_ref, g_ref, b_ref, sall_ref, tall_ref, do_ref,
             dq_ref, dk_ref, dv_ref, dg_ref, db_ref, ds_s):
        @pl.when(pl.program_id(0) == 0)
        def _():
            ds_s[...] = jnp.zeros(ds_s.shape, F32)

        row, col = _tri_masks(nh)
        tril, stril = row >= col, row > col
        rsum = lambda x: jnp.sum(x, axis=2, keepdims=True)
        qh, kh, vh, gcc, bbh = q_ref[...], k_ref[...], v_ref[...], g_ref[...], b_ref[...]
        sh, th, doh, dsp = sall_ref[:, 0], tall_ref[...], do_ref[...], ds_s[...]
        dm, kb, lm = _gdn_chunk_common(kh, gcc, bbh, row, col, _dot3)
        eg = jnp.exp(gcc)
        glr = gcc[:, CHUNK - 1:CHUNK, :]
        glv = jnp.exp(glr)
        egl = jnp.exp(glr - gcc)
        rw, ru = kb * eg, vh * bbh
        w, u = _dot3(th, rw, NN3), _dot3(th, ru, NN3)
        at = jnp.where(tril, _dot3(qh, kh, NT3) * dm, 0.0)
        qd, kd = qh * eg, kh * egl
        vn = u - _dot3(w, sh, NN3)
        dgl = jnp.sum(rsum(dsp * sh), axis=1, keepdims=True)
        dkd = _dot3(vn, dsp, NT3)
        dvn = _dot3(kd, dsp, NN3)
        dqd = _dot3(doh, sh, NT3)
        dat = jnp.where(tril, _dot3(doh, vn, NT3), 0.0)
        dvn = dvn + _dot3(at, doh, TN3)
        dw = -_dot3(dvn, sh, NT3)
        ds_s[...] = dsp * glv + _dot3(qd, doh, TN3) - _dot3(w, dvn, TN3)
        dpa = dat * dm
        dq_ref[...] = _dot1(dpa, kh, NN3) + dqd * eg
        dk = _dot1(dpa, qh, TN3) + dkd * egl
        t6 = rsum(dkd * kd)
        dgam = rsum(dqd * qd) - t6
        dgam_last = jnp.sum(t6, axis=1, keepdims=True) + dgl * glv
        drw = _dot3(th, dw, TN3)
        dru = _dot3(th, dvn, TN3)
        dl = -jnp.where(stril, _dot3(drw, w, NT3) + _dot3(dru, u, NT3), 0.0)
        dgam = dgam + rsum(drw * rw)
        dv_ref[...] = dru * bbh
        dp2 = dl * dm
        dkb = drw * eg + _dot1(dp2, kh, NN3)
        dk_ref[...] = dk + _dot1(dp2, kb, TN3) + dkb * bbh
        db_ref[...] = rsum(dru * vh) + rsum(dkb * kh) + jnp.zeros((nh, CHUNK, dh), F32)
        e = dat * at + dl * lm
        dgam_b = dgam + rsum(e) - _dot01(e, jnp.ones((nh, CHUNK, CHUNK), F32), TN3)
        dgam_b = dgam_b + jnp.where(row == CHUNK - 1, dgam_last, 0.0)
        dg_ref[...] = dgam_b

    rev = lambda n: (0, nchunk - 1 - n, 0)
    blk = pl.BlockSpec((nh, CHUNK, dh), rev)
    sblk = pl.BlockSpec((nh, 1, dh, dh), lambda n: (0, nchunk - 1 - n, 0, 0))
    out = jax.ShapeDtypeStruct((nh, t, dh), F32)
    return pl.pallas_call(
        body, name="gdn_bwd", grid=(nchunk,), in_specs=[blk] * 5 + [sblk, blk, blk], out_specs=[blk] * 5,
        out_shape=[out] * 5, scratch_shapes=[pltpu.VMEM((nh, dh, dh), F32)],
        compiler_params=_params(("arbitrary",)))(q, k, v, gb, bb, sall, tall, do)


def _group_ones():
    r = lax.broadcasted_iota(jnp.int32, (GDN_W, GDN_W), 0) // GDN_DH
    c = lax.broadcasted_iota(jnp.int32, (GDN_W, GDN_W), 1) // GDN_DH
    return (r == c).astype(F32)


def _conv_taps(x, xprev, w, has_prev):
    row = lax.broadcasted_iota(jnp.int32, x.shape, 0)
    out = x * w[GDN_CONV - 1:GDN_CONV, :]
    for s in range(1, GDN_CONV):
        sh = jnp.where(row >= s, _roll(x, s, 0), _roll(xprev, s, 0) * has_prev)
        out = out + sh * w[GDN_CONV - 1 - s:GDN_CONV - s, :]
    return out


def _head_cols(x, h):
    return x[:, h * GDN_DH:(h + 1) * GDN_DH]


def _heads_spec(tb):
    return pl.BlockSpec((N_HEADS, tb, GDN_DH), lambda i: (0, i, 0))


def _mixer_fwd(x, positions, w, tb, carried=None):
    t, d = x.shape
    tables = _rope_tables(positions)

    def pre(xb, g):
        return (xb * _rms_stats(xb) * g,)

    (hn,) = _rowwise("mix_pre", pre, [x], [w["mix_pre_g"]], [(d, BF16)], [], tb)
    proj = _mm("mix_in", hn, w["w_in_pad_t"], "nt", F32)

    def mla_pre(p0, gq, gkv):
        cq, ckv = p0[:, :MLA_Q_RANK], p0[:, MLA_Q_RANK:MLA_Q_RANK + MLA_KV_RANK]
        return cq * _rms_stats(cq) * gq, ckv * _rms_stats(ckv) * gkv

    nq, nkv = _rowwise("mla_pre", mla_pre, [(proj, 512, PIN_MLA // 512, 0)],
                       [w["mla_q_norm_g"], w["mla_kv_norm_g"]], [(MLA_Q_RANK, BF16), (MLA_KV_RANK, BF16)], [], tb)
    qraw = _mm("mla_uq", nq, w["w_uq_pad"], "nn", F32)
    kv = _mm("mla_ukv", nkv, w["w_kv_pad"], "nn", F32)

    def rope_f(qr, kn, vv, kpe, c, s1, s2):
        qo = _heads_apply(qr, lambda xh: _rope(xh, c, s1, s2)) * _attn_scale()
        kp = _rope(kpe, c, s1, s2)
        return qo, kn + jnp.tile(kp, (1, N_HEADS)), vv

    q, k, v = _rowwise("mla_rope", rope_f,
                       [qraw, (kv, MLA_PAD, 0, 0), (kv, MLA_PAD, 1, 0), (proj, HEAD_LANES, PIN_KPE // HEAD_LANES, 0),
                        tables[0], tables[1], tables[2]], [],
                       [(MLA_PAD, BF16)] * 3, [], tb // 2)
    tq = min(512, t)
    o, lse, *carried_out = _attn_fwd(q, k, v, tq, carried)

    def mla_post(ob, g):
        return (ob * _rms_stats(ob, N_HEADS * MLA_V) * g,)

    (cat,) = _rowwise("mla_post", mla_post, [o], [w["mla_out_g_pad"]], [(MLA_PAD, BF16)], [], tb, wide=(CAT_W, 0))

    gones = _group_ones()
    steps = t // tb

    def gdn_pre(xq, xk, xv, pq, pk, pv, cw, go, has_prev):
        outs = []
        for j, (xc, xp) in enumerate(((xq, pq), (xk, pk), (xv, pv))):
            c = _conv_taps(xc, xp, cw[:, j * GDN_W:(j + 1) * GDN_W], has_prev)
            a = c * _sigmoid(c)
            if j < 2:
                rn = lax.rsqrt(_dot01(a * a, go) + EPS)
                a = a * rn
                if j == 0:
                    a = a * (GDN_DH ** -0.5)
            outs.append(a)
        return tuple(outs)

    qh, kh, vh = _gdn_pre_call("gdn_pre", gdn_pre, proj, w["conv_w"], gones, tb, steps)
    heads_shape = jax.ShapeDtypeStruct((N_HEADS, t, GDN_DH), F32)
    lanes_shape = jax.ShapeDtypeStruct((t, HEAD_LANES), F32)
    lanes_spec = pl.BlockSpec((tb, HEAD_LANES), lambda i: (i, 0))
    vec_spec = lambda n: pl.BlockSpec((1, n), lambda i: (0, 0))

    def gate_f(ab_ref, al_ref, dt_ref, g_ref, b_ref, gh_ref, bh_ref):
        g, b = _gb_fwd(ab_ref[...], al_ref[...], dt_ref[...])
        g_ref[...] = g
        b_ref[...] = b
        gc = _dot01(_chunk_sum_matrix(tb, False), g, ones="lhs")
        for h in range(N_HEADS):
            gh_ref[h] = jnp.broadcast_to(gc[:, h:h + 1], (tb, GDN_DH))
            bh_ref[h] = jnp.broadcast_to(b[:, N_HEADS + h:N_HEADS + h + 1], (tb, GDN_DH))

    g128, b128, gbh, bbh = pl.pallas_call(
        gate_f, name="gdn_gate_f", grid=(steps,),
        in_specs=[pl.BlockSpec((tb, HEAD_LANES), lambda i: (i, PIN_AB // HEAD_LANES)), vec_spec(HEAD_LANES),
                  vec_spec(HEAD_LANES)],
        out_specs=[lanes_spec, lanes_spec, _heads_spec(tb), _heads_spec(tb)],
        out_shape=[lanes_shape, lanes_shape, heads_shape, heads_shape],
        compiler_params=_params(("arbitrary",)))(proj, w["a_log_pad"], w["dt_bias_pad"])
    oh, sall, tall = _gdn_fwd(qh, kh, vh, gbh, bbh)

    def gdn_post(o_ref, gt_ref, g_ref, cat_in, cat_ref):
        gt, g = gt_ref[...], g_ref[...]
        outs = []
        for h in range(N_HEADS):
            ob, gth = o_ref[h], _head_cols(gt, h)
            outs.append(ob * _rms_stats(ob) * g * (gth * _sigmoid(gth)))
        cat_ref[...] = jnp.concatenate(outs, axis=1).astype(cat_ref.dtype)

    gate_spec = pl.BlockSpec((tb, GDN_W), lambda i: (i, PIN_GATE // GDN_W))
    cat = pl.pallas_call(
        gdn_post, name="gdn_post", grid=(steps,),
        in_specs=[_heads_spec(tb), gate_spec, vec_spec(GDN_DH), ANY_SPEC],
        out_specs=pl.BlockSpec((tb, GDN_W), lambda i: (i, MLA_PAD // GDN_W)),
        out_shape=jax.ShapeDtypeStruct((t, CAT_W), BF16), input_output_aliases={3: 0},
        compiler_params=_params(("arbitrary",)))(oh, proj, w["gdn_norm_g"], cat)
    mixed = _mm("mix_out", cat, w["w_out_pad"], "nn", F32)

    def post(xb, hb, g):
        return (xb + hb * _rms_stats(hb) * g,)

    (y,) = _rowwise("mix_post", post, [x, mixed], [w["mix_post_g"]], [(d, F32)], [], tb)
    saved = dict(x=x, hn=hn, proj=proj, nq=nq, nkv=nkv, q=q, k=k, v=v, o=o, lse=lse, qh=qh, kh=kh, vh=vh,
                 gbh=gbh, bbh=bbh, oh=oh, sall=sall, tall=tall, cat=cat, mixed=mixed,
                 tables=tables, g128=g128, b128=b128)
    return y, saved, carried_out


def _qkv_specs(tb):
    base = PIN_QKV // GDN_W
    cur = [pl.BlockSpec((tb, GDN_W), lambda i, j=j: (i, base + j)) for j in range(3)]
    prev = [pl.BlockSpec((tb, GDN_W), lambda i, j=j: (jnp.maximum(i - 1, 0), base + j)) for j in range(3)]
    return cur + prev


def _gdn_pre_call(name, fn, proj, conv_w, gones, tb, steps):
    t = proj.shape[0]

    def body(xq, xk, xv, pq, pk, pv, cw, go, oq, ok, ov):
        has_prev = jnp.where(pl.program_id(0) == 0, 0.0, 1.0)
        outs = fn(xq[...], xk[...], xv[...], pq[...], pk[...], pv[...], cw[...], go[...], has_prev)
        for r, val in zip((oq, ok, ov), outs):
            for h in range(N_HEADS):
                r[h] = _head_cols(val, h)

    return pl.pallas_call(
        body, name=name, grid=(steps,),
        in_specs=_qkv_specs(tb) + [pl.BlockSpec(conv_w.shape, lambda i: (0, 0)),
                                   pl.BlockSpec(gones.shape, lambda i: (0, 0))],
        out_specs=[_heads_spec(tb)] * 3,
        out_shape=[jax.ShapeDtypeStruct((N_HEADS, t, GDN_DH), F32)] * 3,
        compiler_params=_params(("arbitrary",)))(proj, proj, proj, proj, proj, proj, conv_w, gones)


def _softplus(x):
    return jnp.maximum(x, 0.0) + jnp.log1p(jnp.exp(-jnp.abs(x)))


def _gb_fwd(ab, a_log, dt_bias):
    g = -jnp.exp(a_log) * _softplus(ab + dt_bias)
    return g, _sigmoid(ab)


def _rope_tables(positions):
    half = MLA_ROPE // 2
    freqs = ROPE_THETA ** (-jnp.arange(half, dtype=F32) / half)
    ang = positions.reshape(-1).astype(F32)[:, None] * freqs
    cos, sin = jnp.cos(ang), jnp.sin(ang)
    t = ang.shape[0]
    one = jnp.ones((t, MLA_NOPE), F32)
    z16, z32, z64 = jnp.zeros((t, half), F32), jnp.zeros((t, MLA_ROPE), F32), jnp.zeros((t, MLA_NOPE), F32)
    c = jnp.concatenate([one, cos, cos, jnp.ones((t, MLA_ROPE), F32)], axis=1)
    s1 = jnp.concatenate([z64, -sin, z16, z32], axis=1)
    s2 = jnp.concatenate([z64, z16, sin, z32], axis=1)
    return c, s1, s2


def _mixer_bwd(dy, sv, w, tb, carried=None):
    x, proj = sv["x"], sv["proj"]
    t, d = x.shape
    c, s1, s2 = sv["tables"]
    grads = {}

    def post_b(hb, dyb, g):
        return _rms_bwd(hb, _rms_stats(hb), g, dyb)

    dmixed, grads["mix_post_g"] = _rowwise("mix_post_b", post_b, [sv["mixed"], dy], [w["mix_post_g"]],
                                           [(d, BF16)], [(1, d)], tb)
    dcat = _mm("mix_out_bx", dmixed, w["w_out_pad"], "nt", F32)
    grads["w_out_pad"] = _mm("mix_out_bw", sv["cat"], dmixed, "tn", F32)
    steps = t // tb
    vec_spec = lambda n: pl.BlockSpec((1, n), lambda i: (0, 0))

    def gdn_post_b(o_ref, gt_ref, do_ref, g_ref, dproj_ref, doh_ref, dg_ref):
        @pl.when(pl.program_id(0) == 0)
        def _():
            dg_ref[...] = jnp.zeros(dg_ref.shape, F32)

        gt, dob, g = gt_ref[...], do_ref[...], g_ref[...]
        dgates = []
        for h in range(N_HEADS):
            ob, gth, dobh = o_ref[h], _head_cols(gt, h), _head_cols(dob, h)
            sg = _sigmoid(gth)
            r = _rms_stats(ob)
            dxo, dg = _rms_bwd(ob, r, g, dobh * (gth * sg))
            doh_ref[h] = dxo
            dg_ref[...] += dg
            dgates.append(dobh * (ob * r * g) * (sg * (1.0 + gth * (1.0 - sg))))
        dproj_ref[...] = jnp.concatenate(dgates, axis=1).astype(dproj_ref.dtype)

    dproj, doh, grads["gdn_norm_g"] = pl.pallas_call(
        gdn_post_b, name="gdn_post_b", grid=(steps,),
        in_specs=[_heads_spec(tb), pl.BlockSpec((tb, GDN_W), lambda i: (i, PIN_GATE // GDN_W)),
                  pl.BlockSpec((tb, GDN_W), lambda i: (i, MLA_PAD // GDN_W)), vec_spec(GDN_DH)],
        out_specs=[pl.BlockSpec((tb, GDN_W), lambda i: (i, PIN_GATE // GDN_W)), _heads_spec(tb), vec_spec(GDN_DH)],
        out_shape=[jax.ShapeDtypeStruct((t, PIN_W), BF16), jax.ShapeDtypeStruct((N_HEADS, t, GDN_DH), F32),
                   jax.ShapeDtypeStruct((1, GDN_DH), F32)],
        compiler_params=_params(("arbitrary",)))(sv["oh"], proj, dcat, w["gdn_norm_g"])

    def mla_post_b(ob, dmo, g):
        do, dg = _rms_bwd(ob, _rms_stats(ob, N_HEADS * MLA_V), g, dmo, N_HEADS * MLA_V)
        prod = do * ob
        delta = _heads_apply(prod, lambda ph: jnp.sum(ph, axis=1, keepdims=True) + jnp.zeros_like(ph))
        return do, delta, dg

    do, delta, grads["mla_out_g_pad"] = _rowwise(
        "mla_post_b", mla_post_b, [sv["o"], (dcat, MLA_PAD, 0, 0)], [w["mla_out_g_pad"]],
        [(MLA_PAD, BF16), (MLA_PAD, F32)], [(1, MLA_PAD)], tb // 2)
    tq = min(512, t)
    dq = _attn_bwd_dq(sv["q"], sv["k"], sv["v"], do, sv["lse"], delta, tq)
    dk, dv, *carried_out = _attn_bwd_dkv(sv["q"], sv["k"], sv["v"], do, sv["lse"], delta, tq, carried)

    def rope_b(dqb, dkb, dvb, cc, a1, a2):
        dqr = _heads_apply(dqb * _attn_scale(), lambda xh: _rope(xh, cc, -a1, -a2))
        ksum = dkb[:, :HEAD_LANES]
        for h in range(1, N_HEADS):
            ksum = ksum + dkb[:, h * HEAD_LANES:(h + 1) * HEAD_LANES]
        lane = lax.broadcasted_iota(jnp.int32, ksum.shape, 1)
        keep = (lane >= MLA_NOPE) & (lane < MLA_NOPE + MLA_ROPE)
        dkpe = jnp.where(keep, _rope(ksum, cc, -a1, -a2), 0.0)
        return dqr, jnp.concatenate([dkb, dvb], axis=1), dkpe

    dqraw, dkv, dkpe = _rowwise("mla_rope_b", rope_b, [dq, dk, dv, c, s1, s2], [],
                                [(MLA_PAD, BF16), (2 * MLA_PAD, BF16), (HEAD_LANES, F32)], [], tb // 2)
    dnq = _mm("mla_uq_bx", dqraw, w["w_uq_pad"], "nt", F32)
    grads["w_uq_pad"] = _mm("mla_uq_bw", sv["nq"], dqraw, "tn", F32)
    dnkv = _mm("mla_ukv_bx", dkv, w["w_kv_pad"], "nt", F32)
    grads["w_kv_pad"] = _mm("mla_ukv_bw", sv["nkv"], dkv, "tn", F32)

    def mla_pre_b(p0, dnqb, dnkvb, dkpeb, gq, gkv):
        cq, ckv = p0[:, :MLA_Q_RANK], p0[:, MLA_Q_RANK:MLA_Q_RANK + MLA_KV_RANK]
        dcq, dgq = _rms_bwd(cq, _rms_stats(cq), gq, dnqb)
        dckv, dgkv = _rms_bwd(ckv, _rms_stats(ckv), gkv, dnkvb)
        return jnp.concatenate([dcq, dckv, dkpeb], axis=1), dgq, dgkv

    dproj, grads["mla_q_norm_g"], grads["mla_kv_norm_g"] = _rowwise(
        "mla_pre_b", mla_pre_b, [(proj, 512, PIN_MLA // 512, 0), dnq, dnkv, dkpe],
        [w["mla_q_norm_g"], w["mla_kv_norm_g"]], [(512, BF16)], [(1, MLA_Q_RANK), (1, MLA_KV_RANK)], tb,
        wide=(PIN_W, PIN_MLA // 512), carry=dproj)

    dqh, dkh, dvh, dgh, dbh = _gdn_bwd(sv["qh"], sv["kh"], sv["vh"], sv["gbh"], sv["bbh"], sv["sall"], sv["tall"], doh)
    gones = _group_ones()

    def gdn_pre_b(xq, xk, xv, pq, pk, pv, dq_, dk_, dv_, cw, go, has_prev):
        outs = []
        for j, (xc, xp, dd) in enumerate(((xq, pq, dq_), (xk, pk, dk_), (xv, pv, dv_))):
            cc = _conv_taps(xc, xp, cw[:, j * GDN_W:(j + 1) * GDN_W], has_prev)
            sg = _sigmoid(cc)
            a = cc * sg
            if j < 2:
                rn = lax.rsqrt(_dot01(a * a, go) + EPS)
                if j == 0:
                    dd = dd * (GDN_DH ** -0.5)
                da = rn * dd - a * (rn * rn * rn) * _dot01(dd * a, go)
            else:
                da = dd
            outs.append(da * (sg * (1.0 + cc * (1.0 - sg))))
        return tuple(outs)

    dcq, dck, dcv = _gdn_pre_b_call("gdn_pre_b", gdn_pre_b, proj, (dqh, dkh, dvh), w["conv_w"], gones, tb, steps)
    dproj, grads["conv_w"] = _conv_bwd_call("gdn_conv_b", proj, (dcq, dck, dcv), w["conv_w"], dproj, tb, steps)

    def gate_b(ab_ref, g_ref, b_ref, dgh_ref, dbh_ref, al_ref, dt_ref, carry_ref, dab_ref, dal_ref, ddt_ref):
        @pl.when(pl.program_id(0) == 0)
        def _():
            dal_ref[...] = jnp.zeros(dal_ref.shape, F32)
            ddt_ref[...] = jnp.zeros(ddt_ref.shape, F32)

        ab, g128, b128 = ab_ref[...], g_ref[...], b_ref[...]
        lane = lax.broadcasted_iota(jnp.int32, ab.shape, 1)
        dg_ = jnp.zeros(ab.shape, F32)
        db_ = jnp.zeros(ab.shape, F32)
        for h in range(N_HEADS):
            dg_ = dg_ + jnp.where(lane == h, jnp.broadcast_to(dgh_ref[h][:, 0:1], ab.shape), 0.0)
            db_ = db_ + jnp.where(lane == N_HEADS + h, jnp.broadcast_to(dbh_ref[h][:, 0:1], ab.shape), 0.0)
        dg_ = _dot01(_chunk_sum_matrix(tb, True), dg_, ones="lhs")
        slope = -jnp.exp(al_ref[...]) * _sigmoid(ab + dt_ref[...])
        dab_ref[...] = (dg_ * slope + db_ * b128 * (1.0 - b128)).astype(dab_ref.dtype)
        dal_ref[...] += jnp.sum(dg_ * g128, axis=0, keepdims=True)
        ddt_ref[...] += jnp.sum(dg_ * slope, axis=0, keepdims=True)

    lanes_spec = pl.BlockSpec((tb, HEAD_LANES), lambda i: (i, 0))
    ab_spec = pl.BlockSpec((tb, HEAD_LANES), lambda i: (i, PIN_AB // HEAD_LANES))
    dproj, grads["a_log_pad"], grads["dt_bias_pad"] = pl.pallas_call(
        gate_b, name="gdn_gate_b", grid=(steps,),
        in_specs=[ab_spec, lanes_spec, lanes_spec, _heads_spec(tb), _heads_spec(tb), vec_spec(HEAD_LANES),
                  vec_spec(HEAD_LANES), ANY_SPEC],
        out_specs=[ab_spec, vec_spec(HEAD_LANES), vec_spec(HEAD_LANES)],
        out_shape=[jax.ShapeDtypeStruct((t, PIN_W), BF16), jax.ShapeDtypeStruct((1, HEAD_LANES), F32),
                   jax.ShapeDtypeStruct((1, HEAD_LANES), F32)],
        input_output_aliases={7: 0},
        compiler_params=_params(("arbitrary",)))(proj, sv["g128"], sv["b128"], dgh, dbh, w["a_log_pad"],
                                                 w["dt_bias_pad"], dproj)
    dhn = _mm("mix_in_bx", dproj, w["w_in_pad_t"], "nn", F32)
    grads["w_in_pad_t"] = _mm("mix_in_bw", dproj, sv["hn"], "tn", F32)

    def pre_b(xb, dnb, dyb, g):
        dx, dg = _rms_bwd(xb, _rms_stats(xb), g, dnb)
        return dyb + dx, dg

    dx, grads["mix_pre_g"] = _rowwise("mix_pre_b", pre_b, [x, dhn, dy], [w["mix_pre_g"]], [(d, F32)], [(1, d)], tb)
    return dx, grads, carried_out


def _gdn_pre_b_call(name, fn, proj, dd, conv_w, gones, tb, steps):
    t = proj.shape[0]

    def body(xq, xk, xv, pq, pk, pv, d0, d1, d2, cw, go, oq, ok, ov):
        has_prev = jnp.where(pl.program_id(0) == 0, 0.0, 1.0)
        dd_rows = [jnp.concatenate([dr[h] for h in range(N_HEADS)], axis=1) for dr in (d0, d1, d2)]
        outs = fn(xq[...], xk[...], xv[...], pq[...], pk[...], pv[...], *dd_rows, cw[...], go[...], has_prev)
        for r, val in zip((oq, ok, ov), outs):
            r[...] = val

    return pl.pallas_call(
        body, name=name, grid=(steps,),
        in_specs=_qkv_specs(tb) + [_heads_spec(tb)] * 3 + [pl.BlockSpec(conv_w.shape, lambda i: (0, 0)),
                                                          pl.BlockSpec(gones.shape, lambda i: (0, 0))],
        out_specs=[pl.BlockSpec((tb, GDN_W), lambda i: (i, 0))] * 3,
        out_shape=[jax.ShapeDtypeStruct((t, GDN_W), F32)] * 3,
        compiler_params=_params(("arbitrary",)))(proj, proj, proj, proj, proj, proj, *dd, conv_w, gones)


def _conv_bwd_call(name, proj, dc, conv_w, dproj, tb, steps):
    t = proj.shape[0]
    dcur = [pl.BlockSpec((tb, GDN_W), lambda i: (i, 0))] * 3
    dnext = [pl.BlockSpec((tb, GDN_W), lambda i: (jnp.minimum(i + 1, steps - 1), 0))] * 3

    def body(xq, xk, xv, pq, pk, pv, d0, d1, d2, n0, n1, n2, cw, carry_ref, dx_ref, dw_ref):
        i = pl.program_id(0)
        has_prev = jnp.where(i == 0, 0.0, 1.0)
        has_next = jnp.where(i == steps - 1, 0.0, 1.0)

        @pl.when(i == 0)
        def _():
            dw_ref[...] = jnp.zeros(dw_ref.shape, F32)

        wv = cw[...]
        dws, dxs = [], []
        for j, (xr, pr, dr, nr) in enumerate(((xq, pq, d0, n0), (xk, pk, d1, n1), (xv, pv, d2, n2))):
            x, xp, dcv, dnx = xr[...], pr[...], dr[...], nr[...]
            wj = wv[:, j * GDN_W:(j + 1) * GDN_W]
            row = lax.broadcasted_iota(jnp.int32, x.shape, 0)
            dx = dcv * wj[GDN_CONV - 1:GDN_CONV, :]
            rows_w = [jnp.sum(dcv * x, axis=0, keepdims=True)]
            for s in range(1, GDN_CONV):
                up = jnp.where(row < tb - s, _roll(dcv, tb - s, 0), _roll(dnx, tb - s, 0) * has_next)
                dx = dx + up * wj[GDN_CONV - 1 - s:GDN_CONV - s, :]
                sh = jnp.where(row >= s, _roll(x, s, 0), _roll(xp, s, 0) * has_prev)
                rows_w.append(jnp.sum(dcv * sh, axis=0, keepdims=True))
            dxs.append(dx)
            dws.append(jnp.concatenate(rows_w[::-1], axis=0))
        dx_ref[...] = jnp.concatenate(dxs, axis=1).astype(dx_ref.dtype)
        dw_ref[...] += jnp.concatenate(dws, axis=1)

    return pl.pallas_call(
        body, name=name, grid=(steps,),
        in_specs=_qkv_specs(tb) + dcur + dnext + [pl.BlockSpec(conv_w.shape, lambda i: (0, 0)), ANY_SPEC],
        out_specs=[pl.BlockSpec((tb, 3 * GDN_W), lambda i: (i, PIN_QKV // (3 * GDN_W))),
                   pl.BlockSpec(conv_w.shape, lambda i: (0, 0))],
        out_shape=[jax.ShapeDtypeStruct((t, PIN_W), BF16), jax.ShapeDtypeStruct(conv_w.shape, F32)],
        input_output_aliases={13: 0},
        compiler_params=_params(("arbitrary",)))(proj, proj, proj, proj, proj, proj, *dc, *dc, conv_w, dproj)


def _pad_heads_cols(wm, per_head):
    r = wm.shape[0]
    return jnp.pad(wm.reshape(r, N_HEADS, per_head), ((0, 0), (0, 0), (0, HEAD_LANES - per_head))).reshape(r, MLA_PAD)


def _unpad_heads_cols(wm, per_head):
    r = wm.shape[0]
    return wm.reshape(r, N_HEADS, HEAD_LANES)[:, :, :per_head].reshape(r, N_HEADS * per_head)


W_IN_COLS = MLA_Q_RANK + MLA_KV_RANK + MLA_ROPE + 3 * GDN_W + 2 * N_HEADS + GDN_W
W_IN_SHARD = W_IN_COLS // N_SHARD
W_IN_SHARD_PAD = 640
_Q0 = MLA_Q_RANK + MLA_KV_RANK
_Q1 = _Q0 + MLA_ROPE
_Q2 = _Q1 + 3 * GDN_W
_Q3 = _Q2 + 2 * N_HEADS
W_IN_SEGMENTS = [(0, _Q0, PIN_MLA), (_Q0, _Q1, PIN_KPE + MLA_NOPE), (_Q1, _Q2, PIN_QKV), (_Q2, _Q3, PIN_AB),
                 (_Q3, W_IN_COLS, PIN_GATE)]


def _win_pad_t(slabs):
    d = slabs.shape[2]
    pieces, at = [], 0
    for c0, c1, r0 in sorted(W_IN_SEGMENTS, key=lambda s: s[2]):
        if r0 > at:
            pieces.append(jnp.zeros((r0 - at, d), slabs.dtype))
        for q in range(N_SHARD):
            lo, hi = max(c0, q * W_IN_SHARD), min(c1, (q + 1) * W_IN_SHARD)
            if lo < hi:
                pieces.append(slabs[q, lo - q * W_IN_SHARD:hi - q * W_IN_SHARD])
        at = r0 + c1 - c0
    pieces.append(jnp.zeros((PIN_W - at, d), slabs.dtype))
    return jnp.concatenate(pieces, axis=0)


def _win_cols_t(wp_t, c_lo, c_hi):
    pieces = []
    for c0, c1, r0 in W_IN_SEGMENTS:
        lo, hi = max(c0, c_lo), min(c1, c_hi)
        if lo < hi:
            pieces.append(wp_t[r0 + lo - c0:r0 + hi - c0])
    return jnp.concatenate(pieces, axis=0)


def _wkv_to_pad(wkv):
    r = wkv.shape[0]
    w3 = wkv.reshape(r, N_HEADS, MLA_NOPE + MLA_V)
    kpart = jnp.pad(w3[:, :, :MLA_NOPE], ((0, 0), (0, 0), (0, HEAD_LANES - MLA_NOPE))).reshape(r, MLA_PAD)
    vpart = jnp.pad(w3[:, :, MLA_NOPE:], ((0, 0), (0, 0), (0, HEAD_LANES - MLA_V))).reshape(r, MLA_PAD)
    return jnp.concatenate([kpart, vpart], axis=1)


def _wkv_from_pad(wp):
    r = wp.shape[0]
    kpart = wp[:, :MLA_PAD].reshape(r, N_HEADS, HEAD_LANES)[:, :, :MLA_NOPE]
    vpart = wp[:, MLA_PAD:].reshape(r, N_HEADS, HEAD_LANES)[:, :, :MLA_V]
    return jnp.concatenate([kpart, vpart], axis=2).reshape(r, N_HEADS * (MLA_NOPE + MLA_V))


def _wout_to_pad(wo):
    n = wo.shape[1]
    mla = jnp.pad(wo[:N_HEADS * MLA_V].reshape(N_HEADS, MLA_V, n), ((0, 0), (0, HEAD_LANES - MLA_V), (0, 0)))
    return jnp.concatenate([mla.reshape(MLA_PAD, n), wo[N_HEADS * MLA_V:]], axis=0)


def _wout_from_pad(wp):
    n = wp.shape[1]
    mla = wp[:MLA_PAD].reshape(N_HEADS, HEAD_LANES, n)[:, :MLA_V].reshape(N_HEADS * MLA_V, n)
    return jnp.concatenate([mla, wp[MLA_PAD:]], axis=0)


def _pad_lanes(v, n):
    return jnp.pad(v, ((0, 0), (0, n - v.shape[1])))


def _compute_weights(full):
    w = {}
    for n in FFN_BIG:
        if n in full:
            w[n] = full[n].astype(MM_DTYPE)
    w["w_in_pad_t"] = _win_pad_t(full["w_in"]).astype(MM_DTYPE)
    w["w_uq_pad"] = _pad_heads_cols(full["mla_w_uq"], MLA_NOPE + MLA_ROPE).astype(MM_DTYPE)
    w["w_kv_pad"] = _wkv_to_pad(full["mla_w_ukv"]).astype(MM_DTYPE)
    w["w_out_pad"] = _wout_to_pad(full["w_out"]).astype(MM_DTYPE)
    w["conv_w"] = full["gdn_conv_w"].astype(F32)
    for n in ("ffn1_pre_g", "ffn1_post_g", "mix_pre_g", "mla_q_norm_g", "mla_kv_norm_g", "gdn_norm_g", "mix_post_g",
              "ffn2_pre_g", "ffn2_post_g"):
        w[n] = full[n]
    w["mla_out_g_pad"] = _pad_heads_cols(full["mla_out_g"], MLA_V)
    w["a_log_pad"] = _pad_lanes(full["gdn_a_log"], HEAD_LANES)
    w["dt_bias_pad"] = _pad_lanes(full["gdn_dt_bias"], HEAD_LANES)
    return w


FFN2_BIG = FFN_BIG[3:]


def _local_step(x, positions, loss_target, full, late=None):
    t, d = x.shape
    tb = min(512, t)
    tm = min(1024, t)
    w = _compute_weights(full)
    ffn = lambda tag: (w[tag + "_pre_g"], w[tag + "_w_gate"], w[tag + "_w_up"], w[tag + "_w_down"], w[tag + "_post_g"])
    x1, sv1 = _ffn_fwd("ffn1", x, *ffn("ffn1"), tm)
    x2, svm, gathered = _mixer_fwd(x1, positions, w, tb, _carried_gather(late[0]) if late else None)
    for n, gw in zip(FFN2_BIG, gathered):
        w[n] = gw
    x3, sv2 = _ffn_fwd("ffn2", x2, *ffn("ffn2"), tm)

    def loss_f(yb, tg):
        e = yb - tg
        return e * (1.0 / d), jnp.sum(e * e, axis=0, keepdims=True)

    dy, lsum = _rowwise("loss", loss_f, [x3, loss_target], [], [(d, F32)], [(1, d)], tb)
    g = {}
    dx2, g["ffn2_pre_g"], g["ffn2_w_gate"], g["ffn2_w_up"], g["ffn2_w_down"], g["ffn2_post_g"], _, _ = _ffn_bwd(
        "ffn2", dy, sv2, *ffn("ffn2"), tm, tm)

    def pair_sums(arrs, tag):
        got = _swap_halves(arrs, tag)
        return [_add_pair("add_pair%s_%d" % (tag, i), gi, gt, late[1]) for i, (gi, gt) in enumerate(zip(arrs, got))]

    def chip_sums(pairs, slabs, tag):
        return [_add_chips("add_chips%s_%d" % (tag, i), pr, sl, late[2]) for i, (pr, sl) in enumerate(zip(pairs, slabs))]

    if late:
        pairs2 = pair_sums([g[n] for n in FFN2_BIG], "_ffn2")
        dx1, gm, slabs2 = _mixer_bwd(dx2, svm, w, tb, _carried_scatter(pairs2))
        for n, hs in zip(FFN2_BIG, chip_sums(pairs2, slabs2, "_ffn2")):
            g[n] = hs
    else:
        dx1, gm, _ = _mixer_bwd(dx2, svm, w, tb)
    g["w_in"] = jnp.stack([jnp.pad(_win_cols_t(gm["w_in_pad_t"], q * W_IN_SHARD, (q + 1) * W_IN_SHARD),
                                   ((0, W_IN_SHARD_PAD - W_IN_SHARD), (0, 0))) for q in range(N_SHARD)])
    g["mla_w_uq"] = _unpad_heads_cols(gm["w_uq_pad"], MLA_NOPE + MLA_ROPE)
    g["mla_w_ukv"] = _wkv_from_pad(gm["w_kv_pad"])
    g["gdn_conv_w"] = gm["conv_w"]
    g["w_out"] = _wout_from_pad(gm["w_out_pad"])
    ffn1_names = FFN_BIG[:3]
    if late:
        quarters = [_pack([jnp.split(g[n], N_SHARD, axis=SHARD_AXIS[n])[q] for n in MIX_BIG], MM_DTYPE)
                    for q in range(N_SHARD)]
        pairs_m = pair_sums([g["w_in"].astype(MM_DTYPE), jnp.stack(quarters)], "_mix")
        pairs1 = []

        def make_up(*dws):
            pairs1.extend(pair_sums(list(dws), "_ffn1"))
            return _carried_scatter(pairs1)

        dx0, g["ffn1_pre_g"], _, _, _, g["ffn1_post_g"], slabs_m, slabs1 = _ffn_bwd(
            "ffn1", dx1, sv1, *ffn("ffn1"), tm, tm, _carried_scatter(pairs_m), make_up)
        for n, hs in zip(ffn1_names, chip_sums(pairs1, slabs1, "_ffn1")):
            g[n] = hs
        g["w_in"], g["mix_pack"] = chip_sums(pairs_m, slabs_m, "_mix")
    else:
        dx0, g["ffn1_pre_g"], g["ffn1_w_gate"], g["ffn1_w_up"], g["ffn1_w_down"], g["ffn1_post_g"], _, _ = _ffn_bwd(
            "ffn1", dx1, sv1, *ffn("ffn1"), tm, tm)
    g["mix_pre_g"], g["mix_post_g"] = gm["mix_pre_g"], gm["mix_post_g"]
    g["mla_q_norm_g"], g["mla_kv_norm_g"] = gm["mla_q_norm_g"], gm["mla_kv_norm_g"]
    g["gdn_norm_g"] = gm["gdn_norm_g"]
    g["mla_out_g"] = _unpad_heads_cols(gm["mla_out_g_pad"], MLA_V)
    g["gdn_a_log"] = gm["a_log_pad"][:, :N_HEADS]
    g["gdn_dt_bias"] = gm["dt_bias_pad"][:, :N_HEADS]
    return lsum, dx0, g


HBM_SPEC = pl.BlockSpec(memory_space=pltpu.HBM)


def _place():
    return lax.axis_index("x"), lax.axis_index("y"), lax.axis_index("c")


def _exchange_call(name, body, ins, out_shapes, n_remote, n_local):
    return pl.pallas_call(
        body, name=name, in_specs=[HBM_SPEC] * len(ins), out_specs=[HBM_SPEC] * len(out_shapes), out_shape=out_shapes,
        scratch_shapes=[pltpu.SemaphoreType.DMA((n_remote,)), pltpu.SemaphoreType.DMA((n_remote,)),
                        pltpu.SemaphoreType.DMA((n_local,))])(*ins)


def _other_chips(x, y):
    return [(1 - x, y), (x, 1 - y), (1 - x, 1 - y)]


def _at_each_chip(fn):
    x, y, _ = _place()
    for cx in range(2):
        for cy in range(2):
            pl.when((x == cx) & (y == cy))(functools.partial(fn, cx, cy))


def _at_each_device(fn):
    x, y, c = _place()
    for cx in range(2):
        for cy in range(2):
            for cc in range(2):
                pl.when((x == cx) & (y == cy) & (c == cc))(functools.partial(fn, cx, cy, cc))


def _at_each_core(fn):
    c = lax.axis_index("c")
    for cc in range(2):
        pl.when(c == cc)(functools.partial(fn, cc))


def _gather_shards(ws):
    nw = len(ws)

    def body(*refs):
        w_refs, out_refs = refs[:nw], refs[nw:2 * nw]
        send_sems, recv_sems, local_sems = refs[2 * nw:]

        def run(x, y, c):
            chips = _other_chips(x, y)
            me, sibling = 2 * x + y, (x, y, 1 - c)

            def half(ref, which):
                hr = ref.shape[0] // 2
                return ref.at[pl.ds(which * hr, hr)]

            def over_ici(i, j, src, slab, to):
                return pltpu.make_async_remote_copy(
                    src_ref=half(src, c), dst_ref=half(out_refs[i].at[slab], c), send_sem=send_sems.at[7 * i + j],
                    recv_sem=recv_sems.at[7 * i + j], device_id=to, device_id_type=MESH)

            def over_d2d(i, j, slab, which):
                return pltpu.make_async_remote_copy(
                    src_ref=half(out_refs[i].at[slab], which), dst_ref=half(out_refs[i].at[slab], which),
                    send_sem=send_sems.at[7 * i + 3 + j], recv_sem=recv_sems.at[7 * i + 3 + j], device_id=sibling,
                    device_id_type=MESH)

            def own(i, w_ref):
                return pltpu.make_async_remote_copy(
                    src_ref=w_ref, dst_ref=out_refs[i].at[me], send_sem=send_sems.at[7 * i + 6],
                    recv_sem=recv_sems.at[7 * i + 6], device_id=sibling, device_id_type=MESH)

            sends, passed = [], []
            for i, w_ref in enumerate(w_refs):
                for j, (px, py) in enumerate(chips):
                    sends.append(over_ici(i, j, w_ref, me, (px, py, c)))
                    sends[-1].start()
            for i, w_ref in enumerate(w_refs):
                sends.append(own(i, w_ref))
                sends[-1].start()
            for i, w_ref in enumerate(w_refs):
                for j, (px, py) in enumerate(chips):
                    over_ici(i, j, w_ref, 2 * px + py, (px, py, c)).wait_recv()
                    passed.append(over_d2d(i, j, 2 * px + py, c))
                    passed[-1].start()
            for i, w_ref in enumerate(w_refs):
                own(i, w_ref).wait_recv()
                for j, (px, py) in enumerate(chips):
                    over_d2d(i, j, 2 * px + py, 1 - c).wait_recv()
            for cp in sends + passed:
                cp.wait_send()

        _at_each_device(run)

    outs = [jax.ShapeDtypeStruct((N_SHARD,) + w.shape, w.dtype) for w in ws]
    return _exchange_call("gather_weight_shards", body, ws, outs, 7 * nw, 1)


def _swap_halves(gs, tag=""):
    ng = len(gs)

    def body(*refs):
        g_refs, got_refs = refs[:ng], refs[ng:2 * ng]
        send_sems, recv_sems, _ = refs[2 * ng:]
        x, y, _ = _place()

        def run(c):
            sends = []
            for i, (g_ref, got_ref) in enumerate(zip(g_refs, got_refs)):
                hr = got_ref.shape[1]
                sends.append(pltpu.make_async_remote_copy(
                    src_ref=g_ref.at[:, pl.ds((1 - c) * hr, hr)], dst_ref=got_ref, send_sem=send_sems.at[i],
                    recv_sem=recv_sems.at[i], device_id=(x, y, 1 - c), device_id_type=MESH))
                sends[-1].start()
            for cp in sends:
                cp.wait()

        _at_each_core(run)

    halves = [jax.ShapeDtypeStruct((g.shape[0], g.shape[1] // 2, g.shape[2]), g.dtype) for g in gs]
    return _exchange_call("swap_grad_halves" + tag, body, gs, halves, ng, 1)


def _scatter_copies(p_refs, out_refs, send_sems, recv_sems, x, y):
    c = lax.axis_index("c")
    copies = []
    for i, (p_ref, out_ref) in enumerate(zip(p_refs, out_refs)):
        for j, (px, py) in enumerate(_other_chips(x, y)):
            copies.append(pltpu.make_async_remote_copy(
                src_ref=p_ref.at[2 * px + py], dst_ref=out_ref.at[j], send_sem=send_sems.at[3 * i + j],
                recv_sem=recv_sems.at[3 * i + j], device_id=(px, py, c), device_id_type=MESH))
    return copies


def _start_all(make, *refs):
    def run(x, y):
        for cp in make(*refs, x, y):
            cp.start()

    _at_each_chip(run)


def _wait_all(make, *refs):
    def run(x, y):
        copies = make(*refs, x, y)
        for cp in copies:
            cp.wait_recv()
        for cp in copies:
            cp.wait_send()

    _at_each_chip(run)


def _scatter_shapes(ps):
    return [jax.ShapeDtypeStruct((3,) + p.shape[1:], p.dtype) for p in ps]


def _carried_scatter(ps):
    return _Carried(ps, _scatter_shapes(ps), 3 * len(ps), functools.partial(_start_all, _scatter_copies),
                    functools.partial(_wait_all, _scatter_copies))


def _direct_gather_copies(w_refs, out_refs, send_sems, recv_sems, x, y, arriving):
    c = lax.axis_index("c")
    me = 2 * x + y
    peers = [((px, py, c), 2 * px + py) for px, py in _other_chips(x, y)] + [((x, y, 1 - c), me)]
    copies = []
    for i, (w_ref, out_ref) in enumerate(zip(w_refs, out_refs)):
        for j, (peer, slab) in enumerate(peers):
            copies.append(pltpu.make_async_remote_copy(
                src_ref=w_ref, dst_ref=out_ref.at[slab if arriving else me], send_sem=send_sems.at[4 * i + j],
                recv_sem=recv_sems.at[4 * i + j], device_id=peer, device_id_type=MESH))
    return copies


def _carried_gather(ws):
    def start(w_refs, out_refs, send_sems, recv_sems):
        def run(x, y):
            for cp in _direct_gather_copies(w_refs, out_refs, send_sems, recv_sems, x, y, False):
                cp.start()

        _at_each_chip(run)

    def finish(w_refs, out_refs, send_sems, recv_sems):
        def run(x, y):
            for cp in _direct_gather_copies(w_refs, out_refs, send_sems, recv_sems, x, y, True):
                cp.wait_recv()
            for cp in _direct_gather_copies(w_refs, out_refs, send_sems, recv_sems, x, y, False):
                cp.wait_send()

        _at_each_chip(run)

    outs = [jax.ShapeDtypeStruct((N_SHARD,) + w.shape, w.dtype) for w in ws]
    return _Carried(ws, outs, 4 * len(ws), start, finish)


def _share_halves(hs):
    n = len(hs)

    def body(*refs):
        h_refs, out_refs = refs[:n], refs[n:2 * n]
        send_sems, recv_sems, _ = refs[2 * n:]
        x, y, c = _place()
        sends = []
        for i, (h_ref, out_ref) in enumerate(zip(h_refs, out_refs)):
            sends.append(pltpu.make_async_remote_copy(
                src_ref=h_ref, dst_ref=out_ref, send_sem=send_sems.at[i], recv_sem=recv_sems.at[i],
                device_id=(x, y, 1 - c), device_id_type=MESH))
            sends[-1].start()
        for cp in sends:
            cp.wait()

    outs = [jax.ShapeDtypeStruct(h.shape, h.dtype) for h in hs]
    return _exchange_call("share_grad_halves", body, hs, outs, n, 1)


def _scalar_grid_call(name, body, scalars, grid, in_specs, out_specs, out_shape, args):
    grid_spec = pltpu.PrefetchScalarGridSpec(num_scalar_prefetch=len(scalars), grid=grid, in_specs=in_specs,
                                             out_specs=out_specs)
    return pl.pallas_call(body, name=name, grid_spec=grid_spec, out_shape=out_shape,
                          compiler_params=_params(("arbitrary",) * len(grid)))(*scalars, *args)


def _add_pair(name, g, got, core):
    ns_, hr, cols = got.shape
    th = _row_tile(hr, 512)
    nb = hr // th

    def body(core_ref, g_ref, got_ref, out_ref):
        out_ref[...] = (g_ref[...].astype(F32) + got_ref[...].astype(F32)).astype(out_ref.dtype)

    blk = pl.BlockSpec((1, th, cols), lambda q, j, core_ref: (q, j, 0))
    own = pl.BlockSpec((1, th, cols), lambda q, j, core_ref: (q, core_ref[0] * nb + j, 0))
    return _scalar_grid_call(name, body, [core], (ns_, nb), [own, blk], blk,
                             jax.ShapeDtypeStruct(got.shape, got.dtype), [g, got])


def _add_chips(name, pairs, slabs, chip):
    _, hr, cols = slabs.shape
    th = _row_tile(hr, 512)

    def body(chip_ref, own_ref, s0_ref, s1_ref, s2_ref, out_ref):
        total = own_ref[0].astype(F32) + s0_ref[0].astype(F32)
        out_ref[...] = (total + s1_ref[0].astype(F32)) + s2_ref[0].astype(F32)

    own = pl.BlockSpec((1, th, cols), lambda j, chip_ref: (chip_ref[0], j, 0))
    others = [pl.BlockSpec((1, th, cols), lambda j, chip_ref, k=k: (k, j, 0)) for k in range(3)]
    return _scalar_grid_call(name, body, [chip], (hr // th,), [own] + others,
                             pl.BlockSpec((th, cols), lambda j, chip_ref: (j, 0)),
                             jax.ShapeDtypeStruct((hr, cols), F32), [pairs, slabs, slabs, slabs])


def _join_halves(name, mine, other, core):
    hr, cols = mine.shape
    th = _row_tile(hr, 512)
    nb = hr // th

    def body(core_ref, mine_ref, other_ref, out_ref):
        is_mine = pl.program_id(0) == core_ref[0]

        @pl.when(is_mine)
        def _():
            out_ref[0] = mine_ref[...]

        @pl.when(jnp.logical_not(is_mine))
        def _():
            out_ref[0] = other_ref[...]

    blk = pl.BlockSpec((th, cols), lambda h, j, core_ref: (j, 0))
    return _scalar_grid_call(name, body, [core], (2, nb), [blk, blk],
                             pl.BlockSpec((1, th, cols), lambda h, j, core_ref: (0, h * nb + j, 0)),
                             jax.ShapeDtypeStruct((1, 2 * hr, cols), mine.dtype), [mine, other])


def _gather_small(sp):
    def body(s_ref, out_ref, send_sems, recv_sems, local_sem):
        x, y, c = _place()
        me = 4 * x + 2 * y + c
        peers = [(x ^ (m >> 2), y ^ ((m >> 1) & 1), c ^ (m & 1)) for m in range(1, 8)]
        mine = pltpu.make_async_copy(s_ref, out_ref.at[me], local_sem)
        mine.start()
        sends = [pltpu.make_async_remote_copy(src_ref=s_ref, dst_ref=out_ref.at[me], send_sem=send_sems.at[j],
                                              recv_sem=recv_sems.at[j], device_id=p, device_id_type=MESH)
                 for j, p in enumerate(peers)]
        for cp in sends:
            cp.start()
        for j, (px, py, pc) in enumerate(peers):
            pltpu.make_async_remote_copy(src_ref=s_ref, dst_ref=out_ref.at[4 * px + 2 * py + pc],
                                         send_sem=send_sems.at[j], recv_sem=recv_sems.at[j], device_id=(px, py, pc),
                                         device_id_type=MESH).wait_recv()
        for cp in sends:
            cp.wait_send()
        mine.wait()

    return pl.pallas_call(
        body, name="gather_small_grads", in_specs=[HBM_SPEC], out_specs=HBM_SPEC,
        out_shape=jax.ShapeDtypeStruct((8,) + sp.shape, sp.dtype),
        scratch_shapes=[pltpu.SemaphoreType.DMA((7,)), pltpu.SemaphoreType.DMA((7,)), pltpu.SemaphoreType.DMA])(sp)


def _pack_rows(total):
    rows = -(-total // LANES)
    return -(-rows // 32) * 32


def _pack(arrs, dtype):
    flat = jnp.concatenate([a.reshape(-1).astype(dtype) for a in arrs])
    rows = _pack_rows(flat.shape[0])
    return jnp.pad(flat, (0, rows * LANES - flat.shape[0])).reshape(rows, LANES)


def _unpack(buf, shapes):
    flat = buf.reshape(-1)
    out, off = {}, 0
    for n, shp in shapes:
        size = shp[0] * shp[1]
        out[n] = flat[off:off + size].reshape(shp)
        off += size
    return out


def _to_wire(name, w3):
    _, r, cols = w3.shape
    tb = _row_tile(r, 512)

    def body(w_ref, o_ref):
        o_ref[...] = w_ref[0].astype(o_ref.dtype)

    return pl.pallas_call(
        body, name=name, grid=(r // tb,), in_specs=[pl.BlockSpec((1, tb, cols), lambda i: (0, i, 0))],
        out_specs=pl.BlockSpec((tb, cols), lambda i: (i, 0)), out_shape=jax.ShapeDtypeStruct((r, cols), MM_DTYPE),
        compiler_params=_params(("arbitrary",)))(w3)


def _adamw(name, w3, g, m3, v3, tb):
    c1 = 1.0 - ADAM_B1 ** ADAM_STEP
    c2 = 1.0 - ADAM_B2 ** ADAM_STEP
    _, r, cols = w3.shape
    emit = g.ndim == 2
    blk3 = pl.BlockSpec((1, tb, cols), lambda i: (0, i, 0))
    g_spec = pl.BlockSpec((tb, cols), lambda i: (i, 0)) if emit else blk3

    def body(w_ref, g_ref, m_ref, v_ref, *out_refs):
        gb = g_ref[...] if emit else g_ref[0]
        m2 = ADAM_B1 * m_ref[0] + (1.0 - ADAM_B1) * gb
        v2 = ADAM_B2 * v_ref[0] + (1.0 - ADAM_B2) * (gb * gb)
        out_refs[-3][0] = -ADAM_LR * ((m2 / c1) / (jnp.sqrt(v2 / c2) + ADAM_EPS) + ADAM_WD * w_ref[0])
        out_refs[-2][0] = m2
        out_refs[-1][0] = v2
        if emit:
            out_refs[0][0] = gb

    n_out = 4 if emit else 3
    outs = pl.pallas_call(
        body, name=name, grid=(r // tb,), in_specs=[blk3, g_spec, blk3, blk3], out_specs=[blk3] * n_out,
        out_shape=[jax.ShapeDtypeStruct((1, r, cols), F32)] * n_out,
        compiler_params=_params(("arbitrary",)))(w3, g, m3, v3)
    return outs if emit else [g] + list(outs)


def _row_tile(rows, pref):
    if rows <= pref:
        return rows
    t = pref
    while t >= 8:
        if rows % t == 0 and t % 8 == 0:
            return t
        t -= 8
    return rows


def kernel(x, positions, ffn1_pre_g, ffn1_w_gate, ffn1_w_up, ffn1_w_down, ffn1_post_g, mix_pre_g, w_in, mla_q_norm_g, mla_w_uq, mla_kv_norm_g, mla_w_ukv, mla_out_g, gdn_conv_w, gdn_a_log, gdn_dt_bias, gdn_norm_g, w_out, mix_post_g, ffn2_pre_g, ffn2_w_gate, ffn2_w_up, ffn2_w_down, ffn2_post_g, loss_target, m_ffn1_pre_g, m_ffn1_w_gate, m_ffn1_w_up, m_ffn1_w_down, m_ffn1_post_g, m_mix_pre_g, m_w_in, m_mla_q_norm_g, m_mla_w_uq, m_mla_kv_norm_g, m_mla_w_ukv, m_mla_out_g, m_gdn_conv_w, m_gdn_a_log, m_gdn_dt_bias, m_gdn_norm_g, m_w_out, m_mix_post_g, m_ffn2_pre_g, m_ffn2_w_gate, m_ffn2_w_up, m_ffn2_w_down, m_ffn2_post_g, v_ffn1_pre_g, v_ffn1_w_gate, v_ffn1_w_up, v_ffn1_w_down, v_ffn1_post_g, v_mix_pre_g, v_w_in, v_mla_q_norm_g, v_mla_w_uq, v_mla_kv_norm_g, v_mla_w_ukv, v_mla_out_g, v_gdn_conv_w, v_gdn_a_log, v_gdn_dt_bias, v_gdn_norm_g, v_w_out, v_mix_post_g, v_ffn2_pre_g, v_ffn2_w_gate, v_ffn2_w_up, v_ffn2_w_down, v_ffn2_post_g):
    args = dict(locals())
    wsh = {n: args[n][0] for n in WEIGHTS}
    msh = {n: args["m_" + n][0] if args["m_" + n].ndim == 3 else args["m_" + n] for n in WEIGHTS}
    vsh = {n: args["v_" + n][0] if args["v_" + n].ndim == 3 else args["v_" + n] for n in WEIGHTS}
    for n in SMALL:
        wsh[n] = args[n]
    mix_shapes = [(n, wsh[n].shape) for n in MIX_BIG]

    early = FFN_BIG[:3]
    held = lambda a, n: jnp.swapaxes(a, 1, 2) if n in TRANSPOSED else a
    w_in_wire = jnp.pad(held(w_in, "w_in")[0].astype(MM_DTYPE), ((0, W_IN_SHARD_PAD - W_IN_SHARD), (0, 0)))
    gathered = _gather_shards([_to_wire("wire_" + n, held(args[n], n)) for n in early]
                              + [w_in_wire, _pack([wsh[n] for n in MIX_BIG], MM_DTYPE)])
    full = {n: wsh[n] for n in SMALL}
    for n, gw in zip(early + ["w_in"], gathered):
        full[n] = gw
    parts = [_unpack(gathered[-1][q], mix_shapes) for q in range(N_SHARD)]
    for n in MIX_BIG:
        full[n] = jnp.concatenate([parts[q][n] for q in range(N_SHARD)], axis=SHARD_AXIS[n])

    core = lax.axis_index("c").astype(jnp.int32).reshape(1)
    chip = (2 * lax.axis_index("x") + lax.axis_index("y")).astype(jnp.int32).reshape(1)
    late = ([_to_wire("wire_" + n, held(args[n], n)) for n in FFN2_BIG], core, chip)
    lsum, grad_x, g = _local_step(x[0], positions, loss_target[0], full, late)
    loss = lax.psum(0.5 * jnp.sum(lsum) / x.shape[-1], ("x", "y", "c"))

    halves = [g[n] for n in FFN_BIG] + [g["w_in"], g["mix_pack"]]
    others = _share_halves(halves)
    shared = [_join_halves("join_halves_%d" % i, hm, ho, core) for i, (hm, ho) in enumerate(zip(halves, others))]
    gsh = _unpack(shared[-1], mix_shapes)
    for n, sg_ in zip(FFN_BIG, shared):
        gsh[n] = sg_
    gsh["w_in"] = shared[-2][:, :W_IN_SHARD]

    small_shapes = [(n, wsh[n].shape) for n in SMALL]
    pack_small = lambda d: jnp.concatenate(
        [_pad_lanes(d[n].astype(F32), LANES) for n in SMALL] + [jnp.zeros((SMALL_ROWS - len(SMALL), LANES), F32)], axis=0)
    slots = _gather_small(pack_small(g))

    c1 = 1.0 - ADAM_B1 ** ADAM_STEP
    c2 = 1.0 - ADAM_B2 ** ADAM_STEP

    def small_update(wb, mb, vb, s8):
        gs = s8[0:SMALL_ROWS]
        for d in range(1, 8):
            gs = gs + s8[d * SMALL_ROWS:(d + 1) * SMALL_ROWS]
        m2 = ADAM_B1 * mb + (1.0 - ADAM_B1) * gs
        v2 = ADAM_B2 * vb + (1.0 - ADAM_B2) * (gs * gs)
        delta = -ADAM_LR * ((m2 / c1) / (jnp.sqrt(v2 / c2) + ADAM_EPS) + ADAM_WD * wb)
        return gs, delta, m2, v2

    sg, sd, sm, sv_ = _rowwise("adamw_small", small_update,
                               [pack_small(wsh), pack_small(msh), pack_small(vsh)],
                               [slots.reshape(8 * SMALL_ROWS, LANES)], [(LANES, F32)] * 4, [], SMALL_ROWS)
    grads, deltas, new_m, new_v = {}, {}, {}, {}
    for i, (n, shp) in enumerate(small_shapes):
        grads[n], deltas[n] = sg[i:i + 1, :shp[1]], sd[i:i + 1, :shp[1]]
        new_m[n], new_v[n] = sm[i:i + 1, :shp[1]], sv_[i:i + 1, :shp[1]]
    for n in BIG:
        w3 = held(args[n], n)
        outs = _adamw("adamw_" + n, w3, gsh[n], held(args["m_" + n], n), held(args["v_" + n], n),
                      _row_tile(w3.shape[1], 256))
        grads[n], deltas[n], new_m[n], new_v[n] = [held(o, n) for o in outs]

    return (loss, grad_x[None], *[grads[n] for n in WEIGHTS], *[deltas[n] for n in WEIGHTS],
            *[new_m[n] for n in WEIGHTS], *[new_v[n] for n in WEIGHTS])
```

```python
import functools

import jax
import jax.numpy as jnp
from jax import lax
from jax.experimental import pallas as pl
from jax.experimental.pallas import tpu as pltpu

F32 = jnp.float32
BF16 = jnp.bfloat16
MM_DTYPE = BF16
MESH = pl.DeviceIdType.MESH

D_MODEL = 1024
D_FF = 2816
N_HEADS = 8
MLA_Q_RANK = 256
MLA_KV_RANK = 128
MLA_NOPE = 64
MLA_ROPE = 32
MLA_V = 64
ROPE_THETA = 10000.0
GDN_DH = 64
GDN_W = N_HEADS * GDN_DH
GDN_CONV = 4
CHUNK = 64
HEAD_LANES = 128
HEADS_PER_STEP = 8
MLA_PAD = N_HEADS * HEAD_LANES
EPS = 1e-6
N_SHARD = 4
LANES = 1024

PIN_QKV = 0
PIN_MLA = 1536
PIN_KPE = 1920
PIN_GATE = 2048
PIN_AB = 2560
PIN_W = 2688
CAT_W = MLA_PAD + GDN_W

ADAM_LR = 0.001
ADAM_B1 = 0.9
ADAM_B2 = 0.999
ADAM_EPS = 1e-08
ADAM_WD = 0.01
ADAM_STEP = 10

VMEM_LIMIT_V7X = 56 * 1024 * 1024

BIG = ["ffn1_w_gate", "ffn1_w_up", "ffn1_w_down", "w_in", "mla_w_uq", "mla_w_ukv", "gdn_conv_w", "w_out",
       "ffn2_w_gate", "ffn2_w_up", "ffn2_w_down"]
FFN_BIG = ["ffn1_w_gate", "ffn1_w_up", "ffn1_w_down", "ffn2_w_gate", "ffn2_w_up", "ffn2_w_down"]
TRANSPOSED = ["ffn1_w_gate", "ffn1_w_up", "ffn2_w_gate", "ffn2_w_up", "w_in"]
MIX_BIG = ["mla_w_uq", "mla_w_ukv", "gdn_conv_w", "w_out"]
SMALL = ["ffn1_pre_g", "ffn1_post_g", "mix_pre_g", "mla_q_norm_g", "mla_kv_norm_g", "mla_out_g", "gdn_a_log",
         "gdn_dt_bias", "gdn_norm_g", "mix_post_g", "ffn2_pre_g", "ffn2_post_g"]
WEIGHTS = ["ffn1_pre_g", "ffn1_w_gate", "ffn1_w_up", "ffn1_w_down", "ffn1_post_g", "mix_pre_g", "w_in",
           "mla_q_norm_g", "mla_w_uq", "mla_kv_norm_g", "mla_w_ukv", "mla_out_g", "gdn_conv_w", "gdn_a_log",
           "gdn_dt_bias", "gdn_norm_g", "w_out", "mix_post_g", "ffn2_pre_g", "ffn2_w_gate", "ffn2_w_up",
           "ffn2_w_down", "ffn2_post_g"]
SHARD_AXIS = {"ffn1_w_gate": 1, "ffn1_w_up": 1, "ffn1_w_down": 0, "w_in": 1, "mla_w_uq": 1, "mla_w_ukv": 1,
              "gdn_conv_w": 1, "w_out": 0, "ffn2_w_gate": 1, "ffn2_w_up": 1, "ffn2_w_down": 0}
SMALL_ROWS = 16


def _params(sem):
    return pltpu.CompilerParams(dimension_semantics=sem, vmem_limit_bytes=VMEM_LIMIT_V7X)


def _pick(dim, pref):
    if dim <= pref:
        return dim
    t = (pref // 128) * 128
    while t >= 128:
        if dim % t == 0:
            return t
        t -= 128
    return dim


ANY_SPEC = pl.BlockSpec(memory_space=pl.ANY)


def _rowwise(name, fn, row_ins, bc_ins, row_outs, acc_outs, tb, wide=None, carry=None):
    ents = []
    for e in row_ins:
        ents.append(e if isinstance(e, tuple) else (e, e.shape[1], 0, 0))
    over = [o[2] for o in row_outs if len(o) == 3]
    rows = over[0] if over else ents[0][0].shape[0]
    steps = rows // tb
    assert steps * tb == rows, (name, rows, tb)
    in_specs, args = [], []
    for a, w, j, r0 in ents:
        in_specs.append(pl.BlockSpec((tb, w), lambda i, j=j, r0=r0: (i + r0, j)))
        args.append(a)
    for b in bc_ins:
        in_specs.append(pl.BlockSpec(b.shape, lambda i: (0, 0)))
        args.append(b)
    n_in = len(args)
    aliases = {}
    if carry is not None:
        in_specs.append(ANY_SPEC)
        args.append(carry)
        aliases = {n_in: 0}
    out_shape = [jax.ShapeDtypeStruct((rows, o[0]), o[1]) for o in row_outs]
    out_specs = [pl.BlockSpec((tb, o[0]), lambda i: (i, 0)) for o in row_outs]
    if wide is not None:
        out_shape[0] = jax.ShapeDtypeStruct((rows, wide[0]), row_outs[0][1])
        out_specs[0] = pl.BlockSpec((tb, row_outs[0][0]), lambda i: (i, wide[1]))
    out_shape += [jax.ShapeDtypeStruct((r, c), F32) for r, c in acc_outs]
    out_specs += [pl.BlockSpec((r, c), lambda i: (0, 0)) for r, c in acc_outs]
    n_ro, n_acc, n_args = len(row_outs), len(acc_outs), len(args)

    def body(*refs):
        vals = fn(*[r[...] for r in refs[:n_in]])
        if not isinstance(vals, (tuple, list)):
            vals = (vals,)
        for r, v in zip(refs[n_args:n_args + n_ro], vals[:n_ro]):
            r[...] = v.astype(r.dtype)
        if n_acc:
            acc_refs = refs[n_args + n_ro:]

            @pl.when(pl.program_id(0) == 0)
            def _():
                for r in acc_refs:
                    r[...] = jnp.zeros(r.shape, r.dtype)

            for r, v in zip(acc_refs, vals[n_ro:]):
                r[...] += v

    outs = pl.pallas_call(body, name=name, grid=(steps,), in_specs=in_specs, out_specs=out_specs,
                          out_shape=out_shape, input_output_aliases=aliases,
                          compiler_params=_params(("arbitrary",)))(*args)
    return outs


def _mm(name, a, b, mode, out_dtype, tm=1024, tn=1024, tk=1024):
    if mode == "nn":
        (m, k), (k2, n) = a.shape, b.shape
    elif mode == "nt":
        (m, k), (n, k2) = a.shape, b.shape
    else:
        (k, m), (k2, n) = a.shape, b.shape
    assert k == k2, (name, a.shape, b.shape)
    tm, tn, tk = _pick(m, tm), _pick(n, tn), _pick(k, tk)
    nk = k // tk
    if mode == "nn":
        a_spec = pl.BlockSpec((tm, tk), lambda i, j, kk: (i, kk))
        b_spec = pl.BlockSpec((tk, tn), lambda i, j, kk: (kk, j))
        dims = (((1,), (0,)), ((), ()))
    elif mode == "nt":
        a_spec = pl.BlockSpec((tm, tk), lambda i, j, kk: (i, kk))
        b_spec = pl.BlockSpec((tn, tk), lambda i, j, kk: (j, kk))
        dims = (((1,), (1,)), ((), ()))
    else:
        a_spec = pl.BlockSpec((tk, tm), lambda i, j, kk: (kk, i))
        b_spec = pl.BlockSpec((tk, tn), lambda i, j, kk: (kk, j))
        dims = (((0,), (0,)), ((), ()))

    def body(a_ref, b_ref, o_ref, acc_ref):
        kk = pl.program_id(2)

        @pl.when(kk == 0)
        def _():
            acc_ref[...] = jnp.zeros(acc_ref.shape, F32)

        acc_ref[...] += lax.dot_general(a_ref[...].astype(MM_DTYPE), b_ref[...].astype(MM_DTYPE), dims,
                                        preferred_element_type=F32)

        @pl.when(kk == nk - 1)
        def _():
            o_ref[...] = acc_ref[...].astype(o_ref.dtype)

    return pl.pallas_call(
        body, name=name, grid=(m // tm, n // tn, nk), in_specs=[a_spec, b_spec],
        out_specs=pl.BlockSpec((tm, tn), lambda i, j, kk: (i, j)),
        out_shape=jax.ShapeDtypeStruct((m, n), out_dtype),
        scratch_shapes=[pltpu.VMEM((tm, tn), F32)],
        compiler_params=_params(("parallel", "parallel", "arbitrary")))(a, b)


def _rms_stats(x, n_real=None):
    n = x.shape[-1] if n_real is None else n_real
    return lax.rsqrt(jnp.sum(x * x, axis=-1, keepdims=True) / n + EPS)


def _rms_bwd(x, r, g, dz, n_real=None):
    n = x.shape[-1] if n_real is None else n_real
    xh = x * r
    dxh = dz * g
    dx = r * (dxh - xh * (jnp.sum(dxh * xh, axis=-1, keepdims=True) / n))
    return dx, jnp.sum(dz * xh, axis=0, keepdims=True)


def _sigmoid(x):
    return 0.5 * jnp.tanh(0.5 * x) + 0.5


def _roll(x, s, axis):
    return pltpu.roll(x, s, axis)


def _rope(x, c, s1, s2):
    return x * c + _roll(x, HEAD_LANES - MLA_ROPE // 2, 1) * s1 + _roll(x, MLA_ROPE // 2, 1) * s2


def _heads_apply(x, fn):
    return jnp.concatenate([fn(x[:, h * HEAD_LANES:(h + 1) * HEAD_LANES]) for h in range(N_HEADS)], axis=1)


ROW_CHUNK = 256


def _row_chunks(rows):
    step = min(ROW_CHUNK, rows)
    return [pl.ds(r, step) for r in range(0, rows, step)]


def _ffn_fwd(tag, x, g_pre, wg, wu, wd, g_post, tm):
    t, d = x.shape
    ns, fs, _ = wg.shape
    nt = t // tm
    row = pl.BlockSpec((tm, d), lambda i, q: (i, 0))
    vec = pl.BlockSpec((1, d), lambda i, q: (0, 0))
    act3 = pl.BlockSpec((1, tm, fs), lambda i, q: (q, i, 0))
    wrow = pl.BlockSpec((1, fs, d), lambda i, q: (q, 0, 0))
    nt_dims = (((1,), (1,)), ((), ()))

    def gate_up(x_ref, g_ref, wg_ref, wu_ref, n_ref, sl_ref, ud_ref, s_ref, n_s):
        @pl.when(pl.program_id(1) == 0)
        def _():
            for r in _row_chunks(tm):
                xb = x_ref[r, :]
                n_s[r, :] = (xb * _rms_stats(xb) * g_ref[...]).astype(MM_DTYPE)
            n_ref[...] = n_s[...]

        for r in _row_chunks(tm):
            n = n_s[r, :]
            a = lax.dot_general(n, wg_ref[0], nt_dims, preferred_element_type=F32)
            u = lax.dot_general(n, wu_ref[0], nt_dims, preferred_element_type=F32)
            sg = _sigmoid(a)
            sl = a * sg
            sl_ref[0, r, :] = sl.astype(sl_ref.dtype)
            ud_ref[0, r, :] = (u * (sg + sl * (1.0 - sg))).astype(ud_ref.dtype)
            s_ref[0, r, :] = (sl * u).astype(s_ref.dtype)

    n, sl, ud, s = pl.pallas_call(
        gate_up, name=tag + "_gate_up", grid=(nt, ns), in_specs=[row, vec, wrow, wrow],
        out_specs=[row, act3, act3, act3],
        out_shape=[jax.ShapeDtypeStruct((t, d), MM_DTYPE)] + [jax.ShapeDtypeStruct((ns, t, fs), MM_DTYPE)] * 3,
        scratch_shapes=[pltpu.VMEM((tm, d), MM_DTYPE)],
        compiler_params=_params(("parallel", "arbitrary")))(x, g_pre, wg, wu)

    def down(s_ref, wd_ref, x_ref, g_ref, h_ref, y_ref, acc):
        q = pl.program_id(1)

        @pl.when(q == 0)
        def _():
            acc[...] = jnp.zeros(acc.shape, F32)

        for r in _row_chunks(tm):
            acc[r, :] += jnp.dot(s_ref[0, r, :], wd_ref[0], preferred_element_type=F32)

        @pl.when(q == ns - 1)
        def _():
            for r in _row_chunks(tm):
                hb = acc[r, :]
                h_ref[r, :] = hb
                y_ref[r, :] = x_ref[r, :] + 0.5 * (hb * _rms_stats(hb) * g_ref[...])

    h, y = pl.pallas_call(
        down, name=tag + "_down", grid=(nt, ns), in_specs=[act3, wrow, row, vec], out_specs=[row, row],
        out_shape=[jax.ShapeDtypeStruct((t, d), F32)] * 2, scratch_shapes=[pltpu.VMEM((tm, d), F32)],
        compiler_params=_params(("parallel", "arbitrary")))(s, wd, x, g_post)
    return y, (x, n, sl, ud, s, h)


def _carry(body, n_in, n_out, grid, carried):
    if carried is None:
        return body, [], [], [], [], []
    nx_in, nx_out = len(carried.ins), len(carried.outs)

    def wrapped(*refs):
        ins, rest = refs[:n_in], refs[n_in:]
        xi, rest = rest[:nx_in], rest[nx_in:]
        outs, rest = rest[:n_out], rest[n_out:]
        xo, rest = rest[:nx_out], rest[nx_out:]
        scr, sems = rest[:len(rest) - 2], rest[len(rest) - 2:]
        first, last = True, True
        for dim, size in enumerate(grid):
            first = first & (pl.program_id(dim) == 0)
            last = last & (pl.program_id(dim) == size - 1)

        @pl.when(first)
        def _():
            carried.start(xi, xo, *sems)

        body(*ins, *outs, *scr)

        @pl.when(last)
        def _():
            carried.finish(xi, xo, *sems)

    sems = [pltpu.SemaphoreType.DMA((carried.n_sem,)), pltpu.SemaphoreType.DMA((carried.n_sem,))]
    return (wrapped, [HBM_SPEC] * nx_in, [HBM_SPEC] * nx_out, list(carried.outs), sems, list(carried.ins))


def _ffn_bwd(tag, dy, saved, g_pre, wg, wu, wd, g_post, tm, tk, carried_down=None, make_carried_up=None):
    x, n, sl, ud, s, h = saved
    t, d = x.shape
    ns, fs, _ = wg.shape
    nt, nk = t // tm, t // tk
    row = pl.BlockSpec((tm, d), lambda i, q: (i, 0))
    vec = pl.BlockSpec((1, d), lambda i, q: (0, 0))
    act3 = pl.BlockSpec((1, tm, fs), lambda i, q: (q, i, 0))
    wrow = pl.BlockSpec((1, fs, d), lambda i, q: (q, 0, 0))
    nt_dims = (((1,), (1,)), ((), ()))
    tn_dims = (((0,), (0,)), ((), ()))

    def down_b(h_ref, dy_ref, g_ref, wd_ref, sl_ref, ud_ref, dh_ref, da_ref, du_ref, dg_ref, dh_s):
        i, q = pl.program_id(0), pl.program_id(1)

        @pl.when((i == 0) & (q == 0))
        def _():
            dg_ref[...] = jnp.zeros(dg_ref.shape, F32)

        @pl.when(q == 0)
        def _():
            for r in _row_chunks(tm):
                hb = h_ref[r, :]
                dh, dg = _rms_bwd(hb, _rms_stats(hb), g_ref[...], 0.5 * dy_ref[r, :])
                dh_s[r, :] = dh.astype(MM_DTYPE)
                dg_ref[...] += dg
            dh_ref[...] = dh_s[...]

        for r in _row_chunks(tm):
            ds = lax.dot_general(dh_s[r, :], wd_ref[0], nt_dims, preferred_element_type=F32)
            da_ref[0, r, :] = (ds * ud_ref[0, r, :].astype(F32)).astype(da_ref.dtype)
            du_ref[0, r, :] = (ds * sl_ref[0, r, :].astype(F32)).astype(du_ref.dtype)

    down_b, x_in, x_out, x_shape, x_scr, x_args = _carry(down_b, 6, 4, (nt, ns), carried_down)
    dh, da, du, dg_post, *from_down = pl.pallas_call(
        down_b, name=tag + "_down_b", grid=(nt, ns), in_specs=[row, row, vec, wrow, act3, act3] + x_in,
        out_specs=[row, act3, act3, vec] + x_out,
        out_shape=[jax.ShapeDtypeStruct((t, d), MM_DTYPE)] + [jax.ShapeDtypeStruct((ns, t, fs), MM_DTYPE)] * 2
        + [jax.ShapeDtypeStruct((1, d), F32)] + x_shape,
        scratch_shapes=[pltpu.VMEM((tm, d), MM_DTYPE)] + x_scr,
        compiler_params=_params(("arbitrary", "arbitrary")))(h, dy, g_post, wd, sl, ud, *x_args)

    def down_w(s_ref, dh_ref, dw_ref, acc):
        kk = pl.program_id(1)

        @pl.when(kk == 0)
        def _():
            acc[...] = jnp.zeros(acc.shape, F32)

        acc[...] += lax.dot_general(s_ref[0], dh_ref[...], tn_dims, preferred_element_type=F32)

        @pl.when(kk == nk - 1)
        def _():
            dw_ref[0] = acc[...].astype(dw_ref.dtype)

    dwd = pl.pallas_call(
        down_w, name=tag + "_down_w", grid=(ns, nk),
        in_specs=[pl.BlockSpec((1, tk, fs), lambda q, kk: (q, kk, 0)), pl.BlockSpec((tk, d), lambda q, kk: (kk, 0))],
        out_specs=pl.BlockSpec((1, fs, d), lambda q, kk: (q, 0, 0)),
        out_shape=jax.ShapeDtypeStruct((ns, fs, d), MM_DTYPE), scratch_shapes=[pltpu.VMEM((fs, d), F32)],
        compiler_params=_params(("parallel", "arbitrary")))(s, dh)

    def gate_up_b(da_ref, du_ref, wg_ref, wu_ref, x_ref, dy_ref, g_ref, dx_ref, dg_ref, acc):
        i, q = pl.program_id(0), pl.program_id(1)

        @pl.when((i == 0) & (q == 0))
        def _():
            dg_ref[...] = jnp.zeros(dg_ref.shape, F32)

        @pl.when(q == 0)
        def _():
            acc[...] = jnp.zeros(acc.shape, F32)

        for r in _row_chunks(tm):
            acc[r, :] += (jnp.dot(da_ref[0, r, :], wg_ref[0], preferred_element_type=F32)
                          + jnp.dot(du_ref[0, r, :], wu_ref[0], preferred_element_type=F32))

        @pl.when(q == ns - 1)
        def _():
            for r in _row_chunks(tm):
                xb = x_ref[r, :]
                dx, dg = _rms_bwd(xb, _rms_stats(xb), g_ref[...], acc[r, :])
                dx_ref[r, :] = dy_ref[r, :] + dx
                dg_ref[...] += dg

    def gate_up_w(n_ref, da_ref, du_ref, dwg_ref, dwu_ref, acc_g, acc_u):
        kk = pl.program_id(1)

        @pl.when(kk == 0)
        def _():
            acc_g[...] = jnp.zeros(acc_g.shape, F32)
            acc_u[...] = jnp.zeros(acc_u.shape, F32)

        nb = n_ref[...]
        acc_g[...] += lax.dot_general(da_ref[0], nb, tn_dims, preferred_element_type=F32)
        acc_u[...] += lax.dot_general(du_ref[0], nb, tn_dims, preferred_element_type=F32)

        @pl.when(kk == nk - 1)
        def _():
            dwg_ref[0] = acc_g[...].astype(dwg_ref.dtype)
            dwu_ref[0] = acc_u[...].astype(dwu_ref.dtype)

    k3 = pl.BlockSpec((1, tk, fs), lambda q, kk: (q, kk, 0))
    wout = pl.BlockSpec((1, fs, d), lambda q, kk: (q, 0, 0))
    dwg, dwu = pl.pallas_call(
        gate_up_w, name=tag + "_gate_up_w", grid=(ns, nk),
        in_specs=[pl.BlockSpec((tk, d), lambda q, kk: (kk, 0)), k3, k3], out_specs=[wout, wout],
        out_shape=[jax.ShapeDtypeStruct((ns, fs, d), MM_DTYPE)] * 2,
        scratch_shapes=[pltpu.VMEM((fs, d), F32)] * 2,
        compiler_params=_params(("parallel", "arbitrary")))(n, da, du)

    carried_up = make_carried_up(dwg, dwu, dwd) if make_carried_up else None
    gate_up_b, x_in, x_out, x_shape, x_scr, x_args = _carry(gate_up_b, 7, 2, (nt, ns), carried_up)
    dx, dg_pre, *from_up = pl.pallas_call(
        gate_up_b, name=tag + "_gate_up_b", grid=(nt, ns), in_specs=[act3, act3, wrow, wrow, row, row, vec] + x_in,
        out_specs=[row, vec] + x_out,
        out_shape=[jax.ShapeDtypeStruct((t, d), F32), jax.ShapeDtypeStruct((1, d), F32)] + x_shape,
        scratch_shapes=[pltpu.VMEM((tm, d), F32)] + x_scr,
        compiler_params=_params(("arbitrary", "arbitrary")))(da, du, wg, wu, x, dy, g_pre, *x_args)
    return dx, dg_pre, dwg, dwu, dwd, dg_post, from_down, from_up


NEG = -1e30


def _attn_scale():
    return (MLA_NOPE + MLA_ROPE) ** -0.5


def _causal_pairs(nq, by_key):
    if by_key:
        pairs = [(qi, ki) for ki in range(nq) for qi in range(ki, nq)]
    else:
        pairs = [(qi, ki) for qi in range(nq) for ki in range(qi + 1)]
    return jnp.asarray([p[0] for p in pairs], jnp.int32), jnp.asarray([p[1] for p in pairs], jnp.int32)


def _below_diagonal(shape):
    return lax.broadcasted_iota(jnp.int32, shape, 1) <= lax.broadcasted_iota(jnp.int32, shape, 0)


def _attn_call(name, body, tables, args, in_kinds, out_kinds, scratch, t, tq, carried=None):
    qmap = lambda h, p, qt, kt: (qt[p], h)
    kmap = lambda h, p, qt, kt: (kt[p], h)
    width = HEADS_PER_STEP * HEAD_LANES
    spec = lambda kind: pl.BlockSpec((tq, width), qmap if kind == "q" else kmap)
    n_pairs = tables[0].shape[0]
    n_groups = N_HEADS // HEADS_PER_STEP
    n_in, n_out, n_scr = len(in_kinds), len(out_kinds), scratch
    x_ins = list(carried.ins) if carried else []
    x_outs = list(carried.outs) if carried else []
    x_scr = [pltpu.SemaphoreType.DMA((carried.n_sem,)), pltpu.SemaphoreType.DMA((carried.n_sem,))] if carried else []

    def full_body(qt, kt, *refs):
        ins, refs = refs[:n_in], refs[n_in:]
        xi, refs = refs[:len(x_ins)], refs[len(x_ins):]
        outs, refs = refs[:n_out], refs[n_out:]
        xo, refs = refs[:len(x_outs)], refs[len(x_outs):]
        scr, sems = refs[:n_scr], refs[n_scr:]
        if carried:
            @pl.when((pl.program_id(0) == 0) & (pl.program_id(1) == 0))
            def _():
                carried.start(xi, xo, *sems)

        heads = [tuple(r.at[:, pl.ds(hh * HEAD_LANES, HEAD_LANES)] for r in (*ins, *outs, *scr))
                 for hh in range(HEADS_PER_STEP)]
        body(qt, kt, heads)
        if carried:
            @pl.when((pl.program_id(0) == n_groups - 1) & (pl.program_id(1) == n_pairs - 1))
            def _():
                carried.finish(xi, xo, *sems)

    grid_spec = pltpu.PrefetchScalarGridSpec(
        num_scalar_prefetch=2, grid=(n_groups, n_pairs),
        in_specs=[spec(kd) for kd in in_kinds] + [HBM_SPEC] * len(x_ins),
        out_specs=[spec(kd) for kd in out_kinds] + [HBM_SPEC] * len(x_outs),
        scratch_shapes=[pltpu.VMEM((tq, width), F32)] * n_scr + x_scr)
    return pl.pallas_call(full_body, name=name, grid_spec=grid_spec,
                          out_shape=[jax.ShapeDtypeStruct((t, MLA_PAD), F32) for _ in out_kinds] + x_outs,
                          compiler_params=_params(("arbitrary", "arbitrary")))(*tables, *args, *x_ins)


class _Carried:
    def __init__(self, ins, outs, n_sem, start, finish):
        self.ins, self.outs, self.n_sem, self.start, self.finish = ins, outs, n_sem, start, finish


def _attn_fwd(q, k, v, tq, carried=None):
    t = q.shape[0]
    nq = t // tq

    def body(qt, kt, heads):
        p_id = pl.program_id(1)
        qi, ki = qt[p_id], kt[p_id]

        @pl.when(ki == 0)
        def _():
            for _, _, _, _, _, m_s, l_s, acc_s in heads:
                m_s[...] = jnp.full(m_s.shape, NEG, F32)
                l_s[...] = jnp.zeros(l_s.shape, F32)
                acc_s[...] = jnp.zeros(acc_s.shape, F32)

        def update(diagonal):
            for q_ref, k_ref, v_ref, _, _, m_s, l_s, acc_s in heads:
                s = lax.dot_general(q_ref[...], k_ref[...], (((1,), (1,)), ((), ())), preferred_element_type=F32)
                if diagonal:
                    s = jnp.where(_below_diagonal(s.shape), s, NEG)
                m_old = m_s[...]
                m_new = jnp.maximum(m_old, jnp.max(s, axis=1, keepdims=True))
                alpha = jnp.exp(m_old - m_new)
                p = jnp.exp(s - m_new[:, :1])
                l_s[...] = l_s[...] * alpha + jnp.sum(p, axis=1, keepdims=True)
                acc_s[...] = acc_s[...] * alpha + jnp.dot(p.astype(MM_DTYPE), v_ref[...], preferred_element_type=F32)
                m_s[...] = m_new

        @pl.when(ki < qi)
        def _():
            update(False)

        @pl.when(ki == qi)
        def _():
            update(True)
            for _, _, _, o_ref, lse_ref, m_s, l_s, acc_s in heads:
                o_ref[...] = acc_s[...] / l_s[...]
                lse_ref[...] = m_s[...] + jnp.log(l_s[...])

    return _attn_call("mla_attn_fwd", body, _causal_pairs(nq, False), (q, k, v), "qkk", "qq", 3, t, tq, carried)


def _attn_probs(q, k, lse, diagonal):
    s = lax.dot_general(q, k, (((1,), (1,)), ((), ())), preferred_element_type=F32)
    p = jnp.exp(s - lse[:, :1])
    return jnp.where(_below_diagonal(s.shape), p, 0.0) if diagonal else p


def _attn_bwd_dq(q, k, v, do, lse, delta, tq):
    t = q.shape[0]
    nq = t // tq

    def body(qt, kt, heads):
        p_id = pl.program_id(1)
        qi, ki = qt[p_id], kt[p_id]

        @pl.when(ki == 0)
        def _():
            for refs in heads:
                refs[-1][...] = jnp.zeros(refs[-1].shape, F32)

        def step(diagonal):
            for q_ref, k_ref, v_ref, do_ref, lse_ref, dl_ref, _, acc_s in heads:
                p = _attn_probs(q_ref[...], k_ref[...], lse_ref[...], diagonal)
                dp = lax.dot_general(do_ref[...], v_ref[...], (((1,), (1,)), ((), ())), preferred_element_type=F32)
                ds = p * (dp - dl_ref[...][:, :1])
                acc_s[...] += jnp.dot(ds.astype(MM_DTYPE), k_ref[...], preferred_element_type=F32)

        @pl.when(ki < qi)
        def _():
            step(False)

        @pl.when(ki == qi)
        def _():
            step(True)
            for refs in heads:
                refs[-2][...] = refs[-1][...]

    return _attn_call("mla_attn_bwd_dq", body, _causal_pairs(nq, False), (q, k, v, do, lse, delta), "qkkqqq", "q",
                      1, t, tq)[0]


def _attn_bwd_dkv(q, k, v, do, lse, delta, tq, carried=None):
    t = q.shape[0]
    nq = t // tq

    def body(qt, kt, heads):
        p_id = pl.program_id(1)
        qi, ki = qt[p_id], kt[p_id]

        def step(diagonal):
            for q_ref, k_ref, v_ref, do_ref, lse_ref, dl_ref, _, _, dk_s, dv_s in heads:
                p = _attn_probs(q_ref[...], k_ref[...], lse_ref[...], diagonal)
                dv_s[...] += lax.dot_general(p.astype(MM_DTYPE), do_ref[...], (((0,), (0,)), ((), ())),
                                             preferred_element_type=F32)
                dp = lax.dot_general(do_ref[...], v_ref[...], (((1,), (1,)), ((), ())), preferred_element_type=F32)
                ds = p * (dp - dl_ref[...][:, :1])
                dk_s[...] += lax.dot_general(ds.astype(MM_DTYPE), q_ref[...], (((0,), (0,)), ((), ())),
                                             preferred_element_type=F32)

        @pl.when(qi == ki)
        def _():
            for refs in heads:
                refs[-2][...] = jnp.zeros(refs[-2].shape, F32)
                refs[-1][...] = jnp.zeros(refs[-1].shape, F32)
            step(True)

        @pl.when(qi > ki)
        def _():
            step(False)

        @pl.when(qi == nq - 1)
        def _():
            for refs in heads:
                refs[-4][...] = refs[-2][...]
                refs[-3][...] = refs[-1][...]

    return _attn_call("mla_attn_bwd_dkv", body, _causal_pairs(nq, True), (q, k, v, do, lse, delta), "qkkqqq", "kk",
                      2, t, tq, carried)


def _dot01(a, b, dims=(((1,), (0,)), ((), ())), ones="rhs"):
    val, sel = (a, b) if ones == "rhs" else (b, a)
    head = val.astype(BF16)
    tail = (val - head.astype(F32)).astype(BF16)
    sel = sel.astype(BF16)
    dot = lambda part: (lax.dot_general(part, sel, dims, preferred_element_type=F32) if ones == "rhs"
                        else lax.dot_general(sel, part, dims, preferred_element_type=F32))
    return dot(head) + dot(tail)


def _dot1(a, b, dims=(((1,), (0,)), ((), ()))):
    return lax.dot_general(a.astype(MM_DTYPE), b.astype(MM_DTYPE), dims, preferred_element_type=F32)


def _dot3(a, b, dims=(((1,), (0,)), ((), ()))):
    return lax.dot_general(a, b, dims, preferred_element_type=F32, precision=lax.Precision.HIGH)


NN3 = (((2,), (1,)), ((0,), (0,)))
NT3 = (((2,), (2,)), ((0,), (0,)))
TN3 = (((1,), (1,)), ((0,), (0,)))


def _tri_masks(nh):
    shape = (nh, CHUNK, CHUNK)
    return lax.broadcasted_iota(jnp.int32, shape, 1), lax.broadcasted_iota(jnp.int32, shape, 2)


def _gdn_chunk_common(k, gcc, bb, row, col, dot=_dot1):
    tril = row >= col
    gcr = jnp.swapaxes(gcc, 1, 2)
    dm = jnp.exp(jnp.where(tril, gcc - gcr, NEG))
    kb = k * bb
    lm = jnp.where(row > col, dot(kb, k, NT3) * dm, 0.0)
    return dm, kb, lm


def _unit_lower_inverse(lm, eye):
    t = eye - lm
    p = lm
    for _ in range(CHUNK.bit_length() - 2):
        p = _dot3(p, p, NN3)
        t = t + _dot3(t, p, NN3)
    return t


def _chunk_sum_matrix(tb, upper):
    r = lax.broadcasted_iota(jnp.int32, (tb, tb), 0)
    c = lax.broadcasted_iota(jnp.int32, (tb, tb), 1)
    same = (r // CHUNK) == (c // CHUNK)
    return (same & ((c >= r) if upper else (c <= r))).astype(F32)


def _gdn_fwd(q, k, v, gb, bb):
    nh, t, dh = q.shape
    nchunk = t // CHUNK

    def body(q_ref, k_ref, v_ref, g_ref, b_ref, o_ref, sall_ref, tall_ref, s_s):
        @pl.when(pl.program_id(0) == 0)
        def _():
            s_s[...] = jnp.zeros(s_s.shape, F32)

        row, col = _tri_masks(nh)
        qh, kh, vh, bbh, gcc = q_ref[...], k_ref[...], v_ref[...], b_ref[...], g_ref[...]
        dm, kb, lm = _gdn_chunk_common(kh, gcc, bbh, row, col)
        eg = jnp.exp(gcc)
        glr = gcc[:, CHUNK - 1:CHUNK, :]
        th = _unit_lower_inverse(lm, (row == col).astype(F32))
        w = _dot1(th, kb * eg, NN3)
        u = _dot1(th, vh * bbh, NN3)
        at = jnp.where(row >= col, _dot1(qh, kh, NT3) * dm, 0.0)
        sh = s_s[...]
        vn = u - _dot1(w, sh, NN3)
        o_ref[...] = _dot1(qh * eg, sh, NN3) + _dot1(at, vn, NN3)
        kd = kh * jnp.exp(glr - gcc)
        sall_ref[:, 0] = sh
        tall_ref[...] = th
        s_s[...] = sh * jnp.exp(glr) + _dot1(kd, vn, TN3)

    blk = pl.BlockSpec((nh, CHUNK, dh), lambda n: (0, n, 0))
    return pl.pallas_call(
        body, name="gdn_fwd", grid=(nchunk,), in_specs=[blk] * 5,
        out_specs=[blk, pl.BlockSpec((nh, 1, dh, dh), lambda n: (0, n, 0, 0)), blk],
        out_shape=[jax.ShapeDtypeStruct((nh, t, dh), F32), jax.ShapeDtypeStruct((nh, nchunk, dh, dh), F32),
                   jax.ShapeDtypeStruct((nh, t, CHUNK), F32)],
        scratch_shapes=[pltpu.VMEM((nh, dh, dh), F32)],
        compiler_params=_params(("arbitrary",)))(q, k, v, gb, bb)


def _gdn_bwd(q, k, v, gb, bb, sall, tall, do):
    nh, t, dh = q.shape
    nchunk = t // CHUNK

    def body(q_ref, k_ref, v_ref, g_ref, b_ref, sall_ref, tall_ref, do_ref,
             dq_ref, dk_ref, dv_ref, dg_ref, db_ref, ds_s):
        @pl.when(pl.program_id(0) == 0)
        def _():
            ds_s[...] = jnp.zeros(ds_s.shape, F32)

        row, col = _tri_masks(nh)
        tril, stril = row >= col, row > col
        rsum = lambda x: jnp.sum(x, axis=2, keepdims=True)
        qh, kh, vh, gcc, bbh = q_ref[...], k_ref[...], v_ref[...], g_ref[...], b_ref[...]
        sh, th, doh, dsp = sall_ref[:, 0], tall_ref[...], do_ref[...], ds_s[...]
        dm, kb, lm = _gdn_chunk_common(kh, gcc, bbh, row, col, _dot3)
        eg = jnp.exp(gcc)
        glr = gcc[:, CHUNK - 1:CHUNK, :]
        glv = jnp.exp(glr)
        egl = jnp.exp(glr - gcc)
        rw, ru = kb * eg, vh * bbh
        w, u = _dot3(th, rw, NN3), _dot3(th, ru, NN3)
        at = jnp.where(tril, _dot3(qh, kh, NT3) * dm, 0.0)
        qd, kd = qh * eg, kh * egl
        vn = u - _dot3(w, sh, NN3)
        dgl = jnp.sum(rsum(dsp * sh), axis=1, keepdims=True)
        dkd = _dot3(vn, dsp, NT3)
        dvn = _dot3(kd, dsp, NN3)
        dqd = _dot3(doh, sh, NT3)
        dat = jnp.where(tril, _dot3(doh, vn, NT3), 0.0)
        dvn = dvn + _dot3(at, doh, TN3)
        dw = -_dot3(dvn, sh, NT3)
        ds_s[...] = dsp * glv + _dot3(qd, doh, TN3) - _dot3(w, dvn, TN3)
        dpa = dat * dm
        dq_ref[...] = _dot1(dpa, kh, NN3) + dqd * eg
        dk = _dot1(dpa, qh, TN3) + dkd * egl
        t6 = rsum(dkd * kd)
        dgam = rsum(dqd * qd) - t6
        dgam_last = jnp.sum(t6, axis=1, keepdims=True) + dgl * glv
        drw = _dot3(th, dw, TN3)
        dru = _dot3(th, dvn, TN3)
        dl = -jnp.where(stril, _dot3(drw, w, NT3) + _dot3(dru, u, NT3), 0.0)
        dgam = dgam + rsum(drw * rw)
        dv_ref[...] = dru * bbh
        dp2 = dl * dm
        dkb = drw * eg + _dot1(dp2, kh, NN3)
        dk_ref[...] = dk + _dot1(dp2, kb, TN3) + dkb * bbh
        db_ref[...] = rsum(dru * vh) + rsum(dkb * kh) + jnp.zeros((nh, CHUNK, dh), F32)
        e = dat * at + dl * lm
        dgam_b = dgam + rsum(e) - _dot01(e, jnp.ones((nh, CHUNK, CHUNK), F32), TN3)
        dgam_b = dgam_b + jnp.where(row == CHUNK - 1, dgam_last, 0.0)
        dg_ref[...] = dgam_b

    rev = lambda n: (0, nchunk - 1 - n, 0)
    blk = pl.BlockSpec((nh, CHUNK, dh), rev)
    sblk = pl.BlockSpec((nh, 1, dh, dh), lambda n: (0, nchunk - 1 - n, 0, 0))
    out = jax.ShapeDtypeStruct((nh, t, dh), F32)
    return pl.pallas_call(
        body, name="gdn_bwd", grid=(nchunk,), in_specs=[blk] * 5 + [sblk, blk, blk], out_specs=[blk] * 5,
        out_shape=[out] * 5, scratch_shapes=[pltpu.VMEM((nh, dh, dh), F32)],
        compiler_params=_params(("arbitrary",)))(q, k, v, gb, bb, sall, tall, do)


def _group_ones():
    r = lax.broadcasted_iota(jnp.int32, (GDN_W, GDN_W), 0) // GDN_DH
    c = lax.broadcasted_iota(jnp.int32, (GDN_W, GDN_W), 1) // GDN_DH
    return (r == c).astype(F32)


def _conv_taps(x, xprev, w, has_prev):
    row = lax.broadcasted_iota(jnp.int32, x.shape, 0)
    out = x * w[GDN_CONV - 1:GDN_CONV, :]
    for s in range(1, GDN_CONV):
        sh = jnp.where(row >= s, _roll(x, s, 0), _roll(xprev, s, 0) * has_prev)
        out = out + sh * w[GDN_CONV - 1 - s:GDN_CONV - s, :]
    return out


def _head_cols(x, h):
    return x[:, h * GDN_DH:(h + 1) * GDN_DH]


def _heads_spec(tb):
    return pl.BlockSpec((N_HEADS, tb, GDN_DH), lambda i: (0, i, 0))


def _mixer_fwd(x, positions, w, tb, carried=None):
    t, d = x.shape
    tables = _rope_tables(positions)

    def pre(xb, g):
        return (xb * _rms_stats(xb) * g,)

    (hn,) = _rowwise("mix_pre", pre, [x], [w["mix_pre_g"]], [(d, BF16)], [], tb)
    proj = _mm("mix_in", hn, w["w_in_pad_t"], "nt", F32)

    def mla_pre(p0, gq, gkv):
        cq, ckv = p0[:, :MLA_Q_RANK], p0[:, MLA_Q_RANK:MLA_Q_RANK + MLA_KV_RANK]
        return cq * _rms_stats(cq) * gq, ckv * _rms_stats(ckv) * gkv

    nq, nkv = _rowwise("mla_pre", mla_pre, [(proj, 512, PIN_MLA // 512, 0)],
                       [w["mla_q_norm_g"], w["mla_kv_norm_g"]], [(MLA_Q_RANK, BF16), (MLA_KV_RANK, BF16)], [], tb)
    qraw = _mm("mla_uq", nq, w["w_uq_pad"], "nn", F32)
    kv = _mm("mla_ukv", nkv, w["w_kv_pad"], "nn", F32)

    def rope_f(qr, kn, vv, kpe, c, s1, s2):
        qo = _heads_apply(qr, lambda xh: _rope(xh, c, s1, s2)) * _attn_scale()
        kp = _rope(kpe, c, s1, s2)
        return qo, kn + jnp.tile(kp, (1, N_HEADS)), vv

    q, k, v = _rowwise("mla_rope", rope_f,
                       [qraw, (kv, MLA_PAD, 0, 0), (kv, MLA_PAD, 1, 0), (proj, HEAD_LANES, PIN_KPE // HEAD_LANES, 0),
                        tables[0], tables[1], tables[2]], [],
                       [(MLA_PAD, BF16)] * 3, [], tb // 2)
    tq = min(512, t)
    o, lse, *carried_out = _attn_fwd(q, k, v, tq, carried)

    def mla_post(ob, g):
        return (ob * _rms_stats(ob, N_HEADS * MLA_V) * g,)

    (cat,) = _rowwise("mla_post", mla_post, [o], [w["mla_out_g_pad"]], [(MLA_PAD, BF16)], [], tb, wide=(CAT_W, 0))

    gones = _group_ones()
    steps = t // tb

    def gdn_pre(xq, xk, xv, pq, pk, pv, cw, go, has_prev):
        outs = []
        for j, (xc, xp) in enumerate(((xq, pq), (xk, pk), (xv, pv))):
            c = _conv_taps(xc, xp, cw[:, j * GDN_W:(j + 1) * GDN_W], has_prev)
            a = c * _sigmoid(c)
            if j < 2:
                rn = lax.rsqrt(_dot01(a * a, go) + EPS)
                a = a * rn
                if j == 0:
                    a = a * (GDN_DH ** -0.5)
            outs.append(a)
        return tuple(outs)

    qh, kh, vh = _gdn_pre_call("gdn_pre", gdn_pre, proj, w["conv_w"], gones, tb, steps)
    heads_shape = jax.ShapeDtypeStruct((N_HEADS, t, GDN_DH), F32)
    lanes_shape = jax.ShapeDtypeStruct((t, HEAD_LANES), F32)
    lanes_spec = pl.BlockSpec((tb, HEAD_LANES), lambda i: (i, 0))
    vec_spec = lambda n: pl.BlockSpec((1, n), lambda i: (0, 0))

    def gate_f(ab_ref, al_ref, dt_ref, g_ref, b_ref, gh_ref, bh_ref):
        g, b = _gb_fwd(ab_ref[...], al_ref[...], dt_ref[...])
        g_ref[...] = g
        b_ref[...] = b
        gc = _dot01(_chunk_sum_matrix(tb, False), g, ones="lhs")
        for h in range(N_HEADS):
            gh_ref[h] = jnp.broadcast_to(gc[:, h:h + 1], (tb, GDN_DH))
            bh_ref[h] = jnp.broadcast_to(b[:, N_HEADS + h:N_HEADS + h + 1], (tb, GDN_DH))

    g128, b128, gbh, bbh = pl.pallas_call(
        gate_f, name="gdn_gate_f", grid=(steps,),
        in_specs=[pl.BlockSpec((tb, HEAD_LANES), lambda i: (i, PIN_AB // HEAD_LANES)), vec_spec(HEAD_LANES),
                  vec_spec(HEAD_LANES)],
        out_specs=[lanes_spec, lanes_spec, _heads_spec(tb), _heads_spec(tb)],
        out_shape=[lanes_shape, lanes_shape, heads_shape, heads_shape],
        compiler_params=_params(("arbitrary",)))(proj, w["a_log_pad"], w["dt_bias_pad"])
    oh, sall, tall = _gdn_fwd(qh, kh, vh, gbh, bbh)

    def gdn_post(o_ref, gt_ref, g_ref, cat_in, cat_ref):
        gt, g = gt_ref[...], g_ref[...]
        outs = []
        for h in range(N_HEADS):
            ob, gth = o_ref[h], _head_cols(gt, h)
            outs.append(ob * _rms_stats(ob) * g * (gth * _sigmoid(gth)))
        cat_ref[...] = jnp.concatenate(outs, axis=1).astype(cat_ref.dtype)

    gate_spec = pl.BlockSpec((tb, GDN_W), lambda i: (i, PIN_GATE // GDN_W))
    cat = pl.pallas_call(
        gdn_post, name="gdn_post", grid=(steps,),
        in_specs=[_heads_spec(tb), gate_spec, vec_spec(GDN_DH), ANY_SPEC],
        out_specs=pl.BlockSpec((tb, GDN_W), lambda i: (i, MLA_PAD // GDN_W)),
        out_shape=jax.ShapeDtypeStruct((t, CAT_W), BF16), input_output_aliases={3: 0},
        compiler_params=_params(("arbitrary",)))(oh, proj, w["gdn_norm_g"], cat)
    mixed = _mm("mix_out", cat, w["w_out_pad"], "nn", F32)

    def post(xb, hb, g):
        return (xb + hb * _rms_stats(hb) * g,)

    (y,) = _rowwise("mix_post", post, [x, mixed], [w["mix_post_g"]], [(d, F32)], [], tb)
    saved = dict(x=x, hn=hn, proj=proj, nq=nq, nkv=nkv, q=q, k=k, v=v, o=o, lse=lse, qh=qh, kh=kh, vh=vh,
                 gbh=gbh, bbh=bbh, oh=oh, sall=sall, tall=tall, cat=cat, mixed=mixed,
                 tables=tables, g128=g128, b128=b128)
    return y, saved, carried_out


def _qkv_specs(tb):
    base = PIN_QKV // GDN_W
    cur = [pl.BlockSpec((tb, GDN_W), lambda i, j=j: (i, base + j)) for j in range(3)]
    prev = [pl.BlockSpec((tb, GDN_W), lambda i, j=j: (jnp.maximum(i - 1, 0), base + j)) for j in range(3)]
    return cur + prev


def _gdn_pre_call(name, fn, proj, conv_w, gones, tb, steps):
    t = proj.shape[0]

    def body(xq, xk, xv, pq, pk, pv, cw, go, oq, ok, ov):
        has_prev = jnp.where(pl.program_id(0) == 0, 0.0, 1.0)
        outs = fn(xq[...], xk[...], xv[...], pq[...], pk[...], pv[...], cw[...], go[...], has_prev)
        for r, val in zip((oq, ok, ov), outs):
            for h in range(N_HEADS):
                r[h] = _head_cols(val, h)

    return pl.pallas_call(
        body, name=name, grid=(steps,),
        in_specs=_qkv_specs(tb) + [pl.BlockSpec(conv_w.shape, lambda i: (0, 0)),
                                   pl.BlockSpec(gones.shape, lambda i: (0, 0))],
        out_specs=[_heads_spec(tb)] * 3,
        out_shape=[jax.ShapeDtypeStruct((N_HEADS, t, GDN_DH), F32)] * 3,
        compiler_params=_params(("arbitrary",)))(proj, proj, proj, proj, proj, proj, conv_w, gones)


def _softplus(x):
    return jnp.maximum(x, 0.0) + jnp.log1p(jnp.exp(-jnp.abs(x)))


def _gb_fwd(ab, a_log, dt_bias):
    g = -jnp.exp(a_log) * _softplus(ab + dt_bias)
    return g, _sigmoid(ab)


def _rope_tables(positions):
    half = MLA_ROPE // 2
    freqs = ROPE_THETA ** (-jnp.arange(half, dtype=F32) / half)
    ang = positions.reshape(-1).astype(F32)[:, None] * freqs
    cos, sin = jnp.cos(ang), jnp.sin(ang)
    t = ang.shape[0]
    one = jnp.ones((t, MLA_NOPE), F32)
    z16, z32, z64 = jnp.zeros((t, half), F32), jnp.zeros((t, MLA_ROPE), F32), jnp.zeros((t, MLA_NOPE), F32)
    c = jnp.concatenate([one, cos, cos, jnp.ones((t, MLA_ROPE), F32)], axis=1)
    s1 = jnp.concatenate([z64, -sin, z16, z32], axis=1)
    s2 = jnp.concatenate([z64, z16, sin, z32], axis=1)
    return c, s1, s2


def _mixer_bwd(dy, sv, w, tb, carried=None):
    x, proj = sv["x"], sv["proj"]
    t, d = x.shape
    c, s1, s2 = sv["tables"]
    grads = {}

    def post_b(hb, dyb, g):
        return _rms_bwd(hb, _rms_stats(hb), g, dyb)

    dmixed, grads["mix_post_g"] = _rowwise("mix_post_b", post_b, [sv["mixed"], dy], [w["mix_post_g"]],
                                           [(d, BF16)], [(1, d)], tb)
    dcat = _mm("mix_out_bx", dmixed, w["w_out_pad"], "nt", F32)
    grads["w_out_pad"] = _mm("mix_out_bw", sv["cat"], dmixed, "tn", F32)
    steps = t // tb
    vec_spec = lambda n: pl.BlockSpec((1, n), lambda i: (0, 0))

    def gdn_post_b(o_ref, gt_ref, do_ref, g_ref, dproj_ref, doh_ref, dg_ref):
        @pl.when(pl.program_id(0) == 0)
        def _():
            dg_ref[...] = jnp.zeros(dg_ref.shape, F32)

        gt, dob, g = gt_ref[...], do_ref[...], g_ref[...]
        dgates = []
        for h in range(N_HEADS):
            ob, gth, dobh = o_ref[h], _head_cols(gt, h), _head_cols(dob, h)
            sg = _sigmoid(gth)
            r = _rms_stats(ob)
            dxo, dg = _rms_bwd(ob, r, g, dobh * (gth * sg))
            doh_ref[h] = dxo
            dg_ref[...] += dg
            dgates.append(dobh * (ob * r * g) * (sg * (1.0 + gth * (1.0 - sg))))
        dproj_ref[...] = jnp.concatenate(dgates, axis=1).astype(dproj_ref.dtype)

    dproj, doh, grads["gdn_norm_g"] = pl.pallas_call(
        gdn_post_b, name="gdn_post_b", grid=(steps,),
        in_specs=[_heads_spec(tb), pl.BlockSpec((tb, GDN_W), lambda i: (i, PIN_GATE // GDN_W)),
                  pl.BlockSpec((tb, GDN_W), lambda i: (i, MLA_PAD // GDN_W)), vec_spec(GDN_DH)],
        out_specs=[pl.BlockSpec((tb, GDN_W), lambda i: (i, PIN_GATE // GDN_W)), _heads_spec(tb), vec_spec(GDN_DH)],
        out_shape=[jax.ShapeDtypeStruct((t, PIN_W), BF16), jax.ShapeDtypeStruct((N_HEADS, t, GDN_DH), F32),
                   jax.ShapeDtypeStruct((1, GDN_DH), F32)],
        compiler_params=_params(("arbitrary",)))(sv["oh"], proj, dcat, w["gdn_norm_g"])

    def mla_post_b(ob, dmo, g):
        do, dg = _rms_bwd(ob, _rms_stats(ob, N_HEADS * MLA_V), g, dmo, N_HEADS * MLA_V)
        prod = do * ob
        delta = _heads_apply(prod, lambda ph: jnp.sum(ph, axis=1, keepdims=True) + jnp.zeros_like(ph))
        return do, delta, dg

    do, delta, grads["mla_out_g_pad"] = _rowwise(
        "mla_post_b", mla_post_b, [sv["o"], (dcat, MLA_PAD, 0, 0)], [w["mla_out_g_pad"]],
        [(MLA_PAD, BF16), (MLA_PAD, F32)], [(1, MLA_PAD)], tb // 2)
    tq = min(512, t)
    dq = _attn_bwd_dq(sv["q"], sv["k"], sv["v"], do, sv["lse"], delta, tq)
    dk, dv, *carried_out = _attn_bwd_dkv(sv["q"], sv["k"], sv["v"], do, sv["lse"], delta, tq, carried)

    def rope_b(dqb, dkb, dvb, cc, a1, a2):
        dqr = _heads_apply(dqb * _attn_scale(), lambda xh: _rope(xh, cc, -a1, -a2))
        ksum = dkb[:, :HEAD_LANES]
        for h in range(1, N_HEADS):
            ksum = ksum + dkb[:, h * HEAD_LANES:(h + 1) * HEAD_LANES]
        lane = lax.broadcasted_iota(jnp.int32, ksum.shape, 1)
        keep = (lane >= MLA_NOPE) & (lane < MLA_NOPE + MLA_ROPE)
        dkpe = jnp.where(keep, _rope(ksum, cc, -a1, -a2), 0.0)
        return dqr, jnp.concatenate([dkb, dvb], axis=1), dkpe

    dqraw, dkv, dkpe = _rowwise("mla_rope_b", rope_b, [dq, dk, dv, c, s1, s2], [],
                                [(MLA_PAD, BF16), (2 * MLA_PAD, BF16), (HEAD_LANES, F32)], [], tb // 2)
    dnq = _mm("mla_uq_bx", dqraw, w["w_uq_pad"], "nt", F32)
    grads["w_uq_pad"] = _mm("mla_uq_bw", sv["nq"], dqraw, "tn", F32)
    dnkv = _mm("mla_ukv_bx", dkv, w["w_kv_pad"], "nt", F32)
    grads["w_kv_pad"] = _mm("mla_ukv_bw", sv["nkv"], dkv, "tn", F32)

    def mla_pre_b(p0, dnqb, dnkvb, dkpeb, gq, gkv):
        cq, ckv = p0[:, :MLA_Q_RANK], p0[:, MLA_Q_RANK:MLA_Q_RANK + MLA_KV_RANK]
        dcq, dgq = _rms_bwd(cq, _rms_stats(cq), gq, dnqb)
        dckv, dgkv = _rms_bwd(ckv, _rms_stats(ckv), gkv, dnkvb)
        return jnp.concatenate([dcq, dckv, dkpeb], axis=1), dgq, dgkv

    dproj, grads["mla_q_norm_g"], grads["mla_kv_norm_g"] = _rowwise(
        "mla_pre_b", mla_pre_b, [(proj, 512, PIN_MLA // 512, 0), dnq, dnkv, dkpe],
        [w["mla_q_norm_g"], w["mla_kv_norm_g"]], [(512, BF16)], [(1, MLA_Q_RANK), (1, MLA_KV_RANK)], tb,
        wide=(PIN_W, PIN_MLA // 512), carry=dproj)

    dqh, dkh, dvh, dgh, dbh = _gdn_bwd(sv["qh"], sv["kh"], sv["vh"], sv["gbh"], sv["bbh"], sv["sall"], sv["tall"], doh)
    gones = _group_ones()

    def gdn_pre_b(xq, xk, xv, pq, pk, pv, dq_, dk_, dv_, cw, go, has_prev):
        outs = []
        for j, (xc, xp, dd) in enumerate(((xq, pq, dq_), (xk, pk, dk_), (xv, pv, dv_))):
            cc = _conv_taps(xc, xp, cw[:, j * GDN_W:(j + 1) * GDN_W], has_prev)
            sg = _sigmoid(cc)
            a = cc * sg
            if j < 2:
                rn = lax.rsqrt(_dot01(a * a, go) + EPS)
                if j == 0:
                    dd = dd * (GDN_DH ** -0.5)
                da = rn * dd - a * (rn * rn * rn) * _dot01(dd * a, go)
            else:
                da = dd
            outs.append(da * (sg * (1.0 + cc * (1.0 - sg))))
        return tuple(outs)

    dcq, dck, dcv = _gdn_pre_b_call("gdn_pre_b", gdn_pre_b, proj, (dqh, dkh, dvh), w["conv_w"], gones, tb, steps)
    dproj, grads["conv_w"] = _conv_bwd_call("gdn_conv_b", proj, (dcq, dck, dcv), w["conv_w"], dproj, tb, steps)

    def gate_b(ab_ref, g_ref, b_ref, dgh_ref, dbh_ref, al_ref, dt_ref, carry_ref, dab_ref, dal_ref, ddt_ref):
        @pl.when(pl.program_id(0) == 0)
        def _():
            dal_ref[...] = jnp.zeros(dal_ref.shape, F32)
            ddt_ref[...] = jnp.zeros(ddt_ref.shape, F32)

        ab, g128, b128 = ab_ref[...], g_ref[...], b_ref[...]
        lane = lax.broadcasted_iota(jnp.int32, ab.shape, 1)
        dg_ = jnp.zeros(ab.shape, F32)
        db_ = jnp.zeros(ab.shape, F32)
        for h in range(N_HEADS):
            dg_ = dg_ + jnp.where(lane == h, jnp.broadcast_to(dgh_ref[h][:, 0:1], ab.shape), 0.0)
            db_ = db_ + jnp.where(lane == N_HEADS + h, jnp.broadcast_to(dbh_ref[h][:, 0:1], ab.shape), 0.0)
        dg_ = _dot01(_chunk_sum_matrix(tb, True), dg_, ones="lhs")
        slope = -jnp.exp(al_ref[...]) * _sigmoid(ab + dt_ref[...])
        dab_ref[...] = (dg_ * slope + db_ * b128 * (1.0 - b128)).astype(dab_ref.dtype)
        dal_ref[...] += jnp.sum(dg_ * g128, axis=0, keepdims=True)
        ddt_ref[...] += jnp.sum(dg_ * slope, axis=0, keepdims=True)

    lanes_spec = pl.BlockSpec((tb, HEAD_LANES), lambda i: (i, 0))
    ab_spec = pl.BlockSpec((tb, HEAD_LANES), lambda i: (i, PIN_AB // HEAD_LANES))
    dproj, grads["a_log_pad"], grads["dt_bias_pad"] = pl.pallas_call(
        gate_b, name="gdn_gate_b", grid=(steps,),
        in_specs=[ab_spec, lanes_spec, lanes_spec, _heads_spec(tb), _heads_spec(tb), vec_spec(HEAD_LANES),
                  vec_spec(HEAD_LANES), ANY_SPEC],
        out_specs=[ab_spec, vec_spec(HEAD_LANES), vec_spec(HEAD_LANES)],
        out_shape=[jax.ShapeDtypeStruct((t, PIN_W), BF16), jax.ShapeDtypeStruct((1, HEAD_LANES), F32),
                   jax.ShapeDtypeStruct((1, HEAD_LANES), F32)],
        input_output_aliases={7: 0},
        compiler_params=_params(("arbitrary",)))(proj, sv["g128"], sv["b128"], dgh, dbh, w["a_log_pad"],
                                                 w["dt_bias_pad"], dproj)
    dhn = _mm("mix_in_bx", dproj, w["w_in_pad_t"], "nn", F32)
    grads["w_in_pad_t"] = _mm("mix_in_bw", dproj, sv["hn"], "tn", F32)

    def pre_b(xb, dnb, dyb, g):
        dx, dg = _rms_bwd(xb, _rms_stats(xb), g, dnb)
        return dyb + dx, dg

    dx, grads["mix_pre_g"] = _rowwise("mix_pre_b", pre_b, [x, dhn, dy], [w["mix_pre_g"]], [(d, F32)], [(1, d)], tb)
    return dx, grads, carried_out


def _gdn_pre_b_call(name, fn, proj, dd, conv_w, gones, tb, steps):
    t = proj.shape[0]

    def body(xq, xk, xv, pq, pk, pv, d0, d1, d2, cw, go, oq, ok, ov):
        has_prev = jnp.where(pl.program_id(0) == 0, 0.0, 1.0)
        dd_rows = [jnp.concatenate([dr[h] for h in range(N_HEADS)], axis=1) for dr in (d0, d1, d2)]
        outs = fn(xq[...], xk[...], xv[...], pq[...], pk[...], pv[...], *dd_rows, cw[...], go[...], has_prev)
        for r, val in zip((oq, ok, ov), outs):
            r[...] = val

    return pl.pallas_call(
        body, name=name, grid=(steps,),
        in_specs=_qkv_specs(tb) + [_heads_spec(tb)] * 3 + [pl.BlockSpec(conv_w.shape, lambda i: (0, 0)),
                                                          pl.BlockSpec(gones.shape, lambda i: (0, 0))],
        out_specs=[pl.BlockSpec((tb, GDN_W), lambda i: (i, 0))] * 3,
        out_shape=[jax.ShapeDtypeStruct((t, GDN_W), F32)] * 3,
        compiler_params=_params(("arbitrary",)))(proj, proj, proj, proj, proj, proj, *dd, conv_w, gones)


def _conv_bwd_call(name, proj, dc, conv_w, dproj, tb, steps):
    t = proj.shape[0]
    dcur = [pl.BlockSpec((tb, GDN_W), lambda i: (i, 0))] * 3
    dnext = [pl.BlockSpec((tb, GDN_W), lambda i: (jnp.minimum(i + 1, steps - 1), 0))] * 3

    def body(xq, xk, xv, pq, pk, pv, d0, d1, d2, n0, n1, n2, cw, carry_ref, dx_ref, dw_ref):
        i = pl.program_id(0)
        has_prev = jnp.where(i == 0, 0.0, 1.0)
        has_next = jnp.where(i == steps - 1, 0.0, 1.0)

        @pl.when(i == 0)
        def _():
            dw_ref[...] = jnp.zeros(dw_ref.shape, F32)

        wv = cw[...]
        dws, dxs = [], []
        for j, (xr, pr, dr, nr) in enumerate(((xq, pq, d0, n0), (xk, pk, d1, n1), (xv, pv, d2, n2))):
            x, xp, dcv, dnx = xr[...], pr[...], dr[...], nr[...]
            wj = wv[:, j * GDN_W:(j + 1) * GDN_W]
            row = lax.broadcasted_iota(jnp.int32, x.shape, 0)
            dx = dcv * wj[GDN_CONV - 1:GDN_CONV, :]
            rows_w = [jnp.sum(dcv * x, axis=0, keepdims=True)]
            for s in range(1, GDN_CONV):
                up = jnp.where(row < tb - s, _roll(dcv, tb - s, 0), _roll(dnx, tb - s, 0) * has_next)
                dx = dx + up * wj[GDN_CONV - 1 - s:GDN_CONV - s, :]
                sh = jnp.where(row >= s, _roll(x, s, 0), _roll(xp, s, 0) * has_prev)
                rows_w.append(jnp.sum(dcv * sh, axis=0, keepdims=True))
            dxs.append(dx)
            dws.append(jnp.concatenate(rows_w[::-1], axis=0))
        dx_ref[...] = jnp.concatenate(dxs, axis=1).astype(dx_ref.dtype)
        dw_ref[...] += jnp.concatenate(dws, axis=1)

    return pl.pallas_call(
        body, name=name, grid=(steps,),
        in_specs=_qkv_specs(tb) + dcur + dnext + [pl.BlockSpec(conv_w.shape, lambda i: (0, 0)), ANY_SPEC],
        out_specs=[pl.BlockSpec((tb, 3 * GDN_W), lambda i: (i, PIN_QKV // (3 * GDN_W))),
                   pl.BlockSpec(conv_w.shape, lambda i: (0, 0))],
        out_shape=[jax.ShapeDtypeStruct((t, PIN_W), BF16), jax.ShapeDtypeStruct(conv_w.shape, F32)],
        input_output_aliases={13: 0},
        compiler_params=_params(("arbitrary",)))(proj, proj, proj, proj, proj, proj, *dc, *dc, conv_w, dproj)


def _pad_heads_cols(wm, per_head):
    r = wm.shape[0]
    return jnp.pad(wm.reshape(r, N_HEADS, per_head), ((0, 0), (0, 0), (0, HEAD_LANES - per_head))).reshape(r, MLA_PAD)


def _unpad_heads_cols(wm, per_head):
    r = wm.shape[0]
    return wm.reshape(r, N_HEADS, HEAD_LANES)[:, :, :per_head].reshape(r, N_HEADS * per_head)


W_IN_COLS = MLA_Q_RANK + MLA_KV_RANK + MLA_ROPE + 3 * GDN_W + 2 * N_HEADS + GDN_W
W_IN_SHARD = W_IN_COLS // N_SHARD
W_IN_SHARD_PAD = 640
_Q0 = MLA_Q_RANK + MLA_KV_RANK
_Q1 = _Q0 + MLA_ROPE
_Q2 = _Q1 + 3 * GDN_W
_Q3 = _Q2 + 2 * N_HEADS
W_IN_SEGMENTS = [(0, _Q0, PIN_MLA), (_Q0, _Q1, PIN_KPE + MLA_NOPE), (_Q1, _Q2, PIN_QKV), (_Q2, _Q3, PIN_AB),
                 (_Q3, W_IN_COLS, PIN_GATE)]


def _win_pad_t(slabs):
    d = slabs.shape[2]
    pieces, at = [], 0
    for c0, c1, r0 in sorted(W_IN_SEGMENTS, key=lambda s: s[2]):
        if r0 > at:
            pieces.append(jnp.zeros((r0 - at, d), slabs.dtype))
        for q in range(N_SHARD):
            lo, hi = max(c0, q * W_IN_SHARD), min(c1, (q + 1) * W_IN_SHARD)
            if lo < hi:
                pieces.append(slabs[q, lo - q * W_IN_SHARD:hi - q * W_IN_SHARD])
        at = r0 + c1 - c0
    pieces.append(jnp.zeros((PIN_W - at, d), slabs.dtype))
    return jnp.concatenate(pieces, axis=0)


def _win_cols_t(wp_t, c_lo, c_hi):
    pieces = []
    for c0, c1, r0 in W_IN_SEGMENTS:
        lo, hi = max(c0, c_lo), min(c1, c_hi)
        if lo < hi:
            pieces.append(wp_t[r0 + lo - c0:r0 + hi - c0])
    return jnp.concatenate(pieces, axis=0)


def _wkv_to_pad(wkv):
    r = wkv.shape[0]
    w3 = wkv.reshape(r, N_HEADS, MLA_NOPE + MLA_V)
    kpart = jnp.pad(w3[:, :, :MLA_NOPE], ((0, 0), (0, 0), (0, HEAD_LANES - MLA_NOPE))).reshape(r, MLA_PAD)
    vpart = jnp.pad(w3[:, :, MLA_NOPE:], ((0, 0), (0, 0), (0, HEAD_LANES - MLA_V))).reshape(r, MLA_PAD)
    return jnp.concatenate([kpart, vpart], axis=1)


def _wkv_from_pad(wp):
    r = wp.shape[0]
    kpart = wp[:, :MLA_PAD].reshape(r, N_HEADS, HEAD_LANES)[:, :, :MLA_NOPE]
    vpart = wp[:, MLA_PAD:].reshape(r, N_HEADS, HEAD_LANES)[:, :, :MLA_V]
    return jnp.concatenate([kpart, vpart], axis=2).reshape(r, N_HEADS * (MLA_NOPE + MLA_V))


def _wout_to_pad(wo):
    n = wo.shape[1]
    mla = jnp.pad(wo[:N_HEADS * MLA_V].reshape(N_HEADS, MLA_V, n), ((0, 0), (0, HEAD_LANES - MLA_V), (0, 0)))
    return jnp.concatenate([mla.reshape(MLA_PAD, n), wo[N_HEADS * MLA_V:]], axis=0)


def _wout_from_pad(wp):
    n = wp.shape[1]
    mla = wp[:MLA_PAD].reshape(N_HEADS, HEAD_LANES, n)[:, :MLA_V].reshape(N_HEADS * MLA_V, n)
    return jnp.concatenate([mla, wp[MLA_PAD:]], axis=0)


def _pad_lanes(v, n):
    return jnp.pad(v, ((0, 0), (0, n - v.shape[1])))


def _compute_weights(full):
    w = {}
    for n in FFN_BIG:
        if n in full:
            w[n] = full[n].astype(MM_DTYPE)
    w["w_in_pad_t"] = _win_pad_t(full["w_in"]).astype(MM_DTYPE)
    w["w_uq_pad"] = _pad_heads_cols(full["mla_w_uq"], MLA_NOPE + MLA_ROPE).astype(MM_DTYPE)
    w["w_kv_pad"] = _wkv_to_pad(full["mla_w_ukv"]).astype(MM_DTYPE)
    w["w_out_pad"] = _wout_to_pad(full["w_out"]).astype(MM_DTYPE)
    w["conv_w"] = full["gdn_conv_w"].astype(F32)
    for n in ("ffn1_pre_g", "ffn1_post_g", "mix_pre_g", "mla_q_norm_g", "mla_kv_norm_g", "gdn_norm_g", "mix_post_g",
              "ffn2_pre_g", "ffn2_post_g"):
        w[n] = full[n]
    w["mla_out_g_pad"] = _pad_heads_cols(full["mla_out_g"], MLA_V)
    w["a_log_pad"] = _pad_lanes(full["gdn_a_log"], HEAD_LANES)
    w["dt_bias_pad"] = _pad_lanes(full["gdn_dt_bias"], HEAD_LANES)
    return w


FFN2_BIG = FFN_BIG[3:]


def _local_step(x, positions, loss_target, full, late=None):
    t, d = x.shape
    tb = min(512, t)
    tm = min(1024, t)
    w = _compute_weights(full)
    ffn = lambda tag: (w[tag + "_pre_g"], w[tag + "_w_gate"], w[tag + "_w_up"], w[tag + "_w_down"], w[tag + "_post_g"])
    x1, sv1 = _ffn_fwd("ffn1", x, *ffn("ffn1"), tm)
    x2, svm, gathered = _mixer_fwd(x1, positions, w, tb, _carried_gather(late[0]) if late else None)
    for n, gw in zip(FFN2_BIG, gathered):
        w[n] = gw
    x3, sv2 = _ffn_fwd("ffn2", x2, *ffn("ffn2"), tm)

    def loss_f(yb, tg):
        e = yb - tg
        return e * (1.0 / d), jnp.sum(e * e, axis=0, keepdims=True)

    dy, lsum = _rowwise("loss", loss_f, [x3, loss_target], [], [(d, F32)], [(1, d)], tb)
    g = {}
    dx2, g["ffn2_pre_g"], g["ffn2_w_gate"], g["ffn2_w_up"], g["ffn2_w_down"], g["ffn2_post_g"], _, _ = _ffn_bwd(
        "ffn2", dy, sv2, *ffn("ffn2"), tm, tm)

    def pair_sums(arrs, tag):
        got = _swap_halves(arrs, tag)
        return [_add_pair("add_pair%s_%d" % (tag, i), gi, gt, late[1]) for i, (gi, gt) in enumerate(zip(arrs, got))]

    def chip_sums(pairs, slabs, tag):
        return [_add_chips("add_chips%s_%d" % (tag, i), pr, sl, late[2]) for i, (pr, sl) in enumerate(zip(pairs, slabs))]

    if late:
        pairs2 = pair_sums([g[n] for n in FFN2_BIG], "_ffn2")
        dx1, gm, slabs2 = _mixer_bwd(dx2, svm, w, tb, _carried_scatter(pairs2))
        for n, hs in zip(FFN2_BIG, chip_sums(pairs2, slabs2, "_ffn2")):
            g[n] = hs
    else:
        dx1, gm, _ = _mixer_bwd(dx2, svm, w, tb)
    g["w_in"] = jnp.stack([jnp.pad(_win_cols_t(gm["w_in_pad_t"], q * W_IN_SHARD, (q + 1) * W_IN_SHARD),
                                   ((0, W_IN_SHARD_PAD - W_IN_SHARD), (0, 0))) for q in range(N_SHARD)])
    g["mla_w_uq"] = _unpad_heads_cols(gm["w_uq_pad"], MLA_NOPE + MLA_ROPE)
    g["mla_w_ukv"] = _wkv_from_pad(gm["w_kv_pad"])
    g["gdn_conv_w"] = gm["conv_w"]
    g["w_out"] = _wout_from_pad(gm["w_out_pad"])
    ffn1_names = FFN_BIG[:3]
    if late:
        quarters = [_pack([jnp.split(g[n], N_SHARD, axis=SHARD_AXIS[n])[q] for n in MIX_BIG], MM_DTYPE)
                    for q in range(N_SHARD)]
        pairs_m = pair_sums([g["w_in"].astype(MM_DTYPE), jnp.stack(quarters)], "_mix")
        pairs1 = []

        def make_up(*dws):
            pairs1.extend(pair_sums(list(dws), "_ffn1"))
            return _carried_scatter(pairs1)

        dx0, g["ffn1_pre_g"], _, _, _, g["ffn1_post_g"], slabs_m, slabs1 = _ffn_bwd(
            "ffn1", dx1, sv1, *ffn("ffn1"), tm, tm, _carried_scatter(pairs_m), make_up)
        for n, hs in zip(ffn1_names, chip_sums(pairs1, slabs1, "_ffn1")):
            g[n] = hs
        g["w_in"], g["mix_pack"] = chip_sums(pairs_m, slabs_m, "_mix")
    else:
        dx0, g["ffn1_pre_g"], g["ffn1_w_gate"], g["ffn1_w_up"], g["ffn1_w_down"], g["ffn1_post_g"], _, _ = _ffn_bwd(
            "ffn1", dx1, sv1, *ffn("ffn1"), tm, tm)
    g["mix_pre_g"], g["mix_post_g"] = gm["mix_pre_g"], gm["mix_post_g"]
    g["mla_q_norm_g"], g["mla_kv_norm_g"] = gm["mla_q_norm_g"], gm["mla_kv_norm_g"]
    g["gdn_norm_g"] = gm["gdn_norm_g"]
    g["mla_out_g"] = _unpad_heads_cols(gm["mla_out_g_pad"], MLA_V)
    g["gdn_a_log"] = gm["a_log_pad"][:, :N_HEADS]
    g["gdn_dt_bias"] = gm["dt_bias_pad"][:, :N_HEADS]
    return lsum, dx0, g


HBM_SPEC = pl.BlockSpec(memory_space=pltpu.HBM)


def _place():
    return lax.axis_index("x"), lax.axis_index("y"), lax.axis_index("c")


def _exchange_call(name, body, ins, out_shapes, n_remote, n_local):
    return pl.pallas_call(
        body, name=name, in_specs=[HBM_SPEC] * len(ins), out_specs=[HBM_SPEC] * len(out_shapes), out_shape=out_shapes,
        scratch_shapes=[pltpu.SemaphoreType.DMA((n_remote,)), pltpu.SemaphoreType.DMA((n_remote,)),
                        pltpu.SemaphoreType.DMA((n_local,))])(*ins)


def _other_chips(x, y):
    return [(1 - x, y), (x, 1 - y), (1 - x, 1 - y)]


def _at_each_chip(fn):
    x, y, _ = _place()
    for cx in range(2):
        for cy in range(2):
            pl.when((x == cx) & (y == cy))(functools.partial(fn, cx, cy))


def _at_each_device(fn):
    x, y, c = _place()
    for cx in range(2):
        for cy in range(2):
            for cc in range(2):
                pl.when((x == cx) & (y == cy) & (c == cc))(functools.partial(fn, cx, cy, cc))


def _at_each_core(fn):
    c = lax.axis_index("c")
    for cc in range(2):
        pl.when(c == cc)(functools.partial(fn, cc))


def _gather_shards(ws):
    nw = len(ws)

    def body(*refs):
        w_refs, out_refs = refs[:nw], refs[nw:2 * nw]
        send_sems, recv_sems, local_sems = refs[2 * nw:]

        def run(x, y, c):
            chips = _other_chips(x, y)
            me, sibling = 2 * x + y, (x, y, 1 - c)

            def half(ref, which):
                hr = ref.shape[0] // 2
                return ref.at[pl.ds(which * hr, hr)]

            def over_ici(i, j, src, slab, to):
                return pltpu.make_async_remote_copy(
                    src_ref=half(src, c), dst_ref=half(out_refs[i].at[slab], c), send_sem=send_sems.at[7 * i + j],
                    recv_sem=recv_sems.at[7 * i + j], device_id=to, device_id_type=MESH)

            def over_d2d(i, j, slab, which):
                return pltpu.make_async_remote_copy(
                    src_ref=half(out_refs[i].at[slab], which), dst_ref=half(out_refs[i].at[slab], which),
                    send_sem=send_sems.at[7 * i + 3 + j], recv_sem=recv_sems.at[7 * i + 3 + j], device_id=sibling,
                    device_id_type=MESH)

            def own(i, w_ref):
                return pltpu.make_async_remote_copy(
                    src_ref=w_ref, dst_ref=out_refs[i].at[me], send_sem=send_sems.at[7 * i + 6],
                    recv_sem=recv_sems.at[7 * i + 6], device_id=sibling, device_id_type=MESH)

            sends, passed = [], []
            for i, w_ref in enumerate(w_refs):
                for j, (px, py) in enumerate(chips):
                    sends.append(over_ici(i, j, w_ref, me, (px, py, c)))
                    sends[-1].start()
            for i, w_ref in enumerate(w_refs):
                sends.append(own(i, w_ref))
                sends[-1].start()
            for i, w_ref in enumerate(w_refs):
                for j, (px, py) in enumerate(chips):
                    over_ici(i, j, w_ref, 2 * px + py, (px, py, c)).wait_recv()
                    passed.append(over_d2d(i, j, 2 * px + py, c))
                    passed[-1].start()
            for i, w_ref in enumerate(w_refs):
                own(i, w_ref).wait_recv()
                for j, (px, py) in enumerate(chips):
                    over_d2d(i, j, 2 * px + py, 1 - c).wait_recv()
            for cp in sends + passed:
                cp.wait_send()

        _at_each_device(run)

    outs = [jax.ShapeDtypeStruct((N_SHARD,) + w.shape, w.dtype) for w in ws]
    return _exchange_call("gather_weight_shards", body, ws, outs, 7 * nw, 1)


def _swap_halves(gs, tag=""):
    ng = len(gs)

    def body(*refs):
        g_refs, got_refs = refs[:ng], refs[ng:2 * ng]
        send_sems, recv_sems, _ = refs[2 * ng:]
        x, y, _ = _place()

        def run(c):
            sends = []
            for i, (g_ref, got_ref) in enumerate(zip(g_refs, got_refs)):
                hr = got_ref.shape[1]
                sends.append(pltpu.make_async_remote_copy(
                    src_ref=g_ref.at[:, pl.ds((1 - c) * hr, hr)], dst_ref=got_ref, send_sem=send_sems.at[i],
                    recv_sem=recv_sems.at[i], device_id=(x, y, 1 - c), device_id_type=MESH))
                sends[-1].start()
            for cp in sends:
                cp.wait()

        _at_each_core(run)

    halves = [jax.ShapeDtypeStruct((g.shape[0], g.shape[1] // 2, g.shape[2]), g.dtype) for g in gs]
    return _exchange_call("swap_grad_halves" + tag, body, gs, halves, ng, 1)


def _scatter_copies(p_refs, out_refs, send_sems, recv_sems, x, y):
    c = lax.axis_index("c")
    copies = []
    for i, (p_ref, out_ref) in enumerate(zip(p_refs, out_refs)):
        for j, (px, py) in enumerate(_other_chips(x, y)):
            copies.append(pltpu.make_async_remote_copy(
                src_ref=p_ref.at[2 * px + py], dst_ref=out_ref.at[j], send_sem=send_sems.at[3 * i + j],
                recv_sem=recv_sems.at[3 * i + j], device_id=(px, py, c), device_id_type=MESH))
    return copies


def _start_all(make, *refs):
    def run(x, y):
        for cp in make(*refs, x, y):
            cp.start()

    _at_each_chip(run)


def _wait_all(make, *refs):
    def run(x, y):
        copies = make(*refs, x, y)
        for cp in copies:
            cp.wait_recv()
        for cp in copies:
            cp.wait_send()

    _at_each_chip(run)


def _scatter_shapes(ps):
    return [jax.ShapeDtypeStruct((3,) + p.shape[1:], p.dtype) for p in ps]


def _carried_scatter(ps):
    return _Carried(ps, _scatter_shapes(ps), 3 * len(ps), functools.partial(_start_all, _scatter_copies),
                    functools.partial(_wait_all, _scatter_copies))


def _direct_gather_copies(w_refs, out_refs, send_sems, recv_sems, x, y, arriving):
    c = lax.axis_index("c")
    me = 2 * x + y
    peers = [((px, py, c), 2 * px + py) for px, py in _other_chips(x, y)] + [((x, y, 1 - c), me)]
    copies = []
    for i, (w_ref, out_ref) in enumerate(zip(w_refs, out_refs)):
        for j, (peer, slab) in enumerate(peers):
            copies.append(pltpu.make_async_remote_copy(
                src_ref=w_ref, dst_ref=out_ref.at[slab if arriving else me], send_sem=send_sems.at[4 * i + j],
                recv_sem=recv_sems.at[4 * i + j], device_id=peer, device_id_type=MESH))
    return copies


def _carried_gather(ws):
    def start(w_refs, out_refs, send_sems, recv_sems):
        def run(x, y):
            for cp in _direct_gather_copies(w_refs, out_refs, send_sems, recv_sems, x, y, False):
                cp.start()

        _at_each_chip(run)

    def finish(w_refs, out_refs, send_sems, recv_sems):
        def run(x, y):
            for cp in _direct_gather_copies(w_refs, out_refs, send_sems, recv_sems, x, y, True):
                cp.wait_recv()
            for cp in _direct_gather_copies(w_refs, out_refs, send_sems, recv_sems, x, y, False):
                cp.wait_send()

        _at_each_chip(run)

    outs = [jax.ShapeDtypeStruct((N_SHARD,) + w.shape, w.dtype) for w in ws]
    return _Carried(ws, outs, 4 * len(ws), start, finish)


def _share_halves(hs):
    n = len(hs)

    def body(*refs):
        h_refs, out_refs = refs[:n], refs[n:2 * n]
        send_sems, recv_sems, _ = refs[2 * n:]
        x, y, c = _place()
        sends = []
        for i, (h_ref, out_ref) in enumerate(zip(h_refs, out_refs)):
            sends.append(pltpu.make_async_remote_copy(
                src_ref=h_ref, dst_ref=out_ref, send_sem=send_sems.at[i], recv_sem=recv_sems.at[i],
                device_id=(x, y, 1 - c), device_id_type=MESH))
            sends[-1].start()
        for cp in sends:
            cp.wait()

    outs = [jax.ShapeDtypeStruct(h.shape, h.dtype) for h in hs]
    return _exchange_call("share_grad_halves", body, hs, outs, n, 1)


def _scalar_grid_call(name, body, scalars, grid, in_specs, out_specs, out_shape, args):
    grid_spec = pltpu.PrefetchScalarGridSpec(num_scalar_prefetch=len(scalars), grid=grid, in_specs=in_specs,
                                             out_specs=out_specs)
    return pl.pallas_call(body, name=name, grid_spec=grid_spec, out_shape=out_shape,
                          compiler_params=_params(("arbitrary",) * len(grid)))(*scalars, *args)


def _add_pair(name, g, got, core):
    ns_, hr, cols = got.shape
    th = _row_tile(hr, 512)
    nb = hr // th

    def body(core_ref, g_ref, got_ref, out_ref):
        out_ref[...] = (g_ref[...].astype(F32) + got_ref[...].astype(F32)).astype(out_ref.dtype)

    blk = pl.BlockSpec((1, th, cols), lambda q, j, core_ref: (q, j, 0))
    own = pl.BlockSpec((1, th, cols), lambda q, j, core_ref: (q, core_ref[0] * nb + j, 0))
    return _scalar_grid_call(name, body, [core], (ns_, nb), [own, blk], blk,
                             jax.ShapeDtypeStruct(got.shape, got.dtype), [g, got])


def _add_chips(name, pairs, slabs, chip):
    _, hr, cols = slabs.shape
    th = _row_tile(hr, 512)

    def body(chip_ref, own_ref, s0_ref, s1_ref, s2_ref, out_ref):
        total = own_ref[0].astype(F32) + s0_ref[0].astype(F32)
        out_ref[...] = (total + s1_ref[0].astype(F32)) + s2_ref[0].astype(F32)

    own = pl.BlockSpec((1, th, cols), lambda j, chip_ref: (chip_ref[0], j, 0))
    others = [pl.BlockSpec((1, th, cols), lambda j, chip_ref, k=k: (k, j, 0)) for k in range(3)]
    return _scalar_grid_call(name, body, [chip], (hr // th,), [own] + others,
                             pl.BlockSpec((th, cols), lambda j, chip_ref: (j, 0)),
                             jax.ShapeDtypeStruct((hr, cols), F32), [pairs, slabs, slabs, slabs])


def _join_halves(name, mine, other, core):
    hr, cols = mine.shape
    th = _row_tile(hr, 512)
    nb = hr // th

    def body(core_ref, mine_ref, other_ref, out_ref):
        is_mine = pl.program_id(0) == core_ref[0]

        @pl.when(is_mine)
        def _():
            out_ref[0] = mine_ref[...]

        @pl.when(jnp.logical_not(is_mine))
        def _():
            out_ref[0] = other_ref[...]

    blk = pl.BlockSpec((th, cols), lambda h, j, core_ref: (j, 0))
    return _scalar_grid_call(name, body, [core], (2, nb), [blk, blk],
                             pl.BlockSpec((1, th, cols), lambda h, j, core_ref: (0, h * nb + j, 0)),
                             jax.ShapeDtypeStruct((1, 2 * hr, cols), mine.dtype), [mine, other])


def _gather_small(sp):
    def body(s_ref, out_ref, send_sems, recv_sems, local_sem):
        x, y, c = _place()
        me = 4 * x + 2 * y + c
        peers = [(x ^ (m >> 2), y ^ ((m >> 1) & 1), c ^ (m & 1)) for m in range(1, 8)]
        mine = pltpu.make_async_copy(s_ref, out_ref.at[me], local_sem)
        mine.start()
        sends = [pltpu.make_async_remote_copy(src_ref=s_ref, dst_ref=out_ref.at[me], send_sem=send_sems.at[j],
                                              recv_sem=recv_sems.at[j], device_id=p, device_id_type=MESH)
                 for j, p in enumerate(peers)]
        for cp in sends:
            cp.start()
        for j, (px, py, pc) in enumerate(peers):
            pltpu.make_async_remote_copy(src_ref=s_ref, dst_ref=out_ref.at[4 * px + 2 * py + pc],
                                         send_sem=send_sems.at[j], recv_sem=recv_sems.at[j], device_id=(px, py, pc),
                                         device_id_type=MESH).wait_recv()
        for cp in sends:
            cp.wait_send()
        mine.wait()

    return pl.pallas_call(
        body, name="gather_small_grads", in_specs=[HBM_SPEC], out_specs=HBM_SPEC,
        out_shape=jax.ShapeDtypeStruct((8,) + sp.shape, sp.dtype),
        scratch_shapes=[pltpu.SemaphoreType.DMA((7,)), pltpu.SemaphoreType.DMA((7,)), pltpu.SemaphoreType.DMA])(sp)


def _pack_rows(total):
    rows = -(-total // LANES)
    return -(-rows // 32) * 32


def _pack(arrs, dtype):
    flat = jnp.concatenate([a.reshape(-1).astype(dtype) for a in arrs])
    rows = _pack_rows(flat.shape[0])
    return jnp.pad(flat, (0, rows * LANES - flat.shape[0])).reshape(rows, LANES)


def _unpack(buf, shapes):
    flat = buf.reshape(-1)
    out, off = {}, 0
    for n, shp in shapes:
        size = shp[0] * shp[1]
        out[n] = flat[off:off + size].reshape(shp)
        off += size
    return out


def _to_wire(name, w3):
    _, r, cols = w3.shape
    tb = _row_tile(r, 512)

    def body(w_ref, o_ref):
        o_ref[...] = w_ref[0].astype(o_ref.dtype)

    return pl.pallas_call(
        body, name=name, grid=(r // tb,), in_specs=[pl.BlockSpec((1, tb, cols), lambda i: (0, i, 0))],
        out_specs=pl.BlockSpec((tb, cols), lambda i: (i, 0)), out_shape=jax.ShapeDtypeStruct((r, cols), MM_DTYPE),
        compiler_params=_params(("arbitrary",)))(w3)


def _adamw(name, w3, g, m3, v3, tb):
    c1 = 1.0 - ADAM_B1 ** ADAM_STEP
    c2 = 1.0 - ADAM_B2 ** ADAM_STEP
    _, r, cols = w3.shape
    emit = g.ndim == 2
    blk3 = pl.BlockSpec((1, tb, cols), lambda i: (0, i, 0))
    g_spec = pl.BlockSpec((tb, cols), lambda i: (i, 0)) if emit else blk3

    def body(w_ref, g_ref, m_ref, v_ref, *out_refs):
        gb = g_ref[...] if emit else g_ref[0]
        m2 = ADAM_B1 * m_ref[0] + (1.0 - ADAM_B1) * gb
        v2 = ADAM_B2 * v_ref[0] + (1.0 - ADAM_B2) * (gb * gb)
        out_refs[-3][0] = -ADAM_LR * ((m2 / c1) / (jnp.sqrt(v2 / c2) + ADAM_EPS) + ADAM_WD * w_ref[0])
        out_refs[-2][0] = m2
        out_refs[-1][0] = v2
        if emit:
            out_refs[0][0] = gb

    n_out = 4 if emit else 3
    outs = pl.pallas_call(
        body, name=name, grid=(r // tb,), in_specs=[blk3, g_spec, blk3, blk3], out_specs=[blk3] * n_out,
        out_shape=[jax.ShapeDtypeStruct((1, r, cols), F32)] * n_out,
        compiler_params=_params(("arbitrary",)))(w3, g, m3, v3)
    return outs if emit else [g] + list(outs)


def _row_tile(rows, pref):
    if rows <= pref:
        return rows
    t = pref
    while t >= 8:
        if rows % t == 0 and t % 8 == 0:
            return t
        t -= 8
    return rows


def kernel(x, positions, ffn1_pre_g, ffn1_w_gate, ffn1_w_up, ffn1_w_down, ffn1_post_g, mix_pre_g, w_in, mla_q_norm_g, mla_w_uq, mla_kv_norm_g, mla_w_ukv, mla_out_g, gdn_conv_w, gdn_a_log, gdn_dt_bias, gdn_norm_g, w_out, mix_post_g, ffn2_pre_g, ffn2_w_gate, ffn2_w_up, ffn2_w_down, ffn2_post_g, loss_target, m_ffn1_pre_g, m_ffn1_w_gate, m_ffn1_w_up, m_ffn1_w_down, m_ffn1_post_g, m_mix_pre_g, m_w_in, m_mla_q_norm_g, m_mla_w_uq, m_mla_kv_norm_g, m_mla_w_ukv, m_mla_out_g, m_gdn_conv_w, m_gdn_a_log, m_gdn_dt_bias, m_gdn_norm_g, m_w_out, m_mix_post_g, m_ffn2_pre_g, m_ffn2_w_gate, m_ffn2_w_up, m_ffn2_w_down, m_ffn2_post_g, v_ffn1_pre_g, v_ffn1_w_gate, v_ffn1_w_up, v_ffn1_w_down, v_ffn1_post_g, v_mix_pre_g, v_w_in, v_mla_q_norm_g, v_mla_w_uq, v_mla_kv_norm_g, v_mla_w_ukv, v_mla_out_g, v_gdn_conv_w, v_gdn_a_log, v_gdn_dt_bias, v_gdn_norm_g, v_w_out, v_mix_post_g, v_ffn2_pre_g, v_ffn2_w_gate, v_ffn2_w_up, v_ffn2_w_down, v_ffn2_post_g):
    args = dict(locals())
    wsh = {n: args[n][0] for n in WEIGHTS}
    msh = {n: args["m_" + n] for n in SMALL}
    vsh = {n: args["v_" + n] for n in SMALL}
    for n in SMALL:
        wsh[n] = args[n]
    mix_shapes = [(n, wsh[n].shape) for n in MIX_BIG]

    early = FFN_BIG[:3]
    held = lambda a, n: jnp.swapaxes(a, 1, 2) if n in TRANSPOSED else a
    w_in_wire = jnp.pad(held(w_in, "w_in")[0].astype(MM_DTYPE), ((0, W_IN_SHARD_PAD - W_IN_SHARD), (0, 0)))
    gathered = _gather_shards([_to_wire("wire_" + n, held(args[n], n)) for n in early]
                              + [w_in_wire, _pack([wsh[n] for n in MIX_BIG], MM_DTYPE)])
    full = {n: wsh[n] for n in SMALL}
    for n, gw in zip(early + ["w_in"], gathered):
        full[n] = gw
    parts = [_unpack(gathered[-1][q], mix_shapes) for q in range(N_SHARD)]
    for n in MIX_BIG:
        full[n] = jnp.concatenate([parts[q][n] for q in range(N_SHARD)], axis=SHARD_AXIS[n])

    core = lax.axis_index("c").astype(jnp.int32).reshape(1)
    chip = (2 * lax.axis_index("x") + lax.axis_index("y")).astype(jnp.int32).reshape(1)
    late = ([_to_wire("wire_" + n, held(args[n], n)) for n in FFN2_BIG], core, chip)
    lsum, grad_x, g = _local_step(x[0], positions, loss_target[0], full, late)
    loss = lax.psum(0.5 * jnp.sum(lsum) / x.shape[-1], ("x", "y", "c"))

    halves = [g[n] for n in FFN_BIG] + [g["w_in"], g["mix_pack"]]
    others = _share_halves(halves)
    shared = [_join_halves("join_halves_%d" % i, hm, ho, core) for i, (hm, ho) in enumerate(zip(halves, others))]
    gsh = _unpack(shared[-1], mix_shapes)
    for n, sg_ in zip(FFN_BIG, shared):
        gsh[n] = sg_
    gsh["w_in"] = shared[-2][:, :W_IN_SHARD]

    small_shapes = [(n, wsh[n].shape) for n in SMALL]
    pack_small = lambda d: jnp.concatenate(
        [_pad_lanes(d[n].astype(F32), LANES) for n in SMALL] + [jnp.zeros((SMALL_ROWS - len(SMALL), LANES), F32)], axis=0)
    slots = _gather_small(pack_small(g))

    c1 = 1.0 - ADAM_B1 ** ADAM_STEP
    c2 = 1.0 - ADAM_B2 ** ADAM_STEP

    def small_update(wb, mb, vb, s8):
        gs = s8[0:SMALL_ROWS]
        for d in range(1, 8):
            gs = gs + s8[d * SMALL_ROWS:(d + 1) * SMALL_ROWS]
        m2 = ADAM_B1 * mb + (1.0 - ADAM_B1) * gs
        v2 = ADAM_B2 * vb + (1.0 - ADAM_B2) * (gs * gs)
        delta = -ADAM_LR * ((m2 / c1) / (jnp.sqrt(v2 / c2) + ADAM_EPS) + ADAM_WD * wb)
        return gs, delta, m2, v2

    sg, sd, sm, sv_ = _rowwise("adamw_small", small_update,
                               [pack_small(wsh), pack_small(msh), pack_small(vsh)],
                               [slots.reshape(8 * SMALL_ROWS, LANES)], [(LANES, F32)] * 4, [], SMALL_ROWS)
    grads, deltas, new_m, new_v = {}, {}, {}, {}
    for i, (n, shp) in enumerate(small_shapes):
        grads[n], deltas[n] = sg[i:i + 1, :shp[1]], sd[i:i + 1, :shp[1]]
        new_m[n], new_v[n] = sm[i:i + 1, :shp[1]], sv_[i:i + 1, :shp[1]]
    for n in BIG:
        w3 = held(args[n], n)
        outs = _adamw("adamw_" + n, w3, gsh[n], held(args["m_" + n], n), held(args["v_" + n], n),
                      _row_tile(w3.shape[1], 256))
        grads[n], deltas[n], new_m[n], new_v[n] = [held(o, n) for o in outs]

    return (loss, grad_x[None], *[grads[n] for n in WEIGHTS], *[deltas[n] for n in WEIGHTS],
            *[new_m[n] for n in WEIGHTS], *[new_v[n] for n in WEIGHTS])
```

```python
import functools

import jax
import jax.numpy as jnp
from jax import lax
from jax.experimental import pallas as pl
from jax.experimental.pallas import tpu as pltpu

F32 = jnp.float32
BF16 = jnp.bfloat16
MM_DTYPE = BF16
MESH = pl.DeviceIdType.MESH

D_MODEL = 1024
D_FF = 2816
N_HEADS = 8
MLA_Q_RANK = 256
MLA_KV_RANK = 128
MLA_NOPE = 64
MLA_ROPE = 32
MLA_V = 64
ROPE_THETA = 10000.0
GDN_DH = 64
GDN_W = N_HEADS * GDN_DH
GDN_CONV = 4
CHUNK = 64
HEAD_LANES = 128
HEADS_PER_STEP = 8
MLA_PAD = N_HEADS * HEAD_LANES
EPS = 1e-6
N_SHARD = 4
LANES = 1024

PIN_QKV = 0
PIN_MLA = 1536
PIN_KPE = 1920
PIN_GATE = 2048
PIN_AB = 2560
PIN_W = 2688
CAT_W = MLA_PAD + GDN_W

ADAM_LR = 0.001
ADAM_B1 = 0.9
ADAM_B2 = 0.999
ADAM_EPS = 1e-08
ADAM_WD = 0.01
ADAM_STEP = 10

VMEM_LIMIT_V7X = 56 * 1024 * 1024

BIG = ["ffn1_w_gate", "ffn1_w_up", "ffn1_w_down", "w_in", "mla_w_uq", "mla_w_ukv", "gdn_conv_w", "w_out",
       "ffn2_w_gate", "ffn2_w_up", "ffn2_w_down"]
FFN_BIG = ["ffn1_w_gate", "ffn1_w_up", "ffn1_w_down", "ffn2_w_gate", "ffn2_w_up", "ffn2_w_down"]
TRANSPOSED = ["ffn1_w_gate", "ffn1_w_up", "ffn2_w_gate", "ffn2_w_up", "w_in"]
MIX_BIG = ["mla_w_uq", "mla_w_ukv", "gdn_conv_w", "w_out"]
SMALL = ["ffn1_pre_g", "ffn1_post_g", "mix_pre_g", "mla_q_norm_g", "mla_kv_norm_g", "mla_out_g", "gdn_a_log",
         "gdn_dt_bias", "gdn_norm_g", "mix_post_g", "ffn2_pre_g", "ffn2_post_g"]
WEIGHTS = ["ffn1_pre_g", "ffn1_w_gate", "ffn1_w_up", "ffn1_w_down", "ffn1_post_g", "mix_pre_g", "w_in",
           "mla_q_norm_g", "mla_w_uq", "mla_kv_norm_g", "mla_w_ukv", "mla_out_g", "gdn_conv_w", "gdn_a_log",
           "gdn_dt_bias", "gdn_norm_g", "w_out", "mix_post_g", "ffn2_pre_g", "ffn2_w_gate", "ffn2_w_up",
           "ffn2_w_down", "ffn2_post_g"]
SHARD_AXIS = {"ffn1_w_gate": 1, "ffn1_w_up": 1, "ffn1_w_down": 0, "w_in": 1, "mla_w_uq": 1, "mla_w_ukv": 1,
              "gdn_conv_w": 1, "w_out": 0, "ffn2_w_gate": 1, "ffn2_w_up": 1, "ffn2_w_down": 0}
SMALL_ROWS = 16


def _params(sem):
    return pltpu.CompilerParams(dimension_semantics=sem, vmem_limit_bytes=VMEM_LIMIT_V7X)


def _pick(dim, pref):
    if dim <= pref:
        return dim
    t = (pref // 128) * 128
    while t >= 128:
        if dim % t == 0:
            return t
        t -= 128
    return dim


ANY_SPEC = pl.BlockSpec(memory_space=pl.ANY)


def _rowwise(name, fn, row_ins, bc_ins, row_outs, acc_outs, tb, wide=None, carry=None):
    ents = []
    for e in row_ins:
        ents.append(e if isinstance(e, tuple) else (e, e.shape[1], 0, 0))
    over = [o[2] for o in row_outs if len(o) == 3]
    rows = over[0] if over else ents[0][0].shape[0]
    steps = rows // tb
    assert steps * tb == rows, (name, rows, tb)
    in_specs, args = [], []
    for a, w, j, r0 in ents:
        in_specs.append(pl.BlockSpec((tb, w), lambda i, j=j, r0=r0: (i + r0, j)))
        args.append(a)
    for b in bc_ins:
        in_specs.append(pl.BlockSpec(b.shape, lambda i: (0, 0)))
        args.append(b)
    n_in = len(args)
    aliases = {}
    if carry is not None:
        in_specs.append(ANY_SPEC)
        args.append(carry)
        aliases = {n_in: 0}
    out_shape = [jax.ShapeDtypeStruct((rows, o[0]), o[1]) for o in row_outs]
    out_specs = [pl.BlockSpec((tb, o[0]), lambda i: (i, 0)) for o in row_outs]
    if wide is not None:
        out_shape[0] = jax.ShapeDtypeStruct((rows, wide[0]), row_outs[0][1])
        out_specs[0] = pl.BlockSpec((tb, row_outs[0][0]), lambda i: (i, wide[1]))
    out_shape += [jax.ShapeDtypeStruct((r, c), F32) for r, c in acc_outs]
    out_specs += [pl.BlockSpec((r, c), lambda i: (0, 0)) for r, c in acc_outs]
    n_ro, n_acc, n_args = len(row_outs), len(acc_outs), len(args)

    def body(*refs):
        vals = fn(*[r[...] for r in refs[:n_in]])
        if not isinstance(vals, (tuple, list)):
            vals = (vals,)
        for r, v in zip(refs[n_args:n_args + n_ro], vals[:n_ro]):
            r[...] = v.astype(r.dtype)
        if n_acc:
            acc_refs = refs[n_args + n_ro:]

            @pl.when(pl.program_id(0) == 0)
            def _():
                for r in acc_refs:
                    r[...] = jnp.zeros(r.shape, r.dtype)

            for r, v in zip(acc_refs, vals[n_ro:]):
                r[...] += v

    outs = pl.pallas_call(body, name=name, grid=(steps,), in_specs=in_specs, out_specs=out_specs,
                          out_shape=out_shape, input_output_aliases=aliases,
                          compiler_params=_params(("arbitrary",)))(*args)
    return outs


def _mm(name, a, b, mode, out_dtype, tm=1024, tn=1024, tk=1024):
    if mode == "nn":
        (m, k), (k2, n) = a.shape, b.shape
    elif mode == "nt":
        (m, k), (n, k2) = a.shape, b.shape
    else:
        (k, m), (k2, n) = a.shape, b.shape
    assert k == k2, (name, a.shape, b.shape)
    tm, tn, tk = _pick(m, tm), _pick(n, tn), _pick(k, tk)
    nk = k // tk
    if mode == "nn":
        a_spec = pl.BlockSpec((tm, tk), lambda i, j, kk: (i, kk))
        b_spec = pl.BlockSpec((tk, tn), lambda i, j, kk: (kk, j))
        dims = (((1,), (0,)), ((), ()))
    elif mode == "nt":
        a_spec = pl.BlockSpec((tm, tk), lambda i, j, kk: (i, kk))
        b_spec = pl.BlockSpec((tn, tk), lambda i, j, kk: (j, kk))
        dims = (((1,), (1,)), ((), ()))
    else:
        a_spec = pl.BlockSpec((tk, tm), lambda i, j, kk: (kk, i))
        b_spec = pl.BlockSpec((tk, tn), lambda i, j, kk: (kk, j))
        dims = (((0,), (0,)), ((), ()))

    def body(a_ref, b_ref, o_ref, acc_ref):
        kk = pl.program_id(2)

        @pl.when(kk == 0)
        def _():
            acc_ref[...] = jnp.zeros(acc_ref.shape, F32)

        acc_ref[...] += lax.dot_general(a_ref[...].astype(MM_DTYPE), b_ref[...].astype(MM_DTYPE), dims,
                                        preferred_element_type=F32)

        @pl.when(kk == nk - 1)
        def _():
            o_ref[...] = acc_ref[...].astype(o_ref.dtype)

    return pl.pallas_call(
        body, name=name, grid=(m // tm, n // tn, nk), in_specs=[a_spec, b_spec],
        out_specs=pl.BlockSpec((tm, tn), lambda i, j, kk: (i, j)),
        out_shape=jax.ShapeDtypeStruct((m, n), out_dtype),
        scratch_shapes=[pltpu.VMEM((tm, tn), F32)],
        compiler_params=_params(("parallel", "parallel", "arbitrary")))(a, b)


def _rms_stats(x, n_real=None):
    n = x.shape[-1] if n_real is None else n_real
    return lax.rsqrt(jnp.sum(x * x, axis=-1, keepdims=True) / n + EPS)


def _rms_bwd(x, r, g, dz, n_real=None):
    n = x.shape[-1] if n_real is None else n_real
    xh = x * r
    dxh = dz * g
    dx = r * (dxh - xh * (jnp.sum(dxh * xh, axis=-1, keepdims=True) / n))
    return dx, jnp.sum(dz * xh, axis=0, keepdims=True)


def _sigmoid(x):
    return 0.5 * jnp.tanh(0.5 * x) + 0.5


def _roll(x, s, axis):
    return pltpu.roll(x, s, axis)


def _rope(x, c, s1, s2):
    return x * c + _roll(x, HEAD_LANES - MLA_ROPE // 2, 1) * s1 + _roll(x, MLA_ROPE // 2, 1) * s2


def _heads_apply(x, fn):
    return jnp.concatenate([fn(x[:, h * HEAD_LANES:(h + 1) * HEAD_LANES]) for h in range(N_HEADS)], axis=1)


ROW_CHUNK = 256


def _row_chunks(rows):
    step = min(ROW_CHUNK, rows)
    return [pl.ds(r, step) for r in range(0, rows, step)]


def _ffn_fwd(tag, x, g_pre, wg, wu, wd, g_post, tm):
    t, d = x.shape
    ns, fs, _ = wg.shape
    nt = t // tm
    row = pl.BlockSpec((tm, d), lambda i, q: (i, 0))
    vec = pl.BlockSpec((1, d), lambda i, q: (0, 0))
    act3 = pl.BlockSpec((1, tm, fs), lambda i, q: (q, i, 0))
    wrow = pl.BlockSpec((1, fs, d), lambda i, q: (q, 0, 0))
    nt_dims = (((1,), (1,)), ((), ()))

    def gate_up(x_ref, g_ref, wg_ref, wu_ref, n_ref, sl_ref, ud_ref, s_ref, n_s):
        @pl.when(pl.program_id(1) == 0)
        def _():
            for r in _row_chunks(tm):
                xb = x_ref[r, :]
                n_s[r, :] = (xb * _rms_stats(xb) * g_ref[...]).astype(MM_DTYPE)
            n_ref[...] = n_s[...]

        for r in _row_chunks(tm):
            n = n_s[r, :]
            a = lax.dot_general(n, wg_ref[0], nt_dims, preferred_element_type=F32)
            u = lax.dot_general(n, wu_ref[0], nt_dims, preferred_element_type=F32)
            sg = _sigmoid(a)
            sl = a * sg
            sl_ref[0, r, :] = sl.astype(sl_ref.dtype)
            ud_ref[0, r, :] = (u * (sg + sl * (1.0 - sg))).astype(ud_ref.dtype)
            s_ref[0, r, :] = (sl * u).astype(s_ref.dtype)

    n, sl, ud, s = pl.pallas_call(
        gate_up, name=tag + "_gate_up", grid=(nt, ns), in_specs=[row, vec, wrow, wrow],
        out_specs=[row, act3, act3, act3],
        out_shape=[jax.ShapeDtypeStruct((t, d), MM_DTYPE)] + [jax.ShapeDtypeStruct((ns, t, fs), MM_DTYPE)] * 3,
        scratch_shapes=[pltpu.VMEM((tm, d), MM_DTYPE)],
        compiler_params=_params(("parallel", "arbitrary")))(x, g_pre, wg, wu)

    def down(s_ref, wd_ref, x_ref, g_ref, h_ref, y_ref, acc):
        q = pl.program_id(1)

        @pl.when(q == 0)
        def _():
            acc[...] = jnp.zeros(acc.shape, F32)

        for r in _row_chunks(tm):
            acc[r, :] += jnp.dot(s_ref[0, r, :], wd_ref[0], preferred_element_type=F32)

        @pl.when(q == ns - 1)
        def _():
            for r in _row_chunks(tm):
                hb = acc[r, :]
                h_ref[r, :] = hb
                y_ref[r, :] = x_ref[r, :] + 0.5 * (hb * _rms_stats(hb) * g_ref[...])

    h, y = pl.pallas_call(
        down, name=tag + "_down", grid=(nt, ns), in_specs=[act3, wrow, row, vec], out_specs=[row, row],
        out_shape=[jax.ShapeDtypeStruct((t, d), F32)] * 2, scratch_shapes=[pltpu.VMEM((tm, d), F32)],
        compiler_params=_params(("parallel", "arbitrary")))(s, wd, x, g_post)
    return y, (x, n, sl, ud, s, h)


def _carry(body, n_in, n_out, grid, carried):
    if carried is None:
        return body, [], [], [], [], []
    nx_in, nx_out = len(carried.ins), len(carried.outs)

    def wrapped(*refs):
        ins, rest = refs[:n_in], refs[n_in:]
        xi, rest = rest[:nx_in], rest[nx_in:]
        outs, rest = rest[:n_out], rest[n_out:]
        xo, rest = rest[:nx_out], rest[nx_out:]
        scr, sems = rest[:len(rest) - 2], rest[len(rest) - 2:]
        first, last = True, True
        for dim, size in enumerate(grid):
            first = first & (pl.program_id(dim) == 0)
            last = last & (pl.program_id(dim) == size - 1)

        @pl.when(first)
        def _():
            carried.start(xi, xo, *sems)

        body(*ins, *outs, *scr)

        @pl.when(last)
        def _():
            carried.finish(xi, xo, *sems)

    sems = [pltpu.SemaphoreType.DMA((carried.n_sem,)), pltpu.SemaphoreType.DMA((carried.n_sem,))]
    return (wrapped, [HBM_SPEC] * nx_in, [HBM_SPEC] * nx_out, list(carried.outs), sems, list(carried.ins))


def _ffn_bwd(tag, dy, saved, g_pre, wg, wu, wd, g_post, tm, tk, carried_down=None, make_carried_up=None):
    x, n, sl, ud, s, h = saved
    t, d = x.shape
    ns, fs, _ = wg.shape
    nt, nk = t // tm, t // tk
    row = pl.BlockSpec((tm, d), lambda i, q: (i, 0))
    vec = pl.BlockSpec((1, d), lambda i, q: (0, 0))
    act3 = pl.BlockSpec((1, tm, fs), lambda i, q: (q, i, 0))
    wrow = pl.BlockSpec((1, fs, d), lambda i, q: (q, 0, 0))
    nt_dims = (((1,), (1,)), ((), ()))
    tn_dims = (((0,), (0,)), ((), ()))

    def down_b(h_ref, dy_ref, g_ref, wd_ref, sl_ref, ud_ref, dh_ref, da_ref, du_ref, dg_ref, dh_s):
        i, q = pl.program_id(0), pl.program_id(1)

        @pl.when((i == 0) & (q == 0))
        def _():
            dg_ref[...] = jnp.zeros(dg_ref.shape, F32)

        @pl.when(q == 0)
        def _():
            for r in _row_chunks(tm):
                hb = h_ref[r, :]
                dh, dg = _rms_bwd(hb, _rms_stats(hb), g_ref[...], 0.5 * dy_ref[r, :])
                dh_s[r, :] = dh.astype(MM_DTYPE)
                dg_ref[...] += dg
            dh_ref[...] = dh_s[...]

        for r in _row_chunks(tm):
            ds = lax.dot_general(dh_s[r, :], wd_ref[0], nt_dims, preferred_element_type=F32)
            da_ref[0, r, :] = (ds * ud_ref[0, r, :].astype(F32)).astype(da_ref.dtype)
            du_ref[0, r, :] = (ds * sl_ref[0, r, :].astype(F32)).astype(du_ref.dtype)

    down_b, x_in, x_out, x_shape, x_scr, x_args = _carry(down_b, 6, 4, (nt, ns), carried_down)
    dh, da, du, dg_post, *from_down = pl.pallas_call(
        down_b, name=tag + "_down_b", grid=(nt, ns), in_specs=[row, row, vec, wrow, act3, act3] + x_in,
        out_specs=[row, act3, act3, vec] + x_out,
        out_shape=[jax.ShapeDtypeStruct((t, d), MM_DTYPE)] + [jax.ShapeDtypeStruct((ns, t, fs), MM_DTYPE)] * 2
        + [jax.ShapeDtypeStruct((1, d), F32)] + x_shape,
        scratch_shapes=[pltpu.VMEM((tm, d), MM_DTYPE)] + x_scr,
        compiler_params=_params(("arbitrary", "arbitrary")))(h, dy, g_post, wd, sl, ud, *x_args)

    def down_w(s_ref, dh_ref, dw_ref, acc):
        kk = pl.program_id(1)

        @pl.when(kk == 0)
        def _():
            acc[...] = jnp.zeros(acc.shape, F32)

        acc[...] += lax.dot_general(s_ref[0], dh_ref[...], tn_dims, preferred_element_type=F32)

        @pl.when(kk == nk - 1)
        def _():
            dw_ref[0] = acc[...].astype(dw_ref.dtype)

    dwd = pl.pallas_call(
        down_w, name=tag + "_down_w", grid=(ns, nk),
        in_specs=[pl.BlockSpec((1, tk, fs), lambda q, kk: (q, kk, 0)), pl.BlockSpec((tk, d), lambda q, kk: (kk, 0))],
        out_specs=pl.BlockSpec((1, fs, d), lambda q, kk: (q, 0, 0)),
        out_shape=jax.ShapeDtypeStruct((ns, fs, d), MM_DTYPE), scratch_shapes=[pltpu.VMEM((fs, d), F32)],
        compiler_params=_params(("parallel", "arbitrary")))(s, dh)

    def gate_up_b(da_ref, du_ref, wg_ref, wu_ref, x_ref, dy_ref, g_ref, dx_ref, dg_ref, acc):
        i, q = pl.program_id(0), pl.program_id(1)

        @pl.when((i == 0) & (q == 0))
        def _():
            dg_ref[...] = jnp.zeros(dg_ref.shape, F32)

        @pl.when(q == 0)
        def _():
            acc[...] = jnp.zeros(acc.shape, F32)

        for r in _row_chunks(tm):
            acc[r, :] += (jnp.dot(da_ref[0, r, :], wg_ref[0], preferred_element_type=F32)
                          + jnp.dot(du_ref[0, r, :], wu_ref[0], preferred_element_type=F32))

        @pl.when(q == ns - 1)
        def _():
            for r in _row_chunks(tm):
                xb = x_ref[r, :]
                dx, dg = _rms_bwd(xb, _rms_stats(xb), g_ref[...], acc[r, :])
                dx_ref[r, :] = dy_ref[r, :] + dx
                dg_ref[...] += dg

    def gate_up_w(n_ref, da_ref, du_ref, dwg_ref, dwu_ref, acc_g, acc_u):
        kk = pl.program_id(1)

        @pl.when(kk == 0)
        def _():
            acc_g[...] = jnp.zeros(acc_g.shape, F32)
            acc_u[...] = jnp.zeros(acc_u.shape, F32)

        nb = n_ref[...]
        acc_g[...] += lax.dot_general(da_ref[0], nb, tn_dims, preferred_element_type=F32)
        acc_u[...] += lax.dot_general(du_ref[0], nb, tn_dims, preferred_element_type=F32)

        @pl.when(kk == nk - 1)
        def _():
            dwg_ref[0] = acc_g[...].astype(dwg_ref.dtype)
            dwu_ref[0] = acc_u[...].astype(dwu_ref.dtype)

    k3 = pl.BlockSpec((1, tk, fs), lambda q, kk: (q, kk, 0))
    wout = pl.BlockSpec((1, fs, d), lambda q, kk: (q, 0, 0))
    dwg, dwu = pl.pallas_call(
        gate_up_w, name=tag + "_gate_up_w", grid=(ns, nk),
        in_specs=[pl.BlockSpec((tk, d), lambda q, kk: (kk, 0)), k3, k3], out_specs=[wout, wout],
        out_shape=[jax.ShapeDtypeStruct((ns, fs, d), MM_DTYPE)] * 2,
        scratch_shapes=[pltpu.VMEM((fs, d), F32)] * 2,
        compiler_params=_params(("parallel", "arbitrary")))(n, da, du)

    carried_up = make_carried_up(dwg, dwu, dwd) if make_carried_up else None
    gate_up_b, x_in, x_out, x_shape, x_scr, x_args = _carry(gate_up_b, 7, 2, (nt, ns), carried_up)
    dx, dg_pre, *from_up = pl.pallas_call(
        gate_up_b, name=tag + "_gate_up_b", grid=(nt, ns), in_specs=[act3, act3, wrow, wrow, row, row, vec] + x_in,
        out_specs=[row, vec] + x_out,
        out_shape=[jax.ShapeDtypeStruct((t, d), F32), jax.ShapeDtypeStruct((1, d), F32)] + x_shape,
        scratch_shapes=[pltpu.VMEM((tm, d), F32)] + x_scr,
        compiler_params=_params(("arbitrary", "arbitrary")))(da, du, wg, wu, x, dy, g_pre, *x_args)
    return dx, dg_pre, dwg, dwu, dwd, dg_post, from_down, from_up


NEG = -1e30


def _attn_scale():
    return (MLA_NOPE + MLA_ROPE) ** -0.5


def _causal_pairs(nq, by_key):
    if by_key:
        pairs = [(qi, ki) for ki in range(nq) for qi in range(ki, nq)]
    else:
        pairs = [(qi, ki) for qi in range(nq) for ki in range(qi + 1)]
    return jnp.asarray([p[0] for p in pairs], jnp.int32), jnp.asarray([p[1] for p in pairs], jnp.int32)


def _below_diagonal(shape):
    return lax.broadcasted_iota(jnp.int32, shape, 1) <= lax.broadcasted_iota(jnp.int32, shape, 0)


def _attn_call(name, body, tables, args, in_kinds, out_kinds, scratch, t, tq, carried=None):
    qmap = lambda h, p, qt, kt: (qt[p], h)
    kmap = lambda h, p, qt, kt: (kt[p], h)
    width = HEADS_PER_STEP * HEAD_LANES
    spec = lambda kind: pl.BlockSpec((tq, width), qmap if kind == "q" else kmap)
    n_pairs = tables[0].shape[0]
    n_groups = N_HEADS // HEADS_PER_STEP
    n_in, n_out, n_scr = len(in_kinds), len(out_kinds), scratch
    x_ins = list(carried.ins) if carried else []
    x_outs = list(carried.outs) if carried else []
    x_scr = [pltpu.SemaphoreType.DMA((carried.n_sem,)), pltpu.SemaphoreType.DMA((carried.n_sem,))] if carried else []

    def full_body(qt, kt, *refs):
        ins, refs = refs[:n_in], refs[n_in:]
        xi, refs = refs[:len(x_ins)], refs[len(x_ins):]
        outs, refs = refs[:n_out], refs[n_out:]
        xo, refs = refs[:len(x_outs)], refs[len(x_outs):]
        scr, sems = refs[:n_scr], refs[n_scr:]
        if carried:
            @pl.when((pl.program_id(0) == 0) & (pl.program_id(1) == 0))
            def _():
                carried.start(xi, xo, *sems)

        heads = [tuple(r.at[:, pl.ds(hh * HEAD_LANES, HEAD_LANES)] for r in (*ins, *outs, *scr))
                 for hh in range(HEADS_PER_STEP)]
        body(qt, kt, heads)
        if carried:
            @pl.when((pl.program_id(0) == n_groups - 1) & (pl.program_id(1) == n_pairs - 1))
            def _():
                carried.finish(xi, xo, *sems)

    grid_spec = pltpu.PrefetchScalarGridSpec(
        num_scalar_prefetch=2, grid=(n_groups, n_pairs),
        in_specs=[spec(kd) for kd in in_kinds] + [HBM_SPEC] * len(x_ins),
        out_specs=[spec(kd) for kd in out_kinds] + [HBM_SPEC] * len(x_outs),
        scratch_shapes=[pltpu.VMEM((tq, width), F32)] * n_scr + x_scr)
    return pl.pallas_call(full_body, name=name, grid_spec=grid_spec,
                          out_shape=[jax.ShapeDtypeStruct((t, MLA_PAD), F32) for _ in out_kinds] + x_outs,
                          compiler_params=_params(("arbitrary", "arbitrary")))(*tables, *args, *x_ins)


class _Carried:
    def __init__(self, ins, outs, n_sem, start, finish):
        self.ins, self.outs, self.n_sem, self.start, self.finish = ins, outs, n_sem, start, finish


def _attn_fwd(q, k, v, tq, carried=None):
    t = q.shape[0]
    nq = t // tq

    def body(qt, kt, heads):
        p_id = pl.program_id(1)
        qi, ki = qt[p_id], kt[p_id]

        @pl.when(ki == 0)
        def _():
            for _, _, _, _, _, m_s, l_s, acc_s in heads:
                m_s[...] = jnp.full(m_s.shape, NEG, F32)
                l_s[...] = jnp.zeros(l_s.shape, F32)
                acc_s[...] = jnp.zeros(acc_s.shape, F32)

        def update(diagonal):
            for q_ref, k_ref, v_ref, _, _, m_s, l_s, acc_s in heads:
                s = lax.dot_general(q_ref[...], k_ref[...], (((1,), (1,)), ((), ())), preferred_element_type=F32)
                if diagonal:
                    s = jnp.where(_below_diagonal(s.shape), s, NEG)
                m_old = m_s[...]
                m_new = jnp.maximum(m_old, jnp.max(s, axis=1, keepdims=True))
                alpha = jnp.exp(m_old - m_new)
                p = jnp.exp(s - m_new[:, :1])
                l_s[...] = l_s[...] * alpha + jnp.sum(p, axis=1, keepdims=True)
                acc_s[...] = acc_s[...] * alpha + jnp.dot(p.astype(MM_DTYPE), v_ref[...], preferred_element_type=F32)
                m_s[...] = m_new

        @pl.when(ki < qi)
        def _():
            update(False)

        @pl.when(ki == qi)
        def _():
            update(True)
            for _, _, _, o_ref, lse_ref, m_s, l_s, acc_s in heads:
                o_ref[...] = acc_s[...] / l_s[...]
                lse_ref[...] = m_s[...] + jnp.log(l_s[...])

    return _attn_call("mla_attn_fwd", body, _causal_pairs(nq, False), (q, k, v), "qkk", "qq", 3, t, tq, carried)


def _attn_probs(q, k, lse, diagonal):
    s = lax.dot_general(q, k, (((1,), (1,)), ((), ())), preferred_element_type=F32)
    p = jnp.exp(s - lse[:, :1])
    return jnp.where(_below_diagonal(s.shape), p, 0.0) if diagonal else p


BWD_HEADS = 4


def _attn_bwd(q, k, v, do, lse, delta, tq, carried=None):
    t = q.shape[0]
    nq = t // tq
    width = BWD_HEADS * HEAD_LANES
    n_groups = N_HEADS // BWD_HEADS
    qt_tab, kt_tab = _causal_pairs(nq, True)
    n_pairs = qt_tab.shape[0]
    qmap = lambda h, p, qt, kt: (qt[p], h)
    kmap = lambda h, p, qt, kt: (kt[p], h)
    qs, ks = pl.BlockSpec((tq, width), qmap), pl.BlockSpec((tq, width), kmap)
    x_ins = list(carried.ins) if carried else []
    x_outs = list(carried.outs) if carried else []
    x_scr = [pltpu.SemaphoreType.DMA((carried.n_sem,)), pltpu.SemaphoreType.DMA((carried.n_sem,))] if carried else []
    nt_dims = (((1,), (1,)), ((), ()))
    tn_dims = (((0,), (0,)), ((), ()))

    def body(qt, kt, q_ref, k_ref, v_ref, do_ref, lse_ref, dl_ref, *rest):
        xi, rest = rest[:len(x_ins)], rest[len(x_ins):]
        dq_hbm, dk_ref, dv_ref = rest[:3]
        xo, rest = rest[3:3 + len(x_outs)], rest[3 + len(x_outs):]
        dk_s, dv_s, dq_s, dq_sem = rest[:4]
        sems = rest[4:]
        grp, p_id = pl.program_id(0), pl.program_id(1)
        qi, ki = qt[p_id], kt[p_id]
        if carried:
            @pl.when((grp == 0) & (p_id == 0))
            def _():
                carried.start(xi, xo, *sems)

        @pl.when(p_id == 0)
        def _():
            dq_s[...] = jnp.zeros(dq_s.shape, F32)

        def step(diagonal):
            rows = pl.ds(pl.multiple_of(qi * tq, tq), tq)
            for hh in range(BWD_HEADS):
                ln = pl.ds(hh * HEAD_LANES, HEAD_LANES)
                qb, kb, vb, dob = q_ref[:, ln], k_ref[:, ln], v_ref[:, ln], do_ref[:, ln]
                p = _attn_probs(qb, kb, lse_ref[:, ln], diagonal)
                dv_s[:, ln] += lax.dot_general(p.astype(MM_DTYPE), dob, tn_dims, preferred_element_type=F32)
                dp = lax.dot_general(dob, vb, nt_dims, preferred_element_type=F32)
                ds = (p * (dp - dl_ref[:, ln][:, :1])).astype(MM_DTYPE)
                dk_s[:, ln] += lax.dot_general(ds, qb, tn_dims, preferred_element_type=F32)
                dq_s[rows, ln] += jnp.dot(ds, kb, preferred_element_type=F32)

        @pl.when(qi == ki)
        def _():
            dk_s[...] = jnp.zeros(dk_s.shape, F32)
            dv_s[...] = jnp.zeros(dv_s.shape, F32)
            step(True)

        @pl.when(qi > ki)
        def _():
            step(False)

        @pl.when(qi == nq - 1)
        def _():
            dk_ref[...] = dk_s[...]
            dv_ref[...] = dv_s[...]

        @pl.when(p_id == n_pairs - 1)
        def _():
            out = pltpu.make_async_copy(dq_s, dq_hbm.at[pl.ds(pl.multiple_of(grp * t, t), t)], dq_sem)
            out.start()
            out.wait()

        if carried:
            @pl.when((grp == n_groups - 1) & (p_id == n_pairs - 1))
            def _():
                carried.finish(xi, xo, *sems)

    grid_spec = pltpu.PrefetchScalarGridSpec(
        num_scalar_prefetch=2, grid=(n_groups, n_pairs),
        in_specs=[qs, ks, ks, qs, qs, qs] + [HBM_SPEC] * len(x_ins),
        out_specs=[HBM_SPEC, ks, ks] + [HBM_SPEC] * len(x_outs),
        scratch_shapes=[pltpu.VMEM((tq, width), F32), pltpu.VMEM((tq, width), F32), pltpu.VMEM((t, width), F32),
                        pltpu.SemaphoreType.DMA] + x_scr)
    return pl.pallas_call(
        body, name="mla_attn_bwd", grid_spec=grid_spec,
        out_shape=[jax.ShapeDtypeStruct((n_groups * t, width), F32), jax.ShapeDtypeStruct((t, MLA_PAD), F32),
                   jax.ShapeDtypeStruct((t, MLA_PAD), F32)] + x_outs,
        compiler_params=_params(("arbitrary", "arbitrary")))(qt_tab, kt_tab, q, k, v, do, lse, delta, *x_ins)


def _dot01(a, b, dims=(((1,), (0,)), ((), ())), ones="rhs"):
    val, sel = (a, b) if ones == "rhs" else (b, a)
    head = val.astype(BF16)
    tail = (val - head.astype(F32)).astype(BF16)
    sel = sel.astype(BF16)
    dot = lambda part: (lax.dot_general(part, sel, dims, preferred_element_type=F32) if ones == "rhs"
                        else lax.dot_general(sel, part, dims, preferred_element_type=F32))
    return dot(head) + dot(tail)


def _dot1(a, b, dims=(((1,), (0,)), ((), ()))):
    return lax.dot_general(a.astype(MM_DTYPE), b.astype(MM_DTYPE), dims, preferred_element_type=F32)


def _dot3(a, b, dims=(((1,), (0,)), ((), ()))):
    return lax.dot_general(a, b, dims, preferred_element_type=F32, precision=lax.Precision.HIGH)


NN3 = (((2,), (1,)), ((0,), (0,)))
NT3 = (((2,), (2,)), ((0,), (0,)))
TN3 = (((1,), (1,)), ((0,), (0,)))


def _tri_masks(nh):
    shape = (nh, CHUNK, CHUNK)
    return lax.broadcasted_iota(jnp.int32, shape, 1), lax.broadcasted_iota(jnp.int32, shape, 2)


def _gdn_chunk_common(k, gcc, bb, row, col, dot=_dot1):
    tril = row >= col
    gcr = jnp.swapaxes(gcc, 1, 2)
    dm = jnp.exp(jnp.where(tril, gcc - gcr, NEG))
    kb = k * bb
    lm = jnp.where(row > col, dot(kb, k, NT3) * dm, 0.0)
    return dm, kb, lm


def _unit_lower_inverse(lm, eye):
    t = eye - lm
    p = lm
    for _ in range(CHUNK.bit_length() - 2):
        p = _dot3(p, p, NN3)
        t = t + _dot3(t, p, NN3)
    return t


def _chunk_sum_matrix(tb, upper):
    r = lax.broadcasted_iota(jnp.int32, (tb, tb), 0)
    c = lax.broadcasted_iota(jnp.int32, (tb, tb), 1)
    same = (r // CHUNK) == (c // CHUNK)
    return (same & ((c >= r) if upper else (c <= r))).astype(F32)


def _gdn_fwd(q, k, v, gb, bb):
    nh, t, dh = q.shape
    nchunk = t // CHUNK

    def body(q_ref, k_ref, v_ref, g_ref, b_ref, o_ref, sall_ref, tall_ref, s_s):
        @pl.when(pl.program_id(0) == 0)
        def _():
            s_s[...] = jnp.zeros(s_s.shape, F32)

        row, col = _tri_masks(nh)
        qh, kh, vh, bbh, gcc = q_ref[...], k_ref[...], v_ref[...], b_ref[...], g_ref[...]
        dm, kb, lm = _gdn_chunk_common(kh, gcc, bbh, row, col)
        eg = jnp.exp(gcc)
        glr = gcc[:, CHUNK - 1:CHUNK, :]
        th = _unit_lower_inverse(lm, (row == col).astype(F32))
        w = _dot1(th, kb * eg, NN3)
        u = _dot1(th, vh * bbh, NN3)
        at = jnp.where(row >= col, _dot1(qh, kh, NT3) * dm, 0.0)
        sh = s_s[...]
        vn = u - _dot1(w, sh, NN3)
        o_ref[...] = _dot1(qh * eg, sh, NN3) + _dot1(at, vn, NN3)
        kd = kh * jnp.exp(glr - gcc)
        sall_ref[:, 0] = sh
        tall_ref[...] = th
        s_s[...] = sh * jnp.exp(glr) + _dot1(kd, vn, TN3)

    blk = pl.BlockSpec((nh, CHUNK, dh), lambda n: (0, n, 0))
    return pl.pallas_call(
        body, name="gdn_fwd", grid=(nchunk,), in_specs=[blk] * 5,
        out_specs=[blk, pl.BlockSpec((nh, 1, dh, dh), lambda n: (0, n, 0, 0)), blk],
        out_shape=[jax.ShapeDtypeStruct((nh, t, dh), F32), jax.ShapeDtypeStruct((nh, nchunk, dh, dh), F32),
                   jax.ShapeDtypeStruct((nh, t, CHUNK), F32)],
        scratch_shapes=[pltpu.VMEM((nh, dh, dh), F32)],
        compiler_params=_params(("arbitrary",)))(q, k, v, gb, bb)


def _gdn_bwd(q, k, v, gb, bb, sall, tall, do):
    nh, t, dh = q.shape
    nchunk = t // CHUNK

    def body(q_ref, k_ref, v_ref, g_ref, b_ref, sall_ref, tall_ref, do_ref,
             dq_ref, dk_ref, dv_ref, dg_ref, db_ref, ds_s):
        @pl.when(pl.program_id(0) == 0)
        def _():
            ds_s[...] = jnp.zeros(ds_s.shape, F32)

        row, col = _tri_masks(nh)
        tril, stril = row >= col, row > col
        rsum = lambda x: jnp.sum(x, axis=2, keepdims=True)
        qh, kh, vh, gcc, bbh = q_ref[...], k_ref[...], v_ref[...], g_ref[...], b_ref[...]
        sh, th, doh, dsp = sall_ref[:, 0], tall_ref[...], do_ref[...], ds_s[...]
        dm, kb, lm = _gdn_chunk_common(kh, gcc, bbh, row, col, _dot3)
        eg = jnp.exp(gcc)
        glr = gcc[:, CHUNK - 1:CHUNK, :]
        glv = jnp.exp(glr)
        egl = jnp.exp(glr - gcc)
        rw, ru = kb * eg, vh * bbh
        w, u = _dot3(th, rw, NN3), _dot3(th, ru, NN3)
        at = jnp.where(tril, _dot3(qh, kh, NT3) * dm, 0.0)
        qd, kd = qh * eg, kh * egl
        vn = u - _dot3(w, sh, NN3)
        dgl = jnp.sum(rsum(dsp * sh), axis=1, keepdims=True)
        dkd = _dot3(vn, dsp, NT3)
        dvn = _dot3(kd, dsp, NN3)
        dqd = _dot3(doh, sh, NT3)
        dat = jnp.where(tril, _dot3(doh, vn, NT3), 0.0)
        dvn = dvn + _dot3(at, doh, TN3)
        dw = -_dot3(dvn, sh, NT3)
        ds_s[...] = dsp * glv + _dot3(qd, doh, TN3) - _dot3(w, dvn, TN3)
        dpa = dat * dm
        dq_ref[...] = _dot1(dpa, kh, NN3) + dqd * eg
        dk = _dot1(dpa, qh, TN3) + dkd * egl
        t6 = rsum(dkd * kd)
        dgam = rsum(dqd * qd) - t6
        dgam_last = jnp.sum(t6, axis=1, keepdims=True) + dgl * glv
        drw = _dot3(th, dw, TN3)
        dru = _dot3(th, dvn, TN3)
        dl = -jnp.where(stril, _dot3(drw, w, NT3) + _dot3(dru, u, NT3), 0.0)
        dgam = dgam + rsum(drw * rw)
        dv_ref[...] = dru * bbh
        dp2 = dl * dm
        dkb = drw * eg + _dot1(dp2, kh, NN3)
        dk_ref[...] = dk + _dot1(dp2, kb, TN3) + dkb * bbh
        db_ref[...] = rsum(dru * vh) + rsum(dkb * kh) + jnp.zeros((nh, CHUNK, dh), F32)
        e = dat * at + dl * lm
        dgam_b = dgam + rsum(e) - _dot01(e, jnp.ones((nh, CHUNK, CHUNK), F32), TN3)
        dgam_b = dgam_b + jnp.where(row == CHUNK - 1, dgam_last, 0.0)
        dg_ref[...] = dgam_b

    rev = lambda n: (0, nchunk - 1 - n, 0)
    blk = pl.BlockSpec((nh, CHUNK, dh), rev)
    sblk = pl.BlockSpec((nh, 1, dh, dh), lambda n: (0, nchunk - 1 - n, 0, 0))
    out = jax.ShapeDtypeStruct((nh, t, dh), F32)
    return pl.pallas_call(
        body, name="gdn_bwd", grid=(nchunk,), in_specs=[blk] * 5 + [sblk, blk, blk], out_specs=[blk] * 5,
        out_shape=[out] * 5, scratch_shapes=[pltpu.VMEM((nh, dh, dh), F32)],
        compiler_params=_params(("arbitrary",)))(q, k, v, gb, bb, sall, tall, do)


def _group_ones():
    r = lax.broadcasted_iota(jnp.int32, (GDN_W, GDN_W), 0) // GDN_DH
    c = lax.broadcasted_iota(jnp.int32, (GDN_W, GDN_W), 1) // GDN_DH
    return (r == c).astype(F32)


def _conv_taps(x, xprev, w, has_prev):
    row = lax.broadcasted_iota(jnp.int32, x.shape, 0)
    out = x * w[GDN_CONV - 1:GDN_CONV, :]
    for s in range(1, GDN_CONV):
        sh = jnp.where(row >= s, _roll(x, s, 0), _roll(xprev, s, 0) * has_prev)
        out = out + sh * w[GDN_CONV - 1 - s:GDN_CONV - s, :]
    return out


def _head_cols(x, h):
    return x[:, h * GDN_DH:(h + 1) * GDN_DH]


def _heads_spec(tb):
    return pl.BlockSpec((N_HEADS, tb, GDN_DH), lambda i: (0, i, 0))


def _mixer_fwd(x, positions, w, tb, carried=None):
    t, d = x.shape
    tables = _rope_tables(positions)

    def pre(xb, g):
        return (xb * _rms_stats(xb) * g,)

    (hn,) = _rowwise("mix_pre", pre, [x], [w["mix_pre_g"]], [(d, BF16)], [], tb)
    proj = _mm("mix_in", hn, w["w_in_pad_t"], "nt", F32)

    def mla_pre(p0, gq, gkv):
        cq, ckv = p0[:, :MLA_Q_RANK], p0[:, MLA_Q_RANK:MLA_Q_RANK + MLA_KV_RANK]
        return cq * _rms_stats(cq) * gq, ckv * _rms_stats(ckv) * gkv

    nq, nkv = _rowwise("mla_pre", mla_pre, [(proj, 512, PIN_MLA // 512, 0)],
                       [w["mla_q_norm_g"], w["mla_kv_norm_g"]], [(MLA_Q_RANK, BF16), (MLA_KV_RANK, BF16)], [], tb)
    qraw = _mm("mla_uq", nq, w["w_uq_pad"], "nn", F32)
    kv = _mm("mla_ukv", nkv, w["w_kv_pad"], "nn", F32)

    def rope_f(qr, kn, vv, kpe, c, s1, s2):
        qo = _heads_apply(qr, lambda xh: _rope(xh, c, s1, s2)) * _attn_scale()
        kp = _rope(kpe, c, s1, s2)
        return qo, kn + jnp.tile(kp, (1, N_HEADS)), vv

    q, k, v = _rowwise("mla_rope", rope_f,
                       [qraw, (kv, MLA_PAD, 0, 0), (kv, MLA_PAD, 1, 0), (proj, HEAD_LANES, PIN_KPE // HEAD_LANES, 0),
                        tables[0], tables[1], tables[2]], [],
                       [(MLA_PAD, BF16)] * 3, [], tb // 2)
    tq = min(512, t)
    o, lse, *carried_out = _attn_fwd(q, k, v, tq, carried)

    def mla_post(ob, g):
        return (ob * _rms_stats(ob, N_HEADS * MLA_V) * g,)

    (cat,) = _rowwise("mla_post", mla_post, [o], [w["mla_out_g_pad"]], [(MLA_PAD, BF16)], [], tb, wide=(CAT_W, 0))

    gones = _group_ones()
    steps = t // tb

    def gdn_pre(xq, xk, xv, pq, pk, pv, cw, go, has_prev):
        outs = []
        for j, (xc, xp) in enumerate(((xq, pq), (xk, pk), (xv, pv))):
            c = _conv_taps(xc, xp, cw[:, j * GDN_W:(j + 1) * GDN_W], has_prev)
            a = c * _sigmoid(c)
            if j < 2:
                rn = lax.rsqrt(_dot01(a * a, go) + EPS)
                a = a * rn
                if j == 0:
                    a = a * (GDN_DH ** -0.5)
            outs.append(a)
        return tuple(outs)

    qh, kh, vh = _gdn_pre_call("gdn_pre", gdn_pre, proj, w["conv_w"], gones, tb, steps)
    heads_shape = jax.ShapeDtypeStruct((N_HEADS, t, GDN_DH), F32)
    lanes_shape = jax.ShapeDtypeStruct((t, HEAD_LANES), F32)
    lanes_spec = pl.BlockSpec((tb, HEAD_LANES), lambda i: (i, 0))
    vec_spec = lambda n: pl.BlockSpec((1, n), lambda i: (0, 0))

    def gate_f(ab_ref, al_ref, dt_ref, g_ref, b_ref, gh_ref, bh_ref):
        g, b = _gb_fwd(ab_ref[...], al_ref[...], dt_ref[...])
        g_ref[...] = g
        b_ref[...] = b
        gc = _dot01(_chunk_sum_matrix(tb, False), g, ones="lhs")
        for h in range(N_HEADS):
            gh_ref[h] = jnp.broadcast_to(gc[:, h:h + 1], (tb, GDN_DH))
            bh_ref[h] = jnp.broadcast_to(b[:, N_HEADS + h:N_HEADS + h + 1], (tb, GDN_DH))

    g128, b128, gbh, bbh = pl.pallas_call(
        gate_f, name="gdn_gate_f", grid=(steps,),
        in_specs=[pl.BlockSpec((tb, HEAD_LANES), lambda i: (i, PIN_AB // HEAD_LANES)), vec_spec(HEAD_LANES),
                  vec_spec(HEAD_LANES)],
        out_specs=[lanes_spec, lanes_spec, _heads_spec(tb), _heads_spec(tb)],
        out_shape=[lanes_shape, lanes_shape, heads_shape, heads_shape],
        compiler_params=_params(("arbitrary",)))(proj, w["a_log_pad"], w["dt_bias_pad"])
    oh, sall, tall = _gdn_fwd(qh, kh, vh, gbh, bbh)

    def gdn_post(o_ref, gt_ref, g_ref, cat_in, cat_ref):
        gt, g = gt_ref[...], g_ref[...]
        outs = []
        for h in range(N_HEADS):
            ob, gth = o_ref[h], _head_cols(gt, h)
            outs.append(ob * _rms_stats(ob) * g * (gth * _sigmoid(gth)))
        cat_ref[...] = jnp.concatenate(outs, axis=1).astype(cat_ref.dtype)

    gate_spec = pl.BlockSpec((tb, GDN_W), lambda i: (i, PIN_GATE // GDN_W))
    cat = pl.pallas_call(
        gdn_post, name="gdn_post", grid=(steps,),
        in_specs=[_heads_spec(tb), gate_spec, vec_spec(GDN_DH), ANY_SPEC],
        out_specs=pl.BlockSpec((tb, GDN_W), lambda i: (i, MLA_PAD // GDN_W)),
        out_shape=jax.ShapeDtypeStruct((t, CAT_W), BF16), input_output_aliases={3: 0},
        compiler_params=_params(("arbitrary",)))(oh, proj, w["gdn_norm_g"], cat)
    mixed = _mm("mix_out", cat, w["w_out_pad"], "nn", F32)

    def post(xb, hb, g):
        return (xb + hb * _rms_stats(hb) * g,)

    (y,) = _rowwise("mix_post", post, [x, mixed], [w["mix_post_g"]], [(d, F32)], [], tb)
    saved = dict(x=x, hn=hn, proj=proj, nq=nq, nkv=nkv, q=q, k=k, v=v, o=o, lse=lse, qh=qh, kh=kh, vh=vh,
                 gbh=gbh, bbh=bbh, oh=oh, sall=sall, tall=tall, cat=cat, mixed=mixed,
                 tables=tables, g128=g128, b128=b128)
    return y, saved, carried_out


def _qkv_specs(tb):
    base = PIN_QKV // GDN_W
    cur = [pl.BlockSpec((tb, GDN_W), lambda i, j=j: (i, base + j)) for j in range(3)]
    prev = [pl.BlockSpec((tb, GDN_W), lambda i, j=j: (jnp.maximum(i - 1, 0), base + j)) for j in range(3)]
    return cur + prev


def _gdn_pre_call(name, fn, proj, conv_w, gones, tb, steps):
    t = proj.shape[0]

    def body(xq, xk, xv, pq, pk, pv, cw, go, oq, ok, ov):
        has_prev = jnp.where(pl.program_id(0) == 0, 0.0, 1.0)
        outs = fn(xq[...], xk[...], xv[...], pq[...], pk[...], pv[...], cw[...], go[...], has_prev)
        for r, val in zip((oq, ok, ov), outs):
            for h in range(N_HEADS):
                r[h] = _head_cols(val, h)

    return pl.pallas_call(
        body, name=name, grid=(steps,),
        in_specs=_qkv_specs(tb) + [pl.BlockSpec(conv_w.shape, lambda i: (0, 0)),
                                   pl.BlockSpec(gones.shape, lambda i: (0, 0))],
        out_specs=[_heads_spec(tb)] * 3,
        out_shape=[jax.ShapeDtypeStruct((N_HEADS, t, GDN_DH), F32)] * 3,
        compiler_params=_params(("arbitrary",)))(proj, proj, proj, proj, proj, proj, conv_w, gones)


def _softplus(x):
    return jnp.maximum(x, 0.0) + jnp.log1p(jnp.exp(-jnp.abs(x)))


def _gb_fwd(ab, a_log, dt_bias):
    g = -jnp.exp(a_log) * _softplus(ab + dt_bias)
    return g, _sigmoid(ab)


def _rope_tables(positions):
    half = MLA_ROPE // 2
    freqs = ROPE_THETA ** (-jnp.arange(half, dtype=F32) / half)
    ang = positions.reshape(-1).astype(F32)[:, None] * freqs
    cos, sin = jnp.cos(ang), jnp.sin(ang)
    t = ang.shape[0]
    one = jnp.ones((t, MLA_NOPE), F32)
    z16, z32, z64 = jnp.zeros((t, half), F32), jnp.zeros((t, MLA_ROPE), F32), jnp.zeros((t, MLA_NOPE), F32)
    c = jnp.concatenate([one, cos, cos, jnp.ones((t, MLA_ROPE), F32)], axis=1)
    s1 = jnp.concatenate([z64, -sin, z16, z32], axis=1)
    s2 = jnp.concatenate([z64, z16, sin, z32], axis=1)
    return c, s1, s2


def _mixer_bwd(dy, sv, w, tb, carried=None):
    x, proj = sv["x"], sv["proj"]
    t, d = x.shape
    c, s1, s2 = sv["tables"]
    grads = {}

    def post_b(hb, dyb, g):
        return _rms_bwd(hb, _rms_stats(hb), g, dyb)

    dmixed, grads["mix_post_g"] = _rowwise("mix_post_b", post_b, [sv["mixed"], dy], [w["mix_post_g"]],
                                           [(d, BF16)], [(1, d)], tb)
    dcat = _mm("mix_out_bx", dmixed, w["w_out_pad"], "nt", F32)
    grads["w_out_pad"] = _mm("mix_out_bw", sv["cat"], dmixed, "tn", F32)
    steps = t // tb
    vec_spec = lambda n: pl.BlockSpec((1, n), lambda i: (0, 0))

    def gdn_post_b(o_ref, gt_ref, do_ref, g_ref, dproj_ref, doh_ref, dg_ref):
        @pl.when(pl.program_id(0) == 0)
        def _():
            dg_ref[...] = jnp.zeros(dg_ref.shape, F32)

        gt, dob, g = gt_ref[...], do_ref[...], g_ref[...]
        dgates = []
        for h in range(N_HEADS):
            ob, gth, dobh = o_ref[h], _head_cols(gt, h), _head_cols(dob, h)
            sg = _sigmoid(gth)
            r = _rms_stats(ob)
            dxo, dg = _rms_bwd(ob, r, g, dobh * (gth * sg))
            doh_ref[h] = dxo
            dg_ref[...] += dg
            dgates.append(dobh * (ob * r * g) * (sg * (1.0 + gth * (1.0 - sg))))
        dproj_ref[...] = jnp.concatenate(dgates, axis=1).astype(dproj_ref.dtype)

    dproj, doh, grads["gdn_norm_g"] = pl.pallas_call(
        gdn_post_b, name="gdn_post_b", grid=(steps,),
        in_specs=[_heads_spec(tb), pl.BlockSpec((tb, GDN_W), lambda i: (i, PIN_GATE // GDN_W)),
                  pl.BlockSpec((tb, GDN_W), lambda i: (i, MLA_PAD // GDN_W)), vec_spec(GDN_DH)],
        out_specs=[pl.BlockSpec((tb, GDN_W), lambda i: (i, PIN_GATE // GDN_W)), _heads_spec(tb), vec_spec(GDN_DH)],
        out_shape=[jax.ShapeDtypeStruct((t, PIN_W), BF16), jax.ShapeDtypeStruct((N_HEADS, t, GDN_DH), F32),
                   jax.ShapeDtypeStruct((1, GDN_DH), F32)],
        compiler_params=_params(("arbitrary",)))(sv["oh"], proj, dcat, w["gdn_norm_g"])

    def mla_post_b(ob, dmo, g):
        do, dg = _rms_bwd(ob, _rms_stats(ob, N_HEADS * MLA_V), g, dmo, N_HEADS * MLA_V)
        prod = do * ob
        delta = _heads_apply(prod, lambda ph: jnp.sum(ph, axis=1, keepdims=True) + jnp.zeros_like(ph))
        return do, delta, dg

    do, delta, grads["mla_out_g_pad"] = _rowwise(
        "mla_post_b", mla_post_b, [sv["o"], (dcat, MLA_PAD, 0, 0)], [w["mla_out_g_pad"]],
        [(MLA_PAD, BF16), (MLA_PAD, F32)], [(1, MLA_PAD)], tb // 2)
    tq = min(512, t)
    dq, dk, dv, *carried_out = _attn_bwd(sv["q"], sv["k"], sv["v"], do, sv["lse"], delta, tq, carried)
    n_groups = N_HEADS // BWD_HEADS
    tr = tb // 2
    dq_groups = [(dq, BWD_HEADS * HEAD_LANES, 0, grp * (t // tr)) for grp in range(n_groups)]

    def rope_b(*blocks):
        dqb = jnp.concatenate(blocks[:n_groups], axis=1)
        dkb, dvb, cc, a1, a2 = blocks[n_groups:]
        dqr = _heads_apply(dqb * _attn_scale(), lambda xh: _rope(xh, cc, -a1, -a2))
        ksum = dkb[:, :HEAD_LANES]
        for h in range(1, N_HEADS):
            ksum = ksum + dkb[:, h * HEAD_LANES:(h + 1) * HEAD_LANES]
        lane = lax.broadcasted_iota(jnp.int32, ksum.shape, 1)
        keep = (lane >= MLA_NOPE) & (lane < MLA_NOPE + MLA_ROPE)
        dkpe = jnp.where(keep, _rope(ksum, cc, -a1, -a2), 0.0)
        return dqr, jnp.concatenate([dkb, dvb], axis=1), dkpe

    dqraw, dkv, dkpe = _rowwise("mla_rope_b", rope_b, dq_groups + [dk, dv, c, s1, s2], [],
                                [(MLA_PAD, BF16, t), (2 * MLA_PAD, BF16), (HEAD_LANES, F32)], [], tr)
    dnq = _mm("mla_uq_bx", dqraw, w["w_uq_pad"], "nt", F32)
    grads["w_uq_pad"] = _mm("mla_uq_bw", sv["nq"], dqraw, "tn", F32)
    dnkv = _mm("mla_ukv_bx", dkv, w["w_kv_pad"], "nt", F32)
    grads["w_kv_pad"] = _mm("mla_ukv_bw", sv["nkv"], dkv, "tn", F32)

    def mla_pre_b(p0, dnqb, dnkvb, dkpeb, gq, gkv):
        cq, ckv = p0[:, :MLA_Q_RANK], p0[:, MLA_Q_RANK:MLA_Q_RANK + MLA_KV_RANK]
        dcq, dgq = _rms_bwd(cq, _rms_stats(cq), gq, dnqb)
        dckv, dgkv = _rms_bwd(ckv, _rms_stats(ckv), gkv, dnkvb)
        return jnp.concatenate([dcq, dckv, dkpeb], axis=1), dgq, dgkv

    dproj, grads["mla_q_norm_g"], grads["mla_kv_norm_g"] = _rowwise(
        "mla_pre_b", mla_pre_b, [(proj, 512, PIN_MLA // 512, 0), dnq, dnkv, dkpe],
        [w["mla_q_norm_g"], w["mla_kv_norm_g"]], [(512, BF16)], [(1, MLA_Q_RANK), (1, MLA_KV_RANK)], tb,
        wide=(PIN_W, PIN_MLA // 512), carry=dproj)

    dqh, dkh, dvh, dgh, dbh = _gdn_bwd(sv["qh"], sv["kh"], sv["vh"], sv["gbh"], sv["bbh"], sv["sall"], sv["tall"], doh)
    gones = _group_ones()

    def gdn_pre_b(xq, xk, xv, pq, pk, pv, dq_, dk_, dv_, cw, go, has_prev):
        outs = []
        for j, (xc, xp, dd) in enumerate(((xq, pq, dq_), (xk, pk, dk_), (xv, pv, dv_))):
            cc = _conv_taps(xc, xp, cw[:, j * GDN_W:(j + 1) * GDN_W], has_prev)
            sg = _sigmoid(cc)
            a = cc * sg
            if j < 2:
                rn = lax.rsqrt(_dot01(a * a, go) + EPS)
                if j == 0:
                    dd = dd * (GDN_DH ** -0.5)
                da = rn * dd - a * (rn * rn * rn) * _dot01(dd * a, go)
            else:
                da = dd
            outs.append(da * (sg * (1.0 + cc * (1.0 - sg))))
        return tuple(outs)

    dcq, dck, dcv = _gdn_pre_b_call("gdn_pre_b", gdn_pre_b, proj, (dqh, dkh, dvh), w["conv_w"], gones, tb, steps)
    dproj, grads["conv_w"] = _conv_bwd_call("gdn_conv_b", proj, (dcq, dck, dcv), w["conv_w"], dproj, tb, steps)

    def gate_b(ab_ref, g_ref, b_ref, dgh_ref, dbh_ref, al_ref, dt_ref, carry_ref, dab_ref, dal_ref, ddt_ref):
        @pl.when(pl.program_id(0) == 0)
        def _():
            dal_ref[...] = jnp.zeros(dal_ref.shape, F32)
            ddt_ref[...] = jnp.zeros(ddt_ref.shape, F32)

        ab, g128, b128 = ab_ref[...], g_ref[...], b_ref[...]
        lane = lax.broadcasted_iota(jnp.int32, ab.shape, 1)
        dg_ = jnp.zeros(ab.shape, F32)
        db_ = jnp.zeros(ab.shape, F32)
        for h in range(N_HEADS):
            dg_ = dg_ + jnp.where(lane == h, jnp.broadcast_to(dgh_ref[h][:, 0:1], ab.shape), 0.0)
            db_ = db_ + jnp.where(lane == N_HEADS + h, jnp.broadcast_to(dbh_ref[h][:, 0:1], ab.shape), 0.0)
        dg_ = _dot01(_chunk_sum_matrix(tb, True), dg_, ones="lhs")
        slope = -jnp.exp(al_ref[...]) * _sigmoid(ab + dt_ref[...])
        dab_ref[...] = (dg_ * slope + db_ * b128 * (1.0 - b128)).astype(dab_ref.dtype)
        dal_ref[...] += jnp.sum(dg_ * g128, axis=0, keepdims=True)
        ddt_ref[...] += jnp.sum(dg_ * slope, axis=0, keepdims=True)

    lanes_spec = pl.BlockSpec((tb, HEAD_LANES), lambda i: (i, 0))
    ab_spec = pl.BlockSpec((tb, HEAD_LANES), lambda i: (i, PIN_AB // HEAD_LANES))
    dproj, grads["a_log_pad"], grads["dt_bias_pad"] = pl.pallas_call(
        gate_b, name="gdn_gate_b", grid=(steps,),
        in_specs=[ab_spec, lanes_spec, lanes_spec, _heads_spec(tb), _heads_spec(tb), vec_spec(HEAD_LANES),
                  vec_spec(HEAD_LANES), ANY_SPEC],
        out_specs=[ab_spec, vec_spec(HEAD_LANES), vec_spec(HEAD_LANES)],
        out_shape=[jax.ShapeDtypeStruct((t, PIN_W), BF16), jax.ShapeDtypeStruct((1, HEAD_LANES), F32),
                   jax.ShapeDtypeStruct((1, HEAD_LANES), F32)],
        input_output_aliases={7: 0},
        compiler_params=_params(("arbitrary",)))(proj, sv["g128"], sv["b128"], dgh, dbh, w["a_log_pad"],
                                                 w["dt_bias_pad"], dproj)
    dhn = _mm("mix_in_bx", dproj, w["w_in_pad_t"], "nn", F32)
    grads["w_in_pad_t"] = _mm("mix_in_bw", dproj, sv["hn"], "tn", F32)

    def pre_b(xb, dnb, dyb, g):
        dx, dg = _rms_bwd(xb, _rms_stats(xb), g, dnb)
        return dyb + dx, dg

    dx, grads["mix_pre_g"] = _rowwise("mix_pre_b", pre_b, [x, dhn, dy], [w["mix_pre_g"]], [(d, F32)], [(1, d)], tb)
    return dx, grads, carried_out


def _gdn_pre_b_call(name, fn, proj, dd, conv_w, gones, tb, steps):
    t = proj.shape[0]

    def body(xq, xk, xv, pq, pk, pv, d0, d1, d2, cw, go, oq, ok, ov):
        has_prev = jnp.where(pl.program_id(0) == 0, 0.0, 1.0)
        dd_rows = [jnp.concatenate([dr[h] for h in range(N_HEADS)], axis=1) for dr in (d0, d1, d2)]
        outs = fn(xq[...], xk[...], xv[...], pq[...], pk[...], pv[...], *dd_rows, cw[...], go[...], has_prev)
        for r, val in zip((oq, ok, ov), outs):
            r[...] = val

    return pl.pallas_call(
        body, name=name, grid=(steps,),
        in_specs=_qkv_specs(tb) + [_heads_spec(tb)] * 3 + [pl.BlockSpec(conv_w.shape, lambda i: (0, 0)),
                                                          pl.BlockSpec(gones.shape, lambda i: (0, 0))],
        out_specs=[pl.BlockSpec((tb, GDN_W), lambda i: (i, 0))] * 3,
        out_shape=[jax.ShapeDtypeStruct((t, GDN_W), F32)] * 3,
        compiler_params=_params(("arbitrary",)))(proj, proj, proj, proj, proj, proj, *dd, conv_w, gones)


def _conv_bwd_call(name, proj, dc, conv_w, dproj, tb, steps):
    t = proj.shape[0]
    dcur = [pl.BlockSpec((tb, GDN_W), lambda i: (i, 0))] * 3
    dnext = [pl.BlockSpec((tb, GDN_W), lambda i: (jnp.minimum(i + 1, steps - 1), 0))] * 3

    def body(xq, xk, xv, pq, pk, pv, d0, d1, d2, n0, n1, n2, cw, carry_ref, dx_ref, dw_ref):
        i = pl.program_id(0)
        has_prev = jnp.where(i == 0, 0.0, 1.0)
        has_next = jnp.where(i == steps - 1, 0.0, 1.0)

        @pl.when(i == 0)
        def _():
            dw_ref[...] = jnp.zeros(dw_ref.shape, F32)

        wv = cw[...]
        dws, dxs = [], []
        for j, (xr, pr, dr, nr) in enumerate(((xq, pq, d0, n0), (xk, pk, d1, n1), (xv, pv, d2, n2))):
            x, xp, dcv, dnx = xr[...], pr[...], dr[...], nr[...]
            wj = wv[:, j * GDN_W:(j + 1) * GDN_W]
            row = lax.broadcasted_iota(jnp.int32, x.shape, 0)
            dx = dcv * wj[GDN_CONV - 1:GDN_CONV, :]
            rows_w = [jnp.sum(dcv * x, axis=0, keepdims=True)]
            for s in range(1, GDN_CONV):
                up = jnp.where(row < tb - s, _roll(dcv, tb - s, 0), _roll(dnx, tb - s, 0) * has_next)
                dx = dx + up * wj[GDN_CONV - 1 - s:GDN_CONV - s, :]
                sh = jnp.where(row >= s, _roll(x, s, 0), _roll(xp, s, 0) * has_prev)
                rows_w.append(jnp.sum(dcv * sh, axis=0, keepdims=True))
            dxs.append(dx)
            dws.append(jnp.concatenate(rows_w[::-1], axis=0))
        dx_ref[...] = jnp.concatenate(dxs, axis=1).astype(dx_ref.dtype)
        dw_ref[...] += jnp.concatenate(dws, axis=1)

    return pl.pallas_call(
        body, name=name, grid=(steps,),
        in_specs=_qkv_specs(tb) + dcur + dnext + [pl.BlockSpec(conv_w.shape, lambda i: (0, 0)), ANY_SPEC],
        out_specs=[pl.BlockSpec((tb, 3 * GDN_W), lambda i: (i, PIN_QKV // (3 * GDN_W))),
                   pl.BlockSpec(conv_w.shape, lambda i: (0, 0))],
        out_shape=[jax.ShapeDtypeStruct((t, PIN_W), BF16), jax.ShapeDtypeStruct(conv_w.shape, F32)],
        input_output_aliases={13: 0},
        compiler_params=_params(("arbitrary",)))(proj, proj, proj, proj, proj, proj, *dc, *dc, conv_w, dproj)


def _pad_heads_cols(wm, per_head):
    r = wm.shape[0]
    return jnp.pad(wm.reshape(r, N_HEADS, per_head), ((0, 0), (0, 0), (0, HEAD_LANES - per_head))).reshape(r, MLA_PAD)


def _unpad_heads_cols(wm, per_head):
    r = wm.shape[0]
    return wm.reshape(r, N_HEADS, HEAD_LANES)[:, :, :per_head].reshape(r, N_HEADS * per_head)


W_IN_COLS = MLA_Q_RANK + MLA_KV_RANK + MLA_ROPE + 3 * GDN_W + 2 * N_HEADS + GDN_W
W_IN_SHARD = W_IN_COLS // N_SHARD
W_IN_SHARD_PAD = 640
_Q0 = MLA_Q_RANK + MLA_KV_RANK
_Q1 = _Q0 + MLA_ROPE
_Q2 = _Q1 + 3 * GDN_W
_Q3 = _Q2 + 2 * N_HEADS
W_IN_SEGMENTS = [(0, _Q0, PIN_MLA), (_Q0, _Q1, PIN_KPE + MLA_NOPE), (_Q1, _Q2, PIN_QKV), (_Q2, _Q3, PIN_AB),
                 (_Q3, W_IN_COLS, PIN_GATE)]


def _win_pad_t(slabs):
    d = slabs.shape[2]
    pieces, at = [], 0
    for c0, c1, r0 in sorted(W_IN_SEGMENTS, key=lambda s: s[2]):
        if r0 > at:
            pieces.append(jnp.zeros((r0 - at, d), slabs.dtype))
        for q in range(N_SHARD):
            lo, hi = max(c0, q * W_IN_SHARD), min(c1, (q + 1) * W_IN_SHARD)
            if lo < hi:
                pieces.append(slabs[q, lo - q * W_IN_SHARD:hi - q * W_IN_SHARD])
        at = r0 + c1 - c0
    pieces.append(jnp.zeros((PIN_W - at, d), slabs.dtype))
    return jnp.concatenate(pieces, axis=0)


def _win_cols_t(wp_t, c_lo, c_hi):
    pieces = []
    for c0, c1, r0 in W_IN_SEGMENTS:
        lo, hi = max(c0, c_lo), min(c1, c_hi)
        if lo < hi:
            pieces.append(wp_t[r0 + lo - c0:r0 + hi - c0])
    return jnp.concatenate(pieces, axis=0)


def _wkv_to_pad(wkv):
    r = wkv.shape[0]
    w3 = wkv.reshape(r, N_HEADS, MLA_NOPE + MLA_V)
    kpart = jnp.pad(w3[:, :, :MLA_NOPE], ((0, 0), (0, 0), (0, HEAD_LANES - MLA_NOPE))).reshape(r, MLA_PAD)
    vpart = jnp.pad(w3[:, :, MLA_NOPE:], ((0, 0), (0, 0), (0, HEAD_LANES - MLA_V))).reshape(r, MLA_PAD)
    return jnp.concatenate([kpart, vpart], axis=1)


def _wkv_from_pad(wp):
    r = wp.shape[0]
    kpart = wp[:, :MLA_PAD].reshape(r, N_HEADS, HEAD_LANES)[:, :, :MLA_NOPE]
    vpart = wp[:, MLA_PAD:].reshape(r, N_HEADS, HEAD_LANES)[:, :, :MLA_V]
    return jnp.concatenate([kpart, vpart], axis=2).reshape(r, N_HEADS * (MLA_NOPE + MLA_V))


def _wout_to_pad(wo):
    n = wo.shape[1]
    mla = jnp.pad(wo[:N_HEADS * MLA_V].reshape(N_HEADS, MLA_V, n), ((0, 0), (0, HEAD_LANES - MLA_V), (0, 0)))
    return jnp.concatenate([mla.reshape(MLA_PAD, n), wo[N_HEADS * MLA_V:]], axis=0)


def _wout_from_pad(wp):
    n = wp.shape[1]
    mla = wp[:MLA_PAD].reshape(N_HEADS, HEAD_LANES, n)[:, :MLA_V].reshape(N_HEADS * MLA_V, n)
    return jnp.concatenate([mla, wp[MLA_PAD:]], axis=0)


def _pad_lanes(v, n):
    return jnp.pad(v, ((0, 0), (0, n - v.shape[1])))


def _compute_weights(full):
    w = {}
    for n in FFN_BIG:
        if n in full:
            w[n] = full[n].astype(MM_DTYPE)
    w["w_in_pad_t"] = _win_pad_t(full["w_in"]).astype(MM_DTYPE)
    w["w_uq_pad"] = _pad_heads_cols(full["mla_w_uq"], MLA_NOPE + MLA_ROPE).astype(MM_DTYPE)
    w["w_kv_pad"] = _wkv_to_pad(full["mla_w_ukv"]).astype(MM_DTYPE)
    w["w_out_pad"] = _wout_to_pad(full["w_out"]).astype(MM_DTYPE)
    w["conv_w"] = full["gdn_conv_w"].astype(F32)
    for n in ("ffn1_pre_g", "ffn1_post_g", "mix_pre_g", "mla_q_norm_g", "mla_kv_norm_g", "gdn_norm_g", "mix_post_g",
              "ffn2_pre_g", "ffn2_post_g"):
        w[n] = full[n]
    w["mla_out_g_pad"] = _pad_heads_cols(full["mla_out_g"], MLA_V)
    w["a_log_pad"] = _pad_lanes(full["gdn_a_log"], HEAD_LANES)
    w["dt_bias_pad"] = _pad_lanes(full["gdn_dt_bias"], HEAD_LANES)
    return w


FFN2_BIG = FFN_BIG[3:]


def _local_step(x, positions, loss_target, full, late=None):
    t, d = x.shape
    tb = min(512, t)
    tm = min(1024, t)
    w = _compute_weights(full)
    ffn = lambda tag: (w[tag + "_pre_g"], w[tag + "_w_gate"], w[tag + "_w_up"], w[tag + "_w_down"], w[tag + "_post_g"])
    x1, sv1 = _ffn_fwd("ffn1", x, *ffn("ffn1"), tm)
    x2, svm, gathered = _mixer_fwd(x1, positions, w, tb, _carried_gather(late[0]) if late else None)
    for n, gw in zip(FFN2_BIG, gathered):
        w[n] = gw
    x3, sv2 = _ffn_fwd("ffn2", x2, *ffn("ffn2"), tm)

    def loss_f(yb, tg):
        e = yb - tg
        return e * (1.0 / d), jnp.sum(e * e, axis=0, keepdims=True)

    dy, lsum = _rowwise("loss", loss_f, [x3, loss_target], [], [(d, F32)], [(1, d)], tb)
    g = {}
    dx2, g["ffn2_pre_g"], g["ffn2_w_gate"], g["ffn2_w_up"], g["ffn2_w_down"], g["ffn2_post_g"], _, _ = _ffn_bwd(
        "ffn2", dy, sv2, *ffn("ffn2"), tm, tm)

    def pair_sums(arrs, tag):
        got = _swap_halves(arrs, tag)
        return [_add_pair("add_pair%s_%d" % (tag, i), gi, gt, late[1]) for i, (gi, gt) in enumerate(zip(arrs, got))]

    def chip_sums(pairs, slabs, tag):
        return [_add_chips("add_chips%s_%d" % (tag, i), pr, sl, late[2]) for i, (pr, sl) in enumerate(zip(pairs, slabs))]

    if late:
        pairs2 = pair_sums([g[n] for n in FFN2_BIG], "_ffn2")
        dx1, gm, slabs2 = _mixer_bwd(dx2, svm, w, tb, _carried_scatter(pairs2))
        for n, hs in zip(FFN2_BIG, chip_sums(pairs2, slabs2, "_ffn2")):
            g[n] = hs
    else:
        dx1, gm, _ = _mixer_bwd(dx2, svm, w, tb)
    g["w_in"] = jnp.stack([jnp.pad(_win_cols_t(gm["w_in_pad_t"], q * W_IN_SHARD, (q + 1) * W_IN_SHARD),
                                   ((0, W_IN_SHARD_PAD - W_IN_SHARD), (0, 0))) for q in range(N_SHARD)])
    g["mla_w_uq"] = _unpad_heads_cols(gm["w_uq_pad"], MLA_NOPE + MLA_ROPE)
    g["mla_w_ukv"] = _wkv_from_pad(gm["w_kv_pad"])
    g["gdn_conv_w"] = gm["conv_w"]
    g["w_out"] = _wout_from_pad(gm["w_out_pad"])
    ffn1_names = FFN_BIG[:3]
    if late:
        quarters = [_pack([jnp.split(g[n], N_SHARD, axis=SHARD_AXIS[n])[q] for n in MIX_BIG], MM_DTYPE)
                    for q in range(N_SHARD)]
        pairs_m = pair_sums([g["w_in"].astype(MM_DTYPE), jnp.stack(quarters)], "_mix")
        pairs1 = []

        def make_up(*dws):
            pairs1.extend(pair_sums(list(dws), "_ffn1"))
            return _carried_scatter(pairs1)

        dx0, g["ffn1_pre_g"], _, _, _, g["ffn1_post_g"], slabs_m, slabs1 = _ffn_bwd(
            "ffn1", dx1, sv1, *ffn("ffn1"), tm, tm, _carried_scatter(pairs_m), make_up)
        for n, hs in zip(ffn1_names, chip_sums(pairs1, slabs1, "_ffn1")):
            g[n] = hs
        g["w_in"], g["mix_pack"] = chip_sums(pairs_m, slabs_m, "_mix")
    else:
        dx0, g["ffn1_pre_g"], g["ffn1_w_gate"], g["ffn1_w_up"], g["ffn1_w_down"], g["ffn1_post_g"], _, _ = _ffn_bwd(
            "ffn1", dx1, sv1, *ffn("ffn1"), tm, tm)
    g["mix_pre_g"], g["mix_post_g"] = gm["mix_pre_g"], gm["mix_post_g"]
    g["mla_q_norm_g"], g["mla_kv_norm_g"] = gm["mla_q_norm_g"], gm["mla_kv_norm_g"]
    g["gdn_norm_g"] = gm["gdn_norm_g"]
    g["mla_out_g"] = _unpad_heads_cols(gm["mla_out_g_pad"], MLA_V)
    g["gdn_a_log"] = gm["a_log_pad"][:, :N_HEADS]
    g["gdn_dt_bias"] = gm["dt_bias_pad"][:, :N_HEADS]
    return lsum, dx0, g


HBM_SPEC = pl.BlockSpec(memory_space=pltpu.HBM)


def _place():
    return lax.axis_index("x"), lax.axis_index("y"), lax.axis_index("c")


def _exchange_call(name, body, ins, out_shapes, n_remote, n_local):
    return pl.pallas_call(
        body, name=name, in_specs=[HBM_SPEC] * len(ins), out_specs=[HBM_SPEC] * len(out_shapes), out_shape=out_shapes,
        scratch_shapes=[pltpu.SemaphoreType.DMA((n_remote,)), pltpu.SemaphoreType.DMA((n_remote,)),
                        pltpu.SemaphoreType.DMA((n_local,))])(*ins)


def _other_chips(x, y):
    return [(1 - x, y), (x, 1 - y), (1 - x, 1 - y)]


def _at_each_chip(fn):
    x, y, _ = _place()
    for cx in range(2):
        for cy in range(2):
            pl.when((x == cx) & (y == cy))(functools.partial(fn, cx, cy))


def _at_each_device(fn):
    x, y, c = _place()
    for cx in range(2):
        for cy in range(2):
            for cc in range(2):
                pl.when((x == cx) & (y == cy) & (c == cc))(functools.partial(fn, cx, cy, cc))


def _at_each_core(fn):
    c = lax.axis_index("c")
    for cc in range(2):
        pl.when(c == cc)(functools.partial(fn, cc))


def _gather_shards(ws):
    nw = len(ws)

    def body(*refs):
        w_refs, out_refs = refs[:nw], refs[nw:2 * nw]
        send_sems, recv_sems, local_sems = refs[2 * nw:]

        def run(x, y, c):
            chips = _other_chips(x, y)
            me, sibling = 2 * x + y, (x, y, 1 - c)

            def half(ref, which):
                hr = ref.shape[0] // 2
                return ref.at[pl.ds(which * hr, hr)]

            def over_ici(i, j, src, slab, to):
                return pltpu.make_async_remote_copy(
                    src_ref=half(src, c), dst_ref=half(out_refs[i].at[slab], c), send_sem=send_sems.at[7 * i + j],
                    recv_sem=recv_sems.at[7 * i + j], device_id=to, device_id_type=MESH)

            def over_d2d(i, j, slab, which):
                return pltpu.make_async_remote_copy(
                    src_ref=half(out_refs[i].at[slab], which), dst_ref=half(out_refs[i].at[slab], which),
                    send_sem=send_sems.at[7 * i + 3 + j], recv_sem=recv_sems.at[7 * i + 3 + j], device_id=sibling,
                    device_id_type=MESH)

            def own(i, w_ref):
                return pltpu.make_async_remote_copy(
                    src_ref=w_ref, dst_ref=out_refs[i].at[me], send_sem=send_sems.at[7 * i + 6],
                    recv_sem=recv_sems.at[7 * i + 6], device_id=sibling, device_id_type=MESH)

            sends, passed = [], []
            for i, w_ref in enumerate(w_refs):
                for j, (px, py) in enumerate(chips):
                    sends.append(over_ici(i, j, w_ref, me, (px, py, c)))
                    sends[-1].start()
            for i, w_ref in enumerate(w_refs):
                sends.append(own(i, w_ref))
                sends[-1].start()
            for i, w_ref in enumerate(w_refs):
                for j, (px, py) in enumerate(chips):
                    over_ici(i, j, w_ref, 2 * px + py, (px, py, c)).wait_recv()
                    passed.append(over_d2d(i, j, 2 * px + py, c))
                    passed[-1].start()
            for i, w_ref in enumerate(w_refs):
                own(i, w_ref).wait_recv()
                for j, (px, py) in enumerate(chips):
                    over_d2d(i, j, 2 * px + py, 1 - c).wait_recv()
            for cp in sends + passed:
                cp.wait_send()

        _at_each_device(run)

    outs = [jax.ShapeDtypeStruct((N_SHARD,) + w.shape, w.dtype) for w in ws]
    return _exchange_call("gather_weight_shards", body, ws, outs, 7 * nw, 1)


def _swap_halves(gs, tag=""):
    ng = len(gs)

    def body(*refs):
        g_refs, got_refs = refs[:ng], refs[ng:2 * ng]
        send_sems, recv_sems, _ = refs[2 * ng:]
        x, y, _ = _place()

        def run(c):
            sends = []
            for i, (g_ref, got_ref) in enumerate(zip(g_refs, got_refs)):
                hr = got_ref.shape[1]
                sends.append(pltpu.make_async_remote_copy(
                    src_ref=g_ref.at[:, pl.ds((1 - c) * hr, hr)], dst_ref=got_ref, send_sem=send_sems.at[i],
                    recv_sem=recv_sems.at[i], device_id=(x, y, 1 - c), device_id_type=MESH))
                sends[-1].start()
            for cp in sends:
                cp.wait()

        _at_each_core(run)

    halves = [jax.ShapeDtypeStruct((g.shape[0], g.shape[1] // 2, g.shape[2]), g.dtype) for g in gs]
    return _exchange_call("swap_grad_halves" + tag, body, gs, halves, ng, 1)


def _scatter_copies(p_refs, out_refs, send_sems, recv_sems, x, y):
    c = lax.axis_index("c")
    copies = []
    for i, (p_ref, out_ref) in enumerate(zip(p_refs, out_refs)):
        for j, (px, py) in enumerate(_other_chips(x, y)):
            copies.append(pltpu.make_async_remote_copy(
                src_ref=p_ref.at[2 * px + py], dst_ref=out_ref.at[j], send_sem=send_sems.at[3 * i + j],
                recv_sem=recv_sems.at[3 * i + j], device_id=(px, py, c), device_id_type=MESH))
    return copies


def _start_all(make, *refs):
    def run(x, y):
        for cp in make(*refs, x, y):
            cp.start()

    _at_each_chip(run)


def _wait_all(make, *refs):
    def run(x, y):
        copies = make(*refs, x, y)
        for cp in copies:
            cp.wait_recv()
        for cp in copies:
            cp.wait_send()

    _at_each_chip(run)


def _scatter_shapes(ps):
    return [jax.ShapeDtypeStruct((3,) + p.shape[1:], p.dtype) for p in ps]


def _carried_scatter(ps):
    return _Carried(ps, _scatter_shapes(ps), 3 * len(ps), functools.partial(_start_all, _scatter_copies),
                    functools.partial(_wait_all, _scatter_copies))


def _direct_gather_copies(w_refs, out_refs, send_sems, recv_sems, x, y, arriving):
    c = lax.axis_index("c")
    me = 2 * x + y
    peers = [((px, py, c), 2 * px + py) for px, py in _other_chips(x, y)] + [((x, y, 1 - c), me)]
    copies = []
    for i, (w_ref, out_ref) in enumerate(zip(w_refs, out_refs)):
        for j, (peer, slab) in enumerate(peers):
            copies.append(pltpu.make_async_remote_copy(
                src_ref=w_ref, dst_ref=out_ref.at[slab if arriving else me], send_sem=send_sems.at[4 * i + j],
                recv_sem=recv_sems.at[4 * i + j], device_id=peer, device_id_type=MESH))
    return copies


def _carried_gather(ws):
    def start(w_refs, out_refs, send_sems, recv_sems):
        def run(x, y):
            for cp in _direct_gather_copies(w_refs, out_refs, send_sems, recv_sems, x, y, False):
                cp.start()

        _at_each_chip(run)

    def finish(w_refs, out_refs, send_sems, recv_sems):
        def run(x, y):
            for cp in _direct_gather_copies(w_refs, out_refs, send_sems, recv_sems, x, y, True):
                cp.wait_recv()
            for cp in _direct_gather_copies(w_refs, out_refs, send_sems, recv_sems, x, y, False):
                cp.wait_send()

        _at_each_chip(run)

    outs = [jax.ShapeDtypeStruct((N_SHARD,) + w.shape, w.dtype) for w in ws]
    return _Carried(ws, outs, 4 * len(ws), start, finish)


def _share_halves(hs):
    n = len(hs)

    def body(*refs):
        h_refs, out_refs = refs[:n], refs[n:2 * n]
        send_sems, recv_sems, _ = refs[2 * n:]
        x, y, c = _place()
        sends = []
        for i, (h_ref, out_ref) in enumerate(zip(h_refs, out_refs)):
            sends.append(pltpu.make_async_remote_copy(
                src_ref=h_ref, dst_ref=out_ref, send_sem=send_sems.at[i], recv_sem=recv_sems.at[i],
                device_id=(x, y, 1 - c), device_id_type=MESH))
            sends[-1].start()
        for cp in sends:
            cp.wait()

    outs = [jax.ShapeDtypeStruct(h.shape, h.dtype) for h in hs]
    return _exchange_call("share_grad_halves", body, hs, outs, n, 1)


def _scalar_grid_call(name, body, scalars, grid, in_specs, out_specs, out_shape, args):
    grid_spec = pltpu.PrefetchScalarGridSpec(num_scalar_prefetch=len(scalars), grid=grid, in_specs=in_specs,
                                             out_specs=out_specs)
    return pl.pallas_call(body, name=name, grid_spec=grid_spec, out_shape=out_shape,
                          compiler_params=_params(("arbitrary",) * len(grid)))(*scalars, *args)


def _add_pair(name, g, got, core):
    ns_, hr, cols = got.shape
    th = _row_tile(hr, 512)
    nb = hr // th

    def body(core_ref, g_ref, got_ref, out_ref):
        out_ref[...] = (g_ref[...].astype(F32) + got_ref[...].astype(F32)).astype(out_ref.dtype)

    blk = pl.BlockSpec((1, th, cols), lambda q, j, core_ref: (q, j, 0))
    own = pl.BlockSpec((1, th, cols), lambda q, j, core_ref: (q, core_ref[0] * nb + j, 0))
    return _scalar_grid_call(name, body, [core], (ns_, nb), [own, blk], blk,
                             jax.ShapeDtypeStruct(got.shape, got.dtype), [g, got])


def _add_chips(name, pairs, slabs, chip):
    _, hr, cols = slabs.shape
    th = _row_tile(hr, 512)

    def body(chip_ref, own_ref, s0_ref, s1_ref, s2_ref, out_ref):
        total = own_ref[0].astype(F32) + s0_ref[0].astype(F32)
        out_ref[...] = (total + s1_ref[0].astype(F32)) + s2_ref[0].astype(F32)

    own = pl.BlockSpec((1, th, cols), lambda j, chip_ref: (chip_ref[0], j, 0))
    others = [pl.BlockSpec((1, th, cols), lambda j, chip_ref, k=k: (k, j, 0)) for k in range(3)]
    return _scalar_grid_call(name, body, [chip], (hr // th,), [own] + others,
                             pl.BlockSpec((th, cols), lambda j, chip_ref: (j, 0)),
                             jax.ShapeDtypeStruct((hr, cols), F32), [pairs, slabs, slabs, slabs])


def _join_halves(name, mine, other, core):
    hr, cols = mine.shape
    th = _row_tile(hr, 512)
    nb = hr // th

    def body(core_ref, mine_ref, other_ref, out_ref):
        is_mine = pl.program_id(0) == core_ref[0]

        @pl.when(is_mine)
        def _():
            out_ref[0] = mine_ref[...]

        @pl.when(jnp.logical_not(is_mine))
        def _():
            out_ref[0] = other_ref[...]

    blk = pl.BlockSpec((th, cols), lambda h, j, core_ref: (j, 0))
    return _scalar_grid_call(name, body, [core], (2, nb), [blk, blk],
                             pl.BlockSpec((1, th, cols), lambda h, j, core_ref: (0, h * nb + j, 0)),
                             jax.ShapeDtypeStruct((1, 2 * hr, cols), mine.dtype), [mine, other])


def _gather_small(sp):
    def body(s_ref, out_ref, send_sems, recv_sems, local_sem):
        x, y, c = _place()
        me = 4 * x + 2 * y + c
        peers = [(x ^ (m >> 2), y ^ ((m >> 1) & 1), c ^ (m & 1)) for m in range(1, 8)]
        mine = pltpu.make_async_copy(s_ref, out_ref.at[me], local_sem)
        mine.start()
        sends = [pltpu.make_async_remote_copy(src_ref=s_ref, dst_ref=out_ref.at[me], send_sem=send_sems.at[j],
                                              recv_sem=recv_sems.at[j], device_id=p, device_id_type=MESH)
                 for j, p in enumerate(peers)]
        for cp in sends:
            cp.start()
        for j, (px, py, pc) in enumerate(peers):
            pltpu.make_async_remote_copy(src_ref=s_ref, dst_ref=out_ref.at[4 * px + 2 * py + pc],
                                         send_sem=send_sems.at[j], recv_sem=recv_sems.at[j], device_id=(px, py, pc),
                                         device_id_type=MESH).wait_recv()
        for cp in sends:
            cp.wait_send()
        mine.wait()

    return pl.pallas_call(
        body, name="gather_small_grads", in_specs=[HBM_SPEC], out_specs=HBM_SPEC,
        out_shape=jax.ShapeDtypeStruct((8,) + sp.shape, sp.dtype),
        scratch_shapes=[pltpu.SemaphoreType.DMA((7,)), pltpu.SemaphoreType.DMA((7,)), pltpu.SemaphoreType.DMA])(sp)


def _pack_rows(total):
    rows = -(-total // LANES)
    return -(-rows // 32) * 32


def _pack(arrs, dtype):
    flat = jnp.concatenate([a.reshape(-1).astype(dtype) for a in arrs])
    rows = _pack_rows(flat.shape[0])
    return jnp.pad(flat, (0, rows * LANES - flat.shape[0])).reshape(rows, LANES)


def _unpack(buf, shapes):
    flat = buf.reshape(-1)
    out, off = {}, 0
    for n, shp in shapes:
        size = shp[0] * shp[1]
        out[n] = flat[off:off + size].reshape(shp)
        off += size
    return out


def _to_wire(name, w3):
    _, r, cols = w3.shape
    tb = _row_tile(r, 512)

    def body(w_ref, o_ref):
        o_ref[...] = w_ref[0].astype(o_ref.dtype)

    return pl.pallas_call(
        body, name=name, grid=(r // tb,), in_specs=[pl.BlockSpec((1, tb, cols), lambda i: (0, i, 0))],
        out_specs=pl.BlockSpec((tb, cols), lambda i: (i, 0)), out_shape=jax.ShapeDtypeStruct((r, cols), MM_DTYPE),
        compiler_params=_params(("arbitrary",)))(w3)


def _adamw(name, w3, g, m3, v3, tb):
    c1 = 1.0 - ADAM_B1 ** ADAM_STEP
    c2 = 1.0 - ADAM_B2 ** ADAM_STEP
    _, r, cols = w3.shape
    emit = g.ndim == 2
    blk3 = pl.BlockSpec((1, tb, cols), lambda i: (0, i, 0))
    g_spec = pl.BlockSpec((tb, cols), lambda i: (i, 0)) if emit else blk3

    def body(w_ref, g_ref, m_ref, v_ref, *out_refs):
        gb = g_ref[...] if emit else g_ref[0]
        m2 = ADAM_B1 * m_ref[0] + (1.0 - ADAM_B1) * gb
        v2 = ADAM_B2 * v_ref[0] + (1.0 - ADAM_B2) * (gb * gb)
        out_refs[-3][0] = -ADAM_LR * ((m2 / c1) / (jnp.sqrt(v2 / c2) + ADAM_EPS) + ADAM_WD * w_ref[0])
        out_refs[-2][0] = m2
        out_refs[-1][0] = v2
        if emit:
            out_refs[0][0] = gb

    n_out = 4 if emit else 3
    outs = pl.pallas_call(
        body, name=name, grid=(r // tb,), in_specs=[blk3, g_spec, blk3, blk3], out_specs=[blk3] * n_out,
        out_shape=[jax.ShapeDtypeStruct((1, r, cols), F32)] * n_out,
        compiler_params=_params(("arbitrary",)))(w3, g, m3, v3)
    return outs if emit else [g] + list(outs)


def _row_tile(rows, pref):
    if rows <= pref:
        return rows
    t = pref
    while t >= 8:
        if rows % t == 0 and t % 8 == 0:
            return t
        t -= 8
    return rows


def kernel(x, positions, ffn1_pre_g, ffn1_w_gate, ffn1_w_up, ffn1_w_down, ffn1_post_g, mix_pre_g, w_in, mla_q_norm_g, mla_w_uq, mla_kv_norm_g, mla_w_ukv, mla_out_g, gdn_conv_w, gdn_a_log, gdn_dt_bias, gdn_norm_g, w_out, mix_post_g, ffn2_pre_g, ffn2_w_gate, ffn2_w_up, ffn2_w_down, ffn2_post_g, loss_target, m_ffn1_pre_g, m_ffn1_w_gate, m_ffn1_w_up, m_ffn1_w_down, m_ffn1_post_g, m_mix_pre_g, m_w_in, m_mla_q_norm_g, m_mla_w_uq, m_mla_kv_norm_g, m_mla_w_ukv, m_mla_out_g, m_gdn_conv_w, m_gdn_a_log, m_gdn_dt_bias, m_gdn_norm_g, m_w_out, m_mix_post_g, m_ffn2_pre_g, m_ffn2_w_gate, m_ffn2_w_up, m_ffn2_w_down, m_ffn2_post_g, v_ffn1_pre_g, v_ffn1_w_gate, v_ffn1_w_up, v_ffn1_w_down, v_ffn1_post_g, v_mix_pre_g, v_w_in, v_mla_q_norm_g, v_mla_w_uq, v_mla_kv_norm_g, v_mla_w_ukv, v_mla_out_g, v_gdn_conv_w, v_gdn_a_log, v_gdn_dt_bias, v_gdn_norm_g, v_w_out, v_mix_post_g, v_ffn2_pre_g, v_ffn2_w_gate, v_ffn2_w_up, v_ffn2_w_down, v_ffn2_post_g):
    args = dict(locals())
    wsh = {n: args[n][0] for n in WEIGHTS}
    msh = {n: args["m_" + n] for n in SMALL}
    vsh = {n: args["v_" + n] for n in SMALL}
    for n in SMALL:
        wsh[n] = args[n]
    mix_shapes = [(n, wsh[n].shape) for n in MIX_BIG]

    early = FFN_BIG[:3]
    held = lambda a, n: jnp.swapaxes(a, 1, 2) if n in TRANSPOSED else a
    w_in_wire = jnp.pad(held(w_in, "w_in")[0].astype(MM_DTYPE), ((0, W_IN_SHARD_PAD - W_IN_SHARD), (0, 0)))
    gathered = _gather_shards([_to_wire("wire_" + n, held(args[n], n)) for n in early]
                              + [w_in_wire, _pack([wsh[n] for n in MIX_BIG], MM_DTYPE)])
    full = {n: wsh[n] for n in SMALL}
    for n, gw in zip(early + ["w_in"], gathered):
        full[n] = gw
    parts = [_unpack(gathered[-1][q], mix_shapes) for q in range(N_SHARD)]
    for n in MIX_BIG:
        full[n] = jnp.concatenate([parts[q][n] for q in range(N_SHARD)], axis=SHARD_AXIS[n])

    core = lax.axis_index("c").astype(jnp.int32).reshape(1)
    chip = (2 * lax.axis_index("x") + lax.axis_index("y")).astype(jnp.int32).reshape(1)
    late = ([_to_wire("wire_" + n, held(args[n], n)) for n in FFN2_BIG], core, chip)
    lsum, grad_x, g = _local_step(x[0], positions, loss_target[0], full, late)
    loss = lax.psum(0.5 * jnp.sum(lsum) / x.shape[-1], ("x", "y", "c"))

    halves = [g[n] for n in FFN_BIG] + [g["w_in"], g["mix_pack"]]
    others = _share_halves(halves)
    shared = [_join_halves("join_halves_%d" % i, hm, ho, core) for i, (hm, ho) in enumerate(zip(halves, others))]
    gsh = _unpack(shared[-1], mix_shapes)
    for n, sg_ in zip(FFN_BIG, shared):
        gsh[n] = sg_
    gsh["w_in"] = shared[-2][:, :W_IN_SHARD]

    small_shapes = [(n, wsh[n].shape) for n in SMALL]
    pack_small = lambda d: jnp.concatenate(
        [_pad_lanes(d[n].astype(F32), LANES) for n in SMALL] + [jnp.zeros((SMALL_ROWS - len(SMALL), LANES), F32)], axis=0)
    slots = _gather_small(pack_small(g))

    c1 = 1.0 - ADAM_B1 ** ADAM_STEP
    c2 = 1.0 - ADAM_B2 ** ADAM_STEP

    def small_update(wb, mb, vb, s8):
        gs = s8[0:SMALL_ROWS]
        for d in range(1, 8):
            gs = gs + s8[d * SMALL_ROWS:(d + 1) * SMALL_ROWS]
        m2 = ADAM_B1 * mb + (1.0 - ADAM_B1) * gs
        v2 = ADAM_B2 * vb + (1.0 - ADAM_B2) * (gs * gs)
        delta = -ADAM_LR * ((m2 / c1) / (jnp.sqrt(v2 / c2) + ADAM_EPS) + ADAM_WD * wb)
        return gs, delta, m2, v2

    sg, sd, sm, sv_ = _rowwise("adamw_small", small_update,
                               [pack_small(wsh), pack_small(msh), pack_small(vsh)],
                               [slots.reshape(8 * SMALL_ROWS, LANES)], [(LANES, F32)] * 4, [], SMALL_ROWS)
    grads, deltas, new_m, new_v = {}, {}, {}, {}
    for i, (n, shp) in enumerate(small_shapes):
        grads[n], deltas[n] = sg[i:i + 1, :shp[1]], sd[i:i + 1, :shp[1]]
        new_m[n], new_v[n] = sm[i:i + 1, :shp[1]], sv_[i:i + 1, :shp[1]]
    for n in BIG:
        w3 = held(args[n], n)
        outs = _adamw("adamw_" + n, w3, gsh[n], held(args["m_" + n], n), held(args["v_" + n], n),
                      _row_tile(w3.shape[1], 256))
        grads[n], deltas[n], new_m[n], new_v[n] = [held(o, n) for o in outs]

    return (loss, grad_x[None], *[grads[n] for n in WEIGHTS], *[deltas[n] for n in WEIGHTS],
            *[new_m[n] for n in WEIGHTS], *[new_v[n] for n in WEIGHTS])
```

```python
import functools

import jax
import jax.numpy as jnp
from jax import lax
from jax.experimental import pallas as pl
from jax.experimental.pallas import tpu as pltpu

F32 = jnp.float32
BF16 = jnp.bfloat16
MM_DTYPE = BF16
MESH = pl.DeviceIdType.MESH

D_MODEL = 1024
D_FF = 2816
N_HEADS = 8
MLA_Q_RANK = 256
MLA_KV_RANK = 128
MLA_NOPE = 64
MLA_ROPE = 32
MLA_V = 64
ROPE_THETA = 10000.0
GDN_DH = 64
GDN_W = N_HEADS * GDN_DH
GDN_CONV = 4
CHUNK = 64
HEAD_LANES = 128
HEADS_PER_STEP = 4
MLA_PAD = N_HEADS * HEAD_LANES
EPS = 1e-6
N_SHARD = 4
LANES = 1024

PIN_QKV = 0
PIN_MLA = 1536
PIN_KPE = 1920
PIN_GATE = 2048
PIN_AB = 2560
PIN_W = 2688
CAT_W = MLA_PAD + GDN_W

ADAM_LR = 0.001
ADAM_B1 = 0.9
ADAM_B2 = 0.999
ADAM_EPS = 1e-08
ADAM_WD = 0.01
ADAM_STEP = 10

VMEM_LIMIT_V7X = 56 * 1024 * 1024

BIG = ["ffn1_w_gate", "ffn1_w_up", "ffn1_w_down", "w_in", "mla_w_uq", "mla_w_ukv", "gdn_conv_w", "w_out",
       "ffn2_w_gate", "ffn2_w_up", "ffn2_w_down"]
FFN_BIG = ["ffn1_w_gate", "ffn1_w_up", "ffn1_w_down", "ffn2_w_gate", "ffn2_w_up", "ffn2_w_down"]
TRANSPOSED = ["ffn1_w_gate", "ffn1_w_up", "ffn2_w_gate", "ffn2_w_up", "w_in"]
MIX_BIG = ["mla_w_uq", "mla_w_ukv", "gdn_conv_w", "w_out"]
SMALL = ["ffn1_pre_g", "ffn1_post_g", "mix_pre_g", "mla_q_norm_g", "mla_kv_norm_g", "mla_out_g", "gdn_a_log",
         "gdn_dt_bias", "gdn_norm_g", "mix_post_g", "ffn2_pre_g", "ffn2_post_g"]
WEIGHTS = ["ffn1_pre_g", "ffn1_w_gate", "ffn1_w_up", "ffn1_w_down", "ffn1_post_g", "mix_pre_g", "w_in",
           "mla_q_norm_g", "mla_w_uq", "mla_kv_norm_g", "mla_w_ukv", "mla_out_g", "gdn_conv_w", "gdn_a_log",
           "gdn_dt_bias", "gdn_norm_g", "w_out", "mix_post_g", "ffn2_pre_g", "ffn2_w_gate", "ffn2_w_up",
           "ffn2_w_down", "ffn2_post_g"]
SHARD_AXIS = {"ffn1_w_gate": 1, "ffn1_w_up": 1, "ffn1_w_down": 0, "w_in": 1, "mla_w_uq": 1, "mla_w_ukv": 1,
              "gdn_conv_w": 1, "w_out": 0, "ffn2_w_gate": 1, "ffn2_w_up": 1, "ffn2_w_down": 0}
SMALL_ROWS = 16


def _params(sem):
    return pltpu.CompilerParams(dimension_semantics=sem, vmem_limit_bytes=VMEM_LIMIT_V7X)


def _pick(dim, pref):
    if dim <= pref:
        return dim
    t = (pref // 128) * 128
    while t >= 128:
        if dim % t == 0:
            return t
        t -= 128
    return dim


ANY_SPEC = pl.BlockSpec(memory_space=pl.ANY)


def _rowwise(name, fn, row_ins, bc_ins, row_outs, acc_outs, tb, wide=None, carry=None):
    ents = []
    for e in row_ins:
        ents.append(e if isinstance(e, tuple) else (e, e.shape[1], 0, 0))
    over = [o[2] for o in row_outs if len(o) == 3]
    rows = over[0] if over else ents[0][0].shape[0]
    steps = rows // tb
    assert steps * tb == rows, (name, rows, tb)
    in_specs, args = [], []
    for a, w, j, r0 in ents:
        in_specs.append(pl.BlockSpec((tb, w), lambda i, j=j, r0=r0: (i + r0, j)))
        args.append(a)
    for b in bc_ins:
        in_specs.append(pl.BlockSpec(b.shape, lambda i: (0, 0)))
        args.append(b)
    n_in = len(args)
    aliases = {}
    if carry is not None:
        in_specs.append(ANY_SPEC)
        args.append(carry)
        aliases = {n_in: 0}
    out_shape = [jax.ShapeDtypeStruct((rows, o[0]), o[1]) for o in row_outs]
    out_specs = [pl.BlockSpec((tb, o[0]), lambda i: (i, 0)) for o in row_outs]
    if wide is not None:
        out_shape[0] = jax.ShapeDtypeStruct((rows, wide[0]), row_outs[0][1])
        out_specs[0] = pl.BlockSpec((tb, row_outs[0][0]), lambda i: (i, wide[1]))
    out_shape += [jax.ShapeDtypeStruct((r, c), F32) for r, c in acc_outs]
    out_specs += [pl.BlockSpec((r, c), lambda i: (0, 0)) for r, c in acc_outs]
    n_ro, n_acc, n_args = len(row_outs), len(acc_outs), len(args)

    def body(*refs):
        vals = fn(*[r[...] for r in refs[:n_in]])
        if not isinstance(vals, (tuple, list)):
            vals = (vals,)
        for r, v in zip(refs[n_args:n_args + n_ro], vals[:n_ro]):
            r[...] = v.astype(r.dtype)
        if n_acc:
            acc_refs = refs[n_args + n_ro:]

            @pl.when(pl.program_id(0) == 0)
            def _():
                for r in acc_refs:
                    r[...] = jnp.zeros(r.shape, r.dtype)

            for r, v in zip(acc_refs, vals[n_ro:]):
                r[...] += v

    outs = pl.pallas_call(body, name=name, grid=(steps,), in_specs=in_specs, out_specs=out_specs,
                          out_shape=out_shape, input_output_aliases=aliases,
                          compiler_params=_params(("arbitrary",)))(*args)
    return outs


def _mm(name, a, b, mode, out_dtype, tm=1024, tn=1024, tk=1024):
    if mode == "nn":
        (m, k), (k2, n) = a.shape, b.shape
    elif mode == "nt":
        (m, k), (n, k2) = a.shape, b.shape
    else:
        (k, m), (k2, n) = a.shape, b.shape
    assert k == k2, (name, a.shape, b.shape)
    tm, tn, tk = _pick(m, tm), _pick(n, tn), _pick(k, tk)
    nk = k // tk
    if mode == "nn":
        a_spec = pl.BlockSpec((tm, tk), lambda i, j, kk: (i, kk))
        b_spec = pl.BlockSpec((tk, tn), lambda i, j, kk: (kk, j))
        dims = (((1,), (0,)), ((), ()))
    elif mode == "nt":
        a_spec = pl.BlockSpec((tm, tk), lambda i, j, kk: (i, kk))
        b_spec = pl.BlockSpec((tn, tk), lambda i, j, kk: (j, kk))
        dims = (((1,), (1,)), ((), ()))
    else:
        a_spec = pl.BlockSpec((tk, tm), lambda i, j, kk: (kk, i))
        b_spec = pl.BlockSpec((tk, tn), lambda i, j, kk: (kk, j))
        dims = (((0,), (0,)), ((), ()))

    def body(a_ref, b_ref, o_ref, acc_ref):
        kk = pl.program_id(2)

        @pl.when(kk == 0)
        def _():
            acc_ref[...] = jnp.zeros(acc_ref.shape, F32)

        acc_ref[...] += lax.dot_general(a_ref[...].astype(MM_DTYPE), b_ref[...].astype(MM_DTYPE), dims,
                                        preferred_element_type=F32)

        @pl.when(kk == nk - 1)
        def _():
            o_ref[...] = acc_ref[...].astype(o_ref.dtype)

    return pl.pallas_call(
        body, name=name, grid=(m // tm, n // tn, nk), in_specs=[a_spec, b_spec],
        out_specs=pl.BlockSpec((tm, tn), lambda i, j, kk: (i, j)),
        out_shape=jax.ShapeDtypeStruct((m, n), out_dtype),
        scratch_shapes=[pltpu.VMEM((tm, tn), F32)],
        compiler_params=_params(("parallel", "parallel", "arbitrary")))(a, b)


def _rms_stats(x, n_real=None):
    n = x.shape[-1] if n_real is None else n_real
    return lax.rsqrt(jnp.sum(x * x, axis=-1, keepdims=True) / n + EPS)


def _rms_bwd(x, r, g, dz, n_real=None):
    n = x.shape[-1] if n_real is None else n_real
    xh = x * r
    dxh = dz * g
    dx = r * (dxh - xh * (jnp.sum(dxh * xh, axis=-1, keepdims=True) / n))
    return dx, jnp.sum(dz * xh, axis=0, keepdims=True)


def _sigmoid(x):
    return 0.5 * jnp.tanh(0.5 * x) + 0.5


def _roll(x, s, axis):
    return pltpu.roll(x, s, axis)


def _rope(x, c, s1, s2):
    return x * c + _roll(x, HEAD_LANES - MLA_ROPE // 2, 1) * s1 + _roll(x, MLA_ROPE // 2, 1) * s2


def _heads_apply(x, fn):
    return jnp.concatenate([fn(x[:, h * HEAD_LANES:(h + 1) * HEAD_LANES]) for h in range(N_HEADS)], axis=1)


ROW_CHUNK = 256


def _row_chunks(rows):
    step = min(ROW_CHUNK, rows)
    return [pl.ds(r, step) for r in range(0, rows, step)]


def _ffn_fwd(tag, x, g_pre, wg, wu, wd, g_post, tm):
    t, d = x.shape
    ns, fs, _ = wg.shape
    nt = t // tm
    row = pl.BlockSpec((tm, d), lambda i, q: (i, 0))
    vec = pl.BlockSpec((1, d), lambda i, q: (0, 0))
    act3 = pl.BlockSpec((1, tm, fs), lambda i, q: (q, i, 0))
    wrow = pl.BlockSpec((1, fs, d), lambda i, q: (q, 0, 0))
    nt_dims = (((1,), (1,)), ((), ()))

    def gate_up(x_ref, g_ref, wg_ref, wu_ref, n_ref, sl_ref, ud_ref, s_ref, n_s):
        @pl.when(pl.program_id(1) == 0)
        def _():
            for r in _row_chunks(tm):
                xb = x_ref[r, :]
                n_s[r, :] = (xb * _rms_stats(xb) * g_ref[...]).astype(MM_DTYPE)
            n_ref[...] = n_s[...]

        for r in _row_chunks(tm):
            n = n_s[r, :]
            a = lax.dot_general(n, wg_ref[0], nt_dims, preferred_element_type=F32)
            u = lax.dot_general(n, wu_ref[0], nt_dims, preferred_element_type=F32)
            sg = _sigmoid(a)
            sl = a * sg
            sl_ref[0, r, :] = sl.astype(sl_ref.dtype)
            ud_ref[0, r, :] = (u * (sg + sl * (1.0 - sg))).astype(ud_ref.dtype)
            s_ref[0, r, :] = (sl * u).astype(s_ref.dtype)

    n, sl, ud, s = pl.pallas_call(
        gate_up, name=tag + "_gate_up", grid=(nt, ns), in_specs=[row, vec, wrow, wrow],
        out_specs=[row, act3, act3, act3],
        out_shape=[jax.ShapeDtypeStruct((t, d), MM_DTYPE)] + [jax.ShapeDtypeStruct((ns, t, fs), MM_DTYPE)] * 3,
        scratch_shapes=[pltpu.VMEM((tm, d), MM_DTYPE)],
        compiler_params=_params(("parallel", "arbitrary")))(x, g_pre, wg, wu)

    def down(s_ref, wd_ref, x_ref, g_ref, h_ref, y_ref, acc):
        q = pl.program_id(1)

        @pl.when(q == 0)
        def _():
            acc[...] = jnp.zeros(acc.shape, F32)

        for r in _row_chunks(tm):
            acc[r, :] += jnp.dot(s_ref[0, r, :], wd_ref[0], preferred_element_type=F32)

        @pl.when(q == ns - 1)
        def _():
            for r in _row_chunks(tm):
                hb = acc[r, :]
                h_ref[r, :] = hb
                y_ref[r, :] = x_ref[r, :] + 0.5 * (hb * _rms_stats(hb) * g_ref[...])

    h, y = pl.pallas_call(
        down, name=tag + "_down", grid=(nt, ns), in_specs=[act3, wrow, row, vec], out_specs=[row, row],
        out_shape=[jax.ShapeDtypeStruct((t, d), F32)] * 2, scratch_shapes=[pltpu.VMEM((tm, d), F32)],
        compiler_params=_params(("parallel", "arbitrary")))(s, wd, x, g_post)
    return y, (x, n, sl, ud, s, h)


def _carry(body, n_in, n_out, grid, carried):
    if carried is None:
        return body, [], [], [], [], []
    nx_in, nx_out = len(carried.ins), len(carried.outs)

    def wrapped(*refs):
        ins, rest = refs[:n_in], refs[n_in:]
        xi, rest = rest[:nx_in], rest[nx_in:]
        outs, rest = rest[:n_out], rest[n_out:]
        xo, rest = rest[:nx_out], rest[nx_out:]
        scr, sems = rest[:len(rest) - 2], rest[len(rest) - 2:]
        first, last = True, True
        for dim, size in enumerate(grid):
            first = first & (pl.program_id(dim) == 0)
            last = last & (pl.program_id(dim) == size - 1)

        @pl.when(first)
        def _():
            carried.start(xi, xo, *sems)

        body(*ins, *outs, *scr)

        @pl.when(last)
        def _():
            carried.finish(xi, xo, *sems)

    sems = [pltpu.SemaphoreType.DMA((carried.n_sem,)), pltpu.SemaphoreType.DMA((carried.n_sem,))]
    return (wrapped, [HBM_SPEC] * nx_in, [HBM_SPEC] * nx_out, list(carried.outs), sems, list(carried.ins))


def _ffn_bwd(tag, dy, saved, g_pre, wg, wu, wd, g_post, tm, tk, carried_down=None, make_carried_up=None):
    x, n, sl, ud, s, h = saved
    t, d = x.shape
    ns, fs, _ = wg.shape
    nt, nk = t // tm, t // tk
    row = pl.BlockSpec((tm, d), lambda i, q: (i, 0))
    vec = pl.BlockSpec((1, d), lambda i, q: (0, 0))
    act3 = pl.BlockSpec((1, tm, fs), lambda i, q: (q, i, 0))
    wrow = pl.BlockSpec((1, fs, d), lambda i, q: (q, 0, 0))
    nt_dims = (((1,), (1,)), ((), ()))
    tn_dims = (((0,), (0,)), ((), ()))

    def down_b(h_ref, dy_ref, g_ref, wd_ref, sl_ref, ud_ref, dh_ref, da_ref, du_ref, dg_ref, dh_s):
        i, q = pl.program_id(0), pl.program_id(1)

        @pl.when((i == 0) & (q == 0))
        def _():
            dg_ref[...] = jnp.zeros(dg_ref.shape, F32)

        @pl.when(q == 0)
        def _():
            for r in _row_chunks(tm):
                hb = h_ref[r, :]
                dh, dg = _rms_bwd(hb, _rms_stats(hb), g_ref[...], 0.5 * dy_ref[r, :])
                dh_s[r, :] = dh.astype(MM_DTYPE)
                dg_ref[...] += dg
            dh_ref[...] = dh_s[...]

        for r in _row_chunks(tm):
            ds = lax.dot_general(dh_s[r, :], wd_ref[0], nt_dims, preferred_element_type=F32)
            da_ref[0, r, :] = (ds * ud_ref[0, r, :].astype(F32)).astype(da_ref.dtype)
            du_ref[0, r, :] = (ds * sl_ref[0, r, :].astype(F32)).astype(du_ref.dtype)

    down_b, x_in, x_out, x_shape, x_scr, x_args = _carry(down_b, 6, 4, (nt, ns), carried_down)
    dh, da, du, dg_post, *from_down = pl.pallas_call(
        down_b, name=tag + "_down_b", grid=(nt, ns), in_specs=[row, row, vec, wrow, act3, act3] + x_in,
        out_specs=[row, act3, act3, vec] + x_out,
        out_shape=[jax.ShapeDtypeStruct((t, d), MM_DTYPE)] + [jax.ShapeDtypeStruct((ns, t, fs), MM_DTYPE)] * 2
        + [jax.ShapeDtypeStruct((1, d), F32)] + x_shape,
        scratch_shapes=[pltpu.VMEM((tm, d), MM_DTYPE)] + x_scr,
        compiler_params=_params(("arbitrary", "arbitrary")))(h, dy, g_post, wd, sl, ud, *x_args)

    def down_w(s_ref, dh_ref, dw_ref, acc):
        kk = pl.program_id(1)

        @pl.when(kk == 0)
        def _():
            acc[...] = jnp.zeros(acc.shape, F32)

        acc[...] += lax.dot_general(s_ref[0], dh_ref[...], tn_dims, preferred_element_type=F32)

        @pl.when(kk == nk - 1)
        def _():
            dw_ref[0] = acc[...].astype(dw_ref.dtype)

    dwd = pl.pallas_call(
        down_w, name=tag + "_down_w", grid=(ns, nk),
        in_specs=[pl.BlockSpec((1, tk, fs), lambda q, kk: (q, kk, 0)), pl.BlockSpec((tk, d), lambda q, kk: (kk, 0))],
        out_specs=pl.BlockSpec((1, fs, d), lambda q, kk: (q, 0, 0)),
        out_shape=jax.ShapeDtypeStruct((ns, fs, d), MM_DTYPE), scratch_shapes=[pltpu.VMEM((fs, d), F32)],
        compiler_params=_params(("parallel", "arbitrary")))(s, dh)

    def gate_up_b(da_ref, du_ref, wg_ref, wu_ref, x_ref, dy_ref, g_ref, dx_ref, dg_ref, acc):
        i, q = pl.program_id(0), pl.program_id(1)

        @pl.when((i == 0) & (q == 0))
        def _():
            dg_ref[...] = jnp.zeros(dg_ref.shape, F32)

        @pl.when(q == 0)
        def _():
            acc[...] = jnp.zeros(acc.shape, F32)

        for r in _row_chunks(tm):
            acc[r, :] += (jnp.dot(da_ref[0, r, :], wg_ref[0], preferred_element_type=F32)
                          + jnp.dot(du_ref[0, r, :], wu_ref[0], preferred_element_type=F32))

        @pl.when(q == ns - 1)
        def _():
            for r in _row_chunks(tm):
                xb = x_ref[r, :]
                dx, dg = _rms_bwd(xb, _rms_stats(xb), g_ref[...], acc[r, :])
                dx_ref[r, :] = dy_ref[r, :] + dx
                dg_ref[...] += dg

    def gate_up_w(n_ref, da_ref, du_ref, dwg_ref, dwu_ref, acc_g, acc_u):
        kk = pl.program_id(1)

        @pl.when(kk == 0)
        def _():
            acc_g[...] = jnp.zeros(acc_g.shape, F32)
            acc_u[...] = jnp.zeros(acc_u.shape, F32)

        nb = n_ref[...]
        acc_g[...] += lax.dot_general(da_ref[0], nb, tn_dims, preferred_element_type=F32)
        acc_u[...] += lax.dot_general(du_ref[0], nb, tn_dims, preferred_element_type=F32)

        @pl.when(kk == nk - 1)
        def _():
            dwg_ref[0] = acc_g[...].astype(dwg_ref.dtype)
            dwu_ref[0] = acc_u[...].astype(dwu_ref.dtype)

    k3 = pl.BlockSpec((1, tk, fs), lambda q, kk: (q, kk, 0))
    wout = pl.BlockSpec((1, fs, d), lambda q, kk: (q, 0, 0))
    dwg, dwu = pl.pallas_call(
        gate_up_w, name=tag + "_gate_up_w", grid=(ns, nk),
        in_specs=[pl.BlockSpec((tk, d), lambda q, kk: (kk, 0)), k3, k3], out_specs=[wout, wout],
        out_shape=[jax.ShapeDtypeStruct((ns, fs, d), MM_DTYPE)] * 2,
        scratch_shapes=[pltpu.VMEM((fs, d), F32)] * 2,
        compiler_params=_params(("parallel", "arbitrary")))(n, da, du)

    carried_up = make_carried_up(dwg, dwu, dwd) if make_carried_up else None
    gate_up_b, x_in, x_out, x_shape, x_scr, x_args = _carry(gate_up_b, 7, 2, (nt, ns), carried_up)
    dx, dg_pre, *from_up = pl.pallas_call(
        gate_up_b, name=tag + "_gate_up_b", grid=(nt, ns), in_specs=[act3, act3, wrow, wrow, row, row, vec] + x_in,
        out_specs=[row, vec] + x_out,
        out_shape=[jax.ShapeDtypeStruct((t, d), F32), jax.ShapeDtypeStruct((1, d), F32)] + x_shape,
        scratch_shapes=[pltpu.VMEM((tm, d), F32)] + x_scr,
        compiler_params=_params(("arbitrary", "arbitrary")))(da, du, wg, wu, x, dy, g_pre, *x_args)
    return dx, dg_pre, dwg, dwu, dwd, dg_post, from_down, from_up


NEG = -1e30


def _attn_scale():
    return (MLA_NOPE + MLA_ROPE) ** -0.5


def _causal_pairs(nq, by_key):
    if by_key:
        pairs = [(qi, ki) for ki in range(nq) for qi in range(ki, nq)]
    else:
        pairs = [(qi, ki) for qi in range(nq) for ki in range(qi + 1)]
    return jnp.asarray([p[0] for p in pairs], jnp.int32), jnp.asarray([p[1] for p in pairs], jnp.int32)


def _below_diagonal(shape):
    return lax.broadcasted_iota(jnp.int32, shape, 1) <= lax.broadcasted_iota(jnp.int32, shape, 0)


def _attn_call(name, body, tables, args, in_kinds, out_kinds, scratch, t, tq, carried=None):
    qmap = lambda h, p, qt, kt: (qt[p], h)
    kmap = lambda h, p, qt, kt: (kt[p], h)
    width = HEADS_PER_STEP * HEAD_LANES
    spec = lambda kind: pl.BlockSpec((tq, width), qmap if kind == "q" else kmap)
    n_pairs = tables[0].shape[0]
    n_groups = N_HEADS // HEADS_PER_STEP
    n_in, n_out, n_scr = len(in_kinds), len(out_kinds), scratch
    x_ins = list(carried.ins) if carried else []
    x_outs = list(carried.outs) if carried else []
    x_scr = [pltpu.SemaphoreType.DMA((carried.n_sem,)), pltpu.SemaphoreType.DMA((carried.n_sem,))] if carried else []

    def full_body(qt, kt, *refs):
        ins, refs = refs[:n_in], refs[n_in:]
        xi, refs = refs[:len(x_ins)], refs[len(x_ins):]
        outs, refs = refs[:n_out], refs[n_out:]
        xo, refs = refs[:len(x_outs)], refs[len(x_outs):]
        scr, sems = refs[:n_scr], refs[n_scr:]
        if carried:
            @pl.when((pl.program_id(0) == 0) & (pl.program_id(1) == 0))
            def _():
                carried.start(xi, xo, *sems)

        heads = [tuple(r.at[:, pl.ds(hh * HEAD_LANES, HEAD_LANES)] for r in (*ins, *outs, *scr))
                 for hh in range(HEADS_PER_STEP)]
        body(qt, kt, heads)
        if carried:
            @pl.when((pl.program_id(0) == n_groups - 1) & (pl.program_id(1) == n_pairs - 1))
            def _():
                carried.finish(xi, xo, *sems)

    grid_spec = pltpu.PrefetchScalarGridSpec(
        num_scalar_prefetch=2, grid=(n_groups, n_pairs),
        in_specs=[spec(kd) for kd in in_kinds] + [HBM_SPEC] * len(x_ins),
        out_specs=[spec(kd) for kd in out_kinds] + [HBM_SPEC] * len(x_outs),
        scratch_shapes=[pltpu.VMEM((tq, width), F32)] * n_scr + x_scr)
    return pl.pallas_call(full_body, name=name, grid_spec=grid_spec,
                          out_shape=[jax.ShapeDtypeStruct((t, MLA_PAD), F32) for _ in out_kinds] + x_outs,
                          compiler_params=_params(("arbitrary", "arbitrary")))(*tables, *args, *x_ins)


class _Carried:
    def __init__(self, ins, outs, n_sem, start, finish):
        self.ins, self.outs, self.n_sem, self.start, self.finish = ins, outs, n_sem, start, finish


def _attn_fwd(q, k, v, tq, carried=None):
    t = q.shape[0]
    nq = t // tq

    def body(qt, kt, heads):
        p_id = pl.program_id(1)
        qi, ki = qt[p_id], kt[p_id]

        @pl.when(ki == 0)
        def _():
            for _, _, _, _, _, m_s, l_s, acc_s in heads:
                m_s[...] = jnp.full(m_s.shape, NEG, F32)
                l_s[...] = jnp.zeros(l_s.shape, F32)
                acc_s[...] = jnp.zeros(acc_s.shape, F32)

        def update(diagonal):
            for q_ref, k_ref, v_ref, _, _, m_s, l_s, acc_s in heads:
                s = lax.dot_general(q_ref[...], k_ref[...], (((1,), (1,)), ((), ())), preferred_element_type=F32)
                if diagonal:
                    s = jnp.where(_below_diagonal(s.shape), s, NEG)
                m_old = m_s[...]
                m_new = jnp.maximum(m_old, jnp.max(s, axis=1, keepdims=True))
                alpha = jnp.exp(m_old - m_new)
                p = jnp.exp(s - m_new[:, :1])
                l_s[...] = l_s[...] * alpha + jnp.sum(p, axis=1, keepdims=True)
                acc_s[...] = acc_s[...] * alpha + jnp.dot(p.astype(MM_DTYPE), v_ref[...], preferred_element_type=F32)
                m_s[...] = m_new

        @pl.when(ki < qi)
        def _():
            update(False)

        @pl.when(ki == qi)
        def _():
            update(True)
            for _, _, _, o_ref, lse_ref, m_s, l_s, acc_s in heads:
                o_ref[...] = acc_s[...] / l_s[...]
                lse_ref[...] = m_s[...] + jnp.log(l_s[...])

    return _attn_call("mla_attn_fwd", body, _causal_pairs(nq, False), (q, k, v), "qkk", "qq", 3, t, tq, carried)


def _attn_probs(q, k, lse, diagonal):
    s = lax.dot_general(q, k, (((1,), (1,)), ((), ())), preferred_element_type=F32)
    p = jnp.exp(s - lse[:, :1])
    return jnp.where(_below_diagonal(s.shape), p, 0.0) if diagonal else p


BWD_HEADS = 2


def _attn_bwd(q, k, v, do, lse, delta, tq, carried=None):
    t = q.shape[0]
    nq = t // tq
    width = BWD_HEADS * HEAD_LANES
    n_groups = N_HEADS // BWD_HEADS
    qt_tab, kt_tab = _causal_pairs(nq, True)
    n_pairs = qt_tab.shape[0]
    qmap = lambda h, p, qt, kt: (qt[p], h)
    kmap = lambda h, p, qt, kt: (kt[p], h)
    qs, ks = pl.BlockSpec((tq, width), qmap), pl.BlockSpec((tq, width), kmap)
    x_ins = list(carried.ins) if carried else []
    x_outs = list(carried.outs) if carried else []
    x_scr = [pltpu.SemaphoreType.DMA((carried.n_sem,)), pltpu.SemaphoreType.DMA((carried.n_sem,))] if carried else []
    nt_dims = (((1,), (1,)), ((), ()))
    tn_dims = (((0,), (0,)), ((), ()))

    def body(qt, kt, q_ref, k_ref, v_ref, do_ref, lse_ref, dl_ref, *rest):
        xi, rest = rest[:len(x_ins)], rest[len(x_ins):]
        dq_hbm, dk_ref, dv_ref = rest[:3]
        xo, rest = rest[3:3 + len(x_outs)], rest[3 + len(x_outs):]
        dk_s, dv_s, dq_s, dq_sem = rest[:4]
        sems = rest[4:]
        grp, p_id = pl.program_id(0), pl.program_id(1)
        qi, ki = qt[p_id], kt[p_id]
        if carried:
            @pl.when((grp == 0) & (p_id == 0))
            def _():
                carried.start(xi, xo, *sems)

        @pl.when(p_id == 0)
        def _():
            dq_s[...] = jnp.zeros(dq_s.shape, F32)

        def step(diagonal):
            rows = pl.ds(pl.multiple_of(qi * tq, tq), tq)
            for hh in range(BWD_HEADS):
                ln = pl.ds(hh * HEAD_LANES, HEAD_LANES)
                qb, kb, vb, dob = q_ref[:, ln], k_ref[:, ln], v_ref[:, ln], do_ref[:, ln]
                p = _attn_probs(qb, kb, lse_ref[:, ln], diagonal)
                dv_s[:, ln] += lax.dot_general(p.astype(MM_DTYPE), dob, tn_dims, preferred_element_type=F32)
                dp = lax.dot_general(dob, vb, nt_dims, preferred_element_type=F32)
                ds = (p * (dp - dl_ref[:, ln][:, :1])).astype(MM_DTYPE)
                dk_s[:, ln] += lax.dot_general(ds, qb, tn_dims, preferred_element_type=F32)
                dq_s[rows, ln] += jnp.dot(ds, kb, preferred_element_type=F32)

        @pl.when(qi == ki)
        def _():
            dk_s[...] = jnp.zeros(dk_s.shape, F32)
            dv_s[...] = jnp.zeros(dv_s.shape, F32)
            step(True)

        @pl.when(qi > ki)
        def _():
            step(False)

        @pl.when(qi == nq - 1)
        def _():
            dk_ref[...] = dk_s[...]
            dv_ref[...] = dv_s[...]

        @pl.when(p_id == n_pairs - 1)
        def _():
            out = pltpu.make_async_copy(dq_s, dq_hbm.at[pl.ds(pl.multiple_of(grp * t, t), t)], dq_sem)
            out.start()
            out.wait()

        if carried:
            @pl.when((grp == n_groups - 1) & (p_id == n_pairs - 1))
            def _():
                carried.finish(xi, xo, *sems)

    grid_spec = pltpu.PrefetchScalarGridSpec(
        num_scalar_prefetch=2, grid=(n_groups, n_pairs),
        in_specs=[qs, ks, ks, qs, qs, qs] + [HBM_SPEC] * len(x_ins),
        out_specs=[HBM_SPEC, ks, ks] + [HBM_SPEC] * len(x_outs),
        scratch_shapes=[pltpu.VMEM((tq, width), F32), pltpu.VMEM((tq, width), F32), pltpu.VMEM((t, width), F32),
                        pltpu.SemaphoreType.DMA] + x_scr)
    return pl.pallas_call(
        body, name="mla_attn_bwd", grid_spec=grid_spec,
        out_shape=[jax.ShapeDtypeStruct((n_groups * t, width), F32), jax.ShapeDtypeStruct((t, MLA_PAD), F32),
                   jax.ShapeDtypeStruct((t, MLA_PAD), F32)] + x_outs,
        compiler_params=_params(("arbitrary", "arbitrary")))(qt_tab, kt_tab, q, k, v, do, lse, delta, *x_ins)


def _dot01(a, b, dims=(((1,), (0,)), ((), ())), ones="rhs"):
    val, sel = (a, b) if ones == "rhs" else (b, a)
    head = val.astype(BF16)
    tail = (val - head.astype(F32)).astype(BF16)
    sel = sel.astype(BF16)
    dot = lambda part: (lax.dot_general(part, sel, dims, preferred_element_type=F32) if ones == "rhs"
                        else lax.dot_general(sel, part, dims, preferred_element_type=F32))
    return dot(head) + dot(tail)


def _dot1(a, b, dims=(((1,), (0,)), ((), ()))):
    return lax.dot_general(a.astype(MM_DTYPE), b.astype(MM_DTYPE), dims, preferred_element_type=F32)


def _dot3(a, b, dims=(((1,), (0,)), ((), ()))):
    return lax.dot_general(a, b, dims, preferred_element_type=F32, precision=lax.Precision.HIGH)


NN3 = (((2,), (1,)), ((0,), (0,)))
NT3 = (((2,), (2,)), ((0,), (0,)))
TN3 = (((1,), (1,)), ((0,), (0,)))


def _tri_masks(nh):
    shape = (nh, CHUNK, CHUNK)
    return lax.broadcasted_iota(jnp.int32, shape, 1), lax.broadcasted_iota(jnp.int32, shape, 2)


def _gdn_chunk_common(k, gcc, bb, row, col, dot=_dot1):
    tril = row >= col
    gcr = jnp.swapaxes(gcc, 1, 2)
    dm = jnp.exp(jnp.where(tril, gcc - gcr, NEG))
    kb = k * bb
    lm = jnp.where(row > col, dot(kb, k, NT3) * dm, 0.0)
    return dm, kb, lm


def _unit_lower_inverse(lm, eye):
    t = eye - lm
    p = lm
    for _ in range(CHUNK.bit_length() - 2):
        p = _dot3(p, p, NN3)
        t = t + _dot3(t, p, NN3)
    return t


def _chunk_sum_matrix(tb, upper):
    r = lax.broadcasted_iota(jnp.int32, (tb, tb), 0)
    c = lax.broadcasted_iota(jnp.int32, (tb, tb), 1)
    same = (r // CHUNK) == (c // CHUNK)
    return (same & ((c >= r) if upper else (c <= r))).astype(F32)


def _gdn_fwd(q, k, v, gb, bb):
    nh, t, dh = q.shape
    nchunk = t // CHUNK

    def body(q_ref, k_ref, v_ref, g_ref, b_ref, o_ref, sall_ref, tall_ref, s_s):
        @pl.when(pl.program_id(0) == 0)
        def _():
            s_s[...] = jnp.zeros(s_s.shape, F32)

        row, col = _tri_masks(nh)
        qh, kh, vh, bbh, gcc = q_ref[...], k_ref[...], v_ref[...], b_ref[...], g_ref[...]
        dm, kb, lm = _gdn_chunk_common(kh, gcc, bbh, row, col)
        eg = jnp.exp(gcc)
        glr = gcc[:, CHUNK - 1:CHUNK, :]
        th = _unit_lower_inverse(lm, (row == col).astype(F32))
        w = _dot1(th, kb * eg, NN3)
        u = _dot1(th, vh * bbh, NN3)
        at = jnp.where(row >= col, _dot1(qh, kh, NT3) * dm, 0.0)
        sh = s_s[...]
        vn = u - _dot1(w, sh, NN3)
        o_ref[...] = _dot1(qh * eg, sh, NN3) + _dot1(at, vn, NN3)
        kd = kh * jnp.exp(glr - gcc)
        sall_ref[:, 0] = sh
        tall_ref[...] = th
        s_s[...] = sh * jnp.exp(glr) + _dot1(kd, vn, TN3)

    blk = pl.BlockSpec((nh, CHUNK, dh), lambda n: (0, n, 0))
    return pl.pallas_call(
        body, name="gdn_fwd", grid=(nchunk,), in_specs=[blk] * 5,
        out_specs=[blk, pl.BlockSpec((nh, 1, dh, dh), lambda n: (0, n, 0, 0)), blk],
        out_shape=[jax.ShapeDtypeStruct((nh, t, dh), F32), jax.ShapeDtypeStruct((nh, nchunk, dh, dh), F32),
                   jax.ShapeDtypeStruct((nh, t, CHUNK), F32)],
        scratch_shapes=[pltpu.VMEM((nh, dh, dh), F32)],
        compiler_params=_params(("arbitrary",)))(q, k, v, gb, bb)


def _gdn_bwd(q, k, v, gb, bb, sall, tall, do):
    nh, t, dh = q.shape
    nchunk = t // CHUNK

    def body(q_ref, k_ref, v_ref, g_ref, b_ref, sall_ref, tall_ref, do_ref,
             dq_ref, dk_ref, dv_ref, dg_ref, db_ref, ds_s):
        @pl.when(pl.program_id(0) == 0)
        def _():
            ds_s[...] = jnp.zeros(ds_s.shape, F32)

        row, col = _tri_masks(nh)
        tril, stril = row >= col, row > col
        rsum = lambda x: jnp.sum(x, axis=2, keepdims=True)
        qh, kh, vh, gcc, bbh = q_ref[...], k_ref[...], v_ref[...], g_ref[...], b_ref[...]
        sh, th, doh, dsp = sall_ref[:, 0], tall_ref[...], do_ref[...], ds_s[...]
        dm, kb, lm = _gdn_chunk_common(kh, gcc, bbh, row, col, _dot3)
        eg = jnp.exp(gcc)
        glr = gcc[:, CHUNK - 1:CHUNK, :]
        glv = jnp.exp(glr)
        egl = jnp.exp(glr - gcc)
        rw, ru = kb * eg, vh * bbh
        w, u = _dot3(th, rw, NN3), _dot3(th, ru, NN3)
        at = jnp.where(tril, _dot3(qh, kh, NT3) * dm, 0.0)
        qd, kd = qh * eg, kh * egl
        vn = u - _dot3(w, sh, NN3)
        dgl = jnp.sum(rsum(dsp * sh), axis=1, keepdims=True)
        dkd = _dot3(vn, dsp, NT3)
        dvn = _dot3(kd, dsp, NN3)
        dqd = _dot3(doh, sh, NT3)
        dat = jnp.where(tril, _dot3(doh, vn, NT3), 0.0)
        dvn = dvn + _dot3(at, doh, TN3)
        dw = -_dot3(dvn, sh, NT3)
        ds_s[...] = dsp * glv + _dot3(qd, doh, TN3) - _dot3(w, dvn, TN3)
        dpa = dat * dm
        dq_ref[...] = _dot1(dpa, kh, NN3) + dqd * eg
        dk = _dot1(dpa, qh, TN3) + dkd * egl
        t6 = rsum(dkd * kd)
        dgam = rsum(dqd * qd) - t6
        dgam_last = jnp.sum(t6, axis=1, keepdims=True) + dgl * glv
        drw = _dot3(th, dw, TN3)
        dru = _dot3(th, dvn, TN3)
        dl = -jnp.where(stril, _dot3(drw, w, NT3) + _dot3(dru, u, NT3), 0.0)
        dgam = dgam + rsum(drw * rw)
        dv_ref[...] = dru * bbh
        dp2 = dl * dm
        dkb = drw * eg + _dot1(dp2, kh, NN3)
        dk_ref[...] = dk + _dot1(dp2, kb, TN3) + dkb * bbh
        db_ref[...] = rsum(dru * vh) + rsum(dkb * kh) + jnp.zeros((nh, CHUNK, dh), F32)
        e = dat * at + dl * lm
        dgam_b = dgam + rsum(e) - _dot01(e, jnp.ones((nh, CHUNK, CHUNK), F32), TN3)
        dgam_b = dgam_b + jnp.where(row == CHUNK - 1, dgam_last, 0.0)
        dg_ref[...] = dgam_b

    rev = lambda n: (0, nchunk - 1 - n, 0)
    blk = pl.BlockSpec((nh, CHUNK, dh), rev)
    sblk = pl.BlockSpec((nh, 1, dh, dh), lambda n: (0, nchunk - 1 - n, 0, 0))
    out = jax.ShapeDtypeStruct((nh, t, dh), F32)
    return pl.pallas_call(
        body, name="gdn_bwd", grid=(nchunk,), in_specs=[blk] * 5 + [sblk, blk, blk], out_specs=[blk] * 5,
        out_shape=[out] * 5, scratch_shapes=[pltpu.VMEM((nh, dh, dh), F32)],
        compiler_params=_params(("arbitrary",)))(q, k, v, gb, bb, sall, tall, do)


def _group_ones():
    r = lax.broadcasted_iota(jnp.int32, (GDN_W, GDN_W), 0) // GDN_DH
    c = lax.broadcasted_iota(jnp.int32, (GDN_W, GDN_W), 1) // GDN_DH
    return (r == c).astype(F32)


def _conv_taps(x, xprev, w, has_prev):
    row = lax.broadcasted_iota(jnp.int32, x.shape, 0)
    out = x * w[GDN_CONV - 1:GDN_CONV, :]
    for s in range(1, GDN_CONV):
        sh = jnp.where(row >= s, _roll(x, s, 0), _roll(xprev, s, 0) * has_prev)
        out = out + sh * w[GDN_CONV - 1 - s:GDN_CONV - s, :]
    return out


def _head_cols(x, h):
    return x[:, h * GDN_DH:(h + 1) * GDN_DH]


def _heads_spec(tb):
    return pl.BlockSpec((N_HEADS, tb, GDN_DH), lambda i: (0, i, 0))


def _mixer_fwd(x, positions, w, tb, carried=None):
    t, d = x.shape
    tables = _rope_tables(positions)

    def pre(xb, g):
        return (xb * _rms_stats(xb) * g,)

    (hn,) = _rowwise("mix_pre", pre, [x], [w["mix_pre_g"]], [(d, BF16)], [], tb)
    proj = _mm("mix_in", hn, w["w_in_pad_t"], "nt", F32)

    def mla_pre(p0, gq, gkv):
        cq, ckv = p0[:, :MLA_Q_RANK], p0[:, MLA_Q_RANK:MLA_Q_RANK + MLA_KV_RANK]
        return cq * _rms_stats(cq) * gq, ckv * _rms_stats(ckv) * gkv

    nq, nkv = _rowwise("mla_pre", mla_pre, [(proj, 512, PIN_MLA // 512, 0)],
                       [w["mla_q_norm_g"], w["mla_kv_norm_g"]], [(MLA_Q_RANK, BF16), (MLA_KV_RANK, BF16)], [], tb)
    qraw = _mm("mla_uq", nq, w["w_uq_pad"], "nn", F32)
    kv = _mm("mla_ukv", nkv, w["w_kv_pad"], "nn", F32)

    def rope_f(qr, kn, vv, kpe, c, s1, s2):
        qo = _heads_apply(qr, lambda xh: _rope(xh, c, s1, s2)) * _attn_scale()
        kp = _rope(kpe, c, s1, s2)
        return qo, kn + jnp.tile(kp, (1, N_HEADS)), vv

    q, k, v = _rowwise("mla_rope", rope_f,
                       [qraw, (kv, MLA_PAD, 0, 0), (kv, MLA_PAD, 1, 0), (proj, HEAD_LANES, PIN_KPE // HEAD_LANES, 0),
                        tables[0], tables[1], tables[2]], [],
                       [(MLA_PAD, BF16)] * 3, [], tb // 2)
    tq = min(1024, t)
    o, lse, *carried_out = _attn_fwd(q, k, v, tq, carried)

    def mla_post(ob, g):
        return (ob * _rms_stats(ob, N_HEADS * MLA_V) * g,)

    (cat,) = _rowwise("mla_post", mla_post, [o], [w["mla_out_g_pad"]], [(MLA_PAD, BF16)], [], tb, wide=(CAT_W, 0))

    gones = _group_ones()
    steps = t // tb

    def gdn_pre(xq, xk, xv, pq, pk, pv, cw, go, has_prev):
        outs = []
        for j, (xc, xp) in enumerate(((xq, pq), (xk, pk), (xv, pv))):
            c = _conv_taps(xc, xp, cw[:, j * GDN_W:(j + 1) * GDN_W], has_prev)
            a = c * _sigmoid(c)
            if j < 2:
                rn = lax.rsqrt(_dot01(a * a, go) + EPS)
                a = a * rn
                if j == 0:
                    a = a * (GDN_DH ** -0.5)
            outs.append(a)
        return tuple(outs)

    qh, kh, vh = _gdn_pre_call("gdn_pre", gdn_pre, proj, w["conv_w"], gones, tb, steps)
    heads_shape = jax.ShapeDtypeStruct((N_HEADS, t, GDN_DH), F32)
    lanes_shape = jax.ShapeDtypeStruct((t, HEAD_LANES), F32)
    lanes_spec = pl.BlockSpec((tb, HEAD_LANES), lambda i: (i, 0))
    vec_spec = lambda n: pl.BlockSpec((1, n), lambda i: (0, 0))

    def gate_f(ab_ref, al_ref, dt_ref, g_ref, b_ref, gh_ref, bh_ref):
        g, b = _gb_fwd(ab_ref[...], al_ref[...], dt_ref[...])
        g_ref[...] = g
        b_ref[...] = b
        gc = _dot01(_chunk_sum_matrix(tb, False), g, ones="lhs")
        for h in range(N_HEADS):
            gh_ref[h] = jnp.broadcast_to(gc[:, h:h + 1], (tb, GDN_DH))
            bh_ref[h] = jnp.broadcast_to(b[:, N_HEADS + h:N_HEADS + h + 1], (tb, GDN_DH))

    g128, b128, gbh, bbh = pl.pallas_call(
        gate_f, name="gdn_gate_f", grid=(steps,),
        in_specs=[pl.BlockSpec((tb, HEAD_LANES), lambda i: (i, PIN_AB // HEAD_LANES)), vec_spec(HEAD_LANES),
                  vec_spec(HEAD_LANES)],
        out_specs=[lanes_spec, lanes_spec, _heads_spec(tb), _heads_spec(tb)],
        out_shape=[lanes_shape, lanes_shape, heads_shape, heads_shape],
        compiler_params=_params(("arbitrary",)))(proj, w["a_log_pad"], w["dt_bias_pad"])
    oh, sall, tall = _gdn_fwd(qh, kh, vh, gbh, bbh)

    def gdn_post(o_ref, gt_ref, g_ref, cat_in, cat_ref):
        gt, g = gt_ref[...], g_ref[...]
        outs = []
        for h in range(N_HEADS):
            ob, gth = o_ref[h], _head_cols(gt, h)
            outs.append(ob * _rms_stats(ob) * g * (gth * _sigmoid(gth)))
        cat_ref[...] = jnp.concatenate(outs, axis=1).astype(cat_ref.dtype)

    gate_spec = pl.BlockSpec((tb, GDN_W), lambda i: (i, PIN_GATE // GDN_W))
    cat = pl.pallas_call(
        gdn_post, name="gdn_post", grid=(steps,),
        in_specs=[_heads_spec(tb), gate_spec, vec_spec(GDN_DH), ANY_SPEC],
        out_specs=pl.BlockSpec((tb, GDN_W), lambda i: (i, MLA_PAD // GDN_W)),
        out_shape=jax.ShapeDtypeStruct((t, CAT_W), BF16), input_output_aliases={3: 0},
        compiler_params=_params(("arbitrary",)))(oh, proj, w["gdn_norm_g"], cat)
    mixed = _mm("mix_out", cat, w["w_out_pad"], "nn", F32)

    def post(xb, hb, g):
        return (xb + hb * _rms_stats(hb) * g,)

    (y,) = _rowwise("mix_post", post, [x, mixed], [w["mix_post_g"]], [(d, F32)], [], tb)
    saved = dict(x=x, hn=hn, proj=proj, nq=nq, nkv=nkv, q=q, k=k, v=v, o=o, lse=lse, qh=qh, kh=kh, vh=vh,
                 gbh=gbh, bbh=bbh, oh=oh, sall=sall, tall=tall, cat=cat, mixed=mixed,
                 tables=tables, g128=g128, b128=b128)
    return y, saved, carried_out


def _qkv_specs(tb):
    base = PIN_QKV // GDN_W
    cur = [pl.BlockSpec((tb, GDN_W), lambda i, j=j: (i, base + j)) for j in range(3)]
    prev = [pl.BlockSpec((tb, GDN_W), lambda i, j=j: (jnp.maximum(i - 1, 0), base + j)) for j in range(3)]
    return cur + prev


def _gdn_pre_call(name, fn, proj, conv_w, gones, tb, steps):
    t = proj.shape[0]

    def body(xq, xk, xv, pq, pk, pv, cw, go, oq, ok, ov):
        has_prev = jnp.where(pl.program_id(0) == 0, 0.0, 1.0)
        outs = fn(xq[...], xk[...], xv[...], pq[...], pk[...], pv[...], cw[...], go[...], has_prev)
        for r, val in zip((oq, ok, ov), outs):
            for h in range(N_HEADS):
                r[h] = _head_cols(val, h)

    return pl.pallas_call(
        body, name=name, grid=(steps,),
        in_specs=_qkv_specs(tb) + [pl.BlockSpec(conv_w.shape, lambda i: (0, 0)),
                                   pl.BlockSpec(gones.shape, lambda i: (0, 0))],
        out_specs=[_heads_spec(tb)] * 3,
        out_shape=[jax.ShapeDtypeStruct((N_HEADS, t, GDN_DH), F32)] * 3,
        compiler_params=_params(("arbitrary",)))(proj, proj, proj, proj, proj, proj, conv_w, gones)


def _softplus(x):
    return jnp.maximum(x, 0.0) + jnp.log1p(jnp.exp(-jnp.abs(x)))


def _gb_fwd(ab, a_log, dt_bias):
    g = -jnp.exp(a_log) * _softplus(ab + dt_bias)
    return g, _sigmoid(ab)


def _rope_tables(positions):
    half = MLA_ROPE // 2
    freqs = ROPE_THETA ** (-jnp.arange(half, dtype=F32) / half)
    ang = positions.reshape(-1).astype(F32)[:, None] * freqs
    cos, sin = jnp.cos(ang), jnp.sin(ang)
    t = ang.shape[0]
    one = jnp.ones((t, MLA_NOPE), F32)
    z16, z32, z64 = jnp.zeros((t, half), F32), jnp.zeros((t, MLA_ROPE), F32), jnp.zeros((t, MLA_NOPE), F32)
    c = jnp.concatenate([one, cos, cos, jnp.ones((t, MLA_ROPE), F32)], axis=1)
    s1 = jnp.concatenate([z64, -sin, z16, z32], axis=1)
    s2 = jnp.concatenate([z64, z16, sin, z32], axis=1)
    return c, s1, s2


def _mixer_bwd(dy, sv, w, tb, carried=None):
    x, proj = sv["x"], sv["proj"]
    t, d = x.shape
    c, s1, s2 = sv["tables"]
    grads = {}

    def post_b(hb, dyb, g):
        return _rms_bwd(hb, _rms_stats(hb), g, dyb)

    dmixed, grads["mix_post_g"] = _rowwise("mix_post_b", post_b, [sv["mixed"], dy], [w["mix_post_g"]],
                                           [(d, BF16)], [(1, d)], tb)
    dcat = _mm("mix_out_bx", dmixed, w["w_out_pad"], "nt", F32)
    grads["w_out_pad"] = _mm("mix_out_bw", sv["cat"], dmixed, "tn", F32)
    steps = t // tb
    vec_spec = lambda n: pl.BlockSpec((1, n), lambda i: (0, 0))

    def gdn_post_b(o_ref, gt_ref, do_ref, g_ref, dproj_ref, doh_ref, dg_ref):
        @pl.when(pl.program_id(0) == 0)
        def _():
            dg_ref[...] = jnp.zeros(dg_ref.shape, F32)

        gt, dob, g = gt_ref[...], do_ref[...], g_ref[...]
        dgates = []
        for h in range(N_HEADS):
            ob, gth, dobh = o_ref[h], _head_cols(gt, h), _head_cols(dob, h)
            sg = _sigmoid(gth)
            r = _rms_stats(ob)
            dxo, dg = _rms_bwd(ob, r, g, dobh * (gth * sg))
            doh_ref[h] = dxo
            dg_ref[...] += dg
            dgates.append(dobh * (ob * r * g) * (sg * (1.0 + gth * (1.0 - sg))))
        dproj_ref[...] = jnp.concatenate(dgates, axis=1).astype(dproj_ref.dtype)

    dproj, doh, grads["gdn_norm_g"] = pl.pallas_call(
        gdn_post_b, name="gdn_post_b", grid=(steps,),
        in_specs=[_heads_spec(tb), pl.BlockSpec((tb, GDN_W), lambda i: (i, PIN_GATE // GDN_W)),
                  pl.BlockSpec((tb, GDN_W), lambda i: (i, MLA_PAD // GDN_W)), vec_spec(GDN_DH)],
        out_specs=[pl.BlockSpec((tb, GDN_W), lambda i: (i, PIN_GATE // GDN_W)), _heads_spec(tb), vec_spec(GDN_DH)],
        out_shape=[jax.ShapeDtypeStruct((t, PIN_W), BF16), jax.ShapeDtypeStruct((N_HEADS, t, GDN_DH), F32),
                   jax.ShapeDtypeStruct((1, GDN_DH), F32)],
        compiler_params=_params(("arbitrary",)))(sv["oh"], proj, dcat, w["gdn_norm_g"])

    def mla_post_b(ob, dmo, g):
        do, dg = _rms_bwd(ob, _rms_stats(ob, N_HEADS * MLA_V), g, dmo, N_HEADS * MLA_V)
        prod = do * ob
        delta = _heads_apply(prod, lambda ph: jnp.sum(ph, axis=1, keepdims=True) + jnp.zeros_like(ph))
        return do, delta, dg

    do, delta, grads["mla_out_g_pad"] = _rowwise(
        "mla_post_b", mla_post_b, [sv["o"], (dcat, MLA_PAD, 0, 0)], [w["mla_out_g_pad"]],
        [(MLA_PAD, BF16), (MLA_PAD, F32)], [(1, MLA_PAD)], tb // 2)
    tq = min(1024, t)
    dq, dk, dv, *carried_out = _attn_bwd(sv["q"], sv["k"], sv["v"], do, sv["lse"], delta, tq, carried)
    n_groups = N_HEADS // BWD_HEADS
    tr = tb // 2
    dq_groups = [(dq, BWD_HEADS * HEAD_LANES, 0, grp * (t // tr)) for grp in range(n_groups)]

    def rope_b(*blocks):
        dqb = jnp.concatenate(blocks[:n_groups], axis=1)
        dkb, dvb, cc, a1, a2 = blocks[n_groups:]
        dqr = _heads_apply(dqb * _attn_scale(), lambda xh: _rope(xh, cc, -a1, -a2))
        ksum = dkb[:, :HEAD_LANES]
        for h in range(1, N_HEADS):
            ksum = ksum + dkb[:, h * HEAD_LANES:(h + 1) * HEAD_LANES]
        lane = lax.broadcasted_iota(jnp.int32, ksum.shape, 1)
        keep = (lane >= MLA_NOPE) & (lane < MLA_NOPE + MLA_ROPE)
        dkpe = jnp.where(keep, _rope(ksum, cc, -a1, -a2), 0.0)
        return dqr, jnp.concatenate([dkb, dvb], axis=1), dkpe

    dqraw, dkv, dkpe = _rowwise("mla_rope_b", rope_b, dq_groups + [dk, dv, c, s1, s2], [],
                                [(MLA_PAD, BF16, t), (2 * MLA_PAD, BF16), (HEAD_LANES, F32)], [], tr)
    dnq = _mm("mla_uq_bx", dqraw, w["w_uq_pad"], "nt", F32)
    grads["w_uq_pad"] = _mm("mla_uq_bw", sv["nq"], dqraw, "tn", F32)
    dnkv = _mm("mla_ukv_bx", dkv, w["w_kv_pad"], "nt", F32)
    grads["w_kv_pad"] = _mm("mla_ukv_bw", sv["nkv"], dkv, "tn", F32)

    def mla_pre_b(p0, dnqb, dnkvb, dkpeb, gq, gkv):
        cq, ckv = p0[:, :MLA_Q_RANK], p0[:, MLA_Q_RANK:MLA_Q_RANK + MLA_KV_RANK]
        dcq, dgq = _rms_bwd(cq, _rms_stats(cq), gq, dnqb)
        dckv, dgkv = _rms_bwd(ckv, _rms_stats(ckv), gkv, dnkvb)
        return jnp.concatenate([dcq, dckv, dkpeb], axis=1), dgq, dgkv

    dproj, grads["mla_q_norm_g"], grads["mla_kv_norm_g"] = _rowwise(
        "mla_pre_b", mla_pre_b, [(proj, 512, PIN_MLA // 512, 0), dnq, dnkv, dkpe],
        [w["mla_q_norm_g"], w["mla_kv_norm_g"]], [(512, BF16)], [(1, MLA_Q_RANK), (1, MLA_KV_RANK)], tb,
        wide=(PIN_W, PIN_MLA // 512), carry=dproj)

    dqh, dkh, dvh, dgh, dbh = _gdn_bwd(sv["qh"], sv["kh"], sv["vh"], sv["gbh"], sv["bbh"], sv["sall"], sv["tall"], doh)
    gones = _group_ones()

    def gdn_pre_b(xq, xk, xv, pq, pk, pv, dq_, dk_, dv_, cw, go, has_prev):
        outs = []
        for j, (xc, xp, dd) in enumerate(((xq, pq, dq_), (xk, pk, dk_), (xv, pv, dv_))):
            cc = _conv_taps(xc, xp, cw[:, j * GDN_W:(j + 1) * GDN_W], has_prev)
            sg = _sigmoid(cc)
            a = cc * sg
            if j < 2:
                rn = lax.rsqrt(_dot01(a * a, go) + EPS)
                if j == 0:
                    dd = dd * (GDN_DH ** -0.5)
                da = rn * dd - a * (rn * rn * rn) * _dot01(dd * a, go)
            else:
                da = dd
            outs.append(da * (sg * (1.0 + cc * (1.0 - sg))))
        return tuple(outs)

    dcq, dck, dcv = _gdn_pre_b_call("gdn_pre_b", gdn_pre_b, proj, (dqh, dkh, dvh), w["conv_w"], gones, tb, steps)
    dproj, grads["conv_w"] = _conv_bwd_call("gdn_conv_b", proj, (dcq, dck, dcv), w["conv_w"], dproj, tb, steps)

    def gate_b(ab_ref, g_ref, b_ref, dgh_ref, dbh_ref, al_ref, dt_ref, carry_ref, dab_ref, dal_ref, ddt_ref):
        @pl.when(pl.program_id(0) == 0)
        def _():
            dal_ref[...] = jnp.zeros(dal_ref.shape, F32)
            ddt_ref[...] = jnp.zeros(ddt_ref.shape, F32)

        ab, g128, b128 = ab_ref[...], g_ref[...], b_ref[...]
        lane = lax.broadcasted_iota(jnp.int32, ab.shape, 1)
        dg_ = jnp.zeros(ab.shape, F32)
        db_ = jnp.zeros(ab.shape, F32)
        for h in range(N_HEADS):
            dg_ = dg_ + jnp.where(lane == h, jnp.broadcast_to(dgh_ref[h][:, 0:1], ab.shape), 0.0)
            db_ = db_ + jnp.where(lane == N_HEADS + h, jnp.broadcast_to(dbh_ref[h][:, 0:1], ab.shape), 0.0)
        dg_ = _dot01(_chunk_sum_matrix(tb, True), dg_, ones="lhs")
        slope = -jnp.exp(al_ref[...]) * _sigmoid(ab + dt_ref[...])
        dab_ref[...] = (dg_ * slope + db_ * b128 * (1.0 - b128)).astype(dab_ref.dtype)
        dal_ref[...] += jnp.sum(dg_ * g128, axis=0, keepdims=True)
        ddt_ref[...] += jnp.sum(dg_ * slope, axis=0, keepdims=True)

    lanes_spec = pl.BlockSpec((tb, HEAD_LANES), lambda i: (i, 0))
    ab_spec = pl.BlockSpec((tb, HEAD_LANES), lambda i: (i, PIN_AB // HEAD_LANES))
    dproj, grads["a_log_pad"], grads["dt_bias_pad"] = pl.pallas_call(
        gate_b, name="gdn_gate_b", grid=(steps,),
        in_specs=[ab_spec, lanes_spec, lanes_spec, _heads_spec(tb), _heads_spec(tb), vec_spec(HEAD_LANES),
                  vec_spec(HEAD_LANES), ANY_SPEC],
        out_specs=[ab_spec, vec_spec(HEAD_LANES), vec_spec(HEAD_LANES)],
        out_shape=[jax.ShapeDtypeStruct((t, PIN_W), BF16), jax.ShapeDtypeStruct((1, HEAD_LANES), F32),
                   jax.ShapeDtypeStruct((1, HEAD_LANES), F32)],
        input_output_aliases={7: 0},
        compiler_params=_params(("arbitrary",)))(proj, sv["g128"], sv["b128"], dgh, dbh, w["a_log_pad"],
                                                 w["dt_bias_pad"], dproj)
    dhn = _mm("mix_in_bx", dproj, w["w_in_pad_t"], "nn", F32)
    grads["w_in_pad_t"] = _mm("mix_in_bw", dproj, sv["hn"], "tn", F32)

    def pre_b(xb, dnb, dyb, g):
        dx, dg = _rms_bwd(xb, _rms_stats(xb), g, dnb)
        return dyb + dx, dg

    dx, grads["mix_pre_g"] = _rowwise("mix_pre_b", pre_b, [x, dhn, dy], [w["mix_pre_g"]], [(d, F32)], [(1, d)], tb)
    return dx, grads, carried_out


def _gdn_pre_b_call(name, fn, proj, dd, conv_w, gones, tb, steps):
    t = proj.shape[0]

    def body(xq, xk, xv, pq, pk, pv, d0, d1, d2, cw, go, oq, ok, ov):
        has_prev = jnp.where(pl.program_id(0) == 0, 0.0, 1.0)
        dd_rows = [jnp.concatenate([dr[h] for h in range(N_HEADS)], axis=1) for dr in (d0, d1, d2)]
        outs = fn(xq[...], xk[...], xv[...], pq[...], pk[...], pv[...], *dd_rows, cw[...], go[...], has_prev)
        for r, val in zip((oq, ok, ov), outs):
            r[...] = val

    return pl.pallas_call(
        body, name=name, grid=(steps,),
        in_specs=_qkv_specs(tb) + [_heads_spec(tb)] * 3 + [pl.BlockSpec(conv_w.shape, lambda i: (0, 0)),
                                                          pl.BlockSpec(gones.shape, lambda i: (0, 0))],
        out_specs=[pl.BlockSpec((tb, GDN_W), lambda i: (i, 0))] * 3,
        out_shape=[jax.ShapeDtypeStruct((t, GDN_W), F32)] * 3,
        compiler_params=_params(("arbitrary",)))(proj, proj, proj, proj, proj, proj, *dd, conv_w, gones)


def _conv_bwd_call(name, proj, dc, conv_w, dproj, tb, steps):
    t = proj.shape[0]
    dcur = [pl.BlockSpec((tb, GDN_W), lambda i: (i, 0))] * 3
    dnext = [pl.BlockSpec((tb, GDN_W), lambda i: (jnp.minimum(i + 1, steps - 1), 0))] * 3

    def body(xq, xk, xv, pq, pk, pv, d0, d1, d2, n0, n1, n2, cw, carry_ref, dx_ref, dw_ref):
        i = pl.program_id(0)
        has_prev = jnp.where(i == 0, 0.0, 1.0)
        has_next = jnp.where(i == steps - 1, 0.0, 1.0)

        @pl.when(i == 0)
        def _():
            dw_ref[...] = jnp.zeros(dw_ref.shape, F32)

        wv = cw[...]
        dws, dxs = [], []
        for j, (xr, pr, dr, nr) in enumerate(((xq, pq, d0, n0), (xk, pk, d1, n1), (xv, pv, d2, n2))):
            x, xp, dcv, dnx = xr[...], pr[...], dr[...], nr[...]
            wj = wv[:, j * GDN_W:(j + 1) * GDN_W]
            row = lax.broadcasted_iota(jnp.int32, x.shape, 0)
            dx = dcv * wj[GDN_CONV - 1:GDN_CONV, :]
            rows_w = [jnp.sum(dcv * x, axis=0, keepdims=True)]
            for s in range(1, GDN_CONV):
                up = jnp.where(row < tb - s, _roll(dcv, tb - s, 0), _roll(dnx, tb - s, 0) * has_next)
                dx = dx + up * wj[GDN_CONV - 1 - s:GDN_CONV - s, :]
                sh = jnp.where(row >= s, _roll(x, s, 0), _roll(xp, s, 0) * has_prev)
                rows_w.append(jnp.sum(dcv * sh, axis=0, keepdims=True))
            dxs.append(dx)
            dws.append(jnp.concatenate(rows_w[::-1], axis=0))
        dx_ref[...] = jnp.concatenate(dxs, axis=1).astype(dx_ref.dtype)
        dw_ref[...] += jnp.concatenate(dws, axis=1)

    return pl.pallas_call(
        body, name=name, grid=(steps,),
        in_specs=_qkv_specs(tb) + dcur + dnext + [pl.BlockSpec(conv_w.shape, lambda i: (0, 0)), ANY_SPEC],
        out_specs=[pl.BlockSpec((tb, 3 * GDN_W), lambda i: (i, PIN_QKV // (3 * GDN_W))),
                   pl.BlockSpec(conv_w.shape, lambda i: (0, 0))],
        out_shape=[jax.ShapeDtypeStruct((t, PIN_W), BF16), jax.ShapeDtypeStruct(conv_w.shape, F32)],
        input_output_aliases={13: 0},
        compiler_params=_params(("arbitrary",)))(proj, proj, proj, proj, proj, proj, *dc, *dc, conv_w, dproj)


def _pad_heads_cols(wm, per_head):
    r = wm.shape[0]
    return jnp.pad(wm.reshape(r, N_HEADS, per_head), ((0, 0), (0, 0), (0, HEAD_LANES - per_head))).reshape(r, MLA_PAD)


def _unpad_heads_cols(wm, per_head):
    r = wm.shape[0]
    return wm.reshape(r, N_HEADS, HEAD_LANES)[:, :, :per_head].reshape(r, N_HEADS * per_head)


W_IN_COLS = MLA_Q_RANK + MLA_KV_RANK + MLA_ROPE + 3 * GDN_W + 2 * N_HEADS + GDN_W
W_IN_SHARD = W_IN_COLS // N_SHARD
W_IN_SHARD_PAD = 640
_Q0 = MLA_Q_RANK + MLA_KV_RANK
_Q1 = _Q0 + MLA_ROPE
_Q2 = _Q1 + 3 * GDN_W
_Q3 = _Q2 + 2 * N_HEADS
W_IN_SEGMENTS = [(0, _Q0, PIN_MLA), (_Q0, _Q1, PIN_KPE + MLA_NOPE), (_Q1, _Q2, PIN_QKV), (_Q2, _Q3, PIN_AB),
                 (_Q3, W_IN_COLS, PIN_GATE)]


def _win_pad_t(slabs):
    d = slabs.shape[2]
    pieces, at = [], 0
    for c0, c1, r0 in sorted(W_IN_SEGMENTS, key=lambda s: s[2]):
        if r0 > at:
            pieces.append(jnp.zeros((r0 - at, d), slabs.dtype))
        for q in range(N_SHARD):
            lo, hi = max(c0, q * W_IN_SHARD), min(c1, (q + 1) * W_IN_SHARD)
            if lo < hi:
                pieces.append(slabs[q, lo - q * W_IN_SHARD:hi - q * W_IN_SHARD])
        at = r0 + c1 - c0
    pieces.append(jnp.zeros((PIN_W - at, d), slabs.dtype))
    return jnp.concatenate(pieces, axis=0)


def _win_cols_t(wp_t, c_lo, c_hi):
    pieces = []
    for c0, c1, r0 in W_IN_SEGMENTS:
        lo, hi = max(c0, c_lo), min(c1, c_hi)
        if lo < hi:
            pieces.append(wp_t[r0 + lo - c0:r0 + hi - c0])
    return jnp.concatenate(pieces, axis=0)


def _wkv_to_pad(wkv):
    r = wkv.shape[0]
    w3 = wkv.reshape(r, N_HEADS, MLA_NOPE + MLA_V)
    kpart = jnp.pad(w3[:, :, :MLA_NOPE], ((0, 0), (0, 0), (0, HEAD_LANES - MLA_NOPE))).reshape(r, MLA_PAD)
    vpart = jnp.pad(w3[:, :, MLA_NOPE:], ((0, 0), (0, 0), (0, HEAD_LANES - MLA_V))).reshape(r, MLA_PAD)
    return jnp.concatenate([kpart, vpart], axis=1)


def _wkv_from_pad(wp):
    r = wp.shape[0]
    kpart = wp[:, :MLA_PAD].reshape(r, N_HEADS, HEAD_LANES)[:, :, :MLA_NOPE]
    vpart = wp[:, MLA_PAD:].reshape(r, N_HEADS, HEAD_LANES)[:, :, :MLA_V]
    return jnp.concatenate([kpart, vpart], axis=2).reshape(r, N_HEADS * (MLA_NOPE + MLA_V))


def _wout_to_pad(wo):
    n = wo.shape[1]
    mla = jnp.pad(wo[:N_HEADS * MLA_V].reshape(N_HEADS, MLA_V, n), ((0, 0), (0, HEAD_LANES - MLA_V), (0, 0)))
    return jnp.concatenate([mla.reshape(MLA_PAD, n), wo[N_HEADS * MLA_V:]], axis=0)


def _wout_from_pad(wp):
    n = wp.shape[1]
    mla = wp[:MLA_PAD].reshape(N_HEADS, HEAD_LANES, n)[:, :MLA_V].reshape(N_HEADS * MLA_V, n)
    return jnp.concatenate([mla, wp[MLA_PAD:]], axis=0)


def _pad_lanes(v, n):
    return jnp.pad(v, ((0, 0), (0, n - v.shape[1])))


def _compute_weights(full):
    w = {}
    for n in FFN_BIG:
        if n in full:
            w[n] = full[n].astype(MM_DTYPE)
    w["w_in_pad_t"] = _win_pad_t(full["w_in"]).astype(MM_DTYPE)
    w["w_uq_pad"] = _pad_heads_cols(full["mla_w_uq"], MLA_NOPE + MLA_ROPE).astype(MM_DTYPE)
    w["w_kv_pad"] = _wkv_to_pad(full["mla_w_ukv"]).astype(MM_DTYPE)
    w["w_out_pad"] = _wout_to_pad(full["w_out"]).astype(MM_DTYPE)
    w["conv_w"] = full["gdn_conv_w"].astype(F32)
    for n in ("ffn1_pre_g", "ffn1_post_g", "mix_pre_g", "mla_q_norm_g", "mla_kv_norm_g", "gdn_norm_g", "mix_post_g",
              "ffn2_pre_g", "ffn2_post_g"):
        w[n] = full[n]
    w["mla_out_g_pad"] = _pad_heads_cols(full["mla_out_g"], MLA_V)
    w["a_log_pad"] = _pad_lanes(full["gdn_a_log"], HEAD_LANES)
    w["dt_bias_pad"] = _pad_lanes(full["gdn_dt_bias"], HEAD_LANES)
    return w


FFN2_BIG = FFN_BIG[3:]


def _local_step(x, positions, loss_target, full, late=None):
    t, d = x.shape
    tb = min(512, t)
    tm = min(1024, t)
    w = _compute_weights(full)
    ffn = lambda tag: (w[tag + "_pre_g"], w[tag + "_w_gate"], w[tag + "_w_up"], w[tag + "_w_down"], w[tag + "_post_g"])
    x1, sv1 = _ffn_fwd("ffn1", x, *ffn("ffn1"), tm)
    x2, svm, gathered = _mixer_fwd(x1, positions, w, tb, _carried_gather(late[0]) if late else None)
    for n, gw in zip(FFN2_BIG, gathered):
        w[n] = gw
    x3, sv2 = _ffn_fwd("ffn2", x2, *ffn("ffn2"), tm)

    def loss_f(yb, tg):
        e = yb - tg
        return e * (1.0 / d), jnp.sum(e * e, axis=0, keepdims=True)

    dy, lsum = _rowwise("loss", loss_f, [x3, loss_target], [], [(d, F32)], [(1, d)], tb)
    g = {}
    dx2, g["ffn2_pre_g"], g["ffn2_w_gate"], g["ffn2_w_up"], g["ffn2_w_down"], g["ffn2_post_g"], _, _ = _ffn_bwd(
        "ffn2", dy, sv2, *ffn("ffn2"), tm, tm)

    def pair_sums(arrs, tag):
        got = _swap_halves(arrs, tag)
        return [_add_pair("add_pair%s_%d" % (tag, i), gi, gt, late[1]) for i, (gi, gt) in enumerate(zip(arrs, got))]

    def chip_sums(pairs, slabs, tag):
        return [_add_chips("add_chips%s_%d" % (tag, i), pr, sl, late[2]) for i, (pr, sl) in enumerate(zip(pairs, slabs))]

    if late:
        pairs2 = pair_sums([g[n] for n in FFN2_BIG], "_ffn2")
        dx1, gm, slabs2 = _mixer_bwd(dx2, svm, w, tb, _carried_scatter(pairs2))
        for n, hs in zip(FFN2_BIG, chip_sums(pairs2, slabs2, "_ffn2")):
            g[n] = hs
    else:
        dx1, gm, _ = _mixer_bwd(dx2, svm, w, tb)
    g["w_in"] = jnp.stack([jnp.pad(_win_cols_t(gm["w_in_pad_t"], q * W_IN_SHARD, (q + 1) * W_IN_SHARD),
                                   ((0, W_IN_SHARD_PAD - W_IN_SHARD), (0, 0))) for q in range(N_SHARD)])
    g["mla_w_uq"] = _unpad_heads_cols(gm["w_uq_pad"], MLA_NOPE + MLA_ROPE)
    g["mla_w_ukv"] = _wkv_from_pad(gm["w_kv_pad"])
    g["gdn_conv_w"] = gm["conv_w"]
    g["w_out"] = _wout_from_pad(gm["w_out_pad"])
    ffn1_names = FFN_BIG[:3]
    if late:
        quarters = [_pack([jnp.split(g[n], N_SHARD, axis=SHARD_AXIS[n])[q] for n in MIX_BIG], MM_DTYPE)
                    for q in range(N_SHARD)]
        pairs_m = pair_sums([g["w_in"].astype(MM_DTYPE), jnp.stack(quarters)], "_mix")
        pairs1 = []

        def make_up(*dws):
            pairs1.extend(pair_sums(list(dws), "_ffn1"))
            return _carried_scatter(pairs1)

        dx0, g["ffn1_pre_g"], _, _, _, g["ffn1_post_g"], slabs_m, slabs1 = _ffn_bwd(
            "ffn1", dx1, sv1, *ffn("ffn1"), tm, tm, _carried_scatter(pairs_m), make_up)
        for n, hs in zip(ffn1_names, chip_sums(pairs1, slabs1, "_ffn1")):
            g[n] = hs
        g["w_in"], g["mix_pack"] = chip_sums(pairs_m, slabs_m, "_mix")
    else:
        dx0, g["ffn1_pre_g"], g["ffn1_w_gate"], g["ffn1_w_up"], g["ffn1_w_down"], g["ffn1_post_g"], _, _ = _ffn_bwd(
            "ffn1", dx1, sv1, *ffn("ffn1"), tm, tm)
    g["mix_pre_g"], g["mix_post_g"] = gm["mix_pre_g"], gm["mix_post_g"]
    g["mla_q_norm_g"], g["mla_kv_norm_g"] = gm["mla_q_norm_g"], gm["mla_kv_norm_g"]
    g["gdn_norm_g"] = gm["gdn_norm_g"]
    g["mla_out_g"] = _unpad_heads_cols(gm["mla_out_g_pad"], MLA_V)
    g["gdn_a_log"] = gm["a_log_pad"][:, :N_HEADS]
    g["gdn_dt_bias"] = gm["dt_bias_pad"][:, :N_HEADS]
    return lsum, dx0, g


HBM_SPEC = pl.BlockSpec(memory_space=pltpu.HBM)


def _place():
    return lax.axis_index("x"), lax.axis_index("y"), lax.axis_index("c")


def _exchange_call(name, body, ins, out_shapes, n_remote, n_local):
    return pl.pallas_call(
        body, name=name, in_specs=[HBM_SPEC] * len(ins), out_specs=[HBM_SPEC] * len(out_shapes), out_shape=out_shapes,
        scratch_shapes=[pltpu.SemaphoreType.DMA((n_remote,)), pltpu.SemaphoreType.DMA((n_remote,)),
                        pltpu.SemaphoreType.DMA((n_local,))])(*ins)


def _other_chips(x, y):
    return [(1 - x, y), (x, 1 - y), (1 - x, 1 - y)]


def _at_each_chip(fn):
    x, y, _ = _place()
    for cx in range(2):
        for cy in range(2):
            pl.when((x == cx) & (y == cy))(functools.partial(fn, cx, cy))


def _at_each_device(fn):
    x, y, c = _place()
    for cx in range(2):
        for cy in range(2):
            for cc in range(2):
                pl.when((x == cx) & (y == cy) & (c == cc))(functools.partial(fn, cx, cy, cc))


def _at_each_core(fn):
    c = lax.axis_index("c")
    for cc in range(2):
        pl.when(c == cc)(functools.partial(fn, cc))


def _gather_shards(ws):
    nw = len(ws)

    def body(*refs):
        w_refs, out_refs = refs[:nw], refs[nw:2 * nw]
        send_sems, recv_sems, local_sems = refs[2 * nw:]

        def run(x, y, c):
            chips = _other_chips(x, y)
            me, sibling = 2 * x + y, (x, y, 1 - c)

            def half(ref, which):
                hr = ref.shape[0] // 2
                return ref.at[pl.ds(which * hr, hr)]

            def over_ici(i, j, src, slab, to):
                return pltpu.make_async_remote_copy(
                    src_ref=half(src, c), dst_ref=half(out_refs[i].at[slab], c), send_sem=send_sems.at[7 * i + j],
                    recv_sem=recv_sems.at[7 * i + j], device_id=to, device_id_type=MESH)

            def over_d2d(i, j, slab, which):
                return pltpu.make_async_remote_copy(
                    src_ref=half(out_refs[i].at[slab], which), dst_ref=half(out_refs[i].at[slab], which),
                    send_sem=send_sems.at[7 * i + 3 + j], recv_sem=recv_sems.at[7 * i + 3 + j], device_id=sibling,
                    device_id_type=MESH)

            def own(i, w_ref):
                return pltpu.make_async_remote_copy(
                    src_ref=w_ref, dst_ref=out_refs[i].at[me], send_sem=send_sems.at[7 * i + 6],
                    recv_sem=recv_sems.at[7 * i + 6], device_id=sibling, device_id_type=MESH)

            sends, passed = [], []
            for i, w_ref in enumerate(w_refs):
                for j, (px, py) in enumerate(chips):
                    sends.append(over_ici(i, j, w_ref, me, (px, py, c)))
                    sends[-1].start()
            for i, w_ref in enumerate(w_refs):
                sends.append(own(i, w_ref))
                sends[-1].start()
            for i, w_ref in enumerate(w_refs):
                for j, (px, py) in enumerate(chips):
                    over_ici(i, j, w_ref, 2 * px + py, (px, py, c)).wait_recv()
                    passed.append(over_d2d(i, j, 2 * px + py, c))
                    passed[-1].start()
            for i, w_ref in enumerate(w_refs):
                own(i, w_ref).wait_recv()
                for j, (px, py) in enumerate(chips):
                    over_d2d(i, j, 2 * px + py, 1 - c).wait_recv()
            for cp in sends + passed:
                cp.wait_send()

        _at_each_device(run)

    outs = [jax.ShapeDtypeStruct((N_SHARD,) + w.shape, w.dtype) for w in ws]
    return _exchange_call("gather_weight_shards", body, ws, outs, 7 * nw, 1)


def _swap_halves(gs, tag=""):
    ng = len(gs)

    def body(*refs):
        g_refs, got_refs = refs[:ng], refs[ng:2 * ng]
        send_sems, recv_sems, _ = refs[2 * ng:]
        x, y, _ = _place()

        def run(c):
            sends = []
            for i, (g_ref, got_ref) in enumerate(zip(g_refs, got_refs)):
                hr = got_ref.shape[1]
                sends.append(pltpu.make_async_remote_copy(
                    src_ref=g_ref.at[:, pl.ds((1 - c) * hr, hr)], dst_ref=got_ref, send_sem=send_sems.at[i],
                    recv_sem=recv_sems.at[i], device_id=(x, y, 1 - c), device_id_type=MESH))
                sends[-1].start()
            for cp in sends:
                cp.wait()

        _at_each_core(run)

    halves = [jax.ShapeDtypeStruct((g.shape[0], g.shape[1] // 2, g.shape[2]), g.dtype) for g in gs]
    return _exchange_call("swap_grad_halves" + tag, body, gs, halves, ng, 1)


def _scatter_copies(p_refs, out_refs, send_sems, recv_sems, x, y):
    c = lax.axis_index("c")
    copies = []
    for i, (p_ref, out_ref) in enumerate(zip(p_refs, out_refs)):
        for j, (px, py) in enumerate(_other_chips(x, y)):
            copies.append(pltpu.make_async_remote_copy(
                src_ref=p_ref.at[2 * px + py], dst_ref=out_ref.at[j], send_sem=send_sems.at[3 * i + j],
                recv_sem=recv_sems.at[3 * i + j], device_id=(px, py, c), device_id_type=MESH))
    return copies


def _start_all(make, *refs):
    def run(x, y):
        for cp in make(*refs, x, y):
            cp.start()

    _at_each_chip(run)


def _wait_all(make, *refs):
    def run(x, y):
        copies = make(*refs, x, y)
        for cp in copies:
            cp.wait_recv()
        for cp in copies:
            cp.wait_send()

    _at_each_chip(run)


def _scatter_shapes(ps):
    return [jax.ShapeDtypeStruct((3,) + p.shape[1:], p.dtype) for p in ps]


def _carried_scatter(ps):
    return _Carried(ps, _scatter_shapes(ps), 3 * len(ps), functools.partial(_start_all, _scatter_copies),
                    functools.partial(_wait_all, _scatter_copies))


def _direct_gather_copies(w_refs, out_refs, send_sems, recv_sems, x, y, arriving):
    c = lax.axis_index("c")
    me = 2 * x + y
    peers = [((px, py, c), 2 * px + py) for px, py in _other_chips(x, y)] + [((x, y, 1 - c), me)]
    copies = []
    for i, (w_ref, out_ref) in enumerate(zip(w_refs, out_refs)):
        for j, (peer, slab) in enumerate(peers):
            copies.append(pltpu.make_async_remote_copy(
                src_ref=w_ref, dst_ref=out_ref.at[slab if arriving else me], send_sem=send_sems.at[4 * i + j],
                recv_sem=recv_sems.at[4 * i + j], device_id=peer, device_id_type=MESH))
    return copies


def _carried_gather(ws):
    def start(w_refs, out_refs, send_sems, recv_sems):
        def run(x, y):
            for cp in _direct_gather_copies(w_refs, out_refs, send_sems, recv_sems, x, y, False):
                cp.start()

        _at_each_chip(run)

    def finish(w_refs, out_refs, send_sems, recv_sems):
        def run(x, y):
            for cp in _direct_gather_copies(w_refs, out_refs, send_sems, recv_sems, x, y, True):
                cp.wait_recv()
            for cp in _direct_gather_copies(w_refs, out_refs, send_sems, recv_sems, x, y, False):
                cp.wait_send()

        _at_each_chip(run)

    outs = [jax.ShapeDtypeStruct((N_SHARD,) + w.shape, w.dtype) for w in ws]
    return _Carried(ws, outs, 4 * len(ws), start, finish)


def _share_halves(hs):
    n = len(hs)

    def body(*refs):
        h_refs, out_refs = refs[:n], refs[n:2 * n]
        send_sems, recv_sems, _ = refs[2 * n:]
        x, y, c = _place()
        sends = []
        for i, (h_ref, out_ref) in enumerate(zip(h_refs, out_refs)):
            sends.append(pltpu.make_async_remote_copy(
                src_ref=h_ref, dst_ref=out_ref, send_sem=send_sems.at[i], recv_sem=recv_sems.at[i],
                device_id=(x, y, 1 - c), device_id_type=MESH))
            sends[-1].start()
        for cp in sends:
            cp.wait()

    outs = [jax.ShapeDtypeStruct(h.shape, h.dtype) for h in hs]
    return _exchange_call("share_grad_halves", body, hs, outs, n, 1)


def _scalar_grid_call(name, body, scalars, grid, in_specs, out_specs, out_shape, args):
    grid_spec = pltpu.PrefetchScalarGridSpec(num_scalar_prefetch=len(scalars), grid=grid, in_specs=in_specs,
                                             out_specs=out_specs)
    return pl.pallas_call(body, name=name, grid_spec=grid_spec, out_shape=out_shape,
                          compiler_params=_params(("arbitrary",) * len(grid)))(*scalars, *args)


def _add_pair(name, g, got, core):
    ns_, hr, cols = got.shape
    th = _row_tile(hr, 512)
    nb = hr // th

    def body(core_ref, g_ref, got_ref, out_ref):
        out_ref[...] = (g_ref[...].astype(F32) + got_ref[...].astype(F32)).astype(out_ref.dtype)

    blk = pl.BlockSpec((1, th, cols), lambda q, j, core_ref: (q, j, 0))
    own = pl.BlockSpec((1, th, cols), lambda q, j, core_ref: (q, core_ref[0] * nb + j, 0))
    return _scalar_grid_call(name, body, [core], (ns_, nb), [own, blk], blk,
                             jax.ShapeDtypeStruct(got.shape, got.dtype), [g, got])


def _add_chips(name, pairs, slabs, chip):
    _, hr, cols = slabs.shape
    th = _row_tile(hr, 512)

    def body(chip_ref, own_ref, s0_ref, s1_ref, s2_ref, out_ref):
        total = own_ref[0].astype(F32) + s0_ref[0].astype(F32)
        out_ref[...] = (total + s1_ref[0].astype(F32)) + s2_ref[0].astype(F32)

    own = pl.BlockSpec((1, th, cols), lambda j, chip_ref: (chip_ref[0], j, 0))
    others = [pl.BlockSpec((1, th, cols), lambda j, chip_ref, k=k: (k, j, 0)) for k in range(3)]
    return _scalar_grid_call(name, body, [chip], (hr // th,), [own] + others,
                             pl.BlockSpec((th, cols), lambda j, chip_ref: (j, 0)),
                             jax.ShapeDtypeStruct((hr, cols), F32), [pairs, slabs, slabs, slabs])


def _join_halves(name, mine, other, core):
    hr, cols = mine.shape
    th = _row_tile(hr, 512)
    nb = hr // th

    def body(core_ref, mine_ref, other_ref, out_ref):
        is_mine = pl.program_id(0) == core_ref[0]

        @pl.when(is_mine)
        def _():
            out_ref[0] = mine_ref[...]

        @pl.when(jnp.logical_not(is_mine))
        def _():
            out_ref[0] = other_ref[...]

    blk = pl.BlockSpec((th, cols), lambda h, j, core_ref: (j, 0))
    return _scalar_grid_call(name, body, [core], (2, nb), [blk, blk],
                             pl.BlockSpec((1, th, cols), lambda h, j, core_ref: (0, h * nb + j, 0)),
                             jax.ShapeDtypeStruct((1, 2 * hr, cols), mine.dtype), [mine, other])


def _gather_small(sp):
    def body(s_ref, out_ref, send_sems, recv_sems, local_sem):
        x, y, c = _place()
        me = 4 * x + 2 * y + c
        peers = [(x ^ (m >> 2), y ^ ((m >> 1) & 1), c ^ (m & 1)) for m in range(1, 8)]
        mine = pltpu.make_async_copy(s_ref, out_ref.at[me], local_sem)
        mine.start()
        sends = [pltpu.make_async_remote_copy(src_ref=s_ref, dst_ref=out_ref.at[me], send_sem=send_sems.at[j],
                                              recv_sem=recv_sems.at[j], device_id=p, device_id_type=MESH)
                 for j, p in enumerate(peers)]
        for cp in sends:
            cp.start()
        for j, (px, py, pc) in enumerate(peers):
            pltpu.make_async_remote_copy(src_ref=s_ref, dst_ref=out_ref.at[4 * px + 2 * py + pc],
                                         send_sem=send_sems.at[j], recv_sem=recv_sems.at[j], device_id=(px, py, pc),
                                         device_id_type=MESH).wait_recv()
        for cp in sends:
            cp.wait_send()
        mine.wait()

    return pl.pallas_call(
        body, name="gather_small_grads", in_specs=[HBM_SPEC], out_specs=HBM_SPEC,
        out_shape=jax.ShapeDtypeStruct((8,) + sp.shape, sp.dtype),
        scratch_shapes=[pltpu.SemaphoreType.DMA((7,)), pltpu.SemaphoreType.DMA((7,)), pltpu.SemaphoreType.DMA])(sp)


def _pack_rows(total):
    rows = -(-total // LANES)
    return -(-rows // 32) * 32


def _pack(arrs, dtype):
    flat = jnp.concatenate([a.reshape(-1).astype(dtype) for a in arrs])
    rows = _pack_rows(flat.shape[0])
    return jnp.pad(flat, (0, rows * LANES - flat.shape[0])).reshape(rows, LANES)


def _unpack(buf, shapes):
    flat = buf.reshape(-1)
    out, off = {}, 0
    for n, shp in shapes:
        size = shp[0] * shp[1]
        out[n] = flat[off:off + size].reshape(shp)
        off += size
    return out


def _to_wire(name, w3):
    _, r, cols = w3.shape
    tb = _row_tile(r, 512)

    def body(w_ref, o_ref):
        o_ref[...] = w_ref[0].astype(o_ref.dtype)

    return pl.pallas_call(
        body, name=name, grid=(r // tb,), in_specs=[pl.BlockSpec((1, tb, cols), lambda i: (0, i, 0))],
        out_specs=pl.BlockSpec((tb, cols), lambda i: (i, 0)), out_shape=jax.ShapeDtypeStruct((r, cols), MM_DTYPE),
        compiler_params=_params(("arbitrary",)))(w3)


def _adamw(name, w3, g, m3, v3, tb):
    c1 = 1.0 - ADAM_B1 ** ADAM_STEP
    c2 = 1.0 - ADAM_B2 ** ADAM_STEP
    _, r, cols = w3.shape
    emit = g.ndim == 2
    blk3 = pl.BlockSpec((1, tb, cols), lambda i: (0, i, 0))
    g_spec = pl.BlockSpec((tb, cols), lambda i: (i, 0)) if emit else blk3

    def body(w_ref, g_ref, m_ref, v_ref, *out_refs):
        gb = g_ref[...] if emit else g_ref[0]
        m2 = ADAM_B1 * m_ref[0] + (1.0 - ADAM_B1) * gb
        v2 = ADAM_B2 * v_ref[0] + (1.0 - ADAM_B2) * (gb * gb)
        out_refs[-3][0] = -ADAM_LR * ((m2 / c1) / (jnp.sqrt(v2 / c2) + ADAM_EPS) + ADAM_WD * w_ref[0])
        out_refs[-2][0] = m2
        out_refs[-1][0] = v2
        if emit:
            out_refs[0][0] = gb

    n_out = 4 if emit else 3
    outs = pl.pallas_call(
        body, name=name, grid=(r // tb,), in_specs=[blk3, g_spec, blk3, blk3], out_specs=[blk3] * n_out,
        out_shape=[jax.ShapeDtypeStruct((1, r, cols), F32)] * n_out,
        compiler_params=_params(("arbitrary",)))(w3, g, m3, v3)
    return outs if emit else [g] + list(outs)


def _row_tile(rows, pref):
    if rows <= pref:
        return rows
    t = pref
    while t >= 8:
        if rows % t == 0 and t % 8 == 0:
            return t
        t -= 8
    return rows


def kernel(x, positions, ffn1_pre_g, ffn1_w_gate, ffn1_w_up, ffn1_w_down, ffn1_post_g, mix_pre_g, w_in, mla_q_norm_g, mla_w_uq, mla_kv_norm_g, mla_w_ukv, mla_out_g, gdn_conv_w, gdn_a_log, gdn_dt_bias, gdn_norm_g, w_out, mix_post_g, ffn2_pre_g, ffn2_w_gate, ffn2_w_up, ffn2_w_down, ffn2_post_g, loss_target, m_ffn1_pre_g, m_ffn1_w_gate, m_ffn1_w_up, m_ffn1_w_down, m_ffn1_post_g, m_mix_pre_g, m_w_in, m_mla_q_norm_g, m_mla_w_uq, m_mla_kv_norm_g, m_mla_w_ukv, m_mla_out_g, m_gdn_conv_w, m_gdn_a_log, m_gdn_dt_bias, m_gdn_norm_g, m_w_out, m_mix_post_g, m_ffn2_pre_g, m_ffn2_w_gate, m_ffn2_w_up, m_ffn2_w_down, m_ffn2_post_g, v_ffn1_pre_g, v_ffn1_w_gate, v_ffn1_w_up, v_ffn1_w_down, v_ffn1_post_g, v_mix_pre_g, v_w_in, v_mla_q_norm_g, v_mla_w_uq, v_mla_kv_norm_g, v_mla_w_ukv, v_mla_out_g, v_gdn_conv_w, v_gdn_a_log, v_gdn_dt_bias, v_gdn_norm_g, v_w_out, v_mix_post_g, v_ffn2_pre_g, v_ffn2_w_gate, v_ffn2_w_up, v_ffn2_w_down, v_ffn2_post_g):
    args = dict(locals())
    wsh = {n: args[n][0] for n in WEIGHTS}
    msh = {n: args["m_" + n] for n in SMALL}
    vsh = {n: args["v_" + n] for n in SMALL}
    for n in SMALL:
        wsh[n] = args[n]
    mix_shapes = [(n, wsh[n].shape) for n in MIX_BIG]

    early = FFN_BIG[:3]
    held = lambda a, n: jnp.swapaxes(a, 1, 2) if n in TRANSPOSED else a
    w_in_wire = jnp.pad(held(w_in, "w_in")[0].astype(MM_DTYPE), ((0, W_IN_SHARD_PAD - W_IN_SHARD), (0, 0)))
    gathered = _gather_shards([_to_wire("wire_" + n, held(args[n], n)) for n in early]
                              + [w_in_wire, _pack([wsh[n] for n in MIX_BIG], MM_DTYPE)])
    full = {n: wsh[n] for n in SMALL}
    for n, gw in zip(early + ["w_in"], gathered):
        full[n] = gw
    parts = [_unpack(gathered[-1][q], mix_shapes) for q in range(N_SHARD)]
    for n in MIX_BIG:
        full[n] = jnp.concatenate([parts[q][n] for q in range(N_SHARD)], axis=SHARD_AXIS[n])

    core = lax.axis_index("c").astype(jnp.int32).reshape(1)
    chip = (2 * lax.axis_index("x") + lax.axis_index("y")).astype(jnp.int32).reshape(1)
    late = ([_to_wire("wire_" + n, held(args[n], n)) for n in FFN2_BIG], core, chip)
    lsum, grad_x, g = _local_step(x[0], positions, loss_target[0], full, late)
    loss = lax.psum(0.5 * jnp.sum(lsum) / x.shape[-1], ("x", "y", "c"))

    halves = [g[n] for n in FFN_BIG] + [g["w_in"], g["mix_pack"]]
    others = _share_halves(halves)
    shared = [_join_halves("join_halves_%d" % i, hm, ho, core) for i, (hm, ho) in enumerate(zip(halves, others))]
    gsh = _unpack(shared[-1], mix_shapes)
    for n, sg_ in zip(FFN_BIG, shared):
        gsh[n] = sg_
    gsh["w_in"] = shared[-2][:, :W_IN_SHARD]

    small_shapes = [(n, wsh[n].shape) for n in SMALL]
    pack_small = lambda d: jnp.concatenate(
        [_pad_lanes(d[n].astype(F32), LANES) for n in SMALL] + [jnp.zeros((SMALL_ROWS - len(SMALL), LANES), F32)], axis=0)
    slots = _gather_small(pack_small(g))

    c1 = 1.0 - ADAM_B1 ** ADAM_STEP
    c2 = 1.0 - ADAM_B2 ** ADAM_STEP

    def small_update(wb, mb, vb, s8):
        gs = s8[0:SMALL_ROWS]
        for d in range(1, 8):
            gs = gs + s8[d * SMALL_ROWS:(d + 1) * SMALL_ROWS]
        m2 = ADAM_B1 * mb + (1.0 - ADAM_B1) * gs
        v2 = ADAM_B2 * vb + (1.0 - ADAM_B2) * (gs * gs)
        delta = -ADAM_LR * ((m2 / c1) / (jnp.sqrt(v2 / c2) + ADAM_EPS) + ADAM_WD * wb)
        return gs, delta, m2, v2

    sg, sd, sm, sv_ = _rowwise("adamw_small", small_update,
                               [pack_small(wsh), pack_small(msh), pack_small(vsh)],
                               [slots.reshape(8 * SMALL_ROWS, LANES)], [(LANES, F32)] * 4, [], SMALL_ROWS)
    grads, deltas, new_m, new_v = {}, {}, {}, {}
    for i, (n, shp) in enumerate(small_shapes):
        grads[n], deltas[n] = sg[i:i + 1, :shp[1]], sd[i:i + 1, :shp[1]]
        new_m[n], new_v[n] = sm[i:i + 1, :shp[1]], sv_[i:i + 1, :shp[1]]
    for n in BIG:
        w3 = held(args[n], n)
        outs = _adamw("adamw_" + n, w3, gsh[n], held(args["m_" + n], n), held(args["v_" + n], n),
                      _row_tile(w3.shape[1], 256))
        grads[n], deltas[n], new_m[n], new_v[n] = [held(o, n) for o in outs]

    return (loss, grad_x[None], *[grads[n] for n in WEIGHTS], *[deltas[n] for n in WEIGHTS],
            *[new_m[n] for n in WEIGHTS], *[new_v[n] for n in WEIGHTS])
```

```python
import functools

import jax
import jax.numpy as jnp
from jax import lax
from jax.experimental import pallas as pl
from jax.experimental.pallas import tpu as pltpu

F32 = jnp.float32
BF16 = jnp.bfloat16
MM_DTYPE = BF16
MESH = pl.DeviceIdType.MESH

D_MODEL = 1024
D_FF = 2816
N_HEADS = 8
MLA_Q_RANK = 256
MLA_KV_RANK = 128
MLA_NOPE = 64
MLA_ROPE = 32
MLA_V = 64
ROPE_THETA = 10000.0
GDN_DH = 64
GDN_W = N_HEADS * GDN_DH
GDN_CONV = 4
CHUNK = 64
HEAD_LANES = 128
HEADS_PER_STEP = 4
MLA_PAD = N_HEADS * HEAD_LANES
EPS = 1e-6
N_SHARD = 4
LANES = 1024

PIN_QKV = 0
PIN_MLA = 1536
PIN_KPE = 1920
PIN_GATE = 2048
PIN_AB = 2560
PIN_W = 2688
CAT_W = MLA_PAD + GDN_W

ADAM_LR = 0.001
ADAM_B1 = 0.9
ADAM_B2 = 0.999
ADAM_EPS = 1e-08
ADAM_WD = 0.01
ADAM_STEP = 10

VMEM_LIMIT_V7X = 56 * 1024 * 1024

BIG = ["ffn1_w_gate", "ffn1_w_up", "ffn1_w_down", "w_in", "mla_w_uq", "mla_w_ukv", "gdn_conv_w", "w_out",
       "ffn2_w_gate", "ffn2_w_up", "ffn2_w_down"]
FFN_BIG = ["ffn1_w_gate", "ffn1_w_up", "ffn1_w_down", "ffn2_w_gate", "ffn2_w_up", "ffn2_w_down"]
TRANSPOSED = ["ffn1_w_gate", "ffn1_w_up", "ffn2_w_gate", "ffn2_w_up", "w_in"]
MIX_BIG = ["mla_w_uq", "mla_w_ukv", "gdn_conv_w", "w_out"]
SMALL = ["ffn1_pre_g", "ffn1_post_g", "mix_pre_g", "mla_q_norm_g", "mla_kv_norm_g", "mla_out_g", "gdn_a_log",
         "gdn_dt_bias", "gdn_norm_g", "mix_post_g", "ffn2_pre_g", "ffn2_post_g"]
WEIGHTS = ["ffn1_pre_g", "ffn1_w_gate", "ffn1_w_up", "ffn1_w_down", "ffn1_post_g", "mix_pre_g", "w_in",
           "mla_q_norm_g", "mla_w_uq", "mla_kv_norm_g", "mla_w_ukv", "mla_out_g", "gdn_conv_w", "gdn_a_log",
           "gdn_dt_bias", "gdn_norm_g", "w_out", "mix_post_g", "ffn2_pre_g", "ffn2_w_gate", "ffn2_w_up",
           "ffn2_w_down", "ffn2_post_g"]
SHARD_AXIS = {"ffn1_w_gate": 1, "ffn1_w_up": 1, "ffn1_w_down": 0, "w_in": 1, "mla_w_uq": 1, "mla_w_ukv": 1,
              "gdn_conv_w": 1, "w_out": 0, "ffn2_w_gate": 1, "ffn2_w_up": 1, "ffn2_w_down": 0}
SMALL_ROWS = 16


def _params(sem):
    return pltpu.CompilerParams(dimension_semantics=sem, vmem_limit_bytes=VMEM_LIMIT_V7X)


def _pick(dim, pref):
    if dim <= pref:
        return dim
    t = (pref // 128) * 128
    while t >= 128:
        if dim % t == 0:
            return t
        t -= 128
    return dim


ANY_SPEC = pl.BlockSpec(memory_space=pl.ANY)


def _rowwise(name, fn, row_ins, bc_ins, row_outs, acc_outs, tb, wide=None, carry=None):
    ents = []
    for e in row_ins:
        ents.append(e if isinstance(e, tuple) else (e, e.shape[1], 0, 0))
    over = [o[2] for o in row_outs if len(o) == 3]
    rows = over[0] if over else ents[0][0].shape[0]
    steps = rows // tb
    assert steps * tb == rows, (name, rows, tb)
    in_specs, args = [], []
    for a, w, j, r0 in ents:
        in_specs.append(pl.BlockSpec((tb, w), lambda i, j=j, r0=r0: (i + r0, j)))
        args.append(a)
    for b in bc_ins:
        in_specs.append(pl.BlockSpec(b.shape, lambda i: (0, 0)))
        args.append(b)
    n_in = len(args)
    aliases = {}
    if carry is not None:
        in_specs.append(ANY_SPEC)
        args.append(carry)
        aliases = {n_in: 0}
    out_shape = [jax.ShapeDtypeStruct((rows, o[0]), o[1]) for o in row_outs]
    out_specs = [pl.BlockSpec((tb, o[0]), lambda i: (i, 0)) for o in row_outs]
    if wide is not None:
        out_shape[0] = jax.ShapeDtypeStruct((rows, wide[0]), row_outs[0][1])
        out_specs[0] = pl.BlockSpec((tb, row_outs[0][0]), lambda i: (i, wide[1]))
    out_shape += [jax.ShapeDtypeStruct((r, c), F32) for r, c in acc_outs]
    out_specs += [pl.BlockSpec((r, c), lambda i: (0, 0)) for r, c in acc_outs]
    n_ro, n_acc, n_args = len(row_outs), len(acc_outs), len(args)

    def body(*refs):
        vals = fn(*[r[...] for r in refs[:n_in]])
        if not isinstance(vals, (tuple, list)):
            vals = (vals,)
        for r, v in zip(refs[n_args:n_args + n_ro], vals[:n_ro]):
            r[...] = v.astype(r.dtype)
        if n_acc:
            acc_refs = refs[n_args + n_ro:]

            @pl.when(pl.program_id(0) == 0)
            def _():
                for r in acc_refs:
                    r[...] = jnp.zeros(r.shape, r.dtype)

            for r, v in zip(acc_refs, vals[n_ro:]):
                r[...] += v

    outs = pl.pallas_call(body, name=name, grid=(steps,), in_specs=in_specs, out_specs=out_specs,
                          out_shape=out_shape, input_output_aliases=aliases,
                          compiler_params=_params(("arbitrary",)))(*args)
    return outs


def _mm(name, a, b, mode, out_dtype, tm=1024, tn=1024, tk=1024):
    if mode == "nn":
        (m, k), (k2, n) = a.shape, b.shape
    elif mode == "nt":
        (m, k), (n, k2) = a.shape, b.shape
    else:
        (k, m), (k2, n) = a.shape, b.shape
    assert k == k2, (name, a.shape, b.shape)
    tm, tn, tk = _pick(m, tm), _pick(n, tn), _pick(k, tk)
    nk = k // tk
    if mode == "nn":
        a_spec = pl.BlockSpec((tm, tk), lambda i, j, kk: (i, kk))
        b_spec = pl.BlockSpec((tk, tn), lambda i, j, kk: (kk, j))
        dims = (((1,), (0,)), ((), ()))
    elif mode == "nt":
        a_spec = pl.BlockSpec((tm, tk), lambda i, j, kk: (i, kk))
        b_spec = pl.BlockSpec((tn, tk), lambda i, j, kk: (j, kk))
        dims = (((1,), (1,)), ((), ()))
    else:
        a_spec = pl.BlockSpec((tk, tm), lambda i, j, kk: (kk, i))
        b_spec = pl.BlockSpec((tk, tn), lambda i, j, kk: (kk, j))
        dims = (((0,), (0,)), ((), ()))

    def body(a_ref, b_ref, o_ref, acc_ref):
        kk = pl.program_id(2)

        @pl.when(kk == 0)
        def _():
            acc_ref[...] = jnp.zeros(acc_ref.shape, F32)

        acc_ref[...] += lax.dot_general(a_ref[...].astype(MM_DTYPE), b_ref[...].astype(MM_DTYPE), dims,
                                        preferred_element_type=F32)

        @pl.when(kk == nk - 1)
        def _():
            o_ref[...] = acc_ref[...].astype(o_ref.dtype)

    return pl.pallas_call(
        body, name=name, grid=(m // tm, n // tn, nk), in_specs=[a_spec, b_spec],
        out_specs=pl.BlockSpec((tm, tn), lambda i, j, kk: (i, j)),
        out_shape=jax.ShapeDtypeStruct((m, n), out_dtype),
        scratch_shapes=[pltpu.VMEM((tm, tn), F32)],
        compiler_params=_params(("parallel", "parallel", "arbitrary")))(a, b)


def _rms_stats(x, n_real=None):
    n = x.shape[-1] if n_real is None else n_real
    return lax.rsqrt(jnp.sum(x * x, axis=-1, keepdims=True) / n + EPS)


def _rms_bwd(x, r, g, dz, n_real=None):
    n = x.shape[-1] if n_real is None else n_real
    xh = x * r
    dxh = dz * g
    dx = r * (dxh - xh * (jnp.sum(dxh * xh, axis=-1, keepdims=True) / n))
    return dx, jnp.sum(dz * xh, axis=0, keepdims=True)


def _sigmoid(x):
    return 0.5 * jnp.tanh(0.5 * x) + 0.5


def _roll(x, s, axis):
    return pltpu.roll(x, s, axis)


def _rope(x, c, s1, s2):
    return x * c + _roll(x, HEAD_LANES - MLA_ROPE // 2, 1) * s1 + _roll(x, MLA_ROPE // 2, 1) * s2


def _heads_apply(x, fn):
    return jnp.concatenate([fn(x[:, h * HEAD_LANES:(h + 1) * HEAD_LANES]) for h in range(N_HEADS)], axis=1)


ROW_CHUNK = 256


def _row_chunks(rows):
    step = min(ROW_CHUNK, rows)
    return [pl.ds(r, step) for r in range(0, rows, step)]


def _ffn_fwd(tag, x, g_pre, wg, wu, wd, g_post, tm):
    t, d = x.shape
    ns, fs, _ = wg.shape
    nt = t // tm
    row = pl.BlockSpec((tm, d), lambda i, q: (i, 0))
    vec = pl.BlockSpec((1, d), lambda i, q: (0, 0))
    act3 = pl.BlockSpec((1, tm, fs), lambda i, q: (q, i, 0))
    wrow = pl.BlockSpec((1, fs, d), lambda i, q: (q, 0, 0))
    nt_dims = (((1,), (1,)), ((), ()))

    def gate_up(x_ref, g_ref, wg_ref, wu_ref, n_ref, sl_ref, ud_ref, s_ref, n_s):
        @pl.when(pl.program_id(1) == 0)
        def _():
            for r in _row_chunks(tm):
                xb = x_ref[r, :]
                n_s[r, :] = (xb * _rms_stats(xb) * g_ref[...]).astype(MM_DTYPE)
            n_ref[...] = n_s[...]

        for r in _row_chunks(tm):
            n = n_s[r, :]
            a = lax.dot_general(n, wg_ref[0], nt_dims, preferred_element_type=F32)
            u = lax.dot_general(n, wu_ref[0], nt_dims, preferred_element_type=F32)
            sg = _sigmoid(a)
            sl = a * sg
            sl_ref[0, r, :] = sl.astype(sl_ref.dtype)
            ud_ref[0, r, :] = (u * (sg + sl * (1.0 - sg))).astype(ud_ref.dtype)
            s_ref[0, r, :] = (sl * u).astype(s_ref.dtype)

    n, sl, ud, s = pl.pallas_call(
        gate_up, name=tag + "_gate_up", grid=(nt, ns), in_specs=[row, vec, wrow, wrow],
        out_specs=[row, act3, act3, act3],
        out_shape=[jax.ShapeDtypeStruct((t, d), MM_DTYPE)] + [jax.ShapeDtypeStruct((ns, t, fs), MM_DTYPE)] * 3,
        scratch_shapes=[pltpu.VMEM((tm, d), MM_DTYPE)],
        compiler_params=_params(("parallel", "arbitrary")))(x, g_pre, wg, wu)

    def down(s_ref, wd_ref, x_ref, g_ref, h_ref, y_ref, acc):
        q = pl.program_id(1)

        @pl.when(q == 0)
        def _():
            acc[...] = jnp.zeros(acc.shape, F32)

        for r in _row_chunks(tm):
            acc[r, :] += jnp.dot(s_ref[0, r, :], wd_ref[0], preferred_element_type=F32)

        @pl.when(q == ns - 1)
        def _():
            for r in _row_chunks(tm):
                hb = acc[r, :]
                h_ref[r, :] = hb
                y_ref[r, :] = x_ref[r, :] + 0.5 * (hb * _rms_stats(hb) * g_ref[...])

    h, y = pl.pallas_call(
        down, name=tag + "_down", grid=(nt, ns), in_specs=[act3, wrow, row, vec], out_specs=[row, row],
        out_shape=[jax.ShapeDtypeStruct((t, d), F32)] * 2, scratch_shapes=[pltpu.VMEM((tm, d), F32)],
        compiler_params=_params(("parallel", "arbitrary")))(s, wd, x, g_post)
    return y, (x, n, sl, ud, s, h)


def _carry(body, n_in, n_out, grid, carried):
    if carried is None:
        return body, [], [], [], [], []
    nx_in, nx_out = len(carried.ins), len(carried.outs)

    def wrapped(*refs):
        ins, rest = refs[:n_in], refs[n_in:]
        xi, rest = rest[:nx_in], rest[nx_in:]
        outs, rest = rest[:n_out], rest[n_out:]
        xo, rest = rest[:nx_out], rest[nx_out:]
        scr, sems = rest[:len(rest) - 2], rest[len(rest) - 2:]
        first, last = True, True
        for dim, size in enumerate(grid):
            first = first & (pl.program_id(dim) == 0)
            last = last & (pl.program_id(dim) == size - 1)

        @pl.when(first)
        def _():
            carried.start(xi, xo, *sems)

        body(*ins, *outs, *scr)

        @pl.when(last)
        def _():
            carried.finish(xi, xo, *sems)

    sems = [pltpu.SemaphoreType.DMA((carried.n_sem,)), pltpu.SemaphoreType.DMA((carried.n_sem,))]
    return (wrapped, [HBM_SPEC] * nx_in, [HBM_SPEC] * nx_out, list(carried.outs), sems, list(carried.ins))


def _ffn_bwd(tag, dy, saved, g_pre, wg, wu, wd, g_post, tm, tk, carried_down=None, make_carried_mid=None,
             make_carried_up=None):
    x, n, sl, ud, s, h = saved
    t, d = x.shape
    ns, fs, _ = wg.shape
    nt, nk = t // tm, t // tk
    row = pl.BlockSpec((tm, d), lambda i, q: (i, 0))
    vec = pl.BlockSpec((1, d), lambda i, q: (0, 0))
    act3 = pl.BlockSpec((1, tm, fs), lambda i, q: (q, i, 0))
    wrow = pl.BlockSpec((1, fs, d), lambda i, q: (q, 0, 0))
    nt_dims = (((1,), (1,)), ((), ()))
    tn_dims = (((0,), (0,)), ((), ()))

    def down_b(h_ref, dy_ref, g_ref, wd_ref, sl_ref, ud_ref, dh_ref, da_ref, du_ref, dg_ref, dh_s):
        i, q = pl.program_id(0), pl.program_id(1)

        @pl.when((i == 0) & (q == 0))
        def _():
            dg_ref[...] = jnp.zeros(dg_ref.shape, F32)

        @pl.when(q == 0)
        def _():
            for r in _row_chunks(tm):
                hb = h_ref[r, :]
                dh, dg = _rms_bwd(hb, _rms_stats(hb), g_ref[...], 0.5 * dy_ref[r, :])
                dh_s[r, :] = dh.astype(MM_DTYPE)
                dg_ref[...] += dg
            dh_ref[...] = dh_s[...]

        for r in _row_chunks(tm):
            ds = lax.dot_general(dh_s[r, :], wd_ref[0], nt_dims, preferred_element_type=F32)
            da_ref[0, r, :] = (ds * ud_ref[0, r, :].astype(F32)).astype(da_ref.dtype)
            du_ref[0, r, :] = (ds * sl_ref[0, r, :].astype(F32)).astype(du_ref.dtype)

    down_b, x_in, x_out, x_shape, x_scr, x_args = _carry(down_b, 6, 4, (nt, ns), carried_down)
    dh, da, du, dg_post, *from_down = pl.pallas_call(
        down_b, name=tag + "_down_b", grid=(nt, ns), in_specs=[row, row, vec, wrow, act3, act3] + x_in,
        out_specs=[row, act3, act3, vec] + x_out,
        out_shape=[jax.ShapeDtypeStruct((t, d), MM_DTYPE)] + [jax.ShapeDtypeStruct((ns, t, fs), MM_DTYPE)] * 2
        + [jax.ShapeDtypeStruct((1, d), F32)] + x_shape,
        scratch_shapes=[pltpu.VMEM((tm, d), MM_DTYPE)] + x_scr,
        compiler_params=_params(("arbitrary", "arbitrary")))(h, dy, g_post, wd, sl, ud, *x_args)

    def down_w(s_ref, dh_ref, dw_ref, acc):
        kk = pl.program_id(1)

        @pl.when(kk == 0)
        def _():
            acc[...] = jnp.zeros(acc.shape, F32)

        acc[...] += lax.dot_general(s_ref[0], dh_ref[...], tn_dims, preferred_element_type=F32)

        @pl.when(kk == nk - 1)
        def _():
            dw_ref[0] = acc[...].astype(dw_ref.dtype)

    dwd = pl.pallas_call(
        down_w, name=tag + "_down_w", grid=(ns, nk),
        in_specs=[pl.BlockSpec((1, tk, fs), lambda q, kk: (q, kk, 0)), pl.BlockSpec((tk, d), lambda q, kk: (kk, 0))],
        out_specs=pl.BlockSpec((1, fs, d), lambda q, kk: (q, 0, 0)),
        out_shape=jax.ShapeDtypeStruct((ns, fs, d), MM_DTYPE), scratch_shapes=[pltpu.VMEM((fs, d), F32)],
        compiler_params=_params(("parallel", "arbitrary")))(s, dh)

    def gate_up_b(da_ref, du_ref, wg_ref, wu_ref, x_ref, dy_ref, g_ref, dx_ref, dg_ref, acc):
        i, q = pl.program_id(0), pl.program_id(1)

        @pl.when((i == 0) & (q == 0))
        def _():
            dg_ref[...] = jnp.zeros(dg_ref.shape, F32)

        @pl.when(q == 0)
        def _():
            acc[...] = jnp.zeros(acc.shape, F32)

        for r in _row_chunks(tm):
            acc[r, :] += (jnp.dot(da_ref[0, r, :], wg_ref[0], preferred_element_type=F32)
                          + jnp.dot(du_ref[0, r, :], wu_ref[0], preferred_element_type=F32))

        @pl.when(q == ns - 1)
        def _():
            for r in _row_chunks(tm):
                xb = x_ref[r, :]
                dx, dg = _rms_bwd(xb, _rms_stats(xb), g_ref[...], acc[r, :])
                dx_ref[r, :] = dy_ref[r, :] + dx
                dg_ref[...] += dg

    def gate_up_w(n_ref, da_ref, du_ref, dwg_ref, dwu_ref, acc_g, acc_u):
        kk = pl.program_id(1)

        @pl.when(kk == 0)
        def _():
            acc_g[...] = jnp.zeros(acc_g.shape, F32)
            acc_u[...] = jnp.zeros(acc_u.shape, F32)

        nb = n_ref[...]
        acc_g[...] += lax.dot_general(da_ref[0], nb, tn_dims, preferred_element_type=F32)
        acc_u[...] += lax.dot_general(du_ref[0], nb, tn_dims, preferred_element_type=F32)

        @pl.when(kk == nk - 1)
        def _():
            dwg_ref[0] = acc_g[...].astype(dwg_ref.dtype)
            dwu_ref[0] = acc_u[...].astype(dwu_ref.dtype)

    k3 = pl.BlockSpec((1, tk, fs), lambda q, kk: (q, kk, 0))
    wout = pl.BlockSpec((1, fs, d), lambda q, kk: (q, 0, 0))
    carried_mid = make_carried_mid(dwd) if make_carried_mid else None
    gate_up_w, x_in, x_out, x_shape, x_scr, x_args = _carry(gate_up_w, 3, 2, (ns, nk), carried_mid)
    dwg, dwu, *from_mid = pl.pallas_call(
        gate_up_w, name=tag + "_gate_up_w", grid=(ns, nk),
        in_specs=[pl.BlockSpec((tk, d), lambda q, kk: (kk, 0)), k3, k3] + x_in, out_specs=[wout, wout] + x_out,
        out_shape=[jax.ShapeDtypeStruct((ns, fs, d), MM_DTYPE)] * 2 + x_shape,
        scratch_shapes=[pltpu.VMEM((fs, d), F32)] * 2 + x_scr,
        compiler_params=_params(("arbitrary", "arbitrary")))(n, da, du, *x_args)

    carried_up = make_carried_up(dwg, dwu) if make_carried_up else None
    gate_up_b, x_in, x_out, x_shape, x_scr, x_args = _carry(gate_up_b, 7, 2, (nt, ns), carried_up)
    dx, dg_pre, *from_up = pl.pallas_call(
        gate_up_b, name=tag + "_gate_up_b", grid=(nt, ns), in_specs=[act3, act3, wrow, wrow, row, row, vec] + x_in,
        out_specs=[row, vec] + x_out,
        out_shape=[jax.ShapeDtypeStruct((t, d), F32), jax.ShapeDtypeStruct((1, d), F32)] + x_shape,
        scratch_shapes=[pltpu.VMEM((tm, d), F32)] + x_scr,
        compiler_params=_params(("arbitrary", "arbitrary")))(da, du, wg, wu, x, dy, g_pre, *x_args)
    return dx, dg_pre, dwg, dwu, dwd, dg_post, from_down, from_mid, from_up


NEG = -1e30


def _attn_scale():
    return (MLA_NOPE + MLA_ROPE) ** -0.5


def _causal_pairs(nq, by_key):
    if by_key:
        pairs = [(qi, ki) for ki in range(nq) for qi in range(ki, nq)]
    else:
        pairs = [(qi, ki) for qi in range(nq) for ki in range(qi + 1)]
    return jnp.asarray([p[0] for p in pairs], jnp.int32), jnp.asarray([p[1] for p in pairs], jnp.int32)


def _below_diagonal(shape):
    return lax.broadcasted_iota(jnp.int32, shape, 1) <= lax.broadcasted_iota(jnp.int32, shape, 0)


def _attn_call(name, body, tables, args, in_kinds, out_kinds, scratch, t, tq, carried=None):
    qmap = lambda h, p, qt, kt: (qt[p], h)
    kmap = lambda h, p, qt, kt: (kt[p], h)
    width = HEADS_PER_STEP * HEAD_LANES
    spec = lambda kind: pl.BlockSpec((tq, width), qmap if kind == "q" else kmap)
    n_pairs = tables[0].shape[0]
    n_groups = N_HEADS // HEADS_PER_STEP
    n_in, n_out, n_scr = len(in_kinds), len(out_kinds), scratch
    x_ins = list(carried.ins) if carried else []
    x_outs = list(carried.outs) if carried else []
    x_scr = [pltpu.SemaphoreType.DMA((carried.n_sem,)), pltpu.SemaphoreType.DMA((carried.n_sem,))] if carried else []

    def full_body(qt, kt, *refs):
        ins, refs = refs[:n_in], refs[n_in:]
        xi, refs = refs[:len(x_ins)], refs[len(x_ins):]
        outs, refs = refs[:n_out], refs[n_out:]
        xo, refs = refs[:len(x_outs)], refs[len(x_outs):]
        scr, sems = refs[:n_scr], refs[n_scr:]
        if carried:
            @pl.when((pl.program_id(0) == 0) & (pl.program_id(1) == 0))
            def _():
                carried.start(xi, xo, *sems)

        heads = [tuple(r.at[:, pl.ds(hh * HEAD_LANES, HEAD_LANES)] for r in (*ins, *outs, *scr))
                 for hh in range(HEADS_PER_STEP)]
        body(qt, kt, heads)
        if carried:
            @pl.when((pl.program_id(0) == n_groups - 1) & (pl.program_id(1) == n_pairs - 1))
            def _():
                carried.finish(xi, xo, *sems)

    grid_spec = pltpu.PrefetchScalarGridSpec(
        num_scalar_prefetch=2, grid=(n_groups, n_pairs),
        in_specs=[spec(kd) for kd in in_kinds] + [HBM_SPEC] * len(x_ins),
        out_specs=[spec(kd) for kd in out_kinds] + [HBM_SPEC] * len(x_outs),
        scratch_shapes=[pltpu.VMEM((tq, width), F32)] * n_scr + x_scr)
    return pl.pallas_call(full_body, name=name, grid_spec=grid_spec,
                          out_shape=[jax.ShapeDtypeStruct((t, MLA_PAD), F32) for _ in out_kinds] + x_outs,
                          compiler_params=_params(("arbitrary", "arbitrary")))(*tables, *args, *x_ins)


class _Carried:
    def __init__(self, ins, outs, n_sem, start, finish):
        self.ins, self.outs, self.n_sem, self.start, self.finish = ins, outs, n_sem, start, finish


def _attn_fwd(q, k, v, tq, carried=None):
    t = q.shape[0]
    nq = t // tq

    def body(qt, kt, heads):
        p_id = pl.program_id(1)
        qi, ki = qt[p_id], kt[p_id]

        @pl.when(ki == 0)
        def _():
            for _, _, _, _, _, m_s, l_s, acc_s in heads:
                m_s[...] = jnp.full(m_s.shape, NEG, F32)
                l_s[...] = jnp.zeros(l_s.shape, F32)
                acc_s[...] = jnp.zeros(acc_s.shape, F32)

        def update(diagonal):
            for q_ref, k_ref, v_ref, _, _, m_s, l_s, acc_s in heads:
                s = lax.dot_general(q_ref[...], k_ref[...], (((1,), (1,)), ((), ())), preferred_element_type=F32)
                if diagonal:
                    s = jnp.where(_below_diagonal(s.shape), s, NEG)
                m_old = m_s[...]
                m_new = jnp.maximum(m_old, jnp.max(s, axis=1, keepdims=True))
                alpha = jnp.exp(m_old - m_new)
                p = jnp.exp(s - m_new[:, :1])
                l_s[...] = l_s[...] * alpha + jnp.sum(p, axis=1, keepdims=True)
                acc_s[...] = acc_s[...] * alpha + jnp.dot(p.astype(MM_DTYPE), v_ref[...], preferred_element_type=F32)
                m_s[...] = m_new

        @pl.when(ki < qi)
        def _():
            update(False)

        @pl.when(ki == qi)
        def _():
            update(True)
            for _, _, _, o_ref, lse_ref, m_s, l_s, acc_s in heads:
                o_ref[...] = acc_s[...] / l_s[...]
                lse_ref[...] = m_s[...] + jnp.log(l_s[...])

    return _attn_call("mla_attn_fwd", body, _causal_pairs(nq, False), (q, k, v), "qkk", "qq", 3, t, tq, carried)


def _attn_probs(q, k, lse, diagonal):
    s = lax.dot_general(q, k, (((1,), (1,)), ((), ())), preferred_element_type=F32)
    p = jnp.exp(s - lse[:, :1])
    return jnp.where(_below_diagonal(s.shape), p, 0.0) if diagonal else p


BWD_HEADS = 2


def _attn_bwd(q, k, v, do, lse, delta, tq, carried=None):
    t = q.shape[0]
    nq = t // tq
    width = BWD_HEADS * HEAD_LANES
    n_groups = N_HEADS // BWD_HEADS
    qt_tab, kt_tab = _causal_pairs(nq, True)
    n_pairs = qt_tab.shape[0]
    qmap = lambda h, p, qt, kt: (qt[p], h)
    kmap = lambda h, p, qt, kt: (kt[p], h)
    qs, ks = pl.BlockSpec((tq, width), qmap), pl.BlockSpec((tq, width), kmap)
    x_ins = list(carried.ins) if carried else []
    x_outs = list(carried.outs) if carried else []
    x_scr = [pltpu.SemaphoreType.DMA((carried.n_sem,)), pltpu.SemaphoreType.DMA((carried.n_sem,))] if carried else []
    nt_dims = (((1,), (1,)), ((), ()))
    tn_dims = (((0,), (0,)), ((), ()))

    def body(qt, kt, q_ref, k_ref, v_ref, do_ref, lse_ref, dl_ref, *rest):
        xi, rest = rest[:len(x_ins)], rest[len(x_ins):]
        dq_hbm, dk_ref, dv_ref = rest[:3]
        xo, rest = rest[3:3 + len(x_outs)], rest[3 + len(x_outs):]
        dk_s, dv_s, dq_s, dq_sem = rest[:4]
        sems = rest[4:]
        grp, p_id = pl.program_id(0), pl.program_id(1)
        qi, ki = qt[p_id], kt[p_id]
        if carried:
            @pl.when((grp == 0) & (p_id == 0))
            def _():
                carried.start(xi, xo, *sems)

        @pl.when(p_id == 0)
        def _():
            dq_s[...] = jnp.zeros(dq_s.shape, F32)

        def step(diagonal):
            rows = pl.ds(pl.multiple_of(qi * tq, tq), tq)
            for hh in range(BWD_HEADS):
                ln = pl.ds(hh * HEAD_LANES, HEAD_LANES)
                qb, kb, vb, dob = q_ref[:, ln], k_ref[:, ln], v_ref[:, ln], do_ref[:, ln]
                p = _attn_probs(qb, kb, lse_ref[:, ln], diagonal)
                dv_s[:, ln] += lax.dot_general(p.astype(MM_DTYPE), dob, tn_dims, preferred_element_type=F32)
                dp = lax.dot_general(dob, vb, nt_dims, preferred_element_type=F32)
                ds = (p * (dp - dl_ref[:, ln][:, :1])).astype(MM_DTYPE)
                dk_s[:, ln] += lax.dot_general(ds, qb, tn_dims, preferred_element_type=F32)
                dq_s[rows, ln] += jnp.dot(ds, kb, preferred_element_type=F32)

        @pl.when(qi == ki)
        def _():
            dk_s[...] = jnp.zeros(dk_s.shape, F32)
            dv_s[...] = jnp.zeros(dv_s.shape, F32)
            step(True)

        @pl.when(qi > ki)
        def _():
            step(False)

        @pl.when(qi == nq - 1)
        def _():
            dk_ref[...] = dk_s[...]
            dv_ref[...] = dv_s[...]

        @pl.when(p_id == n_pairs - 1)
        def _():
            out = pltpu.make_async_copy(dq_s, dq_hbm.at[pl.ds(pl.multiple_of(grp * t, t), t)], dq_sem)
            out.start()
            out.wait()

        if carried:
            @pl.when((grp == n_groups - 1) & (p_id == n_pairs - 1))
            def _():
                carried.finish(xi, xo, *sems)

    grid_spec = pltpu.PrefetchScalarGridSpec(
        num_scalar_prefetch=2, grid=(n_groups, n_pairs),
        in_specs=[qs, ks, ks, qs, qs, qs] + [HBM_SPEC] * len(x_ins),
        out_specs=[HBM_SPEC, ks, ks] + [HBM_SPEC] * len(x_outs),
        scratch_shapes=[pltpu.VMEM((tq, width), F32), pltpu.VMEM((tq, width), F32), pltpu.VMEM((t, width), F32),
                        pltpu.SemaphoreType.DMA] + x_scr)
    return pl.pallas_call(
        body, name="mla_attn_bwd", grid_spec=grid_spec,
        out_shape=[jax.ShapeDtypeStruct((n_groups * t, width), F32), jax.ShapeDtypeStruct((t, MLA_PAD), F32),
                   jax.ShapeDtypeStruct((t, MLA_PAD), F32)] + x_outs,
        compiler_params=_params(("arbitrary", "arbitrary")))(qt_tab, kt_tab, q, k, v, do, lse, delta, *x_ins)


def _dot01(a, b, dims=(((1,), (0,)), ((), ())), ones="rhs"):
    val, sel = (a, b) if ones == "rhs" else (b, a)
    head = val.astype(BF16)
    tail = (val - head.astype(F32)).astype(BF16)
    sel = sel.astype(BF16)
    dot = lambda part: (lax.dot_general(part, sel, dims, preferred_element_type=F32) if ones == "rhs"
                        else lax.dot_general(sel, part, dims, preferred_element_type=F32))
    return dot(head) + dot(tail)


def _dot1(a, b, dims=(((1,), (0,)), ((), ()))):
    return lax.dot_general(a.astype(MM_DTYPE), b.astype(MM_DTYPE), dims, preferred_element_type=F32)


def _dot3(a, b, dims=(((1,), (0,)), ((), ()))):
    return lax.dot_general(a, b, dims, preferred_element_type=F32, precision=lax.Precision.HIGH)


NN3 = (((2,), (1,)), ((0,), (0,)))
NT3 = (((2,), (2,)), ((0,), (0,)))
TN3 = (((1,), (1,)), ((0,), (0,)))


def _tri_masks(nh):
    shape = (nh, CHUNK, CHUNK)
    return lax.broadcasted_iota(jnp.int32, shape, 1), lax.broadcasted_iota(jnp.int32, shape, 2)


def _gdn_chunk_common(k, gcc, bb, row, col, dot=_dot1):
    tril = row >= col
    gcr = jnp.swapaxes(gcc, 1, 2)
    dm = jnp.exp(jnp.where(tril, gcc - gcr, NEG))
    kb = k * bb
    lm = jnp.where(row > col, dot(kb, k, NT3) * dm, 0.0)
    return dm, kb, lm


def _unit_lower_inverse(lm, eye):
    t = eye - lm
    p = lm
    for _ in range(CHUNK.bit_length() - 2):
        p = _dot3(p, p, NN3)
        t = t + _dot3(t, p, NN3)
    return t


def _chunk_sum_matrix(tb, upper):
    r = lax.broadcasted_iota(jnp.int32, (tb, tb), 0)
    c = lax.broadcasted_iota(jnp.int32, (tb, tb), 1)
    same = (r // CHUNK) == (c // CHUNK)
    return (same & ((c >= r) if upper else (c <= r))).astype(F32)


def _gdn_fwd(q, k, v, gb, bb, carried=None):
    nh, t, dh = q.shape
    nchunk = t // CHUNK

    def body(q_ref, k_ref, v_ref, g_ref, b_ref, o_ref, sall_ref, tall_ref, s_s):
        @pl.when(pl.program_id(0) == 0)
        def _():
            s_s[...] = jnp.zeros(s_s.shape, F32)

        row, col = _tri_masks(nh)
        qh, kh, vh, bbh, gcc = q_ref[...], k_ref[...], v_ref[...], b_ref[...], g_ref[...]
        dm, kb, lm = _gdn_chunk_common(kh, gcc, bbh, row, col)
        eg = jnp.exp(gcc)
        glr = gcc[:, CHUNK - 1:CHUNK, :]
        th = _unit_lower_inverse(lm, (row == col).astype(F32))
        w = _dot1(th, kb * eg, NN3)
        u = _dot1(th, vh * bbh, NN3)
        at = jnp.where(row >= col, _dot1(qh, kh, NT3) * dm, 0.0)
        sh = s_s[...]
        vn = u - _dot1(w, sh, NN3)
        o_ref[...] = _dot1(qh * eg, sh, NN3) + _dot1(at, vn, NN3)
        kd = kh * jnp.exp(glr - gcc)
        sall_ref[:, 0] = sh
        tall_ref[...] = th
        s_s[...] = sh * jnp.exp(glr) + _dot1(kd, vn, TN3)

    blk = pl.BlockSpec((nh, CHUNK, dh), lambda n: (0, n, 0))
    body, x_in, x_out, x_shape, x_scr, x_args = _carry(body, 5, 3, (nchunk,), carried)
    return pl.pallas_call(
        body, name="gdn_fwd", grid=(nchunk,), in_specs=[blk] * 5 + x_in,
        out_specs=[blk, pl.BlockSpec((nh, 1, dh, dh), lambda n: (0, n, 0, 0)), blk] + x_out,
        out_shape=[jax.ShapeDtypeStruct((nh, t, dh), F32), jax.ShapeDtypeStruct((nh, nchunk, dh, dh), F32),
                   jax.ShapeDtypeStruct((nh, t, CHUNK), F32)] + x_shape,
        scratch_shapes=[pltpu.VMEM((nh, dh, dh), F32)] + x_scr,
        compiler_params=_params(("arbitrary",)))(q, k, v, gb, bb, *x_args)


def _gdn_bwd(q, k, v, gb, bb, sall, tall, do):
    nh, t, dh = q.shape
    nchunk = t // CHUNK

    def body(q_ref, k_ref, v_ref, g_ref, b_ref, sall_ref, tall_ref, do_ref,
             dq_ref, dk_ref, dv_ref, dg_ref, db_ref, ds_s):
        @pl.when(pl.program_id(0) == 0)
        def _():
            ds_s[...] = jnp.zeros(ds_s.shape, F32)

        row, col = _tri_masks(nh)
        tril, stril = row >= col, row > col
        rsum = lambda x: jnp.sum(x, axis=2, keepdims=True)
        qh, kh, vh, gcc, bbh = q_ref[...], k_ref[...], v_ref[...], g_ref[...], b_ref[...]
        sh, th, doh, dsp = sall_ref[:, 0], tall_ref[...], do_ref[...], ds_s[...]
        dm, kb, lm = _gdn_chunk_common(kh, gcc, bbh, row, col, _dot3)
        eg = jnp.exp(gcc)
        glr = gcc[:, CHUNK - 1:CHUNK, :]
        glv = jnp.exp(glr)
        egl = jnp.exp(glr - gcc)
        rw, ru = kb * eg, vh * bbh
        w, u = _dot3(th, rw, NN3), _dot3(th, ru, NN3)
        at = jnp.where(tril, _dot3(qh, kh, NT3) * dm, 0.0)
        qd, kd = qh * eg, kh * egl
        vn = u - _dot3(w, sh, NN3)
        dgl = jnp.sum(rsum(dsp * sh), axis=1, keepdims=True)
        dkd = _dot3(vn, dsp, NT3)
        dvn = _dot3(kd, dsp, NN3)
        dqd = _dot3(doh, sh, NT3)
        dat = jnp.where(tril, _dot3(doh, vn, NT3), 0.0)
        dvn = dvn + _dot3(at, doh, TN3)
        dw = -_dot3(dvn, sh, NT3)
        ds_s[...] = dsp * glv + _dot3(qd, doh, TN3) - _dot3(w, dvn, TN3)
        dpa = dat * dm
        dq_ref[...] = _dot1(dpa, kh, NN3) + dqd * eg
        dk = _dot1(dpa, qh, TN3) + dkd * egl
        t6 = rsum(dkd * kd)
        dgam = rsum(dqd * qd) - t6
        dgam_last = jnp.sum(t6, axis=1, keepdims=True) + dgl * glv
        drw = _dot3(th, dw, TN3)
        dru = _dot3(th, dvn, TN3)
        dl = -jnp.where(stril, _dot3(drw, w, NT3) + _dot3(dru, u, NT3), 0.0)
        dgam = dgam + rsum(drw * rw)
        dv_ref[...] = dru * bbh
        dp2 = dl * dm
        dkb = drw * eg + _dot1(dp2, kh, NN3)
        dk_ref[...] = dk + _dot1(dp2, kb, TN3) + dkb * bbh
        db_ref[...] = rsum(dru * vh) + rsum(dkb * kh) + jnp.zeros((nh, CHUNK, dh), F32)
        e = dat * at + dl * lm
        dgam_b = dgam + rsum(e) - _dot01(e, jnp.ones((nh, CHUNK, CHUNK), F32), TN3)
        dgam_b = dgam_b + jnp.where(row == CHUNK - 1, dgam_last, 0.0)
        dg_ref[...] = dgam_b

    rev = lambda n: (0, nchunk - 1 - n, 0)
    blk = pl.BlockSpec((nh, CHUNK, dh), rev)
    sblk = pl.BlockSpec((nh, 1, dh, dh), lambda n: (0, nchunk - 1 - n, 0, 0))
    out = jax.ShapeDtypeStruct((nh, t, dh), F32)
    return pl.pallas_call(
        body, name="gdn_bwd", grid=(nchunk,), in_specs=[blk] * 5 + [sblk, blk, blk], out_specs=[blk] * 5,
        out_shape=[out] * 5, scratch_shapes=[pltpu.VMEM((nh, dh, dh), F32)],
        compiler_params=_params(("arbitrary",)))(q, k, v, gb, bb, sall, tall, do)


def _group_ones():
    r = lax.broadcasted_iota(jnp.int32, (GDN_W, GDN_W), 0) // GDN_DH
    c = lax.broadcasted_iota(jnp.int32, (GDN_W, GDN_W), 1) // GDN_DH
    return (r == c).astype(F32)


def _conv_taps(x, xprev, w, has_prev):
    row = lax.broadcasted_iota(jnp.int32, x.shape, 0)
    out = x * w[GDN_CONV - 1:GDN_CONV, :]
    for s in range(1, GDN_CONV):
        sh = jnp.where(row >= s, _roll(x, s, 0), _roll(xprev, s, 0) * has_prev)
        out = out + sh * w[GDN_CONV - 1 - s:GDN_CONV - s, :]
    return out


def _head_cols(x, h):
    return x[:, h * GDN_DH:(h + 1) * GDN_DH]


def _heads_spec(tb):
    return pl.BlockSpec((N_HEADS, tb, GDN_DH), lambda i: (0, i, 0))


def _mixer_fwd(x, positions, w, tb, carried=None, carried_gdn=None):
    t, d = x.shape
    tables = _rope_tables(positions)

    def pre(xb, g):
        return (xb * _rms_stats(xb) * g,)

    (hn,) = _rowwise("mix_pre", pre, [x], [w["mix_pre_g"]], [(d, BF16)], [], tb)
    proj = _mm("mix_in", hn, w["w_in_pad_t"], "nt", F32)

    def mla_pre(p0, gq, gkv):
        cq, ckv = p0[:, :MLA_Q_RANK], p0[:, MLA_Q_RANK:MLA_Q_RANK + MLA_KV_RANK]
        return cq * _rms_stats(cq) * gq, ckv * _rms_stats(ckv) * gkv

    nq, nkv = _rowwise("mla_pre", mla_pre, [(proj, 512, PIN_MLA // 512, 0)],
                       [w["mla_q_norm_g"], w["mla_kv_norm_g"]], [(MLA_Q_RANK, BF16), (MLA_KV_RANK, BF16)], [], tb)
    qraw = _mm("mla_uq", nq, w["w_uq_pad"], "nn", F32)
    kv = _mm("mla_ukv", nkv, w["w_kv_pad"], "nn", F32)

    def rope_f(qr, kn, vv, kpe, c, s1, s2):
        qo = _heads_apply(qr, lambda xh: _rope(xh, c, s1, s2)) * _attn_scale()
        kp = _rope(kpe, c, s1, s2)
        return qo, kn + jnp.tile(kp, (1, N_HEADS)), vv

    q, k, v = _rowwise("mla_rope", rope_f,
                       [qraw, (kv, MLA_PAD, 0, 0), (kv, MLA_PAD, 1, 0), (proj, HEAD_LANES, PIN_KPE // HEAD_LANES, 0),
                        tables[0], tables[1], tables[2]], [],
                       [(MLA_PAD, BF16)] * 3, [], tb // 2)
    tq = min(1024, t)
    o, lse, *carried_out = _attn_fwd(q, k, v, tq, carried)

    def mla_post(ob, g):
        return (ob * _rms_stats(ob, N_HEADS * MLA_V) * g,)

    (cat,) = _rowwise("mla_post", mla_post, [o], [w["mla_out_g_pad"]], [(MLA_PAD, BF16)], [], tb, wide=(CAT_W, 0))

    gones = _group_ones()
    steps = t // tb

    def gdn_pre(xq, xk, xv, pq, pk, pv, cw, go, has_prev):
        outs = []
        for j, (xc, xp) in enumerate(((xq, pq), (xk, pk), (xv, pv))):
            c = _conv_taps(xc, xp, cw[:, j * GDN_W:(j + 1) * GDN_W], has_prev)
            a = c * _sigmoid(c)
            if j < 2:
                rn = lax.rsqrt(_dot01(a * a, go) + EPS)
                a = a * rn
                if j == 0:
                    a = a * (GDN_DH ** -0.5)
            outs.append(a)
        return tuple(outs)

    qh, kh, vh = _gdn_pre_call("gdn_pre", gdn_pre, proj, w["conv_w"], gones, tb, steps)
    heads_shape = jax.ShapeDtypeStruct((N_HEADS, t, GDN_DH), F32)
    lanes_shape = jax.ShapeDtypeStruct((t, HEAD_LANES), F32)
    lanes_spec = pl.BlockSpec((tb, HEAD_LANES), lambda i: (i, 0))
    vec_spec = lambda n: pl.BlockSpec((1, n), lambda i: (0, 0))

    def gate_f(ab_ref, al_ref, dt_ref, g_ref, b_ref, gh_ref, bh_ref):
        g, b = _gb_fwd(ab_ref[...], al_ref[...], dt_ref[...])
        g_ref[...] = g
        b_ref[...] = b
        gc = _dot01(_chunk_sum_matrix(tb, False), g, ones="lhs")
        for h in range(N_HEADS):
            gh_ref[h] = jnp.broadcast_to(gc[:, h:h + 1], (tb, GDN_DH))
            bh_ref[h] = jnp.broadcast_to(b[:, N_HEADS + h:N_HEADS + h + 1], (tb, GDN_DH))

    g128, b128, gbh, bbh = pl.pallas_call(
        gate_f, name="gdn_gate_f", grid=(steps,),
        in_specs=[pl.BlockSpec((tb, HEAD_LANES), lambda i: (i, PIN_AB // HEAD_LANES)), vec_spec(HEAD_LANES),
                  vec_spec(HEAD_LANES)],
        out_specs=[lanes_spec, lanes_spec, _heads_spec(tb), _heads_spec(tb)],
        out_shape=[lanes_shape, lanes_shape, heads_shape, heads_shape],
        compiler_params=_params(("arbitrary",)))(proj, w["a_log_pad"], w["dt_bias_pad"])
    oh, sall, tall, *carried_out_gdn = _gdn_fwd(qh, kh, vh, gbh, bbh, carried_gdn)

    def gdn_post(o_ref, gt_ref, g_ref, cat_in, cat_ref):
        gt, g = gt_ref[...], g_ref[...]
        outs = []
        for h in range(N_HEADS):
            ob, gth = o_ref[h], _head_cols(gt, h)
            outs.append(ob * _rms_stats(ob) * g * (gth * _sigmoid(gth)))
        cat_ref[...] = jnp.concatenate(outs, axis=1).astype(cat_ref.dtype)

    gate_spec = pl.BlockSpec((tb, GDN_W), lambda i: (i, PIN_GATE // GDN_W))
    cat = pl.pallas_call(
        gdn_post, name="gdn_post", grid=(steps,),
        in_specs=[_heads_spec(tb), gate_spec, vec_spec(GDN_DH), ANY_SPEC],
        out_specs=pl.BlockSpec((tb, GDN_W), lambda i: (i, MLA_PAD // GDN_W)),
        out_shape=jax.ShapeDtypeStruct((t, CAT_W), BF16), input_output_aliases={3: 0},
        compiler_params=_params(("arbitrary",)))(oh, proj, w["gdn_norm_g"], cat)
    mixed = _mm("mix_out", cat, w["w_out_pad"], "nn", F32)

    def post(xb, hb, g):
        return (xb + hb * _rms_stats(hb) * g,)

    (y,) = _rowwise("mix_post", post, [x, mixed], [w["mix_post_g"]], [(d, F32)], [], tb)
    saved = dict(x=x, hn=hn, proj=proj, nq=nq, nkv=nkv, q=q, k=k, v=v, o=o, lse=lse, qh=qh, kh=kh, vh=vh,
                 gbh=gbh, bbh=bbh, oh=oh, sall=sall, tall=tall, cat=cat, mixed=mixed,
                 tables=tables, g128=g128, b128=b128)
    return y, saved, list(carried_out) + list(carried_out_gdn)


def _qkv_specs(tb):
    base = PIN_QKV // GDN_W
    cur = [pl.BlockSpec((tb, GDN_W), lambda i, j=j: (i, base + j)) for j in range(3)]
    prev = [pl.BlockSpec((tb, GDN_W), lambda i, j=j: (jnp.maximum(i - 1, 0), base + j)) for j in range(3)]
    return cur + prev


def _gdn_pre_call(name, fn, proj, conv_w, gones, tb, steps):
    t = proj.shape[0]

    def body(xq, xk, xv, pq, pk, pv, cw, go, oq, ok, ov):
        has_prev = jnp.where(pl.program_id(0) == 0, 0.0, 1.0)
        outs = fn(xq[...], xk[...], xv[...], pq[...], pk[...], pv[...], cw[...], go[...], has_prev)
        for r, val in zip((oq, ok, ov), outs):
            for h in range(N_HEADS):
                r[h] = _head_cols(val, h)

    return pl.pallas_call(
        body, name=name, grid=(steps,),
        in_specs=_qkv_specs(tb) + [pl.BlockSpec(conv_w.shape, lambda i: (0, 0)),
                                   pl.BlockSpec(gones.shape, lambda i: (0, 0))],
        out_specs=[_heads_spec(tb)] * 3,
        out_shape=[jax.ShapeDtypeStruct((N_HEADS, t, GDN_DH), F32)] * 3,
        compiler_params=_params(("arbitrary",)))(proj, proj, proj, proj, proj, proj, conv_w, gones)


def _softplus(x):
    return jnp.maximum(x, 0.0) + jnp.log1p(jnp.exp(-jnp.abs(x)))


def _gb_fwd(ab, a_log, dt_bias):
    g = -jnp.exp(a_log) * _softplus(ab + dt_bias)
    return g, _sigmoid(ab)


def _rope_tables(positions):
    half = MLA_ROPE // 2
    freqs = ROPE_THETA ** (-jnp.arange(half, dtype=F32) / half)
    ang = positions.reshape(-1).astype(F32)[:, None] * freqs
    cos, sin = jnp.cos(ang), jnp.sin(ang)
    t = ang.shape[0]
    one = jnp.ones((t, MLA_NOPE), F32)
    z16, z32, z64 = jnp.zeros((t, half), F32), jnp.zeros((t, MLA_ROPE), F32), jnp.zeros((t, MLA_NOPE), F32)
    c = jnp.concatenate([one, cos, cos, jnp.ones((t, MLA_ROPE), F32)], axis=1)
    s1 = jnp.concatenate([z64, -sin, z16, z32], axis=1)
    s2 = jnp.concatenate([z64, z16, sin, z32], axis=1)
    return c, s1, s2


def _mixer_bwd(dy, sv, w, tb, carried=None):
    x, proj = sv["x"], sv["proj"]
    t, d = x.shape
    c, s1, s2 = sv["tables"]
    grads = {}

    def post_b(hb, dyb, g):
        return _rms_bwd(hb, _rms_stats(hb), g, dyb)

    dmixed, grads["mix_post_g"] = _rowwise("mix_post_b", post_b, [sv["mixed"], dy], [w["mix_post_g"]],
                                           [(d, BF16)], [(1, d)], tb)
    dcat = _mm("mix_out_bx", dmixed, w["w_out_pad"], "nt", F32)
    grads["w_out_pad"] = _mm("mix_out_bw", sv["cat"], dmixed, "tn", F32)
    steps = t // tb
    vec_spec = lambda n: pl.BlockSpec((1, n), lambda i: (0, 0))

    def gdn_post_b(o_ref, gt_ref, do_ref, g_ref, dproj_ref, doh_ref, dg_ref):
        @pl.when(pl.program_id(0) == 0)
        def _():
            dg_ref[...] = jnp.zeros(dg_ref.shape, F32)

        gt, dob, g = gt_ref[...], do_ref[...], g_ref[...]
        dgates = []
        for h in range(N_HEADS):
            ob, gth, dobh = o_ref[h], _head_cols(gt, h), _head_cols(dob, h)
            sg = _sigmoid(gth)
            r = _rms_stats(ob)
            dxo, dg = _rms_bwd(ob, r, g, dobh * (gth * sg))
            doh_ref[h] = dxo
            dg_ref[...] += dg
            dgates.append(dobh * (ob * r * g) * (sg * (1.0 + gth * (1.0 - sg))))
        dproj_ref[...] = jnp.concatenate(dgates, axis=1).astype(dproj_ref.dtype)

    dproj, doh, grads["gdn_norm_g"] = pl.pallas_call(
        gdn_post_b, name="gdn_post_b", grid=(steps,),
        in_specs=[_heads_spec(tb), pl.BlockSpec((tb, GDN_W), lambda i: (i, PIN_GATE // GDN_W)),
                  pl.BlockSpec((tb, GDN_W), lambda i: (i, MLA_PAD // GDN_W)), vec_spec(GDN_DH)],
        out_specs=[pl.BlockSpec((tb, GDN_W), lambda i: (i, PIN_GATE // GDN_W)), _heads_spec(tb), vec_spec(GDN_DH)],
        out_shape=[jax.ShapeDtypeStruct((t, PIN_W), BF16), jax.ShapeDtypeStruct((N_HEADS, t, GDN_DH), F32),
                   jax.ShapeDtypeStruct((1, GDN_DH), F32)],
        compiler_params=_params(("arbitrary",)))(sv["oh"], proj, dcat, w["gdn_norm_g"])

    def mla_post_b(ob, dmo, g):
        do, dg = _rms_bwd(ob, _rms_stats(ob, N_HEADS * MLA_V), g, dmo, N_HEADS * MLA_V)
        prod = do * ob
        delta = _heads_apply(prod, lambda ph: jnp.sum(ph, axis=1, keepdims=True) + jnp.zeros_like(ph))
        return do, delta, dg

    do, delta, grads["mla_out_g_pad"] = _rowwise(
        "mla_post_b", mla_post_b, [sv["o"], (dcat, MLA_PAD, 0, 0)], [w["mla_out_g_pad"]],
        [(MLA_PAD, BF16), (MLA_PAD, F32)], [(1, MLA_PAD)], tb // 2)
    tq = min(1024, t)
    dq, dk, dv, *carried_out = _attn_bwd(sv["q"], sv["k"], sv["v"], do, sv["lse"], delta, tq, carried)
    n_groups = N_HEADS // BWD_HEADS
    tr = tb // 2
    dq_groups = [(dq, BWD_HEADS * HEAD_LANES, 0, grp * (t // tr)) for grp in range(n_groups)]

    def rope_b(*blocks):
        dqb = jnp.concatenate(blocks[:n_groups], axis=1)
        dkb, dvb, cc, a1, a2 = blocks[n_groups:]
        dqr = _heads_apply(dqb * _attn_scale(), lambda xh: _rope(xh, cc, -a1, -a2))
        ksum = dkb[:, :HEAD_LANES]
        for h in range(1, N_HEADS):
            ksum = ksum + dkb[:, h * HEAD_LANES:(h + 1) * HEAD_LANES]
        lane = lax.broadcasted_iota(jnp.int32, ksum.shape, 1)
        keep = (lane >= MLA_NOPE) & (lane < MLA_NOPE + MLA_ROPE)
        dkpe = jnp.where(keep, _rope(ksum, cc, -a1, -a2), 0.0)
        return dqr, jnp.concatenate([dkb, dvb], axis=1), dkpe

    dqraw, dkv, dkpe = _rowwise("mla_rope_b", rope_b, dq_groups + [dk, dv, c, s1, s2], [],
                                [(MLA_PAD, BF16, t), (2 * MLA_PAD, BF16), (HEAD_LANES, F32)], [], tr)
    dnq = _mm("mla_uq_bx", dqraw, w["w_uq_pad"], "nt", F32)
    grads["w_uq_pad"] = _mm("mla_uq_bw", sv["nq"], dqraw, "tn", F32)
    dnkv = _mm("mla_ukv_bx", dkv, w["w_kv_pad"], "nt", F32)
    grads["w_kv_pad"] = _mm("mla_ukv_bw", sv["nkv"], dkv, "tn", F32)

    def mla_pre_b(p0, dnqb, dnkvb, dkpeb, gq, gkv):
        cq, ckv = p0[:, :MLA_Q_RANK], p0[:, MLA_Q_RANK:MLA_Q_RANK + MLA_KV_RANK]
        dcq, dgq = _rms_bwd(cq, _rms_stats(cq), gq, dnqb)
        dckv, dgkv = _rms_bwd(ckv, _rms_stats(ckv), gkv, dnkvb)
        return jnp.concatenate([dcq, dckv, dkpeb], axis=1), dgq, dgkv

    dproj, grads["mla_q_norm_g"], grads["mla_kv_norm_g"] = _rowwise(
        "mla_pre_b", mla_pre_b, [(proj, 512, PIN_MLA // 512, 0), dnq, dnkv, dkpe],
        [w["mla_q_norm_g"], w["mla_kv_norm_g"]], [(512, BF16)], [(1, MLA_Q_RANK), (1, MLA_KV_RANK)], tb,
        wide=(PIN_W, PIN_MLA // 512), carry=dproj)

    dqh, dkh, dvh, dgh, dbh = _gdn_bwd(sv["qh"], sv["kh"], sv["vh"], sv["gbh"], sv["bbh"], sv["sall"], sv["tall"], doh)
    gones = _group_ones()

    def gdn_pre_b(xq, xk, xv, pq, pk, pv, dq_, dk_, dv_, cw, go, has_prev):
        outs = []
        for j, (xc, xp, dd) in enumerate(((xq, pq, dq_), (xk, pk, dk_), (xv, pv, dv_))):
            cc = _conv_taps(xc, xp, cw[:, j * GDN_W:(j + 1) * GDN_W], has_prev)
            sg = _sigmoid(cc)
            a = cc * sg
            if j < 2:
                rn = lax.rsqrt(_dot01(a * a, go) + EPS)
                if j == 0:
                    dd = dd * (GDN_DH ** -0.5)
                da = rn * dd - a * (rn * rn * rn) * _dot01(dd * a, go)
            else:
                da = dd
            outs.append(da * (sg * (1.0 + cc * (1.0 - sg))))
        return tuple(outs)

    dcq, dck, dcv = _gdn_pre_b_call("gdn_pre_b", gdn_pre_b, proj, (dqh, dkh, dvh), w["conv_w"], gones, tb, steps)
    dproj, grads["conv_w"] = _conv_bwd_call("gdn_conv_b", proj, (dcq, dck, dcv), w["conv_w"], dproj, tb, steps)

    def gate_b(ab_ref, g_ref, b_ref, dgh_ref, dbh_ref, al_ref, dt_ref, carry_ref, dab_ref, dal_ref, ddt_ref):
        @pl.when(pl.program_id(0) == 0)
        def _():
            dal_ref[...] = jnp.zeros(dal_ref.shape, F32)
            ddt_ref[...] = jnp.zeros(ddt_ref.shape, F32)

        ab, g128, b128 = ab_ref[...], g_ref[...], b_ref[...]
        lane = lax.broadcasted_iota(jnp.int32, ab.shape, 1)
        dg_ = jnp.zeros(ab.shape, F32)
        db_ = jnp.zeros(ab.shape, F32)
        for h in range(N_HEADS):
            dg_ = dg_ + jnp.where(lane == h, jnp.broadcast_to(dgh_ref[h][:, 0:1], ab.shape), 0.0)
            db_ = db_ + jnp.where(lane == N_HEADS + h, jnp.broadcast_to(dbh_ref[h][:, 0:1], ab.shape), 0.0)
        dg_ = _dot01(_chunk_sum_matrix(tb, True), dg_, ones="lhs")
        slope = -jnp.exp(al_ref[...]) * _sigmoid(ab + dt_ref[...])
        dab_ref[...] = (dg_ * slope + db_ * b128 * (1.0 - b128)).astype(dab_ref.dtype)
        dal_ref[...] += jnp.sum(dg_ * g128, axis=0, keepdims=True)
        ddt_ref[...] += jnp.sum(dg_ * slope, axis=0, keepdims=True)

    lanes_spec = pl.BlockSpec((tb, HEAD_LANES), lambda i: (i, 0))
    ab_spec = pl.BlockSpec((tb, HEAD_LANES), lambda i: (i, PIN_AB // HEAD_LANES))
    dproj, grads["a_log_pad"], grads["dt_bias_pad"] = pl.pallas_call(
        gate_b, name="gdn_gate_b", grid=(steps,),
        in_specs=[ab_spec, lanes_spec, lanes_spec, _heads_spec(tb), _heads_spec(tb), vec_spec(HEAD_LANES),
                  vec_spec(HEAD_LANES), ANY_SPEC],
        out_specs=[ab_spec, vec_spec(HEAD_LANES), vec_spec(HEAD_LANES)],
        out_shape=[jax.ShapeDtypeStruct((t, PIN_W), BF16), jax.ShapeDtypeStruct((1, HEAD_LANES), F32),
                   jax.ShapeDtypeStruct((1, HEAD_LANES), F32)],
        input_output_aliases={7: 0},
        compiler_params=_params(("arbitrary",)))(proj, sv["g128"], sv["b128"], dgh, dbh, w["a_log_pad"],
                                                 w["dt_bias_pad"], dproj)
    dhn = _mm("mix_in_bx", dproj, w["w_in_pad_t"], "nn", F32)
    grads["w_in_pad_t"] = _mm("mix_in_bw", dproj, sv["hn"], "tn", F32)

    def pre_b(xb, dnb, dyb, g):
        dx, dg = _rms_bwd(xb, _rms_stats(xb), g, dnb)
        return dyb + dx, dg

    dx, grads["mix_pre_g"] = _rowwise("mix_pre_b", pre_b, [x, dhn, dy], [w["mix_pre_g"]], [(d, F32)], [(1, d)], tb)
    return dx, grads, carried_out


def _gdn_pre_b_call(name, fn, proj, dd, conv_w, gones, tb, steps):
    t = proj.shape[0]

    def body(xq, xk, xv, pq, pk, pv, d0, d1, d2, cw, go, oq, ok, ov):
        has_prev = jnp.where(pl.program_id(0) == 0, 0.0, 1.0)
        dd_rows = [jnp.concatenate([dr[h] for h in range(N_HEADS)], axis=1) for dr in (d0, d1, d2)]
        outs = fn(xq[...], xk[...], xv[...], pq[...], pk[...], pv[...], *dd_rows, cw[...], go[...], has_prev)
        for r, val in zip((oq, ok, ov), outs):
            r[...] = val

    return pl.pallas_call(
        body, name=name, grid=(steps,),
        in_specs=_qkv_specs(tb) + [_heads_spec(tb)] * 3 + [pl.BlockSpec(conv_w.shape, lambda i: (0, 0)),
                                                          pl.BlockSpec(gones.shape, lambda i: (0, 0))],
        out_specs=[pl.BlockSpec((tb, GDN_W), lambda i: (i, 0))] * 3,
        out_shape=[jax.ShapeDtypeStruct((t, GDN_W), F32)] * 3,
        compiler_params=_params(("arbitrary",)))(proj, proj, proj, proj, proj, proj, *dd, conv_w, gones)


def _conv_bwd_call(name, proj, dc, conv_w, dproj, tb, steps):
    t = proj.shape[0]
    dcur = [pl.BlockSpec((tb, GDN_W), lambda i: (i, 0))] * 3
    dnext = [pl.BlockSpec((tb, GDN_W), lambda i: (jnp.minimum(i + 1, steps - 1), 0))] * 3

    def body(xq, xk, xv, pq, pk, pv, d0, d1, d2, n0, n1, n2, cw, carry_ref, dx_ref, dw_ref):
        i = pl.program_id(0)
        has_prev = jnp.where(i == 0, 0.0, 1.0)
        has_next = jnp.where(i == steps - 1, 0.0, 1.0)

        @pl.when(i == 0)
        def _():
            dw_ref[...] = jnp.zeros(dw_ref.shape, F32)

        wv = cw[...]
        dws, dxs = [], []
        for j, (xr, pr, dr, nr) in enumerate(((xq, pq, d0, n0), (xk, pk, d1, n1), (xv, pv, d2, n2))):
            x, xp, dcv, dnx = xr[...], pr[...], dr[...], nr[...]
            wj = wv[:, j * GDN_W:(j + 1) * GDN_W]
            row = lax.broadcasted_iota(jnp.int32, x.shape, 0)
            dx = dcv * wj[GDN_CONV - 1:GDN_CONV, :]
            rows_w = [jnp.sum(dcv * x, axis=0, keepdims=True)]
            for s in range(1, GDN_CONV):
                up = jnp.where(row < tb - s, _roll(dcv, tb - s, 0), _roll(dnx, tb - s, 0) * has_next)
                dx = dx + up * wj[GDN_CONV - 1 - s:GDN_CONV - s, :]
                sh = jnp.where(row >= s, _roll(x, s, 0), _roll(xp, s, 0) * has_prev)
                rows_w.append(jnp.sum(dcv * sh, axis=0, keepdims=True))
            dxs.append(dx)
            dws.append(jnp.concatenate(rows_w[::-1], axis=0))
        dx_ref[...] = jnp.concatenate(dxs, axis=1).astype(dx_ref.dtype)
        dw_ref[...] += jnp.concatenate(dws, axis=1)

    return pl.pallas_call(
        body, name=name, grid=(steps,),
        in_specs=_qkv_specs(tb) + dcur + dnext + [pl.BlockSpec(conv_w.shape, lambda i: (0, 0)), ANY_SPEC],
        out_specs=[pl.BlockSpec((tb, 3 * GDN_W), lambda i: (i, PIN_QKV // (3 * GDN_W))),
                   pl.BlockSpec(conv_w.shape, lambda i: (0, 0))],
        out_shape=[jax.ShapeDtypeStruct((t, PIN_W), BF16), jax.ShapeDtypeStruct(conv_w.shape, F32)],
        input_output_aliases={13: 0},
        compiler_params=_params(("arbitrary",)))(proj, proj, proj, proj, proj, proj, *dc, *dc, conv_w, dproj)


def _pad_heads_cols(wm, per_head):
    r = wm.shape[0]
    return jnp.pad(wm.reshape(r, N_HEADS, per_head), ((0, 0), (0, 0), (0, HEAD_LANES - per_head))).reshape(r, MLA_PAD)


def _unpad_heads_cols(wm, per_head):
    r = wm.shape[0]
    return wm.reshape(r, N_HEADS, HEAD_LANES)[:, :, :per_head].reshape(r, N_HEADS * per_head)


W_IN_COLS = MLA_Q_RANK + MLA_KV_RANK + MLA_ROPE + 3 * GDN_W + 2 * N_HEADS + GDN_W
W_IN_SHARD = W_IN_COLS // N_SHARD
W_IN_SHARD_PAD = 640
_Q0 = MLA_Q_RANK + MLA_KV_RANK
_Q1 = _Q0 + MLA_ROPE
_Q2 = _Q1 + 3 * GDN_W
_Q3 = _Q2 + 2 * N_HEADS
W_IN_SEGMENTS = [(0, _Q0, PIN_MLA), (_Q0, _Q1, PIN_KPE + MLA_NOPE), (_Q1, _Q2, PIN_QKV), (_Q2, _Q3, PIN_AB),
                 (_Q3, W_IN_COLS, PIN_GATE)]


def _win_pad_t(slabs):
    d = slabs.shape[2]
    pieces, at = [], 0
    for c0, c1, r0 in sorted(W_IN_SEGMENTS, key=lambda s: s[2]):
        if r0 > at:
            pieces.append(jnp.zeros((r0 - at, d), slabs.dtype))
        for q in range(N_SHARD):
            lo, hi = max(c0, q * W_IN_SHARD), min(c1, (q + 1) * W_IN_SHARD)
            if lo < hi:
                pieces.append(slabs[q, lo - q * W_IN_SHARD:hi - q * W_IN_SHARD])
        at = r0 + c1 - c0
    pieces.append(jnp.zeros((PIN_W - at, d), slabs.dtype))
    return jnp.concatenate(pieces, axis=0)


def _win_cols_t(wp_t, c_lo, c_hi):
    pieces = []
    for c0, c1, r0 in W_IN_SEGMENTS:
        lo, hi = max(c0, c_lo), min(c1, c_hi)
        if lo < hi:
            pieces.append(wp_t[r0 + lo - c0:r0 + hi - c0])
    return jnp.concatenate(pieces, axis=0)


def _wkv_to_pad(wkv):
    r = wkv.shape[0]
    w3 = wkv.reshape(r, N_HEADS, MLA_NOPE + MLA_V)
    kpart = jnp.pad(w3[:, :, :MLA_NOPE], ((0, 0), (0, 0), (0, HEAD_LANES - MLA_NOPE))).reshape(r, MLA_PAD)
    vpart = jnp.pad(w3[:, :, MLA_NOPE:], ((0, 0), (0, 0), (0, HEAD_LANES - MLA_V))).reshape(r, MLA_PAD)
    return jnp.concatenate([kpart, vpart], axis=1)


def _wkv_from_pad(wp):
    r = wp.shape[0]
    kpart = wp[:, :MLA_PAD].reshape(r, N_HEADS, HEAD_LANES)[:, :, :MLA_NOPE]
    vpart = wp[:, MLA_PAD:].reshape(r, N_HEADS, HEAD_LANES)[:, :, :MLA_V]
    return jnp.concatenate([kpart, vpart], axis=2).reshape(r, N_HEADS * (MLA_NOPE + MLA_V))


def _wout_to_pad(wo):
    n = wo.shape[1]
    mla = jnp.pad(wo[:N_HEADS * MLA_V].reshape(N_HEADS, MLA_V, n), ((0, 0), (0, HEAD_LANES - MLA_V), (0, 0)))
    return jnp.concatenate([mla.reshape(MLA_PAD, n), wo[N_HEADS * MLA_V:]], axis=0)


def _wout_from_pad(wp):
    n = wp.shape[1]
    mla = wp[:MLA_PAD].reshape(N_HEADS, HEAD_LANES, n)[:, :MLA_V].reshape(N_HEADS * MLA_V, n)
    return jnp.concatenate([mla, wp[MLA_PAD:]], axis=0)


def _pad_lanes(v, n):
    return jnp.pad(v, ((0, 0), (0, n - v.shape[1])))


def _compute_weights(full):
    w = {}
    for n in FFN_BIG:
        if n in full:
            w[n] = full[n].astype(MM_DTYPE)
    w["w_in_pad_t"] = _win_pad_t(full["w_in"]).astype(MM_DTYPE)
    w["w_uq_pad"] = _pad_heads_cols(full["mla_w_uq"], MLA_NOPE + MLA_ROPE).astype(MM_DTYPE)
    w["w_kv_pad"] = _wkv_to_pad(full["mla_w_ukv"]).astype(MM_DTYPE)
    w["w_out_pad"] = _wout_to_pad(full["w_out"]).astype(MM_DTYPE)
    w["conv_w"] = full["gdn_conv_w"].astype(F32)
    for n in ("ffn1_pre_g", "ffn1_post_g", "mix_pre_g", "mla_q_norm_g", "mla_kv_norm_g", "gdn_norm_g", "mix_post_g",
              "ffn2_pre_g", "ffn2_post_g"):
        w[n] = full[n]
    w["mla_out_g_pad"] = _pad_heads_cols(full["mla_out_g"], MLA_V)
    w["a_log_pad"] = _pad_lanes(full["gdn_a_log"], HEAD_LANES)
    w["dt_bias_pad"] = _pad_lanes(full["gdn_dt_bias"], HEAD_LANES)
    return w


FFN2_BIG = FFN_BIG[3:]


def _local_step(x, positions, loss_target, full, late=None):
    t, d = x.shape
    tb = min(512, t)
    tm = min(1024, t)
    w = _compute_weights(full)
    ffn = lambda tag: (w[tag + "_pre_g"], w[tag + "_w_gate"], w[tag + "_w_up"], w[tag + "_w_down"], w[tag + "_post_g"])
    x1, sv1 = _ffn_fwd("ffn1", x, *ffn("ffn1"), tm)
    x2, svm, gathered = _mixer_fwd(x1, positions, w, tb, _carried_gather(late[0][:2]) if late else None,
                                   _carried_gather(late[0][2:]) if late else None)
    for n, gw in zip(FFN2_BIG, gathered):
        w[n] = gw
    x3, sv2 = _ffn_fwd("ffn2", x2, *ffn("ffn2"), tm)

    def loss_f(yb, tg):
        e = yb - tg
        return e * (1.0 / d), jnp.sum(e * e, axis=0, keepdims=True)

    dy, lsum = _rowwise("loss", loss_f, [x3, loss_target], [], [(d, F32)], [(1, d)], tb)
    g = {}
    dx2, g["ffn2_pre_g"], g["ffn2_w_gate"], g["ffn2_w_up"], g["ffn2_w_down"], g["ffn2_post_g"] = _ffn_bwd(
        "ffn2", dy, sv2, *ffn("ffn2"), tm, tm)[:6]

    def pair_sums(arrs, tag):
        got = _swap_halves(arrs, tag)
        return [_add_pair("add_pair%s_%d" % (tag, i), gi, gt, late[1]) for i, (gi, gt) in enumerate(zip(arrs, got))]

    def chip_sums(pairs, slabs, tag):
        return [_add_chips("add_chips%s_%d" % (tag, i), pr, sl, late[2]) for i, (pr, sl) in enumerate(zip(pairs, slabs))]

    if late:
        pairs2 = pair_sums([g[n] for n in FFN2_BIG], "_ffn2")
        dx1, gm, slabs2 = _mixer_bwd(dx2, svm, w, tb, _carried_scatter(pairs2))
        for n, hs in zip(FFN2_BIG, chip_sums(pairs2, slabs2, "_ffn2")):
            g[n] = hs
    else:
        dx1, gm, _ = _mixer_bwd(dx2, svm, w, tb)
    g["w_in"] = jnp.stack([jnp.pad(_win_cols_t(gm["w_in_pad_t"], q * W_IN_SHARD, (q + 1) * W_IN_SHARD),
                                   ((0, W_IN_SHARD_PAD - W_IN_SHARD), (0, 0))) for q in range(N_SHARD)])
    g["mla_w_uq"] = _unpad_heads_cols(gm["w_uq_pad"], MLA_NOPE + MLA_ROPE)
    g["mla_w_ukv"] = _wkv_from_pad(gm["w_kv_pad"])
    g["gdn_conv_w"] = gm["conv_w"]
    g["w_out"] = _wout_from_pad(gm["w_out_pad"])
    if late:
        quarters = [_pack([jnp.split(g[n], N_SHARD, axis=SHARD_AXIS[n])[q] for n in MIX_BIG], MM_DTYPE)
                    for q in range(N_SHARD)]
        pairs_m = pair_sums([g["w_in"].astype(MM_DTYPE), jnp.stack(quarters)], "_mix")
        pairs_d, pairs_gu = [], []

        def make_mid(dwd):
            pairs_d.extend(pair_sums([dwd], "_ffn1d"))
            return _carried_scatter(pairs_d)

        def make_up(dwg, dwu):
            pairs_gu.extend(pair_sums([dwg, dwu], "_ffn1"))
            return _carried_scatter(pairs_gu)

        dx0, g["ffn1_pre_g"], _, _, _, g["ffn1_post_g"], slabs_m, slabs_d, slabs_gu = _ffn_bwd(
            "ffn1", dx1, sv1, *ffn("ffn1"), tm, tm, _carried_scatter(pairs_m), make_mid, make_up)
        g["ffn1_w_gate"], g["ffn1_w_up"] = chip_sums(pairs_gu, slabs_gu, "_ffn1")
        g["ffn1_w_down"] = chip_sums(pairs_d, slabs_d, "_ffn1d")[0]
        g["w_in"], g["mix_pack"] = chip_sums(pairs_m, slabs_m, "_mix")
    else:
        dx0, g["ffn1_pre_g"], g["ffn1_w_gate"], g["ffn1_w_up"], g["ffn1_w_down"], g["ffn1_post_g"] = _ffn_bwd(
            "ffn1", dx1, sv1, *ffn("ffn1"), tm, tm)[:6]
    g["mix_pre_g"], g["mix_post_g"] = gm["mix_pre_g"], gm["mix_post_g"]
    g["mla_q_norm_g"], g["mla_kv_norm_g"] = gm["mla_q_norm_g"], gm["mla_kv_norm_g"]
    g["gdn_norm_g"] = gm["gdn_norm_g"]
    g["mla_out_g"] = _unpad_heads_cols(gm["mla_out_g_pad"], MLA_V)
    g["gdn_a_log"] = gm["a_log_pad"][:, :N_HEADS]
    g["gdn_dt_bias"] = gm["dt_bias_pad"][:, :N_HEADS]
    return lsum, dx0, g


HBM_SPEC = pl.BlockSpec(memory_space=pltpu.HBM)


def _place():
    return lax.axis_index("x"), lax.axis_index("y"), lax.axis_index("c")


def _exchange_call(name, body, ins, out_shapes, n_remote, n_local):
    return pl.pallas_call(
        body, name=name, in_specs=[HBM_SPEC] * len(ins), out_specs=[HBM_SPEC] * len(out_shapes), out_shape=out_shapes,
        scratch_shapes=[pltpu.SemaphoreType.DMA((n_remote,)), pltpu.SemaphoreType.DMA((n_remote,)),
                        pltpu.SemaphoreType.DMA((n_local,))])(*ins)


def _other_chips(x, y):
    return [(1 - x, y), (x, 1 - y), (1 - x, 1 - y)]


def _at_each_chip(fn):
    x, y, _ = _place()
    for cx in range(2):
        for cy in range(2):
            pl.when((x == cx) & (y == cy))(functools.partial(fn, cx, cy))


def _at_each_device(fn):
    x, y, c = _place()
    for cx in range(2):
        for cy in range(2):
            for cc in range(2):
                pl.when((x == cx) & (y == cy) & (c == cc))(functools.partial(fn, cx, cy, cc))


def _at_each_core(fn):
    c = lax.axis_index("c")
    for cc in range(2):
        pl.when(c == cc)(functools.partial(fn, cc))


def _gather_shards(ws):
    nw = len(ws)

    def body(*refs):
        w_refs, out_refs = refs[:nw], refs[nw:2 * nw]
        send_sems, recv_sems, local_sems = refs[2 * nw:]

        def run(x, y, c):
            chips = _other_chips(x, y)
            me, sibling = 2 * x + y, (x, y, 1 - c)

            def half(ref, which):
                hr = ref.shape[0] // 2
                return ref.at[pl.ds(which * hr, hr)]

            def over_ici(i, j, src, slab, to):
                return pltpu.make_async_remote_copy(
                    src_ref=half(src, c), dst_ref=half(out_refs[i].at[slab], c), send_sem=send_sems.at[7 * i + j],
                    recv_sem=recv_sems.at[7 * i + j], device_id=to, device_id_type=MESH)

            def over_d2d(i, j, slab, which):
                return pltpu.make_async_remote_copy(
                    src_ref=half(out_refs[i].at[slab], which), dst_ref=half(out_refs[i].at[slab], which),
                    send_sem=send_sems.at[7 * i + 3 + j], recv_sem=recv_sems.at[7 * i + 3 + j], device_id=sibling,
                    device_id_type=MESH)

            def own(i, w_ref):
                return pltpu.make_async_remote_copy(
                    src_ref=w_ref, dst_ref=out_refs[i].at[me], send_sem=send_sems.at[7 * i + 6],
                    recv_sem=recv_sems.at[7 * i + 6], device_id=sibling, device_id_type=MESH)

            sends, passed = [], []
            for i, w_ref in enumerate(w_refs):
                for j, (px, py) in enumerate(chips):
                    sends.append(over_ici(i, j, w_ref, me, (px, py, c)))
                    sends[-1].start()
            for i, w_ref in enumerate(w_refs):
                sends.append(own(i, w_ref))
                sends[-1].start()
            for i, w_ref in enumerate(w_refs):
                for j, (px, py) in enumerate(chips):
                    over_ici(i, j, w_ref, 2 * px + py, (px, py, c)).wait_recv()
                    passed.append(over_d2d(i, j, 2 * px + py, c))
                    passed[-1].start()
            for i, w_ref in enumerate(w_refs):
                own(i, w_ref).wait_recv()
                for j, (px, py) in enumerate(chips):
                    over_d2d(i, j, 2 * px + py, 1 - c).wait_recv()
            for cp in sends + passed:
                cp.wait_send()

        _at_each_device(run)

    outs = [jax.ShapeDtypeStruct((N_SHARD,) + w.shape, w.dtype) for w in ws]
    return _exchange_call("gather_weight_shards", body, ws, outs, 7 * nw, 1)


def _swap_halves(gs, tag=""):
    ng = len(gs)

    def body(*refs):
        g_refs, got_refs = refs[:ng], refs[ng:2 * ng]
        send_sems, recv_sems, _ = refs[2 * ng:]
        x, y, _ = _place()

        def run(c):
            sends = []
            for i, (g_ref, got_ref) in enumerate(zip(g_refs, got_refs)):
                hr = got_ref.shape[1]
                sends.append(pltpu.make_async_remote_copy(
                    src_ref=g_ref.at[:, pl.ds((1 - c) * hr, hr)], dst_ref=got_ref, send_sem=send_sems.at[i],
                    recv_sem=recv_sems.at[i], device_id=(x, y, 1 - c), device_id_type=MESH))
                sends[-1].start()
            for cp in sends:
                cp.wait()

        _at_each_core(run)

    halves = [jax.ShapeDtypeStruct((g.shape[0], g.shape[1] // 2, g.shape[2]), g.dtype) for g in gs]
    return _exchange_call("swap_grad_halves" + tag, body, gs, halves, ng, 1)


def _scatter_copies(p_refs, out_refs, send_sems, recv_sems, x, y):
    c = lax.axis_index("c")
    copies = []
    for i, (p_ref, out_ref) in enumerate(zip(p_refs, out_refs)):
        for j, (px, py) in enumerate(_other_chips(x, y)):
            copies.append(pltpu.make_async_remote_copy(
                src_ref=p_ref.at[2 * px + py], dst_ref=out_ref.at[j], send_sem=send_sems.at[3 * i + j],
                recv_sem=recv_sems.at[3 * i + j], device_id=(px, py, c), device_id_type=MESH))
    return copies


def _start_all(make, *refs):
    def run(x, y):
        for cp in make(*refs, x, y):
            cp.start()

    _at_each_chip(run)


def _wait_all(make, *refs):
    def run(x, y):
        copies = make(*refs, x, y)
        for cp in copies:
            cp.wait_recv()
        for cp in copies:
            cp.wait_send()

    _at_each_chip(run)


def _scatter_shapes(ps):
    return [jax.ShapeDtypeStruct((3,) + p.shape[1:], p.dtype) for p in ps]


def _carried_scatter(ps):
    return _Carried(ps, _scatter_shapes(ps), 3 * len(ps), functools.partial(_start_all, _scatter_copies),
                    functools.partial(_wait_all, _scatter_copies))


def _direct_gather_copies(w_refs, out_refs, send_sems, recv_sems, x, y, arriving):
    c = lax.axis_index("c")
    me = 2 * x + y
    peers = [((px, py, c), 2 * px + py) for px, py in _other_chips(x, y)] + [((x, y, 1 - c), me)]
    copies = []
    for i, (w_ref, out_ref) in enumerate(zip(w_refs, out_refs)):
        for j, (peer, slab) in enumerate(peers):
            copies.append(pltpu.make_async_remote_copy(
                src_ref=w_ref, dst_ref=out_ref.at[slab if arriving else me], send_sem=send_sems.at[4 * i + j],
                recv_sem=recv_sems.at[4 * i + j], device_id=peer, device_id_type=MESH))
    return copies


def _carried_gather(ws):
    def start(w_refs, out_refs, send_sems, recv_sems):
        def run(x, y):
            for cp in _direct_gather_copies(w_refs, out_refs, send_sems, recv_sems, x, y, False):
                cp.start()

        _at_each_chip(run)

    def finish(w_refs, out_refs, send_sems, recv_sems):
        def run(x, y):
            for cp in _direct_gather_copies(w_refs, out_refs, send_sems, recv_sems, x, y, True):
                cp.wait_recv()
            for cp in _direct_gather_copies(w_refs, out_refs, send_sems, recv_sems, x, y, False):
                cp.wait_send()

        _at_each_chip(run)

    outs = [jax.ShapeDtypeStruct((N_SHARD,) + w.shape, w.dtype) for w in ws]
    return _Carried(ws, outs, 4 * len(ws), start, finish)


def _share_halves(hs):
    n = len(hs)

    def body(*refs):
        h_refs, out_refs = refs[:n], refs[n:2 * n]
        send_sems, recv_sems, _ = refs[2 * n:]
        x, y, c = _place()
        sends = []
        for i, (h_ref, out_ref) in enumerate(zip(h_refs, out_refs)):
            sends.append(pltpu.make_async_remote_copy(
                src_ref=h_ref, dst_ref=out_ref, send_sem=send_sems.at[i], recv_sem=recv_sems.at[i],
                device_id=(x, y, 1 - c), device_id_type=MESH))
            sends[-1].start()
        for cp in sends:
            cp.wait()

    outs = [jax.ShapeDtypeStruct(h.shape, h.dtype) for h in hs]
    return _exchange_call("share_grad_halves", body, hs, outs, n, 1)


def _scalar_grid_call(name, body, scalars, grid, in_specs, out_specs, out_shape, args):
    grid_spec = pltpu.PrefetchScalarGridSpec(num_scalar_prefetch=len(scalars), grid=grid, in_specs=in_specs,
                                             out_specs=out_specs)
    return pl.pallas_call(body, name=name, grid_spec=grid_spec, out_shape=out_shape,
                          compiler_params=_params(("arbitrary",) * len(grid)))(*scalars, *args)


def _add_pair(name, g, got, core):
    ns_, hr, cols = got.shape
    th = _row_tile(hr, 512)
    nb = hr // th

    def body(core_ref, g_ref, got_ref, out_ref):
        out_ref[...] = (g_ref[...].astype(F32) + got_ref[...].astype(F32)).astype(out_ref.dtype)

    blk = pl.BlockSpec((1, th, cols), lambda q, j, core_ref: (q, j, 0))
    own = pl.BlockSpec((1, th, cols), lambda q, j, core_ref: (q, core_ref[0] * nb + j, 0))
    return _scalar_grid_call(name, body, [core], (ns_, nb), [own, blk], blk,
                             jax.ShapeDtypeStruct(got.shape, got.dtype), [g, got])


def _add_chips(name, pairs, slabs, chip):
    _, hr, cols = slabs.shape
    th = _row_tile(hr, 512)

    def body(chip_ref, own_ref, s0_ref, s1_ref, s2_ref, out_ref):
        total = own_ref[0].astype(F32) + s0_ref[0].astype(F32)
        out_ref[...] = (total + s1_ref[0].astype(F32)) + s2_ref[0].astype(F32)

    own = pl.BlockSpec((1, th, cols), lambda j, chip_ref: (chip_ref[0], j, 0))
    others = [pl.BlockSpec((1, th, cols), lambda j, chip_ref, k=k: (k, j, 0)) for k in range(3)]
    return _scalar_grid_call(name, body, [chip], (hr // th,), [own] + others,
                             pl.BlockSpec((th, cols), lambda j, chip_ref: (j, 0)),
                             jax.ShapeDtypeStruct((hr, cols), F32), [pairs, slabs, slabs, slabs])


def _join_halves(name, mine, other, core):
    hr, cols = mine.shape
    th = _row_tile(hr, 512)
    nb = hr // th

    def body(core_ref, mine_ref, other_ref, out_ref):
        is_mine = pl.program_id(0) == core_ref[0]

        @pl.when(is_mine)
        def _():
            out_ref[0] = mine_ref[...]

        @pl.when(jnp.logical_not(is_mine))
        def _():
            out_ref[0] = other_ref[...]

    blk = pl.BlockSpec((th, cols), lambda h, j, core_ref: (j, 0))
    return _scalar_grid_call(name, body, [core], (2, nb), [blk, blk],
                             pl.BlockSpec((1, th, cols), lambda h, j, core_ref: (0, h * nb + j, 0)),
                             jax.ShapeDtypeStruct((1, 2 * hr, cols), mine.dtype), [mine, other])


def _gather_small(sp):
    def body(s_ref, out_ref, send_sems, recv_sems, local_sem):
        x, y, c = _place()
        me = 4 * x + 2 * y + c
        peers = [(x ^ (m >> 2), y ^ ((m >> 1) & 1), c ^ (m & 1)) for m in range(1, 8)]
        mine = pltpu.make_async_copy(s_ref, out_ref.at[me], local_sem)
        mine.start()
        sends = [pltpu.make_async_remote_copy(src_ref=s_ref, dst_ref=out_ref.at[me], send_sem=send_sems.at[j],
                                              recv_sem=recv_sems.at[j], device_id=p, device_id_type=MESH)
                 for j, p in enumerate(peers)]
        for cp in sends:
            cp.start()
        for j, (px, py, pc) in enumerate(peers):
            pltpu.make_async_remote_copy(src_ref=s_ref, dst_ref=out_ref.at[4 * px + 2 * py + pc],
                                         send_sem=send_sems.at[j], recv_sem=recv_sems.at[j], device_id=(px, py, pc),
                                         device_id_type=MESH).wait_recv()
        for cp in sends:
            cp.wait_send()
        mine.wait()

    return pl.pallas_call(
        body, name="gather_small_grads", in_specs=[HBM_SPEC], out_specs=HBM_SPEC,
        out_shape=jax.ShapeDtypeStruct((8,) + sp.shape, sp.dtype),
        scratch_shapes=[pltpu.SemaphoreType.DMA((7,)), pltpu.SemaphoreType.DMA((7,)), pltpu.SemaphoreType.DMA])(sp)


def _pack_rows(total):
    rows = -(-total // LANES)
    return -(-rows // 32) * 32


def _pack(arrs, dtype):
    flat = jnp.concatenate([a.reshape(-1).astype(dtype) for a in arrs])
    rows = _pack_rows(flat.shape[0])
    return jnp.pad(flat, (0, rows * LANES - flat.shape[0])).reshape(rows, LANES)


def _unpack(buf, shapes):
    flat = buf.reshape(-1)
    out, off = {}, 0
    for n, shp in shapes:
        size = shp[0] * shp[1]
        out[n] = flat[off:off + size].reshape(shp)
        off += size
    return out


def _to_wire(name, w3):
    _, r, cols = w3.shape
    tb = _row_tile(r, 512)

    def body(w_ref, o_ref):
        o_ref[...] = w_ref[0].astype(o_ref.dtype)

    return pl.pallas_call(
        body, name=name, grid=(r // tb,), in_specs=[pl.BlockSpec((1, tb, cols), lambda i: (0, i, 0))],
        out_specs=pl.BlockSpec((tb, cols), lambda i: (i, 0)), out_shape=jax.ShapeDtypeStruct((r, cols), MM_DTYPE),
        compiler_params=_params(("arbitrary",)))(w3)


def _adamw(name, w3, g, m3, v3, tb):
    c1 = 1.0 - ADAM_B1 ** ADAM_STEP
    c2 = 1.0 - ADAM_B2 ** ADAM_STEP
    _, r, cols = w3.shape
    emit = g.ndim == 2
    blk3 = pl.BlockSpec((1, tb, cols), lambda i: (0, i, 0))
    g_spec = pl.BlockSpec((tb, cols), lambda i: (i, 0)) if emit else blk3

    def body(w_ref, g_ref, m_ref, v_ref, *out_refs):
        gb = g_ref[...] if emit else g_ref[0]
        m2 = ADAM_B1 * m_ref[0] + (1.0 - ADAM_B1) * gb
        v2 = ADAM_B2 * v_ref[0] + (1.0 - ADAM_B2) * (gb * gb)
        out_refs[-3][0] = -ADAM_LR * ((m2 / c1) / (jnp.sqrt(v2 / c2) + ADAM_EPS) + ADAM_WD * w_ref[0])
        out_refs[-2][0] = m2
        out_refs[-1][0] = v2
        if emit:
            out_refs[0][0] = gb

    n_out = 4 if emit else 3
    outs = pl.pallas_call(
        body, name=name, grid=(r // tb,), in_specs=[blk3, g_spec, blk3, blk3], out_specs=[blk3] * n_out,
        out_shape=[jax.ShapeDtypeStruct((1, r, cols), F32)] * n_out,
        compiler_params=_params(("arbitrary",)))(w3, g, m3, v3)
    return outs if emit else [g] + list(outs)


def _row_tile(rows, pref):
    if rows <= pref:
        return rows
    t = pref
    while t >= 8:
        if rows % t == 0 and t % 8 == 0:
            return t
        t -= 8
    return rows


def kernel(x, positions, ffn1_pre_g, ffn1_w_gate, ffn1_w_up, ffn1_w_down, ffn1_post_g, mix_pre_g, w_in, mla_q_norm_g, mla_w_uq, mla_kv_norm_g, mla_w_ukv, mla_out_g, gdn_conv_w, gdn_a_log, gdn_dt_bias, gdn_norm_g, w_out, mix_post_g, ffn2_pre_g, ffn2_w_gate, ffn2_w_up, ffn2_w_down, ffn2_post_g, loss_target, m_ffn1_pre_g, m_ffn1_w_gate, m_ffn1_w_up, m_ffn1_w_down, m_ffn1_post_g, m_mix_pre_g, m_w_in, m_mla_q_norm_g, m_mla_w_uq, m_mla_kv_norm_g, m_mla_w_ukv, m_mla_out_g, m_gdn_conv_w, m_gdn_a_log, m_gdn_dt_bias, m_gdn_norm_g, m_w_out, m_mix_post_g, m_ffn2_pre_g, m_ffn2_w_gate, m_ffn2_w_up, m_ffn2_w_down, m_ffn2_post_g, v_ffn1_pre_g, v_ffn1_w_gate, v_ffn1_w_up, v_ffn1_w_down, v_ffn1_post_g, v_mix_pre_g, v_w_in, v_mla_q_norm_g, v_mla_w_uq, v_mla_kv_norm_g, v_mla_w_ukv, v_mla_out_g, v_gdn_conv_w, v_gdn_a_log, v_gdn_dt_bias, v_gdn_norm_g, v_w_out, v_mix_post_g, v_ffn2_pre_g, v_ffn2_w_gate, v_ffn2_w_up, v_ffn2_w_down, v_ffn2_post_g):
    args = dict(locals())
    wsh = {n: args[n][0] for n in WEIGHTS}
    msh = {n: args["m_" + n] for n in SMALL}
    vsh = {n: args["v_" + n] for n in SMALL}
    for n in SMALL:
        wsh[n] = args[n]
    mix_shapes = [(n, wsh[n].shape) for n in MIX_BIG]

    early = FFN_BIG[:3]
    held = lambda a, n: jnp.swapaxes(a, 1, 2) if n in TRANSPOSED else a
    w_in_wire = jnp.pad(held(w_in, "w_in")[0].astype(MM_DTYPE), ((0, W_IN_SHARD_PAD - W_IN_SHARD), (0, 0)))
    gathered = _gather_shards([_to_wire("wire_" + n, held(args[n], n)) for n in early]
                              + [w_in_wire, _pack([wsh[n] for n in MIX_BIG], MM_DTYPE)])
    full = {n: wsh[n] for n in SMALL}
    for n, gw in zip(early + ["w_in"], gathered):
        full[n] = gw
    parts = [_unpack(gathered[-1][q], mix_shapes) for q in range(N_SHARD)]
    for n in MIX_BIG:
        full[n] = jnp.concatenate([parts[q][n] for q in range(N_SHARD)], axis=SHARD_AXIS[n])

    core = lax.axis_index("c").astype(jnp.int32).reshape(1)
    chip = (2 * lax.axis_index("x") + lax.axis_index("y")).astype(jnp.int32).reshape(1)
    late = ([_to_wire("wire_" + n, held(args[n], n)) for n in FFN2_BIG], core, chip)
    lsum, grad_x, g = _local_step(x[0], positions, loss_target[0], full, late)
    loss = lax.psum(0.5 * jnp.sum(lsum) / x.shape[-1], ("x", "y", "c"))

    halves = [g[n] for n in FFN_BIG] + [g["w_in"], g["mix_pack"]]
    others = _share_halves(halves)
    shared = [_join_halves("join_halves_%d" % i, hm, ho, core) for i, (hm, ho) in enumerate(zip(halves, others))]
    gsh = _unpack(shared[-1], mix_shapes)
    for n, sg_ in zip(FFN_BIG, shared):
        gsh[n] = sg_
    gsh["w_in"] = shared[-2][:, :W_IN_SHARD]

    small_shapes = [(n, wsh[n].shape) for n in SMALL]
    pack_small = lambda d: jnp.concatenate(
        [_pad_lanes(d[n].astype(F32), LANES) for n in SMALL] + [jnp.zeros((SMALL_ROWS - len(SMALL), LANES), F32)], axis=0)
    slots = _gather_small(pack_small(g))

    c1 = 1.0 - ADAM_B1 ** ADAM_STEP
    c2 = 1.0 - ADAM_B2 ** ADAM_STEP

    def small_update(wb, mb, vb, s8):
        gs = s8[0:SMALL_ROWS]
        for d in range(1, 8):
            gs = gs + s8[d * SMALL_ROWS:(d + 1) * SMALL_ROWS]
        m2 = ADAM_B1 * mb + (1.0 - ADAM_B1) * gs
        v2 = ADAM_B2 * vb + (1.0 - ADAM_B2) * (gs * gs)
        delta = -ADAM_LR * ((m2 / c1) / (jnp.sqrt(v2 / c2) + ADAM_EPS) + ADAM_WD * wb)
        return gs, delta, m2, v2

    sg, sd, sm, sv_ = _rowwise("adamw_small", small_update,
                               [pack_small(wsh), pack_small(msh), pack_small(vsh)],
                               [slots.reshape(8 * SMALL_ROWS, LANES)], [(LANES, F32)] * 4, [], SMALL_ROWS)
    grads, deltas, new_m, new_v = {}, {}, {}, {}
    for i, (n, shp) in enumerate(small_shapes):
        grads[n], deltas[n] = sg[i:i + 1, :shp[1]], sd[i:i + 1, :shp[1]]
        new_m[n], new_v[n] = sm[i:i + 1, :shp[1]], sv_[i:i + 1, :shp[1]]
    for n in BIG:
        w3 = held(args[n], n)
        outs = _adamw("adamw_" + n, w3, gsh[n], held(args["m_" + n], n), held(args["v_" + n], n),
                      _row_tile(w3.shape[1], 256))
        grads[n], deltas[n], new_m[n], new_v[n] = [held(o, n) for o in outs]

    return (loss, grad_x[None], *[grads[n] for n in WEIGHTS], *[deltas[n] for n in WEIGHTS],
            *[new_m[n] for n in WEIGHTS], *[new_v[n] for n in WEIGHTS])
```

```python
import functools

import jax
import jax.numpy as jnp
from jax import lax
from jax.experimental import pallas as pl
from jax.experimental.pallas import tpu as pltpu

F32 = jnp.float32
BF16 = jnp.bfloat16
MM_DTYPE = BF16
MESH = pl.DeviceIdType.MESH

D_MODEL = 1024
D_FF = 2816
N_HEADS = 8
MLA_Q_RANK = 256
MLA_KV_RANK = 128
MLA_NOPE = 64
MLA_ROPE = 32
MLA_V = 64
ROPE_THETA = 10000.0
GDN_DH = 64
GDN_W = N_HEADS * GDN_DH
GDN_CONV = 4
CHUNK = 64
GDN_STEP_CHUNKS = 2
HEAD_LANES = 128
HEADS_PER_STEP = 4
MLA_PAD = N_HEADS * HEAD_LANES
EPS = 1e-6
N_SHARD = 4
LANES = 1024

PIN_QKV = 0
PIN_MLA = 1536
PIN_KPE = 1920
PIN_GATE = 2048
PIN_AB = 2560
PIN_W = 2688
CAT_W = MLA_PAD + GDN_W

ADAM_LR = 0.001
ADAM_B1 = 0.9
ADAM_B2 = 0.999
ADAM_EPS = 1e-08
ADAM_WD = 0.01
ADAM_STEP = 10

VMEM_LIMIT_V7X = 56 * 1024 * 1024

BIG = ["ffn1_w_gate", "ffn1_w_up", "ffn1_w_down", "w_in", "mla_w_uq", "mla_w_ukv", "gdn_conv_w", "w_out",
       "ffn2_w_gate", "ffn2_w_up", "ffn2_w_down"]
FFN_BIG = ["ffn1_w_gate", "ffn1_w_up", "ffn1_w_down", "ffn2_w_gate", "ffn2_w_up", "ffn2_w_down"]
TRANSPOSED = ["ffn1_w_gate", "ffn1_w_up", "ffn2_w_gate", "ffn2_w_up", "w_in"]
MIX_BIG = ["mla_w_uq", "mla_w_ukv", "gdn_conv_w", "w_out"]
SMALL = ["ffn1_pre_g", "ffn1_post_g", "mix_pre_g", "mla_q_norm_g", "mla_kv_norm_g", "mla_out_g", "gdn_a_log",
         "gdn_dt_bias", "gdn_norm_g", "mix_post_g", "ffn2_pre_g", "ffn2_post_g"]
WEIGHTS = ["ffn1_pre_g", "ffn1_w_gate", "ffn1_w_up", "ffn1_w_down", "ffn1_post_g", "mix_pre_g", "w_in",
           "mla_q_norm_g", "mla_w_uq", "mla_kv_norm_g", "mla_w_ukv", "mla_out_g", "gdn_conv_w", "gdn_a_log",
           "gdn_dt_bias", "gdn_norm_g", "w_out", "mix_post_g", "ffn2_pre_g", "ffn2_w_gate", "ffn2_w_up",
           "ffn2_w_down", "ffn2_post_g"]
SHARD_AXIS = {"ffn1_w_gate": 1, "ffn1_w_up": 1, "ffn1_w_down": 0, "w_in": 1, "mla_w_uq": 1, "mla_w_ukv": 1,
              "gdn_conv_w": 1, "w_out": 0, "ffn2_w_gate": 1, "ffn2_w_up": 1, "ffn2_w_down": 0}
SMALL_ROWS = 16


def _params(sem):
    return pltpu.CompilerParams(dimension_semantics=sem, vmem_limit_bytes=VMEM_LIMIT_V7X)


def _pick(dim, pref):
    if dim <= pref:
        return dim
    t = (pref // 128) * 128
    while t >= 128:
        if dim % t == 0:
            return t
        t -= 128
    return dim


ANY_SPEC = pl.BlockSpec(memory_space=pl.ANY)


def _rowwise(name, fn, row_ins, bc_ins, row_outs, acc_outs, tb, wide=None, carry=None):
    ents = []
    for e in row_ins:
        ents.append(e if isinstance(e, tuple) else (e, e.shape[1], 0, 0))
    over = [o[2] for o in row_outs if len(o) == 3]
    rows = over[0] if over else ents[0][0].shape[0]
    steps = rows // tb
    assert steps * tb == rows, (name, rows, tb)
    in_specs, args = [], []
    for a, w, j, r0 in ents:
        in_specs.append(pl.BlockSpec((tb, w), lambda i, j=j, r0=r0: (i + r0, j)))
        args.append(a)
    for b in bc_ins:
        in_specs.append(pl.BlockSpec(b.shape, lambda i: (0, 0)))
        args.append(b)
    n_in = len(args)
    aliases = {}
    if carry is not None:
        in_specs.append(ANY_SPEC)
        args.append(carry)
        aliases = {n_in: 0}
    out_shape = [jax.ShapeDtypeStruct((rows, o[0]), o[1]) for o in row_outs]
    out_specs = [pl.BlockSpec((tb, o[0]), lambda i: (i, 0)) for o in row_outs]
    if wide is not None:
        out_shape[0] = jax.ShapeDtypeStruct((rows, wide[0]), row_outs[0][1])
        out_specs[0] = pl.BlockSpec((tb, row_outs[0][0]), lambda i: (i, wide[1]))
    out_shape += [jax.ShapeDtypeStruct((r, c), F32) for r, c in acc_outs]
    out_specs += [pl.BlockSpec((r, c), lambda i: (0, 0)) for r, c in acc_outs]
    n_ro, n_acc, n_args = len(row_outs), len(acc_outs), len(args)

    def body(*refs):
        vals = fn(*[r[...] for r in refs[:n_in]])
        if not isinstance(vals, (tuple, list)):
            vals = (vals,)
        for r, v in zip(refs[n_args:n_args + n_ro], vals[:n_ro]):
            r[...] = v.astype(r.dtype)
        if n_acc:
            acc_refs = refs[n_args + n_ro:]

            @pl.when(pl.program_id(0) == 0)
            def _():
                for r in acc_refs:
                    r[...] = jnp.zeros(r.shape, r.dtype)

            for r, v in zip(acc_refs, vals[n_ro:]):
                r[...] += v

    outs = pl.pallas_call(body, name=name, grid=(steps,), in_specs=in_specs, out_specs=out_specs,
                          out_shape=out_shape, input_output_aliases=aliases,
                          compiler_params=_params(("arbitrary",)))(*args)
    return outs


def _mm(name, a, b, mode, out_dtype, tm=1024, tn=1024, tk=1024):
    if mode == "nn":
        (m, k), (k2, n) = a.shape, b.shape
    elif mode == "nt":
        (m, k), (n, k2) = a.shape, b.shape
    else:
        (k, m), (k2, n) = a.shape, b.shape
    assert k == k2, (name, a.shape, b.shape)
    tm, tn, tk = _pick(m, tm), _pick(n, tn), _pick(k, tk)
    nk = k // tk
    if mode == "nn":
        a_spec = pl.BlockSpec((tm, tk), lambda i, j, kk: (i, kk))
        b_spec = pl.BlockSpec((tk, tn), lambda i, j, kk: (kk, j))
        dims = (((1,), (0,)), ((), ()))
    elif mode == "nt":
        a_spec = pl.BlockSpec((tm, tk), lambda i, j, kk: (i, kk))
        b_spec = pl.BlockSpec((tn, tk), lambda i, j, kk: (j, kk))
        dims = (((1,), (1,)), ((), ()))
    else:
        a_spec = pl.BlockSpec((tk, tm), lambda i, j, kk: (kk, i))
        b_spec = pl.BlockSpec((tk, tn), lambda i, j, kk: (kk, j))
        dims = (((0,), (0,)), ((), ()))

    def body(a_ref, b_ref, o_ref, acc_ref):
        kk = pl.program_id(2)

        @pl.when(kk == 0)
        def _():
            acc_ref[...] = jnp.zeros(acc_ref.shape, F32)

        acc_ref[...] += lax.dot_general(a_ref[...].astype(MM_DTYPE), b_ref[...].astype(MM_DTYPE), dims,
                                        preferred_element_type=F32)

        @pl.when(kk == nk - 1)
        def _():
            o_ref[...] = acc_ref[...].astype(o_ref.dtype)

    return pl.pallas_call(
        body, name=name, grid=(m // tm, n // tn, nk), in_specs=[a_spec, b_spec],
        out_specs=pl.BlockSpec((tm, tn), lambda i, j, kk: (i, j)),
        out_shape=jax.ShapeDtypeStruct((m, n), out_dtype),
        scratch_shapes=[pltpu.VMEM((tm, tn), F32)],
        compiler_params=_params(("parallel", "parallel", "arbitrary")))(a, b)


def _rms_stats(x, n_real=None):
    n = x.shape[-1] if n_real is None else n_real
    return lax.rsqrt(jnp.sum(x * x, axis=-1, keepdims=True) / n + EPS)


def _rms_bwd(x, r, g, dz, n_real=None):
    n = x.shape[-1] if n_real is None else n_real
    xh = x * r
    dxh = dz * g
    dx = r * (dxh - xh * (jnp.sum(dxh * xh, axis=-1, keepdims=True) / n))
    return dx, jnp.sum(dz * xh, axis=0, keepdims=True)


def _sigmoid(x):
    return 0.5 * jnp.tanh(0.5 * x) + 0.5


def _roll(x, s, axis):
    return pltpu.roll(x, s, axis)


def _rope(x, c, s1, s2):
    return x * c + _roll(x, HEAD_LANES - MLA_ROPE // 2, 1) * s1 + _roll(x, MLA_ROPE // 2, 1) * s2


def _heads_apply(x, fn):
    return jnp.concatenate([fn(x[:, h * HEAD_LANES:(h + 1) * HEAD_LANES]) for h in range(N_HEADS)], axis=1)


ROW_CHUNK = 256


def _row_chunks(rows):
    step = min(ROW_CHUNK, rows)
    return [pl.ds(r, step) for r in range(0, rows, step)]


def _ffn_fwd(tag, x, g_pre, wg, wu, wd, g_post, tm):
    t, d = x.shape
    ns, fs, _ = wg.shape
    nt = t // tm
    row = pl.BlockSpec((tm, d), lambda i, q: (i, 0))
    vec = pl.BlockSpec((1, d), lambda i, q: (0, 0))
    act3 = pl.BlockSpec((1, tm, fs), lambda i, q: (q, i, 0))
    wrow = pl.BlockSpec((1, fs, d), lambda i, q: (q, 0, 0))
    nt_dims = (((1,), (1,)), ((), ()))

    def gate_up(x_ref, g_ref, wg_ref, wu_ref, n_ref, sl_ref, ud_ref, s_ref, n_s):
        @pl.when(pl.program_id(1) == 0)
        def _():
            for r in _row_chunks(tm):
                xb = x_ref[r, :]
                n_s[r, :] = (xb * _rms_stats(xb) * g_ref[...]).astype(MM_DTYPE)
            n_ref[...] = n_s[...]

        for r in _row_chunks(tm):
            n = n_s[r, :]
            a = lax.dot_general(n, wg_ref[0], nt_dims, preferred_element_type=F32)
            u = lax.dot_general(n, wu_ref[0], nt_dims, preferred_element_type=F32)
            sg = _sigmoid(a)
            sl = a * sg
            sl_ref[0, r, :] = sl.astype(sl_ref.dtype)
            ud_ref[0, r, :] = (u * (sg + sl * (1.0 - sg))).astype(ud_ref.dtype)
            s_ref[0, r, :] = (sl * u).astype(s_ref.dtype)

    n, sl, ud, s = pl.pallas_call(
        gate_up, name=tag + "_gate_up", grid=(nt, ns), in_specs=[row, vec, wrow, wrow],
        out_specs=[row, act3, act3, act3],
        out_shape=[jax.ShapeDtypeStruct((t, d), MM_DTYPE)] + [jax.ShapeDtypeStruct((ns, t, fs), MM_DTYPE)] * 3,
        scratch_shapes=[pltpu.VMEM((tm, d), MM_DTYPE)],
        compiler_params=_params(("parallel", "arbitrary")))(x, g_pre, wg, wu)

    def down(s_ref, wd_ref, x_ref, g_ref, h_ref, y_ref, acc):
        q = pl.program_id(1)

        @pl.when(q == 0)
        def _():
            acc[...] = jnp.zeros(acc.shape, F32)

        for r in _row_chunks(tm):
            acc[r, :] += jnp.dot(s_ref[0, r, :], wd_ref[0], preferred_element_type=F32)

        @pl.when(q == ns - 1)
        def _():
            for r in _row_chunks(tm):
                hb = acc[r, :]
                h_ref[r, :] = hb
                y_ref[r, :] = x_ref[r, :] + 0.5 * (hb * _rms_stats(hb) * g_ref[...])

    h, y = pl.pallas_call(
        down, name=tag + "_down", grid=(nt, ns), in_specs=[act3, wrow, row, vec], out_specs=[row, row],
        out_shape=[jax.ShapeDtypeStruct((t, d), F32)] * 2, scratch_shapes=[pltpu.VMEM((tm, d), F32)],
        compiler_params=_params(("parallel", "arbitrary")))(s, wd, x, g_post)
    return y, (x, n, sl, ud, s, h)


def _carry(body, n_in, n_out, grid, carried):
    if carried is None:
        return body, [], [], [], [], []
    nx_in, nx_out = len(carried.ins), len(carried.outs)

    def wrapped(*refs):
        ins, rest = refs[:n_in], refs[n_in:]
        xi, rest = rest[:nx_in], rest[nx_in:]
        outs, rest = rest[:n_out], rest[n_out:]
        xo, rest = rest[:nx_out], rest[nx_out:]
        scr, sems = rest[:len(rest) - 2], rest[len(rest) - 2:]
        first, last = True, True
        for dim, size in enumerate(grid):
            first = first & (pl.program_id(dim) == 0)
            last = last & (pl.program_id(dim) == size - 1)

        @pl.when(first)
        def _():
            carried.start(xi, xo, *sems)

        body(*ins, *outs, *scr)

        @pl.when(last)
        def _():
            carried.finish(xi, xo, *sems)

    sems = [pltpu.SemaphoreType.DMA((carried.n_sem,)), pltpu.SemaphoreType.DMA((carried.n_sem,))]
    return (wrapped, [HBM_SPEC] * nx_in, [HBM_SPEC] * nx_out, list(carried.outs), sems, list(carried.ins))


def _ffn_bwd(tag, dy, saved, g_pre, wg, wu, wd, g_post, tm, tk, carried_down=None, make_carried_mid=None,
             make_carried_up=None):
    x, n, sl, ud, s, h = saved
    t, d = x.shape
    ns, fs, _ = wg.shape
    nt, nk = t // tm, t // tk
    row = pl.BlockSpec((tm, d), lambda i, q: (i, 0))
    vec = pl.BlockSpec((1, d), lambda i, q: (0, 0))
    act3 = pl.BlockSpec((1, tm, fs), lambda i, q: (q, i, 0))
    wrow = pl.BlockSpec((1, fs, d), lambda i, q: (q, 0, 0))
    nt_dims = (((1,), (1,)), ((), ()))
    tn_dims = (((0,), (0,)), ((), ()))

    def down_b(h_ref, dy_ref, g_ref, wd_ref, sl_ref, ud_ref, dh_ref, da_ref, du_ref, dg_ref, dh_s):
        i, q = pl.program_id(0), pl.program_id(1)

        @pl.when((i == 0) & (q == 0))
        def _():
            dg_ref[...] = jnp.zeros(dg_ref.shape, F32)

        @pl.when(q == 0)
        def _():
            for r in _row_chunks(tm):
                hb = h_ref[r, :]
                dh, dg = _rms_bwd(hb, _rms_stats(hb), g_ref[...], 0.5 * dy_ref[r, :])
                dh_s[r, :] = dh.astype(MM_DTYPE)
                dg_ref[...] += dg
            dh_ref[...] = dh_s[...]

        for r in _row_chunks(tm):
            ds = lax.dot_general(dh_s[r, :], wd_ref[0], nt_dims, preferred_element_type=F32)
            da_ref[0, r, :] = (ds * ud_ref[0, r, :].astype(F32)).astype(da_ref.dtype)
            du_ref[0, r, :] = (ds * sl_ref[0, r, :].astype(F32)).astype(du_ref.dtype)

    down_b, x_in, x_out, x_shape, x_scr, x_args = _carry(down_b, 6, 4, (nt, ns), carried_down)
    dh, da, du, dg_post, *from_down = pl.pallas_call(
        down_b, name=tag + "_down_b", grid=(nt, ns), in_specs=[row, row, vec, wrow, act3, act3] + x_in,
        out_specs=[row, act3, act3, vec] + x_out,
        out_shape=[jax.ShapeDtypeStruct((t, d), MM_DTYPE)] + [jax.ShapeDtypeStruct((ns, t, fs), MM_DTYPE)] * 2
        + [jax.ShapeDtypeStruct((1, d), F32)] + x_shape,
        scratch_shapes=[pltpu.VMEM((tm, d), MM_DTYPE)] + x_scr,
        compiler_params=_params(("arbitrary", "arbitrary")))(h, dy, g_post, wd, sl, ud, *x_args)

    def down_w(s_ref, dh_ref, dw_ref, acc):
        kk = pl.program_id(1)

        @pl.when(kk == 0)
        def _():
            acc[...] = jnp.zeros(acc.shape, F32)

        acc[...] += lax.dot_general(s_ref[0], dh_ref[...], tn_dims, preferred_element_type=F32)

        @pl.when(kk == nk - 1)
        def _():
            dw_ref[0] = acc[...].astype(dw_ref.dtype)

    dwd = pl.pallas_call(
        down_w, name=tag + "_down_w", grid=(ns, nk),
        in_specs=[pl.BlockSpec((1, tk, fs), lambda q, kk: (q, kk, 0)), pl.BlockSpec((tk, d), lambda q, kk: (kk, 0))],
        out_specs=pl.BlockSpec((1, fs, d), lambda q, kk: (q, 0, 0)),
        out_shape=jax.ShapeDtypeStruct((ns, fs, d), MM_DTYPE), scratch_shapes=[pltpu.VMEM((fs, d), F32)],
        compiler_params=_params(("parallel", "arbitrary")))(s, dh)

    def gate_up_b(da_ref, du_ref, wg_ref, wu_ref, x_ref, dy_ref, g_ref, dx_ref, dg_ref, acc):
        i, q = pl.program_id(0), pl.program_id(1)

        @pl.when((i == 0) & (q == 0))
        def _():
            dg_ref[...] = jnp.zeros(dg_ref.shape, F32)

        @pl.when(q == 0)
        def _():
            acc[...] = jnp.zeros(acc.shape, F32)

        for r in _row_chunks(tm):
            acc[r, :] += (jnp.dot(da_ref[0, r, :], wg_ref[0], preferred_element_type=F32)
                          + jnp.dot(du_ref[0, r, :], wu_ref[0], preferred_element_type=F32))

        @pl.when(q == ns - 1)
        def _():
            for r in _row_chunks(tm):
                xb = x_ref[r, :]
                dx, dg = _rms_bwd(xb, _rms_stats(xb), g_ref[...], acc[r, :])
                dx_ref[r, :] = dy_ref[r, :] + dx
                dg_ref[...] += dg

    def gate_up_w(n_ref, da_ref, du_ref, dwg_ref, dwu_ref, acc_g, acc_u):
        kk = pl.program_id(1)

        @pl.when(kk == 0)
        def _():
            acc_g[...] = jnp.zeros(acc_g.shape, F32)
            acc_u[...] = jnp.zeros(acc_u.shape, F32)

        nb = n_ref[...]
        acc_g[...] += lax.dot_general(da_ref[0], nb, tn_dims, preferred_element_type=F32)
        acc_u[...] += lax.dot_general(du_ref[0], nb, tn_dims, preferred_element_type=F32)

        @pl.when(kk == nk - 1)
        def _():
            dwg_ref[0] = acc_g[...].astype(dwg_ref.dtype)
            dwu_ref[0] = acc_u[...].astype(dwu_ref.dtype)

    k3 = pl.BlockSpec((1, tk, fs), lambda q, kk: (q, kk, 0))
    wout = pl.BlockSpec((1, fs, d), lambda q, kk: (q, 0, 0))
    carried_mid = make_carried_mid(dwd) if make_carried_mid else None
    gate_up_w, x_in, x_out, x_shape, x_scr, x_args = _carry(gate_up_w, 3, 2, (ns, nk), carried_mid)
    dwg, dwu, *from_mid = pl.pallas_call(
        gate_up_w, name=tag + "_gate_up_w", grid=(ns, nk),
        in_specs=[pl.BlockSpec((tk, d), lambda q, kk: (kk, 0)), k3, k3] + x_in, out_specs=[wout, wout] + x_out,
        out_shape=[jax.ShapeDtypeStruct((ns, fs, d), MM_DTYPE)] * 2 + x_shape,
        scratch_shapes=[pltpu.VMEM((fs, d), F32)] * 2 + x_scr,
        compiler_params=_params(("arbitrary", "arbitrary")))(n, da, du, *x_args)

    carried_up = make_carried_up(dwg, dwu) if make_carried_up else None
    gate_up_b, x_in, x_out, x_shape, x_scr, x_args = _carry(gate_up_b, 7, 2, (nt, ns), carried_up)
    dx, dg_pre, *from_up = pl.pallas_call(
        gate_up_b, name=tag + "_gate_up_b", grid=(nt, ns), in_specs=[act3, act3, wrow, wrow, row, row, vec] + x_in,
        out_specs=[row, vec] + x_out,
        out_shape=[jax.ShapeDtypeStruct((t, d), F32), jax.ShapeDtypeStruct((1, d), F32)] + x_shape,
        scratch_shapes=[pltpu.VMEM((tm, d), F32)] + x_scr,
        compiler_params=_params(("arbitrary", "arbitrary")))(da, du, wg, wu, x, dy, g_pre, *x_args)
    return dx, dg_pre, dwg, dwu, dwd, dg_post, from_down, from_mid, from_up


NEG = -1e30


def _attn_scale():
    return (MLA_NOPE + MLA_ROPE) ** -0.5


def _causal_pairs(nq, by_key):
    if by_key:
        pairs = [(qi, ki) for ki in range(nq) for qi in range(ki, nq)]
    else:
        pairs = [(qi, ki) for qi in range(nq) for ki in range(qi + 1)]
    return jnp.asarray([p[0] for p in pairs], jnp.int32), jnp.asarray([p[1] for p in pairs], jnp.int32)


def _below_diagonal(shape):
    return lax.broadcasted_iota(jnp.int32, shape, 1) <= lax.broadcasted_iota(jnp.int32, shape, 0)


def _attn_call(name, body, tables, args, in_kinds, out_kinds, scratch, t, tq, carried=None):
    qmap = lambda h, p, qt, kt: (qt[p], h)
    kmap = lambda h, p, qt, kt: (kt[p], h)
    width = HEADS_PER_STEP * HEAD_LANES
    spec = lambda kind: pl.BlockSpec((tq, width), qmap if kind == "q" else kmap)
    n_pairs = tables[0].shape[0]
    n_groups = N_HEADS // HEADS_PER_STEP
    n_in, n_out, n_scr = len(in_kinds), len(out_kinds), scratch
    x_ins = list(carried.ins) if carried else []
    x_outs = list(carried.outs) if carried else []
    x_scr = [pltpu.SemaphoreType.DMA((carried.n_sem,)), pltpu.SemaphoreType.DMA((carried.n_sem,))] if carried else []

    def full_body(qt, kt, *refs):
        ins, refs = refs[:n_in], refs[n_in:]
        xi, refs = refs[:len(x_ins)], refs[len(x_ins):]
        outs, refs = refs[:n_out], refs[n_out:]
        xo, refs = refs[:len(x_outs)], refs[len(x_outs):]
        scr, sems = refs[:n_scr], refs[n_scr:]
        if carried:
            @pl.when((pl.program_id(0) == 0) & (pl.program_id(1) == 0))
            def _():
                carried.start(xi, xo, *sems)

        heads = [tuple(r.at[:, pl.ds(hh * HEAD_LANES, HEAD_LANES)] for r in (*ins, *outs, *scr))
                 for hh in range(HEADS_PER_STEP)]
        body(qt, kt, heads)
        if carried:
            @pl.when((pl.program_id(0) == n_groups - 1) & (pl.program_id(1) == n_pairs - 1))
            def _():
                carried.finish(xi, xo, *sems)

    grid_spec = pltpu.PrefetchScalarGridSpec(
        num_scalar_prefetch=2, grid=(n_groups, n_pairs),
        in_specs=[spec(kd) for kd in in_kinds] + [HBM_SPEC] * len(x_ins),
        out_specs=[spec(kd) for kd in out_kinds] + [HBM_SPEC] * len(x_outs),
        scratch_shapes=[pltpu.VMEM((tq, width), F32)] * n_scr + x_scr)
    return pl.pallas_call(full_body, name=name, grid_spec=grid_spec,
                          out_shape=[jax.ShapeDtypeStruct((t, MLA_PAD), F32) for _ in out_kinds] + x_outs,
                          compiler_params=_params(("arbitrary", "arbitrary")))(*tables, *args, *x_ins)


class _Carried:
    def __init__(self, ins, outs, n_sem, start, finish):
        self.ins, self.outs, self.n_sem, self.start, self.finish = ins, outs, n_sem, start, finish


def _attn_fwd(q, k, v, tq, carried=None):
    t = q.shape[0]
    nq = t // tq

    def body(qt, kt, heads):
        p_id = pl.program_id(1)
        qi, ki = qt[p_id], kt[p_id]

        @pl.when(ki == 0)
        def _():
            for _, _, _, _, _, m_s, l_s, acc_s in heads:
                m_s[...] = jnp.full(m_s.shape, NEG, F32)
                l_s[...] = jnp.zeros(l_s.shape, F32)
                acc_s[...] = jnp.zeros(acc_s.shape, F32)

        def update(diagonal):
            for q_ref, k_ref, v_ref, _, _, m_s, l_s, acc_s in heads:
                s = lax.dot_general(q_ref[...], k_ref[...], (((1,), (1,)), ((), ())), preferred_element_type=F32)
                if diagonal:
                    s = jnp.where(_below_diagonal(s.shape), s, NEG)
                m_old = m_s[...]
                m_new = jnp.maximum(m_old, jnp.max(s, axis=1, keepdims=True))
                alpha = jnp.exp(m_old - m_new)
                p = jnp.exp(s - m_new[:, :1])
                l_s[...] = l_s[...] * alpha + jnp.sum(p, axis=1, keepdims=True)
                acc_s[...] = acc_s[...] * alpha + jnp.dot(p.astype(MM_DTYPE), v_ref[...], preferred_element_type=F32)
                m_s[...] = m_new

        @pl.when(ki < qi)
        def _():
            update(False)

        @pl.when(ki == qi)
        def _():
            update(True)
            for _, _, _, o_ref, lse_ref, m_s, l_s, acc_s in heads:
                o_ref[...] = acc_s[...] / l_s[...]
                lse_ref[...] = m_s[...] + jnp.log(l_s[...])

    return _attn_call("mla_attn_fwd", body, _causal_pairs(nq, False), (q, k, v), "qkk", "qq", 3, t, tq, carried)


def _attn_probs(q, k, lse, diagonal):
    s = lax.dot_general(q, k, (((1,), (1,)), ((), ())), preferred_element_type=F32)
    p = jnp.exp(s - lse[:, :1])
    return jnp.where(_below_diagonal(s.shape), p, 0.0) if diagonal else p


BWD_HEADS = 2


def _attn_bwd(q, k, v, do, lse, delta, tq, carried=None):
    t = q.shape[0]
    nq = t // tq
    width = BWD_HEADS * HEAD_LANES
    n_groups = N_HEADS // BWD_HEADS
    qt_tab, kt_tab = _causal_pairs(nq, True)
    n_pairs = qt_tab.shape[0]
    qmap = lambda h, p, qt, kt: (qt[p], h)
    kmap = lambda h, p, qt, kt: (kt[p], h)
    qs, ks = pl.BlockSpec((tq, width), qmap), pl.BlockSpec((tq, width), kmap)
    x_ins = list(carried.ins) if carried else []
    x_outs = list(carried.outs) if carried else []
    x_scr = [pltpu.SemaphoreType.DMA((carried.n_sem,)), pltpu.SemaphoreType.DMA((carried.n_sem,))] if carried else []
    nt_dims = (((1,), (1,)), ((), ()))
    tn_dims = (((0,), (0,)), ((), ()))

    def body(qt, kt, q_ref, k_ref, v_ref, do_ref, lse_ref, dl_ref, *rest):
        xi, rest = rest[:len(x_ins)], rest[len(x_ins):]
        dq_hbm, dk_ref, dv_ref = rest[:3]
        xo, rest = rest[3:3 + len(x_outs)], rest[3 + len(x_outs):]
        dk_s, dv_s, dq_s, dq_sem = rest[:4]
        sems = rest[4:]
        grp, p_id = pl.program_id(0), pl.program_id(1)
        qi, ki = qt[p_id], kt[p_id]
        if carried:
            @pl.when((grp == 0) & (p_id == 0))
            def _():
                carried.start(xi, xo, *sems)

        @pl.when(p_id == 0)
        def _():
            dq_s[...] = jnp.zeros(dq_s.shape, F32)

        def step(diagonal):
            rows = pl.ds(pl.multiple_of(qi * tq, tq), tq)
            for hh in range(BWD_HEADS):
                ln = pl.ds(hh * HEAD_LANES, HEAD_LANES)
                qb, kb, vb, dob = q_ref[:, ln], k_ref[:, ln], v_ref[:, ln], do_ref[:, ln]
                p = _attn_probs(qb, kb, lse_ref[:, ln], diagonal)
                dv_s[:, ln] += lax.dot_general(p.astype(MM_DTYPE), dob, tn_dims, preferred_element_type=F32)
                dp = lax.dot_general(dob, vb, nt_dims, preferred_element_type=F32)
                ds = (p * (dp - dl_ref[:, ln][:, :1])).astype(MM_DTYPE)
                dk_s[:, ln] += lax.dot_general(ds, qb, tn_dims, preferred_element_type=F32)
                dq_s[rows, ln] += jnp.dot(ds, kb, preferred_element_type=F32)

        @pl.when(qi == ki)
        def _():
            dk_s[...] = jnp.zeros(dk_s.shape, F32)
            dv_s[...] = jnp.zeros(dv_s.shape, F32)
            step(True)

        @pl.when(qi > ki)
        def _():
            step(False)

        @pl.when(qi == nq - 1)
        def _():
            dk_ref[...] = dk_s[...]
            dv_ref[...] = dv_s[...]

        @pl.when(p_id == n_pairs - 1)
        def _():
            out = pltpu.make_async_copy(dq_s, dq_hbm.at[pl.ds(pl.multiple_of(grp * t, t), t)], dq_sem)
            out.start()
            out.wait()

        if carried:
            @pl.when((grp == n_groups - 1) & (p_id == n_pairs - 1))
            def _():
                carried.finish(xi, xo, *sems)

    grid_spec = pltpu.PrefetchScalarGridSpec(
        num_scalar_prefetch=2, grid=(n_groups, n_pairs),
        in_specs=[qs, ks, ks, qs, qs, qs] + [HBM_SPEC] * len(x_ins),
        out_specs=[HBM_SPEC, ks, ks] + [HBM_SPEC] * len(x_outs),
        scratch_shapes=[pltpu.VMEM((tq, width), F32), pltpu.VMEM((tq, width), F32), pltpu.VMEM((t, width), F32),
                        pltpu.SemaphoreType.DMA] + x_scr)
    return pl.pallas_call(
        body, name="mla_attn_bwd", grid_spec=grid_spec,
        out_shape=[jax.ShapeDtypeStruct((n_groups * t, width), F32), jax.ShapeDtypeStruct((t, MLA_PAD), F32),
                   jax.ShapeDtypeStruct((t, MLA_PAD), F32)] + x_outs,
        compiler_params=_params(("arbitrary", "arbitrary")))(qt_tab, kt_tab, q, k, v, do, lse, delta, *x_ins)


def _dot01(a, b, dims=(((1,), (0,)), ((), ())), ones="rhs"):
    val, sel = (a, b) if ones == "rhs" else (b, a)
    head = val.astype(BF16)
    tail = (val - head.astype(F32)).astype(BF16)
    sel = sel.astype(BF16)
    dot = lambda part: (lax.dot_general(part, sel, dims, preferred_element_type=F32) if ones == "rhs"
                        else lax.dot_general(sel, part, dims, preferred_element_type=F32))
    return dot(head) + dot(tail)


def _dot1(a, b, dims=(((1,), (0,)), ((), ()))):
    return lax.dot_general(a.astype(MM_DTYPE), b.astype(MM_DTYPE), dims, preferred_element_type=F32)


def _dot3(a, b, dims=(((1,), (0,)), ((), ()))):
    return lax.dot_general(a, b, dims, preferred_element_type=F32, precision=lax.Precision.HIGH)


NN3 = (((2,), (1,)), ((0,), (0,)))
NT3 = (((2,), (2,)), ((0,), (0,)))
TN3 = (((1,), (1,)), ((0,), (0,)))


def _tri_masks(nh):
    shape = (nh, CHUNK, CHUNK)
    return lax.broadcasted_iota(jnp.int32, shape, 1), lax.broadcasted_iota(jnp.int32, shape, 2)


def _gdn_chunk_common(k, gcc, bb, row, col, dot=_dot1):
    tril = row >= col
    gcr = jnp.swapaxes(gcc, 1, 2)
    dm = jnp.exp(jnp.where(tril, gcc - gcr, NEG))
    kb = k * bb
    lm = jnp.where(row > col, dot(kb, k, NT3) * dm, 0.0)
    return dm, kb, lm


def _unit_lower_inverse(lm, eye):
    t = eye - lm
    p = lm
    for _ in range(CHUNK.bit_length() - 2):
        p = _dot3(p, p, NN3)
        t = t + _dot3(t, p, NN3)
    return t


def _chunk_sum_matrix(tb, upper):
    r = lax.broadcasted_iota(jnp.int32, (tb, tb), 0)
    c = lax.broadcasted_iota(jnp.int32, (tb, tb), 1)
    same = (r // CHUNK) == (c // CHUNK)
    return (same & ((c >= r) if upper else (c <= r))).astype(F32)


def _gdn_fwd(q, k, v, gb, bb, carried=None):
    nh, t, dh = q.shape
    nchunk = t // CHUNK

    def body(q_ref, k_ref, v_ref, g_ref, b_ref, o_ref, sall_ref, tall_ref, s_s):
        @pl.when(pl.program_id(0) == 0)
        def _():
            s_s[...] = jnp.zeros(s_s.shape, F32)

        row, col = _tri_masks(nh)
        sh = s_s[...]
        for cc in range(cps):
            rows = pl.ds(cc * CHUNK, CHUNK)
            qh, kh, vh, bbh, gcc = q_ref[:, rows, :], k_ref[:, rows, :], v_ref[:, rows, :], b_ref[:, rows, :], \
                g_ref[:, rows, :]
            dm, kb, lm = _gdn_chunk_common(kh, gcc, bbh, row, col)
            eg = jnp.exp(gcc)
            glr = gcc[:, CHUNK - 1:CHUNK, :]
            th = _unit_lower_inverse(lm, (row == col).astype(F32))
            w = _dot1(th, kb * eg, NN3)
            u = _dot1(th, vh * bbh, NN3)
            at = jnp.where(row >= col, _dot1(qh, kh, NT3) * dm, 0.0)
            vn = u - _dot1(w, sh, NN3)
            o_ref[:, rows, :] = _dot1(qh * eg, sh, NN3) + _dot1(at, vn, NN3)
            kd = kh * jnp.exp(glr - gcc)
            sall_ref[:, cc] = sh
            tall_ref[:, rows, :] = th
            sh = sh * jnp.exp(glr) + _dot1(kd, vn, TN3)
        s_s[...] = sh

    cps = GDN_STEP_CHUNKS
    steps = nchunk // cps
    blk = pl.BlockSpec((nh, cps * CHUNK, dh), lambda n: (0, n, 0))
    body, x_in, x_out, x_shape, x_scr, x_args = _carry(body, 5, 3, (steps,), carried)
    return pl.pallas_call(
        body, name="gdn_fwd", grid=(steps,), in_specs=[blk] * 5 + x_in,
        out_specs=[blk, pl.BlockSpec((nh, cps, dh, dh), lambda n: (0, n, 0, 0)), blk] + x_out,
        out_shape=[jax.ShapeDtypeStruct((nh, t, dh), F32), jax.ShapeDtypeStruct((nh, nchunk, dh, dh), F32),
                   jax.ShapeDtypeStruct((nh, t, CHUNK), F32)] + x_shape,
        scratch_shapes=[pltpu.VMEM((nh, dh, dh), F32)] + x_scr,
        compiler_params=_params(("arbitrary",)))(q, k, v, gb, bb, *x_args)


def _gdn_bwd(q, k, v, gb, bb, sall, tall, do):
    nh, t, dh = q.shape
    nchunk = t // CHUNK

    def body(q_ref, k_ref, v_ref, g_ref, b_ref, sall_ref, tall_ref, do_ref,
             dq_ref, dk_ref, dv_ref, dg_ref, db_ref, ds_s):
        @pl.when(pl.program_id(0) == 0)
        def _():
            ds_s[...] = jnp.zeros(ds_s.shape, F32)

        row, col = _tri_masks(nh)
        tril, stril = row >= col, row > col
        rsum = lambda x: jnp.sum(x, axis=2, keepdims=True)
        dsp = ds_s[...]
        for cc in reversed(range(cps)):
            rows = pl.ds(cc * CHUNK, CHUNK)
            dsp = chunk_bwd(rows, cc, dsp, row, col, tril, stril, rsum, q_ref, k_ref, v_ref, g_ref, b_ref, sall_ref,
                            tall_ref, do_ref, dq_ref, dk_ref, dv_ref, dg_ref, db_ref)
        ds_s[...] = dsp

    def chunk_bwd(rows, cc, dsp, row, col, tril, stril, rsum, q_ref, k_ref, v_ref, g_ref, b_ref, sall_ref, tall_ref,
                  do_ref, dq_ref, dk_ref, dv_ref, dg_ref, db_ref):
        qh, kh, vh, gcc, bbh = q_ref[:, rows, :], k_ref[:, rows, :], v_ref[:, rows, :], g_ref[:, rows, :], \
            b_ref[:, rows, :]
        sh, th, doh = sall_ref[:, cc], tall_ref[:, rows, :], do_ref[:, rows, :]
        dm, kb, lm = _gdn_chunk_common(kh, gcc, bbh, row, col, _dot3)
        eg = jnp.exp(gcc)
        glr = gcc[:, CHUNK - 1:CHUNK, :]
        glv = jnp.exp(glr)
        egl = jnp.exp(glr - gcc)
        rw, ru = kb * eg, vh * bbh
        w, u = _dot3(th, rw, NN3), _dot3(th, ru, NN3)
        at = jnp.where(tril, _dot3(qh, kh, NT3) * dm, 0.0)
        qd, kd = qh * eg, kh * egl
        vn = u - _dot3(w, sh, NN3)
        dgl = jnp.sum(rsum(dsp * sh), axis=1, keepdims=True)
        dkd = _dot3(vn, dsp, NT3)
        dvn = _dot3(kd, dsp, NN3)
        dqd = _dot3(doh, sh, NT3)
        dat = jnp.where(tril, _dot3(doh, vn, NT3), 0.0)
        dvn = dvn + _dot3(at, doh, TN3)
        dw = -_dot3(dvn, sh, NT3)
        ds_before = dsp * glv + _dot3(qd, doh, TN3) - _dot3(w, dvn, TN3)
        dpa = dat * dm
        dq_ref[:, rows, :] = _dot1(dpa, kh, NN3) + dqd * eg
        dk = _dot1(dpa, qh, TN3) + dkd * egl
        t6 = rsum(dkd * kd)
        dgam = rsum(dqd * qd) - t6
        dgam_last = jnp.sum(t6, axis=1, keepdims=True) + dgl * glv
        drw = _dot3(th, dw, TN3)
        dru = _dot3(th, dvn, TN3)
        dl = -jnp.where(stril, _dot3(drw, w, NT3) + _dot3(dru, u, NT3), 0.0)
        dgam = dgam + rsum(drw * rw)
        dv_ref[:, rows, :] = dru * bbh
        dp2 = dl * dm
        dkb = drw * eg + _dot1(dp2, kh, NN3)
        dk_ref[:, rows, :] = dk + _dot1(dp2, kb, TN3) + dkb * bbh
        db_ref[:, rows, :] = rsum(dru * vh) + rsum(dkb * kh) + jnp.zeros((nh, CHUNK, dh), F32)
        e = dat * at + dl * lm
        dgam_b = dgam + rsum(e) - _dot01(e, jnp.ones((nh, CHUNK, CHUNK), F32), TN3)
        dg_ref[:, rows, :] = dgam_b + jnp.where(row == CHUNK - 1, dgam_last, 0.0)
        return ds_before

    cps = GDN_STEP_CHUNKS
    steps = nchunk // cps
    rev = lambda n: (0, steps - 1 - n, 0)
    blk = pl.BlockSpec((nh, cps * CHUNK, dh), rev)
    sblk = pl.BlockSpec((nh, cps, dh, dh), lambda n: (0, steps - 1 - n, 0, 0))
    out = jax.ShapeDtypeStruct((nh, t, dh), F32)
    return pl.pallas_call(
        body, name="gdn_bwd", grid=(steps,), in_specs=[blk] * 5 + [sblk, blk, blk], out_specs=[blk] * 5,
        out_shape=[out] * 5, scratch_shapes=[pltpu.VMEM((nh, dh, dh), F32)],
        compiler_params=_params(("arbitrary",)))(q, k, v, gb, bb, sall, tall, do)


def _group_ones():
    r = lax.broadcasted_iota(jnp.int32, (GDN_W, GDN_W), 0) // GDN_DH
    c = lax.broadcasted_iota(jnp.int32, (GDN_W, GDN_W), 1) // GDN_DH
    return (r == c).astype(F32)


def _conv_taps(x, xprev, w, has_prev):
    row = lax.broadcasted_iota(jnp.int32, x.shape, 0)
    out = x * w[GDN_CONV - 1:GDN_CONV, :]
    for s in range(1, GDN_CONV):
        sh = jnp.where(row >= s, _roll(x, s, 0), _roll(xprev, s, 0) * has_prev)
        out = out + sh * w[GDN_CONV - 1 - s:GDN_CONV - s, :]
    return out


def _head_cols(x, h):
    return x[:, h * GDN_DH:(h + 1) * GDN_DH]


def _heads_spec(tb):
    return pl.BlockSpec((N_HEADS, tb, GDN_DH), lambda i: (0, i, 0))


def _mixer_fwd(x, positions, w, tb, carried=None, carried_gdn=None):
    t, d = x.shape
    tables = _rope_tables(positions)

    def pre(xb, g):
        return (xb * _rms_stats(xb) * g,)

    (hn,) = _rowwise("mix_pre", pre, [x], [w["mix_pre_g"]], [(d, BF16)], [], tb)
    proj = _mm("mix_in", hn, w["w_in_pad_t"], "nt", F32)

    def mla_pre(p0, gq, gkv):
        cq, ckv = p0[:, :MLA_Q_RANK], p0[:, MLA_Q_RANK:MLA_Q_RANK + MLA_KV_RANK]
        return cq * _rms_stats(cq) * gq, ckv * _rms_stats(ckv) * gkv

    nq, nkv = _rowwise("mla_pre", mla_pre, [(proj, 512, PIN_MLA // 512, 0)],
                       [w["mla_q_norm_g"], w["mla_kv_norm_g"]], [(MLA_Q_RANK, BF16), (MLA_KV_RANK, BF16)], [], tb)
    qraw = _mm("mla_uq", nq, w["w_uq_pad"], "nn", F32)
    kv = _mm("mla_ukv", nkv, w["w_kv_pad"], "nn", F32)

    def rope_f(qr, kn, vv, kpe, c, s1, s2):
        qo = _heads_apply(qr, lambda xh: _rope(xh, c, s1, s2)) * _attn_scale()
        kp = _rope(kpe, c, s1, s2)
        return qo, kn + jnp.tile(kp, (1, N_HEADS)), vv

    q, k, v = _rowwise("mla_rope", rope_f,
                       [qraw, (kv, MLA_PAD, 0, 0), (kv, MLA_PAD, 1, 0), (proj, HEAD_LANES, PIN_KPE // HEAD_LANES, 0),
                        tables[0], tables[1], tables[2]], [],
                       [(MLA_PAD, BF16)] * 3, [], tb // 2)
    tq = min(1024, t)
    o, lse, *carried_out = _attn_fwd(q, k, v, tq, carried)

    def mla_post(ob, g):
        return (ob * _rms_stats(ob, N_HEADS * MLA_V) * g,)

    (cat,) = _rowwise("mla_post", mla_post, [o], [w["mla_out_g_pad"]], [(MLA_PAD, BF16)], [], tb, wide=(CAT_W, 0))

    gones = _group_ones()
    steps = t // tb

    def gdn_pre(xq, xk, xv, pq, pk, pv, cw, go, has_prev):
        outs = []
        for j, (xc, xp) in enumerate(((xq, pq), (xk, pk), (xv, pv))):
            c = _conv_taps(xc, xp, cw[:, j * GDN_W:(j + 1) * GDN_W], has_prev)
            a = c * _sigmoid(c)
            if j < 2:
                rn = lax.rsqrt(_dot01(a * a, go) + EPS)
                a = a * rn
                if j == 0:
                    a = a * (GDN_DH ** -0.5)
            outs.append(a)
        return tuple(outs)

    qh, kh, vh = _gdn_pre_call("gdn_pre", gdn_pre, proj, w["conv_w"], gones, tb, steps)
    heads_shape = jax.ShapeDtypeStruct((N_HEADS, t, GDN_DH), F32)
    lanes_shape = jax.ShapeDtypeStruct((t, HEAD_LANES), F32)
    lanes_spec = pl.BlockSpec((tb, HEAD_LANES), lambda i: (i, 0))
    vec_spec = lambda n: pl.BlockSpec((1, n), lambda i: (0, 0))

    def gate_f(ab_ref, al_ref, dt_ref, g_ref, b_ref, gh_ref, bh_ref):
        g, b = _gb_fwd(ab_ref[...], al_ref[...], dt_ref[...])
        g_ref[...] = g
        b_ref[...] = b
        gc = _dot01(_chunk_sum_matrix(tb, False), g, ones="lhs")
        for h in range(N_HEADS):
            gh_ref[h] = jnp.broadcast_to(gc[:, h:h + 1], (tb, GDN_DH))
            bh_ref[h] = jnp.broadcast_to(b[:, N_HEADS + h:N_HEADS + h + 1], (tb, GDN_DH))

    g128, b128, gbh, bbh = pl.pallas_call(
        gate_f, name="gdn_gate_f", grid=(steps,),
        in_specs=[pl.BlockSpec((tb, HEAD_LANES), lambda i: (i, PIN_AB // HEAD_LANES)), vec_spec(HEAD_LANES),
                  vec_spec(HEAD_LANES)],
        out_specs=[lanes_spec, lanes_spec, _heads_spec(tb), _heads_spec(tb)],
        out_shape=[lanes_shape, lanes_shape, heads_shape, heads_shape],
        compiler_params=_params(("arbitrary",)))(proj, w["a_log_pad"], w["dt_bias_pad"])
    oh, sall, tall, *carried_out_gdn = _gdn_fwd(qh, kh, vh, gbh, bbh, carried_gdn)

    def gdn_post(o_ref, gt_ref, g_ref, cat_in, cat_ref):
        gt, g = gt_ref[...], g_ref[...]
        outs = []
        for h in range(N_HEADS):
            ob, gth = o_ref[h], _head_cols(gt, h)
            outs.append(ob * _rms_stats(ob) * g * (gth * _sigmoid(gth)))
        cat_ref[...] = jnp.concatenate(outs, axis=1).astype(cat_ref.dtype)

    gate_spec = pl.BlockSpec((tb, GDN_W), lambda i: (i, PIN_GATE // GDN_W))
    cat = pl.pallas_call(
        gdn_post, name="gdn_post", grid=(steps,),
        in_specs=[_heads_spec(tb), gate_spec, vec_spec(GDN_DH), ANY_SPEC],
        out_specs=pl.BlockSpec((tb, GDN_W), lambda i: (i, MLA_PAD // GDN_W)),
        out_shape=jax.ShapeDtypeStruct((t, CAT_W), BF16), input_output_aliases={3: 0},
        compiler_params=_params(("arbitrary",)))(oh, proj, w["gdn_norm_g"], cat)
    mixed = _mm("mix_out", cat, w["w_out_pad"], "nn", F32)

    def post(xb, hb, g):
        return (xb + hb * _rms_stats(hb) * g,)

    (y,) = _rowwise("mix_post", post, [x, mixed], [w["mix_post_g"]], [(d, F32)], [], tb)
    saved = dict(x=x, hn=hn, proj=proj, nq=nq, nkv=nkv, q=q, k=k, v=v, o=o, lse=lse, qh=qh, kh=kh, vh=vh,
                 gbh=gbh, bbh=bbh, oh=oh, sall=sall, tall=tall, cat=cat, mixed=mixed,
                 tables=tables, g128=g128, b128=b128)
    return y, saved, list(carried_out) + list(carried_out_gdn)


def _qkv_specs(tb):
    base = PIN_QKV // GDN_W
    cur = [pl.BlockSpec((tb, GDN_W), lambda i, j=j: (i, base + j)) for j in range(3)]
    prev = [pl.BlockSpec((tb, GDN_W), lambda i, j=j: (jnp.maximum(i - 1, 0), base + j)) for j in range(3)]
    return cur + prev


def _gdn_pre_call(name, fn, proj, conv_w, gones, tb, steps):
    t = proj.shape[0]

    def body(xq, xk, xv, pq, pk, pv, cw, go, oq, ok, ov):
        has_prev = jnp.where(pl.program_id(0) == 0, 0.0, 1.0)
        outs = fn(xq[...], xk[...], xv[...], pq[...], pk[...], pv[...], cw[...], go[...], has_prev)
        for r, val in zip((oq, ok, ov), outs):
            for h in range(N_HEADS):
                r[h] = _head_cols(val, h)

    return pl.pallas_call(
        body, name=name, grid=(steps,),
        in_specs=_qkv_specs(tb) + [pl.BlockSpec(conv_w.shape, lambda i: (0, 0)),
                                   pl.BlockSpec(gones.shape, lambda i: (0, 0))],
        out_specs=[_heads_spec(tb)] * 3,
        out_shape=[jax.ShapeDtypeStruct((N_HEADS, t, GDN_DH), F32)] * 3,
        compiler_params=_params(("arbitrary",)))(proj, proj, proj, proj, proj, proj, conv_w, gones)


def _softplus(x):
    return jnp.maximum(x, 0.0) + jnp.log1p(jnp.exp(-jnp.abs(x)))


def _gb_fwd(ab, a_log, dt_bias):
    g = -jnp.exp(a_log) * _softplus(ab + dt_bias)
    return g, _sigmoid(ab)


def _rope_tables(positions):
    half = MLA_ROPE // 2
    freqs = ROPE_THETA ** (-jnp.arange(half, dtype=F32) / half)
    ang = positions.reshape(-1).astype(F32)[:, None] * freqs
    cos, sin = jnp.cos(ang), jnp.sin(ang)
    t = ang.shape[0]
    one = jnp.ones((t, MLA_NOPE), F32)
    z16, z32, z64 = jnp.zeros((t, half), F32), jnp.zeros((t, MLA_ROPE), F32), jnp.zeros((t, MLA_NOPE), F32)
    c = jnp.concatenate([one, cos, cos, jnp.ones((t, MLA_ROPE), F32)], axis=1)
    s1 = jnp.concatenate([z64, -sin, z16, z32], axis=1)
    s2 = jnp.concatenate([z64, z16, sin, z32], axis=1)
    return c, s1, s2


def _mixer_bwd(dy, sv, w, tb, carried=None):
    x, proj = sv["x"], sv["proj"]
    t, d = x.shape
    c, s1, s2 = sv["tables"]
    grads = {}

    def post_b(hb, dyb, g):
        return _rms_bwd(hb, _rms_stats(hb), g, dyb)

    dmixed, grads["mix_post_g"] = _rowwise("mix_post_b", post_b, [sv["mixed"], dy], [w["mix_post_g"]],
                                           [(d, BF16)], [(1, d)], tb)
    dcat = _mm("mix_out_bx", dmixed, w["w_out_pad"], "nt", F32)
    grads["w_out_pad"] = _mm("mix_out_bw", sv["cat"], dmixed, "tn", F32)
    steps = t // tb
    vec_spec = lambda n: pl.BlockSpec((1, n), lambda i: (0, 0))

    def gdn_post_b(o_ref, gt_ref, do_ref, g_ref, dproj_ref, doh_ref, dg_ref):
        @pl.when(pl.program_id(0) == 0)
        def _():
            dg_ref[...] = jnp.zeros(dg_ref.shape, F32)

        gt, dob, g = gt_ref[...], do_ref[...], g_ref[...]
        dgates = []
        for h in range(N_HEADS):
            ob, gth, dobh = o_ref[h], _head_cols(gt, h), _head_cols(dob, h)
            sg = _sigmoid(gth)
            r = _rms_stats(ob)
            dxo, dg = _rms_bwd(ob, r, g, dobh * (gth * sg))
            doh_ref[h] = dxo
            dg_ref[...] += dg
            dgates.append(dobh * (ob * r * g) * (sg * (1.0 + gth * (1.0 - sg))))
        dproj_ref[...] = jnp.concatenate(dgates, axis=1).astype(dproj_ref.dtype)

    dproj, doh, grads["gdn_norm_g"] = pl.pallas_call(
        gdn_post_b, name="gdn_post_b", grid=(steps,),
        in_specs=[_heads_spec(tb), pl.BlockSpec((tb, GDN_W), lambda i: (i, PIN_GATE // GDN_W)),
                  pl.BlockSpec((tb, GDN_W), lambda i: (i, MLA_PAD // GDN_W)), vec_spec(GDN_DH)],
        out_specs=[pl.BlockSpec((tb, GDN_W), lambda i: (i, PIN_GATE // GDN_W)), _heads_spec(tb), vec_spec(GDN_DH)],
        out_shape=[jax.ShapeDtypeStruct((t, PIN_W), BF16), jax.ShapeDtypeStruct((N_HEADS, t, GDN_DH), F32),
                   jax.ShapeDtypeStruct((1, GDN_DH), F32)],
        compiler_params=_params(("arbitrary",)))(sv["oh"], proj, dcat, w["gdn_norm_g"])

    def mla_post_b(ob, dmo, g):
        do, dg = _rms_bwd(ob, _rms_stats(ob, N_HEADS * MLA_V), g, dmo, N_HEADS * MLA_V)
        prod = do * ob
        delta = _heads_apply(prod, lambda ph: jnp.sum(ph, axis=1, keepdims=True) + jnp.zeros_like(ph))
        return do, delta, dg

    do, delta, grads["mla_out_g_pad"] = _rowwise(
        "mla_post_b", mla_post_b, [sv["o"], (dcat, MLA_PAD, 0, 0)], [w["mla_out_g_pad"]],
        [(MLA_PAD, BF16), (MLA_PAD, F32)], [(1, MLA_PAD)], tb // 2)
    tq = min(1024, t)
    dq, dk, dv, *carried_out = _attn_bwd(sv["q"], sv["k"], sv["v"], do, sv["lse"], delta, tq, carried)
    n_groups = N_HEADS // BWD_HEADS
    tr = tb // 2
    dq_groups = [(dq, BWD_HEADS * HEAD_LANES, 0, grp * (t // tr)) for grp in range(n_groups)]

    def rope_b(*blocks):
        dqb = jnp.concatenate(blocks[:n_groups], axis=1)
        dkb, dvb, cc, a1, a2 = blocks[n_groups:]
        dqr = _heads_apply(dqb * _attn_scale(), lambda xh: _rope(xh, cc, -a1, -a2))
        ksum = dkb[:, :HEAD_LANES]
        for h in range(1, N_HEADS):
            ksum = ksum + dkb[:, h * HEAD_LANES:(h + 1) * HEAD_LANES]
        lane = lax.broadcasted_iota(jnp.int32, ksum.shape, 1)
        keep = (lane >= MLA_NOPE) & (lane < MLA_NOPE + MLA_ROPE)
        dkpe = jnp.where(keep, _rope(ksum, cc, -a1, -a2), 0.0)
        return dqr, jnp.concatenate([dkb, dvb], axis=1), dkpe

    dqraw, dkv, dkpe = _rowwise("mla_rope_b", rope_b, dq_groups + [dk, dv, c, s1, s2], [],
                                [(MLA_PAD, BF16, t), (2 * MLA_PAD, BF16), (HEAD_LANES, F32)], [], tr)
    dnq = _mm("mla_uq_bx", dqraw, w["w_uq_pad"], "nt", F32)
    grads["w_uq_pad"] = _mm("mla_uq_bw", sv["nq"], dqraw, "tn", F32)
    dnkv = _mm("mla_ukv_bx", dkv, w["w_kv_pad"], "nt", F32)
    grads["w_kv_pad"] = _mm("mla_ukv_bw", sv["nkv"], dkv, "tn", F32)

    def mla_pre_b(p0, dnqb, dnkvb, dkpeb, gq, gkv):
        cq, ckv = p0[:, :MLA_Q_RANK], p0[:, MLA_Q_RANK:MLA_Q_RANK + MLA_KV_RANK]
        dcq, dgq = _rms_bwd(cq, _rms_stats(cq), gq, dnqb)
        dckv, dgkv = _rms_bwd(ckv, _rms_stats(ckv), gkv, dnkvb)
        return jnp.concatenate([dcq, dckv, dkpeb], axis=1), dgq, dgkv

    dproj, grads["mla_q_norm_g"], grads["mla_kv_norm_g"] = _rowwise(
        "mla_pre_b", mla_pre_b, [(proj, 512, PIN_MLA // 512, 0), dnq, dnkv, dkpe],
        [w["mla_q_norm_g"], w["mla_kv_norm_g"]], [(512, BF16)], [(1, MLA_Q_RANK), (1, MLA_KV_RANK)], tb,
        wide=(PIN_W, PIN_MLA // 512), carry=dproj)

    dqh, dkh, dvh, dgh, dbh = _gdn_bwd(sv["qh"], sv["kh"], sv["vh"], sv["gbh"], sv["bbh"], sv["sall"], sv["tall"], doh)
    gones = _group_ones()

    def gdn_pre_b(xq, xk, xv, pq, pk, pv, dq_, dk_, dv_, cw, go, has_prev):
        outs = []
        for j, (xc, xp, dd) in enumerate(((xq, pq, dq_), (xk, pk, dk_), (xv, pv, dv_))):
            cc = _conv_taps(xc, xp, cw[:, j * GDN_W:(j + 1) * GDN_W], has_prev)
            sg = _sigmoid(cc)
            a = cc * sg
            if j < 2:
                rn = lax.rsqrt(_dot01(a * a, go) + EPS)
                if j == 0:
                    dd = dd * (GDN_DH ** -0.5)
                da = rn * dd - a * (rn * rn * rn) * _dot01(dd * a, go)
            else:
                da = dd
            outs.append(da * (sg * (1.0 + cc * (1.0 - sg))))
        return tuple(outs)

    dcq, dck, dcv = _gdn_pre_b_call("gdn_pre_b", gdn_pre_b, proj, (dqh, dkh, dvh), w["conv_w"], gones, tb, steps)
    dproj, grads["conv_w"] = _conv_bwd_call("gdn_conv_b", proj, (dcq, dck, dcv), w["conv_w"], dproj, tb, steps)

    def gate_b(ab_ref, g_ref, b_ref, dgh_ref, dbh_ref, al_ref, dt_ref, carry_ref, dab_ref, dal_ref, ddt_ref):
        @pl.when(pl.program_id(0) == 0)
        def _():
            dal_ref[...] = jnp.zeros(dal_ref.shape, F32)
            ddt_ref[...] = jnp.zeros(ddt_ref.shape, F32)

        ab, g128, b128 = ab_ref[...], g_ref[...], b_ref[...]
        lane = lax.broadcasted_iota(jnp.int32, ab.shape, 1)
        dg_ = jnp.zeros(ab.shape, F32)
        db_ = jnp.zeros(ab.shape, F32)
        for h in range(N_HEADS):
            dg_ = dg_ + jnp.where(lane == h, jnp.broadcast_to(dgh_ref[h][:, 0:1], ab.shape), 0.0)
            db_ = db_ + jnp.where(lane == N_HEADS + h, jnp.broadcast_to(dbh_ref[h][:, 0:1], ab.shape), 0.0)
        dg_ = _dot01(_chunk_sum_matrix(tb, True), dg_, ones="lhs")
        slope = -jnp.exp(al_ref[...]) * _sigmoid(ab + dt_ref[...])
        dab_ref[...] = (dg_ * slope + db_ * b128 * (1.0 - b128)).astype(dab_ref.dtype)
        dal_ref[...] += jnp.sum(dg_ * g128, axis=0, keepdims=True)
        ddt_ref[...] += jnp.sum(dg_ * slope, axis=0, keepdims=True)

    lanes_spec = pl.BlockSpec((tb, HEAD_LANES), lambda i: (i, 0))
    ab_spec = pl.BlockSpec((tb, HEAD_LANES), lambda i: (i, PIN_AB // HEAD_LANES))
    dproj, grads["a_log_pad"], grads["dt_bias_pad"] = pl.pallas_call(
        gate_b, name="gdn_gate_b", grid=(steps,),
        in_specs=[ab_spec, lanes_spec, lanes_spec, _heads_spec(tb), _heads_spec(tb), vec_spec(HEAD_LANES),
                  vec_spec(HEAD_LANES), ANY_SPEC],
        out_specs=[ab_spec, vec_spec(HEAD_LANES), vec_spec(HEAD_LANES)],
        out_shape=[jax.ShapeDtypeStruct((t, PIN_W), BF16), jax.ShapeDtypeStruct((1, HEAD_LANES), F32),
                   jax.ShapeDtypeStruct((1, HEAD_LANES), F32)],
        input_output_aliases={7: 0},
        compiler_params=_params(("arbitrary",)))(proj, sv["g128"], sv["b128"], dgh, dbh, w["a_log_pad"],
                                                 w["dt_bias_pad"], dproj)
    dhn = _mm("mix_in_bx", dproj, w["w_in_pad_t"], "nn", F32)
    grads["w_in_pad_t"] = _mm("mix_in_bw", dproj, sv["hn"], "tn", F32)

    def pre_b(xb, dnb, dyb, g):
        dx, dg = _rms_bwd(xb, _rms_stats(xb), g, dnb)
        return dyb + dx, dg

    dx, grads["mix_pre_g"] = _rowwise("mix_pre_b", pre_b, [x, dhn, dy], [w["mix_pre_g"]], [(d, F32)], [(1, d)], tb)
    return dx, grads, carried_out


def _gdn_pre_b_call(name, fn, proj, dd, conv_w, gones, tb, steps):
    t = proj.shape[0]

    def body(xq, xk, xv, pq, pk, pv, d0, d1, d2, cw, go, oq, ok, ov):
        has_prev = jnp.where(pl.program_id(0) == 0, 0.0, 1.0)
        dd_rows = [jnp.concatenate([dr[h] for h in range(N_HEADS)], axis=1) for dr in (d0, d1, d2)]
        outs = fn(xq[...], xk[...], xv[...], pq[...], pk[...], pv[...], *dd_rows, cw[...], go[...], has_prev)
        for r, val in zip((oq, ok, ov), outs):
            r[...] = val

    return pl.pallas_call(
        body, name=name, grid=(steps,),
        in_specs=_qkv_specs(tb) + [_heads_spec(tb)] * 3 + [pl.BlockSpec(conv_w.shape, lambda i: (0, 0)),
                                                          pl.BlockSpec(gones.shape, lambda i: (0, 0))],
        out_specs=[pl.BlockSpec((tb, GDN_W), lambda i: (i, 0))] * 3,
        out_shape=[jax.ShapeDtypeStruct((t, GDN_W), F32)] * 3,
        compiler_params=_params(("arbitrary",)))(proj, proj, proj, proj, proj, proj, *dd, conv_w, gones)


def _conv_bwd_call(name, proj, dc, conv_w, dproj, tb, steps):
    t = proj.shape[0]
    dcur = [pl.BlockSpec((tb, GDN_W), lambda i: (i, 0))] * 3
    dnext = [pl.BlockSpec((tb, GDN_W), lambda i: (jnp.minimum(i + 1, steps - 1), 0))] * 3

    def body(xq, xk, xv, pq, pk, pv, d0, d1, d2, n0, n1, n2, cw, carry_ref, dx_ref, dw_ref):
        i = pl.program_id(0)
        has_prev = jnp.where(i == 0, 0.0, 1.0)
        has_next = jnp.where(i == steps - 1, 0.0, 1.0)

        @pl.when(i == 0)
        def _():
            dw_ref[...] = jnp.zeros(dw_ref.shape, F32)

        wv = cw[...]
        dws, dxs = [], []
        for j, (xr, pr, dr, nr) in enumerate(((xq, pq, d0, n0), (xk, pk, d1, n1), (xv, pv, d2, n2))):
            x, xp, dcv, dnx = xr[...], pr[...], dr[...], nr[...]
            wj = wv[:, j * GDN_W:(j + 1) * GDN_W]
            row = lax.broadcasted_iota(jnp.int32, x.shape, 0)
            dx = dcv * wj[GDN_CONV - 1:GDN_CONV, :]
            rows_w = [jnp.sum(dcv * x, axis=0, keepdims=True)]
            for s in range(1, GDN_CONV):
                up = jnp.where(row < tb - s, _roll(dcv, tb - s, 0), _roll(dnx, tb - s, 0) * has_next)
                dx = dx + up * wj[GDN_CONV - 1 - s:GDN_CONV - s, :]
                sh = jnp.where(row >= s, _roll(x, s, 0), _roll(xp, s, 0) * has_prev)
                rows_w.append(jnp.sum(dcv * sh, axis=0, keepdims=True))
            dxs.append(dx)
            dws.append(jnp.concatenate(rows_w[::-1], axis=0))
        dx_ref[...] = jnp.concatenate(dxs, axis=1).astype(dx_ref.dtype)
        dw_ref[...] += jnp.concatenate(dws, axis=1)

    return pl.pallas_call(
        body, name=name, grid=(steps,),
        in_specs=_qkv_specs(tb) + dcur + dnext + [pl.BlockSpec(conv_w.shape, lambda i: (0, 0)), ANY_SPEC],
        out_specs=[pl.BlockSpec((tb, 3 * GDN_W), lambda i: (i, PIN_QKV // (3 * GDN_W))),
                   pl.BlockSpec(conv_w.shape, lambda i: (0, 0))],
        out_shape=[jax.ShapeDtypeStruct((t, PIN_W), BF16), jax.ShapeDtypeStruct(conv_w.shape, F32)],
        input_output_aliases={13: 0},
        compiler_params=_params(("arbitrary",)))(proj, proj, proj, proj, proj, proj, *dc, *dc, conv_w, dproj)


def _pad_heads_cols(wm, per_head):
    r = wm.shape[0]
    return jnp.pad(wm.reshape(r, N_HEADS, per_head), ((0, 0), (0, 0), (0, HEAD_LANES - per_head))).reshape(r, MLA_PAD)


def _unpad_heads_cols(wm, per_head):
    r = wm.shape[0]
    return wm.reshape(r, N_HEADS, HEAD_LANES)[:, :, :per_head].reshape(r, N_HEADS * per_head)


W_IN_COLS = MLA_Q_RANK + MLA_KV_RANK + MLA_ROPE + 3 * GDN_W + 2 * N_HEADS + GDN_W
W_IN_SHARD = W_IN_COLS // N_SHARD
W_IN_SHARD_PAD = 640
_Q0 = MLA_Q_RANK + MLA_KV_RANK
_Q1 = _Q0 + MLA_ROPE
_Q2 = _Q1 + 3 * GDN_W
_Q3 = _Q2 + 2 * N_HEADS
W_IN_SEGMENTS = [(0, _Q0, PIN_MLA), (_Q0, _Q1, PIN_KPE + MLA_NOPE), (_Q1, _Q2, PIN_QKV), (_Q2, _Q3, PIN_AB),
                 (_Q3, W_IN_COLS, PIN_GATE)]


def _win_pad_t(slabs):
    d = slabs.shape[2]
    pieces, at = [], 0
    for c0, c1, r0 in sorted(W_IN_SEGMENTS, key=lambda s: s[2]):
        if r0 > at:
            pieces.append(jnp.zeros((r0 - at, d), slabs.dtype))
        for q in range(N_SHARD):
            lo, hi = max(c0, q * W_IN_SHARD), min(c1, (q + 1) * W_IN_SHARD)
            if lo < hi:
                pieces.append(slabs[q, lo - q * W_IN_SHARD:hi - q * W_IN_SHARD])
        at = r0 + c1 - c0
    pieces.append(jnp.zeros((PIN_W - at, d), slabs.dtype))
    return jnp.concatenate(pieces, axis=0)


def _win_cols_t(wp_t, c_lo, c_hi):
    pieces = []
    for c0, c1, r0 in W_IN_SEGMENTS:
        lo, hi = max(c0, c_lo), min(c1, c_hi)
        if lo < hi:
            pieces.append(wp_t[r0 + lo - c0:r0 + hi - c0])
    return jnp.concatenate(pieces, axis=0)


def _wkv_to_pad(wkv):
    r = wkv.shape[0]
    w3 = wkv.reshape(r, N_HEADS, MLA_NOPE + MLA_V)
    kpart = jnp.pad(w3[:, :, :MLA_NOPE], ((0, 0), (0, 0), (0, HEAD_LANES - MLA_NOPE))).reshape(r, MLA_PAD)
    vpart = jnp.pad(w3[:, :, MLA_NOPE:], ((0, 0), (0, 0), (0, HEAD_LANES - MLA_V))).reshape(r, MLA_PAD)
    return jnp.concatenate([kpart, vpart], axis=1)


def _wkv_from_pad(wp):
    r = wp.shape[0]
    kpart = wp[:, :MLA_PAD].reshape(r, N_HEADS, HEAD_LANES)[:, :, :MLA_NOPE]
    vpart = wp[:, MLA_PAD:].reshape(r, N_HEADS, HEAD_LANES)[:, :, :MLA_V]
    return jnp.concatenate([kpart, vpart], axis=2).reshape(r, N_HEADS * (MLA_NOPE + MLA_V))


def _wout_to_pad(wo):
    n = wo.shape[1]
    mla = jnp.pad(wo[:N_HEADS * MLA_V].reshape(N_HEADS, MLA_V, n), ((0, 0), (0, HEAD_LANES - MLA_V), (0, 0)))
    return jnp.concatenate([mla.reshape(MLA_PAD, n), wo[N_HEADS * MLA_V:]], axis=0)


def _wout_from_pad(wp):
    n = wp.shape[1]
    mla = wp[:MLA_PAD].reshape(N_HEADS, HEAD_LANES, n)[:, :MLA_V].reshape(N_HEADS * MLA_V, n)
    return jnp.concatenate([mla, wp[MLA_PAD:]], axis=0)


def _pad_lanes(v, n):
    return jnp.pad(v, ((0, 0), (0, n - v.shape[1])))


def _compute_weights(full):
    w = {}
    for n in FFN_BIG:
        if n in full:
            w[n] = full[n].astype(MM_DTYPE)
    w["w_in_pad_t"] = _win_pad_t(full["w_in"]).astype(MM_DTYPE)
    w["w_uq_pad"] = _pad_heads_cols(full["mla_w_uq"], MLA_NOPE + MLA_ROPE).astype(MM_DTYPE)
    w["w_kv_pad"] = _wkv_to_pad(full["mla_w_ukv"]).astype(MM_DTYPE)
    w["w_out_pad"] = _wout_to_pad(full["w_out"]).astype(MM_DTYPE)
    w["conv_w"] = full["gdn_conv_w"].astype(F32)
    for n in ("ffn1_pre_g", "ffn1_post_g", "mix_pre_g", "mla_q_norm_g", "mla_kv_norm_g", "gdn_norm_g", "mix_post_g",
              "ffn2_pre_g", "ffn2_post_g"):
        w[n] = full[n]
    w["mla_out_g_pad"] = _pad_heads_cols(full["mla_out_g"], MLA_V)
    w["a_log_pad"] = _pad_lanes(full["gdn_a_log"], HEAD_LANES)
    w["dt_bias_pad"] = _pad_lanes(full["gdn_dt_bias"], HEAD_LANES)
    return w


FFN2_BIG = FFN_BIG[3:]


def _local_step(x, positions, loss_target, full, late=None):
    t, d = x.shape
    tb = min(512, t)
    tm = min(1024, t)
    tk = min(2048, t)
    w = _compute_weights(full)
    ffn = lambda tag: (w[tag + "_pre_g"], w[tag + "_w_gate"], w[tag + "_w_up"], w[tag + "_w_down"], w[tag + "_post_g"])
    x1, sv1 = _ffn_fwd("ffn1", x, *ffn("ffn1"), tm)
    x2, svm, gathered = _mixer_fwd(x1, positions, w, tb, _carried_gather(late[0][:2]) if late else None,
                                   _carried_gather(late[0][2:]) if late else None)
    for n, gw in zip(FFN2_BIG, gathered):
        w[n] = gw
    x3, sv2 = _ffn_fwd("ffn2", x2, *ffn("ffn2"), tm)

    def loss_f(yb, tg):
        e = yb - tg
        return e * (1.0 / d), jnp.sum(e * e, axis=0, keepdims=True)

    dy, lsum = _rowwise("loss", loss_f, [x3, loss_target], [], [(d, F32)], [(1, d)], tb)
    g = {}
    dx2, g["ffn2_pre_g"], g["ffn2_w_gate"], g["ffn2_w_up"], g["ffn2_w_down"], g["ffn2_post_g"] = _ffn_bwd(
        "ffn2", dy, sv2, *ffn("ffn2"), tm, tk)[:6]

    def pair_sums(arrs, tag):
        got = _swap_halves(arrs, tag)
        return [_add_pair("add_pair%s_%d" % (tag, i), gi, gt, late[1]) for i, (gi, gt) in enumerate(zip(arrs, got))]

    def chip_sums(pairs, slabs, tag):
        return [_add_chips("add_chips%s_%d" % (tag, i), pr, sl, late[2]) for i, (pr, sl) in enumerate(zip(pairs, slabs))]

    if late:
        pairs2 = pair_sums([g[n] for n in FFN2_BIG], "_ffn2")
        dx1, gm, slabs2 = _mixer_bwd(dx2, svm, w, tb, _carried_scatter(pairs2))
        for n, hs in zip(FFN2_BIG, chip_sums(pairs2, slabs2, "_ffn2")):
            g[n] = hs
    else:
        dx1, gm, _ = _mixer_bwd(dx2, svm, w, tb)
    g["w_in"] = jnp.stack([jnp.pad(_win_cols_t(gm["w_in_pad_t"], q * W_IN_SHARD, (q + 1) * W_IN_SHARD),
                                   ((0, W_IN_SHARD_PAD - W_IN_SHARD), (0, 0))) for q in range(N_SHARD)])
    g["mla_w_uq"] = _unpad_heads_cols(gm["w_uq_pad"], MLA_NOPE + MLA_ROPE)
    g["mla_w_ukv"] = _wkv_from_pad(gm["w_kv_pad"])
    g["gdn_conv_w"] = gm["conv_w"]
    g["w_out"] = _wout_from_pad(gm["w_out_pad"])
    if late:
        quarters = [_pack([jnp.split(g[n], N_SHARD, axis=SHARD_AXIS[n])[q] for n in MIX_BIG], MM_DTYPE)
                    for q in range(N_SHARD)]
        pairs_m = pair_sums([g["w_in"].astype(MM_DTYPE), jnp.stack(quarters)], "_mix")
        pairs_d, pairs_gu = [], []

        def make_mid(dwd):
            pairs_d.extend(pair_sums([dwd], "_ffn1d"))
            return _carried_scatter(pairs_d)

        def make_up(dwg, dwu):
            pairs_gu.extend(pair_sums([dwg, dwu], "_ffn1"))
            return _carried_scatter(pairs_gu)

        dx0, g["ffn1_pre_g"], _, _, _, g["ffn1_post_g"], slabs_m, slabs_d, slabs_gu = _ffn_bwd(
            "ffn1", dx1, sv1, *ffn("ffn1"), tm, tk, _carried_scatter(pairs_m), make_mid, make_up)
        g["ffn1_w_gate"], g["ffn1_w_up"] = chip_sums(pairs_gu, slabs_gu, "_ffn1")
        g["ffn1_w_down"] = chip_sums(pairs_d, slabs_d, "_ffn1d")[0]
        g["w_in"], g["mix_pack"] = chip_sums(pairs_m, slabs_m, "_mix")
    else:
        dx0, g["ffn1_pre_g"], g["ffn1_w_gate"], g["ffn1_w_up"], g["ffn1_w_down"], g["ffn1_post_g"] = _ffn_bwd(
            "ffn1", dx1, sv1, *ffn("ffn1"), tm, tk)[:6]
    g["mix_pre_g"], g["mix_post_g"] = gm["mix_pre_g"], gm["mix_post_g"]
    g["mla_q_norm_g"], g["mla_kv_norm_g"] = gm["mla_q_norm_g"], gm["mla_kv_norm_g"]
    g["gdn_norm_g"] = gm["gdn_norm_g"]
    g["mla_out_g"] = _unpad_heads_cols(gm["mla_out_g_pad"], MLA_V)
    g["gdn_a_log"] = gm["a_log_pad"][:, :N_HEADS]
    g["gdn_dt_bias"] = gm["dt_bias_pad"][:, :N_HEADS]
    return lsum, dx0, g


HBM_SPEC = pl.BlockSpec(memory_space=pltpu.HBM)


def _place():
    return lax.axis_index("x"), lax.axis_index("y"), lax.axis_index("c")


def _exchange_call(name, body, ins, out_shapes, n_remote, n_local):
    return pl.pallas_call(
        body, name=name, in_specs=[HBM_SPEC] * len(ins), out_specs=[HBM_SPEC] * len(out_shapes), out_shape=out_shapes,
        scratch_shapes=[pltpu.SemaphoreType.DMA((n_remote,)), pltpu.SemaphoreType.DMA((n_remote,)),
                        pltpu.SemaphoreType.DMA((n_local,))])(*ins)


def _other_chips(x, y):
    return [(1 - x, y), (x, 1 - y), (1 - x, 1 - y)]


def _at_each_chip(fn):
    x, y, _ = _place()
    for cx in range(2):
        for cy in range(2):
            pl.when((x == cx) & (y == cy))(functools.partial(fn, cx, cy))


def _at_each_device(fn):
    x, y, c = _place()
    for cx in range(2):
        for cy in range(2):
            for cc in range(2):
                pl.when((x == cx) & (y == cy) & (c == cc))(functools.partial(fn, cx, cy, cc))


def _at_each_core(fn):
    c = lax.axis_index("c")
    for cc in range(2):
        pl.when(c == cc)(functools.partial(fn, cc))


def _gather_shards(ws):
    nw = len(ws)

    def body(*refs):
        w_refs, out_refs = refs[:nw], refs[nw:2 * nw]
        send_sems, recv_sems, local_sems = refs[2 * nw:]

        def run(x, y, c):
            chips = _other_chips(x, y)
            me, sibling = 2 * x + y, (x, y, 1 - c)

            def half(ref, which):
                hr = ref.shape[0] // 2
                return ref.at[pl.ds(which * hr, hr)]

            def over_ici(i, j, src, slab, to):
                return pltpu.make_async_remote_copy(
                    src_ref=half(src, c), dst_ref=half(out_refs[i].at[slab], c), send_sem=send_sems.at[7 * i + j],
                    recv_sem=recv_sems.at[7 * i + j], device_id=to, device_id_type=MESH)

            def over_d2d(i, j, slab, which):
                return pltpu.make_async_remote_copy(
                    src_ref=half(out_refs[i].at[slab], which), dst_ref=half(out_refs[i].at[slab], which),
                    send_sem=send_sems.at[7 * i + 3 + j], recv_sem=recv_sems.at[7 * i + 3 + j], device_id=sibling,
                    device_id_type=MESH)

            def own(i, w_ref):
                return pltpu.make_async_remote_copy(
                    src_ref=w_ref, dst_ref=out_refs[i].at[me], send_sem=send_sems.at[7 * i + 6],
                    recv_sem=recv_sems.at[7 * i + 6], device_id=sibling, device_id_type=MESH)

            sends, passed = [], []
            for i, w_ref in enumerate(w_refs):
                for j, (px, py) in enumerate(chips):
                    sends.append(over_ici(i, j, w_ref, me, (px, py, c)))
                    sends[-1].start()
            for i, w_ref in enumerate(w_refs):
                sends.append(own(i, w_ref))
                sends[-1].start()
            for i, w_ref in enumerate(w_refs):
                for j, (px, py) in enumerate(chips):
                    over_ici(i, j, w_ref, 2 * px + py, (px, py, c)).wait_recv()
                    passed.append(over_d2d(i, j, 2 * px + py, c))
                    passed[-1].start()
            for i, w_ref in enumerate(w_refs):
                own(i, w_ref).wait_recv()
                for j, (px, py) in enumerate(chips):
                    over_d2d(i, j, 2 * px + py, 1 - c).wait_recv()
            for cp in sends + passed:
                cp.wait_send()

        _at_each_device(run)

    outs = [jax.ShapeDtypeStruct((N_SHARD,) + w.shape, w.dtype) for w in ws]
    return _exchange_call("gather_weight_shards", body, ws, outs, 7 * nw, 1)


def _swap_halves(gs, tag=""):
    ng = len(gs)

    def body(*refs):
        g_refs, got_refs = refs[:ng], refs[ng:2 * ng]
        send_sems, recv_sems, _ = refs[2 * ng:]
        x, y, _ = _place()

        def run(c):
            sends = []
            for i, (g_ref, got_ref) in enumerate(zip(g_refs, got_refs)):
                hr = got_ref.shape[1]
                sends.append(pltpu.make_async_remote_copy(
                    src_ref=g_ref.at[:, pl.ds((1 - c) * hr, hr)], dst_ref=got_ref, send_sem=send_sems.at[i],
                    recv_sem=recv_sems.at[i], device_id=(x, y, 1 - c), device_id_type=MESH))
                sends[-1].start()
            for cp in sends:
                cp.wait()

        _at_each_core(run)

    halves = [jax.ShapeDtypeStruct((g.shape[0], g.shape[1] // 2, g.shape[2]), g.dtype) for g in gs]
    return _exchange_call("swap_grad_halves" + tag, body, gs, halves, ng, 1)


def _scatter_copies(p_refs, out_refs, send_sems, recv_sems, x, y):
    c = lax.axis_index("c")
    copies = []
    for i, (p_ref, out_ref) in enumerate(zip(p_refs, out_refs)):
        for j, (px, py) in enumerate(_other_chips(x, y)):
            copies.append(pltpu.make_async_remote_copy(
                src_ref=p_ref.at[2 * px + py], dst_ref=out_ref.at[j], send_sem=send_sems.at[3 * i + j],
                recv_sem=recv_sems.at[3 * i + j], device_id=(px, py, c), device_id_type=MESH))
    return copies


def _start_all(make, *refs):
    def run(x, y):
        for cp in make(*refs, x, y):
            cp.start()

    _at_each_chip(run)


def _wait_all(make, *refs):
    def run(x, y):
        copies = make(*refs, x, y)
        for cp in copies:
            cp.wait_recv()
        for cp in copies:
            cp.wait_send()

    _at_each_chip(run)


def _scatter_shapes(ps):
    return [jax.ShapeDtypeStruct((3,) + p.shape[1:], p.dtype) for p in ps]


def _carried_scatter(ps):
    return _Carried(ps, _scatter_shapes(ps), 3 * len(ps), functools.partial(_start_all, _scatter_copies),
                    functools.partial(_wait_all, _scatter_copies))


def _direct_gather_copies(w_refs, out_refs, send_sems, recv_sems, x, y, arriving):
    c = lax.axis_index("c")
    me = 2 * x + y
    peers = [((px, py, c), 2 * px + py) for px, py in _other_chips(x, y)] + [((x, y, 1 - c), me)]
    copies = []
    for i, (w_ref, out_ref) in enumerate(zip(w_refs, out_refs)):
        for j, (peer, slab) in enumerate(peers):
            copies.append(pltpu.make_async_remote_copy(
                src_ref=w_ref, dst_ref=out_ref.at[slab if arriving else me], send_sem=send_sems.at[4 * i + j],
                recv_sem=recv_sems.at[4 * i + j], device_id=peer, device_id_type=MESH))
    return copies


def _carried_gather(ws):
    def start(w_refs, out_refs, send_sems, recv_sems):
        def run(x, y):
            for cp in _direct_gather_copies(w_refs, out_refs, send_sems, recv_sems, x, y, False):
                cp.start()

        _at_each_chip(run)

    def finish(w_refs, out_refs, send_sems, recv_sems):
        def run(x, y):
            for cp in _direct_gather_copies(w_refs, out_refs, send_sems, recv_sems, x, y, True):
                cp.wait_recv()
            for cp in _direct_gather_copies(w_refs, out_refs, send_sems, recv_sems, x, y, False):
                cp.wait_send()

        _at_each_chip(run)

    outs = [jax.ShapeDtypeStruct((N_SHARD,) + w.shape, w.dtype) for w in ws]
    return _Carried(ws, outs, 4 * len(ws), start, finish)


def _share_halves(hs):
    n = len(hs)

    def body(*refs):
        h_refs, out_refs = refs[:n], refs[n:2 * n]
        send_sems, recv_sems, _ = refs[2 * n:]
        x, y, c = _place()
        sends = []
        for i, (h_ref, out_ref) in enumerate(zip(h_refs, out_refs)):
            sends.append(pltpu.make_async_remote_copy(
                src_ref=h_ref, dst_ref=out_ref, send_sem=send_sems.at[i], recv_sem=recv_sems.at[i],
                device_id=(x, y, 1 - c), device_id_type=MESH))
            sends[-1].start()
        for cp in sends:
            cp.wait()

    outs = [jax.ShapeDtypeStruct(h.shape, h.dtype) for h in hs]
    return _exchange_call("share_grad_halves", body, hs, outs, n, 1)


def _scalar_grid_call(name, body, scalars, grid, in_specs, out_specs, out_shape, args):
    grid_spec = pltpu.PrefetchScalarGridSpec(num_scalar_prefetch=len(scalars), grid=grid, in_specs=in_specs,
                                             out_specs=out_specs)
    return pl.pallas_call(body, name=name, grid_spec=grid_spec, out_shape=out_shape,
                          compiler_params=_params(("arbitrary",) * len(grid)))(*scalars, *args)


def _add_pair(name, g, got, core):
    ns_, hr, cols = got.shape
    th = _row_tile(hr, 512)
    nb = hr // th

    def body(core_ref, g_ref, got_ref, out_ref):
        out_ref[...] = (g_ref[...].astype(F32) + got_ref[...].astype(F32)).astype(out_ref.dtype)

    blk = pl.BlockSpec((1, th, cols), lambda q, j, core_ref: (q, j, 0))
    own = pl.BlockSpec((1, th, cols), lambda q, j, core_ref: (q, core_ref[0] * nb + j, 0))
    return _scalar_grid_call(name, body, [core], (ns_, nb), [own, blk], blk,
                             jax.ShapeDtypeStruct(got.shape, got.dtype), [g, got])


def _add_chips(name, pairs, slabs, chip):
    _, hr, cols = slabs.shape
    th = _row_tile(hr, 512)

    def body(chip_ref, own_ref, s0_ref, s1_ref, s2_ref, out_ref):
        total = own_ref[0].astype(F32) + s0_ref[0].astype(F32)
        out_ref[...] = (total + s1_ref[0].astype(F32)) + s2_ref[0].astype(F32)

    own = pl.BlockSpec((1, th, cols), lambda j, chip_ref: (chip_ref[0], j, 0))
    others = [pl.BlockSpec((1, th, cols), lambda j, chip_ref, k=k: (k, j, 0)) for k in range(3)]
    return _scalar_grid_call(name, body, [chip], (hr // th,), [own] + others,
                             pl.BlockSpec((th, cols), lambda j, chip_ref: (j, 0)),
                             jax.ShapeDtypeStruct((hr, cols), F32), [pairs, slabs, slabs, slabs])


def _join_halves(name, mine, other, core):
    hr, cols = mine.shape
    th = _row_tile(hr, 512)
    nb = hr // th

    def body(core_ref, mine_ref, other_ref, out_ref):
        is_mine = pl.program_id(0) == core_ref[0]

        @pl.when(is_mine)
        def _():
            out_ref[0] = mine_ref[...]

        @pl.when(jnp.logical_not(is_mine))
        def _():
            out_ref[0] = other_ref[...]

    blk = pl.BlockSpec((th, cols), lambda h, j, core_ref: (j, 0))
    return _scalar_grid_call(name, body, [core], (2, nb), [blk, blk],
                             pl.BlockSpec((1, th, cols), lambda h, j, core_ref: (0, h * nb + j, 0)),
                             jax.ShapeDtypeStruct((1, 2 * hr, cols), mine.dtype), [mine, other])


def _gather_small(sp):
    def body(s_ref, out_ref, send_sems, recv_sems, local_sem):
        x, y, c = _place()
        me = 4 * x + 2 * y + c
        peers = [(x ^ (m >> 2), y ^ ((m >> 1) & 1), c ^ (m & 1)) for m in range(1, 8)]
        mine = pltpu.make_async_copy(s_ref, out_ref.at[me], local_sem)
        mine.start()
        sends = [pltpu.make_async_remote_copy(src_ref=s_ref, dst_ref=out_ref.at[me], send_sem=send_sems.at[j],
                                              recv_sem=recv_sems.at[j], device_id=p, device_id_type=MESH)
                 for j, p in enumerate(peers)]
        for cp in sends:
            cp.start()
        for j, (px, py, pc) in enumerate(peers):
            pltpu.make_async_remote_copy(src_ref=s_ref, dst_ref=out_ref.at[4 * px + 2 * py + pc],
                                         send_sem=send_sems.at[j], recv_sem=recv_sems.at[j], device_id=(px, py, pc),
                                         device_id_type=MESH).wait_recv()
        for cp in sends:
            cp.wait_send()
        mine.wait()

    return pl.pallas_call(
        body, name="gather_small_grads", in_specs=[HBM_SPEC], out_specs=HBM_SPEC,
        out_shape=jax.ShapeDtypeStruct((8,) + sp.shape, sp.dtype),
        scratch_shapes=[pltpu.SemaphoreType.DMA((7,)), pltpu.SemaphoreType.DMA((7,)), pltpu.SemaphoreType.DMA])(sp)


def _pack_rows(total):
    rows = -(-total // LANES)
    return -(-rows // 32) * 32


def _pack(arrs, dtype):
    flat = jnp.concatenate([a.reshape(-1).astype(dtype) for a in arrs])
    rows = _pack_rows(flat.shape[0])
    return jnp.pad(flat, (0, rows * LANES - flat.shape[0])).reshape(rows, LANES)


def _unpack(buf, shapes):
    flat = buf.reshape(-1)
    out, off = {}, 0
    for n, shp in shapes:
        size = shp[0] * shp[1]
        out[n] = flat[off:off + size].reshape(shp)
        off += size
    return out


def _to_wire(name, w3):
    _, r, cols = w3.shape
    tb = _row_tile(r, 512)

    def body(w_ref, o_ref):
        o_ref[...] = w_ref[0].astype(o_ref.dtype)

    return pl.pallas_call(
        body, name=name, grid=(r // tb,), in_specs=[pl.BlockSpec((1, tb, cols), lambda i: (0, i, 0))],
        out_specs=pl.BlockSpec((tb, cols), lambda i: (i, 0)), out_shape=jax.ShapeDtypeStruct((r, cols), MM_DTYPE),
        compiler_params=_params(("arbitrary",)))(w3)


def _adamw(name, w3, g, m3, v3, tb):
    c1 = 1.0 - ADAM_B1 ** ADAM_STEP
    c2 = 1.0 - ADAM_B2 ** ADAM_STEP
    _, r, cols = w3.shape
    emit = g.ndim == 2
    blk3 = pl.BlockSpec((1, tb, cols), lambda i: (0, i, 0))
    g_spec = pl.BlockSpec((tb, cols), lambda i: (i, 0)) if emit else blk3

    def body(w_ref, g_ref, m_ref, v_ref, *out_refs):
        gb = g_ref[...] if emit else g_ref[0]
        m2 = ADAM_B1 * m_ref[0] + (1.0 - ADAM_B1) * gb
        v2 = ADAM_B2 * v_ref[0] + (1.0 - ADAM_B2) * (gb * gb)
        out_refs[-3][0] = -ADAM_LR * ((m2 / c1) / (jnp.sqrt(v2 / c2) + ADAM_EPS) + ADAM_WD * w_ref[0])
        out_refs[-2][0] = m2
        out_refs[-1][0] = v2
        if emit:
            out_refs[0][0] = gb

    n_out = 4 if emit else 3
    outs = pl.pallas_call(
        body, name=name, grid=(r // tb,), in_specs=[blk3, g_spec, blk3, blk3], out_specs=[blk3] * n_out,
        out_shape=[jax.ShapeDtypeStruct((1, r, cols), F32)] * n_out,
        compiler_params=_params(("arbitrary",)))(w3, g, m3, v3)
    return outs if emit else [g] + list(outs)


def _row_tile(rows, pref):
    if rows <= pref:
        return rows
    t = pref
    while t >= 8:
        if rows % t == 0 and t % 8 == 0:
            return t
        t -= 8
    return rows


def kernel(x, positions, ffn1_pre_g, ffn1_w_gate, ffn1_w_up, ffn1_w_down, ffn1_post_g, mix_pre_g, w_in, mla_q_norm_g, mla_w_uq, mla_kv_norm_g, mla_w_ukv, mla_out_g, gdn_conv_w, gdn_a_log, gdn_dt_bias, gdn_norm_g, w_out, mix_post_g, ffn2_pre_g, ffn2_w_gate, ffn2_w_up, ffn2_w_down, ffn2_post_g, loss_target, m_ffn1_pre_g, m_ffn1_w_gate, m_ffn1_w_up, m_ffn1_w_down, m_ffn1_post_g, m_mix_pre_g, m_w_in, m_mla_q_norm_g, m_mla_w_uq, m_mla_kv_norm_g, m_mla_w_ukv, m_mla_out_g, m_gdn_conv_w, m_gdn_a_log, m_gdn_dt_bias, m_gdn_norm_g, m_w_out, m_mix_post_g, m_ffn2_pre_g, m_ffn2_w_gate, m_ffn2_w_up, m_ffn2_w_down, m_ffn2_post_g, v_ffn1_pre_g, v_ffn1_w_gate, v_ffn1_w_up, v_ffn1_w_down, v_ffn1_post_g, v_mix_pre_g, v_w_in, v_mla_q_norm_g, v_mla_w_uq, v_mla_kv_norm_g, v_mla_w_ukv, v_mla_out_g, v_gdn_conv_w, v_gdn_a_log, v_gdn_dt_bias, v_gdn_norm_g, v_w_out, v_mix_post_g, v_ffn2_pre_g, v_ffn2_w_gate, v_ffn2_w_up, v_ffn2_w_down, v_ffn2_post_g):
    args = dict(locals())
    wsh = {n: args[n][0] for n in WEIGHTS}
    msh = {n: args["m_" + n] for n in SMALL}
    vsh = {n: args["v_" + n] for n in SMALL}
    for n in SMALL:
        wsh[n] = args[n]
    mix_shapes = [(n, wsh[n].shape) for n in MIX_BIG]

    early = FFN_BIG[:3]
    held = lambda a, n: jnp.swapaxes(a, 1, 2) if n in TRANSPOSED else a
    w_in_wire = jnp.pad(held(w_in, "w_in")[0].astype(MM_DTYPE), ((0, W_IN_SHARD_PAD - W_IN_SHARD), (0, 0)))
    gathered = _gather_shards([_to_wire("wire_" + n, held(args[n], n)) for n in early]
                              + [w_in_wire, _pack([wsh[n] for n in MIX_BIG], MM_DTYPE)])
    full = {n: wsh[n] for n in SMALL}
    for n, gw in zip(early + ["w_in"], gathered):
        full[n] = gw
    parts = [_unpack(gathered[-1][q], mix_shapes) for q in range(N_SHARD)]
    for n in MIX_BIG:
        full[n] = jnp.concatenate([parts[q][n] for q in range(N_SHARD)], axis=SHARD_AXIS[n])

    core = lax.axis_index("c").astype(jnp.int32).reshape(1)
    chip = (2 * lax.axis_index("x") + lax.axis_index("y")).astype(jnp.int32).reshape(1)
    late = ([_to_wire("wire_" + n, held(args[n], n)) for n in FFN2_BIG], core, chip)
    lsum, grad_x, g = _local_step(x[0], positions, loss_target[0], full, late)
    loss = lax.psum(0.5 * jnp.sum(lsum) / x.shape[-1], ("x", "y", "c"))

    halves = [g[n] for n in FFN_BIG] + [g["w_in"], g["mix_pack"]]
    others = _share_halves(halves)
    shared = [_join_halves("join_halves_%d" % i, hm, ho, core) for i, (hm, ho) in enumerate(zip(halves, others))]
    gsh = _unpack(shared[-1], mix_shapes)
    for n, sg_ in zip(FFN_BIG, shared):
        gsh[n] = sg_
    gsh["w_in"] = shared[-2][:, :W_IN_SHARD]

    small_shapes = [(n, wsh[n].shape) for n in SMALL]
    pack_small = lambda d: jnp.concatenate(
        [_pad_lanes(d[n].astype(F32), LANES) for n in SMALL] + [jnp.zeros((SMALL_ROWS - len(SMALL), LANES), F32)], axis=0)
    slots = _gather_small(pack_small(g))

    c1 = 1.0 - ADAM_B1 ** ADAM_STEP
    c2 = 1.0 - ADAM_B2 ** ADAM_STEP

    def small_update(wb, mb, vb, s8):
        gs = s8[0:SMALL_ROWS]
        for d in range(1, 8):
            gs = gs + s8[d * SMALL_ROWS:(d + 1) * SMALL_ROWS]
        m2 = ADAM_B1 * mb + (1.0 - ADAM_B1) * gs
        v2 = ADAM_B2 * vb + (1.0 - ADAM_B2) * (gs * gs)
        delta = -ADAM_LR * ((m2 / c1) / (jnp.sqrt(v2 / c2) + ADAM_EPS) + ADAM_WD * wb)
        return gs, delta, m2, v2

    sg, sd, sm, sv_ = _rowwise("adamw_small", small_update,
                               [pack_small(wsh), pack_small(msh), pack_small(vsh)],
                               [slots.reshape(8 * SMALL_ROWS, LANES)], [(LANES, F32)] * 4, [], SMALL_ROWS)
    grads, deltas, new_m, new_v = {}, {}, {}, {}
    for i, (n, shp) in enumerate(small_shapes):
        grads[n], deltas[n] = sg[i:i + 1, :shp[1]], sd[i:i + 1, :shp[1]]
        new_m[n], new_v[n] = sm[i:i + 1, :shp[1]], sv_[i:i + 1, :shp[1]]
    for n in BIG:
        w3 = held(args[n], n)
        outs = _adamw("adamw_" + n, w3, gsh[n], held(args["m_" + n], n), held(args["v_" + n], n),
                      _row_tile(w3.shape[1], 256))
        grads[n], deltas[n], new_m[n], new_v[n] = [held(o, n) for o in outs]

    return (loss, grad_x[None], *[grads[n] for n in WEIGHTS], *[deltas[n] for n in WEIGHTS],
            *[new_m[n] for n in WEIGHTS], *[new_v[n] for n in WEIGHTS])
```

```python
import functools

import jax
import jax.numpy as jnp
from jax import lax
from jax.experimental import pallas as pl
from jax.experimental.pallas import tpu as pltpu

F32 = jnp.float32
BF16 = jnp.bfloat16
MM_DTYPE = BF16
MESH = pl.DeviceIdType.MESH

D_MODEL = 1024
D_FF = 2816
N_HEADS = 8
MLA_Q_RANK = 256
MLA_KV_RANK = 128
MLA_NOPE = 64
MLA_ROPE = 32
MLA_V = 64
ROPE_THETA = 10000.0
GDN_DH = 64
GDN_W = N_HEADS * GDN_DH
GDN_CONV = 4
CHUNK = 64
GDN_STEP_CHUNKS = 4
HEAD_LANES = 128
HEADS_PER_STEP = 4
MLA_PAD = N_HEADS * HEAD_LANES
EPS = 1e-6
N_SHARD = 4
LANES = 1024

PIN_QKV = 0
PIN_MLA = 1536
PIN_KPE = 1920
PIN_GATE = 2048
PIN_AB = 2560
PIN_W = 2688
CAT_W = MLA_PAD + GDN_W

ADAM_LR = 0.001
ADAM_B1 = 0.9
ADAM_B2 = 0.999
ADAM_EPS = 1e-08
ADAM_WD = 0.01
ADAM_STEP = 10

VMEM_LIMIT_V7X = 56 * 1024 * 1024

BIG = ["ffn1_w_gate", "ffn1_w_up", "ffn1_w_down", "w_in", "mla_w_uq", "mla_w_ukv", "gdn_conv_w", "w_out",
       "ffn2_w_gate", "ffn2_w_up", "ffn2_w_down"]
FFN_BIG = ["ffn1_w_gate", "ffn1_w_up", "ffn1_w_down", "ffn2_w_gate", "ffn2_w_up", "ffn2_w_down"]
TRANSPOSED = ["ffn1_w_gate", "ffn1_w_up", "ffn2_w_gate", "ffn2_w_up", "w_in"]
MIX_BIG = ["mla_w_uq", "mla_w_ukv", "gdn_conv_w", "w_out"]
SMALL = ["ffn1_pre_g", "ffn1_post_g", "mix_pre_g", "mla_q_norm_g", "mla_kv_norm_g", "mla_out_g", "gdn_a_log",
         "gdn_dt_bias", "gdn_norm_g", "mix_post_g", "ffn2_pre_g", "ffn2_post_g"]
WEIGHTS = ["ffn1_pre_g", "ffn1_w_gate", "ffn1_w_up", "ffn1_w_down", "ffn1_post_g", "mix_pre_g", "w_in",
           "mla_q_norm_g", "mla_w_uq", "mla_kv_norm_g", "mla_w_ukv", "mla_out_g", "gdn_conv_w", "gdn_a_log",
           "gdn_dt_bias", "gdn_norm_g", "w_out", "mix_post_g", "ffn2_pre_g", "ffn2_w_gate", "ffn2_w_up",
           "ffn2_w_down", "ffn2_post_g"]
SHARD_AXIS = {"ffn1_w_gate": 1, "ffn1_w_up": 1, "ffn1_w_down": 0, "w_in": 1, "mla_w_uq": 1, "mla_w_ukv": 1,
              "gdn_conv_w": 1, "w_out": 0, "ffn2_w_gate": 1, "ffn2_w_up": 1, "ffn2_w_down": 0}
SMALL_ROWS = 16


def _params(sem):
    return pltpu.CompilerParams(dimension_semantics=sem, vmem_limit_bytes=VMEM_LIMIT_V7X)


def _pick(dim, pref):
    if dim <= pref:
        return dim
    t = (pref // 128) * 128
    while t >= 128:
        if dim % t == 0:
            return t
        t -= 128
    return dim


ANY_SPEC = pl.BlockSpec(memory_space=pl.ANY)


def _rowwise(name, fn, row_ins, bc_ins, row_outs, acc_outs, tb, wide=None, carry=None):
    ents = []
    for e in row_ins:
        ents.append(e if isinstance(e, tuple) else (e, e.shape[1], 0, 0))
    over = [o[2] for o in row_outs if len(o) == 3]
    rows = over[0] if over else ents[0][0].shape[0]
    steps = rows // tb
    assert steps * tb == rows, (name, rows, tb)
    in_specs, args = [], []
    for a, w, j, r0 in ents:
        in_specs.append(pl.BlockSpec((tb, w), lambda i, j=j, r0=r0: (i + r0, j)))
        args.append(a)
    for b in bc_ins:
        in_specs.append(pl.BlockSpec(b.shape, lambda i: (0, 0)))
        args.append(b)
    n_in = len(args)
    aliases = {}
    if carry is not None:
        in_specs.append(ANY_SPEC)
        args.append(carry)
        aliases = {n_in: 0}
    out_shape = [jax.ShapeDtypeStruct((rows, o[0]), o[1]) for o in row_outs]
    out_specs = [pl.BlockSpec((tb, o[0]), lambda i: (i, 0)) for o in row_outs]
    if wide is not None:
        out_shape[0] = jax.ShapeDtypeStruct((rows, wide[0]), row_outs[0][1])
        out_specs[0] = pl.BlockSpec((tb, row_outs[0][0]), lambda i: (i, wide[1]))
    out_shape += [jax.ShapeDtypeStruct((r, c), F32) for r, c in acc_outs]
    out_specs += [pl.BlockSpec((r, c), lambda i: (0, 0)) for r, c in acc_outs]
    n_ro, n_acc, n_args = len(row_outs), len(acc_outs), len(args)

    def body(*refs):
        vals = fn(*[r[...] for r in refs[:n_in]])
        if not isinstance(vals, (tuple, list)):
            vals = (vals,)
        for r, v in zip(refs[n_args:n_args + n_ro], vals[:n_ro]):
            r[...] = v.astype(r.dtype)
        if n_acc:
            acc_refs = refs[n_args + n_ro:]

            @pl.when(pl.program_id(0) == 0)
            def _():
                for r in acc_refs:
                    r[...] = jnp.zeros(r.shape, r.dtype)

            for r, v in zip(acc_refs, vals[n_ro:]):
                r[...] += v

    outs = pl.pallas_call(body, name=name, grid=(steps,), in_specs=in_specs, out_specs=out_specs,
                          out_shape=out_shape, input_output_aliases=aliases,
                          compiler_params=_params(("arbitrary",)))(*args)
    return outs


def _mm(name, a, b, mode, out_dtype, tm=1024, tn=1024, tk=1024):
    if mode == "nn":
        (m, k), (k2, n) = a.shape, b.shape
    elif mode == "nt":
        (m, k), (n, k2) = a.shape, b.shape
    else:
        (k, m), (k2, n) = a.shape, b.shape
    assert k == k2, (name, a.shape, b.shape)
    tm, tn, tk = _pick(m, tm), _pick(n, tn), _pick(k, tk)
    nk = k // tk
    if mode == "nn":
        a_spec = pl.BlockSpec((tm, tk), lambda i, j, kk: (i, kk))
        b_spec = pl.BlockSpec((tk, tn), lambda i, j, kk: (kk, j))
        dims = (((1,), (0,)), ((), ()))
    elif mode == "nt":
        a_spec = pl.BlockSpec((tm, tk), lambda i, j, kk: (i, kk))
        b_spec = pl.BlockSpec((tn, tk), lambda i, j, kk: (j, kk))
        dims = (((1,), (1,)), ((), ()))
    else:
        a_spec = pl.BlockSpec((tk, tm), lambda i, j, kk: (kk, i))
        b_spec = pl.BlockSpec((tk, tn), lambda i, j, kk: (kk, j))
        dims = (((0,), (0,)), ((), ()))

    def body(a_ref, b_ref, o_ref, acc_ref):
        kk = pl.program_id(2)

        @pl.when(kk == 0)
        def _():
            acc_ref[...] = jnp.zeros(acc_ref.shape, F32)

        acc_ref[...] += lax.dot_general(a_ref[...].astype(MM_DTYPE), b_ref[...].astype(MM_DTYPE), dims,
                                        preferred_element_type=F32)

        @pl.when(kk == nk - 1)
        def _():
            o_ref[...] = acc_ref[...].astype(o_ref.dtype)

    return pl.pallas_call(
        body, name=name, grid=(m // tm, n // tn, nk), in_specs=[a_spec, b_spec],
        out_specs=pl.BlockSpec((tm, tn), lambda i, j, kk: (i, j)),
        out_shape=jax.ShapeDtypeStruct((m, n), out_dtype),
        scratch_shapes=[pltpu.VMEM((tm, tn), F32)],
        compiler_params=_params(("parallel", "parallel", "arbitrary")))(a, b)


def _rms_stats(x, n_real=None):
    n = x.shape[-1] if n_real is None else n_real
    return lax.rsqrt(jnp.sum(x * x, axis=-1, keepdims=True) / n + EPS)


def _rms_bwd(x, r, g, dz, n_real=None):
    n = x.shape[-1] if n_real is None else n_real
    xh = x * r
    dxh = dz * g
    dx = r * (dxh - xh * (jnp.sum(dxh * xh, axis=-1, keepdims=True) / n))
    return dx, jnp.sum(dz * xh, axis=0, keepdims=True)


def _sigmoid(x):
    return 0.5 * jnp.tanh(0.5 * x) + 0.5


def _roll(x, s, axis):
    return pltpu.roll(x, s, axis)


def _rope(x, c, s1, s2):
    return x * c + _roll(x, HEAD_LANES - MLA_ROPE // 2, 1) * s1 + _roll(x, MLA_ROPE // 2, 1) * s2


def _heads_apply(x, fn):
    return jnp.concatenate([fn(x[:, h * HEAD_LANES:(h + 1) * HEAD_LANES]) for h in range(N_HEADS)], axis=1)


ROW_CHUNK = 256


def _row_chunks(rows):
    step = min(ROW_CHUNK, rows)
    return [pl.ds(r, step) for r in range(0, rows, step)]


def _ffn_fwd(tag, x, g_pre, wg, wu, wd, g_post, tm):
    t, d = x.shape
    ns, fs, _ = wg.shape
    nt = t // tm
    row = pl.BlockSpec((tm, d), lambda i, q: (i, 0))
    vec = pl.BlockSpec((1, d), lambda i, q: (0, 0))
    act3 = pl.BlockSpec((1, tm, fs), lambda i, q: (q, i, 0))
    wrow = pl.BlockSpec((1, fs, d), lambda i, q: (q, 0, 0))
    nt_dims = (((1,), (1,)), ((), ()))

    def gate_up(x_ref, g_ref, wg_ref, wu_ref, n_ref, sl_ref, ud_ref, s_ref, n_s):
        @pl.when(pl.program_id(1) == 0)
        def _():
            for r in _row_chunks(tm):
                xb = x_ref[r, :]
                n_s[r, :] = (xb * _rms_stats(xb) * g_ref[...]).astype(MM_DTYPE)
            n_ref[...] = n_s[...]

        for r in _row_chunks(tm):
            n = n_s[r, :]
            a = lax.dot_general(n, wg_ref[0], nt_dims, preferred_element_type=F32)
            u = lax.dot_general(n, wu_ref[0], nt_dims, preferred_element_type=F32)
            sg = _sigmoid(a)
            sl = a * sg
            sl_ref[0, r, :] = sl.astype(sl_ref.dtype)
            ud_ref[0, r, :] = (u * (sg + sl * (1.0 - sg))).astype(ud_ref.dtype)
            s_ref[0, r, :] = (sl * u).astype(s_ref.dtype)

    n, sl, ud, s = pl.pallas_call(
        gate_up, name=tag + "_gate_up", grid=(nt, ns), in_specs=[row, vec, wrow, wrow],
        out_specs=[row, act3, act3, act3],
        out_shape=[jax.ShapeDtypeStruct((t, d), MM_DTYPE)] + [jax.ShapeDtypeStruct((ns, t, fs), MM_DTYPE)] * 3,
        scratch_shapes=[pltpu.VMEM((tm, d), MM_DTYPE)],
        compiler_params=_params(("parallel", "arbitrary")))(x, g_pre, wg, wu)

    def down(s_ref, wd_ref, x_ref, g_ref, h_ref, y_ref, acc):
        q = pl.program_id(1)

        @pl.when(q == 0)
        def _():
            acc[...] = jnp.zeros(acc.shape, F32)

        for r in _row_chunks(tm):
            acc[r, :] += jnp.dot(s_ref[0, r, :], wd_ref[0], preferred_element_type=F32)

        @pl.when(q == ns - 1)
        def _():
            for r in _row_chunks(tm):
                hb = acc[r, :]
                h_ref[r, :] = hb
                y_ref[r, :] = x_ref[r, :] + 0.5 * (hb * _rms_stats(hb) * g_ref[...])

    h, y = pl.pallas_call(
        down, name=tag + "_down", grid=(nt, ns), in_specs=[act3, wrow, row, vec], out_specs=[row, row],
        out_shape=[jax.ShapeDtypeStruct((t, d), F32)] * 2, scratch_shapes=[pltpu.VMEM((tm, d), F32)],
        compiler_params=_params(("parallel", "arbitrary")))(s, wd, x, g_post)
    return y, (x, n, sl, ud, s, h)


def _carry(body, n_in, n_out, grid, carried):
    if carried is None:
        return body, [], [], [], [], []
    nx_in, nx_out = len(carried.ins), len(carried.outs)

    def wrapped(*refs):
        ins, rest = refs[:n_in], refs[n_in:]
        xi, rest = rest[:nx_in], rest[nx_in:]
        outs, rest = rest[:n_out], rest[n_out:]
        xo, rest = rest[:nx_out], rest[nx_out:]
        scr, sems = rest[:len(rest) - 2], rest[len(rest) - 2:]
        first, last = True, True
        for dim, size in enumerate(grid):
            first = first & (pl.program_id(dim) == 0)
            last = last & (pl.program_id(dim) == size - 1)

        @pl.when(first)
        def _():
            carried.start(xi, xo, *sems)

        body(*ins, *outs, *scr)

        @pl.when(last)
        def _():
            carried.finish(xi, xo, *sems)

    sems = [pltpu.SemaphoreType.DMA((carried.n_sem,)), pltpu.SemaphoreType.DMA((carried.n_sem,))]
    return (wrapped, [HBM_SPEC] * nx_in, [HBM_SPEC] * nx_out, list(carried.outs), sems, list(carried.ins))


def _ffn_bwd(tag, dy, saved, g_pre, wg, wu, wd, g_post, tm, tk, carried_down=None, make_carried_mid=None,
             make_carried_up=None):
    x, n, sl, ud, s, h = saved
    t, d = x.shape
    ns, fs, _ = wg.shape
    nt, nk = t // tm, t // tk
    row = pl.BlockSpec((tm, d), lambda i, q: (i, 0))
    vec = pl.BlockSpec((1, d), lambda i, q: (0, 0))
    act3 = pl.BlockSpec((1, tm, fs), lambda i, q: (q, i, 0))
    wrow = pl.BlockSpec((1, fs, d), lambda i, q: (q, 0, 0))
    nt_dims = (((1,), (1,)), ((), ()))
    tn_dims = (((0,), (0,)), ((), ()))

    def down_b(h_ref, dy_ref, g_ref, wd_ref, sl_ref, ud_ref, dh_ref, da_ref, du_ref, dg_ref, dh_s):
        i, q = pl.program_id(0), pl.program_id(1)

        @pl.when((i == 0) & (q == 0))
        def _():
            dg_ref[...] = jnp.zeros(dg_ref.shape, F32)

        @pl.when(q == 0)
        def _():
            for r in _row_chunks(tm):
                hb = h_ref[r, :]
                dh, dg = _rms_bwd(hb, _rms_stats(hb), g_ref[...], 0.5 * dy_ref[r, :])
                dh_s[r, :] = dh.astype(MM_DTYPE)
                dg_ref[...] += dg
            dh_ref[...] = dh_s[...]

        for r in _row_chunks(tm):
            ds = lax.dot_general(dh_s[r, :], wd_ref[0], nt_dims, preferred_element_type=F32)
            da_ref[0, r, :] = (ds * ud_ref[0, r, :].astype(F32)).astype(da_ref.dtype)
            du_ref[0, r, :] = (ds * sl_ref[0, r, :].astype(F32)).astype(du_ref.dtype)

    down_b, x_in, x_out, x_shape, x_scr, x_args = _carry(down_b, 6, 4, (nt, ns), carried_down)
    dh, da, du, dg_post, *from_down = pl.pallas_call(
        down_b, name=tag + "_down_b", grid=(nt, ns), in_specs=[row, row, vec, wrow, act3, act3] + x_in,
        out_specs=[row, act3, act3, vec] + x_out,
        out_shape=[jax.ShapeDtypeStruct((t, d), MM_DTYPE)] + [jax.ShapeDtypeStruct((ns, t, fs), MM_DTYPE)] * 2
        + [jax.ShapeDtypeStruct((1, d), F32)] + x_shape,
        scratch_shapes=[pltpu.VMEM((tm, d), MM_DTYPE)] + x_scr,
        compiler_params=_params(("arbitrary", "arbitrary")))(h, dy, g_post, wd, sl, ud, *x_args)

    def down_w(s_ref, dh_ref, dw_ref, acc):
        kk = pl.program_id(1)

        @pl.when(kk == 0)
        def _():
            acc[...] = jnp.zeros(acc.shape, F32)

        acc[...] += lax.dot_general(s_ref[0], dh_ref[...], tn_dims, preferred_element_type=F32)

        @pl.when(kk == nk - 1)
        def _():
            dw_ref[0] = acc[...].astype(dw_ref.dtype)

    dwd = pl.pallas_call(
        down_w, name=tag + "_down_w", grid=(ns, nk),
        in_specs=[pl.BlockSpec((1, tk, fs), lambda q, kk: (q, kk, 0)), pl.BlockSpec((tk, d), lambda q, kk: (kk, 0))],
        out_specs=pl.BlockSpec((1, fs, d), lambda q, kk: (q, 0, 0)),
        out_shape=jax.ShapeDtypeStruct((ns, fs, d), MM_DTYPE), scratch_shapes=[pltpu.VMEM((fs, d), F32)],
        compiler_params=_params(("parallel", "arbitrary")))(s, dh)

    def gate_up_b(da_ref, du_ref, wg_ref, wu_ref, x_ref, dy_ref, g_ref, dx_ref, dg_ref, acc):
        i, q = pl.program_id(0), pl.program_id(1)

        @pl.when((i == 0) & (q == 0))
        def _():
            dg_ref[...] = jnp.zeros(dg_ref.shape, F32)

        @pl.when(q == 0)
        def _():
            acc[...] = jnp.zeros(acc.shape, F32)

        for r in _row_chunks(tm):
            acc[r, :] += (jnp.dot(da_ref[0, r, :], wg_ref[0], preferred_element_type=F32)
                          + jnp.dot(du_ref[0, r, :], wu_ref[0], preferred_element_type=F32))

        @pl.when(q == ns - 1)
        def _():
            for r in _row_chunks(tm):
                xb = x_ref[r, :]
                dx, dg = _rms_bwd(xb, _rms_stats(xb), g_ref[...], acc[r, :])
                dx_ref[r, :] = dy_ref[r, :] + dx
                dg_ref[...] += dg

    def gate_up_w(n_ref, da_ref, du_ref, dwg_ref, dwu_ref, acc_g, acc_u):
        kk = pl.program_id(1)

        @pl.when(kk == 0)
        def _():
            acc_g[...] = jnp.zeros(acc_g.shape, F32)
            acc_u[...] = jnp.zeros(acc_u.shape, F32)

        nb = n_ref[...]
        acc_g[...] += lax.dot_general(da_ref[0], nb, tn_dims, preferred_element_type=F32)
        acc_u[...] += lax.dot_general(du_ref[0], nb, tn_dims, preferred_element_type=F32)

        @pl.when(kk == nk - 1)
        def _():
            dwg_ref[0] = acc_g[...].astype(dwg_ref.dtype)
            dwu_ref[0] = acc_u[...].astype(dwu_ref.dtype)

    k3 = pl.BlockSpec((1, tk, fs), lambda q, kk: (q, kk, 0))
    wout = pl.BlockSpec((1, fs, d), lambda q, kk: (q, 0, 0))
    carried_mid = make_carried_mid(dwd) if make_carried_mid else None
    gate_up_w, x_in, x_out, x_shape, x_scr, x_args = _carry(gate_up_w, 3, 2, (ns, nk), carried_mid)
    dwg, dwu, *from_mid = pl.pallas_call(
        gate_up_w, name=tag + "_gate_up_w", grid=(ns, nk),
        in_specs=[pl.BlockSpec((tk, d), lambda q, kk: (kk, 0)), k3, k3] + x_in, out_specs=[wout, wout] + x_out,
        out_shape=[jax.ShapeDtypeStruct((ns, fs, d), MM_DTYPE)] * 2 + x_shape,
        scratch_shapes=[pltpu.VMEM((fs, d), F32)] * 2 + x_scr,
        compiler_params=_params(("arbitrary", "arbitrary")))(n, da, du, *x_args)

    carried_up = make_carried_up(dwg, dwu) if make_carried_up else None
    gate_up_b, x_in, x_out, x_shape, x_scr, x_args = _carry(gate_up_b, 7, 2, (nt, ns), carried_up)
    dx, dg_pre, *from_up = pl.pallas_call(
        gate_up_b, name=tag + "_gate_up_b", grid=(nt, ns), in_specs=[act3, act3, wrow, wrow, row, row, vec] + x_in,
        out_specs=[row, vec] + x_out,
        out_shape=[jax.ShapeDtypeStruct((t, d), F32), jax.ShapeDtypeStruct((1, d), F32)] + x_shape,
        scratch_shapes=[pltpu.VMEM((tm, d), F32)] + x_scr,
        compiler_params=_params(("arbitrary", "arbitrary")))(da, du, wg, wu, x, dy, g_pre, *x_args)
    return dx, dg_pre, dwg, dwu, dwd, dg_post, from_down, from_mid, from_up


NEG = -1e30


def _attn_scale():
    return (MLA_NOPE + MLA_ROPE) ** -0.5


def _causal_pairs(nq, by_key):
    if by_key:
        pairs = [(qi, ki) for ki in range(nq) for qi in range(ki, nq)]
    else:
        pairs = [(qi, ki) for qi in range(nq) for ki in range(qi + 1)]
    return jnp.asarray([p[0] for p in pairs], jnp.int32), jnp.asarray([p[1] for p in pairs], jnp.int32)


def _below_diagonal(shape):
    return lax.broadcasted_iota(jnp.int32, shape, 1) <= lax.broadcasted_iota(jnp.int32, shape, 0)


def _attn_call(name, body, tables, args, in_kinds, out_kinds, scratch, t, tq, carried=None):
    qmap = lambda h, p, qt, kt: (qt[p], h)
    kmap = lambda h, p, qt, kt: (kt[p], h)
    width = HEADS_PER_STEP * HEAD_LANES
    spec = lambda kind: pl.BlockSpec((tq, width), qmap if kind == "q" else kmap)
    n_pairs = tables[0].shape[0]
    n_groups = N_HEADS // HEADS_PER_STEP
    n_in, n_out, n_scr = len(in_kinds), len(out_kinds), scratch
    x_ins = list(carried.ins) if carried else []
    x_outs = list(carried.outs) if carried else []
    x_scr = [pltpu.SemaphoreType.DMA((carried.n_sem,)), pltpu.SemaphoreType.DMA((carried.n_sem,))] if carried else []

    def full_body(qt, kt, *refs):
        ins, refs = refs[:n_in], refs[n_in:]
        xi, refs = refs[:len(x_ins)], refs[len(x_ins):]
        outs, refs = refs[:n_out], refs[n_out:]
        xo, refs = refs[:len(x_outs)], refs[len(x_outs):]
        scr, sems = refs[:n_scr], refs[n_scr:]
        if carried:
            @pl.when((pl.program_id(0) == 0) & (pl.program_id(1) == 0))
            def _():
                carried.start(xi, xo, *sems)

        heads = [tuple(r.at[:, pl.ds(hh * HEAD_LANES, HEAD_LANES)] for r in (*ins, *outs, *scr))
                 for hh in range(HEADS_PER_STEP)]
        body(qt, kt, heads)
        if carried:
            @pl.when((pl.program_id(0) == n_groups - 1) & (pl.program_id(1) == n_pairs - 1))
            def _():
                carried.finish(xi, xo, *sems)

    grid_spec = pltpu.PrefetchScalarGridSpec(
        num_scalar_prefetch=2, grid=(n_groups, n_pairs),
        in_specs=[spec(kd) for kd in in_kinds] + [HBM_SPEC] * len(x_ins),
        out_specs=[spec(kd) for kd in out_kinds] + [HBM_SPEC] * len(x_outs),
        scratch_shapes=[pltpu.VMEM((tq, width), F32)] * n_scr + x_scr)
    return pl.pallas_call(full_body, name=name, grid_spec=grid_spec,
                          out_shape=[jax.ShapeDtypeStruct((t, MLA_PAD), F32) for _ in out_kinds] + x_outs,
                          compiler_params=_params(("arbitrary", "arbitrary")))(*tables, *args, *x_ins)


class _Carried:
    def __init__(self, ins, outs, n_sem, start, finish):
        self.ins, self.outs, self.n_sem, self.start, self.finish = ins, outs, n_sem, start, finish


def _attn_fwd(q, k, v, tq, carried=None):
    t = q.shape[0]
    nq = t // tq

    def body(qt, kt, heads):
        p_id = pl.program_id(1)
        qi, ki = qt[p_id], kt[p_id]

        @pl.when(ki == 0)
        def _():
            for _, _, _, _, _, m_s, l_s, acc_s in heads:
                m_s[...] = jnp.full(m_s.shape, NEG, F32)
                l_s[...] = jnp.zeros(l_s.shape, F32)
                acc_s[...] = jnp.zeros(acc_s.shape, F32)

        def update(diagonal):
            for q_ref, k_ref, v_ref, _, _, m_s, l_s, acc_s in heads:
                s = lax.dot_general(q_ref[...], k_ref[...], (((1,), (1,)), ((), ())), preferred_element_type=F32)
                if diagonal:
                    s = jnp.where(_below_diagonal(s.shape), s, NEG)
                m_old = m_s[...]
                m_new = jnp.maximum(m_old, jnp.max(s, axis=1, keepdims=True))
                alpha = jnp.exp(m_old - m_new)
                p = jnp.exp(s - m_new[:, :1])
                l_s[...] = l_s[...] * alpha + jnp.sum(p, axis=1, keepdims=True)
                acc_s[...] = acc_s[...] * alpha + jnp.dot(p.astype(MM_DTYPE), v_ref[...], preferred_element_type=F32)
                m_s[...] = m_new

        @pl.when(ki < qi)
        def _():
            update(False)

        @pl.when(ki == qi)
        def _():
            update(True)
            for _, _, _, o_ref, lse_ref, m_s, l_s, acc_s in heads:
                o_ref[...] = acc_s[...] / l_s[...]
                lse_ref[...] = m_s[...] + jnp.log(l_s[...])

    return _attn_call("mla_attn_fwd", body, _causal_pairs(nq, False), (q, k, v), "qkk", "qq", 3, t, tq, carried)


def _attn_probs(q, k, lse, diagonal):
    s = lax.dot_general(q, k, (((1,), (1,)), ((), ())), preferred_element_type=F32)
    p = jnp.exp(s - lse[:, :1])
    return jnp.where(_below_diagonal(s.shape), p, 0.0) if diagonal else p


BWD_HEADS = 2


def _attn_bwd(q, k, v, do, lse, delta, tq, carried=None):
    t = q.shape[0]
    nq = t // tq
    width = BWD_HEADS * HEAD_LANES
    n_groups = N_HEADS // BWD_HEADS
    qt_tab, kt_tab = _causal_pairs(nq, True)
    n_pairs = qt_tab.shape[0]
    qmap = lambda h, p, qt, kt: (qt[p], h)
    kmap = lambda h, p, qt, kt: (kt[p], h)
    qs, ks = pl.BlockSpec((tq, width), qmap), pl.BlockSpec((tq, width), kmap)
    x_ins = list(carried.ins) if carried else []
    x_outs = list(carried.outs) if carried else []
    x_scr = [pltpu.SemaphoreType.DMA((carried.n_sem,)), pltpu.SemaphoreType.DMA((carried.n_sem,))] if carried else []
    nt_dims = (((1,), (1,)), ((), ()))
    tn_dims = (((0,), (0,)), ((), ()))

    def body(qt, kt, q_ref, k_ref, v_ref, do_ref, lse_ref, dl_ref, *rest):
        xi, rest = rest[:len(x_ins)], rest[len(x_ins):]
        dq_hbm, dk_ref, dv_ref = rest[:3]
        xo, rest = rest[3:3 + len(x_outs)], rest[3 + len(x_outs):]
        dk_s, dv_s, dq_s, dq_sem = rest[:4]
        sems = rest[4:]
        grp, p_id = pl.program_id(0), pl.program_id(1)
        qi, ki = qt[p_id], kt[p_id]
        if carried:
            @pl.when((grp == 0) & (p_id == 0))
            def _():
                carried.start(xi, xo, *sems)

        @pl.when(p_id == 0)
        def _():
            dq_s[...] = jnp.zeros(dq_s.shape, F32)

        def step(diagonal):
            rows = pl.ds(pl.multiple_of(qi * tq, tq), tq)
            for hh in range(BWD_HEADS):
                ln = pl.ds(hh * HEAD_LANES, HEAD_LANES)
                qb, kb, vb, dob = q_ref[:, ln], k_ref[:, ln], v_ref[:, ln], do_ref[:, ln]
                p = _attn_probs(qb, kb, lse_ref[:, ln], diagonal)
                dv_s[:, ln] += lax.dot_general(p.astype(MM_DTYPE), dob, tn_dims, preferred_element_type=F32)
                dp = lax.dot_general(dob, vb, nt_dims, preferred_element_type=F32)
                ds = (p * (dp - dl_ref[:, ln][:, :1])).astype(MM_DTYPE)
                dk_s[:, ln] += lax.dot_general(ds, qb, tn_dims, preferred_element_type=F32)
                dq_s[rows, ln] += jnp.dot(ds, kb, preferred_element_type=F32)

        @pl.when(qi == ki)
        def _():
            dk_s[...] = jnp.zeros(dk_s.shape, F32)
            dv_s[...] = jnp.zeros(dv_s.shape, F32)
            step(True)

        @pl.when(qi > ki)
        def _():
            step(False)

        @pl.when(qi == nq - 1)
        def _():
            dk_ref[...] = dk_s[...]
            dv_ref[...] = dv_s[...]

        @pl.when(p_id == n_pairs - 1)
        def _():
            out = pltpu.make_async_copy(dq_s, dq_hbm.at[pl.ds(pl.multiple_of(grp * t, t), t)], dq_sem)
            out.start()
            out.wait()

        if carried:
            @pl.when((grp == n_groups - 1) & (p_id == n_pairs - 1))
            def _():
                carried.finish(xi, xo, *sems)

    grid_spec = pltpu.PrefetchScalarGridSpec(
        num_scalar_prefetch=2, grid=(n_groups, n_pairs),
        in_specs=[qs, ks, ks, qs, qs, qs] + [HBM_SPEC] * len(x_ins),
        out_specs=[HBM_SPEC, ks, ks] + [HBM_SPEC] * len(x_outs),
        scratch_shapes=[pltpu.VMEM((tq, width), F32), pltpu.VMEM((tq, width), F32), pltpu.VMEM((t, width), F32),
                        pltpu.SemaphoreType.DMA] + x_scr)
    return pl.pallas_call(
        body, name="mla_attn_bwd", grid_spec=grid_spec,
        out_shape=[jax.ShapeDtypeStruct((n_groups * t, width), F32), jax.ShapeDtypeStruct((t, MLA_PAD), F32),
                   jax.ShapeDtypeStruct((t, MLA_PAD), F32)] + x_outs,
        compiler_params=_params(("arbitrary", "arbitrary")))(qt_tab, kt_tab, q, k, v, do, lse, delta, *x_ins)


def _dot01(a, b, dims=(((1,), (0,)), ((), ())), ones="rhs"):
    val, sel = (a, b) if ones == "rhs" else (b, a)
    head = val.astype(BF16)
    tail = (val - head.astype(F32)).astype(BF16)
    sel = sel.astype(BF16)
    dot = lambda part: (lax.dot_general(part, sel, dims, preferred_element_type=F32) if ones == "rhs"
                        else lax.dot_general(sel, part, dims, preferred_element_type=F32))
    return dot(head) + dot(tail)


def _dot1(a, b, dims=(((1,), (0,)), ((), ()))):
    return lax.dot_general(a.astype(MM_DTYPE), b.astype(MM_DTYPE), dims, preferred_element_type=F32)


def _dot3(a, b, dims=(((1,), (0,)), ((), ()))):
    return lax.dot_general(a, b, dims, preferred_element_type=F32, precision=lax.Precision.HIGH)


NN3 = (((2,), (1,)), ((0,), (0,)))
NT3 = (((2,), (2,)), ((0,), (0,)))
TN3 = (((1,), (1,)), ((0,), (0,)))


def _tri_masks(nh):
    shape = (nh, CHUNK, CHUNK)
    return lax.broadcasted_iota(jnp.int32, shape, 1), lax.broadcasted_iota(jnp.int32, shape, 2)


def _gdn_chunk_common(k, gcc, bb, row, col, dot=_dot1):
    tril = row >= col
    gcr = jnp.swapaxes(gcc, 1, 2)
    dm = jnp.exp(jnp.where(tril, gcc - gcr, NEG))
    kb = k * bb
    lm = jnp.where(row > col, dot(kb, k, NT3) * dm, 0.0)
    return dm, kb, lm


def _unit_lower_inverse(lm, eye):
    t = eye - lm
    p = lm
    for _ in range(CHUNK.bit_length() - 2):
        p = _dot3(p, p, NN3)
        t = t + _dot3(t, p, NN3)
    return t


def _chunk_sum_matrix(tb, upper):
    r = lax.broadcasted_iota(jnp.int32, (tb, tb), 0)
    c = lax.broadcasted_iota(jnp.int32, (tb, tb), 1)
    same = (r // CHUNK) == (c // CHUNK)
    return (same & ((c >= r) if upper else (c <= r))).astype(F32)


def _gdn_fwd(q, k, v, gb, bb, carried=None):
    nh, t, dh = q.shape
    nchunk = t // CHUNK

    def body(q_ref, k_ref, v_ref, g_ref, b_ref, o_ref, sall_ref, tall_ref, s_s):
        @pl.when(pl.program_id(0) == 0)
        def _():
            s_s[...] = jnp.zeros(s_s.shape, F32)

        row, col = _tri_masks(nh)
        sh = s_s[...]
        for cc in range(cps):
            rows = pl.ds(cc * CHUNK, CHUNK)
            qh, kh, vh, bbh, gcc = q_ref[:, rows, :], k_ref[:, rows, :], v_ref[:, rows, :], b_ref[:, rows, :], \
                g_ref[:, rows, :]
            dm, kb, lm = _gdn_chunk_common(kh, gcc, bbh, row, col)
            eg = jnp.exp(gcc)
            glr = gcc[:, CHUNK - 1:CHUNK, :]
            th = _unit_lower_inverse(lm, (row == col).astype(F32))
            w = _dot1(th, kb * eg, NN3)
            u = _dot1(th, vh * bbh, NN3)
            at = jnp.where(row >= col, _dot1(qh, kh, NT3) * dm, 0.0)
            vn = u - _dot1(w, sh, NN3)
            o_ref[:, rows, :] = _dot1(qh * eg, sh, NN3) + _dot1(at, vn, NN3)
            kd = kh * jnp.exp(glr - gcc)
            sall_ref[:, cc] = sh
            tall_ref[:, rows, :] = th
            sh = sh * jnp.exp(glr) + _dot1(kd, vn, TN3)
        s_s[...] = sh

    cps = min(GDN_STEP_CHUNKS, nchunk)
    steps = nchunk // cps
    blk = pl.BlockSpec((nh, cps * CHUNK, dh), lambda n: (0, n, 0))
    body, x_in, x_out, x_shape, x_scr, x_args = _carry(body, 5, 3, (steps,), carried)
    return pl.pallas_call(
        body, name="gdn_fwd", grid=(steps,), in_specs=[blk] * 5 + x_in,
        out_specs=[blk, pl.BlockSpec((nh, cps, dh, dh), lambda n: (0, n, 0, 0)), blk] + x_out,
        out_shape=[jax.ShapeDtypeStruct((nh, t, dh), F32), jax.ShapeDtypeStruct((nh, nchunk, dh, dh), F32),
                   jax.ShapeDtypeStruct((nh, t, CHUNK), F32)] + x_shape,
        scratch_shapes=[pltpu.VMEM((nh, dh, dh), F32)] + x_scr,
        compiler_params=_params(("arbitrary",)))(q, k, v, gb, bb, *x_args)


def _gdn_bwd(q, k, v, gb, bb, sall, tall, do):
    nh, t, dh = q.shape
    nchunk = t // CHUNK

    def body(q_ref, k_ref, v_ref, g_ref, b_ref, sall_ref, tall_ref, do_ref,
             dq_ref, dk_ref, dv_ref, dg_ref, db_ref, ds_s):
        @pl.when(pl.program_id(0) == 0)
        def _():
            ds_s[...] = jnp.zeros(ds_s.shape, F32)

        row, col = _tri_masks(nh)
        tril, stril = row >= col, row > col
        rsum = lambda x: jnp.sum(x, axis=2, keepdims=True)
        dsp = ds_s[...]
        for cc in reversed(range(cps)):
            rows = pl.ds(cc * CHUNK, CHUNK)
            dsp = chunk_bwd(rows, cc, dsp, row, col, tril, stril, rsum, q_ref, k_ref, v_ref, g_ref, b_ref, sall_ref,
                            tall_ref, do_ref, dq_ref, dk_ref, dv_ref, dg_ref, db_ref)
        ds_s[...] = dsp

    def chunk_bwd(rows, cc, dsp, row, col, tril, stril, rsum, q_ref, k_ref, v_ref, g_ref, b_ref, sall_ref, tall_ref,
                  do_ref, dq_ref, dk_ref, dv_ref, dg_ref, db_ref):
        qh, kh, vh, gcc, bbh = q_ref[:, rows, :], k_ref[:, rows, :], v_ref[:, rows, :], g_ref[:, rows, :], \
            b_ref[:, rows, :]
        sh, th, doh = sall_ref[:, cc], tall_ref[:, rows, :], do_ref[:, rows, :]
        dm, kb, lm = _gdn_chunk_common(kh, gcc, bbh, row, col, _dot3)
        eg = jnp.exp(gcc)
        glr = gcc[:, CHUNK - 1:CHUNK, :]
        glv = jnp.exp(glr)
        egl = jnp.exp(glr - gcc)
        rw, ru = kb * eg, vh * bbh
        w, u = _dot3(th, rw, NN3), _dot3(th, ru, NN3)
        at = jnp.where(tril, _dot3(qh, kh, NT3) * dm, 0.0)
        qd, kd = qh * eg, kh * egl
        vn = u - _dot3(w, sh, NN3)
        dgl = jnp.sum(rsum(dsp * sh), axis=1, keepdims=True)
        dkd = _dot3(vn, dsp, NT3)
        dvn = _dot3(kd, dsp, NN3)
        dqd = _dot3(doh, sh, NT3)
        dat = jnp.where(tril, _dot3(doh, vn, NT3), 0.0)
        dvn = dvn + _dot3(at, doh, TN3)
        dw = -_dot3(dvn, sh, NT3)
        ds_before = dsp * glv + _dot3(qd, doh, TN3) - _dot3(w, dvn, TN3)
        dpa = dat * dm
        dq_ref[:, rows, :] = _dot1(dpa, kh, NN3) + dqd * eg
        dk = _dot1(dpa, qh, TN3) + dkd * egl
        t6 = rsum(dkd * kd)
        dgam = rsum(dqd * qd) - t6
        dgam_last = jnp.sum(t6, axis=1, keepdims=True) + dgl * glv
        drw = _dot3(th, dw, TN3)
        dru = _dot3(th, dvn, TN3)
        dl = -jnp.where(stril, _dot3(drw, w, NT3) + _dot3(dru, u, NT3), 0.0)
        dgam = dgam + rsum(drw * rw)
        dv_ref[:, rows, :] = dru * bbh
        dp2 = dl * dm
        dkb = drw * eg + _dot1(dp2, kh, NN3)
        dk_ref[:, rows, :] = dk + _dot1(dp2, kb, TN3) + dkb * bbh
        db_ref[:, rows, :] = rsum(dru * vh) + rsum(dkb * kh) + jnp.zeros((nh, CHUNK, dh), F32)
        e = dat * at + dl * lm
        dgam_b = dgam + rsum(e) - _dot01(e, jnp.ones((nh, CHUNK, CHUNK), F32), TN3)
        dg_ref[:, rows, :] = dgam_b + jnp.where(row == CHUNK - 1, dgam_last, 0.0)
        return ds_before

    cps = min(GDN_STEP_CHUNKS, nchunk)
    steps = nchunk // cps
    rev = lambda n: (0, steps - 1 - n, 0)
    blk = pl.BlockSpec((nh, cps * CHUNK, dh), rev)
    sblk = pl.BlockSpec((nh, cps, dh, dh), lambda n: (0, steps - 1 - n, 0, 0))
    out = jax.ShapeDtypeStruct((nh, t, dh), F32)
    return pl.pallas_call(
        body, name="gdn_bwd", grid=(steps,), in_specs=[blk] * 5 + [sblk, blk, blk], out_specs=[blk] * 5,
        out_shape=[out] * 5, scratch_shapes=[pltpu.VMEM((nh, dh, dh), F32)],
        compiler_params=_params(("arbitrary",)))(q, k, v, gb, bb, sall, tall, do)


def _group_ones():
    r = lax.broadcasted_iota(jnp.int32, (GDN_W, GDN_W), 0) // GDN_DH
    c = lax.broadcasted_iota(jnp.int32, (GDN_W, GDN_W), 1) // GDN_DH
    return (r == c).astype(F32)


def _conv_taps(x, xprev, w, has_prev):
    row = lax.broadcasted_iota(jnp.int32, x.shape, 0)
    out = x * w[GDN_CONV - 1:GDN_CONV, :]
    for s in range(1, GDN_CONV):
        sh = jnp.where(row >= s, _roll(x, s, 0), _roll(xprev, s, 0) * has_prev)
        out = out + sh * w[GDN_CONV - 1 - s:GDN_CONV - s, :]
    return out


def _head_cols(x, h):
    return x[:, h * GDN_DH:(h + 1) * GDN_DH]


def _heads_spec(tb):
    return pl.BlockSpec((N_HEADS, tb, GDN_DH), lambda i: (0, i, 0))


def _mixer_fwd(x, positions, w, tb, carried=None, carried_gdn=None):
    t, d = x.shape
    tables = _rope_tables(positions)

    def pre(xb, g):
        return (xb * _rms_stats(xb) * g,)

    (hn,) = _rowwise("mix_pre", pre, [x], [w["mix_pre_g"]], [(d, BF16)], [], tb)
    proj = _mm("mix_in", hn, w["w_in_pad_t"], "nt", F32)

    def mla_pre(p0, gq, gkv):
        cq, ckv = p0[:, :MLA_Q_RANK], p0[:, MLA_Q_RANK:MLA_Q_RANK + MLA_KV_RANK]
        return cq * _rms_stats(cq) * gq, ckv * _rms_stats(ckv) * gkv

    nq, nkv = _rowwise("mla_pre", mla_pre, [(proj, 512, PIN_MLA // 512, 0)],
                       [w["mla_q_norm_g"], w["mla_kv_norm_g"]], [(MLA_Q_RANK, BF16), (MLA_KV_RANK, BF16)], [], tb)
    qraw = _mm("mla_uq", nq, w["w_uq_pad"], "nn", F32)
    kv = _mm("mla_ukv", nkv, w["w_kv_pad"], "nn", F32)

    def rope_f(qr, kn, vv, kpe, c, s1, s2):
        qo = _heads_apply(qr, lambda xh: _rope(xh, c, s1, s2)) * _attn_scale()
        kp = _rope(kpe, c, s1, s2)
        return qo, kn + jnp.tile(kp, (1, N_HEADS)), vv

    q, k, v = _rowwise("mla_rope", rope_f,
                       [qraw, (kv, MLA_PAD, 0, 0), (kv, MLA_PAD, 1, 0), (proj, HEAD_LANES, PIN_KPE // HEAD_LANES, 0),
                        tables[0], tables[1], tables[2]], [],
                       [(MLA_PAD, BF16)] * 3, [], tb // 2)
    tq = min(1024, t)
    o, lse, *carried_out = _attn_fwd(q, k, v, tq, carried)

    def mla_post(ob, g):
        return (ob * _rms_stats(ob, N_HEADS * MLA_V) * g,)

    (cat,) = _rowwise("mla_post", mla_post, [o], [w["mla_out_g_pad"]], [(MLA_PAD, BF16)], [], tb, wide=(CAT_W, 0))

    gones = _group_ones()
    steps = t // tb

    def gdn_pre(xq, xk, xv, pq, pk, pv, cw, go, has_prev):
        outs = []
        for j, (xc, xp) in enumerate(((xq, pq), (xk, pk), (xv, pv))):
            c = _conv_taps(xc, xp, cw[:, j * GDN_W:(j + 1) * GDN_W], has_prev)
            a = c * _sigmoid(c)
            if j < 2:
                rn = lax.rsqrt(_dot01(a * a, go) + EPS)
                a = a * rn
                if j == 0:
                    a = a * (GDN_DH ** -0.5)
            outs.append(a)
        return tuple(outs)

    qh, kh, vh = _gdn_pre_call("gdn_pre", gdn_pre, proj, w["conv_w"], gones, tb, steps)
    heads_shape = jax.ShapeDtypeStruct((N_HEADS, t, GDN_DH), F32)
    lanes_shape = jax.ShapeDtypeStruct((t, HEAD_LANES), F32)
    lanes_spec = pl.BlockSpec((tb, HEAD_LANES), lambda i: (i, 0))
    vec_spec = lambda n: pl.BlockSpec((1, n), lambda i: (0, 0))

    def gate_f(ab_ref, al_ref, dt_ref, g_ref, b_ref, gh_ref, bh_ref):
        g, b = _gb_fwd(ab_ref[...], al_ref[...], dt_ref[...])
        g_ref[...] = g
        b_ref[...] = b
        gc = _dot01(_chunk_sum_matrix(tb, False), g, ones="lhs")
        for h in range(N_HEADS):
            gh_ref[h] = jnp.broadcast_to(gc[:, h:h + 1], (tb, GDN_DH))
            bh_ref[h] = jnp.broadcast_to(b[:, N_HEADS + h:N_HEADS + h + 1], (tb, GDN_DH))

    g128, b128, gbh, bbh = pl.pallas_call(
        gate_f, name="gdn_gate_f", grid=(steps,),
        in_specs=[pl.BlockSpec((tb, HEAD_LANES), lambda i: (i, PIN_AB // HEAD_LANES)), vec_spec(HEAD_LANES),
                  vec_spec(HEAD_LANES)],
        out_specs=[lanes_spec, lanes_spec, _heads_spec(tb), _heads_spec(tb)],
        out_shape=[lanes_shape, lanes_shape, heads_shape, heads_shape],
        compiler_params=_params(("arbitrary",)))(proj, w["a_log_pad"], w["dt_bias_pad"])
    oh, sall, tall, *carried_out_gdn = _gdn_fwd(qh, kh, vh, gbh, bbh, carried_gdn)

    def gdn_post(o_ref, gt_ref, g_ref, cat_in, cat_ref):
        gt, g = gt_ref[...], g_ref[...]
        outs = []
        for h in range(N_HEADS):
            ob, gth = o_ref[h], _head_cols(gt, h)
            outs.append(ob * _rms_stats(ob) * g * (gth * _sigmoid(gth)))
        cat_ref[...] = jnp.concatenate(outs, axis=1).astype(cat_ref.dtype)

    gate_spec = pl.BlockSpec((tb, GDN_W), lambda i: (i, PIN_GATE // GDN_W))
    cat = pl.pallas_call(
        gdn_post, name="gdn_post", grid=(steps,),
        in_specs=[_heads_spec(tb), gate_spec, vec_spec(GDN_DH), ANY_SPEC],
        out_specs=pl.BlockSpec((tb, GDN_W), lambda i: (i, MLA_PAD // GDN_W)),
        out_shape=jax.ShapeDtypeStruct((t, CAT_W), BF16), input_output_aliases={3: 0},
        compiler_params=_params(("arbitrary",)))(oh, proj, w["gdn_norm_g"], cat)
    mixed = _mm("mix_out", cat, w["w_out_pad"], "nn", F32)

    def post(xb, hb, g):
        return (xb + hb * _rms_stats(hb) * g,)

    (y,) = _rowwise("mix_post", post, [x, mixed], [w["mix_post_g"]], [(d, F32)], [], tb)
    saved = dict(x=x, hn=hn, proj=proj, nq=nq, nkv=nkv, q=q, k=k, v=v, o=o, lse=lse, qh=qh, kh=kh, vh=vh,
                 gbh=gbh, bbh=bbh, oh=oh, sall=sall, tall=tall, cat=cat, mixed=mixed,
                 tables=tables, g128=g128, b128=b128)
    return y, saved, list(carried_out) + list(carried_out_gdn)


def _qkv_specs(tb):
    base = PIN_QKV // GDN_W
    cur = [pl.BlockSpec((tb, GDN_W), lambda i, j=j: (i, base + j)) for j in range(3)]
    prev = [pl.BlockSpec((tb, GDN_W), lambda i, j=j: (jnp.maximum(i - 1, 0), base + j)) for j in range(3)]
    return cur + prev


def _gdn_pre_call(name, fn, proj, conv_w, gones, tb, steps):
    t = proj.shape[0]

    def body(xq, xk, xv, pq, pk, pv, cw, go, oq, ok, ov):
        has_prev = jnp.where(pl.program_id(0) == 0, 0.0, 1.0)
        outs = fn(xq[...], xk[...], xv[...], pq[...], pk[...], pv[...], cw[...], go[...], has_prev)
        for r, val in zip((oq, ok, ov), outs):
            for h in range(N_HEADS):
                r[h] = _head_cols(val, h)

    return pl.pallas_call(
        body, name=name, grid=(steps,),
        in_specs=_qkv_specs(tb) + [pl.BlockSpec(conv_w.shape, lambda i: (0, 0)),
                                   pl.BlockSpec(gones.shape, lambda i: (0, 0))],
        out_specs=[_heads_spec(tb)] * 3,
        out_shape=[jax.ShapeDtypeStruct((N_HEADS, t, GDN_DH), F32)] * 3,
        compiler_params=_params(("arbitrary",)))(proj, proj, proj, proj, proj, proj, conv_w, gones)


def _softplus(x):
    return jnp.maximum(x, 0.0) + jnp.log1p(jnp.exp(-jnp.abs(x)))


def _gb_fwd(ab, a_log, dt_bias):
    g = -jnp.exp(a_log) * _softplus(ab + dt_bias)
    return g, _sigmoid(ab)


def _rope_tables(positions):
    half = MLA_ROPE // 2
    freqs = ROPE_THETA ** (-jnp.arange(half, dtype=F32) / half)
    ang = positions.reshape(-1).astype(F32)[:, None] * freqs
    cos, sin = jnp.cos(ang), jnp.sin(ang)
    t = ang.shape[0]
    one = jnp.ones((t, MLA_NOPE), F32)
    z16, z32, z64 = jnp.zeros((t, half), F32), jnp.zeros((t, MLA_ROPE), F32), jnp.zeros((t, MLA_NOPE), F32)
    c = jnp.concatenate([one, cos, cos, jnp.ones((t, MLA_ROPE), F32)], axis=1)
    s1 = jnp.concatenate([z64, -sin, z16, z32], axis=1)
    s2 = jnp.concatenate([z64, z16, sin, z32], axis=1)
    return c, s1, s2


def _mixer_bwd(dy, sv, w, tb, carried=None):
    x, proj = sv["x"], sv["proj"]
    t, d = x.shape
    c, s1, s2 = sv["tables"]
    grads = {}

    def post_b(hb, dyb, g):
        return _rms_bwd(hb, _rms_stats(hb), g, dyb)

    dmixed, grads["mix_post_g"] = _rowwise("mix_post_b", post_b, [sv["mixed"], dy], [w["mix_post_g"]],
                                           [(d, BF16)], [(1, d)], tb)
    dcat = _mm("mix_out_bx", dmixed, w["w_out_pad"], "nt", F32)
    grads["w_out_pad"] = _mm("mix_out_bw", sv["cat"], dmixed, "tn", F32)
    steps = t // tb
    vec_spec = lambda n: pl.BlockSpec((1, n), lambda i: (0, 0))

    def gdn_post_b(o_ref, gt_ref, do_ref, g_ref, dproj_ref, doh_ref, dg_ref):
        @pl.when(pl.program_id(0) == 0)
        def _():
            dg_ref[...] = jnp.zeros(dg_ref.shape, F32)

        gt, dob, g = gt_ref[...], do_ref[...], g_ref[...]
        dgates = []
        for h in range(N_HEADS):
            ob, gth, dobh = o_ref[h], _head_cols(gt, h), _head_cols(dob, h)
            sg = _sigmoid(gth)
            r = _rms_stats(ob)
            dxo, dg = _rms_bwd(ob, r, g, dobh * (gth * sg))
            doh_ref[h] = dxo
            dg_ref[...] += dg
            dgates.append(dobh * (ob * r * g) * (sg * (1.0 + gth * (1.0 - sg))))
        dproj_ref[...] = jnp.concatenate(dgates, axis=1).astype(dproj_ref.dtype)

    dproj, doh, grads["gdn_norm_g"] = pl.pallas_call(
        gdn_post_b, name="gdn_post_b", grid=(steps,),
        in_specs=[_heads_spec(tb), pl.BlockSpec((tb, GDN_W), lambda i: (i, PIN_GATE // GDN_W)),
                  pl.BlockSpec((tb, GDN_W), lambda i: (i, MLA_PAD // GDN_W)), vec_spec(GDN_DH)],
        out_specs=[pl.BlockSpec((tb, GDN_W), lambda i: (i, PIN_GATE // GDN_W)), _heads_spec(tb), vec_spec(GDN_DH)],
        out_shape=[jax.ShapeDtypeStruct((t, PIN_W), BF16), jax.ShapeDtypeStruct((N_HEADS, t, GDN_DH), F32),
                   jax.ShapeDtypeStruct((1, GDN_DH), F32)],
        compiler_params=_params(("arbitrary",)))(sv["oh"], proj, dcat, w["gdn_norm_g"])

    def mla_post_b(ob, dmo, g):
        do, dg = _rms_bwd(ob, _rms_stats(ob, N_HEADS * MLA_V), g, dmo, N_HEADS * MLA_V)
        prod = do * ob
        delta = _heads_apply(prod, lambda ph: jnp.sum(ph, axis=1, keepdims=True) + jnp.zeros_like(ph))
        return do, delta, dg

    do, delta, grads["mla_out_g_pad"] = _rowwise(
        "mla_post_b", mla_post_b, [sv["o"], (dcat, MLA_PAD, 0, 0)], [w["mla_out_g_pad"]],
        [(MLA_PAD, BF16), (MLA_PAD, F32)], [(1, MLA_PAD)], tb // 2)
    tq = min(1024, t)
    dq, dk, dv, *carried_out = _attn_bwd(sv["q"], sv["k"], sv["v"], do, sv["lse"], delta, tq, carried)
    n_groups = N_HEADS // BWD_HEADS
    tr = tb // 2
    dq_groups = [(dq, BWD_HEADS * HEAD_LANES, 0, grp * (t // tr)) for grp in range(n_groups)]

    def rope_b(*blocks):
        dqb = jnp.concatenate(blocks[:n_groups], axis=1)
        dkb, dvb, cc, a1, a2 = blocks[n_groups:]
        dqr = _heads_apply(dqb * _attn_scale(), lambda xh: _rope(xh, cc, -a1, -a2))
        ksum = dkb[:, :HEAD_LANES]
        for h in range(1, N_HEADS):
            ksum = ksum + dkb[:, h * HEAD_LANES:(h + 1) * HEAD_LANES]
        lane = lax.broadcasted_iota(jnp.int32, ksum.shape, 1)
        keep = (lane >= MLA_NOPE) & (lane < MLA_NOPE + MLA_ROPE)
        dkpe = jnp.where(keep, _rope(ksum, cc, -a1, -a2), 0.0)
        return dqr, jnp.concatenate([dkb, dvb], axis=1), dkpe

    dqraw, dkv, dkpe = _rowwise("mla_rope_b", rope_b, dq_groups + [dk, dv, c, s1, s2], [],
                                [(MLA_PAD, BF16, t), (2 * MLA_PAD, BF16), (HEAD_LANES, F32)], [], tr)
    dnq = _mm("mla_uq_bx", dqraw, w["w_uq_pad"], "nt", F32)
    grads["w_uq_pad"] = _mm("mla_uq_bw", sv["nq"], dqraw, "tn", F32)
    dnkv = _mm("mla_ukv_bx", dkv, w["w_kv_pad"], "nt", F32)
    grads["w_kv_pad"] = _mm("mla_ukv_bw", sv["nkv"], dkv, "tn", F32)

    def mla_pre_b(p0, dnqb, dnkvb, dkpeb, gq, gkv):
        cq, ckv = p0[:, :MLA_Q_RANK], p0[:, MLA_Q_RANK:MLA_Q_RANK + MLA_KV_RANK]
        dcq, dgq = _rms_bwd(cq, _rms_stats(cq), gq, dnqb)
        dckv, dgkv = _rms_bwd(ckv, _rms_stats(ckv), gkv, dnkvb)
        return jnp.concatenate([dcq, dckv, dkpeb], axis=1), dgq, dgkv

    dproj, grads["mla_q_norm_g"], grads["mla_kv_norm_g"] = _rowwise(
        "mla_pre_b", mla_pre_b, [(proj, 512, PIN_MLA // 512, 0), dnq, dnkv, dkpe],
        [w["mla_q_norm_g"], w["mla_kv_norm_g"]], [(512, BF16)], [(1, MLA_Q_RANK), (1, MLA_KV_RANK)], tb,
        wide=(PIN_W, PIN_MLA // 512), carry=dproj)

    dqh, dkh, dvh, dgh, dbh = _gdn_bwd(sv["qh"], sv["kh"], sv["vh"], sv["gbh"], sv["bbh"], sv["sall"], sv["tall"], doh)
    gones = _group_ones()

    def gdn_pre_b(xq, xk, xv, pq, pk, pv, dq_, dk_, dv_, cw, go, has_prev):
        outs = []
        for j, (xc, xp, dd) in enumerate(((xq, pq, dq_), (xk, pk, dk_), (xv, pv, dv_))):
            cc = _conv_taps(xc, xp, cw[:, j * GDN_W:(j + 1) * GDN_W], has_prev)
            sg = _sigmoid(cc)
            a = cc * sg
            if j < 2:
                rn = lax.rsqrt(_dot01(a * a, go) + EPS)
                if j == 0:
                    dd = dd * (GDN_DH ** -0.5)
                da = rn * dd - a * (rn * rn * rn) * _dot01(dd * a, go)
            else:
                da = dd
            outs.append(da * (sg * (1.0 + cc * (1.0 - sg))))
        return tuple(outs)

    dcq, dck, dcv = _gdn_pre_b_call("gdn_pre_b", gdn_pre_b, proj, (dqh, dkh, dvh), w["conv_w"], gones, tb, steps)
    dproj, grads["conv_w"] = _conv_bwd_call("gdn_conv_b", proj, (dcq, dck, dcv), w["conv_w"], dproj, tb, steps)

    def gate_b(ab_ref, g_ref, b_ref, dgh_ref, dbh_ref, al_ref, dt_ref, carry_ref, dab_ref, dal_ref, ddt_ref):
        @pl.when(pl.program_id(0) == 0)
        def _():
            dal_ref[...] = jnp.zeros(dal_ref.shape, F32)
            ddt_ref[...] = jnp.zeros(ddt_ref.shape, F32)

        ab, g128, b128 = ab_ref[...], g_ref[...], b_ref[...]
        lane = lax.broadcasted_iota(jnp.int32, ab.shape, 1)
        dg_ = jnp.zeros(ab.shape, F32)
        db_ = jnp.zeros(ab.shape, F32)
        for h in range(N_HEADS):
            dg_ = dg_ + jnp.where(lane == h, jnp.broadcast_to(dgh_ref[h][:, 0:1], ab.shape), 0.0)
            db_ = db_ + jnp.where(lane == N_HEADS + h, jnp.broadcast_to(dbh_ref[h][:, 0:1], ab.shape), 0.0)
        dg_ = _dot01(_chunk_sum_matrix(tb, True), dg_, ones="lhs")
        slope = -jnp.exp(al_ref[...]) * _sigmoid(ab + dt_ref[...])
        dab_ref[...] = (dg_ * slope + db_ * b128 * (1.0 - b128)).astype(dab_ref.dtype)
        dal_ref[...] += jnp.sum(dg_ * g128, axis=0, keepdims=True)
        ddt_ref[...] += jnp.sum(dg_ * slope, axis=0, keepdims=True)

    lanes_spec = pl.BlockSpec((tb, HEAD_LANES), lambda i: (i, 0))
    ab_spec = pl.BlockSpec((tb, HEAD_LANES), lambda i: (i, PIN_AB // HEAD_LANES))
    dproj, grads["a_log_pad"], grads["dt_bias_pad"] = pl.pallas_call(
        gate_b, name="gdn_gate_b", grid=(steps,),
        in_specs=[ab_spec, lanes_spec, lanes_spec, _heads_spec(tb), _heads_spec(tb), vec_spec(HEAD_LANES),
                  vec_spec(HEAD_LANES), ANY_SPEC],
        out_specs=[ab_spec, vec_spec(HEAD_LANES), vec_spec(HEAD_LANES)],
        out_shape=[jax.ShapeDtypeStruct((t, PIN_W), BF16), jax.ShapeDtypeStruct((1, HEAD_LANES), F32),
                   jax.ShapeDtypeStruct((1, HEAD_LANES), F32)],
        input_output_aliases={7: 0},
        compiler_params=_params(("arbitrary",)))(proj, sv["g128"], sv["b128"], dgh, dbh, w["a_log_pad"],
                                                 w["dt_bias_pad"], dproj)
    dhn = _mm("mix_in_bx", dproj, w["w_in_pad_t"], "nn", F32)
    grads["w_in_pad_t"] = _mm("mix_in_bw", dproj, sv["hn"], "tn", F32)

    def pre_b(xb, dnb, dyb, g):
        dx, dg = _rms_bwd(xb, _rms_stats(xb), g, dnb)
        return dyb + dx, dg

    dx, grads["mix_pre_g"] = _rowwise("mix_pre_b", pre_b, [x, dhn, dy], [w["mix_pre_g"]], [(d, F32)], [(1, d)], tb)
    return dx, grads, carried_out


def _gdn_pre_b_call(name, fn, proj, dd, conv_w, gones, tb, steps):
    t = proj.shape[0]

    def body(xq, xk, xv, pq, pk, pv, d0, d1, d2, cw, go, oq, ok, ov):
        has_prev = jnp.where(pl.program_id(0) == 0, 0.0, 1.0)
        dd_rows = [jnp.concatenate([dr[h] for h in range(N_HEADS)], axis=1) for dr in (d0, d1, d2)]
        outs = fn(xq[...], xk[...], xv[...], pq[...], pk[...], pv[...], *dd_rows, cw[...], go[...], has_prev)
        for r, val in zip((oq, ok, ov), outs):
            r[...] = val

    return pl.pallas_call(
        body, name=name, grid=(steps,),
        in_specs=_qkv_specs(tb) + [_heads_spec(tb)] * 3 + [pl.BlockSpec(conv_w.shape, lambda i: (0, 0)),
                                                          pl.BlockSpec(gones.shape, lambda i: (0, 0))],
        out_specs=[pl.BlockSpec((tb, GDN_W), lambda i: (i, 0))] * 3,
        out_shape=[jax.ShapeDtypeStruct((t, GDN_W), F32)] * 3,
        compiler_params=_params(("arbitrary",)))(proj, proj, proj, proj, proj, proj, *dd, conv_w, gones)


def _conv_bwd_call(name, proj, dc, conv_w, dproj, tb, steps):
    t = proj.shape[0]
    dcur = [pl.BlockSpec((tb, GDN_W), lambda i: (i, 0))] * 3
    dnext = [pl.BlockSpec((tb, GDN_W), lambda i: (jnp.minimum(i + 1, steps - 1), 0))] * 3

    def body(xq, xk, xv, pq, pk, pv, d0, d1, d2, n0, n1, n2, cw, carry_ref, dx_ref, dw_ref):
        i = pl.program_id(0)
        has_prev = jnp.where(i == 0, 0.0, 1.0)
        has_next = jnp.where(i == steps - 1, 0.0, 1.0)

        @pl.when(i == 0)
        def _():
            dw_ref[...] = jnp.zeros(dw_ref.shape, F32)

        wv = cw[...]
        dws, dxs = [], []
        for j, (xr, pr, dr, nr) in enumerate(((xq, pq, d0, n0), (xk, pk, d1, n1), (xv, pv, d2, n2))):
            x, xp, dcv, dnx = xr[...], pr[...], dr[...], nr[...]
            wj = wv[:, j * GDN_W:(j + 1) * GDN_W]
            row = lax.broadcasted_iota(jnp.int32, x.shape, 0)
            dx = dcv * wj[GDN_CONV - 1:GDN_CONV, :]
            rows_w = [jnp.sum(dcv * x, axis=0, keepdims=True)]
            for s in range(1, GDN_CONV):
                up = jnp.where(row < tb - s, _roll(dcv, tb - s, 0), _roll(dnx, tb - s, 0) * has_next)
                dx = dx + up * wj[GDN_CONV - 1 - s:GDN_CONV - s, :]
                sh = jnp.where(row >= s, _roll(x, s, 0), _roll(xp, s, 0) * has_prev)
                rows_w.append(jnp.sum(dcv * sh, axis=0, keepdims=True))
            dxs.append(dx)
            dws.append(jnp.concatenate(rows_w[::-1], axis=0))
        dx_ref[...] = jnp.concatenate(dxs, axis=1).astype(dx_ref.dtype)
        dw_ref[...] += jnp.concatenate(dws, axis=1)

    return pl.pallas_call(
        body, name=name, grid=(steps,),
        in_specs=_qkv_specs(tb) + dcur + dnext + [pl.BlockSpec(conv_w.shape, lambda i: (0, 0)), ANY_SPEC],
        out_specs=[pl.BlockSpec((tb, 3 * GDN_W), lambda i: (i, PIN_QKV // (3 * GDN_W))),
                   pl.BlockSpec(conv_w.shape, lambda i: (0, 0))],
        out_shape=[jax.ShapeDtypeStruct((t, PIN_W), BF16), jax.ShapeDtypeStruct(conv_w.shape, F32)],
        input_output_aliases={13: 0},
        compiler_params=_params(("arbitrary",)))(proj, proj, proj, proj, proj, proj, *dc, *dc, conv_w, dproj)


def _pad_heads_cols(wm, per_head):
    r = wm.shape[0]
    return jnp.pad(wm.reshape(r, N_HEADS, per_head), ((0, 0), (0, 0), (0, HEAD_LANES - per_head))).reshape(r, MLA_PAD)


def _unpad_heads_cols(wm, per_head):
    r = wm.shape[0]
    return wm.reshape(r, N_HEADS, HEAD_LANES)[:, :, :per_head].reshape(r, N_HEADS * per_head)


W_IN_COLS = MLA_Q_RANK + MLA_KV_RANK + MLA_ROPE + 3 * GDN_W + 2 * N_HEADS + GDN_W
W_IN_SHARD = W_IN_COLS // N_SHARD
W_IN_SHARD_PAD = 640
_Q0 = MLA_Q_RANK + MLA_KV_RANK
_Q1 = _Q0 + MLA_ROPE
_Q2 = _Q1 + 3 * GDN_W
_Q3 = _Q2 + 2 * N_HEADS
W_IN_SEGMENTS = [(0, _Q0, PIN_MLA), (_Q0, _Q1, PIN_KPE + MLA_NOPE), (_Q1, _Q2, PIN_QKV), (_Q2, _Q3, PIN_AB),
                 (_Q3, W_IN_COLS, PIN_GATE)]


def _win_pad_t(slabs):
    d = slabs.shape[2]
    pieces, at = [], 0
    for c0, c1, r0 in sorted(W_IN_SEGMENTS, key=lambda s: s[2]):
        if r0 > at:
            pieces.append(jnp.zeros((r0 - at, d), slabs.dtype))
        for q in range(N_SHARD):
            lo, hi = max(c0, q * W_IN_SHARD), min(c1, (q + 1) * W_IN_SHARD)
            if lo < hi:
                pieces.append(slabs[q, lo - q * W_IN_SHARD:hi - q * W_IN_SHARD])
        at = r0 + c1 - c0
    pieces.append(jnp.zeros((PIN_W - at, d), slabs.dtype))
    return jnp.concatenate(pieces, axis=0)


def _win_cols_t(wp_t, c_lo, c_hi):
    pieces = []
    for c0, c1, r0 in W_IN_SEGMENTS:
        lo, hi = max(c0, c_lo), min(c1, c_hi)
        if lo < hi:
            pieces.append(wp_t[r0 + lo - c0:r0 + hi - c0])
    return jnp.concatenate(pieces, axis=0)


def _wkv_to_pad(wkv):
    r = wkv.shape[0]
    w3 = wkv.reshape(r, N_HEADS, MLA_NOPE + MLA_V)
    kpart = jnp.pad(w3[:, :, :MLA_NOPE], ((0, 0), (0, 0), (0, HEAD_LANES - MLA_NOPE))).reshape(r, MLA_PAD)
    vpart = jnp.pad(w3[:, :, MLA_NOPE:], ((0, 0), (0, 0), (0, HEAD_LANES - MLA_V))).reshape(r, MLA_PAD)
    return jnp.concatenate([kpart, vpart], axis=1)


def _wkv_from_pad(wp):
    r = wp.shape[0]
    kpart = wp[:, :MLA_PAD].reshape(r, N_HEADS, HEAD_LANES)[:, :, :MLA_NOPE]
    vpart = wp[:, MLA_PAD:].reshape(r, N_HEADS, HEAD_LANES)[:, :, :MLA_V]
    return jnp.concatenate([kpart, vpart], axis=2).reshape(r, N_HEADS * (MLA_NOPE + MLA_V))


def _wout_to_pad(wo):
    n = wo.shape[1]
    mla = jnp.pad(wo[:N_HEADS * MLA_V].reshape(N_HEADS, MLA_V, n), ((0, 0), (0, HEAD_LANES - MLA_V), (0, 0)))
    return jnp.concatenate([mla.reshape(MLA_PAD, n), wo[N_HEADS * MLA_V:]], axis=0)


def _wout_from_pad(wp):
    n = wp.shape[1]
    mla = wp[:MLA_PAD].reshape(N_HEADS, HEAD_LANES, n)[:, :MLA_V].reshape(N_HEADS * MLA_V, n)
    return jnp.concatenate([mla, wp[MLA_PAD:]], axis=0)


def _pad_lanes(v, n):
    return jnp.pad(v, ((0, 0), (0, n - v.shape[1])))


def _compute_weights(full):
    w = {}
    for n in FFN_BIG:
        if n in full:
            w[n] = full[n].astype(MM_DTYPE)
    w["w_in_pad_t"] = _win_pad_t(full["w_in"]).astype(MM_DTYPE)
    w["w_uq_pad"] = _pad_heads_cols(full["mla_w_uq"], MLA_NOPE + MLA_ROPE).astype(MM_DTYPE)
    w["w_kv_pad"] = _wkv_to_pad(full["mla_w_ukv"]).astype(MM_DTYPE)
    w["w_out_pad"] = _wout_to_pad(full["w_out"]).astype(MM_DTYPE)
    w["conv_w"] = full["gdn_conv_w"].astype(F32)
    for n in ("ffn1_pre_g", "ffn1_post_g", "mix_pre_g", "mla_q_norm_g", "mla_kv_norm_g", "gdn_norm_g", "mix_post_g",
              "ffn2_pre_g", "ffn2_post_g"):
        w[n] = full[n]
    w["mla_out_g_pad"] = _pad_heads_cols(full["mla_out_g"], MLA_V)
    w["a_log_pad"] = _pad_lanes(full["gdn_a_log"], HEAD_LANES)
    w["dt_bias_pad"] = _pad_lanes(full["gdn_dt_bias"], HEAD_LANES)
    return w


FFN2_BIG = FFN_BIG[3:]


def _local_step(x, positions, loss_target, full, late=None):
    t, d = x.shape
    tb = min(512, t)
    tm = min(1024, t)
    tk = min(2048, t)
    w = _compute_weights(full)
    ffn = lambda tag: (w[tag + "_pre_g"], w[tag + "_w_gate"], w[tag + "_w_up"], w[tag + "_w_down"], w[tag + "_post_g"])
    x1, sv1 = _ffn_fwd("ffn1", x, *ffn("ffn1"), tm)
    x2, svm, gathered = _mixer_fwd(x1, positions, w, tb, _carried_gather(late[0][:2]) if late else None,
                                   _carried_gather(late[0][2:]) if late else None)
    for n, gw in zip(FFN2_BIG, gathered):
        w[n] = gw
    x3, sv2 = _ffn_fwd("ffn2", x2, *ffn("ffn2"), tm)

    def loss_f(yb, tg):
        e = yb - tg
        return e * (1.0 / d), jnp.sum(e * e, axis=0, keepdims=True)

    dy, lsum = _rowwise("loss", loss_f, [x3, loss_target], [], [(d, F32)], [(1, d)], tb)
    g = {}
    dx2, g["ffn2_pre_g"], g["ffn2_w_gate"], g["ffn2_w_up"], g["ffn2_w_down"], g["ffn2_post_g"] = _ffn_bwd(
        "ffn2", dy, sv2, *ffn("ffn2"), tm, tk)[:6]

    def pair_sums(arrs, tag):
        got = _swap_halves(arrs, tag)
        return [_add_pair("add_pair%s_%d" % (tag, i), gi, gt, late[1]) for i, (gi, gt) in enumerate(zip(arrs, got))]

    def chip_sums(pairs, slabs, tag):
        return [_add_chips("add_chips%s_%d" % (tag, i), pr, sl, late[2]) for i, (pr, sl) in enumerate(zip(pairs, slabs))]

    if late:
        pairs2 = pair_sums([g[n] for n in FFN2_BIG], "_ffn2")
        dx1, gm, slabs2 = _mixer_bwd(dx2, svm, w, tb, _carried_scatter(pairs2))
        for n, hs in zip(FFN2_BIG, chip_sums(pairs2, slabs2, "_ffn2")):
            g[n] = hs
    else:
        dx1, gm, _ = _mixer_bwd(dx2, svm, w, tb)
    g["w_in"] = jnp.stack([jnp.pad(_win_cols_t(gm["w_in_pad_t"], q * W_IN_SHARD, (q + 1) * W_IN_SHARD),
                                   ((0, W_IN_SHARD_PAD - W_IN_SHARD), (0, 0))) for q in range(N_SHARD)])
    g["mla_w_uq"] = _unpad_heads_cols(gm["w_uq_pad"], MLA_NOPE + MLA_ROPE)
    g["mla_w_ukv"] = _wkv_from_pad(gm["w_kv_pad"])
    g["gdn_conv_w"] = gm["conv_w"]
    g["w_out"] = _wout_from_pad(gm["w_out_pad"])
    if late:
        quarters = [_pack([jnp.split(g[n], N_SHARD, axis=SHARD_AXIS[n])[q] for n in MIX_BIG], MM_DTYPE)
                    for q in range(N_SHARD)]
        pairs_m = pair_sums([g["w_in"].astype(MM_DTYPE), jnp.stack(quarters)], "_mix")
        pairs_d, pairs_gu = [], []

        def make_mid(dwd):
            pairs_d.extend(pair_sums([dwd], "_ffn1d"))
            return _carried_scatter(pairs_d)

        def make_up(dwg, dwu):
            pairs_gu.extend(pair_sums([dwg, dwu], "_ffn1"))
            return _carried_scatter(pairs_gu)

        dx0, g["ffn1_pre_g"], _, _, _, g["ffn1_post_g"], slabs_m, slabs_d, slabs_gu = _ffn_bwd(
            "ffn1", dx1, sv1, *ffn("ffn1"), tm, tk, _carried_scatter(pairs_m), make_mid, make_up)
        g["ffn1_w_gate"], g["ffn1_w_up"] = chip_sums(pairs_gu, slabs_gu, "_ffn1")
        g["ffn1_w_down"] = chip_sums(pairs_d, slabs_d, "_ffn1d")[0]
        g["w_in"], g["mix_pack"] = chip_sums(pairs_m, slabs_m, "_mix")
    else:
        dx0, g["ffn1_pre_g"], g["ffn1_w_gate"], g["ffn1_w_up"], g["ffn1_w_down"], g["ffn1_post_g"] = _ffn_bwd(
            "ffn1", dx1, sv1, *ffn("ffn1"), tm, tk)[:6]
    g["mix_pre_g"], g["mix_post_g"] = gm["mix_pre_g"], gm["mix_post_g"]
    g["mla_q_norm_g"], g["mla_kv_norm_g"] = gm["mla_q_norm_g"], gm["mla_kv_norm_g"]
    g["gdn_norm_g"] = gm["gdn_norm_g"]
    g["mla_out_g"] = _unpad_heads_cols(gm["mla_out_g_pad"], MLA_V)
    g["gdn_a_log"] = gm["a_log_pad"][:, :N_HEADS]
    g["gdn_dt_bias"] = gm["dt_bias_pad"][:, :N_HEADS]
    return lsum, dx0, g


HBM_SPEC = pl.BlockSpec(memory_space=pltpu.HBM)


def _place():
    return lax.axis_index("x"), lax.axis_index("y"), lax.axis_index("c")


def _exchange_call(name, body, ins, out_shapes, n_remote, n_local):
    return pl.pallas_call(
        body, name=name, in_specs=[HBM_SPEC] * len(ins), out_specs=[HBM_SPEC] * len(out_shapes), out_shape=out_shapes,
        scratch_shapes=[pltpu.SemaphoreType.DMA((n_remote,)), pltpu.SemaphoreType.DMA((n_remote,)),
                        pltpu.SemaphoreType.DMA((n_local,))])(*ins)


def _other_chips(x, y):
    return [(1 - x, y), (x, 1 - y), (1 - x, 1 - y)]


def _at_each_chip(fn):
    x, y, _ = _place()
    for cx in range(2):
        for cy in range(2):
            pl.when((x == cx) & (y == cy))(functools.partial(fn, cx, cy))


def _at_each_device(fn):
    x, y, c = _place()
    for cx in range(2):
        for cy in range(2):
            for cc in range(2):
                pl.when((x == cx) & (y == cy) & (c == cc))(functools.partial(fn, cx, cy, cc))


def _at_each_core(fn):
    c = lax.axis_index("c")
    for cc in range(2):
        pl.when(c == cc)(functools.partial(fn, cc))


def _gather_shards(ws):
    nw = len(ws)

    def body(*refs):
        w_refs, out_refs = refs[:nw], refs[nw:2 * nw]
        send_sems, recv_sems, local_sems = refs[2 * nw:]

        def run(x, y, c):
            chips = _other_chips(x, y)
            me, sibling = 2 * x + y, (x, y, 1 - c)

            def half(ref, which):
                hr = ref.shape[0] // 2
                return ref.at[pl.ds(which * hr, hr)]

            def over_ici(i, j, src, slab, to):
                return pltpu.make_async_remote_copy(
                    src_ref=half(src, c), dst_ref=half(out_refs[i].at[slab], c), send_sem=send_sems.at[7 * i + j],
                    recv_sem=recv_sems.at[7 * i + j], device_id=to, device_id_type=MESH)

            def over_d2d(i, j, slab, which):
                return pltpu.make_async_remote_copy(
                    src_ref=half(out_refs[i].at[slab], which), dst_ref=half(out_refs[i].at[slab], which),
                    send_sem=send_sems.at[7 * i + 3 + j], recv_sem=recv_sems.at[7 * i + 3 + j], device_id=sibling,
                    device_id_type=MESH)

            def own(i, w_ref):
                return pltpu.make_async_remote_copy(
                    src_ref=w_ref, dst_ref=out_refs[i].at[me], send_sem=send_sems.at[7 * i + 6],
                    recv_sem=recv_sems.at[7 * i + 6], device_id=sibling, device_id_type=MESH)

            sends, passed = [], []
            for i, w_ref in enumerate(w_refs):
                for j, (px, py) in enumerate(chips):
                    sends.append(over_ici(i, j, w_ref, me, (px, py, c)))
                    sends[-1].start()
            for i, w_ref in enumerate(w_refs):
                sends.append(own(i, w_ref))
                sends[-1].start()
            for i, w_ref in enumerate(w_refs):
                for j, (px, py) in enumerate(chips):
                    over_ici(i, j, w_ref, 2 * px + py, (px, py, c)).wait_recv()
                    passed.append(over_d2d(i, j, 2 * px + py, c))
                    passed[-1].start()
            for i, w_ref in enumerate(w_refs):
                own(i, w_ref).wait_recv()
                for j, (px, py) in enumerate(chips):
                    over_d2d(i, j, 2 * px + py, 1 - c).wait_recv()
            for cp in sends + passed:
                cp.wait_send()

        _at_each_device(run)

    outs = [jax.ShapeDtypeStruct((N_SHARD,) + w.shape, w.dtype) for w in ws]
    return _exchange_call("gather_weight_shards", body, ws, outs, 7 * nw, 1)


def _swap_halves(gs, tag=""):
    ng = len(gs)

    def body(*refs):
        g_refs, got_refs = refs[:ng], refs[ng:2 * ng]
        send_sems, recv_sems, _ = refs[2 * ng:]
        x, y, _ = _place()

        def run(c):
            sends = []
            for i, (g_ref, got_ref) in enumerate(zip(g_refs, got_refs)):
                hr = got_ref.shape[1]
                sends.append(pltpu.make_async_remote_copy(
                    src_ref=g_ref.at[:, pl.ds((1 - c) * hr, hr)], dst_ref=got_ref, send_sem=send_sems.at[i],
                    recv_sem=recv_sems.at[i], device_id=(x, y, 1 - c), device_id_type=MESH))
                sends[-1].start()
            for cp in sends:
                cp.wait()

        _at_each_core(run)

    halves = [jax.ShapeDtypeStruct((g.shape[0], g.shape[1] // 2, g.shape[2]), g.dtype) for g in gs]
    return _exchange_call("swap_grad_halves" + tag, body, gs, halves, ng, 1)


def _scatter_copies(p_refs, out_refs, send_sems, recv_sems, x, y):
    c = lax.axis_index("c")
    copies = []
    for i, (p_ref, out_ref) in enumerate(zip(p_refs, out_refs)):
        for j, (px, py) in enumerate(_other_chips(x, y)):
            copies.append(pltpu.make_async_remote_copy(
                src_ref=p_ref.at[2 * px + py], dst_ref=out_ref.at[j], send_sem=send_sems.at[3 * i + j],
                recv_sem=recv_sems.at[3 * i + j], device_id=(px, py, c), device_id_type=MESH))
    return copies


def _start_all(make, *refs):
    def run(x, y):
        for cp in make(*refs, x, y):
            cp.start()

    _at_each_chip(run)


def _wait_all(make, *refs):
    def run(x, y):
        copies = make(*refs, x, y)
        for cp in copies:
            cp.wait_recv()
        for cp in copies:
            cp.wait_send()

    _at_each_chip(run)


def _scatter_shapes(ps):
    return [jax.ShapeDtypeStruct((3,) + p.shape[1:], p.dtype) for p in ps]


def _carried_scatter(ps):
    return _Carried(ps, _scatter_shapes(ps), 3 * len(ps), functools.partial(_start_all, _scatter_copies),
                    functools.partial(_wait_all, _scatter_copies))


def _direct_gather_copies(w_refs, out_refs, send_sems, recv_sems, x, y, arriving):
    c = lax.axis_index("c")
    me = 2 * x + y
    peers = [((px, py, c), 2 * px + py) for px, py in _other_chips(x, y)] + [((x, y, 1 - c), me)]
    copies = []
    for i, (w_ref, out_ref) in enumerate(zip(w_refs, out_refs)):
        for j, (peer, slab) in enumerate(peers):
            copies.append(pltpu.make_async_remote_copy(
                src_ref=w_ref, dst_ref=out_ref.at[slab if arriving else me], send_sem=send_sems.at[4 * i + j],
                recv_sem=recv_sems.at[4 * i + j], device_id=peer, device_id_type=MESH))
    return copies


def _carried_gather(ws):
    def start(w_refs, out_refs, send_sems, recv_sems):
        def run(x, y):
            for cp in _direct_gather_copies(w_refs, out_refs, send_sems, recv_sems, x, y, False):
                cp.start()

        _at_each_chip(run)

    def finish(w_refs, out_refs, send_sems, recv_sems):
        def run(x, y):
            for cp in _direct_gather_copies(w_refs, out_refs, send_sems, recv_sems, x, y, True):
                cp.wait_recv()
            for cp in _direct_gather_copies(w_refs, out_refs, send_sems, recv_sems, x, y, False):
                cp.wait_send()

        _at_each_chip(run)

    outs = [jax.ShapeDtypeStruct((N_SHARD,) + w.shape, w.dtype) for w in ws]
    return _Carried(ws, outs, 4 * len(ws), start, finish)


def _share_halves(hs):
    n = len(hs)

    def body(*refs):
        h_refs, out_refs = refs[:n], refs[n:2 * n]
        send_sems, recv_sems, _ = refs[2 * n:]
        x, y, c = _place()
        sends = []
        for i, (h_ref, out_ref) in enumerate(zip(h_refs, out_refs)):
            sends.append(pltpu.make_async_remote_copy(
                src_ref=h_ref, dst_ref=out_ref, send_sem=send_sems.at[i], recv_sem=recv_sems.at[i],
                device_id=(x, y, 1 - c), device_id_type=MESH))
            sends[-1].start()
        for cp in sends:
            cp.wait()

    outs = [jax.ShapeDtypeStruct(h.shape, h.dtype) for h in hs]
    return _exchange_call("share_grad_halves", body, hs, outs, n, 1)


def _scalar_grid_call(name, body, scalars, grid, in_specs, out_specs, out_shape, args):
    grid_spec = pltpu.PrefetchScalarGridSpec(num_scalar_prefetch=len(scalars), grid=grid, in_specs=in_specs,
                                             out_specs=out_specs)
    return pl.pallas_call(body, name=name, grid_spec=grid_spec, out_shape=out_shape,
                          compiler_params=_params(("arbitrary",) * len(grid)))(*scalars, *args)


def _add_pair(name, g, got, core):
    ns_, hr, cols = got.shape
    th = _row_tile(hr, 512)
    nb = hr // th

    def body(core_ref, g_ref, got_ref, out_ref):
        out_ref[...] = (g_ref[...].astype(F32) + got_ref[...].astype(F32)).astype(out_ref.dtype)

    blk = pl.BlockSpec((1, th, cols), lambda q, j, core_ref: (q, j, 0))
    own = pl.BlockSpec((1, th, cols), lambda q, j, core_ref: (q, core_ref[0] * nb + j, 0))
    return _scalar_grid_call(name, body, [core], (ns_, nb), [own, blk], blk,
                             jax.ShapeDtypeStruct(got.shape, got.dtype), [g, got])


def _add_chips(name, pairs, slabs, chip):
    _, hr, cols = slabs.shape
    th = _row_tile(hr, 512)

    def body(chip_ref, own_ref, s0_ref, s1_ref, s2_ref, out_ref):
        total = own_ref[0].astype(F32) + s0_ref[0].astype(F32)
        out_ref[...] = (total + s1_ref[0].astype(F32)) + s2_ref[0].astype(F32)

    own = pl.BlockSpec((1, th, cols), lambda j, chip_ref: (chip_ref[0], j, 0))
    others = [pl.BlockSpec((1, th, cols), lambda j, chip_ref, k=k: (k, j, 0)) for k in range(3)]
    return _scalar_grid_call(name, body, [chip], (hr // th,), [own] + others,
                             pl.BlockSpec((th, cols), lambda j, chip_ref: (j, 0)),
                             jax.ShapeDtypeStruct((hr, cols), F32), [pairs, slabs, slabs, slabs])


def _join_halves(name, mine, other, core):
    hr, cols = mine.shape
    th = _row_tile(hr, 512)
    nb = hr // th

    def body(core_ref, mine_ref, other_ref, out_ref):
        is_mine = pl.program_id(0) == core_ref[0]

        @pl.when(is_mine)
        def _():
            out_ref[0] = mine_ref[...]

        @pl.when(jnp.logical_not(is_mine))
        def _():
            out_ref[0] = other_ref[...]

    blk = pl.BlockSpec((th, cols), lambda h, j, core_ref: (j, 0))
    return _scalar_grid_call(name, body, [core], (2, nb), [blk, blk],
                             pl.BlockSpec((1, th, cols), lambda h, j, core_ref: (0, h * nb + j, 0)),
                             jax.ShapeDtypeStruct((1, 2 * hr, cols), mine.dtype), [mine, other])


def _gather_small(sp):
    def body(s_ref, out_ref, send_sems, recv_sems, local_sem):
        x, y, c = _place()
        me = 4 * x + 2 * y + c
        peers = [(x ^ (m >> 2), y ^ ((m >> 1) & 1), c ^ (m & 1)) for m in range(1, 8)]
        mine = pltpu.make_async_copy(s_ref, out_ref.at[me], local_sem)
        mine.start()
        sends = [pltpu.make_async_remote_copy(src_ref=s_ref, dst_ref=out_ref.at[me], send_sem=send_sems.at[j],
                                              recv_sem=recv_sems.at[j], device_id=p, device_id_type=MESH)
                 for j, p in enumerate(peers)]
        for cp in sends:
            cp.start()
        for j, (px, py, pc) in enumerate(peers):
            pltpu.make_async_remote_copy(src_ref=s_ref, dst_ref=out_ref.at[4 * px + 2 * py + pc],
                                         send_sem=send_sems.at[j], recv_sem=recv_sems.at[j], device_id=(px, py, pc),
                                         device_id_type=MESH).wait_recv()
        for cp in sends:
            cp.wait_send()
        mine.wait()

    return pl.pallas_call(
        body, name="gather_small_grads", in_specs=[HBM_SPEC], out_specs=HBM_SPEC,
        out_shape=jax.ShapeDtypeStruct((8,) + sp.shape, sp.dtype),
        scratch_shapes=[pltpu.SemaphoreType.DMA((7,)), pltpu.SemaphoreType.DMA((7,)), pltpu.SemaphoreType.DMA])(sp)


def _pack_rows(total):
    rows = -(-total // LANES)
    return -(-rows // 32) * 32


def _pack(arrs, dtype):
    flat = jnp.concatenate([a.reshape(-1).astype(dtype) for a in arrs])
    rows = _pack_rows(flat.shape[0])
    return jnp.pad(flat, (0, rows * LANES - flat.shape[0])).reshape(rows, LANES)


def _unpack(buf, shapes):
    flat = buf.reshape(-1)
    out, off = {}, 0
    for n, shp in shapes:
        size = shp[0] * shp[1]
        out[n] = flat[off:off + size].reshape(shp)
        off += size
    return out


def _to_wire(name, w3):
    _, r, cols = w3.shape
    tb = _row_tile(r, 512)

    def body(w_ref, o_ref):
        o_ref[...] = w_ref[0].astype(o_ref.dtype)

    return pl.pallas_call(
        body, name=name, grid=(r // tb,), in_specs=[pl.BlockSpec((1, tb, cols), lambda i: (0, i, 0))],
        out_specs=pl.BlockSpec((tb, cols), lambda i: (i, 0)), out_shape=jax.ShapeDtypeStruct((r, cols), MM_DTYPE),
        compiler_params=_params(("arbitrary",)))(w3)


def _adamw(name, w3, g, m3, v3, tb):
    c1 = 1.0 - ADAM_B1 ** ADAM_STEP
    c2 = 1.0 - ADAM_B2 ** ADAM_STEP
    _, r, cols = w3.shape
    emit = g.ndim == 2
    blk3 = pl.BlockSpec((1, tb, cols), lambda i: (0, i, 0))
    g_spec = pl.BlockSpec((tb, cols), lambda i: (i, 0)) if emit else blk3

    def body(w_ref, g_ref, m_ref, v_ref, *out_refs):
        gb = g_ref[...] if emit else g_ref[0]
        m2 = ADAM_B1 * m_ref[0] + (1.0 - ADAM_B1) * gb
        v2 = ADAM_B2 * v_ref[0] + (1.0 - ADAM_B2) * (gb * gb)
        out_refs[-3][0] = -ADAM_LR * ((m2 / c1) / (jnp.sqrt(v2 / c2) + ADAM_EPS) + ADAM_WD * w_ref[0])
        out_refs[-2][0] = m2
        out_refs[-1][0] = v2
        if emit:
            out_refs[0][0] = gb

    n_out = 4 if emit else 3
    outs = pl.pallas_call(
        body, name=name, grid=(r // tb,), in_specs=[blk3, g_spec, blk3, blk3], out_specs=[blk3] * n_out,
        out_shape=[jax.ShapeDtypeStruct((1, r, cols), F32)] * n_out,
        compiler_params=_params(("arbitrary",)))(w3, g, m3, v3)
    return outs if emit else [g] + list(outs)


def _row_tile(rows, pref):
    if rows <= pref:
        return rows
    t = pref
    while t >= 8:
        if rows % t == 0 and t % 8 == 0:
            return t
        t -= 8
    return rows


def kernel(x, positions, ffn1_pre_g, ffn1_w_gate, ffn1_w_up, ffn1_w_down, ffn1_post_g, mix_pre_g, w_in, mla_q_norm_g, mla_w_uq, mla_kv_norm_g, mla_w_ukv, mla_out_g, gdn_conv_w, gdn_a_log, gdn_dt_bias, gdn_norm_g, w_out, mix_post_g, ffn2_pre_g, ffn2_w_gate, ffn2_w_up, ffn2_w_down, ffn2_post_g, loss_target, m_ffn1_pre_g, m_ffn1_w_gate, m_ffn1_w_up, m_ffn1_w_down, m_ffn1_post_g, m_mix_pre_g, m_w_in, m_mla_q_norm_g, m_mla_w_uq, m_mla_kv_norm_g, m_mla_w_ukv, m_mla_out_g, m_gdn_conv_w, m_gdn_a_log, m_gdn_dt_bias, m_gdn_norm_g, m_w_out, m_mix_post_g, m_ffn2_pre_g, m_ffn2_w_gate, m_ffn2_w_up, m_ffn2_w_down, m_ffn2_post_g, v_ffn1_pre_g, v_ffn1_w_gate, v_ffn1_w_up, v_ffn1_w_down, v_ffn1_post_g, v_mix_pre_g, v_w_in, v_mla_q_norm_g, v_mla_w_uq, v_mla_kv_norm_g, v_mla_w_ukv, v_mla_out_g, v_gdn_conv_w, v_gdn_a_log, v_gdn_dt_bias, v_gdn_norm_g, v_w_out, v_mix_post_g, v_ffn2_pre_g, v_ffn2_w_gate, v_ffn2_w_up, v_ffn2_w_down, v_ffn2_post_g):
    args = dict(locals())
    wsh = {n: args[n][0] for n in WEIGHTS}
    msh = {n: args["m_" + n] for n in SMALL}
    vsh = {n: args["v_" + n] for n in SMALL}
    for n in SMALL:
        wsh[n] = args[n]
    mix_shapes = [(n, wsh[n].shape) for n in MIX_BIG]

    early = FFN_BIG[:3]
    held = lambda a, n: jnp.swapaxes(a, 1, 2) if n in TRANSPOSED else a
    w_in_wire = jnp.pad(held(w_in, "w_in")[0].astype(MM_DTYPE), ((0, W_IN_SHARD_PAD - W_IN_SHARD), (0, 0)))
    gathered = _gather_shards([_to_wire("wire_" + n, held(args[n], n)) for n in early]
                              + [w_in_wire, _pack([wsh[n] for n in MIX_BIG], MM_DTYPE)])
    full = {n: wsh[n] for n in SMALL}
    for n, gw in zip(early + ["w_in"], gathered):
        full[n] = gw
    parts = [_unpack(gathered[-1][q], mix_shapes) for q in range(N_SHARD)]
    for n in MIX_BIG:
        full[n] = jnp.concatenate([parts[q][n] for q in range(N_SHARD)], axis=SHARD_AXIS[n])

    core = lax.axis_index("c").astype(jnp.int32).reshape(1)
    chip = (2 * lax.axis_index("x") + lax.axis_index("y")).astype(jnp.int32).reshape(1)
    late = ([_to_wire("wire_" + n, held(args[n], n)) for n in FFN2_BIG], core, chip)
    lsum, grad_x, g = _local_step(x[0], positions, loss_target[0], full, late)
    loss = lax.psum(0.5 * jnp.sum(lsum) / x.shape[-1], ("x", "y", "c"))

    halves = [g[n] for n in FFN_BIG] + [g["w_in"], g["mix_pack"]]
    others = _share_halves(halves)
    shared = [_join_halves("join_halves_%d" % i, hm, ho, core) for i, (hm, ho) in enumerate(zip(halves, others))]
    gsh = _unpack(shared[-1], mix_shapes)
    for n, sg_ in zip(FFN_BIG, shared):
        gsh[n] = sg_
    gsh["w_in"] = shared[-2][:, :W_IN_SHARD]

    small_shapes = [(n, wsh[n].shape) for n in SMALL]
    pack_small = lambda d: jnp.concatenate(
        [_pad_lanes(d[n].astype(F32), LANES) for n in SMALL] + [jnp.zeros((SMALL_ROWS - len(SMALL), LANES), F32)], axis=0)
    slots = _gather_small(pack_small(g))

    c1 = 1.0 - ADAM_B1 ** ADAM_STEP
    c2 = 1.0 - ADAM_B2 ** ADAM_STEP

    def small_update(wb, mb, vb, s8):
        gs = s8[0:SMALL_ROWS]
        for d in range(1, 8):
            gs = gs + s8[d * SMALL_ROWS:(d + 1) * SMALL_ROWS]
        m2 = ADAM_B1 * mb + (1.0 - ADAM_B1) * gs
        v2 = ADAM_B2 * vb + (1.0 - ADAM_B2) * (gs * gs)
        delta = -ADAM_LR * ((m2 / c1) / (jnp.sqrt(v2 / c2) + ADAM_EPS) + ADAM_WD * wb)
        return gs, delta, m2, v2

    sg, sd, sm, sv_ = _rowwise("adamw_small", small_update,
                               [pack_small(wsh), pack_small(msh), pack_small(vsh)],
                               [slots.reshape(8 * SMALL_ROWS, LANES)], [(LANES, F32)] * 4, [], SMALL_ROWS)
    grads, deltas, new_m, new_v = {}, {}, {}, {}
    for i, (n, shp) in enumerate(small_shapes):
        grads[n], deltas[n] = sg[i:i + 1, :shp[1]], sd[i:i + 1, :shp[1]]
        new_m[n], new_v[n] = sm[i:i + 1, :shp[1]], sv_[i:i + 1, :shp[1]]
    for n in BIG:
        w3 = held(args[n], n)
        outs = _adamw("adamw_" + n, w3, gsh[n], held(args["m_" + n], n), held(args["v_" + n], n),
                      _row_tile(w3.shape[1], 256))
        grads[n], deltas[n], new_m[n], new_v[n] = [held(o, n) for o in outs]

    return (loss, grad_x[None], *[grads[n] for n in WEIGHTS], *[deltas[n] for n in WEIGHTS],
            *[new_m[n] for n in WEIGHTS], *[new_v[n] for n in WEIGHTS])
```

```python
import functools

import jax
import jax.numpy as jnp
from jax import lax
from jax.experimental import pallas as pl
from jax.experimental.pallas import tpu as pltpu

F32 = jnp.float32
BF16 = jnp.bfloat16
MM_DTYPE = BF16
MESH = pl.DeviceIdType.MESH

D_MODEL = 1024
D_FF = 2816
N_HEADS = 8
MLA_Q_RANK = 256
MLA_KV_RANK = 128
MLA_NOPE = 64
MLA_ROPE = 32
MLA_V = 64
ROPE_THETA = 10000.0
GDN_DH = 64
GDN_W = N_HEADS * GDN_DH
GDN_CONV = 4
CHUNK = 64
GDN_STEP_CHUNKS = 4
HEAD_LANES = 128
HEADS_PER_STEP = 4
MLA_PAD = N_HEADS * HEAD_LANES
EPS = 1e-6
N_SHARD = 4
LANES = 1024

PIN_QKV = 0
PIN_MLA = 1536
PIN_KPE = 1920
PIN_GATE = 2048
PIN_AB = 2560
PIN_W = 2688
CAT_W = MLA_PAD + GDN_W

ADAM_LR = 0.001
ADAM_B1 = 0.9
ADAM_B2 = 0.999
ADAM_EPS = 1e-08
ADAM_WD = 0.01
ADAM_STEP = 10

VMEM_LIMIT_V7X = 56 * 1024 * 1024

BIG = ["ffn1_w_gate", "ffn1_w_up", "ffn1_w_down", "w_in", "mla_w_uq", "mla_w_ukv", "gdn_conv_w", "w_out",
       "ffn2_w_gate", "ffn2_w_up", "ffn2_w_down"]
FFN_BIG = ["ffn1_w_gate", "ffn1_w_up", "ffn1_w_down", "ffn2_w_gate", "ffn2_w_up", "ffn2_w_down"]
TRANSPOSED = ["ffn1_w_gate", "ffn1_w_up", "ffn2_w_gate", "ffn2_w_up", "w_in"]
MIX_BIG = ["mla_w_uq", "mla_w_ukv", "gdn_conv_w", "w_out"]
SMALL = ["ffn1_pre_g", "ffn1_post_g", "mix_pre_g", "mla_q_norm_g", "mla_kv_norm_g", "mla_out_g", "gdn_a_log",
         "gdn_dt_bias", "gdn_norm_g", "mix_post_g", "ffn2_pre_g", "ffn2_post_g"]
WEIGHTS = ["ffn1_pre_g", "ffn1_w_gate", "ffn1_w_up", "ffn1_w_down", "ffn1_post_g", "mix_pre_g", "w_in",
           "mla_q_norm_g", "mla_w_uq", "mla_kv_norm_g", "mla_w_ukv", "mla_out_g", "gdn_conv_w", "gdn_a_log",
           "gdn_dt_bias", "gdn_norm_g", "w_out", "mix_post_g", "ffn2_pre_g", "ffn2_w_gate", "ffn2_w_up",
           "ffn2_w_down", "ffn2_post_g"]
SHARD_AXIS = {"ffn1_w_gate": 1, "ffn1_w_up": 1, "ffn1_w_down": 0, "w_in": 1, "mla_w_uq": 1, "mla_w_ukv": 1,
              "gdn_conv_w": 1, "w_out": 0, "ffn2_w_gate": 1, "ffn2_w_up": 1, "ffn2_w_down": 0}
SMALL_ROWS = 16


def _params(sem):
    return pltpu.CompilerParams(dimension_semantics=sem, vmem_limit_bytes=VMEM_LIMIT_V7X)


def _pick(dim, pref):
    if dim <= pref:
        return dim
    t = (pref // 128) * 128
    while t >= 128:
        if dim % t == 0:
            return t
        t -= 128
    return dim


ANY_SPEC = pl.BlockSpec(memory_space=pl.ANY)


def _rowwise(name, fn, row_ins, bc_ins, row_outs, acc_outs, tb, wide=None, carry=None):
    ents = []
    for e in row_ins:
        ents.append(e if isinstance(e, tuple) else (e, e.shape[1], 0, 0))
    over = [o[2] for o in row_outs if len(o) == 3]
    rows = over[0] if over else ents[0][0].shape[0]
    steps = rows // tb
    assert steps * tb == rows, (name, rows, tb)
    in_specs, args = [], []
    for a, w, j, r0 in ents:
        in_specs.append(pl.BlockSpec((tb, w), lambda i, j=j, r0=r0: (i + r0, j)))
        args.append(a)
    for b in bc_ins:
        in_specs.append(pl.BlockSpec(b.shape, lambda i: (0, 0)))
        args.append(b)
    n_in = len(args)
    aliases = {}
    if carry is not None:
        in_specs.append(ANY_SPEC)
        args.append(carry)
        aliases = {n_in: 0}
    out_shape = [jax.ShapeDtypeStruct((rows, o[0]), o[1]) for o in row_outs]
    out_specs = [pl.BlockSpec((tb, o[0]), lambda i: (i, 0)) for o in row_outs]
    if wide is not None:
        out_shape[0] = jax.ShapeDtypeStruct((rows, wide[0]), row_outs[0][1])
        out_specs[0] = pl.BlockSpec((tb, row_outs[0][0]), lambda i: (i, wide[1]))
    out_shape += [jax.ShapeDtypeStruct((r, c), F32) for r, c in acc_outs]
    out_specs += [pl.BlockSpec((r, c), lambda i: (0, 0)) for r, c in acc_outs]
    n_ro, n_acc, n_args = len(row_outs), len(acc_outs), len(args)

    def body(*refs):
        vals = fn(*[r[...] for r in refs[:n_in]])
        if not isinstance(vals, (tuple, list)):
            vals = (vals,)
        for r, v in zip(refs[n_args:n_args + n_ro], vals[:n_ro]):
            r[...] = v.astype(r.dtype)
        if n_acc:
            acc_refs = refs[n_args + n_ro:]

            @pl.when(pl.program_id(0) == 0)
            def _():
                for r in acc_refs:
                    r[...] = jnp.zeros(r.shape, r.dtype)

            for r, v in zip(acc_refs, vals[n_ro:]):
                r[...] += v

    outs = pl.pallas_call(body, name=name, grid=(steps,), in_specs=in_specs, out_specs=out_specs,
                          out_shape=out_shape, input_output_aliases=aliases,
                          compiler_params=_params(("arbitrary",)))(*args)
    return outs


def _mm(name, a, b, mode, out_dtype, tm=1024, tn=1024, tk=1024):
    if mode == "nn":
        (m, k), (k2, n) = a.shape, b.shape
    elif mode == "nt":
        (m, k), (n, k2) = a.shape, b.shape
    else:
        (k, m), (k2, n) = a.shape, b.shape
    assert k == k2, (name, a.shape, b.shape)
    tm, tn, tk = _pick(m, tm), _pick(n, tn), _pick(k, tk)
    nk = k // tk
    if mode == "nn":
        a_spec = pl.BlockSpec((tm, tk), lambda i, j, kk: (i, kk))
        b_spec = pl.BlockSpec((tk, tn), lambda i, j, kk: (kk, j))
        dims = (((1,), (0,)), ((), ()))
    elif mode == "nt":
        a_spec = pl.BlockSpec((tm, tk), lambda i, j, kk: (i, kk))
        b_spec = pl.BlockSpec((tn, tk), lambda i, j, kk: (j, kk))
        dims = (((1,), (1,)), ((), ()))
    else:
        a_spec = pl.BlockSpec((tk, tm), lambda i, j, kk: (kk, i))
        b_spec = pl.BlockSpec((tk, tn), lambda i, j, kk: (kk, j))
        dims = (((0,), (0,)), ((), ()))

    def body(a_ref, b_ref, o_ref, acc_ref):
        kk = pl.program_id(2)

        @pl.when(kk == 0)
        def _():
            acc_ref[...] = jnp.zeros(acc_ref.shape, F32)

        acc_ref[...] += lax.dot_general(a_ref[...].astype(MM_DTYPE), b_ref[...].astype(MM_DTYPE), dims,
                                        preferred_element_type=F32)

        @pl.when(kk == nk - 1)
        def _():
            o_ref[...] = acc_ref[...].astype(o_ref.dtype)

    return pl.pallas_call(
        body, name=name, grid=(m // tm, n // tn, nk), in_specs=[a_spec, b_spec],
        out_specs=pl.BlockSpec((tm, tn), lambda i, j, kk: (i, j)),
        out_shape=jax.ShapeDtypeStruct((m, n), out_dtype),
        scratch_shapes=[pltpu.VMEM((tm, tn), F32)],
        compiler_params=_params(("parallel", "parallel", "arbitrary")))(a, b)


def _prologue_mm(name, row_ins, g, prologue, w, with_gain_grad, tm=1024, tn=1024):
    t, d = row_ins[0].shape
    n = w.shape[0]
    tm, tn = _pick(t, tm), _pick(n, tn)
    n_row = len(row_ins)
    row = pl.BlockSpec((tm, d), lambda i, j: (i, 0))
    vec = pl.BlockSpec((1, d), lambda i, j: (0, 0))

    def body(*refs):
        rows, g_ref, w_ref = refs[:n_row], refs[n_row], refs[n_row + 1]
        lhs_ref, out_ref = refs[n_row + 2], refs[n_row + 3]
        lhs_s = refs[-1]
        i, j = pl.program_id(0), pl.program_id(1)
        if with_gain_grad:
            dg_ref = refs[n_row + 4]

            @pl.when((i == 0) & (j == 0))
            def _():
                dg_ref[...] = jnp.zeros(dg_ref.shape, F32)

        @pl.when(j == 0)
        def _():
            res = prologue(*[r[...] for r in rows], g_ref[...])
            lhs_s[...] = res[0].astype(MM_DTYPE)
            lhs_ref[...] = lhs_s[...]
            if with_gain_grad:
                dg_ref[...] += res[1]

        out_ref[...] = lax.dot_general(lhs_s[...], w_ref[...], (((1,), (1,)), ((), ())), preferred_element_type=F32)

    out_specs = [row, pl.BlockSpec((tm, tn), lambda i, j: (i, j))] + ([vec] if with_gain_grad else [])
    out_shape = [jax.ShapeDtypeStruct((t, d), MM_DTYPE), jax.ShapeDtypeStruct((t, n), F32)]
    out_shape += [jax.ShapeDtypeStruct((1, d), F32)] if with_gain_grad else []
    return pl.pallas_call(
        body, name=name, grid=(t // tm, n // tn),
        in_specs=[row] * n_row + [vec, pl.BlockSpec((tn, d), lambda i, j: (j, 0))], out_specs=out_specs,
        out_shape=out_shape, scratch_shapes=[pltpu.VMEM((tm, d), MM_DTYPE)],
        compiler_params=_params(("arbitrary", "arbitrary")))(*row_ins, g, w)


def _mm_epilogue(name, a, w, row_ins, g, epilogue, n_row_out, with_gain_grad, tm=1024, tk=1024):
    t, k = a.shape
    d = w.shape[1]
    tm, tk = _pick(t, tm), _pick(k, tk)
    nk = k // tk
    n_row = len(row_ins)
    row = pl.BlockSpec((tm, d), lambda i, kk: (i, 0))
    vec = pl.BlockSpec((1, d), lambda i, kk: (0, 0))

    def body(*refs):
        a_ref, w_ref = refs[0], refs[1]
        rows, g_ref = refs[2:2 + n_row], refs[2 + n_row]
        outs = refs[3 + n_row:3 + n_row + n_row_out]
        acc = refs[-1]
        i, kk = pl.program_id(0), pl.program_id(1)
        if with_gain_grad:
            dg_ref = refs[3 + n_row + n_row_out]

            @pl.when((i == 0) & (kk == 0))
            def _():
                dg_ref[...] = jnp.zeros(dg_ref.shape, F32)

        @pl.when(kk == 0)
        def _():
            acc[...] = jnp.zeros(acc.shape, F32)

        acc[...] += jnp.dot(a_ref[...], w_ref[...], preferred_element_type=F32)

        @pl.when(kk == nk - 1)
        def _():
            res = epilogue(acc[...], *[r[...] for r in rows], g_ref[...])
            for o_ref, val in zip(outs, res[:n_row_out]):
                o_ref[...] = val
            if with_gain_grad:
                dg_ref[...] += res[n_row_out]

    out_specs = [row] * n_row_out + ([vec] if with_gain_grad else [])
    out_shape = [jax.ShapeDtypeStruct((t, d), F32)] * n_row_out
    out_shape += [jax.ShapeDtypeStruct((1, d), F32)] if with_gain_grad else []
    return pl.pallas_call(
        body, name=name, grid=(t // tm, nk),
        in_specs=[pl.BlockSpec((tm, tk), lambda i, kk: (i, kk)), pl.BlockSpec((tk, d), lambda i, kk: (kk, 0))]
        + [row] * n_row + [vec],
        out_specs=out_specs, out_shape=out_shape, scratch_shapes=[pltpu.VMEM((tm, d), F32)],
        compiler_params=_params(("arbitrary", "arbitrary")))(a, w, *row_ins, g)


def _rms_stats(x, n_real=None):
    n = x.shape[-1] if n_real is None else n_real
    return lax.rsqrt(jnp.sum(x * x, axis=-1, keepdims=True) / n + EPS)


def _rms_bwd(x, r, g, dz, n_real=None):
    n = x.shape[-1] if n_real is None else n_real
    xh = x * r
    dxh = dz * g
    dx = r * (dxh - xh * (jnp.sum(dxh * xh, axis=-1, keepdims=True) / n))
    return dx, jnp.sum(dz * xh, axis=0, keepdims=True)


def _sigmoid(x):
    return 0.5 * jnp.tanh(0.5 * x) + 0.5


def _roll(x, s, axis):
    return pltpu.roll(x, s, axis)


def _rope(x, c, s1, s2):
    return x * c + _roll(x, HEAD_LANES - MLA_ROPE // 2, 1) * s1 + _roll(x, MLA_ROPE // 2, 1) * s2


def _heads_apply(x, fn):
    return jnp.concatenate([fn(x[:, h * HEAD_LANES:(h + 1) * HEAD_LANES]) for h in range(N_HEADS)], axis=1)


ROW_CHUNK = 256


def _row_chunks(rows):
    step = min(ROW_CHUNK, rows)
    return [pl.ds(r, step) for r in range(0, rows, step)]


def _ffn_fwd(tag, x, g_pre, wg, wu, wd, g_post, tm):
    t, d = x.shape
    ns, fs, _ = wg.shape
    nt = t // tm
    row = pl.BlockSpec((tm, d), lambda i, q: (i, 0))
    vec = pl.BlockSpec((1, d), lambda i, q: (0, 0))
    act3 = pl.BlockSpec((1, tm, fs), lambda i, q: (q, i, 0))
    wrow = pl.BlockSpec((1, fs, d), lambda i, q: (q, 0, 0))
    nt_dims = (((1,), (1,)), ((), ()))

    def gate_up(x_ref, g_ref, wg_ref, wu_ref, n_ref, sl_ref, ud_ref, s_ref, n_s):
        @pl.when(pl.program_id(1) == 0)
        def _():
            for r in _row_chunks(tm):
                xb = x_ref[r, :]
                n_s[r, :] = (xb * _rms_stats(xb) * g_ref[...]).astype(MM_DTYPE)
            n_ref[...] = n_s[...]

        for r in _row_chunks(tm):
            n = n_s[r, :]
            a = lax.dot_general(n, wg_ref[0], nt_dims, preferred_element_type=F32)
            u = lax.dot_general(n, wu_ref[0], nt_dims, preferred_element_type=F32)
            sg = _sigmoid(a)
            sl = a * sg
            sl_ref[0, r, :] = sl.astype(sl_ref.dtype)
            ud_ref[0, r, :] = (u * (sg + sl * (1.0 - sg))).astype(ud_ref.dtype)
            s_ref[0, r, :] = (sl * u).astype(s_ref.dtype)

    n, sl, ud, s = pl.pallas_call(
        gate_up, name=tag + "_gate_up", grid=(nt, ns), in_specs=[row, vec, wrow, wrow],
        out_specs=[row, act3, act3, act3],
        out_shape=[jax.ShapeDtypeStruct((t, d), MM_DTYPE)] + [jax.ShapeDtypeStruct((ns, t, fs), MM_DTYPE)] * 3,
        scratch_shapes=[pltpu.VMEM((tm, d), MM_DTYPE)],
        compiler_params=_params(("parallel", "arbitrary")))(x, g_pre, wg, wu)

    def down(s_ref, wd_ref, x_ref, g_ref, h_ref, y_ref, acc):
        q = pl.program_id(1)

        @pl.when(q == 0)
        def _():
            acc[...] = jnp.zeros(acc.shape, F32)

        for r in _row_chunks(tm):
            acc[r, :] += jnp.dot(s_ref[0, r, :], wd_ref[0], preferred_element_type=F32)

        @pl.when(q == ns - 1)
        def _():
            for r in _row_chunks(tm):
                hb = acc[r, :]
                h_ref[r, :] = hb
                y_ref[r, :] = x_ref[r, :] + 0.5 * (hb * _rms_stats(hb) * g_ref[...])

    h, y = pl.pallas_call(
        down, name=tag + "_down", grid=(nt, ns), in_specs=[act3, wrow, row, vec], out_specs=[row, row],
        out_shape=[jax.ShapeDtypeStruct((t, d), F32)] * 2, scratch_shapes=[pltpu.VMEM((tm, d), F32)],
        compiler_params=_params(("parallel", "arbitrary")))(s, wd, x, g_post)
    return y, (x, n, sl, ud, s, h)


def _carry(body, n_in, n_out, grid, carried):
    if carried is None:
        return body, [], [], [], [], []
    nx_in, nx_out = len(carried.ins), len(carried.outs)

    def wrapped(*refs):
        ins, rest = refs[:n_in], refs[n_in:]
        xi, rest = rest[:nx_in], rest[nx_in:]
        outs, rest = rest[:n_out], rest[n_out:]
        xo, rest = rest[:nx_out], rest[nx_out:]
        scr, sems = rest[:len(rest) - 2], rest[len(rest) - 2:]
        first, last = True, True
        for dim, size in enumerate(grid):
            first = first & (pl.program_id(dim) == 0)
            last = last & (pl.program_id(dim) == size - 1)

        @pl.when(first)
        def _():
            carried.start(xi, xo, *sems)

        body(*ins, *outs, *scr)

        @pl.when(last)
        def _():
            carried.finish(xi, xo, *sems)

    sems = [pltpu.SemaphoreType.DMA((carried.n_sem,)), pltpu.SemaphoreType.DMA((carried.n_sem,))]
    return (wrapped, [HBM_SPEC] * nx_in, [HBM_SPEC] * nx_out, list(carried.outs), sems, list(carried.ins))


def _ffn_bwd(tag, dy, saved, g_pre, wg, wu, wd, g_post, tm, tk, carried_down=None, make_carried_mid=None,
             make_carried_up=None):
    x, n, sl, ud, s, h = saved
    t, d = x.shape
    ns, fs, _ = wg.shape
    nt, nk = t // tm, t // tk
    row = pl.BlockSpec((tm, d), lambda i, q: (i, 0))
    vec = pl.BlockSpec((1, d), lambda i, q: (0, 0))
    act3 = pl.BlockSpec((1, tm, fs), lambda i, q: (q, i, 0))
    wrow = pl.BlockSpec((1, fs, d), lambda i, q: (q, 0, 0))
    nt_dims = (((1,), (1,)), ((), ()))
    tn_dims = (((0,), (0,)), ((), ()))

    def down_b(h_ref, dy_ref, g_ref, wd_ref, sl_ref, ud_ref, dh_ref, da_ref, du_ref, dg_ref, dh_s):
        i, q = pl.program_id(0), pl.program_id(1)

        @pl.when((i == 0) & (q == 0))
        def _():
            dg_ref[...] = jnp.zeros(dg_ref.shape, F32)

        @pl.when(q == 0)
        def _():
            for r in _row_chunks(tm):
                hb = h_ref[r, :]
                dh, dg = _rms_bwd(hb, _rms_stats(hb), g_ref[...], 0.5 * dy_ref[r, :])
                dh_s[r, :] = dh.astype(MM_DTYPE)
                dg_ref[...] += dg
            dh_ref[...] = dh_s[...]

        for r in _row_chunks(tm):
            ds = lax.dot_general(dh_s[r, :], wd_ref[0], nt_dims, preferred_element_type=F32)
            da_ref[0, r, :] = (ds * ud_ref[0, r, :].astype(F32)).astype(da_ref.dtype)
            du_ref[0, r, :] = (ds * sl_ref[0, r, :].astype(F32)).astype(du_ref.dtype)

    down_b, x_in, x_out, x_shape, x_scr, x_args = _carry(down_b, 6, 4, (nt, ns), carried_down)
    dh, da, du, dg_post, *from_down = pl.pallas_call(
        down_b, name=tag + "_down_b", grid=(nt, ns), in_specs=[row, row, vec, wrow, act3, act3] + x_in,
        out_specs=[row, act3, act3, vec] + x_out,
        out_shape=[jax.ShapeDtypeStruct((t, d), MM_DTYPE)] + [jax.ShapeDtypeStruct((ns, t, fs), MM_DTYPE)] * 2
        + [jax.ShapeDtypeStruct((1, d), F32)] + x_shape,
        scratch_shapes=[pltpu.VMEM((tm, d), MM_DTYPE)] + x_scr,
        compiler_params=_params(("arbitrary", "arbitrary")))(h, dy, g_post, wd, sl, ud, *x_args)

    def down_w(s_ref, dh_ref, dw_ref, acc):
        kk = pl.program_id(1)

        @pl.when(kk == 0)
        def _():
            acc[...] = jnp.zeros(acc.shape, F32)

        acc[...] += lax.dot_general(s_ref[0], dh_ref[...], tn_dims, preferred_element_type=F32)

        @pl.when(kk == nk - 1)
        def _():
            dw_ref[0] = acc[...].astype(dw_ref.dtype)

    dwd = pl.pallas_call(
        down_w, name=tag + "_down_w", grid=(ns, nk),
        in_specs=[pl.BlockSpec((1, tk, fs), lambda q, kk: (q, kk, 0)), pl.BlockSpec((tk, d), lambda q, kk: (kk, 0))],
        out_specs=pl.BlockSpec((1, fs, d), lambda q, kk: (q, 0, 0)),
        out_shape=jax.ShapeDtypeStruct((ns, fs, d), MM_DTYPE), scratch_shapes=[pltpu.VMEM((fs, d), F32)],
        compiler_params=_params(("parallel", "arbitrary")))(s, dh)

    def gate_up_b(da_ref, du_ref, wg_ref, wu_ref, x_ref, dy_ref, g_ref, dx_ref, dg_ref, acc):
        i, q = pl.program_id(0), pl.program_id(1)

        @pl.when((i == 0) & (q == 0))
        def _():
            dg_ref[...] = jnp.zeros(dg_ref.shape, F32)

        @pl.when(q == 0)
        def _():
            acc[...] = jnp.zeros(acc.shape, F32)

        for r in _row_chunks(tm):
            acc[r, :] += (jnp.dot(da_ref[0, r, :], wg_ref[0], preferred_element_type=F32)
                          + jnp.dot(du_ref[0, r, :], wu_ref[0], preferred_element_type=F32))

        @pl.when(q == ns - 1)
        def _():
            for r in _row_chunks(tm):
                xb = x_ref[r, :]
                dx, dg = _rms_bwd(xb, _rms_stats(xb), g_ref[...], acc[r, :])
                dx_ref[r, :] = dy_ref[r, :] + dx
                dg_ref[...] += dg

    def gate_up_w(n_ref, da_ref, du_ref, dwg_ref, dwu_ref, acc_g, acc_u):
        kk = pl.program_id(1)

        @pl.when(kk == 0)
        def _():
            acc_g[...] = jnp.zeros(acc_g.shape, F32)
            acc_u[...] = jnp.zeros(acc_u.shape, F32)

        nb = n_ref[...]
        acc_g[...] += lax.dot_general(da_ref[0], nb, tn_dims, preferred_element_type=F32)
        acc_u[...] += lax.dot_general(du_ref[0], nb, tn_dims, preferred_element_type=F32)

        @pl.when(kk == nk - 1)
        def _():
            dwg_ref[0] = acc_g[...].astype(dwg_ref.dtype)
            dwu_ref[0] = acc_u[...].astype(dwu_ref.dtype)

    k3 = pl.BlockSpec((1, tk, fs), lambda q, kk: (q, kk, 0))
    wout = pl.BlockSpec((1, fs, d), lambda q, kk: (q, 0, 0))
    carried_mid = make_carried_mid(dwd) if make_carried_mid else None
    gate_up_w, x_in, x_out, x_shape, x_scr, x_args = _carry(gate_up_w, 3, 2, (ns, nk), carried_mid)
    dwg, dwu, *from_mid = pl.pallas_call(
        gate_up_w, name=tag + "_gate_up_w", grid=(ns, nk),
        in_specs=[pl.BlockSpec((tk, d), lambda q, kk: (kk, 0)), k3, k3] + x_in, out_specs=[wout, wout] + x_out,
        out_shape=[jax.ShapeDtypeStruct((ns, fs, d), MM_DTYPE)] * 2 + x_shape,
        scratch_shapes=[pltpu.VMEM((fs, d), F32)] * 2 + x_scr,
        compiler_params=_params(("arbitrary", "arbitrary")))(n, da, du, *x_args)

    carried_up = make_carried_up(dwg, dwu) if make_carried_up else None
    gate_up_b, x_in, x_out, x_shape, x_scr, x_args = _carry(gate_up_b, 7, 2, (nt, ns), carried_up)
    dx, dg_pre, *from_up = pl.pallas_call(
        gate_up_b, name=tag + "_gate_up_b", grid=(nt, ns), in_specs=[act3, act3, wrow, wrow, row, row, vec] + x_in,
        out_specs=[row, vec] + x_out,
        out_shape=[jax.ShapeDtypeStruct((t, d), F32), jax.ShapeDtypeStruct((1, d), F32)] + x_shape,
        scratch_shapes=[pltpu.VMEM((tm, d), F32)] + x_scr,
        compiler_params=_params(("arbitrary", "arbitrary")))(da, du, wg, wu, x, dy, g_pre, *x_args)
    return dx, dg_pre, dwg, dwu, dwd, dg_post, from_down, from_mid, from_up


NEG = -1e30


def _attn_scale():
    return (MLA_NOPE + MLA_ROPE) ** -0.5


def _causal_pairs(nq, by_key):
    if by_key:
        pairs = [(qi, ki) for ki in range(nq) for qi in range(ki, nq)]
    else:
        pairs = [(qi, ki) for qi in range(nq) for ki in range(qi + 1)]
    return jnp.asarray([p[0] for p in pairs], jnp.int32), jnp.asarray([p[1] for p in pairs], jnp.int32)


def _below_diagonal(shape):
    return lax.broadcasted_iota(jnp.int32, shape, 1) <= lax.broadcasted_iota(jnp.int32, shape, 0)


def _attn_call(name, body, tables, args, in_kinds, out_kinds, scratch, t, tq, carried=None):
    qmap = lambda h, p, qt, kt: (qt[p], h)
    kmap = lambda h, p, qt, kt: (kt[p], h)
    width = HEADS_PER_STEP * HEAD_LANES
    spec = lambda kind: pl.BlockSpec((tq, width), qmap if kind == "q" else kmap)
    n_pairs = tables[0].shape[0]
    n_groups = N_HEADS // HEADS_PER_STEP
    n_in, n_out, n_scr = len(in_kinds), len(out_kinds), scratch
    x_ins = list(carried.ins) if carried else []
    x_outs = list(carried.outs) if carried else []
    x_scr = [pltpu.SemaphoreType.DMA((carried.n_sem,)), pltpu.SemaphoreType.DMA((carried.n_sem,))] if carried else []

    def full_body(qt, kt, *refs):
        ins, refs = refs[:n_in], refs[n_in:]
        xi, refs = refs[:len(x_ins)], refs[len(x_ins):]
        outs, refs = refs[:n_out], refs[n_out:]
        xo, refs = refs[:len(x_outs)], refs[len(x_outs):]
        scr, sems = refs[:n_scr], refs[n_scr:]
        if carried:
            @pl.when((pl.program_id(0) == 0) & (pl.program_id(1) == 0))
            def _():
                carried.start(xi, xo, *sems)

        heads = [tuple(r.at[:, pl.ds(hh * HEAD_LANES, HEAD_LANES)] for r in (*ins, *outs, *scr))
                 for hh in range(HEADS_PER_STEP)]
        body(qt, kt, heads)
        if carried:
            @pl.when((pl.program_id(0) == n_groups - 1) & (pl.program_id(1) == n_pairs - 1))
            def _():
                carried.finish(xi, xo, *sems)

    grid_spec = pltpu.PrefetchScalarGridSpec(
        num_scalar_prefetch=2, grid=(n_groups, n_pairs),
        in_specs=[spec(kd) for kd in in_kinds] + [HBM_SPEC] * len(x_ins),
        out_specs=[spec(kd) for kd in out_kinds] + [HBM_SPEC] * len(x_outs),
        scratch_shapes=[pltpu.VMEM((tq, width), F32)] * n_scr + x_scr)
    return pl.pallas_call(full_body, name=name, grid_spec=grid_spec,
                          out_shape=[jax.ShapeDtypeStruct((t, MLA_PAD), F32) for _ in out_kinds] + x_outs,
                          compiler_params=_params(("arbitrary", "arbitrary")))(*tables, *args, *x_ins)


class _Carried:
    def __init__(self, ins, outs, n_sem, start, finish):
        self.ins, self.outs, self.n_sem, self.start, self.finish = ins, outs, n_sem, start, finish


def _attn_fwd(q, k, v, tq, carried=None):
    t = q.shape[0]
    nq = t // tq

    def body(qt, kt, heads):
        p_id = pl.program_id(1)
        qi, ki = qt[p_id], kt[p_id]

        @pl.when(ki == 0)
        def _():
            for _, _, _, _, _, m_s, l_s, acc_s in heads:
                m_s[...] = jnp.full(m_s.shape, NEG, F32)
                l_s[...] = jnp.zeros(l_s.shape, F32)
                acc_s[...] = jnp.zeros(acc_s.shape, F32)

        def update(diagonal):
            for q_ref, k_ref, v_ref, _, _, m_s, l_s, acc_s in heads:
                s = lax.dot_general(q_ref[...], k_ref[...], (((1,), (1,)), ((), ())), preferred_element_type=F32)
                if diagonal:
                    s = jnp.where(_below_diagonal(s.shape), s, NEG)
                m_old = m_s[...]
                m_new = jnp.maximum(m_old, jnp.max(s, axis=1, keepdims=True))
                alpha = jnp.exp(m_old - m_new)
                p = jnp.exp(s - m_new[:, :1])
                l_s[...] = l_s[...] * alpha + jnp.sum(p, axis=1, keepdims=True)
                acc_s[...] = acc_s[...] * alpha + jnp.dot(p.astype(MM_DTYPE), v_ref[...], preferred_element_type=F32)
                m_s[...] = m_new

        @pl.when(ki < qi)
        def _():
            update(False)

        @pl.when(ki == qi)
        def _():
            update(True)
            for _, _, _, o_ref, lse_ref, m_s, l_s, acc_s in heads:
                o_ref[...] = acc_s[...] / l_s[...]
                lse_ref[...] = m_s[...] + jnp.log(l_s[...])

    return _attn_call("mla_attn_fwd", body, _causal_pairs(nq, False), (q, k, v), "qkk", "qq", 3, t, tq, carried)


def _attn_probs(q, k, lse, diagonal):
    s = lax.dot_general(q, k, (((1,), (1,)), ((), ())), preferred_element_type=F32)
    p = jnp.exp(s - lse[:, :1])
    return jnp.where(_below_diagonal(s.shape), p, 0.0) if diagonal else p


BWD_HEADS = 2


def _attn_bwd(q, k, v, do, lse, delta, tq, carried=None):
    t = q.shape[0]
    nq = t // tq
    width = BWD_HEADS * HEAD_LANES
    n_groups = N_HEADS // BWD_HEADS
    qt_tab, kt_tab = _causal_pairs(nq, True)
    n_pairs = qt_tab.shape[0]
    qmap = lambda h, p, qt, kt: (qt[p], h)
    kmap = lambda h, p, qt, kt: (kt[p], h)
    qs, ks = pl.BlockSpec((tq, width), qmap), pl.BlockSpec((tq, width), kmap)
    x_ins = list(carried.ins) if carried else []
    x_outs = list(carried.outs) if carried else []
    x_scr = [pltpu.SemaphoreType.DMA((carried.n_sem,)), pltpu.SemaphoreType.DMA((carried.n_sem,))] if carried else []
    nt_dims = (((1,), (1,)), ((), ()))
    tn_dims = (((0,), (0,)), ((), ()))

    def body(qt, kt, q_ref, k_ref, v_ref, do_ref, lse_ref, dl_ref, *rest):
        xi, rest = rest[:len(x_ins)], rest[len(x_ins):]
        dq_hbm, dk_ref, dv_ref = rest[:3]
        xo, rest = rest[3:3 + len(x_outs)], rest[3 + len(x_outs):]
        dk_s, dv_s, dq_s, dq_sem = rest[:4]
        sems = rest[4:]
        grp, p_id = pl.program_id(0), pl.program_id(1)
        qi, ki = qt[p_id], kt[p_id]
        if carried:
            @pl.when((grp == 0) & (p_id == 0))
            def _():
                carried.start(xi, xo, *sems)

        @pl.when(p_id == 0)
        def _():
            dq_s[...] = jnp.zeros(dq_s.shape, F32)

        def step(diagonal):
            rows = pl.ds(pl.multiple_of(qi * tq, tq), tq)
            for hh in range(BWD_HEADS):
                ln = pl.ds(hh * HEAD_LANES, HEAD_LANES)
                qb, kb, vb, dob = q_ref[:, ln], k_ref[:, ln], v_ref[:, ln], do_ref[:, ln]
                p = _attn_probs(qb, kb, lse_ref[:, ln], diagonal)
                dv_s[:, ln] += lax.dot_general(p.astype(MM_DTYPE), dob, tn_dims, preferred_element_type=F32)
                dp = lax.dot_general(dob, vb, nt_dims, preferred_element_type=F32)
                ds = (p * (dp - dl_ref[:, ln][:, :1])).astype(MM_DTYPE)
                dk_s[:, ln] += lax.dot_general(ds, qb, tn_dims, preferred_element_type=F32)
                dq_s[rows, ln] += jnp.dot(ds, kb, preferred_element_type=F32)

        @pl.when(qi == ki)
        def _():
            dk_s[...] = jnp.zeros(dk_s.shape, F32)
            dv_s[...] = jnp.zeros(dv_s.shape, F32)
            step(True)

        @pl.when(qi > ki)
        def _():
            step(False)

        @pl.when(qi == nq - 1)
        def _():
            dk_ref[...] = dk_s[...]
            dv_ref[...] = dv_s[...]

        @pl.when(p_id == n_pairs - 1)
        def _():
            out = pltpu.make_async_copy(dq_s, dq_hbm.at[pl.ds(pl.multiple_of(grp * t, t), t)], dq_sem)
            out.start()
            out.wait()

        if carried:
            @pl.when((grp == n_groups - 1) & (p_id == n_pairs - 1))
            def _():
                carried.finish(xi, xo, *sems)

    grid_spec = pltpu.PrefetchScalarGridSpec(
        num_scalar_prefetch=2, grid=(n_groups, n_pairs),
        in_specs=[qs, ks, ks, qs, qs, qs] + [HBM_SPEC] * len(x_ins),
        out_specs=[HBM_SPEC, ks, ks] + [HBM_SPEC] * len(x_outs),
        scratch_shapes=[pltpu.VMEM((tq, width), F32), pltpu.VMEM((tq, width), F32), pltpu.VMEM((t, width), F32),
                        pltpu.SemaphoreType.DMA] + x_scr)
    return pl.pallas_call(
        body, name="mla_attn_bwd", grid_spec=grid_spec,
        out_shape=[jax.ShapeDtypeStruct((n_groups * t, width), F32), jax.ShapeDtypeStruct((t, MLA_PAD), F32),
                   jax.ShapeDtypeStruct((t, MLA_PAD), F32)] + x_outs,
        compiler_params=_params(("arbitrary", "arbitrary")))(qt_tab, kt_tab, q, k, v, do, lse, delta, *x_ins)


def _dot01(a, b, dims=(((1,), (0,)), ((), ())), ones="rhs"):
    val, sel = (a, b) if ones == "rhs" else (b, a)
    head = val.astype(BF16)
    tail = (val - head.astype(F32)).astype(BF16)
    sel = sel.astype(BF16)
    dot = lambda part: (lax.dot_general(part, sel, dims, preferred_element_type=F32) if ones == "rhs"
                        else lax.dot_general(sel, part, dims, preferred_element_type=F32))
    return dot(head) + dot(tail)


def _dot1(a, b, dims=(((1,), (0,)), ((), ()))):
    return lax.dot_general(a.astype(MM_DTYPE), b.astype(MM_DTYPE), dims, preferred_element_type=F32)


def _dot3(a, b, dims=(((1,), (0,)), ((), ()))):
    return lax.dot_general(a, b, dims, preferred_element_type=F32, precision=lax.Precision.HIGH)


NN3 = (((2,), (1,)), ((0,), (0,)))
NT3 = (((2,), (2,)), ((0,), (0,)))
TN3 = (((1,), (1,)), ((0,), (0,)))


def _tri_masks(nh):
    shape = (nh, CHUNK, CHUNK)
    return lax.broadcasted_iota(jnp.int32, shape, 1), lax.broadcasted_iota(jnp.int32, shape, 2)


def _gdn_chunk_common(k, gcc, bb, row, col, dot=_dot1):
    tril = row >= col
    gcr = jnp.swapaxes(gcc, 1, 2)
    dm = jnp.exp(jnp.where(tril, gcc - gcr, NEG))
    kb = k * bb
    lm = jnp.where(row > col, dot(kb, k, NT3) * dm, 0.0)
    return dm, kb, lm


def _unit_lower_inverse(lm, eye):
    t = eye - lm
    p = lm
    for _ in range(CHUNK.bit_length() - 2):
        p = _dot3(p, p, NN3)
        t = t + _dot3(t, p, NN3)
    return t


def _chunk_sum_matrix(tb, upper):
    r = lax.broadcasted_iota(jnp.int32, (tb, tb), 0)
    c = lax.broadcasted_iota(jnp.int32, (tb, tb), 1)
    same = (r // CHUNK) == (c // CHUNK)
    return (same & ((c >= r) if upper else (c <= r))).astype(F32)


def _gdn_fwd(q, k, v, gb, bb, carried=None):
    nh, t, dh = q.shape
    nchunk = t // CHUNK

    def body(q_ref, k_ref, v_ref, g_ref, b_ref, o_ref, sall_ref, tall_ref, s_s):
        @pl.when(pl.program_id(0) == 0)
        def _():
            s_s[...] = jnp.zeros(s_s.shape, F32)

        row, col = _tri_masks(nh)
        sh = s_s[...]
        for cc in range(cps):
            rows = pl.ds(cc * CHUNK, CHUNK)
            qh, kh, vh, bbh, gcc = q_ref[:, rows, :], k_ref[:, rows, :], v_ref[:, rows, :], b_ref[:, rows, :], \
                g_ref[:, rows, :]
            dm, kb, lm = _gdn_chunk_common(kh, gcc, bbh, row, col)
            eg = jnp.exp(gcc)
            glr = gcc[:, CHUNK - 1:CHUNK, :]
            th = _unit_lower_inverse(lm, (row == col).astype(F32))
            w = _dot1(th, kb * eg, NN3)
            u = _dot1(th, vh * bbh, NN3)
            at = jnp.where(row >= col, _dot1(qh, kh, NT3) * dm, 0.0)
            vn = u - _dot1(w, sh, NN3)
            o_ref[:, rows, :] = _dot1(qh * eg, sh, NN3) + _dot1(at, vn, NN3)
            kd = kh * jnp.exp(glr - gcc)
            sall_ref[:, cc] = sh
            tall_ref[:, rows, :] = th
            sh = sh * jnp.exp(glr) + _dot1(kd, vn, TN3)
        s_s[...] = sh

    cps = min(GDN_STEP_CHUNKS, nchunk)
    steps = nchunk // cps
    blk = pl.BlockSpec((nh, cps * CHUNK, dh), lambda n: (0, n, 0))
    body, x_in, x_out, x_shape, x_scr, x_args = _carry(body, 5, 3, (steps,), carried)
    return pl.pallas_call(
        body, name="gdn_fwd", grid=(steps,), in_specs=[blk] * 5 + x_in,
        out_specs=[blk, pl.BlockSpec((nh, cps, dh, dh), lambda n: (0, n, 0, 0)), blk] + x_out,
        out_shape=[jax.ShapeDtypeStruct((nh, t, dh), F32), jax.ShapeDtypeStruct((nh, nchunk, dh, dh), F32),
                   jax.ShapeDtypeStruct((nh, t, CHUNK), F32)] + x_shape,
        scratch_shapes=[pltpu.VMEM((nh, dh, dh), F32)] + x_scr,
        compiler_params=_params(("arbitrary",)))(q, k, v, gb, bb, *x_args)


def _gdn_bwd(q, k, v, gb, bb, sall, tall, do):
    nh, t, dh = q.shape
    nchunk = t // CHUNK

    def body(q_ref, k_ref, v_ref, g_ref, b_ref, sall_ref, tall_ref, do_ref,
             dq_ref, dk_ref, dv_ref, dg_ref, db_ref, ds_s):
        @pl.when(pl.program_id(0) == 0)
        def _():
            ds_s[...] = jnp.zeros(ds_s.shape, F32)

        row, col = _tri_masks(nh)
        tril, stril = row >= col, row > col
        rsum = lambda x: jnp.sum(x, axis=2, keepdims=True)
        dsp = ds_s[...]
        for cc in reversed(range(cps)):
            rows = pl.ds(cc * CHUNK, CHUNK)
            dsp = chunk_bwd(rows, cc, dsp, row, col, tril, stril, rsum, q_ref, k_ref, v_ref, g_ref, b_ref, sall_ref,
                            tall_ref, do_ref, dq_ref, dk_ref, dv_ref, dg_ref, db_ref)
        ds_s[...] = dsp

    def chunk_bwd(rows, cc, dsp, row, col, tril, stril, rsum, q_ref, k_ref, v_ref, g_ref, b_ref, sall_ref, tall_ref,
                  do_ref, dq_ref, dk_ref, dv_ref, dg_ref, db_ref):
        qh, kh, vh, gcc, bbh = q_ref[:, rows, :], k_ref[:, rows, :], v_ref[:, rows, :], g_ref[:, rows, :], \
            b_ref[:, rows, :]
        sh, th, doh = sall_ref[:, cc], tall_ref[:, rows, :], do_ref[:, rows, :]
        dm, kb, lm = _gdn_chunk_common(kh, gcc, bbh, row, col, _dot3)
        eg = jnp.exp(gcc)
        glr = gcc[:, CHUNK - 1:CHUNK, :]
        glv = jnp.exp(glr)
        egl = jnp.exp(glr - gcc)
        rw, ru = kb * eg, vh * bbh
        w, u = _dot3(th, rw, NN3), _dot3(th, ru, NN3)
        at = jnp.where(tril, _dot3(qh, kh, NT3) * dm, 0.0)
        qd, kd = qh * eg, kh * egl
        vn = u - _dot3(w, sh, NN3)
        dgl = jnp.sum(rsum(dsp * sh), axis=1, keepdims=True)
        dkd = _dot3(vn, dsp, NT3)
        dvn = _dot3(kd, dsp, NN3)
        dqd = _dot3(doh, sh, NT3)
        dat = jnp.where(tril, _dot3(doh, vn, NT3), 0.0)
        dvn = dvn + _dot3(at, doh, TN3)
        dw = -_dot3(dvn, sh, NT3)
        ds_before = dsp * glv + _dot3(qd, doh, TN3) - _dot3(w, dvn, TN3)
        dpa = dat * dm
        dq_ref[:, rows, :] = _dot1(dpa, kh, NN3) + dqd * eg
        dk = _dot1(dpa, qh, TN3) + dkd * egl
        t6 = rsum(dkd * kd)
        dgam = rsum(dqd * qd) - t6
        dgam_last = jnp.sum(t6, axis=1, keepdims=True) + dgl * glv
        drw = _dot3(th, dw, TN3)
        dru = _dot3(th, dvn, TN3)
        dl = -jnp.where(stril, _dot3(drw, w, NT3) + _dot3(dru, u, NT3), 0.0)
        dgam = dgam + rsum(drw * rw)
        dv_ref[:, rows, :] = dru * bbh
        dp2 = dl * dm
        dkb = drw * eg + _dot1(dp2, kh, NN3)
        dk_ref[:, rows, :] = dk + _dot1(dp2, kb, TN3) + dkb * bbh
        db_ref[:, rows, :] = rsum(dru * vh) + rsum(dkb * kh) + jnp.zeros((nh, CHUNK, dh), F32)
        e = dat * at + dl * lm
        dgam_b = dgam + rsum(e) - _dot01(e, jnp.ones((nh, CHUNK, CHUNK), F32), TN3)
        dg_ref[:, rows, :] = dgam_b + jnp.where(row == CHUNK - 1, dgam_last, 0.0)
        return ds_before

    cps = min(GDN_STEP_CHUNKS, nchunk)
    steps = nchunk // cps
    rev = lambda n: (0, steps - 1 - n, 0)
    blk = pl.BlockSpec((nh, cps * CHUNK, dh), rev)
    sblk = pl.BlockSpec((nh, cps, dh, dh), lambda n: (0, steps - 1 - n, 0, 0))
    out = jax.ShapeDtypeStruct((nh, t, dh), F32)
    return pl.pallas_call(
        body, name="gdn_bwd", grid=(steps,), in_specs=[blk] * 5 + [sblk, blk, blk], out_specs=[blk] * 5,
        out_shape=[out] * 5, scratch_shapes=[pltpu.VMEM((nh, dh, dh), F32)],
        compiler_params=_params(("arbitrary",)))(q, k, v, gb, bb, sall, tall, do)


def _group_ones():
    r = lax.broadcasted_iota(jnp.int32, (GDN_W, GDN_W), 0) // GDN_DH
    c = lax.broadcasted_iota(jnp.int32, (GDN_W, GDN_W), 1) // GDN_DH
    return (r == c).astype(F32)


def _conv_taps(x, xprev, w, has_prev):
    row = lax.broadcasted_iota(jnp.int32, x.shape, 0)
    out = x * w[GDN_CONV - 1:GDN_CONV, :]
    for s in range(1, GDN_CONV):
        sh = jnp.where(row >= s, _roll(x, s, 0), _roll(xprev, s, 0) * has_prev)
        out = out + sh * w[GDN_CONV - 1 - s:GDN_CONV - s, :]
    return out


def _head_cols(x, h):
    return x[:, h * GDN_DH:(h + 1) * GDN_DH]


def _heads_spec(tb):
    return pl.BlockSpec((N_HEADS, tb, GDN_DH), lambda i: (0, i, 0))


def _mixer_fwd(x, positions, w, tb, carried=None, carried_gdn=None):
    t, d = x.shape
    tables = _rope_tables(positions)

    hn, proj = _prologue_mm("mix_in", [x], w["mix_pre_g"], lambda xb, g: (xb * _rms_stats(xb) * g,), w["w_in_pad_t"],
                            False)

    def mla_pre(p0, gq, gkv):
        cq, ckv = p0[:, :MLA_Q_RANK], p0[:, MLA_Q_RANK:MLA_Q_RANK + MLA_KV_RANK]
        return cq * _rms_stats(cq) * gq, ckv * _rms_stats(ckv) * gkv

    nq, nkv = _rowwise("mla_pre", mla_pre, [(proj, 512, PIN_MLA // 512, 0)],
                       [w["mla_q_norm_g"], w["mla_kv_norm_g"]], [(MLA_Q_RANK, BF16), (MLA_KV_RANK, BF16)], [], tb)
    qraw = _mm("mla_uq", nq, w["w_uq_pad"], "nn", F32)
    kv = _mm("mla_ukv", nkv, w["w_kv_pad"], "nn", F32)

    def rope_f(qr, kn, vv, kpe, c, s1, s2):
        qo = _heads_apply(qr, lambda xh: _rope(xh, c, s1, s2)) * _attn_scale()
        kp = _rope(kpe, c, s1, s2)
        return qo, kn + jnp.tile(kp, (1, N_HEADS)), vv

    q, k, v = _rowwise("mla_rope", rope_f,
                       [qraw, (kv, MLA_PAD, 0, 0), (kv, MLA_PAD, 1, 0), (proj, HEAD_LANES, PIN_KPE // HEAD_LANES, 0),
                        tables[0], tables[1], tables[2]], [],
                       [(MLA_PAD, BF16)] * 3, [], tb // 2)
    tq = min(1024, t)
    o, lse, *carried_out = _attn_fwd(q, k, v, tq, carried)

    def mla_post(ob, g):
        return (ob * _rms_stats(ob, N_HEADS * MLA_V) * g,)

    (cat,) = _rowwise("mla_post", mla_post, [o], [w["mla_out_g_pad"]], [(MLA_PAD, BF16)], [], tb, wide=(CAT_W, 0))

    gones = _group_ones()
    steps = t // tb

    def gdn_pre(xq, xk, xv, pq, pk, pv, cw, go, has_prev):
        outs = []
        for j, (xc, xp) in enumerate(((xq, pq), (xk, pk), (xv, pv))):
            c = _conv_taps(xc, xp, cw[:, j * GDN_W:(j + 1) * GDN_W], has_prev)
            a = c * _sigmoid(c)
            if j < 2:
                rn = lax.rsqrt(_dot01(a * a, go) + EPS)
                a = a * rn
                if j == 0:
                    a = a * (GDN_DH ** -0.5)
            outs.append(a)
        return tuple(outs)

    qh, kh, vh = _gdn_pre_call("gdn_pre", gdn_pre, proj, w["conv_w"], gones, tb, steps)
    heads_shape = jax.ShapeDtypeStruct((N_HEADS, t, GDN_DH), F32)
    lanes_shape = jax.ShapeDtypeStruct((t, HEAD_LANES), F32)
    lanes_spec = pl.BlockSpec((tb, HEAD_LANES), lambda i: (i, 0))
    vec_spec = lambda n: pl.BlockSpec((1, n), lambda i: (0, 0))

    def gate_f(ab_ref, al_ref, dt_ref, g_ref, b_ref, gh_ref, bh_ref):
        g, b = _gb_fwd(ab_ref[...], al_ref[...], dt_ref[...])
        g_ref[...] = g
        b_ref[...] = b
        gc = _dot01(_chunk_sum_matrix(tb, False), g, ones="lhs")
        for h in range(N_HEADS):
            gh_ref[h] = jnp.broadcast_to(gc[:, h:h + 1], (tb, GDN_DH))
            bh_ref[h] = jnp.broadcast_to(b[:, N_HEADS + h:N_HEADS + h + 1], (tb, GDN_DH))

    g128, b128, gbh, bbh = pl.pallas_call(
        gate_f, name="gdn_gate_f", grid=(steps,),
        in_specs=[pl.BlockSpec((tb, HEAD_LANES), lambda i: (i, PIN_AB // HEAD_LANES)), vec_spec(HEAD_LANES),
                  vec_spec(HEAD_LANES)],
        out_specs=[lanes_spec, lanes_spec, _heads_spec(tb), _heads_spec(tb)],
        out_shape=[lanes_shape, lanes_shape, heads_shape, heads_shape],
        compiler_params=_params(("arbitrary",)))(proj, w["a_log_pad"], w["dt_bias_pad"])
    oh, sall, tall, *carried_out_gdn = _gdn_fwd(qh, kh, vh, gbh, bbh, carried_gdn)

    def gdn_post(o_ref, gt_ref, g_ref, cat_in, cat_ref):
        gt, g = gt_ref[...], g_ref[...]
        outs = []
        for h in range(N_HEADS):
            ob, gth = o_ref[h], _head_cols(gt, h)
            outs.append(ob * _rms_stats(ob) * g * (gth * _sigmoid(gth)))
        cat_ref[...] = jnp.concatenate(outs, axis=1).astype(cat_ref.dtype)

    gate_spec = pl.BlockSpec((tb, GDN_W), lambda i: (i, PIN_GATE // GDN_W))
    cat = pl.pallas_call(
        gdn_post, name="gdn_post", grid=(steps,),
        in_specs=[_heads_spec(tb), gate_spec, vec_spec(GDN_DH), ANY_SPEC],
        out_specs=pl.BlockSpec((tb, GDN_W), lambda i: (i, MLA_PAD // GDN_W)),
        out_shape=jax.ShapeDtypeStruct((t, CAT_W), BF16), input_output_aliases={3: 0},
        compiler_params=_params(("arbitrary",)))(oh, proj, w["gdn_norm_g"], cat)
    mixed, y = _mm_epilogue("mix_out", cat, w["w_out_pad"], [x], w["mix_post_g"],
                            lambda hb, xb, g: (hb, xb + hb * _rms_stats(hb) * g), 2, False, tk=CAT_W)
    saved = dict(x=x, hn=hn, proj=proj, nq=nq, nkv=nkv, q=q, k=k, v=v, o=o, lse=lse, qh=qh, kh=kh, vh=vh,
                 gbh=gbh, bbh=bbh, oh=oh, sall=sall, tall=tall, cat=cat, mixed=mixed,
                 tables=tables, g128=g128, b128=b128)
    return y, saved, list(carried_out) + list(carried_out_gdn)


def _qkv_specs(tb):
    base = PIN_QKV // GDN_W
    cur = [pl.BlockSpec((tb, GDN_W), lambda i, j=j: (i, base + j)) for j in range(3)]
    prev = [pl.BlockSpec((tb, GDN_W), lambda i, j=j: (jnp.maximum(i - 1, 0), base + j)) for j in range(3)]
    return cur + prev


def _gdn_pre_call(name, fn, proj, conv_w, gones, tb, steps):
    t = proj.shape[0]

    def body(xq, xk, xv, pq, pk, pv, cw, go, oq, ok, ov):
        has_prev = jnp.where(pl.program_id(0) == 0, 0.0, 1.0)
        outs = fn(xq[...], xk[...], xv[...], pq[...], pk[...], pv[...], cw[...], go[...], has_prev)
        for r, val in zip((oq, ok, ov), outs):
            for h in range(N_HEADS):
                r[h] = _head_cols(val, h)

    return pl.pallas_call(
        body, name=name, grid=(steps,),
        in_specs=_qkv_specs(tb) + [pl.BlockSpec(conv_w.shape, lambda i: (0, 0)),
                                   pl.BlockSpec(gones.shape, lambda i: (0, 0))],
        out_specs=[_heads_spec(tb)] * 3,
        out_shape=[jax.ShapeDtypeStruct((N_HEADS, t, GDN_DH), F32)] * 3,
        compiler_params=_params(("arbitrary",)))(proj, proj, proj, proj, proj, proj, conv_w, gones)


def _softplus(x):
    return jnp.maximum(x, 0.0) + jnp.log1p(jnp.exp(-jnp.abs(x)))


def _gb_fwd(ab, a_log, dt_bias):
    g = -jnp.exp(a_log) * _softplus(ab + dt_bias)
    return g, _sigmoid(ab)


def _rope_tables(positions):
    half = MLA_ROPE // 2
    freqs = ROPE_THETA ** (-jnp.arange(half, dtype=F32) / half)
    ang = positions.reshape(-1).astype(F32)[:, None] * freqs
    cos, sin = jnp.cos(ang), jnp.sin(ang)
    t = ang.shape[0]
    one = jnp.ones((t, MLA_NOPE), F32)
    z16, z32, z64 = jnp.zeros((t, half), F32), jnp.zeros((t, MLA_ROPE), F32), jnp.zeros((t, MLA_NOPE), F32)
    c = jnp.concatenate([one, cos, cos, jnp.ones((t, MLA_ROPE), F32)], axis=1)
    s1 = jnp.concatenate([z64, -sin, z16, z32], axis=1)
    s2 = jnp.concatenate([z64, z16, sin, z32], axis=1)
    return c, s1, s2


def _mixer_bwd(dy, sv, w, tb, carried=None):
    x, proj = sv["x"], sv["proj"]
    t, d = x.shape
    c, s1, s2 = sv["tables"]
    grads = {}

    dmixed, dcat, grads["mix_post_g"] = _prologue_mm(
        "mix_out_bx", [sv["mixed"], dy], w["mix_post_g"], lambda hb, dyb, g: _rms_bwd(hb, _rms_stats(hb), g, dyb),
        w["w_out_pad"], True)
    grads["w_out_pad"] = _mm("mix_out_bw", sv["cat"], dmixed, "tn", F32)
    steps = t // tb
    vec_spec = lambda n: pl.BlockSpec((1, n), lambda i: (0, 0))

    def gdn_post_b(o_ref, gt_ref, do_ref, g_ref, dproj_ref, doh_ref, dg_ref):
        @pl.when(pl.program_id(0) == 0)
        def _():
            dg_ref[...] = jnp.zeros(dg_ref.shape, F32)

        gt, dob, g = gt_ref[...], do_ref[...], g_ref[...]
        dgates = []
        for h in range(N_HEADS):
            ob, gth, dobh = o_ref[h], _head_cols(gt, h), _head_cols(dob, h)
            sg = _sigmoid(gth)
            r = _rms_stats(ob)
            dxo, dg = _rms_bwd(ob, r, g, dobh * (gth * sg))
            doh_ref[h] = dxo
            dg_ref[...] += dg
            dgates.append(dobh * (ob * r * g) * (sg * (1.0 + gth * (1.0 - sg))))
        dproj_ref[...] = jnp.concatenate(dgates, axis=1).astype(dproj_ref.dtype)

    dproj, doh, grads["gdn_norm_g"] = pl.pallas_call(
        gdn_post_b, name="gdn_post_b", grid=(steps,),
        in_specs=[_heads_spec(tb), pl.BlockSpec((tb, GDN_W), lambda i: (i, PIN_GATE // GDN_W)),
                  pl.BlockSpec((tb, GDN_W), lambda i: (i, MLA_PAD // GDN_W)), vec_spec(GDN_DH)],
        out_specs=[pl.BlockSpec((tb, GDN_W), lambda i: (i, PIN_GATE // GDN_W)), _heads_spec(tb), vec_spec(GDN_DH)],
        out_shape=[jax.ShapeDtypeStruct((t, PIN_W), BF16), jax.ShapeDtypeStruct((N_HEADS, t, GDN_DH), F32),
                   jax.ShapeDtypeStruct((1, GDN_DH), F32)],
        compiler_params=_params(("arbitrary",)))(sv["oh"], proj, dcat, w["gdn_norm_g"])

    def mla_post_b(ob, dmo, g):
        do, dg = _rms_bwd(ob, _rms_stats(ob, N_HEADS * MLA_V), g, dmo, N_HEADS * MLA_V)
        prod = do * ob
        delta = _heads_apply(prod, lambda ph: jnp.sum(ph, axis=1, keepdims=True) + jnp.zeros_like(ph))
        return do, delta, dg

    do, delta, grads["mla_out_g_pad"] = _rowwise(
        "mla_post_b", mla_post_b, [sv["o"], (dcat, MLA_PAD, 0, 0)], [w["mla_out_g_pad"]],
        [(MLA_PAD, BF16), (MLA_PAD, F32)], [(1, MLA_PAD)], tb // 2)
    tq = min(1024, t)
    dq, dk, dv, *carried_out = _attn_bwd(sv["q"], sv["k"], sv["v"], do, sv["lse"], delta, tq, carried)
    n_groups = N_HEADS // BWD_HEADS
    tr = tb // 2
    dq_groups = [(dq, BWD_HEADS * HEAD_LANES, 0, grp * (t // tr)) for grp in range(n_groups)]

    def rope_b(*blocks):
        dqb = jnp.concatenate(blocks[:n_groups], axis=1)
        dkb, dvb, cc, a1, a2 = blocks[n_groups:]
        dqr = _heads_apply(dqb * _attn_scale(), lambda xh: _rope(xh, cc, -a1, -a2))
        ksum = dkb[:, :HEAD_LANES]
        for h in range(1, N_HEADS):
            ksum = ksum + dkb[:, h * HEAD_LANES:(h + 1) * HEAD_LANES]
        lane = lax.broadcasted_iota(jnp.int32, ksum.shape, 1)
        keep = (lane >= MLA_NOPE) & (lane < MLA_NOPE + MLA_ROPE)
        dkpe = jnp.where(keep, _rope(ksum, cc, -a1, -a2), 0.0)
        return dqr, jnp.concatenate([dkb, dvb], axis=1), dkpe

    dqraw, dkv, dkpe = _rowwise("mla_rope_b", rope_b, dq_groups + [dk, dv, c, s1, s2], [],
                                [(MLA_PAD, BF16, t), (2 * MLA_PAD, BF16), (HEAD_LANES, F32)], [], tr)
    dnq = _mm("mla_uq_bx", dqraw, w["w_uq_pad"], "nt", F32)
    grads["w_uq_pad"] = _mm("mla_uq_bw", sv["nq"], dqraw, "tn", F32)
    dnkv = _mm("mla_ukv_bx", dkv, w["w_kv_pad"], "nt", F32)
    grads["w_kv_pad"] = _mm("mla_ukv_bw", sv["nkv"], dkv, "tn", F32)

    def mla_pre_b(p0, dnqb, dnkvb, dkpeb, gq, gkv):
        cq, ckv = p0[:, :MLA_Q_RANK], p0[:, MLA_Q_RANK:MLA_Q_RANK + MLA_KV_RANK]
        dcq, dgq = _rms_bwd(cq, _rms_stats(cq), gq, dnqb)
        dckv, dgkv = _rms_bwd(ckv, _rms_stats(ckv), gkv, dnkvb)
        return jnp.concatenate([dcq, dckv, dkpeb], axis=1), dgq, dgkv

    dproj, grads["mla_q_norm_g"], grads["mla_kv_norm_g"] = _rowwise(
        "mla_pre_b", mla_pre_b, [(proj, 512, PIN_MLA // 512, 0), dnq, dnkv, dkpe],
        [w["mla_q_norm_g"], w["mla_kv_norm_g"]], [(512, BF16)], [(1, MLA_Q_RANK), (1, MLA_KV_RANK)], tb,
        wide=(PIN_W, PIN_MLA // 512), carry=dproj)

    dqh, dkh, dvh, dgh, dbh = _gdn_bwd(sv["qh"], sv["kh"], sv["vh"], sv["gbh"], sv["bbh"], sv["sall"], sv["tall"], doh)
    gones = _group_ones()

    def gdn_pre_b(xq, xk, xv, pq, pk, pv, dq_, dk_, dv_, cw, go, has_prev):
        outs = []
        for j, (xc, xp, dd) in enumerate(((xq, pq, dq_), (xk, pk, dk_), (xv, pv, dv_))):
            cc = _conv_taps(xc, xp, cw[:, j * GDN_W:(j + 1) * GDN_W], has_prev)
            sg = _sigmoid(cc)
            a = cc * sg
            if j < 2:
                rn = lax.rsqrt(_dot01(a * a, go) + EPS)
                if j == 0:
                    dd = dd * (GDN_DH ** -0.5)
                da = rn * dd - a * (rn * rn * rn) * _dot01(dd * a, go)
            else:
                da = dd
            outs.append(da * (sg * (1.0 + cc * (1.0 - sg))))
        return tuple(outs)

    dcq, dck, dcv = _gdn_pre_b_call("gdn_pre_b", gdn_pre_b, proj, (dqh, dkh, dvh), w["conv_w"], gones, tb, steps)
    dproj, grads["conv_w"] = _conv_bwd_call("gdn_conv_b", proj, (dcq, dck, dcv), w["conv_w"], dproj, tb, steps)

    def gate_b(ab_ref, g_ref, b_ref, dgh_ref, dbh_ref, al_ref, dt_ref, carry_ref, dab_ref, dal_ref, ddt_ref):
        @pl.when(pl.program_id(0) == 0)
        def _():
            dal_ref[...] = jnp.zeros(dal_ref.shape, F32)
            ddt_ref[...] = jnp.zeros(ddt_ref.shape, F32)

        ab, g128, b128 = ab_ref[...], g_ref[...], b_ref[...]
        lane = lax.broadcasted_iota(jnp.int32, ab.shape, 1)
        dg_ = jnp.zeros(ab.shape, F32)
        db_ = jnp.zeros(ab.shape, F32)
        for h in range(N_HEADS):
            dg_ = dg_ + jnp.where(lane == h, jnp.broadcast_to(dgh_ref[h][:, 0:1], ab.shape), 0.0)
            db_ = db_ + jnp.where(lane == N_HEADS + h, jnp.broadcast_to(dbh_ref[h][:, 0:1], ab.shape), 0.0)
        dg_ = _dot01(_chunk_sum_matrix(tb, True), dg_, ones="lhs")
        slope = -jnp.exp(al_ref[...]) * _sigmoid(ab + dt_ref[...])
        dab_ref[...] = (dg_ * slope + db_ * b128 * (1.0 - b128)).astype(dab_ref.dtype)
        dal_ref[...] += jnp.sum(dg_ * g128, axis=0, keepdims=True)
        ddt_ref[...] += jnp.sum(dg_ * slope, axis=0, keepdims=True)

    lanes_spec = pl.BlockSpec((tb, HEAD_LANES), lambda i: (i, 0))
    ab_spec = pl.BlockSpec((tb, HEAD_LANES), lambda i: (i, PIN_AB // HEAD_LANES))
    dproj, grads["a_log_pad"], grads["dt_bias_pad"] = pl.pallas_call(
        gate_b, name="gdn_gate_b", grid=(steps,),
        in_specs=[ab_spec, lanes_spec, lanes_spec, _heads_spec(tb), _heads_spec(tb), vec_spec(HEAD_LANES),
                  vec_spec(HEAD_LANES), ANY_SPEC],
        out_specs=[ab_spec, vec_spec(HEAD_LANES), vec_spec(HEAD_LANES)],
        out_shape=[jax.ShapeDtypeStruct((t, PIN_W), BF16), jax.ShapeDtypeStruct((1, HEAD_LANES), F32),
                   jax.ShapeDtypeStruct((1, HEAD_LANES), F32)],
        input_output_aliases={7: 0},
        compiler_params=_params(("arbitrary",)))(proj, sv["g128"], sv["b128"], dgh, dbh, w["a_log_pad"],
                                                 w["dt_bias_pad"], dproj)
    grads["w_in_pad_t"] = _mm("mix_in_bw", dproj, sv["hn"], "tn", F32)

    def pre_b(dhn, xb, dyb, g):
        dx, dg = _rms_bwd(xb, _rms_stats(xb), g, dhn)
        return dyb + dx, dg

    dx, grads["mix_pre_g"] = _mm_epilogue("mix_in_bx", dproj, w["w_in_pad_t"], [x, dy], w["mix_pre_g"], pre_b, 1, True,
                                          tk=PIN_W // 3)
    return dx, grads, carried_out


def _gdn_pre_b_call(name, fn, proj, dd, conv_w, gones, tb, steps):
    t = proj.shape[0]

    def body(xq, xk, xv, pq, pk, pv, d0, d1, d2, cw, go, oq, ok, ov):
        has_prev = jnp.where(pl.program_id(0) == 0, 0.0, 1.0)
        dd_rows = [jnp.concatenate([dr[h] for h in range(N_HEADS)], axis=1) for dr in (d0, d1, d2)]
        outs = fn(xq[...], xk[...], xv[...], pq[...], pk[...], pv[...], *dd_rows, cw[...], go[...], has_prev)
        for r, val in zip((oq, ok, ov), outs):
            r[...] = val

    return pl.pallas_call(
        body, name=name, grid=(steps,),
        in_specs=_qkv_specs(tb) + [_heads_spec(tb)] * 3 + [pl.BlockSpec(conv_w.shape, lambda i: (0, 0)),
                                                          pl.BlockSpec(gones.shape, lambda i: (0, 0))],
        out_specs=[pl.BlockSpec((tb, GDN_W), lambda i: (i, 0))] * 3,
        out_shape=[jax.ShapeDtypeStruct((t, GDN_W), F32)] * 3,
        compiler_params=_params(("arbitrary",)))(proj, proj, proj, proj, proj, proj, *dd, conv_w, gones)


def _conv_bwd_call(name, proj, dc, conv_w, dproj, tb, steps):
    t = proj.shape[0]
    dcur = [pl.BlockSpec((tb, GDN_W), lambda i: (i, 0))] * 3
    dnext = [pl.BlockSpec((tb, GDN_W), lambda i: (jnp.minimum(i + 1, steps - 1), 0))] * 3

    def body(xq, xk, xv, pq, pk, pv, d0, d1, d2, n0, n1, n2, cw, carry_ref, dx_ref, dw_ref):
        i = pl.program_id(0)
        has_prev = jnp.where(i == 0, 0.0, 1.0)
        has_next = jnp.where(i == steps - 1, 0.0, 1.0)

        @pl.when(i == 0)
        def _():
            dw_ref[...] = jnp.zeros(dw_ref.shape, F32)

        wv = cw[...]
        dws, dxs = [], []
        for j, (xr, pr, dr, nr) in enumerate(((xq, pq, d0, n0), (xk, pk, d1, n1), (xv, pv, d2, n2))):
            x, xp, dcv, dnx = xr[...], pr[...], dr[...], nr[...]
            wj = wv[:, j * GDN_W:(j + 1) * GDN_W]
            row = lax.broadcasted_iota(jnp.int32, x.shape, 0)
            dx = dcv * wj[GDN_CONV - 1:GDN_CONV, :]
            rows_w = [jnp.sum(dcv * x, axis=0, keepdims=True)]
            for s in range(1, GDN_CONV):
                up = jnp.where(row < tb - s, _roll(dcv, tb - s, 0), _roll(dnx, tb - s, 0) * has_next)
                dx = dx + up * wj[GDN_CONV - 1 - s:GDN_CONV - s, :]
                sh = jnp.where(row >= s, _roll(x, s, 0), _roll(xp, s, 0) * has_prev)
                rows_w.append(jnp.sum(dcv * sh, axis=0, keepdims=True))
            dxs.append(dx)
            dws.append(jnp.concatenate(rows_w[::-1], axis=0))
        dx_ref[...] = jnp.concatenate(dxs, axis=1).astype(dx_ref.dtype)
        dw_ref[...] += jnp.concatenate(dws, axis=1)

    return pl.pallas_call(
        body, name=name, grid=(steps,),
        in_specs=_qkv_specs(tb) + dcur + dnext + [pl.BlockSpec(conv_w.shape, lambda i: (0, 0)), ANY_SPEC],
        out_specs=[pl.BlockSpec((tb, 3 * GDN_W), lambda i: (i, PIN_QKV // (3 * GDN_W))),
                   pl.BlockSpec(conv_w.shape, lambda i: (0, 0))],
        out_shape=[jax.ShapeDtypeStruct((t, PIN_W), BF16), jax.ShapeDtypeStruct(conv_w.shape, F32)],
        input_output_aliases={13: 0},
        compiler_params=_params(("arbitrary",)))(proj, proj, proj, proj, proj, proj, *dc, *dc, conv_w, dproj)


def _pad_heads_cols(wm, per_head):
    r = wm.shape[0]
    return jnp.pad(wm.reshape(r, N_HEADS, per_head), ((0, 0), (0, 0), (0, HEAD_LANES - per_head))).reshape(r, MLA_PAD)


def _unpad_heads_cols(wm, per_head):
    r = wm.shape[0]
    return wm.reshape(r, N_HEADS, HEAD_LANES)[:, :, :per_head].reshape(r, N_HEADS * per_head)


W_IN_COLS = MLA_Q_RANK + MLA_KV_RANK + MLA_ROPE + 3 * GDN_W + 2 * N_HEADS + GDN_W
W_IN_SHARD = W_IN_COLS // N_SHARD
W_IN_SHARD_PAD = 640
_Q0 = MLA_Q_RANK + MLA_KV_RANK
_Q1 = _Q0 + MLA_ROPE
_Q2 = _Q1 + 3 * GDN_W
_Q3 = _Q2 + 2 * N_HEADS
W_IN_SEGMENTS = [(0, _Q0, PIN_MLA), (_Q0, _Q1, PIN_KPE + MLA_NOPE), (_Q1, _Q2, PIN_QKV), (_Q2, _Q3, PIN_AB),
                 (_Q3, W_IN_COLS, PIN_GATE)]


def _win_pad_t(slabs):
    d = slabs.shape[2]
    pieces, at = [], 0
    for c0, c1, r0 in sorted(W_IN_SEGMENTS, key=lambda s: s[2]):
        if r0 > at:
            pieces.append(jnp.zeros((r0 - at, d), slabs.dtype))
        for q in range(N_SHARD):
            lo, hi = max(c0, q * W_IN_SHARD), min(c1, (q + 1) * W_IN_SHARD)
            if lo < hi:
                pieces.append(slabs[q, lo - q * W_IN_SHARD:hi - q * W_IN_SHARD])
        at = r0 + c1 - c0
    pieces.append(jnp.zeros((PIN_W - at, d), slabs.dtype))
    return jnp.concatenate(pieces, axis=0)


def _win_cols_t(wp_t, c_lo, c_hi):
    pieces = []
    for c0, c1, r0 in W_IN_SEGMENTS:
        lo, hi = max(c0, c_lo), min(c1, c_hi)
        if lo < hi:
            pieces.append(wp_t[r0 + lo - c0:r0 + hi - c0])
    return jnp.concatenate(pieces, axis=0)


def _wkv_to_pad(wkv):
    r = wkv.shape[0]
    w3 = wkv.reshape(r, N_HEADS, MLA_NOPE + MLA_V)
    kpart = jnp.pad(w3[:, :, :MLA_NOPE], ((0, 0), (0, 0), (0, HEAD_LANES - MLA_NOPE))).reshape(r, MLA_PAD)
    vpart = jnp.pad(w3[:, :, MLA_NOPE:], ((0, 0), (0, 0), (0, HEAD_LANES - MLA_V))).reshape(r, MLA_PAD)
    return jnp.concatenate([kpart, vpart], axis=1)


def _wkv_from_pad(wp):
    r = wp.shape[0]
    kpart = wp[:, :MLA_PAD].reshape(r, N_HEADS, HEAD_LANES)[:, :, :MLA_NOPE]
    vpart = wp[:, MLA_PAD:].reshape(r, N_HEADS, HEAD_LANES)[:, :, :MLA_V]
    return jnp.concatenate([kpart, vpart], axis=2).reshape(r, N_HEADS * (MLA_NOPE + MLA_V))


def _wout_to_pad(wo):
    n = wo.shape[1]
    mla = jnp.pad(wo[:N_HEADS * MLA_V].reshape(N_HEADS, MLA_V, n), ((0, 0), (0, HEAD_LANES - MLA_V), (0, 0)))
    return jnp.concatenate([mla.reshape(MLA_PAD, n), wo[N_HEADS * MLA_V:]], axis=0)


def _wout_from_pad(wp):
    n = wp.shape[1]
    mla = wp[:MLA_PAD].reshape(N_HEADS, HEAD_LANES, n)[:, :MLA_V].reshape(N_HEADS * MLA_V, n)
    return jnp.concatenate([mla, wp[MLA_PAD:]], axis=0)


def _pad_lanes(v, n):
    return jnp.pad(v, ((0, 0), (0, n - v.shape[1])))


def _compute_weights(full):
    w = {}
    for n in FFN_BIG:
        if n in full:
            w[n] = full[n].astype(MM_DTYPE)
    w["w_in_pad_t"] = _win_pad_t(full["w_in"]).astype(MM_DTYPE)
    w["w_uq_pad"] = _pad_heads_cols(full["mla_w_uq"], MLA_NOPE + MLA_ROPE).astype(MM_DTYPE)
    w["w_kv_pad"] = _wkv_to_pad(full["mla_w_ukv"]).astype(MM_DTYPE)
    w["w_out_pad"] = _wout_to_pad(full["w_out"]).astype(MM_DTYPE)
    w["conv_w"] = full["gdn_conv_w"].astype(F32)
    for n in ("ffn1_pre_g", "ffn1_post_g", "mix_pre_g", "mla_q_norm_g", "mla_kv_norm_g", "gdn_norm_g", "mix_post_g",
              "ffn2_pre_g", "ffn2_post_g"):
        w[n] = full[n]
    w["mla_out_g_pad"] = _pad_heads_cols(full["mla_out_g"], MLA_V)
    w["a_log_pad"] = _pad_lanes(full["gdn_a_log"], HEAD_LANES)
    w["dt_bias_pad"] = _pad_lanes(full["gdn_dt_bias"], HEAD_LANES)
    return w


FFN2_BIG = FFN_BIG[3:]


def _local_step(x, positions, loss_target, full, late=None):
    t, d = x.shape
    tb = min(512, t)
    tm = min(1024, t)
    tk = min(2048, t)
    w = _compute_weights(full)
    ffn = lambda tag: (w[tag + "_pre_g"], w[tag + "_w_gate"], w[tag + "_w_up"], w[tag + "_w_down"], w[tag + "_post_g"])
    x1, sv1 = _ffn_fwd("ffn1", x, *ffn("ffn1"), tm)
    x2, svm, gathered = _mixer_fwd(x1, positions, w, tb, _carried_gather(late[0][:2]) if late else None,
                                   _carried_gather(late[0][2:]) if late else None)
    for n, gw in zip(FFN2_BIG, gathered):
        w[n] = gw
    x3, sv2 = _ffn_fwd("ffn2", x2, *ffn("ffn2"), tm)

    def loss_f(yb, tg):
        e = yb - tg
        return e * (1.0 / d), jnp.sum(e * e, axis=0, keepdims=True)

    dy, lsum = _rowwise("loss", loss_f, [x3, loss_target], [], [(d, F32)], [(1, d)], tb)
    g = {}
    dx2, g["ffn2_pre_g"], g["ffn2_w_gate"], g["ffn2_w_up"], g["ffn2_w_down"], g["ffn2_post_g"] = _ffn_bwd(
        "ffn2", dy, sv2, *ffn("ffn2"), tm, tk)[:6]

    def pair_sums(arrs, tag):
        got = _swap_halves(arrs, tag)
        return [_add_pair("add_pair%s_%d" % (tag, i), gi, gt, late[1]) for i, (gi, gt) in enumerate(zip(arrs, got))]

    def chip_sums(pairs, slabs, tag):
        return [_add_chips("add_chips%s_%d" % (tag, i), pr, sl, late[2]) for i, (pr, sl) in enumerate(zip(pairs, slabs))]

    if late:
        pairs2 = pair_sums([g[n] for n in FFN2_BIG], "_ffn2")
        dx1, gm, slabs2 = _mixer_bwd(dx2, svm, w, tb, _carried_scatter(pairs2))
        for n, hs in zip(FFN2_BIG, chip_sums(pairs2, slabs2, "_ffn2")):
            g[n] = hs
    else:
        dx1, gm, _ = _mixer_bwd(dx2, svm, w, tb)
    g["w_in"] = jnp.stack([jnp.pad(_win_cols_t(gm["w_in_pad_t"], q * W_IN_SHARD, (q + 1) * W_IN_SHARD),
                                   ((0, W_IN_SHARD_PAD - W_IN_SHARD), (0, 0))) for q in range(N_SHARD)])
    g["mla_w_uq"] = _unpad_heads_cols(gm["w_uq_pad"], MLA_NOPE + MLA_ROPE)
    g["mla_w_ukv"] = _wkv_from_pad(gm["w_kv_pad"])
    g["gdn_conv_w"] = gm["conv_w"]
    g["w_out"] = _wout_from_pad(gm["w_out_pad"])
    if late:
        quarters = [_pack([jnp.split(g[n], N_SHARD, axis=SHARD_AXIS[n])[q] for n in MIX_BIG], MM_DTYPE)
                    for q in range(N_SHARD)]
        pairs_m = pair_sums([g["w_in"].astype(MM_DTYPE), jnp.stack(quarters)], "_mix")
        pairs_d, pairs_gu = [], []

        def make_mid(dwd):
            pairs_d.extend(pair_sums([dwd], "_ffn1d"))
            return _carried_scatter(pairs_d)

        def make_up(dwg, dwu):
            pairs_gu.extend(pair_sums([dwg, dwu], "_ffn1"))
            return _carried_scatter(pairs_gu)

        dx0, g["ffn1_pre_g"], _, _, _, g["ffn1_post_g"], slabs_m, slabs_d, slabs_gu = _ffn_bwd(
            "ffn1", dx1, sv1, *ffn("ffn1"), tm, tk, _carried_scatter(pairs_m), make_mid, make_up)
        g["ffn1_w_gate"], g["ffn1_w_up"] = chip_sums(pairs_gu, slabs_gu, "_ffn1")
        g["ffn1_w_down"] = chip_sums(pairs_d, slabs_d, "_ffn1d")[0]
        g["w_in"], g["mix_pack"] = chip_sums(pairs_m, slabs_m, "_mix")
    else:
        dx0, g["ffn1_pre_g"], g["ffn1_w_gate"], g["ffn1_w_up"], g["ffn1_w_down"], g["ffn1_post_g"] = _ffn_bwd(
            "ffn1", dx1, sv1, *ffn("ffn1"), tm, tk)[:6]
    g["mix_pre_g"], g["mix_post_g"] = gm["mix_pre_g"], gm["mix_post_g"]
    g["mla_q_norm_g"], g["mla_kv_norm_g"] = gm["mla_q_norm_g"], gm["mla_kv_norm_g"]
    g["gdn_norm_g"] = gm["gdn_norm_g"]
    g["mla_out_g"] = _unpad_heads_cols(gm["mla_out_g_pad"], MLA_V)
    g["gdn_a_log"] = gm["a_log_pad"][:, :N_HEADS]
    g["gdn_dt_bias"] = gm["dt_bias_pad"][:, :N_HEADS]
    return lsum, dx0, g


HBM_SPEC = pl.BlockSpec(memory_space=pltpu.HBM)


def _place():
    return lax.axis_index("x"), lax.axis_index("y"), lax.axis_index("c")


def _exchange_call(name, body, ins, out_shapes, n_remote, n_local):
    return pl.pallas_call(
        body, name=name, in_specs=[HBM_SPEC] * len(ins), out_specs=[HBM_SPEC] * len(out_shapes), out_shape=out_shapes,
        scratch_shapes=[pltpu.SemaphoreType.DMA((n_remote,)), pltpu.SemaphoreType.DMA((n_remote,)),
                        pltpu.SemaphoreType.DMA((n_local,))])(*ins)


def _other_chips(x, y):
    return [(1 - x, y), (x, 1 - y), (1 - x, 1 - y)]


def _at_each_chip(fn):
    x, y, _ = _place()
    for cx in range(2):
        for cy in range(2):
            pl.when((x == cx) & (y == cy))(functools.partial(fn, cx, cy))


def _at_each_device(fn):
    x, y, c = _place()
    for cx in range(2):
        for cy in range(2):
            for cc in range(2):
                pl.when((x == cx) & (y == cy) & (c == cc))(functools.partial(fn, cx, cy, cc))


def _at_each_core(fn):
    c = lax.axis_index("c")
    for cc in range(2):
        pl.when(c == cc)(functools.partial(fn, cc))


def _gather_shards(ws):
    nw = len(ws)

    def body(*refs):
        w_refs, out_refs = refs[:nw], refs[nw:2 * nw]
        send_sems, recv_sems, local_sems = refs[2 * nw:]

        def run(x, y, c):
            chips = _other_chips(x, y)
            me, sibling = 2 * x + y, (x, y, 1 - c)

            def half(ref, which):
                hr = ref.shape[0] // 2
                return ref.at[pl.ds(which * hr, hr)]

            def over_ici(i, j, src, slab, to):
                return pltpu.make_async_remote_copy(
                    src_ref=half(src, c), dst_ref=half(out_refs[i].at[slab], c), send_sem=send_sems.at[7 * i + j],
                    recv_sem=recv_sems.at[7 * i + j], device_id=to, device_id_type=MESH)

            def over_d2d(i, j, slab, which):
                return pltpu.make_async_remote_copy(
                    src_ref=half(out_refs[i].at[slab], which), dst_ref=half(out_refs[i].at[slab], which),
                    send_sem=send_sems.at[7 * i + 3 + j], recv_sem=recv_sems.at[7 * i + 3 + j], device_id=sibling,
                    device_id_type=MESH)

            def own(i, w_ref):
                return pltpu.make_async_remote_copy(
                    src_ref=w_ref, dst_ref=out_refs[i].at[me], send_sem=send_sems.at[7 * i + 6],
                    recv_sem=recv_sems.at[7 * i + 6], device_id=sibling, device_id_type=MESH)

            sends, passed = [], []
            for i, w_ref in enumerate(w_refs):
                for j, (px, py) in enumerate(chips):
                    sends.append(over_ici(i, j, w_ref, me, (px, py, c)))
                    sends[-1].start()
            for i, w_ref in enumerate(w_refs):
                sends.append(own(i, w_ref))
                sends[-1].start()
            for i, w_ref in enumerate(w_refs):
                for j, (px, py) in enumerate(chips):
                    over_ici(i, j, w_ref, 2 * px + py, (px, py, c)).wait_recv()
                    passed.append(over_d2d(i, j, 2 * px + py, c))
                    passed[-1].start()
            for i, w_ref in enumerate(w_refs):
                own(i, w_ref).wait_recv()
                for j, (px, py) in enumerate(chips):
                    over_d2d(i, j, 2 * px + py, 1 - c).wait_recv()
            for cp in sends + passed:
                cp.wait_send()

        _at_each_device(run)

    outs = [jax.ShapeDtypeStruct((N_SHARD,) + w.shape, w.dtype) for w in ws]
    return _exchange_call("gather_weight_shards", body, ws, outs, 7 * nw, 1)


def _swap_halves(gs, tag=""):
    ng = len(gs)

    def body(*refs):
        g_refs, got_refs = refs[:ng], refs[ng:2 * ng]
        send_sems, recv_sems, _ = refs[2 * ng:]
        x, y, _ = _place()

        def run(c):
            sends = []
            for i, (g_ref, got_ref) in enumerate(zip(g_refs, got_refs)):
                hr = got_ref.shape[1]
                sends.append(pltpu.make_async_remote_copy(
                    src_ref=g_ref.at[:, pl.ds((1 - c) * hr, hr)], dst_ref=got_ref, send_sem=send_sems.at[i],
                    recv_sem=recv_sems.at[i], device_id=(x, y, 1 - c), device_id_type=MESH))
                sends[-1].start()
            for cp in sends:
                cp.wait()

        _at_each_core(run)

    halves = [jax.ShapeDtypeStruct((g.shape[0], g.shape[1] // 2, g.shape[2]), g.dtype) for g in gs]
    return _exchange_call("swap_grad_halves" + tag, body, gs, halves, ng, 1)


def _scatter_copies(p_refs, out_refs, send_sems, recv_sems, x, y):
    c = lax.axis_index("c")
    copies = []
    for i, (p_ref, out_ref) in enumerate(zip(p_refs, out_refs)):
        for j, (px, py) in enumerate(_other_chips(x, y)):
            copies.append(pltpu.make_async_remote_copy(
                src_ref=p_ref.at[2 * px + py], dst_ref=out_ref.at[j], send_sem=send_sems.at[3 * i + j],
                recv_sem=recv_sems.at[3 * i + j], device_id=(px, py, c), device_id_type=MESH))
    return copies


def _start_all(make, *refs):
    def run(x, y):
        for cp in make(*refs, x, y):
            cp.start()

    _at_each_chip(run)


def _wait_all(make, *refs):
    def run(x, y):
        copies = make(*refs, x, y)
        for cp in copies:
            cp.wait_recv()
        for cp in copies:
            cp.wait_send()

    _at_each_chip(run)


def _scatter_shapes(ps):
    return [jax.ShapeDtypeStruct((3,) + p.shape[1:], p.dtype) for p in ps]


def _carried_scatter(ps):
    return _Carried(ps, _scatter_shapes(ps), 3 * len(ps), functools.partial(_start_all, _scatter_copies),
                    functools.partial(_wait_all, _scatter_copies))


def _direct_gather_copies(w_refs, out_refs, send_sems, recv_sems, x, y, arriving):
    c = lax.axis_index("c")
    me = 2 * x + y
    peers = [((px, py, c), 2 * px + py) for px, py in _other_chips(x, y)] + [((x, y, 1 - c), me)]
    copies = []
    for i, (w_ref, out_ref) in enumerate(zip(w_refs, out_refs)):
        for j, (peer, slab) in enumerate(peers):
            copies.append(pltpu.make_async_remote_copy(
                src_ref=w_ref, dst_ref=out_ref.at[slab if arriving else me], send_sem=send_sems.at[4 * i + j],
                recv_sem=recv_sems.at[4 * i + j], device_id=peer, device_id_type=MESH))
    return copies


def _carried_gather(ws):
    def start(w_refs, out_refs, send_sems, recv_sems):
        def run(x, y):
            for cp in _direct_gather_copies(w_refs, out_refs, send_sems, recv_sems, x, y, False):
                cp.start()

        _at_each_chip(run)

    def finish(w_refs, out_refs, send_sems, recv_sems):
        def run(x, y):
            for cp in _direct_gather_copies(w_refs, out_refs, send_sems, recv_sems, x, y, True):
                cp.wait_recv()
            for cp in _direct_gather_copies(w_refs, out_refs, send_sems, recv_sems, x, y, False):
                cp.wait_send()

        _at_each_chip(run)

    outs = [jax.ShapeDtypeStruct((N_SHARD,) + w.shape, w.dtype) for w in ws]
    return _Carried(ws, outs, 4 * len(ws), start, finish)


def _share_halves(hs):
    n = len(hs)

    def body(*refs):
        h_refs, out_refs = refs[:n], refs[n:2 * n]
        send_sems, recv_sems, _ = refs[2 * n:]
        x, y, c = _place()
        sends = []
        for i, (h_ref, out_ref) in enumerate(zip(h_refs, out_refs)):
            sends.append(pltpu.make_async_remote_copy(
                src_ref=h_ref, dst_ref=out_ref, send_sem=send_sems.at[i], recv_sem=recv_sems.at[i],
                device_id=(x, y, 1 - c), device_id_type=MESH))
            sends[-1].start()
        for cp in sends:
            cp.wait()

    outs = [jax.ShapeDtypeStruct(h.shape, h.dtype) for h in hs]
    return _exchange_call("share_grad_halves", body, hs, outs, n, 1)


def _scalar_grid_call(name, body, scalars, grid, in_specs, out_specs, out_shape, args):
    grid_spec = pltpu.PrefetchScalarGridSpec(num_scalar_prefetch=len(scalars), grid=grid, in_specs=in_specs,
                                             out_specs=out_specs)
    return pl.pallas_call(body, name=name, grid_spec=grid_spec, out_shape=out_shape,
                          compiler_params=_params(("arbitrary",) * len(grid)))(*scalars, *args)


def _add_pair(name, g, got, core):
    ns_, hr, cols = got.shape
    th = _row_tile(hr, 512)
    nb = hr // th

    def body(core_ref, g_ref, got_ref, out_ref):
        out_ref[...] = (g_ref[...].astype(F32) + got_ref[...].astype(F32)).astype(out_ref.dtype)

    blk = pl.BlockSpec((1, th, cols), lambda q, j, core_ref: (q, j, 0))
    own = pl.BlockSpec((1, th, cols), lambda q, j, core_ref: (q, core_ref[0] * nb + j, 0))
    return _scalar_grid_call(name, body, [core], (ns_, nb), [own, blk], blk,
                             jax.ShapeDtypeStruct(got.shape, got.dtype), [g, got])


def _add_chips(name, pairs, slabs, chip):
    _, hr, cols = slabs.shape
    th = _row_tile(hr, 512)

    def body(chip_ref, own_ref, s0_ref, s1_ref, s2_ref, out_ref):
        total = own_ref[0].astype(F32) + s0_ref[0].astype(F32)
        out_ref[...] = (total + s1_ref[0].astype(F32)) + s2_ref[0].astype(F32)

    own = pl.BlockSpec((1, th, cols), lambda j, chip_ref: (chip_ref[0], j, 0))
    others = [pl.BlockSpec((1, th, cols), lambda j, chip_ref, k=k: (k, j, 0)) for k in range(3)]
    return _scalar_grid_call(name, body, [chip], (hr // th,), [own] + others,
                             pl.BlockSpec((th, cols), lambda j, chip_ref: (j, 0)),
                             jax.ShapeDtypeStruct((hr, cols), F32), [pairs, slabs, slabs, slabs])


def _join_halves(name, mine, other, core):
    hr, cols = mine.shape
    th = _row_tile(hr, 512)
    nb = hr // th

    def body(core_ref, mine_ref, other_ref, out_ref):
        is_mine = pl.program_id(0) == core_ref[0]

        @pl.when(is_mine)
        def _():
            out_ref[0] = mine_ref[...]

        @pl.when(jnp.logical_not(is_mine))
        def _():
            out_ref[0] = other_ref[...]

    blk = pl.BlockSpec((th, cols), lambda h, j, core_ref: (j, 0))
    return _scalar_grid_call(name, body, [core], (2, nb), [blk, blk],
                             pl.BlockSpec((1, th, cols), lambda h, j, core_ref: (0, h * nb + j, 0)),
                             jax.ShapeDtypeStruct((1, 2 * hr, cols), mine.dtype), [mine, other])


def _gather_small(sp):
    def body(s_ref, out_ref, send_sems, recv_sems, local_sem):
        x, y, c = _place()
        me = 4 * x + 2 * y + c
        peers = [(x ^ (m >> 2), y ^ ((m >> 1) & 1), c ^ (m & 1)) for m in range(1, 8)]
        mine = pltpu.make_async_copy(s_ref, out_ref.at[me], local_sem)
        mine.start()
        sends = [pltpu.make_async_remote_copy(src_ref=s_ref, dst_ref=out_ref.at[me], send_sem=send_sems.at[j],
                                              recv_sem=recv_sems.at[j], device_id=p, device_id_type=MESH)
                 for j, p in enumerate(peers)]
        for cp in sends:
            cp.start()
        for j, (px, py, pc) in enumerate(peers):
            pltpu.make_async_remote_copy(src_ref=s_ref, dst_ref=out_ref.at[4 * px + 2 * py + pc],
                                         send_sem=send_sems.at[j], recv_sem=recv_sems.at[j], device_id=(px, py, pc),
                                         device_id_type=MESH).wait_recv()
        for cp in sends:
            cp.wait_send()
        mine.wait()

    return pl.pallas_call(
        body, name="gather_small_grads", in_specs=[HBM_SPEC], out_specs=HBM_SPEC,
        out_shape=jax.ShapeDtypeStruct((8,) + sp.shape, sp.dtype),
        scratch_shapes=[pltpu.SemaphoreType.DMA((7,)), pltpu.SemaphoreType.DMA((7,)), pltpu.SemaphoreType.DMA])(sp)


def _pack_rows(total):
    rows = -(-total // LANES)
    return -(-rows // 32) * 32


def _pack(arrs, dtype):
    flat = jnp.concatenate([a.reshape(-1).astype(dtype) for a in arrs])
    rows = _pack_rows(flat.shape[0])
    return jnp.pad(flat, (0, rows * LANES - flat.shape[0])).reshape(rows, LANES)


def _unpack(buf, shapes):
    flat = buf.reshape(-1)
    out, off = {}, 0
    for n, shp in shapes:
        size = shp[0] * shp[1]
        out[n] = flat[off:off + size].reshape(shp)
        off += size
    return out


def _to_wire(name, w3):
    _, r, cols = w3.shape
    tb = _row_tile(r, 512)

    def body(w_ref, o_ref):
        o_ref[...] = w_ref[0].astype(o_ref.dtype)

    return pl.pallas_call(
        body, name=name, grid=(r // tb,), in_specs=[pl.BlockSpec((1, tb, cols), lambda i: (0, i, 0))],
        out_specs=pl.BlockSpec((tb, cols), lambda i: (i, 0)), out_shape=jax.ShapeDtypeStruct((r, cols), MM_DTYPE),
        compiler_params=_params(("arbitrary",)))(w3)


def _adamw(name, w3, g, m3, v3, tb):
    c1 = 1.0 - ADAM_B1 ** ADAM_STEP
    c2 = 1.0 - ADAM_B2 ** ADAM_STEP
    _, r, cols = w3.shape
    emit = g.ndim == 2
    blk3 = pl.BlockSpec((1, tb, cols), lambda i: (0, i, 0))
    g_spec = pl.BlockSpec((tb, cols), lambda i: (i, 0)) if emit else blk3

    def body(w_ref, g_ref, m_ref, v_ref, *out_refs):
        gb = g_ref[...] if emit else g_ref[0]
        m2 = ADAM_B1 * m_ref[0] + (1.0 - ADAM_B1) * gb
        v2 = ADAM_B2 * v_ref[0] + (1.0 - ADAM_B2) * (gb * gb)
        out_refs[-3][0] = -ADAM_LR * ((m2 / c1) / (jnp.sqrt(v2 / c2) + ADAM_EPS) + ADAM_WD * w_ref[0])
        out_refs[-2][0] = m2
        out_refs[-1][0] = v2
        if emit:
            out_refs[0][0] = gb

    n_out = 4 if emit else 3
    outs = pl.pallas_call(
        body, name=name, grid=(r // tb,), in_specs=[blk3, g_spec, blk3, blk3], out_specs=[blk3] * n_out,
        out_shape=[jax.ShapeDtypeStruct((1, r, cols), F32)] * n_out,
        compiler_params=_params(("arbitrary",)))(w3, g, m3, v3)
    return outs if emit else [g] + list(outs)


def _row_tile(rows, pref):
    if rows <= pref:
        return rows
    t = pref
    while t >= 8:
        if rows % t == 0 and t % 8 == 0:
            return t
        t -= 8
    return rows


def kernel(x, positions, ffn1_pre_g, ffn1_w_gate, ffn1_w_up, ffn1_w_down, ffn1_post_g, mix_pre_g, w_in, mla_q_norm_g, mla_w_uq, mla_kv_norm_g, mla_w_ukv, mla_out_g, gdn_conv_w, gdn_a_log, gdn_dt_bias, gdn_norm_g, w_out, mix_post_g, ffn2_pre_g, ffn2_w_gate, ffn2_w_up, ffn2_w_down, ffn2_post_g, loss_target, m_ffn1_pre_g, m_ffn1_w_gate, m_ffn1_w_up, m_ffn1_w_down, m_ffn1_post_g, m_mix_pre_g, m_w_in, m_mla_q_norm_g, m_mla_w_uq, m_mla_kv_norm_g, m_mla_w_ukv, m_mla_out_g, m_gdn_conv_w, m_gdn_a_log, m_gdn_dt_bias, m_gdn_norm_g, m_w_out, m_mix_post_g, m_ffn2_pre_g, m_ffn2_w_gate, m_ffn2_w_up, m_ffn2_w_down, m_ffn2_post_g, v_ffn1_pre_g, v_ffn1_w_gate, v_ffn1_w_up, v_ffn1_w_down, v_ffn1_post_g, v_mix_pre_g, v_w_in, v_mla_q_norm_g, v_mla_w_uq, v_mla_kv_norm_g, v_mla_w_ukv, v_mla_out_g, v_gdn_conv_w, v_gdn_a_log, v_gdn_dt_bias, v_gdn_norm_g, v_w_out, v_mix_post_g, v_ffn2_pre_g, v_ffn2_w_gate, v_ffn2_w_up, v_ffn2_w_down, v_ffn2_post_g):
    args = dict(locals())
    wsh = {n: args[n][0] for n in WEIGHTS}
    msh = {n: args["m_" + n] for n in SMALL}
    vsh = {n: args["v_" + n] for n in SMALL}
    for n in SMALL:
        wsh[n] = args[n]
    mix_shapes = [(n, wsh[n].shape) for n in MIX_BIG]

    early = FFN_BIG[:3]
    held = lambda a, n: jnp.swapaxes(a, 1, 2) if n in TRANSPOSED else a
    w_in_wire = jnp.pad(held(w_in, "w_in")[0].astype(MM_DTYPE), ((0, W_IN_SHARD_PAD - W_IN_SHARD), (0, 0)))
    gathered = _gather_shards([_to_wire("wire_" + n, held(args[n], n)) for n in early]
                              + [w_in_wire, _pack([wsh[n] for n in MIX_BIG], MM_DTYPE)])
    full = {n: wsh[n] for n in SMALL}
    for n, gw in zip(early + ["w_in"], gathered):
        full[n] = gw
    parts = [_unpack(gathered[-1][q], mix_shapes) for q in range(N_SHARD)]
    for n in MIX_BIG:
        full[n] = jnp.concatenate([parts[q][n] for q in range(N_SHARD)], axis=SHARD_AXIS[n])

    core = lax.axis_index("c").astype(jnp.int32).reshape(1)
    chip = (2 * lax.axis_index("x") + lax.axis_index("y")).astype(jnp.int32).reshape(1)
    late = ([_to_wire("wire_" + n, held(args[n], n)) for n in FFN2_BIG], core, chip)
    lsum, grad_x, g = _local_step(x[0], positions, loss_target[0], full, late)
    loss = lax.psum(0.5 * jnp.sum(lsum) / x.shape[-1], ("x", "y", "c"))

    halves = [g[n] for n in FFN_BIG] + [g["w_in"], g["mix_pack"]]
    others = _share_halves(halves)
    shared = [_join_halves("join_halves_%d" % i, hm, ho, core) for i, (hm, ho) in enumerate(zip(halves, others))]
    gsh = _unpack(shared[-1], mix_shapes)
    for n, sg_ in zip(FFN_BIG, shared):
        gsh[n] = sg_
    gsh["w_in"] = shared[-2][:, :W_IN_SHARD]

    small_shapes = [(n, wsh[n].shape) for n in SMALL]
    pack_small = lambda d: jnp.concatenate(
        [_pad_lanes(d[n].astype(F32), LANES) for n in SMALL] + [jnp.zeros((SMALL_ROWS - len(SMALL), LANES), F32)], axis=0)
    slots = _gather_small(pack_small(g))

    c1 = 1.0 - ADAM_B1 ** ADAM_STEP
    c2 = 1.0 - ADAM_B2 ** ADAM_STEP

    def small_update(wb, mb, vb, s8):
        gs = s8[0:SMALL_ROWS]
        for d in range(1, 8):
            gs = gs + s8[d * SMALL_ROWS:(d + 1) * SMALL_ROWS]
        m2 = ADAM_B1 * mb + (1.0 - ADAM_B1) * gs
        v2 = ADAM_B2 * vb + (1.0 - ADAM_B2) * (gs * gs)
        delta = -ADAM_LR * ((m2 / c1) / (jnp.sqrt(v2 / c2) + ADAM_EPS) + ADAM_WD * wb)
        return gs, delta, m2, v2

    sg, sd, sm, sv_ = _rowwise("adamw_small", small_update,
                               [pack_small(wsh), pack_small(msh), pack_small(vsh)],
                               [slots.reshape(8 * SMALL_ROWS, LANES)], [(LANES, F32)] * 4, [], SMALL_ROWS)
    grads, deltas, new_m, new_v = {}, {}, {}, {}
    for i, (n, shp) in enumerate(small_shapes):
        grads[n], deltas[n] = sg[i:i + 1, :shp[1]], sd[i:i + 1, :shp[1]]
        new_m[n], new_v[n] = sm[i:i + 1, :shp[1]], sv_[i:i + 1, :shp[1]]
    for n in BIG:
        w3 = held(args[n], n)
        outs = _adamw("adamw_" + n, w3, gsh[n], held(args["m_" + n], n), held(args["v_" + n], n),
                      _row_tile(w3.shape[1], 256))
        grads[n], deltas[n], new_m[n], new_v[n] = [held(o, n) for o in outs]

    return (loss, grad_x[None], *[grads[n] for n in WEIGHTS], *[deltas[n] for n in WEIGHTS],
            *[new_m[n] for n in WEIGHTS], *[new_v[n] for n in WEIGHTS])
```

```python
import functools

import jax
import jax.numpy as jnp
from jax import lax
from jax.experimental import pallas as pl
from jax.experimental.pallas import tpu as pltpu

F32 = jnp.float32
BF16 = jnp.bfloat16
MM_DTYPE = BF16
MESH = pl.DeviceIdType.MESH

D_MODEL = 1024
D_FF = 2816
N_HEADS = 8
MLA_Q_RANK = 256
MLA_KV_RANK = 128
MLA_NOPE = 64
MLA_ROPE = 32
MLA_V = 64
ROPE_THETA = 10000.0
GDN_DH = 64
GDN_W = N_HEADS * GDN_DH
GDN_CONV = 4
CHUNK = 64
GDN_STEP_CHUNKS = 4
HEAD_LANES = 128
HEADS_PER_STEP = 4
MLA_PAD = N_HEADS * HEAD_LANES
EPS = 1e-6
N_SHARD = 4
LANES = 1024

PIN_QKV = 0
PIN_MLA = 1536
PIN_KPE = 1920
PIN_GATE = 2048
PIN_AB = 2560
PIN_W = 2688
CAT_W = MLA_PAD + GDN_W

ADAM_LR = 0.001
ADAM_B1 = 0.9
ADAM_B2 = 0.999
ADAM_EPS = 1e-08
ADAM_WD = 0.01
ADAM_STEP = 10

VMEM_LIMIT_V7X = 56 * 1024 * 1024

BIG = ["ffn1_w_gate", "ffn1_w_up", "ffn1_w_down", "w_in", "mla_w_uq", "mla_w_ukv", "gdn_conv_w", "w_out",
       "ffn2_w_gate", "ffn2_w_up", "ffn2_w_down"]
FFN_BIG = ["ffn1_w_gate", "ffn1_w_up", "ffn1_w_down", "ffn2_w_gate", "ffn2_w_up", "ffn2_w_down"]
TRANSPOSED = ["ffn1_w_gate", "ffn1_w_up", "ffn2_w_gate", "ffn2_w_up", "w_in"]
MIX_BIG = ["mla_w_uq", "mla_w_ukv", "gdn_conv_w", "w_out"]
SMALL = ["ffn1_pre_g", "ffn1_post_g", "mix_pre_g", "mla_q_norm_g", "mla_kv_norm_g", "mla_out_g", "gdn_a_log",
         "gdn_dt_bias", "gdn_norm_g", "mix_post_g", "ffn2_pre_g", "ffn2_post_g"]
WEIGHTS = ["ffn1_pre_g", "ffn1_w_gate", "ffn1_w_up", "ffn1_w_down", "ffn1_post_g", "mix_pre_g", "w_in",
           "mla_q_norm_g", "mla_w_uq", "mla_kv_norm_g", "mla_w_ukv", "mla_out_g", "gdn_conv_w", "gdn_a_log",
           "gdn_dt_bias", "gdn_norm_g", "w_out", "mix_post_g", "ffn2_pre_g", "ffn2_w_gate", "ffn2_w_up",
           "ffn2_w_down", "ffn2_post_g"]
SHARD_AXIS = {"ffn1_w_gate": 1, "ffn1_w_up": 1, "ffn1_w_down": 0, "w_in": 1, "mla_w_uq": 1, "mla_w_ukv": 1,
              "gdn_conv_w": 1, "w_out": 0, "ffn2_w_gate": 1, "ffn2_w_up": 1, "ffn2_w_down": 0}
SMALL_ROWS = 16


def _params(sem):
    return pltpu.CompilerParams(dimension_semantics=sem, vmem_limit_bytes=VMEM_LIMIT_V7X)


def _pick(dim, pref):
    if dim <= pref:
        return dim
    t = (pref // 128) * 128
    while t >= 128:
        if dim % t == 0:
            return t
        t -= 128
    return dim


ANY_SPEC = pl.BlockSpec(memory_space=pl.ANY)


def _rowwise(name, fn, row_ins, bc_ins, row_outs, acc_outs, tb, wide=None, carry=None):
    ents = []
    for e in row_ins:
        ents.append(e if isinstance(e, tuple) else (e, e.shape[1], 0, 0))
    over = [o[2] for o in row_outs if len(o) == 3]
    rows = over[0] if over else ents[0][0].shape[0]
    steps = rows // tb
    assert steps * tb == rows, (name, rows, tb)
    in_specs, args = [], []
    for a, w, j, r0 in ents:
        in_specs.append(pl.BlockSpec((tb, w), lambda i, j=j, r0=r0: (i + r0, j)))
        args.append(a)
    for b in bc_ins:
        in_specs.append(pl.BlockSpec(b.shape, lambda i: (0, 0)))
        args.append(b)
    n_in = len(args)
    aliases = {}
    if carry is not None:
        in_specs.append(ANY_SPEC)
        args.append(carry)
        aliases = {n_in: 0}
    out_shape = [jax.ShapeDtypeStruct((rows, o[0]), o[1]) for o in row_outs]
    out_specs = [pl.BlockSpec((tb, o[0]), lambda i: (i, 0)) for o in row_outs]
    if wide is not None:
        out_shape[0] = jax.ShapeDtypeStruct((rows, wide[0]), row_outs[0][1])
        out_specs[0] = pl.BlockSpec((tb, row_outs[0][0]), lambda i: (i, wide[1]))
    out_shape += [jax.ShapeDtypeStruct((r, c), F32) for r, c in acc_outs]
    out_specs += [pl.BlockSpec((r, c), lambda i: (0, 0)) for r, c in acc_outs]
    n_ro, n_acc, n_args = len(row_outs), len(acc_outs), len(args)

    def body(*refs):
        vals = fn(*[r[...] for r in refs[:n_in]])
        if not isinstance(vals, (tuple, list)):
            vals = (vals,)
        for r, v in zip(refs[n_args:n_args + n_ro], vals[:n_ro]):
            r[...] = v.astype(r.dtype)
        if n_acc:
            acc_refs = refs[n_args + n_ro:]

            @pl.when(pl.program_id(0) == 0)
            def _():
                for r in acc_refs:
                    r[...] = jnp.zeros(r.shape, r.dtype)

            for r, v in zip(acc_refs, vals[n_ro:]):
                r[...] += v

    outs = pl.pallas_call(body, name=name, grid=(steps,), in_specs=in_specs, out_specs=out_specs,
                          out_shape=out_shape, input_output_aliases=aliases,
                          compiler_params=_params(("arbitrary",)))(*args)
    return outs


def _mm(name, a, b, mode, out_dtype, tm=1024, tn=1024, tk=1024):
    if mode == "nn":
        (m, k), (k2, n) = a.shape, b.shape
    elif mode == "nt":
        (m, k), (n, k2) = a.shape, b.shape
    else:
        (k, m), (k2, n) = a.shape, b.shape
    assert k == k2, (name, a.shape, b.shape)
    tm, tn, tk = _pick(m, tm), _pick(n, tn), _pick(k, tk)
    nk = k // tk
    if mode == "nn":
        a_spec = pl.BlockSpec((tm, tk), lambda i, j, kk: (i, kk))
        b_spec = pl.BlockSpec((tk, tn), lambda i, j, kk: (kk, j))
        dims = (((1,), (0,)), ((), ()))
    elif mode == "nt":
        a_spec = pl.BlockSpec((tm, tk), lambda i, j, kk: (i, kk))
        b_spec = pl.BlockSpec((tn, tk), lambda i, j, kk: (j, kk))
        dims = (((1,), (1,)), ((), ()))
    else:
        a_spec = pl.BlockSpec((tk, tm), lambda i, j, kk: (kk, i))
        b_spec = pl.BlockSpec((tk, tn), lambda i, j, kk: (kk, j))
        dims = (((0,), (0,)), ((), ()))

    def body(a_ref, b_ref, o_ref, acc_ref):
        kk = pl.program_id(2)

        @pl.when(kk == 0)
        def _():
            acc_ref[...] = jnp.zeros(acc_ref.shape, F32)

        acc_ref[...] += lax.dot_general(a_ref[...].astype(MM_DTYPE), b_ref[...].astype(MM_DTYPE), dims,
                                        preferred_element_type=F32)

        @pl.when(kk == nk - 1)
        def _():
            o_ref[...] = acc_ref[...].astype(o_ref.dtype)

    return pl.pallas_call(
        body, name=name, grid=(m // tm, n // tn, nk), in_specs=[a_spec, b_spec],
        out_specs=pl.BlockSpec((tm, tn), lambda i, j, kk: (i, j)),
        out_shape=jax.ShapeDtypeStruct((m, n), out_dtype),
        scratch_shapes=[pltpu.VMEM((tm, tn), F32)],
        compiler_params=_params(("parallel", "parallel", "arbitrary")))(a, b)


def _prologue_mm(name, row_ins, g, prologue, w, with_gain_grad, tm=1024, tn=1024):
    t, d = row_ins[0].shape
    n = w.shape[0]
    tm, tn = _pick(t, tm), _pick(n, tn)
    n_row = len(row_ins)
    row = pl.BlockSpec((tm, d), lambda i, j: (i, 0))
    vec = pl.BlockSpec((1, d), lambda i, j: (0, 0))

    def body(*refs):
        rows, g_ref, w_ref = refs[:n_row], refs[n_row], refs[n_row + 1]
        lhs_ref, out_ref = refs[n_row + 2], refs[n_row + 3]
        lhs_s = refs[-1]
        i, j = pl.program_id(0), pl.program_id(1)
        if with_gain_grad:
            dg_ref = refs[n_row + 4]

            @pl.when((i == 0) & (j == 0))
            def _():
                dg_ref[...] = jnp.zeros(dg_ref.shape, F32)

        @pl.when(j == 0)
        def _():
            res = prologue(*[r[...] for r in rows], g_ref[...])
            lhs_s[...] = res[0].astype(MM_DTYPE)
            lhs_ref[...] = lhs_s[...]
            if with_gain_grad:
                dg_ref[...] += res[1]

        out_ref[...] = lax.dot_general(lhs_s[...], w_ref[...], (((1,), (1,)), ((), ())), preferred_element_type=F32)

    out_specs = [row, pl.BlockSpec((tm, tn), lambda i, j: (i, j))] + ([vec] if with_gain_grad else [])
    out_shape = [jax.ShapeDtypeStruct((t, d), MM_DTYPE), jax.ShapeDtypeStruct((t, n), F32)]
    out_shape += [jax.ShapeDtypeStruct((1, d), F32)] if with_gain_grad else []
    return pl.pallas_call(
        body, name=name, grid=(t // tm, n // tn),
        in_specs=[row] * n_row + [vec, pl.BlockSpec((tn, d), lambda i, j: (j, 0))], out_specs=out_specs,
        out_shape=out_shape, scratch_shapes=[pltpu.VMEM((tm, d), MM_DTYPE)],
        compiler_params=_params(("arbitrary", "arbitrary")))(*row_ins, g, w)


def _mm_epilogue(name, a, w, row_ins, g, epilogue, n_row_out, with_gain_grad, tm=1024, tk=1024):
    t, k = a.shape
    d = w.shape[1]
    tm, tk = _pick(t, tm), _pick(k, tk)
    nk = k // tk
    n_row = len(row_ins)
    row = pl.BlockSpec((tm, d), lambda i, kk: (i, 0))
    vec = pl.BlockSpec((1, d), lambda i, kk: (0, 0))

    def body(*refs):
        a_ref, w_ref = refs[0], refs[1]
        rows, g_ref = refs[2:2 + n_row], refs[2 + n_row]
        outs = refs[3 + n_row:3 + n_row + n_row_out]
        acc = refs[-1]
        i, kk = pl.program_id(0), pl.program_id(1)
        if with_gain_grad:
            dg_ref = refs[3 + n_row + n_row_out]

            @pl.when((i == 0) & (kk == 0))
            def _():
                dg_ref[...] = jnp.zeros(dg_ref.shape, F32)

        @pl.when(kk == 0)
        def _():
            acc[...] = jnp.zeros(acc.shape, F32)

        acc[...] += jnp.dot(a_ref[...], w_ref[...], preferred_element_type=F32)

        @pl.when(kk == nk - 1)
        def _():
            res = epilogue(acc[...], *[r[...] for r in rows], g_ref[...])
            for o_ref, val in zip(outs, res[:n_row_out]):
                o_ref[...] = val
            if with_gain_grad:
                dg_ref[...] += res[n_row_out]

    out_specs = [row] * n_row_out + ([vec] if with_gain_grad else [])
    out_shape = [jax.ShapeDtypeStruct((t, d), F32)] * n_row_out
    out_shape += [jax.ShapeDtypeStruct((1, d), F32)] if with_gain_grad else []
    return pl.pallas_call(
        body, name=name, grid=(t // tm, nk),
        in_specs=[pl.BlockSpec((tm, tk), lambda i, kk: (i, kk)), pl.BlockSpec((tk, d), lambda i, kk: (kk, 0))]
        + [row] * n_row + [vec],
        out_specs=out_specs, out_shape=out_shape, scratch_shapes=[pltpu.VMEM((tm, d), F32)],
        compiler_params=_params(("arbitrary", "arbitrary")))(a, w, *row_ins, g)


def _rms_stats(x, n_real=None):
    n = x.shape[-1] if n_real is None else n_real
    return lax.rsqrt(jnp.sum(x * x, axis=-1, keepdims=True) / n + EPS)


def _rms_bwd(x, r, g, dz, n_real=None):
    n = x.shape[-1] if n_real is None else n_real
    xh = x * r
    dxh = dz * g
    dx = r * (dxh - xh * (jnp.sum(dxh * xh, axis=-1, keepdims=True) / n))
    return dx, jnp.sum(dz * xh, axis=0, keepdims=True)


def _sigmoid(x):
    return 0.5 * jnp.tanh(0.5 * x) + 0.5


def _roll(x, s, axis):
    return pltpu.roll(x, s, axis)


def _rope(x, c, s1, s2):
    return x * c + _roll(x, HEAD_LANES - MLA_ROPE // 2, 1) * s1 + _roll(x, MLA_ROPE // 2, 1) * s2


def _heads_apply(x, fn):
    return jnp.concatenate([fn(x[:, h * HEAD_LANES:(h + 1) * HEAD_LANES]) for h in range(N_HEADS)], axis=1)


ROW_CHUNK = 256


def _row_chunks(rows):
    step = min(ROW_CHUNK, rows)
    return [pl.ds(r, step) for r in range(0, rows, step)]


def _ffn_fwd(tag, x, g_pre, wg, wu, wd, g_post, tm, loss_target=None):
    t, d = x.shape
    ns, fs, _ = wg.shape
    nt = t // tm
    row = pl.BlockSpec((tm, d), lambda i, q: (i, 0))
    vec = pl.BlockSpec((1, d), lambda i, q: (0, 0))
    act3 = pl.BlockSpec((1, tm, fs), lambda i, q: (q, i, 0))
    wrow = pl.BlockSpec((1, fs, d), lambda i, q: (q, 0, 0))
    nt_dims = (((1,), (1,)), ((), ()))

    def gate_up(x_ref, g_ref, wg_ref, wu_ref, n_ref, sl_ref, ud_ref, s_ref, n_s):
        @pl.when(pl.program_id(1) == 0)
        def _():
            for r in _row_chunks(tm):
                xb = x_ref[r, :]
                n_s[r, :] = (xb * _rms_stats(xb) * g_ref[...]).astype(MM_DTYPE)
            n_ref[...] = n_s[...]

        for r in _row_chunks(tm):
            n = n_s[r, :]
            a = lax.dot_general(n, wg_ref[0], nt_dims, preferred_element_type=F32)
            u = lax.dot_general(n, wu_ref[0], nt_dims, preferred_element_type=F32)
            sg = _sigmoid(a)
            sl = a * sg
            sl_ref[0, r, :] = sl.astype(sl_ref.dtype)
            ud_ref[0, r, :] = (u * (sg + sl * (1.0 - sg))).astype(ud_ref.dtype)
            s_ref[0, r, :] = (sl * u).astype(s_ref.dtype)

    n, sl, ud, s = pl.pallas_call(
        gate_up, name=tag + "_gate_up", grid=(nt, ns), in_specs=[row, vec, wrow, wrow],
        out_specs=[row, act3, act3, act3],
        out_shape=[jax.ShapeDtypeStruct((t, d), MM_DTYPE)] + [jax.ShapeDtypeStruct((ns, t, fs), MM_DTYPE)] * 3,
        scratch_shapes=[pltpu.VMEM((tm, d), MM_DTYPE)],
        compiler_params=_params(("parallel", "arbitrary")))(x, g_pre, wg, wu)

    def down(s_ref, wd_ref, x_ref, g_ref, *rest):
        if loss_target is not None:
            tgt_ref, h_ref, y_ref, ls_ref, acc = rest
        else:
            h_ref, y_ref, acc = rest
        i, q = pl.program_id(0), pl.program_id(1)

        @pl.when(q == 0)
        def _():
            acc[...] = jnp.zeros(acc.shape, F32)

        if loss_target is not None:
            @pl.when((i == 0) & (q == 0))
            def _():
                ls_ref[...] = jnp.zeros(ls_ref.shape, F32)

        for r in _row_chunks(tm):
            acc[r, :] += jnp.dot(s_ref[0, r, :], wd_ref[0], preferred_element_type=F32)

        @pl.when(q == ns - 1)
        def _():
            for r in _row_chunks(tm):
                hb = acc[r, :]
                h_ref[r, :] = hb
                yb = x_ref[r, :] + 0.5 * (hb * _rms_stats(hb) * g_ref[...])
                if loss_target is None:
                    y_ref[r, :] = yb
                else:
                    e = yb - tgt_ref[r, :]
                    y_ref[r, :] = e * (1.0 / d)
                    ls_ref[...] += jnp.sum(e * e, axis=0, keepdims=True)

    with_loss = loss_target is not None
    outs = pl.pallas_call(
        down, name=tag + "_down", grid=(nt, ns), in_specs=[act3, wrow, row, vec] + ([row] if with_loss else []),
        out_specs=[row, row] + ([vec] if with_loss else []),
        out_shape=[jax.ShapeDtypeStruct((t, d), F32)] * 2 + ([jax.ShapeDtypeStruct((1, d), F32)] if with_loss else []),
        scratch_shapes=[pltpu.VMEM((tm, d), F32)],
        compiler_params=_params(("arbitrary", "arbitrary")))(s, wd, x, g_post, *([loss_target] if with_loss else []))
    if with_loss:
        h, dy, lsum = outs
        return (dy, lsum), (x, n, sl, ud, s, h)
    h, y = outs
    return y, (x, n, sl, ud, s, h)


def _carry(body, n_in, n_out, grid, carried):
    if carried is None:
        return body, [], [], [], [], []
    nx_in, nx_out = len(carried.ins), len(carried.outs)

    def wrapped(*refs):
        ins, rest = refs[:n_in], refs[n_in:]
        xi, rest = rest[:nx_in], rest[nx_in:]
        outs, rest = rest[:n_out], rest[n_out:]
        xo, rest = rest[:nx_out], rest[nx_out:]
        scr, sems = rest[:len(rest) - 2], rest[len(rest) - 2:]
        first, last = True, True
        for dim, size in enumerate(grid):
            first = first & (pl.program_id(dim) == 0)
            last = last & (pl.program_id(dim) == size - 1)

        @pl.when(first)
        def _():
            carried.start(xi, xo, *sems)

        body(*ins, *outs, *scr)

        @pl.when(last)
        def _():
            carried.finish(xi, xo, *sems)

    sems = [pltpu.SemaphoreType.DMA((carried.n_sem,)), pltpu.SemaphoreType.DMA((carried.n_sem,))]
    return (wrapped, [HBM_SPEC] * nx_in, [HBM_SPEC] * nx_out, list(carried.outs), sems, list(carried.ins))


def _ffn_bwd(tag, dy, saved, g_pre, wg, wu, wd, g_post, tm, tk, carried_down=None, make_carried_mid=None,
             make_carried_up=None):
    x, n, sl, ud, s, h = saved
    t, d = x.shape
    ns, fs, _ = wg.shape
    nt, nk = t // tm, t // tk
    row = pl.BlockSpec((tm, d), lambda i, q: (i, 0))
    vec = pl.BlockSpec((1, d), lambda i, q: (0, 0))
    act3 = pl.BlockSpec((1, tm, fs), lambda i, q: (q, i, 0))
    wrow = pl.BlockSpec((1, fs, d), lambda i, q: (q, 0, 0))
    nt_dims = (((1,), (1,)), ((), ()))
    tn_dims = (((0,), (0,)), ((), ()))

    def down_b(h_ref, dy_ref, g_ref, wd_ref, sl_ref, ud_ref, dh_ref, da_ref, du_ref, dg_ref, dh_s):
        i, q = pl.program_id(0), pl.program_id(1)

        @pl.when((i == 0) & (q == 0))
        def _():
            dg_ref[...] = jnp.zeros(dg_ref.shape, F32)

        @pl.when(q == 0)
        def _():
            for r in _row_chunks(tm):
                hb = h_ref[r, :]
                dh, dg = _rms_bwd(hb, _rms_stats(hb), g_ref[...], 0.5 * dy_ref[r, :])
                dh_s[r, :] = dh.astype(MM_DTYPE)
                dg_ref[...] += dg
            dh_ref[...] = dh_s[...]

        for r in _row_chunks(tm):
            ds = lax.dot_general(dh_s[r, :], wd_ref[0], nt_dims, preferred_element_type=F32)
            da_ref[0, r, :] = (ds * ud_ref[0, r, :].astype(F32)).astype(da_ref.dtype)
            du_ref[0, r, :] = (ds * sl_ref[0, r, :].astype(F32)).astype(du_ref.dtype)

    down_b, x_in, x_out, x_shape, x_scr, x_args = _carry(down_b, 6, 4, (nt, ns), carried_down)
    dh, da, du, dg_post, *from_down = pl.pallas_call(
        down_b, name=tag + "_down_b", grid=(nt, ns), in_specs=[row, row, vec, wrow, act3, act3] + x_in,
        out_specs=[row, act3, act3, vec] + x_out,
        out_shape=[jax.ShapeDtypeStruct((t, d), MM_DTYPE)] + [jax.ShapeDtypeStruct((ns, t, fs), MM_DTYPE)] * 2
        + [jax.ShapeDtypeStruct((1, d), F32)] + x_shape,
        scratch_shapes=[pltpu.VMEM((tm, d), MM_DTYPE)] + x_scr,
        compiler_params=_params(("arbitrary", "arbitrary")))(h, dy, g_post, wd, sl, ud, *x_args)

    def down_w(s_ref, dh_ref, dw_ref, acc):
        kk = pl.program_id(1)

        @pl.when(kk == 0)
        def _():
            acc[...] = jnp.zeros(acc.shape, F32)

        acc[...] += lax.dot_general(s_ref[0], dh_ref[...], tn_dims, preferred_element_type=F32)

        @pl.when(kk == nk - 1)
        def _():
            dw_ref[0] = acc[...].astype(dw_ref.dtype)

    dwd = pl.pallas_call(
        down_w, name=tag + "_down_w", grid=(ns, nk),
        in_specs=[pl.BlockSpec((1, tk, fs), lambda q, kk: (q, kk, 0)), pl.BlockSpec((tk, d), lambda q, kk: (kk, 0))],
        out_specs=pl.BlockSpec((1, fs, d), lambda q, kk: (q, 0, 0)),
        out_shape=jax.ShapeDtypeStruct((ns, fs, d), MM_DTYPE), scratch_shapes=[pltpu.VMEM((fs, d), F32)],
        compiler_params=_params(("parallel", "arbitrary")))(s, dh)

    def gate_up_b(da_ref, du_ref, wg_ref, wu_ref, x_ref, dy_ref, g_ref, dx_ref, dg_ref, acc):
        i, q = pl.program_id(0), pl.program_id(1)

        @pl.when((i == 0) & (q == 0))
        def _():
            dg_ref[...] = jnp.zeros(dg_ref.shape, F32)

        @pl.when(q == 0)
        def _():
            acc[...] = jnp.zeros(acc.shape, F32)

        for r in _row_chunks(tm):
            acc[r, :] += (jnp.dot(da_ref[0, r, :], wg_ref[0], preferred_element_type=F32)
                          + jnp.dot(du_ref[0, r, :], wu_ref[0], preferred_element_type=F32))

        @pl.when(q == ns - 1)
        def _():
            for r in _row_chunks(tm):
                xb = x_ref[r, :]
                dx, dg = _rms_bwd(xb, _rms_stats(xb), g_ref[...], acc[r, :])
                dx_ref[r, :] = dy_ref[r, :] + dx
                dg_ref[...] += dg

    def gate_up_w(n_ref, da_ref, du_ref, dwg_ref, dwu_ref, acc_g, acc_u):
        kk = pl.program_id(1)

        @pl.when(kk == 0)
        def _():
            acc_g[...] = jnp.zeros(acc_g.shape, F32)
            acc_u[...] = jnp.zeros(acc_u.shape, F32)

        nb = n_ref[...]
        acc_g[...] += lax.dot_general(da_ref[0], nb, tn_dims, preferred_element_type=F32)
        acc_u[...] += lax.dot_general(du_ref[0], nb, tn_dims, preferred_element_type=F32)

        @pl.when(kk == nk - 1)
        def _():
            dwg_ref[0] = acc_g[...].astype(dwg_ref.dtype)
            dwu_ref[0] = acc_u[...].astype(dwu_ref.dtype)

    k3 = pl.BlockSpec((1, tk, fs), lambda q, kk: (q, kk, 0))
    wout = pl.BlockSpec((1, fs, d), lambda q, kk: (q, 0, 0))
    carried_mid = make_carried_mid(dwd) if make_carried_mid else None
    gate_up_w, x_in, x_out, x_shape, x_scr, x_args = _carry(gate_up_w, 3, 2, (ns, nk), carried_mid)
    dwg, dwu, *from_mid = pl.pallas_call(
        gate_up_w, name=tag + "_gate_up_w", grid=(ns, nk),
        in_specs=[pl.BlockSpec((tk, d), lambda q, kk: (kk, 0)), k3, k3] + x_in, out_specs=[wout, wout] + x_out,
        out_shape=[jax.ShapeDtypeStruct((ns, fs, d), MM_DTYPE)] * 2 + x_shape,
        scratch_shapes=[pltpu.VMEM((fs, d), F32)] * 2 + x_scr,
        compiler_params=_params(("arbitrary", "arbitrary")))(n, da, du, *x_args)

    carried_up = make_carried_up(dwg, dwu) if make_carried_up else None
    gate_up_b, x_in, x_out, x_shape, x_scr, x_args = _carry(gate_up_b, 7, 2, (nt, ns), carried_up)
    dx, dg_pre, *from_up = pl.pallas_call(
        gate_up_b, name=tag + "_gate_up_b", grid=(nt, ns), in_specs=[act3, act3, wrow, wrow, row, row, vec] + x_in,
        out_specs=[row, vec] + x_out,
        out_shape=[jax.ShapeDtypeStruct((t, d), F32), jax.ShapeDtypeStruct((1, d), F32)] + x_shape,
        scratch_shapes=[pltpu.VMEM((tm, d), F32)] + x_scr,
        compiler_params=_params(("arbitrary", "arbitrary")))(da, du, wg, wu, x, dy, g_pre, *x_args)
    return dx, dg_pre, dwg, dwu, dwd, dg_post, from_down, from_mid, from_up


NEG = -1e30


def _attn_scale():
    return (MLA_NOPE + MLA_ROPE) ** -0.5


def _causal_pairs(nq, by_key):
    if by_key:
        pairs = [(qi, ki) for ki in range(nq) for qi in range(ki, nq)]
    else:
        pairs = [(qi, ki) for qi in range(nq) for ki in range(qi + 1)]
    return jnp.asarray([p[0] for p in pairs], jnp.int32), jnp.asarray([p[1] for p in pairs], jnp.int32)


def _below_diagonal(shape):
    return lax.broadcasted_iota(jnp.int32, shape, 1) <= lax.broadcasted_iota(jnp.int32, shape, 0)


def _attn_call(name, body, tables, args, in_kinds, out_kinds, scratch, t, tq, carried=None):
    qmap = lambda h, p, qt, kt: (qt[p], h)
    kmap = lambda h, p, qt, kt: (kt[p], h)
    width = HEADS_PER_STEP * HEAD_LANES
    spec = lambda kind: pl.BlockSpec((tq, width), qmap if kind == "q" else kmap)
    n_pairs = tables[0].shape[0]
    n_groups = N_HEADS // HEADS_PER_STEP
    n_in, n_out, n_scr = len(in_kinds), len(out_kinds), scratch
    x_ins = list(carried.ins) if carried else []
    x_outs = list(carried.outs) if carried else []
    x_scr = [pltpu.SemaphoreType.DMA((carried.n_sem,)), pltpu.SemaphoreType.DMA((carried.n_sem,))] if carried else []

    def full_body(qt, kt, *refs):
        ins, refs = refs[:n_in], refs[n_in:]
        xi, refs = refs[:len(x_ins)], refs[len(x_ins):]
        outs, refs = refs[:n_out], refs[n_out:]
        xo, refs = refs[:len(x_outs)], refs[len(x_outs):]
        scr, sems = refs[:n_scr], refs[n_scr:]
        if carried:
            @pl.when((pl.program_id(0) == 0) & (pl.program_id(1) == 0))
            def _():
                carried.start(xi, xo, *sems)

        heads = [tuple(r.at[:, pl.ds(hh * HEAD_LANES, HEAD_LANES)] for r in (*ins, *outs, *scr))
                 for hh in range(HEADS_PER_STEP)]
        body(qt, kt, heads)
        if carried:
            @pl.when((pl.program_id(0) == n_groups - 1) & (pl.program_id(1) == n_pairs - 1))
            def _():
                carried.finish(xi, xo, *sems)

    grid_spec = pltpu.PrefetchScalarGridSpec(
        num_scalar_prefetch=2, grid=(n_groups, n_pairs),
        in_specs=[spec(kd) for kd in in_kinds] + [HBM_SPEC] * len(x_ins),
        out_specs=[spec(kd) for kd in out_kinds] + [HBM_SPEC] * len(x_outs),
        scratch_shapes=[pltpu.VMEM((tq, width), F32)] * n_scr + x_scr)
    return pl.pallas_call(full_body, name=name, grid_spec=grid_spec,
                          out_shape=[jax.ShapeDtypeStruct((t, MLA_PAD), F32) for _ in out_kinds] + x_outs,
                          compiler_params=_params(("arbitrary", "arbitrary")))(*tables, *args, *x_ins)


class _Carried:
    def __init__(self, ins, outs, n_sem, start, finish):
        self.ins, self.outs, self.n_sem, self.start, self.finish = ins, outs, n_sem, start, finish


def _attn_fwd(q, k, v, tq, carried=None):
    t = q.shape[0]
    nq = t // tq

    def body(qt, kt, heads):
        p_id = pl.program_id(1)
        qi, ki = qt[p_id], kt[p_id]

        @pl.when(ki == 0)
        def _():
            for _, _, _, _, _, m_s, l_s, acc_s in heads:
                m_s[...] = jnp.full(m_s.shape, NEG, F32)
                l_s[...] = jnp.zeros(l_s.shape, F32)
                acc_s[...] = jnp.zeros(acc_s.shape, F32)

        def update(diagonal):
            for q_ref, k_ref, v_ref, _, _, m_s, l_s, acc_s in heads:
                s = lax.dot_general(q_ref[...], k_ref[...], (((1,), (1,)), ((), ())), preferred_element_type=F32)
                if diagonal:
                    s = jnp.where(_below_diagonal(s.shape), s, NEG)
                m_old = m_s[...]
                m_new = jnp.maximum(m_old, jnp.max(s, axis=1, keepdims=True))
                alpha = jnp.exp(m_old - m_new)
                p = jnp.exp(s - m_new[:, :1])
                l_s[...] = l_s[...] * alpha + jnp.sum(p, axis=1, keepdims=True)
                acc_s[...] = acc_s[...] * alpha + jnp.dot(p.astype(MM_DTYPE), v_ref[...], preferred_element_type=F32)
                m_s[...] = m_new

        @pl.when(ki < qi)
        def _():
            update(False)

        @pl.when(ki == qi)
        def _():
            update(True)
            for _, _, _, o_ref, lse_ref, m_s, l_s, acc_s in heads:
                o_ref[...] = acc_s[...] / l_s[...]
                lse_ref[...] = m_s[...] + jnp.log(l_s[...])

    return _attn_call("mla_attn_fwd", body, _causal_pairs(nq, False), (q, k, v), "qkk", "qq", 3, t, tq, carried)


def _attn_probs(q, k, lse, diagonal):
    s = lax.dot_general(q, k, (((1,), (1,)), ((), ())), preferred_element_type=F32)
    p = jnp.exp(s - lse[:, :1])
    return jnp.where(_below_diagonal(s.shape), p, 0.0) if diagonal else p


BWD_HEADS = 2


def _attn_bwd(q, k, v, do, lse, delta, tq, carried=None):
    t = q.shape[0]
    nq = t // tq
    width = BWD_HEADS * HEAD_LANES
    n_groups = N_HEADS // BWD_HEADS
    qt_tab, kt_tab = _causal_pairs(nq, True)
    n_pairs = qt_tab.shape[0]
    qmap = lambda h, p, qt, kt: (qt[p], h)
    kmap = lambda h, p, qt, kt: (kt[p], h)
    qs, ks = pl.BlockSpec((tq, width), qmap), pl.BlockSpec((tq, width), kmap)
    x_ins = list(carried.ins) if carried else []
    x_outs = list(carried.outs) if carried else []
    x_scr = [pltpu.SemaphoreType.DMA((carried.n_sem,)), pltpu.SemaphoreType.DMA((carried.n_sem,))] if carried else []
    nt_dims = (((1,), (1,)), ((), ()))
    tn_dims = (((0,), (0,)), ((), ()))

    def body(qt, kt, q_ref, k_ref, v_ref, do_ref, lse_ref, dl_ref, *rest):
        xi, rest = rest[:len(x_ins)], rest[len(x_ins):]
        dq_hbm, dk_ref, dv_ref = rest[:3]
        xo, rest = rest[3:3 + len(x_outs)], rest[3 + len(x_outs):]
        dk_s, dv_s, dq_s, dq_sem = rest[:4]
        sems = rest[4:]
        grp, p_id = pl.program_id(0), pl.program_id(1)
        qi, ki = qt[p_id], kt[p_id]
        if carried:
            @pl.when((grp == 0) & (p_id == 0))
            def _():
                carried.start(xi, xo, *sems)

        @pl.when(p_id == 0)
        def _():
            dq_s[...] = jnp.zeros(dq_s.shape, F32)

        def step(diagonal):
            rows = pl.ds(pl.multiple_of(qi * tq, tq), tq)
            for hh in range(BWD_HEADS):
                ln = pl.ds(hh * HEAD_LANES, HEAD_LANES)
                qb, kb, vb, dob = q_ref[:, ln], k_ref[:, ln], v_ref[:, ln], do_ref[:, ln]
                p = _attn_probs(qb, kb, lse_ref[:, ln], diagonal)
                dv_s[:, ln] += lax.dot_general(p.astype(MM_DTYPE), dob, tn_dims, preferred_element_type=F32)
                dp = lax.dot_general(dob, vb, nt_dims, preferred_element_type=F32)
                ds = (p * (dp - dl_ref[:, ln][:, :1])).astype(MM_DTYPE)
                dk_s[:, ln] += lax.dot_general(ds, qb, tn_dims, preferred_element_type=F32)
                dq_s[rows, ln] += jnp.dot(ds, kb, preferred_element_type=F32)

        @pl.when(qi == ki)
        def _():
            dk_s[...] = jnp.zeros(dk_s.shape, F32)
            dv_s[...] = jnp.zeros(dv_s.shape, F32)
            step(True)

        @pl.when(qi > ki)
        def _():
            step(False)

        @pl.when(qi == nq - 1)
        def _():
            dk_ref[...] = dk_s[...]
            dv_ref[...] = dv_s[...]

        @pl.when(p_id == n_pairs - 1)
        def _():
            out = pltpu.make_async_copy(dq_s, dq_hbm.at[pl.ds(pl.multiple_of(grp * t, t), t)], dq_sem)
            out.start()
            out.wait()

        if carried:
            @pl.when((grp == n_groups - 1) & (p_id == n_pairs - 1))
            def _():
                carried.finish(xi, xo, *sems)

    grid_spec = pltpu.PrefetchScalarGridSpec(
        num_scalar_prefetch=2, grid=(n_groups, n_pairs),
        in_specs=[qs, ks, ks, qs, qs, qs] + [HBM_SPEC] * len(x_ins),
        out_specs=[HBM_SPEC, ks, ks] + [HBM_SPEC] * len(x_outs),
        scratch_shapes=[pltpu.VMEM((tq, width), F32), pltpu.VMEM((tq, width), F32), pltpu.VMEM((t, width), F32),
                        pltpu.SemaphoreType.DMA] + x_scr)
    return pl.pallas_call(
        body, name="mla_attn_bwd", grid_spec=grid_spec,
        out_shape=[jax.ShapeDtypeStruct((n_groups * t, width), F32), jax.ShapeDtypeStruct((t, MLA_PAD), F32),
                   jax.ShapeDtypeStruct((t, MLA_PAD), F32)] + x_outs,
        compiler_params=_params(("arbitrary", "arbitrary")))(qt_tab, kt_tab, q, k, v, do, lse, delta, *x_ins)


def _dot01(a, b, dims=(((1,), (0,)), ((), ())), ones="rhs"):
    val, sel = (a, b) if ones == "rhs" else (b, a)
    head = val.astype(BF16)
    tail = (val - head.astype(F32)).astype(BF16)
    sel = sel.astype(BF16)
    dot = lambda part: (lax.dot_general(part, sel, dims, preferred_element_type=F32) if ones == "rhs"
                        else lax.dot_general(sel, part, dims, preferred_element_type=F32))
    return dot(head) + dot(tail)


def _dot1(a, b, dims=(((1,), (0,)), ((), ()))):
    return lax.dot_general(a.astype(MM_DTYPE), b.astype(MM_DTYPE), dims, preferred_element_type=F32)


def _dot3(a, b, dims=(((1,), (0,)), ((), ()))):
    return lax.dot_general(a, b, dims, preferred_element_type=F32, precision=lax.Precision.HIGH)


NN3 = (((2,), (1,)), ((0,), (0,)))
NT3 = (((2,), (2,)), ((0,), (0,)))
TN3 = (((1,), (1,)), ((0,), (0,)))


def _tri_masks(nh):
    shape = (nh, CHUNK, CHUNK)
    return lax.broadcasted_iota(jnp.int32, shape, 1), lax.broadcasted_iota(jnp.int32, shape, 2)


def _gdn_chunk_common(k, gcc, bb, row, col, dot=_dot1):
    tril = row >= col
    gcr = jnp.swapaxes(gcc, 1, 2)
    dm = jnp.exp(jnp.where(tril, gcc - gcr, NEG))
    kb = k * bb
    lm = jnp.where(row > col, dot(kb, k, NT3) * dm, 0.0)
    return dm, kb, lm


def _unit_lower_inverse(lm, eye):
    t = eye - lm
    p = lm
    for _ in range(CHUNK.bit_length() - 2):
        p = _dot3(p, p, NN3)
        t = t + _dot3(t, p, NN3)
    return t


def _chunk_sum_matrix(tb, upper):
    r = lax.broadcasted_iota(jnp.int32, (tb, tb), 0)
    c = lax.broadcasted_iota(jnp.int32, (tb, tb), 1)
    same = (r // CHUNK) == (c // CHUNK)
    return (same & ((c >= r) if upper else (c <= r))).astype(F32)


def _gdn_fwd(q, k, v, gb, bb, carried=None):
    nh, t, dh = q.shape
    nchunk = t // CHUNK

    def body(q_ref, k_ref, v_ref, g_ref, b_ref, o_ref, sall_ref, tall_ref, s_s):
        @pl.when(pl.program_id(0) == 0)
        def _():
            s_s[...] = jnp.zeros(s_s.shape, F32)

        row, col = _tri_masks(nh)
        sh = s_s[...]
        for cc in range(cps):
            rows = pl.ds(cc * CHUNK, CHUNK)
            qh, kh, vh, bbh, gcc = q_ref[:, rows, :], k_ref[:, rows, :], v_ref[:, rows, :], b_ref[:, rows, :], \
                g_ref[:, rows, :]
            dm, kb, lm = _gdn_chunk_common(kh, gcc, bbh, row, col)
            eg = jnp.exp(gcc)
            glr = gcc[:, CHUNK - 1:CHUNK, :]
            th = _unit_lower_inverse(lm, (row == col).astype(F32))
            w = _dot1(th, kb * eg, NN3)
            u = _dot1(th, vh * bbh, NN3)
            at = jnp.where(row >= col, _dot1(qh, kh, NT3) * dm, 0.0)
            vn = u - _dot1(w, sh, NN3)
            o_ref[:, rows, :] = _dot1(qh * eg, sh, NN3) + _dot1(at, vn, NN3)
            kd = kh * jnp.exp(glr - gcc)
            sall_ref[:, cc] = sh
            tall_ref[:, rows, :] = th
            sh = sh * jnp.exp(glr) + _dot1(kd, vn, TN3)
        s_s[...] = sh

    cps = min(GDN_STEP_CHUNKS, nchunk)
    steps = nchunk // cps
    blk = pl.BlockSpec((nh, cps * CHUNK, dh), lambda n: (0, n, 0))
    body, x_in, x_out, x_shape, x_scr, x_args = _carry(body, 5, 3, (steps,), carried)
    return pl.pallas_call(
        body, name="gdn_fwd", grid=(steps,), in_specs=[blk] * 5 + x_in,
        out_specs=[blk, pl.BlockSpec((nh, cps, dh, dh), lambda n: (0, n, 0, 0)), blk] + x_out,
        out_shape=[jax.ShapeDtypeStruct((nh, t, dh), F32), jax.ShapeDtypeStruct((nh, nchunk, dh, dh), F32),
                   jax.ShapeDtypeStruct((nh, t, CHUNK), F32)] + x_shape,
        scratch_shapes=[pltpu.VMEM((nh, dh, dh), F32)] + x_scr,
        compiler_params=_params(("arbitrary",)))(q, k, v, gb, bb, *x_args)


def _gdn_bwd(q, k, v, gb, bb, sall, tall, do):
    nh, t, dh = q.shape
    nchunk = t // CHUNK

    def body(q_ref, k_ref, v_ref, g_ref, b_ref, sall_ref, tall_ref, do_ref,
             dq_ref, dk_ref, dv_ref, dg_ref, db_ref, ds_s):
        @pl.when(pl.program_id(0) == 0)
        def _():
            ds_s[...] = jnp.zeros(ds_s.shape, F32)

        row, col = _tri_masks(nh)
        tril, stril = row >= col, row > col
        rsum = lambda x: jnp.sum(x, axis=2, keepdims=True)
        dsp = ds_s[...]
        for cc in reversed(range(cps)):
            rows = pl.ds(cc * CHUNK, CHUNK)
            dsp = chunk_bwd(rows, cc, dsp, row, col, tril, stril, rsum, q_ref, k_ref, v_ref, g_ref, b_ref, sall_ref,
                            tall_ref, do_ref, dq_ref, dk_ref, dv_ref, dg_ref, db_ref)
        ds_s[...] = dsp

    def chunk_bwd(rows, cc, dsp, row, col, tril, stril, rsum, q_ref, k_ref, v_ref, g_ref, b_ref, sall_ref, tall_ref,
                  do_ref, dq_ref, dk_ref, dv_ref, dg_ref, db_ref):
        qh, kh, vh, gcc, bbh = q_ref[:, rows, :], k_ref[:, rows, :], v_ref[:, rows, :], g_ref[:, rows, :], \
            b_ref[:, rows, :]
        sh, th, doh = sall_ref[:, cc], tall_ref[:, rows, :], do_ref[:, rows, :]
        dm, kb, lm = _gdn_chunk_common(kh, gcc, bbh, row, col, _dot3)
        eg = jnp.exp(gcc)
        glr = gcc[:, CHUNK - 1:CHUNK, :]
        glv = jnp.exp(glr)
        egl = jnp.exp(glr - gcc)
        rw, ru = kb * eg, vh * bbh
        w, u = _dot3(th, rw, NN3), _dot3(th, ru, NN3)
        at = jnp.where(tril, _dot3(qh, kh, NT3) * dm, 0.0)
        qd, kd = qh * eg, kh * egl
        vn = u - _dot3(w, sh, NN3)
        dgl = jnp.sum(rsum(dsp * sh), axis=1, keepdims=True)
        dkd = _dot3(vn, dsp, NT3)
        dvn = _dot3(kd, dsp, NN3)
        dqd = _dot3(doh, sh, NT3)
        dat = jnp.where(tril, _dot3(doh, vn, NT3), 0.0)
        dvn = dvn + _dot3(at, doh, TN3)
        dw = -_dot3(dvn, sh, NT3)
        ds_before = dsp * glv + _dot3(qd, doh, TN3) - _dot3(w, dvn, TN3)
        dpa = dat * dm
        dq_ref[:, rows, :] = _dot1(dpa, kh, NN3) + dqd * eg
        dk = _dot1(dpa, qh, TN3) + dkd * egl
        t6 = rsum(dkd * kd)
        dgam = rsum(dqd * qd) - t6
        dgam_last = jnp.sum(t6, axis=1, keepdims=True) + dgl * glv
        drw = _dot3(th, dw, TN3)
        dru = _dot3(th, dvn, TN3)
        dl = -jnp.where(stril, _dot3(drw, w, NT3) + _dot3(dru, u, NT3), 0.0)
        dgam = dgam + rsum(drw * rw)
        dv_ref[:, rows, :] = dru * bbh
        dp2 = dl * dm
        dkb = drw * eg + _dot1(dp2, kh, NN3)
        dk_ref[:, rows, :] = dk + _dot1(dp2, kb, TN3) + dkb * bbh
        db_ref[:, rows, :] = rsum(dru * vh) + rsum(dkb * kh) + jnp.zeros((nh, CHUNK, dh), F32)
        e = dat * at + dl * lm
        dgam_b = dgam + rsum(e) - _dot01(e, jnp.ones((nh, CHUNK, CHUNK), F32), TN3)
        dg_ref[:, rows, :] = dgam_b + jnp.where(row == CHUNK - 1, dgam_last, 0.0)
        return ds_before

    cps = min(GDN_STEP_CHUNKS, nchunk)
    steps = nchunk // cps
    rev = lambda n: (0, steps - 1 - n, 0)
    blk = pl.BlockSpec((nh, cps * CHUNK, dh), rev)
    sblk = pl.BlockSpec((nh, cps, dh, dh), lambda n: (0, steps - 1 - n, 0, 0))
    out = jax.ShapeDtypeStruct((nh, t, dh), F32)
    return pl.pallas_call(
        body, name="gdn_bwd", grid=(steps,), in_specs=[blk] * 5 + [sblk, blk, blk], out_specs=[blk] * 5,
        out_shape=[out] * 5, scratch_shapes=[pltpu.VMEM((nh, dh, dh), F32)],
        compiler_params=_params(("arbitrary",)))(q, k, v, gb, bb, sall, tall, do)


def _group_ones():
    r = lax.broadcasted_iota(jnp.int32, (GDN_W, GDN_W), 0) // GDN_DH
    c = lax.broadcasted_iota(jnp.int32, (GDN_W, GDN_W), 1) // GDN_DH
    return (r == c).astype(F32)


def _conv_taps(x, xprev, w, has_prev):
    row = lax.broadcasted_iota(jnp.int32, x.shape, 0)
    out = x * w[GDN_CONV - 1:GDN_CONV, :]
    for s in range(1, GDN_CONV):
        sh = jnp.where(row >= s, _roll(x, s, 0), _roll(xprev, s, 0) * has_prev)
        out = out + sh * w[GDN_CONV - 1 - s:GDN_CONV - s, :]
    return out


def _head_cols(x, h):
    return x[:, h * GDN_DH:(h + 1) * GDN_DH]


def _heads_spec(tb):
    return pl.BlockSpec((N_HEADS, tb, GDN_DH), lambda i: (0, i, 0))


def _mixer_fwd(x, positions, w, tb, carried=None, carried_gdn=None):
    t, d = x.shape
    tables = _rope_tables(positions)

    hn, proj = _prologue_mm("mix_in", [x], w["mix_pre_g"], lambda xb, g: (xb * _rms_stats(xb) * g,), w["w_in_pad_t"],
                            False)

    def mla_pre(p0, gq, gkv):
        cq, ckv = p0[:, :MLA_Q_RANK], p0[:, MLA_Q_RANK:MLA_Q_RANK + MLA_KV_RANK]
        return cq * _rms_stats(cq) * gq, ckv * _rms_stats(ckv) * gkv

    nq, nkv = _rowwise("mla_pre", mla_pre, [(proj, 512, PIN_MLA // 512, 0)],
                       [w["mla_q_norm_g"], w["mla_kv_norm_g"]], [(MLA_Q_RANK, BF16), (MLA_KV_RANK, BF16)], [], tb)
    qraw = _mm("mla_uq", nq, w["w_uq_pad"], "nn", F32)
    kv = _mm("mla_ukv", nkv, w["w_kv_pad"], "nn", F32)

    def rope_f(qr, kn, vv, kpe, c, s1, s2):
        qo = _heads_apply(qr, lambda xh: _rope(xh, c, s1, s2)) * _attn_scale()
        kp = _rope(kpe, c, s1, s2)
        return qo, kn + jnp.tile(kp, (1, N_HEADS)), vv

    q, k, v = _rowwise("mla_rope", rope_f,
                       [qraw, (kv, MLA_PAD, 0, 0), (kv, MLA_PAD, 1, 0), (proj, HEAD_LANES, PIN_KPE // HEAD_LANES, 0),
                        tables[0], tables[1], tables[2]], [],
                       [(MLA_PAD, BF16)] * 3, [], tb // 2)
    tq = min(1024, t)
    o, lse, *carried_out = _attn_fwd(q, k, v, tq, carried)

    def mla_post(ob, g):
        return (ob * _rms_stats(ob, N_HEADS * MLA_V) * g,)

    (cat,) = _rowwise("mla_post", mla_post, [o], [w["mla_out_g_pad"]], [(MLA_PAD, BF16)], [], tb, wide=(CAT_W, 0))

    gones = _group_ones()
    steps = t // tb

    def gdn_pre(xq, xk, xv, pq, pk, pv, cw, go, has_prev):
        outs = []
        for j, (xc, xp) in enumerate(((xq, pq), (xk, pk), (xv, pv))):
            c = _conv_taps(xc, xp, cw[:, j * GDN_W:(j + 1) * GDN_W], has_prev)
            a = c * _sigmoid(c)
            if j < 2:
                rn = lax.rsqrt(_dot01(a * a, go) + EPS)
                a = a * rn
                if j == 0:
                    a = a * (GDN_DH ** -0.5)
            outs.append(a)
        return tuple(outs)

    qh, kh, vh = _gdn_pre_call("gdn_pre", gdn_pre, proj, w["conv_w"], gones, tb, steps)
    heads_shape = jax.ShapeDtypeStruct((N_HEADS, t, GDN_DH), F32)
    lanes_shape = jax.ShapeDtypeStruct((t, HEAD_LANES), F32)
    lanes_spec = pl.BlockSpec((tb, HEAD_LANES), lambda i: (i, 0))
    vec_spec = lambda n: pl.BlockSpec((1, n), lambda i: (0, 0))

    def gate_f(ab_ref, al_ref, dt_ref, g_ref, b_ref, gh_ref, bh_ref):
        g, b = _gb_fwd(ab_ref[...], al_ref[...], dt_ref[...])
        g_ref[...] = g
        b_ref[...] = b
        gc = _dot01(_chunk_sum_matrix(tb, False), g, ones="lhs")
        for h in range(N_HEADS):
            gh_ref[h] = jnp.broadcast_to(gc[:, h:h + 1], (tb, GDN_DH))
            bh_ref[h] = jnp.broadcast_to(b[:, N_HEADS + h:N_HEADS + h + 1], (tb, GDN_DH))

    g128, b128, gbh, bbh = pl.pallas_call(
        gate_f, name="gdn_gate_f", grid=(steps,),
        in_specs=[pl.BlockSpec((tb, HEAD_LANES), lambda i: (i, PIN_AB // HEAD_LANES)), vec_spec(HEAD_LANES),
                  vec_spec(HEAD_LANES)],
        out_specs=[lanes_spec, lanes_spec, _heads_spec(tb), _heads_spec(tb)],
        out_shape=[lanes_shape, lanes_shape, heads_shape, heads_shape],
        compiler_params=_params(("arbitrary",)))(proj, w["a_log_pad"], w["dt_bias_pad"])
    oh, sall, tall, *carried_out_gdn = _gdn_fwd(qh, kh, vh, gbh, bbh, carried_gdn)

    def gdn_post(o_ref, gt_ref, g_ref, cat_in, cat_ref):
        gt, g = gt_ref[...], g_ref[...]
        outs = []
        for h in range(N_HEADS):
            ob, gth = o_ref[h], _head_cols(gt, h)
            outs.append(ob * _rms_stats(ob) * g * (gth * _sigmoid(gth)))
        cat_ref[...] = jnp.concatenate(outs, axis=1).astype(cat_ref.dtype)

    gate_spec = pl.BlockSpec((tb, GDN_W), lambda i: (i, PIN_GATE // GDN_W))
    cat = pl.pallas_call(
        gdn_post, name="gdn_post", grid=(steps,),
        in_specs=[_heads_spec(tb), gate_spec, vec_spec(GDN_DH), ANY_SPEC],
        out_specs=pl.BlockSpec((tb, GDN_W), lambda i: (i, MLA_PAD // GDN_W)),
        out_shape=jax.ShapeDtypeStruct((t, CAT_W), BF16), input_output_aliases={3: 0},
        compiler_params=_params(("arbitrary",)))(oh, proj, w["gdn_norm_g"], cat)
    mixed, y = _mm_epilogue("mix_out", cat, w["w_out_pad"], [x], w["mix_post_g"],
                            lambda hb, xb, g: (hb, xb + hb * _rms_stats(hb) * g), 2, False, tk=CAT_W)
    saved = dict(x=x, hn=hn, proj=proj, nq=nq, nkv=nkv, q=q, k=k, v=v, o=o, lse=lse, qh=qh, kh=kh, vh=vh,
                 gbh=gbh, bbh=bbh, oh=oh, sall=sall, tall=tall, cat=cat, mixed=mixed,
                 tables=tables, g128=g128, b128=b128)
    return y, saved, list(carried_out) + list(carried_out_gdn)


def _qkv_specs(tb):
    base = PIN_QKV // GDN_W
    cur = [pl.BlockSpec((tb, GDN_W), lambda i, j=j: (i, base + j)) for j in range(3)]
    prev = [pl.BlockSpec((tb, GDN_W), lambda i, j=j: (jnp.maximum(i - 1, 0), base + j)) for j in range(3)]
    return cur + prev


def _gdn_pre_call(name, fn, proj, conv_w, gones, tb, steps):
    t = proj.shape[0]

    def body(xq, xk, xv, pq, pk, pv, cw, go, oq, ok, ov):
        has_prev = jnp.where(pl.program_id(0) == 0, 0.0, 1.0)
        outs = fn(xq[...], xk[...], xv[...], pq[...], pk[...], pv[...], cw[...], go[...], has_prev)
        for r, val in zip((oq, ok, ov), outs):
            for h in range(N_HEADS):
                r[h] = _head_cols(val, h)

    return pl.pallas_call(
        body, name=name, grid=(steps,),
        in_specs=_qkv_specs(tb) + [pl.BlockSpec(conv_w.shape, lambda i: (0, 0)),
                                   pl.BlockSpec(gones.shape, lambda i: (0, 0))],
        out_specs=[_heads_spec(tb)] * 3,
        out_shape=[jax.ShapeDtypeStruct((N_HEADS, t, GDN_DH), F32)] * 3,
        compiler_params=_params(("arbitrary",)))(proj, proj, proj, proj, proj, proj, conv_w, gones)


def _softplus(x):
    return jnp.maximum(x, 0.0) + jnp.log1p(jnp.exp(-jnp.abs(x)))


def _gb_fwd(ab, a_log, dt_bias):
    g = -jnp.exp(a_log) * _softplus(ab + dt_bias)
    return g, _sigmoid(ab)


def _rope_tables(positions):
    half = MLA_ROPE // 2
    freqs = ROPE_THETA ** (-jnp.arange(half, dtype=F32) / half)
    ang = positions.reshape(-1).astype(F32)[:, None] * freqs
    cos, sin = jnp.cos(ang), jnp.sin(ang)
    t = ang.shape[0]
    one = jnp.ones((t, MLA_NOPE), F32)
    z16, z32, z64 = jnp.zeros((t, half), F32), jnp.zeros((t, MLA_ROPE), F32), jnp.zeros((t, MLA_NOPE), F32)
    c = jnp.concatenate([one, cos, cos, jnp.ones((t, MLA_ROPE), F32)], axis=1)
    s1 = jnp.concatenate([z64, -sin, z16, z32], axis=1)
    s2 = jnp.concatenate([z64, z16, sin, z32], axis=1)
    return c, s1, s2


def _mixer_bwd(dy, sv, w, tb, carried=None):
    x, proj = sv["x"], sv["proj"]
    t, d = x.shape
    c, s1, s2 = sv["tables"]
    grads = {}

    dmixed, dcat, grads["mix_post_g"] = _prologue_mm(
        "mix_out_bx", [sv["mixed"], dy], w["mix_post_g"], lambda hb, dyb, g: _rms_bwd(hb, _rms_stats(hb), g, dyb),
        w["w_out_pad"], True)
    grads["w_out_pad"] = _mm("mix_out_bw", sv["cat"], dmixed, "tn", F32)
    steps = t // tb
    vec_spec = lambda n: pl.BlockSpec((1, n), lambda i: (0, 0))

    def gdn_post_b(o_ref, gt_ref, do_ref, g_ref, dproj_ref, doh_ref, dg_ref):
        @pl.when(pl.program_id(0) == 0)
        def _():
            dg_ref[...] = jnp.zeros(dg_ref.shape, F32)

        gt, dob, g = gt_ref[...], do_ref[...], g_ref[...]
        dgates = []
        for h in range(N_HEADS):
            ob, gth, dobh = o_ref[h], _head_cols(gt, h), _head_cols(dob, h)
            sg = _sigmoid(gth)
            r = _rms_stats(ob)
            dxo, dg = _rms_bwd(ob, r, g, dobh * (gth * sg))
            doh_ref[h] = dxo
            dg_ref[...] += dg
            dgates.append(dobh * (ob * r * g) * (sg * (1.0 + gth * (1.0 - sg))))
        dproj_ref[...] = jnp.concatenate(dgates, axis=1).astype(dproj_ref.dtype)

    dproj, doh, grads["gdn_norm_g"] = pl.pallas_call(
        gdn_post_b, name="gdn_post_b", grid=(steps,),
        in_specs=[_heads_spec(tb), pl.BlockSpec((tb, GDN_W), lambda i: (i, PIN_GATE // GDN_W)),
                  pl.BlockSpec((tb, GDN_W), lambda i: (i, MLA_PAD // GDN_W)), vec_spec(GDN_DH)],
        out_specs=[pl.BlockSpec((tb, GDN_W), lambda i: (i, PIN_GATE // GDN_W)), _heads_spec(tb), vec_spec(GDN_DH)],
        out_shape=[jax.ShapeDtypeStruct((t, PIN_W), BF16), jax.ShapeDtypeStruct((N_HEADS, t, GDN_DH), F32),
                   jax.ShapeDtypeStruct((1, GDN_DH), F32)],
        compiler_params=_params(("arbitrary",)))(sv["oh"], proj, dcat, w["gdn_norm_g"])

    def mla_post_b(ob, dmo, g):
        do, dg = _rms_bwd(ob, _rms_stats(ob, N_HEADS * MLA_V), g, dmo, N_HEADS * MLA_V)
        prod = do * ob
        delta = _heads_apply(prod, lambda ph: jnp.sum(ph, axis=1, keepdims=True) + jnp.zeros_like(ph))
        return do, delta, dg

    do, delta, grads["mla_out_g_pad"] = _rowwise(
        "mla_post_b", mla_post_b, [sv["o"], (dcat, MLA_PAD, 0, 0)], [w["mla_out_g_pad"]],
        [(MLA_PAD, BF16), (MLA_PAD, F32)], [(1, MLA_PAD)], tb // 2)
    tq = min(1024, t)
    dq, dk, dv, *carried_out = _attn_bwd(sv["q"], sv["k"], sv["v"], do, sv["lse"], delta, tq, carried)
    n_groups = N_HEADS // BWD_HEADS
    tr = tb // 2
    dq_groups = [(dq, BWD_HEADS * HEAD_LANES, 0, grp * (t // tr)) for grp in range(n_groups)]

    def rope_b(*blocks):
        dqb = jnp.concatenate(blocks[:n_groups], axis=1)
        dkb, dvb, cc, a1, a2 = blocks[n_groups:]
        dqr = _heads_apply(dqb * _attn_scale(), lambda xh: _rope(xh, cc, -a1, -a2))
        ksum = dkb[:, :HEAD_LANES]
        for h in range(1, N_HEADS):
            ksum = ksum + dkb[:, h * HEAD_LANES:(h + 1) * HEAD_LANES]
        lane = lax.broadcasted_iota(jnp.int32, ksum.shape, 1)
        keep = (lane >= MLA_NOPE) & (lane < MLA_NOPE + MLA_ROPE)
        dkpe = jnp.where(keep, _rope(ksum, cc, -a1, -a2), 0.0)
        return dqr, jnp.concatenate([dkb, dvb], axis=1), dkpe

    dqraw, dkv, dkpe = _rowwise("mla_rope_b", rope_b, dq_groups + [dk, dv, c, s1, s2], [],
                                [(MLA_PAD, BF16, t), (2 * MLA_PAD, BF16), (HEAD_LANES, F32)], [], tr)
    dnq = _mm("mla_uq_bx", dqraw, w["w_uq_pad"], "nt", F32)
    grads["w_uq_pad"] = _mm("mla_uq_bw", sv["nq"], dqraw, "tn", F32)
    dnkv = _mm("mla_ukv_bx", dkv, w["w_kv_pad"], "nt", F32)
    grads["w_kv_pad"] = _mm("mla_ukv_bw", sv["nkv"], dkv, "tn", F32)

    def mla_pre_b(p0, dnqb, dnkvb, dkpeb, gq, gkv):
        cq, ckv = p0[:, :MLA_Q_RANK], p0[:, MLA_Q_RANK:MLA_Q_RANK + MLA_KV_RANK]
        dcq, dgq = _rms_bwd(cq, _rms_stats(cq), gq, dnqb)
        dckv, dgkv = _rms_bwd(ckv, _rms_stats(ckv), gkv, dnkvb)
        return jnp.concatenate([dcq, dckv, dkpeb], axis=1), dgq, dgkv

    dproj, grads["mla_q_norm_g"], grads["mla_kv_norm_g"] = _rowwise(
        "mla_pre_b", mla_pre_b, [(proj, 512, PIN_MLA // 512, 0), dnq, dnkv, dkpe],
        [w["mla_q_norm_g"], w["mla_kv_norm_g"]], [(512, BF16)], [(1, MLA_Q_RANK), (1, MLA_KV_RANK)], tb,
        wide=(PIN_W, PIN_MLA // 512), carry=dproj)

    dqh, dkh, dvh, dgh, dbh = _gdn_bwd(sv["qh"], sv["kh"], sv["vh"], sv["gbh"], sv["bbh"], sv["sall"], sv["tall"], doh)
    gones = _group_ones()

    def gdn_pre_b(xq, xk, xv, pq, pk, pv, dq_, dk_, dv_, cw, go, has_prev):
        outs = []
        for j, (xc, xp, dd) in enumerate(((xq, pq, dq_), (xk, pk, dk_), (xv, pv, dv_))):
            cc = _conv_taps(xc, xp, cw[:, j * GDN_W:(j + 1) * GDN_W], has_prev)
            sg = _sigmoid(cc)
            a = cc * sg
            if j < 2:
                rn = lax.rsqrt(_dot01(a * a, go) + EPS)
                if j == 0:
                    dd = dd * (GDN_DH ** -0.5)
                da = rn * dd - a * (rn * rn * rn) * _dot01(dd * a, go)
            else:
                da = dd
            outs.append(da * (sg * (1.0 + cc * (1.0 - sg))))
        return tuple(outs)

    dcq, dck, dcv = _gdn_pre_b_call("gdn_pre_b", gdn_pre_b, proj, (dqh, dkh, dvh), w["conv_w"], gones, tb, steps)
    dproj, grads["conv_w"] = _conv_bwd_call("gdn_conv_b", proj, (dcq, dck, dcv), w["conv_w"], dproj, tb, steps)

    def gate_b(ab_ref, g_ref, b_ref, dgh_ref, dbh_ref, al_ref, dt_ref, carry_ref, dab_ref, dal_ref, ddt_ref):
        @pl.when(pl.program_id(0) == 0)
        def _():
            dal_ref[...] = jnp.zeros(dal_ref.shape, F32)
            ddt_ref[...] = jnp.zeros(ddt_ref.shape, F32)

        ab, g128, b128 = ab_ref[...], g_ref[...], b_ref[...]
        lane = lax.broadcasted_iota(jnp.int32, ab.shape, 1)
        dg_ = jnp.zeros(ab.shape, F32)
        db_ = jnp.zeros(ab.shape, F32)
        for h in range(N_HEADS):
            dg_ = dg_ + jnp.where(lane == h, jnp.broadcast_to(dgh_ref[h][:, 0:1], ab.shape), 0.0)
            db_ = db_ + jnp.where(lane == N_HEADS + h, jnp.broadcast_to(dbh_ref[h][:, 0:1], ab.shape), 0.0)
        dg_ = _dot01(_chunk_sum_matrix(tb, True), dg_, ones="lhs")
        slope = -jnp.exp(al_ref[...]) * _sigmoid(ab + dt_ref[...])
        dab_ref[...] = (dg_ * slope + db_ * b128 * (1.0 - b128)).astype(dab_ref.dtype)
        dal_ref[...] += jnp.sum(dg_ * g128, axis=0, keepdims=True)
        ddt_ref[...] += jnp.sum(dg_ * slope, axis=0, keepdims=True)

    lanes_spec = pl.BlockSpec((tb, HEAD_LANES), lambda i: (i, 0))
    ab_spec = pl.BlockSpec((tb, HEAD_LANES), lambda i: (i, PIN_AB // HEAD_LANES))
    dproj, grads["a_log_pad"], grads["dt_bias_pad"] = pl.pallas_call(
        gate_b, name="gdn_gate_b", grid=(steps,),
        in_specs=[ab_spec, lanes_spec, lanes_spec, _heads_spec(tb), _heads_spec(tb), vec_spec(HEAD_LANES),
                  vec_spec(HEAD_LANES), ANY_SPEC],
        out_specs=[ab_spec, vec_spec(HEAD_LANES), vec_spec(HEAD_LANES)],
        out_shape=[jax.ShapeDtypeStruct((t, PIN_W), BF16), jax.ShapeDtypeStruct((1, HEAD_LANES), F32),
                   jax.ShapeDtypeStruct((1, HEAD_LANES), F32)],
        input_output_aliases={7: 0},
        compiler_params=_params(("arbitrary",)))(proj, sv["g128"], sv["b128"], dgh, dbh, w["a_log_pad"],
                                                 w["dt_bias_pad"], dproj)
    grads["w_in_pad_t"] = _mm("mix_in_bw", dproj, sv["hn"], "tn", F32)

    def pre_b(dhn, xb, dyb, g):
        dx, dg = _rms_bwd(xb, _rms_stats(xb), g, dhn)
        return dyb + dx, dg

    dx, grads["mix_pre_g"] = _mm_epilogue("mix_in_bx", dproj, w["w_in_pad_t"], [x, dy], w["mix_pre_g"], pre_b, 1, True,
                                          tk=PIN_W // 3)
    return dx, grads, carried_out


def _gdn_pre_b_call(name, fn, proj, dd, conv_w, gones, tb, steps):
    t = proj.shape[0]

    def body(xq, xk, xv, pq, pk, pv, d0, d1, d2, cw, go, oq, ok, ov):
        has_prev = jnp.where(pl.program_id(0) == 0, 0.0, 1.0)
        dd_rows = [jnp.concatenate([dr[h] for h in range(N_HEADS)], axis=1) for dr in (d0, d1, d2)]
        outs = fn(xq[...], xk[...], xv[...], pq[...], pk[...], pv[...], *dd_rows, cw[...], go[...], has_prev)
        for r, val in zip((oq, ok, ov), outs):
            r[...] = val

    return pl.pallas_call(
        body, name=name, grid=(steps,),
        in_specs=_qkv_specs(tb) + [_heads_spec(tb)] * 3 + [pl.BlockSpec(conv_w.shape, lambda i: (0, 0)),
                                                          pl.BlockSpec(gones.shape, lambda i: (0, 0))],
        out_specs=[pl.BlockSpec((tb, GDN_W), lambda i: (i, 0))] * 3,
        out_shape=[jax.ShapeDtypeStruct((t, GDN_W), F32)] * 3,
        compiler_params=_params(("arbitrary",)))(proj, proj, proj, proj, proj, proj, *dd, conv_w, gones)


def _conv_bwd_call(name, proj, dc, conv_w, dproj, tb, steps):
    t = proj.shape[0]
    dcur = [pl.BlockSpec((tb, GDN_W), lambda i: (i, 0))] * 3
    dnext = [pl.BlockSpec((tb, GDN_W), lambda i: (jnp.minimum(i + 1, steps - 1), 0))] * 3

    def body(xq, xk, xv, pq, pk, pv, d0, d1, d2, n0, n1, n2, cw, carry_ref, dx_ref, dw_ref):
        i = pl.program_id(0)
        has_prev = jnp.where(i == 0, 0.0, 1.0)
        has_next = jnp.where(i == steps - 1, 0.0, 1.0)

        @pl.when(i == 0)
        def _():
            dw_ref[...] = jnp.zeros(dw_ref.shape, F32)

        wv = cw[...]
        dws, dxs = [], []
        for j, (xr, pr, dr, nr) in enumerate(((xq, pq, d0, n0), (xk, pk, d1, n1), (xv, pv, d2, n2))):
            x, xp, dcv, dnx = xr[...], pr[...], dr[...], nr[...]
            wj = wv[:, j * GDN_W:(j + 1) * GDN_W]
            row = lax.broadcasted_iota(jnp.int32, x.shape, 0)
            dx = dcv * wj[GDN_CONV - 1:GDN_CONV, :]
            rows_w = [jnp.sum(dcv * x, axis=0, keepdims=True)]
            for s in range(1, GDN_CONV):
                up = jnp.where(row < tb - s, _roll(dcv, tb - s, 0), _roll(dnx, tb - s, 0) * has_next)
                dx = dx + up * wj[GDN_CONV - 1 - s:GDN_CONV - s, :]
                sh = jnp.where(row >= s, _roll(x, s, 0), _roll(xp, s, 0) * has_prev)
                rows_w.append(jnp.sum(dcv * sh, axis=0, keepdims=True))
            dxs.append(dx)
            dws.append(jnp.concatenate(rows_w[::-1], axis=0))
        dx_ref[...] = jnp.concatenate(dxs, axis=1).astype(dx_ref.dtype)
        dw_ref[...] += jnp.concatenate(dws, axis=1)

    return pl.pallas_call(
        body, name=name, grid=(steps,),
        in_specs=_qkv_specs(tb) + dcur + dnext + [pl.BlockSpec(conv_w.shape, lambda i: (0, 0)), ANY_SPEC],
        out_specs=[pl.BlockSpec((tb, 3 * GDN_W), lambda i: (i, PIN_QKV // (3 * GDN_W))),
                   pl.BlockSpec(conv_w.shape, lambda i: (0, 0))],
        out_shape=[jax.ShapeDtypeStruct((t, PIN_W), BF16), jax.ShapeDtypeStruct(conv_w.shape, F32)],
        input_output_aliases={13: 0},
        compiler_params=_params(("arbitrary",)))(proj, proj, proj, proj, proj, proj, *dc, *dc, conv_w, dproj)


def _pad_heads_cols(wm, per_head):
    r = wm.shape[0]
    return jnp.pad(wm.reshape(r, N_HEADS, per_head), ((0, 0), (0, 0), (0, HEAD_LANES - per_head))).reshape(r, MLA_PAD)


def _unpad_heads_cols(wm, per_head):
    r = wm.shape[0]
    return wm.reshape(r, N_HEADS, HEAD_LANES)[:, :, :per_head].reshape(r, N_HEADS * per_head)


W_IN_COLS = MLA_Q_RANK + MLA_KV_RANK + MLA_ROPE + 3 * GDN_W + 2 * N_HEADS + GDN_W
W_IN_SHARD = W_IN_COLS // N_SHARD
W_IN_SHARD_PAD = 640
_Q0 = MLA_Q_RANK + MLA_KV_RANK
_Q1 = _Q0 + MLA_ROPE
_Q2 = _Q1 + 3 * GDN_W
_Q3 = _Q2 + 2 * N_HEADS
W_IN_SEGMENTS = [(0, _Q0, PIN_MLA), (_Q0, _Q1, PIN_KPE + MLA_NOPE), (_Q1, _Q2, PIN_QKV), (_Q2, _Q3, PIN_AB),
                 (_Q3, W_IN_COLS, PIN_GATE)]


def _win_pad_t(slabs):
    d = slabs.shape[2]
    pieces, at = [], 0
    for c0, c1, r0 in sorted(W_IN_SEGMENTS, key=lambda s: s[2]):
        if r0 > at:
            pieces.append(jnp.zeros((r0 - at, d), slabs.dtype))
        for q in range(N_SHARD):
            lo, hi = max(c0, q * W_IN_SHARD), min(c1, (q + 1) * W_IN_SHARD)
            if lo < hi:
                pieces.append(slabs[q, lo - q * W_IN_SHARD:hi - q * W_IN_SHARD])
        at = r0 + c1 - c0
    pieces.append(jnp.zeros((PIN_W - at, d), slabs.dtype))
    return jnp.concatenate(pieces, axis=0)


def _win_cols_t(wp_t, c_lo, c_hi):
    pieces = []
    for c0, c1, r0 in W_IN_SEGMENTS:
        lo, hi = max(c0, c_lo), min(c1, c_hi)
        if lo < hi:
            pieces.append(wp_t[r0 + lo - c0:r0 + hi - c0])
    return jnp.concatenate(pieces, axis=0)


def _wkv_to_pad(wkv):
    r = wkv.shape[0]
    w3 = wkv.reshape(r, N_HEADS, MLA_NOPE + MLA_V)
    kpart = jnp.pad(w3[:, :, :MLA_NOPE], ((0, 0), (0, 0), (0, HEAD_LANES - MLA_NOPE))).reshape(r, MLA_PAD)
    vpart = jnp.pad(w3[:, :, MLA_NOPE:], ((0, 0), (0, 0), (0, HEAD_LANES - MLA_V))).reshape(r, MLA_PAD)
    return jnp.concatenate([kpart, vpart], axis=1)


def _wkv_from_pad(wp):
    r = wp.shape[0]
    kpart = wp[:, :MLA_PAD].reshape(r, N_HEADS, HEAD_LANES)[:, :, :MLA_NOPE]
    vpart = wp[:, MLA_PAD:].reshape(r, N_HEADS, HEAD_LANES)[:, :, :MLA_V]
    return jnp.concatenate([kpart, vpart], axis=2).reshape(r, N_HEADS * (MLA_NOPE + MLA_V))


def _wout_to_pad(wo):
    n = wo.shape[1]
    mla = jnp.pad(wo[:N_HEADS * MLA_V].reshape(N_HEADS, MLA_V, n), ((0, 0), (0, HEAD_LANES - MLA_V), (0, 0)))
    return jnp.concatenate([mla.reshape(MLA_PAD, n), wo[N_HEADS * MLA_V:]], axis=0)


def _wout_from_pad(wp):
    n = wp.shape[1]
    mla = wp[:MLA_PAD].reshape(N_HEADS, HEAD_LANES, n)[:, :MLA_V].reshape(N_HEADS * MLA_V, n)
    return jnp.concatenate([mla, wp[MLA_PAD:]], axis=0)


def _pad_lanes(v, n):
    return jnp.pad(v, ((0, 0), (0, n - v.shape[1])))


def _compute_weights(full):
    w = {}
    for n in FFN_BIG:
        if n in full:
            w[n] = full[n].astype(MM_DTYPE)
    w["w_in_pad_t"] = _win_pad_t(full["w_in"]).astype(MM_DTYPE)
    w["w_uq_pad"] = _pad_heads_cols(full["mla_w_uq"], MLA_NOPE + MLA_ROPE).astype(MM_DTYPE)
    w["w_kv_pad"] = _wkv_to_pad(full["mla_w_ukv"]).astype(MM_DTYPE)
    w["w_out_pad"] = _wout_to_pad(full["w_out"]).astype(MM_DTYPE)
    w["conv_w"] = full["gdn_conv_w"].astype(F32)
    for n in ("ffn1_pre_g", "ffn1_post_g", "mix_pre_g", "mla_q_norm_g", "mla_kv_norm_g", "gdn_norm_g", "mix_post_g",
              "ffn2_pre_g", "ffn2_post_g"):
        w[n] = full[n]
    w["mla_out_g_pad"] = _pad_heads_cols(full["mla_out_g"], MLA_V)
    w["a_log_pad"] = _pad_lanes(full["gdn_a_log"], HEAD_LANES)
    w["dt_bias_pad"] = _pad_lanes(full["gdn_dt_bias"], HEAD_LANES)
    return w


FFN2_BIG = FFN_BIG[3:]


def _local_step(x, positions, loss_target, full, late=None):
    t, d = x.shape
    tb = min(512, t)
    tm = min(1024, t)
    tk = min(2048, t)
    w = _compute_weights(full)
    ffn = lambda tag: (w[tag + "_pre_g"], w[tag + "_w_gate"], w[tag + "_w_up"], w[tag + "_w_down"], w[tag + "_post_g"])
    x1, sv1 = _ffn_fwd("ffn1", x, *ffn("ffn1"), tm)
    x2, svm, gathered = _mixer_fwd(x1, positions, w, tb, _carried_gather(late[0][:2]) if late else None,
                                   _carried_gather(late[0][2:]) if late else None)
    for n, gw in zip(FFN2_BIG, gathered):
        w[n] = gw
    (dy, lsum), sv2 = _ffn_fwd("ffn2", x2, *ffn("ffn2"), tm, loss_target)
    g = {}
    dx2, g["ffn2_pre_g"], g["ffn2_w_gate"], g["ffn2_w_up"], g["ffn2_w_down"], g["ffn2_post_g"] = _ffn_bwd(
        "ffn2", dy, sv2, *ffn("ffn2"), tm, tk)[:6]

    def pair_sums(arrs, tag):
        got = _swap_halves(arrs, tag)
        return [_add_pair("add_pair%s_%d" % (tag, i), gi, gt, late[1]) for i, (gi, gt) in enumerate(zip(arrs, got))]

    def chip_sums(pairs, slabs, tag):
        return [_add_chips("add_chips%s_%d" % (tag, i), pr, sl, late[2]) for i, (pr, sl) in enumerate(zip(pairs, slabs))]

    if late:
        pairs2 = pair_sums([g[n] for n in FFN2_BIG], "_ffn2")
        dx1, gm, slabs2 = _mixer_bwd(dx2, svm, w, tb, _carried_scatter(pairs2))
        for n, hs in zip(FFN2_BIG, chip_sums(pairs2, slabs2, "_ffn2")):
            g[n] = hs
    else:
        dx1, gm, _ = _mixer_bwd(dx2, svm, w, tb)
    g["w_in"] = jnp.stack([jnp.pad(_win_cols_t(gm["w_in_pad_t"], q * W_IN_SHARD, (q + 1) * W_IN_SHARD),
                                   ((0, W_IN_SHARD_PAD - W_IN_SHARD), (0, 0))) for q in range(N_SHARD)])
    g["mla_w_uq"] = _unpad_heads_cols(gm["w_uq_pad"], MLA_NOPE + MLA_ROPE)
    g["mla_w_ukv"] = _wkv_from_pad(gm["w_kv_pad"])
    g["gdn_conv_w"] = gm["conv_w"]
    g["w_out"] = _wout_from_pad(gm["w_out_pad"])
    if late:
        quarters = [_pack([jnp.split(g[n], N_SHARD, axis=SHARD_AXIS[n])[q] for n in MIX_BIG], MM_DTYPE)
                    for q in range(N_SHARD)]
        pairs_m = pair_sums([g["w_in"].astype(MM_DTYPE), jnp.stack(quarters)], "_mix")
        pairs_d, pairs_gu = [], []

        def make_mid(dwd):
            pairs_d.extend(pair_sums([dwd], "_ffn1d"))
            return _carried_scatter(pairs_d)

        def make_up(dwg, dwu):
            pairs_gu.extend(pair_sums([dwg, dwu], "_ffn1"))
            return _carried_scatter(pairs_gu)

        dx0, g["ffn1_pre_g"], _, _, _, g["ffn1_post_g"], slabs_m, slabs_d, slabs_gu = _ffn_bwd(
            "ffn1", dx1, sv1, *ffn("ffn1"), tm, tk, _carried_scatter(pairs_m), make_mid, make_up)
        g["ffn1_w_gate"], g["ffn1_w_up"] = chip_sums(pairs_gu, slabs_gu, "_ffn1")
        g["ffn1_w_down"] = chip_sums(pairs_d, slabs_d, "_ffn1d")[0]
        g["w_in"], g["mix_pack"] = chip_sums(pairs_m, slabs_m, "_mix")
    else:
        dx0, g["ffn1_pre_g"], g["ffn1_w_gate"], g["ffn1_w_up"], g["ffn1_w_down"], g["ffn1_post_g"] = _ffn_bwd(
            "ffn1", dx1, sv1, *ffn("ffn1"), tm, tk)[:6]
    g["mix_pre_g"], g["mix_post_g"] = gm["mix_pre_g"], gm["mix_post_g"]
    g["mla_q_norm_g"], g["mla_kv_norm_g"] = gm["mla_q_norm_g"], gm["mla_kv_norm_g"]
    g["gdn_norm_g"] = gm["gdn_norm_g"]
    g["mla_out_g"] = _unpad_heads_cols(gm["mla_out_g_pad"], MLA_V)
    g["gdn_a_log"] = gm["a_log_pad"][:, :N_HEADS]
    g["gdn_dt_bias"] = gm["dt_bias_pad"][:, :N_HEADS]
    return lsum, dx0, g


HBM_SPEC = pl.BlockSpec(memory_space=pltpu.HBM)


def _place():
    return lax.axis_index("x"), lax.axis_index("y"), lax.axis_index("c")


def _exchange_call(name, body, ins, out_shapes, n_remote, n_local):
    return pl.pallas_call(
        body, name=name, in_specs=[HBM_SPEC] * len(ins), out_specs=[HBM_SPEC] * len(out_shapes), out_shape=out_shapes,
        scratch_shapes=[pltpu.SemaphoreType.DMA((n_remote,)), pltpu.SemaphoreType.DMA((n_remote,)),
                        pltpu.SemaphoreType.DMA((n_local,))])(*ins)


def _other_chips(x, y):
    return [(1 - x, y), (x, 1 - y), (1 - x, 1 - y)]


def _at_each_chip(fn):
    x, y, _ = _place()
    for cx in range(2):
        for cy in range(2):
            pl.when((x == cx) & (y == cy))(functools.partial(fn, cx, cy))


def _at_each_device(fn):
    x, y, c = _place()
    for cx in range(2):
        for cy in range(2):
            for cc in range(2):
                pl.when((x == cx) & (y == cy) & (c == cc))(functools.partial(fn, cx, cy, cc))


def _at_each_core(fn):
    c = lax.axis_index("c")
    for cc in range(2):
        pl.when(c == cc)(functools.partial(fn, cc))


def _gather_shards(ws):
    nw = len(ws)

    def body(*refs):
        w_refs, out_refs = refs[:nw], refs[nw:2 * nw]
        send_sems, recv_sems, local_sems = refs[2 * nw:]

        def run(x, y, c):
            chips = _other_chips(x, y)
            me, sibling = 2 * x + y, (x, y, 1 - c)

            def half(ref, which):
                hr = ref.shape[0] // 2
                return ref.at[pl.ds(which * hr, hr)]

            def over_ici(i, j, src, slab, to):
                return pltpu.make_async_remote_copy(
                    src_ref=half(src, c), dst_ref=half(out_refs[i].at[slab], c), send_sem=send_sems.at[7 * i + j],
                    recv_sem=recv_sems.at[7 * i + j], device_id=to, device_id_type=MESH)

            def over_d2d(i, j, slab, which):
                return pltpu.make_async_remote_copy(
                    src_ref=half(out_refs[i].at[slab], which), dst_ref=half(out_refs[i].at[slab], which),
                    send_sem=send_sems.at[7 * i + 3 + j], recv_sem=recv_sems.at[7 * i + 3 + j], device_id=sibling,
                    device_id_type=MESH)

            def own(i, w_ref):
                return pltpu.make_async_remote_copy(
                    src_ref=w_ref, dst_ref=out_refs[i].at[me], send_sem=send_sems.at[7 * i + 6],
                    recv_sem=recv_sems.at[7 * i + 6], device_id=sibling, device_id_type=MESH)

            sends, passed = [], []
            for i, w_ref in enumerate(w_refs):
                for j, (px, py) in enumerate(chips):
                    sends.append(over_ici(i, j, w_ref, me, (px, py, c)))
                    sends[-1].start()
            for i, w_ref in enumerate(w_refs):
                sends.append(own(i, w_ref))
                sends[-1].start()
            for i, w_ref in enumerate(w_refs):
                for j, (px, py) in enumerate(chips):
                    over_ici(i, j, w_ref, 2 * px + py, (px, py, c)).wait_recv()
                    passed.append(over_d2d(i, j, 2 * px + py, c))
                    passed[-1].start()
            for i, w_ref in enumerate(w_refs):
                own(i, w_ref).wait_recv()
                for j, (px, py) in enumerate(chips):
                    over_d2d(i, j, 2 * px + py, 1 - c).wait_recv()
            for cp in sends + passed:
                cp.wait_send()

        _at_each_device(run)

    outs = [jax.ShapeDtypeStruct((N_SHARD,) + w.shape, w.dtype) for w in ws]
    return _exchange_call("gather_weight_shards", body, ws, outs, 7 * nw, 1)


def _swap_halves(gs, tag=""):
    ng = len(gs)

    def body(*refs):
        g_refs, got_refs = refs[:ng], refs[ng:2 * ng]
        send_sems, recv_sems, _ = refs[2 * ng:]
        x, y, _ = _place()

        def run(c):
            sends = []
            for i, (g_ref, got_ref) in enumerate(zip(g_refs, got_refs)):
                hr = got_ref.shape[1]
                sends.append(pltpu.make_async_remote_copy(
                    src_ref=g_ref.at[:, pl.ds((1 - c) * hr, hr)], dst_ref=got_ref, send_sem=send_sems.at[i],
                    recv_sem=recv_sems.at[i], device_id=(x, y, 1 - c), device_id_type=MESH))
                sends[-1].start()
            for cp in sends:
                cp.wait()

        _at_each_core(run)

    halves = [jax.ShapeDtypeStruct((g.shape[0], g.shape[1] // 2, g.shape[2]), g.dtype) for g in gs]
    return _exchange_call("swap_grad_halves" + tag, body, gs, halves, ng, 1)


def _scatter_copies(p_refs, out_refs, send_sems, recv_sems, x, y):
    c = lax.axis_index("c")
    copies = []
    for i, (p_ref, out_ref) in enumerate(zip(p_refs, out_refs)):
        for j, (px, py) in enumerate(_other_chips(x, y)):
            copies.append(pltpu.make_async_remote_copy(
                src_ref=p_ref.at[2 * px + py], dst_ref=out_ref.at[j], send_sem=send_sems.at[3 * i + j],
                recv_sem=recv_sems.at[3 * i + j], device_id=(px, py, c), device_id_type=MESH))
    return copies


def _start_all(make, *refs):
    def run(x, y):
        for cp in make(*refs, x, y):
            cp.start()

    _at_each_chip(run)


def _wait_all(make, *refs):
    def run(x, y):
        copies = make(*refs, x, y)
        for cp in copies:
            cp.wait_recv()
        for cp in copies:
            cp.wait_send()

    _at_each_chip(run)


def _scatter_shapes(ps):
    return [jax.ShapeDtypeStruct((3,) + p.shape[1:], p.dtype) for p in ps]


def _carried_scatter(ps):
    return _Carried(ps, _scatter_shapes(ps), 3 * len(ps), functools.partial(_start_all, _scatter_copies),
                    functools.partial(_wait_all, _scatter_copies))


def _direct_gather_copies(w_refs, out_refs, send_sems, recv_sems, x, y, arriving):
    c = lax.axis_index("c")
    me = 2 * x + y
    peers = [((px, py, c), 2 * px + py) for px, py in _other_chips(x, y)] + [((x, y, 1 - c), me)]
    copies = []
    for i, (w_ref, out_ref) in enumerate(zip(w_refs, out_refs)):
        for j, (peer, slab) in enumerate(peers):
            copies.append(pltpu.make_async_remote_copy(
                src_ref=w_ref, dst_ref=out_ref.at[slab if arriving else me], send_sem=send_sems.at[4 * i + j],
                recv_sem=recv_sems.at[4 * i + j], device_id=peer, device_id_type=MESH))
    return copies


def _carried_gather(ws):
    def start(w_refs, out_refs, send_sems, recv_sems):
        def run(x, y):
            for cp in _direct_gather_copies(w_refs, out_refs, send_sems, recv_sems, x, y, False):
                cp.start()

        _at_each_chip(run)

    def finish(w_refs, out_refs, send_sems, recv_sems):
        def run(x, y):
            for cp in _direct_gather_copies(w_refs, out_refs, send_sems, recv_sems, x, y, True):
                cp.wait_recv()
            for cp in _direct_gather_copies(w_refs, out_refs, send_sems, recv_sems, x, y, False):
                cp.wait_send()

        _at_each_chip(run)

    outs = [jax.ShapeDtypeStruct((N_SHARD,) + w.shape, w.dtype) for w in ws]
    return _Carried(ws, outs, 4 * len(ws), start, finish)


def _share_halves(hs):
    n = len(hs)

    def body(*refs):
        h_refs, out_refs = refs[:n], refs[n:2 * n]
        send_sems, recv_sems, _ = refs[2 * n:]
        x, y, c = _place()
        sends = []
        for i, (h_ref, out_ref) in enumerate(zip(h_refs, out_refs)):
            sends.append(pltpu.make_async_remote_copy(
                src_ref=h_ref, dst_ref=out_ref, send_sem=send_sems.at[i], recv_sem=recv_sems.at[i],
                device_id=(x, y, 1 - c), device_id_type=MESH))
            sends[-1].start()
        for cp in sends:
            cp.wait()

    outs = [jax.ShapeDtypeStruct(h.shape, h.dtype) for h in hs]
    return _exchange_call("share_grad_halves", body, hs, outs, n, 1)


def _scalar_grid_call(name, body, scalars, grid, in_specs, out_specs, out_shape, args):
    grid_spec = pltpu.PrefetchScalarGridSpec(num_scalar_prefetch=len(scalars), grid=grid, in_specs=in_specs,
                                             out_specs=out_specs)
    return pl.pallas_call(body, name=name, grid_spec=grid_spec, out_shape=out_shape,
                          compiler_params=_params(("arbitrary",) * len(grid)))(*scalars, *args)


def _add_pair(name, g, got, core):
    ns_, hr, cols = got.shape
    th = _row_tile(hr, 512)
    nb = hr // th

    def body(core_ref, g_ref, got_ref, out_ref):
        out_ref[...] = (g_ref[...].astype(F32) + got_ref[...].astype(F32)).astype(out_ref.dtype)

    blk = pl.BlockSpec((1, th, cols), lambda q, j, core_ref: (q, j, 0))
    own = pl.BlockSpec((1, th, cols), lambda q, j, core_ref: (q, core_ref[0] * nb + j, 0))
    return _scalar_grid_call(name, body, [core], (ns_, nb), [own, blk], blk,
                             jax.ShapeDtypeStruct(got.shape, got.dtype), [g, got])


def _add_chips(name, pairs, slabs, chip):
    _, hr, cols = slabs.shape
    th = _row_tile(hr, 512)

    def body(chip_ref, own_ref, s0_ref, s1_ref, s2_ref, out_ref):
        total = own_ref[0].astype(F32) + s0_ref[0].astype(F32)
        out_ref[...] = (total + s1_ref[0].astype(F32)) + s2_ref[0].astype(F32)

    own = pl.BlockSpec((1, th, cols), lambda j, chip_ref: (chip_ref[0], j, 0))
    others = [pl.BlockSpec((1, th, cols), lambda j, chip_ref, k=k: (k, j, 0)) for k in range(3)]
    return _scalar_grid_call(name, body, [chip], (hr // th,), [own] + others,
                             pl.BlockSpec((th, cols), lambda j, chip_ref: (j, 0)),
                             jax.ShapeDtypeStruct((hr, cols), F32), [pairs, slabs, slabs, slabs])


def _join_halves(name, mine, other, core):
    hr, cols = mine.shape
    th = _row_tile(hr, 512)
    nb = hr // th

    def body(core_ref, mine_ref, other_ref, out_ref):
        is_mine = pl.program_id(0) == core_ref[0]

        @pl.when(is_mine)
        def _():
            out_ref[0] = mine_ref[...]

        @pl.when(jnp.logical_not(is_mine))
        def _():
            out_ref[0] = other_ref[...]

    blk = pl.BlockSpec((th, cols), lambda h, j, core_ref: (j, 0))
    return _scalar_grid_call(name, body, [core], (2, nb), [blk, blk],
                             pl.BlockSpec((1, th, cols), lambda h, j, core_ref: (0, h * nb + j, 0)),
                             jax.ShapeDtypeStruct((1, 2 * hr, cols), mine.dtype), [mine, other])


def _gather_small(sp):
    def body(s_ref, out_ref, send_sems, recv_sems, local_sem):
        x, y, c = _place()
        me = 4 * x + 2 * y + c
        peers = [(x ^ (m >> 2), y ^ ((m >> 1) & 1), c ^ (m & 1)) for m in range(1, 8)]
        mine = pltpu.make_async_copy(s_ref, out_ref.at[me], local_sem)
        mine.start()
        sends = [pltpu.make_async_remote_copy(src_ref=s_ref, dst_ref=out_ref.at[me], send_sem=send_sems.at[j],
                                              recv_sem=recv_sems.at[j], device_id=p, device_id_type=MESH)
                 for j, p in enumerate(peers)]
        for cp in sends:
            cp.start()
        for j, (px, py, pc) in enumerate(peers):
            pltpu.make_async_remote_copy(src_ref=s_ref, dst_ref=out_ref.at[4 * px + 2 * py + pc],
                                         send_sem=send_sems.at[j], recv_sem=recv_sems.at[j], device_id=(px, py, pc),
                                         device_id_type=MESH).wait_recv()
        for cp in sends:
            cp.wait_send()
        mine.wait()

    return pl.pallas_call(
        body, name="gather_small_grads", in_specs=[HBM_SPEC], out_specs=HBM_SPEC,
        out_shape=jax.ShapeDtypeStruct((8,) + sp.shape, sp.dtype),
        scratch_shapes=[pltpu.SemaphoreType.DMA((7,)), pltpu.SemaphoreType.DMA((7,)), pltpu.SemaphoreType.DMA])(sp)


def _pack_rows(total):
    rows = -(-total // LANES)
    return -(-rows // 32) * 32


def _pack(arrs, dtype):
    flat = jnp.concatenate([a.reshape(-1).astype(dtype) for a in arrs])
    rows = _pack_rows(flat.shape[0])
    return jnp.pad(flat, (0, rows * LANES - flat.shape[0])).reshape(rows, LANES)


def _unpack(buf, shapes):
    flat = buf.reshape(-1)
    out, off = {}, 0
    for n, shp in shapes:
        size = shp[0] * shp[1]
        out[n] = flat[off:off + size].reshape(shp)
        off += size
    return out


def _to_wire(name, w3):
    _, r, cols = w3.shape
    tb = _row_tile(r, 512)

    def body(w_ref, o_ref):
        o_ref[...] = w_ref[0].astype(o_ref.dtype)

    return pl.pallas_call(
        body, name=name, grid=(r // tb,), in_specs=[pl.BlockSpec((1, tb, cols), lambda i: (0, i, 0))],
        out_specs=pl.BlockSpec((tb, cols), lambda i: (i, 0)), out_shape=jax.ShapeDtypeStruct((r, cols), MM_DTYPE),
        compiler_params=_params(("arbitrary",)))(w3)


def _adamw(name, w3, g, m3, v3, tb):
    c1 = 1.0 - ADAM_B1 ** ADAM_STEP
    c2 = 1.0 - ADAM_B2 ** ADAM_STEP
    _, r, cols = w3.shape
    emit = g.ndim == 2
    blk3 = pl.BlockSpec((1, tb, cols), lambda i: (0, i, 0))
    g_spec = pl.BlockSpec((tb, cols), lambda i: (i, 0)) if emit else blk3

    def body(w_ref, g_ref, m_ref, v_ref, *out_refs):
        gb = g_ref[...] if emit else g_ref[0]
        m2 = ADAM_B1 * m_ref[0] + (1.0 - ADAM_B1) * gb
        v2 = ADAM_B2 * v_ref[0] + (1.0 - ADAM_B2) * (gb * gb)
        out_refs[-3][0] = -ADAM_LR * ((m2 / c1) / (jnp.sqrt(v2 / c2) + ADAM_EPS) + ADAM_WD * w_ref[0])
        out_refs[-2][0] = m2
        out_refs[-1][0] = v2
        if emit:
            out_refs[0][0] = gb

    n_out = 4 if emit else 3
    outs = pl.pallas_call(
        body, name=name, grid=(r // tb,), in_specs=[blk3, g_spec, blk3, blk3], out_specs=[blk3] * n_out,
        out_shape=[jax.ShapeDtypeStruct((1, r, cols), F32)] * n_out,
        compiler_params=_params(("arbitrary",)))(w3, g, m3, v3)
    return outs if emit else [g] + list(outs)


def _row_tile(rows, pref):
    if rows <= pref:
        return rows
    t = pref
    while t >= 8:
        if rows % t == 0 and t % 8 == 0:
            return t
        t -= 8
    return rows


def kernel(x, positions, ffn1_pre_g, ffn1_w_gate, ffn1_w_up, ffn1_w_down, ffn1_post_g, mix_pre_g, w_in, mla_q_norm_g, mla_w_uq, mla_kv_norm_g, mla_w_ukv, mla_out_g, gdn_conv_w, gdn_a_log, gdn_dt_bias, gdn_norm_g, w_out, mix_post_g, ffn2_pre_g, ffn2_w_gate, ffn2_w_up, ffn2_w_down, ffn2_post_g, loss_target, m_ffn1_pre_g, m_ffn1_w_gate, m_ffn1_w_up, m_ffn1_w_down, m_ffn1_post_g, m_mix_pre_g, m_w_in, m_mla_q_norm_g, m_mla_w_uq, m_mla_kv_norm_g, m_mla_w_ukv, m_mla_out_g, m_gdn_conv_w, m_gdn_a_log, m_gdn_dt_bias, m_gdn_norm_g, m_w_out, m_mix_post_g, m_ffn2_pre_g, m_ffn2_w_gate, m_ffn2_w_up, m_ffn2_w_down, m_ffn2_post_g, v_ffn1_pre_g, v_ffn1_w_gate, v_ffn1_w_up, v_ffn1_w_down, v_ffn1_post_g, v_mix_pre_g, v_w_in, v_mla_q_norm_g, v_mla_w_uq, v_mla_kv_norm_g, v_mla_w_ukv, v_mla_out_g, v_gdn_conv_w, v_gdn_a_log, v_gdn_dt_bias, v_gdn_norm_g, v_w_out, v_mix_post_g, v_ffn2_pre_g, v_ffn2_w_gate, v_ffn2_w_up, v_ffn2_w_down, v_ffn2_post_g):
    args = dict(locals())
    wsh = {n: args[n][0] for n in WEIGHTS}
    msh = {n: args["m_" + n] for n in SMALL}
    vsh = {n: args["v_" + n] for n in SMALL}
    for n in SMALL:
        wsh[n] = args[n]
    mix_shapes = [(n, wsh[n].shape) for n in MIX_BIG]

    early = FFN_BIG[:3]
    held = lambda a, n: jnp.swapaxes(a, 1, 2) if n in TRANSPOSED else a
    w_in_wire = jnp.pad(held(w_in, "w_in")[0].astype(MM_DTYPE), ((0, W_IN_SHARD_PAD - W_IN_SHARD), (0, 0)))
    gathered = _gather_shards([_to_wire("wire_" + n, held(args[n], n)) for n in early]
                              + [w_in_wire, _pack([wsh[n] for n in MIX_BIG], MM_DTYPE)])
    full = {n: wsh[n] for n in SMALL}
    for n, gw in zip(early + ["w_in"], gathered):
        full[n] = gw
    parts = [_unpack(gathered[-1][q], mix_shapes) for q in range(N_SHARD)]
    for n in MIX_BIG:
        full[n] = jnp.concatenate([parts[q][n] for q in range(N_SHARD)], axis=SHARD_AXIS[n])

    core = lax.axis_index("c").astype(jnp.int32).reshape(1)
    chip = (2 * lax.axis_index("x") + lax.axis_index("y")).astype(jnp.int32).reshape(1)
    late = ([_to_wire("wire_" + n, held(args[n], n)) for n in FFN2_BIG], core, chip)
    lsum, grad_x, g = _local_step(x[0], positions, loss_target[0], full, late)
    loss = lax.psum(0.5 * jnp.sum(lsum) / x.shape[-1], ("x", "y", "c"))

    halves = [g[n] for n in FFN_BIG] + [g["w_in"], g["mix_pack"]]
    others = _share_halves(halves)
    shared = [_join_halves("join_halves_%d" % i, hm, ho, core) for i, (hm, ho) in enumerate(zip(halves, others))]
    gsh = _unpack(shared[-1], mix_shapes)
    for n, sg_ in zip(FFN_BIG, shared):
        gsh[n] = sg_
    gsh["w_in"] = shared[-2][:, :W_IN_SHARD]

    small_shapes = [(n, wsh[n].shape) for n in SMALL]
    pack_small = lambda d: jnp.concatenate(
        [_pad_lanes(d[n].astype(F32), LANES) for n in SMALL] + [jnp.zeros((SMALL_ROWS - len(SMALL), LANES), F32)], axis=0)
    slots = _gather_small(pack_small(g))

    c1 = 1.0 - ADAM_B1 ** ADAM_STEP
    c2 = 1.0 - ADAM_B2 ** ADAM_STEP

    def small_update(wb, mb, vb, s8):
        gs = s8[0:SMALL_ROWS]
        for d in range(1, 8):
            gs = gs + s8[d * SMALL_ROWS:(d + 1) * SMALL_ROWS]
        m2 = ADAM_B1 * mb + (1.0 - ADAM_B1) * gs
        v2 = ADAM_B2 * vb + (1.0 - ADAM_B2) * (gs * gs)
        delta = -ADAM_LR * ((m2 / c1) / (jnp.sqrt(v2 / c2) + ADAM_EPS) + ADAM_WD * wb)
        return gs, delta, m2, v2

    sg, sd, sm, sv_ = _rowwise("adamw_small", small_update,
                               [pack_small(wsh), pack_small(msh), pack_small(vsh)],
                               [slots.reshape(8 * SMALL_ROWS, LANES)], [(LANES, F32)] * 4, [], SMALL_ROWS)
    grads, deltas, new_m, new_v = {}, {}, {}, {}
    for i, (n, shp) in enumerate(small_shapes):
        grads[n], deltas[n] = sg[i:i + 1, :shp[1]], sd[i:i + 1, :shp[1]]
        new_m[n], new_v[n] = sm[i:i + 1, :shp[1]], sv_[i:i + 1, :shp[1]]
    for n in BIG:
        w3 = held(args[n], n)
        outs = _adamw("adamw_" + n, w3, gsh[n], held(args["m_" + n], n), held(args["v_" + n], n),
                      _row_tile(w3.shape[1], 256))
        grads[n], deltas[n], new_m[n], new_v[n] = [held(o, n) for o in outs]

    return (loss, grad_x[None], *[grads[n] for n in WEIGHTS], *[deltas[n] for n in WEIGHTS],
            *[new_m[n] for n in WEIGHTS], *[new_v[n] for n in WEIGHTS])
```

```python
import functools

import jax
import jax.numpy as jnp
from jax import lax
from jax.experimental import pallas as pl
from jax.experimental.pallas import tpu as pltpu

F32 = jnp.float32
BF16 = jnp.bfloat16
MM_DTYPE = BF16
MESH = pl.DeviceIdType.MESH

D_MODEL = 1024
D_FF = 2816
N_HEADS = 8
MLA_Q_RANK = 256
MLA_KV_RANK = 128
MLA_NOPE = 64
MLA_ROPE = 32
MLA_V = 64
ROPE_THETA = 10000.0
GDN_DH = 64
GDN_W = N_HEADS * GDN_DH
GDN_CONV = 4
CHUNK = 64
GDN_STEP_CHUNKS = 4
HEAD_LANES = 128
HEADS_PER_STEP = 4
MLA_PAD = N_HEADS * HEAD_LANES
EPS = 1e-6
N_SHARD = 4
LANES = 1024

PIN_QKV = 0
PIN_MLA = 1536
PIN_KPE = 1920
PIN_GATE = 2048
PIN_AB = 2560
PIN_W = 2688
CAT_W = MLA_PAD + GDN_W

ADAM_LR = 0.001
ADAM_B1 = 0.9
ADAM_B2 = 0.999
ADAM_EPS = 1e-08
ADAM_WD = 0.01
ADAM_STEP = 10

VMEM_LIMIT_V7X = 56 * 1024 * 1024

BIG = ["ffn1_w_gate", "ffn1_w_up", "ffn1_w_down", "w_in", "mla_w_uq", "mla_w_ukv", "gdn_conv_w", "w_out",
       "ffn2_w_gate", "ffn2_w_up", "ffn2_w_down"]
FFN_BIG = ["ffn1_w_gate", "ffn1_w_up", "ffn1_w_down", "ffn2_w_gate", "ffn2_w_up", "ffn2_w_down"]
TRANSPOSED = ["ffn1_w_gate", "ffn1_w_up", "ffn2_w_gate", "ffn2_w_up", "w_in"]
MIX_BIG = ["mla_w_uq", "mla_w_ukv", "gdn_conv_w", "w_out"]
SMALL = ["ffn1_pre_g", "ffn1_post_g", "mix_pre_g", "mla_q_norm_g", "mla_kv_norm_g", "mla_out_g", "gdn_a_log",
         "gdn_dt_bias", "gdn_norm_g", "mix_post_g", "ffn2_pre_g", "ffn2_post_g"]
WEIGHTS = ["ffn1_pre_g", "ffn1_w_gate", "ffn1_w_up", "ffn1_w_down", "ffn1_post_g", "mix_pre_g", "w_in",
           "mla_q_norm_g", "mla_w_uq", "mla_kv_norm_g", "mla_w_ukv", "mla_out_g", "gdn_conv_w", "gdn_a_log",
           "gdn_dt_bias", "gdn_norm_g", "w_out", "mix_post_g", "ffn2_pre_g", "ffn2_w_gate", "ffn2_w_up",
           "ffn2_w_down", "ffn2_post_g"]
SHARD_AXIS = {"ffn1_w_gate": 1, "ffn1_w_up": 1, "ffn1_w_down": 0, "w_in": 1, "mla_w_uq": 1, "mla_w_ukv": 1,
              "gdn_conv_w": 1, "w_out": 0, "ffn2_w_gate": 1, "ffn2_w_up": 1, "ffn2_w_down": 0}
SMALL_ROWS = 16


def _params(sem):
    return pltpu.CompilerParams(dimension_semantics=sem, vmem_limit_bytes=VMEM_LIMIT_V7X)


def _pick(dim, pref):
    if dim <= pref:
        return dim
    t = (pref // 128) * 128
    while t >= 128:
        if dim % t == 0:
            return t
        t -= 128
    return dim


ANY_SPEC = pl.BlockSpec(memory_space=pl.ANY)


def _rowwise(name, fn, row_ins, bc_ins, row_outs, acc_outs, tb, wide=None, carry=None):
    ents = []
    for e in row_ins:
        ents.append(e if isinstance(e, tuple) else (e, e.shape[1], 0, 0))
    over = [o[2] for o in row_outs if len(o) == 3]
    rows = over[0] if over else ents[0][0].shape[0]
    steps = rows // tb
    assert steps * tb == rows, (name, rows, tb)
    in_specs, args = [], []
    for a, w, j, r0 in ents:
        in_specs.append(pl.BlockSpec((tb, w), lambda i, j=j, r0=r0: (i + r0, j)))
        args.append(a)
    for b in bc_ins:
        in_specs.append(pl.BlockSpec(b.shape, lambda i: (0, 0)))
        args.append(b)
    n_in = len(args)
    aliases = {}
    if carry is not None:
        in_specs.append(ANY_SPEC)
        args.append(carry)
        aliases = {n_in: 0}
    out_shape = [jax.ShapeDtypeStruct((rows, o[0]), o[1]) for o in row_outs]
    out_specs = [pl.BlockSpec((tb, o[0]), lambda i: (i, 0)) for o in row_outs]
    if wide is not None:
        out_shape[0] = jax.ShapeDtypeStruct((rows, wide[0]), row_outs[0][1])
        out_specs[0] = pl.BlockSpec((tb, row_outs[0][0]), lambda i: (i, wide[1]))
    out_shape += [jax.ShapeDtypeStruct((r, c), F32) for r, c in acc_outs]
    out_specs += [pl.BlockSpec((r, c), lambda i: (0, 0)) for r, c in acc_outs]
    n_ro, n_acc, n_args = len(row_outs), len(acc_outs), len(args)

    def body(*refs):
        vals = fn(*[r[...] for r in refs[:n_in]])
        if not isinstance(vals, (tuple, list)):
            vals = (vals,)
        for r, v in zip(refs[n_args:n_args + n_ro], vals[:n_ro]):
            r[...] = v.astype(r.dtype)
        if n_acc:
            acc_refs = refs[n_args + n_ro:]

            @pl.when(pl.program_id(0) == 0)
            def _():
                for r in acc_refs:
                    r[...] = jnp.zeros(r.shape, r.dtype)

            for r, v in zip(acc_refs, vals[n_ro:]):
                r[...] += v

    outs = pl.pallas_call(body, name=name, grid=(steps,), in_specs=in_specs, out_specs=out_specs,
                          out_shape=out_shape, input_output_aliases=aliases,
                          compiler_params=_params(("arbitrary",)))(*args)
    return outs


def _mm(name, a, b, mode, out_dtype, tm=1024, tn=1024, tk=1024):
    if mode == "nn":
        (m, k), (k2, n) = a.shape, b.shape
    elif mode == "nt":
        (m, k), (n, k2) = a.shape, b.shape
    else:
        (k, m), (k2, n) = a.shape, b.shape
    assert k == k2, (name, a.shape, b.shape)
    tm, tn, tk = _pick(m, tm), _pick(n, tn), _pick(k, tk)
    nk = k // tk
    if mode == "nn":
        a_spec = pl.BlockSpec((tm, tk), lambda i, j, kk: (i, kk))
        b_spec = pl.BlockSpec((tk, tn), lambda i, j, kk: (kk, j))
        dims = (((1,), (0,)), ((), ()))
    elif mode == "nt":
        a_spec = pl.BlockSpec((tm, tk), lambda i, j, kk: (i, kk))
        b_spec = pl.BlockSpec((tn, tk), lambda i, j, kk: (j, kk))
        dims = (((1,), (1,)), ((), ()))
    else:
        a_spec = pl.BlockSpec((tk, tm), lambda i, j, kk: (kk, i))
        b_spec = pl.BlockSpec((tk, tn), lambda i, j, kk: (kk, j))
        dims = (((0,), (0,)), ((), ()))

    def body(a_ref, b_ref, o_ref, acc_ref):
        kk = pl.program_id(2)

        @pl.when(kk == 0)
        def _():
            acc_ref[...] = jnp.zeros(acc_ref.shape, F32)

        acc_ref[...] += lax.dot_general(a_ref[...].astype(MM_DTYPE), b_ref[...].astype(MM_DTYPE), dims,
                                        preferred_element_type=F32)

        @pl.when(kk == nk - 1)
        def _():
            o_ref[...] = acc_ref[...].astype(o_ref.dtype)

    return pl.pallas_call(
        body, name=name, grid=(m // tm, n // tn, nk), in_specs=[a_spec, b_spec],
        out_specs=pl.BlockSpec((tm, tn), lambda i, j, kk: (i, j)),
        out_shape=jax.ShapeDtypeStruct((m, n), out_dtype),
        scratch_shapes=[pltpu.VMEM((tm, tn), F32)],
        compiler_params=_params(("parallel", "parallel", "arbitrary")))(a, b)


def _prologue_mm(name, row_ins, g, prologue, w, with_gain_grad, tm=1024, tn=1024):
    t, d = row_ins[0].shape
    n = w.shape[0]
    tm, tn = _pick(t, tm), _pick(n, tn)
    n_row = len(row_ins)
    row = pl.BlockSpec((tm, d), lambda i, j: (i, 0))
    vec = pl.BlockSpec((1, d), lambda i, j: (0, 0))

    def body(*refs):
        rows, g_ref, w_ref = refs[:n_row], refs[n_row], refs[n_row + 1]
        lhs_ref, out_ref = refs[n_row + 2], refs[n_row + 3]
        lhs_s = refs[-1]
        i, j = pl.program_id(0), pl.program_id(1)
        if with_gain_grad:
            dg_ref = refs[n_row + 4]

            @pl.when((i == 0) & (j == 0))
            def _():
                dg_ref[...] = jnp.zeros(dg_ref.shape, F32)

        @pl.when(j == 0)
        def _():
            res = prologue(*[r[...] for r in rows], g_ref[...])
            lhs_s[...] = res[0].astype(MM_DTYPE)
            lhs_ref[...] = lhs_s[...]
            if with_gain_grad:
                dg_ref[...] += res[1]

        out_ref[...] = lax.dot_general(lhs_s[...], w_ref[...], (((1,), (1,)), ((), ())), preferred_element_type=F32)

    out_specs = [row, pl.BlockSpec((tm, tn), lambda i, j: (i, j))] + ([vec] if with_gain_grad else [])
    out_shape = [jax.ShapeDtypeStruct((t, d), MM_DTYPE), jax.ShapeDtypeStruct((t, n), F32)]
    out_shape += [jax.ShapeDtypeStruct((1, d), F32)] if with_gain_grad else []
    return pl.pallas_call(
        body, name=name, grid=(t // tm, n // tn),
        in_specs=[row] * n_row + [vec, pl.BlockSpec((tn, d), lambda i, j: (j, 0))], out_specs=out_specs,
        out_shape=out_shape, scratch_shapes=[pltpu.VMEM((tm, d), MM_DTYPE)],
        compiler_params=_params(("arbitrary", "arbitrary")))(*row_ins, g, w)


def _mm_epilogue(name, a, w, row_ins, g, epilogue, n_row_out, with_gain_grad, tm=1024, tk=1024):
    t, k = a.shape
    d = w.shape[1]
    tm, tk = _pick(t, tm), _pick(k, tk)
    nk = k // tk
    n_row = len(row_ins)
    row = pl.BlockSpec((tm, d), lambda i, kk: (i, 0))
    vec = pl.BlockSpec((1, d), lambda i, kk: (0, 0))

    def body(*refs):
        a_ref, w_ref = refs[0], refs[1]
        rows, g_ref = refs[2:2 + n_row], refs[2 + n_row]
        outs = refs[3 + n_row:3 + n_row + n_row_out]
        acc = refs[-1]
        i, kk = pl.program_id(0), pl.program_id(1)
        if with_gain_grad:
            dg_ref = refs[3 + n_row + n_row_out]

            @pl.when((i == 0) & (kk == 0))
            def _():
                dg_ref[...] = jnp.zeros(dg_ref.shape, F32)

        @pl.when(kk == 0)
        def _():
            acc[...] = jnp.zeros(acc.shape, F32)

        acc[...] += jnp.dot(a_ref[...], w_ref[...], preferred_element_type=F32)

        @pl.when(kk == nk - 1)
        def _():
            res = epilogue(acc[...], *[r[...] for r in rows], g_ref[...])
            for o_ref, val in zip(outs, res[:n_row_out]):
                o_ref[...] = val
            if with_gain_grad:
                dg_ref[...] += res[n_row_out]

    out_specs = [row] * n_row_out + ([vec] if with_gain_grad else [])
    out_shape = [jax.ShapeDtypeStruct((t, d), F32)] * n_row_out
    out_shape += [jax.ShapeDtypeStruct((1, d), F32)] if with_gain_grad else []
    return pl.pallas_call(
        body, name=name, grid=(t // tm, nk),
        in_specs=[pl.BlockSpec((tm, tk), lambda i, kk: (i, kk)), pl.BlockSpec((tk, d), lambda i, kk: (kk, 0))]
        + [row] * n_row + [vec],
        out_specs=out_specs, out_shape=out_shape, scratch_shapes=[pltpu.VMEM((tm, d), F32)],
        compiler_params=_params(("arbitrary", "arbitrary")))(a, w, *row_ins, g)


def _rms_stats(x, n_real=None):
    n = x.shape[-1] if n_real is None else n_real
    return lax.rsqrt(jnp.sum(x * x, axis=-1, keepdims=True) / n + EPS)


def _rms_bwd(x, r, g, dz, n_real=None):
    n = x.shape[-1] if n_real is None else n_real
    xh = x * r
    dxh = dz * g
    dx = r * (dxh - xh * (jnp.sum(dxh * xh, axis=-1, keepdims=True) / n))
    return dx, jnp.sum(dz * xh, axis=0, keepdims=True)


def _sigmoid(x):
    return 0.5 * jnp.tanh(0.5 * x) + 0.5


def _roll(x, s, axis):
    return pltpu.roll(x, s, axis)


def _rope(x, c, s1, s2):
    return x * c + _roll(x, HEAD_LANES - MLA_ROPE // 2, 1) * s1 + _roll(x, MLA_ROPE // 2, 1) * s2


def _heads_apply(x, fn):
    return jnp.concatenate([fn(x[:, h * HEAD_LANES:(h + 1) * HEAD_LANES]) for h in range(N_HEADS)], axis=1)


ROW_CHUNK = 256


def _row_chunks(rows):
    step = min(ROW_CHUNK, rows)
    return [pl.ds(r, step) for r in range(0, rows, step)]


def _ffn_fwd(tag, x, g_pre, wg, wu, wd, g_post, tm, loss_target=None, carried_up=None):
    t, d = x.shape
    ns, fs, _ = wg.shape
    nt = t // tm
    row = pl.BlockSpec((tm, d), lambda i, q: (i, 0))
    vec = pl.BlockSpec((1, d), lambda i, q: (0, 0))
    act3 = pl.BlockSpec((1, tm, fs), lambda i, q: (q, i, 0))
    wrow = pl.BlockSpec((1, fs, d), lambda i, q: (q, 0, 0))
    nt_dims = (((1,), (1,)), ((), ()))

    def gate_up(x_ref, g_ref, wg_ref, wu_ref, n_ref, sl_ref, ud_ref, s_ref, n_s):
        @pl.when(pl.program_id(1) == 0)
        def _():
            for r in _row_chunks(tm):
                xb = x_ref[r, :]
                n_s[r, :] = (xb * _rms_stats(xb) * g_ref[...]).astype(MM_DTYPE)
            n_ref[...] = n_s[...]

        for r in _row_chunks(tm):
            n = n_s[r, :]
            a = lax.dot_general(n, wg_ref[0], nt_dims, preferred_element_type=F32)
            u = lax.dot_general(n, wu_ref[0], nt_dims, preferred_element_type=F32)
            sg = _sigmoid(a)
            sl = a * sg
            sl_ref[0, r, :] = sl.astype(sl_ref.dtype)
            ud_ref[0, r, :] = (u * (sg + sl * (1.0 - sg))).astype(ud_ref.dtype)
            s_ref[0, r, :] = (sl * u).astype(s_ref.dtype)

    gate_up, x_in, x_out, x_shape, x_scr, x_args = _carry(gate_up, 4, 4, (nt, ns), carried_up)
    n, sl, ud, s, *from_up = pl.pallas_call(
        gate_up, name=tag + "_gate_up", grid=(nt, ns), in_specs=[row, vec, wrow, wrow] + x_in,
        out_specs=[row, act3, act3, act3] + x_out,
        out_shape=[jax.ShapeDtypeStruct((t, d), MM_DTYPE)] + [jax.ShapeDtypeStruct((ns, t, fs), MM_DTYPE)] * 3 + x_shape,
        scratch_shapes=[pltpu.VMEM((tm, d), MM_DTYPE)] + x_scr,
        compiler_params=_params(("arbitrary", "arbitrary")))(x, g_pre, wg, wu, *x_args)
    if wd is None:
        wd = from_up[0]

    def down(s_ref, wd_ref, x_ref, g_ref, *rest):
        if loss_target is not None:
            tgt_ref, h_ref, y_ref, ls_ref, acc = rest
        else:
            h_ref, y_ref, acc = rest
        i, q = pl.program_id(0), pl.program_id(1)

        @pl.when(q == 0)
        def _():
            acc[...] = jnp.zeros(acc.shape, F32)

        if loss_target is not None:
            @pl.when((i == 0) & (q == 0))
            def _():
                ls_ref[...] = jnp.zeros(ls_ref.shape, F32)

        for r in _row_chunks(tm):
            acc[r, :] += jnp.dot(s_ref[0, r, :], wd_ref[0], preferred_element_type=F32)

        @pl.when(q == ns - 1)
        def _():
            for r in _row_chunks(tm):
                hb = acc[r, :]
                h_ref[r, :] = hb
                yb = x_ref[r, :] + 0.5 * (hb * _rms_stats(hb) * g_ref[...])
                if loss_target is None:
                    y_ref[r, :] = yb
                else:
                    e = yb - tgt_ref[r, :]
                    y_ref[r, :] = e * (1.0 / d)
                    ls_ref[...] += jnp.sum(e * e, axis=0, keepdims=True)

    with_loss = loss_target is not None
    outs = pl.pallas_call(
        down, name=tag + "_down", grid=(nt, ns), in_specs=[act3, wrow, row, vec] + ([row] if with_loss else []),
        out_specs=[row, row] + ([vec] if with_loss else []),
        out_shape=[jax.ShapeDtypeStruct((t, d), F32)] * 2 + ([jax.ShapeDtypeStruct((1, d), F32)] if with_loss else []),
        scratch_shapes=[pltpu.VMEM((tm, d), F32)],
        compiler_params=_params(("arbitrary", "arbitrary")))(s, wd, x, g_post, *([loss_target] if with_loss else []))
    if with_loss:
        h, dy, lsum = outs
        return (dy, lsum), (x, n, sl, ud, s, h), wd
    h, y = outs
    return y, (x, n, sl, ud, s, h), wd


def _carry(body, n_in, n_out, grid, carried):
    if carried is None:
        return body, [], [], [], [], []
    nx_in, nx_out = len(carried.ins), len(carried.outs)

    def wrapped(*refs):
        ins, rest = refs[:n_in], refs[n_in:]
        xi, rest = rest[:nx_in], rest[nx_in:]
        outs, rest = rest[:n_out], rest[n_out:]
        xo, rest = rest[:nx_out], rest[nx_out:]
        scr, sems = rest[:len(rest) - 2], rest[len(rest) - 2:]
        first, last = True, True
        for dim, size in enumerate(grid):
            first = first & (pl.program_id(dim) == 0)
            last = last & (pl.program_id(dim) == size - 1)

        @pl.when(first)
        def _():
            carried.start(xi, xo, *sems)

        body(*ins, *outs, *scr)

        @pl.when(last)
        def _():
            carried.finish(xi, xo, *sems)

    sems = [pltpu.SemaphoreType.DMA((carried.n_sem,)), pltpu.SemaphoreType.DMA((carried.n_sem,))]
    return (wrapped, [HBM_SPEC] * nx_in, [HBM_SPEC] * nx_out, list(carried.outs), sems, list(carried.ins))


def _ffn_bwd(tag, dy, saved, g_pre, wg, wu, wd, g_post, tm, tk, carried_down=None, make_carried_mid=None,
             make_carried_up=None):
    x, n, sl, ud, s, h = saved
    t, d = x.shape
    ns, fs, _ = wg.shape
    nt, nk = t // tm, t // tk
    row = pl.BlockSpec((tm, d), lambda i, q: (i, 0))
    vec = pl.BlockSpec((1, d), lambda i, q: (0, 0))
    act3 = pl.BlockSpec((1, tm, fs), lambda i, q: (q, i, 0))
    wrow = pl.BlockSpec((1, fs, d), lambda i, q: (q, 0, 0))
    nt_dims = (((1,), (1,)), ((), ()))
    tn_dims = (((0,), (0,)), ((), ()))

    def down_b(h_ref, dy_ref, g_ref, wd_ref, sl_ref, ud_ref, dh_ref, da_ref, du_ref, dg_ref, dh_s):
        i, q = pl.program_id(0), pl.program_id(1)

        @pl.when((i == 0) & (q == 0))
        def _():
            dg_ref[...] = jnp.zeros(dg_ref.shape, F32)

        @pl.when(q == 0)
        def _():
            for r in _row_chunks(tm):
                hb = h_ref[r, :]
                dh, dg = _rms_bwd(hb, _rms_stats(hb), g_ref[...], 0.5 * dy_ref[r, :])
                dh_s[r, :] = dh.astype(MM_DTYPE)
                dg_ref[...] += dg
            dh_ref[...] = dh_s[...]

        for r in _row_chunks(tm):
            ds = lax.dot_general(dh_s[r, :], wd_ref[0], nt_dims, preferred_element_type=F32)
            da_ref[0, r, :] = (ds * ud_ref[0, r, :].astype(F32)).astype(da_ref.dtype)
            du_ref[0, r, :] = (ds * sl_ref[0, r, :].astype(F32)).astype(du_ref.dtype)

    down_b, x_in, x_out, x_shape, x_scr, x_args = _carry(down_b, 6, 4, (nt, ns), carried_down)
    dh, da, du, dg_post, *from_down = pl.pallas_call(
        down_b, name=tag + "_down_b", grid=(nt, ns), in_specs=[row, row, vec, wrow, act3, act3] + x_in,
        out_specs=[row, act3, act3, vec] + x_out,
        out_shape=[jax.ShapeDtypeStruct((t, d), MM_DTYPE)] + [jax.ShapeDtypeStruct((ns, t, fs), MM_DTYPE)] * 2
        + [jax.ShapeDtypeStruct((1, d), F32)] + x_shape,
        scratch_shapes=[pltpu.VMEM((tm, d), MM_DTYPE)] + x_scr,
        compiler_params=_params(("arbitrary", "arbitrary")))(h, dy, g_post, wd, sl, ud, *x_args)

    def down_w(s_ref, dh_ref, dw_ref, acc):
        kk = pl.program_id(1)

        @pl.when(kk == 0)
        def _():
            acc[...] = jnp.zeros(acc.shape, F32)

        acc[...] += lax.dot_general(s_ref[0], dh_ref[...], tn_dims, preferred_element_type=F32)

        @pl.when(kk == nk - 1)
        def _():
            dw_ref[0] = acc[...].astype(dw_ref.dtype)

    dwd = pl.pallas_call(
        down_w, name=tag + "_down_w", grid=(ns, nk),
        in_specs=[pl.BlockSpec((1, tk, fs), lambda q, kk: (q, kk, 0)), pl.BlockSpec((tk, d), lambda q, kk: (kk, 0))],
        out_specs=pl.BlockSpec((1, fs, d), lambda q, kk: (q, 0, 0)),
        out_shape=jax.ShapeDtypeStruct((ns, fs, d), MM_DTYPE), scratch_shapes=[pltpu.VMEM((fs, d), F32)],
        compiler_params=_params(("parallel", "arbitrary")))(s, dh)

    def gate_up_b(da_ref, du_ref, wg_ref, wu_ref, x_ref, dy_ref, g_ref, dx_ref, dg_ref, acc):
        i, q = pl.program_id(0), pl.program_id(1)

        @pl.when((i == 0) & (q == 0))
        def _():
            dg_ref[...] = jnp.zeros(dg_ref.shape, F32)

        @pl.when(q == 0)
        def _():
            acc[...] = jnp.zeros(acc.shape, F32)

        for r in _row_chunks(tm):
            acc[r, :] += (jnp.dot(da_ref[0, r, :], wg_ref[0], preferred_element_type=F32)
                          + jnp.dot(du_ref[0, r, :], wu_ref[0], preferred_element_type=F32))

        @pl.when(q == ns - 1)
        def _():
            for r in _row_chunks(tm):
                xb = x_ref[r, :]
                dx, dg = _rms_bwd(xb, _rms_stats(xb), g_ref[...], acc[r, :])
                dx_ref[r, :] = dy_ref[r, :] + dx
                dg_ref[...] += dg

    def gate_up_w(n_ref, da_ref, du_ref, dwg_ref, dwu_ref, acc_g, acc_u):
        kk = pl.program_id(1)

        @pl.when(kk == 0)
        def _():
            acc_g[...] = jnp.zeros(acc_g.shape, F32)
            acc_u[...] = jnp.zeros(acc_u.shape, F32)

        nb = n_ref[...]
        acc_g[...] += lax.dot_general(da_ref[0], nb, tn_dims, preferred_element_type=F32)
        acc_u[...] += lax.dot_general(du_ref[0], nb, tn_dims, preferred_element_type=F32)

        @pl.when(kk == nk - 1)
        def _():
            dwg_ref[0] = acc_g[...].astype(dwg_ref.dtype)
            dwu_ref[0] = acc_u[...].astype(dwu_ref.dtype)

    k3 = pl.BlockSpec((1, tk, fs), lambda q, kk: (q, kk, 0))
    wout = pl.BlockSpec((1, fs, d), lambda q, kk: (q, 0, 0))
    carried_mid = make_carried_mid(dwd) if make_carried_mid else None
    gate_up_w, x_in, x_out, x_shape, x_scr, x_args = _carry(gate_up_w, 3, 2, (ns, nk), carried_mid)
    dwg, dwu, *from_mid = pl.pallas_call(
        gate_up_w, name=tag + "_gate_up_w", grid=(ns, nk),
        in_specs=[pl.BlockSpec((tk, d), lambda q, kk: (kk, 0)), k3, k3] + x_in, out_specs=[wout, wout] + x_out,
        out_shape=[jax.ShapeDtypeStruct((ns, fs, d), MM_DTYPE)] * 2 + x_shape,
        scratch_shapes=[pltpu.VMEM((fs, d), F32)] * 2 + x_scr,
        compiler_params=_params(("arbitrary", "arbitrary")))(n, da, du, *x_args)

    carried_up = make_carried_up(dwg, dwu) if make_carried_up else None
    gate_up_b, x_in, x_out, x_shape, x_scr, x_args = _carry(gate_up_b, 7, 2, (nt, ns), carried_up)
    dx, dg_pre, *from_up = pl.pallas_call(
        gate_up_b, name=tag + "_gate_up_b", grid=(nt, ns), in_specs=[act3, act3, wrow, wrow, row, row, vec] + x_in,
        out_specs=[row, vec] + x_out,
        out_shape=[jax.ShapeDtypeStruct((t, d), F32), jax.ShapeDtypeStruct((1, d), F32)] + x_shape,
        scratch_shapes=[pltpu.VMEM((tm, d), F32)] + x_scr,
        compiler_params=_params(("arbitrary", "arbitrary")))(da, du, wg, wu, x, dy, g_pre, *x_args)
    return dx, dg_pre, dwg, dwu, dwd, dg_post, from_down, from_mid, from_up


NEG = -1e30


def _attn_scale():
    return (MLA_NOPE + MLA_ROPE) ** -0.5


def _causal_pairs(nq, by_key):
    if by_key:
        pairs = [(qi, ki) for ki in range(nq) for qi in range(ki, nq)]
    else:
        pairs = [(qi, ki) for qi in range(nq) for ki in range(qi + 1)]
    return jnp.asarray([p[0] for p in pairs], jnp.int32), jnp.asarray([p[1] for p in pairs], jnp.int32)


def _below_diagonal(shape):
    return lax.broadcasted_iota(jnp.int32, shape, 1) <= lax.broadcasted_iota(jnp.int32, shape, 0)


def _attn_call(name, body, tables, args, in_kinds, out_kinds, scratch, t, tq, carried=None):
    qmap = lambda h, p, qt, kt: (qt[p], h)
    kmap = lambda h, p, qt, kt: (kt[p], h)
    width = HEADS_PER_STEP * HEAD_LANES
    spec = lambda kind: pl.BlockSpec((tq, width), qmap if kind == "q" else kmap)
    n_pairs = tables[0].shape[0]
    n_groups = N_HEADS // HEADS_PER_STEP
    n_in, n_out, n_scr = len(in_kinds), len(out_kinds), scratch
    x_ins = list(carried.ins) if carried else []
    x_outs = list(carried.outs) if carried else []
    x_scr = [pltpu.SemaphoreType.DMA((carried.n_sem,)), pltpu.SemaphoreType.DMA((carried.n_sem,))] if carried else []

    def full_body(qt, kt, *refs):
        ins, refs = refs[:n_in], refs[n_in:]
        xi, refs = refs[:len(x_ins)], refs[len(x_ins):]
        outs, refs = refs[:n_out], refs[n_out:]
        xo, refs = refs[:len(x_outs)], refs[len(x_outs):]
        scr, sems = refs[:n_scr], refs[n_scr:]
        if carried:
            @pl.when((pl.program_id(0) == 0) & (pl.program_id(1) == 0))
            def _():
                carried.start(xi, xo, *sems)

        heads = [tuple(r.at[:, pl.ds(hh * HEAD_LANES, HEAD_LANES)] for r in (*ins, *outs, *scr))
                 for hh in range(HEADS_PER_STEP)]
        body(qt, kt, heads)
        if carried:
            @pl.when((pl.program_id(0) == n_groups - 1) & (pl.program_id(1) == n_pairs - 1))
            def _():
                carried.finish(xi, xo, *sems)

    grid_spec = pltpu.PrefetchScalarGridSpec(
        num_scalar_prefetch=2, grid=(n_groups, n_pairs),
        in_specs=[spec(kd) for kd in in_kinds] + [HBM_SPEC] * len(x_ins),
        out_specs=[spec(kd) for kd in out_kinds] + [HBM_SPEC] * len(x_outs),
        scratch_shapes=[pltpu.VMEM((tq, width), F32)] * n_scr + x_scr)
    return pl.pallas_call(full_body, name=name, grid_spec=grid_spec,
                          out_shape=[jax.ShapeDtypeStruct((t, MLA_PAD), F32) for _ in out_kinds] + x_outs,
                          compiler_params=_params(("arbitrary", "arbitrary")))(*tables, *args, *x_ins)


class _Carried:
    def __init__(self, ins, outs, n_sem, start, finish):
        self.ins, self.outs, self.n_sem, self.start, self.finish = ins, outs, n_sem, start, finish


def _attn_fwd(q, k, v, tq, carried=None):
    t = q.shape[0]
    nq = t // tq

    def body(qt, kt, heads):
        p_id = pl.program_id(1)
        qi, ki = qt[p_id], kt[p_id]

        @pl.when(ki == 0)
        def _():
            for _, _, _, _, _, m_s, l_s, acc_s in heads:
                m_s[...] = jnp.full(m_s.shape, NEG, F32)
                l_s[...] = jnp.zeros(l_s.shape, F32)
                acc_s[...] = jnp.zeros(acc_s.shape, F32)

        def update(diagonal):
            for q_ref, k_ref, v_ref, _, _, m_s, l_s, acc_s in heads:
                s = lax.dot_general(q_ref[...], k_ref[...], (((1,), (1,)), ((), ())), preferred_element_type=F32)
                if diagonal:
                    s = jnp.where(_below_diagonal(s.shape), s, NEG)
                m_old = m_s[...]
                m_new = jnp.maximum(m_old, jnp.max(s, axis=1, keepdims=True))
                alpha = jnp.exp(m_old - m_new)
                p = jnp.exp(s - m_new[:, :1])
                l_s[...] = l_s[...] * alpha + jnp.sum(p, axis=1, keepdims=True)
                acc_s[...] = acc_s[...] * alpha + jnp.dot(p.astype(MM_DTYPE), v_ref[...], preferred_element_type=F32)
                m_s[...] = m_new

        @pl.when(ki < qi)
        def _():
            update(False)

        @pl.when(ki == qi)
        def _():
            update(True)
            for _, _, _, o_ref, lse_ref, m_s, l_s, acc_s in heads:
                o_ref[...] = acc_s[...] / l_s[...]
                lse_ref[...] = m_s[...] + jnp.log(l_s[...])

    return _attn_call("mla_attn_fwd", body, _causal_pairs(nq, False), (q, k, v), "qkk", "qq", 3, t, tq, carried)


def _attn_probs(q, k, lse, diagonal):
    s = lax.dot_general(q, k, (((1,), (1,)), ((), ())), preferred_element_type=F32)
    p = jnp.exp(s - lse[:, :1])
    return jnp.where(_below_diagonal(s.shape), p, 0.0) if diagonal else p


BWD_HEADS = 2


def _attn_bwd(q, k, v, do, lse, delta, tq, carried=None):
    t = q.shape[0]
    nq = t // tq
    width = BWD_HEADS * HEAD_LANES
    n_groups = N_HEADS // BWD_HEADS
    qt_tab, kt_tab = _causal_pairs(nq, True)
    n_pairs = qt_tab.shape[0]
    qmap = lambda h, p, qt, kt: (qt[p], h)
    kmap = lambda h, p, qt, kt: (kt[p], h)
    qs, ks = pl.BlockSpec((tq, width), qmap), pl.BlockSpec((tq, width), kmap)
    x_ins = list(carried.ins) if carried else []
    x_outs = list(carried.outs) if carried else []
    x_scr = [pltpu.SemaphoreType.DMA((carried.n_sem,)), pltpu.SemaphoreType.DMA((carried.n_sem,))] if carried else []
    nt_dims = (((1,), (1,)), ((), ()))
    tn_dims = (((0,), (0,)), ((), ()))

    def body(qt, kt, q_ref, k_ref, v_ref, do_ref, lse_ref, dl_ref, *rest):
        xi, rest = rest[:len(x_ins)], rest[len(x_ins):]
        dq_hbm, dk_ref, dv_ref = rest[:3]
        xo, rest = rest[3:3 + len(x_outs)], rest[3 + len(x_outs):]
        dk_s, dv_s, dq_s, dq_sem = rest[:4]
        sems = rest[4:]
        grp, p_id = pl.program_id(0), pl.program_id(1)
        qi, ki = qt[p_id], kt[p_id]
        if carried:
            @pl.when((grp == 0) & (p_id == 0))
            def _():
                carried.start(xi, xo, *sems)

        @pl.when(p_id == 0)
        def _():
            dq_s[...] = jnp.zeros(dq_s.shape, F32)

        def step(diagonal):
            rows = pl.ds(pl.multiple_of(qi * tq, tq), tq)
            for hh in range(BWD_HEADS):
                ln = pl.ds(hh * HEAD_LANES, HEAD_LANES)
                qb, kb, vb, dob = q_ref[:, ln], k_ref[:, ln], v_ref[:, ln], do_ref[:, ln]
                p = _attn_probs(qb, kb, lse_ref[:, ln], diagonal)
                dv_s[:, ln] += lax.dot_general(p.astype(MM_DTYPE), dob, tn_dims, preferred_element_type=F32)
                dp = lax.dot_general(dob, vb, nt_dims, preferred_element_type=F32)
                ds = (p * (dp - dl_ref[:, ln][:, :1])).astype(MM_DTYPE)
                dk_s[:, ln] += lax.dot_general(ds, qb, tn_dims, preferred_element_type=F32)
                dq_s[rows, ln] += jnp.dot(ds, kb, preferred_element_type=F32)

        @pl.when(qi == ki)
        def _():
            dk_s[...] = jnp.zeros(dk_s.shape, F32)
            dv_s[...] = jnp.zeros(dv_s.shape, F32)
            step(True)

        @pl.when(qi > ki)
        def _():
            step(False)

        @pl.when(qi == nq - 1)
        def _():
            dk_ref[...] = dk_s[...]
            dv_ref[...] = dv_s[...]

        @pl.when(p_id == n_pairs - 1)
        def _():
            out = pltpu.make_async_copy(dq_s, dq_hbm.at[pl.ds(pl.multiple_of(grp * t, t), t)], dq_sem)
            out.start()
            out.wait()

        if carried:
            @pl.when((grp == n_groups - 1) & (p_id == n_pairs - 1))
            def _():
                carried.finish(xi, xo, *sems)

    grid_spec = pltpu.PrefetchScalarGridSpec(
        num_scalar_prefetch=2, grid=(n_groups, n_pairs),
        in_specs=[qs, ks, ks, qs, qs, qs] + [HBM_SPEC] * len(x_ins),
        out_specs=[HBM_SPEC, ks, ks] + [HBM_SPEC] * len(x_outs),
        scratch_shapes=[pltpu.VMEM((tq, width), F32), pltpu.VMEM((tq, width), F32), pltpu.VMEM((t, width), F32),
                        pltpu.SemaphoreType.DMA] + x_scr)
    return pl.pallas_call(
        body, name="mla_attn_bwd", grid_spec=grid_spec,
        out_shape=[jax.ShapeDtypeStruct((n_groups * t, width), F32), jax.ShapeDtypeStruct((t, MLA_PAD), F32),
                   jax.ShapeDtypeStruct((t, MLA_PAD), F32)] + x_outs,
        compiler_params=_params(("arbitrary", "arbitrary")))(qt_tab, kt_tab, q, k, v, do, lse, delta, *x_ins)


def _dot01(a, b, dims=(((1,), (0,)), ((), ())), ones="rhs"):
    val, sel = (a, b) if ones == "rhs" else (b, a)
    head = val.astype(BF16)
    tail = (val - head.astype(F32)).astype(BF16)
    sel = sel.astype(BF16)
    dot = lambda part: (lax.dot_general(part, sel, dims, preferred_element_type=F32) if ones == "rhs"
                        else lax.dot_general(sel, part, dims, preferred_element_type=F32))
    return dot(head) + dot(tail)


def _dot1(a, b, dims=(((1,), (0,)), ((), ()))):
    return lax.dot_general(a.astype(MM_DTYPE), b.astype(MM_DTYPE), dims, preferred_element_type=F32)


def _dot3(a, b, dims=(((1,), (0,)), ((), ()))):
    return lax.dot_general(a, b, dims, preferred_element_type=F32, precision=lax.Precision.HIGH)


NN3 = (((2,), (1,)), ((0,), (0,)))
NT3 = (((2,), (2,)), ((0,), (0,)))
TN3 = (((1,), (1,)), ((0,), (0,)))


def _tri_masks(nh):
    shape = (nh, CHUNK, CHUNK)
    return lax.broadcasted_iota(jnp.int32, shape, 1), lax.broadcasted_iota(jnp.int32, shape, 2)


def _gdn_chunk_common(k, gcc, bb, row, col, dot=_dot1):
    tril = row >= col
    gcr = jnp.swapaxes(gcc, 1, 2)
    dm = jnp.exp(jnp.where(tril, gcc - gcr, NEG))
    kb = k * bb
    lm = jnp.where(row > col, dot(kb, k, NT3) * dm, 0.0)
    return dm, kb, lm


def _unit_lower_inverse(lm, eye):
    t = eye - lm
    p = lm
    for _ in range(CHUNK.bit_length() - 2):
        p = _dot3(p, p, NN3)
        t = t + _dot3(t, p, NN3)
    return t


def _chunk_sum_matrix(tb, upper):
    r = lax.broadcasted_iota(jnp.int32, (tb, tb), 0)
    c = lax.broadcasted_iota(jnp.int32, (tb, tb), 1)
    same = (r // CHUNK) == (c // CHUNK)
    return (same & ((c >= r) if upper else (c <= r))).astype(F32)


def _gdn_fwd(q, k, v, gb, bb, carried=None):
    nh, t, dh = q.shape
    nchunk = t // CHUNK

    def body(q_ref, k_ref, v_ref, g_ref, b_ref, o_ref, sall_ref, tall_ref, s_s):
        @pl.when(pl.program_id(0) == 0)
        def _():
            s_s[...] = jnp.zeros(s_s.shape, F32)

        row, col = _tri_masks(nh)
        sh = s_s[...]
        for cc in range(cps):
            rows = pl.ds(cc * CHUNK, CHUNK)
            qh, kh, vh, bbh, gcc = q_ref[:, rows, :], k_ref[:, rows, :], v_ref[:, rows, :], b_ref[:, rows, :], \
                g_ref[:, rows, :]
            dm, kb, lm = _gdn_chunk_common(kh, gcc, bbh, row, col)
            eg = jnp.exp(gcc)
            glr = gcc[:, CHUNK - 1:CHUNK, :]
            th = _unit_lower_inverse(lm, (row == col).astype(F32))
            w = _dot1(th, kb * eg, NN3)
            u = _dot1(th, vh * bbh, NN3)
            at = jnp.where(row >= col, _dot1(qh, kh, NT3) * dm, 0.0)
            vn = u - _dot1(w, sh, NN3)
            o_ref[:, rows, :] = _dot1(qh * eg, sh, NN3) + _dot1(at, vn, NN3)
            kd = kh * jnp.exp(glr - gcc)
            sall_ref[:, cc] = sh
            tall_ref[:, rows, :] = th
            sh = sh * jnp.exp(glr) + _dot1(kd, vn, TN3)
        s_s[...] = sh

    cps = min(GDN_STEP_CHUNKS, nchunk)
    steps = nchunk // cps
    blk = pl.BlockSpec((nh, cps * CHUNK, dh), lambda n: (0, n, 0))
    body, x_in, x_out, x_shape, x_scr, x_args = _carry(body, 5, 3, (steps,), carried)
    return pl.pallas_call(
        body, name="gdn_fwd", grid=(steps,), in_specs=[blk] * 5 + x_in,
        out_specs=[blk, pl.BlockSpec((nh, cps, dh, dh), lambda n: (0, n, 0, 0)), blk] + x_out,
        out_shape=[jax.ShapeDtypeStruct((nh, t, dh), F32), jax.ShapeDtypeStruct((nh, nchunk, dh, dh), F32),
                   jax.ShapeDtypeStruct((nh, t, CHUNK), F32)] + x_shape,
        scratch_shapes=[pltpu.VMEM((nh, dh, dh), F32)] + x_scr,
        compiler_params=_params(("arbitrary",)))(q, k, v, gb, bb, *x_args)


def _gdn_bwd(q, k, v, gb, bb, sall, tall, do):
    nh, t, dh = q.shape
    nchunk = t // CHUNK

    def body(q_ref, k_ref, v_ref, g_ref, b_ref, sall_ref, tall_ref, do_ref,
             dq_ref, dk_ref, dv_ref, dg_ref, db_ref, ds_s):
        @pl.when(pl.program_id(0) == 0)
        def _():
            ds_s[...] = jnp.zeros(ds_s.shape, F32)

        row, col = _tri_masks(nh)
        tril, stril = row >= col, row > col
        rsum = lambda x: jnp.sum(x, axis=2, keepdims=True)
        dsp = ds_s[...]
        for cc in reversed(range(cps)):
            rows = pl.ds(cc * CHUNK, CHUNK)
            dsp = chunk_bwd(rows, cc, dsp, row, col, tril, stril, rsum, q_ref, k_ref, v_ref, g_ref, b_ref, sall_ref,
                            tall_ref, do_ref, dq_ref, dk_ref, dv_ref, dg_ref, db_ref)
        ds_s[...] = dsp

    def chunk_bwd(rows, cc, dsp, row, col, tril, stril, rsum, q_ref, k_ref, v_ref, g_ref, b_ref, sall_ref, tall_ref,
                  do_ref, dq_ref, dk_ref, dv_ref, dg_ref, db_ref):
        qh, kh, vh, gcc, bbh = q_ref[:, rows, :], k_ref[:, rows, :], v_ref[:, rows, :], g_ref[:, rows, :], \
            b_ref[:, rows, :]
        sh, th, doh = sall_ref[:, cc], tall_ref[:, rows, :], do_ref[:, rows, :]
        dm, kb, lm = _gdn_chunk_common(kh, gcc, bbh, row, col, _dot3)
        eg = jnp.exp(gcc)
        glr = gcc[:, CHUNK - 1:CHUNK, :]
        glv = jnp.exp(glr)
        egl = jnp.exp(glr - gcc)
        rw, ru = kb * eg, vh * bbh
        w, u = _dot3(th, rw, NN3), _dot3(th, ru, NN3)
        at = jnp.where(tril, _dot3(qh, kh, NT3) * dm, 0.0)
        qd, kd = qh * eg, kh * egl
        vn = u - _dot3(w, sh, NN3)
        dgl = jnp.sum(rsum(dsp * sh), axis=1, keepdims=True)
        dkd = _dot3(vn, dsp, NT3)
        dvn = _dot3(kd, dsp, NN3)
        dqd = _dot3(doh, sh, NT3)
        dat = jnp.where(tril, _dot3(doh, vn, NT3), 0.0)
        dvn = dvn + _dot3(at, doh, TN3)
        dw = -_dot3(dvn, sh, NT3)
        ds_before = dsp * glv + _dot3(qd, doh, TN3) - _dot3(w, dvn, TN3)
        dpa = dat * dm
        dq_ref[:, rows, :] = _dot1(dpa, kh, NN3) + dqd * eg
        dk = _dot1(dpa, qh, TN3) + dkd * egl
        t6 = rsum(dkd * kd)
        dgam = rsum(dqd * qd) - t6
        dgam_last = jnp.sum(t6, axis=1, keepdims=True) + dgl * glv
        drw = _dot3(th, dw, TN3)
        dru = _dot3(th, dvn, TN3)
        dl = -jnp.where(stril, _dot3(drw, w, NT3) + _dot3(dru, u, NT3), 0.0)
        dgam = dgam + rsum(drw * rw)
        dv_ref[:, rows, :] = dru * bbh
        dp2 = dl * dm
        dkb = drw * eg + _dot1(dp2, kh, NN3)
        dk_ref[:, rows, :] = dk + _dot1(dp2, kb, TN3) + dkb * bbh
        db_ref[:, rows, :] = rsum(dru * vh) + rsum(dkb * kh) + jnp.zeros((nh, CHUNK, dh), F32)
        e = dat * at + dl * lm
        dgam_b = dgam + rsum(e) - _dot01(e, jnp.ones((nh, CHUNK, CHUNK), F32), TN3)
        dg_ref[:, rows, :] = dgam_b + jnp.where(row == CHUNK - 1, dgam_last, 0.0)
        return ds_before

    cps = min(GDN_STEP_CHUNKS, nchunk)
    steps = nchunk // cps
    rev = lambda n: (0, steps - 1 - n, 0)
    blk = pl.BlockSpec((nh, cps * CHUNK, dh), rev)
    sblk = pl.BlockSpec((nh, cps, dh, dh), lambda n: (0, steps - 1 - n, 0, 0))
    out = jax.ShapeDtypeStruct((nh, t, dh), F32)
    return pl.pallas_call(
        body, name="gdn_bwd", grid=(steps,), in_specs=[blk] * 5 + [sblk, blk, blk], out_specs=[blk] * 5,
        out_shape=[out] * 5, scratch_shapes=[pltpu.VMEM((nh, dh, dh), F32)],
        compiler_params=_params(("arbitrary",)))(q, k, v, gb, bb, sall, tall, do)


def _group_ones():
    r = lax.broadcasted_iota(jnp.int32, (GDN_W, GDN_W), 0) // GDN_DH
    c = lax.broadcasted_iota(jnp.int32, (GDN_W, GDN_W), 1) // GDN_DH
    return (r == c).astype(F32)


def _conv_taps(x, xprev, w, has_prev):
    row = lax.broadcasted_iota(jnp.int32, x.shape, 0)
    out = x * w[GDN_CONV - 1:GDN_CONV, :]
    for s in range(1, GDN_CONV):
        sh = jnp.where(row >= s, _roll(x, s, 0), _roll(xprev, s, 0) * has_prev)
        out = out + sh * w[GDN_CONV - 1 - s:GDN_CONV - s, :]
    return out


def _head_cols(x, h):
    return x[:, h * GDN_DH:(h + 1) * GDN_DH]


def _heads_spec(tb):
    return pl.BlockSpec((N_HEADS, tb, GDN_DH), lambda i: (0, i, 0))


def _mixer_fwd(x, positions, w, tb, carried=None, carried_gdn=None):
    t, d = x.shape
    tables = _rope_tables(positions)

    hn, proj = _prologue_mm("mix_in", [x], w["mix_pre_g"], lambda xb, g: (xb * _rms_stats(xb) * g,), w["w_in_pad_t"],
                            False)

    def mla_pre(p0, gq, gkv):
        cq, ckv = p0[:, :MLA_Q_RANK], p0[:, MLA_Q_RANK:MLA_Q_RANK + MLA_KV_RANK]
        return cq * _rms_stats(cq) * gq, ckv * _rms_stats(ckv) * gkv

    nq, nkv = _rowwise("mla_pre", mla_pre, [(proj, 512, PIN_MLA // 512, 0)],
                       [w["mla_q_norm_g"], w["mla_kv_norm_g"]], [(MLA_Q_RANK, BF16), (MLA_KV_RANK, BF16)], [], tb)
    qraw = _mm("mla_uq", nq, w["w_uq_pad"], "nn", F32)
    kv = _mm("mla_ukv", nkv, w["w_kv_pad"], "nn", F32)

    def rope_f(qr, kn, vv, kpe, c, s1, s2):
        qo = _heads_apply(qr, lambda xh: _rope(xh, c, s1, s2)) * _attn_scale()
        kp = _rope(kpe, c, s1, s2)
        return qo, kn + jnp.tile(kp, (1, N_HEADS)), vv

    q, k, v = _rowwise("mla_rope", rope_f,
                       [qraw, (kv, MLA_PAD, 0, 0), (kv, MLA_PAD, 1, 0), (proj, HEAD_LANES, PIN_KPE // HEAD_LANES, 0),
                        tables[0], tables[1], tables[2]], [],
                       [(MLA_PAD, BF16)] * 3, [], tb // 2)
    tq = min(1024, t)
    o, lse, *carried_out = _attn_fwd(q, k, v, tq, carried)

    def mla_post(ob, g):
        return (ob * _rms_stats(ob, N_HEADS * MLA_V) * g,)

    (cat,) = _rowwise("mla_post", mla_post, [o], [w["mla_out_g_pad"]], [(MLA_PAD, BF16)], [], tb, wide=(CAT_W, 0))

    gones = _group_ones()
    steps = t // tb

    def gdn_pre(xq, xk, xv, pq, pk, pv, cw, go, has_prev):
        outs = []
        for j, (xc, xp) in enumerate(((xq, pq), (xk, pk), (xv, pv))):
            c = _conv_taps(xc, xp, cw[:, j * GDN_W:(j + 1) * GDN_W], has_prev)
            a = c * _sigmoid(c)
            if j < 2:
                rn = lax.rsqrt(_dot01(a * a, go) + EPS)
                a = a * rn
                if j == 0:
                    a = a * (GDN_DH ** -0.5)
            outs.append(a)
        return tuple(outs)

    qh, kh, vh = _gdn_pre_call("gdn_pre", gdn_pre, proj, w["conv_w"], gones, tb, steps)
    heads_shape = jax.ShapeDtypeStruct((N_HEADS, t, GDN_DH), F32)
    lanes_shape = jax.ShapeDtypeStruct((t, HEAD_LANES), F32)
    lanes_spec = pl.BlockSpec((tb, HEAD_LANES), lambda i: (i, 0))
    vec_spec = lambda n: pl.BlockSpec((1, n), lambda i: (0, 0))

    def gate_f(ab_ref, al_ref, dt_ref, g_ref, b_ref, gh_ref, bh_ref):
        g, b = _gb_fwd(ab_ref[...], al_ref[...], dt_ref[...])
        g_ref[...] = g
        b_ref[...] = b
        gc = _dot01(_chunk_sum_matrix(tb, False), g, ones="lhs")
        for h in range(N_HEADS):
            gh_ref[h] = jnp.broadcast_to(gc[:, h:h + 1], (tb, GDN_DH))
            bh_ref[h] = jnp.broadcast_to(b[:, N_HEADS + h:N_HEADS + h + 1], (tb, GDN_DH))

    g128, b128, gbh, bbh = pl.pallas_call(
        gate_f, name="gdn_gate_f", grid=(steps,),
        in_specs=[pl.BlockSpec((tb, HEAD_LANES), lambda i: (i, PIN_AB // HEAD_LANES)), vec_spec(HEAD_LANES),
                  vec_spec(HEAD_LANES)],
        out_specs=[lanes_spec, lanes_spec, _heads_spec(tb), _heads_spec(tb)],
        out_shape=[lanes_shape, lanes_shape, heads_shape, heads_shape],
        compiler_params=_params(("arbitrary",)))(proj, w["a_log_pad"], w["dt_bias_pad"])
    oh, sall, tall, *carried_out_gdn = _gdn_fwd(qh, kh, vh, gbh, bbh, carried_gdn)

    def gdn_post(o_ref, gt_ref, g_ref, cat_in, cat_ref):
        gt, g = gt_ref[...], g_ref[...]
        outs = []
        for h in range(N_HEADS):
            ob, gth = o_ref[h], _head_cols(gt, h)
            outs.append(ob * _rms_stats(ob) * g * (gth * _sigmoid(gth)))
        cat_ref[...] = jnp.concatenate(outs, axis=1).astype(cat_ref.dtype)

    gate_spec = pl.BlockSpec((tb, GDN_W), lambda i: (i, PIN_GATE // GDN_W))
    cat = pl.pallas_call(
        gdn_post, name="gdn_post", grid=(steps,),
        in_specs=[_heads_spec(tb), gate_spec, vec_spec(GDN_DH), ANY_SPEC],
        out_specs=pl.BlockSpec((tb, GDN_W), lambda i: (i, MLA_PAD // GDN_W)),
        out_shape=jax.ShapeDtypeStruct((t, CAT_W), BF16), input_output_aliases={3: 0},
        compiler_params=_params(("arbitrary",)))(oh, proj, w["gdn_norm_g"], cat)
    mixed, y = _mm_epilogue("mix_out", cat, w["w_out_pad"], [x], w["mix_post_g"],
                            lambda hb, xb, g: (hb, xb + hb * _rms_stats(hb) * g), 2, False, tk=CAT_W)
    saved = dict(x=x, hn=hn, proj=proj, nq=nq, nkv=nkv, q=q, k=k, v=v, o=o, lse=lse, qh=qh, kh=kh, vh=vh,
                 gbh=gbh, bbh=bbh, oh=oh, sall=sall, tall=tall, cat=cat, mixed=mixed,
                 tables=tables, g128=g128, b128=b128)
    return y, saved, list(carried_out) + list(carried_out_gdn)


def _qkv_specs(tb):
    base = PIN_QKV // GDN_W
    cur = [pl.BlockSpec((tb, GDN_W), lambda i, j=j: (i, base + j)) for j in range(3)]
    prev = [pl.BlockSpec((tb, GDN_W), lambda i, j=j: (jnp.maximum(i - 1, 0), base + j)) for j in range(3)]
    return cur + prev


def _gdn_pre_call(name, fn, proj, conv_w, gones, tb, steps):
    t = proj.shape[0]

    def body(xq, xk, xv, pq, pk, pv, cw, go, oq, ok, ov):
        has_prev = jnp.where(pl.program_id(0) == 0, 0.0, 1.0)
        outs = fn(xq[...], xk[...], xv[...], pq[...], pk[...], pv[...], cw[...], go[...], has_prev)
        for r, val in zip((oq, ok, ov), outs):
            for h in range(N_HEADS):
                r[h] = _head_cols(val, h)

    return pl.pallas_call(
        body, name=name, grid=(steps,),
        in_specs=_qkv_specs(tb) + [pl.BlockSpec(conv_w.shape, lambda i: (0, 0)),
                                   pl.BlockSpec(gones.shape, lambda i: (0, 0))],
        out_specs=[_heads_spec(tb)] * 3,
        out_shape=[jax.ShapeDtypeStruct((N_HEADS, t, GDN_DH), F32)] * 3,
        compiler_params=_params(("arbitrary",)))(proj, proj, proj, proj, proj, proj, conv_w, gones)


def _softplus(x):
    return jnp.maximum(x, 0.0) + jnp.log1p(jnp.exp(-jnp.abs(x)))


def _gb_fwd(ab, a_log, dt_bias):
    g = -jnp.exp(a_log) * _softplus(ab + dt_bias)
    return g, _sigmoid(ab)


def _rope_tables(positions):
    half = MLA_ROPE // 2
    freqs = ROPE_THETA ** (-jnp.arange(half, dtype=F32) / half)
    ang = positions.reshape(-1).astype(F32)[:, None] * freqs
    cos, sin = jnp.cos(ang), jnp.sin(ang)
    t = ang.shape[0]
    one = jnp.ones((t, MLA_NOPE), F32)
    z16, z32, z64 = jnp.zeros((t, half), F32), jnp.zeros((t, MLA_ROPE), F32), jnp.zeros((t, MLA_NOPE), F32)
    c = jnp.concatenate([one, cos, cos, jnp.ones((t, MLA_ROPE), F32)], axis=1)
    s1 = jnp.concatenate([z64, -sin, z16, z32], axis=1)
    s2 = jnp.concatenate([z64, z16, sin, z32], axis=1)
    return c, s1, s2


def _mixer_bwd(dy, sv, w, tb, carried=None):
    x, proj = sv["x"], sv["proj"]
    t, d = x.shape
    c, s1, s2 = sv["tables"]
    grads = {}

    dmixed, dcat, grads["mix_post_g"] = _prologue_mm(
        "mix_out_bx", [sv["mixed"], dy], w["mix_post_g"], lambda hb, dyb, g: _rms_bwd(hb, _rms_stats(hb), g, dyb),
        w["w_out_pad"], True)
    grads["w_out_pad"] = _mm("mix_out_bw", sv["cat"], dmixed, "tn", F32)
    steps = t // tb
    vec_spec = lambda n: pl.BlockSpec((1, n), lambda i: (0, 0))

    def gdn_post_b(o_ref, gt_ref, do_ref, g_ref, dproj_ref, doh_ref, dg_ref):
        @pl.when(pl.program_id(0) == 0)
        def _():
            dg_ref[...] = jnp.zeros(dg_ref.shape, F32)

        gt, dob, g = gt_ref[...], do_ref[...], g_ref[...]
        dgates = []
        for h in range(N_HEADS):
            ob, gth, dobh = o_ref[h], _head_cols(gt, h), _head_cols(dob, h)
            sg = _sigmoid(gth)
            r = _rms_stats(ob)
            dxo, dg = _rms_bwd(ob, r, g, dobh * (gth * sg))
            doh_ref[h] = dxo
            dg_ref[...] += dg
            dgates.append(dobh * (ob * r * g) * (sg * (1.0 + gth * (1.0 - sg))))
        dproj_ref[...] = jnp.concatenate(dgates, axis=1).astype(dproj_ref.dtype)

    dproj, doh, grads["gdn_norm_g"] = pl.pallas_call(
        gdn_post_b, name="gdn_post_b", grid=(steps,),
        in_specs=[_heads_spec(tb), pl.BlockSpec((tb, GDN_W), lambda i: (i, PIN_GATE // GDN_W)),
                  pl.BlockSpec((tb, GDN_W), lambda i: (i, MLA_PAD // GDN_W)), vec_spec(GDN_DH)],
        out_specs=[pl.BlockSpec((tb, GDN_W), lambda i: (i, PIN_GATE // GDN_W)), _heads_spec(tb), vec_spec(GDN_DH)],
        out_shape=[jax.ShapeDtypeStruct((t, PIN_W), BF16), jax.ShapeDtypeStruct((N_HEADS, t, GDN_DH), F32),
                   jax.ShapeDtypeStruct((1, GDN_DH), F32)],
        compiler_params=_params(("arbitrary",)))(sv["oh"], proj, dcat, w["gdn_norm_g"])

    def mla_post_b(ob, dmo, g):
        do, dg = _rms_bwd(ob, _rms_stats(ob, N_HEADS * MLA_V), g, dmo, N_HEADS * MLA_V)
        prod = do * ob
        delta = _heads_apply(prod, lambda ph: jnp.sum(ph, axis=1, keepdims=True) + jnp.zeros_like(ph))
        return do, delta, dg

    do, delta, grads["mla_out_g_pad"] = _rowwise(
        "mla_post_b", mla_post_b, [sv["o"], (dcat, MLA_PAD, 0, 0)], [w["mla_out_g_pad"]],
        [(MLA_PAD, BF16), (MLA_PAD, F32)], [(1, MLA_PAD)], tb // 2)
    tq = min(1024, t)
    dq, dk, dv, *carried_out = _attn_bwd(sv["q"], sv["k"], sv["v"], do, sv["lse"], delta, tq, carried)
    n_groups = N_HEADS // BWD_HEADS
    tr = tb // 2
    dq_groups = [(dq, BWD_HEADS * HEAD_LANES, 0, grp * (t // tr)) for grp in range(n_groups)]

    def rope_b(*blocks):
        dqb = jnp.concatenate(blocks[:n_groups], axis=1)
        dkb, dvb, cc, a1, a2 = blocks[n_groups:]
        dqr = _heads_apply(dqb * _attn_scale(), lambda xh: _rope(xh, cc, -a1, -a2))
        ksum = dkb[:, :HEAD_LANES]
        for h in range(1, N_HEADS):
            ksum = ksum + dkb[:, h * HEAD_LANES:(h + 1) * HEAD_LANES]
        lane = lax.broadcasted_iota(jnp.int32, ksum.shape, 1)
        keep = (lane >= MLA_NOPE) & (lane < MLA_NOPE + MLA_ROPE)
        dkpe = jnp.where(keep, _rope(ksum, cc, -a1, -a2), 0.0)
        return dqr, jnp.concatenate([dkb, dvb], axis=1), dkpe

    dqraw, dkv, dkpe = _rowwise("mla_rope_b", rope_b, dq_groups + [dk, dv, c, s1, s2], [],
                                [(MLA_PAD, BF16, t), (2 * MLA_PAD, BF16), (HEAD_LANES, F32)], [], tr)
    dnq = _mm("mla_uq_bx", dqraw, w["w_uq_pad"], "nt", F32)
    grads["w_uq_pad"] = _mm("mla_uq_bw", sv["nq"], dqraw, "tn", F32)
    dnkv = _mm("mla_ukv_bx", dkv, w["w_kv_pad"], "nt", F32)
    grads["w_kv_pad"] = _mm("mla_ukv_bw", sv["nkv"], dkv, "tn", F32)

    def mla_pre_b(p0, dnqb, dnkvb, dkpeb, gq, gkv):
        cq, ckv = p0[:, :MLA_Q_RANK], p0[:, MLA_Q_RANK:MLA_Q_RANK + MLA_KV_RANK]
        dcq, dgq = _rms_bwd(cq, _rms_stats(cq), gq, dnqb)
        dckv, dgkv = _rms_bwd(ckv, _rms_stats(ckv), gkv, dnkvb)
        return jnp.concatenate([dcq, dckv, dkpeb], axis=1), dgq, dgkv

    dproj, grads["mla_q_norm_g"], grads["mla_kv_norm_g"] = _rowwise(
        "mla_pre_b", mla_pre_b, [(proj, 512, PIN_MLA // 512, 0), dnq, dnkv, dkpe],
        [w["mla_q_norm_g"], w["mla_kv_norm_g"]], [(512, BF16)], [(1, MLA_Q_RANK), (1, MLA_KV_RANK)], tb,
        wide=(PIN_W, PIN_MLA // 512), carry=dproj)

    dqh, dkh, dvh, dgh, dbh = _gdn_bwd(sv["qh"], sv["kh"], sv["vh"], sv["gbh"], sv["bbh"], sv["sall"], sv["tall"], doh)
    gones = _group_ones()

    def gdn_pre_b(xq, xk, xv, pq, pk, pv, dq_, dk_, dv_, cw, go, has_prev):
        outs = []
        for j, (xc, xp, dd) in enumerate(((xq, pq, dq_), (xk, pk, dk_), (xv, pv, dv_))):
            cc = _conv_taps(xc, xp, cw[:, j * GDN_W:(j + 1) * GDN_W], has_prev)
            sg = _sigmoid(cc)
            a = cc * sg
            if j < 2:
                rn = lax.rsqrt(_dot01(a * a, go) + EPS)
                if j == 0:
                    dd = dd * (GDN_DH ** -0.5)
                da = rn * dd - a * (rn * rn * rn) * _dot01(dd * a, go)
            else:
                da = dd
            outs.append(da * (sg * (1.0 + cc * (1.0 - sg))))
        return tuple(outs)

    dcq, dck, dcv = _gdn_pre_b_call("gdn_pre_b", gdn_pre_b, proj, (dqh, dkh, dvh), w["conv_w"], gones, tb, steps)
    dproj, grads["conv_w"] = _conv_bwd_call("gdn_conv_b", proj, (dcq, dck, dcv), w["conv_w"], dproj, tb, steps)

    def gate_b(ab_ref, g_ref, b_ref, dgh_ref, dbh_ref, al_ref, dt_ref, carry_ref, dab_ref, dal_ref, ddt_ref):
        @pl.when(pl.program_id(0) == 0)
        def _():
            dal_ref[...] = jnp.zeros(dal_ref.shape, F32)
            ddt_ref[...] = jnp.zeros(ddt_ref.shape, F32)

        ab, g128, b128 = ab_ref[...], g_ref[...], b_ref[...]
        lane = lax.broadcasted_iota(jnp.int32, ab.shape, 1)
        dg_ = jnp.zeros(ab.shape, F32)
        db_ = jnp.zeros(ab.shape, F32)
        for h in range(N_HEADS):
            dg_ = dg_ + jnp.where(lane == h, jnp.broadcast_to(dgh_ref[h][:, 0:1], ab.shape), 0.0)
            db_ = db_ + jnp.where(lane == N_HEADS + h, jnp.broadcast_to(dbh_ref[h][:, 0:1], ab.shape), 0.0)
        dg_ = _dot01(_chunk_sum_matrix(tb, True), dg_, ones="lhs")
        slope = -jnp.exp(al_ref[...]) * _sigmoid(ab + dt_ref[...])
        dab_ref[...] = (dg_ * slope + db_ * b128 * (1.0 - b128)).astype(dab_ref.dtype)
        dal_ref[...] += jnp.sum(dg_ * g128, axis=0, keepdims=True)
        ddt_ref[...] += jnp.sum(dg_ * slope, axis=0, keepdims=True)

    lanes_spec = pl.BlockSpec((tb, HEAD_LANES), lambda i: (i, 0))
    ab_spec = pl.BlockSpec((tb, HEAD_LANES), lambda i: (i, PIN_AB // HEAD_LANES))
    dproj, grads["a_log_pad"], grads["dt_bias_pad"] = pl.pallas_call(
        gate_b, name="gdn_gate_b", grid=(steps,),
        in_specs=[ab_spec, lanes_spec, lanes_spec, _heads_spec(tb), _heads_spec(tb), vec_spec(HEAD_LANES),
                  vec_spec(HEAD_LANES), ANY_SPEC],
        out_specs=[ab_spec, vec_spec(HEAD_LANES), vec_spec(HEAD_LANES)],
        out_shape=[jax.ShapeDtypeStruct((t, PIN_W), BF16), jax.ShapeDtypeStruct((1, HEAD_LANES), F32),
                   jax.ShapeDtypeStruct((1, HEAD_LANES), F32)],
        input_output_aliases={7: 0},
        compiler_params=_params(("arbitrary",)))(proj, sv["g128"], sv["b128"], dgh, dbh, w["a_log_pad"],
                                                 w["dt_bias_pad"], dproj)
    grads["w_in_pad_t"] = _mm("mix_in_bw", dproj, sv["hn"], "tn", F32)

    def pre_b(dhn, xb, dyb, g):
        dx, dg = _rms_bwd(xb, _rms_stats(xb), g, dhn)
        return dyb + dx, dg

    dx, grads["mix_pre_g"] = _mm_epilogue("mix_in_bx", dproj, w["w_in_pad_t"], [x, dy], w["mix_pre_g"], pre_b, 1, True,
                                          tk=PIN_W // 3)
    return dx, grads, carried_out


def _gdn_pre_b_call(name, fn, proj, dd, conv_w, gones, tb, steps):
    t = proj.shape[0]

    def body(xq, xk, xv, pq, pk, pv, d0, d1, d2, cw, go, oq, ok, ov):
        has_prev = jnp.where(pl.program_id(0) == 0, 0.0, 1.0)
        dd_rows = [jnp.concatenate([dr[h] for h in range(N_HEADS)], axis=1) for dr in (d0, d1, d2)]
        outs = fn(xq[...], xk[...], xv[...], pq[...], pk[...], pv[...], *dd_rows, cw[...], go[...], has_prev)
        for r, val in zip((oq, ok, ov), outs):
            r[...] = val

    return pl.pallas_call(
        body, name=name, grid=(steps,),
        in_specs=_qkv_specs(tb) + [_heads_spec(tb)] * 3 + [pl.BlockSpec(conv_w.shape, lambda i: (0, 0)),
                                                          pl.BlockSpec(gones.shape, lambda i: (0, 0))],
        out_specs=[pl.BlockSpec((tb, GDN_W), lambda i: (i, 0))] * 3,
        out_shape=[jax.ShapeDtypeStruct((t, GDN_W), F32)] * 3,
        compiler_params=_params(("arbitrary",)))(proj, proj, proj, proj, proj, proj, *dd, conv_w, gones)


def _conv_bwd_call(name, proj, dc, conv_w, dproj, tb, steps):
    t = proj.shape[0]
    dcur = [pl.BlockSpec((tb, GDN_W), lambda i: (i, 0))] * 3
    dnext = [pl.BlockSpec((tb, GDN_W), lambda i: (jnp.minimum(i + 1, steps - 1), 0))] * 3

    def body(xq, xk, xv, pq, pk, pv, d0, d1, d2, n0, n1, n2, cw, carry_ref, dx_ref, dw_ref):
        i = pl.program_id(0)
        has_prev = jnp.where(i == 0, 0.0, 1.0)
        has_next = jnp.where(i == steps - 1, 0.0, 1.0)

        @pl.when(i == 0)
        def _():
            dw_ref[...] = jnp.zeros(dw_ref.shape, F32)

        wv = cw[...]
        dws, dxs = [], []
        for j, (xr, pr, dr, nr) in enumerate(((xq, pq, d0, n0), (xk, pk, d1, n1), (xv, pv, d2, n2))):
            x, xp, dcv, dnx = xr[...], pr[...], dr[...], nr[...]
            wj = wv[:, j * GDN_W:(j + 1) * GDN_W]
            row = lax.broadcasted_iota(jnp.int32, x.shape, 0)
            dx = dcv * wj[GDN_CONV - 1:GDN_CONV, :]
            rows_w = [jnp.sum(dcv * x, axis=0, keepdims=True)]
            for s in range(1, GDN_CONV):
                up = jnp.where(row < tb - s, _roll(dcv, tb - s, 0), _roll(dnx, tb - s, 0) * has_next)
                dx = dx + up * wj[GDN_CONV - 1 - s:GDN_CONV - s, :]
                sh = jnp.where(row >= s, _roll(x, s, 0), _roll(xp, s, 0) * has_prev)
                rows_w.append(jnp.sum(dcv * sh, axis=0, keepdims=True))
            dxs.append(dx)
            dws.append(jnp.concatenate(rows_w[::-1], axis=0))
        dx_ref[...] = jnp.concatenate(dxs, axis=1).astype(dx_ref.dtype)
        dw_ref[...] += jnp.concatenate(dws, axis=1)

    return pl.pallas_call(
        body, name=name, grid=(steps,),
        in_specs=_qkv_specs(tb) + dcur + dnext + [pl.BlockSpec(conv_w.shape, lambda i: (0, 0)), ANY_SPEC],
        out_specs=[pl.BlockSpec((tb, 3 * GDN_W), lambda i: (i, PIN_QKV // (3 * GDN_W))),
                   pl.BlockSpec(conv_w.shape, lambda i: (0, 0))],
        out_shape=[jax.ShapeDtypeStruct((t, PIN_W), BF16), jax.ShapeDtypeStruct(conv_w.shape, F32)],
        input_output_aliases={13: 0},
        compiler_params=_params(("arbitrary",)))(proj, proj, proj, proj, proj, proj, *dc, *dc, conv_w, dproj)


def _pad_heads_cols(wm, per_head):
    r = wm.shape[0]
    return jnp.pad(wm.reshape(r, N_HEADS, per_head), ((0, 0), (0, 0), (0, HEAD_LANES - per_head))).reshape(r, MLA_PAD)


def _unpad_heads_cols(wm, per_head):
    r = wm.shape[0]
    return wm.reshape(r, N_HEADS, HEAD_LANES)[:, :, :per_head].reshape(r, N_HEADS * per_head)


W_IN_COLS = MLA_Q_RANK + MLA_KV_RANK + MLA_ROPE + 3 * GDN_W + 2 * N_HEADS + GDN_W
W_IN_SHARD = W_IN_COLS // N_SHARD
W_IN_SHARD_PAD = 640
_Q0 = MLA_Q_RANK + MLA_KV_RANK
_Q1 = _Q0 + MLA_ROPE
_Q2 = _Q1 + 3 * GDN_W
_Q3 = _Q2 + 2 * N_HEADS
W_IN_SEGMENTS = [(0, _Q0, PIN_MLA), (_Q0, _Q1, PIN_KPE + MLA_NOPE), (_Q1, _Q2, PIN_QKV), (_Q2, _Q3, PIN_AB),
                 (_Q3, W_IN_COLS, PIN_GATE)]


def _win_pad_t(slabs):
    d = slabs.shape[2]
    pieces, at = [], 0
    for c0, c1, r0 in sorted(W_IN_SEGMENTS, key=lambda s: s[2]):
        if r0 > at:
            pieces.append(jnp.zeros((r0 - at, d), slabs.dtype))
        for q in range(N_SHARD):
            lo, hi = max(c0, q * W_IN_SHARD), min(c1, (q + 1) * W_IN_SHARD)
            if lo < hi:
                pieces.append(slabs[q, lo - q * W_IN_SHARD:hi - q * W_IN_SHARD])
        at = r0 + c1 - c0
    pieces.append(jnp.zeros((PIN_W - at, d), slabs.dtype))
    return jnp.concatenate(pieces, axis=0)


def _win_cols_t(wp_t, c_lo, c_hi):
    pieces = []
    for c0, c1, r0 in W_IN_SEGMENTS:
        lo, hi = max(c0, c_lo), min(c1, c_hi)
        if lo < hi:
            pieces.append(wp_t[r0 + lo - c0:r0 + hi - c0])
    return jnp.concatenate(pieces, axis=0)


def _wkv_to_pad(wkv):
    r = wkv.shape[0]
    w3 = wkv.reshape(r, N_HEADS, MLA_NOPE + MLA_V)
    kpart = jnp.pad(w3[:, :, :MLA_NOPE], ((0, 0), (0, 0), (0, HEAD_LANES - MLA_NOPE))).reshape(r, MLA_PAD)
    vpart = jnp.pad(w3[:, :, MLA_NOPE:], ((0, 0), (0, 0), (0, HEAD_LANES - MLA_V))).reshape(r, MLA_PAD)
    return jnp.concatenate([kpart, vpart], axis=1)


def _wkv_from_pad(wp):
    r = wp.shape[0]
    kpart = wp[:, :MLA_PAD].reshape(r, N_HEADS, HEAD_LANES)[:, :, :MLA_NOPE]
    vpart = wp[:, MLA_PAD:].reshape(r, N_HEADS, HEAD_LANES)[:, :, :MLA_V]
    return jnp.concatenate([kpart, vpart], axis=2).reshape(r, N_HEADS * (MLA_NOPE + MLA_V))


def _wout_to_pad(wo):
    n = wo.shape[1]
    mla = jnp.pad(wo[:N_HEADS * MLA_V].reshape(N_HEADS, MLA_V, n), ((0, 0), (0, HEAD_LANES - MLA_V), (0, 0)))
    return jnp.concatenate([mla.reshape(MLA_PAD, n), wo[N_HEADS * MLA_V:]], axis=0)


def _wout_from_pad(wp):
    n = wp.shape[1]
    mla = wp[:MLA_PAD].reshape(N_HEADS, HEAD_LANES, n)[:, :MLA_V].reshape(N_HEADS * MLA_V, n)
    return jnp.concatenate([mla, wp[MLA_PAD:]], axis=0)


def _pad_lanes(v, n):
    return jnp.pad(v, ((0, 0), (0, n - v.shape[1])))


def _compute_weights(full):
    w = {}
    for n in FFN_BIG:
        if n in full:
            w[n] = full[n].astype(MM_DTYPE)
    w["w_in_pad_t"] = _win_pad_t(full["w_in"]).astype(MM_DTYPE)
    w["w_uq_pad"] = _pad_heads_cols(full["mla_w_uq"], MLA_NOPE + MLA_ROPE).astype(MM_DTYPE)
    w["w_kv_pad"] = _wkv_to_pad(full["mla_w_ukv"]).astype(MM_DTYPE)
    w["w_out_pad"] = _wout_to_pad(full["w_out"]).astype(MM_DTYPE)
    w["conv_w"] = full["gdn_conv_w"].astype(F32)
    for n in ("ffn1_pre_g", "ffn1_post_g", "mix_pre_g", "mla_q_norm_g", "mla_kv_norm_g", "gdn_norm_g", "mix_post_g",
              "ffn2_pre_g", "ffn2_post_g"):
        w[n] = full[n]
    w["mla_out_g_pad"] = _pad_heads_cols(full["mla_out_g"], MLA_V)
    w["a_log_pad"] = _pad_lanes(full["gdn_a_log"], HEAD_LANES)
    w["dt_bias_pad"] = _pad_lanes(full["gdn_dt_bias"], HEAD_LANES)
    return w


FFN2_BIG = FFN_BIG[3:]


def _local_step(x, positions, loss_target, full, late=None):
    t, d = x.shape
    tb = min(512, t)
    tm = min(1024, t)
    tk = min(2048, t)
    w = _compute_weights(full)
    ffn = lambda tag: (w[tag + "_pre_g"], w[tag + "_w_gate"], w[tag + "_w_up"], w.get(tag + "_w_down"),
                       w[tag + "_post_g"])
    x1, sv1, w["ffn1_w_down"] = _ffn_fwd("ffn1", x, *ffn("ffn1"), tm,
                                         carried_up=_carried_gather([late[3]]) if late else None)
    x2, svm, gathered = _mixer_fwd(x1, positions, w, tb, _carried_gather(late[0][:2]) if late else None,
                                   _carried_gather(late[0][2:]) if late else None)
    for n, gw in zip(FFN2_BIG, gathered):
        w[n] = gw
    (dy, lsum), sv2, _ = _ffn_fwd("ffn2", x2, *ffn("ffn2"), tm, loss_target)
    g = {}
    dx2, g["ffn2_pre_g"], g["ffn2_w_gate"], g["ffn2_w_up"], g["ffn2_w_down"], g["ffn2_post_g"] = _ffn_bwd(
        "ffn2", dy, sv2, *ffn("ffn2"), tm, tk)[:6]

    def pair_sums(arrs, tag):
        got = _swap_halves(arrs, tag)
        return [_add_pair("add_pair%s_%d" % (tag, i), gi, gt, late[1]) for i, (gi, gt) in enumerate(zip(arrs, got))]

    def chip_sums(pairs, slabs, tag):
        return [_add_chips("add_chips%s_%d" % (tag, i), pr, sl, late[2]) for i, (pr, sl) in enumerate(zip(pairs, slabs))]

    if late:
        pairs2 = pair_sums([g[n] for n in FFN2_BIG], "_ffn2")
        dx1, gm, slabs2 = _mixer_bwd(dx2, svm, w, tb, _carried_scatter(pairs2))
        for n, hs in zip(FFN2_BIG, chip_sums(pairs2, slabs2, "_ffn2")):
            g[n] = hs
    else:
        dx1, gm, _ = _mixer_bwd(dx2, svm, w, tb)
    g["w_in"] = jnp.stack([jnp.pad(_win_cols_t(gm["w_in_pad_t"], q * W_IN_SHARD, (q + 1) * W_IN_SHARD),
                                   ((0, W_IN_SHARD_PAD - W_IN_SHARD), (0, 0))) for q in range(N_SHARD)])
    g["mla_w_uq"] = _unpad_heads_cols(gm["w_uq_pad"], MLA_NOPE + MLA_ROPE)
    g["mla_w_ukv"] = _wkv_from_pad(gm["w_kv_pad"])
    g["gdn_conv_w"] = gm["conv_w"]
    g["w_out"] = _wout_from_pad(gm["w_out_pad"])
    if late:
        quarters = [_pack([jnp.split(g[n], N_SHARD, axis=SHARD_AXIS[n])[q] for n in MIX_BIG], MM_DTYPE)
                    for q in range(N_SHARD)]
        pairs_m = pair_sums([g["w_in"].astype(MM_DTYPE), jnp.stack(quarters)], "_mix")
        pairs_d, pairs_gu = [], []

        def make_mid(dwd):
            pairs_d.extend(pair_sums([dwd], "_ffn1d"))
            return _carried_scatter(pairs_d)

        def make_up(dwg, dwu):
            pairs_gu.extend(pair_sums([dwg, dwu], "_ffn1"))
            return _carried_scatter(pairs_gu)

        dx0, g["ffn1_pre_g"], _, _, _, g["ffn1_post_g"], slabs_m, slabs_d, slabs_gu = _ffn_bwd(
            "ffn1", dx1, sv1, *ffn("ffn1"), tm, tk, _carried_scatter(pairs_m), make_mid, make_up)
        g["ffn1_w_gate"], g["ffn1_w_up"] = chip_sums(pairs_gu, slabs_gu, "_ffn1")
        g["ffn1_w_down"] = chip_sums(pairs_d, slabs_d, "_ffn1d")[0]
        g["w_in"], g["mix_pack"] = chip_sums(pairs_m, slabs_m, "_mix")
    else:
        dx0, g["ffn1_pre_g"], g["ffn1_w_gate"], g["ffn1_w_up"], g["ffn1_w_down"], g["ffn1_post_g"] = _ffn_bwd(
            "ffn1", dx1, sv1, *ffn("ffn1"), tm, tk)[:6]
    g["mix_pre_g"], g["mix_post_g"] = gm["mix_pre_g"], gm["mix_post_g"]
    g["mla_q_norm_g"], g["mla_kv_norm_g"] = gm["mla_q_norm_g"], gm["mla_kv_norm_g"]
    g["gdn_norm_g"] = gm["gdn_norm_g"]
    g["mla_out_g"] = _unpad_heads_cols(gm["mla_out_g_pad"], MLA_V)
    g["gdn_a_log"] = gm["a_log_pad"][:, :N_HEADS]
    g["gdn_dt_bias"] = gm["dt_bias_pad"][:, :N_HEADS]
    return lsum, dx0, g


HBM_SPEC = pl.BlockSpec(memory_space=pltpu.HBM)


def _place():
    return lax.axis_index("x"), lax.axis_index("y"), lax.axis_index("c")


def _exchange_call(name, body, ins, out_shapes, n_remote, n_local):
    return pl.pallas_call(
        body, name=name, in_specs=[HBM_SPEC] * len(ins), out_specs=[HBM_SPEC] * len(out_shapes), out_shape=out_shapes,
        scratch_shapes=[pltpu.SemaphoreType.DMA((n_remote,)), pltpu.SemaphoreType.DMA((n_remote,)),
                        pltpu.SemaphoreType.DMA((n_local,))])(*ins)


def _other_chips(x, y):
    return [(1 - x, y), (x, 1 - y), (1 - x, 1 - y)]


def _at_each_chip(fn):
    x, y, _ = _place()
    for cx in range(2):
        for cy in range(2):
            pl.when((x == cx) & (y == cy))(functools.partial(fn, cx, cy))


def _at_each_device(fn):
    x, y, c = _place()
    for cx in range(2):
        for cy in range(2):
            for cc in range(2):
                pl.when((x == cx) & (y == cy) & (c == cc))(functools.partial(fn, cx, cy, cc))


def _at_each_core(fn):
    c = lax.axis_index("c")
    for cc in range(2):
        pl.when(c == cc)(functools.partial(fn, cc))


def _gather_shards(ws):
    nw = len(ws)

    def body(*refs):
        w_refs, out_refs = refs[:nw], refs[nw:2 * nw]
        send_sems, recv_sems, local_sems = refs[2 * nw:]

        def run(x, y, c):
            chips = _other_chips(x, y)
            me, sibling = 2 * x + y, (x, y, 1 - c)

            def half(ref, which):
                hr = ref.shape[0] // 2
                return ref.at[pl.ds(which * hr, hr)]

            def over_ici(i, j, src, slab, to):
                return pltpu.make_async_remote_copy(
                    src_ref=half(src, c), dst_ref=half(out_refs[i].at[slab], c), send_sem=send_sems.at[7 * i + j],
                    recv_sem=recv_sems.at[7 * i + j], device_id=to, device_id_type=MESH)

            def over_d2d(i, j, slab, which):
                return pltpu.make_async_remote_copy(
                    src_ref=half(out_refs[i].at[slab], which), dst_ref=half(out_refs[i].at[slab], which),
                    send_sem=send_sems.at[7 * i + 3 + j], recv_sem=recv_sems.at[7 * i + 3 + j], device_id=sibling,
                    device_id_type=MESH)

            def own(i, w_ref):
                return pltpu.make_async_remote_copy(
                    src_ref=w_ref, dst_ref=out_refs[i].at[me], send_sem=send_sems.at[7 * i + 6],
                    recv_sem=recv_sems.at[7 * i + 6], device_id=sibling, device_id_type=MESH)

            sends, passed = [], []
            for i, w_ref in enumerate(w_refs):
                for j, (px, py) in enumerate(chips):
                    sends.append(over_ici(i, j, w_ref, me, (px, py, c)))
                    sends[-1].start()
            for i, w_ref in enumerate(w_refs):
                sends.append(own(i, w_ref))
                sends[-1].start()
            for i, w_ref in enumerate(w_refs):
                for j, (px, py) in enumerate(chips):
                    over_ici(i, j, w_ref, 2 * px + py, (px, py, c)).wait_recv()
                    passed.append(over_d2d(i, j, 2 * px + py, c))
                    passed[-1].start()
            for i, w_ref in enumerate(w_refs):
                own(i, w_ref).wait_recv()
                for j, (px, py) in enumerate(chips):
                    over_d2d(i, j, 2 * px + py, 1 - c).wait_recv()
            for cp in sends + passed:
                cp.wait_send()

        _at_each_device(run)

    outs = [jax.ShapeDtypeStruct((N_SHARD,) + w.shape, w.dtype) for w in ws]
    return _exchange_call("gather_weight_shards", body, ws, outs, 7 * nw, 1)


def _swap_halves(gs, tag=""):
    ng = len(gs)

    def body(*refs):
        g_refs, got_refs = refs[:ng], refs[ng:2 * ng]
        send_sems, recv_sems, _ = refs[2 * ng:]
        x, y, _ = _place()

        def run(c):
            sends = []
            for i, (g_ref, got_ref) in enumerate(zip(g_refs, got_refs)):
                hr = got_ref.shape[1]
                sends.append(pltpu.make_async_remote_copy(
                    src_ref=g_ref.at[:, pl.ds((1 - c) * hr, hr)], dst_ref=got_ref, send_sem=send_sems.at[i],
                    recv_sem=recv_sems.at[i], device_id=(x, y, 1 - c), device_id_type=MESH))
                sends[-1].start()
            for cp in sends:
                cp.wait()

        _at_each_core(run)

    halves = [jax.ShapeDtypeStruct((g.shape[0], g.shape[1] // 2, g.shape[2]), g.dtype) for g in gs]
    return _exchange_call("swap_grad_halves" + tag, body, gs, halves, ng, 1)


def _scatter_copies(p_refs, out_refs, send_sems, recv_sems, x, y):
    c = lax.axis_index("c")
    copies = []
    for i, (p_ref, out_ref) in enumerate(zip(p_refs, out_refs)):
        for j, (px, py) in enumerate(_other_chips(x, y)):
            copies.append(pltpu.make_async_remote_copy(
                src_ref=p_ref.at[2 * px + py], dst_ref=out_ref.at[j], send_sem=send_sems.at[3 * i + j],
                recv_sem=recv_sems.at[3 * i + j], device_id=(px, py, c), device_id_type=MESH))
    return copies


def _start_all(make, *refs):
    def run(x, y):
        for cp in make(*refs, x, y):
            cp.start()

    _at_each_chip(run)


def _wait_all(make, *refs):
    def run(x, y):
        copies = make(*refs, x, y)
        for cp in copies:
            cp.wait_recv()
        for cp in copies:
            cp.wait_send()

    _at_each_chip(run)


def _scatter_shapes(ps):
    return [jax.ShapeDtypeStruct((3,) + p.shape[1:], p.dtype) for p in ps]


def _carried_scatter(ps):
    return _Carried(ps, _scatter_shapes(ps), 3 * len(ps), functools.partial(_start_all, _scatter_copies),
                    functools.partial(_wait_all, _scatter_copies))


def _direct_gather_copies(w_refs, out_refs, send_sems, recv_sems, x, y, arriving):
    c = lax.axis_index("c")
    me = 2 * x + y
    peers = [((px, py, c), 2 * px + py) for px, py in _other_chips(x, y)] + [((x, y, 1 - c), me)]
    copies = []
    for i, (w_ref, out_ref) in enumerate(zip(w_refs, out_refs)):
        for j, (peer, slab) in enumerate(peers):
            copies.append(pltpu.make_async_remote_copy(
                src_ref=w_ref, dst_ref=out_ref.at[slab if arriving else me], send_sem=send_sems.at[4 * i + j],
                recv_sem=recv_sems.at[4 * i + j], device_id=peer, device_id_type=MESH))
    return copies


def _carried_gather(ws):
    def start(w_refs, out_refs, send_sems, recv_sems):
        def run(x, y):
            for cp in _direct_gather_copies(w_refs, out_refs, send_sems, recv_sems, x, y, False):
                cp.start()

        _at_each_chip(run)

    def finish(w_refs, out_refs, send_sems, recv_sems):
        def run(x, y):
            for cp in _direct_gather_copies(w_refs, out_refs, send_sems, recv_sems, x, y, True):
                cp.wait_recv()
            for cp in _direct_gather_copies(w_refs, out_refs, send_sems, recv_sems, x, y, False):
                cp.wait_send()

        _at_each_chip(run)

    outs = [jax.ShapeDtypeStruct((N_SHARD,) + w.shape, w.dtype) for w in ws]
    return _Carried(ws, outs, 4 * len(ws), start, finish)


def _share_halves(hs):
    n = len(hs)

    def body(*refs):
        h_refs, out_refs = refs[:n], refs[n:2 * n]
        send_sems, recv_sems, _ = refs[2 * n:]
        x, y, c = _place()
        sends = []
        for i, (h_ref, out_ref) in enumerate(zip(h_refs, out_refs)):
            sends.append(pltpu.make_async_remote_copy(
                src_ref=h_ref, dst_ref=out_ref, send_sem=send_sems.at[i], recv_sem=recv_sems.at[i],
                device_id=(x, y, 1 - c), device_id_type=MESH))
            sends[-1].start()
        for cp in sends:
            cp.wait()

    outs = [jax.ShapeDtypeStruct(h.shape, h.dtype) for h in hs]
    return _exchange_call("share_grad_halves", body, hs, outs, n, 1)


def _scalar_grid_call(name, body, scalars, grid, in_specs, out_specs, out_shape, args):
    grid_spec = pltpu.PrefetchScalarGridSpec(num_scalar_prefetch=len(scalars), grid=grid, in_specs=in_specs,
                                             out_specs=out_specs)
    return pl.pallas_call(body, name=name, grid_spec=grid_spec, out_shape=out_shape,
                          compiler_params=_params(("arbitrary",) * len(grid)))(*scalars, *args)


def _add_pair(name, g, got, core):
    ns_, hr, cols = got.shape
    th = _row_tile(hr, 512)
    nb = hr // th

    def body(core_ref, g_ref, got_ref, out_ref):
        out_ref[...] = (g_ref[...].astype(F32) + got_ref[...].astype(F32)).astype(out_ref.dtype)

    blk = pl.BlockSpec((1, th, cols), lambda q, j, core_ref: (q, j, 0))
    own = pl.BlockSpec((1, th, cols), lambda q, j, core_ref: (q, core_ref[0] * nb + j, 0))
    return _scalar_grid_call(name, body, [core], (ns_, nb), [own, blk], blk,
                             jax.ShapeDtypeStruct(got.shape, got.dtype), [g, got])


def _add_chips(name, pairs, slabs, chip):
    _, hr, cols = slabs.shape
    th = _row_tile(hr, 512)

    def body(chip_ref, own_ref, s0_ref, s1_ref, s2_ref, out_ref):
        total = own_ref[0].astype(F32) + s0_ref[0].astype(F32)
        out_ref[...] = (total + s1_ref[0].astype(F32)) + s2_ref[0].astype(F32)

    own = pl.BlockSpec((1, th, cols), lambda j, chip_ref: (chip_ref[0], j, 0))
    others = [pl.BlockSpec((1, th, cols), lambda j, chip_ref, k=k: (k, j, 0)) for k in range(3)]
    return _scalar_grid_call(name, body, [chip], (hr // th,), [own] + others,
                             pl.BlockSpec((th, cols), lambda j, chip_ref: (j, 0)),
                             jax.ShapeDtypeStruct((hr, cols), F32), [pairs, slabs, slabs, slabs])


def _join_halves(name, mine, other, core):
    hr, cols = mine.shape
    th = _row_tile(hr, 512)
    nb = hr // th

    def body(core_ref, mine_ref, other_ref, out_ref):
        is_mine = pl.program_id(0) == core_ref[0]

        @pl.when(is_mine)
        def _():
            out_ref[0] = mine_ref[...]

        @pl.when(jnp.logical_not(is_mine))
        def _():
            out_ref[0] = other_ref[...]

    blk = pl.BlockSpec((th, cols), lambda h, j, core_ref: (j, 0))
    return _scalar_grid_call(name, body, [core], (2, nb), [blk, blk],
                             pl.BlockSpec((1, th, cols), lambda h, j, core_ref: (0, h * nb + j, 0)),
                             jax.ShapeDtypeStruct((1, 2 * hr, cols), mine.dtype), [mine, other])


def _gather_small(sp):
    def body(s_ref, out_ref, send_sems, recv_sems, local_sem):
        x, y, c = _place()
        me = 4 * x + 2 * y + c
        peers = [(x ^ (m >> 2), y ^ ((m >> 1) & 1), c ^ (m & 1)) for m in range(1, 8)]
        mine = pltpu.make_async_copy(s_ref, out_ref.at[me], local_sem)
        mine.start()
        sends = [pltpu.make_async_remote_copy(src_ref=s_ref, dst_ref=out_ref.at[me], send_sem=send_sems.at[j],
                                              recv_sem=recv_sems.at[j], device_id=p, device_id_type=MESH)
                 for j, p in enumerate(peers)]
        for cp in sends:
            cp.start()
        for j, (px, py, pc) in enumerate(peers):
            pltpu.make_async_remote_copy(src_ref=s_ref, dst_ref=out_ref.at[4 * px + 2 * py + pc],
                                         send_sem=send_sems.at[j], recv_sem=recv_sems.at[j], device_id=(px, py, pc),
                                         device_id_type=MESH).wait_recv()
        for cp in sends:
            cp.wait_send()
        mine.wait()

    return pl.pallas_call(
        body, name="gather_small_grads", in_specs=[HBM_SPEC], out_specs=HBM_SPEC,
        out_shape=jax.ShapeDtypeStruct((8,) + sp.shape, sp.dtype),
        scratch_shapes=[pltpu.SemaphoreType.DMA((7,)), pltpu.SemaphoreType.DMA((7,)), pltpu.SemaphoreType.DMA])(sp)


def _pack_rows(total):
    rows = -(-total // LANES)
    return -(-rows // 32) * 32


def _pack(arrs, dtype):
    flat = jnp.concatenate([a.reshape(-1).astype(dtype) for a in arrs])
    rows = _pack_rows(flat.shape[0])
    return jnp.pad(flat, (0, rows * LANES - flat.shape[0])).reshape(rows, LANES)


def _unpack(buf, shapes):
    flat = buf.reshape(-1)
    out, off = {}, 0
    for n, shp in shapes:
        size = shp[0] * shp[1]
        out[n] = flat[off:off + size].reshape(shp)
        off += size
    return out


def _to_wire(name, w3):
    _, r, cols = w3.shape
    tb = _row_tile(r, 512)

    def body(w_ref, o_ref):
        o_ref[...] = w_ref[0].astype(o_ref.dtype)

    return pl.pallas_call(
        body, name=name, grid=(r // tb,), in_specs=[pl.BlockSpec((1, tb, cols), lambda i: (0, i, 0))],
        out_specs=pl.BlockSpec((tb, cols), lambda i: (i, 0)), out_shape=jax.ShapeDtypeStruct((r, cols), MM_DTYPE),
        compiler_params=_params(("arbitrary",)))(w3)


def _adamw(name, w3, g, m3, v3, tb):
    c1 = 1.0 - ADAM_B1 ** ADAM_STEP
    c2 = 1.0 - ADAM_B2 ** ADAM_STEP
    _, r, cols = w3.shape
    emit = g.ndim == 2
    blk3 = pl.BlockSpec((1, tb, cols), lambda i: (0, i, 0))
    g_spec = pl.BlockSpec((tb, cols), lambda i: (i, 0)) if emit else blk3

    def body(w_ref, g_ref, m_ref, v_ref, *out_refs):
        gb = g_ref[...] if emit else g_ref[0]
        m2 = ADAM_B1 * m_ref[0] + (1.0 - ADAM_B1) * gb
        v2 = ADAM_B2 * v_ref[0] + (1.0 - ADAM_B2) * (gb * gb)
        out_refs[-3][0] = -ADAM_LR * ((m2 / c1) / (jnp.sqrt(v2 / c2) + ADAM_EPS) + ADAM_WD * w_ref[0])
        out_refs[-2][0] = m2
        out_refs[-1][0] = v2
        if emit:
            out_refs[0][0] = gb

    n_out = 4 if emit else 3
    outs = pl.pallas_call(
        body, name=name, grid=(r // tb,), in_specs=[blk3, g_spec, blk3, blk3], out_specs=[blk3] * n_out,
        out_shape=[jax.ShapeDtypeStruct((1, r, cols), F32)] * n_out,
        compiler_params=_params(("arbitrary",)))(w3, g, m3, v3)
    return outs if emit else [g] + list(outs)


def _row_tile(rows, pref):
    if rows <= pref:
        return rows
    t = pref
    while t >= 8:
        if rows % t == 0 and t % 8 == 0:
            return t
        t -= 8
    return rows


def kernel(x, positions, ffn1_pre_g, ffn1_w_gate, ffn1_w_up, ffn1_w_down, ffn1_post_g, mix_pre_g, w_in, mla_q_norm_g, mla_w_uq, mla_kv_norm_g, mla_w_ukv, mla_out_g, gdn_conv_w, gdn_a_log, gdn_dt_bias, gdn_norm_g, w_out, mix_post_g, ffn2_pre_g, ffn2_w_gate, ffn2_w_up, ffn2_w_down, ffn2_post_g, loss_target, m_ffn1_pre_g, m_ffn1_w_gate, m_ffn1_w_up, m_ffn1_w_down, m_ffn1_post_g, m_mix_pre_g, m_w_in, m_mla_q_norm_g, m_mla_w_uq, m_mla_kv_norm_g, m_mla_w_ukv, m_mla_out_g, m_gdn_conv_w, m_gdn_a_log, m_gdn_dt_bias, m_gdn_norm_g, m_w_out, m_mix_post_g, m_ffn2_pre_g, m_ffn2_w_gate, m_ffn2_w_up, m_ffn2_w_down, m_ffn2_post_g, v_ffn1_pre_g, v_ffn1_w_gate, v_ffn1_w_up, v_ffn1_w_down, v_ffn1_post_g, v_mix_pre_g, v_w_in, v_mla_q_norm_g, v_mla_w_uq, v_mla_kv_norm_g, v_mla_w_ukv, v_mla_out_g, v_gdn_conv_w, v_gdn_a_log, v_gdn_dt_bias, v_gdn_norm_g, v_w_out, v_mix_post_g, v_ffn2_pre_g, v_ffn2_w_gate, v_ffn2_w_up, v_ffn2_w_down, v_ffn2_post_g):
    args = dict(locals())
    wsh = {n: args[n][0] for n in WEIGHTS}
    msh = {n: args["m_" + n] for n in SMALL}
    vsh = {n: args["v_" + n] for n in SMALL}
    for n in SMALL:
        wsh[n] = args[n]
    mix_shapes = [(n, wsh[n].shape) for n in MIX_BIG]

    early = FFN_BIG[:2]
    held = lambda a, n: jnp.swapaxes(a, 1, 2) if n in TRANSPOSED else a
    w_in_wire = jnp.pad(held(w_in, "w_in")[0].astype(MM_DTYPE), ((0, W_IN_SHARD_PAD - W_IN_SHARD), (0, 0)))
    gathered = _gather_shards([_to_wire("wire_" + n, held(args[n], n)) for n in early]
                              + [w_in_wire, _pack([wsh[n] for n in MIX_BIG], MM_DTYPE)])
    full = {n: wsh[n] for n in SMALL}
    for n, gw in zip(early + ["w_in"], gathered):
        full[n] = gw
    parts = [_unpack(gathered[-1][q], mix_shapes) for q in range(N_SHARD)]
    for n in MIX_BIG:
        full[n] = jnp.concatenate([parts[q][n] for q in range(N_SHARD)], axis=SHARD_AXIS[n])

    core = lax.axis_index("c").astype(jnp.int32).reshape(1)
    chip = (2 * lax.axis_index("x") + lax.axis_index("y")).astype(jnp.int32).reshape(1)
    late = ([_to_wire("wire_" + n, held(args[n], n)) for n in FFN2_BIG], core, chip,
            _to_wire("wire_ffn1_w_down", ffn1_w_down))
    lsum, grad_x, g = _local_step(x[0], positions, loss_target[0], full, late)
    loss = lax.psum(0.5 * jnp.sum(lsum) / x.shape[-1], ("x", "y", "c"))

    halves = [g[n] for n in FFN_BIG] + [g["w_in"], g["mix_pack"]]
    others = _share_halves(halves)
    shared = [_join_halves("join_halves_%d" % i, hm, ho, core) for i, (hm, ho) in enumerate(zip(halves, others))]
    gsh = _unpack(shared[-1], mix_shapes)
    for n, sg_ in zip(FFN_BIG, shared):
        gsh[n] = sg_
    gsh["w_in"] = shared[-2][:, :W_IN_SHARD]

    small_shapes = [(n, wsh[n].shape) for n in SMALL]
    pack_small = lambda d: jnp.concatenate(
        [_pad_lanes(d[n].astype(F32), LANES) for n in SMALL] + [jnp.zeros((SMALL_ROWS - len(SMALL), LANES), F32)], axis=0)
    slots = _gather_small(pack_small(g))

    c1 = 1.0 - ADAM_B1 ** ADAM_STEP
    c2 = 1.0 - ADAM_B2 ** ADAM_STEP

    def small_update(wb, mb, vb, s8):
        gs = s8[0:SMALL_ROWS]
        for d in range(1, 8):
            gs = gs + s8[d * SMALL_ROWS:(d + 1) * SMALL_ROWS]
        m2 = ADAM_B1 * mb + (1.0 - ADAM_B1) * gs
        v2 = ADAM_B2 * vb + (1.0 - ADAM_B2) * (gs * gs)
        delta = -ADAM_LR * ((m2 / c1) / (jnp.sqrt(v2 / c2) + ADAM_EPS) + ADAM_WD * wb)
        return gs, delta, m2, v2

    sg, sd, sm, sv_ = _rowwise("adamw_small", small_update,
                               [pack_small(wsh), pack_small(msh), pack_small(vsh)],
                               [slots.reshape(8 * SMALL_ROWS, LANES)], [(LANES, F32)] * 4, [], SMALL_ROWS)
    grads, deltas, new_m, new_v = {}, {}, {}, {}
    for i, (n, shp) in enumerate(small_shapes):
        grads[n], deltas[n] = sg[i:i + 1, :shp[1]], sd[i:i + 1, :shp[1]]
        new_m[n], new_v[n] = sm[i:i + 1, :shp[1]], sv_[i:i + 1, :shp[1]]
    for n in BIG:
        w3 = held(args[n], n)
        outs = _adamw("adamw_" + n, w3, gsh[n], held(args["m_" + n], n), held(args["v_" + n], n),
                      _row_tile(w3.shape[1], 256))
        grads[n], deltas[n], new_m[n], new_v[n] = [held(o, n) for o in outs]

    return (loss, grad_x[None], *[grads[n] for n in WEIGHTS], *[deltas[n] for n in WEIGHTS],
            *[new_m[n] for n in WEIGHTS], *[new_v[n] for n in WEIGHTS])
```

```python
import functools

import jax
import jax.numpy as jnp
from jax import lax
from jax.experimental import pallas as pl
from jax.experimental.pallas import tpu as pltpu

F32 = jnp.float32
BF16 = jnp.bfloat16
MM_DTYPE = BF16
MESH = pl.DeviceIdType.MESH

D_MODEL = 1024
D_FF = 2816
N_HEADS = 8
MLA_Q_RANK = 256
MLA_KV_RANK = 128
MLA_NOPE = 64
MLA_ROPE = 32
MLA_V = 64
ROPE_THETA = 10000.0
GDN_DH = 64
GDN_W = N_HEADS * GDN_DH
GDN_CONV = 4
CHUNK = 64
GDN_STEP_CHUNKS = 4
HEAD_LANES = 128
HEADS_PER_STEP = 4
MLA_PAD = N_HEADS * HEAD_LANES
EPS = 1e-6
N_SHARD = 4
LANES = 1024

PIN_QKV = 0
PIN_MLA = 1536
PIN_KPE = 1920
PIN_GATE = 2048
PIN_AB = 2560
PIN_W = 2688
CAT_W = MLA_PAD + GDN_W

ADAM_LR = 0.001
ADAM_B1 = 0.9
ADAM_B2 = 0.999
ADAM_EPS = 1e-08
ADAM_WD = 0.01
ADAM_STEP = 10

VMEM_LIMIT_V7X = 56 * 1024 * 1024

BIG = ["ffn1_w_gate", "ffn1_w_up", "ffn1_w_down", "w_in", "mla_w_uq", "mla_w_ukv", "gdn_conv_w", "w_out",
       "ffn2_w_gate", "ffn2_w_up", "ffn2_w_down"]
FFN_BIG = ["ffn1_w_gate", "ffn1_w_up", "ffn1_w_down", "ffn2_w_gate", "ffn2_w_up", "ffn2_w_down"]
TRANSPOSED = ["ffn1_w_gate", "ffn1_w_up", "ffn2_w_gate", "ffn2_w_up", "w_in"]
MIX_BIG = ["mla_w_uq", "mla_w_ukv", "gdn_conv_w", "w_out"]
SMALL = ["ffn1_pre_g", "ffn1_post_g", "mix_pre_g", "mla_q_norm_g", "mla_kv_norm_g", "mla_out_g", "gdn_a_log",
         "gdn_dt_bias", "gdn_norm_g", "mix_post_g", "ffn2_pre_g", "ffn2_post_g"]
WEIGHTS = ["ffn1_pre_g", "ffn1_w_gate", "ffn1_w_up", "ffn1_w_down", "ffn1_post_g", "mix_pre_g", "w_in",
           "mla_q_norm_g", "mla_w_uq", "mla_kv_norm_g", "mla_w_ukv", "mla_out_g", "gdn_conv_w", "gdn_a_log",
           "gdn_dt_bias", "gdn_norm_g", "w_out", "mix_post_g", "ffn2_pre_g", "ffn2_w_gate", "ffn2_w_up",
           "ffn2_w_down", "ffn2_post_g"]
SHARD_AXIS = {"ffn1_w_gate": 1, "ffn1_w_up": 1, "ffn1_w_down": 0, "w_in": 1, "mla_w_uq": 1, "mla_w_ukv": 1,
              "gdn_conv_w": 1, "w_out": 0, "ffn2_w_gate": 1, "ffn2_w_up": 1, "ffn2_w_down": 0}
SMALL_ROWS = 16


def _params(sem):
    return pltpu.CompilerParams(dimension_semantics=sem, vmem_limit_bytes=VMEM_LIMIT_V7X)


def _pick(dim, pref):
    if dim <= pref:
        return dim
    t = (pref // 128) * 128
    while t >= 128:
        if dim % t == 0:
            return t
        t -= 128
    return dim


ANY_SPEC = pl.BlockSpec(memory_space=pl.ANY)


def _rowwise(name, fn, row_ins, bc_ins, row_outs, acc_outs, tb, wide=None, carry=None):
    ents = []
    for e in row_ins:
        ents.append(e if isinstance(e, tuple) else (e, e.shape[1], 0, 0))
    over = [o[2] for o in row_outs if len(o) == 3]
    rows = over[0] if over else ents[0][0].shape[0]
    steps = rows // tb
    assert steps * tb == rows, (name, rows, tb)
    in_specs, args = [], []
    for a, w, j, r0 in ents:
        in_specs.append(pl.BlockSpec((tb, w), lambda i, j=j, r0=r0: (i + r0, j)))
        args.append(a)
    for b in bc_ins:
        in_specs.append(pl.BlockSpec(b.shape, lambda i: (0, 0)))
        args.append(b)
    n_in = len(args)
    aliases = {}
    if carry is not None:
        in_specs.append(ANY_SPEC)
        args.append(carry)
        aliases = {n_in: 0}
    out_shape = [jax.ShapeDtypeStruct((rows, o[0]), o[1]) for o in row_outs]
    out_specs = [pl.BlockSpec((tb, o[0]), lambda i: (i, 0)) for o in row_outs]
    if wide is not None:
        out_shape[0] = jax.ShapeDtypeStruct((rows, wide[0]), row_outs[0][1])
        out_specs[0] = pl.BlockSpec((tb, row_outs[0][0]), lambda i: (i, wide[1]))
    out_shape += [jax.ShapeDtypeStruct((r, c), F32) for r, c in acc_outs]
    out_specs += [pl.BlockSpec((r, c), lambda i: (0, 0)) for r, c in acc_outs]
    n_ro, n_acc, n_args = len(row_outs), len(acc_outs), len(args)

    def body(*refs):
        vals = fn(*[r[...] for r in refs[:n_in]])
        if not isinstance(vals, (tuple, list)):
            vals = (vals,)
        for r, v in zip(refs[n_args:n_args + n_ro], vals[:n_ro]):
            r[...] = v.astype(r.dtype)
        if n_acc:
            acc_refs = refs[n_args + n_ro:]

            @pl.when(pl.program_id(0) == 0)
            def _():
                for r in acc_refs:
                    r[...] = jnp.zeros(r.shape, r.dtype)

            for r, v in zip(acc_refs, vals[n_ro:]):
                r[...] += v

    outs = pl.pallas_call(body, name=name, grid=(steps,), in_specs=in_specs, out_specs=out_specs,
                          out_shape=out_shape, input_output_aliases=aliases,
                          compiler_params=_params(("arbitrary",)))(*args)
    return outs


def _mm(name, a, b, mode, out_dtype, tm=1024, tn=1024, tk=1024):
    if mode == "nn":
        (m, k), (k2, n) = a.shape, b.shape
    elif mode == "nt":
        (m, k), (n, k2) = a.shape, b.shape
    else:
        (k, m), (k2, n) = a.shape, b.shape
        tk = 2 * tk
    assert k == k2, (name, a.shape, b.shape)
    tm, tn, tk = _pick(m, tm), _pick(n, tn), _pick(k, tk)
    nk = k // tk
    if mode == "nn":
        a_spec = pl.BlockSpec((tm, tk), lambda i, j, kk: (i, kk))
        b_spec = pl.BlockSpec((tk, tn), lambda i, j, kk: (kk, j))
        dims = (((1,), (0,)), ((), ()))
    elif mode == "nt":
        a_spec = pl.BlockSpec((tm, tk), lambda i, j, kk: (i, kk))
        b_spec = pl.BlockSpec((tn, tk), lambda i, j, kk: (j, kk))
        dims = (((1,), (1,)), ((), ()))
    else:
        a_spec = pl.BlockSpec((tk, tm), lambda i, j, kk: (kk, i))
        b_spec = pl.BlockSpec((tk, tn), lambda i, j, kk: (kk, j))
        dims = (((0,), (0,)), ((), ()))

    def body(a_ref, b_ref, o_ref, acc_ref):
        kk = pl.program_id(2)

        @pl.when(kk == 0)
        def _():
            acc_ref[...] = jnp.zeros(acc_ref.shape, F32)

        acc_ref[...] += lax.dot_general(a_ref[...].astype(MM_DTYPE), b_ref[...].astype(MM_DTYPE), dims,
                                        preferred_element_type=F32)

        @pl.when(kk == nk - 1)
        def _():
            o_ref[...] = acc_ref[...].astype(o_ref.dtype)

    return pl.pallas_call(
        body, name=name, grid=(m // tm, n // tn, nk), in_specs=[a_spec, b_spec],
        out_specs=pl.BlockSpec((tm, tn), lambda i, j, kk: (i, j)),
        out_shape=jax.ShapeDtypeStruct((m, n), out_dtype),
        scratch_shapes=[pltpu.VMEM((tm, tn), F32)],
        compiler_params=_params(("parallel", "parallel", "arbitrary")))(a, b)


def _prologue_mm(name, row_ins, g, prologue, w, with_gain_grad, tm=1024, tn=1024):
    t, d = row_ins[0].shape
    n = w.shape[0]
    tm, tn = _pick(t, tm), _pick(n, tn)
    n_row = len(row_ins)
    row = pl.BlockSpec((tm, d), lambda i, j: (i, 0))
    vec = pl.BlockSpec((1, d), lambda i, j: (0, 0))

    def body(*refs):
        rows, g_ref, w_ref = refs[:n_row], refs[n_row], refs[n_row + 1]
        lhs_ref, out_ref = refs[n_row + 2], refs[n_row + 3]
        lhs_s = refs[-1]
        i, j = pl.program_id(0), pl.program_id(1)
        if with_gain_grad:
            dg_ref = refs[n_row + 4]

            @pl.when((i == 0) & (j == 0))
            def _():
                dg_ref[...] = jnp.zeros(dg_ref.shape, F32)

        @pl.when(j == 0)
        def _():
            res = prologue(*[r[...] for r in rows], g_ref[...])
            lhs_s[...] = res[0].astype(MM_DTYPE)
            lhs_ref[...] = lhs_s[...]
            if with_gain_grad:
                dg_ref[...] += res[1]

        out_ref[...] = lax.dot_general(lhs_s[...], w_ref[...], (((1,), (1,)), ((), ())), preferred_element_type=F32)

    out_specs = [row, pl.BlockSpec((tm, tn), lambda i, j: (i, j))] + ([vec] if with_gain_grad else [])
    out_shape = [jax.ShapeDtypeStruct((t, d), MM_DTYPE), jax.ShapeDtypeStruct((t, n), F32)]
    out_shape += [jax.ShapeDtypeStruct((1, d), F32)] if with_gain_grad else []
    return pl.pallas_call(
        body, name=name, grid=(t // tm, n // tn),
        in_specs=[row] * n_row + [vec, pl.BlockSpec((tn, d), lambda i, j: (j, 0))], out_specs=out_specs,
        out_shape=out_shape, scratch_shapes=[pltpu.VMEM((tm, d), MM_DTYPE)],
        compiler_params=_params(("arbitrary", "arbitrary")))(*row_ins, g, w)


def _mm_epilogue(name, a, w, row_ins, g, epilogue, n_row_out, with_gain_grad, tm=1024, tk=1024):
    t, k = a.shape
    d = w.shape[1]
    tm, tk = _pick(t, tm), _pick(k, tk)
    nk = k // tk
    n_row = len(row_ins)
    row = pl.BlockSpec((tm, d), lambda i, kk: (i, 0))
    vec = pl.BlockSpec((1, d), lambda i, kk: (0, 0))

    def body(*refs):
        a_ref, w_ref = refs[0], refs[1]
        rows, g_ref = refs[2:2 + n_row], refs[2 + n_row]
        outs = refs[3 + n_row:3 + n_row + n_row_out]
        acc = refs[-1]
        i, kk = pl.program_id(0), pl.program_id(1)
        if with_gain_grad:
            dg_ref = refs[3 + n_row + n_row_out]

            @pl.when((i == 0) & (kk == 0))
            def _():
                dg_ref[...] = jnp.zeros(dg_ref.shape, F32)

        @pl.when(kk == 0)
        def _():
            acc[...] = jnp.zeros(acc.shape, F32)

        acc[...] += jnp.dot(a_ref[...], w_ref[...], preferred_element_type=F32)

        @pl.when(kk == nk - 1)
        def _():
            res = epilogue(acc[...], *[r[...] for r in rows], g_ref[...])
            for o_ref, val in zip(outs, res[:n_row_out]):
                o_ref[...] = val
            if with_gain_grad:
                dg_ref[...] += res[n_row_out]

    out_specs = [row] * n_row_out + ([vec] if with_gain_grad else [])
    out_shape = [jax.ShapeDtypeStruct((t, d), F32)] * n_row_out
    out_shape += [jax.ShapeDtypeStruct((1, d), F32)] if with_gain_grad else []
    return pl.pallas_call(
        body, name=name, grid=(t // tm, nk),
        in_specs=[pl.BlockSpec((tm, tk), lambda i, kk: (i, kk)), pl.BlockSpec((tk, d), lambda i, kk: (kk, 0))]
        + [row] * n_row + [vec],
        out_specs=out_specs, out_shape=out_shape, scratch_shapes=[pltpu.VMEM((tm, d), F32)],
        compiler_params=_params(("arbitrary", "arbitrary")))(a, w, *row_ins, g)


def _rms_stats(x, n_real=None):
    n = x.shape[-1] if n_real is None else n_real
    return lax.rsqrt(jnp.sum(x * x, axis=-1, keepdims=True) / n + EPS)


def _rms_bwd(x, r, g, dz, n_real=None):
    n = x.shape[-1] if n_real is None else n_real
    xh = x * r
    dxh = dz * g
    dx = r * (dxh - xh * (jnp.sum(dxh * xh, axis=-1, keepdims=True) / n))
    return dx, jnp.sum(dz * xh, axis=0, keepdims=True)


def _sigmoid(x):
    return 0.5 * jnp.tanh(0.5 * x) + 0.5


def _roll(x, s, axis):
    return pltpu.roll(x, s, axis)


def _rope(x, c, s1, s2):
    return x * c + _roll(x, HEAD_LANES - MLA_ROPE // 2, 1) * s1 + _roll(x, MLA_ROPE // 2, 1) * s2


def _heads_apply(x, fn):
    return jnp.concatenate([fn(x[:, h * HEAD_LANES:(h + 1) * HEAD_LANES]) for h in range(N_HEADS)], axis=1)


ROW_CHUNK = 256


def _row_chunks(rows):
    step = min(ROW_CHUNK, rows)
    return [pl.ds(r, step) for r in range(0, rows, step)]


def _ffn_fwd(tag, x, g_pre, wg, wu, wd, g_post, tm, loss_target=None, carried_up=None):
    t, d = x.shape
    ns, fs, _ = wg.shape
    nt = t // tm
    row = pl.BlockSpec((tm, d), lambda i, q: (i, 0))
    vec = pl.BlockSpec((1, d), lambda i, q: (0, 0))
    act3 = pl.BlockSpec((1, tm, fs), lambda i, q: (q, i, 0))
    wrow = pl.BlockSpec((1, fs, d), lambda i, q: (q, 0, 0))
    nt_dims = (((1,), (1,)), ((), ()))

    def gate_up(x_ref, g_ref, wg_ref, wu_ref, n_ref, sl_ref, ud_ref, s_ref, n_s):
        @pl.when(pl.program_id(1) == 0)
        def _():
            for r in _row_chunks(tm):
                xb = x_ref[r, :]
                n_s[r, :] = (xb * _rms_stats(xb) * g_ref[...]).astype(MM_DTYPE)
            n_ref[...] = n_s[...]

        for r in _row_chunks(tm):
            n = n_s[r, :]
            a = lax.dot_general(n, wg_ref[0], nt_dims, preferred_element_type=F32)
            u = lax.dot_general(n, wu_ref[0], nt_dims, preferred_element_type=F32)
            sg = _sigmoid(a)
            sl = a * sg
            sl_ref[0, r, :] = sl.astype(sl_ref.dtype)
            ud_ref[0, r, :] = (u * (sg + sl * (1.0 - sg))).astype(ud_ref.dtype)
            s_ref[0, r, :] = (sl * u).astype(s_ref.dtype)

    gate_up, x_in, x_out, x_shape, x_scr, x_args = _carry(gate_up, 4, 4, (nt, ns), carried_up)
    n, sl, ud, s, *from_up = pl.pallas_call(
        gate_up, name=tag + "_gate_up", grid=(nt, ns), in_specs=[row, vec, wrow, wrow] + x_in,
        out_specs=[row, act3, act3, act3] + x_out,
        out_shape=[jax.ShapeDtypeStruct((t, d), MM_DTYPE)] + [jax.ShapeDtypeStruct((ns, t, fs), MM_DTYPE)] * 3 + x_shape,
        scratch_shapes=[pltpu.VMEM((tm, d), MM_DTYPE)] + x_scr,
        compiler_params=_params(("arbitrary", "arbitrary")))(x, g_pre, wg, wu, *x_args)
    if wd is None:
        wd = from_up[0]

    def down(s_ref, wd_ref, x_ref, g_ref, *rest):
        if loss_target is not None:
            tgt_ref, h_ref, y_ref, ls_ref, acc = rest
        else:
            h_ref, y_ref, acc = rest
        i, q = pl.program_id(0), pl.program_id(1)

        @pl.when(q == 0)
        def _():
            acc[...] = jnp.zeros(acc.shape, F32)

        if loss_target is not None:
            @pl.when((i == 0) & (q == 0))
            def _():
                ls_ref[...] = jnp.zeros(ls_ref.shape, F32)

        for r in _row_chunks(tm):
            acc[r, :] += jnp.dot(s_ref[0, r, :], wd_ref[0], preferred_element_type=F32)

        @pl.when(q == ns - 1)
        def _():
            for r in _row_chunks(tm):
                hb = acc[r, :]
                h_ref[r, :] = hb
                yb = x_ref[r, :] + 0.5 * (hb * _rms_stats(hb) * g_ref[...])
                if loss_target is None:
                    y_ref[r, :] = yb
                else:
                    e = yb - tgt_ref[r, :]
                    y_ref[r, :] = e * (1.0 / d)
                    ls_ref[...] += jnp.sum(e * e, axis=0, keepdims=True)

    with_loss = loss_target is not None
    outs = pl.pallas_call(
        down, name=tag + "_down", grid=(nt, ns), in_specs=[act3, wrow, row, vec] + ([row] if with_loss else []),
        out_specs=[row, row] + ([vec] if with_loss else []),
        out_shape=[jax.ShapeDtypeStruct((t, d), F32)] * 2 + ([jax.ShapeDtypeStruct((1, d), F32)] if with_loss else []),
        scratch_shapes=[pltpu.VMEM((tm, d), F32)],
        compiler_params=_params(("arbitrary", "arbitrary")))(s, wd, x, g_post, *([loss_target] if with_loss else []))
    if with_loss:
        h, dy, lsum = outs
        return (dy, lsum), (x, n, sl, ud, s, h), wd
    h, y = outs
    return y, (x, n, sl, ud, s, h), wd


def _carry(body, n_in, n_out, grid, carried):
    if carried is None:
        return body, [], [], [], [], []
    nx_in, nx_out = len(carried.ins), len(carried.outs)

    def wrapped(*refs):
        ins, rest = refs[:n_in], refs[n_in:]
        xi, rest = rest[:nx_in], rest[nx_in:]
        outs, rest = rest[:n_out], rest[n_out:]
        xo, rest = rest[:nx_out], rest[nx_out:]
        scr, sems = rest[:len(rest) - 2], rest[len(rest) - 2:]
        first, last = True, True
        for dim, size in enumerate(grid):
            first = first & (pl.program_id(dim) == 0)
            last = last & (pl.program_id(dim) == size - 1)

        @pl.when(first)
        def _():
            carried.start(xi, xo, *sems)

        body(*ins, *outs, *scr)

        @pl.when(last)
        def _():
            carried.finish(xi, xo, *sems)

    sems = [pltpu.SemaphoreType.DMA((carried.n_sem,)), pltpu.SemaphoreType.DMA((carried.n_sem,))]
    return (wrapped, [HBM_SPEC] * nx_in, [HBM_SPEC] * nx_out, list(carried.outs), sems, list(carried.ins))


def _ffn_bwd(tag, dy, saved, g_pre, wg, wu, wd, g_post, tm, tk, carried_down=None, make_carried_mid=None,
             make_carried_up=None):
    x, n, sl, ud, s, h = saved
    t, d = x.shape
    ns, fs, _ = wg.shape
    nt, nk = t // tm, t // tk
    row = pl.BlockSpec((tm, d), lambda i, q: (i, 0))
    vec = pl.BlockSpec((1, d), lambda i, q: (0, 0))
    act3 = pl.BlockSpec((1, tm, fs), lambda i, q: (q, i, 0))
    wrow = pl.BlockSpec((1, fs, d), lambda i, q: (q, 0, 0))
    nt_dims = (((1,), (1,)), ((), ()))
    tn_dims = (((0,), (0,)), ((), ()))

    def down_b(h_ref, dy_ref, g_ref, wd_ref, sl_ref, ud_ref, dh_ref, da_ref, du_ref, dg_ref, dh_s):
        i, q = pl.program_id(0), pl.program_id(1)

        @pl.when((i == 0) & (q == 0))
        def _():
            dg_ref[...] = jnp.zeros(dg_ref.shape, F32)

        @pl.when(q == 0)
        def _():
            for r in _row_chunks(tm):
                hb = h_ref[r, :]
                dh, dg = _rms_bwd(hb, _rms_stats(hb), g_ref[...], 0.5 * dy_ref[r, :])
                dh_s[r, :] = dh.astype(MM_DTYPE)
                dg_ref[...] += dg
            dh_ref[...] = dh_s[...]

        for r in _row_chunks(tm):
            ds = lax.dot_general(dh_s[r, :], wd_ref[0], nt_dims, preferred_element_type=F32)
            da_ref[0, r, :] = (ds * ud_ref[0, r, :].astype(F32)).astype(da_ref.dtype)
            du_ref[0, r, :] = (ds * sl_ref[0, r, :].astype(F32)).astype(du_ref.dtype)

    down_b, x_in, x_out, x_shape, x_scr, x_args = _carry(down_b, 6, 4, (nt, ns), carried_down)
    dh, da, du, dg_post, *from_down = pl.pallas_call(
        down_b, name=tag + "_down_b", grid=(nt, ns), in_specs=[row, row, vec, wrow, act3, act3] + x_in,
        out_specs=[row, act3, act3, vec] + x_out,
        out_shape=[jax.ShapeDtypeStruct((t, d), MM_DTYPE)] + [jax.ShapeDtypeStruct((ns, t, fs), MM_DTYPE)] * 2
        + [jax.ShapeDtypeStruct((1, d), F32)] + x_shape,
        scratch_shapes=[pltpu.VMEM((tm, d), MM_DTYPE)] + x_scr,
        compiler_params=_params(("arbitrary", "arbitrary")))(h, dy, g_post, wd, sl, ud, *x_args)

    def down_w(s_ref, dh_ref, dw_ref, acc):
        kk = pl.program_id(1)

        @pl.when(kk == 0)
        def _():
            acc[...] = jnp.zeros(acc.shape, F32)

        acc[...] += lax.dot_general(s_ref[0], dh_ref[...], tn_dims, preferred_element_type=F32)

        @pl.when(kk == nk - 1)
        def _():
            dw_ref[0] = acc[...].astype(dw_ref.dtype)

    dwd = pl.pallas_call(
        down_w, name=tag + "_down_w", grid=(ns, nk),
        in_specs=[pl.BlockSpec((1, tk, fs), lambda q, kk: (q, kk, 0)), pl.BlockSpec((tk, d), lambda q, kk: (kk, 0))],
        out_specs=pl.BlockSpec((1, fs, d), lambda q, kk: (q, 0, 0)),
        out_shape=jax.ShapeDtypeStruct((ns, fs, d), MM_DTYPE), scratch_shapes=[pltpu.VMEM((fs, d), F32)],
        compiler_params=_params(("parallel", "arbitrary")))(s, dh)

    def gate_up_b(da_ref, du_ref, wg_ref, wu_ref, x_ref, dy_ref, g_ref, dx_ref, dg_ref, acc):
        i, q = pl.program_id(0), pl.program_id(1)

        @pl.when((i == 0) & (q == 0))
        def _():
            dg_ref[...] = jnp.zeros(dg_ref.shape, F32)

        @pl.when(q == 0)
        def _():
            acc[...] = jnp.zeros(acc.shape, F32)

        for r in _row_chunks(tm):
            acc[r, :] += (jnp.dot(da_ref[0, r, :], wg_ref[0], preferred_element_type=F32)
                          + jnp.dot(du_ref[0, r, :], wu_ref[0], preferred_element_type=F32))

        @pl.when(q == ns - 1)
        def _():
            for r in _row_chunks(tm):
                xb = x_ref[r, :]
                dx, dg = _rms_bwd(xb, _rms_stats(xb), g_ref[...], acc[r, :])
                dx_ref[r, :] = dy_ref[r, :] + dx
                dg_ref[...] += dg

    def gate_up_w(n_ref, da_ref, du_ref, dwg_ref, dwu_ref, acc_g, acc_u):
        kk = pl.program_id(1)

        @pl.when(kk == 0)
        def _():
            acc_g[...] = jnp.zeros(acc_g.shape, F32)
            acc_u[...] = jnp.zeros(acc_u.shape, F32)

        nb = n_ref[...]
        acc_g[...] += lax.dot_general(da_ref[0], nb, tn_dims, preferred_element_type=F32)
        acc_u[...] += lax.dot_general(du_ref[0], nb, tn_dims, preferred_element_type=F32)

        @pl.when(kk == nk - 1)
        def _():
            dwg_ref[0] = acc_g[...].astype(dwg_ref.dtype)
            dwu_ref[0] = acc_u[...].astype(dwu_ref.dtype)

    k3 = pl.BlockSpec((1, tk, fs), lambda q, kk: (q, kk, 0))
    wout = pl.BlockSpec((1, fs, d), lambda q, kk: (q, 0, 0))
    carried_mid = make_carried_mid(dwd) if make_carried_mid else None
    gate_up_w, x_in, x_out, x_shape, x_scr, x_args = _carry(gate_up_w, 3, 2, (ns, nk), carried_mid)
    dwg, dwu, *from_mid = pl.pallas_call(
        gate_up_w, name=tag + "_gate_up_w", grid=(ns, nk),
        in_specs=[pl.BlockSpec((tk, d), lambda q, kk: (kk, 0)), k3, k3] + x_in, out_specs=[wout, wout] + x_out,
        out_shape=[jax.ShapeDtypeStruct((ns, fs, d), MM_DTYPE)] * 2 + x_shape,
        scratch_shapes=[pltpu.VMEM((fs, d), F32)] * 2 + x_scr,
        compiler_params=_params(("arbitrary", "arbitrary")))(n, da, du, *x_args)

    carried_up = make_carried_up(dwg, dwu) if make_carried_up else None
    gate_up_b, x_in, x_out, x_shape, x_scr, x_args = _carry(gate_up_b, 7, 2, (nt, ns), carried_up)
    dx, dg_pre, *from_up = pl.pallas_call(
        gate_up_b, name=tag + "_gate_up_b", grid=(nt, ns), in_specs=[act3, act3, wrow, wrow, row, row, vec] + x_in,
        out_specs=[row, vec] + x_out,
        out_shape=[jax.ShapeDtypeStruct((t, d), F32), jax.ShapeDtypeStruct((1, d), F32)] + x_shape,
        scratch_shapes=[pltpu.VMEM((tm, d), F32)] + x_scr,
        compiler_params=_params(("arbitrary", "arbitrary")))(da, du, wg, wu, x, dy, g_pre, *x_args)
    return dx, dg_pre, dwg, dwu, dwd, dg_post, from_down, from_mid, from_up


NEG = -1e30


def _attn_scale():
    return (MLA_NOPE + MLA_ROPE) ** -0.5


def _causal_pairs(nq, by_key):
    if by_key:
        pairs = [(qi, ki) for ki in range(nq) for qi in range(ki, nq)]
    else:
        pairs = [(qi, ki) for qi in range(nq) for ki in range(qi + 1)]
    return jnp.asarray([p[0] for p in pairs], jnp.int32), jnp.asarray([p[1] for p in pairs], jnp.int32)


def _below_diagonal(shape):
    return lax.broadcasted_iota(jnp.int32, shape, 1) <= lax.broadcasted_iota(jnp.int32, shape, 0)


def _attn_call(name, body, tables, args, in_kinds, out_kinds, scratch, t, tq, carried=None):
    qmap = lambda h, p, qt, kt: (qt[p], h)
    kmap = lambda h, p, qt, kt: (kt[p], h)
    width = HEADS_PER_STEP * HEAD_LANES
    spec = lambda kind: pl.BlockSpec((tq, width), qmap if kind == "q" else kmap)
    n_pairs = tables[0].shape[0]
    n_groups = N_HEADS // HEADS_PER_STEP
    n_in, n_out, n_scr = len(in_kinds), len(out_kinds), scratch
    x_ins = list(carried.ins) if carried else []
    x_outs = list(carried.outs) if carried else []
    x_scr = [pltpu.SemaphoreType.DMA((carried.n_sem,)), pltpu.SemaphoreType.DMA((carried.n_sem,))] if carried else []

    def full_body(qt, kt, *refs):
        ins, refs = refs[:n_in], refs[n_in:]
        xi, refs = refs[:len(x_ins)], refs[len(x_ins):]
        outs, refs = refs[:n_out], refs[n_out:]
        xo, refs = refs[:len(x_outs)], refs[len(x_outs):]
        scr, sems = refs[:n_scr], refs[n_scr:]
        if carried:
            @pl.when((pl.program_id(0) == 0) & (pl.program_id(1) == 0))
            def _():
                carried.start(xi, xo, *sems)

        heads = [tuple(r.at[:, pl.ds(hh * HEAD_LANES, HEAD_LANES)] for r in (*ins, *outs, *scr))
                 for hh in range(HEADS_PER_STEP)]
        body(qt, kt, heads)
        if carried:
            @pl.when((pl.program_id(0) == n_groups - 1) & (pl.program_id(1) == n_pairs - 1))
            def _():
                carried.finish(xi, xo, *sems)

    grid_spec = pltpu.PrefetchScalarGridSpec(
        num_scalar_prefetch=2, grid=(n_groups, n_pairs),
        in_specs=[spec(kd) for kd in in_kinds] + [HBM_SPEC] * len(x_ins),
        out_specs=[spec(kd) for kd in out_kinds] + [HBM_SPEC] * len(x_outs),
        scratch_shapes=[pltpu.VMEM((tq, width), F32)] * n_scr + x_scr)
    return pl.pallas_call(full_body, name=name, grid_spec=grid_spec,
                          out_shape=[jax.ShapeDtypeStruct((t, MLA_PAD), F32) for _ in out_kinds] + x_outs,
                          compiler_params=_params(("arbitrary", "arbitrary")))(*tables, *args, *x_ins)


class _Carried:
    def __init__(self, ins, outs, n_sem, start, finish):
        self.ins, self.outs, self.n_sem, self.start, self.finish = ins, outs, n_sem, start, finish


def _attn_fwd(q, k, v, tq, carried=None):
    t = q.shape[0]
    nq = t // tq

    def body(qt, kt, heads):
        p_id = pl.program_id(1)
        qi, ki = qt[p_id], kt[p_id]

        @pl.when(ki == 0)
        def _():
            for _, _, _, _, _, m_s, l_s, acc_s in heads:
                m_s[...] = jnp.full(m_s.shape, NEG, F32)
                l_s[...] = jnp.zeros(l_s.shape, F32)
                acc_s[...] = jnp.zeros(acc_s.shape, F32)

        def update(diagonal):
            for q_ref, k_ref, v_ref, _, _, m_s, l_s, acc_s in heads:
                s = lax.dot_general(q_ref[...], k_ref[...], (((1,), (1,)), ((), ())), preferred_element_type=F32)
                if diagonal:
                    s = jnp.where(_below_diagonal(s.shape), s, NEG)
                m_old = m_s[...]
                m_new = jnp.maximum(m_old, jnp.max(s, axis=1, keepdims=True))
                alpha = jnp.exp(m_old - m_new)
                p = jnp.exp(s - m_new[:, :1])
                l_s[...] = l_s[...] * alpha + jnp.sum(p, axis=1, keepdims=True)
                acc_s[...] = acc_s[...] * alpha + jnp.dot(p.astype(MM_DTYPE), v_ref[...], preferred_element_type=F32)
                m_s[...] = m_new

        @pl.when(ki < qi)
        def _():
            update(False)

        @pl.when(ki == qi)
        def _():
            update(True)
            for _, _, _, o_ref, lse_ref, m_s, l_s, acc_s in heads:
                o_ref[...] = acc_s[...] / l_s[...]
                lse_ref[...] = m_s[...] + jnp.log(l_s[...])

    return _attn_call("mla_attn_fwd", body, _causal_pairs(nq, False), (q, k, v), "qkk", "qq", 3, t, tq, carried)


def _attn_probs(q, k, lse, diagonal):
    s = lax.dot_general(q, k, (((1,), (1,)), ((), ())), preferred_element_type=F32)
    p = jnp.exp(s - lse[:, :1])
    return jnp.where(_below_diagonal(s.shape), p, 0.0) if diagonal else p


BWD_HEADS = 2


def _attn_bwd(q, k, v, do, lse, delta, tq, carried=None):
    t = q.shape[0]
    nq = t // tq
    width = BWD_HEADS * HEAD_LANES
    n_groups = N_HEADS // BWD_HEADS
    qt_tab, kt_tab = _causal_pairs(nq, True)
    n_pairs = qt_tab.shape[0]
    qmap = lambda h, p, qt, kt: (qt[p], h)
    kmap = lambda h, p, qt, kt: (kt[p], h)
    qs, ks = pl.BlockSpec((tq, width), qmap), pl.BlockSpec((tq, width), kmap)
    x_ins = list(carried.ins) if carried else []
    x_outs = list(carried.outs) if carried else []
    x_scr = [pltpu.SemaphoreType.DMA((carried.n_sem,)), pltpu.SemaphoreType.DMA((carried.n_sem,))] if carried else []
    nt_dims = (((1,), (1,)), ((), ()))
    tn_dims = (((0,), (0,)), ((), ()))

    def body(qt, kt, q_ref, k_ref, v_ref, do_ref, lse_ref, dl_ref, *rest):
        xi, rest = rest[:len(x_ins)], rest[len(x_ins):]
        dq_hbm, dk_ref, dv_ref = rest[:3]
        xo, rest = rest[3:3 + len(x_outs)], rest[3 + len(x_outs):]
        dk_s, dv_s, dq_s, dq_sem = rest[:4]
        sems = rest[4:]
        grp, p_id = pl.program_id(0), pl.program_id(1)
        qi, ki = qt[p_id], kt[p_id]
        if carried:
            @pl.when((grp == 0) & (p_id == 0))
            def _():
                carried.start(xi, xo, *sems)

        @pl.when(p_id == 0)
        def _():
            dq_s[...] = jnp.zeros(dq_s.shape, F32)

        def step(diagonal):
            rows = pl.ds(pl.multiple_of(qi * tq, tq), tq)
            for hh in range(BWD_HEADS):
                ln = pl.ds(hh * HEAD_LANES, HEAD_LANES)
                qb, kb, vb, dob = q_ref[:, ln], k_ref[:, ln], v_ref[:, ln], do_ref[:, ln]
                p = _attn_probs(qb, kb, lse_ref[:, ln], diagonal)
                dv_s[:, ln] += lax.dot_general(p.astype(MM_DTYPE), dob, tn_dims, preferred_element_type=F32)
                dp = lax.dot_general(dob, vb, nt_dims, preferred_element_type=F32)
                ds = (p * (dp - dl_ref[:, ln][:, :1])).astype(MM_DTYPE)
                dk_s[:, ln] += lax.dot_general(ds, qb, tn_dims, preferred_element_type=F32)
                dq_s[rows, ln] += jnp.dot(ds, kb, preferred_element_type=F32)

        @pl.when(qi == ki)
        def _():
            dk_s[...] = jnp.zeros(dk_s.shape, F32)
            dv_s[...] = jnp.zeros(dv_s.shape, F32)
            step(True)

        @pl.when(qi > ki)
        def _():
            step(False)

        @pl.when(qi == nq - 1)
        def _():
            dk_ref[...] = dk_s[...]
            dv_ref[...] = dv_s[...]

        @pl.when(p_id == n_pairs - 1)
        def _():
            out = pltpu.make_async_copy(dq_s, dq_hbm.at[pl.ds(pl.multiple_of(grp * t, t), t)], dq_sem)
            out.start()
            out.wait()

        if carried:
            @pl.when((grp == n_groups - 1) & (p_id == n_pairs - 1))
            def _():
                carried.finish(xi, xo, *sems)

    grid_spec = pltpu.PrefetchScalarGridSpec(
        num_scalar_prefetch=2, grid=(n_groups, n_pairs),
        in_specs=[qs, ks, ks, qs, qs, qs] + [HBM_SPEC] * len(x_ins),
        out_specs=[HBM_SPEC, ks, ks] + [HBM_SPEC] * len(x_outs),
        scratch_shapes=[pltpu.VMEM((tq, width), F32), pltpu.VMEM((tq, width), F32), pltpu.VMEM((t, width), F32),
                        pltpu.SemaphoreType.DMA] + x_scr)
    return pl.pallas_call(
        body, name="mla_attn_bwd", grid_spec=grid_spec,
        out_shape=[jax.ShapeDtypeStruct((n_groups * t, width), F32), jax.ShapeDtypeStruct((t, MLA_PAD), F32),
                   jax.ShapeDtypeStruct((t, MLA_PAD), F32)] + x_outs,
        compiler_params=_params(("arbitrary", "arbitrary")))(qt_tab, kt_tab, q, k, v, do, lse, delta, *x_ins)


def _dot01(a, b, dims=(((1,), (0,)), ((), ())), ones="rhs"):
    val, sel = (a, b) if ones == "rhs" else (b, a)
    head = val.astype(BF16)
    tail = (val - head.astype(F32)).astype(BF16)
    sel = sel.astype(BF16)
    dot = lambda part: (lax.dot_general(part, sel, dims, preferred_element_type=F32) if ones == "rhs"
                        else lax.dot_general(sel, part, dims, preferred_element_type=F32))
    return dot(head) + dot(tail)


def _dot1(a, b, dims=(((1,), (0,)), ((), ()))):
    return lax.dot_general(a.astype(MM_DTYPE), b.astype(MM_DTYPE), dims, preferred_element_type=F32)


def _dot3(a, b, dims=(((1,), (0,)), ((), ()))):
    return lax.dot_general(a, b, dims, preferred_element_type=F32, precision=lax.Precision.HIGH)


NN3 = (((2,), (1,)), ((0,), (0,)))
NT3 = (((2,), (2,)), ((0,), (0,)))
TN3 = (((1,), (1,)), ((0,), (0,)))


def _tri_masks(nh):
    shape = (nh, CHUNK, CHUNK)
    return lax.broadcasted_iota(jnp.int32, shape, 1), lax.broadcasted_iota(jnp.int32, shape, 2)


def _gdn_chunk_common(k, gcc, bb, row, col, dot=_dot1):
    tril = row >= col
    gcr = jnp.swapaxes(gcc, 1, 2)
    dm = jnp.exp(jnp.where(tril, gcc - gcr, NEG))
    kb = k * bb
    lm = jnp.where(row > col, dot(kb, k, NT3) * dm, 0.0)
    return dm, kb, lm


def _unit_lower_inverse(lm, eye):
    t = eye - lm
    p = lm
    for _ in range(CHUNK.bit_length() - 2):
        p = _dot3(p, p, NN3)
        t = t + _dot3(t, p, NN3)
    return t


def _chunk_sum_matrix(tb, upper):
    r = lax.broadcasted_iota(jnp.int32, (tb, tb), 0)
    c = lax.broadcasted_iota(jnp.int32, (tb, tb), 1)
    same = (r // CHUNK) == (c // CHUNK)
    return (same & ((c >= r) if upper else (c <= r))).astype(F32)


def _gdn_fwd(q, k, v, gb, bb, carried=None):
    nh, t, dh = q.shape
    nchunk = t // CHUNK

    def body(q_ref, k_ref, v_ref, g_ref, b_ref, o_ref, sall_ref, tall_ref, s_s):
        @pl.when(pl.program_id(0) == 0)
        def _():
            s_s[...] = jnp.zeros(s_s.shape, F32)

        row, col = _tri_masks(nh)
        sh = s_s[...]
        for cc in range(cps):
            rows = pl.ds(cc * CHUNK, CHUNK)
            qh, kh, vh, bbh, gcc = q_ref[:, rows, :], k_ref[:, rows, :], v_ref[:, rows, :], b_ref[:, rows, :], \
                g_ref[:, rows, :]
            dm, kb, lm = _gdn_chunk_common(kh, gcc, bbh, row, col)
            eg = jnp.exp(gcc)
            glr = gcc[:, CHUNK - 1:CHUNK, :]
            th = _unit_lower_inverse(lm, (row == col).astype(F32))
            w = _dot1(th, kb * eg, NN3)
            u = _dot1(th, vh * bbh, NN3)
            at = jnp.where(row >= col, _dot1(qh, kh, NT3) * dm, 0.0)
            vn = u - _dot1(w, sh, NN3)
            o_ref[:, rows, :] = _dot1(qh * eg, sh, NN3) + _dot1(at, vn, NN3)
            kd = kh * jnp.exp(glr - gcc)
            sall_ref[:, cc] = sh
            tall_ref[:, rows, :] = th
            sh = sh * jnp.exp(glr) + _dot1(kd, vn, TN3)
        s_s[...] = sh

    cps = min(GDN_STEP_CHUNKS, nchunk)
    steps = nchunk // cps
    blk = pl.BlockSpec((nh, cps * CHUNK, dh), lambda n: (0, n, 0))
    body, x_in, x_out, x_shape, x_scr, x_args = _carry(body, 5, 3, (steps,), carried)
    return pl.pallas_call(
        body, name="gdn_fwd", grid=(steps,), in_specs=[blk] * 5 + x_in,
        out_specs=[blk, pl.BlockSpec((nh, cps, dh, dh), lambda n: (0, n, 0, 0)), blk] + x_out,
        out_shape=[jax.ShapeDtypeStruct((nh, t, dh), F32), jax.ShapeDtypeStruct((nh, nchunk, dh, dh), F32),
                   jax.ShapeDtypeStruct((nh, t, CHUNK), F32)] + x_shape,
        scratch_shapes=[pltpu.VMEM((nh, dh, dh), F32)] + x_scr,
        compiler_params=_params(("arbitrary",)))(q, k, v, gb, bb, *x_args)


def _gdn_bwd(q, k, v, gb, bb, sall, tall, do):
    nh, t, dh = q.shape
    nchunk = t // CHUNK

    def body(q_ref, k_ref, v_ref, g_ref, b_ref, sall_ref, tall_ref, do_ref,
             dq_ref, dk_ref, dv_ref, dg_ref, db_ref, ds_s):
        @pl.when(pl.program_id(0) == 0)
        def _():
            ds_s[...] = jnp.zeros(ds_s.shape, F32)

        row, col = _tri_masks(nh)
        tril, stril = row >= col, row > col
        rsum = lambda x: jnp.sum(x, axis=2, keepdims=True)
        dsp = ds_s[...]
        for cc in reversed(range(cps)):
            rows = pl.ds(cc * CHUNK, CHUNK)
            dsp = chunk_bwd(rows, cc, dsp, row, col, tril, stril, rsum, q_ref, k_ref, v_ref, g_ref, b_ref, sall_ref,
                            tall_ref, do_ref, dq_ref, dk_ref, dv_ref, dg_ref, db_ref)
        ds_s[...] = dsp

    def chunk_bwd(rows, cc, dsp, row, col, tril, stril, rsum, q_ref, k_ref, v_ref, g_ref, b_ref, sall_ref, tall_ref,
                  do_ref, dq_ref, dk_ref, dv_ref, dg_ref, db_ref):
        qh, kh, vh, gcc, bbh = q_ref[:, rows, :], k_ref[:, rows, :], v_ref[:, rows, :], g_ref[:, rows, :], \
            b_ref[:, rows, :]
        sh, th, doh = sall_ref[:, cc], tall_ref[:, rows, :], do_ref[:, rows, :]
        dm, kb, lm = _gdn_chunk_common(kh, gcc, bbh, row, col, _dot3)
        eg = jnp.exp(gcc)
        glr = gcc[:, CHUNK - 1:CHUNK, :]
        glv = jnp.exp(glr)
        egl = jnp.exp(glr - gcc)
        rw, ru = kb * eg, vh * bbh
        w, u = _dot3(th, rw, NN3), _dot3(th, ru, NN3)
        at = jnp.where(tril, _dot3(qh, kh, NT3) * dm, 0.0)
        qd, kd = qh * eg, kh * egl
        vn = u - _dot3(w, sh, NN3)
        dgl = jnp.sum(rsum(dsp * sh), axis=1, keepdims=True)
        dkd = _dot3(vn, dsp, NT3)
        dvn = _dot3(kd, dsp, NN3)
        dqd = _dot3(doh, sh, NT3)
        dat = jnp.where(tril, _dot3(doh, vn, NT3), 0.0)
        dvn = dvn + _dot3(at, doh, TN3)
        dw = -_dot3(dvn, sh, NT3)
        ds_before = dsp * glv + _dot3(qd, doh, TN3) - _dot3(w, dvn, TN3)
        dpa = dat * dm
        dq_ref[:, rows, :] = _dot1(dpa, kh, NN3) + dqd * eg
        dk = _dot1(dpa, qh, TN3) + dkd * egl
        t6 = rsum(dkd * kd)
        dgam = rsum(dqd * qd) - t6
        dgam_last = jnp.sum(t6, axis=1, keepdims=True) + dgl * glv
        drw = _dot3(th, dw, TN3)
        dru = _dot3(th, dvn, TN3)
        dl = -jnp.where(stril, _dot3(drw, w, NT3) + _dot3(dru, u, NT3), 0.0)
        dgam = dgam + rsum(drw * rw)
        dv_ref[:, rows, :] = dru * bbh
        dp2 = dl * dm
        dkb = drw * eg + _dot1(dp2, kh, NN3)
        dk_ref[:, rows, :] = dk + _dot1(dp2, kb, TN3) + dkb * bbh
        db_ref[:, rows, :] = rsum(dru * vh) + rsum(dkb * kh) + jnp.zeros((nh, CHUNK, dh), F32)
        e = dat * at + dl * lm
        dgam_b = dgam + rsum(e) - _dot01(e, jnp.ones((nh, CHUNK, CHUNK), F32), TN3)
        dg_ref[:, rows, :] = dgam_b + jnp.where(row == CHUNK - 1, dgam_last, 0.0)
        return ds_before

    cps = min(GDN_STEP_CHUNKS, nchunk)
    steps = nchunk // cps
    rev = lambda n: (0, steps - 1 - n, 0)
    blk = pl.BlockSpec((nh, cps * CHUNK, dh), rev)
    sblk = pl.BlockSpec((nh, cps, dh, dh), lambda n: (0, steps - 1 - n, 0, 0))
    out = jax.ShapeDtypeStruct((nh, t, dh), F32)
    return pl.pallas_call(
        body, name="gdn_bwd", grid=(steps,), in_specs=[blk] * 5 + [sblk, blk, blk], out_specs=[blk] * 5,
        out_shape=[out] * 5, scratch_shapes=[pltpu.VMEM((nh, dh, dh), F32)],
        compiler_params=_params(("arbitrary",)))(q, k, v, gb, bb, sall, tall, do)


def _group_ones():
    r = lax.broadcasted_iota(jnp.int32, (GDN_W, GDN_W), 0) // GDN_DH
    c = lax.broadcasted_iota(jnp.int32, (GDN_W, GDN_W), 1) // GDN_DH
    return (r == c).astype(F32)


def _conv_taps(x, xprev, w, has_prev):
    row = lax.broadcasted_iota(jnp.int32, x.shape, 0)
    out = x * w[GDN_CONV - 1:GDN_CONV, :]
    for s in range(1, GDN_CONV):
        sh = jnp.where(row >= s, _roll(x, s, 0), _roll(xprev, s, 0) * has_prev)
        out = out + sh * w[GDN_CONV - 1 - s:GDN_CONV - s, :]
    return out


def _head_cols(x, h):
    return x[:, h * GDN_DH:(h + 1) * GDN_DH]


def _heads_spec(tb):
    return pl.BlockSpec((N_HEADS, tb, GDN_DH), lambda i: (0, i, 0))


def _mixer_fwd(x, positions, w, tb, carried=None, carried_gdn=None):
    t, d = x.shape
    tables = _rope_tables(positions)

    hn, proj = _prologue_mm("mix_in", [x], w["mix_pre_g"], lambda xb, g: (xb * _rms_stats(xb) * g,), w["w_in_pad_t"],
                            False)

    def mla_pre(p0, gq, gkv):
        cq, ckv = p0[:, :MLA_Q_RANK], p0[:, MLA_Q_RANK:MLA_Q_RANK + MLA_KV_RANK]
        return cq * _rms_stats(cq) * gq, ckv * _rms_stats(ckv) * gkv

    nq, nkv = _rowwise("mla_pre", mla_pre, [(proj, 512, PIN_MLA // 512, 0)],
                       [w["mla_q_norm_g"], w["mla_kv_norm_g"]], [(MLA_Q_RANK, BF16), (MLA_KV_RANK, BF16)], [], tb)
    qraw = _mm("mla_uq", nq, w["w_uq_pad"], "nn", F32)
    kv = _mm("mla_ukv", nkv, w["w_kv_pad"], "nn", F32)

    def rope_f(qr, kn, vv, kpe, c, s1, s2):
        qo = _heads_apply(qr, lambda xh: _rope(xh, c, s1, s2)) * _attn_scale()
        kp = _rope(kpe, c, s1, s2)
        return qo, kn + jnp.tile(kp, (1, N_HEADS)), vv

    q, k, v = _rowwise("mla_rope", rope_f,
                       [qraw, (kv, MLA_PAD, 0, 0), (kv, MLA_PAD, 1, 0), (proj, HEAD_LANES, PIN_KPE // HEAD_LANES, 0),
                        tables[0], tables[1], tables[2]], [],
                       [(MLA_PAD, BF16)] * 3, [], tb // 2)
    tq = min(1024, t)
    o, lse, *carried_out = _attn_fwd(q, k, v, tq, carried)

    def mla_post(ob, g):
        return (ob * _rms_stats(ob, N_HEADS * MLA_V) * g,)

    (cat,) = _rowwise("mla_post", mla_post, [o], [w["mla_out_g_pad"]], [(MLA_PAD, BF16)], [], tb, wide=(CAT_W, 0))

    gones = _group_ones()
    steps = t // tb

    def gdn_pre(xq, xk, xv, pq, pk, pv, cw, go, has_prev):
        outs = []
        for j, (xc, xp) in enumerate(((xq, pq), (xk, pk), (xv, pv))):
            c = _conv_taps(xc, xp, cw[:, j * GDN_W:(j + 1) * GDN_W], has_prev)
            a = c * _sigmoid(c)
            if j < 2:
                rn = lax.rsqrt(_dot01(a * a, go) + EPS)
                a = a * rn
                if j == 0:
                    a = a * (GDN_DH ** -0.5)
            outs.append(a)
        return tuple(outs)

    qh, kh, vh = _gdn_pre_call("gdn_pre", gdn_pre, proj, w["conv_w"], gones, tb, steps)
    heads_shape = jax.ShapeDtypeStruct((N_HEADS, t, GDN_DH), F32)
    lanes_shape = jax.ShapeDtypeStruct((t, HEAD_LANES), F32)
    lanes_spec = pl.BlockSpec((tb, HEAD_LANES), lambda i: (i, 0))
    vec_spec = lambda n: pl.BlockSpec((1, n), lambda i: (0, 0))

    def gate_f(ab_ref, al_ref, dt_ref, g_ref, b_ref, gh_ref, bh_ref):
        g, b = _gb_fwd(ab_ref[...], al_ref[...], dt_ref[...])
        g_ref[...] = g
        b_ref[...] = b
        gc = _dot01(_chunk_sum_matrix(tb, False), g, ones="lhs")
        for h in range(N_HEADS):
            gh_ref[h] = jnp.broadcast_to(gc[:, h:h + 1], (tb, GDN_DH))
            bh_ref[h] = jnp.broadcast_to(b[:, N_HEADS + h:N_HEADS + h + 1], (tb, GDN_DH))

    g128, b128, gbh, bbh = pl.pallas_call(
        gate_f, name="gdn_gate_f", grid=(steps,),
        in_specs=[pl.BlockSpec((tb, HEAD_LANES), lambda i: (i, PIN_AB // HEAD_LANES)), vec_spec(HEAD_LANES),
                  vec_spec(HEAD_LANES)],
        out_specs=[lanes_spec, lanes_spec, _heads_spec(tb), _heads_spec(tb)],
        out_shape=[lanes_shape, lanes_shape, heads_shape, heads_shape],
        compiler_params=_params(("arbitrary",)))(proj, w["a_log_pad"], w["dt_bias_pad"])
    oh, sall, tall, *carried_out_gdn = _gdn_fwd(qh, kh, vh, gbh, bbh, carried_gdn)

    def gdn_post(o_ref, gt_ref, g_ref, cat_in, cat_ref):
        gt, g = gt_ref[...], g_ref[...]
        outs = []
        for h in range(N_HEADS):
            ob, gth = o_ref[h], _head_cols(gt, h)
            outs.append(ob * _rms_stats(ob) * g * (gth * _sigmoid(gth)))
        cat_ref[...] = jnp.concatenate(outs, axis=1).astype(cat_ref.dtype)

    gate_spec = pl.BlockSpec((tb, GDN_W), lambda i: (i, PIN_GATE // GDN_W))
    cat = pl.pallas_call(
        gdn_post, name="gdn_post", grid=(steps,),
        in_specs=[_heads_spec(tb), gate_spec, vec_spec(GDN_DH), ANY_SPEC],
        out_specs=pl.BlockSpec((tb, GDN_W), lambda i: (i, MLA_PAD // GDN_W)),
        out_shape=jax.ShapeDtypeStruct((t, CAT_W), BF16), input_output_aliases={3: 0},
        compiler_params=_params(("arbitrary",)))(oh, proj, w["gdn_norm_g"], cat)
    mixed, y = _mm_epilogue("mix_out", cat, w["w_out_pad"], [x], w["mix_post_g"],
                            lambda hb, xb, g: (hb, xb + hb * _rms_stats(hb) * g), 2, False, tk=CAT_W)
    saved = dict(x=x, hn=hn, proj=proj, nq=nq, nkv=nkv, q=q, k=k, v=v, o=o, lse=lse, qh=qh, kh=kh, vh=vh,
                 gbh=gbh, bbh=bbh, oh=oh, sall=sall, tall=tall, cat=cat, mixed=mixed,
                 tables=tables, g128=g128, b128=b128)
    return y, saved, list(carried_out) + list(carried_out_gdn)


def _qkv_specs(tb):
    base = PIN_QKV // GDN_W
    cur = [pl.BlockSpec((tb, GDN_W), lambda i, j=j: (i, base + j)) for j in range(3)]
    prev = [pl.BlockSpec((tb, GDN_W), lambda i, j=j: (jnp.maximum(i - 1, 0), base + j)) for j in range(3)]
    return cur + prev


def _gdn_pre_call(name, fn, proj, conv_w, gones, tb, steps):
    t = proj.shape[0]

    def body(xq, xk, xv, pq, pk, pv, cw, go, oq, ok, ov):
        has_prev = jnp.where(pl.program_id(0) == 0, 0.0, 1.0)
        outs = fn(xq[...], xk[...], xv[...], pq[...], pk[...], pv[...], cw[...], go[...], has_prev)
        for r, val in zip((oq, ok, ov), outs):
            for h in range(N_HEADS):
                r[h] = _head_cols(val, h)

    return pl.pallas_call(
        body, name=name, grid=(steps,),
        in_specs=_qkv_specs(tb) + [pl.BlockSpec(conv_w.shape, lambda i: (0, 0)),
                                   pl.BlockSpec(gones.shape, lambda i: (0, 0))],
        out_specs=[_heads_spec(tb)] * 3,
        out_shape=[jax.ShapeDtypeStruct((N_HEADS, t, GDN_DH), F32)] * 3,
        compiler_params=_params(("arbitrary",)))(proj, proj, proj, proj, proj, proj, conv_w, gones)


def _softplus(x):
    return jnp.maximum(x, 0.0) + jnp.log1p(jnp.exp(-jnp.abs(x)))


def _gb_fwd(ab, a_log, dt_bias):
    g = -jnp.exp(a_log) * _softplus(ab + dt_bias)
    return g, _sigmoid(ab)


def _rope_tables(positions):
    half = MLA_ROPE // 2
    freqs = ROPE_THETA ** (-jnp.arange(half, dtype=F32) / half)
    ang = positions.reshape(-1).astype(F32)[:, None] * freqs
    cos, sin = jnp.cos(ang), jnp.sin(ang)
    t = ang.shape[0]
    one = jnp.ones((t, MLA_NOPE), F32)
    z16, z32, z64 = jnp.zeros((t, half), F32), jnp.zeros((t, MLA_ROPE), F32), jnp.zeros((t, MLA_NOPE), F32)
    c = jnp.concatenate([one, cos, cos, jnp.ones((t, MLA_ROPE), F32)], axis=1)
    s1 = jnp.concatenate([z64, -sin, z16, z32], axis=1)
    s2 = jnp.concatenate([z64, z16, sin, z32], axis=1)
    return c, s1, s2


def _mixer_bwd(dy, sv, w, tb, carried=None):
    x, proj = sv["x"], sv["proj"]
    t, d = x.shape
    c, s1, s2 = sv["tables"]
    grads = {}

    dmixed, dcat, grads["mix_post_g"] = _prologue_mm(
        "mix_out_bx", [sv["mixed"], dy], w["mix_post_g"], lambda hb, dyb, g: _rms_bwd(hb, _rms_stats(hb), g, dyb),
        w["w_out_pad"], True)
    grads["w_out_pad"] = _mm("mix_out_bw", sv["cat"], dmixed, "tn", F32)
    steps = t // tb
    vec_spec = lambda n: pl.BlockSpec((1, n), lambda i: (0, 0))

    def gdn_post_b(o_ref, gt_ref, do_ref, g_ref, dproj_ref, doh_ref, dg_ref):
        @pl.when(pl.program_id(0) == 0)
        def _():
            dg_ref[...] = jnp.zeros(dg_ref.shape, F32)

        gt, dob, g = gt_ref[...], do_ref[...], g_ref[...]
        dgates = []
        for h in range(N_HEADS):
            ob, gth, dobh = o_ref[h], _head_cols(gt, h), _head_cols(dob, h)
            sg = _sigmoid(gth)
            r = _rms_stats(ob)
            dxo, dg = _rms_bwd(ob, r, g, dobh * (gth * sg))
            doh_ref[h] = dxo
            dg_ref[...] += dg
            dgates.append(dobh * (ob * r * g) * (sg * (1.0 + gth * (1.0 - sg))))
        dproj_ref[...] = jnp.concatenate(dgates, axis=1).astype(dproj_ref.dtype)

    dproj, doh, grads["gdn_norm_g"] = pl.pallas_call(
        gdn_post_b, name="gdn_post_b", grid=(steps,),
        in_specs=[_heads_spec(tb), pl.BlockSpec((tb, GDN_W), lambda i: (i, PIN_GATE // GDN_W)),
                  pl.BlockSpec((tb, GDN_W), lambda i: (i, MLA_PAD // GDN_W)), vec_spec(GDN_DH)],
        out_specs=[pl.BlockSpec((tb, GDN_W), lambda i: (i, PIN_GATE // GDN_W)), _heads_spec(tb), vec_spec(GDN_DH)],
        out_shape=[jax.ShapeDtypeStruct((t, PIN_W), BF16), jax.ShapeDtypeStruct((N_HEADS, t, GDN_DH), F32),
                   jax.ShapeDtypeStruct((1, GDN_DH), F32)],
        compiler_params=_params(("arbitrary",)))(sv["oh"], proj, dcat, w["gdn_norm_g"])

    def mla_post_b(ob, dmo, g):
        do, dg = _rms_bwd(ob, _rms_stats(ob, N_HEADS * MLA_V), g, dmo, N_HEADS * MLA_V)
        prod = do * ob
        delta = _heads_apply(prod, lambda ph: jnp.sum(ph, axis=1, keepdims=True) + jnp.zeros_like(ph))
        return do, delta, dg

    do, delta, grads["mla_out_g_pad"] = _rowwise(
        "mla_post_b", mla_post_b, [sv["o"], (dcat, MLA_PAD, 0, 0)], [w["mla_out_g_pad"]],
        [(MLA_PAD, BF16), (MLA_PAD, F32)], [(1, MLA_PAD)], tb // 2)
    tq = min(1024, t)
    dq, dk, dv, *carried_out = _attn_bwd(sv["q"], sv["k"], sv["v"], do, sv["lse"], delta, tq, carried)
    n_groups = N_HEADS // BWD_HEADS
    tr = tb // 2
    dq_groups = [(dq, BWD_HEADS * HEAD_LANES, 0, grp * (t // tr)) for grp in range(n_groups)]

    def rope_b(*blocks):
        dqb = jnp.concatenate(blocks[:n_groups], axis=1)
        dkb, dvb, cc, a1, a2 = blocks[n_groups:]
        dqr = _heads_apply(dqb * _attn_scale(), lambda xh: _rope(xh, cc, -a1, -a2))
        ksum = dkb[:, :HEAD_LANES]
        for h in range(1, N_HEADS):
            ksum = ksum + dkb[:, h * HEAD_LANES:(h + 1) * HEAD_LANES]
        lane = lax.broadcasted_iota(jnp.int32, ksum.shape, 1)
        keep = (lane >= MLA_NOPE) & (lane < MLA_NOPE + MLA_ROPE)
        dkpe = jnp.where(keep, _rope(ksum, cc, -a1, -a2), 0.0)
        return dqr, jnp.concatenate([dkb, dvb], axis=1), dkpe

    dqraw, dkv, dkpe = _rowwise("mla_rope_b", rope_b, dq_groups + [dk, dv, c, s1, s2], [],
                                [(MLA_PAD, BF16, t), (2 * MLA_PAD, BF16), (HEAD_LANES, F32)], [], tr)
    dnq = _mm("mla_uq_bx", dqraw, w["w_uq_pad"], "nt", F32)
    grads["w_uq_pad"] = _mm("mla_uq_bw", sv["nq"], dqraw, "tn", F32)
    dnkv = _mm("mla_ukv_bx", dkv, w["w_kv_pad"], "nt", F32)
    grads["w_kv_pad"] = _mm("mla_ukv_bw", sv["nkv"], dkv, "tn", F32)

    def mla_pre_b(p0, dnqb, dnkvb, dkpeb, gq, gkv):
        cq, ckv = p0[:, :MLA_Q_RANK], p0[:, MLA_Q_RANK:MLA_Q_RANK + MLA_KV_RANK]
        dcq, dgq = _rms_bwd(cq, _rms_stats(cq), gq, dnqb)
        dckv, dgkv = _rms_bwd(ckv, _rms_stats(ckv), gkv, dnkvb)
        return jnp.concatenate([dcq, dckv, dkpeb], axis=1), dgq, dgkv

    dproj, grads["mla_q_norm_g"], grads["mla_kv_norm_g"] = _rowwise(
        "mla_pre_b", mla_pre_b, [(proj, 512, PIN_MLA // 512, 0), dnq, dnkv, dkpe],
        [w["mla_q_norm_g"], w["mla_kv_norm_g"]], [(512, BF16)], [(1, MLA_Q_RANK), (1, MLA_KV_RANK)], tb,
        wide=(PIN_W, PIN_MLA // 512), carry=dproj)

    dqh, dkh, dvh, dgh, dbh = _gdn_bwd(sv["qh"], sv["kh"], sv["vh"], sv["gbh"], sv["bbh"], sv["sall"], sv["tall"], doh)
    gones = _group_ones()

    def gdn_pre_b(xq, xk, xv, pq, pk, pv, dq_, dk_, dv_, cw, go, has_prev):
        outs = []
        for j, (xc, xp, dd) in enumerate(((xq, pq, dq_), (xk, pk, dk_), (xv, pv, dv_))):
            cc = _conv_taps(xc, xp, cw[:, j * GDN_W:(j + 1) * GDN_W], has_prev)
            sg = _sigmoid(cc)
            a = cc * sg
            if j < 2:
                rn = lax.rsqrt(_dot01(a * a, go) + EPS)
                if j == 0:
                    dd = dd * (GDN_DH ** -0.5)
                da = rn * dd - a * (rn * rn * rn) * _dot01(dd * a, go)
            else:
                da = dd
            outs.append(da * (sg * (1.0 + cc * (1.0 - sg))))
        return tuple(outs)

    dcq, dck, dcv = _gdn_pre_b_call("gdn_pre_b", gdn_pre_b, proj, (dqh, dkh, dvh), w["conv_w"], gones, tb, steps)
    dproj, grads["conv_w"] = _conv_bwd_call("gdn_conv_b", proj, (dcq, dck, dcv), w["conv_w"], dproj, tb, steps)

    def gate_b(ab_ref, g_ref, b_ref, dgh_ref, dbh_ref, al_ref, dt_ref, carry_ref, dab_ref, dal_ref, ddt_ref):
        @pl.when(pl.program_id(0) == 0)
        def _():
            dal_ref[...] = jnp.zeros(dal_ref.shape, F32)
            ddt_ref[...] = jnp.zeros(ddt_ref.shape, F32)

        ab, g128, b128 = ab_ref[...], g_ref[...], b_ref[...]
        lane = lax.broadcasted_iota(jnp.int32, ab.shape, 1)
        dg_ = jnp.zeros(ab.shape, F32)
        db_ = jnp.zeros(ab.shape, F32)
        for h in range(N_HEADS):
            dg_ = dg_ + jnp.where(lane == h, jnp.broadcast_to(dgh_ref[h][:, 0:1], ab.shape), 0.0)
            db_ = db_ + jnp.where(lane == N_HEADS + h, jnp.broadcast_to(dbh_ref[h][:, 0:1], ab.shape), 0.0)
        dg_ = _dot01(_chunk_sum_matrix(tb, True), dg_, ones="lhs")
        slope = -jnp.exp(al_ref[...]) * _sigmoid(ab + dt_ref[...])
        dab_ref[...] = (dg_ * slope + db_ * b128 * (1.0 - b128)).astype(dab_ref.dtype)
        dal_ref[...] += jnp.sum(dg_ * g128, axis=0, keepdims=True)
        ddt_ref[...] += jnp.sum(dg_ * slope, axis=0, keepdims=True)

    lanes_spec = pl.BlockSpec((tb, HEAD_LANES), lambda i: (i, 0))
    ab_spec = pl.BlockSpec((tb, HEAD_LANES), lambda i: (i, PIN_AB // HEAD_LANES))
    dproj, grads["a_log_pad"], grads["dt_bias_pad"] = pl.pallas_call(
        gate_b, name="gdn_gate_b", grid=(steps,),
        in_specs=[ab_spec, lanes_spec, lanes_spec, _heads_spec(tb), _heads_spec(tb), vec_spec(HEAD_LANES),
                  vec_spec(HEAD_LANES), ANY_SPEC],
        out_specs=[ab_spec, vec_spec(HEAD_LANES), vec_spec(HEAD_LANES)],
        out_shape=[jax.ShapeDtypeStruct((t, PIN_W), BF16), jax.ShapeDtypeStruct((1, HEAD_LANES), F32),
                   jax.ShapeDtypeStruct((1, HEAD_LANES), F32)],
        input_output_aliases={7: 0},
        compiler_params=_params(("arbitrary",)))(proj, sv["g128"], sv["b128"], dgh, dbh, w["a_log_pad"],
                                                 w["dt_bias_pad"], dproj)
    grads["w_in_pad_t"] = _mm("mix_in_bw", dproj, sv["hn"], "tn", F32)

    def pre_b(dhn, xb, dyb, g):
        dx, dg = _rms_bwd(xb, _rms_stats(xb), g, dhn)
        return dyb + dx, dg

    dx, grads["mix_pre_g"] = _mm_epilogue("mix_in_bx", dproj, w["w_in_pad_t"], [x, dy], w["mix_pre_g"], pre_b, 1, True,
                                          tk=PIN_W // 3)
    return dx, grads, carried_out


def _gdn_pre_b_call(name, fn, proj, dd, conv_w, gones, tb, steps):
    t = proj.shape[0]

    def body(xq, xk, xv, pq, pk, pv, d0, d1, d2, cw, go, oq, ok, ov):
        has_prev = jnp.where(pl.program_id(0) == 0, 0.0, 1.0)
        dd_rows = [jnp.concatenate([dr[h] for h in range(N_HEADS)], axis=1) for dr in (d0, d1, d2)]
        outs = fn(xq[...], xk[...], xv[...], pq[...], pk[...], pv[...], *dd_rows, cw[...], go[...], has_prev)
        for r, val in zip((oq, ok, ov), outs):
            r[...] = val

    return pl.pallas_call(
        body, name=name, grid=(steps,),
        in_specs=_qkv_specs(tb) + [_heads_spec(tb)] * 3 + [pl.BlockSpec(conv_w.shape, lambda i: (0, 0)),
                                                          pl.BlockSpec(gones.shape, lambda i: (0, 0))],
        out_specs=[pl.BlockSpec((tb, GDN_W), lambda i: (i, 0))] * 3,
        out_shape=[jax.ShapeDtypeStruct((t, GDN_W), F32)] * 3,
        compiler_params=_params(("arbitrary",)))(proj, proj, proj, proj, proj, proj, *dd, conv_w, gones)


def _conv_bwd_call(name, proj, dc, conv_w, dproj, tb, steps):
    t = proj.shape[0]
    dcur = [pl.BlockSpec((tb, GDN_W), lambda i: (i, 0))] * 3
    dnext = [pl.BlockSpec((tb, GDN_W), lambda i: (jnp.minimum(i + 1, steps - 1), 0))] * 3

    def body(xq, xk, xv, pq, pk, pv, d0, d1, d2, n0, n1, n2, cw, carry_ref, dx_ref, dw_ref):
        i = pl.program_id(0)
        has_prev = jnp.where(i == 0, 0.0, 1.0)
        has_next = jnp.where(i == steps - 1, 0.0, 1.0)

        @pl.when(i == 0)
        def _():
            dw_ref[...] = jnp.zeros(dw_ref.shape, F32)

        wv = cw[...]
        dws, dxs = [], []
        for j, (xr, pr, dr, nr) in enumerate(((xq, pq, d0, n0), (xk, pk, d1, n1), (xv, pv, d2, n2))):
            x, xp, dcv, dnx = xr[...], pr[...], dr[...], nr[...]
            wj = wv[:, j * GDN_W:(j + 1) * GDN_W]
            row = lax.broadcasted_iota(jnp.int32, x.shape, 0)
            dx = dcv * wj[GDN_CONV - 1:GDN_CONV, :]
            rows_w = [jnp.sum(dcv * x, axis=0, keepdims=True)]
            for s in range(1, GDN_CONV):
                up = jnp.where(row < tb - s, _roll(dcv, tb - s, 0), _roll(dnx, tb - s, 0) * has_next)
                dx = dx + up * wj[GDN_CONV - 1 - s:GDN_CONV - s, :]
                sh = jnp.where(row >= s, _roll(x, s, 0), _roll(xp, s, 0) * has_prev)
                rows_w.append(jnp.sum(dcv * sh, axis=0, keepdims=True))
            dxs.append(dx)
            dws.append(jnp.concatenate(rows_w[::-1], axis=0))
        dx_ref[...] = jnp.concatenate(dxs, axis=1).astype(dx_ref.dtype)
        dw_ref[...] += jnp.concatenate(dws, axis=1)

    return pl.pallas_call(
        body, name=name, grid=(steps,),
        in_specs=_qkv_specs(tb) + dcur + dnext + [pl.BlockSpec(conv_w.shape, lambda i: (0, 0)), ANY_SPEC],
        out_specs=[pl.BlockSpec((tb, 3 * GDN_W), lambda i: (i, PIN_QKV // (3 * GDN_W))),
                   pl.BlockSpec(conv_w.shape, lambda i: (0, 0))],
        out_shape=[jax.ShapeDtypeStruct((t, PIN_W), BF16), jax.ShapeDtypeStruct(conv_w.shape, F32)],
        input_output_aliases={13: 0},
        compiler_params=_params(("arbitrary",)))(proj, proj, proj, proj, proj, proj, *dc, *dc, conv_w, dproj)


def _pad_heads_cols(wm, per_head):
    r = wm.shape[0]
    return jnp.pad(wm.reshape(r, N_HEADS, per_head), ((0, 0), (0, 0), (0, HEAD_LANES - per_head))).reshape(r, MLA_PAD)


def _unpad_heads_cols(wm, per_head):
    r = wm.shape[0]
    return wm.reshape(r, N_HEADS, HEAD_LANES)[:, :, :per_head].reshape(r, N_HEADS * per_head)


W_IN_COLS = MLA_Q_RANK + MLA_KV_RANK + MLA_ROPE + 3 * GDN_W + 2 * N_HEADS + GDN_W
W_IN_SHARD = W_IN_COLS // N_SHARD
W_IN_SHARD_PAD = 640
_Q0 = MLA_Q_RANK + MLA_KV_RANK
_Q1 = _Q0 + MLA_ROPE
_Q2 = _Q1 + 3 * GDN_W
_Q3 = _Q2 + 2 * N_HEADS
W_IN_SEGMENTS = [(0, _Q0, PIN_MLA), (_Q0, _Q1, PIN_KPE + MLA_NOPE), (_Q1, _Q2, PIN_QKV), (_Q2, _Q3, PIN_AB),
                 (_Q3, W_IN_COLS, PIN_GATE)]


def _win_pad_t(slabs):
    d = slabs.shape[2]
    pieces, at = [], 0
    for c0, c1, r0 in sorted(W_IN_SEGMENTS, key=lambda s: s[2]):
        if r0 > at:
            pieces.append(jnp.zeros((r0 - at, d), slabs.dtype))
        for q in range(N_SHARD):
            lo, hi = max(c0, q * W_IN_SHARD), min(c1, (q + 1) * W_IN_SHARD)
            if lo < hi:
                pieces.append(slabs[q, lo - q * W_IN_SHARD:hi - q * W_IN_SHARD])
        at = r0 + c1 - c0
    pieces.append(jnp.zeros((PIN_W - at, d), slabs.dtype))
    return jnp.concatenate(pieces, axis=0)


def _win_cols_t(wp_t, c_lo, c_hi):
    pieces = []
    for c0, c1, r0 in W_IN_SEGMENTS:
        lo, hi = max(c0, c_lo), min(c1, c_hi)
        if lo < hi:
            pieces.append(wp_t[r0 + lo - c0:r0 + hi - c0])
    return jnp.concatenate(pieces, axis=0)


def _wkv_to_pad(wkv):
    r = wkv.shape[0]
    w3 = wkv.reshape(r, N_HEADS, MLA_NOPE + MLA_V)
    kpart = jnp.pad(w3[:, :, :MLA_NOPE], ((0, 0), (0, 0), (0, HEAD_LANES - MLA_NOPE))).reshape(r, MLA_PAD)
    vpart = jnp.pad(w3[:, :, MLA_NOPE:], ((0, 0), (0, 0), (0, HEAD_LANES - MLA_V))).reshape(r, MLA_PAD)
    return jnp.concatenate([kpart, vpart], axis=1)


def _wkv_from_pad(wp):
    r = wp.shape[0]
    kpart = wp[:, :MLA_PAD].reshape(r, N_HEADS, HEAD_LANES)[:, :, :MLA_NOPE]
    vpart = wp[:, MLA_PAD:].reshape(r, N_HEADS, HEAD_LANES)[:, :, :MLA_V]
    return jnp.concatenate([kpart, vpart], axis=2).reshape(r, N_HEADS * (MLA_NOPE + MLA_V))


def _wout_to_pad(wo):
    n = wo.shape[1]
    mla = jnp.pad(wo[:N_HEADS * MLA_V].reshape(N_HEADS, MLA_V, n), ((0, 0), (0, HEAD_LANES - MLA_V), (0, 0)))
    return jnp.concatenate([mla.reshape(MLA_PAD, n), wo[N_HEADS * MLA_V:]], axis=0)


def _wout_from_pad(wp):
    n = wp.shape[1]
    mla = wp[:MLA_PAD].reshape(N_HEADS, HEAD_LANES, n)[:, :MLA_V].reshape(N_HEADS * MLA_V, n)
    return jnp.concatenate([mla, wp[MLA_PAD:]], axis=0)


def _pad_lanes(v, n):
    return jnp.pad(v, ((0, 0), (0, n - v.shape[1])))


def _compute_weights(full):
    w = {}
    for n in FFN_BIG:
        if n in full:
            w[n] = full[n].astype(MM_DTYPE)
    w["w_in_pad_t"] = _win_pad_t(full["w_in"]).astype(MM_DTYPE)
    w["w_uq_pad"] = _pad_heads_cols(full["mla_w_uq"], MLA_NOPE + MLA_ROPE).astype(MM_DTYPE)
    w["w_kv_pad"] = _wkv_to_pad(full["mla_w_ukv"]).astype(MM_DTYPE)
    w["w_out_pad"] = _wout_to_pad(full["w_out"]).astype(MM_DTYPE)
    w["conv_w"] = full["gdn_conv_w"].astype(F32)
    for n in ("ffn1_pre_g", "ffn1_post_g", "mix_pre_g", "mla_q_norm_g", "mla_kv_norm_g", "gdn_norm_g", "mix_post_g",
              "ffn2_pre_g", "ffn2_post_g"):
        w[n] = full[n]
    w["mla_out_g_pad"] = _pad_heads_cols(full["mla_out_g"], MLA_V)
    w["a_log_pad"] = _pad_lanes(full["gdn_a_log"], HEAD_LANES)
    w["dt_bias_pad"] = _pad_lanes(full["gdn_dt_bias"], HEAD_LANES)
    return w


FFN2_BIG = FFN_BIG[3:]


def _local_step(x, positions, loss_target, full, late=None):
    t, d = x.shape
    tb = min(512, t)
    tm = min(1024, t)
    tk = min(2048, t)
    w = _compute_weights(full)
    ffn = lambda tag: (w[tag + "_pre_g"], w[tag + "_w_gate"], w[tag + "_w_up"], w.get(tag + "_w_down"),
                       w[tag + "_post_g"])
    x1, sv1, w["ffn1_w_down"] = _ffn_fwd("ffn1", x, *ffn("ffn1"), tm,
                                         carried_up=_carried_gather([late[3]]) if late else None)
    x2, svm, gathered = _mixer_fwd(x1, positions, w, tb, _carried_gather(late[0][:2]) if late else None,
                                   _carried_gather(late[0][2:]) if late else None)
    for n, gw in zip(FFN2_BIG, gathered):
        w[n] = gw
    (dy, lsum), sv2, _ = _ffn_fwd("ffn2", x2, *ffn("ffn2"), tm, loss_target)
    g = {}
    dx2, g["ffn2_pre_g"], g["ffn2_w_gate"], g["ffn2_w_up"], g["ffn2_w_down"], g["ffn2_post_g"] = _ffn_bwd(
        "ffn2", dy, sv2, *ffn("ffn2"), tm, tk)[:6]

    def pair_sums(arrs, tag):
        got = _swap_halves(arrs, tag)
        return [_add_pair("add_pair%s_%d" % (tag, i), gi, gt, late[1]) for i, (gi, gt) in enumerate(zip(arrs, got))]

    def chip_sums(pairs, slabs, tag):
        return [_add_chips("add_chips%s_%d" % (tag, i), pr, sl, late[2]) for i, (pr, sl) in enumerate(zip(pairs, slabs))]

    if late:
        pairs2 = pair_sums([g[n] for n in FFN2_BIG], "_ffn2")
        dx1, gm, slabs2 = _mixer_bwd(dx2, svm, w, tb, _carried_scatter(pairs2))
        for n, hs in zip(FFN2_BIG, chip_sums(pairs2, slabs2, "_ffn2")):
            g[n] = hs
    else:
        dx1, gm, _ = _mixer_bwd(dx2, svm, w, tb)
    g["w_in"] = jnp.stack([jnp.pad(_win_cols_t(gm["w_in_pad_t"], q * W_IN_SHARD, (q + 1) * W_IN_SHARD),
                                   ((0, W_IN_SHARD_PAD - W_IN_SHARD), (0, 0))) for q in range(N_SHARD)])
    g["mla_w_uq"] = _unpad_heads_cols(gm["w_uq_pad"], MLA_NOPE + MLA_ROPE)
    g["mla_w_ukv"] = _wkv_from_pad(gm["w_kv_pad"])
    g["gdn_conv_w"] = gm["conv_w"]
    g["w_out"] = _wout_from_pad(gm["w_out_pad"])
    if late:
        quarters = [_pack([jnp.split(g[n], N_SHARD, axis=SHARD_AXIS[n])[q] for n in MIX_BIG], MM_DTYPE)
                    for q in range(N_SHARD)]
        pairs_m = pair_sums([g["w_in"].astype(MM_DTYPE), jnp.stack(quarters)], "_mix")
        pairs_d, pairs_gu = [], []

        def make_mid(dwd):
            pairs_d.extend(pair_sums([dwd], "_ffn1d"))
            return _carried_scatter(pairs_d)

        def make_up(dwg, dwu):
            pairs_gu.extend(pair_sums([dwg, dwu], "_ffn1"))
            return _carried_scatter(pairs_gu)

        dx0, g["ffn1_pre_g"], _, _, _, g["ffn1_post_g"], slabs_m, slabs_d, slabs_gu = _ffn_bwd(
            "ffn1", dx1, sv1, *ffn("ffn1"), tm, tk, _carried_scatter(pairs_m), make_mid, make_up)
        g["ffn1_w_gate"], g["ffn1_w_up"] = chip_sums(pairs_gu, slabs_gu, "_ffn1")
        g["ffn1_w_down"] = chip_sums(pairs_d, slabs_d, "_ffn1d")[0]
        g["w_in"], g["mix_pack"] = chip_sums(pairs_m, slabs_m, "_mix")
    else:
        dx0, g["ffn1_pre_g"], g["ffn1_w_gate"], g["ffn1_w_up"], g["ffn1_w_down"], g["ffn1_post_g"] = _ffn_bwd(
            "ffn1", dx1, sv1, *ffn("ffn1"), tm, tk)[:6]
    g["mix_pre_g"], g["mix_post_g"] = gm["mix_pre_g"], gm["mix_post_g"]
    g["mla_q_norm_g"], g["mla_kv_norm_g"] = gm["mla_q_norm_g"], gm["mla_kv_norm_g"]
    g["gdn_norm_g"] = gm["gdn_norm_g"]
    g["mla_out_g"] = _unpad_heads_cols(gm["mla_out_g_pad"], MLA_V)
    g["gdn_a_log"] = gm["a_log_pad"][:, :N_HEADS]
    g["gdn_dt_bias"] = gm["dt_bias_pad"][:, :N_HEADS]
    return lsum, dx0, g


HBM_SPEC = pl.BlockSpec(memory_space=pltpu.HBM)


def _place():
    return lax.axis_index("x"), lax.axis_index("y"), lax.axis_index("c")


def _exchange_call(name, body, ins, out_shapes, n_remote, n_local):
    return pl.pallas_call(
        body, name=name, in_specs=[HBM_SPEC] * len(ins), out_specs=[HBM_SPEC] * len(out_shapes), out_shape=out_shapes,
        scratch_shapes=[pltpu.SemaphoreType.DMA((n_remote,)), pltpu.SemaphoreType.DMA((n_remote,)),
                        pltpu.SemaphoreType.DMA((n_local,))])(*ins)


def _other_chips(x, y):
    return [(1 - x, y), (x, 1 - y), (1 - x, 1 - y)]


def _at_each_chip(fn):
    x, y, _ = _place()
    for cx in range(2):
        for cy in range(2):
            pl.when((x == cx) & (y == cy))(functools.partial(fn, cx, cy))


def _at_each_device(fn):
    x, y, c = _place()
    for cx in range(2):
        for cy in range(2):
            for cc in range(2):
                pl.when((x == cx) & (y == cy) & (c == cc))(functools.partial(fn, cx, cy, cc))


def _at_each_core(fn):
    c = lax.axis_index("c")
    for cc in range(2):
        pl.when(c == cc)(functools.partial(fn, cc))


def _gather_shards(ws):
    nw = len(ws)

    def body(*refs):
        w_refs, out_refs = refs[:nw], refs[nw:2 * nw]
        send_sems, recv_sems, local_sems = refs[2 * nw:]

        def run(x, y, c):
            chips = _other_chips(x, y)
            me, sibling = 2 * x + y, (x, y, 1 - c)

            def half(ref, which):
                hr = ref.shape[0] // 2
                return ref.at[pl.ds(which * hr, hr)]

            def over_ici(i, j, src, slab, to):
                return pltpu.make_async_remote_copy(
                    src_ref=half(src, c), dst_ref=half(out_refs[i].at[slab], c), send_sem=send_sems.at[7 * i + j],
                    recv_sem=recv_sems.at[7 * i + j], device_id=to, device_id_type=MESH)

            def over_d2d(i, j, slab, which):
                return pltpu.make_async_remote_copy(
                    src_ref=half(out_refs[i].at[slab], which), dst_ref=half(out_refs[i].at[slab], which),
                    send_sem=send_sems.at[7 * i + 3 + j], recv_sem=recv_sems.at[7 * i + 3 + j], device_id=sibling,
                    device_id_type=MESH)

            def own(i, w_ref):
                return pltpu.make_async_remote_copy(
                    src_ref=w_ref, dst_ref=out_refs[i].at[me], send_sem=send_sems.at[7 * i + 6],
                    recv_sem=recv_sems.at[7 * i + 6], device_id=sibling, device_id_type=MESH)

            sends, passed = [], []
            for i, w_ref in enumerate(w_refs):
                for j, (px, py) in enumerate(chips):
                    sends.append(over_ici(i, j, w_ref, me, (px, py, c)))
                    sends[-1].start()
            for i, w_ref in enumerate(w_refs):
                sends.append(own(i, w_ref))
                sends[-1].start()
            for i, w_ref in enumerate(w_refs):
                for j, (px, py) in enumerate(chips):
                    over_ici(i, j, w_ref, 2 * px + py, (px, py, c)).wait_recv()
                    passed.append(over_d2d(i, j, 2 * px + py, c))
                    passed[-1].start()
            for i, w_ref in enumerate(w_refs):
                own(i, w_ref).wait_recv()
                for j, (px, py) in enumerate(chips):
                    over_d2d(i, j, 2 * px + py, 1 - c).wait_recv()
            for cp in sends + passed:
                cp.wait_send()

        _at_each_device(run)

    outs = [jax.ShapeDtypeStruct((N_SHARD,) + w.shape, w.dtype) for w in ws]
    return _exchange_call("gather_weight_shards", body, ws, outs, 7 * nw, 1)


def _swap_halves(gs, tag=""):
    ng = len(gs)

    def body(*refs):
        g_refs, got_refs = refs[:ng], refs[ng:2 * ng]
        send_sems, recv_sems, _ = refs[2 * ng:]
        x, y, _ = _place()

        def run(c):
            sends = []
            for i, (g_ref, got_ref) in enumerate(zip(g_refs, got_refs)):
                hr = got_ref.shape[1]
                sends.append(pltpu.make_async_remote_copy(
                    src_ref=g_ref.at[:, pl.ds((1 - c) * hr, hr)], dst_ref=got_ref, send_sem=send_sems.at[i],
                    recv_sem=recv_sems.at[i], device_id=(x, y, 1 - c), device_id_type=MESH))
                sends[-1].start()
            for cp in sends:
                cp.wait()

        _at_each_core(run)

    halves = [jax.ShapeDtypeStruct((g.shape[0], g.shape[1] // 2, g.shape[2]), g.dtype) for g in gs]
    return _exchange_call("swap_grad_halves" + tag, body, gs, halves, ng, 1)


def _scatter_copies(p_refs, out_refs, send_sems, recv_sems, x, y):
    c = lax.axis_index("c")
    copies = []
    for i, (p_ref, out_ref) in enumerate(zip(p_refs, out_refs)):
        for j, (px, py) in enumerate(_other_chips(x, y)):
            copies.append(pltpu.make_async_remote_copy(
                src_ref=p_ref.at[2 * px + py], dst_ref=out_ref.at[j], send_sem=send_sems.at[3 * i + j],
                recv_sem=recv_sems.at[3 * i + j], device_id=(px, py, c), device_id_type=MESH))
    return copies


def _start_all(make, *refs):
    def run(x, y):
        for cp in make(*refs, x, y):
            cp.start()

    _at_each_chip(run)


def _wait_all(make, *refs):
    def run(x, y):
        copies = make(*refs, x, y)
        for cp in copies:
            cp.wait_recv()
        for cp in copies:
            cp.wait_send()

    _at_each_chip(run)


def _scatter_shapes(ps):
    return [jax.ShapeDtypeStruct((3,) + p.shape[1:], p.dtype) for p in ps]


def _carried_scatter(ps):
    return _Carried(ps, _scatter_shapes(ps), 3 * len(ps), functools.partial(_start_all, _scatter_copies),
                    functools.partial(_wait_all, _scatter_copies))


def _direct_gather_copies(w_refs, out_refs, send_sems, recv_sems, x, y, arriving):
    c = lax.axis_index("c")
    me = 2 * x + y
    peers = [((px, py, c), 2 * px + py) for px, py in _other_chips(x, y)] + [((x, y, 1 - c), me)]
    copies = []
    for i, (w_ref, out_ref) in enumerate(zip(w_refs, out_refs)):
        for j, (peer, slab) in enumerate(peers):
            copies.append(pltpu.make_async_remote_copy(
                src_ref=w_ref, dst_ref=out_ref.at[slab if arriving else me], send_sem=send_sems.at[4 * i + j],
                recv_sem=recv_sems.at[4 * i + j], device_id=peer, device_id_type=MESH))
    return copies


def _carried_gather(ws):
    def start(w_refs, out_refs, send_sems, recv_sems):
        def run(x, y):
            for cp in _direct_gather_copies(w_refs, out_refs, send_sems, recv_sems, x, y, False):
                cp.start()

        _at_each_chip(run)

    def finish(w_refs, out_refs, send_sems, recv_sems):
        def run(x, y):
            for cp in _direct_gather_copies(w_refs, out_refs, send_sems, recv_sems, x, y, True):
                cp.wait_recv()
            for cp in _direct_gather_copies(w_refs, out_refs, send_sems, recv_sems, x, y, False):
                cp.wait_send()

        _at_each_chip(run)

    outs = [jax.ShapeDtypeStruct((N_SHARD,) + w.shape, w.dtype) for w in ws]
    return _Carried(ws, outs, 4 * len(ws), start, finish)


def _share_halves(hs):
    n = len(hs)

    def body(*refs):
        h_refs, out_refs = refs[:n], refs[n:2 * n]
        send_sems, recv_sems, _ = refs[2 * n:]
        x, y, c = _place()
        sends = []
        for i, (h_ref, out_ref) in enumerate(zip(h_refs, out_refs)):
            sends.append(pltpu.make_async_remote_copy(
                src_ref=h_ref, dst_ref=out_ref, send_sem=send_sems.at[i], recv_sem=recv_sems.at[i],
                device_id=(x, y, 1 - c), device_id_type=MESH))
            sends[-1].start()
        for cp in sends:
            cp.wait()

    outs = [jax.ShapeDtypeStruct(h.shape, h.dtype) for h in hs]
    return _exchange_call("share_grad_halves", body, hs, outs, n, 1)


def _scalar_grid_call(name, body, scalars, grid, in_specs, out_specs, out_shape, args):
    grid_spec = pltpu.PrefetchScalarGridSpec(num_scalar_prefetch=len(scalars), grid=grid, in_specs=in_specs,
                                             out_specs=out_specs)
    return pl.pallas_call(body, name=name, grid_spec=grid_spec, out_shape=out_shape,
                          compiler_params=_params(("arbitrary",) * len(grid)))(*scalars, *args)


def _add_pair(name, g, got, core):
    ns_, hr, cols = got.shape
    th = _row_tile(hr, 512)
    nb = hr // th

    def body(core_ref, g_ref, got_ref, out_ref):
        out_ref[...] = (g_ref[...].astype(F32) + got_ref[...].astype(F32)).astype(out_ref.dtype)

    blk = pl.BlockSpec((1, th, cols), lambda q, j, core_ref: (q, j, 0))
    own = pl.BlockSpec((1, th, cols), lambda q, j, core_ref: (q, core_ref[0] * nb + j, 0))
    return _scalar_grid_call(name, body, [core], (ns_, nb), [own, blk], blk,
                             jax.ShapeDtypeStruct(got.shape, got.dtype), [g, got])


def _add_chips(name, pairs, slabs, chip):
    _, hr, cols = slabs.shape
    th = _row_tile(hr, 512)

    def body(chip_ref, own_ref, s0_ref, s1_ref, s2_ref, out_ref):
        total = own_ref[0].astype(F32) + s0_ref[0].astype(F32)
        out_ref[...] = (total + s1_ref[0].astype(F32)) + s2_ref[0].astype(F32)

    own = pl.BlockSpec((1, th, cols), lambda j, chip_ref: (chip_ref[0], j, 0))
    others = [pl.BlockSpec((1, th, cols), lambda j, chip_ref, k=k: (k, j, 0)) for k in range(3)]
    return _scalar_grid_call(name, body, [chip], (hr // th,), [own] + others,
                             pl.BlockSpec((th, cols), lambda j, chip_ref: (j, 0)),
                             jax.ShapeDtypeStruct((hr, cols), F32), [pairs, slabs, slabs, slabs])


def _join_halves(name, mine, other, core):
    hr, cols = mine.shape
    th = _row_tile(hr, 512)
    nb = hr // th

    def body(core_ref, mine_ref, other_ref, out_ref):
        is_mine = pl.program_id(0) == core_ref[0]

        @pl.when(is_mine)
        def _():
            out_ref[0] = mine_ref[...]

        @pl.when(jnp.logical_not(is_mine))
        def _():
            out_ref[0] = other_ref[...]

    blk = pl.BlockSpec((th, cols), lambda h, j, core_ref: (j, 0))
    return _scalar_grid_call(name, body, [core], (2, nb), [blk, blk],
                             pl.BlockSpec((1, th, cols), lambda h, j, core_ref: (0, h * nb + j, 0)),
                             jax.ShapeDtypeStruct((1, 2 * hr, cols), mine.dtype), [mine, other])


def _gather_small(sp):
    def body(s_ref, out_ref, send_sems, recv_sems, local_sem):
        x, y, c = _place()
        me = 4 * x + 2 * y + c
        peers = [(x ^ (m >> 2), y ^ ((m >> 1) & 1), c ^ (m & 1)) for m in range(1, 8)]
        mine = pltpu.make_async_copy(s_ref, out_ref.at[me], local_sem)
        mine.start()
        sends = [pltpu.make_async_remote_copy(src_ref=s_ref, dst_ref=out_ref.at[me], send_sem=send_sems.at[j],
                                              recv_sem=recv_sems.at[j], device_id=p, device_id_type=MESH)
                 for j, p in enumerate(peers)]
        for cp in sends:
            cp.start()
        for j, (px, py, pc) in enumerate(peers):
            pltpu.make_async_remote_copy(src_ref=s_ref, dst_ref=out_ref.at[4 * px + 2 * py + pc],
                                         send_sem=send_sems.at[j], recv_sem=recv_sems.at[j], device_id=(px, py, pc),
                                         device_id_type=MESH).wait_recv()
        for cp in sends:
            cp.wait_send()
        mine.wait()

    return pl.pallas_call(
        body, name="gather_small_grads", in_specs=[HBM_SPEC], out_specs=HBM_SPEC,
        out_shape=jax.ShapeDtypeStruct((8,) + sp.shape, sp.dtype),
        scratch_shapes=[pltpu.SemaphoreType.DMA((7,)), pltpu.SemaphoreType.DMA((7,)), pltpu.SemaphoreType.DMA])(sp)


def _pack_rows(total):
    rows = -(-total // LANES)
    return -(-rows // 32) * 32


def _pack(arrs, dtype):
    flat = jnp.concatenate([a.reshape(-1).astype(dtype) for a in arrs])
    rows = _pack_rows(flat.shape[0])
    return jnp.pad(flat, (0, rows * LANES - flat.shape[0])).reshape(rows, LANES)


def _unpack(buf, shapes):
    flat = buf.reshape(-1)
    out, off = {}, 0
    for n, shp in shapes:
        size = shp[0] * shp[1]
        out[n] = flat[off:off + size].reshape(shp)
        off += size
    return out


def _to_wire(name, w3):
    _, r, cols = w3.shape
    tb = _row_tile(r, 512)

    def body(w_ref, o_ref):
        o_ref[...] = w_ref[0].astype(o_ref.dtype)

    return pl.pallas_call(
        body, name=name, grid=(r // tb,), in_specs=[pl.BlockSpec((1, tb, cols), lambda i: (0, i, 0))],
        out_specs=pl.BlockSpec((tb, cols), lambda i: (i, 0)), out_shape=jax.ShapeDtypeStruct((r, cols), MM_DTYPE),
        compiler_params=_params(("arbitrary",)))(w3)


def _adamw(name, w3, g, m3, v3, tb):
    c1 = 1.0 - ADAM_B1 ** ADAM_STEP
    c2 = 1.0 - ADAM_B2 ** ADAM_STEP
    _, r, cols = w3.shape
    emit = g.ndim == 2
    blk3 = pl.BlockSpec((1, tb, cols), lambda i: (0, i, 0))
    g_spec = pl.BlockSpec((tb, cols), lambda i: (i, 0)) if emit else blk3

    def body(w_ref, g_ref, m_ref, v_ref, *out_refs):
        gb = g_ref[...] if emit else g_ref[0]
        m2 = ADAM_B1 * m_ref[0] + (1.0 - ADAM_B1) * gb
        v2 = ADAM_B2 * v_ref[0] + (1.0 - ADAM_B2) * (gb * gb)
        out_refs[-3][0] = -ADAM_LR * ((m2 / c1) / (jnp.sqrt(v2 / c2) + ADAM_EPS) + ADAM_WD * w_ref[0])
        out_refs[-2][0] = m2
        out_refs[-1][0] = v2
        if emit:
            out_refs[0][0] = gb

    n_out = 4 if emit else 3
    outs = pl.pallas_call(
        body, name=name, grid=(r // tb,), in_specs=[blk3, g_spec, blk3, blk3], out_specs=[blk3] * n_out,
        out_shape=[jax.ShapeDtypeStruct((1, r, cols), F32)] * n_out,
        compiler_params=_params(("arbitrary",)))(w3, g, m3, v3)
    return outs if emit else [g] + list(outs)


def _row_tile(rows, pref):
    if rows <= pref:
        return rows
    t = pref
    while t >= 8:
        if rows % t == 0 and t % 8 == 0:
            return t
        t -= 8
    return rows


def kernel(x, positions, ffn1_pre_g, ffn1_w_gate, ffn1_w_up, ffn1_w_down, ffn1_post_g, mix_pre_g, w_in, mla_q_norm_g, mla_w_uq, mla_kv_norm_g, mla_w_ukv, mla_out_g, gdn_conv_w, gdn_a_log, gdn_dt_bias, gdn_norm_g, w_out, mix_post_g, ffn2_pre_g, ffn2_w_gate, ffn2_w_up, ffn2_w_down, ffn2_post_g, loss_target, m_ffn1_pre_g, m_ffn1_w_gate, m_ffn1_w_up, m_ffn1_w_down, m_ffn1_post_g, m_mix_pre_g, m_w_in, m_mla_q_norm_g, m_mla_w_uq, m_mla_kv_norm_g, m_mla_w_ukv, m_mla_out_g, m_gdn_conv_w, m_gdn_a_log, m_gdn_dt_bias, m_gdn_norm_g, m_w_out, m_mix_post_g, m_ffn2_pre_g, m_ffn2_w_gate, m_ffn2_w_up, m_ffn2_w_down, m_ffn2_post_g, v_ffn1_pre_g, v_ffn1_w_gate, v_ffn1_w_up, v_ffn1_w_down, v_ffn1_post_g, v_mix_pre_g, v_w_in, v_mla_q_norm_g, v_mla_w_uq, v_mla_kv_norm_g, v_mla_w_ukv, v_mla_out_g, v_gdn_conv_w, v_gdn_a_log, v_gdn_dt_bias, v_gdn_norm_g, v_w_out, v_mix_post_g, v_ffn2_pre_g, v_ffn2_w_gate, v_ffn2_w_up, v_ffn2_w_down, v_ffn2_post_g):
    args = dict(locals())
    wsh = {n: args[n][0] for n in WEIGHTS}
    msh = {n: args["m_" + n] for n in SMALL}
    vsh = {n: args["v_" + n] for n in SMALL}
    for n in SMALL:
        wsh[n] = args[n]
    mix_shapes = [(n, wsh[n].shape) for n in MIX_BIG]

    early = FFN_BIG[:2]
    held = lambda a, n: jnp.swapaxes(a, 1, 2) if n in TRANSPOSED else a
    w_in_wire = jnp.pad(held(w_in, "w_in")[0].astype(MM_DTYPE), ((0, W_IN_SHARD_PAD - W_IN_SHARD), (0, 0)))
    gathered = _gather_shards([_to_wire("wire_" + n, held(args[n], n)) for n in early]
                              + [w_in_wire, _pack([wsh[n] for n in MIX_BIG], MM_DTYPE)])
    full = {n: wsh[n] for n in SMALL}
    for n, gw in zip(early + ["w_in"], gathered):
        full[n] = gw
    parts = [_unpack(gathered[-1][q], mix_shapes) for q in range(N_SHARD)]
    for n in MIX_BIG:
        full[n] = jnp.concatenate([parts[q][n] for q in range(N_SHARD)], axis=SHARD_AXIS[n])

    core = lax.axis_index("c").astype(jnp.int32).reshape(1)
    chip = (2 * lax.axis_index("x") + lax.axis_index("y")).astype(jnp.int32).reshape(1)
    late = ([_to_wire("wire_" + n, held(args[n], n)) for n in FFN2_BIG], core, chip,
            _to_wire("wire_ffn1_w_down", ffn1_w_down))
    lsum, grad_x, g = _local_step(x[0], positions, loss_target[0], full, late)
    loss = lax.psum(0.5 * jnp.sum(lsum) / x.shape[-1], ("x", "y", "c"))

    halves = [g[n] for n in FFN_BIG] + [g["w_in"], g["mix_pack"]]
    others = _share_halves(halves)
    shared = [_join_halves("join_halves_%d" % i, hm, ho, core) for i, (hm, ho) in enumerate(zip(halves, others))]
    gsh = _unpack(shared[-1], mix_shapes)
    for n, sg_ in zip(FFN_BIG, shared):
        gsh[n] = sg_
    gsh["w_in"] = shared[-2][:, :W_IN_SHARD]

    small_shapes = [(n, wsh[n].shape) for n in SMALL]
    pack_small = lambda d: jnp.concatenate(
        [_pad_lanes(d[n].astype(F32), LANES) for n in SMALL] + [jnp.zeros((SMALL_ROWS - len(SMALL), LANES), F32)], axis=0)
    slots = _gather_small(pack_small(g))

    c1 = 1.0 - ADAM_B1 ** ADAM_STEP
    c2 = 1.0 - ADAM_B2 ** ADAM_STEP

    def small_update(wb, mb, vb, s8):
        gs = s8[0:SMALL_ROWS]
        for d in range(1, 8):
            gs = gs + s8[d * SMALL_ROWS:(d + 1) * SMALL_ROWS]
        m2 = ADAM_B1 * mb + (1.0 - ADAM_B1) * gs
        v2 = ADAM_B2 * vb + (1.0 - ADAM_B2) * (gs * gs)
        delta = -ADAM_LR * ((m2 / c1) / (jnp.sqrt(v2 / c2) + ADAM_EPS) + ADAM_WD * wb)
        return gs, delta, m2, v2

    sg, sd, sm, sv_ = _rowwise("adamw_small", small_update,
                               [pack_small(wsh), pack_small(msh), pack_small(vsh)],
                               [slots.reshape(8 * SMALL_ROWS, LANES)], [(LANES, F32)] * 4, [], SMALL_ROWS)
    grads, deltas, new_m, new_v = {}, {}, {}, {}
    for i, (n, shp) in enumerate(small_shapes):
        grads[n], deltas[n] = sg[i:i + 1, :shp[1]], sd[i:i + 1, :shp[1]]
        new_m[n], new_v[n] = sm[i:i + 1, :shp[1]], sv_[i:i + 1, :shp[1]]
    for n in BIG:
        w3 = held(args[n], n)
        outs = _adamw("adamw_" + n, w3, gsh[n], held(args["m_" + n], n), held(args["v_" + n], n),
                      _row_tile(w3.shape[1], 256))
        grads[n], deltas[n], new_m[n], new_v[n] = [held(o, n) for o in outs]

    return (loss, grad_x[None], *[grads[n] for n in WEIGHTS], *[deltas[n] for n in WEIGHTS],
            *[new_m[n] for n in WEIGHTS], *[new_v[n] for n in WEIGHTS])
```

```python
import functools

import jax
import jax.numpy as jnp
from jax import lax
from jax.experimental import pallas as pl
from jax.experimental.pallas import tpu as pltpu

F32 = jnp.float32
BF16 = jnp.bfloat16
MM_DTYPE = BF16
MESH = pl.DeviceIdType.MESH

D_MODEL = 1024
D_FF = 2816
N_HEADS = 8
MLA_Q_RANK = 256
MLA_KV_RANK = 128
MLA_NOPE = 64
MLA_ROPE = 32
MLA_V = 64
ROPE_THETA = 10000.0
GDN_DH = 64
GDN_W = N_HEADS * GDN_DH
GDN_CONV = 4
CHUNK = 64
GDN_STEP_CHUNKS = 4
HEAD_LANES = 128
HEADS_PER_STEP = 4
MLA_PAD = N_HEADS * HEAD_LANES
EPS = 1e-6
N_SHARD = 4
LANES = 1024

PIN_QKV = 0
PIN_MLA = 1536
PIN_KPE = 1920
PIN_GATE = 2048
PIN_AB = 2560
PIN_W = 2688
CAT_W = MLA_PAD + GDN_W

ADAM_LR = 0.001
ADAM_B1 = 0.9
ADAM_B2 = 0.999
ADAM_EPS = 1e-08
ADAM_WD = 0.01
ADAM_STEP = 10

VMEM_LIMIT_V7X = 56 * 1024 * 1024

BIG = ["ffn1_w_gate", "ffn1_w_up", "ffn1_w_down", "w_in", "mla_w_uq", "mla_w_ukv", "gdn_conv_w", "w_out",
       "ffn2_w_gate", "ffn2_w_up", "ffn2_w_down"]
FFN_BIG = ["ffn1_w_gate", "ffn1_w_up", "ffn1_w_down", "ffn2_w_gate", "ffn2_w_up", "ffn2_w_down"]
TRANSPOSED = ["ffn1_w_gate", "ffn1_w_up", "ffn2_w_gate", "ffn2_w_up", "w_in"]
MIX_BIG = ["mla_w_uq", "mla_w_ukv", "gdn_conv_w", "w_out"]
SMALL = ["ffn1_pre_g", "ffn1_post_g", "mix_pre_g", "mla_q_norm_g", "mla_kv_norm_g", "mla_out_g", "gdn_a_log",
         "gdn_dt_bias", "gdn_norm_g", "mix_post_g", "ffn2_pre_g", "ffn2_post_g"]
WEIGHTS = ["ffn1_pre_g", "ffn1_w_gate", "ffn1_w_up", "ffn1_w_down", "ffn1_post_g", "mix_pre_g", "w_in",
           "mla_q_norm_g", "mla_w_uq", "mla_kv_norm_g", "mla_w_ukv", "mla_out_g", "gdn_conv_w", "gdn_a_log",
           "gdn_dt_bias", "gdn_norm_g", "w_out", "mix_post_g", "ffn2_pre_g", "ffn2_w_gate", "ffn2_w_up",
           "ffn2_w_down", "ffn2_post_g"]
SHARD_AXIS = {"ffn1_w_gate": 1, "ffn1_w_up": 1, "ffn1_w_down": 0, "w_in": 1, "mla_w_uq": 1, "mla_w_ukv": 1,
              "gdn_conv_w": 1, "w_out": 0, "ffn2_w_gate": 1, "ffn2_w_up": 1, "ffn2_w_down": 0}
SMALL_ROWS = 16


def _params(sem):
    return pltpu.CompilerParams(dimension_semantics=sem, vmem_limit_bytes=VMEM_LIMIT_V7X)


def _pick(dim, pref):
    if dim <= pref:
        return dim
    t = (pref // 128) * 128
    while t >= 128:
        if dim % t == 0:
            return t
        t -= 128
    return dim


ANY_SPEC = pl.BlockSpec(memory_space=pl.ANY)


def _rowwise(name, fn, row_ins, bc_ins, row_outs, acc_outs, tb, wide=None, carry=None):
    ents = []
    for e in row_ins:
        ents.append(e if isinstance(e, tuple) else (e, e.shape[1], 0, 0))
    over = [o[2] for o in row_outs if len(o) == 3]
    rows = over[0] if over else ents[0][0].shape[0]
    steps = rows // tb
    assert steps * tb == rows, (name, rows, tb)
    in_specs, args = [], []
    for a, w, j, r0 in ents:
        in_specs.append(pl.BlockSpec((tb, w), lambda i, j=j, r0=r0: (i + r0, j)))
        args.append(a)
    for b in bc_ins:
        in_specs.append(pl.BlockSpec(b.shape, lambda i: (0, 0)))
        args.append(b)
    n_in = len(args)
    aliases = {}
    if carry is not None:
        in_specs.append(ANY_SPEC)
        args.append(carry)
        aliases = {n_in: 0}
    out_shape = [jax.ShapeDtypeStruct((rows, o[0]), o[1]) for o in row_outs]
    out_specs = [pl.BlockSpec((tb, o[0]), lambda i: (i, 0)) for o in row_outs]
    if wide is not None:
        out_shape[0] = jax.ShapeDtypeStruct((rows, wide[0]), row_outs[0][1])
        out_specs[0] = pl.BlockSpec((tb, row_outs[0][0]), lambda i: (i, wide[1]))
    out_shape += [jax.ShapeDtypeStruct((r, c), F32) for r, c in acc_outs]
    out_specs += [pl.BlockSpec((r, c), lambda i: (0, 0)) for r, c in acc_outs]
    n_ro, n_acc, n_args = len(row_outs), len(acc_outs), len(args)

    def body(*refs):
        vals = fn(*[r[...] for r in refs[:n_in]])
        if not isinstance(vals, (tuple, list)):
            vals = (vals,)
        for r, v in zip(refs[n_args:n_args + n_ro], vals[:n_ro]):
            r[...] = v.astype(r.dtype)
        if n_acc:
            acc_refs = refs[n_args + n_ro:]

            @pl.when(pl.program_id(0) == 0)
            def _():
                for r in acc_refs:
                    r[...] = jnp.zeros(r.shape, r.dtype)

            for r, v in zip(acc_refs, vals[n_ro:]):
                r[...] += v

    outs = pl.pallas_call(body, name=name, grid=(steps,), in_specs=in_specs, out_specs=out_specs,
                          out_shape=out_shape, input_output_aliases=aliases,
                          compiler_params=_params(("arbitrary",)))(*args)
    return outs


def _mm(name, a, b, mode, out_dtype, tm=1024, tn=1024, tk=1024):
    if mode == "nn":
        (m, k), (k2, n) = a.shape, b.shape
    elif mode == "nt":
        (m, k), (n, k2) = a.shape, b.shape
    else:
        (k, m), (k2, n) = a.shape, b.shape
    assert k == k2, (name, a.shape, b.shape)
    tm, tn, tk = _pick(m, tm), _pick(n, tn), _pick(k, tk)
    nk = k // tk
    if mode == "nn":
        a_spec = pl.BlockSpec((tm, tk), lambda i, j, kk: (i, kk))
        b_spec = pl.BlockSpec((tk, tn), lambda i, j, kk: (kk, j))
        dims = (((1,), (0,)), ((), ()))
    elif mode == "nt":
        a_spec = pl.BlockSpec((tm, tk), lambda i, j, kk: (i, kk))
        b_spec = pl.BlockSpec((tn, tk), lambda i, j, kk: (j, kk))
        dims = (((1,), (1,)), ((), ()))
    else:
        a_spec = pl.BlockSpec((tk, tm), lambda i, j, kk: (kk, i))
        b_spec = pl.BlockSpec((tk, tn), lambda i, j, kk: (kk, j))
        dims = (((0,), (0,)), ((), ()))

    def body(a_ref, b_ref, o_ref, acc_ref):
        kk = pl.program_id(2)

        @pl.when(kk == 0)
        def _():
            acc_ref[...] = jnp.zeros(acc_ref.shape, F32)

        acc_ref[...] += lax.dot_general(a_ref[...].astype(MM_DTYPE), b_ref[...].astype(MM_DTYPE), dims,
                                        preferred_element_type=F32)

        @pl.when(kk == nk - 1)
        def _():
            o_ref[...] = acc_ref[...].astype(o_ref.dtype)

    return pl.pallas_call(
        body, name=name, grid=(m // tm, n // tn, nk), in_specs=[a_spec, b_spec],
        out_specs=pl.BlockSpec((tm, tn), lambda i, j, kk: (i, j)),
        out_shape=jax.ShapeDtypeStruct((m, n), out_dtype),
        scratch_shapes=[pltpu.VMEM((tm, tn), F32)],
        compiler_params=_params(("parallel", "parallel", "arbitrary")))(a, b)


def _prologue_mm(name, row_ins, g, prologue, w, with_gain_grad, tm=1024, tn=1024):
    t, d = row_ins[0].shape
    n = w.shape[0]
    tm, tn = _pick(t, tm), _pick(n, tn)
    n_row = len(row_ins)
    row = pl.BlockSpec((tm, d), lambda i, j: (i, 0))
    vec = pl.BlockSpec((1, d), lambda i, j: (0, 0))

    def body(*refs):
        rows, g_ref, w_ref = refs[:n_row], refs[n_row], refs[n_row + 1]
        lhs_ref, out_ref = refs[n_row + 2], refs[n_row + 3]
        lhs_s = refs[-1]
        i, j = pl.program_id(0), pl.program_id(1)
        if with_gain_grad:
            dg_ref = refs[n_row + 4]

            @pl.when((i == 0) & (j == 0))
            def _():
                dg_ref[...] = jnp.zeros(dg_ref.shape, F32)

        @pl.when(j == 0)
        def _():
            res = prologue(*[r[...] for r in rows], g_ref[...])
            lhs_s[...] = res[0].astype(MM_DTYPE)
            lhs_ref[...] = lhs_s[...]
            if with_gain_grad:
                dg_ref[...] += res[1]

        out_ref[...] = lax.dot_general(lhs_s[...], w_ref[...], (((1,), (1,)), ((), ())), preferred_element_type=F32)

    out_specs = [row, pl.BlockSpec((tm, tn), lambda i, j: (i, j))] + ([vec] if with_gain_grad else [])
    out_shape = [jax.ShapeDtypeStruct((t, d), MM_DTYPE), jax.ShapeDtypeStruct((t, n), F32)]
    out_shape += [jax.ShapeDtypeStruct((1, d), F32)] if with_gain_grad else []
    return pl.pallas_call(
        body, name=name, grid=(t // tm, n // tn),
        in_specs=[row] * n_row + [vec, pl.BlockSpec((tn, d), lambda i, j: (j, 0))], out_specs=out_specs,
        out_shape=out_shape, scratch_shapes=[pltpu.VMEM((tm, d), MM_DTYPE)],
        compiler_params=_params(("arbitrary", "arbitrary")))(*row_ins, g, w)


def _mm_epilogue(name, a, w, row_ins, g, epilogue, n_row_out, with_gain_grad, tm=1024, tk=1024):
    t, k = a.shape
    d = w.shape[1]
    tm, tk = _pick(t, tm), _pick(k, tk)
    nk = k // tk
    n_row = len(row_ins)
    row = pl.BlockSpec((tm, d), lambda i, kk: (i, 0))
    vec = pl.BlockSpec((1, d), lambda i, kk: (0, 0))

    def body(*refs):
        a_ref, w_ref = refs[0], refs[1]
        rows, g_ref = refs[2:2 + n_row], refs[2 + n_row]
        outs = refs[3 + n_row:3 + n_row + n_row_out]
        acc = refs[-1]
        i, kk = pl.program_id(0), pl.program_id(1)
        if with_gain_grad:
            dg_ref = refs[3 + n_row + n_row_out]

            @pl.when((i == 0) & (kk == 0))
            def _():
                dg_ref[...] = jnp.zeros(dg_ref.shape, F32)

        @pl.when(kk == 0)
        def _():
            acc[...] = jnp.zeros(acc.shape, F32)

        acc[...] += jnp.dot(a_ref[...], w_ref[...], preferred_element_type=F32)

        @pl.when(kk == nk - 1)
        def _():
            res = epilogue(acc[...], *[r[...] for r in rows], g_ref[...])
            for o_ref, val in zip(outs, res[:n_row_out]):
                o_ref[...] = val
            if with_gain_grad:
                dg_ref[...] += res[n_row_out]

    out_specs = [row] * n_row_out + ([vec] if with_gain_grad else [])
    out_shape = [jax.ShapeDtypeStruct((t, d), F32)] * n_row_out
    out_shape += [jax.ShapeDtypeStruct((1, d), F32)] if with_gain_grad else []
    return pl.pallas_call(
        body, name=name, grid=(t // tm, nk),
        in_specs=[pl.BlockSpec((tm, tk), lambda i, kk: (i, kk)), pl.BlockSpec((tk, d), lambda i, kk: (kk, 0))]
        + [row] * n_row + [vec],
        out_specs=out_specs, out_shape=out_shape, scratch_shapes=[pltpu.VMEM((tm, d), F32)],
        compiler_params=_params(("arbitrary", "arbitrary")))(a, w, *row_ins, g)


def _rms_stats(x, n_real=None):
    n = x.shape[-1] if n_real is None else n_real
    return lax.rsqrt(jnp.sum(x * x, axis=-1, keepdims=True) / n + EPS)


def _rms_bwd(x, r, g, dz, n_real=None):
    n = x.shape[-1] if n_real is None else n_real
    xh = x * r
    dxh = dz * g
    dx = r * (dxh - xh * (jnp.sum(dxh * xh, axis=-1, keepdims=True) / n))
    return dx, jnp.sum(dz * xh, axis=0, keepdims=True)


def _sigmoid(x):
    return 0.5 * jnp.tanh(0.5 * x) + 0.5


def _roll(x, s, axis):
    return pltpu.roll(x, s, axis)


def _rope(x, c, s1, s2):
    return x * c + _roll(x, HEAD_LANES - MLA_ROPE // 2, 1) * s1 + _roll(x, MLA_ROPE // 2, 1) * s2


def _heads_apply(x, fn):
    return jnp.concatenate([fn(x[:, h * HEAD_LANES:(h + 1) * HEAD_LANES]) for h in range(N_HEADS)], axis=1)


ROW_CHUNK = 256


def _row_chunks(rows):
    step = min(ROW_CHUNK, rows)
    return [pl.ds(r, step) for r in range(0, rows, step)]


def _ffn_fwd(tag, x, g_pre, wg, wu, wd, g_post, tm, loss_target=None, carried_up=None):
    t, d = x.shape
    ns, fs, _ = wg.shape
    nt = t // tm
    row = pl.BlockSpec((tm, d), lambda i, q: (i, 0))
    vec = pl.BlockSpec((1, d), lambda i, q: (0, 0))
    act3 = pl.BlockSpec((1, tm, fs), lambda i, q: (q, i, 0))
    wrow = pl.BlockSpec((1, fs, d), lambda i, q: (q, 0, 0))
    nt_dims = (((1,), (1,)), ((), ()))

    def gate_up(x_ref, g_ref, wg_ref, wu_ref, n_ref, sl_ref, ud_ref, s_ref, n_s):
        @pl.when(pl.program_id(1) == 0)
        def _():
            for r in _row_chunks(tm):
                xb = x_ref[r, :]
                n_s[r, :] = (xb * _rms_stats(xb) * g_ref[...]).astype(MM_DTYPE)
            n_ref[...] = n_s[...]

        for r in _row_chunks(tm):
            n = n_s[r, :]
            a = lax.dot_general(n, wg_ref[0], nt_dims, preferred_element_type=F32)
            u = lax.dot_general(n, wu_ref[0], nt_dims, preferred_element_type=F32)
            sg = _sigmoid(a)
            sl = a * sg
            sl_ref[0, r, :] = sl.astype(sl_ref.dtype)
            ud_ref[0, r, :] = (u * (sg + sl * (1.0 - sg))).astype(ud_ref.dtype)
            s_ref[0, r, :] = (sl * u).astype(s_ref.dtype)

    gate_up, x_in, x_out, x_shape, x_scr, x_args = _carry(gate_up, 4, 4, (nt, ns), carried_up)
    n, sl, ud, s, *from_up = pl.pallas_call(
        gate_up, name=tag + "_gate_up", grid=(nt, ns), in_specs=[row, vec, wrow, wrow] + x_in,
        out_specs=[row, act3, act3, act3] + x_out,
        out_shape=[jax.ShapeDtypeStruct((t, d), MM_DTYPE)] + [jax.ShapeDtypeStruct((ns, t, fs), MM_DTYPE)] * 3 + x_shape,
        scratch_shapes=[pltpu.VMEM((tm, d), MM_DTYPE)] + x_scr,
        compiler_params=_params(("arbitrary", "arbitrary")))(x, g_pre, wg, wu, *x_args)
    if wd is None:
        wd = from_up[0]

    def down(s_ref, wd_ref, x_ref, g_ref, *rest):
        if loss_target is not None:
            tgt_ref, h_ref, y_ref, ls_ref, acc = rest
        else:
            h_ref, y_ref, acc = rest
        i, q = pl.program_id(0), pl.program_id(1)

        @pl.when(q == 0)
        def _():
            acc[...] = jnp.zeros(acc.shape, F32)

        if loss_target is not None:
            @pl.when((i == 0) & (q == 0))
            def _():
                ls_ref[...] = jnp.zeros(ls_ref.shape, F32)

        for r in _row_chunks(tm):
            acc[r, :] += jnp.dot(s_ref[0, r, :], wd_ref[0], preferred_element_type=F32)

        @pl.when(q == ns - 1)
        def _():
            for r in _row_chunks(tm):
                hb = acc[r, :]
                h_ref[r, :] = hb
                yb = x_ref[r, :] + 0.5 * (hb * _rms_stats(hb) * g_ref[...])
                if loss_target is None:
                    y_ref[r, :] = yb
                else:
                    e = yb - tgt_ref[r, :]
                    y_ref[r, :] = e * (1.0 / d)
                    ls_ref[...] += jnp.sum(e * e, axis=0, keepdims=True)

    with_loss = loss_target is not None
    outs = pl.pallas_call(
        down, name=tag + "_down", grid=(nt, ns), in_specs=[act3, wrow, row, vec] + ([row] if with_loss else []),
        out_specs=[row, row] + ([vec] if with_loss else []),
        out_shape=[jax.ShapeDtypeStruct((t, d), F32)] * 2 + ([jax.ShapeDtypeStruct((1, d), F32)] if with_loss else []),
        scratch_shapes=[pltpu.VMEM((tm, d), F32)],
        compiler_params=_params(("arbitrary", "arbitrary")))(s, wd, x, g_post, *([loss_target] if with_loss else []))
    if with_loss:
        h, dy, lsum = outs
        return (dy, lsum), (x, n, sl, ud, s, h), wd
    h, y = outs
    return y, (x, n, sl, ud, s, h), wd


def _carry(body, n_in, n_out, grid, carried):
    if carried is None:
        return body, [], [], [], [], []
    nx_in, nx_out = len(carried.ins), len(carried.outs)

    def wrapped(*refs):
        ins, rest = refs[:n_in], refs[n_in:]
        xi, rest = rest[:nx_in], rest[nx_in:]
        outs, rest = rest[:n_out], rest[n_out:]
        xo, rest = rest[:nx_out], rest[nx_out:]
        scr, sems = rest[:len(rest) - 2], rest[len(rest) - 2:]
        first, last = True, True
        for dim, size in enumerate(grid):
            first = first & (pl.program_id(dim) == 0)
            last = last & (pl.program_id(dim) == size - 1)

        @pl.when(first)
        def _():
            carried.start(xi, xo, *sems)

        body(*ins, *outs, *scr)

        @pl.when(last)
        def _():
            carried.finish(xi, xo, *sems)

    sems = [pltpu.SemaphoreType.DMA((carried.n_sem,)), pltpu.SemaphoreType.DMA((carried.n_sem,))]
    return (wrapped, [HBM_SPEC] * nx_in, [HBM_SPEC] * nx_out, list(carried.outs), sems, list(carried.ins))


def _ffn_bwd(tag, dy, saved, g_pre, wg, wu, wd, g_post, tm, tk, carried_down=None, make_carried_mid=None,
             make_carried_up=None):
    x, n, sl, ud, s, h = saved
    t, d = x.shape
    ns, fs, _ = wg.shape
    nt, nk = t // tm, t // tk
    row = pl.BlockSpec((tm, d), lambda i, q: (i, 0))
    vec = pl.BlockSpec((1, d), lambda i, q: (0, 0))
    act3 = pl.BlockSpec((1, tm, fs), lambda i, q: (q, i, 0))
    wrow = pl.BlockSpec((1, fs, d), lambda i, q: (q, 0, 0))
    nt_dims = (((1,), (1,)), ((), ()))
    tn_dims = (((0,), (0,)), ((), ()))

    def down_b(h_ref, dy_ref, g_ref, wd_ref, sl_ref, ud_ref, dh_ref, da_ref, du_ref, dg_ref, dh_s):
        i, q = pl.program_id(0), pl.program_id(1)

        @pl.when((i == 0) & (q == 0))
        def _():
            dg_ref[...] = jnp.zeros(dg_ref.shape, F32)

        @pl.when(q == 0)
        def _():
            for r in _row_chunks(tm):
                hb = h_ref[r, :]
                dh, dg = _rms_bwd(hb, _rms_stats(hb), g_ref[...], 0.5 * dy_ref[r, :])
                dh_s[r, :] = dh.astype(MM_DTYPE)
                dg_ref[...] += dg
            dh_ref[...] = dh_s[...]

        for r in _row_chunks(tm):
            ds = lax.dot_general(dh_s[r, :], wd_ref[0], nt_dims, preferred_element_type=F32)
            da_ref[0, r, :] = (ds * ud_ref[0, r, :].astype(F32)).astype(da_ref.dtype)
            du_ref[0, r, :] = (ds * sl_ref[0, r, :].astype(F32)).astype(du_ref.dtype)

    down_b, x_in, x_out, x_shape, x_scr, x_args = _carry(down_b, 6, 4, (nt, ns), carried_down)
    dh, da, du, dg_post, *from_down = pl.pallas_call(
        down_b, name=tag + "_down_b", grid=(nt, ns), in_specs=[row, row, vec, wrow, act3, act3] + x_in,
        out_specs=[row, act3, act3, vec] + x_out,
        out_shape=[jax.ShapeDtypeStruct((t, d), MM_DTYPE)] + [jax.ShapeDtypeStruct((ns, t, fs), MM_DTYPE)] * 2
        + [jax.ShapeDtypeStruct((1, d), F32)] + x_shape,
        scratch_shapes=[pltpu.VMEM((tm, d), MM_DTYPE)] + x_scr,
        compiler_params=_params(("arbitrary", "arbitrary")))(h, dy, g_post, wd, sl, ud, *x_args)

    def down_w(s_ref, dh_ref, dw_ref, acc):
        kk = pl.program_id(1)

        @pl.when(kk == 0)
        def _():
            acc[...] = jnp.zeros(acc.shape, F32)

        acc[...] += lax.dot_general(s_ref[0], dh_ref[...], tn_dims, preferred_element_type=F32)

        @pl.when(kk == nk - 1)
        def _():
            dw_ref[0] = acc[...].astype(dw_ref.dtype)

    dwd = pl.pallas_call(
        down_w, name=tag + "_down_w", grid=(ns, nk),
        in_specs=[pl.BlockSpec((1, tk, fs), lambda q, kk: (q, kk, 0)), pl.BlockSpec((tk, d), lambda q, kk: (kk, 0))],
        out_specs=pl.BlockSpec((1, fs, d), lambda q, kk: (q, 0, 0)),
        out_shape=jax.ShapeDtypeStruct((ns, fs, d), MM_DTYPE), scratch_shapes=[pltpu.VMEM((fs, d), F32)],
        compiler_params=_params(("parallel", "arbitrary")))(s, dh)

    def gate_up_b(da_ref, du_ref, wg_ref, wu_ref, x_ref, dy_ref, g_ref, dx_ref, dg_ref, acc):
        i, q = pl.program_id(0), pl.program_id(1)

        @pl.when((i == 0) & (q == 0))
        def _():
            dg_ref[...] = jnp.zeros(dg_ref.shape, F32)

        @pl.when(q == 0)
        def _():
            acc[...] = jnp.zeros(acc.shape, F32)

        for r in _row_chunks(tm):
            acc[r, :] += (jnp.dot(da_ref[0, r, :], wg_ref[0], preferred_element_type=F32)
                          + jnp.dot(du_ref[0, r, :], wu_ref[0], preferred_element_type=F32))

        @pl.when(q == ns - 1)
        def _():
            for r in _row_chunks(tm):
                xb = x_ref[r, :]
                dx, dg = _rms_bwd(xb, _rms_stats(xb), g_ref[...], acc[r, :])
                dx_ref[r, :] = dy_ref[r, :] + dx
                dg_ref[...] += dg

    def gate_up_w(n_ref, da_ref, du_ref, dwg_ref, dwu_ref, acc_g, acc_u):
        kk = pl.program_id(1)

        @pl.when(kk == 0)
        def _():
            acc_g[...] = jnp.zeros(acc_g.shape, F32)
            acc_u[...] = jnp.zeros(acc_u.shape, F32)

        nb = n_ref[...]
        acc_g[...] += lax.dot_general(da_ref[0], nb, tn_dims, preferred_element_type=F32)
        acc_u[...] += lax.dot_general(du_ref[0], nb, tn_dims, preferred_element_type=F32)

        @pl.when(kk == nk - 1)
        def _():
            dwg_ref[0] = acc_g[...].astype(dwg_ref.dtype)
            dwu_ref[0] = acc_u[...].astype(dwu_ref.dtype)

    k3 = pl.BlockSpec((1, tk, fs), lambda q, kk: (q, kk, 0))
    wout = pl.BlockSpec((1, fs, d), lambda q, kk: (q, 0, 0))
    carried_mid = make_carried_mid(dwd) if make_carried_mid else None
    gate_up_w, x_in, x_out, x_shape, x_scr, x_args = _carry(gate_up_w, 3, 2, (ns, nk), carried_mid)
    dwg, dwu, *from_mid = pl.pallas_call(
        gate_up_w, name=tag + "_gate_up_w", grid=(ns, nk),
        in_specs=[pl.BlockSpec((tk, d), lambda q, kk: (kk, 0)), k3, k3] + x_in, out_specs=[wout, wout] + x_out,
        out_shape=[jax.ShapeDtypeStruct((ns, fs, d), MM_DTYPE)] * 2 + x_shape,
        scratch_shapes=[pltpu.VMEM((fs, d), F32)] * 2 + x_scr,
        compiler_params=_params(("arbitrary", "arbitrary")))(n, da, du, *x_args)

    carried_up = make_carried_up(dwg, dwu) if make_carried_up else None
    gate_up_b, x_in, x_out, x_shape, x_scr, x_args = _carry(gate_up_b, 7, 2, (nt, ns), carried_up)
    dx, dg_pre, *from_up = pl.pallas_call(
        gate_up_b, name=tag + "_gate_up_b", grid=(nt, ns), in_specs=[act3, act3, wrow, wrow, row, row, vec] + x_in,
        out_specs=[row, vec] + x_out,
        out_shape=[jax.ShapeDtypeStruct((t, d), F32), jax.ShapeDtypeStruct((1, d), F32)] + x_shape,
        scratch_shapes=[pltpu.VMEM((tm, d), F32)] + x_scr,
        compiler_params=_params(("arbitrary", "arbitrary")))(da, du, wg, wu, x, dy, g_pre, *x_args)
    return dx, dg_pre, dwg, dwu, dwd, dg_post, from_down, from_mid, from_up


NEG = -1e30


def _attn_scale():
    return (MLA_NOPE + MLA_ROPE) ** -0.5


def _causal_pairs(nq, by_key):
    if by_key:
        pairs = [(qi, ki) for ki in range(nq) for qi in range(ki, nq)]
    else:
        pairs = [(qi, ki) for qi in range(nq) for ki in range(qi + 1)]
    return jnp.asarray([p[0] for p in pairs], jnp.int32), jnp.asarray([p[1] for p in pairs], jnp.int32)


def _below_diagonal(shape):
    return lax.broadcasted_iota(jnp.int32, shape, 1) <= lax.broadcasted_iota(jnp.int32, shape, 0)


def _attn_call(name, body, tables, args, in_kinds, out_kinds, scratch, t, tq, carried=None):
    qmap = lambda h, p, qt, kt: (qt[p], h)
    kmap = lambda h, p, qt, kt: (kt[p], h)
    width = HEADS_PER_STEP * HEAD_LANES
    spec = lambda kind: pl.BlockSpec((tq, width), qmap if kind == "q" else kmap)
    n_pairs = tables[0].shape[0]
    n_groups = N_HEADS // HEADS_PER_STEP
    n_in, n_out, n_scr = len(in_kinds), len(out_kinds), scratch
    x_ins = list(carried.ins) if carried else []
    x_outs = list(carried.outs) if carried else []
    x_scr = [pltpu.SemaphoreType.DMA((carried.n_sem,)), pltpu.SemaphoreType.DMA((carried.n_sem,))] if carried else []

    def full_body(qt, kt, *refs):
        ins, refs = refs[:n_in], refs[n_in:]
        xi, refs = refs[:len(x_ins)], refs[len(x_ins):]
        outs, refs = refs[:n_out], refs[n_out:]
        xo, refs = refs[:len(x_outs)], refs[len(x_outs):]
        scr, sems = refs[:n_scr], refs[n_scr:]
        if carried:
            @pl.when((pl.program_id(0) == 0) & (pl.program_id(1) == 0))
            def _():
                carried.start(xi, xo, *sems)

        heads = [tuple(r.at[:, pl.ds(hh * HEAD_LANES, HEAD_LANES)] for r in (*ins, *outs, *scr))
                 for hh in range(HEADS_PER_STEP)]
        body(qt, kt, heads)
        if carried:
            @pl.when((pl.program_id(0) == n_groups - 1) & (pl.program_id(1) == n_pairs - 1))
            def _():
                carried.finish(xi, xo, *sems)

    grid_spec = pltpu.PrefetchScalarGridSpec(
        num_scalar_prefetch=2, grid=(n_groups, n_pairs),
        in_specs=[spec(kd) for kd in in_kinds] + [HBM_SPEC] * len(x_ins),
        out_specs=[spec(kd) for kd in out_kinds] + [HBM_SPEC] * len(x_outs),
        scratch_shapes=[pltpu.VMEM((tq, width), F32)] * n_scr + x_scr)
    return pl.pallas_call(full_body, name=name, grid_spec=grid_spec,
                          out_shape=[jax.ShapeDtypeStruct((t, MLA_PAD), F32) for _ in out_kinds] + x_outs,
                          compiler_params=_params(("arbitrary", "arbitrary")))(*tables, *args, *x_ins)


class _Carried:
    def __init__(self, ins, outs, n_sem, start, finish):
        self.ins, self.outs, self.n_sem, self.start, self.finish = ins, outs, n_sem, start, finish


def _attn_fwd(q, k, v, tq, carried=None):
    t = q.shape[0]
    nq = t // tq

    def body(qt, kt, heads):
        p_id = pl.program_id(1)
        qi, ki = qt[p_id], kt[p_id]

        @pl.when(ki == 0)
        def _():
            for _, _, _, _, _, m_s, l_s, acc_s in heads:
                m_s[...] = jnp.full(m_s.shape, NEG, F32)
                l_s[...] = jnp.zeros(l_s.shape, F32)
                acc_s[...] = jnp.zeros(acc_s.shape, F32)

        def update(diagonal):
            for q_ref, k_ref, v_ref, _, _, m_s, l_s, acc_s in heads:
                s = lax.dot_general(q_ref[...], k_ref[...], (((1,), (1,)), ((), ())), preferred_element_type=F32)
                if diagonal:
                    s = jnp.where(_below_diagonal(s.shape), s, NEG)
                m_old = m_s[...]
                m_new = jnp.maximum(m_old, jnp.max(s, axis=1, keepdims=True))
                alpha = jnp.exp(m_old - m_new)
                p = jnp.exp(s - m_new[:, :1])
                l_s[...] = l_s[...] * alpha + jnp.sum(p, axis=1, keepdims=True)
                acc_s[...] = acc_s[...] * alpha + jnp.dot(p.astype(MM_DTYPE), v_ref[...], preferred_element_type=F32)
                m_s[...] = m_new

        @pl.when(ki < qi)
        def _():
            update(False)

        @pl.when(ki == qi)
        def _():
            update(True)
            for _, _, _, o_ref, lse_ref, m_s, l_s, acc_s in heads:
                o_ref[...] = acc_s[...] / l_s[...]
                lse_ref[...] = m_s[...] + jnp.log(l_s[...])

    return _attn_call("mla_attn_fwd", body, _causal_pairs(nq, False), (q, k, v), "qkk", "qq", 3, t, tq, carried)


def _attn_probs(q, k, lse, diagonal):
    s = lax.dot_general(q, k, (((1,), (1,)), ((), ())), preferred_element_type=F32)
    p = jnp.exp(s - lse[:, :1])
    return jnp.where(_below_diagonal(s.shape), p, 0.0) if diagonal else p


BWD_HEADS = 4


def _attn_bwd(q, k, v, do, lse, delta, tq, carried=None):
    t = q.shape[0]
    nq = t // tq
    width = BWD_HEADS * HEAD_LANES
    n_groups = N_HEADS // BWD_HEADS
    qt_tab, kt_tab = _causal_pairs(nq, True)
    n_pairs = qt_tab.shape[0]
    qmap = lambda h, p, qt, kt: (qt[p], h)
    kmap = lambda h, p, qt, kt: (kt[p], h)
    qs, ks = pl.BlockSpec((tq, width), qmap), pl.BlockSpec((tq, width), kmap)
    x_ins = list(carried.ins) if carried else []
    x_outs = list(carried.outs) if carried else []
    x_scr = [pltpu.SemaphoreType.DMA((carried.n_sem,)), pltpu.SemaphoreType.DMA((carried.n_sem,))] if carried else []
    nt_dims = (((1,), (1,)), ((), ()))
    tn_dims = (((0,), (0,)), ((), ()))

    def body(qt, kt, q_ref, k_ref, v_ref, do_ref, lse_ref, dl_ref, *rest):
        xi, rest = rest[:len(x_ins)], rest[len(x_ins):]
        dq_hbm, dk_ref, dv_ref = rest[:3]
        xo, rest = rest[3:3 + len(x_outs)], rest[3 + len(x_outs):]
        dk_s, dv_s, dq_s, dq_sem = rest[:4]
        sems = rest[4:]
        grp, p_id = pl.program_id(0), pl.program_id(1)
        qi, ki = qt[p_id], kt[p_id]
        if carried:
            @pl.when((grp == 0) & (p_id == 0))
            def _():
                carried.start(xi, xo, *sems)

        @pl.when(p_id == 0)
        def _():
            dq_s[...] = jnp.zeros(dq_s.shape, F32)

        def step(diagonal):
            rows = pl.ds(pl.multiple_of(qi * tq, tq), tq)
            for hh in range(BWD_HEADS):
                ln = pl.ds(hh * HEAD_LANES, HEAD_LANES)
                qb, kb, vb, dob = q_ref[:, ln], k_ref[:, ln], v_ref[:, ln], do_ref[:, ln]
                p = _attn_probs(qb, kb, lse_ref[:, ln], diagonal)
                dv_s[:, ln] += lax.dot_general(p.astype(MM_DTYPE), dob, tn_dims, preferred_element_type=F32)
                dp = lax.dot_general(dob, vb, nt_dims, preferred_element_type=F32)
                ds = (p * (dp - dl_ref[:, ln][:, :1])).astype(MM_DTYPE)
                dk_s[:, ln] += lax.dot_general(ds, qb, tn_dims, preferred_element_type=F32)
                dq_s[rows, ln] += jnp.dot(ds, kb, preferred_element_type=F32)

        @pl.when(qi == ki)
        def _():
            dk_s[...] = jnp.zeros(dk_s.shape, F32)
            dv_s[...] = jnp.zeros(dv_s.shape, F32)
            step(True)

        @pl.when(qi > ki)
        def _():
            step(False)

        @pl.when(qi == nq - 1)
        def _():
            dk_ref[...] = dk_s[...]
            dv_ref[...] = dv_s[...]

        @pl.when(p_id == n_pairs - 1)
        def _():
            out = pltpu.make_async_copy(dq_s, dq_hbm.at[pl.ds(pl.multiple_of(grp * t, t), t)], dq_sem)
            out.start()
            out.wait()

        if carried:
            @pl.when((grp == n_groups - 1) & (p_id == n_pairs - 1))
            def _():
                carried.finish(xi, xo, *sems)

    grid_spec = pltpu.PrefetchScalarGridSpec(
        num_scalar_prefetch=2, grid=(n_groups, n_pairs),
        in_specs=[qs, ks, ks, qs, qs, qs] + [HBM_SPEC] * len(x_ins),
        out_specs=[HBM_SPEC, ks, ks] + [HBM_SPEC] * len(x_outs),
        scratch_shapes=[pltpu.VMEM((tq, width), F32), pltpu.VMEM((tq, width), F32), pltpu.VMEM((t, width), F32),
                        pltpu.SemaphoreType.DMA] + x_scr)
    return pl.pallas_call(
        body, name="mla_attn_bwd", grid_spec=grid_spec,
        out_shape=[jax.ShapeDtypeStruct((n_groups * t, width), F32), jax.ShapeDtypeStruct((t, MLA_PAD), F32),
                   jax.ShapeDtypeStruct((t, MLA_PAD), F32)] + x_outs,
        compiler_params=_params(("arbitrary", "arbitrary")))(qt_tab, kt_tab, q, k, v, do, lse, delta, *x_ins)


def _dot01(a, b, dims=(((1,), (0,)), ((), ())), ones="rhs"):
    val, sel = (a, b) if ones == "rhs" else (b, a)
    head = val.astype(BF16)
    tail = (val - head.astype(F32)).astype(BF16)
    sel = sel.astype(BF16)
    dot = lambda part: (lax.dot_general(part, sel, dims, preferred_element_type=F32) if ones == "rhs"
                        else lax.dot_general(sel, part, dims, preferred_element_type=F32))
    return dot(head) + dot(tail)


def _dot1(a, b, dims=(((1,), (0,)), ((), ()))):
    return lax.dot_general(a.astype(MM_DTYPE), b.astype(MM_DTYPE), dims, preferred_element_type=F32)


def _dot3(a, b, dims=(((1,), (0,)), ((), ()))):
    return lax.dot_general(a, b, dims, preferred_element_type=F32, precision=lax.Precision.HIGH)


NN3 = (((2,), (1,)), ((0,), (0,)))
NT3 = (((2,), (2,)), ((0,), (0,)))
TN3 = (((1,), (1,)), ((0,), (0,)))


def _tri_masks(nh):
    shape = (nh, CHUNK, CHUNK)
    return lax.broadcasted_iota(jnp.int32, shape, 1), lax.broadcasted_iota(jnp.int32, shape, 2)


def _gdn_chunk_common(k, gcc, bb, row, col, dot=_dot1):
    tril = row >= col
    gcr = jnp.swapaxes(gcc, 1, 2)
    dm = jnp.exp(jnp.where(tril, gcc - gcr, NEG))
    kb = k * bb
    lm = jnp.where(row > col, dot(kb, k, NT3) * dm, 0.0)
    return dm, kb, lm


def _unit_lower_inverse(lm, eye):
    t = eye - lm
    p = lm
    for _ in range(CHUNK.bit_length() - 2):
        p = _dot3(p, p, NN3)
        t = t + _dot3(t, p, NN3)
    return t


def _chunk_sum_matrix(tb, upper):
    r = lax.broadcasted_iota(jnp.int32, (tb, tb), 0)
    c = lax.broadcasted_iota(jnp.int32, (tb, tb), 1)
    same = (r // CHUNK) == (c // CHUNK)
    return (same & ((c >= r) if upper else (c <= r))).astype(F32)


def _gdn_fwd(q, k, v, gb, bb, carried=None):
    nh, t, dh = q.shape
    nchunk = t // CHUNK

    def body(q_ref, k_ref, v_ref, g_ref, b_ref, o_ref, sall_ref, tall_ref, s_s):
        @pl.when(pl.program_id(0) == 0)
        def _():
            s_s[...] = jnp.zeros(s_s.shape, F32)

        row, col = _tri_masks(nh)
        sh = s_s[...]
        for cc in range(cps):
            rows = pl.ds(cc * CHUNK, CHUNK)
            qh, kh, vh, bbh, gcc = q_ref[:, rows, :], k_ref[:, rows, :], v_ref[:, rows, :], b_ref[:, rows, :], \
                g_ref[:, rows, :]
            dm, kb, lm = _gdn_chunk_common(kh, gcc, bbh, row, col)
            eg = jnp.exp(gcc)
            glr = gcc[:, CHUNK - 1:CHUNK, :]
            th = _unit_lower_inverse(lm, (row == col).astype(F32))
            w = _dot1(th, kb * eg, NN3)
            u = _dot1(th, vh * bbh, NN3)
            at = jnp.where(row >= col, _dot1(qh, kh, NT3) * dm, 0.0)
            vn = u - _dot1(w, sh, NN3)
            o_ref[:, rows, :] = _dot1(qh * eg, sh, NN3) + _dot1(at, vn, NN3)
            kd = kh * jnp.exp(glr - gcc)
            sall_ref[:, cc] = sh
            tall_ref[:, rows, :] = th
            sh = sh * jnp.exp(glr) + _dot1(kd, vn, TN3)
        s_s[...] = sh

    cps = min(GDN_STEP_CHUNKS, nchunk)
    steps = nchunk // cps
    blk = pl.BlockSpec((nh, cps * CHUNK, dh), lambda n: (0, n, 0))
    body, x_in, x_out, x_shape, x_scr, x_args = _carry(body, 5, 3, (steps,), carried)
    return pl.pallas_call(
        body, name="gdn_fwd", grid=(steps,), in_specs=[blk] * 5 + x_in,
        out_specs=[blk, pl.BlockSpec((nh, cps, dh, dh), lambda n: (0, n, 0, 0)), blk] + x_out,
        out_shape=[jax.ShapeDtypeStruct((nh, t, dh), F32), jax.ShapeDtypeStruct((nh, nchunk, dh, dh), F32),
                   jax.ShapeDtypeStruct((nh, t, CHUNK), F32)] + x_shape,
        scratch_shapes=[pltpu.VMEM((nh, dh, dh), F32)] + x_scr,
        compiler_params=_params(("arbitrary",)))(q, k, v, gb, bb, *x_args)


def _gdn_bwd(q, k, v, gb, bb, sall, tall, do):
    nh, t, dh = q.shape
    nchunk = t // CHUNK

    def body(q_ref, k_ref, v_ref, g_ref, b_ref, sall_ref, tall_ref, do_ref,
             dq_ref, dk_ref, dv_ref, dg_ref, db_ref, ds_s):
        @pl.when(pl.program_id(0) == 0)
        def _():
            ds_s[...] = jnp.zeros(ds_s.shape, F32)

        row, col = _tri_masks(nh)
        tril, stril = row >= col, row > col
        rsum = lambda x: jnp.sum(x, axis=2, keepdims=True)
        dsp = ds_s[...]
        for cc in reversed(range(cps)):
            rows = pl.ds(cc * CHUNK, CHUNK)
            dsp = chunk_bwd(rows, cc, dsp, row, col, tril, stril, rsum, q_ref, k_ref, v_ref, g_ref, b_ref, sall_ref,
                            tall_ref, do_ref, dq_ref, dk_ref, dv_ref, dg_ref, db_ref)
        ds_s[...] = dsp

    def chunk_bwd(rows, cc, dsp, row, col, tril, stril, rsum, q_ref, k_ref, v_ref, g_ref, b_ref, sall_ref, tall_ref,
                  do_ref, dq_ref, dk_ref, dv_ref, dg_ref, db_ref):
        qh, kh, vh, gcc, bbh = q_ref[:, rows, :], k_ref[:, rows, :], v_ref[:, rows, :], g_ref[:, rows, :], \
            b_ref[:, rows, :]
        sh, th, doh = sall_ref[:, cc], tall_ref[:, rows, :], do_ref[:, rows, :]
        dm, kb, lm = _gdn_chunk_common(kh, gcc, bbh, row, col, _dot3)
        eg = jnp.exp(gcc)
        glr = gcc[:, CHUNK - 1:CHUNK, :]
        glv = jnp.exp(glr)
        egl = jnp.exp(glr - gcc)
        rw, ru = kb * eg, vh * bbh
        w, u = _dot3(th, rw, NN3), _dot3(th, ru, NN3)
        at = jnp.where(tril, _dot3(qh, kh, NT3) * dm, 0.0)
        qd, kd = qh * eg, kh * egl
        vn = u - _dot3(w, sh, NN3)
        dgl = jnp.sum(rsum(dsp * sh), axis=1, keepdims=True)
        dkd = _dot3(vn, dsp, NT3)
        dvn = _dot3(kd, dsp, NN3)
        dqd = _dot3(doh, sh, NT3)
        dat = jnp.where(tril, _dot3(doh, vn, NT3), 0.0)
        dvn = dvn + _dot3(at, doh, TN3)
        dw = -_dot3(dvn, sh, NT3)
        ds_before = dsp * glv + _dot3(qd, doh, TN3) - _dot3(w, dvn, TN3)
        dpa = dat * dm
        dq_ref[:, rows, :] = _dot1(dpa, kh, NN3) + dqd * eg
        dk = _dot1(dpa, qh, TN3) + dkd * egl
        t6 = rsum(dkd * kd)
        dgam = rsum(dqd * qd) - t6
        dgam_last = jnp.sum(t6, axis=1, keepdims=True) + dgl * glv
        drw = _dot3(th, dw, TN3)
        dru = _dot3(th, dvn, TN3)
        dl = -jnp.where(stril, _dot3(drw, w, NT3) + _dot3(dru, u, NT3), 0.0)
        dgam = dgam + rsum(drw * rw)
        dv_ref[:, rows, :] = dru * bbh
        dp2 = dl * dm
        dkb = drw * eg + _dot1(dp2, kh, NN3)
        dk_ref[:, rows, :] = dk + _dot1(dp2, kb, TN3) + dkb * bbh
        db_ref[:, rows, :] = rsum(dru * vh) + rsum(dkb * kh) + jnp.zeros((nh, CHUNK, dh), F32)
        e = dat * at + dl * lm
        dgam_b = dgam + rsum(e) - _dot01(e, jnp.ones((nh, CHUNK, CHUNK), F32), TN3)
        dg_ref[:, rows, :] = dgam_b + jnp.where(row == CHUNK - 1, dgam_last, 0.0)
        return ds_before

    cps = min(GDN_STEP_CHUNKS, nchunk)
    steps = nchunk // cps
    rev = lambda n: (0, steps - 1 - n, 0)
    blk = pl.BlockSpec((nh, cps * CHUNK, dh), rev)
    sblk = pl.BlockSpec((nh, cps, dh, dh), lambda n: (0, steps - 1 - n, 0, 0))
    out = jax.ShapeDtypeStruct((nh, t, dh), F32)
    return pl.pallas_call(
        body, name="gdn_bwd", grid=(steps,), in_specs=[blk] * 5 + [sblk, blk, blk], out_specs=[blk] * 5,
        out_shape=[out] * 5, scratch_shapes=[pltpu.VMEM((nh, dh, dh), F32)],
        compiler_params=_params(("arbitrary",)))(q, k, v, gb, bb, sall, tall, do)


def _group_ones():
    r = lax.broadcasted_iota(jnp.int32, (GDN_W, GDN_W), 0) // GDN_DH
    c = lax.broadcasted_iota(jnp.int32, (GDN_W, GDN_W), 1) // GDN_DH
    return (r == c).astype(F32)


def _conv_taps(x, xprev, w, has_prev):
    row = lax.broadcasted_iota(jnp.int32, x.shape, 0)
    out = x * w[GDN_CONV - 1:GDN_CONV, :]
    for s in range(1, GDN_CONV):
        sh = jnp.where(row >= s, _roll(x, s, 0), _roll(xprev, s, 0) * has_prev)
        out = out + sh * w[GDN_CONV - 1 - s:GDN_CONV - s, :]
    return out


def _head_cols(x, h):
    return x[:, h * GDN_DH:(h + 1) * GDN_DH]


def _heads_spec(tb):
    return pl.BlockSpec((N_HEADS, tb, GDN_DH), lambda i: (0, i, 0))


def _mixer_fwd(x, positions, w, tb, carried=None, carried_gdn=None):
    t, d = x.shape
    tables = _rope_tables(positions)

    hn, proj = _prologue_mm("mix_in", [x], w["mix_pre_g"], lambda xb, g: (xb * _rms_stats(xb) * g,), w["w_in_pad_t"],
                            False)

    def mla_pre(p0, gq, gkv):
        cq, ckv = p0[:, :MLA_Q_RANK], p0[:, MLA_Q_RANK:MLA_Q_RANK + MLA_KV_RANK]
        return cq * _rms_stats(cq) * gq, ckv * _rms_stats(ckv) * gkv

    nq, nkv = _rowwise("mla_pre", mla_pre, [(proj, 512, PIN_MLA // 512, 0)],
                       [w["mla_q_norm_g"], w["mla_kv_norm_g"]], [(MLA_Q_RANK, BF16), (MLA_KV_RANK, BF16)], [], tb)
    qraw = _mm("mla_uq", nq, w["w_uq_pad"], "nn", F32)
    kv = _mm("mla_ukv", nkv, w["w_kv_pad"], "nn", F32)

    def rope_f(qr, kn, vv, kpe, c, s1, s2):
        qo = _heads_apply(qr, lambda xh: _rope(xh, c, s1, s2)) * _attn_scale()
        kp = _rope(kpe, c, s1, s2)
        return qo, kn + jnp.tile(kp, (1, N_HEADS)), vv

    q, k, v = _rowwise("mla_rope", rope_f,
                       [qraw, (kv, MLA_PAD, 0, 0), (kv, MLA_PAD, 1, 0), (proj, HEAD_LANES, PIN_KPE // HEAD_LANES, 0),
                        tables[0], tables[1], tables[2]], [],
                       [(MLA_PAD, BF16)] * 3, [], tb // 2)
    tq = min(1024, t)
    o, lse, *carried_out = _attn_fwd(q, k, v, tq, carried)

    def mla_post(ob, g):
        return (ob * _rms_stats(ob, N_HEADS * MLA_V) * g,)

    (cat,) = _rowwise("mla_post", mla_post, [o], [w["mla_out_g_pad"]], [(MLA_PAD, BF16)], [], tb, wide=(CAT_W, 0))

    gones = _group_ones()
    steps = t // tb

    def gdn_pre(xq, xk, xv, pq, pk, pv, cw, go, has_prev):
        outs = []
        for j, (xc, xp) in enumerate(((xq, pq), (xk, pk), (xv, pv))):
            c = _conv_taps(xc, xp, cw[:, j * GDN_W:(j + 1) * GDN_W], has_prev)
            a = c * _sigmoid(c)
            if j < 2:
                rn = lax.rsqrt(_dot01(a * a, go) + EPS)
                a = a * rn
                if j == 0:
                    a = a * (GDN_DH ** -0.5)
            outs.append(a)
        return tuple(outs)

    qh, kh, vh = _gdn_pre_call("gdn_pre", gdn_pre, proj, w["conv_w"], gones, tb, steps)
    heads_shape = jax.ShapeDtypeStruct((N_HEADS, t, GDN_DH), F32)
    lanes_shape = jax.ShapeDtypeStruct((t, HEAD_LANES), F32)
    lanes_spec = pl.BlockSpec((tb, HEAD_LANES), lambda i: (i, 0))
    vec_spec = lambda n: pl.BlockSpec((1, n), lambda i: (0, 0))

    def gate_f(ab_ref, al_ref, dt_ref, g_ref, b_ref, gh_ref, bh_ref):
        g, b = _gb_fwd(ab_ref[...], al_ref[...], dt_ref[...])
        g_ref[...] = g
        b_ref[...] = b
        gc = _dot01(_chunk_sum_matrix(tb, False), g, ones="lhs")
        for h in range(N_HEADS):
            gh_ref[h] = jnp.broadcast_to(gc[:, h:h + 1], (tb, GDN_DH))
            bh_ref[h] = jnp.broadcast_to(b[:, N_HEADS + h:N_HEADS + h + 1], (tb, GDN_DH))

    g128, b128, gbh, bbh = pl.pallas_call(
        gate_f, name="gdn_gate_f", grid=(steps,),
        in_specs=[pl.BlockSpec((tb, HEAD_LANES), lambda i: (i, PIN_AB // HEAD_LANES)), vec_spec(HEAD_LANES),
                  vec_spec(HEAD_LANES)],
        out_specs=[lanes_spec, lanes_spec, _heads_spec(tb), _heads_spec(tb)],
        out_shape=[lanes_shape, lanes_shape, heads_shape, heads_shape],
        compiler_params=_params(("arbitrary",)))(proj, w["a_log_pad"], w["dt_bias_pad"])
    oh, sall, tall, *carried_out_gdn = _gdn_fwd(qh, kh, vh, gbh, bbh, carried_gdn)

    def gdn_post(o_ref, gt_ref, g_ref, cat_in, cat_ref):
        gt, g = gt_ref[...], g_ref[...]
        outs = []
        for h in range(N_HEADS):
            ob, gth = o_ref[h], _head_cols(gt, h)
            outs.append(ob * _rms_stats(ob) * g * (gth * _sigmoid(gth)))
        cat_ref[...] = jnp.concatenate(outs, axis=1).astype(cat_ref.dtype)

    gate_spec = pl.BlockSpec((tb, GDN_W), lambda i: (i, PIN_GATE // GDN_W))
    cat = pl.pallas_call(
        gdn_post, name="gdn_post", grid=(steps,),
        in_specs=[_heads_spec(tb), gate_spec, vec_spec(GDN_DH), ANY_SPEC],
        out_specs=pl.BlockSpec((tb, GDN_W), lambda i: (i, MLA_PAD // GDN_W)),
        out_shape=jax.ShapeDtypeStruct((t, CAT_W), BF16), input_output_aliases={3: 0},
        compiler_params=_params(("arbitrary",)))(oh, proj, w["gdn_norm_g"], cat)
    mixed, y = _mm_epilogue("mix_out", cat, w["w_out_pad"], [x], w["mix_post_g"],
                            lambda hb, xb, g: (hb, xb + hb * _rms_stats(hb) * g), 2, False, tk=CAT_W)
    saved = dict(x=x, hn=hn, proj=proj, nq=nq, nkv=nkv, q=q, k=k, v=v, o=o, lse=lse, qh=qh, kh=kh, vh=vh,
                 gbh=gbh, bbh=bbh, oh=oh, sall=sall, tall=tall, cat=cat, mixed=mixed,
                 tables=tables, g128=g128, b128=b128)
    return y, saved, list(carried_out) + list(carried_out_gdn)


def _qkv_specs(tb):
    base = PIN_QKV // GDN_W
    cur = [pl.BlockSpec((tb, GDN_W), lambda i, j=j: (i, base + j)) for j in range(3)]
    prev = [pl.BlockSpec((tb, GDN_W), lambda i, j=j: (jnp.maximum(i - 1, 0), base + j)) for j in range(3)]
    return cur + prev


def _gdn_pre_call(name, fn, proj, conv_w, gones, tb, steps):
    t = proj.shape[0]

    def body(xq, xk, xv, pq, pk, pv, cw, go, oq, ok, ov):
        has_prev = jnp.where(pl.program_id(0) == 0, 0.0, 1.0)
        outs = fn(xq[...], xk[...], xv[...], pq[...], pk[...], pv[...], cw[...], go[...], has_prev)
        for r, val in zip((oq, ok, ov), outs):
            for h in range(N_HEADS):
                r[h] = _head_cols(val, h)

    return pl.pallas_call(
        body, name=name, grid=(steps,),
        in_specs=_qkv_specs(tb) + [pl.BlockSpec(conv_w.shape, lambda i: (0, 0)),
                                   pl.BlockSpec(gones.shape, lambda i: (0, 0))],
        out_specs=[_heads_spec(tb)] * 3,
        out_shape=[jax.ShapeDtypeStruct((N_HEADS, t, GDN_DH), F32)] * 3,
        compiler_params=_params(("arbitrary",)))(proj, proj, proj, proj, proj, proj, conv_w, gones)


def _softplus(x):
    return jnp.maximum(x, 0.0) + jnp.log1p(jnp.exp(-jnp.abs(x)))


def _gb_fwd(ab, a_log, dt_bias):
    g = -jnp.exp(a_log) * _softplus(ab + dt_bias)
    return g, _sigmoid(ab)


def _rope_tables(positions):
    half = MLA_ROPE // 2
    freqs = ROPE_THETA ** (-jnp.arange(half, dtype=F32) / half)
    ang = positions.reshape(-1).astype(F32)[:, None] * freqs
    cos, sin = jnp.cos(ang), jnp.sin(ang)
    t = ang.shape[0]
    one = jnp.ones((t, MLA_NOPE), F32)
    z16, z32, z64 = jnp.zeros((t, half), F32), jnp.zeros((t, MLA_ROPE), F32), jnp.zeros((t, MLA_NOPE), F32)
    c = jnp.concatenate([one, cos, cos, jnp.ones((t, MLA_ROPE), F32)], axis=1)
    s1 = jnp.concatenate([z64, -sin, z16, z32], axis=1)
    s2 = jnp.concatenate([z64, z16, sin, z32], axis=1)
    return c, s1, s2


def _mixer_bwd(dy, sv, w, tb, carried=None):
    x, proj = sv["x"], sv["proj"]
    t, d = x.shape
    c, s1, s2 = sv["tables"]
    grads = {}

    dmixed, dcat, grads["mix_post_g"] = _prologue_mm(
        "mix_out_bx", [sv["mixed"], dy], w["mix_post_g"], lambda hb, dyb, g: _rms_bwd(hb, _rms_stats(hb), g, dyb),
        w["w_out_pad"], True)
    grads["w_out_pad"] = _mm("mix_out_bw", sv["cat"], dmixed, "tn", F32)
    steps = t // tb
    vec_spec = lambda n: pl.BlockSpec((1, n), lambda i: (0, 0))

    def gdn_post_b(o_ref, gt_ref, do_ref, g_ref, dproj_ref, doh_ref, dg_ref):
        @pl.when(pl.program_id(0) == 0)
        def _():
            dg_ref[...] = jnp.zeros(dg_ref.shape, F32)

        gt, dob, g = gt_ref[...], do_ref[...], g_ref[...]
        dgates = []
        for h in range(N_HEADS):
            ob, gth, dobh = o_ref[h], _head_cols(gt, h), _head_cols(dob, h)
            sg = _sigmoid(gth)
            r = _rms_stats(ob)
            dxo, dg = _rms_bwd(ob, r, g, dobh * (gth * sg))
            doh_ref[h] = dxo
            dg_ref[...] += dg
            dgates.append(dobh * (ob * r * g) * (sg * (1.0 + gth * (1.0 - sg))))
        dproj_ref[...] = jnp.concatenate(dgates, axis=1).astype(dproj_ref.dtype)

    dproj, doh, grads["gdn_norm_g"] = pl.pallas_call(
        gdn_post_b, name="gdn_post_b", grid=(steps,),
        in_specs=[_heads_spec(tb), pl.BlockSpec((tb, GDN_W), lambda i: (i, PIN_GATE // GDN_W)),
                  pl.BlockSpec((tb, GDN_W), lambda i: (i, MLA_PAD // GDN_W)), vec_spec(GDN_DH)],
        out_specs=[pl.BlockSpec((tb, GDN_W), lambda i: (i, PIN_GATE // GDN_W)), _heads_spec(tb), vec_spec(GDN_DH)],
        out_shape=[jax.ShapeDtypeStruct((t, PIN_W), BF16), jax.ShapeDtypeStruct((N_HEADS, t, GDN_DH), F32),
                   jax.ShapeDtypeStruct((1, GDN_DH), F32)],
        compiler_params=_params(("arbitrary",)))(sv["oh"], proj, dcat, w["gdn_norm_g"])

    def mla_post_b(ob, dmo, g):
        do, dg = _rms_bwd(ob, _rms_stats(ob, N_HEADS * MLA_V), g, dmo, N_HEADS * MLA_V)
        prod = do * ob
        delta = _heads_apply(prod, lambda ph: jnp.sum(ph, axis=1, keepdims=True) + jnp.zeros_like(ph))
        return do, delta, dg

    do, delta, grads["mla_out_g_pad"] = _rowwise(
        "mla_post_b", mla_post_b, [sv["o"], (dcat, MLA_PAD, 0, 0)], [w["mla_out_g_pad"]],
        [(MLA_PAD, BF16), (MLA_PAD, F32)], [(1, MLA_PAD)], tb // 2)
    tq = min(1024, t)
    dq, dk, dv, *carried_out = _attn_bwd(sv["q"], sv["k"], sv["v"], do, sv["lse"], delta, tq, carried)
    n_groups = N_HEADS // BWD_HEADS
    tr = tb // 2
    dq_groups = [(dq, BWD_HEADS * HEAD_LANES, 0, grp * (t // tr)) for grp in range(n_groups)]

    def rope_b(*blocks):
        dqb = jnp.concatenate(blocks[:n_groups], axis=1)
        dkb, dvb, cc, a1, a2 = blocks[n_groups:]
        dqr = _heads_apply(dqb * _attn_scale(), lambda xh: _rope(xh, cc, -a1, -a2))
        ksum = dkb[:, :HEAD_LANES]
        for h in range(1, N_HEADS):
            ksum = ksum + dkb[:, h * HEAD_LANES:(h + 1) * HEAD_LANES]
        lane = lax.broadcasted_iota(jnp.int32, ksum.shape, 1)
        keep = (lane >= MLA_NOPE) & (lane < MLA_NOPE + MLA_ROPE)
        dkpe = jnp.where(keep, _rope(ksum, cc, -a1, -a2), 0.0)
        return dqr, jnp.concatenate([dkb, dvb], axis=1), dkpe

    dqraw, dkv, dkpe = _rowwise("mla_rope_b", rope_b, dq_groups + [dk, dv, c, s1, s2], [],
                                [(MLA_PAD, BF16, t), (2 * MLA_PAD, BF16), (HEAD_LANES, F32)], [], tr)
    dnq = _mm("mla_uq_bx", dqraw, w["w_uq_pad"], "nt", F32)
    grads["w_uq_pad"] = _mm("mla_uq_bw", sv["nq"], dqraw, "tn", F32)
    dnkv = _mm("mla_ukv_bx", dkv, w["w_kv_pad"], "nt", F32)
    grads["w_kv_pad"] = _mm("mla_ukv_bw", sv["nkv"], dkv, "tn", F32)

    def mla_pre_b(p0, dnqb, dnkvb, dkpeb, gq, gkv):
        cq, ckv = p0[:, :MLA_Q_RANK], p0[:, MLA_Q_RANK:MLA_Q_RANK + MLA_KV_RANK]
        dcq, dgq = _rms_bwd(cq, _rms_stats(cq), gq, dnqb)
        dckv, dgkv = _rms_bwd(ckv, _rms_stats(ckv), gkv, dnkvb)
        return jnp.concatenate([dcq, dckv, dkpeb], axis=1), dgq, dgkv

    dproj, grads["mla_q_norm_g"], grads["mla_kv_norm_g"] = _rowwise(
        "mla_pre_b", mla_pre_b, [(proj, 512, PIN_MLA // 512, 0), dnq, dnkv, dkpe],
        [w["mla_q_norm_g"], w["mla_kv_norm_g"]], [(512, BF16)], [(1, MLA_Q_RANK), (1, MLA_KV_RANK)], tb,
        wide=(PIN_W, PIN_MLA // 512), carry=dproj)

    dqh, dkh, dvh, dgh, dbh = _gdn_bwd(sv["qh"], sv["kh"], sv["vh"], sv["gbh"], sv["bbh"], sv["sall"], sv["tall"], doh)
    gones = _group_ones()

    def gdn_pre_b(xq, xk, xv, pq, pk, pv, dq_, dk_, dv_, cw, go, has_prev):
        outs = []
        for j, (xc, xp, dd) in enumerate(((xq, pq, dq_), (xk, pk, dk_), (xv, pv, dv_))):
            cc = _conv_taps(xc, xp, cw[:, j * GDN_W:(j + 1) * GDN_W], has_prev)
            sg = _sigmoid(cc)
            a = cc * sg
            if j < 2:
                rn = lax.rsqrt(_dot01(a * a, go) + EPS)
                if j == 0:
                    dd = dd * (GDN_DH ** -0.5)
                da = rn * dd - a * (rn * rn * rn) * _dot01(dd * a, go)
            else:
                da = dd
            outs.append(da * (sg * (1.0 + cc * (1.0 - sg))))
        return tuple(outs)

    dcq, dck, dcv = _gdn_pre_b_call("gdn_pre_b", gdn_pre_b, proj, (dqh, dkh, dvh), w["conv_w"], gones, tb, steps)
    dproj, grads["conv_w"] = _conv_bwd_call("gdn_conv_b", proj, (dcq, dck, dcv), w["conv_w"], dproj, tb, steps)

    def gate_b(ab_ref, g_ref, b_ref, dgh_ref, dbh_ref, al_ref, dt_ref, carry_ref, dab_ref, dal_ref, ddt_ref):
        @pl.when(pl.program_id(0) == 0)
        def _():
            dal_ref[...] = jnp.zeros(dal_ref.shape, F32)
            ddt_ref[...] = jnp.zeros(ddt_ref.shape, F32)

        ab, g128, b128 = ab_ref[...], g_ref[...], b_ref[...]
        lane = lax.broadcasted_iota(jnp.int32, ab.shape, 1)
        dg_ = jnp.zeros(ab.shape, F32)
        db_ = jnp.zeros(ab.shape, F32)
        for h in range(N_HEADS):
            dg_ = dg_ + jnp.where(lane == h, jnp.broadcast_to(dgh_ref[h][:, 0:1], ab.shape), 0.0)
            db_ = db_ + jnp.where(lane == N_HEADS + h, jnp.broadcast_to(dbh_ref[h][:, 0:1], ab.shape), 0.0)
        dg_ = _dot01(_chunk_sum_matrix(tb, True), dg_, ones="lhs")
        slope = -jnp.exp(al_ref[...]) * _sigmoid(ab + dt_ref[...])
        dab_ref[...] = (dg_ * slope + db_ * b128 * (1.0 - b128)).astype(dab_ref.dtype)
        dal_ref[...] += jnp.sum(dg_ * g128, axis=0, keepdims=True)
        ddt_ref[...] += jnp.sum(dg_ * slope, axis=0, keepdims=True)

    lanes_spec = pl.BlockSpec((tb, HEAD_LANES), lambda i: (i, 0))
    ab_spec = pl.BlockSpec((tb, HEAD_LANES), lambda i: (i, PIN_AB // HEAD_LANES))
    dproj, grads["a_log_pad"], grads["dt_bias_pad"] = pl.pallas_call(
        gate_b, name="gdn_gate_b", grid=(steps,),
        in_specs=[ab_spec, lanes_spec, lanes_spec, _heads_spec(tb), _heads_spec(tb), vec_spec(HEAD_LANES),
                  vec_spec(HEAD_LANES), ANY_SPEC],
        out_specs=[ab_spec, vec_spec(HEAD_LANES), vec_spec(HEAD_LANES)],
        out_shape=[jax.ShapeDtypeStruct((t, PIN_W), BF16), jax.ShapeDtypeStruct((1, HEAD_LANES), F32),
                   jax.ShapeDtypeStruct((1, HEAD_LANES), F32)],
        input_output_aliases={7: 0},
        compiler_params=_params(("arbitrary",)))(proj, sv["g128"], sv["b128"], dgh, dbh, w["a_log_pad"],
                                                 w["dt_bias_pad"], dproj)
    grads["w_in_pad_t"] = _mm("mix_in_bw", dproj, sv["hn"], "tn", F32)

    def pre_b(dhn, xb, dyb, g):
        dx, dg = _rms_bwd(xb, _rms_stats(xb), g, dhn)
        return dyb + dx, dg

    dx, grads["mix_pre_g"] = _mm_epilogue("mix_in_bx", dproj, w["w_in_pad_t"], [x, dy], w["mix_pre_g"], pre_b, 1, True,
                                          tk=PIN_W // 3)
    return dx, grads, carried_out


def _gdn_pre_b_call(name, fn, proj, dd, conv_w, gones, tb, steps):
    t = proj.shape[0]

    def body(xq, xk, xv, pq, pk, pv, d0, d1, d2, cw, go, oq, ok, ov):
        has_prev = jnp.where(pl.program_id(0) == 0, 0.0, 1.0)
        dd_rows = [jnp.concatenate([dr[h] for h in range(N_HEADS)], axis=1) for dr in (d0, d1, d2)]
        outs = fn(xq[...], xk[...], xv[...], pq[...], pk[...], pv[...], *dd_rows, cw[...], go[...], has_prev)
        for r, val in zip((oq, ok, ov), outs):
            r[...] = val

    return pl.pallas_call(
        body, name=name, grid=(steps,),
        in_specs=_qkv_specs(tb) + [_heads_spec(tb)] * 3 + [pl.BlockSpec(conv_w.shape, lambda i: (0, 0)),
                                                          pl.BlockSpec(gones.shape, lambda i: (0, 0))],
        out_specs=[pl.BlockSpec((tb, GDN_W), lambda i: (i, 0))] * 3,
        out_shape=[jax.ShapeDtypeStruct((t, GDN_W), F32)] * 3,
        compiler_params=_params(("arbitrary",)))(proj, proj, proj, proj, proj, proj, *dd, conv_w, gones)


def _conv_bwd_call(name, proj, dc, conv_w, dproj, tb, steps):
    t = proj.shape[0]
    dcur = [pl.BlockSpec((tb, GDN_W), lambda i: (i, 0))] * 3
    dnext = [pl.BlockSpec((tb, GDN_W), lambda i: (jnp.minimum(i + 1, steps - 1), 0))] * 3

    def body(xq, xk, xv, pq, pk, pv, d0, d1, d2, n0, n1, n2, cw, carry_ref, dx_ref, dw_ref):
        i = pl.program_id(0)
        has_prev = jnp.where(i == 0, 0.0, 1.0)
        has_next = jnp.where(i == steps - 1, 0.0, 1.0)

        @pl.when(i == 0)
        def _():
            dw_ref[...] = jnp.zeros(dw_ref.shape, F32)

        wv = cw[...]
        dws, dxs = [], []
        for j, (xr, pr, dr, nr) in enumerate(((xq, pq, d0, n0), (xk, pk, d1, n1), (xv, pv, d2, n2))):
            x, xp, dcv, dnx = xr[...], pr[...], dr[...], nr[...]
            wj = wv[:, j * GDN_W:(j + 1) * GDN_W]
            row = lax.broadcasted_iota(jnp.int32, x.shape, 0)
            dx = dcv * wj[GDN_CONV - 1:GDN_CONV, :]
            rows_w = [jnp.sum(dcv * x, axis=0, keepdims=True)]
            for s in range(1, GDN_CONV):
                up = jnp.where(row < tb - s, _roll(dcv, tb - s, 0), _roll(dnx, tb - s, 0) * has_next)
                dx = dx + up * wj[GDN_CONV - 1 - s:GDN_CONV - s, :]
                sh = jnp.where(row >= s, _roll(x, s, 0), _roll(xp, s, 0) * has_prev)
                rows_w.append(jnp.sum(dcv * sh, axis=0, keepdims=True))
            dxs.append(dx)
            dws.append(jnp.concatenate(rows_w[::-1], axis=0))
        dx_ref[...] = jnp.concatenate(dxs, axis=1).astype(dx_ref.dtype)
        dw_ref[...] += jnp.concatenate(dws, axis=1)

    return pl.pallas_call(
        body, name=name, grid=(steps,),
        in_specs=_qkv_specs(tb) + dcur + dnext + [pl.BlockSpec(conv_w.shape, lambda i: (0, 0)), ANY_SPEC],
        out_specs=[pl.BlockSpec((tb, 3 * GDN_W), lambda i: (i, PIN_QKV // (3 * GDN_W))),
                   pl.BlockSpec(conv_w.shape, lambda i: (0, 0))],
        out_shape=[jax.ShapeDtypeStruct((t, PIN_W), BF16), jax.ShapeDtypeStruct(conv_w.shape, F32)],
        input_output_aliases={13: 0},
        compiler_params=_params(("arbitrary",)))(proj, proj, proj, proj, proj, proj, *dc, *dc, conv_w, dproj)


def _pad_heads_cols(wm, per_head):
    r = wm.shape[0]
    return jnp.pad(wm.reshape(r, N_HEADS, per_head), ((0, 0), (0, 0), (0, HEAD_LANES - per_head))).reshape(r, MLA_PAD)


def _unpad_heads_cols(wm, per_head):
    r = wm.shape[0]
    return wm.reshape(r, N_HEADS, HEAD_LANES)[:, :, :per_head].reshape(r, N_HEADS * per_head)


W_IN_COLS = MLA_Q_RANK + MLA_KV_RANK + MLA_ROPE + 3 * GDN_W + 2 * N_HEADS + GDN_W
W_IN_SHARD = W_IN_COLS // N_SHARD
W_IN_SHARD_PAD = 640
_Q0 = MLA_Q_RANK + MLA_KV_RANK
_Q1 = _Q0 + MLA_ROPE
_Q2 = _Q1 + 3 * GDN_W
_Q3 = _Q2 + 2 * N_HEADS
W_IN_SEGMENTS = [(0, _Q0, PIN_MLA), (_Q0, _Q1, PIN_KPE + MLA_NOPE), (_Q1, _Q2, PIN_QKV), (_Q2, _Q3, PIN_AB),
                 (_Q3, W_IN_COLS, PIN_GATE)]


def _win_pad_t(slabs):
    d = slabs.shape[2]
    pieces, at = [], 0
    for c0, c1, r0 in sorted(W_IN_SEGMENTS, key=lambda s: s[2]):
        if r0 > at:
            pieces.append(jnp.zeros((r0 - at, d), slabs.dtype))
        for q in range(N_SHARD):
            lo, hi = max(c0, q * W_IN_SHARD), min(c1, (q + 1) * W_IN_SHARD)
            if lo < hi:
                pieces.append(slabs[q, lo - q * W_IN_SHARD:hi - q * W_IN_SHARD])
        at = r0 + c1 - c0
    pieces.append(jnp.zeros((PIN_W - at, d), slabs.dtype))
    return jnp.concatenate(pieces, axis=0)


def _win_cols_t(wp_t, c_lo, c_hi):
    pieces = []
    for c0, c1, r0 in W_IN_SEGMENTS:
        lo, hi = max(c0, c_lo), min(c1, c_hi)
        if lo < hi:
            pieces.append(wp_t[r0 + lo - c0:r0 + hi - c0])
    return jnp.concatenate(pieces, axis=0)


def _wkv_to_pad(wkv):
    r = wkv.shape[0]
    w3 = wkv.reshape(r, N_HEADS, MLA_NOPE + MLA_V)
    kpart = jnp.pad(w3[:, :, :MLA_NOPE], ((0, 0), (0, 0), (0, HEAD_LANES - MLA_NOPE))).reshape(r, MLA_PAD)
    vpart = jnp.pad(w3[:, :, MLA_NOPE:], ((0, 0), (0, 0), (0, HEAD_LANES - MLA_V))).reshape(r, MLA_PAD)
    return jnp.concatenate([kpart, vpart], axis=1)


def _wkv_from_pad(wp):
    r = wp.shape[0]
    kpart = wp[:, :MLA_PAD].reshape(r, N_HEADS, HEAD_LANES)[:, :, :MLA_NOPE]
    vpart = wp[:, MLA_PAD:].reshape(r, N_HEADS, HEAD_LANES)[:, :, :MLA_V]
    return jnp.concatenate([kpart, vpart], axis=2).reshape(r, N_HEADS * (MLA_NOPE + MLA_V))


def _wout_to_pad(wo):
    n = wo.shape[1]
    mla = jnp.pad(wo[:N_HEADS * MLA_V].reshape(N_HEADS, MLA_V, n), ((0, 0), (0, HEAD_LANES - MLA_V), (0, 0)))
    return jnp.concatenate([mla.reshape(MLA_PAD, n), wo[N_HEADS * MLA_V:]], axis=0)


def _wout_from_pad(wp):
    n = wp.shape[1]
    mla = wp[:MLA_PAD].reshape(N_HEADS, HEAD_LANES, n)[:, :MLA_V].reshape(N_HEADS * MLA_V, n)
    return jnp.concatenate([mla, wp[MLA_PAD:]], axis=0)


def _pad_lanes(v, n):
    return jnp.pad(v, ((0, 0), (0, n - v.shape[1])))


def _compute_weights(full):
    w = {}
    for n in FFN_BIG:
        if n in full:
            w[n] = full[n].astype(MM_DTYPE)
    w["w_in_pad_t"] = _win_pad_t(full["w_in"]).astype(MM_DTYPE)
    w["w_uq_pad"] = _pad_heads_cols(full["mla_w_uq"], MLA_NOPE + MLA_ROPE).astype(MM_DTYPE)
    w["w_kv_pad"] = _wkv_to_pad(full["mla_w_ukv"]).astype(MM_DTYPE)
    w["w_out_pad"] = _wout_to_pad(full["w_out"]).astype(MM_DTYPE)
    w["conv_w"] = full["gdn_conv_w"].astype(F32)
    for n in ("ffn1_pre_g", "ffn1_post_g", "mix_pre_g", "mla_q_norm_g", "mla_kv_norm_g", "gdn_norm_g", "mix_post_g",
              "ffn2_pre_g", "ffn2_post_g"):
        w[n] = full[n]
    w["mla_out_g_pad"] = _pad_heads_cols(full["mla_out_g"], MLA_V)
    w["a_log_pad"] = _pad_lanes(full["gdn_a_log"], HEAD_LANES)
    w["dt_bias_pad"] = _pad_lanes(full["gdn_dt_bias"], HEAD_LANES)
    return w


FFN2_BIG = FFN_BIG[3:]


def _local_step(x, positions, loss_target, full, late=None):
    t, d = x.shape
    tb = min(512, t)
    tm = min(1024, t)
    tk = min(2048, t)
    w = _compute_weights(full)
    ffn = lambda tag: (w[tag + "_pre_g"], w[tag + "_w_gate"], w[tag + "_w_up"], w.get(tag + "_w_down"),
                       w[tag + "_post_g"])
    x1, sv1, w["ffn1_w_down"] = _ffn_fwd("ffn1", x, *ffn("ffn1"), tm,
                                         carried_up=_carried_gather([late[3]]) if late else None)
    x2, svm, gathered = _mixer_fwd(x1, positions, w, tb, _carried_gather(late[0][:2]) if late else None,
                                   _carried_gather(late[0][2:]) if late else None)
    for n, gw in zip(FFN2_BIG, gathered):
        w[n] = gw
    (dy, lsum), sv2, _ = _ffn_fwd("ffn2", x2, *ffn("ffn2"), tm, loss_target)
    g = {}
    dx2, g["ffn2_pre_g"], g["ffn2_w_gate"], g["ffn2_w_up"], g["ffn2_w_down"], g["ffn2_post_g"] = _ffn_bwd(
        "ffn2", dy, sv2, *ffn("ffn2"), tm, tk)[:6]

    def pair_sums(arrs, tag):
        got = _swap_halves(arrs, tag)
        return [_add_pair("add_pair%s_%d" % (tag, i), gi, gt, late[1]) for i, (gi, gt) in enumerate(zip(arrs, got))]

    def chip_sums(pairs, slabs, tag):
        return [_add_chips("add_chips%s_%d" % (tag, i), pr, sl, late[2]) for i, (pr, sl) in enumerate(zip(pairs, slabs))]

    if late:
        pairs2 = pair_sums([g[n] for n in FFN2_BIG], "_ffn2")
        dx1, gm, slabs2 = _mixer_bwd(dx2, svm, w, tb, _carried_scatter(pairs2))
        for n, hs in zip(FFN2_BIG, chip_sums(pairs2, slabs2, "_ffn2")):
            g[n] = hs
    else:
        dx1, gm, _ = _mixer_bwd(dx2, svm, w, tb)
    g["w_in"] = jnp.stack([jnp.pad(_win_cols_t(gm["w_in_pad_t"], q * W_IN_SHARD, (q + 1) * W_IN_SHARD),
                                   ((0, W_IN_SHARD_PAD - W_IN_SHARD), (0, 0))) for q in range(N_SHARD)])
    g["mla_w_uq"] = _unpad_heads_cols(gm["w_uq_pad"], MLA_NOPE + MLA_ROPE)
    g["mla_w_ukv"] = _wkv_from_pad(gm["w_kv_pad"])
    g["gdn_conv_w"] = gm["conv_w"]
    g["w_out"] = _wout_from_pad(gm["w_out_pad"])
    if late:
        quarters = [_pack([jnp.split(g[n], N_SHARD, axis=SHARD_AXIS[n])[q] for n in MIX_BIG], MM_DTYPE)
                    for q in range(N_SHARD)]
        pairs_m = pair_sums([g["w_in"].astype(MM_DTYPE), jnp.stack(quarters)], "_mix")
        pairs_d, pairs_gu = [], []

        def make_mid(dwd):
            pairs_d.extend(pair_sums([dwd], "_ffn1d"))
            return _carried_scatter(pairs_d)

        def make_up(dwg, dwu):
            pairs_gu.extend(pair_sums([dwg, dwu], "_ffn1"))
            return _carried_scatter(pairs_gu)

        dx0, g["ffn1_pre_g"], _, _, _, g["ffn1_post_g"], slabs_m, slabs_d, slabs_gu = _ffn_bwd(
            "ffn1", dx1, sv1, *ffn("ffn1"), tm, tk, _carried_scatter(pairs_m), make_mid, make_up)
        g["ffn1_w_gate"], g["ffn1_w_up"] = chip_sums(pairs_gu, slabs_gu, "_ffn1")
        g["ffn1_w_down"] = chip_sums(pairs_d, slabs_d, "_ffn1d")[0]
        g["w_in"], g["mix_pack"] = chip_sums(pairs_m, slabs_m, "_mix")
    else:
        dx0, g["ffn1_pre_g"], g["ffn1_w_gate"], g["ffn1_w_up"], g["ffn1_w_down"], g["ffn1_post_g"] = _ffn_bwd(
            "ffn1", dx1, sv1, *ffn("ffn1"), tm, tk)[:6]
    g["mix_pre_g"], g["mix_post_g"] = gm["mix_pre_g"], gm["mix_post_g"]
    g["mla_q_norm_g"], g["mla_kv_norm_g"] = gm["mla_q_norm_g"], gm["mla_kv_norm_g"]
    g["gdn_norm_g"] = gm["gdn_norm_g"]
    g["mla_out_g"] = _unpad_heads_cols(gm["mla_out_g_pad"], MLA_V)
    g["gdn_a_log"] = gm["a_log_pad"][:, :N_HEADS]
    g["gdn_dt_bias"] = gm["dt_bias_pad"][:, :N_HEADS]
    return lsum, dx0, g


HBM_SPEC = pl.BlockSpec(memory_space=pltpu.HBM)


def _place():
    return lax.axis_index("x"), lax.axis_index("y"), lax.axis_index("c")


def _exchange_call(name, body, ins, out_shapes, n_remote, n_local):
    return pl.pallas_call(
        body, name=name, in_specs=[HBM_SPEC] * len(ins), out_specs=[HBM_SPEC] * len(out_shapes), out_shape=out_shapes,
        scratch_shapes=[pltpu.SemaphoreType.DMA((n_remote,)), pltpu.SemaphoreType.DMA((n_remote,)),
                        pltpu.SemaphoreType.DMA((n_local,))])(*ins)


def _other_chips(x, y):
    return [(1 - x, y), (x, 1 - y), (1 - x, 1 - y)]


def _at_each_chip(fn):
    x, y, _ = _place()
    for cx in range(2):
        for cy in range(2):
            pl.when((x == cx) & (y == cy))(functools.partial(fn, cx, cy))


def _at_each_device(fn):
    x, y, c = _place()
    for cx in range(2):
        for cy in range(2):
            for cc in range(2):
                pl.when((x == cx) & (y == cy) & (c == cc))(functools.partial(fn, cx, cy, cc))


def _at_each_core(fn):
    c = lax.axis_index("c")
    for cc in range(2):
        pl.when(c == cc)(functools.partial(fn, cc))


def _gather_shards(ws):
    nw = len(ws)

    def body(*refs):
        w_refs, out_refs = refs[:nw], refs[nw:2 * nw]
        send_sems, recv_sems, local_sems = refs[2 * nw:]

        def run(x, y, c):
            chips = _other_chips(x, y)
            me, sibling = 2 * x + y, (x, y, 1 - c)

            def half(ref, which):
                hr = ref.shape[0] // 2
                return ref.at[pl.ds(which * hr, hr)]

            def over_ici(i, j, src, slab, to):
                return pltpu.make_async_remote_copy(
                    src_ref=half(src, c), dst_ref=half(out_refs[i].at[slab], c), send_sem=send_sems.at[7 * i + j],
                    recv_sem=recv_sems.at[7 * i + j], device_id=to, device_id_type=MESH)

            def over_d2d(i, j, slab, which):
                return pltpu.make_async_remote_copy(
                    src_ref=half(out_refs[i].at[slab], which), dst_ref=half(out_refs[i].at[slab], which),
                    send_sem=send_sems.at[7 * i + 3 + j], recv_sem=recv_sems.at[7 * i + 3 + j], device_id=sibling,
                    device_id_type=MESH)

            def own(i, w_ref):
                return pltpu.make_async_remote_copy(
                    src_ref=w_ref, dst_ref=out_refs[i].at[me], send_sem=send_sems.at[7 * i + 6],
                    recv_sem=recv_sems.at[7 * i + 6], device_id=sibling, device_id_type=MESH)

            sends, passed = [], []
            for i, w_ref in enumerate(w_refs):
                for j, (px, py) in enumerate(chips):
                    sends.append(over_ici(i, j, w_ref, me, (px, py, c)))
                    sends[-1].start()
            for i, w_ref in enumerate(w_refs):
                sends.append(own(i, w_ref))
                sends[-1].start()
            for i, w_ref in enumerate(w_refs):
                for j, (px, py) in enumerate(chips):
                    over_ici(i, j, w_ref, 2 * px + py, (px, py, c)).wait_recv()
                    passed.append(over_d2d(i, j, 2 * px + py, c))
                    passed[-1].start()
            for i, w_ref in enumerate(w_refs):
                own(i, w_ref).wait_recv()
                for j, (px, py) in enumerate(chips):
                    over_d2d(i, j, 2 * px + py, 1 - c).wait_recv()
            for cp in sends + passed:
                cp.wait_send()

        _at_each_device(run)

    outs = [jax.ShapeDtypeStruct((N_SHARD,) + w.shape, w.dtype) for w in ws]
    return _exchange_call("gather_weight_shards", body, ws, outs, 7 * nw, 1)


def _swap_halves(gs, tag=""):
    ng = len(gs)

    def body(*refs):
        g_refs, got_refs = refs[:ng], refs[ng:2 * ng]
        send_sems, recv_sems, _ = refs[2 * ng:]
        x, y, _ = _place()

        def run(c):
            sends = []
            for i, (g_ref, got_ref) in enumerate(zip(g_refs, got_refs)):
                hr = got_ref.shape[1]
                sends.append(pltpu.make_async_remote_copy(
                    src_ref=g_ref.at[:, pl.ds((1 - c) * hr, hr)], dst_ref=got_ref, send_sem=send_sems.at[i],
                    recv_sem=recv_sems.at[i], device_id=(x, y, 1 - c), device_id_type=MESH))
                sends[-1].start()
            for cp in sends:
                cp.wait()

        _at_each_core(run)

    halves = [jax.ShapeDtypeStruct((g.shape[0], g.shape[1] // 2, g.shape[2]), g.dtype) for g in gs]
    return _exchange_call("swap_grad_halves" + tag, body, gs, halves, ng, 1)


def _scatter_copies(p_refs, out_refs, send_sems, recv_sems, x, y):
    c = lax.axis_index("c")
    copies = []
    for i, (p_ref, out_ref) in enumerate(zip(p_refs, out_refs)):
        for j, (px, py) in enumerate(_other_chips(x, y)):
            copies.append(pltpu.make_async_remote_copy(
                src_ref=p_ref.at[2 * px + py], dst_ref=out_ref.at[j], send_sem=send_sems.at[3 * i + j],
                recv_sem=recv_sems.at[3 * i + j], device_id=(px, py, c), device_id_type=MESH))
    return copies


def _start_all(make, *refs):
    def run(x, y):
        for cp in make(*refs, x, y):
            cp.start()

    _at_each_chip(run)


def _wait_all(make, *refs):
    def run(x, y):
        copies = make(*refs, x, y)
        for cp in copies:
            cp.wait_recv()
        for cp in copies:
            cp.wait_send()

    _at_each_chip(run)


def _scatter_shapes(ps):
    return [jax.ShapeDtypeStruct((3,) + p.shape[1:], p.dtype) for p in ps]


def _carried_scatter(ps):
    return _Carried(ps, _scatter_shapes(ps), 3 * len(ps), functools.partial(_start_all, _scatter_copies),
                    functools.partial(_wait_all, _scatter_copies))


def _direct_gather_copies(w_refs, out_refs, send_sems, recv_sems, x, y, arriving):
    c = lax.axis_index("c")
    me = 2 * x + y
    peers = [((px, py, c), 2 * px + py) for px, py in _other_chips(x, y)] + [((x, y, 1 - c), me)]
    copies = []
    for i, (w_ref, out_ref) in enumerate(zip(w_refs, out_refs)):
        for j, (peer, slab) in enumerate(peers):
            copies.append(pltpu.make_async_remote_copy(
                src_ref=w_ref, dst_ref=out_ref.at[slab if arriving else me], send_sem=send_sems.at[4 * i + j],
                recv_sem=recv_sems.at[4 * i + j], device_id=peer, device_id_type=MESH))
    return copies


def _carried_gather(ws):
    def start(w_refs, out_refs, send_sems, recv_sems):
        def run(x, y):
            for cp in _direct_gather_copies(w_refs, out_refs, send_sems, recv_sems, x, y, False):
                cp.start()

        _at_each_chip(run)

    def finish(w_refs, out_refs, send_sems, recv_sems):
        def run(x, y):
            for cp in _direct_gather_copies(w_refs, out_refs, send_sems, recv_sems, x, y, True):
                cp.wait_recv()
            for cp in _direct_gather_copies(w_refs, out_refs, send_sems, recv_sems, x, y, False):
                cp.wait_send()

        _at_each_chip(run)

    outs = [jax.ShapeDtypeStruct((N_SHARD,) + w.shape, w.dtype) for w in ws]
    return _Carried(ws, outs, 4 * len(ws), start, finish)


def _share_halves(hs):
    n = len(hs)

    def body(*refs):
        h_refs, out_refs = refs[:n], refs[n:2 * n]
        send_sems, recv_sems, _ = refs[2 * n:]
        x, y, c = _place()
        sends = []
        for i, (h_ref, out_ref) in enumerate(zip(h_refs, out_refs)):
            sends.append(pltpu.make_async_remote_copy(
                src_ref=h_ref, dst_ref=out_ref, send_sem=send_sems.at[i], recv_sem=recv_sems.at[i],
                device_id=(x, y, 1 - c), device_id_type=MESH))
            sends[-1].start()
        for cp in sends:
            cp.wait()

    outs = [jax.ShapeDtypeStruct(h.shape, h.dtype) for h in hs]
    return _exchange_call("share_grad_halves", body, hs, outs, n, 1)


def _scalar_grid_call(name, body, scalars, grid, in_specs, out_specs, out_shape, args):
    grid_spec = pltpu.PrefetchScalarGridSpec(num_scalar_prefetch=len(scalars), grid=grid, in_specs=in_specs,
                                             out_specs=out_specs)
    return pl.pallas_call(body, name=name, grid_spec=grid_spec, out_shape=out_shape,
                          compiler_params=_params(("arbitrary",) * len(grid)))(*scalars, *args)


def _add_pair(name, g, got, core):
    ns_, hr, cols = got.shape
    th = _row_tile(hr, 512)
    nb = hr // th

    def body(core_ref, g_ref, got_ref, out_ref):
        out_ref[...] = (g_ref[...].astype(F32) + got_ref[...].astype(F32)).astype(out_ref.dtype)

    blk = pl.BlockSpec((1, th, cols), lambda q, j, core_ref: (q, j, 0))
    own = pl.BlockSpec((1, th, cols), lambda q, j, core_ref: (q, core_ref[0] * nb + j, 0))
    return _scalar_grid_call(name, body, [core], (ns_, nb), [own, blk], blk,
                             jax.ShapeDtypeStruct(got.shape, got.dtype), [g, got])


def _add_chips(name, pairs, slabs, chip):
    _, hr, cols = slabs.shape
    th = _row_tile(hr, 512)

    def body(chip_ref, own_ref, s0_ref, s1_ref, s2_ref, out_ref):
        total = own_ref[0].astype(F32) + s0_ref[0].astype(F32)
        out_ref[...] = (total + s1_ref[0].astype(F32)) + s2_ref[0].astype(F32)

    own = pl.BlockSpec((1, th, cols), lambda j, chip_ref: (chip_ref[0], j, 0))
    others = [pl.BlockSpec((1, th, cols), lambda j, chip_ref, k=k: (k, j, 0)) for k in range(3)]
    return _scalar_grid_call(name, body, [chip], (hr // th,), [own] + others,
                             pl.BlockSpec((th, cols), lambda j, chip_ref: (j, 0)),
                             jax.ShapeDtypeStruct((hr, cols), F32), [pairs, slabs, slabs, slabs])


def _join_halves(name, mine, other, core):
    hr, cols = mine.shape
    th = _row_tile(hr, 512)
    nb = hr // th

    def body(core_ref, mine_ref, other_ref, out_ref):
        is_mine = pl.program_id(0) == core_ref[0]

        @pl.when(is_mine)
        def _():
            out_ref[0] = mine_ref[...]

        @pl.when(jnp.logical_not(is_mine))
        def _():
            out_ref[0] = other_ref[...]

    blk = pl.BlockSpec((th, cols), lambda h, j, core_ref: (j, 0))
    return _scalar_grid_call(name, body, [core], (2, nb), [blk, blk],
                             pl.BlockSpec((1, th, cols), lambda h, j, core_ref: (0, h * nb + j, 0)),
                             jax.ShapeDtypeStruct((1, 2 * hr, cols), mine.dtype), [mine, other])


def _gather_small(sp):
    def body(s_ref, out_ref, send_sems, recv_sems, local_sem):
        x, y, c = _place()
        me = 4 * x + 2 * y + c
        peers = [(x ^ (m >> 2), y ^ ((m >> 1) & 1), c ^ (m & 1)) for m in range(1, 8)]
        mine = pltpu.make_async_copy(s_ref, out_ref.at[me], local_sem)
        mine.start()
        sends = [pltpu.make_async_remote_copy(src_ref=s_ref, dst_ref=out_ref.at[me], send_sem=send_sems.at[j],
                                              recv_sem=recv_sems.at[j], device_id=p, device_id_type=MESH)
                 for j, p in enumerate(peers)]
        for cp in sends:
            cp.start()
        for j, (px, py, pc) in enumerate(peers):
            pltpu.make_async_remote_copy(src_ref=s_ref, dst_ref=out_ref.at[4 * px + 2 * py + pc],
                                         send_sem=send_sems.at[j], recv_sem=recv_sems.at[j], device_id=(px, py, pc),
                                         device_id_type=MESH).wait_recv()
        for cp in sends:
            cp.wait_send()
        mine.wait()

    return pl.pallas_call(
        body, name="gather_small_grads", in_specs=[HBM_SPEC], out_specs=HBM_SPEC,
        out_shape=jax.ShapeDtypeStruct((8,) + sp.shape, sp.dtype),
        scratch_shapes=[pltpu.SemaphoreType.DMA((7,)), pltpu.SemaphoreType.DMA((7,)), pltpu.SemaphoreType.DMA])(sp)


def _pack_rows(total):
    rows = -(-total // LANES)
    return -(-rows // 32) * 32


def _pack(arrs, dtype):
    flat = jnp.concatenate([a.reshape(-1).astype(dtype) for a in arrs])
    rows = _pack_rows(flat.shape[0])
    return jnp.pad(flat, (0, rows * LANES - flat.shape[0])).reshape(rows, LANES)


def _unpack(buf, shapes):
    flat = buf.reshape(-1)
    out, off = {}, 0
    for n, shp in shapes:
        size = shp[0] * shp[1]
        out[n] = flat[off:off + size].reshape(shp)
        off += size
    return out


def _to_wire(name, w3):
    _, r, cols = w3.shape
    tb = _row_tile(r, 512)

    def body(w_ref, o_ref):
        o_ref[...] = w_ref[0].astype(o_ref.dtype)

    return pl.pallas_call(
        body, name=name, grid=(r // tb,), in_specs=[pl.BlockSpec((1, tb, cols), lambda i: (0, i, 0))],
        out_specs=pl.BlockSpec((tb, cols), lambda i: (i, 0)), out_shape=jax.ShapeDtypeStruct((r, cols), MM_DTYPE),
        compiler_params=_params(("arbitrary",)))(w3)


def _adamw(name, w3, g, m3, v3, tb):
    c1 = 1.0 - ADAM_B1 ** ADAM_STEP
    c2 = 1.0 - ADAM_B2 ** ADAM_STEP
    _, r, cols = w3.shape
    emit = g.ndim == 2
    blk3 = pl.BlockSpec((1, tb, cols), lambda i: (0, i, 0))
    g_spec = pl.BlockSpec((tb, cols), lambda i: (i, 0)) if emit else blk3

    def body(w_ref, g_ref, m_ref, v_ref, *out_refs):
        gb = g_ref[...] if emit else g_ref[0]
        m2 = ADAM_B1 * m_ref[0] + (1.0 - ADAM_B1) * gb
        v2 = ADAM_B2 * v_ref[0] + (1.0 - ADAM_B2) * (gb * gb)
        out_refs[-3][0] = -ADAM_LR * ((m2 / c1) / (jnp.sqrt(v2 / c2) + ADAM_EPS) + ADAM_WD * w_ref[0])
        out_refs[-2][0] = m2
        out_refs[-1][0] = v2
        if emit:
            out_refs[0][0] = gb

    n_out = 4 if emit else 3
    outs = pl.pallas_call(
        body, name=name, grid=(r // tb,), in_specs=[blk3, g_spec, blk3, blk3], out_specs=[blk3] * n_out,
        out_shape=[jax.ShapeDtypeStruct((1, r, cols), F32)] * n_out,
        compiler_params=_params(("arbitrary",)))(w3, g, m3, v3)
    return outs if emit else [g] + list(outs)


def _row_tile(rows, pref):
    if rows <= pref:
        return rows
    t = pref
    while t >= 8:
        if rows % t == 0 and t % 8 == 0:
            return t
        t -= 8
    return rows


def kernel(x, positions, ffn1_pre_g, ffn1_w_gate, ffn1_w_up, ffn1_w_down, ffn1_post_g, mix_pre_g, w_in, mla_q_norm_g, mla_w_uq, mla_kv_norm_g, mla_w_ukv, mla_out_g, gdn_conv_w, gdn_a_log, gdn_dt_bias, gdn_norm_g, w_out, mix_post_g, ffn2_pre_g, ffn2_w_gate, ffn2_w_up, ffn2_w_down, ffn2_post_g, loss_target, m_ffn1_pre_g, m_ffn1_w_gate, m_ffn1_w_up, m_ffn1_w_down, m_ffn1_post_g, m_mix_pre_g, m_w_in, m_mla_q_norm_g, m_mla_w_uq, m_mla_kv_norm_g, m_mla_w_ukv, m_mla_out_g, m_gdn_conv_w, m_gdn_a_log, m_gdn_dt_bias, m_gdn_norm_g, m_w_out, m_mix_post_g, m_ffn2_pre_g, m_ffn2_w_gate, m_ffn2_w_up, m_ffn2_w_down, m_ffn2_post_g, v_ffn1_pre_g, v_ffn1_w_gate, v_ffn1_w_up, v_ffn1_w_down, v_ffn1_post_g, v_mix_pre_g, v_w_in, v_mla_q_norm_g, v_mla_w_uq, v_mla_kv_norm_g, v_mla_w_ukv, v_mla_out_g, v_gdn_conv_w, v_gdn_a_log, v_gdn_dt_bias, v_gdn_norm_g, v_w_out, v_mix_post_g, v_ffn2_pre_g, v_ffn2_w_gate, v_ffn2_w_up, v_ffn2_w_down, v_ffn2_post_g):
    args = dict(locals())
    wsh = {n: args[n][0] for n in WEIGHTS}
    msh = {n: args["m_" + n] for n in SMALL}
    vsh = {n: args["v_" + n] for n in SMALL}
    for n in SMALL:
        wsh[n] = args[n]
    mix_shapes = [(n, wsh[n].shape) for n in MIX_BIG]

    early = FFN_BIG[:2]
    held = lambda a, n: jnp.swapaxes(a, 1, 2) if n in TRANSPOSED else a
    w_in_wire = jnp.pad(held(w_in, "w_in")[0].astype(MM_DTYPE), ((0, W_IN_SHARD_PAD - W_IN_SHARD), (0, 0)))
    gathered = _gather_shards([_to_wire("wire_" + n, held(args[n], n)) for n in early]
                              + [w_in_wire, _pack([wsh[n] for n in MIX_BIG], MM_DTYPE)])
    full = {n: wsh[n] for n in SMALL}
    for n, gw in zip(early + ["w_in"], gathered):
        full[n] = gw
    parts = [_unpack(gathered[-1][q], mix_shapes) for q in range(N_SHARD)]
    for n in MIX_BIG:
        full[n] = jnp.concatenate([parts[q][n] for q in range(N_SHARD)], axis=SHARD_AXIS[n])

    core = lax.axis_index("c").astype(jnp.int32).reshape(1)
    chip = (2 * lax.axis_index("x") + lax.axis_index("y")).astype(jnp.int32).reshape(1)
    late = ([_to_wire("wire_" + n, held(args[n], n)) for n in FFN2_BIG], core, chip,
            _to_wire("wire_ffn1_w_down", ffn1_w_down))
    lsum, grad_x, g = _local_step(x[0], positions, loss_target[0], full, late)
    loss = lax.psum(0.5 * jnp.sum(lsum) / x.shape[-1], ("x", "y", "c"))

    halves = [g[n] for n in FFN_BIG] + [g["w_in"], g["mix_pack"]]
    others = _share_halves(halves)
    shared = [_join_halves("join_halves_%d" % i, hm, ho, core) for i, (hm, ho) in enumerate(zip(halves, others))]
    gsh = _unpack(shared[-1], mix_shapes)
    for n, sg_ in zip(FFN_BIG, shared):
        gsh[n] = sg_
    gsh["w_in"] = shared[-2][:, :W_IN_SHARD]

    small_shapes = [(n, wsh[n].shape) for n in SMALL]
    pack_small = lambda d: jnp.concatenate(
        [_pad_lanes(d[n].astype(F32), LANES) for n in SMALL] + [jnp.zeros((SMALL_ROWS - len(SMALL), LANES), F32)], axis=0)
    slots = _gather_small(pack_small(g))

    c1 = 1.0 - ADAM_B1 ** ADAM_STEP
    c2 = 1.0 - ADAM_B2 ** ADAM_STEP

    def small_update(wb, mb, vb, s8):
        gs = s8[0:SMALL_ROWS]
        for d in range(1, 8):
            gs = gs + s8[d * SMALL_ROWS:(d + 1) * SMALL_ROWS]
        m2 = ADAM_B1 * mb + (1.0 - ADAM_B1) * gs
        v2 = ADAM_B2 * vb + (1.0 - ADAM_B2) * (gs * gs)
        delta = -ADAM_LR * ((m2 / c1) / (jnp.sqrt(v2 / c2) + ADAM_EPS) + ADAM_WD * wb)
        return gs, delta, m2, v2

    sg, sd, sm, sv_ = _rowwise("adamw_small", small_update,
                               [pack_small(wsh), pack_small(msh), pack_small(vsh)],
                               [slots.reshape(8 * SMALL_ROWS, LANES)], [(LANES, F32)] * 4, [], SMALL_ROWS)
    grads, deltas, new_m, new_v = {}, {}, {}, {}
    for i, (n, shp) in enumerate(small_shapes):
        grads[n], deltas[n] = sg[i:i + 1, :shp[1]], sd[i:i + 1, :shp[1]]
        new_m[n], new_v[n] = sm[i:i + 1, :shp[1]], sv_[i:i + 1, :shp[1]]
    for n in BIG:
        w3 = held(args[n], n)
        outs = _adamw("adamw_" + n, w3, gsh[n], held(args["m_" + n], n), held(args["v_" + n], n),
                      _row_tile(w3.shape[1], 256))
        grads[n], deltas[n], new_m[n], new_v[n] = [held(o, n) for o in outs]

    return (loss, grad_x[None], *[grads[n] for n in WEIGHTS], *[deltas[n] for n in WEIGHTS],
            *[new_m[n] for n in WEIGHTS], *[new_v[n] for n in WEIGHTS])
```

```python
import functools

import jax
import jax.numpy as jnp
from jax import lax
from jax.experimental import pallas as pl
from jax.experimental.pallas import tpu as pltpu

F32 = jnp.float32
BF16 = jnp.bfloat16
MM_DTYPE = BF16
MESH = pl.DeviceIdType.MESH

D_MODEL = 1024
D_FF = 2816
N_HEADS = 8
MLA_Q_RANK = 256
MLA_KV_RANK = 128
MLA_NOPE = 64
MLA_ROPE = 32
MLA_V = 64
ROPE_THETA = 10000.0
GDN_DH = 64
GDN_W = N_HEADS * GDN_DH
GDN_CONV = 4
CHUNK = 64
GDN_STEP_CHUNKS = 4
HEAD_LANES = 128
HEADS_PER_STEP = 4
MLA_PAD = N_HEADS * HEAD_LANES
EPS = 1e-6
N_SHARD = 4
LANES = 1024

PIN_QKV = 0
PIN_MLA = 1536
PIN_KPE = 1920
PIN_GATE = 2048
PIN_AB = 2560
PIN_W = 2688
CAT_W = MLA_PAD + GDN_W

ADAM_LR = 0.001
ADAM_B1 = 0.9
ADAM_B2 = 0.999
ADAM_EPS = 1e-08
ADAM_WD = 0.01
ADAM_STEP = 10

VMEM_LIMIT_V7X = 56 * 1024 * 1024

BIG = ["ffn1_w_gate", "ffn1_w_up", "ffn1_w_down", "w_in", "mla_w_uq", "mla_w_ukv", "gdn_conv_w", "w_out",
       "ffn2_w_gate", "ffn2_w_up", "ffn2_w_down"]
FFN_BIG = ["ffn1_w_gate", "ffn1_w_up", "ffn1_w_down", "ffn2_w_gate", "ffn2_w_up", "ffn2_w_down"]
TRANSPOSED = ["ffn1_w_gate", "ffn1_w_up", "ffn2_w_gate", "ffn2_w_up", "w_in"]
MIX_BIG = ["mla_w_uq", "mla_w_ukv", "gdn_conv_w", "w_out"]
SMALL = ["ffn1_pre_g", "ffn1_post_g", "mix_pre_g", "mla_q_norm_g", "mla_kv_norm_g", "mla_out_g", "gdn_a_log",
         "gdn_dt_bias", "gdn_norm_g", "mix_post_g", "ffn2_pre_g", "ffn2_post_g"]
WEIGHTS = ["ffn1_pre_g", "ffn1_w_gate", "ffn1_w_up", "ffn1_w_down", "ffn1_post_g", "mix_pre_g", "w_in",
           "mla_q_norm_g", "mla_w_uq", "mla_kv_norm_g", "mla_w_ukv", "mla_out_g", "gdn_conv_w", "gdn_a_log",
           "gdn_dt_bias", "gdn_norm_g", "w_out", "mix_post_g", "ffn2_pre_g", "ffn2_w_gate", "ffn2_w_up",
           "ffn2_w_down", "ffn2_post_g"]
SHARD_AXIS = {"ffn1_w_gate": 1, "ffn1_w_up": 1, "ffn1_w_down": 0, "w_in": 1, "mla_w_uq": 1, "mla_w_ukv": 1,
              "gdn_conv_w": 1, "w_out": 0, "ffn2_w_gate": 1, "ffn2_w_up": 1, "ffn2_w_down": 0}
SMALL_ROWS = 16


def _params(sem):
    return pltpu.CompilerParams(dimension_semantics=sem, vmem_limit_bytes=VMEM_LIMIT_V7X)


def _pick(dim, pref):
    if dim <= pref:
        return dim
    t = (pref // 128) * 128
    while t >= 128:
        if dim % t == 0:
            return t
        t -= 128
    return dim


ANY_SPEC = pl.BlockSpec(memory_space=pl.ANY)


def _rowwise(name, fn, row_ins, bc_ins, row_outs, acc_outs, tb, wide=None, carry=None):
    ents = []
    for e in row_ins:
        ents.append(e if isinstance(e, tuple) else (e, e.shape[1], 0, 0))
    over = [o[2] for o in row_outs if len(o) == 3]
    rows = over[0] if over else ents[0][0].shape[0]
    steps = rows // tb
    assert steps * tb == rows, (name, rows, tb)
    in_specs, args = [], []
    for a, w, j, r0 in ents:
        in_specs.append(pl.BlockSpec((tb, w), lambda i, j=j, r0=r0: (i + r0, j)))
        args.append(a)
    for b in bc_ins:
        in_specs.append(pl.BlockSpec(b.shape, lambda i: (0, 0)))
        args.append(b)
    n_in = len(args)
    aliases = {}
    if carry is not None:
        in_specs.append(ANY_SPEC)
        args.append(carry)
        aliases = {n_in: 0}
    out_shape = [jax.ShapeDtypeStruct((rows, o[0]), o[1]) for o in row_outs]
    out_specs = [pl.BlockSpec((tb, o[0]), lambda i: (i, 0)) for o in row_outs]
    if wide is not None:
        out_shape[0] = jax.ShapeDtypeStruct((rows, wide[0]), row_outs[0][1])
        out_specs[0] = pl.BlockSpec((tb, row_outs[0][0]), lambda i: (i, wide[1]))
    out_shape += [jax.ShapeDtypeStruct((r, c), F32) for r, c in acc_outs]
    out_specs += [pl.BlockSpec((r, c), lambda i: (0, 0)) for r, c in acc_outs]
    n_ro, n_acc, n_args = len(row_outs), len(acc_outs), len(args)

    def body(*refs):
        vals = fn(*[r[...] for r in refs[:n_in]])
        if not isinstance(vals, (tuple, list)):
            vals = (vals,)
        for r, v in zip(refs[n_args:n_args + n_ro], vals[:n_ro]):
            r[...] = v.astype(r.dtype)
        if n_acc:
            acc_refs = refs[n_args + n_ro:]

            @pl.when(pl.program_id(0) == 0)
            def _():
                for r in acc_refs:
                    r[...] = jnp.zeros(r.shape, r.dtype)

            for r, v in zip(acc_refs, vals[n_ro:]):
                r[...] += v

    outs = pl.pallas_call(body, name=name, grid=(steps,), in_specs=in_specs, out_specs=out_specs,
                          out_shape=out_shape, input_output_aliases=aliases,
                          compiler_params=_params(("arbitrary",)))(*args)
    return outs


def _mm(name, a, b, mode, out_dtype, tm=1024, tn=1024, tk=1024):
    if mode == "nn":
        (m, k), (k2, n) = a.shape, b.shape
    elif mode == "nt":
        (m, k), (n, k2) = a.shape, b.shape
    else:
        (k, m), (k2, n) = a.shape, b.shape
    assert k == k2, (name, a.shape, b.shape)
    tm, tn, tk = _pick(m, tm), _pick(n, tn), _pick(k, tk)
    nk = k // tk
    if mode == "nn":
        a_spec = pl.BlockSpec((tm, tk), lambda i, j, kk: (i, kk))
        b_spec = pl.BlockSpec((tk, tn), lambda i, j, kk: (kk, j))
        dims = (((1,), (0,)), ((), ()))
    elif mode == "nt":
        a_spec = pl.BlockSpec((tm, tk), lambda i, j, kk: (i, kk))
        b_spec = pl.BlockSpec((tn, tk), lambda i, j, kk: (j, kk))
        dims = (((1,), (1,)), ((), ()))
    else:
        a_spec = pl.BlockSpec((tk, tm), lambda i, j, kk: (kk, i))
        b_spec = pl.BlockSpec((tk, tn), lambda i, j, kk: (kk, j))
        dims = (((0,), (0,)), ((), ()))

    def body(a_ref, b_ref, o_ref, acc_ref):
        kk = pl.program_id(2)

        @pl.when(kk == 0)
        def _():
            acc_ref[...] = jnp.zeros(acc_ref.shape, F32)

        acc_ref[...] += lax.dot_general(a_ref[...].astype(MM_DTYPE), b_ref[...].astype(MM_DTYPE), dims,
                                        preferred_element_type=F32)

        @pl.when(kk == nk - 1)
        def _():
            o_ref[...] = acc_ref[...].astype(o_ref.dtype)

    return pl.pallas_call(
        body, name=name, grid=(m // tm, n // tn, nk), in_specs=[a_spec, b_spec],
        out_specs=pl.BlockSpec((tm, tn), lambda i, j, kk: (i, j)),
        out_shape=jax.ShapeDtypeStruct((m, n), out_dtype),
        scratch_shapes=[pltpu.VMEM((tm, tn), F32)],
        compiler_params=_params(("parallel", "parallel", "arbitrary")))(a, b)


def _prologue_mm(name, row_ins, g, prologue, w, with_gain_grad, tm=1024, tn=1024):
    t, d = row_ins[0].shape
    n = w.shape[0]
    tm, tn = _pick(t, tm), _pick(n, tn)
    n_row = len(row_ins)
    row = pl.BlockSpec((tm, d), lambda i, j: (i, 0))
    vec = pl.BlockSpec((1, d), lambda i, j: (0, 0))

    def body(*refs):
        rows, g_ref, w_ref = refs[:n_row], refs[n_row], refs[n_row + 1]
        lhs_ref, out_ref = refs[n_row + 2], refs[n_row + 3]
        lhs_s = refs[-1]
        i, j = pl.program_id(0), pl.program_id(1)
        if with_gain_grad:
            dg_ref = refs[n_row + 4]

            @pl.when((i == 0) & (j == 0))
            def _():
                dg_ref[...] = jnp.zeros(dg_ref.shape, F32)

        @pl.when(j == 0)
        def _():
            res = prologue(*[r[...] for r in rows], g_ref[...])
            lhs_s[...] = res[0].astype(MM_DTYPE)
            lhs_ref[...] = lhs_s[...]
            if with_gain_grad:
                dg_ref[...] += res[1]

        out_ref[...] = lax.dot_general(lhs_s[...], w_ref[...], (((1,), (1,)), ((), ())), preferred_element_type=F32)

    out_specs = [row, pl.BlockSpec((tm, tn), lambda i, j: (i, j))] + ([vec] if with_gain_grad else [])
    out_shape = [jax.ShapeDtypeStruct((t, d), MM_DTYPE), jax.ShapeDtypeStruct((t, n), F32)]
    out_shape += [jax.ShapeDtypeStruct((1, d), F32)] if with_gain_grad else []
    return pl.pallas_call(
        body, name=name, grid=(t // tm, n // tn),
        in_specs=[row] * n_row + [vec, pl.BlockSpec((tn, d), lambda i, j: (j, 0))], out_specs=out_specs,
        out_shape=out_shape, scratch_shapes=[pltpu.VMEM((tm, d), MM_DTYPE)],
        compiler_params=_params(("arbitrary", "arbitrary")))(*row_ins, g, w)


def _mm_epilogue(name, a, w, row_ins, g, epilogue, n_row_out, with_gain_grad, tm=1024, tk=1024):
    t, k = a.shape
    d = w.shape[1]
    tm, tk = _pick(t, tm), _pick(k, tk)
    nk = k // tk
    n_row = len(row_ins)
    row = pl.BlockSpec((tm, d), lambda i, kk: (i, 0))
    vec = pl.BlockSpec((1, d), lambda i, kk: (0, 0))

    def body(*refs):
        a_ref, w_ref = refs[0], refs[1]
        rows, g_ref = refs[2:2 + n_row], refs[2 + n_row]
        outs = refs[3 + n_row:3 + n_row + n_row_out]
        acc = refs[-1]
        i, kk = pl.program_id(0), pl.program_id(1)
        if with_gain_grad:
            dg_ref = refs[3 + n_row + n_row_out]

            @pl.when((i == 0) & (kk == 0))
            def _():
                dg_ref[...] = jnp.zeros(dg_ref.shape, F32)

        @pl.when(kk == 0)
        def _():
            acc[...] = jnp.zeros(acc.shape, F32)

        acc[...] += jnp.dot(a_ref[...], w_ref[...], preferred_element_type=F32)

        @pl.when(kk == nk - 1)
        def _():
            res = epilogue(acc[...], *[r[...] for r in rows], g_ref[...])
            for o_ref, val in zip(outs, res[:n_row_out]):
                o_ref[...] = val
            if with_gain_grad:
                dg_ref[...] += res[n_row_out]

    out_specs = [row] * n_row_out + ([vec] if with_gain_grad else [])
    out_shape = [jax.ShapeDtypeStruct((t, d), F32)] * n_row_out
    out_shape += [jax.ShapeDtypeStruct((1, d), F32)] if with_gain_grad else []
    return pl.pallas_call(
        body, name=name, grid=(t // tm, nk),
        in_specs=[pl.BlockSpec((tm, tk), lambda i, kk: (i, kk)), pl.BlockSpec((tk, d), lambda i, kk: (kk, 0))]
        + [row] * n_row + [vec],
        out_specs=out_specs, out_shape=out_shape, scratch_shapes=[pltpu.VMEM((tm, d), F32)],
        compiler_params=_params(("arbitrary", "arbitrary")))(a, w, *row_ins, g)


def _rms_stats(x, n_real=None):
    n = x.shape[-1] if n_real is None else n_real
    return lax.rsqrt(jnp.sum(x * x, axis=-1, keepdims=True) / n + EPS)


def _rms_bwd(x, r, g, dz, n_real=None):
    n = x.shape[-1] if n_real is None else n_real
    xh = x * r
    dxh = dz * g
    dx = r * (dxh - xh * (jnp.sum(dxh * xh, axis=-1, keepdims=True) / n))
    return dx, jnp.sum(dz * xh, axis=0, keepdims=True)


def _sigmoid(x):
    return 0.5 * jnp.tanh(0.5 * x) + 0.5


def _roll(x, s, axis):
    return pltpu.roll(x, s, axis)


def _rope(x, c, s1, s2):
    return x * c + _roll(x, HEAD_LANES - MLA_ROPE // 2, 1) * s1 + _roll(x, MLA_ROPE // 2, 1) * s2


def _heads_apply(x, fn):
    return jnp.concatenate([fn(x[:, h * HEAD_LANES:(h + 1) * HEAD_LANES]) for h in range(N_HEADS)], axis=1)


ROW_CHUNK = 256


def _row_chunks(rows):
    step = min(ROW_CHUNK, rows)
    return [pl.ds(r, step) for r in range(0, rows, step)]


def _ffn_fwd(tag, x, g_pre, wg, wu, wd, g_post, tm, loss_target=None, carried_up=None):
    t, d = x.shape
    ns, fs, _ = wg.shape
    nt = t // tm
    row = pl.BlockSpec((tm, d), lambda i, q: (i, 0))
    vec = pl.BlockSpec((1, d), lambda i, q: (0, 0))
    act3 = pl.BlockSpec((1, tm, fs), lambda i, q: (q, i, 0))
    wrow = pl.BlockSpec((1, fs, d), lambda i, q: (q, 0, 0))
    nt_dims = (((1,), (1,)), ((), ()))

    def gate_up(x_ref, g_ref, wg_ref, wu_ref, n_ref, sl_ref, ud_ref, s_ref, n_s):
        @pl.when(pl.program_id(1) == 0)
        def _():
            for r in _row_chunks(tm):
                xb = x_ref[r, :]
                n_s[r, :] = (xb * _rms_stats(xb) * g_ref[...]).astype(MM_DTYPE)
            n_ref[...] = n_s[...]

        for r in _row_chunks(tm):
            n = n_s[r, :]
            a = lax.dot_general(n, wg_ref[0], nt_dims, preferred_element_type=F32)
            u = lax.dot_general(n, wu_ref[0], nt_dims, preferred_element_type=F32)
            sg = _sigmoid(a)
            sl = a * sg
            sl_ref[0, r, :] = sl.astype(sl_ref.dtype)
            ud_ref[0, r, :] = (u * (sg + sl * (1.0 - sg))).astype(ud_ref.dtype)
            s_ref[0, r, :] = (sl * u).astype(s_ref.dtype)

    gate_up, x_in, x_out, x_shape, x_scr, x_args = _carry(gate_up, 4, 4, (nt, ns), carried_up)
    n, sl, ud, s, *from_up = pl.pallas_call(
        gate_up, name=tag + "_gate_up", grid=(nt, ns), in_specs=[row, vec, wrow, wrow] + x_in,
        out_specs=[row, act3, act3, act3] + x_out,
        out_shape=[jax.ShapeDtypeStruct((t, d), MM_DTYPE)] + [jax.ShapeDtypeStruct((ns, t, fs), MM_DTYPE)] * 3 + x_shape,
        scratch_shapes=[pltpu.VMEM((tm, d), MM_DTYPE)] + x_scr,
        compiler_params=_params(("arbitrary", "arbitrary")))(x, g_pre, wg, wu, *x_args)
    if wd is None:
        wd = from_up[0]

    def down(s_ref, wd_ref, x_ref, g_ref, *rest):
        if loss_target is not None:
            tgt_ref, h_ref, y_ref, ls_ref = rest

            @pl.when(pl.program_id(0) == 0)
            def _():
                ls_ref[...] = jnp.zeros(ls_ref.shape, F32)
        else:
            h_ref, y_ref = rest

        hb = jnp.dot(s_ref[0], wd_ref[0], preferred_element_type=F32)
        for q in range(1, ns):
            hb = hb + jnp.dot(s_ref[q], wd_ref[q], preferred_element_type=F32)
        h_ref[...] = hb
        yb = x_ref[...] + 0.5 * (hb * _rms_stats(hb) * g_ref[...])
        if loss_target is None:
            y_ref[...] = yb
        else:
            e = yb - tgt_ref[...]
            y_ref[...] = e * (1.0 / d)
            ls_ref[...] += jnp.sum(e * e, axis=0, keepdims=True)

    with_loss = loss_target is not None
    td = min(512, t)
    drow = pl.BlockSpec((td, d), lambda i: (i, 0))
    dvec = pl.BlockSpec((1, d), lambda i: (0, 0))
    outs = pl.pallas_call(
        down, name=tag + "_down", grid=(t // td,),
        in_specs=[pl.BlockSpec((ns, td, fs), lambda i: (0, i, 0)), pl.BlockSpec((ns, fs, d), lambda i: (0, 0, 0)), drow,
                  dvec] + ([drow] if with_loss else []),
        out_specs=[drow, drow] + ([dvec] if with_loss else []),
        out_shape=[jax.ShapeDtypeStruct((t, d), F32)] * 2 + ([jax.ShapeDtypeStruct((1, d), F32)] if with_loss else []),
        compiler_params=_params(("arbitrary",)))(s, wd, x, g_post, *([loss_target] if with_loss else []))
    if with_loss:
        h, dy, lsum = outs
        return (dy, lsum), (x, n, sl, ud, s, h), wd
    h, y = outs
    return y, (x, n, sl, ud, s, h), wd


def _carry(body, n_in, n_out, grid, carried):
    if carried is None:
        return body, [], [], [], [], []
    nx_in, nx_out = len(carried.ins), len(carried.outs)

    def wrapped(*refs):
        ins, rest = refs[:n_in], refs[n_in:]
        xi, rest = rest[:nx_in], rest[nx_in:]
        outs, rest = rest[:n_out], rest[n_out:]
        xo, rest = rest[:nx_out], rest[nx_out:]
        scr, sems = rest[:len(rest) - 2], rest[len(rest) - 2:]
        first, last = True, True
        for dim, size in enumerate(grid):
            first = first & (pl.program_id(dim) == 0)
            last = last & (pl.program_id(dim) == size - 1)

        @pl.when(first)
        def _():
            carried.start(xi, xo, *sems)

        body(*ins, *outs, *scr)

        @pl.when(last)
        def _():
            carried.finish(xi, xo, *sems)

    sems = [pltpu.SemaphoreType.DMA((carried.n_sem,)), pltpu.SemaphoreType.DMA((carried.n_sem,))]
    return (wrapped, [HBM_SPEC] * nx_in, [HBM_SPEC] * nx_out, list(carried.outs), sems, list(carried.ins))


def _ffn_bwd(tag, dy, saved, g_pre, wg, wu, wd, g_post, tm, tk, carried_down=None, make_carried_mid=None,
             make_carried_up=None):
    x, n, sl, ud, s, h = saved
    t, d = x.shape
    ns, fs, _ = wg.shape
    nt, nk = t // tm, t // tk
    row = pl.BlockSpec((tm, d), lambda i, q: (i, 0))
    vec = pl.BlockSpec((1, d), lambda i, q: (0, 0))
    act3 = pl.BlockSpec((1, tm, fs), lambda i, q: (q, i, 0))
    wrow = pl.BlockSpec((1, fs, d), lambda i, q: (q, 0, 0))
    nt_dims = (((1,), (1,)), ((), ()))
    tn_dims = (((0,), (0,)), ((), ()))

    def down_b(h_ref, dy_ref, g_ref, wd_ref, sl_ref, ud_ref, dh_ref, da_ref, du_ref, dg_ref, dh_s):
        i, q = pl.program_id(0), pl.program_id(1)

        @pl.when((i == 0) & (q == 0))
        def _():
            dg_ref[...] = jnp.zeros(dg_ref.shape, F32)

        @pl.when(q == 0)
        def _():
            for r in _row_chunks(tm):
                hb = h_ref[r, :]
                dh, dg = _rms_bwd(hb, _rms_stats(hb), g_ref[...], 0.5 * dy_ref[r, :])
                dh_s[r, :] = dh.astype(MM_DTYPE)
                dg_ref[...] += dg
            dh_ref[...] = dh_s[...]

        for r in _row_chunks(tm):
            ds = lax.dot_general(dh_s[r, :], wd_ref[0], nt_dims, preferred_element_type=F32)
            da_ref[0, r, :] = (ds * ud_ref[0, r, :].astype(F32)).astype(da_ref.dtype)
            du_ref[0, r, :] = (ds * sl_ref[0, r, :].astype(F32)).astype(du_ref.dtype)

    down_b, x_in, x_out, x_shape, x_scr, x_args = _carry(down_b, 6, 4, (nt, ns), carried_down)
    dh, da, du, dg_post, *from_down = pl.pallas_call(
        down_b, name=tag + "_down_b", grid=(nt, ns), in_specs=[row, row, vec, wrow, act3, act3] + x_in,
        out_specs=[row, act3, act3, vec] + x_out,
        out_shape=[jax.ShapeDtypeStruct((t, d), MM_DTYPE)] + [jax.ShapeDtypeStruct((ns, t, fs), MM_DTYPE)] * 2
        + [jax.ShapeDtypeStruct((1, d), F32)] + x_shape,
        scratch_shapes=[pltpu.VMEM((tm, d), MM_DTYPE)] + x_scr,
        compiler_params=_params(("arbitrary", "arbitrary")))(h, dy, g_post, wd, sl, ud, *x_args)

    def down_w(s_ref, dh_ref, dw_ref, acc):
        kk = pl.program_id(1)

        @pl.when(kk == 0)
        def _():
            acc[...] = jnp.zeros(acc.shape, F32)

        acc[...] += lax.dot_general(s_ref[0], dh_ref[...], tn_dims, preferred_element_type=F32)

        @pl.when(kk == nk - 1)
        def _():
            dw_ref[0] = acc[...].astype(dw_ref.dtype)

    dwd = pl.pallas_call(
        down_w, name=tag + "_down_w", grid=(ns, nk),
        in_specs=[pl.BlockSpec((1, tk, fs), lambda q, kk: (q, kk, 0)), pl.BlockSpec((tk, d), lambda q, kk: (kk, 0))],
        out_specs=pl.BlockSpec((1, fs, d), lambda q, kk: (q, 0, 0)),
        out_shape=jax.ShapeDtypeStruct((ns, fs, d), MM_DTYPE), scratch_shapes=[pltpu.VMEM((fs, d), F32)],
        compiler_params=_params(("parallel", "arbitrary")))(s, dh)

    def gate_up_b(da_ref, du_ref, wg_ref, wu_ref, x_ref, dy_ref, g_ref, dx_ref, dg_ref, acc):
        i, q = pl.program_id(0), pl.program_id(1)

        @pl.when((i == 0) & (q == 0))
        def _():
            dg_ref[...] = jnp.zeros(dg_ref.shape, F32)

        @pl.when(q == 0)
        def _():
            acc[...] = jnp.zeros(acc.shape, F32)

        for r in _row_chunks(tm):
            acc[r, :] += (jnp.dot(da_ref[0, r, :], wg_ref[0], preferred_element_type=F32)
                          + jnp.dot(du_ref[0, r, :], wu_ref[0], preferred_element_type=F32))

        @pl.when(q == ns - 1)
        def _():
            for r in _row_chunks(tm):
                xb = x_ref[r, :]
                dx, dg = _rms_bwd(xb, _rms_stats(xb), g_ref[...], acc[r, :])
                dx_ref[r, :] = dy_ref[r, :] + dx
                dg_ref[...] += dg

    def gate_up_w(n_ref, da_ref, du_ref, dwg_ref, dwu_ref, acc_g, acc_u):
        kk = pl.program_id(1)

        @pl.when(kk == 0)
        def _():
            acc_g[...] = jnp.zeros(acc_g.shape, F32)
            acc_u[...] = jnp.zeros(acc_u.shape, F32)

        nb = n_ref[...]
        acc_g[...] += lax.dot_general(da_ref[0], nb, tn_dims, preferred_element_type=F32)
        acc_u[...] += lax.dot_general(du_ref[0], nb, tn_dims, preferred_element_type=F32)

        @pl.when(kk == nk - 1)
        def _():
            dwg_ref[0] = acc_g[...].astype(dwg_ref.dtype)
            dwu_ref[0] = acc_u[...].astype(dwu_ref.dtype)

    k3 = pl.BlockSpec((1, tk, fs), lambda q, kk: (q, kk, 0))
    wout = pl.BlockSpec((1, fs, d), lambda q, kk: (q, 0, 0))
    carried_mid = make_carried_mid(dwd) if make_carried_mid else None
    gate_up_w, x_in, x_out, x_shape, x_scr, x_args = _carry(gate_up_w, 3, 2, (ns, nk), carried_mid)
    dwg, dwu, *from_mid = pl.pallas_call(
        gate_up_w, name=tag + "_gate_up_w", grid=(ns, nk),
        in_specs=[pl.BlockSpec((tk, d), lambda q, kk: (kk, 0)), k3, k3] + x_in, out_specs=[wout, wout] + x_out,
        out_shape=[jax.ShapeDtypeStruct((ns, fs, d), MM_DTYPE)] * 2 + x_shape,
        scratch_shapes=[pltpu.VMEM((fs, d), F32)] * 2 + x_scr,
        compiler_params=_params(("arbitrary", "arbitrary")))(n, da, du, *x_args)

    carried_up = make_carried_up(dwg, dwu) if make_carried_up else None
    gate_up_b, x_in, x_out, x_shape, x_scr, x_args = _carry(gate_up_b, 7, 2, (nt, ns), carried_up)
    dx, dg_pre, *from_up = pl.pallas_call(
        gate_up_b, name=tag + "_gate_up_b", grid=(nt, ns), in_specs=[act3, act3, wrow, wrow, row, row, vec] + x_in,
        out_specs=[row, vec] + x_out,
        out_shape=[jax.ShapeDtypeStruct((t, d), F32), jax.ShapeDtypeStruct((1, d), F32)] + x_shape,
        scratch_shapes=[pltpu.VMEM((tm, d), F32)] + x_scr,
        compiler_params=_params(("arbitrary", "arbitrary")))(da, du, wg, wu, x, dy, g_pre, *x_args)
    return dx, dg_pre, dwg, dwu, dwd, dg_post, from_down, from_mid, from_up


NEG = -1e30


def _attn_scale():
    return (MLA_NOPE + MLA_ROPE) ** -0.5


def _causal_pairs(nq, by_key):
    if by_key:
        pairs = [(qi, ki) for ki in range(nq) for qi in range(ki, nq)]
    else:
        pairs = [(qi, ki) for qi in range(nq) for ki in range(qi + 1)]
    return jnp.asarray([p[0] for p in pairs], jnp.int32), jnp.asarray([p[1] for p in pairs], jnp.int32)


def _below_diagonal(shape):
    return lax.broadcasted_iota(jnp.int32, shape, 1) <= lax.broadcasted_iota(jnp.int32, shape, 0)


def _attn_call(name, body, tables, args, in_kinds, out_kinds, scratch, t, tq, carried=None):
    qmap = lambda h, p, qt, kt: (qt[p], h)
    kmap = lambda h, p, qt, kt: (kt[p], h)
    width = HEADS_PER_STEP * HEAD_LANES
    spec = lambda kind: pl.BlockSpec((tq, width), qmap if kind == "q" else kmap)
    n_pairs = tables[0].shape[0]
    n_groups = N_HEADS // HEADS_PER_STEP
    n_in, n_out, n_scr = len(in_kinds), len(out_kinds), scratch
    x_ins = list(carried.ins) if carried else []
    x_outs = list(carried.outs) if carried else []
    x_scr = [pltpu.SemaphoreType.DMA((carried.n_sem,)), pltpu.SemaphoreType.DMA((carried.n_sem,))] if carried else []

    def full_body(qt, kt, *refs):
        ins, refs = refs[:n_in], refs[n_in:]
        xi, refs = refs[:len(x_ins)], refs[len(x_ins):]
        outs, refs = refs[:n_out], refs[n_out:]
        xo, refs = refs[:len(x_outs)], refs[len(x_outs):]
        scr, sems = refs[:n_scr], refs[n_scr:]
        if carried:
            @pl.when((pl.program_id(0) == 0) & (pl.program_id(1) == 0))
            def _():
                carried.start(xi, xo, *sems)

        heads = [tuple(r.at[:, pl.ds(hh * HEAD_LANES, HEAD_LANES)] for r in (*ins, *outs, *scr))
                 for hh in range(HEADS_PER_STEP)]
        body(qt, kt, heads)
        if carried:
            @pl.when((pl.program_id(0) == n_groups - 1) & (pl.program_id(1) == n_pairs - 1))
            def _():
                carried.finish(xi, xo, *sems)

    grid_spec = pltpu.PrefetchScalarGridSpec(
        num_scalar_prefetch=2, grid=(n_groups, n_pairs),
        in_specs=[spec(kd) for kd in in_kinds] + [HBM_SPEC] * len(x_ins),
        out_specs=[spec(kd) for kd in out_kinds] + [HBM_SPEC] * len(x_outs),
        scratch_shapes=[pltpu.VMEM((tq, width), F32)] * n_scr + x_scr)
    return pl.pallas_call(full_body, name=name, grid_spec=grid_spec,
                          out_shape=[jax.ShapeDtypeStruct((t, MLA_PAD), F32) for _ in out_kinds] + x_outs,
                          compiler_params=_params(("arbitrary", "arbitrary")))(*tables, *args, *x_ins)


class _Carried:
    def __init__(self, ins, outs, n_sem, start, finish):
        self.ins, self.outs, self.n_sem, self.start, self.finish = ins, outs, n_sem, start, finish


def _attn_fwd(q, k, v, tq, carried=None):
    t = q.shape[0]
    nq = t // tq

    def body(qt, kt, heads):
        p_id = pl.program_id(1)
        qi, ki = qt[p_id], kt[p_id]

        @pl.when(ki == 0)
        def _():
            for _, _, _, _, _, m_s, l_s, acc_s in heads:
                m_s[...] = jnp.full(m_s.shape, NEG, F32)
                l_s[...] = jnp.zeros(l_s.shape, F32)
                acc_s[...] = jnp.zeros(acc_s.shape, F32)

        def update(diagonal):
            for q_ref, k_ref, v_ref, _, _, m_s, l_s, acc_s in heads:
                s = lax.dot_general(q_ref[...], k_ref[...], (((1,), (1,)), ((), ())), preferred_element_type=F32)
                if diagonal:
                    s = jnp.where(_below_diagonal(s.shape), s, NEG)
                m_old = m_s[...]
                m_new = jnp.maximum(m_old, jnp.max(s, axis=1, keepdims=True))
                alpha = jnp.exp(m_old - m_new)
                p = jnp.exp(s - m_new[:, :1])
                l_s[...] = l_s[...] * alpha + jnp.sum(p, axis=1, keepdims=True)
                acc_s[...] = acc_s[...] * alpha + jnp.dot(p.astype(MM_DTYPE), v_ref[...], preferred_element_type=F32)
                m_s[...] = m_new

        @pl.when(ki < qi)
        def _():
            update(False)

        @pl.when(ki == qi)
        def _():
            update(True)
            for _, _, _, o_ref, lse_ref, m_s, l_s, acc_s in heads:
                o_ref[...] = acc_s[...] / l_s[...]
                lse_ref[...] = m_s[...] + jnp.log(l_s[...])

    return _attn_call("mla_attn_fwd", body, _causal_pairs(nq, False), (q, k, v), "qkk", "qq", 3, t, tq, carried)


def _attn_probs(q, k, lse, diagonal):
    s = lax.dot_general(q, k, (((1,), (1,)), ((), ())), preferred_element_type=F32)
    p = jnp.exp(s - lse[:, :1])
    return jnp.where(_below_diagonal(s.shape), p, 0.0) if diagonal else p


BWD_HEADS = 4


def _attn_bwd(q, k, v, do, lse, delta, tq, carried=None):
    t = q.shape[0]
    nq = t // tq
    width = BWD_HEADS * HEAD_LANES
    n_groups = N_HEADS // BWD_HEADS
    qt_tab, kt_tab = _causal_pairs(nq, True)
    n_pairs = qt_tab.shape[0]
    qmap = lambda h, p, qt, kt: (qt[p], h)
    kmap = lambda h, p, qt, kt: (kt[p], h)
    qs, ks = pl.BlockSpec((tq, width), qmap), pl.BlockSpec((tq, width), kmap)
    x_ins = list(carried.ins) if carried else []
    x_outs = list(carried.outs) if carried else []
    x_scr = [pltpu.SemaphoreType.DMA((carried.n_sem,)), pltpu.SemaphoreType.DMA((carried.n_sem,))] if carried else []
    nt_dims = (((1,), (1,)), ((), ()))
    tn_dims = (((0,), (0,)), ((), ()))

    def body(qt, kt, q_ref, k_ref, v_ref, do_ref, lse_ref, dl_ref, *rest):
        xi, rest = rest[:len(x_ins)], rest[len(x_ins):]
        dq_hbm, dk_ref, dv_ref = rest[:3]
        xo, rest = rest[3:3 + len(x_outs)], rest[3 + len(x_outs):]
        dk_s, dv_s, dq_s, dq_sem = rest[:4]
        sems = rest[4:]
        grp, p_id = pl.program_id(0), pl.program_id(1)
        qi, ki = qt[p_id], kt[p_id]
        if carried:
            @pl.when((grp == 0) & (p_id == 0))
            def _():
                carried.start(xi, xo, *sems)

        @pl.when(p_id == 0)
        def _():
            dq_s[...] = jnp.zeros(dq_s.shape, F32)

        def step(diagonal):
            rows = pl.ds(pl.multiple_of(qi * tq, tq), tq)
            for hh in range(BWD_HEADS):
                ln = pl.ds(hh * HEAD_LANES, HEAD_LANES)
                qb, kb, vb, dob = q_ref[:, ln], k_ref[:, ln], v_ref[:, ln], do_ref[:, ln]
                p = _attn_probs(qb, kb, lse_ref[:, ln], diagonal)
                dv_s[:, ln] += lax.dot_general(p.astype(MM_DTYPE), dob, tn_dims, preferred_element_type=F32)
                dp = lax.dot_general(dob, vb, nt_dims, preferred_element_type=F32)
                ds = (p * (dp - dl_ref[:, ln][:, :1])).astype(MM_DTYPE)
                dk_s[:, ln] += lax.dot_general(ds, qb, tn_dims, preferred_element_type=F32)
                dq_s[rows, ln] += jnp.dot(ds, kb, preferred_element_type=F32)

        @pl.when(qi == ki)
        def _():
            dk_s[...] = jnp.zeros(dk_s.shape, F32)
            dv_s[...] = jnp.zeros(dv_s.shape, F32)
            step(True)

        @pl.when(qi > ki)
        def _():
            step(False)

        @pl.when(qi == nq - 1)
        def _():
            dk_ref[...] = dk_s[...]
            dv_ref[...] = dv_s[...]

        @pl.when(p_id == n_pairs - 1)
        def _():
            out = pltpu.make_async_copy(dq_s, dq_hbm.at[pl.ds(pl.multiple_of(grp * t, t), t)], dq_sem)
            out.start()
            out.wait()

        if carried:
            @pl.when((grp == n_groups - 1) & (p_id == n_pairs - 1))
            def _():
                carried.finish(xi, xo, *sems)

    grid_spec = pltpu.PrefetchScalarGridSpec(
        num_scalar_prefetch=2, grid=(n_groups, n_pairs),
        in_specs=[qs, ks, ks, qs, qs, qs] + [HBM_SPEC] * len(x_ins),
        out_specs=[HBM_SPEC, ks, ks] + [HBM_SPEC] * len(x_outs),
        scratch_shapes=[pltpu.VMEM((tq, width), F32), pltpu.VMEM((tq, width), F32), pltpu.VMEM((t, width), F32),
                        pltpu.SemaphoreType.DMA] + x_scr)
    return pl.pallas_call(
        body, name="mla_attn_bwd", grid_spec=grid_spec,
        out_shape=[jax.ShapeDtypeStruct((n_groups * t, width), F32), jax.ShapeDtypeStruct((t, MLA_PAD), F32),
                   jax.ShapeDtypeStruct((t, MLA_PAD), F32)] + x_outs,
        compiler_params=_params(("arbitrary", "arbitrary")))(qt_tab, kt_tab, q, k, v, do, lse, delta, *x_ins)


def _dot01(a, b, dims=(((1,), (0,)), ((), ())), ones="rhs"):
    val, sel = (a, b) if ones == "rhs" else (b, a)
    head = val.astype(BF16)
    tail = (val - head.astype(F32)).astype(BF16)
    sel = sel.astype(BF16)
    dot = lambda part: (lax.dot_general(part, sel, dims, preferred_element_type=F32) if ones == "rhs"
                        else lax.dot_general(sel, part, dims, preferred_element_type=F32))
    return dot(head) + dot(tail)


def _dot1(a, b, dims=(((1,), (0,)), ((), ()))):
    return lax.dot_general(a.astype(MM_DTYPE), b.astype(MM_DTYPE), dims, preferred_element_type=F32)


def _dot3(a, b, dims=(((1,), (0,)), ((), ()))):
    return lax.dot_general(a, b, dims, preferred_element_type=F32, precision=lax.Precision.HIGH)


NN3 = (((2,), (1,)), ((0,), (0,)))
NT3 = (((2,), (2,)), ((0,), (0,)))
TN3 = (((1,), (1,)), ((0,), (0,)))


def _tri_masks(nh):
    shape = (nh, CHUNK, CHUNK)
    return lax.broadcasted_iota(jnp.int32, shape, 1), lax.broadcasted_iota(jnp.int32, shape, 2)


def _gdn_chunk_common(k, gcc, bb, row, col, dot=_dot1):
    tril = row >= col
    gcr = jnp.swapaxes(gcc, 1, 2)
    dm = jnp.exp(jnp.where(tril, gcc - gcr, NEG))
    kb = k * bb
    lm = jnp.where(row > col, dot(kb, k, NT3) * dm, 0.0)
    return dm, kb, lm


def _unit_lower_inverse(lm, eye):
    t = eye - lm
    p = lm
    for _ in range(CHUNK.bit_length() - 2):
        p = _dot3(p, p, NN3)
        t = t + _dot3(t, p, NN3)
    return t


def _chunk_sum_matrix(tb, upper):
    r = lax.broadcasted_iota(jnp.int32, (tb, tb), 0)
    c = lax.broadcasted_iota(jnp.int32, (tb, tb), 1)
    same = (r // CHUNK) == (c // CHUNK)
    return (same & ((c >= r) if upper else (c <= r))).astype(F32)


def _gdn_fwd(q, k, v, gb, bb, carried=None):
    nh, t, dh = q.shape
    nchunk = t // CHUNK

    def body(q_ref, k_ref, v_ref, g_ref, b_ref, o_ref, sall_ref, tall_ref, s_s):
        @pl.when(pl.program_id(0) == 0)
        def _():
            s_s[...] = jnp.zeros(s_s.shape, F32)

        row, col = _tri_masks(nh)
        sh = s_s[...]
        for cc in range(cps):
            rows = pl.ds(cc * CHUNK, CHUNK)
            qh, kh, vh, bbh, gcc = q_ref[:, rows, :], k_ref[:, rows, :], v_ref[:, rows, :], b_ref[:, rows, :], \
                g_ref[:, rows, :]
            dm, kb, lm = _gdn_chunk_common(kh, gcc, bbh, row, col)
            eg = jnp.exp(gcc)
            glr = gcc[:, CHUNK - 1:CHUNK, :]
            th = _unit_lower_inverse(lm, (row == col).astype(F32))
            w = _dot1(th, kb * eg, NN3)
            u = _dot1(th, vh * bbh, NN3)
            at = jnp.where(row >= col, _dot1(qh, kh, NT3) * dm, 0.0)
            vn = u - _dot1(w, sh, NN3)
            o_ref[:, rows, :] = _dot1(qh * eg, sh, NN3) + _dot1(at, vn, NN3)
            kd = kh * jnp.exp(glr - gcc)
            sall_ref[:, cc] = sh
            tall_ref[:, rows, :] = th
            sh = sh * jnp.exp(glr) + _dot1(kd, vn, TN3)
        s_s[...] = sh

    cps = min(GDN_STEP_CHUNKS, nchunk)
    steps = nchunk // cps
    blk = pl.BlockSpec((nh, cps * CHUNK, dh), lambda n: (0, n, 0))
    body, x_in, x_out, x_shape, x_scr, x_args = _carry(body, 5, 3, (steps,), carried)
    return pl.pallas_call(
        body, name="gdn_fwd", grid=(steps,), in_specs=[blk] * 5 + x_in,
        out_specs=[blk, pl.BlockSpec((nh, cps, dh, dh), lambda n: (0, n, 0, 0)), blk] + x_out,
        out_shape=[jax.ShapeDtypeStruct((nh, t, dh), F32), jax.ShapeDtypeStruct((nh, nchunk, dh, dh), F32),
                   jax.ShapeDtypeStruct((nh, t, CHUNK), F32)] + x_shape,
        scratch_shapes=[pltpu.VMEM((nh, dh, dh), F32)] + x_scr,
        compiler_params=_params(("arbitrary",)))(q, k, v, gb, bb, *x_args)


def _gdn_bwd(q, k, v, gb, bb, sall, tall, do):
    nh, t, dh = q.shape
    nchunk = t // CHUNK

    def body(q_ref, k_ref, v_ref, g_ref, b_ref, sall_ref, tall_ref, do_ref,
             dq_ref, dk_ref, dv_ref, dg_ref, db_ref, ds_s):
        @pl.when(pl.program_id(0) == 0)
        def _():
            ds_s[...] = jnp.zeros(ds_s.shape, F32)

        row, col = _tri_masks(nh)
        tril, stril = row >= col, row > col
        rsum = lambda x: jnp.sum(x, axis=2, keepdims=True)
        dsp = ds_s[...]
        for cc in reversed(range(cps)):
            rows = pl.ds(cc * CHUNK, CHUNK)
            dsp = chunk_bwd(rows, cc, dsp, row, col, tril, stril, rsum, q_ref, k_ref, v_ref, g_ref, b_ref, sall_ref,
                            tall_ref, do_ref, dq_ref, dk_ref, dv_ref, dg_ref, db_ref)
        ds_s[...] = dsp

    def chunk_bwd(rows, cc, dsp, row, col, tril, stril, rsum, q_ref, k_ref, v_ref, g_ref, b_ref, sall_ref, tall_ref,
                  do_ref, dq_ref, dk_ref, dv_ref, dg_ref, db_ref):
        qh, kh, vh, gcc, bbh = q_ref[:, rows, :], k_ref[:, rows, :], v_ref[:, rows, :], g_ref[:, rows, :], \
            b_ref[:, rows, :]
        sh, th, doh = sall_ref[:, cc], tall_ref[:, rows, :], do_ref[:, rows, :]
        dm, kb, lm = _gdn_chunk_common(kh, gcc, bbh, row, col, _dot3)
        eg = jnp.exp(gcc)
        glr = gcc[:, CHUNK - 1:CHUNK, :]
        glv = jnp.exp(glr)
        egl = jnp.exp(glr - gcc)
        rw, ru = kb * eg, vh * bbh
        w, u = _dot3(th, rw, NN3), _dot3(th, ru, NN3)
        at = jnp.where(tril, _dot3(qh, kh, NT3) * dm, 0.0)
        qd, kd = qh * eg, kh * egl
        vn = u - _dot3(w, sh, NN3)
        dgl = jnp.sum(rsum(dsp * sh), axis=1, keepdims=True)
        dkd = _dot3(vn, dsp, NT3)
        dvn = _dot3(kd, dsp, NN3)
        dqd = _dot3(doh, sh, NT3)
        dat = jnp.where(tril, _dot3(doh, vn, NT3), 0.0)
        dvn = dvn + _dot3(at, doh, TN3)
        dw = -_dot3(dvn, sh, NT3)
        ds_before = dsp * glv + _dot3(qd, doh, TN3) - _dot3(w, dvn, TN3)
        dpa = dat * dm
        dq_ref[:, rows, :] = _dot1(dpa, kh, NN3) + dqd * eg
        dk = _dot1(dpa, qh, TN3) + dkd * egl
        t6 = rsum(dkd * kd)
        dgam = rsum(dqd * qd) - t6
        dgam_last = jnp.sum(t6, axis=1, keepdims=True) + dgl * glv
        drw = _dot3(th, dw, TN3)
        dru = _dot3(th, dvn, TN3)
        dl = -jnp.where(stril, _dot3(drw, w, NT3) + _dot3(dru, u, NT3), 0.0)
        dgam = dgam + rsum(drw * rw)
        dv_ref[:, rows, :] = dru * bbh
        dp2 = dl * dm
        dkb = drw * eg + _dot1(dp2, kh, NN3)
        dk_ref[:, rows, :] = dk + _dot1(dp2, kb, TN3) + dkb * bbh
        db_ref[:, rows, :] = rsum(dru * vh) + rsum(dkb * kh) + jnp.zeros((nh, CHUNK, dh), F32)
        e = dat * at + dl * lm
        dgam_b = dgam + rsum(e) - _dot01(e, jnp.ones((nh, CHUNK, CHUNK), F32), TN3)
        dg_ref[:, rows, :] = dgam_b + jnp.where(row == CHUNK - 1, dgam_last, 0.0)
        return ds_before

    cps = min(GDN_STEP_CHUNKS, nchunk)
    steps = nchunk // cps
    rev = lambda n: (0, steps - 1 - n, 0)
    blk = pl.BlockSpec((nh, cps * CHUNK, dh), rev)
    sblk = pl.BlockSpec((nh, cps, dh, dh), lambda n: (0, steps - 1 - n, 0, 0))
    out = jax.ShapeDtypeStruct((nh, t, dh), F32)
    return pl.pallas_call(
        body, name="gdn_bwd", grid=(steps,), in_specs=[blk] * 5 + [sblk, blk, blk], out_specs=[blk] * 5,
        out_shape=[out] * 5, scratch_shapes=[pltpu.VMEM((nh, dh, dh), F32)],
        compiler_params=_params(("arbitrary",)))(q, k, v, gb, bb, sall, tall, do)


def _group_ones():
    r = lax.broadcasted_iota(jnp.int32, (GDN_W, GDN_W), 0) // GDN_DH
    c = lax.broadcasted_iota(jnp.int32, (GDN_W, GDN_W), 1) // GDN_DH
    return (r == c).astype(F32)


def _conv_taps(x, xprev, w, has_prev):
    row = lax.broadcasted_iota(jnp.int32, x.shape, 0)
    out = x * w[GDN_CONV - 1:GDN_CONV, :]
    for s in range(1, GDN_CONV):
        sh = jnp.where(row >= s, _roll(x, s, 0), _roll(xprev, s, 0) * has_prev)
        out = out + sh * w[GDN_CONV - 1 - s:GDN_CONV - s, :]
    return out


def _head_cols(x, h):
    return x[:, h * GDN_DH:(h + 1) * GDN_DH]


def _heads_spec(tb):
    return pl.BlockSpec((N_HEADS, tb, GDN_DH), lambda i: (0, i, 0))


def _mixer_fwd(x, positions, w, tb, carried=None, carried_gdn=None):
    t, d = x.shape
    tables = _rope_tables(positions)

    hn, proj = _prologue_mm("mix_in", [x], w["mix_pre_g"], lambda xb, g: (xb * _rms_stats(xb) * g,), w["w_in_pad_t"],
                            False)

    def mla_pre(p0, gq, gkv):
        cq, ckv = p0[:, :MLA_Q_RANK], p0[:, MLA_Q_RANK:MLA_Q_RANK + MLA_KV_RANK]
        return cq * _rms_stats(cq) * gq, ckv * _rms_stats(ckv) * gkv

    nq, nkv = _rowwise("mla_pre", mla_pre, [(proj, 512, PIN_MLA // 512, 0)],
                       [w["mla_q_norm_g"], w["mla_kv_norm_g"]], [(MLA_Q_RANK, BF16), (MLA_KV_RANK, BF16)], [], tb)
    qraw = _mm("mla_uq", nq, w["w_uq_pad"], "nn", F32)
    kv = _mm("mla_ukv", nkv, w["w_kv_pad"], "nn", F32)

    def rope_f(qr, kn, vv, kpe, c, s1, s2):
        qo = _heads_apply(qr, lambda xh: _rope(xh, c, s1, s2)) * _attn_scale()
        kp = _rope(kpe, c, s1, s2)
        return qo, kn + jnp.tile(kp, (1, N_HEADS)), vv

    q, k, v = _rowwise("mla_rope", rope_f,
                       [qraw, (kv, MLA_PAD, 0, 0), (kv, MLA_PAD, 1, 0), (proj, HEAD_LANES, PIN_KPE // HEAD_LANES, 0),
                        tables[0], tables[1], tables[2]], [],
                       [(MLA_PAD, BF16)] * 3, [], tb // 2)
    tq = min(1024, t)
    o, lse, *carried_out = _attn_fwd(q, k, v, tq, carried)

    def mla_post(ob, g):
        return (ob * _rms_stats(ob, N_HEADS * MLA_V) * g,)

    (cat,) = _rowwise("mla_post", mla_post, [o], [w["mla_out_g_pad"]], [(MLA_PAD, BF16)], [], tb, wide=(CAT_W, 0))

    gones = _group_ones()
    steps = t // tb

    def gdn_pre(xq, xk, xv, pq, pk, pv, cw, go, has_prev):
        outs = []
        for j, (xc, xp) in enumerate(((xq, pq), (xk, pk), (xv, pv))):
            c = _conv_taps(xc, xp, cw[:, j * GDN_W:(j + 1) * GDN_W], has_prev)
            a = c * _sigmoid(c)
            if j < 2:
                rn = lax.rsqrt(_dot01(a * a, go) + EPS)
                a = a * rn
                if j == 0:
                    a = a * (GDN_DH ** -0.5)
            outs.append(a)
        return tuple(outs)

    qh, kh, vh = _gdn_pre_call("gdn_pre", gdn_pre, proj, w["conv_w"], gones, tb, steps)
    heads_shape = jax.ShapeDtypeStruct((N_HEADS, t, GDN_DH), F32)
    lanes_shape = jax.ShapeDtypeStruct((t, HEAD_LANES), F32)
    lanes_spec = pl.BlockSpec((tb, HEAD_LANES), lambda i: (i, 0))
    vec_spec = lambda n: pl.BlockSpec((1, n), lambda i: (0, 0))

    def gate_f(ab_ref, al_ref, dt_ref, g_ref, b_ref, gh_ref, bh_ref):
        g, b = _gb_fwd(ab_ref[...], al_ref[...], dt_ref[...])
        g_ref[...] = g
        b_ref[...] = b
        gc = _dot01(_chunk_sum_matrix(tb, False), g, ones="lhs")
        for h in range(N_HEADS):
            gh_ref[h] = jnp.broadcast_to(gc[:, h:h + 1], (tb, GDN_DH))
            bh_ref[h] = jnp.broadcast_to(b[:, N_HEADS + h:N_HEADS + h + 1], (tb, GDN_DH))

    g128, b128, gbh, bbh = pl.pallas_call(
        gate_f, name="gdn_gate_f", grid=(steps,),
        in_specs=[pl.BlockSpec((tb, HEAD_LANES), lambda i: (i, PIN_AB // HEAD_LANES)), vec_spec(HEAD_LANES),
                  vec_spec(HEAD_LANES)],
        out_specs=[lanes_spec, lanes_spec, _heads_spec(tb), _heads_spec(tb)],
        out_shape=[lanes_shape, lanes_shape, heads_shape, heads_shape],
        compiler_params=_params(("arbitrary",)))(proj, w["a_log_pad"], w["dt_bias_pad"])
    oh, sall, tall, *carried_out_gdn = _gdn_fwd(qh, kh, vh, gbh, bbh, carried_gdn)

    def gdn_post(o_ref, gt_ref, g_ref, cat_in, cat_ref):
        gt, g = gt_ref[...], g_ref[...]
        outs = []
        for h in range(N_HEADS):
            ob, gth = o_ref[h], _head_cols(gt, h)
            outs.append(ob * _rms_stats(ob) * g * (gth * _sigmoid(gth)))
        cat_ref[...] = jnp.concatenate(outs, axis=1).astype(cat_ref.dtype)

    gate_spec = pl.BlockSpec((tb, GDN_W), lambda i: (i, PIN_GATE // GDN_W))
    cat = pl.pallas_call(
        gdn_post, name="gdn_post", grid=(steps,),
        in_specs=[_heads_spec(tb), gate_spec, vec_spec(GDN_DH), ANY_SPEC],
        out_specs=pl.BlockSpec((tb, GDN_W), lambda i: (i, MLA_PAD // GDN_W)),
        out_shape=jax.ShapeDtypeStruct((t, CAT_W), BF16), input_output_aliases={3: 0},
        compiler_params=_params(("arbitrary",)))(oh, proj, w["gdn_norm_g"], cat)
    mixed, y = _mm_epilogue("mix_out", cat, w["w_out_pad"], [x], w["mix_post_g"],
                            lambda hb, xb, g: (hb, xb + hb * _rms_stats(hb) * g), 2, False, tk=CAT_W)
    saved = dict(x=x, hn=hn, proj=proj, nq=nq, nkv=nkv, q=q, k=k, v=v, o=o, lse=lse, qh=qh, kh=kh, vh=vh,
                 gbh=gbh, bbh=bbh, oh=oh, sall=sall, tall=tall, cat=cat, mixed=mixed,
                 tables=tables, g128=g128, b128=b128)
    return y, saved, list(carried_out) + list(carried_out_gdn)


def _qkv_specs(tb):
    base = PIN_QKV // GDN_W
    cur = [pl.BlockSpec((tb, GDN_W), lambda i, j=j: (i, base + j)) for j in range(3)]
    prev = [pl.BlockSpec((tb, GDN_W), lambda i, j=j: (jnp.maximum(i - 1, 0), base + j)) for j in range(3)]
    return cur + prev


def _gdn_pre_call(name, fn, proj, conv_w, gones, tb, steps):
    t = proj.shape[0]

    def body(xq, xk, xv, pq, pk, pv, cw, go, oq, ok, ov):
        has_prev = jnp.where(pl.program_id(0) == 0, 0.0, 1.0)
        outs = fn(xq[...], xk[...], xv[...], pq[...], pk[...], pv[...], cw[...], go[...], has_prev)
        for r, val in zip((oq, ok, ov), outs):
            for h in range(N_HEADS):
                r[h] = _head_cols(val, h)

    return pl.pallas_call(
        body, name=name, grid=(steps,),
        in_specs=_qkv_specs(tb) + [pl.BlockSpec(conv_w.shape, lambda i: (0, 0)),
                                   pl.BlockSpec(gones.shape, lambda i: (0, 0))],
        out_specs=[_heads_spec(tb)] * 3,
        out_shape=[jax.ShapeDtypeStruct((N_HEADS, t, GDN_DH), F32)] * 3,
        compiler_params=_params(("arbitrary",)))(proj, proj, proj, proj, proj, proj, conv_w, gones)


def _softplus(x):
    return jnp.maximum(x, 0.0) + jnp.log1p(jnp.exp(-jnp.abs(x)))


def _gb_fwd(ab, a_log, dt_bias):
    g = -jnp.exp(a_log) * _softplus(ab + dt_bias)
    return g, _sigmoid(ab)


def _rope_tables(positions):
    half = MLA_ROPE // 2
    freqs = ROPE_THETA ** (-jnp.arange(half, dtype=F32) / half)
    ang = positions.reshape(-1).astype(F32)[:, None] * freqs
    cos, sin = jnp.cos(ang), jnp.sin(ang)
    t = ang.shape[0]
    one = jnp.ones((t, MLA_NOPE), F32)
    z16, z32, z64 = jnp.zeros((t, half), F32), jnp.zeros((t, MLA_ROPE), F32), jnp.zeros((t, MLA_NOPE), F32)
    c = jnp.concatenate([one, cos, cos, jnp.ones((t, MLA_ROPE), F32)], axis=1)
    s1 = jnp.concatenate([z64, -sin, z16, z32], axis=1)
    s2 = jnp.concatenate([z64, z16, sin, z32], axis=1)
    return c, s1, s2


def _mixer_bwd(dy, sv, w, tb, carried=None):
    x, proj = sv["x"], sv["proj"]
    t, d = x.shape
    c, s1, s2 = sv["tables"]
    grads = {}

    dmixed, dcat, grads["mix_post_g"] = _prologue_mm(
        "mix_out_bx", [sv["mixed"], dy], w["mix_post_g"], lambda hb, dyb, g: _rms_bwd(hb, _rms_stats(hb), g, dyb),
        w["w_out_pad"], True)
    grads["w_out_pad"] = _mm("mix_out_bw", sv["cat"], dmixed, "tn", F32)
    steps = t // tb
    vec_spec = lambda n: pl.BlockSpec((1, n), lambda i: (0, 0))

    def gdn_post_b(o_ref, gt_ref, do_ref, g_ref, dproj_ref, doh_ref, dg_ref):
        @pl.when(pl.program_id(0) == 0)
        def _():
            dg_ref[...] = jnp.zeros(dg_ref.shape, F32)

        gt, dob, g = gt_ref[...], do_ref[...], g_ref[...]
        dgates = []
        for h in range(N_HEADS):
            ob, gth, dobh = o_ref[h], _head_cols(gt, h), _head_cols(dob, h)
            sg = _sigmoid(gth)
            r = _rms_stats(ob)
            dxo, dg = _rms_bwd(ob, r, g, dobh * (gth * sg))
            doh_ref[h] = dxo
            dg_ref[...] += dg
            dgates.append(dobh * (ob * r * g) * (sg * (1.0 + gth * (1.0 - sg))))
        dproj_ref[...] = jnp.concatenate(dgates, axis=1).astype(dproj_ref.dtype)

    dproj, doh, grads["gdn_norm_g"] = pl.pallas_call(
        gdn_post_b, name="gdn_post_b", grid=(steps,),
        in_specs=[_heads_spec(tb), pl.BlockSpec((tb, GDN_W), lambda i: (i, PIN_GATE // GDN_W)),
                  pl.BlockSpec((tb, GDN_W), lambda i: (i, MLA_PAD // GDN_W)), vec_spec(GDN_DH)],
        out_specs=[pl.BlockSpec((tb, GDN_W), lambda i: (i, PIN_GATE // GDN_W)), _heads_spec(tb), vec_spec(GDN_DH)],
        out_shape=[jax.ShapeDtypeStruct((t, PIN_W), BF16), jax.ShapeDtypeStruct((N_HEADS, t, GDN_DH), F32),
                   jax.ShapeDtypeStruct((1, GDN_DH), F32)],
        compiler_params=_params(("arbitrary",)))(sv["oh"], proj, dcat, w["gdn_norm_g"])

    def mla_post_b(ob, dmo, g):
        do, dg = _rms_bwd(ob, _rms_stats(ob, N_HEADS * MLA_V), g, dmo, N_HEADS * MLA_V)
        prod = do * ob
        delta = _heads_apply(prod, lambda ph: jnp.sum(ph, axis=1, keepdims=True) + jnp.zeros_like(ph))
        return do, delta, dg

    do, delta, grads["mla_out_g_pad"] = _rowwise(
        "mla_post_b", mla_post_b, [sv["o"], (dcat, MLA_PAD, 0, 0)], [w["mla_out_g_pad"]],
        [(MLA_PAD, BF16), (MLA_PAD, F32)], [(1, MLA_PAD)], tb // 2)
    tq = min(1024, t)
    dq, dk, dv, *carried_out = _attn_bwd(sv["q"], sv["k"], sv["v"], do, sv["lse"], delta, tq, carried)
    n_groups = N_HEADS // BWD_HEADS
    tr = tb // 2
    dq_groups = [(dq, BWD_HEADS * HEAD_LANES, 0, grp * (t // tr)) for grp in range(n_groups)]

    def rope_b(*blocks):
        dqb = jnp.concatenate(blocks[:n_groups], axis=1)
        dkb, dvb, cc, a1, a2 = blocks[n_groups:]
        dqr = _heads_apply(dqb * _attn_scale(), lambda xh: _rope(xh, cc, -a1, -a2))
        ksum = dkb[:, :HEAD_LANES]
        for h in range(1, N_HEADS):
            ksum = ksum + dkb[:, h * HEAD_LANES:(h + 1) * HEAD_LANES]
        lane = lax.broadcasted_iota(jnp.int32, ksum.shape, 1)
        keep = (lane >= MLA_NOPE) & (lane < MLA_NOPE + MLA_ROPE)
        dkpe = jnp.where(keep, _rope(ksum, cc, -a1, -a2), 0.0)
        return dqr, jnp.concatenate([dkb, dvb], axis=1), dkpe

    dqraw, dkv, dkpe = _rowwise("mla_rope_b", rope_b, dq_groups + [dk, dv, c, s1, s2], [],
                                [(MLA_PAD, BF16, t), (2 * MLA_PAD, BF16), (HEAD_LANES, F32)], [], tr)
    dnq = _mm("mla_uq_bx", dqraw, w["w_uq_pad"], "nt", F32)
    grads["w_uq_pad"] = _mm("mla_uq_bw", sv["nq"], dqraw, "tn", F32)
    dnkv = _mm("mla_ukv_bx", dkv, w["w_kv_pad"], "nt", F32)
    grads["w_kv_pad"] = _mm("mla_ukv_bw", sv["nkv"], dkv, "tn", F32)

    def mla_pre_b(p0, dnqb, dnkvb, dkpeb, gq, gkv):
        cq, ckv = p0[:, :MLA_Q_RANK], p0[:, MLA_Q_RANK:MLA_Q_RANK + MLA_KV_RANK]
        dcq, dgq = _rms_bwd(cq, _rms_stats(cq), gq, dnqb)
        dckv, dgkv = _rms_bwd(ckv, _rms_stats(ckv), gkv, dnkvb)
        return jnp.concatenate([dcq, dckv, dkpeb], axis=1), dgq, dgkv

    dproj, grads["mla_q_norm_g"], grads["mla_kv_norm_g"] = _rowwise(
        "mla_pre_b", mla_pre_b, [(proj, 512, PIN_MLA // 512, 0), dnq, dnkv, dkpe],
        [w["mla_q_norm_g"], w["mla_kv_norm_g"]], [(512, BF16)], [(1, MLA_Q_RANK), (1, MLA_KV_RANK)], tb,
        wide=(PIN_W, PIN_MLA // 512), carry=dproj)

    dqh, dkh, dvh, dgh, dbh = _gdn_bwd(sv["qh"], sv["kh"], sv["vh"], sv["gbh"], sv["bbh"], sv["sall"], sv["tall"], doh)
    gones = _group_ones()

    def gdn_pre_b(xq, xk, xv, pq, pk, pv, dq_, dk_, dv_, cw, go, has_prev):
        outs = []
        for j, (xc, xp, dd) in enumerate(((xq, pq, dq_), (xk, pk, dk_), (xv, pv, dv_))):
            cc = _conv_taps(xc, xp, cw[:, j * GDN_W:(j + 1) * GDN_W], has_prev)
            sg = _sigmoid(cc)
            a = cc * sg
            if j < 2:
                rn = lax.rsqrt(_dot01(a * a, go) + EPS)
                if j == 0:
                    dd = dd * (GDN_DH ** -0.5)
                da = rn * dd - a * (rn * rn * rn) * _dot01(dd * a, go)
            else:
                da = dd
            outs.append(da * (sg * (1.0 + cc * (1.0 - sg))))
        return tuple(outs)

    dcq, dck, dcv = _gdn_pre_b_call("gdn_pre_b", gdn_pre_b, proj, (dqh, dkh, dvh), w["conv_w"], gones, tb, steps)
    dproj, grads["conv_w"] = _conv_bwd_call("gdn_conv_b", proj, (dcq, dck, dcv), w["conv_w"], dproj, tb, steps)

    def gate_b(ab_ref, g_ref, b_ref, dgh_ref, dbh_ref, al_ref, dt_ref, carry_ref, dab_ref, dal_ref, ddt_ref):
        @pl.when(pl.program_id(0) == 0)
        def _():
            dal_ref[...] = jnp.zeros(dal_ref.shape, F32)
            ddt_ref[...] = jnp.zeros(ddt_ref.shape, F32)

        ab, g128, b128 = ab_ref[...], g_ref[...], b_ref[...]
        lane = lax.broadcasted_iota(jnp.int32, ab.shape, 1)
        dg_ = jnp.zeros(ab.shape, F32)
        db_ = jnp.zeros(ab.shape, F32)
        for h in range(N_HEADS):
            dg_ = dg_ + jnp.where(lane == h, jnp.broadcast_to(dgh_ref[h][:, 0:1], ab.shape), 0.0)
            db_ = db_ + jnp.where(lane == N_HEADS + h, jnp.broadcast_to(dbh_ref[h][:, 0:1], ab.shape), 0.0)
        dg_ = _dot01(_chunk_sum_matrix(tb, True), dg_, ones="lhs")
        slope = -jnp.exp(al_ref[...]) * _sigmoid(ab + dt_ref[...])
        dab_ref[...] = (dg_ * slope + db_ * b128 * (1.0 - b128)).astype(dab_ref.dtype)
        dal_ref[...] += jnp.sum(dg_ * g128, axis=0, keepdims=True)
        ddt_ref[...] += jnp.sum(dg_ * slope, axis=0, keepdims=True)

    lanes_spec = pl.BlockSpec((tb, HEAD_LANES), lambda i: (i, 0))
    ab_spec = pl.BlockSpec((tb, HEAD_LANES), lambda i: (i, PIN_AB // HEAD_LANES))
    dproj, grads["a_log_pad"], grads["dt_bias_pad"] = pl.pallas_call(
        gate_b, name="gdn_gate_b", grid=(steps,),
        in_specs=[ab_spec, lanes_spec, lanes_spec, _heads_spec(tb), _heads_spec(tb), vec_spec(HEAD_LANES),
                  vec_spec(HEAD_LANES), ANY_SPEC],
        out_specs=[ab_spec, vec_spec(HEAD_LANES), vec_spec(HEAD_LANES)],
        out_shape=[jax.ShapeDtypeStruct((t, PIN_W), BF16), jax.ShapeDtypeStruct((1, HEAD_LANES), F32),
                   jax.ShapeDtypeStruct((1, HEAD_LANES), F32)],
        input_output_aliases={7: 0},
        compiler_params=_params(("arbitrary",)))(proj, sv["g128"], sv["b128"], dgh, dbh, w["a_log_pad"],
                                                 w["dt_bias_pad"], dproj)
    grads["w_in_pad_t"] = _mm("mix_in_bw", dproj, sv["hn"], "tn", F32)

    def pre_b(dhn, xb, dyb, g):
        dx, dg = _rms_bwd(xb, _rms_stats(xb), g, dhn)
        return dyb + dx, dg

    dx, grads["mix_pre_g"] = _mm_epilogue("mix_in_bx", dproj, w["w_in_pad_t"], [x, dy], w["mix_pre_g"], pre_b, 1, True,
                                          tk=PIN_W // 3)
    return dx, grads, carried_out


def _gdn_pre_b_call(name, fn, proj, dd, conv_w, gones, tb, steps):
    t = proj.shape[0]

    def body(xq, xk, xv, pq, pk, pv, d0, d1, d2, cw, go, oq, ok, ov):
        has_prev = jnp.where(pl.program_id(0) == 0, 0.0, 1.0)
        dd_rows = [jnp.concatenate([dr[h] for h in range(N_HEADS)], axis=1) for dr in (d0, d1, d2)]
        outs = fn(xq[...], xk[...], xv[...], pq[...], pk[...], pv[...], *dd_rows, cw[...], go[...], has_prev)
        for r, val in zip((oq, ok, ov), outs):
            r[...] = val

    return pl.pallas_call(
        body, name=name, grid=(steps,),
        in_specs=_qkv_specs(tb) + [_heads_spec(tb)] * 3 + [pl.BlockSpec(conv_w.shape, lambda i: (0, 0)),
                                                          pl.BlockSpec(gones.shape, lambda i: (0, 0))],
        out_specs=[pl.BlockSpec((tb, GDN_W), lambda i: (i, 0))] * 3,
        out_shape=[jax.ShapeDtypeStruct((t, GDN_W), F32)] * 3,
        compiler_params=_params(("arbitrary",)))(proj, proj, proj, proj, proj, proj, *dd, conv_w, gones)


def _conv_bwd_call(name, proj, dc, conv_w, dproj, tb, steps):
    t = proj.shape[0]
    dcur = [pl.BlockSpec((tb, GDN_W), lambda i: (i, 0))] * 3
    dnext = [pl.BlockSpec((tb, GDN_W), lambda i: (jnp.minimum(i + 1, steps - 1), 0))] * 3

    def body(xq, xk, xv, pq, pk, pv, d0, d1, d2, n0, n1, n2, cw, carry_ref, dx_ref, dw_ref):
        i = pl.program_id(0)
        has_prev = jnp.where(i == 0, 0.0, 1.0)
        has_next = jnp.where(i == steps - 1, 0.0, 1.0)

        @pl.when(i == 0)
        def _():
            dw_ref[...] = jnp.zeros(dw_ref.shape, F32)

        wv = cw[...]
        dws, dxs = [], []
        for j, (xr, pr, dr, nr) in enumerate(((xq, pq, d0, n0), (xk, pk, d1, n1), (xv, pv, d2, n2))):
            x, xp, dcv, dnx = xr[...], pr[...], dr[...], nr[...]
            wj = wv[:, j * GDN_W:(j + 1) * GDN_W]
            row = lax.broadcasted_iota(jnp.int32, x.shape, 0)
            dx = dcv * wj[GDN_CONV - 1:GDN_CONV, :]
            rows_w = [jnp.sum(dcv * x, axis=0, keepdims=True)]
            for s in range(1, GDN_CONV):
                up = jnp.where(row < tb - s, _roll(dcv, tb - s, 0), _roll(dnx, tb - s, 0) * has_next)
                dx = dx + up * wj[GDN_CONV - 1 - s:GDN_CONV - s, :]
                sh = jnp.where(row >= s, _roll(x, s, 0), _roll(xp, s, 0) * has_prev)
                rows_w.append(jnp.sum(dcv * sh, axis=0, keepdims=True))
            dxs.append(dx)
            dws.append(jnp.concatenate(rows_w[::-1], axis=0))
        dx_ref[...] = jnp.concatenate(dxs, axis=1).astype(dx_ref.dtype)
        dw_ref[...] += jnp.concatenate(dws, axis=1)

    return pl.pallas_call(
        body, name=name, grid=(steps,),
        in_specs=_qkv_specs(tb) + dcur + dnext + [pl.BlockSpec(conv_w.shape, lambda i: (0, 0)), ANY_SPEC],
        out_specs=[pl.BlockSpec((tb, 3 * GDN_W), lambda i: (i, PIN_QKV // (3 * GDN_W))),
                   pl.BlockSpec(conv_w.shape, lambda i: (0, 0))],
        out_shape=[jax.ShapeDtypeStruct((t, PIN_W), BF16), jax.ShapeDtypeStruct(conv_w.shape, F32)],
        input_output_aliases={13: 0},
        compiler_params=_params(("arbitrary",)))(proj, proj, proj, proj, proj, proj, *dc, *dc, conv_w, dproj)


def _pad_heads_cols(wm, per_head):
    r = wm.shape[0]
    return jnp.pad(wm.reshape(r, N_HEADS, per_head), ((0, 0), (0, 0), (0, HEAD_LANES - per_head))).reshape(r, MLA_PAD)


def _unpad_heads_cols(wm, per_head):
    r = wm.shape[0]
    return wm.reshape(r, N_HEADS, HEAD_LANES)[:, :, :per_head].reshape(r, N_HEADS * per_head)


W_IN_COLS = MLA_Q_RANK + MLA_KV_RANK + MLA_ROPE + 3 * GDN_W + 2 * N_HEADS + GDN_W
W_IN_SHARD = W_IN_COLS // N_SHARD
W_IN_SHARD_PAD = 640
_Q0 = MLA_Q_RANK + MLA_KV_RANK
_Q1 = _Q0 + MLA_ROPE
_Q2 = _Q1 + 3 * GDN_W
_Q3 = _Q2 + 2 * N_HEADS
W_IN_SEGMENTS = [(0, _Q0, PIN_MLA), (_Q0, _Q1, PIN_KPE + MLA_NOPE), (_Q1, _Q2, PIN_QKV), (_Q2, _Q3, PIN_AB),
                 (_Q3, W_IN_COLS, PIN_GATE)]


def _win_pad_t(slabs):
    d = slabs.shape[2]
    pieces, at = [], 0
    for c0, c1, r0 in sorted(W_IN_SEGMENTS, key=lambda s: s[2]):
        if r0 > at:
            pieces.append(jnp.zeros((r0 - at, d), slabs.dtype))
        for q in range(N_SHARD):
            lo, hi = max(c0, q * W_IN_SHARD), min(c1, (q + 1) * W_IN_SHARD)
            if lo < hi:
                pieces.append(slabs[q, lo - q * W_IN_SHARD:hi - q * W_IN_SHARD])
        at = r0 + c1 - c0
    pieces.append(jnp.zeros((PIN_W - at, d), slabs.dtype))
    return jnp.concatenate(pieces, axis=0)


def _win_cols_t(wp_t, c_lo, c_hi):
    pieces = []
    for c0, c1, r0 in W_IN_SEGMENTS:
        lo, hi = max(c0, c_lo), min(c1, c_hi)
        if lo < hi:
            pieces.append(wp_t[r0 + lo - c0:r0 + hi - c0])
    return jnp.concatenate(pieces, axis=0)


def _wkv_to_pad(wkv):
    r = wkv.shape[0]
    w3 = wkv.reshape(r, N_HEADS, MLA_NOPE + MLA_V)
    kpart = jnp.pad(w3[:, :, :MLA_NOPE], ((0, 0), (0, 0), (0, HEAD_LANES - MLA_NOPE))).reshape(r, MLA_PAD)
    vpart = jnp.pad(w3[:, :, MLA_NOPE:], ((0, 0), (0, 0), (0, HEAD_LANES - MLA_V))).reshape(r, MLA_PAD)
    return jnp.concatenate([kpart, vpart], axis=1)


def _wkv_from_pad(wp):
    r = wp.shape[0]
    kpart = wp[:, :MLA_PAD].reshape(r, N_HEADS, HEAD_LANES)[:, :, :MLA_NOPE]
    vpart = wp[:, MLA_PAD:].reshape(r, N_HEADS, HEAD_LANES)[:, :, :MLA_V]
    return jnp.concatenate([kpart, vpart], axis=2).reshape(r, N_HEADS * (MLA_NOPE + MLA_V))


def _wout_to_pad(wo):
    n = wo.shape[1]
    mla = jnp.pad(wo[:N_HEADS * MLA_V].reshape(N_HEADS, MLA_V, n), ((0, 0), (0, HEAD_LANES - MLA_V), (0, 0)))
    return jnp.concatenate([mla.reshape(MLA_PAD, n), wo[N_HEADS * MLA_V:]], axis=0)


def _wout_from_pad(wp):
    n = wp.shape[1]
    mla = wp[:MLA_PAD].reshape(N_HEADS, HEAD_LANES, n)[:, :MLA_V].reshape(N_HEADS * MLA_V, n)
    return jnp.concatenate([mla, wp[MLA_PAD:]], axis=0)


def _pad_lanes(v, n):
    return jnp.pad(v, ((0, 0), (0, n - v.shape[1])))


def _compute_weights(full):
    w = {}
    for n in FFN_BIG:
        if n in full:
            w[n] = full[n].astype(MM_DTYPE)
    w["w_in_pad_t"] = _win_pad_t(full["w_in"]).astype(MM_DTYPE)
    w["w_uq_pad"] = _pad_heads_cols(full["mla_w_uq"], MLA_NOPE + MLA_ROPE).astype(MM_DTYPE)
    w["w_kv_pad"] = _wkv_to_pad(full["mla_w_ukv"]).astype(MM_DTYPE)
    w["w_out_pad"] = _wout_to_pad(full["w_out"]).astype(MM_DTYPE)
    w["conv_w"] = full["gdn_conv_w"].astype(F32)
    for n in ("ffn1_pre_g", "ffn1_post_g", "mix_pre_g", "mla_q_norm_g", "mla_kv_norm_g", "gdn_norm_g", "mix_post_g",
              "ffn2_pre_g", "ffn2_post_g"):
        w[n] = full[n]
    w["mla_out_g_pad"] = _pad_heads_cols(full["mla_out_g"], MLA_V)
    w["a_log_pad"] = _pad_lanes(full["gdn_a_log"], HEAD_LANES)
    w["dt_bias_pad"] = _pad_lanes(full["gdn_dt_bias"], HEAD_LANES)
    return w


FFN2_BIG = FFN_BIG[3:]


def _local_step(x, positions, loss_target, full, late=None):
    t, d = x.shape
    tb = min(512, t)
    tm = min(1024, t)
    tk = min(2048, t)
    w = _compute_weights(full)
    ffn = lambda tag: (w[tag + "_pre_g"], w[tag + "_w_gate"], w[tag + "_w_up"], w.get(tag + "_w_down"),
                       w[tag + "_post_g"])
    x1, sv1, w["ffn1_w_down"] = _ffn_fwd("ffn1", x, *ffn("ffn1"), tm,
                                         carried_up=_carried_gather([late[3]]) if late else None)
    x2, svm, gathered = _mixer_fwd(x1, positions, w, tb, _carried_gather(late[0][:2]) if late else None,
                                   _carried_gather(late[0][2:]) if late else None)
    for n, gw in zip(FFN2_BIG, gathered):
        w[n] = gw
    (dy, lsum), sv2, _ = _ffn_fwd("ffn2", x2, *ffn("ffn2"), tm, loss_target)
    g = {}
    dx2, g["ffn2_pre_g"], g["ffn2_w_gate"], g["ffn2_w_up"], g["ffn2_w_down"], g["ffn2_post_g"] = _ffn_bwd(
        "ffn2", dy, sv2, *ffn("ffn2"), tm, tk)[:6]

    def pair_sums(arrs, tag):
        got = _swap_halves(arrs, tag)
        return [_add_pair("add_pair%s_%d" % (tag, i), gi, gt, late[1]) for i, (gi, gt) in enumerate(zip(arrs, got))]

    def chip_sums(pairs, slabs, tag):
        return [_add_chips("add_chips%s_%d" % (tag, i), pr, sl, late[2]) for i, (pr, sl) in enumerate(zip(pairs, slabs))]

    if late:
        pairs2 = pair_sums([g[n] for n in FFN2_BIG], "_ffn2")
        dx1, gm, slabs2 = _mixer_bwd(dx2, svm, w, tb, _carried_scatter(pairs2))
        for n, hs in zip(FFN2_BIG, chip_sums(pairs2, slabs2, "_ffn2")):
            g[n] = hs
    else:
        dx1, gm, _ = _mixer_bwd(dx2, svm, w, tb)
    g["w_in"] = jnp.stack([jnp.pad(_win_cols_t(gm["w_in_pad_t"], q * W_IN_SHARD, (q + 1) * W_IN_SHARD),
                                   ((0, W_IN_SHARD_PAD - W_IN_SHARD), (0, 0))) for q in range(N_SHARD)])
    g["mla_w_uq"] = _unpad_heads_cols(gm["w_uq_pad"], MLA_NOPE + MLA_ROPE)
    g["mla_w_ukv"] = _wkv_from_pad(gm["w_kv_pad"])
    g["gdn_conv_w"] = gm["conv_w"]
    g["w_out"] = _wout_from_pad(gm["w_out_pad"])
    if late:
        quarters = [_pack([jnp.split(g[n], N_SHARD, axis=SHARD_AXIS[n])[q] for n in MIX_BIG], MM_DTYPE)
                    for q in range(N_SHARD)]
        pairs_m = pair_sums([g["w_in"].astype(MM_DTYPE), jnp.stack(quarters)], "_mix")
        pairs_d, pairs_gu = [], []

        def make_mid(dwd):
            pairs_d.extend(pair_sums([dwd], "_ffn1d"))
            return _carried_scatter(pairs_d)

        def make_up(dwg, dwu):
            pairs_gu.extend(pair_sums([dwg, dwu], "_ffn1"))
            return _carried_scatter(pairs_gu)

        dx0, g["ffn1_pre_g"], _, _, _, g["ffn1_post_g"], slabs_m, slabs_d, slabs_gu = _ffn_bwd(
            "ffn1", dx1, sv1, *ffn("ffn1"), tm, tk, _carried_scatter(pairs_m), make_mid, make_up)
        g["ffn1_w_gate"], g["ffn1_w_up"] = chip_sums(pairs_gu, slabs_gu, "_ffn1")
        g["ffn1_w_down"] = chip_sums(pairs_d, slabs_d, "_ffn1d")[0]
        g["w_in"], g["mix_pack"] = chip_sums(pairs_m, slabs_m, "_mix")
    else:
        dx0, g["ffn1_pre_g"], g["ffn1_w_gate"], g["ffn1_w_up"], g["ffn1_w_down"], g["ffn1_post_g"] = _ffn_bwd(
            "ffn1", dx1, sv1, *ffn("ffn1"), tm, tk)[:6]
    g["mix_pre_g"], g["mix_post_g"] = gm["mix_pre_g"], gm["mix_post_g"]
    g["mla_q_norm_g"], g["mla_kv_norm_g"] = gm["mla_q_norm_g"], gm["mla_kv_norm_g"]
    g["gdn_norm_g"] = gm["gdn_norm_g"]
    g["mla_out_g"] = _unpad_heads_cols(gm["mla_out_g_pad"], MLA_V)
    g["gdn_a_log"] = gm["a_log_pad"][:, :N_HEADS]
    g["gdn_dt_bias"] = gm["dt_bias_pad"][:, :N_HEADS]
    return lsum, dx0, g


HBM_SPEC = pl.BlockSpec(memory_space=pltpu.HBM)


def _place():
    return lax.axis_index("x"), lax.axis_index("y"), lax.axis_index("c")


def _exchange_call(name, body, ins, out_shapes, n_remote, n_local):
    return pl.pallas_call(
        body, name=name, in_specs=[HBM_SPEC] * len(ins), out_specs=[HBM_SPEC] * len(out_shapes), out_shape=out_shapes,
        scratch_shapes=[pltpu.SemaphoreType.DMA((n_remote,)), pltpu.SemaphoreType.DMA((n_remote,)),
                        pltpu.SemaphoreType.DMA((n_local,))])(*ins)


def _other_chips(x, y):
    return [(1 - x, y), (x, 1 - y), (1 - x, 1 - y)]


def _at_each_chip(fn):
    x, y, _ = _place()
    for cx in range(2):
        for cy in range(2):
            pl.when((x == cx) & (y == cy))(functools.partial(fn, cx, cy))


def _at_each_device(fn):
    x, y, c = _place()
    for cx in range(2):
        for cy in range(2):
            for cc in range(2):
                pl.when((x == cx) & (y == cy) & (c == cc))(functools.partial(fn, cx, cy, cc))


def _at_each_core(fn):
    c = lax.axis_index("c")
    for cc in range(2):
        pl.when(c == cc)(functools.partial(fn, cc))


def _gather_shards(ws):
    nw = len(ws)

    def body(*refs):
        w_refs, out_refs = refs[:nw], refs[nw:2 * nw]
        send_sems, recv_sems, local_sems = refs[2 * nw:]

        def run(x, y, c):
            chips = _other_chips(x, y)
            me, sibling = 2 * x + y, (x, y, 1 - c)

            def half(ref, which):
                hr = ref.shape[0] // 2
                return ref.at[pl.ds(which * hr, hr)]

            def over_ici(i, j, src, slab, to):
                return pltpu.make_async_remote_copy(
                    src_ref=half(src, c), dst_ref=half(out_refs[i].at[slab], c), send_sem=send_sems.at[7 * i + j],
                    recv_sem=recv_sems.at[7 * i + j], device_id=to, device_id_type=MESH)

            def over_d2d(i, j, slab, which):
                return pltpu.make_async_remote_copy(
                    src_ref=half(out_refs[i].at[slab], which), dst_ref=half(out_refs[i].at[slab], which),
                    send_sem=send_sems.at[7 * i + 3 + j], recv_sem=recv_sems.at[7 * i + 3 + j], device_id=sibling,
                    device_id_type=MESH)

            def own(i, w_ref):
                return pltpu.make_async_remote_copy(
                    src_ref=w_ref, dst_ref=out_refs[i].at[me], send_sem=send_sems.at[7 * i + 6],
                    recv_sem=recv_sems.at[7 * i + 6], device_id=sibling, device_id_type=MESH)

            sends, passed = [], []
            for i, w_ref in enumerate(w_refs):
                for j, (px, py) in enumerate(chips):
                    sends.append(over_ici(i, j, w_ref, me, (px, py, c)))
                    sends[-1].start()
            for i, w_ref in enumerate(w_refs):
                sends.append(own(i, w_ref))
                sends[-1].start()
            for i, w_ref in enumerate(w_refs):
                for j, (px, py) in enumerate(chips):
                    over_ici(i, j, w_ref, 2 * px + py, (px, py, c)).wait_recv()
                    passed.append(over_d2d(i, j, 2 * px + py, c))
                    passed[-1].start()
            for i, w_ref in enumerate(w_refs):
                own(i, w_ref).wait_recv()
                for j, (px, py) in enumerate(chips):
                    over_d2d(i, j, 2 * px + py, 1 - c).wait_recv()
            for cp in sends + passed:
                cp.wait_send()

        _at_each_device(run)

    outs = [jax.ShapeDtypeStruct((N_SHARD,) + w.shape, w.dtype) for w in ws]
    return _exchange_call("gather_weight_shards", body, ws, outs, 7 * nw, 1)


def _swap_halves(gs, tag=""):
    ng = len(gs)

    def body(*refs):
        g_refs, got_refs = refs[:ng], refs[ng:2 * ng]
        send_sems, recv_sems, _ = refs[2 * ng:]
        x, y, _ = _place()

        def run(c):
            sends = []
            for i, (g_ref, got_ref) in enumerate(zip(g_refs, got_refs)):
                hr = got_ref.shape[1]
                sends.append(pltpu.make_async_remote_copy(
                    src_ref=g_ref.at[:, pl.ds((1 - c) * hr, hr)], dst_ref=got_ref, send_sem=send_sems.at[i],
                    recv_sem=recv_sems.at[i], device_id=(x, y, 1 - c), device_id_type=MESH))
                sends[-1].start()
            for cp in sends:
                cp.wait()

        _at_each_core(run)

    halves = [jax.ShapeDtypeStruct((g.shape[0], g.shape[1] // 2, g.shape[2]), g.dtype) for g in gs]
    return _exchange_call("swap_grad_halves" + tag, body, gs, halves, ng, 1)


def _scatter_copies(p_refs, out_refs, send_sems, recv_sems, x, y):
    c = lax.axis_index("c")
    copies = []
    for i, (p_ref, out_ref) in enumerate(zip(p_refs, out_refs)):
        for j, (px, py) in enumerate(_other_chips(x, y)):
            copies.append(pltpu.make_async_remote_copy(
                src_ref=p_ref.at[2 * px + py], dst_ref=out_ref.at[j], send_sem=send_sems.at[3 * i + j],
                recv_sem=recv_sems.at[3 * i + j], device_id=(px, py, c), device_id_type=MESH))
    return copies


def _start_all(make, *refs):
    def run(x, y):
        for cp in make(*refs, x, y):
            cp.start()

    _at_each_chip(run)


def _wait_all(make, *refs):
    def run(x, y):
        copies = make(*refs, x, y)
        for cp in copies:
            cp.wait_recv()
        for cp in copies:
            cp.wait_send()

    _at_each_chip(run)


def _scatter_shapes(ps):
    return [jax.ShapeDtypeStruct((3,) + p.shape[1:], p.dtype) for p in ps]


def _carried_scatter(ps):
    return _Carried(ps, _scatter_shapes(ps), 3 * len(ps), functools.partial(_start_all, _scatter_copies),
                    functools.partial(_wait_all, _scatter_copies))


def _direct_gather_copies(w_refs, out_refs, send_sems, recv_sems, x, y, arriving):
    c = lax.axis_index("c")
    me = 2 * x + y
    peers = [((px, py, c), 2 * px + py) for px, py in _other_chips(x, y)] + [((x, y, 1 - c), me)]
    copies = []
    for i, (w_ref, out_ref) in enumerate(zip(w_refs, out_refs)):
        for j, (peer, slab) in enumerate(peers):
            copies.append(pltpu.make_async_remote_copy(
                src_ref=w_ref, dst_ref=out_ref.at[slab if arriving else me], send_sem=send_sems.at[4 * i + j],
                recv_sem=recv_sems.at[4 * i + j], device_id=peer, device_id_type=MESH))
    return copies


def _carried_gather(ws):
    def start(w_refs, out_refs, send_sems, recv_sems):
        def run(x, y):
            for cp in _direct_gather_copies(w_refs, out_refs, send_sems, recv_sems, x, y, False):
                cp.start()

        _at_each_chip(run)

    def finish(w_refs, out_refs, send_sems, recv_sems):
        def run(x, y):
            for cp in _direct_gather_copies(w_refs, out_refs, send_sems, recv_sems, x, y, True):
                cp.wait_recv()
            for cp in _direct_gather_copies(w_refs, out_refs, send_sems, recv_sems, x, y, False):
                cp.wait_send()

        _at_each_chip(run)

    outs = [jax.ShapeDtypeStruct((N_SHARD,) + w.shape, w.dtype) for w in ws]
    return _Carried(ws, outs, 4 * len(ws), start, finish)


def _share_halves(hs):
    n = len(hs)

    def body(*refs):
        h_refs, out_refs = refs[:n], refs[n:2 * n]
        send_sems, recv_sems, _ = refs[2 * n:]
        x, y, c = _place()
        sends = []
        for i, (h_ref, out_ref) in enumerate(zip(h_refs, out_refs)):
            sends.append(pltpu.make_async_remote_copy(
                src_ref=h_ref, dst_ref=out_ref, send_sem=send_sems.at[i], recv_sem=recv_sems.at[i],
                device_id=(x, y, 1 - c), device_id_type=MESH))
            sends[-1].start()
        for cp in sends:
            cp.wait()

    outs = [jax.ShapeDtypeStruct(h.shape, h.dtype) for h in hs]
    return _exchange_call("share_grad_halves", body, hs, outs, n, 1)


def _scalar_grid_call(name, body, scalars, grid, in_specs, out_specs, out_shape, args):
    grid_spec = pltpu.PrefetchScalarGridSpec(num_scalar_prefetch=len(scalars), grid=grid, in_specs=in_specs,
                                             out_specs=out_specs)
    return pl.pallas_call(body, name=name, grid_spec=grid_spec, out_shape=out_shape,
                          compiler_params=_params(("arbitrary",) * len(grid)))(*scalars, *args)


def _add_pair(name, g, got, core):
    ns_, hr, cols = got.shape
    th = _row_tile(hr, 512)
    nb = hr // th

    def body(core_ref, g_ref, got_ref, out_ref):
        out_ref[...] = (g_ref[...].astype(F32) + got_ref[...].astype(F32)).astype(out_ref.dtype)

    blk = pl.BlockSpec((1, th, cols), lambda q, j, core_ref: (q, j, 0))
    own = pl.BlockSpec((1, th, cols), lambda q, j, core_ref: (q, core_ref[0] * nb + j, 0))
    return _scalar_grid_call(name, body, [core], (ns_, nb), [own, blk], blk,
                             jax.ShapeDtypeStruct(got.shape, got.dtype), [g, got])


def _add_chips(name, pairs, slabs, chip):
    _, hr, cols = slabs.shape
    th = _row_tile(hr, 512)

    def body(chip_ref, own_ref, s0_ref, s1_ref, s2_ref, out_ref):
        total = own_ref[0].astype(F32) + s0_ref[0].astype(F32)
        out_ref[...] = (total + s1_ref[0].astype(F32)) + s2_ref[0].astype(F32)

    own = pl.BlockSpec((1, th, cols), lambda j, chip_ref: (chip_ref[0], j, 0))
    others = [pl.BlockSpec((1, th, cols), lambda j, chip_ref, k=k: (k, j, 0)) for k in range(3)]
    return _scalar_grid_call(name, body, [chip], (hr // th,), [own] + others,
                             pl.BlockSpec((th, cols), lambda j, chip_ref: (j, 0)),
                             jax.ShapeDtypeStruct((hr, cols), F32), [pairs, slabs, slabs, slabs])


def _join_halves(name, mine, other, core):
    hr, cols = mine.shape
    th = _row_tile(hr, 512)
    nb = hr // th

    def body(core_ref, mine_ref, other_ref, out_ref):
        is_mine = pl.program_id(0) == core_ref[0]

        @pl.when(is_mine)
        def _():
            out_ref[0] = mine_ref[...]

        @pl.when(jnp.logical_not(is_mine))
        def _():
            out_ref[0] = other_ref[...]

    blk = pl.BlockSpec((th, cols), lambda h, j, core_ref: (j, 0))
    return _scalar_grid_call(name, body, [core], (2, nb), [blk, blk],
                             pl.BlockSpec((1, th, cols), lambda h, j, core_ref: (0, h * nb + j, 0)),
                             jax.ShapeDtypeStruct((1, 2 * hr, cols), mine.dtype), [mine, other])


def _gather_small(sp):
    def body(s_ref, out_ref, send_sems, recv_sems, local_sem):
        x, y, c = _place()
        me = 4 * x + 2 * y + c
        peers = [(x ^ (m >> 2), y ^ ((m >> 1) & 1), c ^ (m & 1)) for m in range(1, 8)]
        mine = pltpu.make_async_copy(s_ref, out_ref.at[me], local_sem)
        mine.start()
        sends = [pltpu.make_async_remote_copy(src_ref=s_ref, dst_ref=out_ref.at[me], send_sem=send_sems.at[j],
                                              recv_sem=recv_sems.at[j], device_id=p, device_id_type=MESH)
                 for j, p in enumerate(peers)]
        for cp in sends:
            cp.start()
        for j, (px, py, pc) in enumerate(peers):
            pltpu.make_async_remote_copy(src_ref=s_ref, dst_ref=out_ref.at[4 * px + 2 * py + pc],
                                         send_sem=send_sems.at[j], recv_sem=recv_sems.at[j], device_id=(px, py, pc),
                                         device_id_type=MESH).wait_recv()
        for cp in sends:
            cp.wait_send()
        mine.wait()

    return pl.pallas_call(
        body, name="gather_small_grads", in_specs=[HBM_SPEC], out_specs=HBM_SPEC,
        out_shape=jax.ShapeDtypeStruct((8,) + sp.shape, sp.dtype),
        scratch_shapes=[pltpu.SemaphoreType.DMA((7,)), pltpu.SemaphoreType.DMA((7,)), pltpu.SemaphoreType.DMA])(sp)


def _pack_rows(total):
    rows = -(-total // LANES)
    return -(-rows // 32) * 32


def _pack(arrs, dtype):
    flat = jnp.concatenate([a.reshape(-1).astype(dtype) for a in arrs])
    rows = _pack_rows(flat.shape[0])
    return jnp.pad(flat, (0, rows * LANES - flat.shape[0])).reshape(rows, LANES)


def _unpack(buf, shapes):
    flat = buf.reshape(-1)
    out, off = {}, 0
    for n, shp in shapes:
        size = shp[0] * shp[1]
        out[n] = flat[off:off + size].reshape(shp)
        off += size
    return out


def _to_wire(name, w3):
    _, r, cols = w3.shape
    tb = _row_tile(r, 512)

    def body(w_ref, o_ref):
        o_ref[...] = w_ref[0].astype(o_ref.dtype)

    return pl.pallas_call(
        body, name=name, grid=(r // tb,), in_specs=[pl.BlockSpec((1, tb, cols), lambda i: (0, i, 0))],
        out_specs=pl.BlockSpec((tb, cols), lambda i: (i, 0)), out_shape=jax.ShapeDtypeStruct((r, cols), MM_DTYPE),
        compiler_params=_params(("arbitrary",)))(w3)


def _adamw(name, w3, g, m3, v3, tb):
    c1 = 1.0 - ADAM_B1 ** ADAM_STEP
    c2 = 1.0 - ADAM_B2 ** ADAM_STEP
    _, r, cols = w3.shape
    emit = g.ndim == 2
    blk3 = pl.BlockSpec((1, tb, cols), lambda i: (0, i, 0))
    g_spec = pl.BlockSpec((tb, cols), lambda i: (i, 0)) if emit else blk3

    def body(w_ref, g_ref, m_ref, v_ref, *out_refs):
        gb = g_ref[...] if emit else g_ref[0]
        m2 = ADAM_B1 * m_ref[0] + (1.0 - ADAM_B1) * gb
        v2 = ADAM_B2 * v_ref[0] + (1.0 - ADAM_B2) * (gb * gb)
        out_refs[-3][0] = -ADAM_LR * ((m2 / c1) / (jnp.sqrt(v2 / c2) + ADAM_EPS) + ADAM_WD * w_ref[0])
        out_refs[-2][0] = m2
        out_refs[-1][0] = v2
        if emit:
            out_refs[0][0] = gb

    n_out = 4 if emit else 3
    outs = pl.pallas_call(
        body, name=name, grid=(r // tb,), in_specs=[blk3, g_spec, blk3, blk3], out_specs=[blk3] * n_out,
        out_shape=[jax.ShapeDtypeStruct((1, r, cols), F32)] * n_out,
        compiler_params=_params(("arbitrary",)))(w3, g, m3, v3)
    return outs if emit else [g] + list(outs)


def _row_tile(rows, pref):
    if rows <= pref:
        return rows
    t = pref
    while t >= 8:
        if rows % t == 0 and t % 8 == 0:
            return t
        t -= 8
    return rows


def kernel(x, positions, ffn1_pre_g, ffn1_w_gate, ffn1_w_up, ffn1_w_down, ffn1_post_g, mix_pre_g, w_in, mla_q_norm_g, mla_w_uq, mla_kv_norm_g, mla_w_ukv, mla_out_g, gdn_conv_w, gdn_a_log, gdn_dt_bias, gdn_norm_g, w_out, mix_post_g, ffn2_pre_g, ffn2_w_gate, ffn2_w_up, ffn2_w_down, ffn2_post_g, loss_target, m_ffn1_pre_g, m_ffn1_w_gate, m_ffn1_w_up, m_ffn1_w_down, m_ffn1_post_g, m_mix_pre_g, m_w_in, m_mla_q_norm_g, m_mla_w_uq, m_mla_kv_norm_g, m_mla_w_ukv, m_mla_out_g, m_gdn_conv_w, m_gdn_a_log, m_gdn_dt_bias, m_gdn_norm_g, m_w_out, m_mix_post_g, m_ffn2_pre_g, m_ffn2_w_gate, m_ffn2_w_up, m_ffn2_w_down, m_ffn2_post_g, v_ffn1_pre_g, v_ffn1_w_gate, v_ffn1_w_up, v_ffn1_w_down, v_ffn1_post_g, v_mix_pre_g, v_w_in, v_mla_q_norm_g, v_mla_w_uq, v_mla_kv_norm_g, v_mla_w_ukv, v_mla_out_g, v_gdn_conv_w, v_gdn_a_log, v_gdn_dt_bias, v_gdn_norm_g, v_w_out, v_mix_post_g, v_ffn2_pre_g, v_ffn2_w_gate, v_ffn2_w_up, v_ffn2_w_down, v_ffn2_post_g):
    args = dict(locals())
    wsh = {n: args[n][0] for n in WEIGHTS}
    msh = {n: args["m_" + n] for n in SMALL}
    vsh = {n: args["v_" + n] for n in SMALL}
    for n in SMALL:
        wsh[n] = args[n]
    mix_shapes = [(n, wsh[n].shape) for n in MIX_BIG]

    early = FFN_BIG[:2]
    held = lambda a, n: jnp.swapaxes(a, 1, 2) if n in TRANSPOSED else a
    w_in_wire = jnp.pad(held(w_in, "w_in")[0].astype(MM_DTYPE), ((0, W_IN_SHARD_PAD - W_IN_SHARD), (0, 0)))
    gathered = _gather_shards([_to_wire("wire_" + n, held(args[n], n)) for n in early]
                              + [w_in_wire, _pack([wsh[n] for n in MIX_BIG], MM_DTYPE)])
    full = {n: wsh[n] for n in SMALL}
    for n, gw in zip(early + ["w_in"], gathered):
        full[n] = gw
    parts = [_unpack(gathered[-1][q], mix_shapes) for q in range(N_SHARD)]
    for n in MIX_BIG:
        full[n] = jnp.concatenate([parts[q][n] for q in range(N_SHARD)], axis=SHARD_AXIS[n])

    core = lax.axis_index("c").astype(jnp.int32).reshape(1)
    chip = (2 * lax.axis_index("x") + lax.axis_index("y")).astype(jnp.int32).reshape(1)
    late = ([_to_wire("wire_" + n, held(args[n], n)) for n in FFN2_BIG], core, chip,
            _to_wire("wire_ffn1_w_down", ffn1_w_down))
    lsum, grad_x, g = _local_step(x[0], positions, loss_target[0], full, late)
    loss = lax.psum(0.5 * jnp.sum(lsum) / x.shape[-1], ("x", "y", "c"))

    halves = [g[n] for n in FFN_BIG] + [g["w_in"], g["mix_pack"]]
    others = _share_halves(halves)
    shared = [_join_halves("join_halves_%d" % i, hm, ho, core) for i, (hm, ho) in enumerate(zip(halves, others))]
    gsh = _unpack(shared[-1], mix_shapes)
    for n, sg_ in zip(FFN_BIG, shared):
        gsh[n] = sg_
    gsh["w_in"] = shared[-2][:, :W_IN_SHARD]

    small_shapes = [(n, wsh[n].shape) for n in SMALL]
    pack_small = lambda d: jnp.concatenate(
        [_pad_lanes(d[n].astype(F32), LANES) for n in SMALL] + [jnp.zeros((SMALL_ROWS - len(SMALL), LANES), F32)], axis=0)
    slots = _gather_small(pack_small(g))

    c1 = 1.0 - ADAM_B1 ** ADAM_STEP
    c2 = 1.0 - ADAM_B2 ** ADAM_STEP

    def small_update(wb, mb, vb, s8):
        gs = s8[0:SMALL_ROWS]
        for d in range(1, 8):
            gs = gs + s8[d * SMALL_ROWS:(d + 1) * SMALL_ROWS]
        m2 = ADAM_B1 * mb + (1.0 - ADAM_B1) * gs
        v2 = ADAM_B2 * vb + (1.0 - ADAM_B2) * (gs * gs)
        delta = -ADAM_LR * ((m2 / c1) / (jnp.sqrt(v2 / c2) + ADAM_EPS) + ADAM_WD * wb)
        return gs, delta, m2, v2

    sg, sd, sm, sv_ = _rowwise("adamw_small", small_update,
                               [pack_small(wsh), pack_small(msh), pack_small(vsh)],
                               [slots.reshape(8 * SMALL_ROWS, LANES)], [(LANES, F32)] * 4, [], SMALL_ROWS)
    grads, deltas, new_m, new_v = {}, {}, {}, {}
    for i, (n, shp) in enumerate(small_shapes):
        grads[n], deltas[n] = sg[i:i + 1, :shp[1]], sd[i:i + 1, :shp[1]]
        new_m[n], new_v[n] = sm[i:i + 1, :shp[1]], sv_[i:i + 1, :shp[1]]
    for n in BIG:
        w3 = held(args[n], n)
        outs = _adamw("adamw_" + n, w3, gsh[n], held(args["m_" + n], n), held(args["v_" + n], n),
                      _row_tile(w3.shape[1], 256))
        grads[n], deltas[n], new_m[n], new_v[n] = [held(o, n) for o in outs]

    return (loss, grad_x[None], *[grads[n] for n in WEIGHTS], *[deltas[n] for n in WEIGHTS],
            *[new_m[n] for n in WEIGHTS], *[new_v[n] for n in WEIGHTS])
```
